```python
import math
import jax, jax.numpy as jnp
from jax import lax
import numpy as np

D_MODEL = 1024
BATCH = 8
SEQ = 2048
DEPTH = 1

N_META = 16
MLA_HEADS = 4
QK_NOPE_DIM = 128
QK_ROPE_DIM = 64
QK_HEAD_DIM = QK_NOPE_DIM + QK_ROPE_DIM
V_HEAD_DIM = 128
Q_LORA_RANK = 256
KV_LORA_RANK = 256
ROPE_THETA = 10000.0
Q_BLOCK = 128
DN_HEADS = 4
DN_HEAD_DIM = 128
DN_WIDTH = DN_HEADS * DN_HEAD_DIM
DN_CONV_WIDTH = 4
DN_CHUNK = 64
MIX_WIDTH = MLA_HEADS * V_HEAD_DIM + DN_WIDTH
IN_COLS = Q_LORA_RANK + KV_LORA_RANK + QK_ROPE_DIM + 3 * DN_WIDTH + DN_WIDTH + 2 * DN_HEADS
D_FF = 2816
FFN_CONV_WIDTH = 3
NORM_EPS = 1e-6

kernel_name = "hymba_mla_gdn_convglu_layer"


def _rmsnorm(x, w):
    xf = x.astype(jnp.float32)
    y = xf * lax.rsqrt(jnp.mean(xf * xf, axis=-1, keepdims=True) + NORM_EPS)
    return (y * w.astype(jnp.float32)).astype(x.dtype)


def _l2norm(x):
    return x * lax.rsqrt(jnp.sum(x * x, axis=-1, keepdims=True) + NORM_EPS)


def _causal_dwconv(x, w):
    width, channels = w.shape
    return lax.conv_general_dilated(
        x, w[:, None, :].astype(x.dtype), window_strides=(1,),
        padding=[(width - 1, 0)], dimension_numbers=("NWC", "WIO", "NWC"),
        feature_group_count=channels)


def _rope(x, pos):
    half = x.shape[-1] // 2
    inv_freq = ROPE_THETA ** (-jnp.arange(half, dtype=jnp.float32) / half)
    ang = pos.astype(jnp.float32)[:, None] * inv_freq[None, :]
    cos = jnp.cos(ang)[None, :, None, :]
    sin = jnp.sin(ang)[None, :, None, :]
    xf = x.astype(jnp.float32)
    x1, x2 = xf[..., :half], xf[..., half:]
    return jnp.concatenate([x1 * cos - x2 * sin, x1 * sin + x2 * cos], axis=-1).astype(x.dtype)


def _causal_attention(q, k, v):
    B, H, L, dq = q.shape
    dv = v.shape[-1]
    nb = -(-L // Q_BLOCK)
    Lp = nb * Q_BLOCK
    pad = ((0, 0), (0, 0), (0, Lp - L), (0, 0))
    q, k, v = jnp.pad(q, pad), jnp.pad(k, pad), jnp.pad(v, pad)
    scale = 1.0 / math.sqrt(dq)
    q_blocks = q.reshape(B, H, nb, Q_BLOCK, dq).transpose(2, 0, 1, 3, 4)
    key_pos = jnp.arange(Lp)

    def one_block(args):
        qb, start = args
        s = jnp.einsum("bhqd,bhkd->bhqk", qb, k).astype(jnp.float32) * scale
        q_pos = start + jnp.arange(Q_BLOCK)
        mask = key_pos[None, :] <= q_pos[:, None]
        p = jax.nn.softmax(jnp.where(mask, s, -jnp.inf), axis=-1)
        return jnp.einsum("bhqk,bhkd->bhqd", p.astype(v.dtype), v)

    out = lax.map(one_block, (q_blocks, jnp.arange(nb) * Q_BLOCK))
    return out.transpose(1, 2, 0, 3, 4).reshape(B, H, Lp, dv)[:, :, :L]


def _mla(q_lat, kv_lat, k_pe, pos, q_a_norm_w, w_q_b, kv_a_norm_w, w_kv_b,
         q_norm_w, k_norm_w, mla_out_norm_w):
    B, L, _ = q_lat.shape
    q = (_rmsnorm(q_lat, q_a_norm_w) @ w_q_b).reshape(B, L, MLA_HEADS, QK_HEAD_DIM)
    kv = (_rmsnorm(kv_lat, kv_a_norm_w) @ w_kv_b).reshape(B, L, MLA_HEADS, QK_NOPE_DIM + V_HEAD_DIM)
    k_nope, v = kv[..., :QK_NOPE_DIM], kv[..., QK_NOPE_DIM:]
    k = jnp.concatenate(
        [k_nope, jnp.broadcast_to(k_pe[:, :, None, :], (B, L, MLA_HEADS, QK_ROPE_DIM))], axis=-1)
    q = _rmsnorm(q, q_norm_w)
    k = _rmsnorm(k, k_norm_w)
    q = jnp.concatenate([q[..., :QK_NOPE_DIM], _rope(q[..., QK_NOPE_DIM:], pos)], axis=-1)
    k = jnp.concatenate([k[..., :QK_NOPE_DIM], _rope(k[..., QK_NOPE_DIM:], pos)], axis=-1)
    o = _causal_attention(q.transpose(0, 2, 1, 3), k.transpose(0, 2, 1, 3), v.transpose(0, 2, 1, 3))
    o = _rmsnorm(o.transpose(0, 2, 1, 3), mla_out_norm_w)
    return o.reshape(B, L, MLA_HEADS * V_HEAD_DIM)


def _chunk_gated_delta_rule(q, k, v, g, beta):
    B, H, T, dk = q.shape
    dv = v.shape[-1]
    C = DN_CHUNK
    N = T // C
    q = q * (1.0 / math.sqrt(dk))
    q = q.reshape(B, H, N, C, dk)
    k = k.reshape(B, H, N, C, dk)
    v = v.reshape(B, H, N, C, dv)
    g = jnp.cumsum(g.reshape(B, H, N, C), axis=-1)
    beta = beta.reshape(B, H, N, C)
    tri = jnp.tril(jnp.ones((C, C), dtype=bool))
    strict = jnp.tril(jnp.ones((C, C), dtype=jnp.float32), -1)
    decay = jnp.exp(jnp.where(tri, g[..., :, None] - g[..., None, :], -jnp.inf))
    k_beta = k * beta[..., None]
    v_beta = v * beta[..., None]
    a_strict = jnp.einsum("bhnid,bhnjd->bhnij", k_beta, k) * decay * strict
    eye = jnp.eye(C, dtype=jnp.float32)
    t_inv = lax.linalg.triangular_solve(
        eye + a_strict, jnp.broadcast_to(eye, a_strict.shape),
        left_side=True, lower=True, unit_diagonal=True)
    u = jnp.einsum("bhnij,bhnjd->bhnid", t_inv, v_beta)
    w = jnp.einsum("bhnij,bhnjd->bhnid", t_inv, k_beta * jnp.exp(g)[..., None])
    qk = jnp.einsum("bhnid,bhnjd->bhnij", q, k) * decay

    def step(S, inp):
        q_c, k_c, u_c, w_c, g_c, qk_c = inp
        v_new = u_c - jnp.einsum("bhcd,bhde->bhce", w_c, S)
        o = (jnp.einsum("bhcd,bhde->bhce", q_c * jnp.exp(g_c)[..., None], S)
             + jnp.einsum("bhij,bhje->bhie", qk_c, v_new))
        g_last = g_c[..., -1]
        S = (S * jnp.exp(g_last)[..., None, None]
             + jnp.einsum("bhcd,bhce->bhde", k_c * jnp.exp(g_last[..., None] - g_c)[..., None], v_new))
        return S, o

    to_scan = lambda t: jnp.moveaxis(t, 2, 0)
    S0 = jnp.zeros((B, H, dk, dv), jnp.float32)
    _, o = lax.scan(step, S0, (to_scan(q), to_scan(k), to_scan(u), to_scan(w), to_scan(g), to_scan(qk)))
    return jnp.moveaxis(o, 0, 2).reshape(B, H, T, dv)


def _gated_deltanet(qkv, z, a, b, dn_conv_w, dn_A_log, dn_dt_bias, dn_out_norm_w):
    B, L, _ = qkv.shape
    qkv = jax.nn.silu(_causal_dwconv(qkv, dn_conv_w)).astype(jnp.float32)
    heads = lambda t: t.reshape(B, L, DN_HEADS, DN_HEAD_DIM).transpose(0, 2, 1, 3)
    q = _l2norm(heads(qkv[..., :DN_WIDTH]))
    k = _l2norm(heads(qkv[..., DN_WIDTH:2 * DN_WIDTH]))
    v = heads(qkv[..., 2 * DN_WIDTH:])
    beta = jax.nn.sigmoid(b.astype(jnp.float32)).transpose(0, 2, 1)
    g = (-jnp.exp(dn_A_log.astype(jnp.float32))
         * jax.nn.softplus(a.astype(jnp.float32) + dn_dt_bias.astype(jnp.float32))).transpose(0, 2, 1)
    pad = (-N_META) % DN_CHUNK
    p4 = ((0, 0), (0, 0), (pad, 0), (0, 0))
    p3 = ((0, 0), (0, 0), (pad, 0))
    o = _chunk_gated_delta_rule(jnp.pad(q, p4), jnp.pad(k, p4), jnp.pad(v, p4),
                                jnp.pad(g, p3), jnp.pad(beta, p3))[:, :, pad:]
    o = _rmsnorm(o.transpose(0, 2, 1, 3), dn_out_norm_w)
    o = o * jax.nn.silu(z.reshape(B, L, DN_HEADS, DN_HEAD_DIM).astype(jnp.float32))
    return o.reshape(B, L, DN_WIDTH).astype(qkv.dtype if False else z.dtype)


def _conv_glu(h, w_gate, w_up, ffn_conv_w, ffn_conv_b, w_down):
    gate = _causal_dwconv(h @ w_gate, ffn_conv_w) + ffn_conv_b
    return (jax.nn.silu(gate) * (h @ w_up)) @ w_down


def _fwd_setup_inputs(seed: int = 0) -> dict:
    key = jax.random.key(seed)
    ks = jax.random.split(key, 26)
    f32 = jnp.float32
    nrm = lambda k, shape, scale: jax.random.normal(k, shape, f32) * scale
    gain = lambda k, shape: 1.0 + 0.02 * jax.random.normal(k, shape, f32)
    Ld = DEPTH
    dt = jnp.exp(jax.random.uniform(ks[14], (Ld, DN_HEADS), f32, math.log(1e-3), math.log(1e-1)))
    return {
        "x": nrm(ks[0], (BATCH, SEQ, D_MODEL), 1.0),
        "meta_tokens": nrm(ks[1], (N_META, D_MODEL), 1.0),
        "attn_norm_w": gain(ks[2], (Ld, D_MODEL)),
        "w_in": nrm(ks[3], (Ld, D_MODEL, IN_COLS), D_MODEL ** -0.5),
        "q_a_norm_w": gain(ks[4], (Ld, Q_LORA_RANK)),
        "w_q_b": nrm(ks[5], (Ld, Q_LORA_RANK, MLA_HEADS * QK_HEAD_DIM), Q_LORA_RANK ** -0.5),
        "kv_a_norm_w": gain(ks[6], (Ld, KV_LORA_RANK)),
        "w_kv_b": nrm(ks[7], (Ld, KV_LORA_RANK, MLA_HEADS * (QK_NOPE_DIM + V_HEAD_DIM)), KV_LORA_RANK ** -0.5),
        "q_norm_w": gain(ks[8], (Ld, QK_HEAD_DIM)),
        "k_norm_w": gain(ks[9], (Ld, QK_HEAD_DIM)),
        "mla_out_norm_w": gain(ks[10], (Ld, V_HEAD_DIM)),
        "dn_conv_w": nrm(ks[11], (Ld, DN_CONV_WIDTH, 3 * DN_WIDTH), DN_CONV_WIDTH ** -0.5),
        "dn_A_log": jnp.log(jax.random.uniform(ks[12], (Ld, DN_HEADS), f32, 1.0, 16.0)),
        "dn_dt_bias": dt + jnp.log(-jnp.expm1(-dt)),
        "dn_out_norm_w": gain(ks[13], (Ld, DN_HEAD_DIM)),
        "w_out": nrm(ks[15], (Ld, MIX_WIDTH, D_MODEL), MIX_WIDTH ** -0.5),
        "ffn_norm_w": gain(ks[16], (Ld, D_MODEL)),
        "w_gate": nrm(ks[17], (Ld, D_MODEL, D_FF), D_MODEL ** -0.5),
        "w_up": nrm(ks[18], (Ld, D_MODEL, D_FF), D_MODEL ** -0.5),
        "ffn_conv_w": nrm(ks[19], (Ld, FFN_CONV_WIDTH, D_FF), FFN_CONV_WIDTH ** -0.5),
        "ffn_conv_b": nrm(ks[20], (Ld, D_FF), 0.01),
        "w_down": nrm(ks[21], (Ld, D_FF, D_MODEL), D_FF ** -0.5),
    }


def _fwd_reference(x, meta_tokens, attn_norm_w, w_in, q_a_norm_w, w_q_b, kv_a_norm_w, w_kv_b,
              q_norm_w, k_norm_w, mla_out_norm_w, dn_conv_w, dn_A_log, dn_dt_bias,
              dn_out_norm_w, w_out, ffn_norm_w, w_gate, w_up, ffn_conv_w, ffn_conv_b, w_down):
    B = x.shape[0]
    meta = jnp.broadcast_to(meta_tokens[None].astype(x.dtype), (B, N_META, D_MODEL))
    h = jnp.concatenate([meta, x], axis=1)
    L = h.shape[1]
    pos = jnp.arange(L, dtype=jnp.int32)
    c1 = Q_LORA_RANK
    c2 = c1 + KV_LORA_RANK
    c3 = c2 + QK_ROPE_DIM
    c4 = c3 + 3 * DN_WIDTH
    c5 = c4 + DN_WIDTH
    c6 = c5 + DN_HEADS
    for l in range(DEPTH):
        u = _rmsnorm(h, attn_norm_w[l])
        proj = u @ w_in[l]
        q_lat, kv_lat, k_pe = proj[..., :c1], proj[..., c1:c2], proj[..., c2:c3]
        dn_qkv, dn_z = proj[..., c3:c4], proj[..., c4:c5]
        dn_a, dn_b = proj[..., c5:c6], proj[..., c6:]
        y_mla = _mla(q_lat, kv_lat, k_pe, pos, q_a_norm_w[l], w_q_b[l], kv_a_norm_w[l], w_kv_b[l],
                     q_norm_w[l], k_norm_w[l], mla_out_norm_w[l])
        y_dn = _gated_deltanet(dn_qkv, dn_z, dn_a, dn_b, dn_conv_w[l], dn_A_log[l], dn_dt_bias[l],
                               dn_out_norm_w[l])
        mixed = jnp.concatenate([y_mla, y_dn], axis=-1)
        h = h + mixed @ w_out[l]
        h = h + _conv_glu(_rmsnorm(h, ffn_norm_w[l]), w_gate[l], w_up[l], ffn_conv_w[l],
                          ffn_conv_b[l], w_down[l])
    return h[:, N_META:]


import jax as _jax
import jax.numpy as _jnp

TWIN_FORMAT = 'train_step'
FWD_PARAMS = ['x', 'meta_tokens', 'attn_norm_w', 'w_in', 'q_a_norm_w', 'w_q_b', 'kv_a_norm_w', 'w_kv_b', 'q_norm_w', 'k_norm_w', 'mla_out_norm_w', 'dn_conv_w', 'dn_A_log', 'dn_dt_bias', 'dn_out_norm_w', 'w_out', 'ffn_norm_w', 'w_gate', 'w_up', 'ffn_conv_w', 'ffn_conv_b', 'w_down']
TWIN_WEIGHTS = ['meta_tokens', 'attn_norm_w', 'w_in', 'q_a_norm_w', 'w_q_b', 'kv_a_norm_w', 'w_kv_b', 'q_norm_w', 'k_norm_w', 'mla_out_norm_w', 'dn_conv_w', 'dn_A_log', 'dn_dt_bias', 'dn_out_norm_w', 'w_out', 'ffn_norm_w', 'w_gate', 'w_up', 'ffn_conv_w', 'ffn_conv_b', 'w_down']
TWIN_DIFF_INPUT = 'x'
TWIN_INPUTS = ['x', 'meta_tokens', 'attn_norm_w', 'w_in', 'q_a_norm_w', 'w_q_b', 'kv_a_norm_w', 'w_kv_b', 'q_norm_w', 'k_norm_w', 'mla_out_norm_w', 'dn_conv_w', 'dn_A_log', 'dn_dt_bias', 'dn_out_norm_w', 'w_out', 'ffn_norm_w', 'w_gate', 'w_up', 'ffn_conv_w', 'ffn_conv_b', 'w_down', 'loss_target', 'm_meta_tokens', 'm_attn_norm_w', 'm_w_in', 'm_q_a_norm_w', 'm_w_q_b', 'm_kv_a_norm_w', 'm_w_kv_b', 'm_q_norm_w', 'm_k_norm_w', 'm_mla_out_norm_w', 'm_dn_conv_w', 'm_dn_A_log', 'm_dn_dt_bias', 'm_dn_out_norm_w', 'm_w_out', 'm_ffn_norm_w', 'm_w_gate', 'm_w_up', 'm_ffn_conv_w', 'm_ffn_conv_b', 'm_w_down', 'v_meta_tokens', 'v_attn_norm_w', 'v_w_in', 'v_q_a_norm_w', 'v_w_q_b', 'v_kv_a_norm_w', 'v_w_kv_b', 'v_q_norm_w', 'v_k_norm_w', 'v_mla_out_norm_w', 'v_dn_conv_w', 'v_dn_A_log', 'v_dn_dt_bias', 'v_dn_out_norm_w', 'v_w_out', 'v_ffn_norm_w', 'v_w_gate', 'v_w_up', 'v_ffn_conv_w', 'v_ffn_conv_b', 'v_w_down']
TWIN_OUTPUTS = ['loss', 'grad_x', 'grad_meta_tokens', 'grad_attn_norm_w', 'grad_w_in', 'grad_q_a_norm_w', 'grad_w_q_b', 'grad_kv_a_norm_w', 'grad_w_kv_b', 'grad_q_norm_w', 'grad_k_norm_w', 'grad_mla_out_norm_w', 'grad_dn_conv_w', 'grad_dn_A_log', 'grad_dn_dt_bias', 'grad_dn_out_norm_w', 'grad_w_out', 'grad_ffn_norm_w', 'grad_w_gate', 'grad_w_up', 'grad_ffn_conv_w', 'grad_ffn_conv_b', 'grad_w_down', 'delta_meta_tokens', 'delta_attn_norm_w', 'delta_w_in', 'delta_q_a_norm_w', 'delta_w_q_b', 'delta_kv_a_norm_w', 'delta_w_kv_b', 'delta_q_norm_w', 'delta_k_norm_w', 'delta_mla_out_norm_w', 'delta_dn_conv_w', 'delta_dn_A_log', 'delta_dn_dt_bias', 'delta_dn_out_norm_w', 'delta_w_out', 'delta_ffn_norm_w', 'delta_w_gate', 'delta_w_up', 'delta_ffn_conv_w', 'delta_ffn_conv_b', 'delta_w_down', 'new_m_meta_tokens', 'new_m_attn_norm_w', 'new_m_w_in', 'new_m_q_a_norm_w', 'new_m_w_q_b', 'new_m_kv_a_norm_w', 'new_m_w_kv_b', 'new_m_q_norm_w', 'new_m_k_norm_w', 'new_m_mla_out_norm_w', 'new_m_dn_conv_w', 'new_m_dn_A_log', 'new_m_dn_dt_bias', 'new_m_dn_out_norm_w', 'new_m_w_out', 'new_m_ffn_norm_w', 'new_m_w_gate', 'new_m_w_up', 'new_m_ffn_conv_w', 'new_m_ffn_conv_b', 'new_m_w_down', 'new_v_meta_tokens', 'new_v_attn_norm_w', 'new_v_w_in', 'new_v_q_a_norm_w', 'new_v_w_q_b', 'new_v_kv_a_norm_w', 'new_v_w_kv_b', 'new_v_q_norm_w', 'new_v_k_norm_w', 'new_v_mla_out_norm_w', 'new_v_dn_conv_w', 'new_v_dn_A_log', 'new_v_dn_dt_bias', 'new_v_dn_out_norm_w', 'new_v_w_out', 'new_v_ffn_norm_w', 'new_v_w_gate', 'new_v_w_up', 'new_v_ffn_conv_w', 'new_v_ffn_conv_b', 'new_v_w_down']
TWIN_LEAF_KINDS = {'loss': 'loss', 'grad_x': 'grad_x', 'grad_meta_tokens': 'grad_w', 'grad_attn_norm_w': 'grad_w', 'grad_w_in': 'grad_w', 'grad_q_a_norm_w': 'grad_w', 'grad_w_q_b': 'grad_w', 'grad_kv_a_norm_w': 'grad_w', 'grad_w_kv_b': 'grad_w', 'grad_q_norm_w': 'grad_w', 'grad_k_norm_w': 'grad_w', 'grad_mla_out_norm_w': 'grad_w', 'grad_dn_conv_w': 'grad_w', 'grad_dn_A_log': 'grad_w', 'grad_dn_dt_bias': 'grad_w', 'grad_dn_out_norm_w': 'grad_w', 'grad_w_out': 'grad_w', 'grad_ffn_norm_w': 'grad_w', 'grad_w_gate': 'grad_w', 'grad_w_up': 'grad_w', 'grad_ffn_conv_w': 'grad_w', 'grad_ffn_conv_b': 'grad_w', 'grad_w_down': 'grad_w', 'delta_meta_tokens': 'delta_w', 'delta_attn_norm_w': 'delta_w', 'delta_w_in': 'delta_w', 'delta_q_a_norm_w': 'delta_w', 'delta_w_q_b': 'delta_w', 'delta_kv_a_norm_w': 'delta_w', 'delta_w_kv_b': 'delta_w', 'delta_q_norm_w': 'delta_w', 'delta_k_norm_w': 'delta_w', 'delta_mla_out_norm_w': 'delta_w', 'delta_dn_conv_w': 'delta_w', 'delta_dn_A_log': 'delta_w', 'delta_dn_dt_bias': 'delta_w', 'delta_dn_out_norm_w': 'delta_w', 'delta_w_out': 'delta_w', 'delta_ffn_norm_w': 'delta_w', 'delta_w_gate': 'delta_w', 'delta_w_up': 'delta_w', 'delta_ffn_conv_w': 'delta_w', 'delta_ffn_conv_b': 'delta_w', 'delta_w_down': 'delta_w', 'new_m_meta_tokens': 'new_m', 'new_m_attn_norm_w': 'new_m', 'new_m_w_in': 'new_m', 'new_m_q_a_norm_w': 'new_m', 'new_m_w_q_b': 'new_m', 'new_m_kv_a_norm_w': 'new_m', 'new_m_w_kv_b': 'new_m', 'new_m_q_norm_w': 'new_m', 'new_m_k_norm_w': 'new_m', 'new_m_mla_out_norm_w': 'new_m', 'new_m_dn_conv_w': 'new_m', 'new_m_dn_A_log': 'new_m', 'new_m_dn_dt_bias': 'new_m', 'new_m_dn_out_norm_w': 'new_m', 'new_m_w_out': 'new_m', 'new_m_ffn_norm_w': 'new_m', 'new_m_w_gate': 'new_m', 'new_m_w_up': 'new_m', 'new_m_ffn_conv_w': 'new_m', 'new_m_ffn_conv_b': 'new_m', 'new_m_w_down': 'new_m', 'new_v_meta_tokens': 'new_v', 'new_v_attn_norm_w': 'new_v', 'new_v_w_in': 'new_v', 'new_v_q_a_norm_w': 'new_v', 'new_v_w_q_b': 'new_v', 'new_v_kv_a_norm_w': 'new_v', 'new_v_w_kv_b': 'new_v', 'new_v_q_norm_w': 'new_v', 'new_v_k_norm_w': 'new_v', 'new_v_mla_out_norm_w': 'new_v', 'new_v_dn_conv_w': 'new_v', 'new_v_dn_A_log': 'new_v', 'new_v_dn_dt_bias': 'new_v', 'new_v_dn_out_norm_w': 'new_v', 'new_v_w_out': 'new_v', 'new_v_ffn_norm_w': 'new_v', 'new_v_w_gate': 'new_v', 'new_v_w_up': 'new_v', 'new_v_ffn_conv_w': 'new_v', 'new_v_ffn_conv_b': 'new_v', 'new_v_w_down': 'new_v'}


def _forward(args):
    return _fwd_reference(*[args[k] for k in FWD_PARAMS])


def _output_shape():
    out = _jax.eval_shape(lambda: _forward(_fwd_setup_inputs(0)))
    return out.shape, out.dtype

N_MICROBATCH = 1
ADAM_LR = 0.001
ADAM_B1 = 0.9
ADAM_B2 = 0.999
ADAM_EPS = 1e-08
ADAM_WD = 0.01
ADAM_STEP = 10
PER_EXAMPLE_BATCH_AXIS = {'x': 0, 'loss_target': 0}
SHARED_INPUTS = []
_WEIGHT_DTYPES = {'meta_tokens': _jnp.float32, 'attn_norm_w': _jnp.float32, 'w_in': _jnp.float32, 'q_a_norm_w': _jnp.float32, 'w_q_b': _jnp.float32, 'kv_a_norm_w': _jnp.float32, 'w_kv_b': _jnp.float32, 'q_norm_w': _jnp.float32, 'k_norm_w': _jnp.float32, 'mla_out_norm_w': _jnp.float32, 'dn_conv_w': _jnp.float32, 'dn_A_log': _jnp.float32, 'dn_dt_bias': _jnp.float32, 'dn_out_norm_w': _jnp.float32, 'w_out': _jnp.float32, 'ffn_norm_w': _jnp.float32, 'w_gate': _jnp.float32, 'w_up': _jnp.float32, 'ffn_conv_w': _jnp.float32, 'ffn_conv_b': _jnp.float32, 'w_down': _jnp.float32}
MOMENT_SCALE = {'meta_tokens': 7.208776e-02, 'attn_norm_w': 3.309279e+00, 'w_in': 5.051568e-01, 'q_a_norm_w': 8.225475e-01, 'w_q_b': 3.908036e-01, 'kv_a_norm_w': 2.289150e+00, 'w_kv_b': 8.929043e-01, 'q_norm_w': 8.493015e-01, 'k_norm_w': 8.554674e-01, 'mla_out_norm_w': 6.436406e+01, 'dn_conv_w': 3.102166e-01, 'dn_A_log': 6.145251e+00, 'dn_dt_bias': 6.056656e+00, 'dn_out_norm_w': 2.541427e+01, 'w_out': 9.711401e-01, 'ffn_norm_w': 1.240207e+01, 'w_gate': 3.137420e-01, 'w_up': 1.883971e-01, 'ffn_conv_w': 1.405373e+00, 'ffn_conv_b': 1.710097e+00, 'w_down': 2.573712e-01}


def _to_microbatches(a, axis):
    t = _jnp.moveaxis(a, axis, 0)
    t = t.reshape((N_MICROBATCH, t.shape[0] // N_MICROBATCH) + t.shape[1:])
    return _jnp.moveaxis(t, 1, axis + 1)


def setup_inputs(seed: int = 0) -> dict:
    inp = _fwd_setup_inputs(seed)
    key = _jax.random.fold_in(_jax.random.key(seed), 7919)
    shape, _ = _output_shape()
    out = dict(inp)
    out["loss_target"] = _jax.random.normal(_jax.random.fold_in(key, 0), shape, _jnp.float32)
    for i, name in enumerate(TWIN_WEIGHTS):
        w = inp[name].astype(_jnp.float32)
        if MOMENT_SCALE is None:
            s = _jnp.sqrt(_jnp.mean(_jnp.square(w)) + 1e-30)
        else:
            s = MOMENT_SCALE[name]
        km, kv = _jax.random.split(_jax.random.fold_in(key, i + 1))
        out[name] = w
        out["m_" + name] = s * _jax.random.normal(km, w.shape, _jnp.float32)
        out["v_" + name] = (s * s) * _jax.random.uniform(kv, w.shape, _jnp.float32, 0.5, 1.5)
    if N_MICROBATCH > 1:
        for name, axis in PER_EXAMPLE_BATCH_AXIS.items():
            out[name] = _to_microbatches(out[name], axis)
    return {'x': out['x'], 'meta_tokens': out['meta_tokens'], 'attn_norm_w': out['attn_norm_w'], 'w_in': out['w_in'], 'q_a_norm_w': out['q_a_norm_w'], 'w_q_b': out['w_q_b'], 'kv_a_norm_w': out['kv_a_norm_w'], 'w_kv_b': out['w_kv_b'], 'q_norm_w': out['q_norm_w'], 'k_norm_w': out['k_norm_w'], 'mla_out_norm_w': out['mla_out_norm_w'], 'dn_conv_w': out['dn_conv_w'], 'dn_A_log': out['dn_A_log'], 'dn_dt_bias': out['dn_dt_bias'], 'dn_out_norm_w': out['dn_out_norm_w'], 'w_out': out['w_out'], 'ffn_norm_w': out['ffn_norm_w'], 'w_gate': out['w_gate'], 'w_up': out['w_up'], 'ffn_conv_w': out['ffn_conv_w'], 'ffn_conv_b': out['ffn_conv_b'], 'w_down': out['w_down'], 'loss_target': out['loss_target'], 'm_meta_tokens': out['m_meta_tokens'], 'm_attn_norm_w': out['m_attn_norm_w'], 'm_w_in': out['m_w_in'], 'm_q_a_norm_w': out['m_q_a_norm_w'], 'm_w_q_b': out['m_w_q_b'], 'm_kv_a_norm_w': out['m_kv_a_norm_w'], 'm_w_kv_b': out['m_w_kv_b'], 'm_q_norm_w': out['m_q_norm_w'], 'm_k_norm_w': out['m_k_norm_w'], 'm_mla_out_norm_w': out['m_mla_out_norm_w'], 'm_dn_conv_w': out['m_dn_conv_w'], 'm_dn_A_log': out['m_dn_A_log'], 'm_dn_dt_bias': out['m_dn_dt_bias'], 'm_dn_out_norm_w': out['m_dn_out_norm_w'], 'm_w_out': out['m_w_out'], 'm_ffn_norm_w': out['m_ffn_norm_w'], 'm_w_gate': out['m_w_gate'], 'm_w_up': out['m_w_up'], 'm_ffn_conv_w': out['m_ffn_conv_w'], 'm_ffn_conv_b': out['m_ffn_conv_b'], 'm_w_down': out['m_w_down'], 'v_meta_tokens': out['v_meta_tokens'], 'v_attn_norm_w': out['v_attn_norm_w'], 'v_w_in': out['v_w_in'], 'v_q_a_norm_w': out['v_q_a_norm_w'], 'v_w_q_b': out['v_w_q_b'], 'v_kv_a_norm_w': out['v_kv_a_norm_w'], 'v_w_kv_b': out['v_w_kv_b'], 'v_q_norm_w': out['v_q_norm_w'], 'v_k_norm_w': out['v_k_norm_w'], 'v_mla_out_norm_w': out['v_mla_out_norm_w'], 'v_dn_conv_w': out['v_dn_conv_w'], 'v_dn_A_log': out['v_dn_A_log'], 'v_dn_dt_bias': out['v_dn_dt_bias'], 'v_dn_out_norm_w': out['v_dn_out_norm_w'], 'v_w_out': out['v_w_out'], 'v_ffn_norm_w': out['v_ffn_norm_w'], 'v_w_gate': out['v_w_gate'], 'v_w_up': out['v_w_up'], 'v_ffn_conv_w': out['v_ffn_conv_w'], 'v_ffn_conv_b': out['v_ffn_conv_b'], 'v_w_down': out['v_w_down']}


def _loss(weights, diff, rest, loss_target):
    with _jax.named_scope("forward"):
        args = {**rest, TWIN_DIFF_INPUT: diff, **{k: w.astype(_WEIGHT_DTYPES[k]) for k, w in weights.items()}}
        y = _forward(args)
    with _jax.named_scope("loss_head"):
        err = _jnp.square(y.astype(_jnp.float32) - loss_target)
        return 0.5 * _jnp.sum(_jnp.mean(err, axis=-1)) if err.ndim else 0.5 * err


def _adamw(w, g, m, v):
    m = ADAM_B1 * m + (1.0 - ADAM_B1) * g
    v = ADAM_B2 * v + (1.0 - ADAM_B2) * _jnp.square(g)
    m_hat = m / (1.0 - ADAM_B1 ** ADAM_STEP)
    v_hat = v / (1.0 - ADAM_B2 ** ADAM_STEP)
    delta = -ADAM_LR * (m_hat / (_jnp.sqrt(v_hat) + ADAM_EPS) + ADAM_WD * w)
    return delta, m, v


def reference(x, meta_tokens, attn_norm_w, w_in, q_a_norm_w, w_q_b, kv_a_norm_w, w_kv_b, q_norm_w, k_norm_w, mla_out_norm_w, dn_conv_w, dn_A_log, dn_dt_bias, dn_out_norm_w, w_out, ffn_norm_w, w_gate, w_up, ffn_conv_w, ffn_conv_b, w_down, loss_target, m_meta_tokens, m_attn_norm_w, m_w_in, m_q_a_norm_w, m_w_q_b, m_kv_a_norm_w, m_w_kv_b, m_q_norm_w, m_k_norm_w, m_mla_out_norm_w, m_dn_conv_w, m_dn_A_log, m_dn_dt_bias, m_dn_out_norm_w, m_w_out, m_ffn_norm_w, m_w_gate, m_w_up, m_ffn_conv_w, m_ffn_conv_b, m_w_down, v_meta_tokens, v_attn_norm_w, v_w_in, v_q_a_norm_w, v_w_q_b, v_kv_a_norm_w, v_w_kv_b, v_q_norm_w, v_k_norm_w, v_mla_out_norm_w, v_dn_conv_w, v_dn_A_log, v_dn_dt_bias, v_dn_out_norm_w, v_w_out, v_ffn_norm_w, v_w_gate, v_w_up, v_ffn_conv_w, v_ffn_conv_b, v_w_down):
    given = dict(x=x, meta_tokens=meta_tokens, attn_norm_w=attn_norm_w, w_in=w_in, q_a_norm_w=q_a_norm_w, w_q_b=w_q_b, kv_a_norm_w=kv_a_norm_w, w_kv_b=w_kv_b, q_norm_w=q_norm_w, k_norm_w=k_norm_w, mla_out_norm_w=mla_out_norm_w, dn_conv_w=dn_conv_w, dn_A_log=dn_A_log, dn_dt_bias=dn_dt_bias, dn_out_norm_w=dn_out_norm_w, w_out=w_out, ffn_norm_w=ffn_norm_w, w_gate=w_gate, w_up=w_up, ffn_conv_w=ffn_conv_w, ffn_conv_b=ffn_conv_b, w_down=w_down, loss_target=loss_target, m_meta_tokens=m_meta_tokens, m_attn_norm_w=m_attn_norm_w, m_w_in=m_w_in, m_q_a_norm_w=m_q_a_norm_w, m_w_q_b=m_w_q_b, m_kv_a_norm_w=m_kv_a_norm_w, m_w_kv_b=m_w_kv_b, m_q_norm_w=m_q_norm_w, m_k_norm_w=m_k_norm_w, m_mla_out_norm_w=m_mla_out_norm_w, m_dn_conv_w=m_dn_conv_w, m_dn_A_log=m_dn_A_log, m_dn_dt_bias=m_dn_dt_bias, m_dn_out_norm_w=m_dn_out_norm_w, m_w_out=m_w_out, m_ffn_norm_w=m_ffn_norm_w, m_w_gate=m_w_gate, m_w_up=m_w_up, m_ffn_conv_w=m_ffn_conv_w, m_ffn_conv_b=m_ffn_conv_b, m_w_down=m_w_down, v_meta_tokens=v_meta_tokens, v_attn_norm_w=v_attn_norm_w, v_w_in=v_w_in, v_q_a_norm_w=v_q_a_norm_w, v_w_q_b=v_w_q_b, v_kv_a_norm_w=v_kv_a_norm_w, v_w_kv_b=v_w_kv_b, v_q_norm_w=v_q_norm_w, v_k_norm_w=v_k_norm_w, v_mla_out_norm_w=v_mla_out_norm_w, v_dn_conv_w=v_dn_conv_w, v_dn_A_log=v_dn_A_log, v_dn_dt_bias=v_dn_dt_bias, v_dn_out_norm_w=v_dn_out_norm_w, v_w_out=v_w_out, v_ffn_norm_w=v_ffn_norm_w, v_w_gate=v_w_gate, v_w_up=v_w_up, v_ffn_conv_w=v_ffn_conv_w, v_ffn_conv_b=v_ffn_conv_b, v_w_down=v_w_down)
    weights = {n: given[n] for n in TWIN_WEIGHTS}
    shared = {n: given[n] for n in SHARED_INPUTS}
    per_example = {n: given[n] for n in ['x']}
    grad_fn = _jax.value_and_grad(_loss, argnums=(0, 1))

    def one_microbatch(ex, loss_target):
        ex = dict(ex)
        diff = ex.pop(TWIN_DIFF_INPUT)
        return grad_fn(weights, diff, {**shared, **ex}, loss_target)

    if N_MICROBATCH == 1:
        loss, (grad_w, grad_x) = one_microbatch(per_example, given["loss_target"])
    else:
        def body(carry, xs):
            loss_sum, grad_sum = carry
            l_k, (gw_k, gx_k) = one_microbatch(xs[0], xs[1])
            with _jax.named_scope("update"):
                return (loss_sum + l_k, _jax.tree.map(_jnp.add, grad_sum, gw_k)), gx_k

        init = (_jnp.zeros((), _jnp.float32), _jax.tree.map(_jnp.zeros_like, weights))
        (loss, grad_w), grad_x = _jax.lax.scan(body, init, (per_example, given["loss_target"]))
    with _jax.named_scope("update"):
        delta_w, new_m, new_v = {}, {}, {}
        for n in TWIN_WEIGHTS:
            delta_w[n], new_m[n], new_v[n] = _adamw(weights[n], grad_w[n], given["m_" + n], given["v_" + n])
    return (loss, grad_x, *[grad_w[n] for n in TWIN_WEIGHTS], *[delta_w[n] for n in TWIN_WEIGHTS],
            *[new_m[n] for n in TWIN_WEIGHTS], *[new_v[n] for n in TWIN_WEIGHTS])
```

```python
import functools
import math

import jax
import jax.numpy as jnp
from jax import lax
from jax.experimental import pallas as pl
from jax.experimental.pallas import tpu as pltpu

F32 = jnp.float32
BF16 = jnp.bfloat16
_MXU = jnp.bfloat16
_HI = lax.Precision.HIGHEST

D_MODEL = 1024
N_META = 16
PAD = 112
ROW0 = PAD + N_META
MLA_HEADS = 4
QK_NOPE = 128
QK_ROPE = 64
QK_HEAD = QK_NOPE + QK_ROPE
V_HEAD = 128
Q_LORA = 256
KV_LORA = 256
ROPE_THETA = 10000.0
DN_HEADS = 4
DN_DIM = 128
DN_WIDTH = DN_HEADS * DN_DIM
DN_CONV = 4
DN_CHUNK = 64
D_FF = 2816
FFN_CONV = 3
EPS = 1e-6
HP = 256
C_QKV = 0
C_Z = 1536
C_QL = 2048
C_KVL = 2304
C_KPE = 2560
C_AB = 2688
IN_P = 2816
IN_COLS = 2632

ADAM_LR = 0.001
ADAM_B1 = 0.9
ADAM_B2 = 0.999
ADAM_EPS = 1e-08
ADAM_WD = 0.01
ADAM_STEP = 10

N_DEV = 8
TM = 128
LANE = 128
VMEM_LIMIT = 56 * 1024 * 1024
NEG = -1e30


def _dot(a, b, dims, hp=False):
    if hp:
        return lax.dot_general(a.astype(F32), b.astype(F32), (dims, ((), ())),
                               precision=_HI, preferred_element_type=F32)
    return lax.dot_general(a.astype(_MXU), b.astype(_MXU), (dims, ((), ())),
                           preferred_element_type=F32)


def _nn(a, b, hp=False):
    return _dot(a, b, ((1,), (0,)), hp)


def _nt(a, b, hp=False):
    return _dot(a, b, ((1,), (1,)), hp)


def _tn(a, b, hp=False):
    return _dot(a, b, ((0,), (0,)), hp)


def _sigmoid(x):
    return 1.0 / (1.0 + jnp.exp(-x))


def _rms_fwd(x, w, n):
    r = lax.rsqrt(jnp.sum(x * x, axis=-1, keepdims=True) * (1.0 / n) + EPS)
    return x * r * w, r


def _rms_bwd(x, w, dy, n):
    r = lax.rsqrt(jnp.sum(x * x, axis=-1, keepdims=True) * (1.0 / n) + EPS)
    xh = x * r
    gy = dy * w
    dx = r * (gy - xh * (jnp.sum(gy * xh, axis=-1, keepdims=True) * (1.0 / n)))
    return dx, dy * xh


def _rowsum(x):
    return jnp.sum(x, axis=0, keepdims=True)


def _row_ids(i, tm):
    return i * tm + lax.broadcasted_iota(jnp.int32, (tm, 1), 0)


def _shift_down(ext, s, tm):
    if s == 0:
        return ext[8:8 + tm]
    return pltpu.roll(ext, s, 0)[8:8 + tm]


def _shift_up(ext, s, tm):
    if s == 0:
        return ext[0:tm]
    return pltpu.roll(ext, tm + 8 - s, 0)[0:tm]


def _conv_fwd(x, halo_prev, w, width):
    tm = x.shape[0]
    ext = jnp.concatenate([halo_prev, x], axis=0)
    y = None
    for j in range(width):
        t = w[j:j + 1, :] * _shift_down(ext, width - 1 - j, tm)
        y = t if y is None else y + t
    return y


def _conv_bwd_x(dy, halo_next, w, width):
    tm = dy.shape[0]
    ext = jnp.concatenate([dy, halo_next], axis=0)
    dx = None
    for j in range(width):
        t = w[j:j + 1, :] * _shift_up(ext, width - 1 - j, tm)
        dx = t if dx is None else dx + t
    return dx


def _conv_bwd_w(dy, x, halo_prev, width):
    tm = dy.shape[0]
    ext = jnp.concatenate([halo_prev, x], axis=0)
    rows = [_rowsum(dy * _shift_down(ext, width - 1 - j, tm)) for j in range(width)]
    rows += [jnp.zeros_like(rows[0])] * (8 - width)
    return jnp.concatenate(rows, axis=0)


def _softplus(x):
    e = jnp.exp(-jnp.abs(x))
    u = 1.0 + e
    l1p = jnp.where(u == 1.0, e, jnp.log(u) * e / jnp.where(u == 1.0, 1.0, u - 1.0))
    return jnp.maximum(x, 0.0) + l1p


def _swap_halves(x):
    lane = lax.broadcasted_iota(jnp.int32, x.shape, 1)
    return jnp.where(lane < 32, pltpu.roll(x, 96, 1), jnp.where(lane < 64, pltpu.roll(x, 32, 1), 0.0))


class _In:
    def __init__(self, arr, width=None, cb=0, kind="cur"):
        self.arr, self.kind = arr, kind
        self.width = arr.shape[1] if width is None else width
        self.cb = cb


def _rows(name, fn, tiled, full, outs, accs=(), tm=TM):
    tp = tiled[0].arr.shape[0]
    nt = tp // tm
    r8 = tm // 8
    n_in = len(tiled) + len(full)
    n_out = len(outs)

    def body(*refs):
        i = pl.program_id(0)
        vals = [r[...] for r in refs[:n_in]]
        o_t, o_a = fn(i, *vals)
        for r, v in zip(refs[n_in:n_in + n_out], o_t):
            r[...] = v.astype(r.dtype)
        for r, v in zip(refs[n_in + n_out:], o_a):
            @pl.when(i == 0)
            def _():
                r[...] = v

            @pl.when(i > 0)
            def _():
                r[...] += v

    def spec(t):
        if t.kind == "cur":
            return pl.BlockSpec((tm, t.width), lambda i, cb=t.cb: (i, cb))
        if t.kind == "prev":
            return pl.BlockSpec((8, t.width), lambda i, cb=t.cb: (jnp.maximum(i * r8 - 1, 0), cb))
        return pl.BlockSpec((8, t.width), lambda i, cb=t.cb: (jnp.minimum((i + 1) * r8, tp // 8 - 1), cb))

    in_specs = [spec(t) for t in tiled]
    in_specs += [pl.BlockSpec(a.shape, lambda i, nd=a.ndim: (0,) * nd) for a in full]
    out_specs = [pl.BlockSpec((tm, w), lambda i: (i, 0)) for w, _ in outs]
    out_specs += [pl.BlockSpec((r, w), lambda i: (0, 0)) for r, w in accs]
    out_shape = [jax.ShapeDtypeStruct((tp, w), dt) for w, dt in outs]
    out_shape += [jax.ShapeDtypeStruct((r, w), F32) for r, w in accs]
    res = pl.pallas_call(
        body, name=name, grid=(nt,), in_specs=in_specs, out_specs=out_specs, out_shape=out_shape,
        compiler_params=pltpu.CompilerParams(dimension_semantics=("arbitrary",), vmem_limit_bytes=VMEM_LIMIT),
    )(*[t.arr for t in tiled], *full)
    return res


def _pick(n, cap, mult):
    best = None
    for d in range(mult, min(n, cap) + 1, mult):
        if n % d == 0:
            best = d
    assert best is not None, (n, cap, mult)
    return best


def _mm(name, a, b, mode, out_dtype=F32, resid=None):
    if mode == "tn":
        m, k = a.shape
        n = b.shape[1]
        tm = _pick(m, 640, 16)
        tk = _pick(k, 1408, 128)
        tn = _pick(n, 1408, 128)
        nm = m // tm

        def body_tn(a_ref, b_ref, o_ref):
            s = pl.program_id(2)
            acc = _tn(a_ref[...], b_ref[...])

            @pl.when(s == 0)
            def _():
                o_ref[...] = acc

            @pl.when(s > 0)
            def _():
                o_ref[...] += acc

        return pl.pallas_call(
            body_tn, name=name, grid=(k // tk, n // tn, nm),
            in_specs=[pl.BlockSpec((tm, tk), lambda p, j, s: (s, p)),
                      pl.BlockSpec((tm, tn), lambda p, j, s: (s, j))],
            out_specs=pl.BlockSpec((tk, tn), lambda p, j, s: (p, j)),
            out_shape=jax.ShapeDtypeStruct((k, n), F32),
            compiler_params=pltpu.CompilerParams(
                dimension_semantics=("parallel", "parallel", "arbitrary"), vmem_limit_bytes=VMEM_LIMIT),
        )(a, b)

    m, k = a.shape
    n = b.shape[1] if mode == "nn" else b.shape[0]
    tn = _pick(n, 1408, 128)
    tm = _pick(m, 640 if k <= 3072 else 320, 16)
    dotf = _nn if mode == "nn" else _nt

    def body(*refs):
        if resid is None:
            a_ref, b_ref, o_ref = refs
            o_ref[...] = dotf(a_ref[...], b_ref[...]).astype(o_ref.dtype)
        else:
            a_ref, b_ref, r_ref, o_ref = refs
            o_ref[...] = (r_ref[...] + dotf(a_ref[...], b_ref[...])).astype(o_ref.dtype)

    b_spec = (pl.BlockSpec((k, tn), lambda j, i: (0, j)) if mode == "nn"
              else pl.BlockSpec((tn, k), lambda j, i: (j, 0)))
    in_specs = [pl.BlockSpec((tm, k), lambda j, i: (i, 0)), b_spec]
    args = [a, b]
    if resid is not None:
        in_specs.append(pl.BlockSpec((tm, tn), lambda j, i: (i, j)))
        args.append(resid)
    return pl.pallas_call(
        body, name=name, grid=(n // tn, m // tm), in_specs=in_specs,
        out_specs=pl.BlockSpec((tm, tn), lambda j, i: (i, j)),
        out_shape=jax.ShapeDtypeStruct((m, n), out_dtype),
        compiler_params=pltpu.CompilerParams(
            dimension_semantics=("parallel", "parallel"), vmem_limit_bytes=VMEM_LIMIT),
    )(*args)


def _attn_probs(q, k, i, tq, tp):
    s = _nt(q, k) * (1.0 / math.sqrt(QK_HEAD))
    row = i * tq + lax.broadcasted_iota(jnp.int32, (tq, tp), 0)
    col = lax.broadcasted_iota(jnp.int32, (tq, tp), 1)
    ok = (col <= row) & (col >= PAD)
    s = jnp.where(ok, s, NEG)
    m = jnp.max(s, axis=-1, keepdims=True)
    e = jnp.exp(s - m)
    e = jnp.where(ok, e, 0.0)
    l = jnp.sum(e, axis=-1, keepdims=True)
    return e / jnp.maximum(l, 1e-30)


def _attn_fwd(q, k, v):
    tp = q.shape[0]
    tq = TM

    def body(q_ref, k_ref, v_ref, o_ref):
        i = pl.program_id(1)
        p = _attn_probs(q_ref[...], k_ref[...], i, tq, tp)
        o_ref[...] = _nn(p, v_ref[...])

    return pl.pallas_call(
        body, name="attn_fwd", grid=(MLA_HEADS, tp // tq),
        in_specs=[pl.BlockSpec((tq, HP), lambda h, i: (i, h)),
                  pl.BlockSpec((tp, HP), lambda h, i: (0, h)),
                  pl.BlockSpec((tp, V_HEAD), lambda h, i: (0, h))],
        out_specs=pl.BlockSpec((tq, V_HEAD), lambda h, i: (i, h)),
        out_shape=jax.ShapeDtypeStruct((tp, MLA_HEADS * V_HEAD), F32),
        compiler_params=pltpu.CompilerParams(
            dimension_semantics=("parallel", "arbitrary"), vmem_limit_bytes=VMEM_LIMIT),
    )(q, k, v)


def _attn_bwd(q, k, v, do):
    tp = q.shape[0]
    tq = TM

    def body(q_ref, k_ref, v_ref, do_ref, dq_ref, dk_ref, dv_ref):
        i = pl.program_id(1)
        qb = q_ref[...]
        kk = k_ref[...]
        dob = do_ref[...]
        p = _attn_probs(qb, kk, i, tq, tp)
        dp = _nt(dob, v_ref[...])
        delta = jnp.sum(p * dp, axis=-1, keepdims=True)
        ds = p * (dp - delta) * (1.0 / math.sqrt(QK_HEAD))
        dq_ref[...] = _nn(ds, kk)
        dk = _tn(ds, qb)
        dv = _tn(p, dob)

        @pl.when(i == 0)
        def _():
            dk_ref[...] = dk
            dv_ref[...] = dv

        @pl.when(i > 0)
        def _():
            dk_ref[...] += dk
            dv_ref[...] += dv

    return pl.pallas_call(
        body, name="attn_bwd", grid=(MLA_HEADS, tp // tq),
        in_specs=[pl.BlockSpec((tq, HP), lambda h, i: (i, h)),
                  pl.BlockSpec((tp, HP), lambda h, i: (0, h)),
                  pl.BlockSpec((tp, V_HEAD), lambda h, i: (0, h)),
                  pl.BlockSpec((tq, V_HEAD), lambda h, i: (i, h))],
        out_specs=[pl.BlockSpec((tq, HP), lambda h, i: (i, h)),
                   pl.BlockSpec((tp, HP), lambda h, i: (0, h)),
                   pl.BlockSpec((tp, V_HEAD), lambda h, i: (0, h))],
        out_shape=[jax.ShapeDtypeStruct((tp, MLA_HEADS * HP), F32),
                   jax.ShapeDtypeStruct((tp, MLA_HEADS * HP), F32),
                   jax.ShapeDtypeStruct((tp, MLA_HEADS * V_HEAD), F32)],
        compiler_params=pltpu.CompilerParams(
            dimension_semantics=("parallel", "arbitrary"), vmem_limit_bytes=VMEM_LIMIT),
    )(q, k, v, do)


def _gdn_consts():
    c = DN_CHUNK
    r = lax.broadcasted_iota(jnp.int32, (c, c), 0)
    cc = lax.broadcasted_iota(jnp.int32, (c, c), 1)
    incl = r >= cc
    strict = r > cc
    return incl, strict


def _gdn_chunk_common(q_ref, k_ref, v_ref, g_ref, b_ref, rows):
    c = DN_CHUNK
    incl, strict = _gdn_consts()
    q = q_ref[rows, :] * (1.0 / math.sqrt(DN_DIM))
    k = k_ref[rows, :]
    v = v_ref[rows, :]
    g = g_ref[rows, :]
    beta = b_ref[rows, :]
    gc = _nn(incl.astype(F32), g, hp=True)
    gam = jnp.exp(gc)
    g_last = _rowsum(g)
    grow = _nt(jnp.full((c, LANE), 1.0 / LANE, F32), gc, hp=True)
    dm = jnp.exp(jnp.where(incl, gc[:, :c] - grow, NEG))
    kb = k * beta
    vb = v * beta
    kbg = kb * gam
    kk = _nt(kb, k)
    ek = jnp.exp(g_last - gc)
    kd = k * ek
    qk = _nt(q, k)
    return dict(q=q, k=k, v=v, beta=beta, gc=gc, gam=gam, g_last=g_last, dm=dm, kb=kb, vb=vb,
                kbg=kbg, kk=kk, ek=ek, kd=kd, qk=qk, incl=incl, strict=strict)


def _gdn_fwd(q, k, v, g, beta):
    tp = q.shape[0]
    c = DN_CHUNK
    nch = tp // c

    def body(q_ref, k_ref, v_ref, g_ref, b_ref, o_ref, s_ref, t_ref, s_scr):
        s_scr[...] = jnp.zeros_like(s_scr)
        eye = (lax.broadcasted_iota(jnp.int32, (c, c), 0) == lax.broadcasted_iota(jnp.int32, (c, c), 1)).astype(F32)

        def chunk(n, carry):
            rows = pl.ds(pl.multiple_of(n * c, c), c)
            x = _gdn_chunk_common(q_ref, k_ref, v_ref, g_ref, b_ref, rows)
            a = jnp.where(x["strict"], x["kk"] * x["dm"], 0.0)
            bp = -a
            t = eye + bp
            for _ in range(5):
                bp = _nn(bp, bp, hp=True)
                t = t + _nn(t, bp, hp=True)
            u = _nn(t, x["vb"])
            w = _nn(t, x["kbg"])
            s = s_scr[...]
            s_ref[0, n] = s
            t_ref[0, n] = t
            v_new = u - _nn(w, s)
            o_ref[rows, :] = _nn(x["q"] * x["gam"], s) + _nn(x["qk"] * x["dm"], v_new)
            s_scr[...] = s * jnp.exp(x["g_last"]) + _tn(x["kd"], v_new)
            return carry

        lax.fori_loop(0, nch, chunk, 0)

    hb = lambda h: (0, h)
    return pl.pallas_call(
        body, name="gdn_fwd", grid=(DN_HEADS,),
        in_specs=[pl.BlockSpec((tp, DN_DIM), hb)] * 5,
        out_specs=[pl.BlockSpec((tp, DN_DIM), hb),
                   pl.BlockSpec((1, nch, DN_DIM, DN_DIM), lambda h: (h, 0, 0, 0)),
                   pl.BlockSpec((1, nch, c, c), lambda h: (h, 0, 0, 0))],
        out_shape=[jax.ShapeDtypeStruct((tp, DN_WIDTH), F32),
                   jax.ShapeDtypeStruct((DN_HEADS, nch, DN_DIM, DN_DIM), F32),
                   jax.ShapeDtypeStruct((DN_HEADS, nch, c, c), F32)],
        scratch_shapes=[pltpu.VMEM((DN_DIM, DN_DIM), F32)],
        compiler_params=pltpu.CompilerParams(dimension_semantics=("parallel",), vmem_limit_bytes=VMEM_LIMIT),
    )(q, k, v, g, beta)


def _gdn_bwd(q, k, v, g, beta, s_all, t_all, do):
    tp = q.shape[0]
    c = DN_CHUNK
    nch = tp // c

    def body(q_ref, k_ref, v_ref, g_ref, b_ref, s_ref, t_ref, do_ref,
             dq_ref, dk_ref, dv_ref, dg_ref, db_ref, ds_scr):
        ds_scr[...] = jnp.zeros_like(ds_scr)
        ones_cl = jnp.ones((c, LANE), F32)

        def chunk(it, carry):
            n = nch - 1 - it
            rows = pl.ds(pl.multiple_of(n * c, c), c)
            x = _gdn_chunk_common(q_ref, k_ref, v_ref, g_ref, b_ref, rows)
            qs, kx, vx, beta_, gam, dm = x["q"], x["k"], x["v"], x["beta"], x["gam"], x["dm"]
            kb, vb, kbg, kd, ek = x["kb"], x["vb"], x["kbg"], x["kd"], x["ek"]
            t = t_ref[0, n]
            s = s_ref[0, n]
            dsn = ds_scr[...]
            dob = do_ref[rows, :]
            eg_last = jnp.exp(x["g_last"])
            u = _nn(t, vb)
            w = _nn(t, kbg)
            v_new = u - _nn(w, s)
            mqk = x["qk"] * dm
            qd = qs * gam
            dv_new = _tn(mqk, dob) + _nn(kd, dsn)
            ds_scr[...] = _tn(qd, dob) + eg_last * dsn - _tn(w, dv_new)
            dqd = _nt(dob, s)
            dq = dqd * gam
            dgam = jnp.sum(dqd * qs, axis=1, keepdims=True)
            dmm = jnp.where(x["incl"], _nt(dob, v_new), 0.0)
            dqk = dmm * dm
            e_mat = dmm * mqk
            dq = dq + _nn(dqk, kx)
            dk = _tn(dqk, qs)
            dkd = _nt(v_new, dsn)
            dk = dk + dkd * ek
            e1 = jnp.sum(dkd * kd, axis=1, keepdims=True)
            dgc = -e1
            dg_last = jnp.sum(e1) + eg_last * jnp.sum(s * dsn)
            dw = -_nt(dv_new, s)
            dvb = _tn(t, dv_new)
            dt = _nt(dv_new, vb) + _nt(dw, kbg)
            dkbg = _tn(t, dw)
            da = jnp.where(x["strict"], -_nt(_tn(t, dt, hp=True), t, hp=True), 0.0)
            dkk = da * dm
            e_mat = e_mat + da * x["kk"] * dm
            dkb = _nn(dkk, kx) + dkbg * gam
            dk = dk + _tn(dkk, kb)
            dgam = dgam + jnp.sum(dkbg * kb, axis=1, keepdims=True)
            dk = dk + dkb * beta_
            dbeta = jnp.sum(dkb * kx, axis=1, keepdims=True) + jnp.sum(dvb * vx, axis=1, keepdims=True)
            dv = dvb * beta_
            dgc = dgc + jnp.sum(e_mat, axis=1, keepdims=True) + dgam * gam
            dgc = dgc - _tn(e_mat, ones_cl, hp=True)
            upper = jnp.logical_not(x["strict"]).astype(F32)
            dg = _nn(upper, dgc, hp=True) + dg_last
            dq_ref[rows, :] = dq * (1.0 / math.sqrt(DN_DIM))
            dk_ref[rows, :] = dk
            dv_ref[rows, :] = dv
            dg_ref[rows, :] = dg
            db_ref[rows, :] = jnp.broadcast_to(dbeta, (c, LANE))
            return carry

        lax.fori_loop(0, nch, chunk, 0)

    hb = lambda h: (0, h)
    hs = lambda h: (h, 0, 0, 0)
    return pl.pallas_call(
        body, name="gdn_bwd", grid=(DN_HEADS,),
        in_specs=[pl.BlockSpec((tp, DN_DIM), hb)] * 5
        + [pl.BlockSpec((1, nch, DN_DIM, DN_DIM), hs), pl.BlockSpec((1, nch, c, c), hs),
           pl.BlockSpec((tp, DN_DIM), hb)],
        out_specs=[pl.BlockSpec((tp, DN_DIM), hb)] * 5,
        out_shape=[jax.ShapeDtypeStruct((tp, DN_WIDTH), F32)] * 5,
        scratch_shapes=[pltpu.VMEM((DN_DIM, DN_DIM), F32)],
        compiler_params=pltpu.CompilerParams(dimension_semantics=("parallel",), vmem_limit_bytes=VMEM_LIMIT),
    )(q, k, v, g, beta, s_all, t_all, do)


def _silu_parts(x):
    s = _sigmoid(x)
    return x * s, s * (1.0 + x * (1.0 - s))


def _f_rms_cast(i, x, w):
    y, _ = _rms_fwd(x, w, x.shape[1])
    return (y,), ()


def _f_rms_bwd_add(i, x, dy, dres, w, *, mask_pad):
    dx, dwr = _rms_bwd(x, w, dy, x.shape[1])
    out = dres + dx
    if mask_pad:
        out = jnp.where(_row_ids(i, x.shape[0]) >= PAD, out, 0.0)
    return (out,), (_rowsum(dwr),)


def _f_lat_norm(i, ql, kvl, qw, kvw):
    return (_rms_fwd(ql, qw, Q_LORA)[0], _rms_fwd(kvl, kvw, KV_LORA)[0]), ()


def _f_lat_norm_bwd(i, ql, kvl, dqn, dkvn, qw, kvw):
    dq, dqw = _rms_bwd(ql, qw, dqn, Q_LORA)
    dk, dkw = _rms_bwd(kvl, kvw, dkvn, KV_LORA)
    return (dq, dk), (_rowsum(dqw), _rowsum(dkw))


def _rope(x, cos, sin_s):
    return x * cos + _swap_halves(x) * sin_s


def _rope_t(dy, cos, sin_s):
    return dy * cos + _swap_halves(dy * sin_s)


def _f_mla_qk(i, qf, kvf, kpe, cos, sin_s, qw, kw):
    qs, ks, vs = [], [], []
    for h in range(MLA_HEADS):
        qn, _ = _rms_fwd(qf[:, HP * h:HP * (h + 1)], qw, QK_HEAD)
        qs += [qn[:, :QK_NOPE], _rope(qn[:, QK_NOPE:], cos, sin_s)]
        kh = jnp.concatenate([kvf[:, HP * h:HP * h + QK_NOPE], kpe], axis=1)
        kn, _ = _rms_fwd(kh, kw, QK_HEAD)
        ks += [kn[:, :QK_NOPE], _rope(kn[:, QK_NOPE:], cos, sin_s)]
        vs.append(kvf[:, HP * h + QK_NOPE:HP * (h + 1)])
    return (jnp.concatenate(qs, axis=1), jnp.concatenate(ks, axis=1), jnp.concatenate(vs, axis=1)), ()


def _f_mla_qk_bwd(i, qf, kvf, kpe, cos, sin_s, dq, dk, dv, qw, kw):
    dqf, dkvf = [], []
    dkpe = None
    dqw = None
    dkw = None
    for h in range(MLA_HEADS):
        dqh = dq[:, HP * h:HP * (h + 1)]
        dqn = jnp.concatenate([dqh[:, :QK_NOPE], _rope_t(dqh[:, QK_NOPE:], cos, sin_s)], axis=1)
        dx, dwr = _rms_bwd(qf[:, HP * h:HP * (h + 1)], qw, dqn, QK_HEAD)
        dqf.append(dx)
        dqw = _rowsum(dwr) if dqw is None else dqw + _rowsum(dwr)
        dkh = dk[:, HP * h:HP * (h + 1)]
        dkn = jnp.concatenate([dkh[:, :QK_NOPE], _rope_t(dkh[:, QK_NOPE:], cos, sin_s)], axis=1)
        kh = jnp.concatenate([kvf[:, HP * h:HP * h + QK_NOPE], kpe], axis=1)
        dx, dwr = _rms_bwd(kh, kw, dkn, QK_HEAD)
        dkvf += [dx[:, :QK_NOPE], dv[:, V_HEAD * h:V_HEAD * (h + 1)]]
        dkpe = dx[:, QK_NOPE:] if dkpe is None else dkpe + dx[:, QK_NOPE:]
        dkw = _rowsum(dwr) if dkw is None else dkw + _rowsum(dwr)
    return (jnp.concatenate(dqf, axis=1), jnp.concatenate(dkvf, axis=1), dkpe), (dqw, dkw)


def _gdn_act(i, x, halo, w8):
    tm = x.shape[0]
    halo = jnp.where(i > 0, halo, 0.0)
    c = _conv_fwd(x, halo, w8, DN_CONV)
    act, dact = _silu_parts(c)
    return act, dact


def _f_gdn_prep(i, x, halo, ab, w8, alog, dtb, sel):
    tm = x.shape[0]
    act, _ = _gdn_act(i, x, halo, w8)
    outs = []
    for part in range(2):
        for h in range(DN_HEADS):
            t = act[:, DN_WIDTH * part + DN_DIM * h:DN_WIDTH * part + DN_DIM * (h + 1)]
            outs.append(t * lax.rsqrt(jnp.sum(t * t, axis=-1, keepdims=True) + EPS))
    q = jnp.concatenate(outs[:DN_HEADS], axis=1)
    k = jnp.concatenate(outs[DN_HEADS:], axis=1)
    v = act[:, 2 * DN_WIDTH:]
    abb = _nn(ab, sel, hp=True)
    valid = _row_ids(i, tm) >= PAD
    g = jnp.where(valid, -jnp.exp(alog) * _softplus(abb[:, :DN_WIDTH] + dtb), 0.0)
    beta = jnp.where(valid, _sigmoid(abb[:, DN_WIDTH:]), 0.0)
    return (q, k, v, g, beta), ()


def _f_gdn_prep_bwd(i, x, halo, ab, dq, dk, dv, dg, dbeta, w8, alog, dtb, sel, selpick):
    tm = x.shape[0]
    act, dact = _gdn_act(i, x, halo, w8)
    douts = []
    for part, dd in enumerate((dq, dk)):
        for h in range(DN_HEADS):
            t = act[:, DN_WIDTH * part + DN_DIM * h:DN_WIDTH * part + DN_DIM * (h + 1)]
            r = lax.rsqrt(jnp.sum(t * t, axis=-1, keepdims=True) + EPS)
            y = t * r
            dy = dd[:, DN_DIM * h:DN_DIM * (h + 1)]
            douts.append(r * (dy - y * jnp.sum(dy * y, axis=-1, keepdims=True)))
    douts.append(dv)
    dc = jnp.concatenate(douts, axis=1) * dact
    abb = _nn(ab, sel, hp=True)
    valid = _row_ids(i, tm) >= PAD
    pre = abb[:, :DN_WIDTH] + dtb
    ea = jnp.exp(alog)
    g = -ea * _softplus(pre)
    dg = jnp.where(valid, dg, 0.0)
    dbeta = jnp.where(valid, dbeta, 0.0)
    da = dg * (-ea) * _sigmoid(pre)
    beta = _sigmoid(abb[:, DN_WIDTH:])
    db = dbeta * beta * (1.0 - beta)
    dab = _nn(jnp.concatenate([da, db], axis=1), selpick, hp=True)
    return (dc, dab), (_rowsum(dg * g), _rowsum(da))


def _f_conv_bwd(i, dy, dy_next, x, x_prev, w8, *, width, nt):
    dy_next = jnp.where(i < nt - 1, dy_next, 0.0)
    x_prev = jnp.where(i > 0, x_prev, 0.0)
    return (_conv_bwd_x(dy, dy_next, w8, width),), (_conv_bwd_w(dy, x, x_prev, width),)


def _f_mix(i, o_mla, o_dn, z, w_mla, w_dn):
    tm = o_mla.shape[0]
    valid = _row_ids(i, tm) >= PAD
    outs = []
    for h in range(MLA_HEADS):
        y, _ = _rms_fwd(o_mla[:, V_HEAD * h:V_HEAD * (h + 1)], w_mla, V_HEAD)
        outs.append(jnp.where(valid, y, 0.0))
    for h in range(DN_HEADS):
        y, _ = _rms_fwd(o_dn[:, DN_DIM * h:DN_DIM * (h + 1)], w_dn, DN_DIM)
        outs.append(y * _silu_parts(z[:, DN_DIM * h:DN_DIM * (h + 1)])[0])
    return (jnp.concatenate(outs, axis=1),), ()


def _f_mix_bwd(i, o_mla, o_dn, z, dy_mla, dy_dn, w_mla, w_dn):
    tm = o_mla.shape[0]
    valid = _row_ids(i, tm) >= PAD
    d_mla, d_dn, d_z = [], [], []
    dw_mla = None
    dw_dn = None
    for h in range(MLA_HEADS):
        sl = slice(V_HEAD * h, V_HEAD * (h + 1))
        dx, dwr = _rms_bwd(o_mla[:, sl], w_mla, jnp.where(valid, dy_mla[:, sl], 0.0), V_HEAD)
        d_mla.append(dx)
        dw_mla = _rowsum(dwr) if dw_mla is None else dw_mla + _rowsum(dwr)
    for h in range(DN_HEADS):
        sl = slice(DN_DIM * h, DN_DIM * (h + 1))
        y, _ = _rms_fwd(o_dn[:, sl], w_dn, DN_DIM)
        sz, dsz = _silu_parts(z[:, sl])
        d_z.append(dy_dn[:, sl] * y * dsz)
        dx, dwr = _rms_bwd(o_dn[:, sl], w_dn, dy_dn[:, sl] * sz, DN_DIM)
        d_dn.append(dx)
        dw_dn = _rowsum(dwr) if dw_dn is None else dw_dn + _rowsum(dwr)
    return ((jnp.concatenate(d_mla, axis=1), jnp.concatenate(d_dn, axis=1), jnp.concatenate(d_z, axis=1)),
            (dw_mla, dw_dn))


def _f_ffn_act(i, gate_pre, halo, up, w8, b):
    halo = jnp.where(i > 0, halo, 0.0)
    gate = _conv_fwd(gate_pre, halo, w8, FFN_CONV) + b
    return (_silu_parts(gate)[0] * up,), ()


def _f_ffn_act_bwd(i, gate_pre, halo, up, dact, w8, b):
    halo = jnp.where(i > 0, halo, 0.0)
    gate = _conv_fwd(gate_pre, halo, w8, FFN_CONV) + b
    sg, dsg = _silu_parts(gate)
    dgate = dact * up * dsg
    return (dgate, dact * sg), (_rowsum(dgate),)


def _f_loss(i, h3, tgt):
    tm = h3.shape[0]
    diff = jnp.where(_row_ids(i, tm) >= ROW0, h3 - tgt, 0.0)
    part = 0.5 * jnp.sum(diff * diff) * (1.0 / D_MODEL)
    return (diff * (1.0 / D_MODEL),), (jnp.full((1, LANE), part, F32),)


def _local_step(h0, tgt, w):
    tp = h0.shape[0]
    nt = tp // TM
    bf = (D_MODEL, _MXU)
    u, = _rows("rms_in", _f_rms_cast, [_In(h0)], [w["attn_norm_w"]], [bf])
    proj = _mm("in_proj", u, w["w_in"], "nn")
    p_qkv = lambda kind="cur": _In(proj, 3 * DN_WIDTH, 0, kind)
    p_z = _In(proj, DN_WIDTH, C_Z // DN_WIDTH)
    p_ql = _In(proj, Q_LORA, C_QL // Q_LORA)
    p_kvl = _In(proj, KV_LORA, C_KVL // KV_LORA)
    p_kpe = _In(proj, LANE, C_KPE // LANE)
    p_ab = _In(proj, LANE, C_AB // LANE)
    cos, sin_s = _In(w["cos"]), _In(w["sin_s"])

    qn, kvn = _rows("mla_lat_norm", _f_lat_norm, [p_ql, p_kvl], [w["q_a_norm_w"], w["kv_a_norm_w"]],
                    [(Q_LORA, _MXU), (KV_LORA, _MXU)])
    qf = _mm("mla_q_b", qn, w["w_q_b"], "nn")
    kvf = _mm("mla_kv_b", kvn, w["w_kv_b"], "nn")
    qk_w = [w["q_norm_w"], w["k_norm_w"]]
    q, k, v = _rows("mla_qk", _f_mla_qk, [_In(qf), _In(kvf), p_kpe, cos, sin_s], qk_w,
                    [(MLA_HEADS * HP, _MXU), (MLA_HEADS * HP, _MXU), (MLA_HEADS * V_HEAD, _MXU)])
    o_mla = _attn_fwd(q, k, v)

    dn_w = [w["dn_conv_w"], w["alog_b"], w["dtb_b"], w["sel"]]
    gq, gk, gv, gg, gb = _rows("gdn_prep", _f_gdn_prep, [p_qkv(), p_qkv("prev"), p_ab], dn_w,
                               [(DN_WIDTH, F32)] * 5)
    o_dn, s_all, t_all = _gdn_fwd(gq, gk, gv, gg, gb)

    out_w = [w["mla_out_norm_w"], w["dn_out_norm_w"]]
    mixed, = _rows("mix", _f_mix, [_In(o_mla), _In(o_dn), p_z], out_w, [bf])
    h2 = _mm("out_proj", mixed, w["w_out"], "nn", resid=h0)

    hn, = _rows("rms_ffn", _f_rms_cast, [_In(h2)], [w["ffn_norm_w"]], [bf])
    gate_pre = _mm("ffn_gate", hn, w["w_gate"], "nn")
    up = _mm("ffn_up", hn, w["w_up"], "nn")
    ffn_w = [w["ffn_conv_w"], w["ffn_conv_b"]]
    act, = _rows("ffn_act", _f_ffn_act, [_In(gate_pre), _In(gate_pre, kind="prev"), _In(up)], ffn_w,
                 [(D_FF, _MXU)])
    h3 = _mm("ffn_down", act, w["w_down"], "nn", resid=h2)

    dh3, loss = _rows("loss", _f_loss, [_In(h3), _In(tgt)], [], [(D_MODEL, F32)], [(1, LANE)])

    g = {}
    dact = _mm("ffn_down_dx", dh3, w["w_down"], "nt")
    g["w_down"] = _mm("ffn_down_dw", act, dh3, "tn")
    dgate, dup, g["ffn_conv_b"] = _rows(
        "ffn_act_bwd", _f_ffn_act_bwd, [_In(gate_pre), _In(gate_pre, kind="prev"), _In(up), _In(dact)], ffn_w,
        [(D_FF, F32), (D_FF, F32)], [(1, D_FF)])
    dgate_pre, g["ffn_conv_w"] = _rows(
        "ffn_conv_bwd", functools.partial(_f_conv_bwd, width=FFN_CONV, nt=nt),
        [_In(dgate), _In(dgate, kind="next"), _In(gate_pre), _In(gate_pre, kind="prev")], [w["ffn_conv_w"]],
        [(D_FF, F32)], [(8, D_FF)])
    dhn = _mm("ffn_gate_dx", dgate_pre, w["w_gate"], "nt")
    dhn = _mm("ffn_up_dx", dup, w["w_up"], "nt", resid=dhn)
    g["w_gate"] = _mm("ffn_gate_dw", hn, dgate_pre, "tn")
    g["w_up"] = _mm("ffn_up_dw", hn, dup, "tn")
    dh2, g["ffn_norm_w"] = _rows(
        "rms_ffn_bwd", functools.partial(_f_rms_bwd_add, mask_pad=True), [_In(h2), _In(dhn), _In(dh3)],
        [w["ffn_norm_w"]], [(D_MODEL, F32)], [(1, D_MODEL)])

    dmixed = _mm("out_proj_dx", dh2, w["w_out"], "nt")
    g["w_out"] = _mm("out_proj_dw", mixed, dh2, "tn")
    half = MLA_HEADS * V_HEAD
    do_mla, do_dn, dz, g["mla_out_norm_w"], g["dn_out_norm_w"] = _rows(
        "mix_bwd", _f_mix_bwd, [_In(o_mla), _In(o_dn), p_z, _In(dmixed, half, 0), _In(dmixed, half, 1)], out_w,
        [(half, F32), (DN_WIDTH, F32), (DN_WIDTH, F32)], [(1, V_HEAD), (1, DN_DIM)])

    dq, dk, dv = _attn_bwd(q, k, v, do_mla)
    dqf, dkvf, dkpe, g["q_norm_w"], g["k_norm_w"] = _rows(
        "mla_qk_bwd", _f_mla_qk_bwd, [_In(qf), _In(kvf), p_kpe, cos, sin_s, _In(dq), _In(dk), _In(dv)], qk_w,
        [(MLA_HEADS * HP, F32), (MLA_HEADS * HP, F32), (LANE, F32)], [(1, HP), (1, HP)])
    dqn = _mm("mla_q_b_dx", dqf, w["w_q_b"], "nt")
    g["w_q_b"] = _mm("mla_q_b_dw", qn, dqf, "tn")
    dkvn = _mm("mla_kv_b_dx", dkvf, w["w_kv_b"], "nt")
    g["w_kv_b"] = _mm("mla_kv_b_dw", kvn, dkvf, "tn")
    dql, dkvl, g["q_a_norm_w"], g["kv_a_norm_w"] = _rows(
        "mla_lat_norm_bwd", _f_lat_norm_bwd, [p_ql, p_kvl, _In(dqn), _In(dkvn)],
        [w["q_a_norm_w"], w["kv_a_norm_w"]], [(Q_LORA, F32), (KV_LORA, F32)], [(1, Q_LORA), (1, KV_LORA)])

    dgq, dgk, dgv, dgg, dgb = _gdn_bwd(gq, gk, gv, gg, gb, s_all, t_all, do_dn)
    dc, dab, g["alog_b"], g["dtb_b"] = _rows(
        "gdn_prep_bwd", _f_gdn_prep_bwd,
        [p_qkv(), p_qkv("prev"), p_ab, _In(dgq), _In(dgk), _In(dgv), _In(dgg), _In(dgb)], dn_w + [w["selpick"]],
        [(3 * DN_WIDTH, F32), (LANE, F32)], [(1, DN_WIDTH), (1, DN_WIDTH)])
    dqkv, g["dn_conv_w"] = _rows(
        "gdn_conv_bwd", functools.partial(_f_conv_bwd, width=DN_CONV, nt=nt),
        [_In(dc), _In(dc, kind="next"), p_qkv(), p_qkv("prev")], [w["dn_conv_w"]],
        [(3 * DN_WIDTH, F32)], [(8, 3 * DN_WIDTH)])

    dproj = jnp.concatenate([dqkv, dz, dql, dkvl, dkpe, dab], axis=1)
    du = _mm("in_proj_dx", dproj, w["w_in"], "nt")
    g["w_in"] = _mm("in_proj_dw", u, dproj, "tn")
    dh0, g["attn_norm_w"] = _rows(
        "rms_in_bwd", functools.partial(_f_rms_bwd_add, mask_pad=False), [_In(h0), _In(du), _In(dh2)],
        [w["attn_norm_w"]], [(D_MODEL, F32)], [(1, D_MODEL)])
    return loss, dh0, g


def _w_in_to_padded(w):
    c1, c2, c3 = Q_LORA, Q_LORA + KV_LORA, Q_LORA + KV_LORA + QK_ROPE
    c4 = c3 + 3 * DN_WIDTH
    c5 = c4 + DN_WIDTH
    z = lambda n: jnp.zeros((w.shape[0], n), w.dtype)
    return jnp.concatenate([w[:, c3:c4], w[:, c4:c5], w[:, :c1], w[:, c1:c2], w[:, c2:c3], z(LANE - QK_ROPE),
                            w[:, c5:], z(LANE - 2 * DN_HEADS)], axis=1)


def _w_in_from_padded(g):
    return jnp.concatenate([g[:, C_QL:C_QL + Q_LORA], g[:, C_KVL:C_KVL + KV_LORA], g[:, C_KPE:C_KPE + QK_ROPE],
                            g[:, :C_Z + DN_WIDTH], g[:, C_AB:C_AB + 2 * DN_HEADS]], axis=1)


def _w_q_b_to_padded(w):
    r = w.shape[0]
    w = w.reshape(r, MLA_HEADS, QK_HEAD)
    return jnp.pad(w, ((0, 0), (0, 0), (0, HP - QK_HEAD))).reshape(r, MLA_HEADS * HP)


def _w_q_b_from_padded(g):
    r = g.shape[0]
    return g.reshape(r, MLA_HEADS, HP)[:, :, :QK_HEAD].reshape(r, MLA_HEADS * QK_HEAD)


def _pad_rows8(w):
    return jnp.pad(w, ((0, 8 - w.shape[0]), (0, 0)))


def _prepare(full, tp):
    w = {}
    mx = lambda a: a.astype(_MXU)
    w["attn_norm_w"] = full["attn_norm_w"]
    w["w_in"] = mx(_w_in_to_padded(full["w_in"]))
    w["q_a_norm_w"] = full["q_a_norm_w"]
    w["kv_a_norm_w"] = full["kv_a_norm_w"]
    w["w_q_b"] = mx(_w_q_b_to_padded(full["w_q_b"]))
    w["w_kv_b"] = mx(full["w_kv_b"])
    w["q_norm_w"] = jnp.pad(full["q_norm_w"], ((0, 0), (0, HP - QK_HEAD)))
    w["k_norm_w"] = jnp.pad(full["k_norm_w"], ((0, 0), (0, HP - QK_HEAD)))
    w["mla_out_norm_w"] = full["mla_out_norm_w"]
    w["dn_out_norm_w"] = full["dn_out_norm_w"]
    w["dn_conv_w"] = _pad_rows8(full["dn_conv_w"])
    w["alog_b"] = jnp.repeat(full["dn_A_log"], DN_DIM, axis=1)
    w["dtb_b"] = jnp.repeat(full["dn_dt_bias"], DN_DIM, axis=1)
    w["w_out"] = mx(full["w_out"])
    w["ffn_norm_w"] = full["ffn_norm_w"]
    w["w_gate"] = mx(full["w_gate"])
    w["w_up"] = mx(full["w_up"])
    w["ffn_conv_w"] = _pad_rows8(full["ffn_conv_w"])
    w["ffn_conv_b"] = full["ffn_conv_b"]
    w["w_down"] = mx(full["w_down"])
    half = QK_ROPE // 2
    inv = ROPE_THETA ** (-jnp.arange(half, dtype=F32) / half)
    ang = (jnp.arange(tp, dtype=jnp.int32) - PAD).astype(F32)[:, None] * inv[None, :]
    zc = jnp.zeros((tp, LANE - QK_ROPE), F32)
    w["cos"] = jnp.concatenate([jnp.cos(ang), jnp.cos(ang), zc], axis=1)
    w["sin_s"] = jnp.concatenate([-jnp.sin(ang), jnp.sin(ang), zc], axis=1)
    lane = jnp.arange(2 * DN_WIDTH)[None, :]
    src = jnp.arange(LANE)[:, None]
    w["sel"] = ((lane // DN_DIM) == src).astype(F32)
    w["selpick"] = ((src.T == (lane.T // DN_DIM)) & (lane.T % DN_DIM == 0)).astype(F32)
    return w


def _grads_to_natural(g):
    n = dict(g)
    n["w_in"] = _w_in_from_padded(g["w_in"])
    n["w_q_b"] = _w_q_b_from_padded(g["w_q_b"])
    n["q_norm_w"] = g["q_norm_w"][:, :QK_HEAD]
    n["k_norm_w"] = g["k_norm_w"][:, :QK_HEAD]
    n["dn_conv_w"] = g["dn_conv_w"][:DN_CONV]
    n["ffn_conv_w"] = g["ffn_conv_w"][:FFN_CONV]
    n["dn_A_log"] = n.pop("alog_b")[:, ::DN_DIM]
    n["dn_dt_bias"] = n.pop("dtb_b")[:, ::DN_DIM]
    return n


_MESH = pl.DeviceIdType.MESH
_ANY = pl.BlockSpec(memory_space=pl.ANY)
_CHIP_FLIPS = ((1, 0), (0, 1), (1, 1))


def _me():
    return lax.axis_index("x"), lax.axis_index("y"), lax.axis_index("c")


def _all_gather(name, blk):
    def body(x_ref, out_ref, send_sems, recv_sems, local_sem):
        x, y, c = _me()
        me, sib = (x, y, c), (x, y, 1 - c)
        chips = [(x ^ fx, y ^ fy) for fx, fy in _CHIP_FLIPS]

        def slot(p):
            return out_ref.at[4 * p[0] + 2 * p[1] + p[2]]

        def copy(k, block, to, src=None):
            return pltpu.make_async_remote_copy(
                src_ref=slot(block) if src is None else src, dst_ref=slot(block),
                send_sem=send_sems.at[k], recv_sem=recv_sems.at[k], device_id=to, device_id_type=_MESH)

        mine = pltpu.make_async_copy(x_ref, slot(me), local_sem)
        mine.start()
        first = [copy(0, me, sib, src=x_ref)]
        first += [copy(1 + j, me, (*chip, c), src=x_ref) for j, chip in enumerate(chips)]
        for cp in first:
            cp.start()
        passed = [copy(4 + j, (*chip, c), sib) for j, chip in enumerate(chips)]
        for j, chip in enumerate(chips):
            copy(1 + j, (*chip, c), me).wait_recv()
            passed[j].start()
        copy(0, sib, me).wait_recv()
        for j, chip in enumerate(chips):
            copy(4 + j, (*chip, 1 - c), me).wait_recv()
        for cp in first + passed:
            cp.wait_send()
        mine.wait()

    return pl.pallas_call(
        body, name=name, in_specs=[_ANY], out_specs=_ANY,
        out_shape=jax.ShapeDtypeStruct((N_DEV,) + blk.shape, blk.dtype),
        scratch_shapes=[pltpu.SemaphoreType.DMA((7,)), pltpu.SemaphoreType.DMA((7,)), pltpu.SemaphoreType.DMA],
    )(blk)


def _rs_sibling(name, gb):
    def body(g_ref, out_ref, send_sems, recv_sems):
        x, y, c = _me()
        cps = []
        for j in range(4):
            cp = pltpu.make_async_remote_copy(
                src_ref=g_ref.at[2 * j + (1 - c)], dst_ref=out_ref.at[j], send_sem=send_sems.at[j],
                recv_sem=recv_sems.at[j], device_id=(x, y, 1 - c), device_id_type=_MESH)
            cp.start()
            cps.append(cp)
        for cp in cps:
            cp.wait()

    return pl.pallas_call(
        body, name=name, in_specs=[_ANY], out_specs=_ANY,
        out_shape=jax.ShapeDtypeStruct((4,) + gb.shape[1:], gb.dtype),
        scratch_shapes=[pltpu.SemaphoreType.DMA((4,)), pltpu.SemaphoreType.DMA((4,))],
    )(gb)


def _rs_chips(name, s1):
    def body(s_ref, out_ref, send_sems, recv_sems):
        x, y, c = _me()
        cps = []
        for k, (fx, fy) in enumerate(_CHIP_FLIPS):
            px, py = x ^ fx, y ^ fy
            cp = pltpu.make_async_remote_copy(
                src_ref=s_ref.at[2 * px + py], dst_ref=out_ref.at[k], send_sem=send_sems.at[k],
                recv_sem=recv_sems.at[k], device_id=(px, py, c), device_id_type=_MESH)
            cp.start()
            cps.append(cp)
        for cp in cps:
            cp.wait()

    return pl.pallas_call(
        body, name=name, in_specs=[_ANY], out_specs=_ANY,
        out_shape=jax.ShapeDtypeStruct((3,) + s1.shape[1:], s1.dtype),
        scratch_shapes=[pltpu.SemaphoreType.DMA((3,)), pltpu.SemaphoreType.DMA((3,))],
    )(s1)


def _row_tile(r):
    return _pick(r, 512, 8)


def _pair_sum(name, gb, recv):
    _, r, cols = gb.shape
    tm = _row_tile(r)
    c = lax.axis_index("c").astype(jnp.int32).reshape(1)

    def body(c_ref, a_ref, b_ref, o_ref):
        o_ref[...] = a_ref[...] + b_ref[...]

    return pl.pallas_call(
        body, name=name,
        grid_spec=pltpu.PrefetchScalarGridSpec(
            num_scalar_prefetch=1, grid=(4, r // tm),
            in_specs=[pl.BlockSpec((1, tm, cols), lambda j, i, c_ref: (2 * j + c_ref[0], i, 0)),
                      pl.BlockSpec((1, tm, cols), lambda j, i, c_ref: (j, i, 0))],
            out_specs=pl.BlockSpec((1, tm, cols), lambda j, i, c_ref: (j, i, 0))),
        out_shape=jax.ShapeDtypeStruct((4, r, cols), F32),
        compiler_params=pltpu.CompilerParams(dimension_semantics=("parallel", "parallel")),
    )(c, gb, recv)


def _adam(name, parts, w, m, v):
    r, cols = w.shape
    tm = _row_tile(r)
    idx = jnp.stack([jnp.asarray(s, jnp.int32) for _, s in parts])
    n = len(parts)

    def body(idx_ref, *refs):
        g = refs[0][0]
        for p_ref in refs[1:n]:
            g = g + p_ref[0]
        w_ref, m_ref, v_ref, g_out, d_out, m_out, v_out = refs[n:]
        m_new = ADAM_B1 * m_ref[...] + (1.0 - ADAM_B1) * g
        v_new = ADAM_B2 * v_ref[...] + (1.0 - ADAM_B2) * (g * g)
        m_hat = m_new / (1.0 - ADAM_B1 ** ADAM_STEP)
        v_hat = v_new / (1.0 - ADAM_B2 ** ADAM_STEP)
        g_out[...] = g
        d_out[...] = -ADAM_LR * (m_hat / (jnp.sqrt(v_hat) + ADAM_EPS) + ADAM_WD * w_ref[...])
        m_out[...] = m_new
        v_out[...] = v_new

    part_specs = [pl.BlockSpec((1, tm, cols), lambda i, idx_ref, p=p: (idx_ref[p], i, 0)) for p in range(n)]
    flat = pl.BlockSpec((tm, cols), lambda i, idx_ref: (i, 0))
    return pl.pallas_call(
        body, name=name,
        grid_spec=pltpu.PrefetchScalarGridSpec(
            num_scalar_prefetch=1, grid=(r // tm,), in_specs=part_specs + [flat] * 3, out_specs=[flat] * 4),
        out_shape=[jax.ShapeDtypeStruct((r, cols), F32)] * 4,
        compiler_params=pltpu.CompilerParams(dimension_semantics=("parallel",)),
    )(idx, *[a for a, _ in parts], w, m, v)


_SHARDED = (
    ("meta_tokens", 1, (N_META, D_MODEL)),
    ("w_in", 1, (D_MODEL, IN_COLS)),
    ("w_q_b", 1, (Q_LORA, MLA_HEADS * QK_HEAD)),
    ("w_kv_b", 1, (KV_LORA, MLA_HEADS * (QK_NOPE + V_HEAD))),
    ("dn_conv_w", 1, (DN_CONV, 3 * DN_WIDTH)),
    ("w_out", 0, (2 * DN_WIDTH, D_MODEL)),
    ("w_gate", 1, (D_MODEL, D_FF)),
    ("w_up", 1, (D_MODEL, D_FF)),
    ("ffn_conv_w", 1, (FFN_CONV, D_FF)),
    ("w_down", 0, (D_FF, D_MODEL)),
)
_MXU_GATHERED = ("w_in", "w_q_b", "w_kv_b", "w_out", "w_gate", "w_up", "w_down")
_F32_GATHERED = ("meta_tokens", "dn_conv_w", "ffn_conv_w")
_REPLICATED = (
    ("attn_norm_w", D_MODEL), ("q_a_norm_w", Q_LORA), ("kv_a_norm_w", KV_LORA), ("q_norm_w", QK_HEAD),
    ("k_norm_w", QK_HEAD), ("mla_out_norm_w", V_HEAD), ("dn_A_log", DN_HEADS), ("dn_dt_bias", DN_HEADS),
    ("dn_out_norm_w", DN_DIM), ("ffn_norm_w", D_MODEL), ("ffn_conv_b", D_FF),
)
_PACK_COLS = 1024
_SMALL_SHAPE = (8, 768)


def _local_shape(dim, shape):
    return (shape[0] // N_DEV, shape[1]) if dim == 0 else (shape[0], shape[1] // N_DEV)


def _pack_rows(n, mult):
    rows = -(-n // _PACK_COLS)
    return -(-rows // mult) * mult


def _pack(flats, mult, axis=0):
    cat = jnp.concatenate(flats, axis=-1)
    n = cat.shape[-1]
    r = _pack_rows(n, mult)
    pad = [(0, 0)] * (cat.ndim - 1) + [(0, r * _PACK_COLS - n)]
    return jnp.pad(cat, pad).reshape(cat.shape[:-1] + (r, _PACK_COLS))


def _to_blocks(full, dim):
    r, c = full.shape
    if dim == 0:
        return full.reshape(N_DEV, (r // N_DEV) * c)
    return full.reshape(r, N_DEV, c // N_DEV).transpose(1, 0, 2).reshape(N_DEV, r * (c // N_DEV))


def _from_blocks(blocks, dim, shape):
    r, c = shape
    if dim == 0:
        return blocks.reshape(r, c)
    return blocks.reshape(N_DEV, r, c // N_DEV).transpose(1, 0, 2).reshape(r, c)


def _split(flat, sizes):
    out, o = [], 0
    for s in sizes:
        out.append(flat[..., o:o + s])
        o += s
    return out


def _gather_weights(local, names, dtype, mult):
    specs = [s for s in _SHARDED if s[0] in names]
    pack = _pack([local[n].astype(dtype).reshape(-1) for n, _, _ in specs], mult)
    got = _all_gather("gather_" + "_".join(n[:5] for n in names[:2]), pack)
    flat = got.reshape(N_DEV, -1)
    sizes = [math.prod(_local_shape(d, s)) for _, d, s in specs]
    return {n: _from_blocks(p, d, s) for (n, d, s), p in zip(specs, _split(flat, sizes))}


def kernel(x, meta_tokens, attn_norm_w, w_in, q_a_norm_w, w_q_b, kv_a_norm_w, w_kv_b, q_norm_w, k_norm_w, mla_out_norm_w, dn_conv_w, dn_A_log, dn_dt_bias, dn_out_norm_w, w_out, ffn_norm_w, w_gate, w_up, ffn_conv_w, ffn_conv_b, w_down, loss_target, m_meta_tokens, m_attn_norm_w, m_w_in, m_q_a_norm_w, m_w_q_b, m_kv_a_norm_w, m_w_kv_b, m_q_norm_w, m_k_norm_w, m_mla_out_norm_w, m_dn_conv_w, m_dn_A_log, m_dn_dt_bias, m_dn_out_norm_w, m_w_out, m_ffn_norm_w, m_w_gate, m_w_up, m_ffn_conv_w, m_ffn_conv_b, m_w_down, v_meta_tokens, v_attn_norm_w, v_w_in, v_q_a_norm_w, v_w_q_b, v_kv_a_norm_w, v_w_kv_b, v_q_norm_w, v_k_norm_w, v_mla_out_norm_w, v_dn_conv_w, v_dn_A_log, v_dn_dt_bias, v_dn_out_norm_w, v_w_out, v_ffn_norm_w, v_w_gate, v_w_up, v_ffn_conv_w, v_ffn_conv_b, v_w_down):
    names = [n for n, _, _ in _SHARDED] + [n for n, _ in _REPLICATED]
    given = dict(locals())
    two_d = lambda a: a.reshape(a.shape[-2:])
    wl = {n: two_d(given[n]) for n in names}
    ml = {n: two_d(given["m_" + n]) for n in names}
    vl = {n: two_d(given["v_" + n]) for n in names}
    out_shapes = {n: given[n].shape for n in names}

    full = dict(wl)
    full.update(_gather_weights(wl, _MXU_GATHERED, _MXU, 16))
    full.update(_gather_weights(wl, _F32_GATHERED, F32, 8))

    seq = x.shape[1]
    tp = ROW0 + seq
    h0 = jnp.concatenate([jnp.zeros((PAD, D_MODEL), F32), full["meta_tokens"], x[0]], axis=0)
    tgt = jnp.concatenate([jnp.zeros((ROW0, D_MODEL), F32), loss_target[0]], axis=0)
    loss, dh0, g = _local_step(h0, tgt, _prepare(full, tp))
    g = _grads_to_natural(g)
    g["meta_tokens"] = dh0[PAD:ROW0]
    grad_x = dh0[ROW0:][None]

    gb = _pack([_to_blocks(g[n], d) for n, d, _ in _SHARDED], 8)
    from_sib = _rs_sibling("rs_sibling", gb)
    s1 = _pair_sum("rs_pair_sum", gb, from_sib)
    from_chips = _rs_chips("rs_chips", s1)
    my_chip = 2 * lax.axis_index("x") + lax.axis_index("y")
    flat = lambda d: _pack([d[n].reshape(-1) for n, _, _ in _SHARDED], 8)
    big = _adam("adam_sharded", [(s1, my_chip), (from_chips, 0), (from_chips, 1), (from_chips, 2)],
                flat(wl), flat(ml), flat(vl))
    sizes = [math.prod(_local_shape(d, s)) for _, d, s in _SHARDED]
    big = [dict(zip([n for n, _, _ in _SHARDED], _split(a.reshape(-1), sizes))) for a in big]

    def small(d, extra):
        cat = jnp.concatenate([d[n].reshape(-1) for n, _ in _REPLICATED] + [extra])
        return jnp.pad(cat, (0, math.prod(_SMALL_SHAPE) - cat.shape[0])).reshape(_SMALL_SHAPE)

    zero1 = jnp.zeros((1,), F32)
    parts = _all_gather("gather_small_grads", small(g, loss[0, :1]))
    sm = _adam("adam_replicated", [(parts, d) for d in range(N_DEV)], small(wl, zero1), small(ml, zero1),
               small(vl, zero1))
    rsizes = [n for _, n in _REPLICATED] + [1]
    sm = [dict(zip([n for n, _ in _REPLICATED] + ["loss"], _split(a.reshape(-1), rsizes))) for a in sm]

    outs = [sm[0]["loss"].reshape(()), grad_x]
    for kind in range(4):
        for n in ("meta_tokens", "attn_norm_w", "w_in", "q_a_norm_w", "w_q_b", "kv_a_norm_w", "w_kv_b", "q_norm_w",
                  "k_norm_w", "mla_out_norm_w", "dn_conv_w", "dn_A_log", "dn_dt_bias", "dn_out_norm_w", "w_out",
                  "ffn_norm_w", "w_gate", "w_up", "ffn_conv_w", "ffn_conv_b", "w_down"):
            src = big[kind] if n in big[kind] else sm[kind]
            outs.append(src[n].reshape(out_shapes[n]))
    return tuple(outs)
```

```python
import functools
import math

import jax
import jax.numpy as jnp
from jax import lax
from jax.experimental import pallas as pl
from jax.experimental.pallas import tpu as pltpu

F32 = jnp.float32
BF16 = jnp.bfloat16
_MXU = jnp.bfloat16
_HI = lax.Precision.HIGHEST

D_MODEL = 1024
N_META = 16
PAD = 112
ROW0 = PAD + N_META
MLA_HEADS = 4
QK_NOPE = 128
QK_ROPE = 64
QK_HEAD = QK_NOPE + QK_ROPE
V_HEAD = 128
Q_LORA = 256
KV_LORA = 256
ROPE_THETA = 10000.0
DN_HEADS = 4
DN_DIM = 128
DN_WIDTH = DN_HEADS * DN_DIM
DN_CONV = 4
DN_CHUNK = 64
D_FF = 2816
FFN_CONV = 3
EPS = 1e-6
HP = 256
C_QKV = 0
C_Z = 1536
C_QL = 2048
C_KVL = 2304
C_KPE = 2560
C_AB = 2688
IN_P = 2816
IN_COLS = 2632

ADAM_LR = 0.001
ADAM_B1 = 0.9
ADAM_B2 = 0.999
ADAM_EPS = 1e-08
ADAM_WD = 0.01
ADAM_STEP = 10

N_DEV = 8
TM = 128
LANE = 128
VMEM_LIMIT = 56 * 1024 * 1024
NEG = -1e30


def _dot(a, b, dims, hp=False):
    if hp:
        return lax.dot_general(a.astype(F32), b.astype(F32), (dims, ((), ())),
                               precision=_HI, preferred_element_type=F32)
    return lax.dot_general(a.astype(_MXU), b.astype(_MXU), (dims, ((), ())),
                           preferred_element_type=F32)


def _nn(a, b, hp=False):
    return _dot(a, b, ((1,), (0,)), hp)


def _nt(a, b, hp=False):
    return _dot(a, b, ((1,), (1,)), hp)


def _tn(a, b, hp=False):
    return _dot(a, b, ((0,), (0,)), hp)


def _sigmoid(x):
    return 1.0 / (1.0 + jnp.exp(-x))


def _rms_fwd(x, w, n):
    r = lax.rsqrt(jnp.sum(x * x, axis=-1, keepdims=True) * (1.0 / n) + EPS)
    return x * r * w, r


def _rms_bwd(x, w, dy, n):
    r = lax.rsqrt(jnp.sum(x * x, axis=-1, keepdims=True) * (1.0 / n) + EPS)
    xh = x * r
    gy = dy * w
    dx = r * (gy - xh * (jnp.sum(gy * xh, axis=-1, keepdims=True) * (1.0 / n)))
    return dx, dy * xh


def _rowsum(x):
    return jnp.sum(x, axis=0, keepdims=True)


def _row_ids(i, tm):
    return i * tm + lax.broadcasted_iota(jnp.int32, (tm, 1), 0)


def _shift_down(ext, s, tm):
    if s == 0:
        return ext[8:8 + tm]
    return pltpu.roll(ext, s, 0)[8:8 + tm]


def _shift_up(ext, s, tm):
    if s == 0:
        return ext[0:tm]
    return pltpu.roll(ext, tm + 8 - s, 0)[0:tm]


def _conv_fwd(x, halo_prev, w, width):
    tm = x.shape[0]
    ext = jnp.concatenate([halo_prev, x], axis=0)
    y = None
    for j in range(width):
        t = w[j:j + 1, :] * _shift_down(ext, width - 1 - j, tm)
        y = t if y is None else y + t
    return y


def _conv_bwd_x(dy, halo_next, w, width):
    tm = dy.shape[0]
    ext = jnp.concatenate([dy, halo_next], axis=0)
    dx = None
    for j in range(width):
        t = w[j:j + 1, :] * _shift_up(ext, width - 1 - j, tm)
        dx = t if dx is None else dx + t
    return dx


def _conv_bwd_w(dy, x, halo_prev, width):
    tm = dy.shape[0]
    ext = jnp.concatenate([halo_prev, x], axis=0)
    rows = [_rowsum(dy * _shift_down(ext, width - 1 - j, tm)) for j in range(width)]
    rows += [jnp.zeros_like(rows[0])] * (8 - width)
    return jnp.concatenate(rows, axis=0)


def _softplus(x):
    e = jnp.exp(-jnp.abs(x))
    u = 1.0 + e
    l1p = jnp.where(u == 1.0, e, jnp.log(u) * e / jnp.where(u == 1.0, 1.0, u - 1.0))
    return jnp.maximum(x, 0.0) + l1p


def _swap_halves(x):
    lane = lax.broadcasted_iota(jnp.int32, x.shape, 1)
    return jnp.where(lane < 32, pltpu.roll(x, 96, 1), jnp.where(lane < 64, pltpu.roll(x, 32, 1), 0.0))


class _In:
    def __init__(self, arr, width=None, cb=0, kind="cur"):
        self.arr, self.kind = arr, kind
        self.width = arr.shape[1] if width is None else width
        self.cb = cb


def _rows(name, fn, tiled, full, outs, accs=(), tm=TM):
    tp = tiled[0].arr.shape[0]
    nt = tp // tm
    r8 = tm // 8
    n_in = len(tiled) + len(full)
    n_out = len(outs)

    def body(*refs):
        i = pl.program_id(0)
        vals = [r[...] for r in refs[:n_in]]
        o_t, o_a = fn(i, *vals)
        for r, v in zip(refs[n_in:n_in + n_out], o_t):
            r[...] = v.astype(r.dtype)
        for r, v in zip(refs[n_in + n_out:], o_a):
            @pl.when(i == 0)
            def _():
                r[...] = v

            @pl.when(i > 0)
            def _():
                r[...] += v

    def spec(t):
        if t.kind == "cur":
            return pl.BlockSpec((tm, t.width), lambda i, cb=t.cb: (i, cb))
        if t.kind == "prev":
            return pl.BlockSpec((8, t.width), lambda i, cb=t.cb: (jnp.maximum(i * r8 - 1, 0), cb))
        return pl.BlockSpec((8, t.width), lambda i, cb=t.cb: (jnp.minimum((i + 1) * r8, tp // 8 - 1), cb))

    in_specs = [spec(t) for t in tiled]
    in_specs += [pl.BlockSpec(a.shape, lambda i, nd=a.ndim: (0,) * nd) for a in full]
    out_specs = [pl.BlockSpec((tm, w), lambda i: (i, 0)) for w, _ in outs]
    out_specs += [pl.BlockSpec((r, w), lambda i: (0, 0)) for r, w in accs]
    out_shape = [jax.ShapeDtypeStruct((tp, w), dt) for w, dt in outs]
    out_shape += [jax.ShapeDtypeStruct((r, w), F32) for r, w in accs]
    res = pl.pallas_call(
        body, name=name, grid=(nt,), in_specs=in_specs, out_specs=out_specs, out_shape=out_shape,
        compiler_params=pltpu.CompilerParams(dimension_semantics=("arbitrary",), vmem_limit_bytes=VMEM_LIMIT),
    )(*[t.arr for t in tiled], *full)
    return res


def _pick(n, cap, mult):
    best = None
    for d in range(mult, min(n, cap) + 1, mult):
        if n % d == 0:
            best = d
    assert best is not None, (n, cap, mult)
    return best


def _mm(name, a, b, mode, out_dtype=F32, resid=None):
    if mode == "tn":
        m, k = a.shape
        n = b.shape[1]
        tm = _pick(m, 640, 16)
        tk = _pick(k, 1408, 128)
        tn = _pick(n, 1408, 128)
        nm = m // tm

        def body_tn(a_ref, b_ref, o_ref):
            s = pl.program_id(2)
            acc = _tn(a_ref[...], b_ref[...])

            @pl.when(s == 0)
            def _():
                o_ref[...] = acc

            @pl.when(s > 0)
            def _():
                o_ref[...] += acc

        return pl.pallas_call(
            body_tn, name=name, grid=(k // tk, n // tn, nm),
            in_specs=[pl.BlockSpec((tm, tk), lambda p, j, s: (s, p)),
                      pl.BlockSpec((tm, tn), lambda p, j, s: (s, j))],
            out_specs=pl.BlockSpec((tk, tn), lambda p, j, s: (p, j)),
            out_shape=jax.ShapeDtypeStruct((k, n), F32),
            compiler_params=pltpu.CompilerParams(
                dimension_semantics=("parallel", "parallel", "arbitrary"), vmem_limit_bytes=VMEM_LIMIT),
        )(a, b)

    m, k = a.shape
    n = b.shape[1] if mode == "nn" else b.shape[0]
    tn = _pick(n, 1408, 128)
    tm = _pick(m, 640 if k <= 3072 else 320, 16)
    dotf = _nn if mode == "nn" else _nt

    def body(*refs):
        if resid is None:
            a_ref, b_ref, o_ref = refs
            o_ref[...] = dotf(a_ref[...], b_ref[...]).astype(o_ref.dtype)
        else:
            a_ref, b_ref, r_ref, o_ref = refs
            o_ref[...] = (r_ref[...] + dotf(a_ref[...], b_ref[...])).astype(o_ref.dtype)

    b_spec = (pl.BlockSpec((k, tn), lambda j, i: (0, j)) if mode == "nn"
              else pl.BlockSpec((tn, k), lambda j, i: (j, 0)))
    in_specs = [pl.BlockSpec((tm, k), lambda j, i: (i, 0)), b_spec]
    args = [a, b]
    if resid is not None:
        in_specs.append(pl.BlockSpec((tm, tn), lambda j, i: (i, j)))
        args.append(resid)
    return pl.pallas_call(
        body, name=name, grid=(n // tn, m // tm), in_specs=in_specs,
        out_specs=pl.BlockSpec((tm, tn), lambda j, i: (i, j)),
        out_shape=jax.ShapeDtypeStruct((m, n), out_dtype),
        compiler_params=pltpu.CompilerParams(
            dimension_semantics=("parallel", "parallel"), vmem_limit_bytes=VMEM_LIMIT),
    )(*args)


ATTN_Q_TILES = 4


def _attn_probs(q, k, row0):
    tq, tp = q.shape[0], k.shape[0]
    s = _nt(q, k) * (1.0 / math.sqrt(QK_HEAD))
    row = row0 + lax.broadcasted_iota(jnp.int32, (tq, tp), 0)
    col = lax.broadcasted_iota(jnp.int32, (tq, tp), 1)
    ok = (col <= row) & (col >= PAD)
    s = jnp.where(ok, s, NEG)
    m = jnp.max(s, axis=-1, keepdims=True)
    e = jnp.exp(s - m)
    e = jnp.where(ok, e, 0.0)
    l = jnp.sum(e, axis=-1, keepdims=True)
    return e / jnp.maximum(l, 1e-30)


def _attn_fwd(q, k, v):
    tp = q.shape[0]
    tq = tp // ATTN_Q_TILES

    def body(q_ref, k_ref, v_ref, o_ref):
        for i in range(ATTN_Q_TILES):
            rows = slice(i * tq, (i + 1) * tq)
            keys = slice(0, (i + 1) * tq)
            p = _attn_probs(q_ref[rows, :], k_ref[keys, :], i * tq)
            o_ref[rows, :] = _nn(p, v_ref[keys, :])

    return pl.pallas_call(
        body, name="attn_fwd", grid=(MLA_HEADS,),
        in_specs=[pl.BlockSpec((tp, HP), lambda h: (0, h)),
                  pl.BlockSpec((tp, HP), lambda h: (0, h)),
                  pl.BlockSpec((tp, V_HEAD), lambda h: (0, h))],
        out_specs=pl.BlockSpec((tp, V_HEAD), lambda h: (0, h)),
        out_shape=jax.ShapeDtypeStruct((tp, MLA_HEADS * V_HEAD), F32),
        compiler_params=pltpu.CompilerParams(dimension_semantics=("parallel",), vmem_limit_bytes=VMEM_LIMIT),
    )(q, k, v)


def _attn_bwd(q, k, v, do):
    tp = q.shape[0]
    tq = tp // ATTN_Q_TILES

    def body(q_ref, k_ref, v_ref, do_ref, dq_ref, dk_ref, dv_ref):
        for i in reversed(range(ATTN_Q_TILES)):
            rows = slice(i * tq, (i + 1) * tq)
            keys = slice(0, (i + 1) * tq)
            qb = q_ref[rows, :]
            kk = k_ref[keys, :]
            dob = do_ref[rows, :]
            p = _attn_probs(qb, kk, i * tq)
            dp = _nt(dob, v_ref[keys, :])
            delta = jnp.sum(p * dp, axis=-1, keepdims=True)
            ds = p * (dp - delta) * (1.0 / math.sqrt(QK_HEAD))
            dq_ref[rows, :] = _nn(ds, kk)
            if i == ATTN_Q_TILES - 1:
                dk_ref[...] = _tn(ds, qb)
                dv_ref[...] = _tn(p, dob)
            else:
                dk_ref[keys, :] += _tn(ds, qb)
                dv_ref[keys, :] += _tn(p, dob)

    full = lambda w: pl.BlockSpec((tp, w), lambda h: (0, h))
    return pl.pallas_call(
        body, name="attn_bwd", grid=(MLA_HEADS,),
        in_specs=[full(HP), full(HP), full(V_HEAD), full(V_HEAD)],
        out_specs=[full(HP), full(HP), full(V_HEAD)],
        out_shape=[jax.ShapeDtypeStruct((tp, MLA_HEADS * HP), F32),
                   jax.ShapeDtypeStruct((tp, MLA_HEADS * HP), F32),
                   jax.ShapeDtypeStruct((tp, MLA_HEADS * V_HEAD), F32)],
        compiler_params=pltpu.CompilerParams(dimension_semantics=("parallel",), vmem_limit_bytes=VMEM_LIMIT),
    )(q, k, v, do)


def _gdn_consts():
    c = DN_CHUNK
    r = lax.broadcasted_iota(jnp.int32, (c, c), 0)
    cc = lax.broadcasted_iota(jnp.int32, (c, c), 1)
    incl = r >= cc
    strict = r > cc
    return incl, strict


def _gdn_chunk_common(q_ref, k_ref, v_ref, g_ref, b_ref, sl):
    c = DN_CHUNK
    incl, strict = _gdn_consts()
    q = q_ref[:, sl] * (1.0 / math.sqrt(DN_DIM))
    k = k_ref[:, sl]
    v = v_ref[:, sl]
    g = g_ref[:, sl]
    beta = b_ref[:, sl]
    gc = _nn(incl.astype(F32), g, hp=True)
    gam = jnp.exp(gc)
    g_last = _rowsum(g)
    grow = _nt(jnp.full((c, LANE), 1.0 / LANE, F32), gc, hp=True)
    dm = jnp.exp(jnp.where(incl, gc[:, :c] - grow, NEG))
    kb = k * beta
    vb = v * beta
    kbg = kb * gam
    kk = _nt(kb, k)
    ek = jnp.exp(g_last - gc)
    kd = k * ek
    qk = _nt(q, k)
    return dict(q=q, k=k, v=v, beta=beta, gc=gc, gam=gam, g_last=g_last, dm=dm, kb=kb, vb=vb,
                kbg=kbg, kk=kk, ek=ek, kd=kd, qk=qk, incl=incl, strict=strict)


def _gdn_fwd(q, k, v, g, beta):
    tp = q.shape[0]
    c = DN_CHUNK
    nch = tp // c

    def body(q_ref, k_ref, v_ref, g_ref, b_ref, o_ref, s_ref, t_ref, s_scr):
        @pl.when(pl.program_id(0) == 0)
        def _():
            s_scr[...] = jnp.zeros_like(s_scr)

        eye = (lax.broadcasted_iota(jnp.int32, (c, c), 0) == lax.broadcasted_iota(jnp.int32, (c, c), 1)).astype(F32)
        for h in range(DN_HEADS):
            sl = slice(DN_DIM * h, DN_DIM * (h + 1))
            x = _gdn_chunk_common(q_ref, k_ref, v_ref, g_ref, b_ref, sl)
            a = jnp.where(x["strict"], x["kk"] * x["dm"], 0.0)
            bp = -a
            t = eye + bp
            for _ in range(5):
                bp = _nn(bp, bp, hp=True)
                t = t + _nn(t, bp, hp=True)
            u = _nn(t, x["vb"])
            w = _nn(t, x["kbg"])
            s = s_scr[h]
            s_ref[h, 0] = s
            t_ref[h, 0] = t
            v_new = u - _nn(w, s)
            o_ref[:, sl] = _nn(x["q"] * x["gam"], s) + _nn(x["qk"] * x["dm"], v_new)
            s_scr[h] = s * jnp.exp(x["g_last"]) + _tn(x["kd"], v_new)

    rb = lambda n: (n, 0)
    return pl.pallas_call(
        body, name="gdn_fwd", grid=(nch,),
        in_specs=[pl.BlockSpec((c, DN_WIDTH), rb)] * 5,
        out_specs=[pl.BlockSpec((c, DN_WIDTH), rb),
                   pl.BlockSpec((DN_HEADS, 1, DN_DIM, DN_DIM), lambda n: (0, n, 0, 0)),
                   pl.BlockSpec((DN_HEADS, 1, c, c), lambda n: (0, n, 0, 0))],
        out_shape=[jax.ShapeDtypeStruct((tp, DN_WIDTH), F32),
                   jax.ShapeDtypeStruct((DN_HEADS, nch, DN_DIM, DN_DIM), F32),
                   jax.ShapeDtypeStruct((DN_HEADS, nch, c, c), F32)],
        scratch_shapes=[pltpu.VMEM((DN_HEADS, DN_DIM, DN_DIM), F32)],
        compiler_params=pltpu.CompilerParams(dimension_semantics=("arbitrary",), vmem_limit_bytes=VMEM_LIMIT),
    )(q, k, v, g, beta)


def _gdn_bwd(q, k, v, g, beta, s_all, t_all, do):
    tp = q.shape[0]
    c = DN_CHUNK
    nch = tp // c

    def body(q_ref, k_ref, v_ref, g_ref, b_ref, s_ref, t_ref, do_ref,
             dq_ref, dk_ref, dv_ref, dg_ref, db_ref, ds_scr):
        @pl.when(pl.program_id(0) == 0)
        def _():
            ds_scr[...] = jnp.zeros_like(ds_scr)

        ones_cl = jnp.ones((c, LANE), F32)
        for h in range(DN_HEADS):
            sl = slice(DN_DIM * h, DN_DIM * (h + 1))
            x = _gdn_chunk_common(q_ref, k_ref, v_ref, g_ref, b_ref, sl)
            qs, kx, vx, beta_, gam, dm = x["q"], x["k"], x["v"], x["beta"], x["gam"], x["dm"]
            kb, vb, kbg, kd, ek = x["kb"], x["vb"], x["kbg"], x["kd"], x["ek"]
            t = t_ref[h, 0]
            s = s_ref[h, 0]
            dsn = ds_scr[h]
            dob = do_ref[:, sl]
            eg_last = jnp.exp(x["g_last"])
            u = _nn(t, vb)
            w = _nn(t, kbg)
            v_new = u - _nn(w, s)
            mqk = x["qk"] * dm
            qd = qs * gam
            dv_new = _tn(mqk, dob) + _nn(kd, dsn)
            ds_scr[h] = _tn(qd, dob) + eg_last * dsn - _tn(w, dv_new)
            dqd = _nt(dob, s)
            dq = dqd * gam
            dgam = jnp.sum(dqd * qs, axis=1, keepdims=True)
            dmm = jnp.where(x["incl"], _nt(dob, v_new), 0.0)
            dqk = dmm * dm
            e_mat = dmm * mqk
            dq = dq + _nn(dqk, kx)
            dk = _tn(dqk, qs)
            dkd = _nt(v_new, dsn)
            dk = dk + dkd * ek
            e1 = jnp.sum(dkd * kd, axis=1, keepdims=True)
            dgc = -e1
            dg_last = jnp.sum(e1) + eg_last * jnp.sum(s * dsn)
            dw = -_nt(dv_new, s)
            dvb = _tn(t, dv_new)
            dt = _nt(dv_new, vb) + _nt(dw, kbg)
            dkbg = _tn(t, dw)
            da = jnp.where(x["strict"], -_nt(_tn(t, dt, hp=True), t, hp=True), 0.0)
            dkk = da * dm
            e_mat = e_mat + da * x["kk"] * dm
            dkb = _nn(dkk, kx) + dkbg * gam
            dk = dk + _tn(dkk, kb)
            dgam = dgam + jnp.sum(dkbg * kb, axis=1, keepdims=True)
            dk = dk + dkb * beta_
            dbeta = jnp.sum(dkb * kx, axis=1, keepdims=True) + jnp.sum(dvb * vx, axis=1, keepdims=True)
            dv = dvb * beta_
            dgc = dgc + jnp.sum(e_mat, axis=1, keepdims=True) + dgam * gam
            dgc = dgc - _tn(e_mat, ones_cl, hp=True)
            upper = jnp.logical_not(x["strict"]).astype(F32)
            dg = _nn(upper, dgc, hp=True) + dg_last
            dq_ref[:, sl] = dq * (1.0 / math.sqrt(DN_DIM))
            dk_ref[:, sl] = dk
            dv_ref[:, sl] = dv
            dg_ref[:, sl] = dg
            db_ref[:, sl] = jnp.broadcast_to(dbeta, (c, LANE))

    rb = lambda n: (nch - 1 - n, 0)
    hs = lambda n: (0, nch - 1 - n, 0, 0)
    return pl.pallas_call(
        body, name="gdn_bwd", grid=(nch,),
        in_specs=[pl.BlockSpec((c, DN_WIDTH), rb)] * 5
        + [pl.BlockSpec((DN_HEADS, 1, DN_DIM, DN_DIM), hs), pl.BlockSpec((DN_HEADS, 1, c, c), hs),
           pl.BlockSpec((c, DN_WIDTH), rb)],
        out_specs=[pl.BlockSpec((c, DN_WIDTH), rb)] * 5,
        out_shape=[jax.ShapeDtypeStruct((tp, DN_WIDTH), F32)] * 5,
        scratch_shapes=[pltpu.VMEM((DN_HEADS, DN_DIM, DN_DIM), F32)],
        compiler_params=pltpu.CompilerParams(dimension_semantics=("arbitrary",), vmem_limit_bytes=VMEM_LIMIT),
    )(q, k, v, g, beta, s_all, t_all, do)


def _silu_parts(x):
    s = _sigmoid(x)
    return x * s, s * (1.0 + x * (1.0 - s))


def _f_rms_cast(i, x, w):
    y, _ = _rms_fwd(x, w, x.shape[1])
    return (y,), ()


def _f_rms_bwd_add(i, x, dy, dres, w, *, mask_pad):
    dx, dwr = _rms_bwd(x, w, dy, x.shape[1])
    out = dres + dx
    if mask_pad:
        out = jnp.where(_row_ids(i, x.shape[0]) >= PAD, out, 0.0)
    return (out,), (_rowsum(dwr),)


def _f_lat_norm(i, ql, kvl, qw, kvw):
    return (_rms_fwd(ql, qw, Q_LORA)[0], _rms_fwd(kvl, kvw, KV_LORA)[0]), ()


def _f_lat_norm_bwd(i, ql, kvl, dqn, dkvn, qw, kvw):
    dq, dqw = _rms_bwd(ql, qw, dqn, Q_LORA)
    dk, dkw = _rms_bwd(kvl, kvw, dkvn, KV_LORA)
    return (dq, dk), (_rowsum(dqw), _rowsum(dkw))


def _rope(x, cos, sin_s):
    return x * cos + _swap_halves(x) * sin_s


def _rope_t(dy, cos, sin_s):
    return dy * cos + _swap_halves(dy * sin_s)


def _f_mla_qk(i, qf, kvf, kpe, cos, sin_s, qw, kw):
    qs, ks, vs = [], [], []
    for h in range(MLA_HEADS):
        qn, _ = _rms_fwd(qf[:, HP * h:HP * (h + 1)], qw, QK_HEAD)
        qs += [qn[:, :QK_NOPE], _rope(qn[:, QK_NOPE:], cos, sin_s)]
        kh = jnp.concatenate([kvf[:, HP * h:HP * h + QK_NOPE], kpe], axis=1)
        kn, _ = _rms_fwd(kh, kw, QK_HEAD)
        ks += [kn[:, :QK_NOPE], _rope(kn[:, QK_NOPE:], cos, sin_s)]
        vs.append(kvf[:, HP * h + QK_NOPE:HP * (h + 1)])
    return (jnp.concatenate(qs, axis=1), jnp.concatenate(ks, axis=1), jnp.concatenate(vs, axis=1)), ()


def _f_mla_qk_bwd(i, qf, kvf, kpe, cos, sin_s, dq, dk, dv, qw, kw):
    dqf, dkvf = [], []
    dkpe = None
    dqw = None
    dkw = None
    for h in range(MLA_HEADS):
        dqh = dq[:, HP * h:HP * (h + 1)]
        dqn = jnp.concatenate([dqh[:, :QK_NOPE], _rope_t(dqh[:, QK_NOPE:], cos, sin_s)], axis=1)
        dx, dwr = _rms_bwd(qf[:, HP * h:HP * (h + 1)], qw, dqn, QK_HEAD)
        dqf.append(dx)
        dqw = _rowsum(dwr) if dqw is None else dqw + _rowsum(dwr)
        dkh = dk[:, HP * h:HP * (h + 1)]
        dkn = jnp.concatenate([dkh[:, :QK_NOPE], _rope_t(dkh[:, QK_NOPE:], cos, sin_s)], axis=1)
        kh = jnp.concatenate([kvf[:, HP * h:HP * h + QK_NOPE], kpe], axis=1)
        dx, dwr = _rms_bwd(kh, kw, dkn, QK_HEAD)
        dkvf += [dx[:, :QK_NOPE], dv[:, V_HEAD * h:V_HEAD * (h + 1)]]
        dkpe = dx[:, QK_NOPE:] if dkpe is None else dkpe + dx[:, QK_NOPE:]
        dkw = _rowsum(dwr) if dkw is None else dkw + _rowsum(dwr)
    return (jnp.concatenate(dqf, axis=1), jnp.concatenate(dkvf, axis=1), dkpe), (dqw, dkw)


def _gdn_act(i, x, halo, w8):
    tm = x.shape[0]
    halo = jnp.where(i > 0, halo, 0.0)
    c = _conv_fwd(x, halo, w8, DN_CONV)
    act, dact = _silu_parts(c)
    return act, dact


def _f_gdn_prep(i, x, halo, ab, w8, alog, dtb, sel):
    tm = x.shape[0]
    act, _ = _gdn_act(i, x, halo, w8)
    outs = []
    for part in range(2):
        for h in range(DN_HEADS):
            t = act[:, DN_WIDTH * part + DN_DIM * h:DN_WIDTH * part + DN_DIM * (h + 1)]
            outs.append(t * lax.rsqrt(jnp.sum(t * t, axis=-1, keepdims=True) + EPS))
    q = jnp.concatenate(outs[:DN_HEADS], axis=1)
    k = jnp.concatenate(outs[DN_HEADS:], axis=1)
    v = act[:, 2 * DN_WIDTH:]
    abb = _nn(ab, sel, hp=True)
    valid = _row_ids(i, tm) >= PAD
    g = jnp.where(valid, -jnp.exp(alog) * _softplus(abb[:, :DN_WIDTH] + dtb), 0.0)
    beta = jnp.where(valid, _sigmoid(abb[:, DN_WIDTH:]), 0.0)
    return (q, k, v, g, beta), ()


def _f_gdn_prep_bwd(i, x, halo, ab, dq, dk, dv, dg, dbeta, w8, alog, dtb, sel, selpick):
    tm = x.shape[0]
    act, dact = _gdn_act(i, x, halo, w8)
    douts = []
    for part, dd in enumerate((dq, dk)):
        for h in range(DN_HEADS):
            t = act[:, DN_WIDTH * part + DN_DIM * h:DN_WIDTH * part + DN_DIM * (h + 1)]
            r = lax.rsqrt(jnp.sum(t * t, axis=-1, keepdims=True) + EPS)
            y = t * r
            dy = dd[:, DN_DIM * h:DN_DIM * (h + 1)]
            douts.append(r * (dy - y * jnp.sum(dy * y, axis=-1, keepdims=True)))
    douts.append(dv)
    dc = jnp.concatenate(douts, axis=1) * dact
    abb = _nn(ab, sel, hp=True)
    valid = _row_ids(i, tm) >= PAD
    pre = abb[:, :DN_WIDTH] + dtb
    ea = jnp.exp(alog)
    g = -ea * _softplus(pre)
    dg = jnp.where(valid, dg, 0.0)
    dbeta = jnp.where(valid, dbeta, 0.0)
    da = dg * (-ea) * _sigmoid(pre)
    beta = _sigmoid(abb[:, DN_WIDTH:])
    db = dbeta * beta * (1.0 - beta)
    dab = _nn(jnp.concatenate([da, db], axis=1), selpick, hp=True)
    return (dc, dab), (_rowsum(dg * g), _rowsum(da))


def _f_conv_bwd(i, dy, dy_next, x, x_prev, w8, *, width, nt):
    dy_next = jnp.where(i < nt - 1, dy_next, 0.0)
    x_prev = jnp.where(i > 0, x_prev, 0.0)
    return (_conv_bwd_x(dy, dy_next, w8, width),), (_conv_bwd_w(dy, x, x_prev, width),)


def _f_mix(i, o_mla, o_dn, z, w_mla, w_dn):
    tm = o_mla.shape[0]
    valid = _row_ids(i, tm) >= PAD
    outs = []
    for h in range(MLA_HEADS):
        y, _ = _rms_fwd(o_mla[:, V_HEAD * h:V_HEAD * (h + 1)], w_mla, V_HEAD)
        outs.append(jnp.where(valid, y, 0.0))
    for h in range(DN_HEADS):
        y, _ = _rms_fwd(o_dn[:, DN_DIM * h:DN_DIM * (h + 1)], w_dn, DN_DIM)
        outs.append(y * _silu_parts(z[:, DN_DIM * h:DN_DIM * (h + 1)])[0])
    return (jnp.concatenate(outs, axis=1),), ()


def _f_mix_bwd(i, o_mla, o_dn, z, dy_mla, dy_dn, w_mla, w_dn):
    tm = o_mla.shape[0]
    valid = _row_ids(i, tm) >= PAD
    d_mla, d_dn, d_z = [], [], []
    dw_mla = None
    dw_dn = None
    for h in range(MLA_HEADS):
        sl = slice(V_HEAD * h, V_HEAD * (h + 1))
        dx, dwr = _rms_bwd(o_mla[:, sl], w_mla, jnp.where(valid, dy_mla[:, sl], 0.0), V_HEAD)
        d_mla.append(dx)
        dw_mla = _rowsum(dwr) if dw_mla is None else dw_mla + _rowsum(dwr)
    for h in range(DN_HEADS):
        sl = slice(DN_DIM * h, DN_DIM * (h + 1))
        y, _ = _rms_fwd(o_dn[:, sl], w_dn, DN_DIM)
        sz, dsz = _silu_parts(z[:, sl])
        d_z.append(dy_dn[:, sl] * y * dsz)
        dx, dwr = _rms_bwd(o_dn[:, sl], w_dn, dy_dn[:, sl] * sz, DN_DIM)
        d_dn.append(dx)
        dw_dn = _rowsum(dwr) if dw_dn is None else dw_dn + _rowsum(dwr)
    return ((jnp.concatenate(d_mla, axis=1), jnp.concatenate(d_dn, axis=1), jnp.concatenate(d_z, axis=1)),
            (dw_mla, dw_dn))


def _f_ffn_act(i, gate_pre, halo, up, w8, b):
    halo = jnp.where(i > 0, halo, 0.0)
    gate = _conv_fwd(gate_pre, halo, w8, FFN_CONV) + b
    return (_silu_parts(gate)[0] * up,), ()


def _f_ffn_act_bwd(i, gate_pre, halo, up, dact, w8, b):
    halo = jnp.where(i > 0, halo, 0.0)
    gate = _conv_fwd(gate_pre, halo, w8, FFN_CONV) + b
    sg, dsg = _silu_parts(gate)
    dgate = dact * up * dsg
    return (dgate, dact * sg), (_rowsum(dgate),)


def _f_loss(i, h3, tgt):
    tm = h3.shape[0]
    diff = jnp.where(_row_ids(i, tm) >= ROW0, h3 - tgt, 0.0)
    part = 0.5 * jnp.sum(diff * diff) * (1.0 / D_MODEL)
    return (diff * (1.0 / D_MODEL),), (jnp.full((1, LANE), part, F32),)


def _local_step(h0, tgt, w):
    tp = h0.shape[0]
    nt = tp // TM
    bf = (D_MODEL, _MXU)
    u, = _rows("rms_in", _f_rms_cast, [_In(h0)], [w["attn_norm_w"]], [bf])
    proj = _mm("in_proj", u, w["w_in"], "nn")
    p_qkv = lambda kind="cur": _In(proj, 3 * DN_WIDTH, 0, kind)
    p_z = _In(proj, DN_WIDTH, C_Z // DN_WIDTH)
    p_ql = _In(proj, Q_LORA, C_QL // Q_LORA)
    p_kvl = _In(proj, KV_LORA, C_KVL // KV_LORA)
    p_kpe = _In(proj, LANE, C_KPE // LANE)
    p_ab = _In(proj, LANE, C_AB // LANE)
    cos, sin_s = _In(w["cos"]), _In(w["sin_s"])

    qn, kvn = _rows("mla_lat_norm", _f_lat_norm, [p_ql, p_kvl], [w["q_a_norm_w"], w["kv_a_norm_w"]],
                    [(Q_LORA, _MXU), (KV_LORA, _MXU)])
    qf = _mm("mla_q_b", qn, w["w_q_b"], "nn")
    kvf = _mm("mla_kv_b", kvn, w["w_kv_b"], "nn")
    qk_w = [w["q_norm_w"], w["k_norm_w"]]
    q, k, v = _rows("mla_qk", _f_mla_qk, [_In(qf), _In(kvf), p_kpe, cos, sin_s], qk_w,
                    [(MLA_HEADS * HP, _MXU), (MLA_HEADS * HP, _MXU), (MLA_HEADS * V_HEAD, _MXU)])
    o_mla = _attn_fwd(q, k, v)

    dn_w = [w["dn_conv_w"], w["alog_b"], w["dtb_b"], w["sel"]]
    gq, gk, gv, gg, gb = _rows("gdn_prep", _f_gdn_prep, [p_qkv(), p_qkv("prev"), p_ab], dn_w,
                               [(DN_WIDTH, F32)] * 5)
    o_dn, s_all, t_all = _gdn_fwd(gq, gk, gv, gg, gb)

    out_w = [w["mla_out_norm_w"], w["dn_out_norm_w"]]
    mixed, = _rows("mix", _f_mix, [_In(o_mla), _In(o_dn), p_z], out_w, [bf])
    h2 = _mm("out_proj", mixed, w["w_out"], "nn", resid=h0)

    hn, = _rows("rms_ffn", _f_rms_cast, [_In(h2)], [w["ffn_norm_w"]], [bf])
    gate_pre = _mm("ffn_gate", hn, w["w_gate"], "nn")
    up = _mm("ffn_up", hn, w["w_up"], "nn")
    ffn_w = [w["ffn_conv_w"], w["ffn_conv_b"]]
    act, = _rows("ffn_act", _f_ffn_act, [_In(gate_pre), _In(gate_pre, kind="prev"), _In(up)], ffn_w,
                 [(D_FF, _MXU)])
    h3 = _mm("ffn_down", act, w["w_down"], "nn", resid=h2)

    dh3, loss = _rows("loss", _f_loss, [_In(h3), _In(tgt)], [], [(D_MODEL, F32)], [(1, LANE)])

    g = {}
    dact = _mm("ffn_down_dx", dh3, w["w_down"], "nt")
    g["w_down"] = _mm("ffn_down_dw", act, dh3, "tn")
    dgate, dup, g["ffn_conv_b"] = _rows(
        "ffn_act_bwd", _f_ffn_act_bwd, [_In(gate_pre), _In(gate_pre, kind="prev"), _In(up), _In(dact)], ffn_w,
        [(D_FF, F32), (D_FF, F32)], [(1, D_FF)])
    dgate_pre, g["ffn_conv_w"] = _rows(
        "ffn_conv_bwd", functools.partial(_f_conv_bwd, width=FFN_CONV, nt=nt),
        [_In(dgate), _In(dgate, kind="next"), _In(gate_pre), _In(gate_pre, kind="prev")], [w["ffn_conv_w"]],
        [(D_FF, F32)], [(8, D_FF)])
    dhn = _mm("ffn_gate_dx", dgate_pre, w["w_gate"], "nt")
    dhn = _mm("ffn_up_dx", dup, w["w_up"], "nt", resid=dhn)
    g["w_gate"] = _mm("ffn_gate_dw", hn, dgate_pre, "tn")
    g["w_up"] = _mm("ffn_up_dw", hn, dup, "tn")
    dh2, g["ffn_norm_w"] = _rows(
        "rms_ffn_bwd", functools.partial(_f_rms_bwd_add, mask_pad=True), [_In(h2), _In(dhn), _In(dh3)],
        [w["ffn_norm_w"]], [(D_MODEL, F32)], [(1, D_MODEL)])

    dmixed = _mm("out_proj_dx", dh2, w["w_out"], "nt")
    g["w_out"] = _mm("out_proj_dw", mixed, dh2, "tn")
    half = MLA_HEADS * V_HEAD
    do_mla, do_dn, dz, g["mla_out_norm_w"], g["dn_out_norm_w"] = _rows(
        "mix_bwd", _f_mix_bwd, [_In(o_mla), _In(o_dn), p_z, _In(dmixed, half, 0), _In(dmixed, half, 1)], out_w,
        [(half, F32), (DN_WIDTH, F32), (DN_WIDTH, F32)], [(1, V_HEAD), (1, DN_DIM)])

    dq, dk, dv = _attn_bwd(q, k, v, do_mla)
    dqf, dkvf, dkpe, g["q_norm_w"], g["k_norm_w"] = _rows(
        "mla_qk_bwd", _f_mla_qk_bwd, [_In(qf), _In(kvf), p_kpe, cos, sin_s, _In(dq), _In(dk), _In(dv)], qk_w,
        [(MLA_HEADS * HP, F32), (MLA_HEADS * HP, F32), (LANE, F32)], [(1, HP), (1, HP)])
    dqn = _mm("mla_q_b_dx", dqf, w["w_q_b"], "nt")
    g["w_q_b"] = _mm("mla_q_b_dw", qn, dqf, "tn")
    dkvn = _mm("mla_kv_b_dx", dkvf, w["w_kv_b"], "nt")
    g["w_kv_b"] = _mm("mla_kv_b_dw", kvn, dkvf, "tn")
    dql, dkvl, g["q_a_norm_w"], g["kv_a_norm_w"] = _rows(
        "mla_lat_norm_bwd", _f_lat_norm_bwd, [p_ql, p_kvl, _In(dqn), _In(dkvn)],
        [w["q_a_norm_w"], w["kv_a_norm_w"]], [(Q_LORA, F32), (KV_LORA, F32)], [(1, Q_LORA), (1, KV_LORA)])

    dgq, dgk, dgv, dgg, dgb = _gdn_bwd(gq, gk, gv, gg, gb, s_all, t_all, do_dn)
    dc, dab, g["alog_b"], g["dtb_b"] = _rows(
        "gdn_prep_bwd", _f_gdn_prep_bwd,
        [p_qkv(), p_qkv("prev"), p_ab, _In(dgq), _In(dgk), _In(dgv), _In(dgg), _In(dgb)], dn_w + [w["selpick"]],
        [(3 * DN_WIDTH, F32), (LANE, F32)], [(1, DN_WIDTH), (1, DN_WIDTH)])
    dqkv, g["dn_conv_w"] = _rows(
        "gdn_conv_bwd", functools.partial(_f_conv_bwd, width=DN_CONV, nt=nt),
        [_In(dc), _In(dc, kind="next"), p_qkv(), p_qkv("prev")], [w["dn_conv_w"]],
        [(3 * DN_WIDTH, F32)], [(8, 3 * DN_WIDTH)])

    dproj = jnp.concatenate([dqkv, dz, dql, dkvl, dkpe, dab], axis=1)
    du = _mm("in_proj_dx", dproj, w["w_in"], "nt")
    g["w_in"] = _mm("in_proj_dw", u, dproj, "tn")
    dh0, g["attn_norm_w"] = _rows(
        "rms_in_bwd", functools.partial(_f_rms_bwd_add, mask_pad=False), [_In(h0), _In(du), _In(dh2)],
        [w["attn_norm_w"]], [(D_MODEL, F32)], [(1, D_MODEL)])
    return loss, dh0, g


def _w_in_to_padded(w):
    c1, c2, c3 = Q_LORA, Q_LORA + KV_LORA, Q_LORA + KV_LORA + QK_ROPE
    c4 = c3 + 3 * DN_WIDTH
    c5 = c4 + DN_WIDTH
    z = lambda n: jnp.zeros((w.shape[0], n), w.dtype)
    return jnp.concatenate([w[:, c3:c4], w[:, c4:c5], w[:, :c1], w[:, c1:c2], w[:, c2:c3], z(LANE - QK_ROPE),
                            w[:, c5:], z(LANE - 2 * DN_HEADS)], axis=1)


def _w_in_from_padded(g):
    return jnp.concatenate([g[:, C_QL:C_QL + Q_LORA], g[:, C_KVL:C_KVL + KV_LORA], g[:, C_KPE:C_KPE + QK_ROPE],
                            g[:, :C_Z + DN_WIDTH], g[:, C_AB:C_AB + 2 * DN_HEADS]], axis=1)


def _w_q_b_to_padded(w):
    r = w.shape[0]
    w = w.reshape(r, MLA_HEADS, QK_HEAD)
    return jnp.pad(w, ((0, 0), (0, 0), (0, HP - QK_HEAD))).reshape(r, MLA_HEADS * HP)


def _w_q_b_from_padded(g):
    r = g.shape[0]
    return g.reshape(r, MLA_HEADS, HP)[:, :, :QK_HEAD].reshape(r, MLA_HEADS * QK_HEAD)


def _pad_rows8(w):
    return jnp.pad(w, ((0, 8 - w.shape[0]), (0, 0)))


def _prepare(full, tp):
    w = {}
    mx = lambda a: a.astype(_MXU)
    w["attn_norm_w"] = full["attn_norm_w"]
    w["w_in"] = mx(_w_in_to_padded(full["w_in"]))
    w["q_a_norm_w"] = full["q_a_norm_w"]
    w["kv_a_norm_w"] = full["kv_a_norm_w"]
    w["w_q_b"] = mx(_w_q_b_to_padded(full["w_q_b"]))
    w["w_kv_b"] = mx(full["w_kv_b"])
    w["q_norm_w"] = jnp.pad(full["q_norm_w"], ((0, 0), (0, HP - QK_HEAD)))
    w["k_norm_w"] = jnp.pad(full["k_norm_w"], ((0, 0), (0, HP - QK_HEAD)))
    w["mla_out_norm_w"] = full["mla_out_norm_w"]
    w["dn_out_norm_w"] = full["dn_out_norm_w"]
    w["dn_conv_w"] = _pad_rows8(full["dn_conv_w"])
    w["alog_b"] = jnp.repeat(full["dn_A_log"], DN_DIM, axis=1)
    w["dtb_b"] = jnp.repeat(full["dn_dt_bias"], DN_DIM, axis=1)
    w["w_out"] = mx(full["w_out"])
    w["ffn_norm_w"] = full["ffn_norm_w"]
    w["w_gate"] = mx(full["w_gate"])
    w["w_up"] = mx(full["w_up"])
    w["ffn_conv_w"] = _pad_rows8(full["ffn_conv_w"])
    w["ffn_conv_b"] = full["ffn_conv_b"]
    w["w_down"] = mx(full["w_down"])
    half = QK_ROPE // 2
    inv = ROPE_THETA ** (-jnp.arange(half, dtype=F32) / half)
    ang = (jnp.arange(tp, dtype=jnp.int32) - PAD).astype(F32)[:, None] * inv[None, :]
    zc = jnp.zeros((tp, LANE - QK_ROPE), F32)
    w["cos"] = jnp.concatenate([jnp.cos(ang), jnp.cos(ang), zc], axis=1)
    w["sin_s"] = jnp.concatenate([-jnp.sin(ang), jnp.sin(ang), zc], axis=1)
    lane = jnp.arange(2 * DN_WIDTH)[None, :]
    src = jnp.arange(LANE)[:, None]
    w["sel"] = ((lane // DN_DIM) == src).astype(F32)
    w["selpick"] = ((src.T == (lane.T // DN_DIM)) & (lane.T % DN_DIM == 0)).astype(F32)
    return w


def _grads_to_natural(g):
    n = dict(g)
    n["w_in"] = _w_in_from_padded(g["w_in"])
    n["w_q_b"] = _w_q_b_from_padded(g["w_q_b"])
    n["q_norm_w"] = g["q_norm_w"][:, :QK_HEAD]
    n["k_norm_w"] = g["k_norm_w"][:, :QK_HEAD]
    n["dn_conv_w"] = g["dn_conv_w"][:DN_CONV]
    n["ffn_conv_w"] = g["ffn_conv_w"][:FFN_CONV]
    n["dn_A_log"] = n.pop("alog_b")[:, ::DN_DIM]
    n["dn_dt_bias"] = n.pop("dtb_b")[:, ::DN_DIM]
    return n


_MESH = pl.DeviceIdType.MESH
_ANY = pl.BlockSpec(memory_space=pl.ANY)
_CHIP_FLIPS = ((1, 0), (0, 1), (1, 1))


def _me():
    return lax.axis_index("x"), lax.axis_index("y"), lax.axis_index("c")


def _all_gather(name, blk):
    def body(x_ref, out_ref, send_sems, recv_sems, local_sem):
        x, y, c = _me()
        me, sib = (x, y, c), (x, y, 1 - c)
        chips = [(x ^ fx, y ^ fy) for fx, fy in _CHIP_FLIPS]

        def slot(p):
            return out_ref.at[4 * p[0] + 2 * p[1] + p[2]]

        def copy(k, block, to, src=None):
            return pltpu.make_async_remote_copy(
                src_ref=slot(block) if src is None else src, dst_ref=slot(block),
                send_sem=send_sems.at[k], recv_sem=recv_sems.at[k], device_id=to, device_id_type=_MESH)

        mine = pltpu.make_async_copy(x_ref, slot(me), local_sem)
        mine.start()
        first = [copy(0, me, sib, src=x_ref)]
        first += [copy(1 + j, me, (*chip, c), src=x_ref) for j, chip in enumerate(chips)]
        for cp in first:
            cp.start()
        passed = [copy(4 + j, (*chip, c), sib) for j, chip in enumerate(chips)]
        for j, chip in enumerate(chips):
            copy(1 + j, (*chip, c), me).wait_recv()
            passed[j].start()
        copy(0, sib, me).wait_recv()
        for j, chip in enumerate(chips):
            copy(4 + j, (*chip, 1 - c), me).wait_recv()
        for cp in first + passed:
            cp.wait_send()
        mine.wait()

    return pl.pallas_call(
        body, name=name, in_specs=[_ANY], out_specs=_ANY,
        out_shape=jax.ShapeDtypeStruct((N_DEV,) + blk.shape, blk.dtype),
        scratch_shapes=[pltpu.SemaphoreType.DMA((7,)), pltpu.SemaphoreType.DMA((7,)), pltpu.SemaphoreType.DMA],
    )(blk)


def _rs_sibling(name, gb):
    def body(g_ref, out_ref, send_sems, recv_sems):
        x, y, c = _me()
        cps = []
        for j in range(4):
            cp = pltpu.make_async_remote_copy(
                src_ref=g_ref.at[2 * j + (1 - c)], dst_ref=out_ref.at[j], send_sem=send_sems.at[j],
                recv_sem=recv_sems.at[j], device_id=(x, y, 1 - c), device_id_type=_MESH)
            cp.start()
            cps.append(cp)
        for cp in cps:
            cp.wait()

    return pl.pallas_call(
        body, name=name, in_specs=[_ANY], out_specs=_ANY,
        out_shape=jax.ShapeDtypeStruct((4,) + gb.shape[1:], gb.dtype),
        scratch_shapes=[pltpu.SemaphoreType.DMA((4,)), pltpu.SemaphoreType.DMA((4,))],
    )(gb)


def _rs_chips(name, s1):
    def body(s_ref, out_ref, send_sems, recv_sems):
        x, y, c = _me()
        cps = []
        for k, (fx, fy) in enumerate(_CHIP_FLIPS):
            px, py = x ^ fx, y ^ fy
            cp = pltpu.make_async_remote_copy(
                src_ref=s_ref.at[2 * px + py], dst_ref=out_ref.at[k], send_sem=send_sems.at[k],
                recv_sem=recv_sems.at[k], device_id=(px, py, c), device_id_type=_MESH)
            cp.start()
            cps.append(cp)
        for cp in cps:
            cp.wait()

    return pl.pallas_call(
        body, name=name, in_specs=[_ANY], out_specs=_ANY,
        out_shape=jax.ShapeDtypeStruct((3,) + s1.shape[1:], s1.dtype),
        scratch_shapes=[pltpu.SemaphoreType.DMA((3,)), pltpu.SemaphoreType.DMA((3,))],
    )(s1)


def _row_tile(r):
    return _pick(r, 512, 8)


def _pair_sum(name, gb, recv):
    _, r, cols = gb.shape
    tm = _row_tile(r)
    c = lax.axis_index("c").astype(jnp.int32).reshape(1)

    def body(c_ref, a_ref, b_ref, o_ref):
        o_ref[...] = a_ref[...] + b_ref[...]

    return pl.pallas_call(
        body, name=name,
        grid_spec=pltpu.PrefetchScalarGridSpec(
            num_scalar_prefetch=1, grid=(4, r // tm),
            in_specs=[pl.BlockSpec((1, tm, cols), lambda j, i, c_ref: (2 * j + c_ref[0], i, 0)),
                      pl.BlockSpec((1, tm, cols), lambda j, i, c_ref: (j, i, 0))],
            out_specs=pl.BlockSpec((1, tm, cols), lambda j, i, c_ref: (j, i, 0))),
        out_shape=jax.ShapeDtypeStruct((4, r, cols), F32),
        compiler_params=pltpu.CompilerParams(dimension_semantics=("parallel", "parallel")),
    )(c, gb, recv)


def _adam(name, parts, w, m, v):
    r, cols = w.shape
    tm = _row_tile(r)
    idx = jnp.stack([jnp.asarray(s, jnp.int32) for _, s in parts])
    n = len(parts)

    def body(idx_ref, *refs):
        g = refs[0][0]
        for p_ref in refs[1:n]:
            g = g + p_ref[0]
        w_ref, m_ref, v_ref, g_out, d_out, m_out, v_out = refs[n:]
        m_new = ADAM_B1 * m_ref[...] + (1.0 - ADAM_B1) * g
        v_new = ADAM_B2 * v_ref[...] + (1.0 - ADAM_B2) * (g * g)
        m_hat = m_new / (1.0 - ADAM_B1 ** ADAM_STEP)
        v_hat = v_new / (1.0 - ADAM_B2 ** ADAM_STEP)
        g_out[...] = g
        d_out[...] = -ADAM_LR * (m_hat / (jnp.sqrt(v_hat) + ADAM_EPS) + ADAM_WD * w_ref[...])
        m_out[...] = m_new
        v_out[...] = v_new

    part_specs = [pl.BlockSpec((1, tm, cols), lambda i, idx_ref, p=p: (idx_ref[p], i, 0)) for p in range(n)]
    flat = pl.BlockSpec((tm, cols), lambda i, idx_ref: (i, 0))
    return pl.pallas_call(
        body, name=name,
        grid_spec=pltpu.PrefetchScalarGridSpec(
            num_scalar_prefetch=1, grid=(r // tm,), in_specs=part_specs + [flat] * 3, out_specs=[flat] * 4),
        out_shape=[jax.ShapeDtypeStruct((r, cols), F32)] * 4,
        compiler_params=pltpu.CompilerParams(dimension_semantics=("parallel",)),
    )(idx, *[a for a, _ in parts], w, m, v)


_SHARDED = (
    ("meta_tokens", 1, (N_META, D_MODEL)),
    ("w_in", 1, (D_MODEL, IN_COLS)),
    ("w_q_b", 1, (Q_LORA, MLA_HEADS * QK_HEAD)),
    ("w_kv_b", 1, (KV_LORA, MLA_HEADS * (QK_NOPE + V_HEAD))),
    ("dn_conv_w", 1, (DN_CONV, 3 * DN_WIDTH)),
    ("w_out", 0, (2 * DN_WIDTH, D_MODEL)),
    ("w_gate", 1, (D_MODEL, D_FF)),
    ("w_up", 1, (D_MODEL, D_FF)),
    ("ffn_conv_w", 1, (FFN_CONV, D_FF)),
    ("w_down", 0, (D_FF, D_MODEL)),
)
_MXU_GATHERED = ("w_in", "w_q_b", "w_kv_b", "w_out", "w_gate", "w_up", "w_down")
_F32_GATHERED = ("meta_tokens", "dn_conv_w", "ffn_conv_w")
_REPLICATED = (
    ("attn_norm_w", D_MODEL), ("q_a_norm_w", Q_LORA), ("kv_a_norm_w", KV_LORA), ("q_norm_w", QK_HEAD),
    ("k_norm_w", QK_HEAD), ("mla_out_norm_w", V_HEAD), ("dn_A_log", DN_HEADS), ("dn_dt_bias", DN_HEADS),
    ("dn_out_norm_w", DN_DIM), ("ffn_norm_w", D_MODEL), ("ffn_conv_b", D_FF),
)
_PACK_COLS = 1024
_PACK_ROW_MULT = 320
_SMALL_SHAPE = (8, 768)


def _local_shape(dim, shape):
    return (shape[0] // N_DEV, shape[1]) if dim == 0 else (shape[0], shape[1] // N_DEV)


def _pack_rows(n, mult):
    rows = -(-n // _PACK_COLS)
    return -(-rows // mult) * mult


def _pack(flats, mult, axis=0):
    cat = jnp.concatenate(flats, axis=-1)
    n = cat.shape[-1]
    r = _pack_rows(n, mult)
    pad = [(0, 0)] * (cat.ndim - 1) + [(0, r * _PACK_COLS - n)]
    return jnp.pad(cat, pad).reshape(cat.shape[:-1] + (r, _PACK_COLS))


def _to_blocks(full, dim):
    r, c = full.shape
    if dim == 0:
        return full.reshape(N_DEV, (r // N_DEV) * c)
    return full.reshape(r, N_DEV, c // N_DEV).transpose(1, 0, 2).reshape(N_DEV, r * (c // N_DEV))


def _from_blocks(blocks, dim, shape):
    r, c = shape
    if dim == 0:
        return blocks.reshape(r, c)
    return blocks.reshape(N_DEV, r, c // N_DEV).transpose(1, 0, 2).reshape(r, c)


def _split(flat, sizes):
    out, o = [], 0
    for s in sizes:
        out.append(flat[..., o:o + s])
        o += s
    return out


def _gather_weights(local, names, dtype, mult):
    specs = [s for s in _SHARDED if s[0] in names]
    pack = _pack([local[n].astype(dtype).reshape(-1) for n, _, _ in specs], mult)
    got = _all_gather("gather_" + "_".join(n[:5] for n in names[:2]), pack)
    flat = got.reshape(N_DEV, -1)
    sizes = [math.prod(_local_shape(d, s)) for _, d, s in specs]
    return {n: _from_blocks(p, d, s) for (n, d, s), p in zip(specs, _split(flat, sizes))}


def kernel(x, meta_tokens, attn_norm_w, w_in, q_a_norm_w, w_q_b, kv_a_norm_w, w_kv_b, q_norm_w, k_norm_w, mla_out_norm_w, dn_conv_w, dn_A_log, dn_dt_bias, dn_out_norm_w, w_out, ffn_norm_w, w_gate, w_up, ffn_conv_w, ffn_conv_b, w_down, loss_target, m_meta_tokens, m_attn_norm_w, m_w_in, m_q_a_norm_w, m_w_q_b, m_kv_a_norm_w, m_w_kv_b, m_q_norm_w, m_k_norm_w, m_mla_out_norm_w, m_dn_conv_w, m_dn_A_log, m_dn_dt_bias, m_dn_out_norm_w, m_w_out, m_ffn_norm_w, m_w_gate, m_w_up, m_ffn_conv_w, m_ffn_conv_b, m_w_down, v_meta_tokens, v_attn_norm_w, v_w_in, v_q_a_norm_w, v_w_q_b, v_kv_a_norm_w, v_w_kv_b, v_q_norm_w, v_k_norm_w, v_mla_out_norm_w, v_dn_conv_w, v_dn_A_log, v_dn_dt_bias, v_dn_out_norm_w, v_w_out, v_ffn_norm_w, v_w_gate, v_w_up, v_ffn_conv_w, v_ffn_conv_b, v_w_down):
    names = [n for n, _, _ in _SHARDED] + [n for n, _ in _REPLICATED]
    given = dict(locals())
    two_d = lambda a: a.reshape(a.shape[-2:])
    wl = {n: two_d(given[n]) for n in names}
    ml = {n: two_d(given["m_" + n]) for n in names}
    vl = {n: two_d(given["v_" + n]) for n in names}
    out_shapes = {n: given[n].shape for n in names}

    full = dict(wl)
    full.update(_gather_weights(wl, _MXU_GATHERED, _MXU, 16))
    full.update(_gather_weights(wl, _F32_GATHERED, F32, 8))

    seq = x.shape[1]
    tp = ROW0 + seq
    h0 = jnp.concatenate([jnp.zeros((PAD, D_MODEL), F32), full["meta_tokens"], x[0]], axis=0)
    tgt = jnp.concatenate([jnp.zeros((ROW0, D_MODEL), F32), loss_target[0]], axis=0)
    loss, dh0, g = _local_step(h0, tgt, _prepare(full, tp))
    g = _grads_to_natural(g)
    g["meta_tokens"] = dh0[PAD:ROW0]
    grad_x = dh0[ROW0:][None]

    gb = _pack([_to_blocks(g[n], d) for n, d, _ in _SHARDED], _PACK_ROW_MULT)
    from_sib = _rs_sibling("rs_sibling", gb)
    s1 = _pair_sum("rs_pair_sum", gb, from_sib)
    from_chips = _rs_chips("rs_chips", s1)
    my_chip = 2 * lax.axis_index("x") + lax.axis_index("y")
    flat = lambda d: _pack([d[n].reshape(-1) for n, _, _ in _SHARDED], _PACK_ROW_MULT)
    big = _adam("adam_sharded", [(s1, my_chip), (from_chips, 0), (from_chips, 1), (from_chips, 2)],
                flat(wl), flat(ml), flat(vl))
    sizes = [math.prod(_local_shape(d, s)) for _, d, s in _SHARDED]
    big = [dict(zip([n for n, _, _ in _SHARDED], _split(a.reshape(-1), sizes))) for a in big]

    def small(d, extra):
        cat = jnp.concatenate([d[n].reshape(-1) for n, _ in _REPLICATED] + [extra])
        return jnp.pad(cat, (0, math.prod(_SMALL_SHAPE) - cat.shape[0])).reshape(_SMALL_SHAPE)

    zero1 = jnp.zeros((1,), F32)
    parts = _all_gather("gather_small_grads", small(g, loss[0, :1]))
    sm = _adam("adam_replicated", [(parts, d) for d in range(N_DEV)], small(wl, zero1), small(ml, zero1),
               small(vl, zero1))
    rsizes = [n for _, n in _REPLICATED] + [1]
    sm = [dict(zip([n for n, _ in _REPLICATED] + ["loss"], _split(a.reshape(-1), rsizes))) for a in sm]

    outs = [sm[0]["loss"].reshape(()), grad_x]
    for kind in range(4):
        for n in ("meta_tokens", "attn_norm_w", "w_in", "q_a_norm_w", "w_q_b", "kv_a_norm_w", "w_kv_b", "q_norm_w",
                  "k_norm_w", "mla_out_norm_w", "dn_conv_w", "dn_A_log", "dn_dt_bias", "dn_out_norm_w", "w_out",
                  "ffn_norm_w", "w_gate", "w_up", "ffn_conv_w", "ffn_conv_b", "w_down"):
            src = big[kind] if n in big[kind] else sm[kind]
            outs.append(src[n].reshape(out_shapes[n]))
    return tuple(outs)
```

```python
import functools
import math

import jax
import jax.numpy as jnp
from jax import lax
from jax.experimental import pallas as pl
from jax.experimental.pallas import tpu as pltpu

F32 = jnp.float32
BF16 = jnp.bfloat16
_MXU = jnp.bfloat16
_HI = lax.Precision.HIGHEST

D_MODEL = 1024
N_META = 16
PAD = 112
ROW0 = PAD + N_META
MLA_HEADS = 4
QK_NOPE = 128
QK_ROPE = 64
QK_HEAD = QK_NOPE + QK_ROPE
V_HEAD = 128
Q_LORA = 256
KV_LORA = 256
ROPE_THETA = 10000.0
DN_HEADS = 4
DN_DIM = 128
DN_WIDTH = DN_HEADS * DN_DIM
DN_CONV = 4
DN_CHUNK = 64
D_FF = 2816
FFN_CONV = 3
EPS = 1e-6
HP = 256
C_QKV = 0
C_Z = 1536
C_QL = 2048
C_KVL = 2304
C_KPE = 2560
C_AB = 2688
IN_P = 2816
IN_COLS = 2632

ADAM_LR = 0.001
ADAM_B1 = 0.9
ADAM_B2 = 0.999
ADAM_EPS = 1e-08
ADAM_WD = 0.01
ADAM_STEP = 10

N_DEV = 8
TM = 128
LANE = 128
VMEM_LIMIT = 56 * 1024 * 1024
NEG = -1e30


def _dot(a, b, dims, hp=False):
    if hp:
        return lax.dot_general(a.astype(F32), b.astype(F32), (dims, ((), ())),
                               precision=_HI, preferred_element_type=F32)
    return lax.dot_general(a.astype(_MXU), b.astype(_MXU), (dims, ((), ())),
                           preferred_element_type=F32)


def _nn(a, b, hp=False):
    return _dot(a, b, ((1,), (0,)), hp)


def _nt(a, b, hp=False):
    return _dot(a, b, ((1,), (1,)), hp)


def _tn(a, b, hp=False):
    return _dot(a, b, ((0,), (0,)), hp)


def _sigmoid(x):
    return 1.0 / (1.0 + jnp.exp(-x))


def _rms_fwd(x, w, n):
    r = lax.rsqrt(jnp.sum(x * x, axis=-1, keepdims=True) * (1.0 / n) + EPS)
    return x * r * w, r


def _rms_bwd(x, w, dy, n):
    r = lax.rsqrt(jnp.sum(x * x, axis=-1, keepdims=True) * (1.0 / n) + EPS)
    xh = x * r
    gy = dy * w
    dx = r * (gy - xh * (jnp.sum(gy * xh, axis=-1, keepdims=True) * (1.0 / n)))
    return dx, dy * xh


def _rowsum(x):
    return jnp.sum(x, axis=0, keepdims=True)


def _row_ids(i, tm):
    return i * tm + lax.broadcasted_iota(jnp.int32, (tm, 1), 0)


def _shift_down(ext, s, tm):
    if s == 0:
        return ext[8:8 + tm]
    return pltpu.roll(ext, s, 0)[8:8 + tm]


def _shift_up(ext, s, tm):
    if s == 0:
        return ext[0:tm]
    return pltpu.roll(ext, tm + 8 - s, 0)[0:tm]


def _conv_fwd(x, halo_prev, w, width):
    tm = x.shape[0]
    ext = jnp.concatenate([halo_prev, x], axis=0)
    y = None
    for j in range(width):
        t = w[j:j + 1, :] * _shift_down(ext, width - 1 - j, tm)
        y = t if y is None else y + t
    return y


def _conv_bwd_x(dy, halo_next, w, width):
    tm = dy.shape[0]
    ext = jnp.concatenate([dy, halo_next], axis=0)
    dx = None
    for j in range(width):
        t = w[j:j + 1, :] * _shift_up(ext, width - 1 - j, tm)
        dx = t if dx is None else dx + t
    return dx


def _conv_bwd_w(dy, x, halo_prev, width):
    tm = dy.shape[0]
    ext = jnp.concatenate([halo_prev, x], axis=0)
    rows = [_rowsum(dy * _shift_down(ext, width - 1 - j, tm)) for j in range(width)]
    rows += [jnp.zeros_like(rows[0])] * (8 - width)
    return jnp.concatenate(rows, axis=0)


def _softplus(x):
    e = jnp.exp(-jnp.abs(x))
    u = 1.0 + e
    l1p = jnp.where(u == 1.0, e, jnp.log(u) * e / jnp.where(u == 1.0, 1.0, u - 1.0))
    return jnp.maximum(x, 0.0) + l1p


def _swap_halves(x):
    lane = lax.broadcasted_iota(jnp.int32, x.shape, 1)
    return jnp.where(lane < 32, pltpu.roll(x, 96, 1), jnp.where(lane < 64, pltpu.roll(x, 32, 1), 0.0))


class _In:
    def __init__(self, arr, width=None, cb=0, kind="cur"):
        self.arr, self.kind = arr, kind
        self.width = arr.shape[1] if width is None else width
        self.cb = cb


def _rows(name, fn, tiled, full, outs, accs=(), tm=TM):
    tp = tiled[0].arr.shape[0]
    nt = tp // tm
    r8 = tm // 8
    n_in = len(tiled) + len(full)
    n_out = len(outs)

    def body(*refs):
        i = pl.program_id(0)
        vals = [r[...] for r in refs[:n_in]]
        o_t, o_a = fn(i, *vals)
        for r, v in zip(refs[n_in:n_in + n_out], o_t):
            r[...] = v.astype(r.dtype)
        for r, v in zip(refs[n_in + n_out:], o_a):
            @pl.when(i == 0)
            def _():
                r[...] = v

            @pl.when(i > 0)
            def _():
                r[...] += v

    def spec(t):
        if t.kind == "cur":
            return pl.BlockSpec((tm, t.width), lambda i, cb=t.cb: (i, cb))
        if t.kind == "prev":
            return pl.BlockSpec((8, t.width), lambda i, cb=t.cb: (jnp.maximum(i * r8 - 1, 0), cb))
        return pl.BlockSpec((8, t.width), lambda i, cb=t.cb: (jnp.minimum((i + 1) * r8, tp // 8 - 1), cb))

    in_specs = [spec(t) for t in tiled]
    in_specs += [pl.BlockSpec(a.shape, lambda i, nd=a.ndim: (0,) * nd) for a in full]
    out_specs = [pl.BlockSpec((tm, w), lambda i: (i, 0)) for w, _ in outs]
    out_specs += [pl.BlockSpec((r, w), lambda i: (0, 0)) for r, w in accs]
    out_shape = [jax.ShapeDtypeStruct((tp, w), dt) for w, dt in outs]
    out_shape += [jax.ShapeDtypeStruct((r, w), F32) for r, w in accs]
    res = pl.pallas_call(
        body, name=name, grid=(nt,), in_specs=in_specs, out_specs=out_specs, out_shape=out_shape,
        compiler_params=pltpu.CompilerParams(dimension_semantics=("arbitrary",), vmem_limit_bytes=VMEM_LIMIT),
    )(*[t.arr for t in tiled], *full)
    return res


def _pick(n, cap, mult):
    best = None
    for d in range(mult, min(n, cap) + 1, mult):
        if n % d == 0:
            best = d
    assert best is not None, (n, cap, mult)
    return best


def _mm(name, a, b, mode, out_dtype=F32, resid=None):
    if mode == "tn":
        m, k = a.shape
        n = b.shape[1]
        tm = _pick(m, 640, 16)
        tk = _pick(k, 1408, 128)
        tn = _pick(n, 1408, 128)
        nm = m // tm

        def body_tn(a_ref, b_ref, o_ref):
            s = pl.program_id(2)
            acc = _tn(a_ref[...], b_ref[...])

            @pl.when(s == 0)
            def _():
                o_ref[...] = acc

            @pl.when(s > 0)
            def _():
                o_ref[...] += acc

        return pl.pallas_call(
            body_tn, name=name, grid=(k // tk, n // tn, nm),
            in_specs=[pl.BlockSpec((tm, tk), lambda p, j, s: (s, p)),
                      pl.BlockSpec((tm, tn), lambda p, j, s: (s, j))],
            out_specs=pl.BlockSpec((tk, tn), lambda p, j, s: (p, j)),
            out_shape=jax.ShapeDtypeStruct((k, n), F32),
            compiler_params=pltpu.CompilerParams(
                dimension_semantics=("parallel", "parallel", "arbitrary"), vmem_limit_bytes=VMEM_LIMIT),
        )(a, b)

    m, k = a.shape
    n = b.shape[1] if mode == "nn" else b.shape[0]
    tn = _pick(n, 1408, 128)
    tm = _pick(m, 640 if k <= 3072 else 320, 16)
    dotf = _nn if mode == "nn" else _nt

    def body(*refs):
        if resid is None:
            a_ref, b_ref, o_ref = refs
            o_ref[...] = dotf(a_ref[...], b_ref[...]).astype(o_ref.dtype)
        else:
            a_ref, b_ref, r_ref, o_ref = refs
            o_ref[...] = (r_ref[...] + dotf(a_ref[...], b_ref[...])).astype(o_ref.dtype)

    b_spec = (pl.BlockSpec((k, tn), lambda j, i: (0, j)) if mode == "nn"
              else pl.BlockSpec((tn, k), lambda j, i: (j, 0)))
    in_specs = [pl.BlockSpec((tm, k), lambda j, i: (i, 0)), b_spec]
    args = [a, b]
    if resid is not None:
        in_specs.append(pl.BlockSpec((tm, tn), lambda j, i: (i, j)))
        args.append(resid)
    return pl.pallas_call(
        body, name=name, grid=(n // tn, m // tm), in_specs=in_specs,
        out_specs=pl.BlockSpec((tm, tn), lambda j, i: (i, j)),
        out_shape=jax.ShapeDtypeStruct((m, n), out_dtype),
        compiler_params=pltpu.CompilerParams(
            dimension_semantics=("parallel", "parallel"), vmem_limit_bytes=VMEM_LIMIT),
    )(*args)


ATTN_Q_TILES = 4


def _attn_probs(q, k, row0):
    tq, tp = q.shape[0], k.shape[0]
    s = _nt(q, k) * (1.0 / math.sqrt(QK_HEAD))
    row = row0 + lax.broadcasted_iota(jnp.int32, (tq, tp), 0)
    col = lax.broadcasted_iota(jnp.int32, (tq, tp), 1)
    ok = (col <= row) & (col >= PAD)
    s = jnp.where(ok, s, NEG)
    m = jnp.max(s, axis=-1, keepdims=True)
    e = jnp.exp(s - m)
    e = jnp.where(ok, e, 0.0)
    l = jnp.sum(e, axis=-1, keepdims=True)
    return e / jnp.maximum(l, 1e-30)


def _attn_fwd(q, k, v):
    tp = q.shape[0]
    tq = tp // ATTN_Q_TILES

    def body(q_ref, k_ref, v_ref, o_ref):
        for i in range(ATTN_Q_TILES):
            rows = slice(i * tq, (i + 1) * tq)
            keys = slice(0, (i + 1) * tq)
            p = _attn_probs(q_ref[rows, :], k_ref[keys, :], i * tq)
            o_ref[rows, :] = _nn(p, v_ref[keys, :])

    return pl.pallas_call(
        body, name="attn_fwd", grid=(MLA_HEADS,),
        in_specs=[pl.BlockSpec((tp, HP), lambda h: (0, h)),
                  pl.BlockSpec((tp, HP), lambda h: (0, h)),
                  pl.BlockSpec((tp, V_HEAD), lambda h: (0, h))],
        out_specs=pl.BlockSpec((tp, V_HEAD), lambda h: (0, h)),
        out_shape=jax.ShapeDtypeStruct((tp, MLA_HEADS * V_HEAD), F32),
        compiler_params=pltpu.CompilerParams(dimension_semantics=("parallel",), vmem_limit_bytes=VMEM_LIMIT),
    )(q, k, v)


def _attn_bwd(q, k, v, do):
    tp = q.shape[0]
    tq = tp // ATTN_Q_TILES

    def body(q_ref, k_ref, v_ref, do_ref, dq_ref, dk_ref, dv_ref):
        for i in reversed(range(ATTN_Q_TILES)):
            rows = slice(i * tq, (i + 1) * tq)
            keys = slice(0, (i + 1) * tq)
            qb = q_ref[rows, :]
            kk = k_ref[keys, :]
            dob = do_ref[rows, :]
            p = _attn_probs(qb, kk, i * tq)
            dp = _nt(dob, v_ref[keys, :])
            delta = jnp.sum(p * dp, axis=-1, keepdims=True)
            ds = p * (dp - delta) * (1.0 / math.sqrt(QK_HEAD))
            dq_ref[rows, :] = _nn(ds, kk)
            if i == ATTN_Q_TILES - 1:
                dk_ref[...] = _tn(ds, qb)
                dv_ref[...] = _tn(p, dob)
            else:
                dk_ref[keys, :] += _tn(ds, qb)
                dv_ref[keys, :] += _tn(p, dob)

    full = lambda w: pl.BlockSpec((tp, w), lambda h: (0, h))
    return pl.pallas_call(
        body, name="attn_bwd", grid=(MLA_HEADS,),
        in_specs=[full(HP), full(HP), full(V_HEAD), full(V_HEAD)],
        out_specs=[full(HP), full(HP), full(V_HEAD)],
        out_shape=[jax.ShapeDtypeStruct((tp, MLA_HEADS * HP), F32),
                   jax.ShapeDtypeStruct((tp, MLA_HEADS * HP), F32),
                   jax.ShapeDtypeStruct((tp, MLA_HEADS * V_HEAD), F32)],
        compiler_params=pltpu.CompilerParams(dimension_semantics=("parallel",), vmem_limit_bytes=VMEM_LIMIT),
    )(q, k, v, do)


def _gdn_consts():
    c = DN_CHUNK
    r = lax.broadcasted_iota(jnp.int32, (c, c), 0)
    cc = lax.broadcasted_iota(jnp.int32, (c, c), 1)
    incl = r >= cc
    strict = r > cc
    return incl, strict


def _each(fn, *lists):
    return [fn(*a) for a in zip(*lists)]


def _interleave(chains):
    chains = list(chains)
    while chains:
        for ch in list(chains):
            try:
                next(ch)
            except StopIteration:
                chains.remove(ch)


def _gdn_chunk_common(q_ref, k_ref, v_ref, g_ref, b_ref):
    c = DN_CHUNK
    incl, strict = _gdn_consts()
    sls = [slice(DN_DIM * h, DN_DIM * (h + 1)) for h in range(DN_HEADS)]
    inclf = incl.astype(F32)
    ones = jnp.full((c, LANE), 1.0 / LANE, F32)
    q = [q_ref[:, sl] * (1.0 / math.sqrt(DN_DIM)) for sl in sls]
    k = [k_ref[:, sl] for sl in sls]
    v = [v_ref[:, sl] for sl in sls]
    g = [g_ref[:, sl] for sl in sls]
    beta = [b_ref[:, sl] for sl in sls]
    gc = [_nn(inclf, x, hp=True) for x in g]
    grow = [_nt(ones, x, hp=True) for x in gc]
    kb = _each(jnp.multiply, k, beta)
    kk = _each(_nt, kb, k)
    qk = _each(_nt, q, k)
    gam = [jnp.exp(x) for x in gc]
    g_last = [_rowsum(x) for x in g]
    dm = [jnp.exp(jnp.where(incl, x[:, :c] - y, NEG)) for x, y in zip(gc, grow)]
    vb = _each(jnp.multiply, v, beta)
    kbg = _each(jnp.multiply, kb, gam)
    ek = [jnp.exp(x - y) for x, y in zip(g_last, gc)]
    kd = _each(jnp.multiply, k, ek)
    return dict(q=q, k=k, v=v, beta=beta, gc=gc, gam=gam, g_last=g_last, dm=dm, kb=kb, vb=vb,
                kbg=kbg, kk=kk, ek=ek, kd=kd, qk=qk, incl=incl, strict=strict, sls=sls)


def _gdn_fwd(q, k, v, g, beta):
    tp = q.shape[0]
    c = DN_CHUNK
    nch = tp // c

    def body(q_ref, k_ref, v_ref, g_ref, b_ref, o_ref, s_ref, t_ref, s_scr):
        @pl.when(pl.program_id(0) == 0)
        def _():
            s_scr[...] = jnp.zeros_like(s_scr)

        eye = (lax.broadcasted_iota(jnp.int32, (c, c), 0) == lax.broadcasted_iota(jnp.int32, (c, c), 1)).astype(F32)
        x = _gdn_chunk_common(q_ref, k_ref, v_ref, g_ref, b_ref)
        heads = range(DN_HEADS)
        s = [s_scr[h] for h in heads]
        bp = [-jnp.where(x["strict"], kk * dm, 0.0) for kk, dm in zip(x["kk"], x["dm"])]
        t = [eye + b for b in bp]
        for _ in range(5):
            bp = [_nn(b, b, hp=True) for b in bp]
            t = [tt + _nn(tt, b, hp=True) for tt, b in zip(t, bp)]
        u = _each(_nn, t, x["vb"])
        w = _each(_nn, t, x["kbg"])
        v_new = [uu - _nn(ww, ss) for uu, ww, ss in zip(u, w, s)]
        o = [_nn(q * gam, ss) + _nn(qk * dm, vn)
             for q, gam, ss, qk, dm, vn in zip(x["q"], x["gam"], s, x["qk"], x["dm"], v_new)]
        s_new = [ss * jnp.exp(gl) + _tn(kd, vn) for ss, gl, kd, vn in zip(s, x["g_last"], x["kd"], v_new)]
        for h in heads:
            s_ref[h, 0] = s[h]
            t_ref[h, 0] = t[h]
            o_ref[:, x["sls"][h]] = o[h]
            s_scr[h] = s_new[h]

    rb = lambda n: (n, 0)
    return pl.pallas_call(
        body, name="gdn_fwd", grid=(nch,),
        in_specs=[pl.BlockSpec((c, DN_WIDTH), rb)] * 5,
        out_specs=[pl.BlockSpec((c, DN_WIDTH), rb),
                   pl.BlockSpec((DN_HEADS, 1, DN_DIM, DN_DIM), lambda n: (0, n, 0, 0)),
                   pl.BlockSpec((DN_HEADS, 1, c, c), lambda n: (0, n, 0, 0))],
        out_shape=[jax.ShapeDtypeStruct((tp, DN_WIDTH), F32),
                   jax.ShapeDtypeStruct((DN_HEADS, nch, DN_DIM, DN_DIM), F32),
                   jax.ShapeDtypeStruct((DN_HEADS, nch, c, c), F32)],
        scratch_shapes=[pltpu.VMEM((DN_HEADS, DN_DIM, DN_DIM), F32)],
        compiler_params=pltpu.CompilerParams(dimension_semantics=("arbitrary",), vmem_limit_bytes=VMEM_LIMIT),
    )(q, k, v, g, beta)


def _gdn_bwd(q, k, v, g, beta, s_all, t_all, do):
    tp = q.shape[0]
    c = DN_CHUNK
    nch = tp // c

    def body(q_ref, k_ref, v_ref, g_ref, b_ref, s_ref, t_ref, do_ref,
             dq_ref, dk_ref, dv_ref, dg_ref, db_ref, ds_scr):
        @pl.when(pl.program_id(0) == 0)
        def _():
            ds_scr[...] = jnp.zeros_like(ds_scr)

        ones_cl = jnp.ones((c, LANE), F32)
        xs = _gdn_chunk_common(q_ref, k_ref, v_ref, g_ref, b_ref)
        upper = jnp.logical_not(xs["strict"]).astype(F32)

        def chain(h):
            x = {key: (val[h] if isinstance(val, list) else val) for key, val in xs.items()}
            sl = x["sls"]
            qs, kx, vx, beta_, gam, dm = x["q"], x["k"], x["v"], x["beta"], x["gam"], x["dm"]
            kb, vb, kbg, kd, ek = x["kb"], x["vb"], x["kbg"], x["kd"], x["ek"]
            t = t_ref[h, 0]
            s = s_ref[h, 0]
            dsn = ds_scr[h]
            dob = do_ref[:, sl]
            eg_last = jnp.exp(x["g_last"])
            u = _nn(t, vb)
            w = _nn(t, kbg)
            mqk = x["qk"] * dm
            qd = qs * gam
            dqd = _nt(dob, s)
            dkd_pre = _nn(kd, dsn)
            yield
            v_new = u - _nn(w, s)
            dv_new = _tn(mqk, dob) + dkd_pre
            dq = dqd * gam
            dgam = jnp.sum(dqd * qs, axis=1, keepdims=True)
            yield
            ds_new = _tn(qd, dob) + eg_last * dsn - _tn(w, dv_new)
            dmm = jnp.where(x["incl"], _nt(dob, v_new), 0.0)
            dkd = _nt(v_new, dsn)
            dw = -_nt(dv_new, s)
            dvb = _tn(t, dv_new)
            dt = _nt(dv_new, vb)
            yield
            dqk = dmm * dm
            e_mat = dmm * mqk
            dq = dq + _nn(dqk, kx)
            dk = _tn(dqk, qs) + dkd * ek
            e1 = jnp.sum(dkd * kd, axis=1, keepdims=True)
            dgc = -e1
            dg_last = jnp.sum(e1) + eg_last * jnp.sum(s * dsn)
            dt = dt + _nt(dw, kbg)
            dkbg = _tn(t, dw)
            yield
            tdt = _tn(t, dt, hp=True)
            yield
            da = jnp.where(x["strict"], -_nt(tdt, t, hp=True), 0.0)
            yield
            dkk = da * dm
            e_mat = e_mat + da * x["kk"] * dm
            dkb = _nn(dkk, kx) + dkbg * gam
            dk = dk + _tn(dkk, kb)
            dgam = dgam + jnp.sum(dkbg * kb, axis=1, keepdims=True)
            yield
            dk = dk + dkb * beta_
            dbeta = jnp.sum(dkb * kx, axis=1, keepdims=True) + jnp.sum(dvb * vx, axis=1, keepdims=True)
            dv = dvb * beta_
            dgc = dgc + jnp.sum(e_mat, axis=1, keepdims=True) + dgam * gam
            dgc = dgc - _tn(e_mat, ones_cl, hp=True)
            yield
            dg = _nn(upper, dgc, hp=True) + dg_last
            yield
            ds_scr[h] = ds_new
            dq_ref[:, sl] = dq * (1.0 / math.sqrt(DN_DIM))
            dk_ref[:, sl] = dk
            dv_ref[:, sl] = dv
            dg_ref[:, sl] = dg
            db_ref[:, sl] = jnp.broadcast_to(dbeta, (c, LANE))

        _interleave([chain(h) for h in range(DN_HEADS)])

    rb = lambda n: (nch - 1 - n, 0)
    hs = lambda n: (0, nch - 1 - n, 0, 0)
    return pl.pallas_call(
        body, name="gdn_bwd", grid=(nch,),
        in_specs=[pl.BlockSpec((c, DN_WIDTH), rb)] * 5
        + [pl.BlockSpec((DN_HEADS, 1, DN_DIM, DN_DIM), hs), pl.BlockSpec((DN_HEADS, 1, c, c), hs),
           pl.BlockSpec((c, DN_WIDTH), rb)],
        out_specs=[pl.BlockSpec((c, DN_WIDTH), rb)] * 5,
        out_shape=[jax.ShapeDtypeStruct((tp, DN_WIDTH), F32)] * 5,
        scratch_shapes=[pltpu.VMEM((DN_HEADS, DN_DIM, DN_DIM), F32)],
        compiler_params=pltpu.CompilerParams(dimension_semantics=("arbitrary",), vmem_limit_bytes=VMEM_LIMIT),
    )(q, k, v, g, beta, s_all, t_all, do)


def _silu_parts(x):
    s = _sigmoid(x)
    return x * s, s * (1.0 + x * (1.0 - s))


def _f_rms_cast(i, x, w):
    y, _ = _rms_fwd(x, w, x.shape[1])
    return (y,), ()


def _f_rms_bwd_add(i, x, dy, dres, w, *, mask_pad):
    dx, dwr = _rms_bwd(x, w, dy, x.shape[1])
    out = dres + dx
    if mask_pad:
        out = jnp.where(_row_ids(i, x.shape[0]) >= PAD, out, 0.0)
    return (out,), (_rowsum(dwr),)


def _f_lat_norm(i, ql, kvl, qw, kvw):
    return (_rms_fwd(ql, qw, Q_LORA)[0], _rms_fwd(kvl, kvw, KV_LORA)[0]), ()


def _f_lat_norm_bwd(i, ql, kvl, dqn, dkvn, qw, kvw):
    dq, dqw = _rms_bwd(ql, qw, dqn, Q_LORA)
    dk, dkw = _rms_bwd(kvl, kvw, dkvn, KV_LORA)
    return (dq, dk), (_rowsum(dqw), _rowsum(dkw))


def _rope(x, cos, sin_s):
    return x * cos + _swap_halves(x) * sin_s


def _rope_t(dy, cos, sin_s):
    return dy * cos + _swap_halves(dy * sin_s)


def _f_mla_qk(i, qf, kvf, kpe, cos, sin_s, qw, kw):
    qs, ks, vs = [], [], []
    for h in range(MLA_HEADS):
        qn, _ = _rms_fwd(qf[:, HP * h:HP * (h + 1)], qw, QK_HEAD)
        qs += [qn[:, :QK_NOPE], _rope(qn[:, QK_NOPE:], cos, sin_s)]
        kh = jnp.concatenate([kvf[:, HP * h:HP * h + QK_NOPE], kpe], axis=1)
        kn, _ = _rms_fwd(kh, kw, QK_HEAD)
        ks += [kn[:, :QK_NOPE], _rope(kn[:, QK_NOPE:], cos, sin_s)]
        vs.append(kvf[:, HP * h + QK_NOPE:HP * (h + 1)])
    return (jnp.concatenate(qs, axis=1), jnp.concatenate(ks, axis=1), jnp.concatenate(vs, axis=1)), ()


def _f_mla_qk_bwd(i, qf, kvf, kpe, cos, sin_s, dq, dk, dv, qw, kw):
    dqf, dkvf = [], []
    dkpe = None
    dqw = None
    dkw = None
    for h in range(MLA_HEADS):
        dqh = dq[:, HP * h:HP * (h + 1)]
        dqn = jnp.concatenate([dqh[:, :QK_NOPE], _rope_t(dqh[:, QK_NOPE:], cos, sin_s)], axis=1)
        dx, dwr = _rms_bwd(qf[:, HP * h:HP * (h + 1)], qw, dqn, QK_HEAD)
        dqf.append(dx)
        dqw = _rowsum(dwr) if dqw is None else dqw + _rowsum(dwr)
        dkh = dk[:, HP * h:HP * (h + 1)]
        dkn = jnp.concatenate([dkh[:, :QK_NOPE], _rope_t(dkh[:, QK_NOPE:], cos, sin_s)], axis=1)
        kh = jnp.concatenate([kvf[:, HP * h:HP * h + QK_NOPE], kpe], axis=1)
        dx, dwr = _rms_bwd(kh, kw, dkn, QK_HEAD)
        dkvf += [dx[:, :QK_NOPE], dv[:, V_HEAD * h:V_HEAD * (h + 1)]]
        dkpe = dx[:, QK_NOPE:] if dkpe is None else dkpe + dx[:, QK_NOPE:]
        dkw = _rowsum(dwr) if dkw is None else dkw + _rowsum(dwr)
    return (jnp.concatenate(dqf, axis=1), jnp.concatenate(dkvf, axis=1), dkpe), (dqw, dkw)


def _gdn_act(i, x, halo, w8):
    tm = x.shape[0]
    halo = jnp.where(i > 0, halo, 0.0)
    c = _conv_fwd(x, halo, w8, DN_CONV)
    act, dact = _silu_parts(c)
    return act, dact


def _f_gdn_prep(i, x, halo, ab, w8, alog, dtb, sel):
    tm = x.shape[0]
    act, _ = _gdn_act(i, x, halo, w8)
    outs = []
    for part in range(2):
        for h in range(DN_HEADS):
            t = act[:, DN_WIDTH * part + DN_DIM * h:DN_WIDTH * part + DN_DIM * (h + 1)]
            outs.append(t * lax.rsqrt(jnp.sum(t * t, axis=-1, keepdims=True) + EPS))
    q = jnp.concatenate(outs[:DN_HEADS], axis=1)
    k = jnp.concatenate(outs[DN_HEADS:], axis=1)
    v = act[:, 2 * DN_WIDTH:]
    abb = _nn(ab, sel, hp=True)
    valid = _row_ids(i, tm) >= PAD
    g = jnp.where(valid, -jnp.exp(alog) * _softplus(abb[:, :DN_WIDTH] + dtb), 0.0)
    beta = jnp.where(valid, _sigmoid(abb[:, DN_WIDTH:]), 0.0)
    return (q, k, v, g, beta), ()


def _f_gdn_prep_bwd(i, x, halo, ab, dq, dk, dv, dg, dbeta, w8, alog, dtb, sel, selpick):
    tm = x.shape[0]
    act, dact = _gdn_act(i, x, halo, w8)
    douts = []
    for part, dd in enumerate((dq, dk)):
        for h in range(DN_HEADS):
            t = act[:, DN_WIDTH * part + DN_DIM * h:DN_WIDTH * part + DN_DIM * (h + 1)]
            r = lax.rsqrt(jnp.sum(t * t, axis=-1, keepdims=True) + EPS)
            y = t * r
            dy = dd[:, DN_DIM * h:DN_DIM * (h + 1)]
            douts.append(r * (dy - y * jnp.sum(dy * y, axis=-1, keepdims=True)))
    douts.append(dv)
    dc = jnp.concatenate(douts, axis=1) * dact
    abb = _nn(ab, sel, hp=True)
    valid = _row_ids(i, tm) >= PAD
    pre = abb[:, :DN_WIDTH] + dtb
    ea = jnp.exp(alog)
    g = -ea * _softplus(pre)
    dg = jnp.where(valid, dg, 0.0)
    dbeta = jnp.where(valid, dbeta, 0.0)
    da = dg * (-ea) * _sigmoid(pre)
    beta = _sigmoid(abb[:, DN_WIDTH:])
    db = dbeta * beta * (1.0 - beta)
    dab = _nn(jnp.concatenate([da, db], axis=1), selpick, hp=True)
    return (dc, dab), (_rowsum(dg * g), _rowsum(da))


def _f_conv_bwd(i, dy, dy_next, x, x_prev, w8, *, width, nt):
    dy_next = jnp.where(i < nt - 1, dy_next, 0.0)
    x_prev = jnp.where(i > 0, x_prev, 0.0)
    return (_conv_bwd_x(dy, dy_next, w8, width),), (_conv_bwd_w(dy, x, x_prev, width),)


def _f_mix(i, o_mla, o_dn, z, w_mla, w_dn):
    tm = o_mla.shape[0]
    valid = _row_ids(i, tm) >= PAD
    outs = []
    for h in range(MLA_HEADS):
        y, _ = _rms_fwd(o_mla[:, V_HEAD * h:V_HEAD * (h + 1)], w_mla, V_HEAD)
        outs.append(jnp.where(valid, y, 0.0))
    for h in range(DN_HEADS):
        y, _ = _rms_fwd(o_dn[:, DN_DIM * h:DN_DIM * (h + 1)], w_dn, DN_DIM)
        outs.append(y * _silu_parts(z[:, DN_DIM * h:DN_DIM * (h + 1)])[0])
    return (jnp.concatenate(outs, axis=1),), ()


def _f_mix_bwd(i, o_mla, o_dn, z, dy_mla, dy_dn, w_mla, w_dn):
    tm = o_mla.shape[0]
    valid = _row_ids(i, tm) >= PAD
    d_mla, d_dn, d_z = [], [], []
    dw_mla = None
    dw_dn = None
    for h in range(MLA_HEADS):
        sl = slice(V_HEAD * h, V_HEAD * (h + 1))
        dx, dwr = _rms_bwd(o_mla[:, sl], w_mla, jnp.where(valid, dy_mla[:, sl], 0.0), V_HEAD)
        d_mla.append(dx)
        dw_mla = _rowsum(dwr) if dw_mla is None else dw_mla + _rowsum(dwr)
    for h in range(DN_HEADS):
        sl = slice(DN_DIM * h, DN_DIM * (h + 1))
        y, _ = _rms_fwd(o_dn[:, sl], w_dn, DN_DIM)
        sz, dsz = _silu_parts(z[:, sl])
        d_z.append(dy_dn[:, sl] * y * dsz)
        dx, dwr = _rms_bwd(o_dn[:, sl], w_dn, dy_dn[:, sl] * sz, DN_DIM)
        d_dn.append(dx)
        dw_dn = _rowsum(dwr) if dw_dn is None else dw_dn + _rowsum(dwr)
    return ((jnp.concatenate(d_mla, axis=1), jnp.concatenate(d_dn, axis=1), jnp.concatenate(d_z, axis=1)),
            (dw_mla, dw_dn))


def _f_ffn_act(i, gate_pre, halo, up, w8, b):
    halo = jnp.where(i > 0, halo, 0.0)
    gate = _conv_fwd(gate_pre, halo, w8, FFN_CONV) + b
    return (_silu_parts(gate)[0] * up,), ()


def _f_ffn_act_bwd(i, gate_pre, halo, up, dact, w8, b):
    halo = jnp.where(i > 0, halo, 0.0)
    gate = _conv_fwd(gate_pre, halo, w8, FFN_CONV) + b
    sg, dsg = _silu_parts(gate)
    dgate = dact * up * dsg
    return (dgate, dact * sg), (_rowsum(dgate),)


def _f_loss(i, h3, tgt):
    tm = h3.shape[0]
    diff = jnp.where(_row_ids(i, tm) >= ROW0, h3 - tgt, 0.0)
    part = 0.5 * jnp.sum(diff * diff) * (1.0 / D_MODEL)
    return (diff * (1.0 / D_MODEL),), (jnp.full((1, LANE), part, F32),)


def _local_step(h0, tgt, w):
    tp = h0.shape[0]
    nt = tp // TM
    bf = (D_MODEL, _MXU)
    u, = _rows("rms_in", _f_rms_cast, [_In(h0)], [w["attn_norm_w"]], [bf])
    proj = _mm("in_proj", u, w["w_in"], "nn")
    p_qkv = lambda kind="cur": _In(proj, 3 * DN_WIDTH, 0, kind)
    p_z = _In(proj, DN_WIDTH, C_Z // DN_WIDTH)
    p_ql = _In(proj, Q_LORA, C_QL // Q_LORA)
    p_kvl = _In(proj, KV_LORA, C_KVL // KV_LORA)
    p_kpe = _In(proj, LANE, C_KPE // LANE)
    p_ab = _In(proj, LANE, C_AB // LANE)
    cos, sin_s = _In(w["cos"]), _In(w["sin_s"])

    qn, kvn = _rows("mla_lat_norm", _f_lat_norm, [p_ql, p_kvl], [w["q_a_norm_w"], w["kv_a_norm_w"]],
                    [(Q_LORA, _MXU), (KV_LORA, _MXU)])
    qf = _mm("mla_q_b", qn, w["w_q_b"], "nn")
    kvf = _mm("mla_kv_b", kvn, w["w_kv_b"], "nn")
    qk_w = [w["q_norm_w"], w["k_norm_w"]]
    q, k, v = _rows("mla_qk", _f_mla_qk, [_In(qf), _In(kvf), p_kpe, cos, sin_s], qk_w,
                    [(MLA_HEADS * HP, _MXU), (MLA_HEADS * HP, _MXU), (MLA_HEADS * V_HEAD, _MXU)])
    o_mla = _attn_fwd(q, k, v)

    dn_w = [w["dn_conv_w"], w["alog_b"], w["dtb_b"], w["sel"]]
    gq, gk, gv, gg, gb = _rows("gdn_prep", _f_gdn_prep, [p_qkv(), p_qkv("prev"), p_ab], dn_w,
                               [(DN_WIDTH, F32)] * 5)
    o_dn, s_all, t_all = _gdn_fwd(gq, gk, gv, gg, gb)

    out_w = [w["mla_out_norm_w"], w["dn_out_norm_w"]]
    mixed, = _rows("mix", _f_mix, [_In(o_mla), _In(o_dn), p_z], out_w, [bf])
    h2 = _mm("out_proj", mixed, w["w_out"], "nn", resid=h0)

    hn, = _rows("rms_ffn", _f_rms_cast, [_In(h2)], [w["ffn_norm_w"]], [bf])
    gate_pre = _mm("ffn_gate", hn, w["w_gate"], "nn")
    up = _mm("ffn_up", hn, w["w_up"], "nn")
    ffn_w = [w["ffn_conv_w"], w["ffn_conv_b"]]
    act, = _rows("ffn_act", _f_ffn_act, [_In(gate_pre), _In(gate_pre, kind="prev"), _In(up)], ffn_w,
                 [(D_FF, _MXU)])
    h3 = _mm("ffn_down", act, w["w_down"], "nn", resid=h2)

    dh3, loss = _rows("loss", _f_loss, [_In(h3), _In(tgt)], [], [(D_MODEL, F32)], [(1, LANE)])

    g = {}
    dact = _mm("ffn_down_dx", dh3, w["w_down"], "nt")
    g["w_down"] = _mm("ffn_down_dw", act, dh3, "tn")
    dgate, dup, g["ffn_conv_b"] = _rows(
        "ffn_act_bwd", _f_ffn_act_bwd, [_In(gate_pre), _In(gate_pre, kind="prev"), _In(up), _In(dact)], ffn_w,
        [(D_FF, F32), (D_FF, F32)], [(1, D_FF)])
    dgate_pre, g["ffn_conv_w"] = _rows(
        "ffn_conv_bwd", functools.partial(_f_conv_bwd, width=FFN_CONV, nt=nt),
        [_In(dgate), _In(dgate, kind="next"), _In(gate_pre), _In(gate_pre, kind="prev")], [w["ffn_conv_w"]],
        [(D_FF, F32)], [(8, D_FF)])
    dhn = _mm("ffn_gate_dx", dgate_pre, w["w_gate"], "nt")
    dhn = _mm("ffn_up_dx", dup, w["w_up"], "nt", resid=dhn)
    g["w_gate"] = _mm("ffn_gate_dw", hn, dgate_pre, "tn")
    g["w_up"] = _mm("ffn_up_dw", hn, dup, "tn")
    dh2, g["ffn_norm_w"] = _rows(
        "rms_ffn_bwd", functools.partial(_f_rms_bwd_add, mask_pad=True), [_In(h2), _In(dhn), _In(dh3)],
        [w["ffn_norm_w"]], [(D_MODEL, F32)], [(1, D_MODEL)])

    dmixed = _mm("out_proj_dx", dh2, w["w_out"], "nt")
    g["w_out"] = _mm("out_proj_dw", mixed, dh2, "tn")
    half = MLA_HEADS * V_HEAD
    do_mla, do_dn, dz, g["mla_out_norm_w"], g["dn_out_norm_w"] = _rows(
        "mix_bwd", _f_mix_bwd, [_In(o_mla), _In(o_dn), p_z, _In(dmixed, half, 0), _In(dmixed, half, 1)], out_w,
        [(half, F32), (DN_WIDTH, F32), (DN_WIDTH, F32)], [(1, V_HEAD), (1, DN_DIM)])

    dq, dk, dv = _attn_bwd(q, k, v, do_mla)
    dqf, dkvf, dkpe, g["q_norm_w"], g["k_norm_w"] = _rows(
        "mla_qk_bwd", _f_mla_qk_bwd, [_In(qf), _In(kvf), p_kpe, cos, sin_s, _In(dq), _In(dk), _In(dv)], qk_w,
        [(MLA_HEADS * HP, F32), (MLA_HEADS * HP, F32), (LANE, F32)], [(1, HP), (1, HP)])
    dqn = _mm("mla_q_b_dx", dqf, w["w_q_b"], "nt")
    g["w_q_b"] = _mm("mla_q_b_dw", qn, dqf, "tn")
    dkvn = _mm("mla_kv_b_dx", dkvf, w["w_kv_b"], "nt")
    g["w_kv_b"] = _mm("mla_kv_b_dw", kvn, dkvf, "tn")
    dql, dkvl, g["q_a_norm_w"], g["kv_a_norm_w"] = _rows(
        "mla_lat_norm_bwd", _f_lat_norm_bwd, [p_ql, p_kvl, _In(dqn), _In(dkvn)],
        [w["q_a_norm_w"], w["kv_a_norm_w"]], [(Q_LORA, F32), (KV_LORA, F32)], [(1, Q_LORA), (1, KV_LORA)])

    dgq, dgk, dgv, dgg, dgb = _gdn_bwd(gq, gk, gv, gg, gb, s_all, t_all, do_dn)
    dc, dab, g["alog_b"], g["dtb_b"] = _rows(
        "gdn_prep_bwd", _f_gdn_prep_bwd,
        [p_qkv(), p_qkv("prev"), p_ab, _In(dgq), _In(dgk), _In(dgv), _In(dgg), _In(dgb)], dn_w + [w["selpick"]],
        [(3 * DN_WIDTH, F32), (LANE, F32)], [(1, DN_WIDTH), (1, DN_WIDTH)])
    dqkv, g["dn_conv_w"] = _rows(
        "gdn_conv_bwd", functools.partial(_f_conv_bwd, width=DN_CONV, nt=nt),
        [_In(dc), _In(dc, kind="next"), p_qkv(), p_qkv("prev")], [w["dn_conv_w"]],
        [(3 * DN_WIDTH, F32)], [(8, 3 * DN_WIDTH)])

    dproj = jnp.concatenate([dqkv, dz, dql, dkvl, dkpe, dab], axis=1)
    du = _mm("in_proj_dx", dproj, w["w_in"], "nt")
    g["w_in"] = _mm("in_proj_dw", u, dproj, "tn")
    dh0, g["attn_norm_w"] = _rows(
        "rms_in_bwd", functools.partial(_f_rms_bwd_add, mask_pad=False), [_In(h0), _In(du), _In(dh2)],
        [w["attn_norm_w"]], [(D_MODEL, F32)], [(1, D_MODEL)])
    return loss, dh0, g


def _w_in_to_padded(w):
    c1, c2, c3 = Q_LORA, Q_LORA + KV_LORA, Q_LORA + KV_LORA + QK_ROPE
    c4 = c3 + 3 * DN_WIDTH
    c5 = c4 + DN_WIDTH
    z = lambda n: jnp.zeros((w.shape[0], n), w.dtype)
    return jnp.concatenate([w[:, c3:c4], w[:, c4:c5], w[:, :c1], w[:, c1:c2], w[:, c2:c3], z(LANE - QK_ROPE),
                            w[:, c5:], z(LANE - 2 * DN_HEADS)], axis=1)


def _w_in_from_padded(g):
    return jnp.concatenate([g[:, C_QL:C_QL + Q_LORA], g[:, C_KVL:C_KVL + KV_LORA], g[:, C_KPE:C_KPE + QK_ROPE],
                            g[:, :C_Z + DN_WIDTH], g[:, C_AB:C_AB + 2 * DN_HEADS]], axis=1)


def _w_q_b_to_padded(w):
    r = w.shape[0]
    w = w.reshape(r, MLA_HEADS, QK_HEAD)
    return jnp.pad(w, ((0, 0), (0, 0), (0, HP - QK_HEAD))).reshape(r, MLA_HEADS * HP)


def _w_q_b_from_padded(g):
    r = g.shape[0]
    return g.reshape(r, MLA_HEADS, HP)[:, :, :QK_HEAD].reshape(r, MLA_HEADS * QK_HEAD)


def _pad_rows8(w):
    return jnp.pad(w, ((0, 8 - w.shape[0]), (0, 0)))


def _prepare(full, tp):
    w = {}
    mx = lambda a: a.astype(_MXU)
    w["attn_norm_w"] = full["attn_norm_w"]
    w["w_in"] = mx(_w_in_to_padded(full["w_in"]))
    w["q_a_norm_w"] = full["q_a_norm_w"]
    w["kv_a_norm_w"] = full["kv_a_norm_w"]
    w["w_q_b"] = mx(_w_q_b_to_padded(full["w_q_b"]))
    w["w_kv_b"] = mx(full["w_kv_b"])
    w["q_norm_w"] = jnp.pad(full["q_norm_w"], ((0, 0), (0, HP - QK_HEAD)))
    w["k_norm_w"] = jnp.pad(full["k_norm_w"], ((0, 0), (0, HP - QK_HEAD)))
    w["mla_out_norm_w"] = full["mla_out_norm_w"]
    w["dn_out_norm_w"] = full["dn_out_norm_w"]
    w["dn_conv_w"] = _pad_rows8(full["dn_conv_w"])
    w["alog_b"] = jnp.repeat(full["dn_A_log"], DN_DIM, axis=1)
    w["dtb_b"] = jnp.repeat(full["dn_dt_bias"], DN_DIM, axis=1)
    w["w_out"] = mx(full["w_out"])
    w["ffn_norm_w"] = full["ffn_norm_w"]
    w["w_gate"] = mx(full["w_gate"])
    w["w_up"] = mx(full["w_up"])
    w["ffn_conv_w"] = _pad_rows8(full["ffn_conv_w"])
    w["ffn_conv_b"] = full["ffn_conv_b"]
    w["w_down"] = mx(full["w_down"])
    half = QK_ROPE // 2
    inv = ROPE_THETA ** (-jnp.arange(half, dtype=F32) / half)
    ang = (jnp.arange(tp, dtype=jnp.int32) - PAD).astype(F32)[:, None] * inv[None, :]
    zc = jnp.zeros((tp, LANE - QK_ROPE), F32)
    w["cos"] = jnp.concatenate([jnp.cos(ang), jnp.cos(ang), zc], axis=1)
    w["sin_s"] = jnp.concatenate([-jnp.sin(ang), jnp.sin(ang), zc], axis=1)
    lane = jnp.arange(2 * DN_WIDTH)[None, :]
    src = jnp.arange(LANE)[:, None]
    w["sel"] = ((lane // DN_DIM) == src).astype(F32)
    w["selpick"] = ((src.T == (lane.T // DN_DIM)) & (lane.T % DN_DIM == 0)).astype(F32)
    return w


def _grads_to_natural(g):
    n = dict(g)
    n["w_in"] = _w_in_from_padded(g["w_in"])
    n["w_q_b"] = _w_q_b_from_padded(g["w_q_b"])
    n["q_norm_w"] = g["q_norm_w"][:, :QK_HEAD]
    n["k_norm_w"] = g["k_norm_w"][:, :QK_HEAD]
    n["dn_conv_w"] = g["dn_conv_w"][:DN_CONV]
    n["ffn_conv_w"] = g["ffn_conv_w"][:FFN_CONV]
    n["dn_A_log"] = n.pop("alog_b")[:, ::DN_DIM]
    n["dn_dt_bias"] = n.pop("dtb_b")[:, ::DN_DIM]
    return n


_MESH = pl.DeviceIdType.MESH
_ANY = pl.BlockSpec(memory_space=pl.ANY)
_CHIP_FLIPS = ((1, 0), (0, 1), (1, 1))


def _me():
    return lax.axis_index("x"), lax.axis_index("y"), lax.axis_index("c")


def _all_gather(name, blk):
    def body(x_ref, out_ref, send_sems, recv_sems, local_sem):
        x, y, c = _me()
        me, sib = (x, y, c), (x, y, 1 - c)
        chips = [(x ^ fx, y ^ fy) for fx, fy in _CHIP_FLIPS]

        def slot(p):
            return out_ref.at[4 * p[0] + 2 * p[1] + p[2]]

        def copy(k, block, to, src=None):
            return pltpu.make_async_remote_copy(
                src_ref=slot(block) if src is None else src, dst_ref=slot(block),
                send_sem=send_sems.at[k], recv_sem=recv_sems.at[k], device_id=to, device_id_type=_MESH)

        mine = pltpu.make_async_copy(x_ref, slot(me), local_sem)
        mine.start()
        first = [copy(0, me, sib, src=x_ref)]
        first += [copy(1 + j, me, (*chip, c), src=x_ref) for j, chip in enumerate(chips)]
        for cp in first:
            cp.start()
        passed = [copy(4 + j, (*chip, c), sib) for j, chip in enumerate(chips)]
        for j, chip in enumerate(chips):
            copy(1 + j, (*chip, c), me).wait_recv()
            passed[j].start()
        copy(0, sib, me).wait_recv()
        for j, chip in enumerate(chips):
            copy(4 + j, (*chip, 1 - c), me).wait_recv()
        for cp in first + passed:
            cp.wait_send()
        mine.wait()

    return pl.pallas_call(
        body, name=name, in_specs=[_ANY], out_specs=_ANY,
        out_shape=jax.ShapeDtypeStruct((N_DEV,) + blk.shape, blk.dtype),
        scratch_shapes=[pltpu.SemaphoreType.DMA((7,)), pltpu.SemaphoreType.DMA((7,)), pltpu.SemaphoreType.DMA],
    )(blk)


def _rs_sibling(name, gb):
    def body(g_ref, out_ref, send_sems, recv_sems):
        x, y, c = _me()
        cps = []
        for j in range(4):
            cp = pltpu.make_async_remote_copy(
                src_ref=g_ref.at[2 * j + (1 - c)], dst_ref=out_ref.at[j], send_sem=send_sems.at[j],
                recv_sem=recv_sems.at[j], device_id=(x, y, 1 - c), device_id_type=_MESH)
            cp.start()
            cps.append(cp)
        for cp in cps:
            cp.wait()

    return pl.pallas_call(
        body, name=name, in_specs=[_ANY], out_specs=_ANY,
        out_shape=jax.ShapeDtypeStruct((4,) + gb.shape[1:], gb.dtype),
        scratch_shapes=[pltpu.SemaphoreType.DMA((4,)), pltpu.SemaphoreType.DMA((4,))],
    )(gb)


def _rs_chips(name, s1):
    def body(s_ref, out_ref, send_sems, recv_sems):
        x, y, c = _me()
        cps = []
        for k, (fx, fy) in enumerate(_CHIP_FLIPS):
            px, py = x ^ fx, y ^ fy
            cp = pltpu.make_async_remote_copy(
                src_ref=s_ref.at[2 * px + py], dst_ref=out_ref.at[k], send_sem=send_sems.at[k],
                recv_sem=recv_sems.at[k], device_id=(px, py, c), device_id_type=_MESH)
            cp.start()
            cps.append(cp)
        for cp in cps:
            cp.wait()

    return pl.pallas_call(
        body, name=name, in_specs=[_ANY], out_specs=_ANY,
        out_shape=jax.ShapeDtypeStruct((3,) + s1.shape[1:], s1.dtype),
        scratch_shapes=[pltpu.SemaphoreType.DMA((3,)), pltpu.SemaphoreType.DMA((3,))],
    )(s1)


def _row_tile(r):
    return _pick(r, 512, 8)


def _pair_sum(name, gb, recv):
    _, r, cols = gb.shape
    tm = _row_tile(r)
    c = lax.axis_index("c").astype(jnp.int32).reshape(1)

    def body(c_ref, a_ref, b_ref, o_ref, ob_ref):
        s = a_ref[...] + b_ref[...]
        o_ref[...] = s
        ob_ref[...] = s.astype(BF16)

    blk = pl.BlockSpec((1, tm, cols), lambda j, i, c_ref: (j, i, 0))
    return pl.pallas_call(
        body, name=name,
        grid_spec=pltpu.PrefetchScalarGridSpec(
            num_scalar_prefetch=1, grid=(4, r // tm),
            in_specs=[pl.BlockSpec((1, tm, cols), lambda j, i, c_ref: (2 * j + c_ref[0], i, 0)), blk],
            out_specs=[blk, blk]),
        out_shape=[jax.ShapeDtypeStruct((4, r, cols), F32), jax.ShapeDtypeStruct((4, r, cols), BF16)],
        compiler_params=pltpu.CompilerParams(dimension_semantics=("parallel", "parallel")),
    )(c, gb, recv)


def _adam(name, parts, w, m, v):
    r, cols = w.shape
    tm = _row_tile(r)
    idx = jnp.stack([jnp.asarray(s, jnp.int32) for _, s in parts])
    n = len(parts)

    def body(idx_ref, *refs):
        g = refs[0][0].astype(F32)
        for p_ref in refs[1:n]:
            g = g + p_ref[0].astype(F32)
        w_ref, m_ref, v_ref, g_out, d_out, m_out, v_out = refs[n:]
        m_new = ADAM_B1 * m_ref[...] + (1.0 - ADAM_B1) * g
        v_new = ADAM_B2 * v_ref[...] + (1.0 - ADAM_B2) * (g * g)
        m_hat = m_new / (1.0 - ADAM_B1 ** ADAM_STEP)
        v_hat = v_new / (1.0 - ADAM_B2 ** ADAM_STEP)
        g_out[...] = g
        d_out[...] = -ADAM_LR * (m_hat / (jnp.sqrt(v_hat) + ADAM_EPS) + ADAM_WD * w_ref[...])
        m_out[...] = m_new
        v_out[...] = v_new

    part_specs = [pl.BlockSpec((1, tm, cols), lambda i, idx_ref, p=p: (idx_ref[p], i, 0)) for p in range(n)]
    flat = pl.BlockSpec((tm, cols), lambda i, idx_ref: (i, 0))
    return pl.pallas_call(
        body, name=name,
        grid_spec=pltpu.PrefetchScalarGridSpec(
            num_scalar_prefetch=1, grid=(r // tm,), in_specs=part_specs + [flat] * 3, out_specs=[flat] * 4),
        out_shape=[jax.ShapeDtypeStruct((r, cols), F32)] * 4,
        compiler_params=pltpu.CompilerParams(dimension_semantics=("parallel",)),
    )(idx, *[a for a, _ in parts], w, m, v)


def _all_gather_many(name, blks):
    n = len(blks)

    def body(*refs):
        x_refs, out_refs = refs[:n], refs[n:2 * n]
        send_sems, recv_sems, local_sems = refs[2 * n:]
        x, y, c = _me()
        me, sib = (x, y, c), (x, y, 1 - c)
        chips = [(x ^ fx, y ^ fy) for fx, fy in _CHIP_FLIPS]

        def slot(a, p):
            return out_refs[a].at[4 * p[0] + 2 * p[1] + p[2]]

        def copy(a, k, block, to, src=None):
            return pltpu.make_async_remote_copy(
                src_ref=slot(a, block) if src is None else src, dst_ref=slot(a, block),
                send_sem=send_sems.at[7 * a + k], recv_sem=recv_sems.at[7 * a + k], device_id=to,
                device_id_type=_MESH)

        mine = [pltpu.make_async_copy(x_refs[a], slot(a, me), local_sems.at[a]) for a in range(n)]
        first = []
        for a in range(n):
            mine[a].start()
            first.append(copy(a, 0, me, sib, src=x_refs[a]))
            first += [copy(a, 1 + j, me, (*chip, c), src=x_refs[a]) for j, chip in enumerate(chips)]
        for cp in first:
            cp.start()
        passed = []
        for j, chip in enumerate(chips):
            for a in range(n):
                copy(a, 1 + j, (*chip, c), me).wait_recv()
                cp = copy(a, 4 + j, (*chip, c), sib)
                cp.start()
                passed.append(cp)
        for a in range(n):
            copy(a, 0, sib, me).wait_recv()
            for j, chip in enumerate(chips):
                copy(a, 4 + j, (*chip, 1 - c), me).wait_recv()
        for cp in first + passed:
            cp.wait_send()
        for cp in mine:
            cp.wait()

    return pl.pallas_call(
        body, name=name, in_specs=[_ANY] * n, out_specs=[_ANY] * n,
        out_shape=[jax.ShapeDtypeStruct((N_DEV,) + b.shape, b.dtype) for b in blks],
        scratch_shapes=[pltpu.SemaphoreType.DMA((7 * n,)), pltpu.SemaphoreType.DMA((7 * n,)),
                        pltpu.SemaphoreType.DMA((n,))],
    )(*blks)


def _rs_sibling_many(name, gbs):
    n = len(gbs)

    def body(*refs):
        g_refs, out_refs = refs[:n], refs[n:2 * n]
        send_sems, recv_sems = refs[2 * n:]
        x, y, c = _me()
        cps = []
        for a in range(n):
            for j in range(4):
                cp = pltpu.make_async_remote_copy(
                    src_ref=g_refs[a].at[2 * j + (1 - c)], dst_ref=out_refs[a].at[j],
                    send_sem=send_sems.at[4 * a + j], recv_sem=recv_sems.at[4 * a + j],
                    device_id=(x, y, 1 - c), device_id_type=_MESH)
                cp.start()
                cps.append(cp)
        for cp in cps:
            cp.wait()

    return pl.pallas_call(
        body, name=name, in_specs=[_ANY] * n, out_specs=[_ANY] * n,
        out_shape=[jax.ShapeDtypeStruct((4,) + g.shape[1:], g.dtype) for g in gbs],
        scratch_shapes=[pltpu.SemaphoreType.DMA((4 * n,)), pltpu.SemaphoreType.DMA((4 * n,))],
    )(*gbs)


def _rs_chips_many(name, s1s):
    n = len(s1s)

    def body(*refs):
        s_refs, out_refs = refs[:n], refs[n:2 * n]
        send_sems, recv_sems = refs[2 * n:]
        x, y, c = _me()
        cps = []
        for a in range(n):
            for k, (fx, fy) in enumerate(_CHIP_FLIPS):
                px, py = x ^ fx, y ^ fy
                cp = pltpu.make_async_remote_copy(
                    src_ref=s_refs[a].at[2 * px + py], dst_ref=out_refs[a].at[k],
                    send_sem=send_sems.at[3 * a + k], recv_sem=recv_sems.at[3 * a + k],
                    device_id=(px, py, c), device_id_type=_MESH)
                cp.start()
                cps.append(cp)
        for cp in cps:
            cp.wait()

    return pl.pallas_call(
        body, name=name, in_specs=[_ANY] * n, out_specs=[_ANY] * n,
        out_shape=[jax.ShapeDtypeStruct((3,) + s.shape[1:], s.dtype) for s in s1s],
        scratch_shapes=[pltpu.SemaphoreType.DMA((3 * n,)), pltpu.SemaphoreType.DMA((3 * n,))],
    )(*s1s)


_SHARDED = (
    ("meta_tokens", 1, (N_META, D_MODEL)),
    ("w_in", 1, (D_MODEL, IN_COLS)),
    ("w_q_b", 1, (Q_LORA, MLA_HEADS * QK_HEAD)),
    ("w_kv_b", 1, (KV_LORA, MLA_HEADS * (QK_NOPE + V_HEAD))),
    ("dn_conv_w", 1, (DN_CONV, 3 * DN_WIDTH)),
    ("w_out", 0, (2 * DN_WIDTH, D_MODEL)),
    ("w_gate", 1, (D_MODEL, D_FF)),
    ("w_up", 1, (D_MODEL, D_FF)),
    ("ffn_conv_w", 1, (FFN_CONV, D_FF)),
    ("w_down", 0, (D_FF, D_MODEL)),
)
_MXU_GATHERED = ("w_in", "w_q_b", "w_kv_b", "w_out", "w_gate", "w_up", "w_down")
_F32_GATHERED = ("meta_tokens", "dn_conv_w", "ffn_conv_w")
_REPLICATED = (
    ("attn_norm_w", D_MODEL), ("q_a_norm_w", Q_LORA), ("kv_a_norm_w", KV_LORA), ("q_norm_w", QK_HEAD),
    ("k_norm_w", QK_HEAD), ("mla_out_norm_w", V_HEAD), ("dn_A_log", DN_HEADS), ("dn_dt_bias", DN_HEADS),
    ("dn_out_norm_w", DN_DIM), ("ffn_norm_w", D_MODEL), ("ffn_conv_b", D_FF),
)
_PACK_COLS = 1024
_PACK_ROW_MULT = 320
_SMALL_SHAPE = (8, 768)
_SMALL_BLOCK = (8, 512)


def _local_shape(dim, shape):
    return (shape[0] // N_DEV, shape[1]) if dim == 0 else (shape[0], shape[1] // N_DEV)


def _pack_rows(n, mult):
    rows = -(-n // _PACK_COLS)
    return -(-rows // mult) * mult


def _pack(flats, mult, axis=0):
    cat = jnp.concatenate(flats, axis=-1)
    n = cat.shape[-1]
    r = _pack_rows(n, mult)
    pad = [(0, 0)] * (cat.ndim - 1) + [(0, r * _PACK_COLS - n)]
    return jnp.pad(cat, pad).reshape(cat.shape[:-1] + (r, _PACK_COLS))


def _to_blocks(full, dim):
    r, c = full.shape
    if dim == 0:
        return full.reshape(N_DEV, (r // N_DEV) * c)
    return full.reshape(r, N_DEV, c // N_DEV).transpose(1, 0, 2).reshape(N_DEV, r * (c // N_DEV))


def _from_blocks(blocks, dim, shape):
    r, c = shape
    if dim == 0:
        return blocks.reshape(r, c)
    return blocks.reshape(N_DEV, r, c // N_DEV).transpose(1, 0, 2).reshape(r, c)


def _split(flat, sizes):
    out, o = [], 0
    for s in sizes:
        out.append(flat[..., o:o + s])
        o += s
    return out


def _gather_weights(local, names, dtype, mult):
    specs = [s for s in _SHARDED if s[0] in names]
    pack = _pack([local[n].astype(dtype).reshape(-1) for n, _, _ in specs], mult)
    got = _all_gather("gather_" + "_".join(n[:5] for n in names[:2]), pack)
    flat = got.reshape(N_DEV, -1)
    sizes = [math.prod(_local_shape(d, s)) for _, d, s in specs]
    return {n: _from_blocks(p, d, s) for (n, d, s), p in zip(specs, _split(flat, sizes))}


def kernel(x, meta_tokens, attn_norm_w, w_in, q_a_norm_w, w_q_b, kv_a_norm_w, w_kv_b, q_norm_w, k_norm_w, mla_out_norm_w, dn_conv_w, dn_A_log, dn_dt_bias, dn_out_norm_w, w_out, ffn_norm_w, w_gate, w_up, ffn_conv_w, ffn_conv_b, w_down, loss_target, m_meta_tokens, m_attn_norm_w, m_w_in, m_q_a_norm_w, m_w_q_b, m_kv_a_norm_w, m_w_kv_b, m_q_norm_w, m_k_norm_w, m_mla_out_norm_w, m_dn_conv_w, m_dn_A_log, m_dn_dt_bias, m_dn_out_norm_w, m_w_out, m_ffn_norm_w, m_w_gate, m_w_up, m_ffn_conv_w, m_ffn_conv_b, m_w_down, v_meta_tokens, v_attn_norm_w, v_w_in, v_q_a_norm_w, v_w_q_b, v_kv_a_norm_w, v_w_kv_b, v_q_norm_w, v_k_norm_w, v_mla_out_norm_w, v_dn_conv_w, v_dn_A_log, v_dn_dt_bias, v_dn_out_norm_w, v_w_out, v_ffn_norm_w, v_w_gate, v_w_up, v_ffn_conv_w, v_ffn_conv_b, v_w_down):
    names = [n for n, _, _ in _SHARDED] + [n for n, _ in _REPLICATED]
    given = dict(locals())
    two_d = lambda a: a.reshape(a.shape[-2:])
    wl = {n: two_d(given[n]) for n in names}
    ml = {n: two_d(given["m_" + n]) for n in names}
    vl = {n: two_d(given["v_" + n]) for n in names}
    out_shapes = {n: given[n].shape for n in names}

    spec = {n: (d, s) for n, d, s in _SHARDED}
    small_sizes = [math.prod(_local_shape(*spec[n])) for n in _F32_GATHERED]

    def small_block(d):
        cat = jnp.concatenate([d[n].reshape(d[n].shape[:-2] + (-1,)) for n in _F32_GATHERED], axis=-1)
        pad = [(0, 0)] * (cat.ndim - 1) + [(0, math.prod(_SMALL_BLOCK) - cat.shape[-1])]
        return jnp.pad(cat, pad).reshape(cat.shape[:-1] + _SMALL_BLOCK)

    got = _all_gather_many("gather_weights", [wl[n].astype(_MXU) for n in _MXU_GATHERED] + [small_block(wl)])
    full = dict(wl)
    for n, blocks in zip(_MXU_GATHERED, got):
        d, s = spec[n]
        full[n] = blocks.reshape(s) if d == 0 else blocks.transpose(1, 0, 2).reshape(s)
    for n, p in zip(_F32_GATHERED, _split(got[-1].reshape(N_DEV, -1), small_sizes)):
        full[n] = _from_blocks(p, *spec[n])

    seq = x.shape[1]
    tp = ROW0 + seq
    h0 = jnp.concatenate([jnp.zeros((PAD, D_MODEL), F32), full["meta_tokens"], x[0]], axis=0)
    tgt = jnp.concatenate([jnp.zeros((ROW0, D_MODEL), F32), loss_target[0]], axis=0)
    loss, dh0, g = _local_step(h0, tgt, _prepare(full, tp))
    g = _grads_to_natural(g)
    g["meta_tokens"] = dh0[PAD:ROW0]
    grad_x = dh0[ROW0:][None]

    def dest_blocks(n):
        d, s = spec[n]
        r, c = _local_shape(d, s)
        return g[n].reshape(N_DEV, r, c) if d == 0 else g[n].reshape(r, N_DEV, c).transpose(1, 0, 2)

    gbs = {n: dest_blocks(n) for n, _, _ in _SHARDED}
    groups = list(_MXU_GATHERED) + ["small"]
    gb = [gbs[n] for n in _MXU_GATHERED] + [small_block(gbs)]
    from_sib = _rs_sibling_many("rs_sibling", gb)
    sums = [_pair_sum("rs_pair_sum_" + n, a, b) for n, a, b in zip(groups, gb, from_sib)]
    from_chips = _rs_chips_many("rs_chips", [sb for _, sb in sums])
    my_chip = 2 * lax.axis_index("x") + lax.axis_index("y")
    big = [{}, {}, {}, {}]
    for n, (s1, _), fc in zip(groups, sums, from_chips):
        loc = (lambda d: small_block(d)) if n == "small" else (lambda d, n=n: d[n])
        res = _adam("adam_" + n, [(s1, my_chip), (fc, 0), (fc, 1), (fc, 2)], loc(wl), loc(ml), loc(vl))
        for kind, a in enumerate(res):
            if n == "small":
                big[kind].update(zip(_F32_GATHERED, _split(a.reshape(-1), small_sizes)))
            else:
                big[kind][n] = a

    def small(d, extra):
        cat = jnp.concatenate([d[n].reshape(-1) for n, _ in _REPLICATED] + [extra])
        return jnp.pad(cat, (0, math.prod(_SMALL_SHAPE) - cat.shape[0])).reshape(_SMALL_SHAPE)

    zero1 = jnp.zeros((1,), F32)
    parts = _all_gather("gather_small_grads", small(g, loss[0, :1]))
    sm = _adam("adam_replicated", [(parts, d) for d in range(N_DEV)], small(wl, zero1), small(ml, zero1),
               small(vl, zero1))
    rsizes = [n for _, n in _REPLICATED] + [1]
    sm = [dict(zip([n for n, _ in _REPLICATED] + ["loss"], _split(a.reshape(-1), rsizes))) for a in sm]

    outs = [sm[0]["loss"].reshape(()), grad_x]
    for kind in range(4):
        for n in ("meta_tokens", "attn_norm_w", "w_in", "q_a_norm_w", "w_q_b", "kv_a_norm_w", "w_kv_b", "q_norm_w",
                  "k_norm_w", "mla_out_norm_w", "dn_conv_w", "dn_A_log", "dn_dt_bias", "dn_out_norm_w", "w_out",
                  "ffn_norm_w", "w_gate", "w_up", "ffn_conv_w", "ffn_conv_b", "w_down"):
            src = big[kind] if n in big[kind] else sm[kind]
            outs.append(src[n].reshape(out_shapes[n]))
    return tuple(outs)
```

```python
import functools
import math

import jax
import jax.numpy as jnp
from jax import lax
from jax.experimental import pallas as pl
from jax.experimental.pallas import tpu as pltpu

F32 = jnp.float32
BF16 = jnp.bfloat16
_MXU = jnp.bfloat16
_HI = lax.Precision.HIGHEST

D_MODEL = 1024
N_META = 16
PAD = 112
ROW0 = PAD + N_META
MLA_HEADS = 4
QK_NOPE = 128
QK_ROPE = 64
QK_HEAD = QK_NOPE + QK_ROPE
V_HEAD = 128
Q_LORA = 256
KV_LORA = 256
ROPE_THETA = 10000.0
DN_HEADS = 4
DN_DIM = 128
DN_WIDTH = DN_HEADS * DN_DIM
DN_CONV = 4
DN_CHUNK = 64
D_FF = 2816
FFN_CONV = 3
EPS = 1e-6
HP = 256
C_QKV = 0
C_Z = 1536
C_QL = 2048
C_KVL = 2304
C_KPE = 2560
C_AB = 2688
IN_P = 2816
IN_COLS = 2632

ADAM_LR = 0.001
ADAM_B1 = 0.9
ADAM_B2 = 0.999
ADAM_EPS = 1e-08
ADAM_WD = 0.01
ADAM_STEP = 10

N_DEV = 8
TM = 128
LANE = 128
VMEM_LIMIT = 56 * 1024 * 1024
NEG = -1e30


def _dot(a, b, dims, hp=False):
    if hp:
        return lax.dot_general(a.astype(F32), b.astype(F32), (dims, ((), ())),
                               precision=_HI, preferred_element_type=F32)
    return lax.dot_general(a.astype(_MXU), b.astype(_MXU), (dims, ((), ())),
                           preferred_element_type=F32)


def _nn(a, b, hp=False):
    return _dot(a, b, ((1,), (0,)), hp)


def _nt(a, b, hp=False):
    return _dot(a, b, ((1,), (1,)), hp)


def _tn(a, b, hp=False):
    return _dot(a, b, ((0,), (0,)), hp)


def _sigmoid(x):
    return 1.0 / (1.0 + jnp.exp(-x))


def _rms_fwd(x, w, n):
    r = lax.rsqrt(jnp.sum(x * x, axis=-1, keepdims=True) * (1.0 / n) + EPS)
    return x * r * w, r


def _rms_bwd(x, w, dy, n):
    r = lax.rsqrt(jnp.sum(x * x, axis=-1, keepdims=True) * (1.0 / n) + EPS)
    xh = x * r
    gy = dy * w
    dx = r * (gy - xh * (jnp.sum(gy * xh, axis=-1, keepdims=True) * (1.0 / n)))
    return dx, dy * xh


def _rowsum(x):
    return jnp.sum(x, axis=0, keepdims=True)


def _row_ids(i, tm):
    return i * tm + lax.broadcasted_iota(jnp.int32, (tm, 1), 0)


def _shift_down(ext, s, tm):
    if s == 0:
        return ext[8:8 + tm]
    return pltpu.roll(ext, s, 0)[8:8 + tm]


def _shift_up(ext, s, tm):
    if s == 0:
        return ext[0:tm]
    return pltpu.roll(ext, tm + 8 - s, 0)[0:tm]


def _conv_fwd(x, halo_prev, w, width):
    tm = x.shape[0]
    ext = jnp.concatenate([halo_prev, x], axis=0)
    y = None
    for j in range(width):
        t = w[j:j + 1, :] * _shift_down(ext, width - 1 - j, tm)
        y = t if y is None else y + t
    return y


def _conv_bwd_x(dy, halo_next, w, width):
    tm = dy.shape[0]
    ext = jnp.concatenate([dy, halo_next], axis=0)
    dx = None
    for j in range(width):
        t = w[j:j + 1, :] * _shift_up(ext, width - 1 - j, tm)
        dx = t if dx is None else dx + t
    return dx


def _conv_bwd_w(dy, x, halo_prev, width):
    tm = dy.shape[0]
    ext = jnp.concatenate([halo_prev, x], axis=0)
    rows = [_rowsum(dy * _shift_down(ext, width - 1 - j, tm)) for j in range(width)]
    rows += [jnp.zeros_like(rows[0])] * (8 - width)
    return jnp.concatenate(rows, axis=0)


def _softplus(x):
    e = jnp.exp(-jnp.abs(x))
    u = 1.0 + e
    l1p = jnp.where(u == 1.0, e, jnp.log(u) * e / jnp.where(u == 1.0, 1.0, u - 1.0))
    return jnp.maximum(x, 0.0) + l1p


def _swap_halves(x):
    lane = lax.broadcasted_iota(jnp.int32, x.shape, 1)
    return jnp.where(lane < 32, pltpu.roll(x, 96, 1), jnp.where(lane < 64, pltpu.roll(x, 32, 1), 0.0))


class _In:
    def __init__(self, arr, width=None, cb=0, kind="cur"):
        self.arr, self.kind = arr, kind
        self.width = arr.shape[1] if width is None else width
        self.cb = cb


def _rows(name, fn, tiled, full, outs, accs=(), tm=TM):
    tp = tiled[0].arr.shape[0]
    nt = tp // tm
    r8 = tm // 8
    n_in = len(tiled) + len(full)
    n_out = len(outs)

    def body(*refs):
        i = pl.program_id(0)
        vals = [r[...] for r in refs[:n_in]]
        o_t, o_a = fn(i, *vals)
        for r, v in zip(refs[n_in:n_in + n_out], o_t):
            r[...] = v.astype(r.dtype)
        for r, v in zip(refs[n_in + n_out:], o_a):
            @pl.when(i == 0)
            def _():
                r[...] = v

            @pl.when(i > 0)
            def _():
                r[...] += v

    def spec(t):
        if t.kind == "cur":
            return pl.BlockSpec((tm, t.width), lambda i, cb=t.cb: (i, cb))
        if t.kind == "prev":
            return pl.BlockSpec((8, t.width), lambda i, cb=t.cb: (jnp.maximum(i * r8 - 1, 0), cb))
        return pl.BlockSpec((8, t.width), lambda i, cb=t.cb: (jnp.minimum((i + 1) * r8, tp // 8 - 1), cb))

    in_specs = [spec(t) for t in tiled]
    in_specs += [pl.BlockSpec(a.shape, lambda i, nd=a.ndim: (0,) * nd) for a in full]
    out_specs = [pl.BlockSpec((tm, w), lambda i: (i, 0)) for w, _ in outs]
    out_specs += [pl.BlockSpec((r, w), lambda i: (0, 0)) for r, w in accs]
    out_shape = [jax.ShapeDtypeStruct((tp, w), dt) for w, dt in outs]
    out_shape += [jax.ShapeDtypeStruct((r, w), F32) for r, w in accs]
    res = pl.pallas_call(
        body, name=name, grid=(nt,), in_specs=in_specs, out_specs=out_specs, out_shape=out_shape,
        compiler_params=pltpu.CompilerParams(dimension_semantics=("arbitrary",), vmem_limit_bytes=VMEM_LIMIT),
    )(*[t.arr for t in tiled], *full)
    return res


def _pick(n, cap, mult):
    best = None
    for d in range(mult, min(n, cap) + 1, mult):
        if n % d == 0:
            best = d
    assert best is not None, (n, cap, mult)
    return best


def _mm(name, a, b, mode, out_dtype=F32, resid=None):
    if mode == "tn":
        m, k = a.shape
        n = b.shape[1]
        tk = _pick(k, 512, 128)
        tn = _pick(n, 1408, 128)

        def body_tn(a_ref, b_ref, o_ref):
            o_ref[...] = _tn(a_ref[...], b_ref[...])

        return pl.pallas_call(
            body_tn, name=name, grid=(n // tn, k // tk),
            in_specs=[pl.BlockSpec((m, tk), lambda j, p: (0, p)),
                      pl.BlockSpec((m, tn), lambda j, p: (0, j))],
            out_specs=pl.BlockSpec((tk, tn), lambda j, p: (p, j)),
            out_shape=jax.ShapeDtypeStruct((k, n), F32),
            compiler_params=pltpu.CompilerParams(
                dimension_semantics=("parallel", "parallel"), vmem_limit_bytes=VMEM_LIMIT),
        )(a, b)

    m, k = a.shape
    n = b.shape[1] if mode == "nn" else b.shape[0]
    tn = _pick(n, 1408, 128)
    tm = _pick(m, 640 if k <= 3072 else 320, 16)
    dotf = _nn if mode == "nn" else _nt

    def body(*refs):
        if resid is None:
            a_ref, b_ref, o_ref = refs
            o_ref[...] = dotf(a_ref[...], b_ref[...]).astype(o_ref.dtype)
        else:
            a_ref, b_ref, r_ref, o_ref = refs
            o_ref[...] = (r_ref[...] + dotf(a_ref[...], b_ref[...])).astype(o_ref.dtype)

    b_spec = (pl.BlockSpec((k, tn), lambda j, i: (0, j)) if mode == "nn"
              else pl.BlockSpec((tn, k), lambda j, i: (j, 0)))
    in_specs = [pl.BlockSpec((tm, k), lambda j, i: (i, 0)), b_spec]
    args = [a, b]
    if resid is not None:
        in_specs.append(pl.BlockSpec((tm, tn), lambda j, i: (i, j)))
        args.append(resid)
    return pl.pallas_call(
        body, name=name, grid=(n // tn, m // tm), in_specs=in_specs,
        out_specs=pl.BlockSpec((tm, tn), lambda j, i: (i, j)),
        out_shape=jax.ShapeDtypeStruct((m, n), out_dtype),
        compiler_params=pltpu.CompilerParams(
            dimension_semantics=("parallel", "parallel"), vmem_limit_bytes=VMEM_LIMIT),
    )(*args)


ATTN_Q_TILES = 4


def _attn_probs(q, k, row0):
    tq, tp = q.shape[0], k.shape[0]
    s = _nt(q, k) * (1.0 / math.sqrt(QK_HEAD))
    row = row0 + lax.broadcasted_iota(jnp.int32, (tq, tp), 0)
    col = lax.broadcasted_iota(jnp.int32, (tq, tp), 1)
    ok = (col <= row) & (col >= PAD)
    s = jnp.where(ok, s, NEG)
    m = jnp.max(s, axis=-1, keepdims=True)
    e = jnp.exp(s - m)
    e = jnp.where(ok, e, 0.0)
    l = jnp.sum(e, axis=-1, keepdims=True)
    return e / jnp.maximum(l, 1e-30)


def _attn_fwd(q, k, v):
    tp = q.shape[0]
    tq = tp // ATTN_Q_TILES

    def body(q_ref, k_ref, v_ref, o_ref):
        for i in range(ATTN_Q_TILES):
            rows = slice(i * tq, (i + 1) * tq)
            keys = slice(0, (i + 1) * tq)
            p = _attn_probs(q_ref[rows, :], k_ref[keys, :], i * tq)
            o_ref[rows, :] = _nn(p, v_ref[keys, :])

    return pl.pallas_call(
        body, name="attn_fwd", grid=(MLA_HEADS,),
        in_specs=[pl.BlockSpec((tp, HP), lambda h: (0, h)),
                  pl.BlockSpec((tp, HP), lambda h: (0, h)),
                  pl.BlockSpec((tp, V_HEAD), lambda h: (0, h))],
        out_specs=pl.BlockSpec((tp, V_HEAD), lambda h: (0, h)),
        out_shape=jax.ShapeDtypeStruct((tp, MLA_HEADS * V_HEAD), F32),
        compiler_params=pltpu.CompilerParams(dimension_semantics=("parallel",), vmem_limit_bytes=VMEM_LIMIT),
    )(q, k, v)


def _attn_bwd(q, k, v, do):
    tp = q.shape[0]
    tq = tp // ATTN_Q_TILES

    def body(q_ref, k_ref, v_ref, do_ref, dq_ref, dk_ref, dv_ref):
        for i in reversed(range(ATTN_Q_TILES)):
            rows = slice(i * tq, (i + 1) * tq)
            keys = slice(0, (i + 1) * tq)
            qb = q_ref[rows, :]
            kk = k_ref[keys, :]
            dob = do_ref[rows, :]
            p = _attn_probs(qb, kk, i * tq)
            dp = _nt(dob, v_ref[keys, :])
            delta = jnp.sum(p * dp, axis=-1, keepdims=True)
            ds = p * (dp - delta) * (1.0 / math.sqrt(QK_HEAD))
            dq_ref[rows, :] = _nn(ds, kk)
            if i == ATTN_Q_TILES - 1:
                dk_ref[...] = _tn(ds, qb)
                dv_ref[...] = _tn(p, dob)
            else:
                dk_ref[keys, :] += _tn(ds, qb)
                dv_ref[keys, :] += _tn(p, dob)

    full = lambda w: pl.BlockSpec((tp, w), lambda h: (0, h))
    return pl.pallas_call(
        body, name="attn_bwd", grid=(MLA_HEADS,),
        in_specs=[full(HP), full(HP), full(V_HEAD), full(V_HEAD)],
        out_specs=[full(HP), full(HP), full(V_HEAD)],
        out_shape=[jax.ShapeDtypeStruct((tp, MLA_HEADS * HP), F32),
                   jax.ShapeDtypeStruct((tp, MLA_HEADS * HP), F32),
                   jax.ShapeDtypeStruct((tp, MLA_HEADS * V_HEAD), F32)],
        compiler_params=pltpu.CompilerParams(dimension_semantics=("parallel",), vmem_limit_bytes=VMEM_LIMIT),
    )(q, k, v, do)


def _gdn_consts():
    c = DN_CHUNK
    r = lax.broadcasted_iota(jnp.int32, (c, c), 0)
    cc = lax.broadcasted_iota(jnp.int32, (c, c), 1)
    incl = r >= cc
    strict = r > cc
    return incl, strict


def _each(fn, *lists):
    return [fn(*a) for a in zip(*lists)]


def _interleave(chains):
    chains = list(chains)
    while chains:
        for ch in list(chains):
            try:
                next(ch)
            except StopIteration:
                chains.remove(ch)


def _gdn_chunk_common(q_ref, k_ref, v_ref, g_ref, b_ref):
    c = DN_CHUNK
    incl, strict = _gdn_consts()
    sls = [slice(DN_DIM * h, DN_DIM * (h + 1)) for h in range(DN_HEADS)]
    inclf = incl.astype(F32)
    ones = jnp.full((c, LANE), 1.0 / LANE, F32)
    q = [q_ref[:, sl] * (1.0 / math.sqrt(DN_DIM)) for sl in sls]
    k = [k_ref[:, sl] for sl in sls]
    v = [v_ref[:, sl] for sl in sls]
    g = [g_ref[:, sl] for sl in sls]
    beta = [b_ref[:, sl] for sl in sls]
    gc = [_nn(inclf, x, hp=True) for x in g]
    grow = [_nt(ones, x, hp=True) for x in gc]
    kb = _each(jnp.multiply, k, beta)
    kk = _each(_nt, kb, k)
    qk = _each(_nt, q, k)
    gam = [jnp.exp(x) for x in gc]
    g_last = [_rowsum(x) for x in g]
    dm = [jnp.exp(jnp.where(incl, x[:, :c] - y, NEG)) for x, y in zip(gc, grow)]
    vb = _each(jnp.multiply, v, beta)
    kbg = _each(jnp.multiply, kb, gam)
    ek = [jnp.exp(x - y) for x, y in zip(g_last, gc)]
    kd = _each(jnp.multiply, k, ek)
    return dict(q=q, k=k, v=v, beta=beta, gc=gc, gam=gam, g_last=g_last, dm=dm, kb=kb, vb=vb,
                kbg=kbg, kk=kk, ek=ek, kd=kd, qk=qk, incl=incl, strict=strict, sls=sls)


def _gdn_fwd(q, k, v, g, beta):
    tp = q.shape[0]
    c = DN_CHUNK
    nch = tp // c

    def body(q_ref, k_ref, v_ref, g_ref, b_ref, o_ref, s_ref, t_ref, s_scr):
        @pl.when(pl.program_id(0) == 0)
        def _():
            s_scr[...] = jnp.zeros_like(s_scr)

        eye = (lax.broadcasted_iota(jnp.int32, (c, c), 0) == lax.broadcasted_iota(jnp.int32, (c, c), 1)).astype(F32)
        x = _gdn_chunk_common(q_ref, k_ref, v_ref, g_ref, b_ref)
        heads = range(DN_HEADS)
        s = [s_scr[h] for h in heads]
        bp = [-jnp.where(x["strict"], kk * dm, 0.0) for kk, dm in zip(x["kk"], x["dm"])]
        t = [eye + b for b in bp]
        for _ in range(5):
            bp = [_nn(b, b, hp=True) for b in bp]
            t = [tt + _nn(tt, b, hp=True) for tt, b in zip(t, bp)]
        u = _each(_nn, t, x["vb"])
        w = _each(_nn, t, x["kbg"])
        v_new = [uu - _nn(ww, ss) for uu, ww, ss in zip(u, w, s)]
        o = [_nn(q * gam, ss) + _nn(qk * dm, vn)
             for q, gam, ss, qk, dm, vn in zip(x["q"], x["gam"], s, x["qk"], x["dm"], v_new)]
        s_new = [ss * jnp.exp(gl) + _tn(kd, vn) for ss, gl, kd, vn in zip(s, x["g_last"], x["kd"], v_new)]
        for h in heads:
            s_ref[h, 0] = s[h]
            t_ref[h, 0] = t[h]
            o_ref[:, x["sls"][h]] = o[h]
            s_scr[h] = s_new[h]

    rb = lambda n: (n, 0)
    return pl.pallas_call(
        body, name="gdn_fwd", grid=(nch,),
        in_specs=[pl.BlockSpec((c, DN_WIDTH), rb)] * 5,
        out_specs=[pl.BlockSpec((c, DN_WIDTH), rb),
                   pl.BlockSpec((DN_HEADS, 1, DN_DIM, DN_DIM), lambda n: (0, n, 0, 0)),
                   pl.BlockSpec((DN_HEADS, 1, c, c), lambda n: (0, n, 0, 0))],
        out_shape=[jax.ShapeDtypeStruct((tp, DN_WIDTH), F32),
                   jax.ShapeDtypeStruct((DN_HEADS, nch, DN_DIM, DN_DIM), F32),
                   jax.ShapeDtypeStruct((DN_HEADS, nch, c, c), F32)],
        scratch_shapes=[pltpu.VMEM((DN_HEADS, DN_DIM, DN_DIM), F32)],
        compiler_params=pltpu.CompilerParams(dimension_semantics=("arbitrary",), vmem_limit_bytes=VMEM_LIMIT),
    )(q, k, v, g, beta)


def _gdn_bwd(q, k, v, g, beta, s_all, t_all, do):
    tp = q.shape[0]
    c = DN_CHUNK
    nch = tp // c

    def body(q_ref, k_ref, v_ref, g_ref, b_ref, s_ref, t_ref, do_ref,
             dq_ref, dk_ref, dv_ref, dg_ref, db_ref, ds_scr):
        @pl.when(pl.program_id(0) == 0)
        def _():
            ds_scr[...] = jnp.zeros_like(ds_scr)

        ones_cl = jnp.ones((c, LANE), F32)
        xs = _gdn_chunk_common(q_ref, k_ref, v_ref, g_ref, b_ref)
        upper = jnp.logical_not(xs["strict"]).astype(F32)

        def chain(h):
            x = {key: (val[h] if isinstance(val, list) else val) for key, val in xs.items()}
            sl = x["sls"]
            qs, kx, vx, beta_, gam, dm = x["q"], x["k"], x["v"], x["beta"], x["gam"], x["dm"]
            kb, vb, kbg, kd, ek = x["kb"], x["vb"], x["kbg"], x["kd"], x["ek"]
            t = t_ref[h, 0]
            s = s_ref[h, 0]
            dsn = ds_scr[h]
            dob = do_ref[:, sl]
            eg_last = jnp.exp(x["g_last"])
            u = _nn(t, vb)
            w = _nn(t, kbg)
            mqk = x["qk"] * dm
            qd = qs * gam
            dqd = _nt(dob, s)
            dkd_pre = _nn(kd, dsn)
            yield
            v_new = u - _nn(w, s)
            dv_new = _tn(mqk, dob) + dkd_pre
            dq = dqd * gam
            dgam = jnp.sum(dqd * qs, axis=1, keepdims=True)
            yield
            ds_new = _tn(qd, dob) + eg_last * dsn - _tn(w, dv_new)
            dmm = jnp.where(x["incl"], _nt(dob, v_new), 0.0)
            dkd = _nt(v_new, dsn)
            dw = -_nt(dv_new, s)
            dvb = _tn(t, dv_new)
            dt = _nt(dv_new, vb)
            yield
            dqk = dmm * dm
            e_mat = dmm * mqk
            dq = dq + _nn(dqk, kx)
            dk = _tn(dqk, qs) + dkd * ek
            e1 = jnp.sum(dkd * kd, axis=1, keepdims=True)
            dgc = -e1
            dg_last = jnp.sum(e1) + eg_last * jnp.sum(s * dsn)
            dt = dt + _nt(dw, kbg)
            dkbg = _tn(t, dw)
            yield
            tdt = _tn(t, dt, hp=True)
            yield
            da = jnp.where(x["strict"], -_nt(tdt, t, hp=True), 0.0)
            yield
            dkk = da * dm
            e_mat = e_mat + da * x["kk"] * dm
            dkb = _nn(dkk, kx) + dkbg * gam
            dk = dk + _tn(dkk, kb)
            dgam = dgam + jnp.sum(dkbg * kb, axis=1, keepdims=True)
            yield
            dk = dk + dkb * beta_
            dbeta = jnp.sum(dkb * kx, axis=1, keepdims=True) + jnp.sum(dvb * vx, axis=1, keepdims=True)
            dv = dvb * beta_
            dgc = dgc + jnp.sum(e_mat, axis=1, keepdims=True) + dgam * gam
            dgc = dgc - _tn(e_mat, ones_cl, hp=True)
            yield
            dg = _nn(upper, dgc, hp=True) + dg_last
            yield
            ds_scr[h] = ds_new
            dq_ref[:, sl] = dq * (1.0 / math.sqrt(DN_DIM))
            dk_ref[:, sl] = dk
            dv_ref[:, sl] = dv
            dg_ref[:, sl] = dg
            db_ref[:, sl] = jnp.broadcast_to(dbeta, (c, LANE))

        _interleave([chain(h) for h in range(DN_HEADS)])

    rb = lambda n: (nch - 1 - n, 0)
    hs = lambda n: (0, nch - 1 - n, 0, 0)
    return pl.pallas_call(
        body, name="gdn_bwd", grid=(nch,),
        in_specs=[pl.BlockSpec((c, DN_WIDTH), rb)] * 5
        + [pl.BlockSpec((DN_HEADS, 1, DN_DIM, DN_DIM), hs), pl.BlockSpec((DN_HEADS, 1, c, c), hs),
           pl.BlockSpec((c, DN_WIDTH), rb)],
        out_specs=[pl.BlockSpec((c, DN_WIDTH), rb)] * 5,
        out_shape=[jax.ShapeDtypeStruct((tp, DN_WIDTH), F32)] * 5,
        scratch_shapes=[pltpu.VMEM((DN_HEADS, DN_DIM, DN_DIM), F32)],
        compiler_params=pltpu.CompilerParams(dimension_semantics=("arbitrary",), vmem_limit_bytes=VMEM_LIMIT),
    )(q, k, v, g, beta, s_all, t_all, do)


def _silu_parts(x):
    s = _sigmoid(x)
    return x * s, s * (1.0 + x * (1.0 - s))


def _f_rms_cast(i, x, w):
    y, _ = _rms_fwd(x, w, x.shape[1])
    return (y,), ()


def _f_rms_bwd_add(i, x, dy, dres, w, *, mask_pad):
    dx, dwr = _rms_bwd(x, w, dy, x.shape[1])
    out = dres + dx
    if mask_pad:
        out = jnp.where(_row_ids(i, x.shape[0]) >= PAD, out, 0.0)
    return (out,), (_rowsum(dwr),)


def _f_lat_norm(i, ql, kvl, qw, kvw):
    return (_rms_fwd(ql, qw, Q_LORA)[0], _rms_fwd(kvl, kvw, KV_LORA)[0]), ()


def _f_lat_norm_bwd(i, ql, kvl, dqn, dkvn, qw, kvw):
    dq, dqw = _rms_bwd(ql, qw, dqn, Q_LORA)
    dk, dkw = _rms_bwd(kvl, kvw, dkvn, KV_LORA)
    return (dq, dk), (_rowsum(dqw), _rowsum(dkw))


def _rope(x, cos, sin_s):
    return x * cos + _swap_halves(x) * sin_s


def _rope_t(dy, cos, sin_s):
    return dy * cos + _swap_halves(dy * sin_s)


def _f_mla_qk(i, qf, kvf, kpe, cos, sin_s, qw, kw):
    qs, ks, vs = [], [], []
    for h in range(MLA_HEADS):
        qn, _ = _rms_fwd(qf[:, HP * h:HP * (h + 1)], qw, QK_HEAD)
        qs += [qn[:, :QK_NOPE], _rope(qn[:, QK_NOPE:], cos, sin_s)]
        kh = jnp.concatenate([kvf[:, HP * h:HP * h + QK_NOPE], kpe], axis=1)
        kn, _ = _rms_fwd(kh, kw, QK_HEAD)
        ks += [kn[:, :QK_NOPE], _rope(kn[:, QK_NOPE:], cos, sin_s)]
        vs.append(kvf[:, HP * h + QK_NOPE:HP * (h + 1)])
    return (jnp.concatenate(qs, axis=1), jnp.concatenate(ks, axis=1), jnp.concatenate(vs, axis=1)), ()


def _f_mla_qk_bwd(i, qf, kvf, kpe, cos, sin_s, dq, dk, dv, qw, kw):
    dqf, dkvf = [], []
    dkpe = None
    dqw = None
    dkw = None
    for h in range(MLA_HEADS):
        dqh = dq[:, HP * h:HP * (h + 1)]
        dqn = jnp.concatenate([dqh[:, :QK_NOPE], _rope_t(dqh[:, QK_NOPE:], cos, sin_s)], axis=1)
        dx, dwr = _rms_bwd(qf[:, HP * h:HP * (h + 1)], qw, dqn, QK_HEAD)
        dqf.append(dx)
        dqw = _rowsum(dwr) if dqw is None else dqw + _rowsum(dwr)
        dkh = dk[:, HP * h:HP * (h + 1)]
        dkn = jnp.concatenate([dkh[:, :QK_NOPE], _rope_t(dkh[:, QK_NOPE:], cos, sin_s)], axis=1)
        kh = jnp.concatenate([kvf[:, HP * h:HP * h + QK_NOPE], kpe], axis=1)
        dx, dwr = _rms_bwd(kh, kw, dkn, QK_HEAD)
        dkvf += [dx[:, :QK_NOPE], dv[:, V_HEAD * h:V_HEAD * (h + 1)]]
        dkpe = dx[:, QK_NOPE:] if dkpe is None else dkpe + dx[:, QK_NOPE:]
        dkw = _rowsum(dwr) if dkw is None else dkw + _rowsum(dwr)
    return (jnp.concatenate(dqf, axis=1), jnp.concatenate(dkvf, axis=1), dkpe), (dqw, dkw)


def _gdn_act(i, x, halo, w8):
    tm = x.shape[0]
    halo = jnp.where(i > 0, halo, 0.0)
    c = _conv_fwd(x, halo, w8, DN_CONV)
    act, dact = _silu_parts(c)
    return act, dact


def _f_gdn_prep(i, x, halo, ab, w8, alog, dtb, sel):
    tm = x.shape[0]
    act, _ = _gdn_act(i, x, halo, w8)
    outs = []
    for part in range(2):
        for h in range(DN_HEADS):
            t = act[:, DN_WIDTH * part + DN_DIM * h:DN_WIDTH * part + DN_DIM * (h + 1)]
            outs.append(t * lax.rsqrt(jnp.sum(t * t, axis=-1, keepdims=True) + EPS))
    q = jnp.concatenate(outs[:DN_HEADS], axis=1)
    k = jnp.concatenate(outs[DN_HEADS:], axis=1)
    v = act[:, 2 * DN_WIDTH:]
    abb = _nn(ab, sel, hp=True)
    valid = _row_ids(i, tm) >= PAD
    g = jnp.where(valid, -jnp.exp(alog) * _softplus(abb[:, :DN_WIDTH] + dtb), 0.0)
    beta = jnp.where(valid, _sigmoid(abb[:, DN_WIDTH:]), 0.0)
    return (q, k, v, g, beta), ()


def _f_gdn_prep_bwd(i, x, halo, ab, dq, dk, dv, dg, dbeta, w8, alog, dtb, sel, selpick):
    tm = x.shape[0]
    act, dact = _gdn_act(i, x, halo, w8)
    douts = []
    for part, dd in enumerate((dq, dk)):
        for h in range(DN_HEADS):
            t = act[:, DN_WIDTH * part + DN_DIM * h:DN_WIDTH * part + DN_DIM * (h + 1)]
            r = lax.rsqrt(jnp.sum(t * t, axis=-1, keepdims=True) + EPS)
            y = t * r
            dy = dd[:, DN_DIM * h:DN_DIM * (h + 1)]
            douts.append(r * (dy - y * jnp.sum(dy * y, axis=-1, keepdims=True)))
    douts.append(dv)
    dc = jnp.concatenate(douts, axis=1) * dact
    abb = _nn(ab, sel, hp=True)
    valid = _row_ids(i, tm) >= PAD
    pre = abb[:, :DN_WIDTH] + dtb
    ea = jnp.exp(alog)
    g = -ea * _softplus(pre)
    dg = jnp.where(valid, dg, 0.0)
    dbeta = jnp.where(valid, dbeta, 0.0)
    da = dg * (-ea) * _sigmoid(pre)
    beta = _sigmoid(abb[:, DN_WIDTH:])
    db = dbeta * beta * (1.0 - beta)
    dab = _nn(jnp.concatenate([da, db], axis=1), selpick, hp=True)
    return (dc, dab), (_rowsum(dg * g), _rowsum(da))


def _f_conv_bwd(i, dy, dy_next, x, x_prev, w8, *, width, nt):
    dy_next = jnp.where(i < nt - 1, dy_next, 0.0)
    x_prev = jnp.where(i > 0, x_prev, 0.0)
    return (_conv_bwd_x(dy, dy_next, w8, width),), (_conv_bwd_w(dy, x, x_prev, width),)


def _f_mix(i, o_mla, o_dn, z, w_mla, w_dn):
    tm = o_mla.shape[0]
    valid = _row_ids(i, tm) >= PAD
    outs = []
    for h in range(MLA_HEADS):
        y, _ = _rms_fwd(o_mla[:, V_HEAD * h:V_HEAD * (h + 1)], w_mla, V_HEAD)
        outs.append(jnp.where(valid, y, 0.0))
    for h in range(DN_HEADS):
        y, _ = _rms_fwd(o_dn[:, DN_DIM * h:DN_DIM * (h + 1)], w_dn, DN_DIM)
        outs.append(y * _silu_parts(z[:, DN_DIM * h:DN_DIM * (h + 1)])[0])
    return (jnp.concatenate(outs, axis=1),), ()


def _f_mix_bwd(i, o_mla, o_dn, z, dy_mla, dy_dn, w_mla, w_dn):
    tm = o_mla.shape[0]
    valid = _row_ids(i, tm) >= PAD
    d_mla, d_dn, d_z = [], [], []
    dw_mla = None
    dw_dn = None
    for h in range(MLA_HEADS):
        sl = slice(V_HEAD * h, V_HEAD * (h + 1))
        dx, dwr = _rms_bwd(o_mla[:, sl], w_mla, jnp.where(valid, dy_mla[:, sl], 0.0), V_HEAD)
        d_mla.append(dx)
        dw_mla = _rowsum(dwr) if dw_mla is None else dw_mla + _rowsum(dwr)
    for h in range(DN_HEADS):
        sl = slice(DN_DIM * h, DN_DIM * (h + 1))
        y, _ = _rms_fwd(o_dn[:, sl], w_dn, DN_DIM)
        sz, dsz = _silu_parts(z[:, sl])
        d_z.append(dy_dn[:, sl] * y * dsz)
        dx, dwr = _rms_bwd(o_dn[:, sl], w_dn, dy_dn[:, sl] * sz, DN_DIM)
        d_dn.append(dx)
        dw_dn = _rowsum(dwr) if dw_dn is None else dw_dn + _rowsum(dwr)
    return ((jnp.concatenate(d_mla, axis=1), jnp.concatenate(d_dn, axis=1), jnp.concatenate(d_z, axis=1)),
            (dw_mla, dw_dn))


def _f_ffn_act(i, gate_pre, halo, up, w8, b):
    halo = jnp.where(i > 0, halo, 0.0)
    gate = _conv_fwd(gate_pre, halo, w8, FFN_CONV) + b
    return (_silu_parts(gate)[0] * up,), ()


def _f_ffn_act_bwd(i, gate_pre, halo, up, dact, w8, b):
    halo = jnp.where(i > 0, halo, 0.0)
    gate = _conv_fwd(gate_pre, halo, w8, FFN_CONV) + b
    sg, dsg = _silu_parts(gate)
    dgate = dact * up * dsg
    return (dgate, dact * sg), (_rowsum(dgate),)


def _f_loss(i, h3, tgt):
    tm = h3.shape[0]
    diff = jnp.where(_row_ids(i, tm) >= ROW0, h3 - tgt, 0.0)
    part = 0.5 * jnp.sum(diff * diff) * (1.0 / D_MODEL)
    return (diff * (1.0 / D_MODEL),), (jnp.full((1, LANE), part, F32),)


def _local_step(h0, tgt, w):
    tp = h0.shape[0]
    nt = tp // TM
    bf = (D_MODEL, _MXU)
    u, = _rows("rms_in", _f_rms_cast, [_In(h0)], [w["attn_norm_w"]], [bf])
    proj = _mm("in_proj", u, w["w_in"], "nn")
    p_qkv = lambda kind="cur": _In(proj, 3 * DN_WIDTH, 0, kind)
    p_z = _In(proj, DN_WIDTH, C_Z // DN_WIDTH)
    p_ql = _In(proj, Q_LORA, C_QL // Q_LORA)
    p_kvl = _In(proj, KV_LORA, C_KVL // KV_LORA)
    p_kpe = _In(proj, LANE, C_KPE // LANE)
    p_ab = _In(proj, LANE, C_AB // LANE)
    cos, sin_s = _In(w["cos"]), _In(w["sin_s"])

    qn, kvn = _rows("mla_lat_norm", _f_lat_norm, [p_ql, p_kvl], [w["q_a_norm_w"], w["kv_a_norm_w"]],
                    [(Q_LORA, _MXU), (KV_LORA, _MXU)])
    qf = _mm("mla_q_b", qn, w["w_q_b"], "nn")
    kvf = _mm("mla_kv_b", kvn, w["w_kv_b"], "nn")
    qk_w = [w["q_norm_w"], w["k_norm_w"]]
    q, k, v = _rows("mla_qk", _f_mla_qk, [_In(qf), _In(kvf), p_kpe, cos, sin_s], qk_w,
                    [(MLA_HEADS * HP, _MXU), (MLA_HEADS * HP, _MXU), (MLA_HEADS * V_HEAD, _MXU)])
    o_mla = _attn_fwd(q, k, v)

    dn_w = [w["dn_conv_w"], w["alog_b"], w["dtb_b"], w["sel"]]
    gq, gk, gv, gg, gb = _rows("gdn_prep", _f_gdn_prep, [p_qkv(), p_qkv("prev"), p_ab], dn_w,
                               [(DN_WIDTH, F32)] * 5)
    o_dn, s_all, t_all = _gdn_fwd(gq, gk, gv, gg, gb)

    out_w = [w["mla_out_norm_w"], w["dn_out_norm_w"]]
    mixed, = _rows("mix", _f_mix, [_In(o_mla), _In(o_dn), p_z], out_w, [bf])
    h2 = _mm("out_proj", mixed, w["w_out"], "nn", resid=h0)

    hn, = _rows("rms_ffn", _f_rms_cast, [_In(h2)], [w["ffn_norm_w"]], [bf])
    gate_pre = _mm("ffn_gate", hn, w["w_gate"], "nn")
    up = _mm("ffn_up", hn, w["w_up"], "nn")
    ffn_w = [w["ffn_conv_w"], w["ffn_conv_b"]]
    act, = _rows("ffn_act", _f_ffn_act, [_In(gate_pre), _In(gate_pre, kind="prev"), _In(up)], ffn_w,
                 [(D_FF, _MXU)])
    h3 = _mm("ffn_down", act, w["w_down"], "nn", resid=h2)

    dh3, loss = _rows("loss", _f_loss, [_In(h3), _In(tgt)], [], [(D_MODEL, F32)], [(1, LANE)])

    g = {}
    dact = _mm("ffn_down_dx", dh3, w["w_down"], "nt")
    g["w_down"] = _mm("ffn_down_dw", act, dh3, "tn")
    dgate, dup, g["ffn_conv_b"] = _rows(
        "ffn_act_bwd", _f_ffn_act_bwd, [_In(gate_pre), _In(gate_pre, kind="prev"), _In(up), _In(dact)], ffn_w,
        [(D_FF, F32), (D_FF, _MXU)], [(1, D_FF)])
    dgate_pre, g["ffn_conv_w"] = _rows(
        "ffn_conv_bwd", functools.partial(_f_conv_bwd, width=FFN_CONV, nt=nt),
        [_In(dgate), _In(dgate, kind="next"), _In(gate_pre), _In(gate_pre, kind="prev")], [w["ffn_conv_w"]],
        [(D_FF, _MXU)], [(8, D_FF)])
    dhn = _mm("ffn_gate_dx", dgate_pre, w["w_gate"], "nt")
    dhn = _mm("ffn_up_dx", dup, w["w_up"], "nt", resid=dhn)
    g["w_gate"] = _mm("ffn_gate_dw", hn, dgate_pre, "tn")
    g["w_up"] = _mm("ffn_up_dw", hn, dup, "tn")
    dh2, g["ffn_norm_w"] = _rows(
        "rms_ffn_bwd", functools.partial(_f_rms_bwd_add, mask_pad=True), [_In(h2), _In(dhn), _In(dh3)],
        [w["ffn_norm_w"]], [(D_MODEL, F32)], [(1, D_MODEL)])

    dmixed = _mm("out_proj_dx", dh2, w["w_out"], "nt")
    g["w_out"] = _mm("out_proj_dw", mixed, dh2, "tn")
    half = MLA_HEADS * V_HEAD
    do_mla, do_dn, dz, g["mla_out_norm_w"], g["dn_out_norm_w"] = _rows(
        "mix_bwd", _f_mix_bwd, [_In(o_mla), _In(o_dn), p_z, _In(dmixed, half, 0), _In(dmixed, half, 1)], out_w,
        [(half, F32), (DN_WIDTH, F32), (DN_WIDTH, _MXU)], [(1, V_HEAD), (1, DN_DIM)])

    dq, dk, dv = _attn_bwd(q, k, v, do_mla)
    dqf, dkvf, dkpe, g["q_norm_w"], g["k_norm_w"] = _rows(
        "mla_qk_bwd", _f_mla_qk_bwd, [_In(qf), _In(kvf), p_kpe, cos, sin_s, _In(dq), _In(dk), _In(dv)], qk_w,
        [(MLA_HEADS * HP, _MXU), (MLA_HEADS * HP, _MXU), (LANE, _MXU)], [(1, HP), (1, HP)])
    dqn = _mm("mla_q_b_dx", dqf, w["w_q_b"], "nt")
    g["w_q_b"] = _mm("mla_q_b_dw", qn, dqf, "tn")
    dkvn = _mm("mla_kv_b_dx", dkvf, w["w_kv_b"], "nt")
    g["w_kv_b"] = _mm("mla_kv_b_dw", kvn, dkvf, "tn")
    dql, dkvl, g["q_a_norm_w"], g["kv_a_norm_w"] = _rows(
        "mla_lat_norm_bwd", _f_lat_norm_bwd, [p_ql, p_kvl, _In(dqn), _In(dkvn)],
        [w["q_a_norm_w"], w["kv_a_norm_w"]], [(Q_LORA, _MXU), (KV_LORA, _MXU)], [(1, Q_LORA), (1, KV_LORA)])

    dgq, dgk, dgv, dgg, dgb = _gdn_bwd(gq, gk, gv, gg, gb, s_all, t_all, do_dn)
    dc, dab, g["alog_b"], g["dtb_b"] = _rows(
        "gdn_prep_bwd", _f_gdn_prep_bwd,
        [p_qkv(), p_qkv("prev"), p_ab, _In(dgq), _In(dgk), _In(dgv), _In(dgg), _In(dgb)], dn_w + [w["selpick"]],
        [(3 * DN_WIDTH, F32), (LANE, _MXU)], [(1, DN_WIDTH), (1, DN_WIDTH)])
    dqkv, g["dn_conv_w"] = _rows(
        "gdn_conv_bwd", functools.partial(_f_conv_bwd, width=DN_CONV, nt=nt),
        [_In(dc), _In(dc, kind="next"), p_qkv(), p_qkv("prev")], [w["dn_conv_w"]],
        [(3 * DN_WIDTH, _MXU)], [(8, 3 * DN_WIDTH)])

    dproj = jnp.concatenate([dqkv, dz, dql, dkvl, dkpe, dab], axis=1)
    du = _mm("in_proj_dx", dproj, w["w_in"], "nt")
    g["w_in"] = _mm("in_proj_dw", u, dproj, "tn")
    dh0, g["attn_norm_w"] = _rows(
        "rms_in_bwd", functools.partial(_f_rms_bwd_add, mask_pad=False), [_In(h0), _In(du), _In(dh2)],
        [w["attn_norm_w"]], [(D_MODEL, F32)], [(1, D_MODEL)])
    return loss, dh0, g


def _w_in_to_padded(w):
    c1, c2, c3 = Q_LORA, Q_LORA + KV_LORA, Q_LORA + KV_LORA + QK_ROPE
    c4 = c3 + 3 * DN_WIDTH
    c5 = c4 + DN_WIDTH
    z = lambda n: jnp.zeros((w.shape[0], n), w.dtype)
    return jnp.concatenate([w[:, c3:c4], w[:, c4:c5], w[:, :c1], w[:, c1:c2], w[:, c2:c3], z(LANE - QK_ROPE),
                            w[:, c5:], z(LANE - 2 * DN_HEADS)], axis=1)


def _w_in_from_padded(g):
    return jnp.concatenate([g[:, C_QL:C_QL + Q_LORA], g[:, C_KVL:C_KVL + KV_LORA], g[:, C_KPE:C_KPE + QK_ROPE],
                            g[:, :C_Z + DN_WIDTH], g[:, C_AB:C_AB + 2 * DN_HEADS]], axis=1)


def _w_q_b_to_padded(w):
    r = w.shape[0]
    w = w.reshape(r, MLA_HEADS, QK_HEAD)
    return jnp.pad(w, ((0, 0), (0, 0), (0, HP - QK_HEAD))).reshape(r, MLA_HEADS * HP)


def _w_q_b_from_padded(g):
    r = g.shape[0]
    return g.reshape(r, MLA_HEADS, HP)[:, :, :QK_HEAD].reshape(r, MLA_HEADS * QK_HEAD)


def _pad_rows8(w):
    return jnp.pad(w, ((0, 8 - w.shape[0]), (0, 0)))


def _prepare(full, tp):
    w = {}
    mx = lambda a: a.astype(_MXU)
    w["attn_norm_w"] = full["attn_norm_w"]
    w["w_in"] = mx(_w_in_to_padded(full["w_in"]))
    w["q_a_norm_w"] = full["q_a_norm_w"]
    w["kv_a_norm_w"] = full["kv_a_norm_w"]
    w["w_q_b"] = mx(_w_q_b_to_padded(full["w_q_b"]))
    w["w_kv_b"] = mx(full["w_kv_b"])
    w["q_norm_w"] = jnp.pad(full["q_norm_w"], ((0, 0), (0, HP - QK_HEAD)))
    w["k_norm_w"] = jnp.pad(full["k_norm_w"], ((0, 0), (0, HP - QK_HEAD)))
    w["mla_out_norm_w"] = full["mla_out_norm_w"]
    w["dn_out_norm_w"] = full["dn_out_norm_w"]
    w["dn_conv_w"] = _pad_rows8(full["dn_conv_w"])
    w["alog_b"] = jnp.repeat(full["dn_A_log"], DN_DIM, axis=1)
    w["dtb_b"] = jnp.repeat(full["dn_dt_bias"], DN_DIM, axis=1)
    w["w_out"] = mx(full["w_out"])
    w["ffn_norm_w"] = full["ffn_norm_w"]
    w["w_gate"] = mx(full["w_gate"])
    w["w_up"] = mx(full["w_up"])
    w["ffn_conv_w"] = _pad_rows8(full["ffn_conv_w"])
    w["ffn_conv_b"] = full["ffn_conv_b"]
    w["w_down"] = mx(full["w_down"])
    half = QK_ROPE // 2
    inv = ROPE_THETA ** (-jnp.arange(half, dtype=F32) / half)
    ang = (jnp.arange(tp, dtype=jnp.int32) - PAD).astype(F32)[:, None] * inv[None, :]
    zc = jnp.zeros((tp, LANE - QK_ROPE), F32)
    w["cos"] = jnp.concatenate([jnp.cos(ang), jnp.cos(ang), zc], axis=1)
    w["sin_s"] = jnp.concatenate([-jnp.sin(ang), jnp.sin(ang), zc], axis=1)
    lane = jnp.arange(2 * DN_WIDTH)[None, :]
    src = jnp.arange(LANE)[:, None]
    w["sel"] = ((lane // DN_DIM) == src).astype(F32)
    w["selpick"] = ((src.T == (lane.T // DN_DIM)) & (lane.T % DN_DIM == 0)).astype(F32)
    return w


def _grads_to_natural(g):
    n = dict(g)
    n["w_in"] = _w_in_from_padded(g["w_in"])
    n["w_q_b"] = _w_q_b_from_padded(g["w_q_b"])
    n["q_norm_w"] = g["q_norm_w"][:, :QK_HEAD]
    n["k_norm_w"] = g["k_norm_w"][:, :QK_HEAD]
    n["dn_conv_w"] = g["dn_conv_w"][:DN_CONV]
    n["ffn_conv_w"] = g["ffn_conv_w"][:FFN_CONV]
    n["dn_A_log"] = n.pop("alog_b")[:, ::DN_DIM]
    n["dn_dt_bias"] = n.pop("dtb_b")[:, ::DN_DIM]
    return n


_MESH = pl.DeviceIdType.MESH
_ANY = pl.BlockSpec(memory_space=pl.ANY)
_CHIP_FLIPS = ((1, 0), (0, 1), (1, 1))


def _me():
    return lax.axis_index("x"), lax.axis_index("y"), lax.axis_index("c")


def _all_gather(name, blk):
    def body(x_ref, out_ref, send_sems, recv_sems, local_sem):
        x, y, c = _me()
        me, sib = (x, y, c), (x, y, 1 - c)
        chips = [(x ^ fx, y ^ fy) for fx, fy in _CHIP_FLIPS]

        def slot(p):
            return out_ref.at[4 * p[0] + 2 * p[1] + p[2]]

        def copy(k, block, to, src=None):
            return pltpu.make_async_remote_copy(
                src_ref=slot(block) if src is None else src, dst_ref=slot(block),
                send_sem=send_sems.at[k], recv_sem=recv_sems.at[k], device_id=to, device_id_type=_MESH)

        mine = pltpu.make_async_copy(x_ref, slot(me), local_sem)
        mine.start()
        first = [copy(0, me, sib, src=x_ref)]
        first += [copy(1 + j, me, (*chip, c), src=x_ref) for j, chip in enumerate(chips)]
        for cp in first:
            cp.start()
        passed = [copy(4 + j, (*chip, c), sib) for j, chip in enumerate(chips)]
        for j, chip in enumerate(chips):
            copy(1 + j, (*chip, c), me).wait_recv()
            passed[j].start()
        copy(0, sib, me).wait_recv()
        for j, chip in enumerate(chips):
            copy(4 + j, (*chip, 1 - c), me).wait_recv()
        for cp in first + passed:
            cp.wait_send()
        mine.wait()

    return pl.pallas_call(
        body, name=name, in_specs=[_ANY], out_specs=_ANY,
        out_shape=jax.ShapeDtypeStruct((N_DEV,) + blk.shape, blk.dtype),
        scratch_shapes=[pltpu.SemaphoreType.DMA((7,)), pltpu.SemaphoreType.DMA((7,)), pltpu.SemaphoreType.DMA],
    )(blk)


def _rs_sibling(name, gb):
    def body(g_ref, out_ref, send_sems, recv_sems):
        x, y, c = _me()
        cps = []
        for j in range(4):
            cp = pltpu.make_async_remote_copy(
                src_ref=g_ref.at[2 * j + (1 - c)], dst_ref=out_ref.at[j], send_sem=send_sems.at[j],
                recv_sem=recv_sems.at[j], device_id=(x, y, 1 - c), device_id_type=_MESH)
            cp.start()
            cps.append(cp)
        for cp in cps:
            cp.wait()

    return pl.pallas_call(
        body, name=name, in_specs=[_ANY], out_specs=_ANY,
        out_shape=jax.ShapeDtypeStruct((4,) + gb.shape[1:], gb.dtype),
        scratch_shapes=[pltpu.SemaphoreType.DMA((4,)), pltpu.SemaphoreType.DMA((4,))],
    )(gb)


def _rs_chips(name, s1):
    def body(s_ref, out_ref, send_sems, recv_sems):
        x, y, c = _me()
        cps = []
        for k, (fx, fy) in enumerate(_CHIP_FLIPS):
            px, py = x ^ fx, y ^ fy
            cp = pltpu.make_async_remote_copy(
                src_ref=s_ref.at[2 * px + py], dst_ref=out_ref.at[k], send_sem=send_sems.at[k],
                recv_sem=recv_sems.at[k], device_id=(px, py, c), device_id_type=_MESH)
            cp.start()
            cps.append(cp)
        for cp in cps:
            cp.wait()

    return pl.pallas_call(
        body, name=name, in_specs=[_ANY], out_specs=_ANY,
        out_shape=jax.ShapeDtypeStruct((3,) + s1.shape[1:], s1.dtype),
        scratch_shapes=[pltpu.SemaphoreType.DMA((3,)), pltpu.SemaphoreType.DMA((3,))],
    )(s1)


def _row_tile(r):
    return _pick(r, 512, 8)


def _pair_sum(name, gb, recv):
    _, r, cols = gb.shape
    tm = _row_tile(r)
    c = lax.axis_index("c").astype(jnp.int32).reshape(1)

    def body(c_ref, a_ref, b_ref, o_ref, ob_ref):
        s = a_ref[...] + b_ref[...]
        o_ref[...] = s
        ob_ref[...] = s.astype(BF16)

    blk = pl.BlockSpec((1, tm, cols), lambda j, i, c_ref: (j, i, 0))
    return pl.pallas_call(
        body, name=name,
        grid_spec=pltpu.PrefetchScalarGridSpec(
            num_scalar_prefetch=1, grid=(4, r // tm),
            in_specs=[pl.BlockSpec((1, tm, cols), lambda j, i, c_ref: (2 * j + c_ref[0], i, 0)), blk],
            out_specs=[blk, blk]),
        out_shape=[jax.ShapeDtypeStruct((4, r, cols), F32), jax.ShapeDtypeStruct((4, r, cols), BF16)],
        compiler_params=pltpu.CompilerParams(dimension_semantics=("parallel", "parallel")),
    )(c, gb, recv)


def _adam(name, parts, w, m, v):
    r, cols = w.shape
    tm = _row_tile(r)
    idx = jnp.stack([jnp.asarray(s, jnp.int32) for _, s in parts])
    n = len(parts)

    def body(idx_ref, *refs):
        g = refs[0][0].astype(F32)
        for p_ref in refs[1:n]:
            g = g + p_ref[0].astype(F32)
        w_ref, m_ref, v_ref, g_out, d_out, m_out, v_out = refs[n:]
        m_new = ADAM_B1 * m_ref[...] + (1.0 - ADAM_B1) * g
        v_new = ADAM_B2 * v_ref[...] + (1.0 - ADAM_B2) * (g * g)
        m_hat = m_new / (1.0 - ADAM_B1 ** ADAM_STEP)
        v_hat = v_new / (1.0 - ADAM_B2 ** ADAM_STEP)
        g_out[...] = g
        d_out[...] = -ADAM_LR * (m_hat / (jnp.sqrt(v_hat) + ADAM_EPS) + ADAM_WD * w_ref[...])
        m_out[...] = m_new
        v_out[...] = v_new

    part_specs = [pl.BlockSpec((1, tm, cols), lambda i, idx_ref, p=p: (idx_ref[p], i, 0)) for p in range(n)]
    flat = pl.BlockSpec((tm, cols), lambda i, idx_ref: (i, 0))
    return pl.pallas_call(
        body, name=name,
        grid_spec=pltpu.PrefetchScalarGridSpec(
            num_scalar_prefetch=1, grid=(r // tm,), in_specs=part_specs + [flat] * 3, out_specs=[flat] * 4),
        out_shape=[jax.ShapeDtypeStruct((r, cols), F32)] * 4,
        compiler_params=pltpu.CompilerParams(dimension_semantics=("parallel",)),
    )(idx, *[a for a, _ in parts], w, m, v)


def _all_gather_many(name, blks):
    n = len(blks)

    def body(*refs):
        x_refs, out_refs = refs[:n], refs[n:2 * n]
        send_sems, recv_sems, local_sems = refs[2 * n:]
        x, y, c = _me()
        me, sib = (x, y, c), (x, y, 1 - c)
        chips = [(x ^ fx, y ^ fy) for fx, fy in _CHIP_FLIPS]

        def slot(a, p):
            return out_refs[a].at[4 * p[0] + 2 * p[1] + p[2]]

        def copy(a, k, block, to, src=None):
            return pltpu.make_async_remote_copy(
                src_ref=slot(a, block) if src is None else src, dst_ref=slot(a, block),
                send_sem=send_sems.at[7 * a + k], recv_sem=recv_sems.at[7 * a + k], device_id=to,
                device_id_type=_MESH)

        mine = [pltpu.make_async_copy(x_refs[a], slot(a, me), local_sems.at[a]) for a in range(n)]
        first = []
        for a in range(n):
            mine[a].start()
            first.append(copy(a, 0, me, sib, src=x_refs[a]))
            first += [copy(a, 1 + j, me, (*chip, c), src=x_refs[a]) for j, chip in enumerate(chips)]
        for cp in first:
            cp.start()
        passed = []
        for j, chip in enumerate(chips):
            for a in range(n):
                copy(a, 1 + j, (*chip, c), me).wait_recv()
                cp = copy(a, 4 + j, (*chip, c), sib)
                cp.start()
                passed.append(cp)
        for a in range(n):
            copy(a, 0, sib, me).wait_recv()
            for j, chip in enumerate(chips):
                copy(a, 4 + j, (*chip, 1 - c), me).wait_recv()
        for cp in first + passed:
            cp.wait_send()
        for cp in mine:
            cp.wait()

    return pl.pallas_call(
        body, name=name, in_specs=[_ANY] * n, out_specs=[_ANY] * n,
        out_shape=[jax.ShapeDtypeStruct((N_DEV,) + b.shape, b.dtype) for b in blks],
        scratch_shapes=[pltpu.SemaphoreType.DMA((7 * n,)), pltpu.SemaphoreType.DMA((7 * n,)),
                        pltpu.SemaphoreType.DMA((n,))],
    )(*blks)


def _rs_sibling_many(name, gbs):
    n = len(gbs)

    def body(*refs):
        g_refs, out_refs = refs[:n], refs[n:2 * n]
        send_sems, recv_sems = refs[2 * n:]
        x, y, c = _me()
        cps = []
        for a in range(n):
            for j in range(4):
                cp = pltpu.make_async_remote_copy(
                    src_ref=g_refs[a].at[2 * j + (1 - c)], dst_ref=out_refs[a].at[j],
                    send_sem=send_sems.at[4 * a + j], recv_sem=recv_sems.at[4 * a + j],
                    device_id=(x, y, 1 - c), device_id_type=_MESH)
                cp.start()
                cps.append(cp)
        for cp in cps:
            cp.wait()

    return pl.pallas_call(
        body, name=name, in_specs=[_ANY] * n, out_specs=[_ANY] * n,
        out_shape=[jax.ShapeDtypeStruct((4,) + g.shape[1:], g.dtype) for g in gbs],
        scratch_shapes=[pltpu.SemaphoreType.DMA((4 * n,)), pltpu.SemaphoreType.DMA((4 * n,))],
    )(*gbs)


def _rs_chips_many(name, s1s):
    n = len(s1s)

    def body(*refs):
        s_refs, out_refs = refs[:n], refs[n:2 * n]
        send_sems, recv_sems = refs[2 * n:]
        x, y, c = _me()
        cps = []
        for a in range(n):
            for k, (fx, fy) in enumerate(_CHIP_FLIPS):
                px, py = x ^ fx, y ^ fy
                cp = pltpu.make_async_remote_copy(
                    src_ref=s_refs[a].at[2 * px + py], dst_ref=out_refs[a].at[k],
                    send_sem=send_sems.at[3 * a + k], recv_sem=recv_sems.at[3 * a + k],
                    device_id=(px, py, c), device_id_type=_MESH)
                cp.start()
                cps.append(cp)
        for cp in cps:
            cp.wait()

    return pl.pallas_call(
        body, name=name, in_specs=[_ANY] * n, out_specs=[_ANY] * n,
        out_shape=[jax.ShapeDtypeStruct((3,) + s.shape[1:], s.dtype) for s in s1s],
        scratch_shapes=[pltpu.SemaphoreType.DMA((3 * n,)), pltpu.SemaphoreType.DMA((3 * n,))],
    )(*s1s)


_SHARDED = (
    ("meta_tokens", 1, (N_META, D_MODEL)),
    ("w_in", 1, (D_MODEL, IN_COLS)),
    ("w_q_b", 1, (Q_LORA, MLA_HEADS * QK_HEAD)),
    ("w_kv_b", 1, (KV_LORA, MLA_HEADS * (QK_NOPE + V_HEAD))),
    ("dn_conv_w", 1, (DN_CONV, 3 * DN_WIDTH)),
    ("w_out", 0, (2 * DN_WIDTH, D_MODEL)),
    ("w_gate", 1, (D_MODEL, D_FF)),
    ("w_up", 1, (D_MODEL, D_FF)),
    ("ffn_conv_w", 1, (FFN_CONV, D_FF)),
    ("w_down", 0, (D_FF, D_MODEL)),
)
_MXU_GATHERED = ("w_in", "w_q_b", "w_kv_b", "w_out", "w_gate", "w_up", "w_down")
_F32_GATHERED = ("meta_tokens", "dn_conv_w", "ffn_conv_w")
_REPLICATED = (
    ("attn_norm_w", D_MODEL), ("q_a_norm_w", Q_LORA), ("kv_a_norm_w", KV_LORA), ("q_norm_w", QK_HEAD),
    ("k_norm_w", QK_HEAD), ("mla_out_norm_w", V_HEAD), ("dn_A_log", DN_HEADS), ("dn_dt_bias", DN_HEADS),
    ("dn_out_norm_w", DN_DIM), ("ffn_norm_w", D_MODEL), ("ffn_conv_b", D_FF),
)
_PACK_COLS = 1024
_PACK_ROW_MULT = 320
_SMALL_SHAPE = (8, 768)
_SMALL_BLOCK = (8, 512)


def _local_shape(dim, shape):
    return (shape[0] // N_DEV, shape[1]) if dim == 0 else (shape[0], shape[1] // N_DEV)


def _pack_rows(n, mult):
    rows = -(-n // _PACK_COLS)
    return -(-rows // mult) * mult


def _pack(flats, mult, axis=0):
    cat = jnp.concatenate(flats, axis=-1)
    n = cat.shape[-1]
    r = _pack_rows(n, mult)
    pad = [(0, 0)] * (cat.ndim - 1) + [(0, r * _PACK_COLS - n)]
    return jnp.pad(cat, pad).reshape(cat.shape[:-1] + (r, _PACK_COLS))


def _to_blocks(full, dim):
    r, c = full.shape
    if dim == 0:
        return full.reshape(N_DEV, (r // N_DEV) * c)
    return full.reshape(r, N_DEV, c // N_DEV).transpose(1, 0, 2).reshape(N_DEV, r * (c // N_DEV))


def _from_blocks(blocks, dim, shape):
    r, c = shape
    if dim == 0:
        return blocks.reshape(r, c)
    return blocks.reshape(N_DEV, r, c // N_DEV).transpose(1, 0, 2).reshape(r, c)


def _split(flat, sizes):
    out, o = [], 0
    for s in sizes:
        out.append(flat[..., o:o + s])
        o += s
    return out


def _gather_weights(local, names, dtype, mult):
    specs = [s for s in _SHARDED if s[0] in names]
    pack = _pack([local[n].astype(dtype).reshape(-1) for n, _, _ in specs], mult)
    got = _all_gather("gather_" + "_".join(n[:5] for n in names[:2]), pack)
    flat = got.reshape(N_DEV, -1)
    sizes = [math.prod(_local_shape(d, s)) for _, d, s in specs]
    return {n: _from_blocks(p, d, s) for (n, d, s), p in zip(specs, _split(flat, sizes))}


def kernel(x, meta_tokens, attn_norm_w, w_in, q_a_norm_w, w_q_b, kv_a_norm_w, w_kv_b, q_norm_w, k_norm_w, mla_out_norm_w, dn_conv_w, dn_A_log, dn_dt_bias, dn_out_norm_w, w_out, ffn_norm_w, w_gate, w_up, ffn_conv_w, ffn_conv_b, w_down, loss_target, m_meta_tokens, m_attn_norm_w, m_w_in, m_q_a_norm_w, m_w_q_b, m_kv_a_norm_w, m_w_kv_b, m_q_norm_w, m_k_norm_w, m_mla_out_norm_w, m_dn_conv_w, m_dn_A_log, m_dn_dt_bias, m_dn_out_norm_w, m_w_out, m_ffn_norm_w, m_w_gate, m_w_up, m_ffn_conv_w, m_ffn_conv_b, m_w_down, v_meta_tokens, v_attn_norm_w, v_w_in, v_q_a_norm_w, v_w_q_b, v_kv_a_norm_w, v_w_kv_b, v_q_norm_w, v_k_norm_w, v_mla_out_norm_w, v_dn_conv_w, v_dn_A_log, v_dn_dt_bias, v_dn_out_norm_w, v_w_out, v_ffn_norm_w, v_w_gate, v_w_up, v_ffn_conv_w, v_ffn_conv_b, v_w_down):
    names = [n for n, _, _ in _SHARDED] + [n for n, _ in _REPLICATED]
    given = dict(locals())
    two_d = lambda a: a.reshape(a.shape[-2:])
    wl = {n: two_d(given[n]) for n in names}
    ml = {n: two_d(given["m_" + n]) for n in names}
    vl = {n: two_d(given["v_" + n]) for n in names}
    out_shapes = {n: given[n].shape for n in names}

    spec = {n: (d, s) for n, d, s in _SHARDED}
    small_sizes = [math.prod(_local_shape(*spec[n])) for n in _F32_GATHERED]

    def small_block(d):
        cat = jnp.concatenate([d[n].reshape(d[n].shape[:-2] + (-1,)) for n in _F32_GATHERED], axis=-1)
        pad = [(0, 0)] * (cat.ndim - 1) + [(0, math.prod(_SMALL_BLOCK) - cat.shape[-1])]
        return jnp.pad(cat, pad).reshape(cat.shape[:-1] + _SMALL_BLOCK)

    got = _all_gather_many("gather_weights", [wl[n].astype(_MXU) for n in _MXU_GATHERED] + [small_block(wl)])
    full = dict(wl)
    for n, blocks in zip(_MXU_GATHERED, got):
        d, s = spec[n]
        full[n] = blocks.reshape(s) if d == 0 else blocks.transpose(1, 0, 2).reshape(s)
    for n, p in zip(_F32_GATHERED, _split(got[-1].reshape(N_DEV, -1), small_sizes)):
        full[n] = _from_blocks(p, *spec[n])

    seq = x.shape[1]
    tp = ROW0 + seq
    h0 = jnp.concatenate([jnp.zeros((PAD, D_MODEL), F32), full["meta_tokens"], x[0]], axis=0)
    tgt = jnp.concatenate([jnp.zeros((ROW0, D_MODEL), F32), loss_target[0]], axis=0)
    loss, dh0, g = _local_step(h0, tgt, _prepare(full, tp))
    g = _grads_to_natural(g)
    g["meta_tokens"] = dh0[PAD:ROW0]
    grad_x = dh0[ROW0:][None]

    def dest_blocks(n):
        d, s = spec[n]
        r, c = _local_shape(d, s)
        return g[n].reshape(N_DEV, r, c) if d == 0 else g[n].reshape(r, N_DEV, c).transpose(1, 0, 2)

    gbs = {n: dest_blocks(n) for n, _, _ in _SHARDED}
    groups = list(_MXU_GATHERED) + ["small"]
    gb = [gbs[n] for n in _MXU_GATHERED] + [small_block(gbs)]
    from_sib = _rs_sibling_many("rs_sibling", gb)
    sums = [_pair_sum("rs_pair_sum_" + n, a, b) for n, a, b in zip(groups, gb, from_sib)]
    from_chips = _rs_chips_many("rs_chips", [sb for _, sb in sums])
    my_chip = 2 * lax.axis_index("x") + lax.axis_index("y")
    big = [{}, {}, {}, {}]
    for n, (s1, _), fc in zip(groups, sums, from_chips):
        loc = (lambda d: small_block(d)) if n == "small" else (lambda d, n=n: d[n])
        res = _adam("adam_" + n, [(s1, my_chip), (fc, 0), (fc, 1), (fc, 2)], loc(wl), loc(ml), loc(vl))
        for kind, a in enumerate(res):
            if n == "small":
                big[kind].update(zip(_F32_GATHERED, _split(a.reshape(-1), small_sizes)))
            else:
                big[kind][n] = a

    def small(d, extra):
        cat = jnp.concatenate([d[n].reshape(-1) for n, _ in _REPLICATED] + [extra])
        return jnp.pad(cat, (0, math.prod(_SMALL_SHAPE) - cat.shape[0])).reshape(_SMALL_SHAPE)

    zero1 = jnp.zeros((1,), F32)
    parts = _all_gather("gather_small_grads", small(g, loss[0, :1]))
    sm = _adam("adam_replicated", [(parts, d) for d in range(N_DEV)], small(wl, zero1), small(ml, zero1),
               small(vl, zero1))
    rsizes = [n for _, n in _REPLICATED] + [1]
    sm = [dict(zip([n for n, _ in _REPLICATED] + ["loss"], _split(a.reshape(-1), rsizes))) for a in sm]

    outs = [sm[0]["loss"].reshape(()), grad_x]
    for kind in range(4):
        for n in ("meta_tokens", "attn_norm_w", "w_in", "q_a_norm_w", "w_q_b", "kv_a_norm_w", "w_kv_b", "q_norm_w",
                  "k_norm_w", "mla_out_norm_w", "dn_conv_w", "dn_A_log", "dn_dt_bias", "dn_out_norm_w", "w_out",
                  "ffn_norm_w", "w_gate", "w_up", "ffn_conv_w", "ffn_conv_b", "w_down"):
            src = big[kind] if n in big[kind] else sm[kind]
            outs.append(src[n].reshape(out_shapes[n]))
    return tuple(outs)
```

```python
import functools
import math

import jax
import jax.numpy as jnp
from jax import lax
from jax.experimental import pallas as pl
from jax.experimental.pallas import tpu as pltpu

F32 = jnp.float32
BF16 = jnp.bfloat16
_MXU = jnp.bfloat16
_HI = lax.Precision.HIGHEST

D_MODEL = 1024
N_META = 16
PAD = 112
ROW0 = PAD + N_META
MLA_HEADS = 4
QK_NOPE = 128
QK_ROPE = 64
QK_HEAD = QK_NOPE + QK_ROPE
V_HEAD = 128
Q_LORA = 256
KV_LORA = 256
ROPE_THETA = 10000.0
DN_HEADS = 4
DN_DIM = 128
DN_WIDTH = DN_HEADS * DN_DIM
DN_CONV = 4
DN_CHUNK = 64
D_FF = 2816
FFN_CONV = 3
EPS = 1e-6
HP = 256
C_QKV = 0
C_Z = 1536
C_QL = 2048
C_KVL = 2304
C_KPE = 2560
C_AB = 2688
IN_P = 2816
IN_COLS = 2632

ADAM_LR = 0.001
ADAM_B1 = 0.9
ADAM_B2 = 0.999
ADAM_EPS = 1e-08
ADAM_WD = 0.01
ADAM_STEP = 10

N_DEV = 8
TM = 128
LANE = 128
VMEM_LIMIT = 56 * 1024 * 1024
NEG = -1e30


def _dot(a, b, dims, hp=False):
    if hp:
        return lax.dot_general(a.astype(F32), b.astype(F32), (dims, ((), ())),
                               precision=_HI, preferred_element_type=F32)
    return lax.dot_general(a.astype(_MXU), b.astype(_MXU), (dims, ((), ())),
                           preferred_element_type=F32)


def _nn(a, b, hp=False):
    return _dot(a, b, ((1,), (0,)), hp)


def _nt(a, b, hp=False):
    return _dot(a, b, ((1,), (1,)), hp)


def _tn(a, b, hp=False):
    return _dot(a, b, ((0,), (0,)), hp)


def _sigmoid(x):
    return 1.0 / (1.0 + jnp.exp(-x))


def _rms_fwd(x, w, n):
    r = lax.rsqrt(jnp.sum(x * x, axis=-1, keepdims=True) * (1.0 / n) + EPS)
    return x * r * w, r


def _rms_bwd(x, w, dy, n):
    r = lax.rsqrt(jnp.sum(x * x, axis=-1, keepdims=True) * (1.0 / n) + EPS)
    xh = x * r
    gy = dy * w
    dx = r * (gy - xh * (jnp.sum(gy * xh, axis=-1, keepdims=True) * (1.0 / n)))
    return dx, dy * xh


def _rowsum(x):
    return jnp.sum(x, axis=0, keepdims=True)


def _row_ids(i, tm):
    return i * tm + lax.broadcasted_iota(jnp.int32, (tm, 1), 0)


def _shift_down(ext, s, tm):
    if s == 0:
        return ext[8:8 + tm]
    return pltpu.roll(ext, s, 0)[8:8 + tm]


def _shift_up(ext, s, tm):
    if s == 0:
        return ext[0:tm]
    return pltpu.roll(ext, tm + 8 - s, 0)[0:tm]


def _conv_fwd(x, halo_prev, w, width):
    tm = x.shape[0]
    ext = jnp.concatenate([halo_prev, x], axis=0)
    y = None
    for j in range(width):
        t = w[j:j + 1, :] * _shift_down(ext, width - 1 - j, tm)
        y = t if y is None else y + t
    return y


def _conv_bwd_x(dy, halo_next, w, width):
    tm = dy.shape[0]
    ext = jnp.concatenate([dy, halo_next], axis=0)
    dx = None
    for j in range(width):
        t = w[j:j + 1, :] * _shift_up(ext, width - 1 - j, tm)
        dx = t if dx is None else dx + t
    return dx


def _conv_bwd_w(dy, x, halo_prev, width):
    tm = dy.shape[0]
    ext = jnp.concatenate([halo_prev, x], axis=0)
    rows = [_rowsum(dy * _shift_down(ext, width - 1 - j, tm)) for j in range(width)]
    rows += [jnp.zeros_like(rows[0])] * (8 - width)
    return jnp.concatenate(rows, axis=0)


def _softplus(x):
    e = jnp.exp(-jnp.abs(x))
    u = 1.0 + e
    l1p = jnp.where(u == 1.0, e, jnp.log(u) * e / jnp.where(u == 1.0, 1.0, u - 1.0))
    return jnp.maximum(x, 0.0) + l1p


def _swap_halves(x):
    lane = lax.broadcasted_iota(jnp.int32, x.shape, 1)
    return jnp.where(lane < 32, pltpu.roll(x, 96, 1), jnp.where(lane < 64, pltpu.roll(x, 32, 1), 0.0))


class _In:
    def __init__(self, arr, width=None, cb=0, kind="cur"):
        self.arr, self.kind = arr, kind
        self.width = arr.shape[1] if width is None else width
        self.cb = cb


def _rows(name, fn, tiled, full, outs, accs=(), tm=TM):
    tp = tiled[0].arr.shape[0]
    nt = tp // tm
    r8 = tm // 8
    n_in = len(tiled) + len(full)
    n_out = len(outs)

    def body(*refs):
        i = pl.program_id(0)
        vals = [r[...] for r in refs[:n_in]]
        o_t, o_a = fn(i, *vals)
        for r, v in zip(refs[n_in:n_in + n_out], o_t):
            r[...] = v.astype(r.dtype)
        for r, v in zip(refs[n_in + n_out:], o_a):
            @pl.when(i == 0)
            def _():
                r[...] = v

            @pl.when(i > 0)
            def _():
                r[...] += v

    def spec(t):
        if t.kind == "cur":
            return pl.BlockSpec((tm, t.width), lambda i, cb=t.cb: (i, cb))
        if t.kind == "prev":
            return pl.BlockSpec((8, t.width), lambda i, cb=t.cb: (jnp.maximum(i * r8 - 1, 0), cb))
        return pl.BlockSpec((8, t.width), lambda i, cb=t.cb: (jnp.minimum((i + 1) * r8, tp // 8 - 1), cb))

    in_specs = [spec(t) for t in tiled]
    in_specs += [pl.BlockSpec(a.shape, lambda i, nd=a.ndim: (0,) * nd) for a in full]
    out_specs = [pl.BlockSpec((tm, w), lambda i: (i, 0)) for w, _ in outs]
    out_specs += [pl.BlockSpec((r, w), lambda i: (0, 0)) for r, w in accs]
    out_shape = [jax.ShapeDtypeStruct((tp, w), dt) for w, dt in outs]
    out_shape += [jax.ShapeDtypeStruct((r, w), F32) for r, w in accs]
    res = pl.pallas_call(
        body, name=name, grid=(nt,), in_specs=in_specs, out_specs=out_specs, out_shape=out_shape,
        compiler_params=pltpu.CompilerParams(dimension_semantics=("arbitrary",), vmem_limit_bytes=VMEM_LIMIT),
    )(*[t.arr for t in tiled], *full)
    return res


def _pick(n, cap, mult):
    best = None
    for d in range(mult, min(n, cap) + 1, mult):
        if n % d == 0:
            best = d
    assert best is not None, (n, cap, mult)
    return best


def _mm(name, a, b, mode, out_dtype=F32, resid=None):
    if mode == "tn":
        m, k = a.shape
        n = b.shape[1]
        tk = _pick(k, 512, 128)
        tn = _pick(n, 1408, 128)

        def body_tn(a_ref, b_ref, o_ref):
            o_ref[...] = _tn(a_ref[...], b_ref[...]).astype(o_ref.dtype)

        return pl.pallas_call(
            body_tn, name=name, grid=(n // tn, k // tk),
            in_specs=[pl.BlockSpec((m, tk), lambda j, p: (0, p)),
                      pl.BlockSpec((m, tn), lambda j, p: (0, j))],
            out_specs=pl.BlockSpec((tk, tn), lambda j, p: (p, j)),
            out_shape=jax.ShapeDtypeStruct((k, n), out_dtype),
            compiler_params=pltpu.CompilerParams(
                dimension_semantics=("parallel", "parallel"), vmem_limit_bytes=VMEM_LIMIT),
        )(a, b)

    m, k = a.shape
    n = b.shape[1] if mode == "nn" else b.shape[0]
    tn = _pick(n, 1408, 128)
    tm = _pick(m, 640 if k <= 3072 else 320, 16)
    dotf = _nn if mode == "nn" else _nt

    def body(*refs):
        if resid is None:
            a_ref, b_ref, o_ref = refs
            o_ref[...] = dotf(a_ref[...], b_ref[...]).astype(o_ref.dtype)
        else:
            a_ref, b_ref, r_ref, o_ref = refs
            o_ref[...] = (r_ref[...] + dotf(a_ref[...], b_ref[...])).astype(o_ref.dtype)

    b_spec = (pl.BlockSpec((k, tn), lambda j, i: (0, j)) if mode == "nn"
              else pl.BlockSpec((tn, k), lambda j, i: (j, 0)))
    in_specs = [pl.BlockSpec((tm, k), lambda j, i: (i, 0)), b_spec]
    args = [a, b]
    if resid is not None:
        in_specs.append(pl.BlockSpec((tm, tn), lambda j, i: (i, j)))
        args.append(resid)
    return pl.pallas_call(
        body, name=name, grid=(n // tn, m // tm), in_specs=in_specs,
        out_specs=pl.BlockSpec((tm, tn), lambda j, i: (i, j)),
        out_shape=jax.ShapeDtypeStruct((m, n), out_dtype),
        compiler_params=pltpu.CompilerParams(
            dimension_semantics=("parallel", "parallel"), vmem_limit_bytes=VMEM_LIMIT),
    )(*args)


ATTN_Q_TILES = 4


def _attn_probs(q, k, row0):
    tq, tp = q.shape[0], k.shape[0]
    s = _nt(q, k) * (1.0 / math.sqrt(QK_HEAD))
    row = row0 + lax.broadcasted_iota(jnp.int32, (tq, tp), 0)
    col = lax.broadcasted_iota(jnp.int32, (tq, tp), 1)
    ok = (col <= row) & (col >= PAD)
    s = jnp.where(ok, s, NEG)
    m = jnp.max(s, axis=-1, keepdims=True)
    e = jnp.exp(s - m)
    e = jnp.where(ok, e, 0.0)
    l = jnp.sum(e, axis=-1, keepdims=True)
    return e / jnp.maximum(l, 1e-30)


def _attn_fwd(q, k, v):
    tp = q.shape[0]
    tq = tp // ATTN_Q_TILES

    def body(q_ref, k_ref, v_ref, o_ref):
        for i in range(ATTN_Q_TILES):
            rows = slice(i * tq, (i + 1) * tq)
            keys = slice(0, (i + 1) * tq)
            p = _attn_probs(q_ref[rows, :], k_ref[keys, :], i * tq)
            o_ref[rows, :] = _nn(p, v_ref[keys, :])

    return pl.pallas_call(
        body, name="attn_fwd", grid=(MLA_HEADS,),
        in_specs=[pl.BlockSpec((tp, HP), lambda h: (0, h)),
                  pl.BlockSpec((tp, HP), lambda h: (0, h)),
                  pl.BlockSpec((tp, V_HEAD), lambda h: (0, h))],
        out_specs=pl.BlockSpec((tp, V_HEAD), lambda h: (0, h)),
        out_shape=jax.ShapeDtypeStruct((tp, MLA_HEADS * V_HEAD), F32),
        compiler_params=pltpu.CompilerParams(dimension_semantics=("parallel",), vmem_limit_bytes=VMEM_LIMIT),
    )(q, k, v)


def _attn_bwd(q, k, v, do):
    tp = q.shape[0]
    tq = tp // ATTN_Q_TILES

    def body(q_ref, k_ref, v_ref, do_ref, dq_ref, dk_ref, dv_ref):
        for i in reversed(range(ATTN_Q_TILES)):
            rows = slice(i * tq, (i + 1) * tq)
            keys = slice(0, (i + 1) * tq)
            qb = q_ref[rows, :]
            kk = k_ref[keys, :]
            dob = do_ref[rows, :]
            p = _attn_probs(qb, kk, i * tq)
            dp = _nt(dob, v_ref[keys, :])
            delta = jnp.sum(p * dp, axis=-1, keepdims=True)
            ds = p * (dp - delta) * (1.0 / math.sqrt(QK_HEAD))
            dq_ref[rows, :] = _nn(ds, kk)
            if i == ATTN_Q_TILES - 1:
                dk_ref[...] = _tn(ds, qb)
                dv_ref[...] = _tn(p, dob)
            else:
                dk_ref[keys, :] += _tn(ds, qb)
                dv_ref[keys, :] += _tn(p, dob)

    full = lambda w: pl.BlockSpec((tp, w), lambda h: (0, h))
    return pl.pallas_call(
        body, name="attn_bwd", grid=(MLA_HEADS,),
        in_specs=[full(HP), full(HP), full(V_HEAD), full(V_HEAD)],
        out_specs=[full(HP), full(HP), full(V_HEAD)],
        out_shape=[jax.ShapeDtypeStruct((tp, MLA_HEADS * HP), F32),
                   jax.ShapeDtypeStruct((tp, MLA_HEADS * HP), F32),
                   jax.ShapeDtypeStruct((tp, MLA_HEADS * V_HEAD), F32)],
        compiler_params=pltpu.CompilerParams(dimension_semantics=("parallel",), vmem_limit_bytes=VMEM_LIMIT),
    )(q, k, v, do)


def _gdn_consts():
    c = DN_CHUNK
    r = lax.broadcasted_iota(jnp.int32, (c, c), 0)
    cc = lax.broadcasted_iota(jnp.int32, (c, c), 1)
    incl = r >= cc
    strict = r > cc
    return incl, strict


def _each(fn, *lists):
    return [fn(*a) for a in zip(*lists)]


def _interleave(chains):
    chains = list(chains)
    while chains:
        for ch in list(chains):
            try:
                next(ch)
            except StopIteration:
                chains.remove(ch)


def _gdn_chunk_common(q_ref, k_ref, v_ref, g_ref, b_ref):
    c = DN_CHUNK
    incl, strict = _gdn_consts()
    sls = [slice(DN_DIM * h, DN_DIM * (h + 1)) for h in range(DN_HEADS)]
    inclf = incl.astype(F32)
    ones = jnp.full((c, LANE), 1.0 / LANE, F32)
    q = [q_ref[:, sl] * (1.0 / math.sqrt(DN_DIM)) for sl in sls]
    k = [k_ref[:, sl] for sl in sls]
    v = [v_ref[:, sl] for sl in sls]
    g = [g_ref[:, sl] for sl in sls]
    beta = [b_ref[:, sl] for sl in sls]
    gc = [_nn(inclf, x, hp=True) for x in g]
    grow = [_nt(ones, x, hp=True) for x in gc]
    kb = _each(jnp.multiply, k, beta)
    kk = _each(_nt, kb, k)
    qk = _each(_nt, q, k)
    gam = [jnp.exp(x) for x in gc]
    g_last = [_rowsum(x) for x in g]
    dm = [jnp.exp(jnp.where(incl, x[:, :c] - y, NEG)) for x, y in zip(gc, grow)]
    vb = _each(jnp.multiply, v, beta)
    kbg = _each(jnp.multiply, kb, gam)
    ek = [jnp.exp(x - y) for x, y in zip(g_last, gc)]
    kd = _each(jnp.multiply, k, ek)
    return dict(q=q, k=k, v=v, beta=beta, gc=gc, gam=gam, g_last=g_last, dm=dm, kb=kb, vb=vb,
                kbg=kbg, kk=kk, ek=ek, kd=kd, qk=qk, incl=incl, strict=strict, sls=sls)


def _gdn_fwd(q, k, v, g, beta):
    tp = q.shape[0]
    c = DN_CHUNK
    nch = tp // c

    def body(q_ref, k_ref, v_ref, g_ref, b_ref, o_ref, s_ref, t_ref, s_scr):
        @pl.when(pl.program_id(0) == 0)
        def _():
            s_scr[...] = jnp.zeros_like(s_scr)

        eye = (lax.broadcasted_iota(jnp.int32, (c, c), 0) == lax.broadcasted_iota(jnp.int32, (c, c), 1)).astype(F32)
        x = _gdn_chunk_common(q_ref, k_ref, v_ref, g_ref, b_ref)
        heads = range(DN_HEADS)
        s = [s_scr[h] for h in heads]
        bp = [-jnp.where(x["strict"], kk * dm, 0.0) for kk, dm in zip(x["kk"], x["dm"])]
        t = [eye + b for b in bp]
        for _ in range(5):
            bp = [_nn(b, b, hp=True) for b in bp]
            t = [tt + _nn(tt, b, hp=True) for tt, b in zip(t, bp)]
        u = _each(_nn, t, x["vb"])
        w = _each(_nn, t, x["kbg"])
        v_new = [uu - _nn(ww, ss) for uu, ww, ss in zip(u, w, s)]
        o = [_nn(q * gam, ss) + _nn(qk * dm, vn)
             for q, gam, ss, qk, dm, vn in zip(x["q"], x["gam"], s, x["qk"], x["dm"], v_new)]
        s_new = [ss * jnp.exp(gl) + _tn(kd, vn) for ss, gl, kd, vn in zip(s, x["g_last"], x["kd"], v_new)]
        for h in heads:
            s_ref[h, 0] = s[h]
            t_ref[h, 0] = t[h]
            o_ref[:, x["sls"][h]] = o[h]
            s_scr[h] = s_new[h]

    rb = lambda n: (n, 0)
    return pl.pallas_call(
        body, name="gdn_fwd", grid=(nch,),
        in_specs=[pl.BlockSpec((c, DN_WIDTH), rb)] * 5,
        out_specs=[pl.BlockSpec((c, DN_WIDTH), rb),
                   pl.BlockSpec((DN_HEADS, 1, DN_DIM, DN_DIM), lambda n: (0, n, 0, 0)),
                   pl.BlockSpec((DN_HEADS, 1, c, c), lambda n: (0, n, 0, 0))],
        out_shape=[jax.ShapeDtypeStruct((tp, DN_WIDTH), F32),
                   jax.ShapeDtypeStruct((DN_HEADS, nch, DN_DIM, DN_DIM), F32),
                   jax.ShapeDtypeStruct((DN_HEADS, nch, c, c), F32)],
        scratch_shapes=[pltpu.VMEM((DN_HEADS, DN_DIM, DN_DIM), F32)],
        compiler_params=pltpu.CompilerParams(dimension_semantics=("arbitrary",), vmem_limit_bytes=VMEM_LIMIT),
    )(q, k, v, g, beta)


def _gdn_bwd(q, k, v, g, beta, s_all, t_all, do):
    tp = q.shape[0]
    c = DN_CHUNK
    nch = tp // c

    def body(q_ref, k_ref, v_ref, g_ref, b_ref, s_ref, t_ref, do_ref,
             dq_ref, dk_ref, dv_ref, dg_ref, db_ref, ds_scr):
        @pl.when(pl.program_id(0) == 0)
        def _():
            ds_scr[...] = jnp.zeros_like(ds_scr)

        ones_cl = jnp.ones((c, LANE), F32)
        xs = _gdn_chunk_common(q_ref, k_ref, v_ref, g_ref, b_ref)
        upper = jnp.logical_not(xs["strict"]).astype(F32)

        def chain(h):
            x = {key: (val[h] if isinstance(val, list) else val) for key, val in xs.items()}
            sl = x["sls"]
            qs, kx, vx, beta_, gam, dm = x["q"], x["k"], x["v"], x["beta"], x["gam"], x["dm"]
            kb, vb, kbg, kd, ek = x["kb"], x["vb"], x["kbg"], x["kd"], x["ek"]
            t = t_ref[h, 0]
            s = s_ref[h, 0]
            dsn = ds_scr[h]
            dob = do_ref[:, sl]
            eg_last = jnp.exp(x["g_last"])
            u = _nn(t, vb)
            w = _nn(t, kbg)
            mqk = x["qk"] * dm
            qd = qs * gam
            dqd = _nt(dob, s)
            dkd_pre = _nn(kd, dsn)
            yield
            v_new = u - _nn(w, s)
            dv_new = _tn(mqk, dob) + dkd_pre
            dq = dqd * gam
            dgam = jnp.sum(dqd * qs, axis=1, keepdims=True)
            yield
            ds_new = _tn(qd, dob) + eg_last * dsn - _tn(w, dv_new)
            dmm = jnp.where(x["incl"], _nt(dob, v_new), 0.0)
            dkd = _nt(v_new, dsn)
            dw = -_nt(dv_new, s)
            dvb = _tn(t, dv_new)
            dt = _nt(dv_new, vb)
            yield
            dqk = dmm * dm
            e_mat = dmm * mqk
            dq = dq + _nn(dqk, kx)
            dk = _tn(dqk, qs) + dkd * ek
            e1 = jnp.sum(dkd * kd, axis=1, keepdims=True)
            dgc = -e1
            dg_last = jnp.sum(e1) + eg_last * jnp.sum(s * dsn)
            dt = dt + _nt(dw, kbg)
            dkbg = _tn(t, dw)
            yield
            tdt = _tn(t, dt, hp=True)
            yield
            da = jnp.where(x["strict"], -_nt(tdt, t, hp=True), 0.0)
            yield
            dkk = da * dm
            e_mat = e_mat + da * x["kk"] * dm
            dkb = _nn(dkk, kx) + dkbg * gam
            dk = dk + _tn(dkk, kb)
            dgam = dgam + jnp.sum(dkbg * kb, axis=1, keepdims=True)
            yield
            dk = dk + dkb * beta_
            dbeta = jnp.sum(dkb * kx, axis=1, keepdims=True) + jnp.sum(dvb * vx, axis=1, keepdims=True)
            dv = dvb * beta_
            dgc = dgc + jnp.sum(e_mat, axis=1, keepdims=True) + dgam * gam
            dgc = dgc - _tn(e_mat, ones_cl, hp=True)
            yield
            dg = _nn(upper, dgc, hp=True) + dg_last
            yield
            ds_scr[h] = ds_new
            dq_ref[:, sl] = dq * (1.0 / math.sqrt(DN_DIM))
            dk_ref[:, sl] = dk
            dv_ref[:, sl] = dv
            dg_ref[:, sl] = dg
            db_ref[:, sl] = jnp.broadcast_to(dbeta, (c, LANE))

        _interleave([chain(h) for h in range(DN_HEADS)])

    rb = lambda n: (nch - 1 - n, 0)
    hs = lambda n: (0, nch - 1 - n, 0, 0)
    return pl.pallas_call(
        body, name="gdn_bwd", grid=(nch,),
        in_specs=[pl.BlockSpec((c, DN_WIDTH), rb)] * 5
        + [pl.BlockSpec((DN_HEADS, 1, DN_DIM, DN_DIM), hs), pl.BlockSpec((DN_HEADS, 1, c, c), hs),
           pl.BlockSpec((c, DN_WIDTH), rb)],
        out_specs=[pl.BlockSpec((c, DN_WIDTH), rb)] * 5,
        out_shape=[jax.ShapeDtypeStruct((tp, DN_WIDTH), F32)] * 5,
        scratch_shapes=[pltpu.VMEM((DN_HEADS, DN_DIM, DN_DIM), F32)],
        compiler_params=pltpu.CompilerParams(dimension_semantics=("arbitrary",), vmem_limit_bytes=VMEM_LIMIT),
    )(q, k, v, g, beta, s_all, t_all, do)


def _silu_parts(x):
    s = _sigmoid(x)
    return x * s, s * (1.0 + x * (1.0 - s))


def _f_rms_cast(i, x, w):
    y, _ = _rms_fwd(x, w, x.shape[1])
    return (y,), ()


def _f_rms_bwd_add(i, x, dy, dres, w, *, mask_pad):
    dx, dwr = _rms_bwd(x, w, dy, x.shape[1])
    out = dres + dx
    if mask_pad:
        out = jnp.where(_row_ids(i, x.shape[0]) >= PAD, out, 0.0)
    return (out,), (_rowsum(dwr),)


def _f_lat_norm(i, ql, kvl, qw, kvw):
    return (_rms_fwd(ql, qw, Q_LORA)[0], _rms_fwd(kvl, kvw, KV_LORA)[0]), ()


def _f_lat_norm_bwd(i, ql, kvl, dqn, dkvn, qw, kvw):
    dq, dqw = _rms_bwd(ql, qw, dqn, Q_LORA)
    dk, dkw = _rms_bwd(kvl, kvw, dkvn, KV_LORA)
    return (dq, dk), (_rowsum(dqw), _rowsum(dkw))


def _rope(x, cos, sin_s):
    return x * cos + _swap_halves(x) * sin_s


def _rope_t(dy, cos, sin_s):
    return dy * cos + _swap_halves(dy * sin_s)


def _f_mla_qk(i, qf, kvf, kpe, cos, sin_s, qw, kw):
    qs, ks, vs = [], [], []
    for h in range(MLA_HEADS):
        qn, _ = _rms_fwd(qf[:, HP * h:HP * (h + 1)], qw, QK_HEAD)
        qs += [qn[:, :QK_NOPE], _rope(qn[:, QK_NOPE:], cos, sin_s)]
        kh = jnp.concatenate([kvf[:, HP * h:HP * h + QK_NOPE], kpe], axis=1)
        kn, _ = _rms_fwd(kh, kw, QK_HEAD)
        ks += [kn[:, :QK_NOPE], _rope(kn[:, QK_NOPE:], cos, sin_s)]
        vs.append(kvf[:, HP * h + QK_NOPE:HP * (h + 1)])
    return (jnp.concatenate(qs, axis=1), jnp.concatenate(ks, axis=1), jnp.concatenate(vs, axis=1)), ()


def _f_mla_qk_bwd(i, qf, kvf, kpe, cos, sin_s, dq, dk, dv, qw, kw):
    dqf, dkvf = [], []
    dkpe = None
    dqw = None
    dkw = None
    for h in range(MLA_HEADS):
        dqh = dq[:, HP * h:HP * (h + 1)]
        dqn = jnp.concatenate([dqh[:, :QK_NOPE], _rope_t(dqh[:, QK_NOPE:], cos, sin_s)], axis=1)
        dx, dwr = _rms_bwd(qf[:, HP * h:HP * (h + 1)], qw, dqn, QK_HEAD)
        dqf.append(dx)
        dqw = _rowsum(dwr) if dqw is None else dqw + _rowsum(dwr)
        dkh = dk[:, HP * h:HP * (h + 1)]
        dkn = jnp.concatenate([dkh[:, :QK_NOPE], _rope_t(dkh[:, QK_NOPE:], cos, sin_s)], axis=1)
        kh = jnp.concatenate([kvf[:, HP * h:HP * h + QK_NOPE], kpe], axis=1)
        dx, dwr = _rms_bwd(kh, kw, dkn, QK_HEAD)
        dkvf += [dx[:, :QK_NOPE], dv[:, V_HEAD * h:V_HEAD * (h + 1)]]
        dkpe = dx[:, QK_NOPE:] if dkpe is None else dkpe + dx[:, QK_NOPE:]
        dkw = _rowsum(dwr) if dkw is None else dkw + _rowsum(dwr)
    return (jnp.concatenate(dqf, axis=1), jnp.concatenate(dkvf, axis=1), dkpe), (dqw, dkw)


def _gdn_act(i, x, halo, w8):
    tm = x.shape[0]
    halo = jnp.where(i > 0, halo, 0.0)
    c = _conv_fwd(x, halo, w8, DN_CONV)
    act, dact = _silu_parts(c)
    return act, dact


def _f_gdn_prep(i, x, halo, ab, w8, alog, dtb, sel):
    tm = x.shape[0]
    act, _ = _gdn_act(i, x, halo, w8)
    outs = []
    for part in range(2):
        for h in range(DN_HEADS):
            t = act[:, DN_WIDTH * part + DN_DIM * h:DN_WIDTH * part + DN_DIM * (h + 1)]
            outs.append(t * lax.rsqrt(jnp.sum(t * t, axis=-1, keepdims=True) + EPS))
    q = jnp.concatenate(outs[:DN_HEADS], axis=1)
    k = jnp.concatenate(outs[DN_HEADS:], axis=1)
    v = act[:, 2 * DN_WIDTH:]
    abb = _nn(ab, sel, hp=True)
    valid = _row_ids(i, tm) >= PAD
    g = jnp.where(valid, -jnp.exp(alog) * _softplus(abb[:, :DN_WIDTH] + dtb), 0.0)
    beta = jnp.where(valid, _sigmoid(abb[:, DN_WIDTH:]), 0.0)
    return (q, k, v, g, beta), ()


def _f_gdn_prep_bwd(i, x, halo, ab, dq, dk, dv, dg, dbeta, w8, alog, dtb, sel, selpick):
    tm = x.shape[0]
    act, dact = _gdn_act(i, x, halo, w8)
    douts = []
    for part, dd in enumerate((dq, dk)):
        for h in range(DN_HEADS):
            t = act[:, DN_WIDTH * part + DN_DIM * h:DN_WIDTH * part + DN_DIM * (h + 1)]
            r = lax.rsqrt(jnp.sum(t * t, axis=-1, keepdims=True) + EPS)
            y = t * r
            dy = dd[:, DN_DIM * h:DN_DIM * (h + 1)]
            douts.append(r * (dy - y * jnp.sum(dy * y, axis=-1, keepdims=True)))
    douts.append(dv)
    dc = jnp.concatenate(douts, axis=1) * dact
    abb = _nn(ab, sel, hp=True)
    valid = _row_ids(i, tm) >= PAD
    pre = abb[:, :DN_WIDTH] + dtb
    ea = jnp.exp(alog)
    g = -ea * _softplus(pre)
    dg = jnp.where(valid, dg, 0.0)
    dbeta = jnp.where(valid, dbeta, 0.0)
    da = dg * (-ea) * _sigmoid(pre)
    beta = _sigmoid(abb[:, DN_WIDTH:])
    db = dbeta * beta * (1.0 - beta)
    dab = _nn(jnp.concatenate([da, db], axis=1), selpick, hp=True)
    return (dc, dab), (_rowsum(dg * g), _rowsum(da))


def _f_conv_bwd(i, dy, dy_next, x, x_prev, w8, *, width, nt):
    dy_next = jnp.where(i < nt - 1, dy_next, 0.0)
    x_prev = jnp.where(i > 0, x_prev, 0.0)
    return (_conv_bwd_x(dy, dy_next, w8, width),), (_conv_bwd_w(dy, x, x_prev, width),)


def _f_mix(i, o_mla, o_dn, z, w_mla, w_dn):
    tm = o_mla.shape[0]
    valid = _row_ids(i, tm) >= PAD
    outs = []
    for h in range(MLA_HEADS):
        y, _ = _rms_fwd(o_mla[:, V_HEAD * h:V_HEAD * (h + 1)], w_mla, V_HEAD)
        outs.append(jnp.where(valid, y, 0.0))
    for h in range(DN_HEADS):
        y, _ = _rms_fwd(o_dn[:, DN_DIM * h:DN_DIM * (h + 1)], w_dn, DN_DIM)
        outs.append(y * _silu_parts(z[:, DN_DIM * h:DN_DIM * (h + 1)])[0])
    return (jnp.concatenate(outs, axis=1),), ()


def _f_mix_bwd(i, o_mla, o_dn, z, dy_mla, dy_dn, w_mla, w_dn):
    tm = o_mla.shape[0]
    valid = _row_ids(i, tm) >= PAD
    d_mla, d_dn, d_z = [], [], []
    dw_mla = None
    dw_dn = None
    for h in range(MLA_HEADS):
        sl = slice(V_HEAD * h, V_HEAD * (h + 1))
        dx, dwr = _rms_bwd(o_mla[:, sl], w_mla, jnp.where(valid, dy_mla[:, sl], 0.0), V_HEAD)
        d_mla.append(dx)
        dw_mla = _rowsum(dwr) if dw_mla is None else dw_mla + _rowsum(dwr)
    for h in range(DN_HEADS):
        sl = slice(DN_DIM * h, DN_DIM * (h + 1))
        y, _ = _rms_fwd(o_dn[:, sl], w_dn, DN_DIM)
        sz, dsz = _silu_parts(z[:, sl])
        d_z.append(dy_dn[:, sl] * y * dsz)
        dx, dwr = _rms_bwd(o_dn[:, sl], w_dn, dy_dn[:, sl] * sz, DN_DIM)
        d_dn.append(dx)
        dw_dn = _rowsum(dwr) if dw_dn is None else dw_dn + _rowsum(dwr)
    return ((jnp.concatenate(d_mla, axis=1), jnp.concatenate(d_dn, axis=1), jnp.concatenate(d_z, axis=1)),
            (dw_mla, dw_dn))


def _f_ffn_act(i, gate_pre, halo, up, w8, b):
    halo = jnp.where(i > 0, halo, 0.0)
    gate = _conv_fwd(gate_pre, halo, w8, FFN_CONV) + b
    return (_silu_parts(gate)[0] * up,), ()


def _f_ffn_act_bwd(i, gate_pre, halo, up, dact, w8, b):
    halo = jnp.where(i > 0, halo, 0.0)
    gate = _conv_fwd(gate_pre, halo, w8, FFN_CONV) + b
    sg, dsg = _silu_parts(gate)
    dgate = dact * up * dsg
    return (dgate, dact * sg), (_rowsum(dgate),)


def _f_loss(i, h3, tgt):
    tm = h3.shape[0]
    diff = jnp.where(_row_ids(i, tm) >= ROW0, h3 - tgt, 0.0)
    part = 0.5 * jnp.sum(diff * diff) * (1.0 / D_MODEL)
    return (diff * (1.0 / D_MODEL),), (jnp.full((1, LANE), part, F32),)


def _after(fn):
    return lambda i, *a: fn(i, *a[:-1])


def _local_step(h0, tgt, w, token, late_weights, grads_ready):
    tp = h0.shape[0]
    nt = tp // TM
    bf = (D_MODEL, _MXU)
    u, = _rows("rms_in", _after(_f_rms_cast), [_In(h0)], [w["attn_norm_w"], token], [bf])
    proj = _mm("in_proj", u, w["w_in"], "nn")
    p_qkv = lambda kind="cur": _In(proj, 3 * DN_WIDTH, 0, kind)
    p_z = _In(proj, DN_WIDTH, C_Z // DN_WIDTH)
    p_ql = _In(proj, Q_LORA, C_QL // Q_LORA)
    p_kvl = _In(proj, KV_LORA, C_KVL // KV_LORA)
    p_kpe = _In(proj, LANE, C_KPE // LANE)
    p_ab = _In(proj, LANE, C_AB // LANE)
    cos, sin_s = _In(w["cos"]), _In(w["sin_s"])

    qn, kvn = _rows("mla_lat_norm", _f_lat_norm, [p_ql, p_kvl], [w["q_a_norm_w"], w["kv_a_norm_w"]],
                    [(Q_LORA, _MXU), (KV_LORA, _MXU)])
    qf = _mm("mla_q_b", qn, w["w_q_b"], "nn")
    kvf = _mm("mla_kv_b", kvn, w["w_kv_b"], "nn")
    qk_w = [w["q_norm_w"], w["k_norm_w"]]
    q, k, v = _rows("mla_qk", _f_mla_qk, [_In(qf), _In(kvf), p_kpe, cos, sin_s], qk_w,
                    [(MLA_HEADS * HP, _MXU), (MLA_HEADS * HP, _MXU), (MLA_HEADS * V_HEAD, _MXU)])
    o_mla = _attn_fwd(q, k, v)

    dn_w = [w["dn_conv_w"], w["alog_b"], w["dtb_b"], w["sel"]]
    gq, gk, gv, gg, gb = _rows("gdn_prep", _f_gdn_prep, [p_qkv(), p_qkv("prev"), p_ab], dn_w,
                               [(DN_WIDTH, F32)] * 5)
    o_dn, s_all, t_all = _gdn_fwd(gq, gk, gv, gg, gb)

    out_w = [w["mla_out_norm_w"], w["dn_out_norm_w"]]
    mixed, = _rows("mix", _f_mix, [_In(o_mla), _In(o_dn), p_z], out_w, [bf])
    w = dict(w, **late_weights(mixed))
    h2 = _mm("out_proj", mixed, w["w_out"], "nn", resid=h0)

    hn, = _rows("rms_ffn", _f_rms_cast, [_In(h2)], [w["ffn_norm_w"]], [bf])
    gate_pre = _mm("ffn_gate", hn, w["w_gate"], "nn")
    up = _mm("ffn_up", hn, w["w_up"], "nn")
    ffn_w = [w["ffn_conv_w"], w["ffn_conv_b"]]
    act, = _rows("ffn_act", _f_ffn_act, [_In(gate_pre), _In(gate_pre, kind="prev"), _In(up)], ffn_w,
                 [(D_FF, _MXU)])
    h3 = _mm("ffn_down", act, w["w_down"], "nn", resid=h2)

    dh3, loss = _rows("loss", _f_loss, [_In(h3), _In(tgt)], [], [(D_MODEL, F32)], [(1, LANE)])

    g = {}
    dact = _mm("ffn_down_dx", dh3, w["w_down"], "nt")
    g["w_down"] = _mm("ffn_down_dw", act, dh3, "tn", out_dtype=_MXU)
    dgate, dup, g["ffn_conv_b"] = _rows(
        "ffn_act_bwd", _f_ffn_act_bwd, [_In(gate_pre), _In(gate_pre, kind="prev"), _In(up), _In(dact)], ffn_w,
        [(D_FF, F32), (D_FF, _MXU)], [(1, D_FF)])
    dgate_pre, g["ffn_conv_w"] = _rows(
        "ffn_conv_bwd", functools.partial(_f_conv_bwd, width=FFN_CONV, nt=nt),
        [_In(dgate), _In(dgate, kind="next"), _In(gate_pre), _In(gate_pre, kind="prev")], [w["ffn_conv_w"]],
        [(D_FF, _MXU)], [(8, D_FF)])
    dhn = _mm("ffn_gate_dx", dgate_pre, w["w_gate"], "nt")
    dhn = _mm("ffn_up_dx", dup, w["w_up"], "nt", resid=dhn)
    g["w_gate"] = _mm("ffn_gate_dw", hn, dgate_pre, "tn")
    g["w_up"] = _mm("ffn_up_dw", hn, dup, "tn")
    tok = grads_ready(g, ("w_down", "w_gate", "w_up"))
    dh2, g["ffn_norm_w"] = _rows(
        "rms_ffn_bwd", _after(functools.partial(_f_rms_bwd_add, mask_pad=True)), [_In(h2), _In(dhn), _In(dh3)],
        [w["ffn_norm_w"], tok], [(D_MODEL, F32)], [(1, D_MODEL)])

    dmixed = _mm("out_proj_dx", dh2, w["w_out"], "nt")
    g["w_out"] = _mm("out_proj_dw", mixed, dh2, "tn", out_dtype=_MXU)
    half = MLA_HEADS * V_HEAD
    do_mla, do_dn, dz, g["mla_out_norm_w"], g["dn_out_norm_w"] = _rows(
        "mix_bwd", _f_mix_bwd, [_In(o_mla), _In(o_dn), p_z, _In(dmixed, half, 0), _In(dmixed, half, 1)], out_w,
        [(half, F32), (DN_WIDTH, F32), (DN_WIDTH, _MXU)], [(1, V_HEAD), (1, DN_DIM)])

    dq, dk, dv = _attn_bwd(q, k, v, do_mla)
    dqf, dkvf, dkpe, g["q_norm_w"], g["k_norm_w"] = _rows(
        "mla_qk_bwd", _f_mla_qk_bwd, [_In(qf), _In(kvf), p_kpe, cos, sin_s, _In(dq), _In(dk), _In(dv)], qk_w,
        [(MLA_HEADS * HP, _MXU), (MLA_HEADS * HP, _MXU), (LANE, _MXU)], [(1, HP), (1, HP)])
    dqn = _mm("mla_q_b_dx", dqf, w["w_q_b"], "nt")
    g["w_q_b"] = _mm("mla_q_b_dw", qn, dqf, "tn")
    dkvn = _mm("mla_kv_b_dx", dkvf, w["w_kv_b"], "nt")
    g["w_kv_b"] = _mm("mla_kv_b_dw", kvn, dkvf, "tn")
    tok = grads_ready(g, ("w_out", "w_q_b", "w_kv_b"))
    dql, dkvl, g["q_a_norm_w"], g["kv_a_norm_w"] = _rows(
        "mla_lat_norm_bwd", _after(_f_lat_norm_bwd), [p_ql, p_kvl, _In(dqn), _In(dkvn)],
        [w["q_a_norm_w"], w["kv_a_norm_w"], tok],[(Q_LORA, _MXU), (KV_LORA, _MXU)], [(1, Q_LORA), (1, KV_LORA)])

    dgq, dgk, dgv, dgg, dgb = _gdn_bwd(gq, gk, gv, gg, gb, s_all, t_all, do_dn)
    dc, dab, g["alog_b"], g["dtb_b"] = _rows(
        "gdn_prep_bwd", _f_gdn_prep_bwd,
        [p_qkv(), p_qkv("prev"), p_ab, _In(dgq), _In(dgk), _In(dgv), _In(dgg), _In(dgb)], dn_w + [w["selpick"]],
        [(3 * DN_WIDTH, F32), (LANE, _MXU)], [(1, DN_WIDTH), (1, DN_WIDTH)])
    dqkv, g["dn_conv_w"] = _rows(
        "gdn_conv_bwd", functools.partial(_f_conv_bwd, width=DN_CONV, nt=nt),
        [_In(dc), _In(dc, kind="next"), p_qkv(), p_qkv("prev")], [w["dn_conv_w"]],
        [(3 * DN_WIDTH, _MXU)], [(8, 3 * DN_WIDTH)])

    dproj = jnp.concatenate([dqkv, dz, dql, dkvl, dkpe, dab], axis=1)
    du = _mm("in_proj_dx", dproj, w["w_in"], "nt")
    g["w_in"] = _mm("in_proj_dw", u, dproj, "tn")
    dh0, g["attn_norm_w"] = _rows(
        "rms_in_bwd", functools.partial(_f_rms_bwd_add, mask_pad=False), [_In(h0), _In(du), _In(dh2)],
        [w["attn_norm_w"]], [(D_MODEL, F32)], [(1, D_MODEL)])
    return loss, dh0, g


def _w_in_to_padded(w):
    c1, c2, c3 = Q_LORA, Q_LORA + KV_LORA, Q_LORA + KV_LORA + QK_ROPE
    c4 = c3 + 3 * DN_WIDTH
    c5 = c4 + DN_WIDTH
    z = lambda n: jnp.zeros((w.shape[0], n), w.dtype)
    return jnp.concatenate([w[:, c3:c4], w[:, c4:c5], w[:, :c1], w[:, c1:c2], w[:, c2:c3], z(LANE - QK_ROPE),
                            w[:, c5:], z(LANE - 2 * DN_HEADS)], axis=1)


def _w_in_from_padded(g):
    return jnp.concatenate([g[:, C_QL:C_QL + Q_LORA], g[:, C_KVL:C_KVL + KV_LORA], g[:, C_KPE:C_KPE + QK_ROPE],
                            g[:, :C_Z + DN_WIDTH], g[:, C_AB:C_AB + 2 * DN_HEADS]], axis=1)


def _w_q_b_to_padded(w):
    r = w.shape[0]
    w = w.reshape(r, MLA_HEADS, QK_HEAD)
    return jnp.pad(w, ((0, 0), (0, 0), (0, HP - QK_HEAD))).reshape(r, MLA_HEADS * HP)


def _w_q_b_from_padded(g):
    r = g.shape[0]
    return g.reshape(r, MLA_HEADS, HP)[:, :, :QK_HEAD].reshape(r, MLA_HEADS * QK_HEAD)


def _pad_rows8(w):
    return jnp.pad(w, ((0, 8 - w.shape[0]), (0, 0)))


def _prepare(full, tp):
    w = {}
    mx = lambda a: a.astype(_MXU)
    w["attn_norm_w"] = full["attn_norm_w"]
    w["w_in"] = mx(_w_in_to_padded(full["w_in"]))
    w["q_a_norm_w"] = full["q_a_norm_w"]
    w["kv_a_norm_w"] = full["kv_a_norm_w"]
    w["w_q_b"] = mx(_w_q_b_to_padded(full["w_q_b"]))
    w["w_kv_b"] = mx(full["w_kv_b"])
    w["q_norm_w"] = jnp.pad(full["q_norm_w"], ((0, 0), (0, HP - QK_HEAD)))
    w["k_norm_w"] = jnp.pad(full["k_norm_w"], ((0, 0), (0, HP - QK_HEAD)))
    w["mla_out_norm_w"] = full["mla_out_norm_w"]
    w["dn_out_norm_w"] = full["dn_out_norm_w"]
    w["dn_conv_w"] = _pad_rows8(full["dn_conv_w"])
    w["alog_b"] = jnp.repeat(full["dn_A_log"], DN_DIM, axis=1)
    w["dtb_b"] = jnp.repeat(full["dn_dt_bias"], DN_DIM, axis=1)
    w["ffn_norm_w"] = full["ffn_norm_w"]
    w["ffn_conv_w"] = _pad_rows8(full["ffn_conv_w"])
    w["ffn_conv_b"] = full["ffn_conv_b"]
    for n in _LATE:
        if n in full:
            w[n] = mx(full[n])
    half = QK_ROPE // 2
    inv = ROPE_THETA ** (-jnp.arange(half, dtype=F32) / half)
    ang = (jnp.arange(tp, dtype=jnp.int32) - PAD).astype(F32)[:, None] * inv[None, :]
    zc = jnp.zeros((tp, LANE - QK_ROPE), F32)
    w["cos"] = jnp.concatenate([jnp.cos(ang), jnp.cos(ang), zc], axis=1)
    w["sin_s"] = jnp.concatenate([-jnp.sin(ang), jnp.sin(ang), zc], axis=1)
    lane = jnp.arange(2 * DN_WIDTH)[None, :]
    src = jnp.arange(LANE)[:, None]
    w["sel"] = ((lane // DN_DIM) == src).astype(F32)
    w["selpick"] = ((src.T == (lane.T // DN_DIM)) & (lane.T % DN_DIM == 0)).astype(F32)
    return w


def _grads_to_natural(g):
    convert = {
        "w_in": ("w_in", _w_in_from_padded),
        "w_q_b": ("w_q_b", _w_q_b_from_padded),
        "q_norm_w": ("q_norm_w", lambda a: a[:, :QK_HEAD]),
        "k_norm_w": ("k_norm_w", lambda a: a[:, :QK_HEAD]),
        "dn_conv_w": ("dn_conv_w", lambda a: a[:DN_CONV]),
        "ffn_conv_w": ("ffn_conv_w", lambda a: a[:FFN_CONV]),
        "alog_b": ("dn_A_log", lambda a: a[:, ::DN_DIM]),
        "dtb_b": ("dn_dt_bias", lambda a: a[:, ::DN_DIM]),
    }
    n = {}
    for key, a in g.items():
        name, fn = convert.get(key, (key, lambda t: t))
        n[name] = fn(a)
    return n


_MESH = pl.DeviceIdType.MESH
_ANY = pl.BlockSpec(memory_space=pl.ANY)
_CHIP_FLIPS = ((1, 0), (0, 1), (1, 1))


def _me():
    return lax.axis_index("x"), lax.axis_index("y"), lax.axis_index("c")


def _all_gather(name, blk):
    def body(x_ref, out_ref, send_sems, recv_sems, local_sem):
        x, y, c = _me()
        me, sib = (x, y, c), (x, y, 1 - c)
        chips = [(x ^ fx, y ^ fy) for fx, fy in _CHIP_FLIPS]

        def slot(p):
            return out_ref.at[4 * p[0] + 2 * p[1] + p[2]]

        def copy(k, block, to, src=None):
            return pltpu.make_async_remote_copy(
                src_ref=slot(block) if src is None else src, dst_ref=slot(block),
                send_sem=send_sems.at[k], recv_sem=recv_sems.at[k], device_id=to, device_id_type=_MESH)

        mine = pltpu.make_async_copy(x_ref, slot(me), local_sem)
        mine.start()
        first = [copy(0, me, sib, src=x_ref)]
        first += [copy(1 + j, me, (*chip, c), src=x_ref) for j, chip in enumerate(chips)]
        for cp in first:
            cp.start()
        passed = [copy(4 + j, (*chip, c), sib) for j, chip in enumerate(chips)]
        for j, chip in enumerate(chips):
            copy(1 + j, (*chip, c), me).wait_recv()
            passed[j].start()
        copy(0, sib, me).wait_recv()
        for j, chip in enumerate(chips):
            copy(4 + j, (*chip, 1 - c), me).wait_recv()
        for cp in first + passed:
            cp.wait_send()
        mine.wait()

    return pl.pallas_call(
        body, name=name, in_specs=[_ANY], out_specs=_ANY,
        out_shape=jax.ShapeDtypeStruct((N_DEV,) + blk.shape, blk.dtype),
        scratch_shapes=[pltpu.SemaphoreType.DMA((7,)), pltpu.SemaphoreType.DMA((7,)), pltpu.SemaphoreType.DMA],
    )(blk)


def _rs_sibling(name, gb):
    def body(g_ref, out_ref, send_sems, recv_sems):
        x, y, c = _me()
        cps = []
        for j in range(4):
            cp = pltpu.make_async_remote_copy(
                src_ref=g_ref.at[2 * j + (1 - c)], dst_ref=out_ref.at[j], send_sem=send_sems.at[j],
                recv_sem=recv_sems.at[j], device_id=(x, y, 1 - c), device_id_type=_MESH)
            cp.start()
            cps.append(cp)
        for cp in cps:
            cp.wait()

    return pl.pallas_call(
        body, name=name, in_specs=[_ANY], out_specs=_ANY,
        out_shape=jax.ShapeDtypeStruct((4,) + gb.shape[1:], gb.dtype),
        scratch_shapes=[pltpu.SemaphoreType.DMA((4,)), pltpu.SemaphoreType.DMA((4,))],
    )(gb)


def _rs_chips(name, s1):
    def body(s_ref, out_ref, send_sems, recv_sems):
        x, y, c = _me()
        cps = []
        for k, (fx, fy) in enumerate(_CHIP_FLIPS):
            px, py = x ^ fx, y ^ fy
            cp = pltpu.make_async_remote_copy(
                src_ref=s_ref.at[2 * px + py], dst_ref=out_ref.at[k], send_sem=send_sems.at[k],
                recv_sem=recv_sems.at[k], device_id=(px, py, c), device_id_type=_MESH)
            cp.start()
            cps.append(cp)
        for cp in cps:
            cp.wait()

    return pl.pallas_call(
        body, name=name, in_specs=[_ANY], out_specs=_ANY,
        out_shape=jax.ShapeDtypeStruct((3,) + s1.shape[1:], s1.dtype),
        scratch_shapes=[pltpu.SemaphoreType.DMA((3,)), pltpu.SemaphoreType.DMA((3,))],
    )(s1)


def _row_tile(r):
    return _pick(r, 512, 8)


def _pair_sum(name, gb, recv):
    _, r, cols = gb.shape
    tm = _row_tile(r)
    c = lax.axis_index("c").astype(jnp.int32).reshape(1)

    def body(c_ref, a_ref, b_ref, o_ref, ob_ref):
        s = a_ref[...] + b_ref[...]
        o_ref[...] = s
        ob_ref[...] = s.astype(BF16)

    blk = pl.BlockSpec((1, tm, cols), lambda j, i, c_ref: (j, i, 0))
    return pl.pallas_call(
        body, name=name,
        grid_spec=pltpu.PrefetchScalarGridSpec(
            num_scalar_prefetch=1, grid=(4, r // tm),
            in_specs=[pl.BlockSpec((1, tm, cols), lambda j, i, c_ref: (2 * j + c_ref[0], i, 0)), blk],
            out_specs=[blk, blk]),
        out_shape=[jax.ShapeDtypeStruct((4, r, cols), F32), jax.ShapeDtypeStruct((4, r, cols), BF16)],
        compiler_params=pltpu.CompilerParams(dimension_semantics=("parallel", "parallel")),
    )(c, gb, recv)


def _adam(name, parts, w, m, v):
    r, cols = w.shape
    tm = _row_tile(r)
    idx = jnp.stack([jnp.asarray(s, jnp.int32) for _, s in parts])
    n = len(parts)

    def body(idx_ref, *refs):
        g = refs[0][0].astype(F32)
        for p_ref in refs[1:n]:
            g = g + p_ref[0].astype(F32)
        w_ref, m_ref, v_ref, g_out, d_out, m_out, v_out = refs[n:]
        m_new = ADAM_B1 * m_ref[...] + (1.0 - ADAM_B1) * g
        v_new = ADAM_B2 * v_ref[...] + (1.0 - ADAM_B2) * (g * g)
        m_hat = m_new / (1.0 - ADAM_B1 ** ADAM_STEP)
        v_hat = v_new / (1.0 - ADAM_B2 ** ADAM_STEP)
        g_out[...] = g
        d_out[...] = -ADAM_LR * (m_hat / (jnp.sqrt(v_hat) + ADAM_EPS) + ADAM_WD * w_ref[...])
        m_out[...] = m_new
        v_out[...] = v_new

    part_specs = [pl.BlockSpec((1, tm, cols), lambda i, idx_ref, p=p: (idx_ref[p], i, 0)) for p in range(n)]
    flat = pl.BlockSpec((tm, cols), lambda i, idx_ref: (i, 0))
    return pl.pallas_call(
        body, name=name,
        grid_spec=pltpu.PrefetchScalarGridSpec(
            num_scalar_prefetch=1, grid=(r // tm,), in_specs=part_specs + [flat] * 3, out_specs=[flat] * 4),
        out_shape=[jax.ShapeDtypeStruct((r, cols), F32)] * 4,
        compiler_params=pltpu.CompilerParams(dimension_semantics=("parallel",)),
    )(idx, *[a for a, _ in parts], w, m, v)


def _all_gather_many(name, blks):
    n = len(blks)

    def body(*refs):
        x_refs, out_refs = refs[:n], refs[n:2 * n]
        send_sems, recv_sems, local_sems = refs[2 * n:]
        x, y, c = _me()
        me, sib = (x, y, c), (x, y, 1 - c)
        chips = [(x ^ fx, y ^ fy) for fx, fy in _CHIP_FLIPS]

        def slot(a, p):
            return out_refs[a].at[4 * p[0] + 2 * p[1] + p[2]]

        def copy(a, k, block, to, src=None):
            return pltpu.make_async_remote_copy(
                src_ref=slot(a, block) if src is None else src, dst_ref=slot(a, block),
                send_sem=send_sems.at[7 * a + k], recv_sem=recv_sems.at[7 * a + k], device_id=to,
                device_id_type=_MESH)

        mine = [pltpu.make_async_copy(x_refs[a], slot(a, me), local_sems.at[a]) for a in range(n)]
        first = []
        for a in range(n):
            mine[a].start()
            first.append(copy(a, 0, me, sib, src=x_refs[a]))
            first += [copy(a, 1 + j, me, (*chip, c), src=x_refs[a]) for j, chip in enumerate(chips)]
        for cp in first:
            cp.start()
        passed = []
        for j, chip in enumerate(chips):
            for a in range(n):
                copy(a, 1 + j, (*chip, c), me).wait_recv()
                cp = copy(a, 4 + j, (*chip, c), sib)
                cp.start()
                passed.append(cp)
        for a in range(n):
            copy(a, 0, sib, me).wait_recv()
            for j, chip in enumerate(chips):
                copy(a, 4 + j, (*chip, 1 - c), me).wait_recv()
        for cp in first + passed:
            cp.wait_send()
        for cp in mine:
            cp.wait()

    return pl.pallas_call(
        body, name=name, in_specs=[_ANY] * n, out_specs=[_ANY] * n,
        out_shape=[jax.ShapeDtypeStruct((N_DEV,) + b.shape, b.dtype) for b in blks],
        scratch_shapes=[pltpu.SemaphoreType.DMA((7 * n,)), pltpu.SemaphoreType.DMA((7 * n,)),
                        pltpu.SemaphoreType.DMA((n,))],
    )(*blks)


def _rs_sibling_many(name, gbs):
    n = len(gbs)

    def body(*refs):
        g_refs, out_refs = refs[:n], refs[n:2 * n]
        send_sems, recv_sems = refs[2 * n:]
        x, y, c = _me()
        cps = []
        for a in range(n):
            for j in range(4):
                cp = pltpu.make_async_remote_copy(
                    src_ref=g_refs[a].at[2 * j + (1 - c)], dst_ref=out_refs[a].at[j],
                    send_sem=send_sems.at[4 * a + j], recv_sem=recv_sems.at[4 * a + j],
                    device_id=(x, y, 1 - c), device_id_type=_MESH)
                cp.start()
                cps.append(cp)
        for cp in cps:
            cp.wait()

    return pl.pallas_call(
        body, name=name, in_specs=[_ANY] * n, out_specs=[_ANY] * n,
        out_shape=[jax.ShapeDtypeStruct((4,) + g.shape[1:], g.dtype) for g in gbs],
        scratch_shapes=[pltpu.SemaphoreType.DMA((4 * n,)), pltpu.SemaphoreType.DMA((4 * n,))],
    )(*gbs)


def _rs_chips_many(name, s1s):
    n = len(s1s)

    def body(*refs):
        s_refs, out_refs = refs[:n], refs[n:2 * n]
        send_sems, recv_sems = refs[2 * n:]
        x, y, c = _me()
        cps = []
        for a in range(n):
            for k, (fx, fy) in enumerate(_CHIP_FLIPS):
                px, py = x ^ fx, y ^ fy
                cp = pltpu.make_async_remote_copy(
                    src_ref=s_refs[a].at[2 * px + py], dst_ref=out_refs[a].at[k],
                    send_sem=send_sems.at[3 * a + k], recv_sem=recv_sems.at[3 * a + k],
                    device_id=(px, py, c), device_id_type=_MESH)
                cp.start()
                cps.append(cp)
        for cp in cps:
            cp.wait()

    return pl.pallas_call(
        body, name=name, in_specs=[_ANY] * n, out_specs=[_ANY] * n,
        out_shape=[jax.ShapeDtypeStruct((3,) + s.shape[1:], s.dtype) for s in s1s],
        scratch_shapes=[pltpu.SemaphoreType.DMA((3 * n,)), pltpu.SemaphoreType.DMA((3 * n,))],
    )(*s1s)


_HBM = pl.BlockSpec(memory_space=pltpu.HBM)
_SEM = pl.BlockSpec(memory_space=pltpu.SEMAPHORE)
_EFFECT = pltpu.SideEffectType.DATAFLOW_SIDE_EFFECTING


def _push_copies(src_refs, land_refs, send_sems, recv_sems, src_by_peer):
    x, y, c = _me()
    my_id = 4 * x + 2 * y + c
    out = []
    for a in range(len(src_refs)):
        for f in range(1, N_DEV):
            px, py, pc = x ^ (f >> 2), y ^ ((f >> 1) & 1), c ^ (f & 1)
            pid = 4 * px + 2 * py + pc
            src = src_refs[a].at[pid] if src_by_peer else src_refs[a]
            start = pltpu.make_async_remote_copy(
                src_ref=src, dst_ref=land_refs[a].at[my_id], send_sem=send_sems.at[7 * a + f - 1],
                recv_sem=recv_sems.at[7 * a + f - 1], device_id=(px, py, pc), device_id_type=_MESH)
            landed = pltpu.make_async_remote_copy(
                src_ref=src, dst_ref=land_refs[a].at[pid], send_sem=send_sems.at[7 * a + f - 1],
                recv_sem=recv_sems.at[7 * a + f - 1], device_id=(px, py, pc), device_id_type=_MESH)
            out.append((start, landed))
    return out


def _push_start(name, srcs, src_by_peer, after):
    n = len(srcs)
    lands = [jax.ShapeDtypeStruct((N_DEV,) + (s.shape[1:] if src_by_peer else s.shape), s.dtype) for s in srcs]

    def body(*refs):
        src_refs, land_refs = refs[:n], refs[n:2 * n]
        send_sems, recv_sems = refs[2 * n + 1], refs[2 * n + 2]
        token = refs[-1]
        for start, _ in _push_copies(src_refs, land_refs, send_sems, recv_sems, src_by_peer):
            start.start()
        token[...] = jnp.zeros_like(token)

    hbm = lambda a: pltpu.with_memory_space_constraint(a, pltpu.HBM)
    res = pl.pallas_call(
        body, name=name,
        out_shape=(pltpu.SemaphoreType.DMA((7 * n,)), pltpu.SemaphoreType.DMA((7 * n,)),
                   *[pltpu.HBM(s.shape, s.dtype) for s in srcs], *[pltpu.HBM(s.shape, s.dtype) for s in lands],
                   jax.ShapeDtypeStruct((8, LANE), F32)),
        in_specs=[_HBM] * (2 * n) + [_ANY],
        out_specs=(_SEM, _SEM, *[_HBM] * (2 * n), pl.BlockSpec(memory_space=pltpu.VMEM)),
        input_output_aliases={i: 2 + i for i in range(2 * n)},
        compiler_params=pltpu.CompilerParams(has_side_effects=_EFFECT),
    )(*[hbm(s) for s in srcs], *[hbm(lax.empty(s.shape, s.dtype)) for s in lands], after)
    return res[0], res[1], list(res[2:2 + n]), list(res[2 + n:2 + 2 * n]), res[-1]


def _push_wait(name, send_sems, recv_sems, srcs, lands, src_by_peer, after):
    n = len(srcs)

    def body(*refs):
        src_refs, land_refs = refs[:n], refs[n:2 * n]
        s_sems, r_sems = refs[2 * n], refs[2 * n + 1]
        for _, landed in _push_copies(src_refs, land_refs, s_sems, r_sems, src_by_peer):
            landed.wait_send()
            landed.wait_recv()

    res = pl.pallas_call(
        body, name=name,
        out_shape=tuple(pltpu.HBM(s.shape, s.dtype) for s in list(srcs) + list(lands)),
        in_specs=[_HBM] * (2 * n) + [_SEM, _SEM, _ANY],
        out_specs=tuple([_HBM] * (2 * n)),
        input_output_aliases={i: i for i in range(2 * n)},
        compiler_params=pltpu.CompilerParams(has_side_effects=_EFFECT),
    )(*srcs, *lands, send_sems, recv_sems, after)
    return list(res[:n]), list(res[n:])


_SHARDED = (
    ("meta_tokens", 1, (N_META, D_MODEL)),
    ("w_in", 1, (D_MODEL, IN_COLS)),
    ("w_q_b", 1, (Q_LORA, MLA_HEADS * QK_HEAD)),
    ("w_kv_b", 1, (KV_LORA, MLA_HEADS * (QK_NOPE + V_HEAD))),
    ("dn_conv_w", 1, (DN_CONV, 3 * DN_WIDTH)),
    ("w_out", 0, (2 * DN_WIDTH, D_MODEL)),
    ("w_gate", 1, (D_MODEL, D_FF)),
    ("w_up", 1, (D_MODEL, D_FF)),
    ("ffn_conv_w", 1, (FFN_CONV, D_FF)),
    ("w_down", 0, (D_FF, D_MODEL)),
)
_MXU_GATHERED = ("w_in", "w_q_b", "w_kv_b", "w_out", "w_gate", "w_up", "w_down")
_F32_GATHERED = ("meta_tokens", "dn_conv_w", "ffn_conv_w")
_EARLY = ("w_in", "w_q_b", "w_kv_b")
_LATE = ("w_out", "w_gate", "w_up", "w_down")
_REPLICATED = (
    ("attn_norm_w", D_MODEL), ("q_a_norm_w", Q_LORA), ("kv_a_norm_w", KV_LORA), ("q_norm_w", QK_HEAD),
    ("k_norm_w", QK_HEAD), ("mla_out_norm_w", V_HEAD), ("dn_A_log", DN_HEADS), ("dn_dt_bias", DN_HEADS),
    ("dn_out_norm_w", DN_DIM), ("ffn_norm_w", D_MODEL), ("ffn_conv_b", D_FF),
)
_PACK_COLS = 1024
_PACK_ROW_MULT = 320
_SMALL_SHAPE = (8, 768)
_SMALL_BLOCK = (8, 512)


def _local_shape(dim, shape):
    return (shape[0] // N_DEV, shape[1]) if dim == 0 else (shape[0], shape[1] // N_DEV)


def _pack_rows(n, mult):
    rows = -(-n // _PACK_COLS)
    return -(-rows // mult) * mult


def _pack(flats, mult, axis=0):
    cat = jnp.concatenate(flats, axis=-1)
    n = cat.shape[-1]
    r = _pack_rows(n, mult)
    pad = [(0, 0)] * (cat.ndim - 1) + [(0, r * _PACK_COLS - n)]
    return jnp.pad(cat, pad).reshape(cat.shape[:-1] + (r, _PACK_COLS))


def _to_blocks(full, dim):
    r, c = full.shape
    if dim == 0:
        return full.reshape(N_DEV, (r // N_DEV) * c)
    return full.reshape(r, N_DEV, c // N_DEV).transpose(1, 0, 2).reshape(N_DEV, r * (c // N_DEV))


def _from_blocks(blocks, dim, shape):
    r, c = shape
    if dim == 0:
        return blocks.reshape(r, c)
    return blocks.reshape(N_DEV, r, c // N_DEV).transpose(1, 0, 2).reshape(r, c)


def _split(flat, sizes):
    out, o = [], 0
    for s in sizes:
        out.append(flat[..., o:o + s])
        o += s
    return out


def _gather_weights(local, names, dtype, mult):
    specs = [s for s in _SHARDED if s[0] in names]
    pack = _pack([local[n].astype(dtype).reshape(-1) for n, _, _ in specs], mult)
    got = _all_gather("gather_" + "_".join(n[:5] for n in names[:2]), pack)
    flat = got.reshape(N_DEV, -1)
    sizes = [math.prod(_local_shape(d, s)) for _, d, s in specs]
    return {n: _from_blocks(p, d, s) for (n, d, s), p in zip(specs, _split(flat, sizes))}


def kernel(x, meta_tokens, attn_norm_w, w_in, q_a_norm_w, w_q_b, kv_a_norm_w, w_kv_b, q_norm_w, k_norm_w, mla_out_norm_w, dn_conv_w, dn_A_log, dn_dt_bias, dn_out_norm_w, w_out, ffn_norm_w, w_gate, w_up, ffn_conv_w, ffn_conv_b, w_down, loss_target, m_meta_tokens, m_attn_norm_w, m_w_in, m_q_a_norm_w, m_w_q_b, m_kv_a_norm_w, m_w_kv_b, m_q_norm_w, m_k_norm_w, m_mla_out_norm_w, m_dn_conv_w, m_dn_A_log, m_dn_dt_bias, m_dn_out_norm_w, m_w_out, m_ffn_norm_w, m_w_gate, m_w_up, m_ffn_conv_w, m_ffn_conv_b, m_w_down, v_meta_tokens, v_attn_norm_w, v_w_in, v_q_a_norm_w, v_w_q_b, v_kv_a_norm_w, v_w_kv_b, v_q_norm_w, v_k_norm_w, v_mla_out_norm_w, v_dn_conv_w, v_dn_A_log, v_dn_dt_bias, v_dn_out_norm_w, v_w_out, v_ffn_norm_w, v_w_gate, v_w_up, v_ffn_conv_w, v_ffn_conv_b, v_w_down):
    names = [n for n, _, _ in _SHARDED] + [n for n, _ in _REPLICATED]
    given = dict(locals())
    two_d = lambda a: a.reshape(a.shape[-2:])
    wl = {n: two_d(given[n]) for n in names}
    ml = {n: two_d(given["m_" + n]) for n in names}
    vl = {n: two_d(given["v_" + n]) for n in names}
    out_shapes = {n: given[n].shape for n in names}

    spec = {n: (d, s) for n, d, s in _SHARDED}
    small_sizes = [math.prod(_local_shape(*spec[n])) for n in _F32_GATHERED]

    def small_block(d):
        cat = jnp.concatenate([d[n].reshape(d[n].shape[:-2] + (-1,)) for n in _F32_GATHERED], axis=-1)
        pad = [(0, 0)] * (cat.ndim - 1) + [(0, math.prod(_SMALL_BLOCK) - cat.shape[-1])]
        return jnp.pad(cat, pad).reshape(cat.shape[:-1] + _SMALL_BLOCK)

    def from_slots(n, blocks):
        d, s = spec[n]
        return blocks.reshape(s) if d == 0 else blocks.transpose(1, 0, 2).reshape(s)

    my_id = 4 * lax.axis_index("x") + 2 * lax.axis_index("y") + lax.axis_index("c")
    got = _all_gather_many("gather_early", [wl[n].astype(_MXU) for n in _EARLY] + [small_block(wl)])
    full = {n: a for n, a in wl.items() if n not in _LATE}
    for n, blocks in zip(_EARLY, got):
        full[n] = from_slots(n, blocks)
    for n, p in zip(_F32_GATHERED, _split(got[-1].reshape(N_DEV, -1), small_sizes)):
        full[n] = _from_blocks(p, *spec[n])
    late_own = [wl[n].astype(_MXU) for n in _LATE]
    l_send, l_recv, l_src, l_land, token = _push_start("gather_late_start", late_own, False, got[-1])

    def late_weights(after):
        _, lands = _push_wait("gather_late_wait", l_send, l_recv, l_src, l_land, False, after)
        out = {}
        for n, land, own in zip(_LATE, lands, late_own):
            out[n] = from_slots(n, lax.dynamic_update_slice(land, own[None], (my_id, 0, 0))).astype(_MXU)
        return out

    def dest_blocks(n, a):
        d, s = spec[n]
        r, c = _local_shape(d, s)
        return a.reshape(N_DEV, r, c) if d == 0 else a.reshape(r, N_DEV, c).transpose(1, 0, 2)

    pushed = []

    def grads_ready(g, names):
        nat = _grads_to_natural({n: g[n] for n in names})
        blocks = [dest_blocks(n, nat[n]).astype(_MXU) for n in names]
        sends, recvs, srcs, lands, tok = _push_start("rs_" + names[0] + "_start", blocks, True, token)
        pushed.append((names, sends, recvs, srcs, lands))
        return tok

    seq = x.shape[1]
    tp = ROW0 + seq
    h0 = jnp.concatenate([jnp.zeros((PAD, D_MODEL), F32), full["meta_tokens"], x[0]], axis=0)
    tgt = jnp.concatenate([jnp.zeros((ROW0, D_MODEL), F32), loss_target[0]], axis=0)
    loss, dh0, g = _local_step(h0, tgt, _prepare(full, tp), token, late_weights, grads_ready)
    g = _grads_to_natural(g)
    g["meta_tokens"] = dh0[PAD:ROW0]
    grad_x = dh0[ROW0:][None]

    big = [{}, {}, {}, {}]
    last = ["w_in", "small"]
    gbs = {n: dest_blocks(n, g[n]) for n in ("w_in",) + _F32_GATHERED}
    gb = [gbs["w_in"], small_block(gbs)]
    from_sib = _rs_sibling_many("rs_sibling", gb)
    sums = [_pair_sum("rs_pair_sum_" + n, a, b) for n, a, b in zip(last, gb, from_sib)]
    from_chips = _rs_chips_many("rs_chips", [sb for _, sb in sums])
    my_chip = 2 * lax.axis_index("x") + lax.axis_index("y")
    for n, (s1, _), fc in zip(last, sums, from_chips):
        loc = (lambda d: small_block(d)) if n == "small" else (lambda d, n=n: d[n])
        res = _adam("adam_" + n, [(s1, my_chip), (fc, 0), (fc, 1), (fc, 2)], loc(wl), loc(ml), loc(vl))
        for kind, a in enumerate(res):
            if n == "small":
                big[kind].update(zip(_F32_GATHERED, _split(a.reshape(-1), small_sizes)))
            else:
                big[kind][n] = a
    for names, sends, recvs, srcs, lands in pushed:
        srcs, lands = _push_wait("rs_" + names[0] + "_wait", sends, recvs, srcs, lands, True, dh0)
        for n, src, land in zip(names, srcs, lands):
            parts = [(src, my_id)] + [(land, my_id ^ f) for f in range(1, N_DEV)]
            for kind, a in enumerate(_adam("adam_" + n, parts, wl[n], ml[n], vl[n])):
                big[kind][n] = a

    def small(d, extra):
        cat = jnp.concatenate([d[n].reshape(-1) for n, _ in _REPLICATED] + [extra])
        return jnp.pad(cat, (0, math.prod(_SMALL_SHAPE) - cat.shape[0])).reshape(_SMALL_SHAPE)

    zero1 = jnp.zeros((1,), F32)
    parts = _all_gather("gather_small_grads", small(g, loss[0, :1]))
    sm = _adam("adam_replicated", [(parts, d) for d in range(N_DEV)], small(wl, zero1), small(ml, zero1),
               small(vl, zero1))
    rsizes = [n for _, n in _REPLICATED] + [1]
    sm = [dict(zip([n for n, _ in _REPLICATED] + ["loss"], _split(a.reshape(-1), rsizes))) for a in sm]

    outs = [sm[0]["loss"].reshape(()), grad_x]
    for kind in range(4):
        for n in ("meta_tokens", "attn_norm_w", "w_in", "q_a_norm_w", "w_q_b", "kv_a_norm_w", "w_kv_b", "q_norm_w",
                  "k_norm_w", "mla_out_norm_w", "dn_conv_w", "dn_A_log", "dn_dt_bias", "dn_out_norm_w", "w_out",
                  "ffn_norm_w", "w_gate", "w_up", "ffn_conv_w", "ffn_conv_b", "w_down"):
            src = big[kind] if n in big[kind] else sm[kind]
            outs.append(src[n].reshape(out_shapes[n]))
    return tuple(outs)
```

```python
import functools
import math

import jax
import jax.numpy as jnp
from jax import lax
from jax.experimental import pallas as pl
from jax.experimental.pallas import tpu as pltpu

F32 = jnp.float32
BF16 = jnp.bfloat16
_MXU = jnp.bfloat16
_HI = lax.Precision.HIGHEST

D_MODEL = 1024
N_META = 16
PAD = 112
ROW0 = PAD + N_META
MLA_HEADS = 4
QK_NOPE = 128
QK_ROPE = 64
QK_HEAD = QK_NOPE + QK_ROPE
V_HEAD = 128
Q_LORA = 256
KV_LORA = 256
ROPE_THETA = 10000.0
DN_HEADS = 4
DN_DIM = 128
DN_WIDTH = DN_HEADS * DN_DIM
DN_CONV = 4
DN_CHUNK = 64
D_FF = 2816
FFN_CONV = 3
EPS = 1e-6
HP = 256
C_QKV = 0
C_Z = 1536
C_QL = 2048
C_KVL = 2304
C_KPE = 2560
C_AB = 2688
IN_P = 2816
IN_COLS = 2632

ADAM_LR = 0.001
ADAM_B1 = 0.9
ADAM_B2 = 0.999
ADAM_EPS = 1e-08
ADAM_WD = 0.01
ADAM_STEP = 10

N_DEV = 8
TM = 128
LANE = 128
VMEM_LIMIT = 56 * 1024 * 1024
NEG = -1e30


def _dot(a, b, dims, hp=False):
    if hp:
        return lax.dot_general(a.astype(F32), b.astype(F32), (dims, ((), ())),
                               precision=lax.Precision.HIGH if hp == "3x" else _HI, preferred_element_type=F32)
    return lax.dot_general(a.astype(_MXU), b.astype(_MXU), (dims, ((), ())),
                           preferred_element_type=F32)


def _nn(a, b, hp=False):
    return _dot(a, b, ((1,), (0,)), hp)


def _nt(a, b, hp=False):
    return _dot(a, b, ((1,), (1,)), hp)


def _tn(a, b, hp=False):
    return _dot(a, b, ((0,), (0,)), hp)


def _sigmoid(x):
    return 1.0 / (1.0 + jnp.exp(-x))


def _rms_fwd(x, w, n):
    r = lax.rsqrt(jnp.sum(x * x, axis=-1, keepdims=True) * (1.0 / n) + EPS)
    return x * r * w, r


def _rms_bwd(x, w, dy, n):
    r = lax.rsqrt(jnp.sum(x * x, axis=-1, keepdims=True) * (1.0 / n) + EPS)
    xh = x * r
    gy = dy * w
    dx = r * (gy - xh * (jnp.sum(gy * xh, axis=-1, keepdims=True) * (1.0 / n)))
    return dx, dy * xh


def _rowsum(x):
    return jnp.sum(x, axis=0, keepdims=True)


def _row_ids(i, tm):
    return i * tm + lax.broadcasted_iota(jnp.int32, (tm, 1), 0)


def _shift_down(ext, s, tm):
    if s == 0:
        return ext[8:8 + tm]
    return pltpu.roll(ext, s, 0)[8:8 + tm]


def _shift_up(ext, s, tm):
    if s == 0:
        return ext[0:tm]
    return pltpu.roll(ext, tm + 8 - s, 0)[0:tm]


def _conv_fwd(x, halo_prev, w, width):
    tm = x.shape[0]
    ext = jnp.concatenate([halo_prev, x], axis=0)
    y = None
    for j in range(width):
        t = w[j:j + 1, :] * _shift_down(ext, width - 1 - j, tm)
        y = t if y is None else y + t
    return y


def _conv_bwd_x(dy, halo_next, w, width):
    tm = dy.shape[0]
    ext = jnp.concatenate([dy, halo_next], axis=0)
    dx = None
    for j in range(width):
        t = w[j:j + 1, :] * _shift_up(ext, width - 1 - j, tm)
        dx = t if dx is None else dx + t
    return dx


def _conv_bwd_w(dy, x, halo_prev, width):
    tm = dy.shape[0]
    ext = jnp.concatenate([halo_prev, x], axis=0)
    rows = [_rowsum(dy * _shift_down(ext, width - 1 - j, tm)) for j in range(width)]
    rows += [jnp.zeros_like(rows[0])] * (8 - width)
    return jnp.concatenate(rows, axis=0)


def _softplus(x):
    e = jnp.exp(-jnp.abs(x))
    u = 1.0 + e
    l1p = jnp.where(u == 1.0, e, jnp.log(u) * e / jnp.where(u == 1.0, 1.0, u - 1.0))
    return jnp.maximum(x, 0.0) + l1p


def _swap_halves(x):
    lane = lax.broadcasted_iota(jnp.int32, x.shape, 1)
    return jnp.where(lane < 32, pltpu.roll(x, 96, 1), jnp.where(lane < 64, pltpu.roll(x, 32, 1), 0.0))


class _In:
    def __init__(self, arr, width=None, cb=0, kind="cur"):
        self.arr, self.kind = arr, kind
        self.width = arr.shape[1] if width is None else width
        self.cb = cb


def _rows(name, fn, tiled, full, outs, accs=(), tm=TM):
    tp = tiled[0].arr.shape[0]
    nt = tp // tm
    r8 = tm // 8
    n_in = len(tiled) + len(full)
    n_out = len(outs)

    def body(*refs):
        i = pl.program_id(0)
        vals = [r[...] for r in refs[:n_in]]
        o_t, o_a = fn(i, *vals)
        for r, v in zip(refs[n_in:n_in + n_out], o_t):
            r[...] = v.astype(r.dtype)
        for r, v in zip(refs[n_in + n_out:], o_a):
            @pl.when(i == 0)
            def _():
                r[...] = v

            @pl.when(i > 0)
            def _():
                r[...] += v

    def spec(t):
        if t.kind == "cur":
            return pl.BlockSpec((tm, t.width), lambda i, cb=t.cb: (i, cb))
        if t.kind == "prev":
            return pl.BlockSpec((8, t.width), lambda i, cb=t.cb: (jnp.maximum(i * r8 - 1, 0), cb))
        return pl.BlockSpec((8, t.width), lambda i, cb=t.cb: (jnp.minimum((i + 1) * r8, tp // 8 - 1), cb))

    in_specs = [spec(t) for t in tiled]
    in_specs += [pl.BlockSpec(a.shape, lambda i, nd=a.ndim: (0,) * nd) for a in full]
    out_specs = [pl.BlockSpec((tm, w), lambda i: (i, 0)) for w, _ in outs]
    out_specs += [pl.BlockSpec((r, w), lambda i: (0, 0)) for r, w in accs]
    out_shape = [jax.ShapeDtypeStruct((tp, w), dt) for w, dt in outs]
    out_shape += [jax.ShapeDtypeStruct((r, w), F32) for r, w in accs]
    res = pl.pallas_call(
        body, name=name, grid=(nt,), in_specs=in_specs, out_specs=out_specs, out_shape=out_shape,
        compiler_params=pltpu.CompilerParams(dimension_semantics=("arbitrary",), vmem_limit_bytes=VMEM_LIMIT),
    )(*[t.arr for t in tiled], *full)
    return res


def _pick(n, cap, mult):
    best = None
    for d in range(mult, min(n, cap) + 1, mult):
        if n % d == 0:
            best = d
    assert best is not None, (n, cap, mult)
    return best


_ANY_SPEC = pl.BlockSpec(memory_space=pl.ANY)


def _mm(name, a, b, mode, out_dtype=F32, resid=None, after=None):
    if mode == "tn":
        m, k = a.shape
        n = b.shape[1]
        tk = _pick(k, 512, 128)
        tn = _pick(n, 1408, 128)

        def body_tn(a_ref, b_ref, o_ref):
            o_ref[...] = _tn(a_ref[...], b_ref[...]).astype(o_ref.dtype)

        return pl.pallas_call(
            body_tn, name=name, grid=(n // tn, k // tk),
            in_specs=[pl.BlockSpec((m, tk), lambda j, p: (0, p)),
                      pl.BlockSpec((m, tn), lambda j, p: (0, j))],
            out_specs=pl.BlockSpec((tk, tn), lambda j, p: (p, j)),
            out_shape=jax.ShapeDtypeStruct((k, n), out_dtype),
            compiler_params=pltpu.CompilerParams(
                dimension_semantics=("parallel", "parallel"), vmem_limit_bytes=VMEM_LIMIT),
        )(a, b)

    m, k = a.shape
    n = b.shape[1] if mode == "nn" else b.shape[0]
    tn = _pick(n, 1408, 128)
    tm = _pick(m, 1152, 16)
    dotf = _nn if mode == "nn" else _nt

    def body(*refs):
        a_ref, b_ref, o_ref = refs[0], refs[1], refs[-1]
        acc = dotf(a_ref[...], b_ref[...])
        if resid is not None:
            acc = refs[2][...] + acc
        o_ref[...] = acc.astype(o_ref.dtype)

    b_spec = (pl.BlockSpec((k, tn), lambda j, i: (0, j)) if mode == "nn"
              else pl.BlockSpec((tn, k), lambda j, i: (j, 0)))
    in_specs = [pl.BlockSpec((tm, k), lambda j, i: (i, 0)), b_spec]
    args = [a, b]
    if resid is not None:
        in_specs.append(pl.BlockSpec((tm, tn), lambda j, i: (i, j)))
        args.append(resid)
    if after is not None:
        in_specs.append(_ANY_SPEC)
        args.append(after)
    return pl.pallas_call(
        body, name=name, grid=(n // tn, m // tm), in_specs=in_specs,
        out_specs=pl.BlockSpec((tm, tn), lambda j, i: (i, j)),
        out_shape=jax.ShapeDtypeStruct((m, n), out_dtype),
        compiler_params=pltpu.CompilerParams(
            dimension_semantics=("parallel", "parallel"), vmem_limit_bytes=VMEM_LIMIT),
    )(*args)


ATTN_Q_TILES = 4


def _attn_probs(q, k, row0):
    tq, tp = q.shape[0], k.shape[0]
    s = _nt(q, k) * (1.0 / math.sqrt(QK_HEAD))
    row = row0 + lax.broadcasted_iota(jnp.int32, (tq, tp), 0)
    col = lax.broadcasted_iota(jnp.int32, (tq, tp), 1)
    ok = (col <= row) & (col >= PAD)
    s = jnp.where(ok, s, NEG)
    m = jnp.max(s, axis=-1, keepdims=True)
    e = jnp.exp(s - m)
    e = jnp.where(ok, e, 0.0)
    l = jnp.sum(e, axis=-1, keepdims=True)
    return e / jnp.maximum(l, 1e-30)


def _attn_fwd(q, k, v):
    tp = q.shape[0]
    tq = tp // ATTN_Q_TILES

    def body(q_ref, k_ref, v_ref, o_ref):
        for i in range(ATTN_Q_TILES):
            rows = slice(i * tq, (i + 1) * tq)
            keys = slice(0, (i + 1) * tq)
            p = _attn_probs(q_ref[rows, :], k_ref[keys, :], i * tq)
            o_ref[rows, :] = _nn(p, v_ref[keys, :])

    return pl.pallas_call(
        body, name="attn_fwd", grid=(MLA_HEADS,),
        in_specs=[pl.BlockSpec((tp, HP), lambda h: (0, h)),
                  pl.BlockSpec((tp, HP), lambda h: (0, h)),
                  pl.BlockSpec((tp, V_HEAD), lambda h: (0, h))],
        out_specs=pl.BlockSpec((tp, V_HEAD), lambda h: (0, h)),
        out_shape=jax.ShapeDtypeStruct((tp, MLA_HEADS * V_HEAD), F32),
        compiler_params=pltpu.CompilerParams(dimension_semantics=("parallel",), vmem_limit_bytes=VMEM_LIMIT),
    )(q, k, v)


def _attn_bwd(q, k, v, do):
    tp = q.shape[0]
    tq = tp // ATTN_Q_TILES

    def body(q_ref, k_ref, v_ref, do_ref, dq_ref, dk_ref, dv_ref):
        for i in reversed(range(ATTN_Q_TILES)):
            rows = slice(i * tq, (i + 1) * tq)
            keys = slice(0, (i + 1) * tq)
            qb = q_ref[rows, :]
            kk = k_ref[keys, :]
            dob = do_ref[rows, :]
            p = _attn_probs(qb, kk, i * tq)
            dp = _nt(dob, v_ref[keys, :])
            delta = jnp.sum(p * dp, axis=-1, keepdims=True)
            ds = p * (dp - delta) * (1.0 / math.sqrt(QK_HEAD))
            dq_ref[rows, :] = _nn(ds, kk)
            if i == ATTN_Q_TILES - 1:
                dk_ref[...] = _tn(ds, qb)
                dv_ref[...] = _tn(p, dob)
            else:
                dk_ref[keys, :] += _tn(ds, qb)
                dv_ref[keys, :] += _tn(p, dob)

    full = lambda w: pl.BlockSpec((tp, w), lambda h: (0, h))
    return pl.pallas_call(
        body, name="attn_bwd", grid=(MLA_HEADS,),
        in_specs=[full(HP), full(HP), full(V_HEAD), full(V_HEAD)],
        out_specs=[full(HP), full(HP), full(V_HEAD)],
        out_shape=[jax.ShapeDtypeStruct((tp, MLA_HEADS * HP), F32),
                   jax.ShapeDtypeStruct((tp, MLA_HEADS * HP), F32),
                   jax.ShapeDtypeStruct((tp, MLA_HEADS * V_HEAD), F32)],
        compiler_params=pltpu.CompilerParams(dimension_semantics=("parallel",), vmem_limit_bytes=VMEM_LIMIT),
    )(q, k, v, do)


def _gdn_consts():
    c = DN_CHUNK
    r = lax.broadcasted_iota(jnp.int32, (c, c), 0)
    cc = lax.broadcasted_iota(jnp.int32, (c, c), 1)
    incl = r >= cc
    strict = r > cc
    return incl, strict


def _each(fn, *lists):
    return [fn(*a) for a in zip(*lists)]


def _interleave(chains):
    chains = list(chains)
    while chains:
        for ch in list(chains):
            try:
                next(ch)
            except StopIteration:
                chains.remove(ch)


def _gdn_chunk_common(q_ref, k_ref, v_ref, g_ref, b_ref):
    c = DN_CHUNK
    incl, strict = _gdn_consts()
    sls = [slice(DN_DIM * h, DN_DIM * (h + 1)) for h in range(DN_HEADS)]
    inclf = incl.astype(F32)
    ones = jnp.full((c, LANE), 1.0 / LANE, F32)
    q = [q_ref[:, sl] * (1.0 / math.sqrt(DN_DIM)) for sl in sls]
    k = [k_ref[:, sl] for sl in sls]
    v = [v_ref[:, sl] for sl in sls]
    g = [g_ref[:, sl] for sl in sls]
    beta = [b_ref[:, sl] for sl in sls]
    gc = [_nn(inclf, x, hp=True) for x in g]
    grow = [_nt(ones, x, hp=True) for x in gc]
    kb = _each(jnp.multiply, k, beta)
    kk = _each(_nt, kb, k)
    qk = _each(_nt, q, k)
    gam = [jnp.exp(x) for x in gc]
    g_last = [_rowsum(x) for x in g]
    dm = [jnp.exp(jnp.where(incl, x[:, :c] - y, NEG)) for x, y in zip(gc, grow)]
    vb = _each(jnp.multiply, v, beta)
    kbg = _each(jnp.multiply, kb, gam)
    ek = [jnp.exp(x - y) for x, y in zip(g_last, gc)]
    kd = _each(jnp.multiply, k, ek)
    return dict(q=q, k=k, v=v, beta=beta, gc=gc, gam=gam, g_last=g_last, dm=dm, kb=kb, vb=vb,
                kbg=kbg, kk=kk, ek=ek, kd=kd, qk=qk, incl=incl, strict=strict, sls=sls)


def _gdn_fwd(q, k, v, g, beta):
    tp = q.shape[0]
    c = DN_CHUNK
    nch = tp // c

    def body(q_ref, k_ref, v_ref, g_ref, b_ref, o_ref, s_ref, t_ref, s_scr):
        @pl.when(pl.program_id(0) == 0)
        def _():
            s_scr[...] = jnp.zeros_like(s_scr)

        eye = (lax.broadcasted_iota(jnp.int32, (c, c), 0) == lax.broadcasted_iota(jnp.int32, (c, c), 1)).astype(F32)
        x = _gdn_chunk_common(q_ref, k_ref, v_ref, g_ref, b_ref)
        heads = range(DN_HEADS)
        s = [s_scr[h] for h in heads]
        bp = [-jnp.where(x["strict"], kk * dm, 0.0) for kk, dm in zip(x["kk"], x["dm"])]
        t = [eye + b for b in bp]
        for _ in range(5):
            bp = [_nn(b, b, hp="3x") for b in bp]
            t = [tt + _nn(tt, b, hp="3x") for tt, b in zip(t, bp)]
        u = _each(_nn, t, x["vb"])
        w = _each(_nn, t, x["kbg"])
        v_new = [uu - _nn(ww, ss) for uu, ww, ss in zip(u, w, s)]
        o = [_nn(q * gam, ss) + _nn(qk * dm, vn)
             for q, gam, ss, qk, dm, vn in zip(x["q"], x["gam"], s, x["qk"], x["dm"], v_new)]
        s_new = [ss * jnp.exp(gl) + _tn(kd, vn) for ss, gl, kd, vn in zip(s, x["g_last"], x["kd"], v_new)]
        for h in heads:
            s_ref[h, 0] = s[h]
            t_ref[h, 0] = t[h]
            o_ref[:, x["sls"][h]] = o[h]
            s_scr[h] = s_new[h]

    rb = lambda n: (n, 0)
    return pl.pallas_call(
        body, name="gdn_fwd", grid=(nch,),
        in_specs=[pl.BlockSpec((c, DN_WIDTH), rb)] * 5,
        out_specs=[pl.BlockSpec((c, DN_WIDTH), rb),
                   pl.BlockSpec((DN_HEADS, 1, DN_DIM, DN_DIM), lambda n: (0, n, 0, 0)),
                   pl.BlockSpec((DN_HEADS, 1, c, c), lambda n: (0, n, 0, 0))],
        out_shape=[jax.ShapeDtypeStruct((tp, DN_WIDTH), F32),
                   jax.ShapeDtypeStruct((DN_HEADS, nch, DN_DIM, DN_DIM), F32),
                   jax.ShapeDtypeStruct((DN_HEADS, nch, c, c), F32)],
        scratch_shapes=[pltpu.VMEM((DN_HEADS, DN_DIM, DN_DIM), F32)],
        compiler_params=pltpu.CompilerParams(dimension_semantics=("arbitrary",), vmem_limit_bytes=VMEM_LIMIT),
    )(q, k, v, g, beta)


def _gdn_bwd(q, k, v, g, beta, s_all, t_all, do):
    tp = q.shape[0]
    c = DN_CHUNK
    nch = tp // c

    def body(q_ref, k_ref, v_ref, g_ref, b_ref, s_ref, t_ref, do_ref,
             dq_ref, dk_ref, dv_ref, dg_ref, db_ref, ds_scr):
        @pl.when(pl.program_id(0) == 0)
        def _():
            ds_scr[...] = jnp.zeros_like(ds_scr)

        ones_cl = jnp.ones((c, LANE), F32)
        xs = _gdn_chunk_common(q_ref, k_ref, v_ref, g_ref, b_ref)
        upper = jnp.logical_not(xs["strict"]).astype(F32)

        def chain(h):
            x = {key: (val[h] if isinstance(val, list) else val) for key, val in xs.items()}
            sl = x["sls"]
            qs, kx, vx, beta_, gam, dm = x["q"], x["k"], x["v"], x["beta"], x["gam"], x["dm"]
            kb, vb, kbg, kd, ek = x["kb"], x["vb"], x["kbg"], x["kd"], x["ek"]
            t = t_ref[h, 0]
            s = s_ref[h, 0]
            dsn = ds_scr[h]
            dob = do_ref[:, sl]
            eg_last = jnp.exp(x["g_last"])
            u = _nn(t, vb)
            w = _nn(t, kbg)
            mqk = x["qk"] * dm
            qd = qs * gam
            dqd = _nt(dob, s)
            dkd_pre = _nn(kd, dsn)
            yield
            v_new = u - _nn(w, s)
            dv_new = _tn(mqk, dob) + dkd_pre
            dq = dqd * gam
            dgam = jnp.sum(dqd * qs, axis=1, keepdims=True)
            yield
            ds_new = _tn(qd, dob) + eg_last * dsn - _tn(w, dv_new)
            dmm = jnp.where(x["incl"], _nt(dob, v_new), 0.0)
            dkd = _nt(v_new, dsn)
            dw = -_nt(dv_new, s)
            dvb = _tn(t, dv_new)
            dt = _nt(dv_new, vb)
            yield
            dqk = dmm * dm
            e_mat = dmm * mqk
            dq = dq + _nn(dqk, kx)
            dk = _tn(dqk, qs) + dkd * ek
            e1 = jnp.sum(dkd * kd, axis=1, keepdims=True)
            dgc = -e1
            dg_last = jnp.sum(e1) + eg_last * jnp.sum(s * dsn)
            dt = dt + _nt(dw, kbg)
            dkbg = _tn(t, dw)
            yield
            tdt = _tn(t, dt, hp="3x")
            yield
            da = jnp.where(x["strict"], -_nt(tdt, t, hp="3x"), 0.0)
            yield
            dkk = da * dm
            e_mat = e_mat + da * x["kk"] * dm
            dkb = _nn(dkk, kx) + dkbg * gam
            dk = dk + _tn(dkk, kb)
            dgam = dgam + jnp.sum(dkbg * kb, axis=1, keepdims=True)
            yield
            dk = dk + dkb * beta_
            dbeta = jnp.sum(dkb * kx, axis=1, keepdims=True) + jnp.sum(dvb * vx, axis=1, keepdims=True)
            dv = dvb * beta_
            dgc = dgc + jnp.sum(e_mat, axis=1, keepdims=True) + dgam * gam
            dgc = dgc - _tn(e_mat, ones_cl, hp="3x")
            yield
            dg = _nn(upper, dgc, hp="3x") + dg_last
            yield
            ds_scr[h] = ds_new
            dq_ref[:, sl] = dq * (1.0 / math.sqrt(DN_DIM))
            dk_ref[:, sl] = dk
            dv_ref[:, sl] = dv
            dg_ref[:, sl] = dg
            db_ref[:, sl] = jnp.broadcast_to(dbeta, (c, LANE))

        _interleave([chain(h) for h in range(DN_HEADS)])

    rb = lambda n: (nch - 1 - n, 0)
    hs = lambda n: (0, nch - 1 - n, 0, 0)
    return pl.pallas_call(
        body, name="gdn_bwd", grid=(nch,),
        in_specs=[pl.BlockSpec((c, DN_WIDTH), rb)] * 5
        + [pl.BlockSpec((DN_HEADS, 1, DN_DIM, DN_DIM), hs), pl.BlockSpec((DN_HEADS, 1, c, c), hs),
           pl.BlockSpec((c, DN_WIDTH), rb)],
        out_specs=[pl.BlockSpec((c, DN_WIDTH), rb)] * 5,
        out_shape=[jax.ShapeDtypeStruct((tp, DN_WIDTH), F32)] * 5,
        scratch_shapes=[pltpu.VMEM((DN_HEADS, DN_DIM, DN_DIM), F32)],
        compiler_params=pltpu.CompilerParams(dimension_semantics=("arbitrary",), vmem_limit_bytes=VMEM_LIMIT),
    )(q, k, v, g, beta, s_all, t_all, do)


def _silu_parts(x):
    s = _sigmoid(x)
    return x * s, s * (1.0 + x * (1.0 - s))


def _f_rms_cast(i, x, w):
    y, _ = _rms_fwd(x, w, x.shape[1])
    return (y,), ()


def _f_rms_bwd_add(i, x, dy, dres, w, *, mask_pad):
    dx, dwr = _rms_bwd(x, w, dy, x.shape[1])
    out = dres + dx
    if mask_pad:
        out = jnp.where(_row_ids(i, x.shape[0]) >= PAD, out, 0.0)
    return (out,), (_rowsum(dwr),)


def _f_lat_norm(i, ql, kvl, qw, kvw):
    return (_rms_fwd(ql, qw, Q_LORA)[0], _rms_fwd(kvl, kvw, KV_LORA)[0]), ()


def _f_lat_norm_bwd(i, ql, kvl, dqn, dkvn, qw, kvw):
    dq, dqw = _rms_bwd(ql, qw, dqn, Q_LORA)
    dk, dkw = _rms_bwd(kvl, kvw, dkvn, KV_LORA)
    return (dq, dk), (_rowsum(dqw), _rowsum(dkw))


def _rope(x, cos, sin_s):
    return x * cos + _swap_halves(x) * sin_s


def _rope_t(dy, cos, sin_s):
    return dy * cos + _swap_halves(dy * sin_s)


def _f_mla_qk(i, qf, kvf, kpe, cos, sin_s, qw, kw):
    qs, ks, vs = [], [], []
    for h in range(MLA_HEADS):
        qn, _ = _rms_fwd(qf[:, HP * h:HP * (h + 1)], qw, QK_HEAD)
        qs += [qn[:, :QK_NOPE], _rope(qn[:, QK_NOPE:], cos, sin_s)]
        kh = jnp.concatenate([kvf[:, HP * h:HP * h + QK_NOPE], kpe], axis=1)
        kn, _ = _rms_fwd(kh, kw, QK_HEAD)
        ks += [kn[:, :QK_NOPE], _rope(kn[:, QK_NOPE:], cos, sin_s)]
        vs.append(kvf[:, HP * h + QK_NOPE:HP * (h + 1)])
    return (jnp.concatenate(qs, axis=1), jnp.concatenate(ks, axis=1), jnp.concatenate(vs, axis=1)), ()


def _f_mla_qk_bwd(i, qf, kvf, kpe, cos, sin_s, dq, dk, dv, qw, kw):
    dqf, dkvf = [], []
    dkpe = None
    dqw = None
    dkw = None
    for h in range(MLA_HEADS):
        dqh = dq[:, HP * h:HP * (h + 1)]
        dqn = jnp.concatenate([dqh[:, :QK_NOPE], _rope_t(dqh[:, QK_NOPE:], cos, sin_s)], axis=1)
        dx, dwr = _rms_bwd(qf[:, HP * h:HP * (h + 1)], qw, dqn, QK_HEAD)
        dqf.append(dx)
        dqw = _rowsum(dwr) if dqw is None else dqw + _rowsum(dwr)
        dkh = dk[:, HP * h:HP * (h + 1)]
        dkn = jnp.concatenate([dkh[:, :QK_NOPE], _rope_t(dkh[:, QK_NOPE:], cos, sin_s)], axis=1)
        kh = jnp.concatenate([kvf[:, HP * h:HP * h + QK_NOPE], kpe], axis=1)
        dx, dwr = _rms_bwd(kh, kw, dkn, QK_HEAD)
        dkvf += [dx[:, :QK_NOPE], dv[:, V_HEAD * h:V_HEAD * (h + 1)]]
        dkpe = dx[:, QK_NOPE:] if dkpe is None else dkpe + dx[:, QK_NOPE:]
        dkw = _rowsum(dwr) if dkw is None else dkw + _rowsum(dwr)
    return (jnp.concatenate(dqf, axis=1), jnp.concatenate(dkvf, axis=1), dkpe), (dqw, dkw)


def _gdn_act(i, x, halo, w8):
    tm = x.shape[0]
    halo = jnp.where(i > 0, halo, 0.0)
    c = _conv_fwd(x, halo, w8, DN_CONV)
    act, dact = _silu_parts(c)
    return act, dact


def _f_gdn_prep(i, x, halo, ab, w8, alog, dtb, sel):
    tm = x.shape[0]
    act, _ = _gdn_act(i, x, halo, w8)
    outs = []
    for part in range(2):
        for h in range(DN_HEADS):
            t = act[:, DN_WIDTH * part + DN_DIM * h:DN_WIDTH * part + DN_DIM * (h + 1)]
            outs.append(t * lax.rsqrt(jnp.sum(t * t, axis=-1, keepdims=True) + EPS))
    q = jnp.concatenate(outs[:DN_HEADS], axis=1)
    k = jnp.concatenate(outs[DN_HEADS:], axis=1)
    v = act[:, 2 * DN_WIDTH:]
    abb = _nn(ab, sel, hp=True)
    valid = _row_ids(i, tm) >= PAD
    g = jnp.where(valid, -jnp.exp(alog) * _softplus(abb[:, :DN_WIDTH] + dtb), 0.0)
    beta = jnp.where(valid, _sigmoid(abb[:, DN_WIDTH:]), 0.0)
    return (q, k, v, g, beta), ()


def _f_gdn_prep_bwd(i, x, halo, ab, dq, dk, dv, dg, dbeta, w8, alog, dtb, sel, selpick):
    tm = x.shape[0]
    act, dact = _gdn_act(i, x, halo, w8)
    douts = []
    for part, dd in enumerate((dq, dk)):
        for h in range(DN_HEADS):
            t = act[:, DN_WIDTH * part + DN_DIM * h:DN_WIDTH * part + DN_DIM * (h + 1)]
            r = lax.rsqrt(jnp.sum(t * t, axis=-1, keepdims=True) + EPS)
            y = t * r
            dy = dd[:, DN_DIM * h:DN_DIM * (h + 1)]
            douts.append(r * (dy - y * jnp.sum(dy * y, axis=-1, keepdims=True)))
    douts.append(dv)
    dc = jnp.concatenate(douts, axis=1) * dact
    abb = _nn(ab, sel, hp=True)
    valid = _row_ids(i, tm) >= PAD
    pre = abb[:, :DN_WIDTH] + dtb
    ea = jnp.exp(alog)
    g = -ea * _softplus(pre)
    dg = jnp.where(valid, dg, 0.0)
    dbeta = jnp.where(valid, dbeta, 0.0)
    da = dg * (-ea) * _sigmoid(pre)
    beta = _sigmoid(abb[:, DN_WIDTH:])
    db = dbeta * beta * (1.0 - beta)
    dab = _nn(jnp.concatenate([da, db], axis=1), selpick, hp=True)
    return (dc, dab), (_rowsum(dg * g), _rowsum(da))


def _f_conv_bwd(i, dy, dy_next, x, x_prev, w8, *, width, nt):
    dy_next = jnp.where(i < nt - 1, dy_next, 0.0)
    x_prev = jnp.where(i > 0, x_prev, 0.0)
    return (_conv_bwd_x(dy, dy_next, w8, width),), (_conv_bwd_w(dy, x, x_prev, width),)


def _f_mix(i, o_mla, o_dn, z, w_mla, w_dn):
    tm = o_mla.shape[0]
    valid = _row_ids(i, tm) >= PAD
    outs = []
    for h in range(MLA_HEADS):
        y, _ = _rms_fwd(o_mla[:, V_HEAD * h:V_HEAD * (h + 1)], w_mla, V_HEAD)
        outs.append(jnp.where(valid, y, 0.0))
    for h in range(DN_HEADS):
        y, _ = _rms_fwd(o_dn[:, DN_DIM * h:DN_DIM * (h + 1)], w_dn, DN_DIM)
        outs.append(y * _silu_parts(z[:, DN_DIM * h:DN_DIM * (h + 1)])[0])
    return (jnp.concatenate(outs, axis=1),), ()


def _f_mix_bwd(i, o_mla, o_dn, z, dy_mla, dy_dn, w_mla, w_dn):
    tm = o_mla.shape[0]
    valid = _row_ids(i, tm) >= PAD
    d_mla, d_dn, d_z = [], [], []
    dw_mla = None
    dw_dn = None
    for h in range(MLA_HEADS):
        sl = slice(V_HEAD * h, V_HEAD * (h + 1))
        dx, dwr = _rms_bwd(o_mla[:, sl], w_mla, jnp.where(valid, dy_mla[:, sl], 0.0), V_HEAD)
        d_mla.append(dx)
        dw_mla = _rowsum(dwr) if dw_mla is None else dw_mla + _rowsum(dwr)
    for h in range(DN_HEADS):
        sl = slice(DN_DIM * h, DN_DIM * (h + 1))
        y, _ = _rms_fwd(o_dn[:, sl], w_dn, DN_DIM)
        sz, dsz = _silu_parts(z[:, sl])
        d_z.append(dy_dn[:, sl] * y * dsz)
        dx, dwr = _rms_bwd(o_dn[:, sl], w_dn, dy_dn[:, sl] * sz, DN_DIM)
        d_dn.append(dx)
        dw_dn = _rowsum(dwr) if dw_dn is None else dw_dn + _rowsum(dwr)
    return ((jnp.concatenate(d_mla, axis=1), jnp.concatenate(d_dn, axis=1), jnp.concatenate(d_z, axis=1)),
            (dw_mla, dw_dn))


def _f_ffn_act(i, gate_pre, halo, up, w8, b):
    halo = jnp.where(i > 0, halo, 0.0)
    gate = _conv_fwd(gate_pre, halo, w8, FFN_CONV) + b
    return (_silu_parts(gate)[0] * up,), ()


def _f_ffn_act_bwd(i, gate_pre, halo, up, dact, w8, b):
    halo = jnp.where(i > 0, halo, 0.0)
    gate = _conv_fwd(gate_pre, halo, w8, FFN_CONV) + b
    sg, dsg = _silu_parts(gate)
    dgate = dact * up * dsg
    return (dgate, dact * sg), (_rowsum(dgate),)


def _f_loss(i, h3, tgt):
    tm = h3.shape[0]
    diff = jnp.where(_row_ids(i, tm) >= ROW0, h3 - tgt, 0.0)
    part = 0.5 * jnp.sum(diff * diff) * (1.0 / D_MODEL)
    return (diff * (1.0 / D_MODEL),), (jnp.full((1, LANE), part, F32),)


def _after(fn):
    return lambda i, *a: fn(i, *a[:-1])


def _local_step(h0, tgt, w, token, late_weights, grads_ready):
    tp = h0.shape[0]
    nt = tp // TM
    bf = (D_MODEL, _MXU)
    u, = _rows("rms_in", _after(_f_rms_cast), [_In(h0)], [w["attn_norm_w"], token], [bf])
    proj = _mm("in_proj", u, w["w_in"], "nt")
    p_qkv = lambda kind="cur": _In(proj, 3 * DN_WIDTH, 0, kind)
    p_z = _In(proj, DN_WIDTH, C_Z // DN_WIDTH)
    p_ql = _In(proj, Q_LORA, C_QL // Q_LORA)
    p_kvl = _In(proj, KV_LORA, C_KVL // KV_LORA)
    p_kpe = _In(proj, LANE, C_KPE // LANE)
    p_ab = _In(proj, LANE, C_AB // LANE)
    cos, sin_s = _In(w["cos"]), _In(w["sin_s"])

    qn, kvn = _rows("mla_lat_norm", _f_lat_norm, [p_ql, p_kvl], [w["q_a_norm_w"], w["kv_a_norm_w"]],
                    [(Q_LORA, _MXU), (KV_LORA, _MXU)])
    qf = _mm("mla_q_b", qn, w["w_q_b"], "nn")
    kvf = _mm("mla_kv_b", kvn, w["w_kv_b"], "nn")
    qk_w = [w["q_norm_w"], w["k_norm_w"]]
    q, k, v = _rows("mla_qk", _f_mla_qk, [_In(qf), _In(kvf), p_kpe, cos, sin_s], qk_w,
                    [(MLA_HEADS * HP, _MXU), (MLA_HEADS * HP, _MXU), (MLA_HEADS * V_HEAD, _MXU)])
    o_mla = _attn_fwd(q, k, v)

    dn_w = [w["dn_conv_w"], w["alog_b"], w["dtb_b"], w["sel"]]
    gq, gk, gv, gg, gb = _rows("gdn_prep", _f_gdn_prep, [p_qkv(), p_qkv("prev"), p_ab], dn_w,
                               [(DN_WIDTH, F32)] * 5)
    o_dn, s_all, t_all = _gdn_fwd(gq, gk, gv, gg, gb)

    out_w = [w["mla_out_norm_w"], w["dn_out_norm_w"]]
    mixed, = _rows("mix", _f_mix, [_In(o_mla), _In(o_dn), p_z], out_w, [bf])
    w = dict(w, **late_weights(mixed))
    h2 = _mm("out_proj", mixed, w["w_out"], "nn", resid=h0)

    hn, = _rows("rms_ffn", _f_rms_cast, [_In(h2)], [w["ffn_norm_w"]], [bf])
    gate_pre = _mm("ffn_gate", hn, w["w_gate"], "nt")
    up = _mm("ffn_up", hn, w["w_up"], "nt")
    ffn_w = [w["ffn_conv_w"], w["ffn_conv_b"]]
    act, = _rows("ffn_act", _f_ffn_act, [_In(gate_pre), _In(gate_pre, kind="prev"), _In(up)], ffn_w,
                 [(D_FF, _MXU)])
    h3 = _mm("ffn_down", act, w["w_down"], "nn", resid=h2)

    dh3, loss = _rows("loss", _f_loss, [_In(h3), _In(tgt)], [], [(D_MODEL, F32)], [(1, LANE)])

    g = {}
    dact = _mm("ffn_down_dx", dh3, w["w_down"], "nt")
    g["w_down"] = _mm("ffn_down_dw", act, dh3, "tn", out_dtype=_MXU)
    dgate, dup, g["ffn_conv_b"] = _rows(
        "ffn_act_bwd", _f_ffn_act_bwd, [_In(gate_pre), _In(gate_pre, kind="prev"), _In(up), _In(dact)], ffn_w,
        [(D_FF, F32), (D_FF, _MXU)], [(1, D_FF)])
    dgate_pre, g["ffn_conv_w"] = _rows(
        "ffn_conv_bwd", functools.partial(_f_conv_bwd, width=FFN_CONV, nt=nt),
        [_In(dgate), _In(dgate, kind="next"), _In(gate_pre), _In(gate_pre, kind="prev")], [w["ffn_conv_w"]],
        [(D_FF, _MXU)], [(8, D_FF)])
    dhn = _mm("ffn_gate_dx", dgate_pre, w["w_gate"], "nn")
    dhn = _mm("ffn_up_dx", dup, w["w_up"], "nn", resid=dhn)
    g["w_gate"] = _mm("ffn_gate_dw", dgate_pre, hn, "tn", out_dtype=_MXU)
    g["w_up"] = _mm("ffn_up_dw", dup, hn, "tn", out_dtype=_MXU)
    tok = grads_ready(g, ("w_down", "w_gate", "w_up"))
    dh2, g["ffn_norm_w"] = _rows(
        "rms_ffn_bwd", _after(functools.partial(_f_rms_bwd_add, mask_pad=True)), [_In(h2), _In(dhn), _In(dh3)],
        [w["ffn_norm_w"], tok], [(D_MODEL, F32)], [(1, D_MODEL)])

    dmixed = _mm("out_proj_dx", dh2, w["w_out"], "nt")
    g["w_out"] = _mm("out_proj_dw", mixed, dh2, "tn", out_dtype=_MXU)
    half = MLA_HEADS * V_HEAD
    do_mla, do_dn, dz, g["mla_out_norm_w"], g["dn_out_norm_w"] = _rows(
        "mix_bwd", _f_mix_bwd, [_In(o_mla), _In(o_dn), p_z, _In(dmixed, half, 0), _In(dmixed, half, 1)], out_w,
        [(half, F32), (DN_WIDTH, F32), (DN_WIDTH, _MXU)], [(1, V_HEAD), (1, DN_DIM)])

    dq, dk, dv = _attn_bwd(q, k, v, do_mla)
    dqf, dkvf, dkpe, g["q_norm_w"], g["k_norm_w"] = _rows(
        "mla_qk_bwd", _f_mla_qk_bwd, [_In(qf), _In(kvf), p_kpe, cos, sin_s, _In(dq), _In(dk), _In(dv)], qk_w,
        [(MLA_HEADS * HP, _MXU), (MLA_HEADS * HP, _MXU), (LANE, _MXU)], [(1, HP), (1, HP)])
    dqn = _mm("mla_q_b_dx", dqf, w["w_q_b"], "nt")
    g["w_q_b"] = _mm("mla_q_b_dw", qn, dqf, "tn")
    dkvn = _mm("mla_kv_b_dx", dkvf, w["w_kv_b"], "nt")
    g["w_kv_b"] = _mm("mla_kv_b_dw", kvn, dkvf, "tn")
    tok = grads_ready(g, ("w_out", "w_q_b", "w_kv_b"))
    dql, dkvl, g["q_a_norm_w"], g["kv_a_norm_w"] = _rows(
        "mla_lat_norm_bwd", _after(_f_lat_norm_bwd), [p_ql, p_kvl, _In(dqn), _In(dkvn)],
        [w["q_a_norm_w"], w["kv_a_norm_w"], tok],[(Q_LORA, _MXU), (KV_LORA, _MXU)], [(1, Q_LORA), (1, KV_LORA)])

    dgq, dgk, dgv, dgg, dgb = _gdn_bwd(gq, gk, gv, gg, gb, s_all, t_all, do_dn)
    dc, dab, g["alog_b"], g["dtb_b"] = _rows(
        "gdn_prep_bwd", _f_gdn_prep_bwd,
        [p_qkv(), p_qkv("prev"), p_ab, _In(dgq), _In(dgk), _In(dgv), _In(dgg), _In(dgb)], dn_w + [w["selpick"]],
        [(3 * DN_WIDTH, F32), (LANE, _MXU)], [(1, DN_WIDTH), (1, DN_WIDTH)])
    dqkv, g["dn_conv_w"] = _rows(
        "gdn_conv_bwd", functools.partial(_f_conv_bwd, width=DN_CONV, nt=nt),
        [_In(dc), _In(dc, kind="next"), p_qkv(), p_qkv("prev")], [w["dn_conv_w"]],
        [(3 * DN_WIDTH, _MXU)], [(8, 3 * DN_WIDTH)])

    dproj = jnp.concatenate([dqkv, dz, dql, dkvl, dkpe, dab], axis=1)
    g["w_in"] = _mm("in_proj_dw", dproj, u, "tn", out_dtype=_MXU)
    tok = grads_ready(g, ("w_in",))
    du = _mm("in_proj_dx", dproj, w["w_in"], "nn", after=tok)
    dh0, g["attn_norm_w"] = _rows(
        "rms_in_bwd", functools.partial(_f_rms_bwd_add, mask_pad=False), [_In(h0), _In(du), _In(dh2)],
        [w["attn_norm_w"]], [(D_MODEL, F32)], [(1, D_MODEL)])
    return loss, dh0, g


def _w_in_to_padded(w):
    c1, c2, c3 = Q_LORA, Q_LORA + KV_LORA, Q_LORA + KV_LORA + QK_ROPE
    c4 = c3 + 3 * DN_WIDTH
    c5 = c4 + DN_WIDTH
    z = lambda n: jnp.zeros((n, w.shape[1]), w.dtype)
    return jnp.concatenate([w[c3:c4], w[c4:c5], w[:c1], w[c1:c2], w[c2:c3], z(LANE - QK_ROPE),
                            w[c5:], z(LANE - 2 * DN_HEADS)], axis=0)


def _w_in_from_padded(g):
    return jnp.concatenate([g[C_QL:C_QL + Q_LORA], g[C_KVL:C_KVL + KV_LORA], g[C_KPE:C_KPE + QK_ROPE],
                            g[:C_Z + DN_WIDTH], g[C_AB:C_AB + 2 * DN_HEADS]], axis=0)


def _w_q_b_to_padded(w):
    r = w.shape[0]
    w = w.reshape(r, MLA_HEADS, QK_HEAD)
    return jnp.pad(w, ((0, 0), (0, 0), (0, HP - QK_HEAD))).reshape(r, MLA_HEADS * HP)


def _w_q_b_from_padded(g):
    r = g.shape[0]
    return g.reshape(r, MLA_HEADS, HP)[:, :, :QK_HEAD].reshape(r, MLA_HEADS * QK_HEAD)


def _pad_rows8(w):
    return jnp.pad(w, ((0, 8 - w.shape[0]), (0, 0)))


def _prepare(full, tp):
    w = {}
    mx = lambda a: a.astype(_MXU)
    w["attn_norm_w"] = full["attn_norm_w"]
    w["w_in"] = mx(_w_in_to_padded(full["w_in"]))
    w["q_a_norm_w"] = full["q_a_norm_w"]
    w["kv_a_norm_w"] = full["kv_a_norm_w"]
    w["w_q_b"] = mx(_w_q_b_to_padded(full["w_q_b"]))
    w["w_kv_b"] = mx(full["w_kv_b"])
    w["q_norm_w"] = jnp.pad(full["q_norm_w"], ((0, 0), (0, HP - QK_HEAD)))
    w["k_norm_w"] = jnp.pad(full["k_norm_w"], ((0, 0), (0, HP - QK_HEAD)))
    w["mla_out_norm_w"] = full["mla_out_norm_w"]
    w["dn_out_norm_w"] = full["dn_out_norm_w"]
    w["dn_conv_w"] = _pad_rows8(full["dn_conv_w"])
    w["alog_b"] = jnp.repeat(full["dn_A_log"], DN_DIM, axis=1)
    w["dtb_b"] = jnp.repeat(full["dn_dt_bias"], DN_DIM, axis=1)
    w["ffn_norm_w"] = full["ffn_norm_w"]
    w["ffn_conv_w"] = _pad_rows8(full["ffn_conv_w"])
    w["ffn_conv_b"] = full["ffn_conv_b"]
    for n in _LATE:
        if n in full:
            w[n] = mx(full[n])
    half = QK_ROPE // 2
    inv = ROPE_THETA ** (-jnp.arange(half, dtype=F32) / half)
    ang = (jnp.arange(tp, dtype=jnp.int32) - PAD).astype(F32)[:, None] * inv[None, :]
    zc = jnp.zeros((tp, LANE - QK_ROPE), F32)
    w["cos"] = jnp.concatenate([jnp.cos(ang), jnp.cos(ang), zc], axis=1)
    w["sin_s"] = jnp.concatenate([-jnp.sin(ang), jnp.sin(ang), zc], axis=1)
    lane = jnp.arange(2 * DN_WIDTH)[None, :]
    src = jnp.arange(LANE)[:, None]
    w["sel"] = ((lane // DN_DIM) == src).astype(F32)
    w["selpick"] = ((src.T == (lane.T // DN_DIM)) & (lane.T % DN_DIM == 0)).astype(F32)
    return w


def _grads_to_natural(g):
    convert = {
        "w_in": ("w_in", _w_in_from_padded),
        "w_q_b": ("w_q_b", _w_q_b_from_padded),
        "q_norm_w": ("q_norm_w", lambda a: a[:, :QK_HEAD]),
        "k_norm_w": ("k_norm_w", lambda a: a[:, :QK_HEAD]),
        "dn_conv_w": ("dn_conv_w", lambda a: a[:DN_CONV]),
        "ffn_conv_w": ("ffn_conv_w", lambda a: a[:FFN_CONV]),
        "alog_b": ("dn_A_log", lambda a: a[:, ::DN_DIM]),
        "dtb_b": ("dn_dt_bias", lambda a: a[:, ::DN_DIM]),
    }
    n = {}
    for key, a in g.items():
        name, fn = convert.get(key, (key, lambda t: t))
        n[name] = fn(a)
    return n


_MESH = pl.DeviceIdType.MESH
_ANY = pl.BlockSpec(memory_space=pl.ANY)
_CHIP_FLIPS = ((1, 0), (0, 1), (1, 1))


def _me():
    return lax.axis_index("x"), lax.axis_index("y"), lax.axis_index("c")


def _all_gather(name, blk):
    def body(x_ref, out_ref, send_sems, recv_sems, local_sem):
        x, y, c = _me()
        me, sib = (x, y, c), (x, y, 1 - c)
        chips = [(x ^ fx, y ^ fy) for fx, fy in _CHIP_FLIPS]

        def slot(p):
            return out_ref.at[4 * p[0] + 2 * p[1] + p[2]]

        def copy(k, block, to, src=None):
            return pltpu.make_async_remote_copy(
                src_ref=slot(block) if src is None else src, dst_ref=slot(block),
                send_sem=send_sems.at[k], recv_sem=recv_sems.at[k], device_id=to, device_id_type=_MESH)

        mine = pltpu.make_async_copy(x_ref, slot(me), local_sem)
        mine.start()
        first = [copy(0, me, sib, src=x_ref)]
        first += [copy(1 + j, me, (*chip, c), src=x_ref) for j, chip in enumerate(chips)]
        for cp in first:
            cp.start()
        passed = [copy(4 + j, (*chip, c), sib) for j, chip in enumerate(chips)]
        for j, chip in enumerate(chips):
            copy(1 + j, (*chip, c), me).wait_recv()
            passed[j].start()
        copy(0, sib, me).wait_recv()
        for j, chip in enumerate(chips):
            copy(4 + j, (*chip, 1 - c), me).wait_recv()
        for cp in first + passed:
            cp.wait_send()
        mine.wait()

    return pl.pallas_call(
        body, name=name, in_specs=[_ANY], out_specs=_ANY,
        out_shape=jax.ShapeDtypeStruct((N_DEV,) + blk.shape, blk.dtype),
        scratch_shapes=[pltpu.SemaphoreType.DMA((7,)), pltpu.SemaphoreType.DMA((7,)), pltpu.SemaphoreType.DMA],
    )(blk)


def _rs_sibling(name, gb):
    def body(g_ref, out_ref, send_sems, recv_sems):
        x, y, c = _me()
        cps = []
        for j in range(4):
            cp = pltpu.make_async_remote_copy(
                src_ref=g_ref.at[2 * j + (1 - c)], dst_ref=out_ref.at[j], send_sem=send_sems.at[j],
                recv_sem=recv_sems.at[j], device_id=(x, y, 1 - c), device_id_type=_MESH)
            cp.start()
            cps.append(cp)
        for cp in cps:
            cp.wait()

    return pl.pallas_call(
        body, name=name, in_specs=[_ANY], out_specs=_ANY,
        out_shape=jax.ShapeDtypeStruct((4,) + gb.shape[1:], gb.dtype),
        scratch_shapes=[pltpu.SemaphoreType.DMA((4,)), pltpu.SemaphoreType.DMA((4,))],
    )(gb)


def _rs_chips(name, s1):
    def body(s_ref, out_ref, send_sems, recv_sems):
        x, y, c = _me()
        cps = []
        for k, (fx, fy) in enumerate(_CHIP_FLIPS):
            px, py = x ^ fx, y ^ fy
            cp = pltpu.make_async_remote_copy(
                src_ref=s_ref.at[2 * px + py], dst_ref=out_ref.at[k], send_sem=send_sems.at[k],
                recv_sem=recv_sems.at[k], device_id=(px, py, c), device_id_type=_MESH)
            cp.start()
            cps.append(cp)
        for cp in cps:
            cp.wait()

    return pl.pallas_call(
        body, name=name, in_specs=[_ANY], out_specs=_ANY,
        out_shape=jax.ShapeDtypeStruct((3,) + s1.shape[1:], s1.dtype),
        scratch_shapes=[pltpu.SemaphoreType.DMA((3,)), pltpu.SemaphoreType.DMA((3,))],
    )(s1)


def _row_tile(r):
    divs = [d for d in range(16, min(r, 512) + 1, 16) if r % d == 0]
    return divs[-1] if divs else r


def _pair_sum(name, gb, recv):
    _, r, cols = gb.shape
    tm = _row_tile(r)
    c = lax.axis_index("c").astype(jnp.int32).reshape(1)

    def body(c_ref, a_ref, b_ref, o_ref, ob_ref):
        s = a_ref[...] + b_ref[...]
        o_ref[...] = s
        ob_ref[...] = s.astype(BF16)

    blk = pl.BlockSpec((1, tm, cols), lambda j, i, c_ref: (j, i, 0))
    return pl.pallas_call(
        body, name=name,
        grid_spec=pltpu.PrefetchScalarGridSpec(
            num_scalar_prefetch=1, grid=(4, r // tm),
            in_specs=[pl.BlockSpec((1, tm, cols), lambda j, i, c_ref: (2 * j + c_ref[0], i, 0)), blk],
            out_specs=[blk, blk]),
        out_shape=[jax.ShapeDtypeStruct((4, r, cols), F32), jax.ShapeDtypeStruct((4, r, cols), BF16)],
        compiler_params=pltpu.CompilerParams(dimension_semantics=("parallel", "parallel")),
    )(c, gb, recv)


def _sum_parts(name, parts):
    _, r, cols = parts[0][0].shape
    tm = _row_tile(r)
    idx = jnp.stack([jnp.asarray(s, jnp.int32) for _, s in parts])
    n = len(parts)

    def body(idx_ref, *refs):
        g = refs[0][0].astype(F32)
        for p_ref in refs[1:n]:
            g = g + p_ref[0].astype(F32)
        refs[n][...] = g

    return pl.pallas_call(
        body, name=name,
        grid_spec=pltpu.PrefetchScalarGridSpec(
            num_scalar_prefetch=1, grid=(r // tm,),
            in_specs=[pl.BlockSpec((1, tm, cols), lambda i, idx_ref, p=p: (idx_ref[p], i, 0)) for p in range(n)],
            out_specs=pl.BlockSpec((tm, cols), lambda i, idx_ref: (i, 0))),
        out_shape=jax.ShapeDtypeStruct((r, cols), F32),
        compiler_params=pltpu.CompilerParams(dimension_semantics=("parallel",)),
    )(idx, *[a for a, _ in parts])


def _adam(name, parts, w, m, v):
    r, cols = w.shape
    tm = _row_tile(r)
    idx = jnp.stack([jnp.asarray(s, jnp.int32) for _, s in parts])
    n = len(parts)

    def body(idx_ref, *refs):
        g = refs[0][0].astype(F32)
        for p_ref in refs[1:n]:
            g = g + p_ref[0].astype(F32)
        w_ref, m_ref, v_ref, g_out, d_out, m_out, v_out = refs[n:]
        m_new = ADAM_B1 * m_ref[...] + (1.0 - ADAM_B1) * g
        v_new = ADAM_B2 * v_ref[...] + (1.0 - ADAM_B2) * (g * g)
        m_hat = m_new / (1.0 - ADAM_B1 ** ADAM_STEP)
        v_hat = v_new / (1.0 - ADAM_B2 ** ADAM_STEP)
        g_out[...] = g
        d_out[...] = -ADAM_LR * (m_hat / (jnp.sqrt(v_hat) + ADAM_EPS) + ADAM_WD * w_ref[...])
        m_out[...] = m_new
        v_out[...] = v_new

    part_specs = [pl.BlockSpec((1, tm, cols), lambda i, idx_ref, p=p: (idx_ref[p], i, 0)) for p in range(n)]
    flat = pl.BlockSpec((tm, cols), lambda i, idx_ref: (i, 0))
    return pl.pallas_call(
        body, name=name,
        grid_spec=pltpu.PrefetchScalarGridSpec(
            num_scalar_prefetch=1, grid=(r // tm,), in_specs=part_specs + [flat] * 3, out_specs=[flat] * 4),
        out_shape=[jax.ShapeDtypeStruct((r, cols), F32)] * 4,
        compiler_params=pltpu.CompilerParams(dimension_semantics=("parallel",)),
    )(idx, *[a for a, _ in parts], w, m, v)


def _all_gather_many(name, blks):
    n = len(blks)

    def body(*refs):
        x_refs, out_refs = refs[:n], refs[n:2 * n]
        send_sems, recv_sems, local_sems = refs[2 * n:]
        x, y, c = _me()
        me, sib = (x, y, c), (x, y, 1 - c)
        chips = [(x ^ fx, y ^ fy) for fx, fy in _CHIP_FLIPS]

        def slot(a, p):
            return out_refs[a].at[4 * p[0] + 2 * p[1] + p[2]]

        def copy(a, k, block, to, src=None):
            return pltpu.make_async_remote_copy(
                src_ref=slot(a, block) if src is None else src, dst_ref=slot(a, block),
                send_sem=send_sems.at[7 * a + k], recv_sem=recv_sems.at[7 * a + k], device_id=to,
                device_id_type=_MESH)

        mine = [pltpu.make_async_copy(x_refs[a], slot(a, me), local_sems.at[a]) for a in range(n)]
        first = []
        for a in range(n):
            mine[a].start()
            first.append(copy(a, 0, me, sib, src=x_refs[a]))
            first += [copy(a, 1 + j, me, (*chip, c), src=x_refs[a]) for j, chip in enumerate(chips)]
        for cp in first:
            cp.start()
        passed = []
        for j, chip in enumerate(chips):
            for a in range(n):
                copy(a, 1 + j, (*chip, c), me).wait_recv()
                cp = copy(a, 4 + j, (*chip, c), sib)
                cp.start()
                passed.append(cp)
        for a in range(n):
            copy(a, 0, sib, me).wait_recv()
            for j, chip in enumerate(chips):
                copy(a, 4 + j, (*chip, 1 - c), me).wait_recv()
        for cp in first + passed:
            cp.wait_send()
        for cp in mine:
            cp.wait()

    return pl.pallas_call(
        body, name=name, in_specs=[_ANY] * n, out_specs=[_ANY] * n,
        out_shape=[jax.ShapeDtypeStruct((N_DEV,) + b.shape, b.dtype) for b in blks],
        scratch_shapes=[pltpu.SemaphoreType.DMA((7 * n,)), pltpu.SemaphoreType.DMA((7 * n,)),
                        pltpu.SemaphoreType.DMA((n,))],
    )(*blks)


def _rs_sibling_many(name, gbs):
    n = len(gbs)

    def body(*refs):
        g_refs, out_refs = refs[:n], refs[n:2 * n]
        send_sems, recv_sems = refs[2 * n:]
        x, y, c = _me()
        cps = []
        for a in range(n):
            for j in range(4):
                cp = pltpu.make_async_remote_copy(
                    src_ref=g_refs[a].at[2 * j + (1 - c)], dst_ref=out_refs[a].at[j],
                    send_sem=send_sems.at[4 * a + j], recv_sem=recv_sems.at[4 * a + j],
                    device_id=(x, y, 1 - c), device_id_type=_MESH)
                cp.start()
                cps.append(cp)
        for cp in cps:
            cp.wait()

    return pl.pallas_call(
        body, name=name, in_specs=[_ANY] * n, out_specs=[_ANY] * n,
        out_shape=[jax.ShapeDtypeStruct((4,) + g.shape[1:], g.dtype) for g in gbs],
        scratch_shapes=[pltpu.SemaphoreType.DMA((4 * n,)), pltpu.SemaphoreType.DMA((4 * n,))],
    )(*gbs)


def _rs_chips_many(name, s1s):
    n = len(s1s)

    def body(*refs):
        s_refs, out_refs = refs[:n], refs[n:2 * n]
        send_sems, recv_sems = refs[2 * n:]
        x, y, c = _me()
        cps = []
        for a in range(n):
            for k, (fx, fy) in enumerate(_CHIP_FLIPS):
                px, py = x ^ fx, y ^ fy
                cp = pltpu.make_async_remote_copy(
                    src_ref=s_refs[a].at[2 * px + py], dst_ref=out_refs[a].at[k],
                    send_sem=send_sems.at[3 * a + k], recv_sem=recv_sems.at[3 * a + k],
                    device_id=(px, py, c), device_id_type=_MESH)
                cp.start()
                cps.append(cp)
        for cp in cps:
            cp.wait()

    return pl.pallas_call(
        body, name=name, in_specs=[_ANY] * n, out_specs=[_ANY] * n,
        out_shape=[jax.ShapeDtypeStruct((3,) + s.shape[1:], s.dtype) for s in s1s],
        scratch_shapes=[pltpu.SemaphoreType.DMA((3 * n,)), pltpu.SemaphoreType.DMA((3 * n,))],
    )(*s1s)


_HBM = pl.BlockSpec(memory_space=pltpu.HBM)
_SEM = pl.BlockSpec(memory_space=pltpu.SEMAPHORE)
_EFFECT = pltpu.SideEffectType.DATAFLOW_SIDE_EFFECTING


def _push_copies(src_refs, land_refs, send_sems, recv_sems, src_by_peer):
    x, y, c = _me()
    my_id = 4 * x + 2 * y + c
    out = []
    for a in range(len(src_refs)):
        for f in range(1, N_DEV):
            px, py, pc = x ^ (f >> 2), y ^ ((f >> 1) & 1), c ^ (f & 1)
            pid = 4 * px + 2 * py + pc
            src = src_refs[a].at[pid] if src_by_peer else src_refs[a]
            start = pltpu.make_async_remote_copy(
                src_ref=src, dst_ref=land_refs[a].at[my_id], send_sem=send_sems.at[7 * a + f - 1],
                recv_sem=recv_sems.at[7 * a + f - 1], device_id=(px, py, pc), device_id_type=_MESH)
            landed = pltpu.make_async_remote_copy(
                src_ref=src, dst_ref=land_refs[a].at[pid], send_sem=send_sems.at[7 * a + f - 1],
                recv_sem=recv_sems.at[7 * a + f - 1], device_id=(px, py, pc), device_id_type=_MESH)
            out.append((start, landed))
    return out


def _push_start(name, srcs, src_by_peer, after):
    n = len(srcs)
    lands = [jax.ShapeDtypeStruct((N_DEV,) + (s.shape[1:] if src_by_peer else s.shape), s.dtype) for s in srcs]

    def body(*refs):
        src_refs, land_refs = refs[:n], refs[n:2 * n]
        send_sems, recv_sems = refs[2 * n + 1], refs[2 * n + 2]
        token = refs[-1]
        for start, _ in _push_copies(src_refs, land_refs, send_sems, recv_sems, src_by_peer):
            start.start()
        token[...] = jnp.zeros_like(token)

    hbm = lambda a: pltpu.with_memory_space_constraint(a, pltpu.HBM)
    res = pl.pallas_call(
        body, name=name,
        out_shape=(pltpu.SemaphoreType.DMA((7 * n,)), pltpu.SemaphoreType.DMA((7 * n,)),
                   *[pltpu.HBM(s.shape, s.dtype) for s in srcs], *[pltpu.HBM(s.shape, s.dtype) for s in lands],
                   jax.ShapeDtypeStruct((8, LANE), F32)),
        in_specs=[_HBM] * (2 * n) + [_ANY],
        out_specs=(_SEM, _SEM, *[_HBM] * (2 * n), pl.BlockSpec(memory_space=pltpu.VMEM)),
        input_output_aliases={i: 2 + i for i in range(2 * n)},
        compiler_params=pltpu.CompilerParams(has_side_effects=_EFFECT),
    )(*[hbm(s) for s in srcs], *[hbm(lax.empty(s.shape, s.dtype)) for s in lands], after)
    return res[0], res[1], list(res[2:2 + n]), list(res[2 + n:2 + 2 * n]), res[-1]


def _push_wait(name, send_sems, recv_sems, srcs, lands, src_by_peer, after):
    n = len(srcs)

    def body(*refs):
        src_refs, land_refs = refs[:n], refs[n:2 * n]
        s_sems, r_sems = refs[2 * n], refs[2 * n + 1]
        for _, landed in _push_copies(src_refs, land_refs, s_sems, r_sems, src_by_peer):
            landed.wait_send()
            landed.wait_recv()

    res = pl.pallas_call(
        body, name=name,
        out_shape=tuple(pltpu.HBM(s.shape, s.dtype) for s in list(srcs) + list(lands)),
        in_specs=[_HBM] * (2 * n) + [_SEM, _SEM, _ANY],
        out_specs=tuple([_HBM] * (2 * n)),
        input_output_aliases={i: i for i in range(2 * n)},
        compiler_params=pltpu.CompilerParams(has_side_effects=_EFFECT),
    )(*srcs, *lands, send_sems, recv_sems, after)
    return list(res[:n]), list(res[n:])


_SHARDED = (
    ("meta_tokens", 1, (N_META, D_MODEL)),
    ("w_in", 1, (D_MODEL, IN_COLS)),
    ("w_q_b", 1, (Q_LORA, MLA_HEADS * QK_HEAD)),
    ("w_kv_b", 1, (KV_LORA, MLA_HEADS * (QK_NOPE + V_HEAD))),
    ("dn_conv_w", 1, (DN_CONV, 3 * DN_WIDTH)),
    ("w_out", 0, (2 * DN_WIDTH, D_MODEL)),
    ("w_gate", 1, (D_MODEL, D_FF)),
    ("w_up", 1, (D_MODEL, D_FF)),
    ("ffn_conv_w", 1, (FFN_CONV, D_FF)),
    ("w_down", 0, (D_FF, D_MODEL)),
)
_MXU_GATHERED = ("w_in", "w_q_b", "w_kv_b", "w_out", "w_gate", "w_up", "w_down")
_F32_GATHERED = ("meta_tokens", "dn_conv_w", "ffn_conv_w")
_EARLY = ("w_in", "w_q_b", "w_kv_b")
_LATE = ("w_out", "w_gate", "w_up", "w_down")
_TRANSPOSED = ("w_in", "w_gate", "w_up")
_REPLICATED = (
    ("attn_norm_w", D_MODEL), ("q_a_norm_w", Q_LORA), ("kv_a_norm_w", KV_LORA), ("q_norm_w", QK_HEAD),
    ("k_norm_w", QK_HEAD), ("mla_out_norm_w", V_HEAD), ("dn_A_log", DN_HEADS), ("dn_dt_bias", DN_HEADS),
    ("dn_out_norm_w", DN_DIM), ("ffn_norm_w", D_MODEL), ("ffn_conv_b", D_FF),
)
_PACK_COLS = 1024
_PACK_ROW_MULT = 320
_SMALL_SHAPE = (8, 768)
_SMALL_BLOCK = (8, 512)


def _local_shape(dim, shape):
    return (shape[0] // N_DEV, shape[1]) if dim == 0 else (shape[0], shape[1] // N_DEV)


def _pack_rows(n, mult):
    rows = -(-n // _PACK_COLS)
    return -(-rows // mult) * mult


def _pack(flats, mult, axis=0):
    cat = jnp.concatenate(flats, axis=-1)
    n = cat.shape[-1]
    r = _pack_rows(n, mult)
    pad = [(0, 0)] * (cat.ndim - 1) + [(0, r * _PACK_COLS - n)]
    return jnp.pad(cat, pad).reshape(cat.shape[:-1] + (r, _PACK_COLS))


def _to_blocks(full, dim):
    r, c = full.shape
    if dim == 0:
        return full.reshape(N_DEV, (r // N_DEV) * c)
    return full.reshape(r, N_DEV, c // N_DEV).transpose(1, 0, 2).reshape(N_DEV, r * (c // N_DEV))


def _from_blocks(blocks, dim, shape):
    r, c = shape
    if dim == 0:
        return blocks.reshape(r, c)
    return blocks.reshape(N_DEV, r, c // N_DEV).transpose(1, 0, 2).reshape(r, c)


def _split(flat, sizes):
    out, o = [], 0
    for s in sizes:
        out.append(flat[..., o:o + s])
        o += s
    return out


def _gather_weights(local, names, dtype, mult):
    specs = [s for s in _SHARDED if s[0] in names]
    pack = _pack([local[n].astype(dtype).reshape(-1) for n, _, _ in specs], mult)
    got = _all_gather("gather_" + "_".join(n[:5] for n in names[:2]), pack)
    flat = got.reshape(N_DEV, -1)
    sizes = [math.prod(_local_shape(d, s)) for _, d, s in specs]
    return {n: _from_blocks(p, d, s) for (n, d, s), p in zip(specs, _split(flat, sizes))}


def kernel(x, meta_tokens, attn_norm_w, w_in, q_a_norm_w, w_q_b, kv_a_norm_w, w_kv_b, q_norm_w, k_norm_w, mla_out_norm_w, dn_conv_w, dn_A_log, dn_dt_bias, dn_out_norm_w, w_out, ffn_norm_w, w_gate, w_up, ffn_conv_w, ffn_conv_b, w_down, loss_target, m_meta_tokens, m_attn_norm_w, m_w_in, m_q_a_norm_w, m_w_q_b, m_kv_a_norm_w, m_w_kv_b, m_q_norm_w, m_k_norm_w, m_mla_out_norm_w, m_dn_conv_w, m_dn_A_log, m_dn_dt_bias, m_dn_out_norm_w, m_w_out, m_ffn_norm_w, m_w_gate, m_w_up, m_ffn_conv_w, m_ffn_conv_b, m_w_down, v_meta_tokens, v_attn_norm_w, v_w_in, v_q_a_norm_w, v_w_q_b, v_kv_a_norm_w, v_w_kv_b, v_q_norm_w, v_k_norm_w, v_mla_out_norm_w, v_dn_conv_w, v_dn_A_log, v_dn_dt_bias, v_dn_out_norm_w, v_w_out, v_ffn_norm_w, v_w_gate, v_w_up, v_ffn_conv_w, v_ffn_conv_b, v_w_down):
    names = [n for n, _, _ in _SHARDED] + [n for n, _ in _REPLICATED]
    given = dict(locals())
    two_d = lambda a: a.reshape(a.shape[-2:])
    wl = {n: two_d(given[n]) for n in names}
    ml = {n: two_d(given["m_" + n]) for n in names}
    vl = {n: two_d(given["v_" + n]) for n in names}
    out_shapes = {n: given[n].shape for n in names}

    spec = {n: (d, s) for n, d, s in _SHARDED}
    small_sizes = [math.prod(_local_shape(*spec[n])) for n in _F32_GATHERED]

    def small_block(d):
        cat = jnp.concatenate([d[n].reshape(d[n].shape[:-2] + (-1,)) for n in _F32_GATHERED], axis=-1)
        pad = [(0, 0)] * (cat.ndim - 1) + [(0, math.prod(_SMALL_BLOCK) - cat.shape[-1])]
        return jnp.pad(cat, pad).reshape(cat.shape[:-1] + _SMALL_BLOCK)

    def shard(n):
        a = wl[n].astype(_MXU)
        return a.T if n in _TRANSPOSED else a

    def from_slots(n, blocks):
        d, s = spec[n]
        if d == 0 or n in _TRANSPOSED:
            return blocks.reshape(-1, blocks.shape[-1])
        return blocks.transpose(1, 0, 2).reshape(s)

    my_id = 4 * lax.axis_index("x") + 2 * lax.axis_index("y") + lax.axis_index("c")
    got = _all_gather_many("gather_early", [shard(n) for n in _EARLY] + [small_block(wl)])
    full = {n: a for n, a in wl.items() if n not in _LATE}
    for n, blocks in zip(_EARLY, got):
        full[n] = from_slots(n, blocks)
    for n, p in zip(_F32_GATHERED, _split(got[-1].reshape(N_DEV, -1), small_sizes)):
        full[n] = _from_blocks(p, *spec[n])
    late_own = [shard(n) for n in _LATE]
    l_send, l_recv, l_src, l_land, token = _push_start("gather_late_start", late_own, False, got[-1])

    def late_weights(after):
        _, lands = _push_wait("gather_late_wait", l_send, l_recv, l_src, l_land, False, after)
        out = {}
        for n, land, own in zip(_LATE, lands, late_own):
            out[n] = from_slots(n, lax.dynamic_update_slice(land, own[None], (my_id, 0, 0))).astype(_MXU)
        return out

    def dest_blocks(n, a):
        d, s = spec[n]
        r, c = _local_shape(d, s)
        if n in _TRANSPOSED:
            return a.reshape(N_DEV, c, r)
        return a.reshape(N_DEV, r, c) if d == 0 else a.reshape(r, N_DEV, c).transpose(1, 0, 2)

    pushed = []

    def grads_ready(g, names):
        nat = _grads_to_natural({n: g[n] for n in names})
        blocks = [dest_blocks(n, nat[n]).astype(_MXU) for n in names]
        sends, recvs, srcs, lands, tok = _push_start("rs_" + names[0] + "_start", blocks, True, token)
        pushed.append((names, sends, recvs, srcs, lands))
        return tok

    seq = x.shape[1]
    tp = ROW0 + seq
    h0 = jnp.concatenate([jnp.zeros((PAD, D_MODEL), F32), full["meta_tokens"], x[0]], axis=0)
    tgt = jnp.concatenate([jnp.zeros((ROW0, D_MODEL), F32), loss_target[0]], axis=0)
    loss, dh0, g = _local_step(h0, tgt, _prepare(full, tp), token, late_weights, grads_ready)
    g = _grads_to_natural(g)
    g["meta_tokens"] = dh0[PAD:ROW0]
    grad_x = dh0[ROW0:][None]

    big = [{}, {}, {}, {}]
    rep_names = [n for n, _ in _REPLICATED]
    pieces = [g[n].reshape(-1) for n in rep_names] + [loss[0, :1]] + [g[n].reshape(-1) for n in _F32_GATHERED]
    sizes = [p.shape[0] for p in pieces]
    cat = jnp.concatenate(pieces)
    cols = -(-cat.shape[0] // (8 * LANE)) * LANE
    mine = jnp.pad(cat, (0, 8 * cols - cat.shape[0])).reshape(8, cols)
    everyone = _all_gather("gather_small_grads", mine)
    total = _sum_parts("sum_small_grads", [(everyone, d) for d in range(N_DEV)])
    tot = dict(zip(rep_names + ["loss"] + list(_F32_GATHERED), _split(total.reshape(-1), sizes)))

    def small(d):
        cat = jnp.concatenate([d[n].reshape(-1) for n in rep_names])
        return jnp.pad(cat, (0, math.prod(_SMALL_SHAPE) - cat.shape[0])).reshape(_SMALL_SHAPE)

    sm = _adam("adam_replicated", [(small(tot)[None], 0)], small(wl), small(ml), small(vl))
    sm = [dict(zip(rep_names, _split(a.reshape(-1), [n for _, n in _REPLICATED]))) for a in sm]
    mine_of = {}
    for n in _F32_GATHERED:
        d, s = spec[n]
        r, c = _local_shape(d, s)
        mine_of[n] = lax.dynamic_slice(tot[n].reshape(s), (0, my_id * c), (r, c))
    res = _adam("adam_small_sharded", [(small_block(mine_of)[None], 0)], small_block(wl), small_block(ml),
                small_block(vl))
    for kind, a in enumerate(res):
        big[kind].update(zip(_F32_GATHERED, _split(a.reshape(-1), small_sizes)))

    for names, sends, recvs, srcs, lands in pushed:
        srcs, lands = _push_wait("rs_" + names[0] + "_wait", sends, recvs, srcs, lands, True, dh0)
        for n, src, land in zip(names, srcs, lands):
            parts = [(src, my_id)] + [(land, my_id ^ f) for f in range(1, N_DEV)]
            if n in _TRANSPOSED:
                parts = [(_sum_parts("sum_" + n, parts).T[None], 0)]
            for kind, a in enumerate(_adam("adam_" + n, parts, wl[n], ml[n], vl[n])):
                big[kind][n] = a

    outs = [tot["loss"].reshape(()), grad_x]
    for kind in range(4):
        for n in ("meta_tokens", "attn_norm_w", "w_in", "q_a_norm_w", "w_q_b", "kv_a_norm_w", "w_kv_b", "q_norm_w",
                  "k_norm_w", "mla_out_norm_w", "dn_conv_w", "dn_A_log", "dn_dt_bias", "dn_out_norm_w", "w_out",
                  "ffn_norm_w", "w_gate", "w_up", "ffn_conv_w", "ffn_conv_b", "w_down"):
            src = big[kind] if n in big[kind] else sm[kind]
            outs.append(src[n].reshape(out_shapes[n]))
    return tuple(outs)
```

```python
import functools
import math

import jax
import jax.numpy as jnp
from jax import lax
from jax.experimental import pallas as pl
from jax.experimental.pallas import tpu as pltpu

F32 = jnp.float32
BF16 = jnp.bfloat16
_MXU = jnp.bfloat16
_HI = lax.Precision.HIGHEST

D_MODEL = 1024
N_META = 16
PAD = 112
ROW0 = PAD + N_META
MLA_HEADS = 4
QK_NOPE = 128
QK_ROPE = 64
QK_HEAD = QK_NOPE + QK_ROPE
V_HEAD = 128
Q_LORA = 256
KV_LORA = 256
ROPE_THETA = 10000.0
DN_HEADS = 4
DN_DIM = 128
DN_WIDTH = DN_HEADS * DN_DIM
DN_CONV = 4
DN_CHUNK = 64
D_FF = 2816
FFN_CONV = 3
EPS = 1e-6
HP = 256
C_QKV = 0
C_Z = 1536
C_QL = 2048
C_KVL = 2304
C_KPE = 2560
C_AB = 2688
IN_P = 2816
IN_COLS = 2632

ADAM_LR = 0.001
ADAM_B1 = 0.9
ADAM_B2 = 0.999
ADAM_EPS = 1e-08
ADAM_WD = 0.01
ADAM_STEP = 10

N_DEV = 8
TM = 128
LANE = 128
VMEM_LIMIT = 56 * 1024 * 1024
NEG = -1e30


def _dot(a, b, dims, hp=False):
    if hp:
        return lax.dot_general(a.astype(F32), b.astype(F32), (dims, ((), ())),
                               precision=lax.Precision.HIGH if hp == "3x" else _HI, preferred_element_type=F32)
    return lax.dot_general(a.astype(_MXU), b.astype(_MXU), (dims, ((), ())),
                           preferred_element_type=F32)


def _nn(a, b, hp=False):
    return _dot(a, b, ((1,), (0,)), hp)


def _nt(a, b, hp=False):
    return _dot(a, b, ((1,), (1,)), hp)


def _tn(a, b, hp=False):
    return _dot(a, b, ((0,), (0,)), hp)


def _sigmoid(x):
    return 1.0 / (1.0 + jnp.exp(-x))


def _rms_fwd(x, w, n):
    r = lax.rsqrt(jnp.sum(x * x, axis=-1, keepdims=True) * (1.0 / n) + EPS)
    return x * r * w, r


def _rms_bwd(x, w, dy, n):
    r = lax.rsqrt(jnp.sum(x * x, axis=-1, keepdims=True) * (1.0 / n) + EPS)
    xh = x * r
    gy = dy * w
    dx = r * (gy - xh * (jnp.sum(gy * xh, axis=-1, keepdims=True) * (1.0 / n)))
    return dx, dy * xh


def _rowsum(x):
    return jnp.sum(x, axis=0, keepdims=True)


def _row_ids(i, tm):
    return i * tm + lax.broadcasted_iota(jnp.int32, (tm, 1), 0)


def _shift_down(ext, s, tm):
    if s == 0:
        return ext[8:8 + tm]
    return pltpu.roll(ext, s, 0)[8:8 + tm]


def _shift_up(ext, s, tm):
    if s == 0:
        return ext[0:tm]
    return pltpu.roll(ext, tm + 8 - s, 0)[0:tm]


def _conv_fwd(x, halo_prev, w, width):
    tm = x.shape[0]
    ext = jnp.concatenate([halo_prev, x], axis=0)
    y = None
    for j in range(width):
        t = w[j:j + 1, :] * _shift_down(ext, width - 1 - j, tm)
        y = t if y is None else y + t
    return y


def _conv_bwd_x(dy, halo_next, w, width):
    tm = dy.shape[0]
    ext = jnp.concatenate([dy, halo_next], axis=0)
    dx = None
    for j in range(width):
        t = w[j:j + 1, :] * _shift_up(ext, width - 1 - j, tm)
        dx = t if dx is None else dx + t
    return dx


def _conv_bwd_w(dy, x, halo_prev, width):
    tm = dy.shape[0]
    ext = jnp.concatenate([halo_prev, x], axis=0)
    rows = [_rowsum(dy * _shift_down(ext, width - 1 - j, tm)) for j in range(width)]
    rows += [jnp.zeros_like(rows[0])] * (8 - width)
    return jnp.concatenate(rows, axis=0)


def _softplus(x):
    e = jnp.exp(-jnp.abs(x))
    u = 1.0 + e
    l1p = jnp.where(u == 1.0, e, jnp.log(u) * e / jnp.where(u == 1.0, 1.0, u - 1.0))
    return jnp.maximum(x, 0.0) + l1p


def _swap_halves(x):
    lane = lax.broadcasted_iota(jnp.int32, x.shape, 1)
    return jnp.where(lane < 32, pltpu.roll(x, 96, 1), jnp.where(lane < 64, pltpu.roll(x, 32, 1), 0.0))


class _In:
    def __init__(self, arr, width=None, cb=0, kind="cur"):
        self.arr, self.kind = arr, kind
        self.width = arr.shape[1] if width is None else width
        self.cb = cb


def _rows(name, fn, tiled, full, outs, accs=(), tm=TM):
    tp = tiled[0].arr.shape[0]
    nt = tp // tm
    r8 = tm // 8
    n_in = len(tiled) + len(full)
    n_out = len(outs)

    def body(*refs):
        i = pl.program_id(0)
        vals = [r[...] for r in refs[:n_in]]
        o_t, o_a = fn(i, *vals)
        for r, v in zip(refs[n_in:n_in + n_out], o_t):
            r[...] = v.astype(r.dtype)
        for r, v in zip(refs[n_in + n_out:], o_a):
            @pl.when(i == 0)
            def _():
                r[...] = v

            @pl.when(i > 0)
            def _():
                r[...] += v

    def spec(t):
        if t.kind == "cur":
            return pl.BlockSpec((tm, t.width), lambda i, cb=t.cb: (i, cb))
        if t.kind == "prev":
            return pl.BlockSpec((8, t.width), lambda i, cb=t.cb: (jnp.maximum(i * r8 - 1, 0), cb))
        return pl.BlockSpec((8, t.width), lambda i, cb=t.cb: (jnp.minimum((i + 1) * r8, tp // 8 - 1), cb))

    in_specs = [spec(t) for t in tiled]
    in_specs += [pl.BlockSpec(a.shape, lambda i, nd=a.ndim: (0,) * nd) for a in full]
    out_specs = [pl.BlockSpec((tm, w), lambda i: (i, 0)) for w, _ in outs]
    out_specs += [pl.BlockSpec((r, w), lambda i: (0, 0)) for r, w in accs]
    out_shape = [jax.ShapeDtypeStruct((tp, w), dt) for w, dt in outs]
    out_shape += [jax.ShapeDtypeStruct((r, w), F32) for r, w in accs]
    res = pl.pallas_call(
        body, name=name, grid=(nt,), in_specs=in_specs, out_specs=out_specs, out_shape=out_shape,
        compiler_params=pltpu.CompilerParams(dimension_semantics=("arbitrary",), vmem_limit_bytes=VMEM_LIMIT),
    )(*[t.arr for t in tiled], *full)
    return res


def _pick(n, cap, mult):
    best = None
    for d in range(mult, min(n, cap) + 1, mult):
        if n % d == 0:
            best = d
    assert best is not None, (n, cap, mult)
    return best


_ANY_SPEC = pl.BlockSpec(memory_space=pl.ANY)


def _mm(name, a, b, mode, out_dtype=F32, resid=None, after=None):
    if mode == "tn":
        m, k = a.shape
        n = b.shape[1]
        tk = _pick(k, 512, 128)
        tn = _pick(n, 1408, 128)

        def body_tn(a_ref, b_ref, o_ref):
            o_ref[...] = _tn(a_ref[...], b_ref[...]).astype(o_ref.dtype)

        return pl.pallas_call(
            body_tn, name=name, grid=(n // tn, k // tk),
            in_specs=[pl.BlockSpec((m, tk), lambda j, p: (0, p)),
                      pl.BlockSpec((m, tn), lambda j, p: (0, j))],
            out_specs=pl.BlockSpec((tk, tn), lambda j, p: (p, j)),
            out_shape=jax.ShapeDtypeStruct((k, n), out_dtype),
            compiler_params=pltpu.CompilerParams(
                dimension_semantics=("parallel", "parallel"), vmem_limit_bytes=VMEM_LIMIT),
        )(a, b)

    m, k = a.shape
    n = b.shape[1] if mode == "nn" else b.shape[0]
    tn = _pick(n, 1408, 128)
    tm = _pick(m, 1152, 16)
    dotf = _nn if mode == "nn" else _nt

    def body(*refs):
        a_ref, b_ref, o_ref = refs[0], refs[1], refs[-1]
        acc = dotf(a_ref[...], b_ref[...])
        if resid is not None:
            acc = refs[2][...] + acc
        o_ref[...] = acc.astype(o_ref.dtype)

    b_spec = (pl.BlockSpec((k, tn), lambda j, i: (0, j)) if mode == "nn"
              else pl.BlockSpec((tn, k), lambda j, i: (j, 0)))
    in_specs = [pl.BlockSpec((tm, k), lambda j, i: (i, 0)), b_spec]
    args = [a, b]
    if resid is not None:
        in_specs.append(pl.BlockSpec((tm, tn), lambda j, i: (i, j)))
        args.append(resid)
    if after is not None:
        in_specs.append(_ANY_SPEC)
        args.append(after)
    return pl.pallas_call(
        body, name=name, grid=(n // tn, m // tm), in_specs=in_specs,
        out_specs=pl.BlockSpec((tm, tn), lambda j, i: (i, j)),
        out_shape=jax.ShapeDtypeStruct((m, n), out_dtype),
        compiler_params=pltpu.CompilerParams(
            dimension_semantics=("parallel", "parallel"), vmem_limit_bytes=VMEM_LIMIT),
    )(*args)


def _norm_mm(name, x, norm_w, bt, after=None):
    m, k = x.shape
    n = bt.shape[0]
    tn = _pick(n, 1408, 128)
    tm = _pick(m, 1152, 16)
    extra = [] if after is None else [after]

    def body(x_ref, w_ref, b_ref, *rest):
        o_ref, u_ref = rest[-2:]

        @pl.when(pl.program_id(1) == 0)
        def _():
            u_ref[...] = _rms_fwd(x_ref[...], w_ref[...], k)[0].astype(u_ref.dtype)

        o_ref[...] = _nt(u_ref[...], b_ref[...])

    return pl.pallas_call(
        body, name=name, grid=(m // tm, n // tn),
        in_specs=[pl.BlockSpec((tm, k), lambda i, j: (i, 0)), pl.BlockSpec((1, k), lambda i, j: (0, 0)),
                  pl.BlockSpec((tn, k), lambda i, j: (j, 0))] + [_ANY_SPEC] * len(extra),
        out_specs=[pl.BlockSpec((tm, tn), lambda i, j: (i, j)), pl.BlockSpec((tm, k), lambda i, j: (i, 0))],
        out_shape=[jax.ShapeDtypeStruct((m, n), F32), jax.ShapeDtypeStruct((m, k), _MXU)],
        compiler_params=pltpu.CompilerParams(
            dimension_semantics=("arbitrary", "arbitrary"), vmem_limit_bytes=VMEM_LIMIT),
    )(x, norm_w, bt, *extra)


def _mm_rows(name, a, b, mode, fn, tiled, full, outs, accs=()):
    m = a.shape[0]
    tm = _pick(m, 576, 16)
    dotf = _nn if mode == "nn" else _nt
    n_in = len(tiled) + len(full)
    n_out = len(outs)

    def body(*refs):
        i = pl.program_id(0)
        vals = [r[...] for r in refs[2:2 + n_in]]
        o_t, o_a = fn(i, dotf(refs[0][...], refs[1][...]), *vals)
        for r, v in zip(refs[2 + n_in:2 + n_in + n_out], o_t):
            r[...] = v.astype(r.dtype)
        for r, v in zip(refs[2 + n_in + n_out:], o_a):
            @pl.when(i == 0)
            def _():
                r[...] = v

            @pl.when(i > 0)
            def _():
                r[...] += v

    whole = lambda x: pl.BlockSpec(x.shape, lambda i, nd=x.ndim: (0,) * nd)
    in_specs = [pl.BlockSpec((tm, a.shape[1]), lambda i: (i, 0)), whole(b)]
    in_specs += [pl.BlockSpec((tm, t.width), lambda i, cb=t.cb: (i, cb)) for t in tiled]
    in_specs += [whole(x) for x in full]
    out_specs = [pl.BlockSpec((tm, w), lambda i: (i, 0)) for w, _ in outs]
    out_specs += [pl.BlockSpec((r, w), lambda i: (0, 0)) for r, w in accs]
    out_shape = [jax.ShapeDtypeStruct((m, w), dt) for w, dt in outs]
    out_shape += [jax.ShapeDtypeStruct((r, w), F32) for r, w in accs]
    return pl.pallas_call(
        body, name=name, grid=(m // tm,), in_specs=in_specs, out_specs=out_specs, out_shape=out_shape,
        compiler_params=pltpu.CompilerParams(dimension_semantics=("arbitrary",), vmem_limit_bytes=VMEM_LIMIT),
    )(a, b, *[t.arr for t in tiled], *full)


ATTN_Q_TILES = 4


def _attn_probs(q, k, row0):
    tq, tp = q.shape[0], k.shape[0]
    s = _nt(q, k) * (1.0 / math.sqrt(QK_HEAD))
    row = row0 + lax.broadcasted_iota(jnp.int32, (tq, tp), 0)
    col = lax.broadcasted_iota(jnp.int32, (tq, tp), 1)
    ok = (col <= row) & (col >= PAD)
    s = jnp.where(ok, s, NEG)
    m = jnp.max(s, axis=-1, keepdims=True)
    e = jnp.exp(s - m)
    e = jnp.where(ok, e, 0.0)
    l = jnp.sum(e, axis=-1, keepdims=True)
    return e / jnp.maximum(l, 1e-30)


def _attn_fwd(q, k, v):
    tp = q.shape[0]
    tq = tp // ATTN_Q_TILES

    def body(q_ref, k_ref, v_ref, o_ref):
        for i in range(ATTN_Q_TILES):
            rows = slice(i * tq, (i + 1) * tq)
            keys = slice(0, (i + 1) * tq)
            p = _attn_probs(q_ref[rows, :], k_ref[keys, :], i * tq)
            o_ref[rows, :] = _nn(p, v_ref[keys, :])

    return pl.pallas_call(
        body, name="attn_fwd", grid=(MLA_HEADS,),
        in_specs=[pl.BlockSpec((tp, HP), lambda h: (0, h)),
                  pl.BlockSpec((tp, HP), lambda h: (0, h)),
                  pl.BlockSpec((tp, V_HEAD), lambda h: (0, h))],
        out_specs=pl.BlockSpec((tp, V_HEAD), lambda h: (0, h)),
        out_shape=jax.ShapeDtypeStruct((tp, MLA_HEADS * V_HEAD), F32),
        compiler_params=pltpu.CompilerParams(dimension_semantics=("parallel",), vmem_limit_bytes=VMEM_LIMIT),
    )(q, k, v)


def _attn_bwd(q, k, v, do):
    tp = q.shape[0]
    tq = tp // ATTN_Q_TILES

    def body(q_ref, k_ref, v_ref, do_ref, dq_ref, dk_ref, dv_ref):
        for i in reversed(range(ATTN_Q_TILES)):
            rows = slice(i * tq, (i + 1) * tq)
            keys = slice(0, (i + 1) * tq)
            qb = q_ref[rows, :]
            kk = k_ref[keys, :]
            dob = do_ref[rows, :]
            p = _attn_probs(qb, kk, i * tq)
            dp = _nt(dob, v_ref[keys, :])
            delta = jnp.sum(p * dp, axis=-1, keepdims=True)
            ds = p * (dp - delta) * (1.0 / math.sqrt(QK_HEAD))
            dq_ref[rows, :] = _nn(ds, kk)
            if i == ATTN_Q_TILES - 1:
                dk_ref[...] = _tn(ds, qb)
                dv_ref[...] = _tn(p, dob)
            else:
                dk_ref[keys, :] += _tn(ds, qb)
                dv_ref[keys, :] += _tn(p, dob)

    full = lambda w: pl.BlockSpec((tp, w), lambda h: (0, h))
    return pl.pallas_call(
        body, name="attn_bwd", grid=(MLA_HEADS,),
        in_specs=[full(HP), full(HP), full(V_HEAD), full(V_HEAD)],
        out_specs=[full(HP), full(HP), full(V_HEAD)],
        out_shape=[jax.ShapeDtypeStruct((tp, MLA_HEADS * HP), F32),
                   jax.ShapeDtypeStruct((tp, MLA_HEADS * HP), F32),
                   jax.ShapeDtypeStruct((tp, MLA_HEADS * V_HEAD), F32)],
        compiler_params=pltpu.CompilerParams(dimension_semantics=("parallel",), vmem_limit_bytes=VMEM_LIMIT),
    )(q, k, v, do)


def _gdn_consts():
    c = DN_CHUNK
    r = lax.broadcasted_iota(jnp.int32, (c, c), 0)
    cc = lax.broadcasted_iota(jnp.int32, (c, c), 1)
    incl = r >= cc
    strict = r > cc
    return incl, strict


def _each(fn, *lists):
    return [fn(*a) for a in zip(*lists)]


def _interleave(chains):
    chains = list(chains)
    while chains:
        for ch in list(chains):
            try:
                next(ch)
            except StopIteration:
                chains.remove(ch)


def _gdn_chunk_common(q_ref, k_ref, v_ref, g_ref, b_ref):
    c = DN_CHUNK
    incl, strict = _gdn_consts()
    sls = [slice(DN_DIM * h, DN_DIM * (h + 1)) for h in range(DN_HEADS)]
    inclf = incl.astype(F32)
    ones = jnp.full((c, LANE), 1.0 / LANE, F32)
    q = [q_ref[:, sl] * (1.0 / math.sqrt(DN_DIM)) for sl in sls]
    k = [k_ref[:, sl] for sl in sls]
    v = [v_ref[:, sl] for sl in sls]
    g = [g_ref[:, sl] for sl in sls]
    beta = [b_ref[:, sl] for sl in sls]
    gc = [_nn(inclf, x, hp=True) for x in g]
    grow = [_nt(ones, x, hp=True) for x in gc]
    kb = _each(jnp.multiply, k, beta)
    kk = _each(_nt, kb, k)
    qk = _each(_nt, q, k)
    gam = [jnp.exp(x) for x in gc]
    g_last = [_rowsum(x) for x in g]
    dm = [jnp.exp(jnp.where(incl, x[:, :c] - y, NEG)) for x, y in zip(gc, grow)]
    vb = _each(jnp.multiply, v, beta)
    kbg = _each(jnp.multiply, kb, gam)
    ek = [jnp.exp(x - y) for x, y in zip(g_last, gc)]
    kd = _each(jnp.multiply, k, ek)
    return dict(q=q, k=k, v=v, beta=beta, gc=gc, gam=gam, g_last=g_last, dm=dm, kb=kb, vb=vb,
                kbg=kbg, kk=kk, ek=ek, kd=kd, qk=qk, incl=incl, strict=strict, sls=sls)


def _gdn_fwd(q, k, v, g, beta):
    tp = q.shape[0]
    c = DN_CHUNK
    nch = tp // c

    def body(q_ref, k_ref, v_ref, g_ref, b_ref, o_ref, s_ref, t_ref, s_scr):
        @pl.when(pl.program_id(0) == 0)
        def _():
            s_scr[...] = jnp.zeros_like(s_scr)

        eye = (lax.broadcasted_iota(jnp.int32, (c, c), 0) == lax.broadcasted_iota(jnp.int32, (c, c), 1)).astype(F32)
        x = _gdn_chunk_common(q_ref, k_ref, v_ref, g_ref, b_ref)
        heads = range(DN_HEADS)
        s = [s_scr[h] for h in heads]
        bp = [-jnp.where(x["strict"], kk * dm, 0.0) for kk, dm in zip(x["kk"], x["dm"])]
        t = [eye + b for b in bp]
        for _ in range(5):
            bp = [_nn(b, b, hp="3x") for b in bp]
            t = [tt + _nn(tt, b, hp="3x") for tt, b in zip(t, bp)]
        u = _each(_nn, t, x["vb"])
        w = _each(_nn, t, x["kbg"])
        v_new = [uu - _nn(ww, ss) for uu, ww, ss in zip(u, w, s)]
        o = [_nn(q * gam, ss) + _nn(qk * dm, vn)
             for q, gam, ss, qk, dm, vn in zip(x["q"], x["gam"], s, x["qk"], x["dm"], v_new)]
        s_new = [ss * jnp.exp(gl) + _tn(kd, vn) for ss, gl, kd, vn in zip(s, x["g_last"], x["kd"], v_new)]
        for h in heads:
            s_ref[h, 0] = s[h]
            t_ref[h, 0] = t[h]
            o_ref[:, x["sls"][h]] = o[h]
            s_scr[h] = s_new[h]

    rb = lambda n: (n, 0)
    return pl.pallas_call(
        body, name="gdn_fwd", grid=(nch,),
        in_specs=[pl.BlockSpec((c, DN_WIDTH), rb)] * 5,
        out_specs=[pl.BlockSpec((c, DN_WIDTH), rb),
                   pl.BlockSpec((DN_HEADS, 1, DN_DIM, DN_DIM), lambda n: (0, n, 0, 0)),
                   pl.BlockSpec((DN_HEADS, 1, c, c), lambda n: (0, n, 0, 0))],
        out_shape=[jax.ShapeDtypeStruct((tp, DN_WIDTH), F32),
                   jax.ShapeDtypeStruct((DN_HEADS, nch, DN_DIM, DN_DIM), F32),
                   jax.ShapeDtypeStruct((DN_HEADS, nch, c, c), F32)],
        scratch_shapes=[pltpu.VMEM((DN_HEADS, DN_DIM, DN_DIM), F32)],
        compiler_params=pltpu.CompilerParams(dimension_semantics=("arbitrary",), vmem_limit_bytes=VMEM_LIMIT),
    )(q, k, v, g, beta)


def _gdn_bwd(q, k, v, g, beta, s_all, t_all, do):
    tp = q.shape[0]
    c = DN_CHUNK
    nch = tp // c

    def body(q_ref, k_ref, v_ref, g_ref, b_ref, s_ref, t_ref, do_ref,
             dq_ref, dk_ref, dv_ref, dg_ref, db_ref, ds_scr):
        @pl.when(pl.program_id(0) == 0)
        def _():
            ds_scr[...] = jnp.zeros_like(ds_scr)

        ones_cl = jnp.ones((c, LANE), F32)
        xs = _gdn_chunk_common(q_ref, k_ref, v_ref, g_ref, b_ref)
        upper = jnp.logical_not(xs["strict"]).astype(F32)

        def chain(h):
            x = {key: (val[h] if isinstance(val, list) else val) for key, val in xs.items()}
            sl = x["sls"]
            qs, kx, vx, beta_, gam, dm = x["q"], x["k"], x["v"], x["beta"], x["gam"], x["dm"]
            kb, vb, kbg, kd, ek = x["kb"], x["vb"], x["kbg"], x["kd"], x["ek"]
            t = t_ref[h, 0]
            s = s_ref[h, 0]
            dsn = ds_scr[h]
            dob = do_ref[:, sl]
            eg_last = jnp.exp(x["g_last"])
            u = _nn(t, vb)
            w = _nn(t, kbg)
            mqk = x["qk"] * dm
            qd = qs * gam
            dqd = _nt(dob, s)
            dkd_pre = _nn(kd, dsn)
            yield
            v_new = u - _nn(w, s)
            dv_new = _tn(mqk, dob) + dkd_pre
            dq = dqd * gam
            dgam = jnp.sum(dqd * qs, axis=1, keepdims=True)
            yield
            ds_new = _tn(qd, dob) + eg_last * dsn - _tn(w, dv_new)
            dmm = jnp.where(x["incl"], _nt(dob, v_new), 0.0)
            dkd = _nt(v_new, dsn)
            dw = -_nt(dv_new, s)
            dvb = _tn(t, dv_new)
            dt = _nt(dv_new, vb)
            yield
            dqk = dmm * dm
            e_mat = dmm * mqk
            dq = dq + _nn(dqk, kx)
            dk = _tn(dqk, qs) + dkd * ek
            e1 = jnp.sum(dkd * kd, axis=1, keepdims=True)
            dgc = -e1
            dg_last = jnp.sum(e1) + eg_last * jnp.sum(s * dsn)
            dt = dt + _nt(dw, kbg)
            dkbg = _tn(t, dw)
            yield
            tdt = _tn(t, dt, hp="3x")
            yield
            da = jnp.where(x["strict"], -_nt(tdt, t, hp="3x"), 0.0)
            yield
            dkk = da * dm
            e_mat = e_mat + da * x["kk"] * dm
            dkb = _nn(dkk, kx) + dkbg * gam
            dk = dk + _tn(dkk, kb)
            dgam = dgam + jnp.sum(dkbg * kb, axis=1, keepdims=True)
            yield
            dk = dk + dkb * beta_
            dbeta = jnp.sum(dkb * kx, axis=1, keepdims=True) + jnp.sum(dvb * vx, axis=1, keepdims=True)
            dv = dvb * beta_
            dgc = dgc + jnp.sum(e_mat, axis=1, keepdims=True) + dgam * gam
            dgc = dgc - _tn(e_mat, ones_cl, hp="3x")
            yield
            dg = _nn(upper, dgc, hp="3x") + dg_last
            yield
            ds_scr[h] = ds_new
            dq_ref[:, sl] = dq * (1.0 / math.sqrt(DN_DIM))
            dk_ref[:, sl] = dk
            dv_ref[:, sl] = dv
            dg_ref[:, sl] = dg
            db_ref[:, sl] = jnp.broadcast_to(dbeta, (c, LANE))

        _interleave([chain(h) for h in range(DN_HEADS)])

    rb = lambda n: (nch - 1 - n, 0)
    hs = lambda n: (0, nch - 1 - n, 0, 0)
    return pl.pallas_call(
        body, name="gdn_bwd", grid=(nch,),
        in_specs=[pl.BlockSpec((c, DN_WIDTH), rb)] * 5
        + [pl.BlockSpec((DN_HEADS, 1, DN_DIM, DN_DIM), hs), pl.BlockSpec((DN_HEADS, 1, c, c), hs),
           pl.BlockSpec((c, DN_WIDTH), rb)],
        out_specs=[pl.BlockSpec((c, DN_WIDTH), rb)] * 5,
        out_shape=[jax.ShapeDtypeStruct((tp, DN_WIDTH), F32)] * 5,
        scratch_shapes=[pltpu.VMEM((DN_HEADS, DN_DIM, DN_DIM), F32)],
        compiler_params=pltpu.CompilerParams(dimension_semantics=("arbitrary",), vmem_limit_bytes=VMEM_LIMIT),
    )(q, k, v, g, beta, s_all, t_all, do)


def _silu_parts(x):
    s = _sigmoid(x)
    return x * s, s * (1.0 + x * (1.0 - s))


def _f_rms_cast(i, x, w):
    y, _ = _rms_fwd(x, w, x.shape[1])
    return (y,), ()


def _f_rms_bwd_add(i, x, dy, dres, w, *, mask_pad):
    dx, dwr = _rms_bwd(x, w, dy, x.shape[1])
    out = dres + dx
    if mask_pad:
        out = jnp.where(_row_ids(i, x.shape[0]) >= PAD, out, 0.0)
    return (out,), (_rowsum(dwr),)


def _f_lat_norm(i, ql, kvl, qw, kvw):
    return (_rms_fwd(ql, qw, Q_LORA)[0], _rms_fwd(kvl, kvw, KV_LORA)[0]), ()


def _f_lat_norm_bwd(i, ql, kvl, dqn, dkvn, qw, kvw):
    dq, dqw = _rms_bwd(ql, qw, dqn, Q_LORA)
    dk, dkw = _rms_bwd(kvl, kvw, dkvn, KV_LORA)
    return (dq, dk), (_rowsum(dqw), _rowsum(dkw))


def _rope(x, cos, sin_s):
    return x * cos + _swap_halves(x) * sin_s


def _rope_t(dy, cos, sin_s):
    return dy * cos + _swap_halves(dy * sin_s)


def _f_mla_qk(i, qf, kvf, kpe, cos, sin_s, qw, kw):
    qs, ks, vs = [], [], []
    for h in range(MLA_HEADS):
        qn, _ = _rms_fwd(qf[:, HP * h:HP * (h + 1)], qw, QK_HEAD)
        qs += [qn[:, :QK_NOPE], _rope(qn[:, QK_NOPE:], cos, sin_s)]
        kh = jnp.concatenate([kvf[:, HP * h:HP * h + QK_NOPE], kpe], axis=1)
        kn, _ = _rms_fwd(kh, kw, QK_HEAD)
        ks += [kn[:, :QK_NOPE], _rope(kn[:, QK_NOPE:], cos, sin_s)]
        vs.append(kvf[:, HP * h + QK_NOPE:HP * (h + 1)])
    return (jnp.concatenate(qs, axis=1), jnp.concatenate(ks, axis=1), jnp.concatenate(vs, axis=1)), ()


def _f_mla_qk_bwd(i, qf, kvf, kpe, cos, sin_s, dq, dk, dv, qw, kw):
    dqf, dkvf = [], []
    dkpe = None
    dqw = None
    dkw = None
    for h in range(MLA_HEADS):
        dqh = dq[:, HP * h:HP * (h + 1)]
        dqn = jnp.concatenate([dqh[:, :QK_NOPE], _rope_t(dqh[:, QK_NOPE:], cos, sin_s)], axis=1)
        dx, dwr = _rms_bwd(qf[:, HP * h:HP * (h + 1)], qw, dqn, QK_HEAD)
        dqf.append(dx)
        dqw = _rowsum(dwr) if dqw is None else dqw + _rowsum(dwr)
        dkh = dk[:, HP * h:HP * (h + 1)]
        dkn = jnp.concatenate([dkh[:, :QK_NOPE], _rope_t(dkh[:, QK_NOPE:], cos, sin_s)], axis=1)
        kh = jnp.concatenate([kvf[:, HP * h:HP * h + QK_NOPE], kpe], axis=1)
        dx, dwr = _rms_bwd(kh, kw, dkn, QK_HEAD)
        dkvf += [dx[:, :QK_NOPE], dv[:, V_HEAD * h:V_HEAD * (h + 1)]]
        dkpe = dx[:, QK_NOPE:] if dkpe is None else dkpe + dx[:, QK_NOPE:]
        dkw = _rowsum(dwr) if dkw is None else dkw + _rowsum(dwr)
    return (jnp.concatenate(dqf, axis=1), jnp.concatenate(dkvf, axis=1), dkpe), (dqw, dkw)


def _gdn_act(i, x, halo, w8):
    tm = x.shape[0]
    halo = jnp.where(i > 0, halo, 0.0)
    c = _conv_fwd(x, halo, w8, DN_CONV)
    act, dact = _silu_parts(c)
    return act, dact


def _f_gdn_prep(i, x, halo, ab, w8, alog, dtb, sel):
    tm = x.shape[0]
    act, _ = _gdn_act(i, x, halo, w8)
    outs = []
    for part in range(2):
        for h in range(DN_HEADS):
            t = act[:, DN_WIDTH * part + DN_DIM * h:DN_WIDTH * part + DN_DIM * (h + 1)]
            outs.append(t * lax.rsqrt(jnp.sum(t * t, axis=-1, keepdims=True) + EPS))
    q = jnp.concatenate(outs[:DN_HEADS], axis=1)
    k = jnp.concatenate(outs[DN_HEADS:], axis=1)
    v = act[:, 2 * DN_WIDTH:]
    abb = _nn(ab, sel, hp=True)
    valid = _row_ids(i, tm) >= PAD
    g = jnp.where(valid, -jnp.exp(alog) * _softplus(abb[:, :DN_WIDTH] + dtb), 0.0)
    beta = jnp.where(valid, _sigmoid(abb[:, DN_WIDTH:]), 0.0)
    return (q, k, v, g, beta), ()


def _f_gdn_prep_bwd(i, x, halo, ab, dq, dk, dv, dg, dbeta, w8, alog, dtb, sel, selpick):
    tm = x.shape[0]
    act, dact = _gdn_act(i, x, halo, w8)
    douts = []
    for part, dd in enumerate((dq, dk)):
        for h in range(DN_HEADS):
            t = act[:, DN_WIDTH * part + DN_DIM * h:DN_WIDTH * part + DN_DIM * (h + 1)]
            r = lax.rsqrt(jnp.sum(t * t, axis=-1, keepdims=True) + EPS)
            y = t * r
            dy = dd[:, DN_DIM * h:DN_DIM * (h + 1)]
            douts.append(r * (dy - y * jnp.sum(dy * y, axis=-1, keepdims=True)))
    douts.append(dv)
    dc = jnp.concatenate(douts, axis=1) * dact
    abb = _nn(ab, sel, hp=True)
    valid = _row_ids(i, tm) >= PAD
    pre = abb[:, :DN_WIDTH] + dtb
    ea = jnp.exp(alog)
    g = -ea * _softplus(pre)
    dg = jnp.where(valid, dg, 0.0)
    dbeta = jnp.where(valid, dbeta, 0.0)
    da = dg * (-ea) * _sigmoid(pre)
    beta = _sigmoid(abb[:, DN_WIDTH:])
    db = dbeta * beta * (1.0 - beta)
    dab = _nn(jnp.concatenate([da, db], axis=1), selpick, hp=True)
    return (dc, dab), (_rowsum(dg * g), _rowsum(da))


def _f_conv_bwd(i, dy, dy_next, x, x_prev, w8, *, width, nt):
    dy_next = jnp.where(i < nt - 1, dy_next, 0.0)
    x_prev = jnp.where(i > 0, x_prev, 0.0)
    return (_conv_bwd_x(dy, dy_next, w8, width),), (_conv_bwd_w(dy, x, x_prev, width),)


def _f_mix(i, o_mla, o_dn, z, w_mla, w_dn):
    tm = o_mla.shape[0]
    valid = _row_ids(i, tm) >= PAD
    outs = []
    for h in range(MLA_HEADS):
        y, _ = _rms_fwd(o_mla[:, V_HEAD * h:V_HEAD * (h + 1)], w_mla, V_HEAD)
        outs.append(jnp.where(valid, y, 0.0))
    for h in range(DN_HEADS):
        y, _ = _rms_fwd(o_dn[:, DN_DIM * h:DN_DIM * (h + 1)], w_dn, DN_DIM)
        outs.append(y * _silu_parts(z[:, DN_DIM * h:DN_DIM * (h + 1)])[0])
    return (jnp.concatenate(outs, axis=1),), ()


def _f_mix_bwd(i, o_mla, o_dn, z, dy_mla, dy_dn, w_mla, w_dn):
    tm = o_mla.shape[0]
    valid = _row_ids(i, tm) >= PAD
    d_mla, d_dn, d_z = [], [], []
    dw_mla = None
    dw_dn = None
    for h in range(MLA_HEADS):
        sl = slice(V_HEAD * h, V_HEAD * (h + 1))
        dx, dwr = _rms_bwd(o_mla[:, sl], w_mla, jnp.where(valid, dy_mla[:, sl], 0.0), V_HEAD)
        d_mla.append(dx)
        dw_mla = _rowsum(dwr) if dw_mla is None else dw_mla + _rowsum(dwr)
    for h in range(DN_HEADS):
        sl = slice(DN_DIM * h, DN_DIM * (h + 1))
        y, _ = _rms_fwd(o_dn[:, sl], w_dn, DN_DIM)
        sz, dsz = _silu_parts(z[:, sl])
        d_z.append(dy_dn[:, sl] * y * dsz)
        dx, dwr = _rms_bwd(o_dn[:, sl], w_dn, dy_dn[:, sl] * sz, DN_DIM)
        d_dn.append(dx)
        dw_dn = _rowsum(dwr) if dw_dn is None else dw_dn + _rowsum(dwr)
    return ((jnp.concatenate(d_mla, axis=1), jnp.concatenate(d_dn, axis=1), jnp.concatenate(d_z, axis=1)),
            (dw_mla, dw_dn))


def _f_ffn_act(i, gate_pre, halo, up, w8, b):
    halo = jnp.where(i > 0, halo, 0.0)
    gate = _conv_fwd(gate_pre, halo, w8, FFN_CONV) + b
    return (_silu_parts(gate)[0] * up,), ()


def _f_ffn_act_bwd(i, gp, gp_prev, gp_next, up, up_next, dact, dact_next, w8, b, *, nt):
    tm = gp.shape[0]
    gp_prev = jnp.where(i > 0, gp_prev, 0.0)
    dact_next = jnp.where(i < nt - 1, dact_next, 0.0)
    cat = lambda t, t_next: jnp.concatenate([t, t_next], axis=0)
    gate = _conv_fwd(cat(gp, gp_next), gp_prev, w8, FFN_CONV) + b
    sg, dsg = _silu_parts(gate)
    dact_e = cat(dact, dact_next)
    dgate = dact_e * cat(up, up_next) * dsg
    dgate_pre = _conv_bwd_x(dgate[:tm], dgate[tm:], w8, FFN_CONV)
    dup = dact * sg[:tm]
    return (dgate_pre, dup), (_conv_bwd_w(dgate[:tm], gp, gp_prev, FFN_CONV), _rowsum(dgate[:tm]))


def _f_loss(i, h3, tgt):
    tm = h3.shape[0]
    diff = jnp.where(_row_ids(i, tm) >= ROW0, h3 - tgt, 0.0)
    part = 0.5 * jnp.sum(diff * diff) * (1.0 / D_MODEL)
    return (diff * (1.0 / D_MODEL),), (jnp.full((1, LANE), part, F32),)


def _after(fn):
    return lambda i, *a: fn(i, *a[:-1])


def _local_step(h0, tgt, w, token, late_weights, grads_ready):
    tp = h0.shape[0]
    nt = tp // TM
    bf = (D_MODEL, _MXU)
    proj, u = _norm_mm("in_proj", h0, w["attn_norm_w"], w["w_in"], after=token)
    p_qkv = lambda kind="cur": _In(proj, 3 * DN_WIDTH, 0, kind)
    p_z = _In(proj, DN_WIDTH, C_Z // DN_WIDTH)
    p_ql = _In(proj, Q_LORA, C_QL // Q_LORA)
    p_kvl = _In(proj, KV_LORA, C_KVL // KV_LORA)
    p_kpe = _In(proj, LANE, C_KPE // LANE)
    p_ab = _In(proj, LANE, C_AB // LANE)
    cos, sin_s = _In(w["cos"]), _In(w["sin_s"])

    qn, kvn = _rows("mla_lat_norm", _f_lat_norm, [p_ql, p_kvl], [w["q_a_norm_w"], w["kv_a_norm_w"]],
                    [(Q_LORA, _MXU), (KV_LORA, _MXU)])
    qf = _mm("mla_q_b", qn, w["w_q_b"], "nn")
    kvf = _mm("mla_kv_b", kvn, w["w_kv_b"], "nn")
    qk_w = [w["q_norm_w"], w["k_norm_w"]]
    q, k, v = _rows("mla_qk", _f_mla_qk, [_In(qf), _In(kvf), p_kpe, cos, sin_s], qk_w,
                    [(MLA_HEADS * HP, _MXU), (MLA_HEADS * HP, _MXU), (MLA_HEADS * V_HEAD, _MXU)])
    o_mla = _attn_fwd(q, k, v)

    dn_w = [w["dn_conv_w"], w["alog_b"], w["dtb_b"], w["sel"]]
    gq, gk, gv, gg, gb = _rows("gdn_prep", _f_gdn_prep, [p_qkv(), p_qkv("prev"), p_ab], dn_w,
                               [(DN_WIDTH, F32)] * 5)
    o_dn, s_all, t_all = _gdn_fwd(gq, gk, gv, gg, gb)

    out_w = [w["mla_out_norm_w"], w["dn_out_norm_w"]]
    mixed, = _rows("mix", _f_mix, [_In(o_mla), _In(o_dn), p_z], out_w, [bf])
    w = dict(w, **late_weights(mixed))
    h2 = _mm("out_proj", mixed, w["w_out"], "nn", resid=h0)

    gate_pre, hn = _norm_mm("ffn_gate", h2, w["ffn_norm_w"], w["w_gate"])
    up = _mm("ffn_up", hn, w["w_up"], "nt")
    ffn_w = [w["ffn_conv_w"], w["ffn_conv_b"]]
    act, = _rows("ffn_act", _f_ffn_act, [_In(gate_pre), _In(gate_pre, kind="prev"), _In(up)], ffn_w,
                 [(D_FF, _MXU)])
    dh3, loss = _mm_rows("ffn_down_loss", act, w["w_down"], "nn", lambda i, y, r, t: _f_loss(i, r + y, t),
                         [_In(h2), _In(tgt)], [], [(D_MODEL, F32)], [(1, LANE)])

    g = {}
    dact = _mm("ffn_down_dx", dh3, w["w_down"], "nt")
    g["w_down"] = _mm("ffn_down_dw", act, dh3, "tn", out_dtype=_MXU)
    dgate_pre, dup, g["ffn_conv_w"], g["ffn_conv_b"] = _rows(
        "ffn_act_bwd", functools.partial(_f_ffn_act_bwd, nt=nt),
        [_In(gate_pre), _In(gate_pre, kind="prev"), _In(gate_pre, kind="next"), _In(up), _In(up, kind="next"),
         _In(dact), _In(dact, kind="next")], ffn_w,
        [(D_FF, _MXU), (D_FF, _MXU)], [(8, D_FF), (1, D_FF)])
    g["w_gate"] = _mm("ffn_gate_dw", dgate_pre, hn, "tn", out_dtype=_MXU)
    g["w_up"] = _mm("ffn_up_dw", dup, hn, "tn", out_dtype=_MXU)
    tok = grads_ready(g, ("w_down", "w_gate", "w_up"))
    dhn = _mm("ffn_gate_dx", dgate_pre, w["w_gate"], "nn", after=tok)
    dh2, g["ffn_norm_w"] = _mm_rows(
        "ffn_up_dx_rms", dup, w["w_up"], "nn",
        lambda i, y, d1, x, dres, nw: _f_rms_bwd_add(i, x, d1 + y, dres, nw, mask_pad=True),
        [_In(dhn), _In(h2), _In(dh3)], [w["ffn_norm_w"]], [(D_MODEL, F32)], [(1, D_MODEL)])

    dmixed = _mm("out_proj_dx", dh2, w["w_out"], "nt")
    g["w_out"] = _mm("out_proj_dw", mixed, dh2, "tn", out_dtype=_MXU)
    half = MLA_HEADS * V_HEAD
    do_mla, do_dn, dz, g["mla_out_norm_w"], g["dn_out_norm_w"] = _rows(
        "mix_bwd", _f_mix_bwd, [_In(o_mla), _In(o_dn), p_z, _In(dmixed, half, 0), _In(dmixed, half, 1)], out_w,
        [(half, F32), (DN_WIDTH, F32), (DN_WIDTH, _MXU)], [(1, V_HEAD), (1, DN_DIM)])

    dq, dk, dv = _attn_bwd(q, k, v, do_mla)
    dqf, dkvf, dkpe, g["q_norm_w"], g["k_norm_w"] = _rows(
        "mla_qk_bwd", _f_mla_qk_bwd, [_In(qf), _In(kvf), p_kpe, cos, sin_s, _In(dq), _In(dk), _In(dv)], qk_w,
        [(MLA_HEADS * HP, _MXU), (MLA_HEADS * HP, _MXU), (LANE, _MXU)], [(1, HP), (1, HP)])
    dqn = _mm("mla_q_b_dx", dqf, w["w_q_b"], "nt")
    g["w_q_b"] = _mm("mla_q_b_dw", qn, dqf, "tn")
    dkvn = _mm("mla_kv_b_dx", dkvf, w["w_kv_b"], "nt")
    g["w_kv_b"] = _mm("mla_kv_b_dw", kvn, dkvf, "tn")
    tok = grads_ready(g, ("w_out", "w_q_b", "w_kv_b"))
    dql, dkvl, g["q_a_norm_w"], g["kv_a_norm_w"] = _rows(
        "mla_lat_norm_bwd", _after(_f_lat_norm_bwd), [p_ql, p_kvl, _In(dqn), _In(dkvn)],
        [w["q_a_norm_w"], w["kv_a_norm_w"], tok],[(Q_LORA, _MXU), (KV_LORA, _MXU)], [(1, Q_LORA), (1, KV_LORA)])

    dgq, dgk, dgv, dgg, dgb = _gdn_bwd(gq, gk, gv, gg, gb, s_all, t_all, do_dn)
    dc, dab, g["alog_b"], g["dtb_b"] = _rows(
        "gdn_prep_bwd", _f_gdn_prep_bwd,
        [p_qkv(), p_qkv("prev"), p_ab, _In(dgq), _In(dgk), _In(dgv), _In(dgg), _In(dgb)], dn_w + [w["selpick"]],
        [(3 * DN_WIDTH, F32), (LANE, _MXU)], [(1, DN_WIDTH), (1, DN_WIDTH)])
    dqkv, g["dn_conv_w"] = _rows(
        "gdn_conv_bwd", functools.partial(_f_conv_bwd, width=DN_CONV, nt=nt),
        [_In(dc), _In(dc, kind="next"), p_qkv(), p_qkv("prev")], [w["dn_conv_w"]],
        [(3 * DN_WIDTH, _MXU)], [(8, 3 * DN_WIDTH)])

    dproj = jnp.concatenate([dqkv, dz, dql, dkvl, dkpe, dab], axis=1)
    g["w_in"] = _mm("in_proj_dw", dproj, u, "tn", out_dtype=_MXU)
    tok = grads_ready(g, ("w_in",))
    dh0, g["attn_norm_w"] = _mm_rows(
        "in_proj_dx_rms", dproj, w["w_in"], "nn",
        lambda i, du, x, dres, nw, _tok: _f_rms_bwd_add(i, x, du, dres, nw, mask_pad=False),
        [_In(h0), _In(dh2)], [w["attn_norm_w"], tok], [(D_MODEL, F32)], [(1, D_MODEL)])
    return loss, dh0, g


def _w_in_to_padded(w):
    c1, c2, c3 = Q_LORA, Q_LORA + KV_LORA, Q_LORA + KV_LORA + QK_ROPE
    c4 = c3 + 3 * DN_WIDTH
    c5 = c4 + DN_WIDTH
    z = lambda n: jnp.zeros((n, w.shape[1]), w.dtype)
    return jnp.concatenate([w[c3:c4], w[c4:c5], w[:c1], w[c1:c2], w[c2:c3], z(LANE - QK_ROPE),
                            w[c5:], z(LANE - 2 * DN_HEADS)], axis=0)


def _w_in_from_padded(g):
    return jnp.concatenate([g[C_QL:C_QL + Q_LORA], g[C_KVL:C_KVL + KV_LORA], g[C_KPE:C_KPE + QK_ROPE],
                            g[:C_Z + DN_WIDTH], g[C_AB:C_AB + 2 * DN_HEADS]], axis=0)


def _w_q_b_to_padded(w):
    r = w.shape[0]
    w = w.reshape(r, MLA_HEADS, QK_HEAD)
    return jnp.pad(w, ((0, 0), (0, 0), (0, HP - QK_HEAD))).reshape(r, MLA_HEADS * HP)


def _w_q_b_from_padded(g):
    r = g.shape[0]
    return g.reshape(r, MLA_HEADS, HP)[:, :, :QK_HEAD].reshape(r, MLA_HEADS * QK_HEAD)


def _pad_rows8(w):
    return jnp.pad(w, ((0, 8 - w.shape[0]), (0, 0)))


def _prepare(full, tp):
    w = {}
    mx = lambda a: a.astype(_MXU)
    w["attn_norm_w"] = full["attn_norm_w"]
    w["w_in"] = mx(_w_in_to_padded(full["w_in"]))
    w["q_a_norm_w"] = full["q_a_norm_w"]
    w["kv_a_norm_w"] = full["kv_a_norm_w"]
    w["w_q_b"] = mx(_w_q_b_to_padded(full["w_q_b"]))
    w["w_kv_b"] = mx(full["w_kv_b"])
    w["q_norm_w"] = jnp.pad(full["q_norm_w"], ((0, 0), (0, HP - QK_HEAD)))
    w["k_norm_w"] = jnp.pad(full["k_norm_w"], ((0, 0), (0, HP - QK_HEAD)))
    w["mla_out_norm_w"] = full["mla_out_norm_w"]
    w["dn_out_norm_w"] = full["dn_out_norm_w"]
    w["dn_conv_w"] = _pad_rows8(full["dn_conv_w"])
    w["alog_b"] = jnp.repeat(full["dn_A_log"], DN_DIM, axis=1)
    w["dtb_b"] = jnp.repeat(full["dn_dt_bias"], DN_DIM, axis=1)
    w["ffn_norm_w"] = full["ffn_norm_w"]
    w["ffn_conv_w"] = _pad_rows8(full["ffn_conv_w"])
    w["ffn_conv_b"] = full["ffn_conv_b"]
    for n in _LATE:
        if n in full:
            w[n] = mx(full[n])
    half = QK_ROPE // 2
    inv = ROPE_THETA ** (-jnp.arange(half, dtype=F32) / half)
    ang = (jnp.arange(tp, dtype=jnp.int32) - PAD).astype(F32)[:, None] * inv[None, :]
    zc = jnp.zeros((tp, LANE - QK_ROPE), F32)
    w["cos"] = jnp.concatenate([jnp.cos(ang), jnp.cos(ang), zc], axis=1)
    w["sin_s"] = jnp.concatenate([-jnp.sin(ang), jnp.sin(ang), zc], axis=1)
    lane = jnp.arange(2 * DN_WIDTH)[None, :]
    src = jnp.arange(LANE)[:, None]
    w["sel"] = ((lane // DN_DIM) == src).astype(F32)
    w["selpick"] = ((src.T == (lane.T // DN_DIM)) & (lane.T % DN_DIM == 0)).astype(F32)
    return w


def _grads_to_natural(g):
    convert = {
        "w_in": ("w_in", _w_in_from_padded),
        "w_q_b": ("w_q_b", _w_q_b_from_padded),
        "q_norm_w": ("q_norm_w", lambda a: a[:, :QK_HEAD]),
        "k_norm_w": ("k_norm_w", lambda a: a[:, :QK_HEAD]),
        "dn_conv_w": ("dn_conv_w", lambda a: a[:DN_CONV]),
        "ffn_conv_w": ("ffn_conv_w", lambda a: a[:FFN_CONV]),
        "alog_b": ("dn_A_log", lambda a: a[:, ::DN_DIM]),
        "dtb_b": ("dn_dt_bias", lambda a: a[:, ::DN_DIM]),
    }
    n = {}
    for key, a in g.items():
        name, fn = convert.get(key, (key, lambda t: t))
        n[name] = fn(a)
    return n


_MESH = pl.DeviceIdType.MESH
_ANY = pl.BlockSpec(memory_space=pl.ANY)
_CHIP_FLIPS = ((1, 0), (0, 1), (1, 1))


def _me():
    return lax.axis_index("x"), lax.axis_index("y"), lax.axis_index("c")


def _all_gather(name, blk):
    def body(x_ref, out_ref, send_sems, recv_sems, local_sem):
        x, y, c = _me()
        me, sib = (x, y, c), (x, y, 1 - c)
        chips = [(x ^ fx, y ^ fy) for fx, fy in _CHIP_FLIPS]

        def slot(p):
            return out_ref.at[4 * p[0] + 2 * p[1] + p[2]]

        def copy(k, block, to, src=None):
            return pltpu.make_async_remote_copy(
                src_ref=slot(block) if src is None else src, dst_ref=slot(block),
                send_sem=send_sems.at[k], recv_sem=recv_sems.at[k], device_id=to, device_id_type=_MESH)

        mine = pltpu.make_async_copy(x_ref, slot(me), local_sem)
        mine.start()
        first = [copy(0, me, sib, src=x_ref)]
        first += [copy(1 + j, me, (*chip, c), src=x_ref) for j, chip in enumerate(chips)]
        for cp in first:
            cp.start()
        passed = [copy(4 + j, (*chip, c), sib) for j, chip in enumerate(chips)]
        for j, chip in enumerate(chips):
            copy(1 + j, (*chip, c), me).wait_recv()
            passed[j].start()
        copy(0, sib, me).wait_recv()
        for j, chip in enumerate(chips):
            copy(4 + j, (*chip, 1 - c), me).wait_recv()
        for cp in first + passed:
            cp.wait_send()
        mine.wait()

    return pl.pallas_call(
        body, name=name, in_specs=[_ANY], out_specs=_ANY,
        out_shape=jax.ShapeDtypeStruct((N_DEV,) + blk.shape, blk.dtype),
        scratch_shapes=[pltpu.SemaphoreType.DMA((7,)), pltpu.SemaphoreType.DMA((7,)), pltpu.SemaphoreType.DMA],
    )(blk)


def _rs_sibling(name, gb):
    def body(g_ref, out_ref, send_sems, recv_sems):
        x, y, c = _me()
        cps = []
        for j in range(4):
            cp = pltpu.make_async_remote_copy(
                src_ref=g_ref.at[2 * j + (1 - c)], dst_ref=out_ref.at[j], send_sem=send_sems.at[j],
                recv_sem=recv_sems.at[j], device_id=(x, y, 1 - c), device_id_type=_MESH)
            cp.start()
            cps.append(cp)
        for cp in cps:
            cp.wait()

    return pl.pallas_call(
        body, name=name, in_specs=[_ANY], out_specs=_ANY,
        out_shape=jax.ShapeDtypeStruct((4,) + gb.shape[1:], gb.dtype),
        scratch_shapes=[pltpu.SemaphoreType.DMA((4,)), pltpu.SemaphoreType.DMA((4,))],
    )(gb)


def _rs_chips(name, s1):
    def body(s_ref, out_ref, send_sems, recv_sems):
        x, y, c = _me()
        cps = []
        for k, (fx, fy) in enumerate(_CHIP_FLIPS):
            px, py = x ^ fx, y ^ fy
            cp = pltpu.make_async_remote_copy(
                src_ref=s_ref.at[2 * px + py], dst_ref=out_ref.at[k], send_sem=send_sems.at[k],
                recv_sem=recv_sems.at[k], device_id=(px, py, c), device_id_type=_MESH)
            cp.start()
            cps.append(cp)
        for cp in cps:
            cp.wait()

    return pl.pallas_call(
        body, name=name, in_specs=[_ANY], out_specs=_ANY,
        out_shape=jax.ShapeDtypeStruct((3,) + s1.shape[1:], s1.dtype),
        scratch_shapes=[pltpu.SemaphoreType.DMA((3,)), pltpu.SemaphoreType.DMA((3,))],
    )(s1)


def _row_tile(r):
    divs = [d for d in range(16, min(r, 512) + 1, 16) if r % d == 0]
    return divs[-1] if divs else r


def _pair_sum(name, gb, recv):
    _, r, cols = gb.shape
    tm = _row_tile(r)
    c = lax.axis_index("c").astype(jnp.int32).reshape(1)

    def body(c_ref, a_ref, b_ref, o_ref, ob_ref):
        s = a_ref[...] + b_ref[...]
        o_ref[...] = s
        ob_ref[...] = s.astype(BF16)

    blk = pl.BlockSpec((1, tm, cols), lambda j, i, c_ref: (j, i, 0))
    return pl.pallas_call(
        body, name=name,
        grid_spec=pltpu.PrefetchScalarGridSpec(
            num_scalar_prefetch=1, grid=(4, r // tm),
            in_specs=[pl.BlockSpec((1, tm, cols), lambda j, i, c_ref: (2 * j + c_ref[0], i, 0)), blk],
            out_specs=[blk, blk]),
        out_shape=[jax.ShapeDtypeStruct((4, r, cols), F32), jax.ShapeDtypeStruct((4, r, cols), BF16)],
        compiler_params=pltpu.CompilerParams(dimension_semantics=("parallel", "parallel")),
    )(c, gb, recv)


def _sum_parts(name, parts):
    _, r, cols = parts[0][0].shape
    tm = _row_tile(r)
    idx = jnp.stack([jnp.asarray(s, jnp.int32) for _, s in parts])
    n = len(parts)

    def body(idx_ref, *refs):
        g = refs[0][0].astype(F32)
        for p_ref in refs[1:n]:
            g = g + p_ref[0].astype(F32)
        refs[n][...] = g

    return pl.pallas_call(
        body, name=name,
        grid_spec=pltpu.PrefetchScalarGridSpec(
            num_scalar_prefetch=1, grid=(r // tm,),
            in_specs=[pl.BlockSpec((1, tm, cols), lambda i, idx_ref, p=p: (idx_ref[p], i, 0)) for p in range(n)],
            out_specs=pl.BlockSpec((tm, cols), lambda i, idx_ref: (i, 0))),
        out_shape=jax.ShapeDtypeStruct((r, cols), F32),
        compiler_params=pltpu.CompilerParams(dimension_semantics=("parallel",)),
    )(idx, *[a for a, _ in parts])


def _adam(name, parts, w, m, v):
    r, cols = w.shape
    tm = _row_tile(r)
    idx = jnp.stack([jnp.asarray(s, jnp.int32) for _, s in parts])
    n = len(parts)

    def body(idx_ref, *refs):
        g = refs[0][0].astype(F32)
        for p_ref in refs[1:n]:
            g = g + p_ref[0].astype(F32)
        w_ref, m_ref, v_ref, g_out, d_out, m_out, v_out = refs[n:]
        m_new = ADAM_B1 * m_ref[...] + (1.0 - ADAM_B1) * g
        v_new = ADAM_B2 * v_ref[...] + (1.0 - ADAM_B2) * (g * g)
        m_hat = m_new / (1.0 - ADAM_B1 ** ADAM_STEP)
        v_hat = v_new / (1.0 - ADAM_B2 ** ADAM_STEP)
        g_out[...] = g
        d_out[...] = -ADAM_LR * (m_hat / (jnp.sqrt(v_hat) + ADAM_EPS) + ADAM_WD * w_ref[...])
        m_out[...] = m_new
        v_out[...] = v_new

    part_specs = [pl.BlockSpec((1, tm, cols), lambda i, idx_ref, p=p: (idx_ref[p], i, 0)) for p in range(n)]
    flat = pl.BlockSpec((tm, cols), lambda i, idx_ref: (i, 0))
    return pl.pallas_call(
        body, name=name,
        grid_spec=pltpu.PrefetchScalarGridSpec(
            num_scalar_prefetch=1, grid=(r // tm,), in_specs=part_specs + [flat] * 3, out_specs=[flat] * 4),
        out_shape=[jax.ShapeDtypeStruct((r, cols), F32)] * 4,
        compiler_params=pltpu.CompilerParams(dimension_semantics=("parallel",)),
    )(idx, *[a for a, _ in parts], w, m, v)


def _all_gather_many(name, blks):
    n = len(blks)

    def body(*refs):
        x_refs, out_refs = refs[:n], refs[n:2 * n]
        send_sems, recv_sems, local_sems = refs[2 * n:]
        x, y, c = _me()
        me, sib = (x, y, c), (x, y, 1 - c)
        chips = [(x ^ fx, y ^ fy) for fx, fy in _CHIP_FLIPS]

        def slot(a, p):
            return out_refs[a].at[4 * p[0] + 2 * p[1] + p[2]]

        def copy(a, k, block, to, src=None):
            return pltpu.make_async_remote_copy(
                src_ref=slot(a, block) if src is None else src, dst_ref=slot(a, block),
                send_sem=send_sems.at[7 * a + k], recv_sem=recv_sems.at[7 * a + k], device_id=to,
                device_id_type=_MESH)

        mine = [pltpu.make_async_copy(x_refs[a], slot(a, me), local_sems.at[a]) for a in range(n)]
        first = []
        for a in range(n):
            mine[a].start()
            first.append(copy(a, 0, me, sib, src=x_refs[a]))
            first += [copy(a, 1 + j, me, (*chip, c), src=x_refs[a]) for j, chip in enumerate(chips)]
        for cp in first:
            cp.start()
        passed = []
        for j, chip in enumerate(chips):
            for a in range(n):
                copy(a, 1 + j, (*chip, c), me).wait_recv()
                cp = copy(a, 4 + j, (*chip, c), sib)
                cp.start()
                passed.append(cp)
        for a in range(n):
            copy(a, 0, sib, me).wait_recv()
            for j, chip in enumerate(chips):
                copy(a, 4 + j, (*chip, 1 - c), me).wait_recv()
        for cp in first + passed:
            cp.wait_send()
        for cp in mine:
            cp.wait()

    return pl.pallas_call(
        body, name=name, in_specs=[_ANY] * n, out_specs=[_ANY] * n,
        out_shape=[jax.ShapeDtypeStruct((N_DEV,) + b.shape, b.dtype) for b in blks],
        scratch_shapes=[pltpu.SemaphoreType.DMA((7 * n,)), pltpu.SemaphoreType.DMA((7 * n,)),
                        pltpu.SemaphoreType.DMA((n,))],
    )(*blks)


def _rs_sibling_many(name, gbs):
    n = len(gbs)

    def body(*refs):
        g_refs, out_refs = refs[:n], refs[n:2 * n]
        send_sems, recv_sems = refs[2 * n:]
        x, y, c = _me()
        cps = []
        for a in range(n):
            for j in range(4):
                cp = pltpu.make_async_remote_copy(
                    src_ref=g_refs[a].at[2 * j + (1 - c)], dst_ref=out_refs[a].at[j],
                    send_sem=send_sems.at[4 * a + j], recv_sem=recv_sems.at[4 * a + j],
                    device_id=(x, y, 1 - c), device_id_type=_MESH)
                cp.start()
                cps.append(cp)
        for cp in cps:
            cp.wait()

    return pl.pallas_call(
        body, name=name, in_specs=[_ANY] * n, out_specs=[_ANY] * n,
        out_shape=[jax.ShapeDtypeStruct((4,) + g.shape[1:], g.dtype) for g in gbs],
        scratch_shapes=[pltpu.SemaphoreType.DMA((4 * n,)), pltpu.SemaphoreType.DMA((4 * n,))],
    )(*gbs)


def _rs_chips_many(name, s1s):
    n = len(s1s)

    def body(*refs):
        s_refs, out_refs = refs[:n], refs[n:2 * n]
        send_sems, recv_sems = refs[2 * n:]
        x, y, c = _me()
        cps = []
        for a in range(n):
            for k, (fx, fy) in enumerate(_CHIP_FLIPS):
                px, py = x ^ fx, y ^ fy
                cp = pltpu.make_async_remote_copy(
                    src_ref=s_refs[a].at[2 * px + py], dst_ref=out_refs[a].at[k],
                    send_sem=send_sems.at[3 * a + k], recv_sem=recv_sems.at[3 * a + k],
                    device_id=(px, py, c), device_id_type=_MESH)
                cp.start()
                cps.append(cp)
        for cp in cps:
            cp.wait()

    return pl.pallas_call(
        body, name=name, in_specs=[_ANY] * n, out_specs=[_ANY] * n,
        out_shape=[jax.ShapeDtypeStruct((3,) + s.shape[1:], s.dtype) for s in s1s],
        scratch_shapes=[pltpu.SemaphoreType.DMA((3 * n,)), pltpu.SemaphoreType.DMA((3 * n,))],
    )(*s1s)


_HBM = pl.BlockSpec(memory_space=pltpu.HBM)
_SEM = pl.BlockSpec(memory_space=pltpu.SEMAPHORE)
_EFFECT = pltpu.SideEffectType.DATAFLOW_SIDE_EFFECTING


def _push_copies(src_refs, land_refs, send_sems, recv_sems, src_by_peer):
    x, y, c = _me()
    my_id = 4 * x + 2 * y + c
    out = []
    for a in range(len(src_refs)):
        for f in range(1, N_DEV):
            px, py, pc = x ^ (f >> 2), y ^ ((f >> 1) & 1), c ^ (f & 1)
            pid = 4 * px + 2 * py + pc
            src = src_refs[a].at[pid] if src_by_peer else src_refs[a]
            start = pltpu.make_async_remote_copy(
                src_ref=src, dst_ref=land_refs[a].at[my_id], send_sem=send_sems.at[7 * a + f - 1],
                recv_sem=recv_sems.at[7 * a + f - 1], device_id=(px, py, pc), device_id_type=_MESH)
            landed = pltpu.make_async_remote_copy(
                src_ref=src, dst_ref=land_refs[a].at[pid], send_sem=send_sems.at[7 * a + f - 1],
                recv_sem=recv_sems.at[7 * a + f - 1], device_id=(px, py, pc), device_id_type=_MESH)
            out.append((start, landed))
    return out


def _push_start(name, srcs, src_by_peer, after):
    n = len(srcs)
    lands = [jax.ShapeDtypeStruct((N_DEV,) + (s.shape[1:] if src_by_peer else s.shape), s.dtype) for s in srcs]

    def body(*refs):
        src_refs, land_refs = refs[:n], refs[n:2 * n]
        send_sems, recv_sems = refs[2 * n + 1], refs[2 * n + 2]
        token = refs[-1]
        for start, _ in _push_copies(src_refs, land_refs, send_sems, recv_sems, src_by_peer):
            start.start()
        token[...] = jnp.zeros_like(token)

    hbm = lambda a: pltpu.with_memory_space_constraint(a, pltpu.HBM)
    res = pl.pallas_call(
        body, name=name,
        out_shape=(pltpu.SemaphoreType.DMA((7 * n,)), pltpu.SemaphoreType.DMA((7 * n,)),
                   *[pltpu.HBM(s.shape, s.dtype) for s in srcs], *[pltpu.HBM(s.shape, s.dtype) for s in lands],
                   jax.ShapeDtypeStruct((8, LANE), F32)),
        in_specs=[_HBM] * (2 * n) + [_ANY],
        out_specs=(_SEM, _SEM, *[_HBM] * (2 * n), pl.BlockSpec(memory_space=pltpu.VMEM)),
        input_output_aliases={i: 2 + i for i in range(2 * n)},
        compiler_params=pltpu.CompilerParams(has_side_effects=_EFFECT),
    )(*[hbm(s) for s in srcs], *[hbm(lax.empty(s.shape, s.dtype)) for s in lands], after)
    return res[0], res[1], list(res[2:2 + n]), list(res[2 + n:2 + 2 * n]), res[-1]


def _push_wait(name, send_sems, recv_sems, srcs, lands, src_by_peer, after):
    n = len(srcs)

    def body(*refs):
        src_refs, land_refs = refs[:n], refs[n:2 * n]
        s_sems, r_sems = refs[2 * n], refs[2 * n + 1]
        for _, landed in _push_copies(src_refs, land_refs, s_sems, r_sems, src_by_peer):
            landed.wait_send()
            landed.wait_recv()

    res = pl.pallas_call(
        body, name=name,
        out_shape=tuple(pltpu.HBM(s.shape, s.dtype) for s in list(srcs) + list(lands)),
        in_specs=[_HBM] * (2 * n) + [_SEM, _SEM, _ANY],
        out_specs=tuple([_HBM] * (2 * n)),
        input_output_aliases={i: i for i in range(2 * n)},
        compiler_params=pltpu.CompilerParams(has_side_effects=_EFFECT),
    )(*srcs, *lands, send_sems, recv_sems, after)
    return list(res[:n]), list(res[n:])


_SHARDED = (
    ("meta_tokens", 1, (N_META, D_MODEL)),
    ("w_in", 1, (D_MODEL, IN_COLS)),
    ("w_q_b", 1, (Q_LORA, MLA_HEADS * QK_HEAD)),
    ("w_kv_b", 1, (KV_LORA, MLA_HEADS * (QK_NOPE + V_HEAD))),
    ("dn_conv_w", 1, (DN_CONV, 3 * DN_WIDTH)),
    ("w_out", 0, (2 * DN_WIDTH, D_MODEL)),
    ("w_gate", 1, (D_MODEL, D_FF)),
    ("w_up", 1, (D_MODEL, D_FF)),
    ("ffn_conv_w", 1, (FFN_CONV, D_FF)),
    ("w_down", 0, (D_FF, D_MODEL)),
)
_MXU_GATHERED = ("w_in", "w_q_b", "w_kv_b", "w_out", "w_gate", "w_up", "w_down")
_F32_GATHERED = ("meta_tokens", "dn_conv_w", "ffn_conv_w")
_EARLY = ("w_in", "w_q_b", "w_kv_b")
_LATE = ("w_out", "w_gate", "w_up", "w_down")
_TRANSPOSED = ("w_in", "w_gate", "w_up")
_REPLICATED = (
    ("attn_norm_w", D_MODEL), ("q_a_norm_w", Q_LORA), ("kv_a_norm_w", KV_LORA), ("q_norm_w", QK_HEAD),
    ("k_norm_w", QK_HEAD), ("mla_out_norm_w", V_HEAD), ("dn_A_log", DN_HEADS), ("dn_dt_bias", DN_HEADS),
    ("dn_out_norm_w", DN_DIM), ("ffn_norm_w", D_MODEL), ("ffn_conv_b", D_FF),
)
_PACK_COLS = 1024
_PACK_ROW_MULT = 320
_SMALL_SHAPE = (8, 768)
_SMALL_BLOCK = (8, 512)


def _local_shape(dim, shape):
    return (shape[0] // N_DEV, shape[1]) if dim == 0 else (shape[0], shape[1] // N_DEV)


def _pack_rows(n, mult):
    rows = -(-n // _PACK_COLS)
    return -(-rows // mult) * mult


def _pack(flats, mult, axis=0):
    cat = jnp.concatenate(flats, axis=-1)
    n = cat.shape[-1]
    r = _pack_rows(n, mult)
    pad = [(0, 0)] * (cat.ndim - 1) + [(0, r * _PACK_COLS - n)]
    return jnp.pad(cat, pad).reshape(cat.shape[:-1] + (r, _PACK_COLS))


def _to_blocks(full, dim):
    r, c = full.shape
    if dim == 0:
        return full.reshape(N_DEV, (r // N_DEV) * c)
    return full.reshape(r, N_DEV, c // N_DEV).transpose(1, 0, 2).reshape(N_DEV, r * (c // N_DEV))


def _from_blocks(blocks, dim, shape):
    r, c = shape
    if dim == 0:
        return blocks.reshape(r, c)
    return blocks.reshape(N_DEV, r, c // N_DEV).transpose(1, 0, 2).reshape(r, c)


def _split(flat, sizes):
    out, o = [], 0
    for s in sizes:
        out.append(flat[..., o:o + s])
        o += s
    return out


def _gather_weights(local, names, dtype, mult):
    specs = [s for s in _SHARDED if s[0] in names]
    pack = _pack([local[n].astype(dtype).reshape(-1) for n, _, _ in specs], mult)
    got = _all_gather("gather_" + "_".join(n[:5] for n in names[:2]), pack)
    flat = got.reshape(N_DEV, -1)
    sizes = [math.prod(_local_shape(d, s)) for _, d, s in specs]
    return {n: _from_blocks(p, d, s) for (n, d, s), p in zip(specs, _split(flat, sizes))}


def kernel(x, meta_tokens, attn_norm_w, w_in, q_a_norm_w, w_q_b, kv_a_norm_w, w_kv_b, q_norm_w, k_norm_w, mla_out_norm_w, dn_conv_w, dn_A_log, dn_dt_bias, dn_out_norm_w, w_out, ffn_norm_w, w_gate, w_up, ffn_conv_w, ffn_conv_b, w_down, loss_target, m_meta_tokens, m_attn_norm_w, m_w_in, m_q_a_norm_w, m_w_q_b, m_kv_a_norm_w, m_w_kv_b, m_q_norm_w, m_k_norm_w, m_mla_out_norm_w, m_dn_conv_w, m_dn_A_log, m_dn_dt_bias, m_dn_out_norm_w, m_w_out, m_ffn_norm_w, m_w_gate, m_w_up, m_ffn_conv_w, m_ffn_conv_b, m_w_down, v_meta_tokens, v_attn_norm_w, v_w_in, v_q_a_norm_w, v_w_q_b, v_kv_a_norm_w, v_w_kv_b, v_q_norm_w, v_k_norm_w, v_mla_out_norm_w, v_dn_conv_w, v_dn_A_log, v_dn_dt_bias, v_dn_out_norm_w, v_w_out, v_ffn_norm_w, v_w_gate, v_w_up, v_ffn_conv_w, v_ffn_conv_b, v_w_down):
    names = [n for n, _, _ in _SHARDED] + [n for n, _ in _REPLICATED]
    given = dict(locals())
    two_d = lambda a: a.reshape(a.shape[-2:])
    wl = {n: two_d(given[n]) for n in names}
    ml = {n: two_d(given["m_" + n]) for n in names}
    vl = {n: two_d(given["v_" + n]) for n in names}
    out_shapes = {n: given[n].shape for n in names}

    spec = {n: (d, s) for n, d, s in _SHARDED}
    small_sizes = [math.prod(_local_shape(*spec[n])) for n in _F32_GATHERED]

    def small_block(d):
        cat = jnp.concatenate([d[n].reshape(d[n].shape[:-2] + (-1,)) for n in _F32_GATHERED], axis=-1)
        pad = [(0, 0)] * (cat.ndim - 1) + [(0, math.prod(_SMALL_BLOCK) - cat.shape[-1])]
        return jnp.pad(cat, pad).reshape(cat.shape[:-1] + _SMALL_BLOCK)

    def shard(n):
        a = wl[n].astype(_MXU)
        return a.T if n in _TRANSPOSED else a

    def from_slots(n, blocks):
        d, s = spec[n]
        if d == 0 or n in _TRANSPOSED:
            return blocks.reshape(-1, blocks.shape[-1])
        return blocks.transpose(1, 0, 2).reshape(s)

    my_id = 4 * lax.axis_index("x") + 2 * lax.axis_index("y") + lax.axis_index("c")
    got = _all_gather_many("gather_early", [shard(n) for n in _EARLY] + [small_block(wl)])
    full = {n: a for n, a in wl.items() if n not in _LATE}
    for n, blocks in zip(_EARLY, got):
        full[n] = from_slots(n, blocks)
    for n, p in zip(_F32_GATHERED, _split(got[-1].reshape(N_DEV, -1), small_sizes)):
        full[n] = _from_blocks(p, *spec[n])
    late_own = [shard(n) for n in _LATE]
    l_send, l_recv, l_src, l_land, token = _push_start("gather_late_start", late_own, False, got[-1])

    def late_weights(after):
        _, lands = _push_wait("gather_late_wait", l_send, l_recv, l_src, l_land, False, after)
        out = {}
        for n, land, own in zip(_LATE, lands, late_own):
            out[n] = from_slots(n, lax.dynamic_update_slice(land, own[None], (my_id, 0, 0))).astype(_MXU)
        return out

    def dest_blocks(n, a):
        d, s = spec[n]
        r, c = _local_shape(d, s)
        if n in _TRANSPOSED:
            return a.reshape(N_DEV, c, r)
        return a.reshape(N_DEV, r, c) if d == 0 else a.reshape(r, N_DEV, c).transpose(1, 0, 2)

    pushed = []

    def grads_ready(g, names):
        nat = _grads_to_natural({n: g[n] for n in names})
        blocks = [dest_blocks(n, nat[n]).astype(_MXU) for n in names]
        sends, recvs, srcs, lands, tok = _push_start("rs_" + names[0] + "_start", blocks, True, token)
        pushed.append((names, sends, recvs, srcs, lands))
        return tok

    seq = x.shape[1]
    tp = ROW0 + seq
    h0 = jnp.concatenate([jnp.zeros((PAD, D_MODEL), F32), full["meta_tokens"], x[0]], axis=0)
    tgt = jnp.concatenate([jnp.zeros((ROW0, D_MODEL), F32), loss_target[0]], axis=0)
    loss, dh0, g = _local_step(h0, tgt, _prepare(full, tp), token, late_weights, grads_ready)
    g = _grads_to_natural(g)
    g["meta_tokens"] = dh0[PAD:ROW0]
    grad_x = dh0[ROW0:][None]

    big = [{}, {}, {}, {}]
    rep_names = [n for n, _ in _REPLICATED]
    pieces = [g[n].reshape(-1) for n in rep_names] + [loss[0, :1]] + [g[n].reshape(-1) for n in _F32_GATHERED]
    sizes = [p.shape[0] for p in pieces]
    cat = jnp.concatenate(pieces)
    cols = -(-cat.shape[0] // (8 * LANE)) * LANE
    mine = jnp.pad(cat, (0, 8 * cols - cat.shape[0])).reshape(8, cols)
    everyone = _all_gather("gather_small_grads", mine)
    total = _sum_parts("sum_small_grads", [(everyone, d) for d in range(N_DEV)])
    tot = dict(zip(rep_names + ["loss"] + list(_F32_GATHERED), _split(total.reshape(-1), sizes)))

    def small(d):
        cat = jnp.concatenate([d[n].reshape(-1) for n in rep_names])
        return jnp.pad(cat, (0, math.prod(_SMALL_SHAPE) - cat.shape[0])).reshape(_SMALL_SHAPE)

    sm = _adam("adam_replicated", [(small(tot)[None], 0)], small(wl), small(ml), small(vl))
    sm = [dict(zip(rep_names, _split(a.reshape(-1), [n for _, n in _REPLICATED]))) for a in sm]
    mine_of = {}
    for n in _F32_GATHERED:
        d, s = spec[n]
        r, c = _local_shape(d, s)
        mine_of[n] = lax.dynamic_slice(tot[n].reshape(s), (0, my_id * c), (r, c))
    res = _adam("adam_small_sharded", [(small_block(mine_of)[None], 0)], small_block(wl), small_block(ml),
                small_block(vl))
    for kind, a in enumerate(res):
        big[kind].update(zip(_F32_GATHERED, _split(a.reshape(-1), small_sizes)))

    for names, sends, recvs, srcs, lands in pushed:
        srcs, lands = _push_wait("rs_" + names[0] + "_wait", sends, recvs, srcs, lands, True, dh0)
        for n, src, land in zip(names, srcs, lands):
            parts = [(src, my_id)] + [(land, my_id ^ f) for f in range(1, N_DEV)]
            if n in _TRANSPOSED:
                parts = [(_sum_parts("sum_" + n, parts).T[None], 0)]
            for kind, a in enumerate(_adam("adam_" + n, parts, wl[n], ml[n], vl[n])):
                big[kind][n] = a

    outs = [tot["loss"].reshape(()), grad_x]
    for kind in range(4):
        for n in ("meta_tokens", "attn_norm_w", "w_in", "q_a_norm_w", "w_q_b", "kv_a_norm_w", "w_kv_b", "q_norm_w",
                  "k_norm_w", "mla_out_norm_w", "dn_conv_w", "dn_A_log", "dn_dt_bias", "dn_out_norm_w", "w_out",
                  "ffn_norm_w", "w_gate", "w_up", "ffn_conv_w", "ffn_conv_b", "w_down"):
            src = big[kind] if n in big[kind] else sm[kind]
            outs.append(src[n].reshape(out_shapes[n]))
    return tuple(outs)
```

```python
import functools
import math

import jax
import jax.numpy as jnp
from jax import lax
from jax.experimental import pallas as pl
from jax.experimental.pallas import tpu as pltpu

F32 = jnp.float32
BF16 = jnp.bfloat16
_MXU = jnp.bfloat16
_HI = lax.Precision.HIGHEST

D_MODEL = 1024
N_META = 16
PAD = 112
ROW0 = PAD + N_META
MLA_HEADS = 4
QK_NOPE = 128
QK_ROPE = 64
QK_HEAD = QK_NOPE + QK_ROPE
V_HEAD = 128
Q_LORA = 256
KV_LORA = 256
ROPE_THETA = 10000.0
DN_HEADS = 4
DN_DIM = 128
DN_WIDTH = DN_HEADS * DN_DIM
DN_CONV = 4
DN_CHUNK = 64
D_FF = 2816
FFN_CONV = 3
EPS = 1e-6
HP = 256
C_QKV = 0
C_Z = 1536
C_QL = 2048
C_KVL = 2304
C_KPE = 2560
C_AB = 2688
IN_P = 2816
IN_COLS = 2632

ADAM_LR = 0.001
ADAM_B1 = 0.9
ADAM_B2 = 0.999
ADAM_EPS = 1e-08
ADAM_WD = 0.01
ADAM_STEP = 10

N_DEV = 8
TM = 128
LANE = 128
VMEM_LIMIT = 56 * 1024 * 1024
NEG = -1e30


def _dot(a, b, dims, hp=False):
    if hp:
        return lax.dot_general(a.astype(F32), b.astype(F32), (dims, ((), ())),
                               precision=lax.Precision.HIGH if hp == "3x" else _HI, preferred_element_type=F32)
    return lax.dot_general(a.astype(_MXU), b.astype(_MXU), (dims, ((), ())),
                           preferred_element_type=F32)


def _nn(a, b, hp=False):
    return _dot(a, b, ((1,), (0,)), hp)


def _nt(a, b, hp=False):
    return _dot(a, b, ((1,), (1,)), hp)


def _tn(a, b, hp=False):
    return _dot(a, b, ((0,), (0,)), hp)


def _sigmoid(x):
    return 1.0 / (1.0 + jnp.exp(-x))


def _rms_fwd(x, w, n):
    r = lax.rsqrt(jnp.sum(x * x, axis=-1, keepdims=True) * (1.0 / n) + EPS)
    return x * r * w, r


def _rms_bwd(x, w, dy, n):
    r = lax.rsqrt(jnp.sum(x * x, axis=-1, keepdims=True) * (1.0 / n) + EPS)
    xh = x * r
    gy = dy * w
    dx = r * (gy - xh * (jnp.sum(gy * xh, axis=-1, keepdims=True) * (1.0 / n)))
    return dx, dy * xh


def _rowsum(x):
    return jnp.sum(x, axis=0, keepdims=True)


def _row_ids(i, tm):
    return i * tm + lax.broadcasted_iota(jnp.int32, (tm, 1), 0)


def _shift_down(ext, s, tm):
    if s == 0:
        return ext[8:8 + tm]
    return pltpu.roll(ext, s, 0)[8:8 + tm]


def _shift_up(ext, s, tm):
    if s == 0:
        return ext[0:tm]
    return pltpu.roll(ext, tm + 8 - s, 0)[0:tm]


def _conv_fwd(x, halo_prev, w, width):
    tm = x.shape[0]
    ext = jnp.concatenate([halo_prev, x], axis=0)
    y = None
    for j in range(width):
        t = w[j:j + 1, :] * _shift_down(ext, width - 1 - j, tm)
        y = t if y is None else y + t
    return y


def _conv_bwd_x(dy, halo_next, w, width):
    tm = dy.shape[0]
    ext = jnp.concatenate([dy, halo_next], axis=0)
    dx = None
    for j in range(width):
        t = w[j:j + 1, :] * _shift_up(ext, width - 1 - j, tm)
        dx = t if dx is None else dx + t
    return dx


def _conv_bwd_w(dy, x, halo_prev, width):
    tm = dy.shape[0]
    ext = jnp.concatenate([halo_prev, x], axis=0)
    rows = [_rowsum(dy * _shift_down(ext, width - 1 - j, tm)) for j in range(width)]
    rows += [jnp.zeros_like(rows[0])] * (8 - width)
    return jnp.concatenate(rows, axis=0)


def _softplus(x):
    e = jnp.exp(-jnp.abs(x))
    u = 1.0 + e
    l1p = jnp.where(u == 1.0, e, jnp.log(u) * e / jnp.where(u == 1.0, 1.0, u - 1.0))
    return jnp.maximum(x, 0.0) + l1p


def _swap_halves(x):
    lane = lax.broadcasted_iota(jnp.int32, x.shape, 1)
    return jnp.where(lane < 32, pltpu.roll(x, 96, 1), jnp.where(lane < 64, pltpu.roll(x, 32, 1), 0.0))


class _In:
    def __init__(self, arr, width=None, cb=0, kind="cur"):
        self.arr, self.kind = arr, kind
        self.width = arr.shape[1] if width is None else width
        self.cb = cb


def _rows(name, fn, tiled, full, outs, accs=(), tm=TM):
    tp = tiled[0].arr.shape[0]
    nt = tp // tm
    r8 = tm // 8
    n_in = len(tiled) + len(full)
    n_out = len(outs)

    def body(*refs):
        i = pl.program_id(0)
        vals = [r[...] for r in refs[:n_in]]
        o_t, o_a = fn(i, *vals)
        for r, v in zip(refs[n_in:n_in + n_out], o_t):
            r[...] = v.astype(r.dtype)
        for r, v in zip(refs[n_in + n_out:], o_a):
            @pl.when(i == 0)
            def _():
                r[...] = v

            @pl.when(i > 0)
            def _():
                r[...] += v

    def spec(t):
        if t.kind == "cur":
            return pl.BlockSpec((tm, t.width), lambda i, cb=t.cb: (i, cb))
        if t.kind == "prev":
            return pl.BlockSpec((8, t.width), lambda i, cb=t.cb: (jnp.maximum(i * r8 - 1, 0), cb))
        return pl.BlockSpec((8, t.width), lambda i, cb=t.cb: (jnp.minimum((i + 1) * r8, tp // 8 - 1), cb))

    in_specs = [spec(t) for t in tiled]
    in_specs += [pl.BlockSpec(a.shape, lambda i, nd=a.ndim: (0,) * nd) for a in full]
    out_specs = [pl.BlockSpec((tm, w), lambda i: (i, 0)) for w, _ in outs]
    out_specs += [pl.BlockSpec((r, w), lambda i: (0, 0)) for r, w in accs]
    out_shape = [jax.ShapeDtypeStruct((tp, w), dt) for w, dt in outs]
    out_shape += [jax.ShapeDtypeStruct((r, w), F32) for r, w in accs]
    res = pl.pallas_call(
        body, name=name, grid=(nt,), in_specs=in_specs, out_specs=out_specs, out_shape=out_shape,
        compiler_params=pltpu.CompilerParams(dimension_semantics=("arbitrary",), vmem_limit_bytes=VMEM_LIMIT),
    )(*[t.arr for t in tiled], *full)
    return res


def _pick(n, cap, mult):
    best = None
    for d in range(mult, min(n, cap) + 1, mult):
        if n % d == 0:
            best = d
    assert best is not None, (n, cap, mult)
    return best


_ANY_SPEC = pl.BlockSpec(memory_space=pl.ANY)


def _mm(name, a, b, mode, out_dtype=F32, resid=None, after=None):
    if mode == "tn":
        m, k = a.shape
        n = b.shape[1]
        tk = _pick(k, 512, 128)
        tn = _pick(n, 1408, 128)

        def body_tn(a_ref, b_ref, o_ref):
            o_ref[...] = _tn(a_ref[...], b_ref[...]).astype(o_ref.dtype)

        return pl.pallas_call(
            body_tn, name=name, grid=(n // tn, k // tk),
            in_specs=[pl.BlockSpec((m, tk), lambda j, p: (0, p)),
                      pl.BlockSpec((m, tn), lambda j, p: (0, j))],
            out_specs=pl.BlockSpec((tk, tn), lambda j, p: (p, j)),
            out_shape=jax.ShapeDtypeStruct((k, n), out_dtype),
            compiler_params=pltpu.CompilerParams(
                dimension_semantics=("parallel", "parallel"), vmem_limit_bytes=VMEM_LIMIT),
        )(a, b)

    m, k = a.shape
    n = b.shape[1] if mode == "nn" else b.shape[0]
    tn = _pick(n, 1408, 128)
    tm = _pick(m, 1152, 16)
    dotf = _nn if mode == "nn" else _nt

    def body(*refs):
        a_ref, b_ref, o_ref = refs[0], refs[1], refs[-1]
        acc = dotf(a_ref[...], b_ref[...])
        if resid is not None:
            acc = refs[2][...] + acc
        o_ref[...] = acc.astype(o_ref.dtype)

    b_spec = (pl.BlockSpec((k, tn), lambda j, i: (0, j)) if mode == "nn"
              else pl.BlockSpec((tn, k), lambda j, i: (j, 0)))
    in_specs = [pl.BlockSpec((tm, k), lambda j, i: (i, 0)), b_spec]
    args = [a, b]
    if resid is not None:
        in_specs.append(pl.BlockSpec((tm, tn), lambda j, i: (i, j)))
        args.append(resid)
    if after is not None:
        in_specs.append(_ANY_SPEC)
        args.append(after)
    return pl.pallas_call(
        body, name=name, grid=(n // tn, m // tm), in_specs=in_specs,
        out_specs=pl.BlockSpec((tm, tn), lambda j, i: (i, j)),
        out_shape=jax.ShapeDtypeStruct((m, n), out_dtype),
        compiler_params=pltpu.CompilerParams(
            dimension_semantics=("parallel", "parallel"), vmem_limit_bytes=VMEM_LIMIT),
    )(*args)


def _norm_mm(name, x, norm_w, bt, after=None):
    m, k = x.shape
    n = bt.shape[0]
    tn = _pick(n, 1408, 128)
    tm = _pick(m, 1152, 16)
    extra = [] if after is None else [after]

    def body(x_ref, w_ref, b_ref, *rest):
        o_ref, u_ref = rest[-2:]

        @pl.when(pl.program_id(1) == 0)
        def _():
            u_ref[...] = _rms_fwd(x_ref[...], w_ref[...], k)[0].astype(u_ref.dtype)

        o_ref[...] = _nt(u_ref[...], b_ref[...])

    return pl.pallas_call(
        body, name=name, grid=(m // tm, n // tn),
        in_specs=[pl.BlockSpec((tm, k), lambda i, j: (i, 0)), pl.BlockSpec((1, k), lambda i, j: (0, 0)),
                  pl.BlockSpec((tn, k), lambda i, j: (j, 0))] + [_ANY_SPEC] * len(extra),
        out_specs=[pl.BlockSpec((tm, tn), lambda i, j: (i, j)), pl.BlockSpec((tm, k), lambda i, j: (i, 0))],
        out_shape=[jax.ShapeDtypeStruct((m, n), F32), jax.ShapeDtypeStruct((m, k), _MXU)],
        compiler_params=pltpu.CompilerParams(
            dimension_semantics=("arbitrary", "arbitrary"), vmem_limit_bytes=VMEM_LIMIT),
    )(x, norm_w, bt, *extra)


def _mm_rows(name, a, b, mode, fn, tiled, full, outs, accs=()):
    m = a.shape[0]
    tm = _pick(m, 576, 16)
    dotf = _nn if mode == "nn" else _nt
    n_in = len(tiled) + len(full)
    n_out = len(outs)

    def body(*refs):
        i = pl.program_id(0)
        vals = [r[...] for r in refs[2:2 + n_in]]
        o_t, o_a = fn(i, dotf(refs[0][...], refs[1][...]), *vals)
        for r, v in zip(refs[2 + n_in:2 + n_in + n_out], o_t):
            r[...] = v.astype(r.dtype)
        for r, v in zip(refs[2 + n_in + n_out:], o_a):
            @pl.when(i == 0)
            def _():
                r[...] = v

            @pl.when(i > 0)
            def _():
                r[...] += v

    whole = lambda x: pl.BlockSpec(x.shape, lambda i, nd=x.ndim: (0,) * nd)
    in_specs = [pl.BlockSpec((tm, a.shape[1]), lambda i: (i, 0)), whole(b)]
    in_specs += [pl.BlockSpec((tm, t.width), lambda i, cb=t.cb: (i, cb)) for t in tiled]
    in_specs += [whole(x) for x in full]
    out_specs = [pl.BlockSpec((tm, w), lambda i: (i, 0)) for w, _ in outs]
    out_specs += [pl.BlockSpec((r, w), lambda i: (0, 0)) for r, w in accs]
    out_shape = [jax.ShapeDtypeStruct((m, w), dt) for w, dt in outs]
    out_shape += [jax.ShapeDtypeStruct((r, w), F32) for r, w in accs]
    return pl.pallas_call(
        body, name=name, grid=(m // tm,), in_specs=in_specs, out_specs=out_specs, out_shape=out_shape,
        compiler_params=pltpu.CompilerParams(dimension_semantics=("arbitrary",), vmem_limit_bytes=VMEM_LIMIT),
    )(a, b, *[t.arr for t in tiled], *full)


ATTN_Q_TILES = 4


def _attn_probs(q, k, row0):
    tq, tp = q.shape[0], k.shape[0]
    s = _nt(q, k) * (1.0 / math.sqrt(QK_HEAD))
    row = row0 + lax.broadcasted_iota(jnp.int32, (tq, tp), 0)
    col = lax.broadcasted_iota(jnp.int32, (tq, tp), 1)
    ok = (col <= row) & (col >= PAD)
    s = jnp.where(ok, s, NEG)
    m = jnp.max(s, axis=-1, keepdims=True)
    e = jnp.exp(s - m)
    e = jnp.where(ok, e, 0.0)
    l = jnp.sum(e, axis=-1, keepdims=True)
    return e / jnp.maximum(l, 1e-30)


def _attn_fwd(q, k, v):
    tp = q.shape[0]
    tq = tp // ATTN_Q_TILES

    def body(q_ref, k_ref, v_ref, o_ref):
        for i in range(ATTN_Q_TILES):
            rows = slice(i * tq, (i + 1) * tq)
            keys = slice(0, (i + 1) * tq)
            p = _attn_probs(q_ref[rows, :], k_ref[keys, :], i * tq)
            o_ref[rows, :] = _nn(p, v_ref[keys, :])

    return pl.pallas_call(
        body, name="attn_fwd", grid=(MLA_HEADS,),
        in_specs=[pl.BlockSpec((tp, HP), lambda h: (0, h)),
                  pl.BlockSpec((tp, HP), lambda h: (0, h)),
                  pl.BlockSpec((tp, V_HEAD), lambda h: (0, h))],
        out_specs=pl.BlockSpec((tp, V_HEAD), lambda h: (0, h)),
        out_shape=jax.ShapeDtypeStruct((tp, MLA_HEADS * V_HEAD), F32),
        compiler_params=pltpu.CompilerParams(dimension_semantics=("parallel",), vmem_limit_bytes=VMEM_LIMIT),
    )(q, k, v)


def _attn_bwd(q, k, v, do):
    tp = q.shape[0]
    tq = tp // ATTN_Q_TILES

    def body(q_ref, k_ref, v_ref, do_ref, dq_ref, dk_ref, dv_ref):
        for i in reversed(range(ATTN_Q_TILES)):
            rows = slice(i * tq, (i + 1) * tq)
            keys = slice(0, (i + 1) * tq)
            qb = q_ref[rows, :]
            kk = k_ref[keys, :]
            dob = do_ref[rows, :]
            p = _attn_probs(qb, kk, i * tq)
            dp = _nt(dob, v_ref[keys, :])
            delta = jnp.sum(p * dp, axis=-1, keepdims=True)
            ds = p * (dp - delta) * (1.0 / math.sqrt(QK_HEAD))
            dq_ref[rows, :] = _nn(ds, kk)
            if i == ATTN_Q_TILES - 1:
                dk_ref[...] = _tn(ds, qb)
                dv_ref[...] = _tn(p, dob)
            else:
                dk_ref[keys, :] += _tn(ds, qb)
                dv_ref[keys, :] += _tn(p, dob)

    full = lambda w: pl.BlockSpec((tp, w), lambda h: (0, h))
    return pl.pallas_call(
        body, name="attn_bwd", grid=(MLA_HEADS,),
        in_specs=[full(HP), full(HP), full(V_HEAD), full(V_HEAD)],
        out_specs=[full(HP), full(HP), full(V_HEAD)],
        out_shape=[jax.ShapeDtypeStruct((tp, MLA_HEADS * HP), F32),
                   jax.ShapeDtypeStruct((tp, MLA_HEADS * HP), F32),
                   jax.ShapeDtypeStruct((tp, MLA_HEADS * V_HEAD), F32)],
        compiler_params=pltpu.CompilerParams(dimension_semantics=("parallel",), vmem_limit_bytes=VMEM_LIMIT),
    )(q, k, v, do)


def _gdn_consts():
    c = DN_CHUNK
    r = lax.broadcasted_iota(jnp.int32, (c, c), 0)
    cc = lax.broadcasted_iota(jnp.int32, (c, c), 1)
    incl = r >= cc
    strict = r > cc
    return incl, strict


def _each(fn, *lists):
    return [fn(*a) for a in zip(*lists)]


def _interleave(chains):
    chains = list(chains)
    while chains:
        for ch in list(chains):
            try:
                next(ch)
            except StopIteration:
                chains.remove(ch)


def _gdn_chunk_common(q_ref, k_ref, v_ref, g_ref, b_ref):
    c = DN_CHUNK
    incl, strict = _gdn_consts()
    sls = [slice(DN_DIM * h, DN_DIM * (h + 1)) for h in range(DN_HEADS)]
    inclf = incl.astype(F32)
    ones = jnp.full((c, LANE), 1.0 / LANE, F32)
    q = [q_ref[:, sl] * (1.0 / math.sqrt(DN_DIM)) for sl in sls]
    k = [k_ref[:, sl] for sl in sls]
    v = [v_ref[:, sl] for sl in sls]
    g = [g_ref[:, sl] for sl in sls]
    beta = [b_ref[:, sl] for sl in sls]
    gc = [_nn(inclf, x, hp=True) for x in g]
    grow = [_nt(ones, x, hp=True) for x in gc]
    kb = _each(jnp.multiply, k, beta)
    kk = _each(_nt, kb, k)
    qk = _each(_nt, q, k)
    gam = [jnp.exp(x) for x in gc]
    g_last = [_rowsum(x) for x in g]
    dm = [jnp.exp(jnp.where(incl, x[:, :c] - y, NEG)) for x, y in zip(gc, grow)]
    vb = _each(jnp.multiply, v, beta)
    kbg = _each(jnp.multiply, kb, gam)
    ek = [jnp.exp(x - y) for x, y in zip(g_last, gc)]
    kd = _each(jnp.multiply, k, ek)
    return dict(q=q, k=k, v=v, beta=beta, gc=gc, gam=gam, g_last=g_last, dm=dm, kb=kb, vb=vb,
                kbg=kbg, kk=kk, ek=ek, kd=kd, qk=qk, incl=incl, strict=strict, sls=sls)


def _gdn_fwd(q, k, v, g, beta):
    tp = q.shape[0]
    c = DN_CHUNK
    nch = tp // c

    def body(q_ref, k_ref, v_ref, g_ref, b_ref, o_ref, s_ref, t_ref, s_scr):
        @pl.when(pl.program_id(0) == 0)
        def _():
            s_scr[...] = jnp.zeros_like(s_scr)

        eye = (lax.broadcasted_iota(jnp.int32, (c, c), 0) == lax.broadcasted_iota(jnp.int32, (c, c), 1)).astype(F32)
        x = _gdn_chunk_common(q_ref, k_ref, v_ref, g_ref, b_ref)
        heads = range(DN_HEADS)
        s = [s_scr[h] for h in heads]
        bp = [-jnp.where(x["strict"], kk * dm, 0.0) for kk, dm in zip(x["kk"], x["dm"])]
        t = [eye + b for b in bp]
        for _ in range(5):
            bp = [_nn(b, b, hp="3x") for b in bp]
            t = [tt + _nn(tt, b, hp="3x") for tt, b in zip(t, bp)]
        u = _each(_nn, t, x["vb"])
        w = _each(_nn, t, x["kbg"])
        v_new = [uu - _nn(ww, ss) for uu, ww, ss in zip(u, w, s)]
        o = [_nn(q * gam, ss) + _nn(qk * dm, vn)
             for q, gam, ss, qk, dm, vn in zip(x["q"], x["gam"], s, x["qk"], x["dm"], v_new)]
        s_new = [ss * jnp.exp(gl) + _tn(kd, vn) for ss, gl, kd, vn in zip(s, x["g_last"], x["kd"], v_new)]
        for h in heads:
            s_ref[h, 0] = s[h]
            t_ref[h, 0] = t[h]
            o_ref[:, x["sls"][h]] = o[h]
            s_scr[h] = s_new[h]

    rb = lambda n: (n, 0)
    return pl.pallas_call(
        body, name="gdn_fwd", grid=(nch,),
        in_specs=[pl.BlockSpec((c, DN_WIDTH), rb)] * 5,
        out_specs=[pl.BlockSpec((c, DN_WIDTH), rb),
                   pl.BlockSpec((DN_HEADS, 1, DN_DIM, DN_DIM), lambda n: (0, n, 0, 0)),
                   pl.BlockSpec((DN_HEADS, 1, c, c), lambda n: (0, n, 0, 0))],
        out_shape=[jax.ShapeDtypeStruct((tp, DN_WIDTH), F32),
                   jax.ShapeDtypeStruct((DN_HEADS, nch, DN_DIM, DN_DIM), F32),
                   jax.ShapeDtypeStruct((DN_HEADS, nch, c, c), F32)],
        scratch_shapes=[pltpu.VMEM((DN_HEADS, DN_DIM, DN_DIM), F32)],
        compiler_params=pltpu.CompilerParams(dimension_semantics=("arbitrary",), vmem_limit_bytes=VMEM_LIMIT),
    )(q, k, v, g, beta)


def _gdn_bwd(q, k, v, g, beta, s_all, t_all, do):
    tp = q.shape[0]
    c = DN_CHUNK
    nch = tp // c

    def body(q_ref, k_ref, v_ref, g_ref, b_ref, s_ref, t_ref, do_ref,
             dq_ref, dk_ref, dv_ref, dg_ref, db_ref, ds_scr):
        @pl.when(pl.program_id(0) == 0)
        def _():
            ds_scr[...] = jnp.zeros_like(ds_scr)

        ones_cl = jnp.ones((c, LANE), F32)
        xs = _gdn_chunk_common(q_ref, k_ref, v_ref, g_ref, b_ref)
        upper = jnp.logical_not(xs["strict"]).astype(F32)

        def chain(h):
            x = {key: (val[h] if isinstance(val, list) else val) for key, val in xs.items()}
            sl = x["sls"]
            qs, kx, vx, beta_, gam, dm = x["q"], x["k"], x["v"], x["beta"], x["gam"], x["dm"]
            kb, vb, kbg, kd, ek = x["kb"], x["vb"], x["kbg"], x["kd"], x["ek"]
            t = t_ref[h, 0]
            s = s_ref[h, 0]
            dsn = ds_scr[h]
            dob = do_ref[:, sl]
            eg_last = jnp.exp(x["g_last"])
            u = _nn(t, vb)
            w = _nn(t, kbg)
            mqk = x["qk"] * dm
            qd = qs * gam
            dqd = _nt(dob, s)
            dkd_pre = _nn(kd, dsn)
            yield
            v_new = u - _nn(w, s)
            dv_new = _tn(mqk, dob) + dkd_pre
            dq = dqd * gam
            dgam = jnp.sum(dqd * qs, axis=1, keepdims=True)
            yield
            ds_new = _tn(qd, dob) + eg_last * dsn - _tn(w, dv_new)
            dmm = jnp.where(x["incl"], _nt(dob, v_new), 0.0)
            dkd = _nt(v_new, dsn)
            dw = -_nt(dv_new, s)
            dvb = _tn(t, dv_new)
            dt = _nt(dv_new, vb)
            yield
            dqk = dmm * dm
            e_mat = dmm * mqk
            dq = dq + _nn(dqk, kx)
            dk = _tn(dqk, qs) + dkd * ek
            e1 = jnp.sum(dkd * kd, axis=1, keepdims=True)
            dgc = -e1
            dg_last = jnp.sum(e1) + eg_last * jnp.sum(s * dsn)
            dt = dt + _nt(dw, kbg)
            dkbg = _tn(t, dw)
            yield
            tdt = _tn(t, dt, hp="3x")
            yield
            da = jnp.where(x["strict"], -_nt(tdt, t, hp="3x"), 0.0)
            yield
            dkk = da * dm
            e_mat = e_mat + da * x["kk"] * dm
            dkb = _nn(dkk, kx) + dkbg * gam
            dk = dk + _tn(dkk, kb)
            dgam = dgam + jnp.sum(dkbg * kb, axis=1, keepdims=True)
            yield
            dk = dk + dkb * beta_
            dbeta = jnp.sum(dkb * kx, axis=1, keepdims=True) + jnp.sum(dvb * vx, axis=1, keepdims=True)
            dv = dvb * beta_
            dgc = dgc + jnp.sum(e_mat, axis=1, keepdims=True) + dgam * gam
            dgc = dgc - _tn(e_mat, ones_cl, hp="3x")
            yield
            dg = _nn(upper, dgc, hp="3x") + dg_last
            yield
            ds_scr[h] = ds_new
            dq_ref[:, sl] = dq * (1.0 / math.sqrt(DN_DIM))
            dk_ref[:, sl] = dk
            dv_ref[:, sl] = dv
            dg_ref[:, sl] = dg
            db_ref[:, sl] = jnp.broadcast_to(dbeta, (c, LANE))

        _interleave([chain(h) for h in range(DN_HEADS)])

    rb = lambda n: (nch - 1 - n, 0)
    hs = lambda n: (0, nch - 1 - n, 0, 0)
    return pl.pallas_call(
        body, name="gdn_bwd", grid=(nch,),
        in_specs=[pl.BlockSpec((c, DN_WIDTH), rb)] * 5
        + [pl.BlockSpec((DN_HEADS, 1, DN_DIM, DN_DIM), hs), pl.BlockSpec((DN_HEADS, 1, c, c), hs),
           pl.BlockSpec((c, DN_WIDTH), rb)],
        out_specs=[pl.BlockSpec((c, DN_WIDTH), rb)] * 5,
        out_shape=[jax.ShapeDtypeStruct((tp, DN_WIDTH), F32)] * 5,
        scratch_shapes=[pltpu.VMEM((DN_HEADS, DN_DIM, DN_DIM), F32)],
        compiler_params=pltpu.CompilerParams(dimension_semantics=("arbitrary",), vmem_limit_bytes=VMEM_LIMIT),
    )(q, k, v, g, beta, s_all, t_all, do)


def _silu_parts(x):
    s = _sigmoid(x)
    return x * s, s * (1.0 + x * (1.0 - s))


def _f_rms_cast(i, x, w):
    y, _ = _rms_fwd(x, w, x.shape[1])
    return (y,), ()


def _f_rms_bwd_add(i, x, dy, dres, w, *, mask_pad):
    dx, dwr = _rms_bwd(x, w, dy, x.shape[1])
    out = dres + dx
    if mask_pad:
        out = jnp.where(_row_ids(i, x.shape[0]) >= PAD, out, 0.0)
    return (out,), (_rowsum(dwr),)


def _f_lat_norm(i, ql, kvl, qw, kvw):
    return (_rms_fwd(ql, qw, Q_LORA)[0], _rms_fwd(kvl, kvw, KV_LORA)[0]), ()


def _f_lat_norm_bwd(i, ql, kvl, dqn, dkvn, qw, kvw):
    dq, dqw = _rms_bwd(ql, qw, dqn, Q_LORA)
    dk, dkw = _rms_bwd(kvl, kvw, dkvn, KV_LORA)
    return (dq, dk), (_rowsum(dqw), _rowsum(dkw))


def _rope(x, cos, sin_s):
    return x * cos + _swap_halves(x) * sin_s


def _rope_t(dy, cos, sin_s):
    return dy * cos + _swap_halves(dy * sin_s)


def _f_mla_qk(i, qf, kvf, kpe, cos, sin_s, qw, kw):
    qs, ks, vs = [], [], []
    for h in range(MLA_HEADS):
        qn, _ = _rms_fwd(qf[:, HP * h:HP * (h + 1)], qw, QK_HEAD)
        qs += [qn[:, :QK_NOPE], _rope(qn[:, QK_NOPE:], cos, sin_s)]
        kh = jnp.concatenate([kvf[:, HP * h:HP * h + QK_NOPE], kpe], axis=1)
        kn, _ = _rms_fwd(kh, kw, QK_HEAD)
        ks += [kn[:, :QK_NOPE], _rope(kn[:, QK_NOPE:], cos, sin_s)]
        vs.append(kvf[:, HP * h + QK_NOPE:HP * (h + 1)])
    return (jnp.concatenate(qs, axis=1), jnp.concatenate(ks, axis=1), jnp.concatenate(vs, axis=1)), ()


def _f_mla_qk_bwd(i, qf, kvf, kpe, cos, sin_s, dq, dk, dv, qw, kw):
    dqf, dkvf = [], []
    dkpe = None
    dqw = None
    dkw = None
    for h in range(MLA_HEADS):
        dqh = dq[:, HP * h:HP * (h + 1)]
        dqn = jnp.concatenate([dqh[:, :QK_NOPE], _rope_t(dqh[:, QK_NOPE:], cos, sin_s)], axis=1)
        dx, dwr = _rms_bwd(qf[:, HP * h:HP * (h + 1)], qw, dqn, QK_HEAD)
        dqf.append(dx)
        dqw = _rowsum(dwr) if dqw is None else dqw + _rowsum(dwr)
        dkh = dk[:, HP * h:HP * (h + 1)]
        dkn = jnp.concatenate([dkh[:, :QK_NOPE], _rope_t(dkh[:, QK_NOPE:], cos, sin_s)], axis=1)
        kh = jnp.concatenate([kvf[:, HP * h:HP * h + QK_NOPE], kpe], axis=1)
        dx, dwr = _rms_bwd(kh, kw, dkn, QK_HEAD)
        dkvf += [dx[:, :QK_NOPE], dv[:, V_HEAD * h:V_HEAD * (h + 1)]]
        dkpe = dx[:, QK_NOPE:] if dkpe is None else dkpe + dx[:, QK_NOPE:]
        dkw = _rowsum(dwr) if dkw is None else dkw + _rowsum(dwr)
    return (jnp.concatenate(dqf, axis=1), jnp.concatenate(dkvf, axis=1), dkpe), (dqw, dkw)


def _gdn_act(i, x, halo, w8):
    tm = x.shape[0]
    halo = jnp.where(i > 0, halo, 0.0)
    c = _conv_fwd(x, halo, w8, DN_CONV)
    act, dact = _silu_parts(c)
    return act, dact


def _f_gdn_prep(i, x, halo, ab, w8, alog, dtb, sel):
    tm = x.shape[0]
    act, _ = _gdn_act(i, x, halo, w8)
    outs = []
    for part in range(2):
        for h in range(DN_HEADS):
            t = act[:, DN_WIDTH * part + DN_DIM * h:DN_WIDTH * part + DN_DIM * (h + 1)]
            outs.append(t * lax.rsqrt(jnp.sum(t * t, axis=-1, keepdims=True) + EPS))
    q = jnp.concatenate(outs[:DN_HEADS], axis=1)
    k = jnp.concatenate(outs[DN_HEADS:], axis=1)
    v = act[:, 2 * DN_WIDTH:]
    abb = _nn(ab, sel, hp=True)
    valid = _row_ids(i, tm) >= PAD
    g = jnp.where(valid, -jnp.exp(alog) * _softplus(abb[:, :DN_WIDTH] + dtb), 0.0)
    beta = jnp.where(valid, _sigmoid(abb[:, DN_WIDTH:]), 0.0)
    return (q, k, v, g, beta), ()


def _f_gdn_prep_bwd(i, x, halo, ab, dq, dk, dv, dg, dbeta, w8, alog, dtb, sel, selpick):
    tm = x.shape[0]
    act, dact = _gdn_act(i, x, halo, w8)
    douts = []
    for part, dd in enumerate((dq, dk)):
        for h in range(DN_HEADS):
            t = act[:, DN_WIDTH * part + DN_DIM * h:DN_WIDTH * part + DN_DIM * (h + 1)]
            r = lax.rsqrt(jnp.sum(t * t, axis=-1, keepdims=True) + EPS)
            y = t * r
            dy = dd[:, DN_DIM * h:DN_DIM * (h + 1)]
            douts.append(r * (dy - y * jnp.sum(dy * y, axis=-1, keepdims=True)))
    douts.append(dv)
    dc = jnp.concatenate(douts, axis=1) * dact
    abb = _nn(ab, sel, hp=True)
    valid = _row_ids(i, tm) >= PAD
    pre = abb[:, :DN_WIDTH] + dtb
    ea = jnp.exp(alog)
    g = -ea * _softplus(pre)
    dg = jnp.where(valid, dg, 0.0)
    dbeta = jnp.where(valid, dbeta, 0.0)
    da = dg * (-ea) * _sigmoid(pre)
    beta = _sigmoid(abb[:, DN_WIDTH:])
    db = dbeta * beta * (1.0 - beta)
    dab = _nn(jnp.concatenate([da, db], axis=1), selpick, hp=True)
    return (dc, dab), (_rowsum(dg * g), _rowsum(da))


def _f_conv_bwd(i, dy, dy_next, x, x_prev, w8, *, width, nt):
    dy_next = jnp.where(i < nt - 1, dy_next, 0.0)
    x_prev = jnp.where(i > 0, x_prev, 0.0)
    return (_conv_bwd_x(dy, dy_next, w8, width),), (_conv_bwd_w(dy, x, x_prev, width),)


def _f_mix(i, o_mla, o_dn, z, w_mla, w_dn):
    tm = o_mla.shape[0]
    valid = _row_ids(i, tm) >= PAD
    outs = []
    for h in range(MLA_HEADS):
        y, _ = _rms_fwd(o_mla[:, V_HEAD * h:V_HEAD * (h + 1)], w_mla, V_HEAD)
        outs.append(jnp.where(valid, y, 0.0))
    for h in range(DN_HEADS):
        y, _ = _rms_fwd(o_dn[:, DN_DIM * h:DN_DIM * (h + 1)], w_dn, DN_DIM)
        outs.append(y * _silu_parts(z[:, DN_DIM * h:DN_DIM * (h + 1)])[0])
    return (jnp.concatenate(outs, axis=1),), ()


def _f_mix_bwd(i, o_mla, o_dn, z, dy_mla, dy_dn, w_mla, w_dn):
    tm = o_mla.shape[0]
    valid = _row_ids(i, tm) >= PAD
    d_mla, d_dn, d_z = [], [], []
    dw_mla = None
    dw_dn = None
    for h in range(MLA_HEADS):
        sl = slice(V_HEAD * h, V_HEAD * (h + 1))
        dx, dwr = _rms_bwd(o_mla[:, sl], w_mla, jnp.where(valid, dy_mla[:, sl], 0.0), V_HEAD)
        d_mla.append(dx)
        dw_mla = _rowsum(dwr) if dw_mla is None else dw_mla + _rowsum(dwr)
    for h in range(DN_HEADS):
        sl = slice(DN_DIM * h, DN_DIM * (h + 1))
        y, _ = _rms_fwd(o_dn[:, sl], w_dn, DN_DIM)
        sz, dsz = _silu_parts(z[:, sl])
        d_z.append(dy_dn[:, sl] * y * dsz)
        dx, dwr = _rms_bwd(o_dn[:, sl], w_dn, dy_dn[:, sl] * sz, DN_DIM)
        d_dn.append(dx)
        dw_dn = _rowsum(dwr) if dw_dn is None else dw_dn + _rowsum(dwr)
    return ((jnp.concatenate(d_mla, axis=1), jnp.concatenate(d_dn, axis=1), jnp.concatenate(d_z, axis=1)),
            (dw_mla, dw_dn))


def _f_ffn_act(i, gate_pre, halo, up, w8, b):
    halo = jnp.where(i > 0, halo, 0.0)
    gate = _conv_fwd(gate_pre, halo, w8, FFN_CONV) + b
    return (_silu_parts(gate)[0] * up,), ()


def _f_ffn_act_bwd(i, gp, gp_prev, gp_next, up, up_next, dact, dact_next, w8, b, *, nt):
    tm = gp.shape[0]
    gp_prev = jnp.where(i > 0, gp_prev, 0.0)
    dact_next = jnp.where(i < nt - 1, dact_next, 0.0)
    cat = lambda t, t_next: jnp.concatenate([t, t_next], axis=0)
    gate = _conv_fwd(cat(gp, gp_next), gp_prev, w8, FFN_CONV) + b
    sg, dsg = _silu_parts(gate)
    dact_e = cat(dact, dact_next)
    dgate = dact_e * cat(up, up_next) * dsg
    dgate_pre = _conv_bwd_x(dgate[:tm], dgate[tm:], w8, FFN_CONV)
    dup = dact * sg[:tm]
    return (dgate_pre, dup), (_conv_bwd_w(dgate[:tm], gp, gp_prev, FFN_CONV), _rowsum(dgate[:tm]))


def _f_loss(i, h3, tgt):
    tm = h3.shape[0]
    diff = jnp.where(_row_ids(i, tm) >= ROW0, h3 - tgt, 0.0)
    part = 0.5 * jnp.sum(diff * diff) * (1.0 / D_MODEL)
    return (diff * (1.0 / D_MODEL),), (jnp.full((1, LANE), part, F32),)


def _after(fn):
    return lambda i, *a: fn(i, *a[:-1])


def _local_step(h0, tgt, w, token, late_weights, grads_ready):
    tp = h0.shape[0]
    nt = tp // TM
    bf = (D_MODEL, _MXU)
    proj, u = _norm_mm("in_proj", h0, w["attn_norm_w"], w["w_in"], after=token)
    p_qkv = lambda kind="cur": _In(proj, 3 * DN_WIDTH, 0, kind)
    p_z = _In(proj, DN_WIDTH, C_Z // DN_WIDTH)
    p_ql = _In(proj, Q_LORA, C_QL // Q_LORA)
    p_kvl = _In(proj, KV_LORA, C_KVL // KV_LORA)
    p_kpe = _In(proj, LANE, C_KPE // LANE)
    p_ab = _In(proj, LANE, C_AB // LANE)
    cos, sin_s = _In(w["cos"]), _In(w["sin_s"])

    qn, kvn = _rows("mla_lat_norm", _f_lat_norm, [p_ql, p_kvl], [w["q_a_norm_w"], w["kv_a_norm_w"]],
                    [(Q_LORA, _MXU), (KV_LORA, _MXU)])
    qf = _mm("mla_q_b", qn, w["w_q_b"], "nt")
    kvf = _mm("mla_kv_b", kvn, w["w_kv_b"], "nn")
    qk_w = [w["q_norm_w"], w["k_norm_w"]]
    q, k, v = _rows("mla_qk", _f_mla_qk, [_In(qf), _In(kvf), p_kpe, cos, sin_s], qk_w,
                    [(MLA_HEADS * HP, _MXU), (MLA_HEADS * HP, _MXU), (MLA_HEADS * V_HEAD, _MXU)])
    o_mla = _attn_fwd(q, k, v)

    dn_w = [w["dn_conv_w"], w["alog_b"], w["dtb_b"], w["sel"]]
    gq, gk, gv, gg, gb = _rows("gdn_prep", _f_gdn_prep, [p_qkv(), p_qkv("prev"), p_ab], dn_w,
                               [(DN_WIDTH, F32)] * 5)
    o_dn, s_all, t_all = _gdn_fwd(gq, gk, gv, gg, gb)

    out_w = [w["mla_out_norm_w"], w["dn_out_norm_w"]]
    mixed, = _rows("mix", _f_mix, [_In(o_mla), _In(o_dn), p_z], out_w, [bf])
    w = dict(w, **late_weights(mixed))
    h2 = _mm("out_proj", mixed, w["w_out"], "nn", resid=h0)

    gate_pre, hn = _norm_mm("ffn_gate", h2, w["ffn_norm_w"], w["w_gate"])
    up = _mm("ffn_up", hn, w["w_up"], "nt")
    ffn_w = [w["ffn_conv_w"], w["ffn_conv_b"]]
    act, = _rows("ffn_act", _f_ffn_act, [_In(gate_pre), _In(gate_pre, kind="prev"), _In(up)], ffn_w,
                 [(D_FF, _MXU)])
    dh3, loss = _mm_rows("ffn_down_loss", act, w["w_down"], "nn", lambda i, y, r, t: _f_loss(i, r + y, t),
                         [_In(h2), _In(tgt)], [], [(D_MODEL, F32)], [(1, LANE)])

    g = {}
    dact = _mm("ffn_down_dx", dh3, w["w_down"], "nt")
    g["w_down"] = _mm("ffn_down_dw", act, dh3, "tn", out_dtype=_MXU)
    dgate_pre, dup, g["ffn_conv_w"], g["ffn_conv_b"] = _rows(
        "ffn_act_bwd", functools.partial(_f_ffn_act_bwd, nt=nt),
        [_In(gate_pre), _In(gate_pre, kind="prev"), _In(gate_pre, kind="next"), _In(up), _In(up, kind="next"),
         _In(dact), _In(dact, kind="next")], ffn_w,
        [(D_FF, _MXU), (D_FF, _MXU)], [(8, D_FF), (1, D_FF)])
    g["w_gate"] = _mm("ffn_gate_dw", dgate_pre, hn, "tn", out_dtype=_MXU)
    g["w_up"] = _mm("ffn_up_dw", dup, hn, "tn", out_dtype=_MXU)
    tok = grads_ready(g, ("w_down", "w_gate", "w_up"))
    dhn = _mm("ffn_gate_dx", dgate_pre, w["w_gate"], "nn", after=tok)
    dh2, g["ffn_norm_w"] = _mm_rows(
        "ffn_up_dx_rms", dup, w["w_up"], "nn",
        lambda i, y, d1, x, dres, nw: _f_rms_bwd_add(i, x, d1 + y, dres, nw, mask_pad=True),
        [_In(dhn), _In(h2), _In(dh3)], [w["ffn_norm_w"]], [(D_MODEL, F32)], [(1, D_MODEL)])

    dmixed = _mm("out_proj_dx", dh2, w["w_out"], "nt")
    g["w_out"] = _mm("out_proj_dw", mixed, dh2, "tn", out_dtype=_MXU)
    half = MLA_HEADS * V_HEAD
    do_mla, do_dn, dz, g["mla_out_norm_w"], g["dn_out_norm_w"] = _rows(
        "mix_bwd", _f_mix_bwd, [_In(o_mla), _In(o_dn), p_z, _In(dmixed, half, 0), _In(dmixed, half, 1)], out_w,
        [(half, F32), (DN_WIDTH, F32), (DN_WIDTH, _MXU)], [(1, V_HEAD), (1, DN_DIM)])

    dq, dk, dv = _attn_bwd(q, k, v, do_mla)
    dqf, dkvf, dkpe, g["q_norm_w"], g["k_norm_w"] = _rows(
        "mla_qk_bwd", _f_mla_qk_bwd, [_In(qf), _In(kvf), p_kpe, cos, sin_s, _In(dq), _In(dk), _In(dv)], qk_w,
        [(MLA_HEADS * HP, _MXU), (MLA_HEADS * HP, _MXU), (LANE, _MXU)], [(1, HP), (1, HP)])
    dqn = _mm("mla_q_b_dx", dqf, w["w_q_b"], "nn")
    g["w_q_b"] = _mm("mla_q_b_dw", dqf, qn, "tn")
    dkvn = _mm("mla_kv_b_dx", dkvf, w["w_kv_b"], "nt")
    g["w_kv_b"] = _mm("mla_kv_b_dw", kvn, dkvf, "tn")
    tok = grads_ready(g, ("w_out", "w_q_b", "w_kv_b"))
    dql, dkvl, g["q_a_norm_w"], g["kv_a_norm_w"] = _rows(
        "mla_lat_norm_bwd", _after(_f_lat_norm_bwd), [p_ql, p_kvl, _In(dqn), _In(dkvn)],
        [w["q_a_norm_w"], w["kv_a_norm_w"], tok],[(Q_LORA, _MXU), (KV_LORA, _MXU)], [(1, Q_LORA), (1, KV_LORA)])

    dgq, dgk, dgv, dgg, dgb = _gdn_bwd(gq, gk, gv, gg, gb, s_all, t_all, do_dn)
    dc, dab, g["alog_b"], g["dtb_b"] = _rows(
        "gdn_prep_bwd", _f_gdn_prep_bwd,
        [p_qkv(), p_qkv("prev"), p_ab, _In(dgq), _In(dgk), _In(dgv), _In(dgg), _In(dgb)], dn_w + [w["selpick"]],
        [(3 * DN_WIDTH, F32), (LANE, _MXU)], [(1, DN_WIDTH), (1, DN_WIDTH)])
    dqkv, g["dn_conv_w"] = _rows(
        "gdn_conv_bwd", functools.partial(_f_conv_bwd, width=DN_CONV, nt=nt),
        [_In(dc), _In(dc, kind="next"), p_qkv(), p_qkv("prev")], [w["dn_conv_w"]],
        [(3 * DN_WIDTH, _MXU)], [(8, 3 * DN_WIDTH)])

    dproj = jnp.concatenate([dqkv, dz, dql, dkvl, dkpe, dab], axis=1)
    g["w_in"] = _mm("in_proj_dw", dproj, u, "tn", out_dtype=_MXU)
    tok = grads_ready(g, ("w_in",))
    dh0, g["attn_norm_w"] = _mm_rows(
        "in_proj_dx_rms", dproj, w["w_in"], "nn",
        lambda i, du, x, dres, nw, _tok: _f_rms_bwd_add(i, x, du, dres, nw, mask_pad=False),
        [_In(h0), _In(dh2)], [w["attn_norm_w"], tok], [(D_MODEL, F32)], [(1, D_MODEL)])
    return loss, dh0, g


def _w_in_to_padded(w):
    c1, c2, c3 = Q_LORA, Q_LORA + KV_LORA, Q_LORA + KV_LORA + QK_ROPE
    c4 = c3 + 3 * DN_WIDTH
    c5 = c4 + DN_WIDTH
    z = lambda n: jnp.zeros((n, w.shape[1]), w.dtype)
    return jnp.concatenate([w[c3:c4], w[c4:c5], w[:c1], w[c1:c2], w[c2:c3], z(LANE - QK_ROPE),
                            w[c5:], z(LANE - 2 * DN_HEADS)], axis=0)


def _w_in_from_padded(g):
    return jnp.concatenate([g[C_QL:C_QL + Q_LORA], g[C_KVL:C_KVL + KV_LORA], g[C_KPE:C_KPE + QK_ROPE],
                            g[:C_Z + DN_WIDTH], g[C_AB:C_AB + 2 * DN_HEADS]], axis=0)


def _w_q_b_to_padded(w):
    r = w.shape[1]
    w = w.reshape(MLA_HEADS, QK_HEAD, r)
    return jnp.pad(w, ((0, 0), (0, HP - QK_HEAD), (0, 0))).reshape(MLA_HEADS * HP, r)


def _w_q_b_from_padded(g):
    r = g.shape[1]
    return g.reshape(MLA_HEADS, HP, r)[:, :QK_HEAD].reshape(MLA_HEADS * QK_HEAD, r)


def _pad_rows8(w):
    return jnp.pad(w, ((0, 8 - w.shape[0]), (0, 0)))


def _prepare(full, tp):
    w = {}
    mx = lambda a: a.astype(_MXU)
    w["attn_norm_w"] = full["attn_norm_w"]
    w["w_in"] = mx(_w_in_to_padded(full["w_in"]))
    w["q_a_norm_w"] = full["q_a_norm_w"]
    w["kv_a_norm_w"] = full["kv_a_norm_w"]
    w["w_q_b"] = mx(_w_q_b_to_padded(full["w_q_b"]))
    w["w_kv_b"] = mx(full["w_kv_b"])
    w["q_norm_w"] = jnp.pad(full["q_norm_w"], ((0, 0), (0, HP - QK_HEAD)))
    w["k_norm_w"] = jnp.pad(full["k_norm_w"], ((0, 0), (0, HP - QK_HEAD)))
    w["mla_out_norm_w"] = full["mla_out_norm_w"]
    w["dn_out_norm_w"] = full["dn_out_norm_w"]
    w["dn_conv_w"] = _pad_rows8(full["dn_conv_w"])
    w["alog_b"] = jnp.repeat(full["dn_A_log"], DN_DIM, axis=1)
    w["dtb_b"] = jnp.repeat(full["dn_dt_bias"], DN_DIM, axis=1)
    w["ffn_norm_w"] = full["ffn_norm_w"]
    w["ffn_conv_w"] = _pad_rows8(full["ffn_conv_w"])
    w["ffn_conv_b"] = full["ffn_conv_b"]
    for n in _LATE:
        if n in full:
            w[n] = mx(full[n])
    half = QK_ROPE // 2
    inv = ROPE_THETA ** (-jnp.arange(half, dtype=F32) / half)
    ang = (jnp.arange(tp, dtype=jnp.int32) - PAD).astype(F32)[:, None] * inv[None, :]
    zc = jnp.zeros((tp, LANE - QK_ROPE), F32)
    w["cos"] = jnp.concatenate([jnp.cos(ang), jnp.cos(ang), zc], axis=1)
    w["sin_s"] = jnp.concatenate([-jnp.sin(ang), jnp.sin(ang), zc], axis=1)
    lane = jnp.arange(2 * DN_WIDTH)[None, :]
    src = jnp.arange(LANE)[:, None]
    w["sel"] = ((lane // DN_DIM) == src).astype(F32)
    w["selpick"] = ((src.T == (lane.T // DN_DIM)) & (lane.T % DN_DIM == 0)).astype(F32)
    return w


def _grads_to_natural(g):
    convert = {
        "w_in": ("w_in", _w_in_from_padded),
        "w_q_b": ("w_q_b", _w_q_b_from_padded),
        "q_norm_w": ("q_norm_w", lambda a: a[:, :QK_HEAD]),
        "k_norm_w": ("k_norm_w", lambda a: a[:, :QK_HEAD]),
        "dn_conv_w": ("dn_conv_w", lambda a: a[:DN_CONV]),
        "ffn_conv_w": ("ffn_conv_w", lambda a: a[:FFN_CONV]),
        "alog_b": ("dn_A_log", lambda a: a[:, ::DN_DIM]),
        "dtb_b": ("dn_dt_bias", lambda a: a[:, ::DN_DIM]),
    }
    n = {}
    for key, a in g.items():
        name, fn = convert.get(key, (key, lambda t: t))
        n[name] = fn(a)
    return n


_MESH = pl.DeviceIdType.MESH
_ANY = pl.BlockSpec(memory_space=pl.ANY)
_CHIP_FLIPS = ((1, 0), (0, 1), (1, 1))


def _me():
    return lax.axis_index("x"), lax.axis_index("y"), lax.axis_index("c")


def _all_gather(name, blk):
    def body(x_ref, out_ref, send_sems, recv_sems, local_sem):
        x, y, c = _me()
        me, sib = (x, y, c), (x, y, 1 - c)
        chips = [(x ^ fx, y ^ fy) for fx, fy in _CHIP_FLIPS]

        def slot(p):
            return out_ref.at[4 * p[0] + 2 * p[1] + p[2]]

        def copy(k, block, to, src=None):
            return pltpu.make_async_remote_copy(
                src_ref=slot(block) if src is None else src, dst_ref=slot(block),
                send_sem=send_sems.at[k], recv_sem=recv_sems.at[k], device_id=to, device_id_type=_MESH)

        mine = pltpu.make_async_copy(x_ref, slot(me), local_sem)
        mine.start()
        first = [copy(0, me, sib, src=x_ref)]
        first += [copy(1 + j, me, (*chip, c), src=x_ref) for j, chip in enumerate(chips)]
        for cp in first:
            cp.start()
        passed = [copy(4 + j, (*chip, c), sib) for j, chip in enumerate(chips)]
        for j, chip in enumerate(chips):
            copy(1 + j, (*chip, c), me).wait_recv()
            passed[j].start()
        copy(0, sib, me).wait_recv()
        for j, chip in enumerate(chips):
            copy(4 + j, (*chip, 1 - c), me).wait_recv()
        for cp in first + passed:
            cp.wait_send()
        mine.wait()

    return pl.pallas_call(
        body, name=name, in_specs=[_ANY], out_specs=_ANY,
        out_shape=jax.ShapeDtypeStruct((N_DEV,) + blk.shape, blk.dtype),
        scratch_shapes=[pltpu.SemaphoreType.DMA((7,)), pltpu.SemaphoreType.DMA((7,)), pltpu.SemaphoreType.DMA],
    )(blk)


def _rs_sibling(name, gb):
    def body(g_ref, out_ref, send_sems, recv_sems):
        x, y, c = _me()
        cps = []
        for j in range(4):
            cp = pltpu.make_async_remote_copy(
                src_ref=g_ref.at[2 * j + (1 - c)], dst_ref=out_ref.at[j], send_sem=send_sems.at[j],
                recv_sem=recv_sems.at[j], device_id=(x, y, 1 - c), device_id_type=_MESH)
            cp.start()
            cps.append(cp)
        for cp in cps:
            cp.wait()

    return pl.pallas_call(
        body, name=name, in_specs=[_ANY], out_specs=_ANY,
        out_shape=jax.ShapeDtypeStruct((4,) + gb.shape[1:], gb.dtype),
        scratch_shapes=[pltpu.SemaphoreType.DMA((4,)), pltpu.SemaphoreType.DMA((4,))],
    )(gb)


def _rs_chips(name, s1):
    def body(s_ref, out_ref, send_sems, recv_sems):
        x, y, c = _me()
        cps = []
        for k, (fx, fy) in enumerate(_CHIP_FLIPS):
            px, py = x ^ fx, y ^ fy
            cp = pltpu.make_async_remote_copy(
                src_ref=s_ref.at[2 * px + py], dst_ref=out_ref.at[k], send_sem=send_sems.at[k],
                recv_sem=recv_sems.at[k], device_id=(px, py, c), device_id_type=_MESH)
            cp.start()
            cps.append(cp)
        for cp in cps:
            cp.wait()

    return pl.pallas_call(
        body, name=name, in_specs=[_ANY], out_specs=_ANY,
        out_shape=jax.ShapeDtypeStruct((3,) + s1.shape[1:], s1.dtype),
        scratch_shapes=[pltpu.SemaphoreType.DMA((3,)), pltpu.SemaphoreType.DMA((3,))],
    )(s1)


def _row_tile(r):
    divs = [d for d in range(16, min(r, 512) + 1, 16) if r % d == 0]
    return divs[-1] if divs else r


def _pair_sum(name, gb, recv):
    _, r, cols = gb.shape
    tm = _row_tile(r)
    c = lax.axis_index("c").astype(jnp.int32).reshape(1)

    def body(c_ref, a_ref, b_ref, o_ref, ob_ref):
        s = a_ref[...] + b_ref[...]
        o_ref[...] = s
        ob_ref[...] = s.astype(BF16)

    blk = pl.BlockSpec((1, tm, cols), lambda j, i, c_ref: (j, i, 0))
    return pl.pallas_call(
        body, name=name,
        grid_spec=pltpu.PrefetchScalarGridSpec(
            num_scalar_prefetch=1, grid=(4, r // tm),
            in_specs=[pl.BlockSpec((1, tm, cols), lambda j, i, c_ref: (2 * j + c_ref[0], i, 0)), blk],
            out_specs=[blk, blk]),
        out_shape=[jax.ShapeDtypeStruct((4, r, cols), F32), jax.ShapeDtypeStruct((4, r, cols), BF16)],
        compiler_params=pltpu.CompilerParams(dimension_semantics=("parallel", "parallel")),
    )(c, gb, recv)


def _sum_parts(name, parts):
    _, r, cols = parts[0][0].shape
    tm = _row_tile(r)
    idx = jnp.stack([jnp.asarray(s, jnp.int32) for _, s in parts])
    n = len(parts)

    def body(idx_ref, *refs):
        g = refs[0][0].astype(F32)
        for p_ref in refs[1:n]:
            g = g + p_ref[0].astype(F32)
        refs[n][...] = g

    return pl.pallas_call(
        body, name=name,
        grid_spec=pltpu.PrefetchScalarGridSpec(
            num_scalar_prefetch=1, grid=(r // tm,),
            in_specs=[pl.BlockSpec((1, tm, cols), lambda i, idx_ref, p=p: (idx_ref[p], i, 0)) for p in range(n)],
            out_specs=pl.BlockSpec((tm, cols), lambda i, idx_ref: (i, 0))),
        out_shape=jax.ShapeDtypeStruct((r, cols), F32),
        compiler_params=pltpu.CompilerParams(dimension_semantics=("parallel",)),
    )(idx, *[a for a, _ in parts])


def _adam(name, parts, w, m, v):
    r, cols = w.shape
    tm = _row_tile(r)
    idx = jnp.stack([jnp.asarray(s, jnp.int32) for _, s in parts])
    n = len(parts)

    def body(idx_ref, *refs):
        g = refs[0][0].astype(F32)
        for p_ref in refs[1:n]:
            g = g + p_ref[0].astype(F32)
        w_ref, m_ref, v_ref, g_out, d_out, m_out, v_out = refs[n:]
        m_new = ADAM_B1 * m_ref[...] + (1.0 - ADAM_B1) * g
        v_new = ADAM_B2 * v_ref[...] + (1.0 - ADAM_B2) * (g * g)
        m_hat = m_new / (1.0 - ADAM_B1 ** ADAM_STEP)
        v_hat = v_new / (1.0 - ADAM_B2 ** ADAM_STEP)
        g_out[...] = g
        d_out[...] = -ADAM_LR * (m_hat / (jnp.sqrt(v_hat) + ADAM_EPS) + ADAM_WD * w_ref[...])
        m_out[...] = m_new
        v_out[...] = v_new

    part_specs = [pl.BlockSpec((1, tm, cols), lambda i, idx_ref, p=p: (idx_ref[p], i, 0)) for p in range(n)]
    flat = pl.BlockSpec((tm, cols), lambda i, idx_ref: (i, 0))
    return pl.pallas_call(
        body, name=name,
        grid_spec=pltpu.PrefetchScalarGridSpec(
            num_scalar_prefetch=1, grid=(r // tm,), in_specs=part_specs + [flat] * 3, out_specs=[flat] * 4),
        out_shape=[jax.ShapeDtypeStruct((r, cols), F32)] * 4,
        compiler_params=pltpu.CompilerParams(dimension_semantics=("parallel",)),
    )(idx, *[a for a, _ in parts], w, m, v)


def _all_gather_many(name, blks):
    n = len(blks)

    def body(*refs):
        x_refs, out_refs = refs[:n], refs[n:2 * n]
        send_sems, recv_sems, local_sems = refs[2 * n:]
        x, y, c = _me()
        me, sib = (x, y, c), (x, y, 1 - c)
        chips = [(x ^ fx, y ^ fy) for fx, fy in _CHIP_FLIPS]

        def slot(a, p):
            return out_refs[a].at[4 * p[0] + 2 * p[1] + p[2]]

        def copy(a, k, block, to, src=None):
            return pltpu.make_async_remote_copy(
                src_ref=slot(a, block) if src is None else src, dst_ref=slot(a, block),
                send_sem=send_sems.at[7 * a + k], recv_sem=recv_sems.at[7 * a + k], device_id=to,
                device_id_type=_MESH)

        mine = [pltpu.make_async_copy(x_refs[a], slot(a, me), local_sems.at[a]) for a in range(n)]
        first = []
        for a in range(n):
            mine[a].start()
            first.append(copy(a, 0, me, sib, src=x_refs[a]))
            first += [copy(a, 1 + j, me, (*chip, c), src=x_refs[a]) for j, chip in enumerate(chips)]
        for cp in first:
            cp.start()
        passed = []
        for j, chip in enumerate(chips):
            for a in range(n):
                copy(a, 1 + j, (*chip, c), me).wait_recv()
                cp = copy(a, 4 + j, (*chip, c), sib)
                cp.start()
                passed.append(cp)
        for a in range(n):
            copy(a, 0, sib, me).wait_recv()
            for j, chip in enumerate(chips):
                copy(a, 4 + j, (*chip, 1 - c), me).wait_recv()
        for cp in first + passed:
            cp.wait_send()
        for cp in mine:
            cp.wait()

    return pl.pallas_call(
        body, name=name, in_specs=[_ANY] * n, out_specs=[_ANY] * n,
        out_shape=[jax.ShapeDtypeStruct((N_DEV,) + b.shape, b.dtype) for b in blks],
        scratch_shapes=[pltpu.SemaphoreType.DMA((7 * n,)), pltpu.SemaphoreType.DMA((7 * n,)),
                        pltpu.SemaphoreType.DMA((n,))],
    )(*blks)


def _rs_sibling_many(name, gbs):
    n = len(gbs)

    def body(*refs):
        g_refs, out_refs = refs[:n], refs[n:2 * n]
        send_sems, recv_sems = refs[2 * n:]
        x, y, c = _me()
        cps = []
        for a in range(n):
            for j in range(4):
                cp = pltpu.make_async_remote_copy(
                    src_ref=g_refs[a].at[2 * j + (1 - c)], dst_ref=out_refs[a].at[j],
                    send_sem=send_sems.at[4 * a + j], recv_sem=recv_sems.at[4 * a + j],
                    device_id=(x, y, 1 - c), device_id_type=_MESH)
                cp.start()
                cps.append(cp)
        for cp in cps:
            cp.wait()

    return pl.pallas_call(
        body, name=name, in_specs=[_ANY] * n, out_specs=[_ANY] * n,
        out_shape=[jax.ShapeDtypeStruct((4,) + g.shape[1:], g.dtype) for g in gbs],
        scratch_shapes=[pltpu.SemaphoreType.DMA((4 * n,)), pltpu.SemaphoreType.DMA((4 * n,))],
    )(*gbs)


def _rs_chips_many(name, s1s):
    n = len(s1s)

    def body(*refs):
        s_refs, out_refs = refs[:n], refs[n:2 * n]
        send_sems, recv_sems = refs[2 * n:]
        x, y, c = _me()
        cps = []
        for a in range(n):
            for k, (fx, fy) in enumerate(_CHIP_FLIPS):
                px, py = x ^ fx, y ^ fy
                cp = pltpu.make_async_remote_copy(
                    src_ref=s_refs[a].at[2 * px + py], dst_ref=out_refs[a].at[k],
                    send_sem=send_sems.at[3 * a + k], recv_sem=recv_sems.at[3 * a + k],
                    device_id=(px, py, c), device_id_type=_MESH)
                cp.start()
                cps.append(cp)
        for cp in cps:
            cp.wait()

    return pl.pallas_call(
        body, name=name, in_specs=[_ANY] * n, out_specs=[_ANY] * n,
        out_shape=[jax.ShapeDtypeStruct((3,) + s.shape[1:], s.dtype) for s in s1s],
        scratch_shapes=[pltpu.SemaphoreType.DMA((3 * n,)), pltpu.SemaphoreType.DMA((3 * n,))],
    )(*s1s)


_HBM = pl.BlockSpec(memory_space=pltpu.HBM)
_SEM = pl.BlockSpec(memory_space=pltpu.SEMAPHORE)
_EFFECT = pltpu.SideEffectType.DATAFLOW_SIDE_EFFECTING


def _push_copies(src_refs, land_refs, send_sems, recv_sems, src_by_peer):
    x, y, c = _me()
    my_id = 4 * x + 2 * y + c
    out = []
    for a in range(len(src_refs)):
        for f in range(1, N_DEV):
            px, py, pc = x ^ (f >> 2), y ^ ((f >> 1) & 1), c ^ (f & 1)
            pid = 4 * px + 2 * py + pc
            src = src_refs[a].at[pid] if src_by_peer else src_refs[a]
            start = pltpu.make_async_remote_copy(
                src_ref=src, dst_ref=land_refs[a].at[my_id], send_sem=send_sems.at[7 * a + f - 1],
                recv_sem=recv_sems.at[7 * a + f - 1], device_id=(px, py, pc), device_id_type=_MESH)
            landed = pltpu.make_async_remote_copy(
                src_ref=src, dst_ref=land_refs[a].at[pid], send_sem=send_sems.at[7 * a + f - 1],
                recv_sem=recv_sems.at[7 * a + f - 1], device_id=(px, py, pc), device_id_type=_MESH)
            out.append((start, landed))
    return out


def _push_start(name, srcs, src_by_peer, after):
    n = len(srcs)
    lands = [jax.ShapeDtypeStruct((N_DEV,) + (s.shape[1:] if src_by_peer else s.shape), s.dtype) for s in srcs]

    def body(*refs):
        src_refs, land_refs = refs[:n], refs[n:2 * n]
        send_sems, recv_sems = refs[2 * n + 1], refs[2 * n + 2]
        token = refs[-1]
        for start, _ in _push_copies(src_refs, land_refs, send_sems, recv_sems, src_by_peer):
            start.start()
        token[...] = jnp.zeros_like(token)

    hbm = lambda a: pltpu.with_memory_space_constraint(a, pltpu.HBM)
    res = pl.pallas_call(
        body, name=name,
        out_shape=(pltpu.SemaphoreType.DMA((7 * n,)), pltpu.SemaphoreType.DMA((7 * n,)),
                   *[pltpu.HBM(s.shape, s.dtype) for s in srcs], *[pltpu.HBM(s.shape, s.dtype) for s in lands],
                   jax.ShapeDtypeStruct((8, LANE), F32)),
        in_specs=[_HBM] * (2 * n) + [_ANY],
        out_specs=(_SEM, _SEM, *[_HBM] * (2 * n), pl.BlockSpec(memory_space=pltpu.VMEM)),
        input_output_aliases={i: 2 + i for i in range(2 * n)},
        compiler_params=pltpu.CompilerParams(has_side_effects=_EFFECT),
    )(*[hbm(s) for s in srcs], *[hbm(lax.empty(s.shape, s.dtype)) for s in lands], after)
    return res[0], res[1], list(res[2:2 + n]), list(res[2 + n:2 + 2 * n]), res[-1]


def _push_wait(name, send_sems, recv_sems, srcs, lands, src_by_peer, after):
    n = len(srcs)

    def body(*refs):
        src_refs, land_refs = refs[:n], refs[n:2 * n]
        s_sems, r_sems = refs[2 * n], refs[2 * n + 1]
        for _, landed in _push_copies(src_refs, land_refs, s_sems, r_sems, src_by_peer):
            landed.wait_send()
            landed.wait_recv()

    res = pl.pallas_call(
        body, name=name,
        out_shape=tuple(pltpu.HBM(s.shape, s.dtype) for s in list(srcs) + list(lands)),
        in_specs=[_HBM] * (2 * n) + [_SEM, _SEM, _ANY],
        out_specs=tuple([_HBM] * (2 * n)),
        input_output_aliases={i: i for i in range(2 * n)},
        compiler_params=pltpu.CompilerParams(has_side_effects=_EFFECT),
    )(*srcs, *lands, send_sems, recv_sems, after)
    return list(res[:n]), list(res[n:])


_SHARDED = (
    ("meta_tokens", 1, (N_META, D_MODEL)),
    ("w_in", 1, (D_MODEL, IN_COLS)),
    ("w_q_b", 1, (Q_LORA, MLA_HEADS * QK_HEAD)),
    ("w_kv_b", 1, (KV_LORA, MLA_HEADS * (QK_NOPE + V_HEAD))),
    ("dn_conv_w", 1, (DN_CONV, 3 * DN_WIDTH)),
    ("w_out", 0, (2 * DN_WIDTH, D_MODEL)),
    ("w_gate", 1, (D_MODEL, D_FF)),
    ("w_up", 1, (D_MODEL, D_FF)),
    ("ffn_conv_w", 1, (FFN_CONV, D_FF)),
    ("w_down", 0, (D_FF, D_MODEL)),
)
_MXU_GATHERED = ("w_in", "w_q_b", "w_kv_b", "w_out", "w_gate", "w_up", "w_down")
_F32_GATHERED = ("meta_tokens", "dn_conv_w", "ffn_conv_w")
_EARLY = ("w_in", "w_q_b", "w_kv_b")
_LATE = ("w_out", "w_gate", "w_up", "w_down")
_TRANSPOSED = ("w_in", "w_q_b", "w_gate", "w_up")
_REPLICATED = (
    ("attn_norm_w", D_MODEL), ("q_a_norm_w", Q_LORA), ("kv_a_norm_w", KV_LORA), ("q_norm_w", QK_HEAD),
    ("k_norm_w", QK_HEAD), ("mla_out_norm_w", V_HEAD), ("dn_A_log", DN_HEADS), ("dn_dt_bias", DN_HEADS),
    ("dn_out_norm_w", DN_DIM), ("ffn_norm_w", D_MODEL), ("ffn_conv_b", D_FF),
)
_PACK_COLS = 1024
_PACK_ROW_MULT = 320
_SMALL_SHAPE = (8, 768)
_SMALL_BLOCK = (8, 512)


def _local_shape(dim, shape):
    return (shape[0] // N_DEV, shape[1]) if dim == 0 else (shape[0], shape[1] // N_DEV)


def _pack_rows(n, mult):
    rows = -(-n // _PACK_COLS)
    return -(-rows // mult) * mult


def _pack(flats, mult, axis=0):
    cat = jnp.concatenate(flats, axis=-1)
    n = cat.shape[-1]
    r = _pack_rows(n, mult)
    pad = [(0, 0)] * (cat.ndim - 1) + [(0, r * _PACK_COLS - n)]
    return jnp.pad(cat, pad).reshape(cat.shape[:-1] + (r, _PACK_COLS))


def _to_blocks(full, dim):
    r, c = full.shape
    if dim == 0:
        return full.reshape(N_DEV, (r // N_DEV) * c)
    return full.reshape(r, N_DEV, c // N_DEV).transpose(1, 0, 2).reshape(N_DEV, r * (c // N_DEV))


def _from_blocks(blocks, dim, shape):
    r, c = shape
    if dim == 0:
        return blocks.reshape(r, c)
    return blocks.reshape(N_DEV, r, c // N_DEV).transpose(1, 0, 2).reshape(r, c)


def _split(flat, sizes):
    out, o = [], 0
    for s in sizes:
        out.append(flat[..., o:o + s])
        o += s
    return out


def _gather_weights(local, names, dtype, mult):
    specs = [s for s in _SHARDED if s[0] in names]
    pack = _pack([local[n].astype(dtype).reshape(-1) for n, _, _ in specs], mult)
    got = _all_gather("gather_" + "_".join(n[:5] for n in names[:2]), pack)
    flat = got.reshape(N_DEV, -1)
    sizes = [math.prod(_local_shape(d, s)) for _, d, s in specs]
    return {n: _from_blocks(p, d, s) for (n, d, s), p in zip(specs, _split(flat, sizes))}


def kernel(x, meta_tokens, attn_norm_w, w_in, q_a_norm_w, w_q_b, kv_a_norm_w, w_kv_b, q_norm_w, k_norm_w, mla_out_norm_w, dn_conv_w, dn_A_log, dn_dt_bias, dn_out_norm_w, w_out, ffn_norm_w, w_gate, w_up, ffn_conv_w, ffn_conv_b, w_down, loss_target, m_meta_tokens, m_attn_norm_w, m_w_in, m_q_a_norm_w, m_w_q_b, m_kv_a_norm_w, m_w_kv_b, m_q_norm_w, m_k_norm_w, m_mla_out_norm_w, m_dn_conv_w, m_dn_A_log, m_dn_dt_bias, m_dn_out_norm_w, m_w_out, m_ffn_norm_w, m_w_gate, m_w_up, m_ffn_conv_w, m_ffn_conv_b, m_w_down, v_meta_tokens, v_attn_norm_w, v_w_in, v_q_a_norm_w, v_w_q_b, v_kv_a_norm_w, v_w_kv_b, v_q_norm_w, v_k_norm_w, v_mla_out_norm_w, v_dn_conv_w, v_dn_A_log, v_dn_dt_bias, v_dn_out_norm_w, v_w_out, v_ffn_norm_w, v_w_gate, v_w_up, v_ffn_conv_w, v_ffn_conv_b, v_w_down):
    names = [n for n, _, _ in _SHARDED] + [n for n, _ in _REPLICATED]
    given = dict(locals())
    two_d = lambda a: a.reshape(a.shape[-2:])
    view = lambda a, n: two_d(a).T if n in _TRANSPOSED else two_d(a)
    wl = {n: view(given[n], n) for n in names}
    ml = {n: view(given["m_" + n], n) for n in names}
    vl = {n: view(given["v_" + n], n) for n in names}
    out_shapes = {n: given[n].shape for n in names}

    spec = {n: (d, s) for n, d, s in _SHARDED}
    small_sizes = [math.prod(_local_shape(*spec[n])) for n in _F32_GATHERED]

    def small_block(d):
        cat = jnp.concatenate([d[n].reshape(d[n].shape[:-2] + (-1,)) for n in _F32_GATHERED], axis=-1)
        pad = [(0, 0)] * (cat.ndim - 1) + [(0, math.prod(_SMALL_BLOCK) - cat.shape[-1])]
        return jnp.pad(cat, pad).reshape(cat.shape[:-1] + _SMALL_BLOCK)

    def shard(n):
        return wl[n].astype(_MXU)

    def from_slots(n, blocks):
        d, s = spec[n]
        if d == 0 or n in _TRANSPOSED:
            return blocks.reshape(-1, blocks.shape[-1])
        return blocks.transpose(1, 0, 2).reshape(s)

    my_id = 4 * lax.axis_index("x") + 2 * lax.axis_index("y") + lax.axis_index("c")
    got = _all_gather_many("gather_early", [shard(n) for n in _EARLY] + [small_block(wl)])
    full = {n: a for n, a in wl.items() if n not in _LATE}
    for n, blocks in zip(_EARLY, got):
        full[n] = from_slots(n, blocks)
    for n, p in zip(_F32_GATHERED, _split(got[-1].reshape(N_DEV, -1), small_sizes)):
        full[n] = _from_blocks(p, *spec[n])
    late_own = [shard(n) for n in _LATE]
    l_send, l_recv, l_src, l_land, token = _push_start("gather_late_start", late_own, False, got[-1])

    def late_weights(after):
        _, lands = _push_wait("gather_late_wait", l_send, l_recv, l_src, l_land, False, after)
        out = {}
        for n, land, own in zip(_LATE, lands, late_own):
            out[n] = from_slots(n, lax.dynamic_update_slice(land, own[None], (my_id, 0, 0))).astype(_MXU)
        return out

    def dest_blocks(n, a):
        d, s = spec[n]
        r, c = _local_shape(d, s)
        if n in _TRANSPOSED:
            return a.reshape(N_DEV, c, r)
        return a.reshape(N_DEV, r, c) if d == 0 else a.reshape(r, N_DEV, c).transpose(1, 0, 2)

    pushed = []

    def grads_ready(g, names):
        nat = _grads_to_natural({n: g[n] for n in names})
        blocks = [dest_blocks(n, nat[n]).astype(_MXU) for n in names]
        sends, recvs, srcs, lands, tok = _push_start("rs_" + names[0] + "_start", blocks, True, token)
        pushed.append((names, sends, recvs, srcs, lands))
        return tok

    seq = x.shape[1]
    tp = ROW0 + seq
    h0 = jnp.concatenate([jnp.zeros((PAD, D_MODEL), F32), full["meta_tokens"], x[0]], axis=0)
    tgt = jnp.concatenate([jnp.zeros((ROW0, D_MODEL), F32), loss_target[0]], axis=0)
    loss, dh0, g = _local_step(h0, tgt, _prepare(full, tp), token, late_weights, grads_ready)
    g = _grads_to_natural(g)
    g["meta_tokens"] = dh0[PAD:ROW0]
    grad_x = dh0[ROW0:][None]

    big = [{}, {}, {}, {}]
    rep_names = [n for n, _ in _REPLICATED]
    pieces = [g[n].reshape(-1) for n in rep_names] + [loss[0, :1]] + [g[n].reshape(-1) for n in _F32_GATHERED]
    sizes = [p.shape[0] for p in pieces]
    cat = jnp.concatenate(pieces)
    cols = -(-cat.shape[0] // (8 * LANE)) * LANE
    mine = jnp.pad(cat, (0, 8 * cols - cat.shape[0])).reshape(8, cols)
    everyone = _all_gather("gather_small_grads", mine)
    total = _sum_parts("sum_small_grads", [(everyone, d) for d in range(N_DEV)])
    tot = dict(zip(rep_names + ["loss"] + list(_F32_GATHERED), _split(total.reshape(-1), sizes)))

    def small(d):
        cat = jnp.concatenate([d[n].reshape(-1) for n in rep_names])
        return jnp.pad(cat, (0, math.prod(_SMALL_SHAPE) - cat.shape[0])).reshape(_SMALL_SHAPE)

    sm = _adam("adam_replicated", [(small(tot)[None], 0)], small(wl), small(ml), small(vl))
    sm = [dict(zip(rep_names, _split(a.reshape(-1), [n for _, n in _REPLICATED]))) for a in sm]
    mine_of = {}
    for n in _F32_GATHERED:
        d, s = spec[n]
        r, c = _local_shape(d, s)
        mine_of[n] = lax.dynamic_slice(tot[n].reshape(s), (0, my_id * c), (r, c))
    res = _adam("adam_small_sharded", [(small_block(mine_of)[None], 0)], small_block(wl), small_block(ml),
                small_block(vl))
    for kind, a in enumerate(res):
        big[kind].update(zip(_F32_GATHERED, _split(a.reshape(-1), small_sizes)))

    for names, sends, recvs, srcs, lands in pushed:
        srcs, lands = _push_wait("rs_" + names[0] + "_wait", sends, recvs, srcs, lands, True, dh0)
        for n, src, land in zip(names, srcs, lands):
            parts = [(src, my_id)] + [(land, my_id ^ f) for f in range(1, N_DEV)]
            for kind, a in enumerate(_adam("adam_" + n, parts, wl[n], ml[n], vl[n])):
                big[kind][n] = a

    outs = [tot["loss"].reshape(()), grad_x]
    for kind in range(4):
        for n in ("meta_tokens", "attn_norm_w", "w_in", "q_a_norm_w", "w_q_b", "kv_a_norm_w", "w_kv_b", "q_norm_w",
                  "k_norm_w", "mla_out_norm_w", "dn_conv_w", "dn_A_log", "dn_dt_bias", "dn_out_norm_w", "w_out",
                  "ffn_norm_w", "w_gate", "w_up", "ffn_conv_w", "ffn_conv_b", "w_down"):
            src = big[kind] if n in big[kind] else sm[kind]
            a = src[n].T if n in _TRANSPOSED else src[n]
            outs.append(a.reshape(out_shapes[n]))
    return tuple(outs)
```

```python
import functools
import math

import jax
import jax.numpy as jnp
from jax import lax
from jax.experimental import pallas as pl
from jax.experimental.pallas import tpu as pltpu

F32 = jnp.float32
BF16 = jnp.bfloat16
_MXU = jnp.bfloat16
_HI = lax.Precision.HIGHEST

D_MODEL = 1024
N_META = 16
PAD = 112
ROW0 = PAD + N_META
MLA_HEADS = 4
QK_NOPE = 128
QK_ROPE = 64
QK_HEAD = QK_NOPE + QK_ROPE
V_HEAD = 128
Q_LORA = 256
KV_LORA = 256
ROPE_THETA = 10000.0
DN_HEADS = 4
DN_DIM = 128
DN_WIDTH = DN_HEADS * DN_DIM
DN_CONV = 4
DN_CHUNK = 64
D_FF = 2816
FFN_CONV = 3
EPS = 1e-6
HP = 256
C_QKV = 0
C_Z = 1536
C_QL = 2048
C_KVL = 2304
C_KPE = 2560
C_AB = 2688
IN_P = 2816
IN_COLS = 2632

ADAM_LR = 0.001
ADAM_B1 = 0.9
ADAM_B2 = 0.999
ADAM_EPS = 1e-08
ADAM_WD = 0.01
ADAM_STEP = 10

N_DEV = 8
TM = 128
LANE = 128
VMEM_LIMIT = 56 * 1024 * 1024
NEG = -1e30


def _dot(a, b, dims, hp=False):
    if hp:
        return lax.dot_general(a.astype(F32), b.astype(F32), (dims, ((), ())),
                               precision=lax.Precision.HIGH if hp == "3x" else _HI, preferred_element_type=F32)
    return lax.dot_general(a.astype(_MXU), b.astype(_MXU), (dims, ((), ())),
                           preferred_element_type=F32)


def _nn(a, b, hp=False):
    return _dot(a, b, ((1,), (0,)), hp)


def _nt(a, b, hp=False):
    return _dot(a, b, ((1,), (1,)), hp)


def _tn(a, b, hp=False):
    return _dot(a, b, ((0,), (0,)), hp)


def _sigmoid(x):
    return 1.0 / (1.0 + jnp.exp(-x))


def _rms_fwd(x, w, n):
    r = lax.rsqrt(jnp.sum(x * x, axis=-1, keepdims=True) * (1.0 / n) + EPS)
    return x * r * w, r


def _rms_bwd(x, w, dy, n):
    r = lax.rsqrt(jnp.sum(x * x, axis=-1, keepdims=True) * (1.0 / n) + EPS)
    xh = x * r
    gy = dy * w
    dx = r * (gy - xh * (jnp.sum(gy * xh, axis=-1, keepdims=True) * (1.0 / n)))
    return dx, dy * xh


def _rowsum(x):
    return jnp.sum(x, axis=0, keepdims=True)


def _row_ids(i, tm):
    return i * tm + lax.broadcasted_iota(jnp.int32, (tm, 1), 0)


def _shift_down(ext, s, tm):
    if s == 0:
        return ext[8:8 + tm]
    return pltpu.roll(ext, s, 0)[8:8 + tm]


def _shift_up(ext, s, tm):
    if s == 0:
        return ext[0:tm]
    return pltpu.roll(ext, tm + 8 - s, 0)[0:tm]


def _conv_fwd(x, halo_prev, w, width):
    tm = x.shape[0]
    ext = jnp.concatenate([halo_prev, x], axis=0)
    y = None
    for j in range(width):
        t = w[j:j + 1, :] * _shift_down(ext, width - 1 - j, tm)
        y = t if y is None else y + t
    return y


def _conv_bwd_x(dy, halo_next, w, width):
    tm = dy.shape[0]
    ext = jnp.concatenate([dy, halo_next], axis=0)
    dx = None
    for j in range(width):
        t = w[j:j + 1, :] * _shift_up(ext, width - 1 - j, tm)
        dx = t if dx is None else dx + t
    return dx


def _conv_bwd_w(dy, x, halo_prev, width):
    tm = dy.shape[0]
    ext = jnp.concatenate([halo_prev, x], axis=0)
    rows = [_rowsum(dy * _shift_down(ext, width - 1 - j, tm)) for j in range(width)]
    rows += [jnp.zeros_like(rows[0])] * (8 - width)
    return jnp.concatenate(rows, axis=0)


def _softplus(x):
    e = jnp.exp(-jnp.abs(x))
    u = 1.0 + e
    l1p = jnp.where(u == 1.0, e, jnp.log(u) * e / jnp.where(u == 1.0, 1.0, u - 1.0))
    return jnp.maximum(x, 0.0) + l1p


def _swap_halves(x):
    lane = lax.broadcasted_iota(jnp.int32, x.shape, 1)
    return jnp.where(lane < 32, pltpu.roll(x, 96, 1), jnp.where(lane < 64, pltpu.roll(x, 32, 1), 0.0))


class _In:
    def __init__(self, arr, width=None, cb=0, kind="cur"):
        self.arr, self.kind = arr, kind
        self.width = arr.shape[1] if width is None else width
        self.cb = cb


def _rows(name, fn, tiled, full, outs, accs=(), tm=TM):
    tp = tiled[0].arr.shape[0]
    nt = tp // tm
    r8 = tm // 8
    n_in = len(tiled) + len(full)
    n_out = len(outs)

    def body(*refs):
        i = pl.program_id(0)
        vals = [r[...] for r in refs[:n_in]]
        o_t, o_a = fn(i, *vals)
        for r, v in zip(refs[n_in:n_in + n_out], o_t):
            r[...] = v.astype(r.dtype)
        for r, v in zip(refs[n_in + n_out:], o_a):
            @pl.when(i == 0)
            def _():
                r[...] = v

            @pl.when(i > 0)
            def _():
                r[...] += v

    def spec(t):
        if t.kind == "cur":
            return pl.BlockSpec((tm, t.width), lambda i, cb=t.cb: (i, cb))
        if t.kind == "prev":
            return pl.BlockSpec((8, t.width), lambda i, cb=t.cb: (jnp.maximum(i * r8 - 1, 0), cb))
        return pl.BlockSpec((8, t.width), lambda i, cb=t.cb: (jnp.minimum((i + 1) * r8, tp // 8 - 1), cb))

    in_specs = [spec(t) for t in tiled]
    in_specs += [pl.BlockSpec(a.shape, lambda i, nd=a.ndim: (0,) * nd) for a in full]
    out_specs = [pl.BlockSpec((tm, w), lambda i: (i, 0)) for w, _ in outs]
    out_specs += [pl.BlockSpec((r, w), lambda i: (0, 0)) for r, w in accs]
    out_shape = [jax.ShapeDtypeStruct((tp, w), dt) for w, dt in outs]
    out_shape += [jax.ShapeDtypeStruct((r, w), F32) for r, w in accs]
    res = pl.pallas_call(
        body, name=name, grid=(nt,), in_specs=in_specs, out_specs=out_specs, out_shape=out_shape,
        compiler_params=pltpu.CompilerParams(dimension_semantics=("arbitrary",), vmem_limit_bytes=VMEM_LIMIT),
    )(*[t.arr for t in tiled], *full)
    return res


def _pick(n, cap, mult):
    best = None
    for d in range(mult, min(n, cap) + 1, mult):
        if n % d == 0:
            best = d
    assert best is not None, (n, cap, mult)
    return best


_ANY_SPEC = pl.BlockSpec(memory_space=pl.ANY)


def _mm(name, a, b, mode, out_dtype=F32, resid=None, after=None):
    if mode == "tn":
        m, k = a.shape
        n = b.shape[1]
        tk = _pick(k, 512, 128)
        tn = _pick(n, 1408, 128)

        def body_tn(a_ref, b_ref, o_ref):
            o_ref[...] = _tn(a_ref[...], b_ref[...]).astype(o_ref.dtype)

        return pl.pallas_call(
            body_tn, name=name, grid=(n // tn, k // tk),
            in_specs=[pl.BlockSpec((m, tk), lambda j, p: (0, p)),
                      pl.BlockSpec((m, tn), lambda j, p: (0, j))],
            out_specs=pl.BlockSpec((tk, tn), lambda j, p: (p, j)),
            out_shape=jax.ShapeDtypeStruct((k, n), out_dtype),
            compiler_params=pltpu.CompilerParams(
                dimension_semantics=("parallel", "parallel"), vmem_limit_bytes=VMEM_LIMIT),
        )(a, b)

    m, k = a.shape
    n = b.shape[1] if mode == "nn" else b.shape[0]
    tn = _pick(n, 1408, 128)
    tm = _pick(m, 1152, 16)
    dotf = _nn if mode == "nn" else _nt

    def body(*refs):
        a_ref, b_ref, o_ref = refs[0], refs[1], refs[-1]
        acc = dotf(a_ref[...], b_ref[...])
        if resid is not None:
            acc = refs[2][...] + acc
        o_ref[...] = acc.astype(o_ref.dtype)

    b_spec = (pl.BlockSpec((k, tn), lambda j, i: (0, j)) if mode == "nn"
              else pl.BlockSpec((tn, k), lambda j, i: (j, 0)))
    in_specs = [pl.BlockSpec((tm, k), lambda j, i: (i, 0)), b_spec]
    args = [a, b]
    if resid is not None:
        in_specs.append(pl.BlockSpec((tm, tn), lambda j, i: (i, j)))
        args.append(resid)
    if after is not None:
        in_specs.append(_ANY_SPEC)
        args.append(after)
    return pl.pallas_call(
        body, name=name, grid=(n // tn, m // tm), in_specs=in_specs,
        out_specs=pl.BlockSpec((tm, tn), lambda j, i: (i, j)),
        out_shape=jax.ShapeDtypeStruct((m, n), out_dtype),
        compiler_params=pltpu.CompilerParams(
            dimension_semantics=("parallel", "parallel"), vmem_limit_bytes=VMEM_LIMIT),
    )(*args)


def _norm_mm(name, x, norm_w, b, mode="nt", x_cb=0, after=None):
    m = x.shape[0]
    k = norm_w.shape[1]
    n = b.shape[0] if mode == "nt" else b.shape[1]
    tn = _pick(n, 1408, 128)
    tm = _pick(m, 1152, 16)
    dotf = _nt if mode == "nt" else _nn
    extra = [] if after is None else [after]

    def body(x_ref, w_ref, b_ref, *rest):
        o_ref, u_ref = rest[-2:]

        @pl.when(pl.program_id(1) == 0)
        def _():
            u_ref[...] = _rms_fwd(x_ref[...], w_ref[...], k)[0].astype(u_ref.dtype)

        o_ref[...] = dotf(u_ref[...], b_ref[...])

    b_spec = (pl.BlockSpec((tn, k), lambda i, j: (j, 0)) if mode == "nt"
              else pl.BlockSpec((k, tn), lambda i, j: (0, j)))
    return pl.pallas_call(
        body, name=name, grid=(m // tm, n // tn),
        in_specs=[pl.BlockSpec((tm, k), lambda i, j: (i, x_cb)), pl.BlockSpec((1, k), lambda i, j: (0, 0)),
                  b_spec] + [_ANY_SPEC] * len(extra),
        out_specs=[pl.BlockSpec((tm, tn), lambda i, j: (i, j)), pl.BlockSpec((tm, k), lambda i, j: (i, 0))],
        out_shape=[jax.ShapeDtypeStruct((m, n), F32), jax.ShapeDtypeStruct((m, k), _MXU)],
        compiler_params=pltpu.CompilerParams(
            dimension_semantics=("arbitrary", "arbitrary"), vmem_limit_bytes=VMEM_LIMIT),
    )(x, norm_w, b, *extra)


def _mm_rows(name, a, b, mode, fn, tiled, full, outs, accs=()):
    m = a.shape[0]
    tm = _pick(m, 576, 16)
    dotf = _nn if mode == "nn" else _nt
    n_in = len(tiled) + len(full)
    n_out = len(outs)

    def body(*refs):
        i = pl.program_id(0)
        vals = [r[...] for r in refs[2:2 + n_in]]
        o_t, o_a = fn(i, dotf(refs[0][...], refs[1][...]), *vals)
        for r, v in zip(refs[2 + n_in:2 + n_in + n_out], o_t):
            r[...] = v.astype(r.dtype)
        for r, v in zip(refs[2 + n_in + n_out:], o_a):
            @pl.when(i == 0)
            def _():
                r[...] = v

            @pl.when(i > 0)
            def _():
                r[...] += v

    whole = lambda x: pl.BlockSpec(x.shape, lambda i, nd=x.ndim: (0,) * nd)
    in_specs = [pl.BlockSpec((tm, a.shape[1]), lambda i: (i, 0)), whole(b)]
    in_specs += [pl.BlockSpec((tm, t.width), lambda i, cb=t.cb: (i, cb)) for t in tiled]
    in_specs += [whole(x) for x in full]
    out_specs = [pl.BlockSpec((tm, w), lambda i: (i, 0)) for w, _ in outs]
    out_specs += [pl.BlockSpec((r, w), lambda i: (0, 0)) for r, w in accs]
    out_shape = [jax.ShapeDtypeStruct((m, w), dt) for w, dt in outs]
    out_shape += [jax.ShapeDtypeStruct((r, w), F32) for r, w in accs]
    return pl.pallas_call(
        body, name=name, grid=(m // tm,), in_specs=in_specs, out_specs=out_specs, out_shape=out_shape,
        compiler_params=pltpu.CompilerParams(dimension_semantics=("arbitrary",), vmem_limit_bytes=VMEM_LIMIT),
    )(a, b, *[t.arr for t in tiled], *full)


ATTN_Q_TILES = 4


def _attn_probs(q, k, row0):
    tq, tp = q.shape[0], k.shape[0]
    s = _nt(q, k) * (1.0 / math.sqrt(QK_HEAD))
    row = row0 + lax.broadcasted_iota(jnp.int32, (tq, tp), 0)
    col = lax.broadcasted_iota(jnp.int32, (tq, tp), 1)
    ok = (col <= row) & (col >= PAD)
    s = jnp.where(ok, s, NEG)
    m = jnp.max(s, axis=-1, keepdims=True)
    e = jnp.exp(s - m)
    e = jnp.where(ok, e, 0.0)
    l = jnp.sum(e, axis=-1, keepdims=True)
    return e / jnp.maximum(l, 1e-30)


def _attn_fwd(q, k, v):
    tp = q.shape[0]
    tq = tp // ATTN_Q_TILES

    def body(q_ref, k_ref, v_ref, o_ref):
        for i in range(ATTN_Q_TILES):
            rows = slice(i * tq, (i + 1) * tq)
            keys = slice(0, (i + 1) * tq)
            p = _attn_probs(q_ref[rows, :], k_ref[keys, :], i * tq)
            o_ref[rows, :] = _nn(p, v_ref[keys, :])

    return pl.pallas_call(
        body, name="attn_fwd", grid=(MLA_HEADS,),
        in_specs=[pl.BlockSpec((tp, HP), lambda h: (0, h)),
                  pl.BlockSpec((tp, HP), lambda h: (0, h)),
                  pl.BlockSpec((tp, V_HEAD), lambda h: (0, h))],
        out_specs=pl.BlockSpec((tp, V_HEAD), lambda h: (0, h)),
        out_shape=jax.ShapeDtypeStruct((tp, MLA_HEADS * V_HEAD), F32),
        compiler_params=pltpu.CompilerParams(dimension_semantics=("parallel",), vmem_limit_bytes=VMEM_LIMIT),
    )(q, k, v)


def _attn_bwd(q, k, v, do):
    tp = q.shape[0]
    tq = tp // ATTN_Q_TILES

    def body(q_ref, k_ref, v_ref, do_ref, dq_ref, dk_ref, dv_ref):
        for i in reversed(range(ATTN_Q_TILES)):
            rows = slice(i * tq, (i + 1) * tq)
            keys = slice(0, (i + 1) * tq)
            qb = q_ref[rows, :]
            kk = k_ref[keys, :]
            dob = do_ref[rows, :]
            p = _attn_probs(qb, kk, i * tq)
            dp = _nt(dob, v_ref[keys, :])
            delta = jnp.sum(p * dp, axis=-1, keepdims=True)
            ds = p * (dp - delta) * (1.0 / math.sqrt(QK_HEAD))
            dq_ref[rows, :] = _nn(ds, kk)
            if i == ATTN_Q_TILES - 1:
                dk_ref[...] = _tn(ds, qb)
                dv_ref[...] = _tn(p, dob)
            else:
                dk_ref[keys, :] += _tn(ds, qb)
                dv_ref[keys, :] += _tn(p, dob)

    full = lambda w: pl.BlockSpec((tp, w), lambda h: (0, h))
    return pl.pallas_call(
        body, name="attn_bwd", grid=(MLA_HEADS,),
        in_specs=[full(HP), full(HP), full(V_HEAD), full(V_HEAD)],
        out_specs=[full(HP), full(HP), full(V_HEAD)],
        out_shape=[jax.ShapeDtypeStruct((tp, MLA_HEADS * HP), F32),
                   jax.ShapeDtypeStruct((tp, MLA_HEADS * HP), F32),
                   jax.ShapeDtypeStruct((tp, MLA_HEADS * V_HEAD), F32)],
        compiler_params=pltpu.CompilerParams(dimension_semantics=("parallel",), vmem_limit_bytes=VMEM_LIMIT),
    )(q, k, v, do)


def _gdn_consts():
    c = DN_CHUNK
    r = lax.broadcasted_iota(jnp.int32, (c, c), 0)
    cc = lax.broadcasted_iota(jnp.int32, (c, c), 1)
    incl = r >= cc
    strict = r > cc
    return incl, strict


def _cumsum_rows(x, reverse=False):
    c = x.shape[0]
    row = lax.broadcasted_iota(jnp.int32, x.shape, 0)
    s = 1
    while s < c:
        if reverse:
            x = x + jnp.where(row < c - s, pltpu.roll(x, c - s, 0), 0.0)
        else:
            x = x + jnp.where(row >= s, pltpu.roll(x, s, 0), 0.0)
        s *= 2
    return x


def _each(fn, *lists):
    return [fn(*a) for a in zip(*lists)]


def _interleave(chains):
    chains = list(chains)
    while chains:
        for ch in list(chains):
            try:
                next(ch)
            except StopIteration:
                chains.remove(ch)


def _gdn_chunk_common(q_ref, k_ref, v_ref, g_ref, b_ref):
    c = DN_CHUNK
    incl, strict = _gdn_consts()
    sls = [slice(DN_DIM * h, DN_DIM * (h + 1)) for h in range(DN_HEADS)]
    q = [q_ref[:, sl] * (1.0 / math.sqrt(DN_DIM)) for sl in sls]
    k = [k_ref[:, sl] for sl in sls]
    v = [v_ref[:, sl] for sl in sls]
    g = [g_ref[:, sl] for sl in sls]
    beta = [b_ref[:, sl] for sl in sls]
    gc = [_cumsum_rows(x) for x in g]
    grow = [x.T[:c, :] for x in gc]
    kb = _each(jnp.multiply, k, beta)
    kk = _each(_nt, kb, k)
    qk = _each(_nt, q, k)
    gam = [jnp.exp(x) for x in gc]
    g_last = [_rowsum(x) for x in g]
    dm = [jnp.exp(jnp.where(incl, x[:, :c] - y, NEG)) for x, y in zip(gc, grow)]
    vb = _each(jnp.multiply, v, beta)
    kbg = _each(jnp.multiply, kb, gam)
    ek = [jnp.exp(x - y) for x, y in zip(g_last, gc)]
    kd = _each(jnp.multiply, k, ek)
    return dict(q=q, k=k, v=v, beta=beta, gc=gc, gam=gam, g_last=g_last, dm=dm, kb=kb, vb=vb,
                kbg=kbg, kk=kk, ek=ek, kd=kd, qk=qk, incl=incl, strict=strict, sls=sls)


def _gdn_fwd(q, k, v, g, beta):
    tp = q.shape[0]
    c = DN_CHUNK
    nch = tp // c

    def body(q_ref, k_ref, v_ref, g_ref, b_ref, o_ref, s_ref, t_ref, s_scr):
        @pl.when(pl.program_id(0) == 0)
        def _():
            s_scr[...] = jnp.zeros_like(s_scr)

        eye = (lax.broadcasted_iota(jnp.int32, (c, c), 0) == lax.broadcasted_iota(jnp.int32, (c, c), 1)).astype(F32)
        x = _gdn_chunk_common(q_ref, k_ref, v_ref, g_ref, b_ref)
        heads = range(DN_HEADS)
        s = [s_scr[h] for h in heads]
        bp = [-jnp.where(x["strict"], kk * dm, 0.0) for kk, dm in zip(x["kk"], x["dm"])]
        t = [eye + b for b in bp]
        for _ in range(5):
            bp = [_nn(b, b, hp="3x") for b in bp]
            t = [tt + _nn(tt, b, hp="3x") for tt, b in zip(t, bp)]
        u = _each(_nn, t, x["vb"])
        w = _each(_nn, t, x["kbg"])
        v_new = [uu - _nn(ww, ss) for uu, ww, ss in zip(u, w, s)]
        o = [_nn(q * gam, ss) + _nn(qk * dm, vn)
             for q, gam, ss, qk, dm, vn in zip(x["q"], x["gam"], s, x["qk"], x["dm"], v_new)]
        s_new = [ss * jnp.exp(gl) + _tn(kd, vn) for ss, gl, kd, vn in zip(s, x["g_last"], x["kd"], v_new)]
        for h in heads:
            s_ref[h, 0] = s[h]
            t_ref[h, 0] = t[h]
            o_ref[:, x["sls"][h]] = o[h]
            s_scr[h] = s_new[h]

    rb = lambda n: (n, 0)
    return pl.pallas_call(
        body, name="gdn_fwd", grid=(nch,),
        in_specs=[pl.BlockSpec((c, DN_WIDTH), rb)] * 5,
        out_specs=[pl.BlockSpec((c, DN_WIDTH), rb),
                   pl.BlockSpec((DN_HEADS, 1, DN_DIM, DN_DIM), lambda n: (0, n, 0, 0)),
                   pl.BlockSpec((DN_HEADS, 1, c, c), lambda n: (0, n, 0, 0))],
        out_shape=[jax.ShapeDtypeStruct((tp, DN_WIDTH), F32),
                   jax.ShapeDtypeStruct((DN_HEADS, nch, DN_DIM, DN_DIM), F32),
                   jax.ShapeDtypeStruct((DN_HEADS, nch, c, c), F32)],
        scratch_shapes=[pltpu.VMEM((DN_HEADS, DN_DIM, DN_DIM), F32)],
        compiler_params=pltpu.CompilerParams(dimension_semantics=("arbitrary",), vmem_limit_bytes=VMEM_LIMIT),
    )(q, k, v, g, beta)


def _gdn_bwd(q, k, v, g, beta, s_all, t_all, do):
    tp = q.shape[0]
    c = DN_CHUNK
    nch = tp // c

    def body(q_ref, k_ref, v_ref, g_ref, b_ref, s_ref, t_ref, do_ref,
             dq_ref, dk_ref, dv_ref, dg_ref, db_ref, ds_scr):
        @pl.when(pl.program_id(0) == 0)
        def _():
            ds_scr[...] = jnp.zeros_like(ds_scr)

        xs = _gdn_chunk_common(q_ref, k_ref, v_ref, g_ref, b_ref)

        def chain(h):
            x = {key: (val[h] if isinstance(val, list) else val) for key, val in xs.items()}
            sl = x["sls"]
            qs, kx, vx, beta_, gam, dm = x["q"], x["k"], x["v"], x["beta"], x["gam"], x["dm"]
            kb, vb, kbg, kd, ek = x["kb"], x["vb"], x["kbg"], x["kd"], x["ek"]
            t = t_ref[h, 0]
            s = s_ref[h, 0]
            dsn = ds_scr[h]
            dob = do_ref[:, sl]
            eg_last = jnp.exp(x["g_last"])
            u = _nn(t, vb)
            w = _nn(t, kbg)
            mqk = x["qk"] * dm
            qd = qs * gam
            dqd = _nt(dob, s)
            dkd_pre = _nn(kd, dsn)
            yield
            v_new = u - _nn(w, s)
            dv_new = _tn(mqk, dob) + dkd_pre
            dq = dqd * gam
            dgam = jnp.sum(dqd * qs, axis=1, keepdims=True)
            yield
            ds_new = _tn(qd, dob) + eg_last * dsn - _tn(w, dv_new)
            dmm = jnp.where(x["incl"], _nt(dob, v_new), 0.0)
            dkd = _nt(v_new, dsn)
            dw = -_nt(dv_new, s)
            dvb = _tn(t, dv_new)
            dt = _nt(dv_new, vb)
            yield
            dqk = dmm * dm
            e_mat = dmm * mqk
            dq = dq + _nn(dqk, kx)
            dk = _tn(dqk, qs) + dkd * ek
            e1 = jnp.sum(dkd * kd, axis=1, keepdims=True)
            dgc = -e1
            dg_last = jnp.sum(e1) + eg_last * jnp.sum(s * dsn)
            dt = dt + _nt(dw, kbg)
            dkbg = _tn(t, dw)
            yield
            tdt = _tn(t, dt, hp="3x")
            yield
            da = jnp.where(x["strict"], -_nt(tdt, t, hp="3x"), 0.0)
            yield
            dkk = da * dm
            e_mat = e_mat + da * x["kk"] * dm
            dkb = _nn(dkk, kx) + dkbg * gam
            dk = dk + _tn(dkk, kb)
            dgam = dgam + jnp.sum(dkbg * kb, axis=1, keepdims=True)
            yield
            dk = dk + dkb * beta_
            dbeta = jnp.sum(dkb * kx, axis=1, keepdims=True) + jnp.sum(dvb * vx, axis=1, keepdims=True)
            dv = dvb * beta_
            dgc = dgc + jnp.sum(e_mat, axis=1, keepdims=True) + dgam * gam
            dgc = dgc - jnp.sum(e_mat.T, axis=1, keepdims=True)
            yield
            dg = _cumsum_rows(dgc, reverse=True) + dg_last
            yield
            ds_scr[h] = ds_new
            dq_ref[:, sl] = dq * (1.0 / math.sqrt(DN_DIM))
            dk_ref[:, sl] = dk
            dv_ref[:, sl] = dv
            dg_ref[:, sl] = dg
            db_ref[:, sl] = jnp.broadcast_to(dbeta, (c, LANE))

        _interleave([chain(h) for h in range(DN_HEADS)])

    rb = lambda n: (nch - 1 - n, 0)
    hs = lambda n: (0, nch - 1 - n, 0, 0)
    return pl.pallas_call(
        body, name="gdn_bwd", grid=(nch,),
        in_specs=[pl.BlockSpec((c, DN_WIDTH), rb)] * 5
        + [pl.BlockSpec((DN_HEADS, 1, DN_DIM, DN_DIM), hs), pl.BlockSpec((DN_HEADS, 1, c, c), hs),
           pl.BlockSpec((c, DN_WIDTH), rb)],
        out_specs=[pl.BlockSpec((c, DN_WIDTH), rb)] * 5,
        out_shape=[jax.ShapeDtypeStruct((tp, DN_WIDTH), F32)] * 5,
        scratch_shapes=[pltpu.VMEM((DN_HEADS, DN_DIM, DN_DIM), F32)],
        compiler_params=pltpu.CompilerParams(dimension_semantics=("arbitrary",), vmem_limit_bytes=VMEM_LIMIT),
    )(q, k, v, g, beta, s_all, t_all, do)


def _silu_parts(x):
    s = _sigmoid(x)
    return x * s, s * (1.0 + x * (1.0 - s))


def _f_rms_cast(i, x, w):
    y, _ = _rms_fwd(x, w, x.shape[1])
    return (y,), ()


def _f_rms_bwd_add(i, x, dy, dres, w, *, mask_pad):
    dx, dwr = _rms_bwd(x, w, dy, x.shape[1])
    out = dres + dx
    if mask_pad:
        out = jnp.where(_row_ids(i, x.shape[0]) >= PAD, out, 0.0)
    return (out,), (_rowsum(dwr),)


def _f_lat_norm(i, ql, kvl, qw, kvw):
    return (_rms_fwd(ql, qw, Q_LORA)[0], _rms_fwd(kvl, kvw, KV_LORA)[0]), ()


def _f_lat_norm_bwd(i, ql, kvl, dqn, dkvn, qw, kvw):
    dq, dqw = _rms_bwd(ql, qw, dqn, Q_LORA)
    dk, dkw = _rms_bwd(kvl, kvw, dkvn, KV_LORA)
    return (dq, dk), (_rowsum(dqw), _rowsum(dkw))


def _rope(x, cos, sin_s):
    return x * cos + _swap_halves(x) * sin_s


def _rope_t(dy, cos, sin_s):
    return dy * cos + _swap_halves(dy * sin_s)


def _f_mla_qk(i, qf, kvf, kpe, cos, sin_s, qw, kw):
    qs, ks, vs = [], [], []
    for h in range(MLA_HEADS):
        qn, _ = _rms_fwd(qf[:, HP * h:HP * (h + 1)], qw, QK_HEAD)
        qs += [qn[:, :QK_NOPE], _rope(qn[:, QK_NOPE:], cos, sin_s)]
        kh = jnp.concatenate([kvf[:, HP * h:HP * h + QK_NOPE], kpe], axis=1)
        kn, _ = _rms_fwd(kh, kw, QK_HEAD)
        ks += [kn[:, :QK_NOPE], _rope(kn[:, QK_NOPE:], cos, sin_s)]
        vs.append(kvf[:, HP * h + QK_NOPE:HP * (h + 1)])
    return (jnp.concatenate(qs, axis=1), jnp.concatenate(ks, axis=1), jnp.concatenate(vs, axis=1)), ()


def _f_mla_qk_bwd(i, qf, kvf, kpe, cos, sin_s, dq, dk, dv, qw, kw):
    dqf, dkvf = [], []
    dkpe = None
    dqw = None
    dkw = None
    for h in range(MLA_HEADS):
        dqh = dq[:, HP * h:HP * (h + 1)]
        dqn = jnp.concatenate([dqh[:, :QK_NOPE], _rope_t(dqh[:, QK_NOPE:], cos, sin_s)], axis=1)
        dx, dwr = _rms_bwd(qf[:, HP * h:HP * (h + 1)], qw, dqn, QK_HEAD)
        dqf.append(dx)
        dqw = _rowsum(dwr) if dqw is None else dqw + _rowsum(dwr)
        dkh = dk[:, HP * h:HP * (h + 1)]
        dkn = jnp.concatenate([dkh[:, :QK_NOPE], _rope_t(dkh[:, QK_NOPE:], cos, sin_s)], axis=1)
        kh = jnp.concatenate([kvf[:, HP * h:HP * h + QK_NOPE], kpe], axis=1)
        dx, dwr = _rms_bwd(kh, kw, dkn, QK_HEAD)
        dkvf += [dx[:, :QK_NOPE], dv[:, V_HEAD * h:V_HEAD * (h + 1)]]
        dkpe = dx[:, QK_NOPE:] if dkpe is None else dkpe + dx[:, QK_NOPE:]
        dkw = _rowsum(dwr) if dkw is None else dkw + _rowsum(dwr)
    return (jnp.concatenate(dqf, axis=1), jnp.concatenate(dkvf, axis=1), dkpe), (dqw, dkw)


def _gdn_act(i, x, halo, w8):
    tm = x.shape[0]
    halo = jnp.where(i > 0, halo, 0.0)
    c = _conv_fwd(x, halo, w8, DN_CONV)
    act, dact = _silu_parts(c)
    return act, dact


def _f_gdn_prep(i, x, halo, ab, w8, alog, dtb, sel):
    tm = x.shape[0]
    act, _ = _gdn_act(i, x, halo, w8)
    outs = []
    for part in range(2):
        for h in range(DN_HEADS):
            t = act[:, DN_WIDTH * part + DN_DIM * h:DN_WIDTH * part + DN_DIM * (h + 1)]
            outs.append(t * lax.rsqrt(jnp.sum(t * t, axis=-1, keepdims=True) + EPS))
    q = jnp.concatenate(outs[:DN_HEADS], axis=1)
    k = jnp.concatenate(outs[DN_HEADS:], axis=1)
    v = act[:, 2 * DN_WIDTH:]
    abb = _nn(ab, sel, hp=True)
    valid = _row_ids(i, tm) >= PAD
    g = jnp.where(valid, -jnp.exp(alog) * _softplus(abb[:, :DN_WIDTH] + dtb), 0.0)
    beta = jnp.where(valid, _sigmoid(abb[:, DN_WIDTH:]), 0.0)
    return (q, k, v, g, beta), ()


def _f_gdn_prep_bwd(i, x, x_prev, x_next, ab, dq, dq_next, dk, dk_next, dv, dv_next, dg, dbeta,
                    w8, alog, dtb, sel, selpick, *, nt):
    tm = x.shape[0]
    x_prev = jnp.where(i > 0, x_prev, 0.0)
    more = i < nt - 1
    ext = lambda t, t_next: jnp.concatenate([t, jnp.where(more, t_next, 0.0)], axis=0)
    c = _conv_fwd(jnp.concatenate([x, x_next], axis=0), x_prev, w8, DN_CONV)
    act, dact = _silu_parts(c)
    douts = []
    for part, dd in enumerate((ext(dq, dq_next), ext(dk, dk_next))):
        for h in range(DN_HEADS):
            t = act[:, DN_WIDTH * part + DN_DIM * h:DN_WIDTH * part + DN_DIM * (h + 1)]
            r = lax.rsqrt(jnp.sum(t * t, axis=-1, keepdims=True) + EPS)
            y = t * r
            dy = dd[:, DN_DIM * h:DN_DIM * (h + 1)]
            douts.append(r * (dy - y * jnp.sum(dy * y, axis=-1, keepdims=True)))
    douts.append(ext(dv, dv_next))
    dc = jnp.concatenate(douts, axis=1) * dact
    dqkv = _conv_bwd_x(dc[:tm], dc[tm:], w8, DN_CONV)
    dconv_w = _conv_bwd_w(dc[:tm], x, x_prev, DN_CONV)
    abb = _nn(ab, sel, hp=True)
    valid = _row_ids(i, tm) >= PAD
    pre = abb[:, :DN_WIDTH] + dtb
    ea = jnp.exp(alog)
    g = -ea * _softplus(pre)
    dg = jnp.where(valid, dg, 0.0)
    dbeta = jnp.where(valid, dbeta, 0.0)
    da = dg * (-ea) * _sigmoid(pre)
    beta = _sigmoid(abb[:, DN_WIDTH:])
    db = dbeta * beta * (1.0 - beta)
    dab = _nn(jnp.concatenate([da, db], axis=1), selpick, hp=True)
    return (dqkv, dab), (dconv_w, _rowsum(dg * g), _rowsum(da))


def _f_conv_bwd(i, dy, dy_next, x, x_prev, w8, *, width, nt):
    dy_next = jnp.where(i < nt - 1, dy_next, 0.0)
    x_prev = jnp.where(i > 0, x_prev, 0.0)
    return (_conv_bwd_x(dy, dy_next, w8, width),), (_conv_bwd_w(dy, x, x_prev, width),)


def _f_mix(i, o_mla, o_dn, z, w_mla, w_dn):
    tm = o_mla.shape[0]
    valid = _row_ids(i, tm) >= PAD
    outs = []
    for h in range(MLA_HEADS):
        y, _ = _rms_fwd(o_mla[:, V_HEAD * h:V_HEAD * (h + 1)], w_mla, V_HEAD)
        outs.append(jnp.where(valid, y, 0.0))
    for h in range(DN_HEADS):
        y, _ = _rms_fwd(o_dn[:, DN_DIM * h:DN_DIM * (h + 1)], w_dn, DN_DIM)
        outs.append(y * _silu_parts(z[:, DN_DIM * h:DN_DIM * (h + 1)])[0])
    return (jnp.concatenate(outs, axis=1),), ()


def _f_mix_bwd(i, o_mla, o_dn, z, dy_mla, dy_dn, w_mla, w_dn):
    tm = o_mla.shape[0]
    valid = _row_ids(i, tm) >= PAD
    d_mla, d_dn, d_z = [], [], []
    dw_mla = None
    dw_dn = None
    for h in range(MLA_HEADS):
        sl = slice(V_HEAD * h, V_HEAD * (h + 1))
        dx, dwr = _rms_bwd(o_mla[:, sl], w_mla, jnp.where(valid, dy_mla[:, sl], 0.0), V_HEAD)
        d_mla.append(dx)
        dw_mla = _rowsum(dwr) if dw_mla is None else dw_mla + _rowsum(dwr)
    for h in range(DN_HEADS):
        sl = slice(DN_DIM * h, DN_DIM * (h + 1))
        y, _ = _rms_fwd(o_dn[:, sl], w_dn, DN_DIM)
        sz, dsz = _silu_parts(z[:, sl])
        d_z.append(dy_dn[:, sl] * y * dsz)
        dx, dwr = _rms_bwd(o_dn[:, sl], w_dn, dy_dn[:, sl] * sz, DN_DIM)
        d_dn.append(dx)
        dw_dn = _rowsum(dwr) if dw_dn is None else dw_dn + _rowsum(dwr)
    return ((jnp.concatenate(d_mla, axis=1), jnp.concatenate(d_dn, axis=1), jnp.concatenate(d_z, axis=1)),
            (dw_mla, dw_dn))


def _f_ffn_act(i, gate_pre, halo, up, w8, b):
    halo = jnp.where(i > 0, halo, 0.0)
    gate = _conv_fwd(gate_pre, halo, w8, FFN_CONV) + b
    return (_silu_parts(gate)[0] * up,), ()


def _f_ffn_act_bwd(i, gp, gp_prev, gp_next, up, up_next, dact, dact_next, w8, b, *, nt):
    tm = gp.shape[0]
    gp_prev = jnp.where(i > 0, gp_prev, 0.0)
    dact_next = jnp.where(i < nt - 1, dact_next, 0.0)
    cat = lambda t, t_next: jnp.concatenate([t, t_next], axis=0)
    gate = _conv_fwd(cat(gp, gp_next), gp_prev, w8, FFN_CONV) + b
    sg, dsg = _silu_parts(gate)
    dact_e = cat(dact, dact_next)
    dgate = dact_e * cat(up, up_next) * dsg
    dgate_pre = _conv_bwd_x(dgate[:tm], dgate[tm:], w8, FFN_CONV)
    dup = dact * sg[:tm]
    return (dgate_pre, dup), (_conv_bwd_w(dgate[:tm], gp, gp_prev, FFN_CONV), _rowsum(dgate[:tm]))


def _f_loss(i, h3, tgt):
    tm = h3.shape[0]
    diff = jnp.where(_row_ids(i, tm) >= ROW0, h3 - tgt, 0.0)
    part = 0.5 * jnp.sum(diff * diff) * (1.0 / D_MODEL)
    return (diff * (1.0 / D_MODEL),), (jnp.full((1, LANE), part, F32),)


def _after(fn):
    return lambda i, *a: fn(i, *a[:-1])


def _local_step(h0, tgt, w, token, late_weights, grads_ready):
    tp = h0.shape[0]
    nt = tp // TM
    bf = (D_MODEL, _MXU)
    proj, u = _norm_mm("in_proj", h0, w["attn_norm_w"], w["w_in"], after=token)
    p_qkv = lambda kind="cur": _In(proj, 3 * DN_WIDTH, 0, kind)
    p_z = _In(proj, DN_WIDTH, C_Z // DN_WIDTH)
    p_ql = _In(proj, Q_LORA, C_QL // Q_LORA)
    p_kvl = _In(proj, KV_LORA, C_KVL // KV_LORA)
    p_kpe = _In(proj, LANE, C_KPE // LANE)
    p_ab = _In(proj, LANE, C_AB // LANE)
    cos, sin_s = _In(w["cos"]), _In(w["sin_s"])

    qf, qn = _norm_mm("mla_q_b", proj, w["q_a_norm_w"], w["w_q_b"], "nt", C_QL // Q_LORA)
    kvf, kvn = _norm_mm("mla_kv_b", proj, w["kv_a_norm_w"], w["w_kv_b"], "nn", C_KVL // KV_LORA)
    qk_w = [w["q_norm_w"], w["k_norm_w"]]
    q, k, v = _rows("mla_qk", _f_mla_qk, [_In(qf), _In(kvf), p_kpe, cos, sin_s], qk_w,
                    [(MLA_HEADS * HP, _MXU), (MLA_HEADS * HP, _MXU), (MLA_HEADS * V_HEAD, _MXU)])
    o_mla = _attn_fwd(q, k, v)

    dn_w = [w["dn_conv_w"], w["alog_b"], w["dtb_b"], w["sel"]]
    gq, gk, gv, gg, gb = _rows("gdn_prep", _f_gdn_prep, [p_qkv(), p_qkv("prev"), p_ab], dn_w,
                               [(DN_WIDTH, F32)] * 5)
    o_dn, s_all, t_all = _gdn_fwd(gq, gk, gv, gg, gb)

    out_w = [w["mla_out_norm_w"], w["dn_out_norm_w"]]
    mixed, = _rows("mix", _f_mix, [_In(o_mla), _In(o_dn), p_z], out_w, [bf])
    w = dict(w, **late_weights(mixed))
    h2 = _mm("out_proj", mixed, w["w_out"], "nn", resid=h0)

    gate_pre, hn = _norm_mm("ffn_gate", h2, w["ffn_norm_w"], w["w_gate"])
    up = _mm("ffn_up", hn, w["w_up"], "nt")
    ffn_w = [w["ffn_conv_w"], w["ffn_conv_b"]]
    act, = _rows("ffn_act", _f_ffn_act, [_In(gate_pre), _In(gate_pre, kind="prev"), _In(up)], ffn_w,
                 [(D_FF, _MXU)])
    dh3, loss = _mm_rows("ffn_down_loss", act, w["w_down"], "nn", lambda i, y, r, t: _f_loss(i, r + y, t),
                         [_In(h2), _In(tgt)], [], [(D_MODEL, F32)], [(1, LANE)])

    g = {}
    dact = _mm("ffn_down_dx", dh3, w["w_down"], "nt")
    g["w_down"] = _mm("ffn_down_dw", act, dh3, "tn", out_dtype=_MXU)
    dgate_pre, dup, g["ffn_conv_w"], g["ffn_conv_b"] = _rows(
        "ffn_act_bwd", functools.partial(_f_ffn_act_bwd, nt=nt),
        [_In(gate_pre), _In(gate_pre, kind="prev"), _In(gate_pre, kind="next"), _In(up), _In(up, kind="next"),
         _In(dact), _In(dact, kind="next")], ffn_w,
        [(D_FF, _MXU), (D_FF, _MXU)], [(8, D_FF), (1, D_FF)])
    g["w_gate"] = _mm("ffn_gate_dw", dgate_pre, hn, "tn", out_dtype=_MXU)
    g["w_up"] = _mm("ffn_up_dw", dup, hn, "tn", out_dtype=_MXU)
    tok = grads_ready(g, ("w_down", "w_gate", "w_up"))
    dhn = _mm("ffn_gate_dx", dgate_pre, w["w_gate"], "nn", after=tok)
    dh2, g["ffn_norm_w"] = _mm_rows(
        "ffn_up_dx_rms", dup, w["w_up"], "nn",
        lambda i, y, d1, x, dres, nw: _f_rms_bwd_add(i, x, d1 + y, dres, nw, mask_pad=True),
        [_In(dhn), _In(h2), _In(dh3)], [w["ffn_norm_w"]], [(D_MODEL, F32)], [(1, D_MODEL)])

    dmixed = _mm("out_proj_dx", dh2, w["w_out"], "nt")
    g["w_out"] = _mm("out_proj_dw", mixed, dh2, "tn", out_dtype=_MXU)
    half = MLA_HEADS * V_HEAD
    do_mla, do_dn, dz, g["mla_out_norm_w"], g["dn_out_norm_w"] = _rows(
        "mix_bwd", _f_mix_bwd, [_In(o_mla), _In(o_dn), p_z, _In(dmixed, half, 0), _In(dmixed, half, 1)], out_w,
        [(half, F32), (DN_WIDTH, F32), (DN_WIDTH, _MXU)], [(1, V_HEAD), (1, DN_DIM)])

    dq, dk, dv = _attn_bwd(q, k, v, do_mla)
    dqf, dkvf, dkpe, g["q_norm_w"], g["k_norm_w"] = _rows(
        "mla_qk_bwd", _f_mla_qk_bwd, [_In(qf), _In(kvf), p_kpe, cos, sin_s, _In(dq), _In(dk), _In(dv)], qk_w,
        [(MLA_HEADS * HP, _MXU), (MLA_HEADS * HP, _MXU), (LANE, _MXU)], [(1, HP), (1, HP)])
    g["w_q_b"] = _mm("mla_q_b_dw", dqf, qn, "tn")
    g["w_kv_b"] = _mm("mla_kv_b_dw", kvn, dkvf, "tn")
    tok = grads_ready(g, ("w_out", "w_q_b", "w_kv_b"))

    def lat_bwd(n):
        def fn(i, dy, x, nw, _tok):
            dx, dwr = _rms_bwd(x, nw, dy, n)
            return (dx,), (_rowsum(dwr),)
        return fn

    dql, g["q_a_norm_w"] = _mm_rows("mla_q_b_dx", dqf, w["w_q_b"], "nn", lat_bwd(Q_LORA), [p_ql],
                                    [w["q_a_norm_w"], tok], [(Q_LORA, _MXU)], [(1, Q_LORA)])
    dkvl, g["kv_a_norm_w"] = _mm_rows("mla_kv_b_dx", dkvf, w["w_kv_b"], "nt", lat_bwd(KV_LORA), [p_kvl],
                                      [w["kv_a_norm_w"], tok], [(KV_LORA, _MXU)], [(1, KV_LORA)])

    dgq, dgk, dgv, dgg, dgb = _gdn_bwd(gq, gk, gv, gg, gb, s_all, t_all, do_dn)
    nxt = lambda a: _In(a, kind="next")
    dqkv, dab, g["dn_conv_w"], g["alog_b"], g["dtb_b"] = _rows(
        "gdn_prep_bwd", functools.partial(_f_gdn_prep_bwd, nt=nt),
        [p_qkv(), p_qkv("prev"), p_qkv("next"), p_ab, _In(dgq), nxt(dgq), _In(dgk), nxt(dgk), _In(dgv), nxt(dgv),
         _In(dgg), _In(dgb)], dn_w + [w["selpick"]],
        [(3 * DN_WIDTH, _MXU), (LANE, _MXU)], [(8, 3 * DN_WIDTH), (1, DN_WIDTH), (1, DN_WIDTH)])

    dproj = jnp.concatenate([dqkv, dz, dql, dkvl, dkpe, dab], axis=1)
    g["w_in"] = _mm("in_proj_dw", dproj, u, "tn", out_dtype=_MXU)
    tok = grads_ready(g, ("w_in",))
    dh0, g["attn_norm_w"] = _mm_rows(
        "in_proj_dx_rms", dproj, w["w_in"], "nn",
        lambda i, du, x, dres, nw, _tok: _f_rms_bwd_add(i, x, du, dres, nw, mask_pad=False),
        [_In(h0), _In(dh2)], [w["attn_norm_w"], tok], [(D_MODEL, F32)], [(1, D_MODEL)])
    return loss, dh0, g


def _w_in_to_padded(w):
    c1, c2, c3 = Q_LORA, Q_LORA + KV_LORA, Q_LORA + KV_LORA + QK_ROPE
    c4 = c3 + 3 * DN_WIDTH
    c5 = c4 + DN_WIDTH
    z = lambda n: jnp.zeros((n, w.shape[1]), w.dtype)
    return jnp.concatenate([w[c3:c4], w[c4:c5], w[:c1], w[c1:c2], w[c2:c3], z(LANE - QK_ROPE),
                            w[c5:], z(LANE - 2 * DN_HEADS)], axis=0)


def _w_in_from_padded(g):
    return jnp.concatenate([g[C_QL:C_QL + Q_LORA], g[C_KVL:C_KVL + KV_LORA], g[C_KPE:C_KPE + QK_ROPE],
                            g[:C_Z + DN_WIDTH], g[C_AB:C_AB + 2 * DN_HEADS]], axis=0)


def _w_q_b_to_padded(w):
    r = w.shape[1]
    w = w.reshape(MLA_HEADS, QK_HEAD, r)
    return jnp.pad(w, ((0, 0), (0, HP - QK_HEAD), (0, 0))).reshape(MLA_HEADS * HP, r)


def _w_q_b_from_padded(g):
    r = g.shape[1]
    return g.reshape(MLA_HEADS, HP, r)[:, :QK_HEAD].reshape(MLA_HEADS * QK_HEAD, r)


def _pad_rows8(w):
    return jnp.pad(w, ((0, 8 - w.shape[0]), (0, 0)))


def _prepare(full, tp):
    w = {}
    mx = lambda a: a.astype(_MXU)
    w["attn_norm_w"] = full["attn_norm_w"]
    w["w_in"] = mx(_w_in_to_padded(full["w_in"]))
    w["q_a_norm_w"] = full["q_a_norm_w"]
    w["kv_a_norm_w"] = full["kv_a_norm_w"]
    w["w_q_b"] = mx(_w_q_b_to_padded(full["w_q_b"]))
    w["w_kv_b"] = mx(full["w_kv_b"])
    w["q_norm_w"] = jnp.pad(full["q_norm_w"], ((0, 0), (0, HP - QK_HEAD)))
    w["k_norm_w"] = jnp.pad(full["k_norm_w"], ((0, 0), (0, HP - QK_HEAD)))
    w["mla_out_norm_w"] = full["mla_out_norm_w"]
    w["dn_out_norm_w"] = full["dn_out_norm_w"]
    w["dn_conv_w"] = _pad_rows8(full["dn_conv_w"])
    w["alog_b"] = jnp.repeat(full["dn_A_log"], DN_DIM, axis=1)
    w["dtb_b"] = jnp.repeat(full["dn_dt_bias"], DN_DIM, axis=1)
    w["ffn_norm_w"] = full["ffn_norm_w"]
    w["ffn_conv_w"] = _pad_rows8(full["ffn_conv_w"])
    w["ffn_conv_b"] = full["ffn_conv_b"]
    for n in _LATE:
        if n in full:
            w[n] = mx(full[n])
    half = QK_ROPE // 2
    inv = ROPE_THETA ** (-jnp.arange(half, dtype=F32) / half)
    ang = (jnp.arange(tp, dtype=jnp.int32) - PAD).astype(F32)[:, None] * inv[None, :]
    zc = jnp.zeros((tp, LANE - QK_ROPE), F32)
    w["cos"] = jnp.concatenate([jnp.cos(ang), jnp.cos(ang), zc], axis=1)
    w["sin_s"] = jnp.concatenate([-jnp.sin(ang), jnp.sin(ang), zc], axis=1)
    lane = jnp.arange(2 * DN_WIDTH)[None, :]
    src = jnp.arange(LANE)[:, None]
    w["sel"] = ((lane // DN_DIM) == src).astype(F32)
    w["selpick"] = ((src.T == (lane.T // DN_DIM)) & (lane.T % DN_DIM == 0)).astype(F32)
    return w


def _grads_to_natural(g):
    convert = {
        "w_in": ("w_in", _w_in_from_padded),
        "w_q_b": ("w_q_b", _w_q_b_from_padded),
        "q_norm_w": ("q_norm_w", lambda a: a[:, :QK_HEAD]),
        "k_norm_w": ("k_norm_w", lambda a: a[:, :QK_HEAD]),
        "dn_conv_w": ("dn_conv_w", lambda a: a[:DN_CONV]),
        "ffn_conv_w": ("ffn_conv_w", lambda a: a[:FFN_CONV]),
        "alog_b": ("dn_A_log", lambda a: a[:, ::DN_DIM]),
        "dtb_b": ("dn_dt_bias", lambda a: a[:, ::DN_DIM]),
    }
    n = {}
    for key, a in g.items():
        name, fn = convert.get(key, (key, lambda t: t))
        n[name] = fn(a)
    return n


_MESH = pl.DeviceIdType.MESH
_ANY = pl.BlockSpec(memory_space=pl.ANY)
_CHIP_FLIPS = ((1, 0), (0, 1), (1, 1))


def _me():
    return lax.axis_index("x"), lax.axis_index("y"), lax.axis_index("c")


def _all_gather(name, blk):
    def body(x_ref, out_ref, send_sems, recv_sems, local_sem):
        x, y, c = _me()
        me, sib = (x, y, c), (x, y, 1 - c)
        chips = [(x ^ fx, y ^ fy) for fx, fy in _CHIP_FLIPS]

        def slot(p):
            return out_ref.at[4 * p[0] + 2 * p[1] + p[2]]

        def copy(k, block, to, src=None):
            return pltpu.make_async_remote_copy(
                src_ref=slot(block) if src is None else src, dst_ref=slot(block),
                send_sem=send_sems.at[k], recv_sem=recv_sems.at[k], device_id=to, device_id_type=_MESH)

        mine = pltpu.make_async_copy(x_ref, slot(me), local_sem)
        mine.start()
        first = [copy(0, me, sib, src=x_ref)]
        first += [copy(1 + j, me, (*chip, c), src=x_ref) for j, chip in enumerate(chips)]
        for cp in first:
            cp.start()
        passed = [copy(4 + j, (*chip, c), sib) for j, chip in enumerate(chips)]
        for j, chip in enumerate(chips):
            copy(1 + j, (*chip, c), me).wait_recv()
            passed[j].start()
        copy(0, sib, me).wait_recv()
        for j, chip in enumerate(chips):
            copy(4 + j, (*chip, 1 - c), me).wait_recv()
        for cp in first + passed:
            cp.wait_send()
        mine.wait()

    return pl.pallas_call(
        body, name=name, in_specs=[_ANY], out_specs=_ANY,
        out_shape=jax.ShapeDtypeStruct((N_DEV,) + blk.shape, blk.dtype),
        scratch_shapes=[pltpu.SemaphoreType.DMA((7,)), pltpu.SemaphoreType.DMA((7,)), pltpu.SemaphoreType.DMA],
    )(blk)


def _rs_sibling(name, gb):
    def body(g_ref, out_ref, send_sems, recv_sems):
        x, y, c = _me()
        cps = []
        for j in range(4):
            cp = pltpu.make_async_remote_copy(
                src_ref=g_ref.at[2 * j + (1 - c)], dst_ref=out_ref.at[j], send_sem=send_sems.at[j],
                recv_sem=recv_sems.at[j], device_id=(x, y, 1 - c), device_id_type=_MESH)
            cp.start()
            cps.append(cp)
        for cp in cps:
            cp.wait()

    return pl.pallas_call(
        body, name=name, in_specs=[_ANY], out_specs=_ANY,
        out_shape=jax.ShapeDtypeStruct((4,) + gb.shape[1:], gb.dtype),
        scratch_shapes=[pltpu.SemaphoreType.DMA((4,)), pltpu.SemaphoreType.DMA((4,))],
    )(gb)


def _rs_chips(name, s1):
    def body(s_ref, out_ref, send_sems, recv_sems):
        x, y, c = _me()
        cps = []
        for k, (fx, fy) in enumerate(_CHIP_FLIPS):
            px, py = x ^ fx, y ^ fy
            cp = pltpu.make_async_remote_copy(
                src_ref=s_ref.at[2 * px + py], dst_ref=out_ref.at[k], send_sem=send_sems.at[k],
                recv_sem=recv_sems.at[k], device_id=(px, py, c), device_id_type=_MESH)
            cp.start()
            cps.append(cp)
        for cp in cps:
            cp.wait()

    return pl.pallas_call(
        body, name=name, in_specs=[_ANY], out_specs=_ANY,
        out_shape=jax.ShapeDtypeStruct((3,) + s1.shape[1:], s1.dtype),
        scratch_shapes=[pltpu.SemaphoreType.DMA((3,)), pltpu.SemaphoreType.DMA((3,))],
    )(s1)


def _row_tile(r):
    divs = [d for d in range(16, min(r, 512) + 1, 16) if r % d == 0]
    return divs[-1] if divs else r


def _pair_sum(name, gb, recv):
    _, r, cols = gb.shape
    tm = _row_tile(r)
    c = lax.axis_index("c").astype(jnp.int32).reshape(1)

    def body(c_ref, a_ref, b_ref, o_ref, ob_ref):
        s = a_ref[...] + b_ref[...]
        o_ref[...] = s
        ob_ref[...] = s.astype(BF16)

    blk = pl.BlockSpec((1, tm, cols), lambda j, i, c_ref: (j, i, 0))
    return pl.pallas_call(
        body, name=name,
        grid_spec=pltpu.PrefetchScalarGridSpec(
            num_scalar_prefetch=1, grid=(4, r // tm),
            in_specs=[pl.BlockSpec((1, tm, cols), lambda j, i, c_ref: (2 * j + c_ref[0], i, 0)), blk],
            out_specs=[blk, blk]),
        out_shape=[jax.ShapeDtypeStruct((4, r, cols), F32), jax.ShapeDtypeStruct((4, r, cols), BF16)],
        compiler_params=pltpu.CompilerParams(dimension_semantics=("parallel", "parallel")),
    )(c, gb, recv)


def _sum_parts(name, parts):
    _, r, cols = parts[0][0].shape
    tm = _row_tile(r)
    idx = jnp.stack([jnp.asarray(s, jnp.int32) for _, s in parts])
    n = len(parts)

    def body(idx_ref, *refs):
        g = refs[0][0].astype(F32)
        for p_ref in refs[1:n]:
            g = g + p_ref[0].astype(F32)
        refs[n][...] = g

    return pl.pallas_call(
        body, name=name,
        grid_spec=pltpu.PrefetchScalarGridSpec(
            num_scalar_prefetch=1, grid=(r // tm,),
            in_specs=[pl.BlockSpec((1, tm, cols), lambda i, idx_ref, p=p: (idx_ref[p], i, 0)) for p in range(n)],
            out_specs=pl.BlockSpec((tm, cols), lambda i, idx_ref: (i, 0))),
        out_shape=jax.ShapeDtypeStruct((r, cols), F32),
        compiler_params=pltpu.CompilerParams(dimension_semantics=("parallel",)),
    )(idx, *[a for a, _ in parts])


def _adam(name, parts, w, m, v):
    r, cols = w.shape
    tm = _row_tile(r)
    idx = jnp.stack([jnp.asarray(s, jnp.int32) for _, s in parts])
    n = len(parts)

    def body(idx_ref, *refs):
        g = refs[0][0].astype(F32)
        for p_ref in refs[1:n]:
            g = g + p_ref[0].astype(F32)
        w_ref, m_ref, v_ref, g_out, d_out, m_out, v_out = refs[n:]
        m_new = ADAM_B1 * m_ref[...] + (1.0 - ADAM_B1) * g
        v_new = ADAM_B2 * v_ref[...] + (1.0 - ADAM_B2) * (g * g)
        m_hat = m_new / (1.0 - ADAM_B1 ** ADAM_STEP)
        v_hat = v_new / (1.0 - ADAM_B2 ** ADAM_STEP)
        g_out[...] = g
        d_out[...] = -ADAM_LR * (m_hat / (jnp.sqrt(v_hat) + ADAM_EPS) + ADAM_WD * w_ref[...])
        m_out[...] = m_new
        v_out[...] = v_new

    part_specs = [pl.BlockSpec((1, tm, cols), lambda i, idx_ref, p=p: (idx_ref[p], i, 0)) for p in range(n)]
    flat = pl.BlockSpec((tm, cols), lambda i, idx_ref: (i, 0))
    return pl.pallas_call(
        body, name=name,
        grid_spec=pltpu.PrefetchScalarGridSpec(
            num_scalar_prefetch=1, grid=(r // tm,), in_specs=part_specs + [flat] * 3, out_specs=[flat] * 4),
        out_shape=[jax.ShapeDtypeStruct((r, cols), F32)] * 4,
        compiler_params=pltpu.CompilerParams(dimension_semantics=("parallel",)),
    )(idx, *[a for a, _ in parts], w, m, v)


def _all_gather_many(name, blks):
    n = len(blks)

    def body(*refs):
        x_refs, out_refs = refs[:n], refs[n:2 * n]
        send_sems, recv_sems, local_sems = refs[2 * n:]
        x, y, c = _me()
        me, sib = (x, y, c), (x, y, 1 - c)
        chips = [(x ^ fx, y ^ fy) for fx, fy in _CHIP_FLIPS]

        def slot(a, p):
            return out_refs[a].at[4 * p[0] + 2 * p[1] + p[2]]

        def copy(a, k, block, to, src=None):
            return pltpu.make_async_remote_copy(
                src_ref=slot(a, block) if src is None else src, dst_ref=slot(a, block),
                send_sem=send_sems.at[7 * a + k], recv_sem=recv_sems.at[7 * a + k], device_id=to,
                device_id_type=_MESH)

        mine = [pltpu.make_async_copy(x_refs[a], slot(a, me), local_sems.at[a]) for a in range(n)]
        first = []
        for a in range(n):
            mine[a].start()
            first.append(copy(a, 0, me, sib, src=x_refs[a]))
            first += [copy(a, 1 + j, me, (*chip, c), src=x_refs[a]) for j, chip in enumerate(chips)]
        for cp in first:
            cp.start()
        passed = []
        for j, chip in enumerate(chips):
            for a in range(n):
                copy(a, 1 + j, (*chip, c), me).wait_recv()
                cp = copy(a, 4 + j, (*chip, c), sib)
                cp.start()
                passed.append(cp)
        for a in range(n):
            copy(a, 0, sib, me).wait_recv()
            for j, chip in enumerate(chips):
                copy(a, 4 + j, (*chip, 1 - c), me).wait_recv()
        for cp in first + passed:
            cp.wait_send()
        for cp in mine:
            cp.wait()

    return pl.pallas_call(
        body, name=name, in_specs=[_ANY] * n, out_specs=[_ANY] * n,
        out_shape=[jax.ShapeDtypeStruct((N_DEV,) + b.shape, b.dtype) for b in blks],
        scratch_shapes=[pltpu.SemaphoreType.DMA((7 * n,)), pltpu.SemaphoreType.DMA((7 * n,)),
                        pltpu.SemaphoreType.DMA((n,))],
    )(*blks)


def _rs_sibling_many(name, gbs):
    n = len(gbs)

    def body(*refs):
        g_refs, out_refs = refs[:n], refs[n:2 * n]
        send_sems, recv_sems = refs[2 * n:]
        x, y, c = _me()
        cps = []
        for a in range(n):
            for j in range(4):
                cp = pltpu.make_async_remote_copy(
                    src_ref=g_refs[a].at[2 * j + (1 - c)], dst_ref=out_refs[a].at[j],
                    send_sem=send_sems.at[4 * a + j], recv_sem=recv_sems.at[4 * a + j],
                    device_id=(x, y, 1 - c), device_id_type=_MESH)
                cp.start()
                cps.append(cp)
        for cp in cps:
            cp.wait()

    return pl.pallas_call(
        body, name=name, in_specs=[_ANY] * n, out_specs=[_ANY] * n,
        out_shape=[jax.ShapeDtypeStruct((4,) + g.shape[1:], g.dtype) for g in gbs],
        scratch_shapes=[pltpu.SemaphoreType.DMA((4 * n,)), pltpu.SemaphoreType.DMA((4 * n,))],
    )(*gbs)


def _rs_chips_many(name, s1s):
    n = len(s1s)

    def body(*refs):
        s_refs, out_refs = refs[:n], refs[n:2 * n]
        send_sems, recv_sems = refs[2 * n:]
        x, y, c = _me()
        cps = []
        for a in range(n):
            for k, (fx, fy) in enumerate(_CHIP_FLIPS):
                px, py = x ^ fx, y ^ fy
                cp = pltpu.make_async_remote_copy(
                    src_ref=s_refs[a].at[2 * px + py], dst_ref=out_refs[a].at[k],
                    send_sem=send_sems.at[3 * a + k], recv_sem=recv_sems.at[3 * a + k],
                    device_id=(px, py, c), device_id_type=_MESH)
                cp.start()
                cps.append(cp)
        for cp in cps:
            cp.wait()

    return pl.pallas_call(
        body, name=name, in_specs=[_ANY] * n, out_specs=[_ANY] * n,
        out_shape=[jax.ShapeDtypeStruct((3,) + s.shape[1:], s.dtype) for s in s1s],
        scratch_shapes=[pltpu.SemaphoreType.DMA((3 * n,)), pltpu.SemaphoreType.DMA((3 * n,))],
    )(*s1s)


_HBM = pl.BlockSpec(memory_space=pltpu.HBM)
_SEM = pl.BlockSpec(memory_space=pltpu.SEMAPHORE)
_EFFECT = pltpu.SideEffectType.DATAFLOW_SIDE_EFFECTING


def _push_copies(src_refs, land_refs, send_sems, recv_sems, src_by_peer):
    x, y, c = _me()
    my_id = 4 * x + 2 * y + c
    out = []
    for a in range(len(src_refs)):
        for f in range(1, N_DEV):
            px, py, pc = x ^ (f >> 2), y ^ ((f >> 1) & 1), c ^ (f & 1)
            pid = 4 * px + 2 * py + pc
            src = src_refs[a].at[pid] if src_by_peer else src_refs[a]
            start = pltpu.make_async_remote_copy(
                src_ref=src, dst_ref=land_refs[a].at[my_id], send_sem=send_sems.at[7 * a + f - 1],
                recv_sem=recv_sems.at[7 * a + f - 1], device_id=(px, py, pc), device_id_type=_MESH)
            landed = pltpu.make_async_remote_copy(
                src_ref=src, dst_ref=land_refs[a].at[pid], send_sem=send_sems.at[7 * a + f - 1],
                recv_sem=recv_sems.at[7 * a + f - 1], device_id=(px, py, pc), device_id_type=_MESH)
            out.append((start, landed))
    return out


def _push_start(name, srcs, src_by_peer, after):
    n = len(srcs)
    lands = [jax.ShapeDtypeStruct((N_DEV,) + (s.shape[1:] if src_by_peer else s.shape), s.dtype) for s in srcs]

    def body(*refs):
        src_refs, land_refs = refs[:n], refs[n:2 * n]
        send_sems, recv_sems = refs[2 * n + 1], refs[2 * n + 2]
        token = refs[-1]
        for start, _ in _push_copies(src_refs, land_refs, send_sems, recv_sems, src_by_peer):
            start.start()
        token[...] = jnp.zeros_like(token)

    hbm = lambda a: pltpu.with_memory_space_constraint(a, pltpu.HBM)
    res = pl.pallas_call(
        body, name=name,
        out_shape=(pltpu.SemaphoreType.DMA((7 * n,)), pltpu.SemaphoreType.DMA((7 * n,)),
                   *[pltpu.HBM(s.shape, s.dtype) for s in srcs], *[pltpu.HBM(s.shape, s.dtype) for s in lands],
                   jax.ShapeDtypeStruct((8, LANE), F32)),
        in_specs=[_HBM] * (2 * n) + [_ANY],
        out_specs=(_SEM, _SEM, *[_HBM] * (2 * n), pl.BlockSpec(memory_space=pltpu.VMEM)),
        input_output_aliases={i: 2 + i for i in range(2 * n)},
        compiler_params=pltpu.CompilerParams(has_side_effects=_EFFECT),
    )(*[hbm(s) for s in srcs], *[hbm(lax.empty(s.shape, s.dtype)) for s in lands], after)
    return res[0], res[1], list(res[2:2 + n]), list(res[2 + n:2 + 2 * n]), res[-1]


def _push_wait(name, send_sems, recv_sems, srcs, lands, src_by_peer, after):
    n = len(srcs)

    def body(*refs):
        src_refs, land_refs = refs[:n], refs[n:2 * n]
        s_sems, r_sems = refs[2 * n], refs[2 * n + 1]
        for _, landed in _push_copies(src_refs, land_refs, s_sems, r_sems, src_by_peer):
            landed.wait_send()
            landed.wait_recv()

    res = pl.pallas_call(
        body, name=name,
        out_shape=tuple(pltpu.HBM(s.shape, s.dtype) for s in list(srcs) + list(lands)),
        in_specs=[_HBM] * (2 * n) + [_SEM, _SEM, _ANY],
        out_specs=tuple([_HBM] * (2 * n)),
        input_output_aliases={i: i for i in range(2 * n)},
        compiler_params=pltpu.CompilerParams(has_side_effects=_EFFECT),
    )(*srcs, *lands, send_sems, recv_sems, after)
    return list(res[:n]), list(res[n:])


_SHARDED = (
    ("meta_tokens", 1, (N_META, D_MODEL)),
    ("w_in", 1, (D_MODEL, IN_COLS)),
    ("w_q_b", 1, (Q_LORA, MLA_HEADS * QK_HEAD)),
    ("w_kv_b", 1, (KV_LORA, MLA_HEADS * (QK_NOPE + V_HEAD))),
    ("dn_conv_w", 1, (DN_CONV, 3 * DN_WIDTH)),
    ("w_out", 0, (2 * DN_WIDTH, D_MODEL)),
    ("w_gate", 1, (D_MODEL, D_FF)),
    ("w_up", 1, (D_MODEL, D_FF)),
    ("ffn_conv_w", 1, (FFN_CONV, D_FF)),
    ("w_down", 0, (D_FF, D_MODEL)),
)
_MXU_GATHERED = ("w_in", "w_q_b", "w_kv_b", "w_out", "w_gate", "w_up", "w_down")
_F32_GATHERED = ("meta_tokens", "dn_conv_w", "ffn_conv_w")
_EARLY = ("w_in", "w_q_b", "w_kv_b")
_LATE = ("w_out", "w_gate", "w_up", "w_down")
_TRANSPOSED = ("w_in", "w_q_b", "w_gate", "w_up")
_REPLICATED = (
    ("attn_norm_w", D_MODEL), ("q_a_norm_w", Q_LORA), ("kv_a_norm_w", KV_LORA), ("q_norm_w", QK_HEAD),
    ("k_norm_w", QK_HEAD), ("mla_out_norm_w", V_HEAD), ("dn_A_log", DN_HEADS), ("dn_dt_bias", DN_HEADS),
    ("dn_out_norm_w", DN_DIM), ("ffn_norm_w", D_MODEL), ("ffn_conv_b", D_FF),
)
_PACK_COLS = 1024
_PACK_ROW_MULT = 320
_SMALL_SHAPE = (8, 768)
_SMALL_BLOCK = (8, 512)


def _local_shape(dim, shape):
    return (shape[0] // N_DEV, shape[1]) if dim == 0 else (shape[0], shape[1] // N_DEV)


def _pack_rows(n, mult):
    rows = -(-n // _PACK_COLS)
    return -(-rows // mult) * mult


def _pack(flats, mult, axis=0):
    cat = jnp.concatenate(flats, axis=-1)
    n = cat.shape[-1]
    r = _pack_rows(n, mult)
    pad = [(0, 0)] * (cat.ndim - 1) + [(0, r * _PACK_COLS - n)]
    return jnp.pad(cat, pad).reshape(cat.shape[:-1] + (r, _PACK_COLS))


def _to_blocks(full, dim):
    r, c = full.shape
    if dim == 0:
        return full.reshape(N_DEV, (r // N_DEV) * c)
    return full.reshape(r, N_DEV, c // N_DEV).transpose(1, 0, 2).reshape(N_DEV, r * (c // N_DEV))


def _from_blocks(blocks, dim, shape):
    r, c = shape
    if dim == 0:
        return blocks.reshape(r, c)
    return blocks.reshape(N_DEV, r, c // N_DEV).transpose(1, 0, 2).reshape(r, c)


def _split(flat, sizes):
    out, o = [], 0
    for s in sizes:
        out.append(flat[..., o:o + s])
        o += s
    return out


def _gather_weights(local, names, dtype, mult):
    specs = [s for s in _SHARDED if s[0] in names]
    pack = _pack([local[n].astype(dtype).reshape(-1) for n, _, _ in specs], mult)
    got = _all_gather("gather_" + "_".join(n[:5] for n in names[:2]), pack)
    flat = got.reshape(N_DEV, -1)
    sizes = [math.prod(_local_shape(d, s)) for _, d, s in specs]
    return {n: _from_blocks(p, d, s) for (n, d, s), p in zip(specs, _split(flat, sizes))}


def kernel(x, meta_tokens, attn_norm_w, w_in, q_a_norm_w, w_q_b, kv_a_norm_w, w_kv_b, q_norm_w, k_norm_w, mla_out_norm_w, dn_conv_w, dn_A_log, dn_dt_bias, dn_out_norm_w, w_out, ffn_norm_w, w_gate, w_up, ffn_conv_w, ffn_conv_b, w_down, loss_target, m_meta_tokens, m_attn_norm_w, m_w_in, m_q_a_norm_w, m_w_q_b, m_kv_a_norm_w, m_w_kv_b, m_q_norm_w, m_k_norm_w, m_mla_out_norm_w, m_dn_conv_w, m_dn_A_log, m_dn_dt_bias, m_dn_out_norm_w, m_w_out, m_ffn_norm_w, m_w_gate, m_w_up, m_ffn_conv_w, m_ffn_conv_b, m_w_down, v_meta_tokens, v_attn_norm_w, v_w_in, v_q_a_norm_w, v_w_q_b, v_kv_a_norm_w, v_w_kv_b, v_q_norm_w, v_k_norm_w, v_mla_out_norm_w, v_dn_conv_w, v_dn_A_log, v_dn_dt_bias, v_dn_out_norm_w, v_w_out, v_ffn_norm_w, v_w_gate, v_w_up, v_ffn_conv_w, v_ffn_conv_b, v_w_down):
    names = [n for n, _, _ in _SHARDED] + [n for n, _ in _REPLICATED]
    given = dict(locals())
    two_d = lambda a: a.reshape(a.shape[-2:])
    view = lambda a, n: two_d(a).T if n in _TRANSPOSED else two_d(a)
    wl = {n: view(given[n], n) for n in names}
    ml = {n: view(given["m_" + n], n) for n in names}
    vl = {n: view(given["v_" + n], n) for n in names}
    out_shapes = {n: given[n].shape for n in names}

    spec = {n: (d, s) for n, d, s in _SHARDED}
    small_sizes = [math.prod(_local_shape(*spec[n])) for n in _F32_GATHERED]

    def small_block(d):
        cat = jnp.concatenate([d[n].reshape(d[n].shape[:-2] + (-1,)) for n in _F32_GATHERED], axis=-1)
        pad = [(0, 0)] * (cat.ndim - 1) + [(0, math.prod(_SMALL_BLOCK) - cat.shape[-1])]
        return jnp.pad(cat, pad).reshape(cat.shape[:-1] + _SMALL_BLOCK)

    def shard(n):
        return wl[n].astype(_MXU)

    def from_slots(n, blocks):
        d, s = spec[n]
        if d == 0 or n in _TRANSPOSED:
            return blocks.reshape(-1, blocks.shape[-1])
        return blocks.transpose(1, 0, 2).reshape(s)

    my_id = 4 * lax.axis_index("x") + 2 * lax.axis_index("y") + lax.axis_index("c")
    got = _all_gather_many("gather_early", [shard(n) for n in _EARLY] + [small_block(wl)])
    full = {n: a for n, a in wl.items() if n not in _LATE}
    for n, blocks in zip(_EARLY, got):
        full[n] = from_slots(n, blocks)
    for n, p in zip(_F32_GATHERED, _split(got[-1].reshape(N_DEV, -1), small_sizes)):
        full[n] = _from_blocks(p, *spec[n])
    late_own = [shard(n) for n in _LATE]
    l_send, l_recv, l_src, l_land, token = _push_start("gather_late_start", late_own, False, got[-1])

    def late_weights(after):
        _, lands = _push_wait("gather_late_wait", l_send, l_recv, l_src, l_land, False, after)
        out = {}
        for n, land, own in zip(_LATE, lands, late_own):
            out[n] = from_slots(n, lax.dynamic_update_slice(land, own[None], (my_id, 0, 0))).astype(_MXU)
        return out

    def dest_blocks(n, a):
        d, s = spec[n]
        r, c = _local_shape(d, s)
        if n in _TRANSPOSED:
            return a.reshape(N_DEV, c, r)
        return a.reshape(N_DEV, r, c) if d == 0 else a.reshape(r, N_DEV, c).transpose(1, 0, 2)

    pushed = []

    def grads_ready(g, names):
        nat = _grads_to_natural({n: g[n] for n in names})
        blocks = [dest_blocks(n, nat[n]).astype(_MXU) for n in names]
        sends, recvs, srcs, lands, tok = _push_start("rs_" + names[0] + "_start", blocks, True, token)
        pushed.append((names, sends, recvs, srcs, lands))
        return tok

    seq = x.shape[1]
    tp = ROW0 + seq
    h0 = jnp.concatenate([jnp.zeros((PAD, D_MODEL), F32), full["meta_tokens"], x[0]], axis=0)
    tgt = jnp.concatenate([jnp.zeros((ROW0, D_MODEL), F32), loss_target[0]], axis=0)
    loss, dh0, g = _local_step(h0, tgt, _prepare(full, tp), token, late_weights, grads_ready)
    g = _grads_to_natural(g)
    g["meta_tokens"] = dh0[PAD:ROW0]
    grad_x = dh0[ROW0:][None]

    big = [{}, {}, {}, {}]
    rep_names = [n for n, _ in _REPLICATED]
    pieces = [g[n].reshape(-1) for n in rep_names] + [loss[0, :1]] + [g[n].reshape(-1) for n in _F32_GATHERED]
    sizes = [p.shape[0] for p in pieces]
    cat = jnp.concatenate(pieces)
    cols = -(-cat.shape[0] // (8 * LANE)) * LANE
    mine = jnp.pad(cat, (0, 8 * cols - cat.shape[0])).reshape(8, cols)
    everyone = _all_gather("gather_small_grads", mine)
    total = _sum_parts("sum_small_grads", [(everyone, d) for d in range(N_DEV)])
    tot = dict(zip(rep_names + ["loss"] + list(_F32_GATHERED), _split(total.reshape(-1), sizes)))

    def small(d):
        cat = jnp.concatenate([d[n].reshape(-1) for n in rep_names])
        return jnp.pad(cat, (0, math.prod(_SMALL_SHAPE) - cat.shape[0])).reshape(_SMALL_SHAPE)

    sm = _adam("adam_replicated", [(small(tot)[None], 0)], small(wl), small(ml), small(vl))
    sm = [dict(zip(rep_names, _split(a.reshape(-1), [n for _, n in _REPLICATED]))) for a in sm]
    mine_of = {}
    for n in _F32_GATHERED:
        d, s = spec[n]
        r, c = _local_shape(d, s)
        mine_of[n] = lax.dynamic_slice(tot[n].reshape(s), (0, my_id * c), (r, c))
    res = _adam("adam_small_sharded", [(small_block(mine_of)[None], 0)], small_block(wl), small_block(ml),
                small_block(vl))
    for kind, a in enumerate(res):
        big[kind].update(zip(_F32_GATHERED, _split(a.reshape(-1), small_sizes)))

    for names, sends, recvs, srcs, lands in pushed:
        srcs, lands = _push_wait("rs_" + names[0] + "_wait", sends, recvs, srcs, lands, True, dh0)
        for n, src, land in zip(names, srcs, lands):
            parts = [(src, my_id)] + [(land, my_id ^ f) for f in range(1, N_DEV)]
            for kind, a in enumerate(_adam("adam_" + n, parts, wl[n], ml[n], vl[n])):
                big[kind][n] = a

    outs = [tot["loss"].reshape(()), grad_x]
    for kind in range(4):
        for n in ("meta_tokens", "attn_norm_w", "w_in", "q_a_norm_w", "w_q_b", "kv_a_norm_w", "w_kv_b", "q_norm_w",
                  "k_norm_w", "mla_out_norm_w", "dn_conv_w", "dn_A_log", "dn_dt_bias", "dn_out_norm_w", "w_out",
                  "ffn_norm_w", "w_gate", "w_up", "ffn_conv_w", "ffn_conv_b", "w_down"):
            src = big[kind] if n in big[kind] else sm[kind]
            a = src[n].T if n in _TRANSPOSED else src[n]
            outs.append(a.reshape(out_shapes[n]))
    return tuple(outs)
```

```python
import functools
import math

import jax
import jax.numpy as jnp
from jax import lax
from jax.experimental import pallas as pl
from jax.experimental.pallas import tpu as pltpu

F32 = jnp.float32
BF16 = jnp.bfloat16
_MXU = jnp.bfloat16
_HI = lax.Precision.HIGHEST

D_MODEL = 1024
N_META = 16
PAD = 112
ROW0 = PAD + N_META
MLA_HEADS = 4
QK_NOPE = 128
QK_ROPE = 64
QK_HEAD = QK_NOPE + QK_ROPE
V_HEAD = 128
Q_LORA = 256
KV_LORA = 256
ROPE_THETA = 10000.0
DN_HEADS = 4
DN_DIM = 128
DN_WIDTH = DN_HEADS * DN_DIM
DN_CONV = 4
DN_CHUNK = 64
D_FF = 2816
FFN_CONV = 3
EPS = 1e-6
HP = 256
C_QKV = 0
C_Z = 1536
C_QL = 2048
C_KVL = 2304
C_KPE = 2560
C_AB = 2688
IN_P = 2816
IN_COLS = 2632

ADAM_LR = 0.001
ADAM_B1 = 0.9
ADAM_B2 = 0.999
ADAM_EPS = 1e-08
ADAM_WD = 0.01
ADAM_STEP = 10

N_DEV = 8
TM = 128
LANE = 128
VMEM_LIMIT = 56 * 1024 * 1024
NEG = -1e30


def _dot(a, b, dims, hp=False):
    if hp:
        return lax.dot_general(a.astype(F32), b.astype(F32), (dims, ((), ())),
                               precision=lax.Precision.HIGH if hp == "3x" else _HI, preferred_element_type=F32)
    return lax.dot_general(a.astype(_MXU), b.astype(_MXU), (dims, ((), ())),
                           preferred_element_type=F32)


def _nn(a, b, hp=False):
    return _dot(a, b, ((1,), (0,)), hp)


def _nt(a, b, hp=False):
    return _dot(a, b, ((1,), (1,)), hp)


def _tn(a, b, hp=False):
    return _dot(a, b, ((0,), (0,)), hp)


def _sigmoid(x):
    return 1.0 / (1.0 + jnp.exp(-x))


def _rms_fwd(x, w, n):
    r = lax.rsqrt(jnp.sum(x * x, axis=-1, keepdims=True) * (1.0 / n) + EPS)
    return x * r * w, r


def _rms_bwd(x, w, dy, n):
    r = lax.rsqrt(jnp.sum(x * x, axis=-1, keepdims=True) * (1.0 / n) + EPS)
    xh = x * r
    gy = dy * w
    dx = r * (gy - xh * (jnp.sum(gy * xh, axis=-1, keepdims=True) * (1.0 / n)))
    return dx, dy * xh


def _rowsum(x):
    return jnp.sum(x, axis=0, keepdims=True)


def _row_ids(i, tm):
    return i * tm + lax.broadcasted_iota(jnp.int32, (tm, 1), 0)


def _shift_down(ext, s, tm):
    if s == 0:
        return ext[8:8 + tm]
    return pltpu.roll(ext, s, 0)[8:8 + tm]


def _shift_up(ext, s, tm):
    if s == 0:
        return ext[0:tm]
    return pltpu.roll(ext, tm + 8 - s, 0)[0:tm]


def _conv_fwd(x, halo_prev, w, width):
    tm = x.shape[0]
    ext = jnp.concatenate([halo_prev, x], axis=0)
    y = None
    for j in range(width):
        t = w[j:j + 1, :] * _shift_down(ext, width - 1 - j, tm)
        y = t if y is None else y + t
    return y


def _conv_bwd_x(dy, halo_next, w, width):
    tm = dy.shape[0]
    ext = jnp.concatenate([dy, halo_next], axis=0)
    dx = None
    for j in range(width):
        t = w[j:j + 1, :] * _shift_up(ext, width - 1 - j, tm)
        dx = t if dx is None else dx + t
    return dx


def _conv_bwd_w(dy, x, halo_prev, width):
    tm = dy.shape[0]
    ext = jnp.concatenate([halo_prev, x], axis=0)
    rows = [_rowsum(dy * _shift_down(ext, width - 1 - j, tm)) for j in range(width)]
    rows += [jnp.zeros_like(rows[0])] * (8 - width)
    return jnp.concatenate(rows, axis=0)


def _softplus(x):
    e = jnp.exp(-jnp.abs(x))
    u = 1.0 + e
    l1p = jnp.where(u == 1.0, e, jnp.log(u) * e / jnp.where(u == 1.0, 1.0, u - 1.0))
    return jnp.maximum(x, 0.0) + l1p


def _swap_halves(x):
    lane = lax.broadcasted_iota(jnp.int32, x.shape, 1)
    return jnp.where(lane < 32, pltpu.roll(x, 96, 1), jnp.where(lane < 64, pltpu.roll(x, 32, 1), 0.0))


class _In:
    def __init__(self, arr, width=None, cb=0, kind="cur"):
        self.arr, self.kind = arr, kind
        self.width = arr.shape[1] if width is None else width
        self.cb = cb


def _tile_spec(t, tm, tp):
    r8 = tm // 8
    if t.kind == "cur":
        return pl.BlockSpec((tm, t.width), lambda i, cb=t.cb: (i, cb))
    if t.kind == "prev":
        return pl.BlockSpec((8, t.width), lambda i, cb=t.cb: (jnp.maximum(i * r8 - 1, 0), cb))
    return pl.BlockSpec((8, t.width), lambda i, cb=t.cb: (jnp.minimum((i + 1) * r8, tp // 8 - 1), cb))


def _rows(name, fn, tiled, full, outs, accs=(), tm=TM):
    tp = tiled[0].arr.shape[0]
    nt = tp // tm
    r8 = tm // 8
    n_in = len(tiled) + len(full)
    n_out = len(outs)

    def body(*refs):
        i = pl.program_id(0)
        vals = [r[...] for r in refs[:n_in]]
        o_t, o_a = fn(i, *vals)
        for r, v in zip(refs[n_in:n_in + n_out], o_t):
            r[...] = v.astype(r.dtype)
        for r, v in zip(refs[n_in + n_out:], o_a):
            @pl.when(i == 0)
            def _():
                r[...] = v

            @pl.when(i > 0)
            def _():
                r[...] += v

    in_specs = [_tile_spec(t, tm, tp) for t in tiled]
    in_specs += [pl.BlockSpec(a.shape, lambda i, nd=a.ndim: (0,) * nd) for a in full]
    out_specs = [pl.BlockSpec((tm, w), lambda i: (i, 0)) for w, _ in outs]
    out_specs += [pl.BlockSpec((r, w), lambda i: (0, 0)) for r, w in accs]
    out_shape = [jax.ShapeDtypeStruct((tp, w), dt) for w, dt in outs]
    out_shape += [jax.ShapeDtypeStruct((r, w), F32) for r, w in accs]
    res = pl.pallas_call(
        body, name=name, grid=(nt,), in_specs=in_specs, out_specs=out_specs, out_shape=out_shape,
        compiler_params=pltpu.CompilerParams(dimension_semantics=("arbitrary",), vmem_limit_bytes=VMEM_LIMIT),
    )(*[t.arr for t in tiled], *full)
    return res


def _pick(n, cap, mult):
    best = None
    for d in range(mult, min(n, cap) + 1, mult):
        if n % d == 0:
            best = d
    assert best is not None, (n, cap, mult)
    return best


_ANY_SPEC = pl.BlockSpec(memory_space=pl.ANY)


def _mm(name, a, b, mode, out_dtype=F32, resid=None, after=None):
    if mode == "tn":
        m, k = a.shape
        n = b.shape[1]
        tk = _pick(k, 512, 128)
        tn = _pick(n, 1408, 128)

        def body_tn(a_ref, b_ref, o_ref):
            o_ref[...] = _tn(a_ref[...], b_ref[...]).astype(o_ref.dtype)

        return pl.pallas_call(
            body_tn, name=name, grid=(n // tn, k // tk),
            in_specs=[pl.BlockSpec((m, tk), lambda j, p: (0, p)),
                      pl.BlockSpec((m, tn), lambda j, p: (0, j))],
            out_specs=pl.BlockSpec((tk, tn), lambda j, p: (p, j)),
            out_shape=jax.ShapeDtypeStruct((k, n), out_dtype),
            compiler_params=pltpu.CompilerParams(
                dimension_semantics=("parallel", "parallel"), vmem_limit_bytes=VMEM_LIMIT),
        )(a, b)

    m, k = a.shape
    n = b.shape[1] if mode == "nn" else b.shape[0]
    tn = _pick(n, 1408, 128)
    tm = _pick(m, 1152, 16)
    dotf = _nn if mode == "nn" else _nt

    def body(*refs):
        a_ref, b_ref, o_ref = refs[0], refs[1], refs[-1]
        acc = dotf(a_ref[...], b_ref[...])
        if resid is not None:
            acc = refs[2][...] + acc
        o_ref[...] = acc.astype(o_ref.dtype)

    b_spec = (pl.BlockSpec((k, tn), lambda j, i: (0, j)) if mode == "nn"
              else pl.BlockSpec((tn, k), lambda j, i: (j, 0)))
    in_specs = [pl.BlockSpec((tm, k), lambda j, i: (i, 0)), b_spec]
    args = [a, b]
    if resid is not None:
        in_specs.append(pl.BlockSpec((tm, tn), lambda j, i: (i, j)))
        args.append(resid)
    if after is not None:
        in_specs.append(_ANY_SPEC)
        args.append(after)
    return pl.pallas_call(
        body, name=name, grid=(n // tn, m // tm), in_specs=in_specs,
        out_specs=pl.BlockSpec((tm, tn), lambda j, i: (i, j)),
        out_shape=jax.ShapeDtypeStruct((m, n), out_dtype),
        compiler_params=pltpu.CompilerParams(
            dimension_semantics=("parallel", "parallel"), vmem_limit_bytes=VMEM_LIMIT),
    )(*args)


def _norm_mm(name, x, norm_w, b, mode="nt", x_cb=0, after=None):
    m = x.shape[0]
    k = norm_w.shape[1]
    n = b.shape[0] if mode == "nt" else b.shape[1]
    tn = _pick(n, 1408, 128)
    tm = _pick(m, 1152, 16)
    dotf = _nt if mode == "nt" else _nn
    extra = [] if after is None else [after]

    def body(x_ref, w_ref, b_ref, *rest):
        o_ref, u_ref = rest[-2:]

        @pl.when(pl.program_id(1) == 0)
        def _():
            u_ref[...] = _rms_fwd(x_ref[...], w_ref[...], k)[0].astype(u_ref.dtype)

        o_ref[...] = dotf(u_ref[...], b_ref[...])

    b_spec = (pl.BlockSpec((tn, k), lambda i, j: (j, 0)) if mode == "nt"
              else pl.BlockSpec((k, tn), lambda i, j: (0, j)))
    return pl.pallas_call(
        body, name=name, grid=(m // tm, n // tn),
        in_specs=[pl.BlockSpec((tm, k), lambda i, j: (i, x_cb)), pl.BlockSpec((1, k), lambda i, j: (0, 0)),
                  b_spec] + [_ANY_SPEC] * len(extra),
        out_specs=[pl.BlockSpec((tm, tn), lambda i, j: (i, j)), pl.BlockSpec((tm, k), lambda i, j: (i, 0))],
        out_shape=[jax.ShapeDtypeStruct((m, n), F32), jax.ShapeDtypeStruct((m, k), _MXU)],
        compiler_params=pltpu.CompilerParams(
            dimension_semantics=("arbitrary", "arbitrary"), vmem_limit_bytes=VMEM_LIMIT),
    )(x, norm_w, b, *extra)


def _mm_rows(name, a, b, mode, fn, tiled, full, outs, accs=(), tm_cap=576):
    m = a.shape[0]
    tm = _pick(m, tm_cap, 16)
    dotf = _nn if mode == "nn" else _nt
    n_in = len(tiled) + len(full)
    n_out = len(outs)

    def body(*refs):
        i = pl.program_id(0)
        vals = [r[...] for r in refs[2:2 + n_in]]
        o_t, o_a = fn(i, dotf(refs[0][...], refs[1][...]), *vals)
        for r, v in zip(refs[2 + n_in:2 + n_in + n_out], o_t):
            r[...] = v.astype(r.dtype)
        for r, v in zip(refs[2 + n_in + n_out:], o_a):
            @pl.when(i == 0)
            def _():
                r[...] = v

            @pl.when(i > 0)
            def _():
                r[...] += v

    whole = lambda x: pl.BlockSpec(x.shape, lambda i, nd=x.ndim: (0,) * nd)
    in_specs = [pl.BlockSpec((tm, a.shape[1]), lambda i: (i, 0)), whole(b)]
    in_specs += [_tile_spec(t, tm, m) for t in tiled]
    in_specs += [whole(x) for x in full]
    out_specs = [pl.BlockSpec((tm, w), lambda i: (i, 0)) for w, _ in outs]
    out_specs += [pl.BlockSpec((r, w), lambda i: (0, 0)) for r, w in accs]
    out_shape = [jax.ShapeDtypeStruct((m, w), dt) for w, dt in outs]
    out_shape += [jax.ShapeDtypeStruct((r, w), F32) for r, w in accs]
    return pl.pallas_call(
        body, name=name, grid=(m // tm,), in_specs=in_specs, out_specs=out_specs, out_shape=out_shape,
        compiler_params=pltpu.CompilerParams(dimension_semantics=("arbitrary",), vmem_limit_bytes=VMEM_LIMIT),
    )(a, b, *[t.arr for t in tiled], *full)


ATTN_Q_TILES = 4


def _attn_probs(q, k, row0):
    tq, tp = q.shape[0], k.shape[0]
    s = _nt(q, k) * (1.0 / math.sqrt(QK_HEAD))
    row = row0 + lax.broadcasted_iota(jnp.int32, (tq, tp), 0)
    col = lax.broadcasted_iota(jnp.int32, (tq, tp), 1)
    ok = (col <= row) & (col >= PAD)
    s = jnp.where(ok, s, NEG)
    m = jnp.max(s, axis=-1, keepdims=True)
    e = jnp.exp(s - m)
    e = jnp.where(ok, e, 0.0)
    l = jnp.sum(e, axis=-1, keepdims=True)
    return e / jnp.maximum(l, 1e-30)


def _attn_fwd(q, k, v):
    tp = q.shape[0]
    tq = tp // ATTN_Q_TILES

    def body(q_ref, k_ref, v_ref, o_ref):
        for i in range(ATTN_Q_TILES):
            rows = slice(i * tq, (i + 1) * tq)
            keys = slice(0, (i + 1) * tq)
            p = _attn_probs(q_ref[rows, :], k_ref[keys, :], i * tq)
            o_ref[rows, :] = _nn(p, v_ref[keys, :])

    return pl.pallas_call(
        body, name="attn_fwd", grid=(MLA_HEADS,),
        in_specs=[pl.BlockSpec((tp, HP), lambda h: (0, h)),
                  pl.BlockSpec((tp, HP), lambda h: (0, h)),
                  pl.BlockSpec((tp, V_HEAD), lambda h: (0, h))],
        out_specs=pl.BlockSpec((tp, V_HEAD), lambda h: (0, h)),
        out_shape=jax.ShapeDtypeStruct((tp, MLA_HEADS * V_HEAD), F32),
        compiler_params=pltpu.CompilerParams(dimension_semantics=("parallel",), vmem_limit_bytes=VMEM_LIMIT),
    )(q, k, v)


def _attn_bwd(q, k, v, do):
    tp = q.shape[0]
    tq = tp // ATTN_Q_TILES

    def body(q_ref, k_ref, v_ref, do_ref, dq_ref, dk_ref, dv_ref):
        for i in reversed(range(ATTN_Q_TILES)):
            rows = slice(i * tq, (i + 1) * tq)
            keys = slice(0, (i + 1) * tq)
            qb = q_ref[rows, :]
            kk = k_ref[keys, :]
            dob = do_ref[rows, :]
            p = _attn_probs(qb, kk, i * tq)
            dp = _nt(dob, v_ref[keys, :])
            delta = jnp.sum(p * dp, axis=-1, keepdims=True)
            ds = p * (dp - delta) * (1.0 / math.sqrt(QK_HEAD))
            dq_ref[rows, :] = _nn(ds, kk)
            if i == ATTN_Q_TILES - 1:
                dk_ref[...] = _tn(ds, qb)
                dv_ref[...] = _tn(p, dob)
            else:
                dk_ref[keys, :] += _tn(ds, qb)
                dv_ref[keys, :] += _tn(p, dob)

    full = lambda w: pl.BlockSpec((tp, w), lambda h: (0, h))
    return pl.pallas_call(
        body, name="attn_bwd", grid=(MLA_HEADS,),
        in_specs=[full(HP), full(HP), full(V_HEAD), full(V_HEAD)],
        out_specs=[full(HP), full(HP), full(V_HEAD)],
        out_shape=[jax.ShapeDtypeStruct((tp, MLA_HEADS * HP), F32),
                   jax.ShapeDtypeStruct((tp, MLA_HEADS * HP), F32),
                   jax.ShapeDtypeStruct((tp, MLA_HEADS * V_HEAD), F32)],
        compiler_params=pltpu.CompilerParams(dimension_semantics=("parallel",), vmem_limit_bytes=VMEM_LIMIT),
    )(q, k, v, do)


def _gdn_consts():
    c = DN_CHUNK
    r = lax.broadcasted_iota(jnp.int32, (c, c), 0)
    cc = lax.broadcasted_iota(jnp.int32, (c, c), 1)
    incl = r >= cc
    strict = r > cc
    return incl, strict


def _cumsum_rows(x, reverse=False):
    c = x.shape[0]
    row = lax.broadcasted_iota(jnp.int32, x.shape, 0)
    s = 1
    while s < c:
        if reverse:
            x = x + jnp.where(row < c - s, pltpu.roll(x, c - s, 0), 0.0)
        else:
            x = x + jnp.where(row >= s, pltpu.roll(x, s, 0), 0.0)
        s *= 2
    return x


def _each(fn, *lists):
    return [fn(*a) for a in zip(*lists)]


def _interleave(chains):
    chains = list(chains)
    while chains:
        for ch in list(chains):
            try:
                next(ch)
            except StopIteration:
                chains.remove(ch)


def _gdn_chunk_common(q_ref, k_ref, v_ref, g_ref, b_ref):
    c = DN_CHUNK
    incl, strict = _gdn_consts()
    sls = [slice(DN_DIM * h, DN_DIM * (h + 1)) for h in range(DN_HEADS)]
    q = [q_ref[:, sl] * (1.0 / math.sqrt(DN_DIM)) for sl in sls]
    k = [k_ref[:, sl] for sl in sls]
    v = [v_ref[:, sl] for sl in sls]
    g = [g_ref[:, sl] for sl in sls]
    beta = [b_ref[:, sl] for sl in sls]
    gc = [_cumsum_rows(x) for x in g]
    grow = [x.T[:c, :] for x in gc]
    kb = _each(jnp.multiply, k, beta)
    kk = _each(_nt, kb, k)
    qk = _each(_nt, q, k)
    gam = [jnp.exp(x) for x in gc]
    g_last = [_rowsum(x) for x in g]
    dm = [jnp.exp(jnp.where(incl, x[:, :c] - y, NEG)) for x, y in zip(gc, grow)]
    vb = _each(jnp.multiply, v, beta)
    kbg = _each(jnp.multiply, kb, gam)
    ek = [jnp.exp(x - y) for x, y in zip(g_last, gc)]
    kd = _each(jnp.multiply, k, ek)
    return dict(q=q, k=k, v=v, beta=beta, gc=gc, gam=gam, g_last=g_last, dm=dm, kb=kb, vb=vb,
                kbg=kbg, kk=kk, ek=ek, kd=kd, qk=qk, incl=incl, strict=strict, sls=sls)


def _gdn_fwd(q, k, v, g, beta):
    tp = q.shape[0]
    c = DN_CHUNK
    nch = tp // c

    def body(q_ref, k_ref, v_ref, g_ref, b_ref, o_ref, s_ref, t_ref, s_scr):
        @pl.when(pl.program_id(0) == 0)
        def _():
            s_scr[...] = jnp.zeros_like(s_scr)

        eye = (lax.broadcasted_iota(jnp.int32, (c, c), 0) == lax.broadcasted_iota(jnp.int32, (c, c), 1)).astype(F32)
        x = _gdn_chunk_common(q_ref, k_ref, v_ref, g_ref, b_ref)
        heads = range(DN_HEADS)
        s = [s_scr[h] for h in heads]
        bp = [-jnp.where(x["strict"], kk * dm, 0.0) for kk, dm in zip(x["kk"], x["dm"])]
        t = [eye + b for b in bp]
        for _ in range(5):
            bp = [_nn(b, b, hp="3x") for b in bp]
            t = [tt + _nn(tt, b, hp="3x") for tt, b in zip(t, bp)]
        u = _each(_nn, t, x["vb"])
        w = _each(_nn, t, x["kbg"])
        v_new = [uu - _nn(ww, ss) for uu, ww, ss in zip(u, w, s)]
        o = [_nn(q * gam, ss) + _nn(qk * dm, vn)
             for q, gam, ss, qk, dm, vn in zip(x["q"], x["gam"], s, x["qk"], x["dm"], v_new)]
        s_new = [ss * jnp.exp(gl) + _tn(kd, vn) for ss, gl, kd, vn in zip(s, x["g_last"], x["kd"], v_new)]
        for h in heads:
            s_ref[h, 0] = s[h]
            t_ref[h, 0] = t[h]
            o_ref[:, x["sls"][h]] = o[h]
            s_scr[h] = s_new[h]

    rb = lambda n: (n, 0)
    return pl.pallas_call(
        body, name="gdn_fwd", grid=(nch,),
        in_specs=[pl.BlockSpec((c, DN_WIDTH), rb)] * 5,
        out_specs=[pl.BlockSpec((c, DN_WIDTH), rb),
                   pl.BlockSpec((DN_HEADS, 1, DN_DIM, DN_DIM), lambda n: (0, n, 0, 0)),
                   pl.BlockSpec((DN_HEADS, 1, c, c), lambda n: (0, n, 0, 0))],
        out_shape=[jax.ShapeDtypeStruct((tp, DN_WIDTH), F32),
                   jax.ShapeDtypeStruct((DN_HEADS, nch, DN_DIM, DN_DIM), F32),
                   jax.ShapeDtypeStruct((DN_HEADS, nch, c, c), F32)],
        scratch_shapes=[pltpu.VMEM((DN_HEADS, DN_DIM, DN_DIM), F32)],
        compiler_params=pltpu.CompilerParams(dimension_semantics=("arbitrary",), vmem_limit_bytes=VMEM_LIMIT),
    )(q, k, v, g, beta)


def _gdn_bwd(q, k, v, g, beta, s_all, t_all, do):
    tp = q.shape[0]
    c = DN_CHUNK
    nch = tp // c

    def body(q_ref, k_ref, v_ref, g_ref, b_ref, s_ref, t_ref, do_ref,
             dq_ref, dk_ref, dv_ref, dg_ref, db_ref, ds_scr):
        @pl.when(pl.program_id(0) == 0)
        def _():
            ds_scr[...] = jnp.zeros_like(ds_scr)

        xs = _gdn_chunk_common(q_ref, k_ref, v_ref, g_ref, b_ref)

        def chain(h):
            x = {key: (val[h] if isinstance(val, list) else val) for key, val in xs.items()}
            sl = x["sls"]
            qs, kx, vx, beta_, gam, dm = x["q"], x["k"], x["v"], x["beta"], x["gam"], x["dm"]
            kb, vb, kbg, kd, ek = x["kb"], x["vb"], x["kbg"], x["kd"], x["ek"]
            t = t_ref[h, 0]
            s = s_ref[h, 0]
            dsn = ds_scr[h]
            dob = do_ref[:, sl]
            eg_last = jnp.exp(x["g_last"])
            u = _nn(t, vb)
            w = _nn(t, kbg)
            mqk = x["qk"] * dm
            qd = qs * gam
            dqd = _nt(dob, s)
            dkd_pre = _nn(kd, dsn)
            yield
            v_new = u - _nn(w, s)
            dv_new = _tn(mqk, dob) + dkd_pre
            dq = dqd * gam
            dgam = jnp.sum(dqd * qs, axis=1, keepdims=True)
            yield
            ds_new = _tn(qd, dob) + eg_last * dsn - _tn(w, dv_new)
            dmm = jnp.where(x["incl"], _nt(dob, v_new), 0.0)
            dkd = _nt(v_new, dsn)
            dw = -_nt(dv_new, s)
            dvb = _tn(t, dv_new)
            dt = _nt(dv_new, vb)
            yield
            dqk = dmm * dm
            e_mat = dmm * mqk
            dq = dq + _nn(dqk, kx)
            dk = _tn(dqk, qs) + dkd * ek
            e1 = jnp.sum(dkd * kd, axis=1, keepdims=True)
            dgc = -e1
            dg_last = jnp.sum(e1) + eg_last * jnp.sum(s * dsn)
            dt = dt + _nt(dw, kbg)
            dkbg = _tn(t, dw)
            yield
            tdt = _tn(t, dt, hp="3x")
            yield
            da = jnp.where(x["strict"], -_nt(tdt, t, hp="3x"), 0.0)
            yield
            dkk = da * dm
            e_mat = e_mat + da * x["kk"] * dm
            dkb = _nn(dkk, kx) + dkbg * gam
            dk = dk + _tn(dkk, kb)
            dgam = dgam + jnp.sum(dkbg * kb, axis=1, keepdims=True)
            yield
            dk = dk + dkb * beta_
            dbeta = jnp.sum(dkb * kx, axis=1, keepdims=True) + jnp.sum(dvb * vx, axis=1, keepdims=True)
            dv = dvb * beta_
            dgc = dgc + jnp.sum(e_mat, axis=1, keepdims=True) + dgam * gam
            dgc = dgc - jnp.sum(e_mat.T, axis=1, keepdims=True)
            yield
            dg = _cumsum_rows(dgc, reverse=True) + dg_last
            yield
            ds_scr[h] = ds_new
            dq_ref[:, sl] = dq * (1.0 / math.sqrt(DN_DIM))
            dk_ref[:, sl] = dk
            dv_ref[:, sl] = dv
            dg_ref[:, sl] = dg
            db_ref[:, sl] = jnp.broadcast_to(dbeta, (c, LANE))

        _interleave([chain(h) for h in range(DN_HEADS)])

    rb = lambda n: (nch - 1 - n, 0)
    hs = lambda n: (0, nch - 1 - n, 0, 0)
    return pl.pallas_call(
        body, name="gdn_bwd", grid=(nch,),
        in_specs=[pl.BlockSpec((c, DN_WIDTH), rb)] * 5
        + [pl.BlockSpec((DN_HEADS, 1, DN_DIM, DN_DIM), hs), pl.BlockSpec((DN_HEADS, 1, c, c), hs),
           pl.BlockSpec((c, DN_WIDTH), rb)],
        out_specs=[pl.BlockSpec((c, DN_WIDTH), rb)] * 5,
        out_shape=[jax.ShapeDtypeStruct((tp, DN_WIDTH), F32)] * 5,
        scratch_shapes=[pltpu.VMEM((DN_HEADS, DN_DIM, DN_DIM), F32)],
        compiler_params=pltpu.CompilerParams(dimension_semantics=("arbitrary",), vmem_limit_bytes=VMEM_LIMIT),
    )(q, k, v, g, beta, s_all, t_all, do)


def _silu_parts(x):
    s = _sigmoid(x)
    return x * s, s * (1.0 + x * (1.0 - s))


def _f_rms_cast(i, x, w):
    y, _ = _rms_fwd(x, w, x.shape[1])
    return (y,), ()


def _f_rms_bwd_add(i, x, dy, dres, w, *, mask_pad):
    dx, dwr = _rms_bwd(x, w, dy, x.shape[1])
    out = dres + dx
    if mask_pad:
        out = jnp.where(_row_ids(i, x.shape[0]) >= PAD, out, 0.0)
    return (out,), (_rowsum(dwr),)


def _f_lat_norm(i, ql, kvl, qw, kvw):
    return (_rms_fwd(ql, qw, Q_LORA)[0], _rms_fwd(kvl, kvw, KV_LORA)[0]), ()


def _f_lat_norm_bwd(i, ql, kvl, dqn, dkvn, qw, kvw):
    dq, dqw = _rms_bwd(ql, qw, dqn, Q_LORA)
    dk, dkw = _rms_bwd(kvl, kvw, dkvn, KV_LORA)
    return (dq, dk), (_rowsum(dqw), _rowsum(dkw))


def _rope(x, cos, sin_s):
    return x * cos + _swap_halves(x) * sin_s


def _rope_t(dy, cos, sin_s):
    return dy * cos + _swap_halves(dy * sin_s)


def _f_mla_qk(i, qf, kvf, kpe, cos, sin_s, qw, kw):
    qs, ks, vs = [], [], []
    for h in range(MLA_HEADS):
        qn, _ = _rms_fwd(qf[:, HP * h:HP * (h + 1)], qw, QK_HEAD)
        qs += [qn[:, :QK_NOPE], _rope(qn[:, QK_NOPE:], cos, sin_s)]
        kh = jnp.concatenate([kvf[:, HP * h:HP * h + QK_NOPE], kpe], axis=1)
        kn, _ = _rms_fwd(kh, kw, QK_HEAD)
        ks += [kn[:, :QK_NOPE], _rope(kn[:, QK_NOPE:], cos, sin_s)]
        vs.append(kvf[:, HP * h + QK_NOPE:HP * (h + 1)])
    return (jnp.concatenate(qs, axis=1), jnp.concatenate(ks, axis=1), jnp.concatenate(vs, axis=1)), ()


def _f_mla_qk_bwd(i, qf, kvf, kpe, cos, sin_s, dq, dk, dv, qw, kw):
    dqf, dkvf = [], []
    dkpe = None
    dqw = None
    dkw = None
    for h in range(MLA_HEADS):
        dqh = dq[:, HP * h:HP * (h + 1)]
        dqn = jnp.concatenate([dqh[:, :QK_NOPE], _rope_t(dqh[:, QK_NOPE:], cos, sin_s)], axis=1)
        dx, dwr = _rms_bwd(qf[:, HP * h:HP * (h + 1)], qw, dqn, QK_HEAD)
        dqf.append(dx)
        dqw = _rowsum(dwr) if dqw is None else dqw + _rowsum(dwr)
        dkh = dk[:, HP * h:HP * (h + 1)]
        dkn = jnp.concatenate([dkh[:, :QK_NOPE], _rope_t(dkh[:, QK_NOPE:], cos, sin_s)], axis=1)
        kh = jnp.concatenate([kvf[:, HP * h:HP * h + QK_NOPE], kpe], axis=1)
        dx, dwr = _rms_bwd(kh, kw, dkn, QK_HEAD)
        dkvf += [dx[:, :QK_NOPE], dv[:, V_HEAD * h:V_HEAD * (h + 1)]]
        dkpe = dx[:, QK_NOPE:] if dkpe is None else dkpe + dx[:, QK_NOPE:]
        dkw = _rowsum(dwr) if dkw is None else dkw + _rowsum(dwr)
    return (jnp.concatenate(dqf, axis=1), jnp.concatenate(dkvf, axis=1), dkpe), (dqw, dkw)


def _gdn_act(i, x, halo, w8):
    tm = x.shape[0]
    halo = jnp.where(i > 0, halo, 0.0)
    c = _conv_fwd(x, halo, w8, DN_CONV)
    act, dact = _silu_parts(c)
    return act, dact


def _spread_heads(ab):
    tm = ab.shape[0]
    return jnp.concatenate([jnp.broadcast_to(ab[:, h:h + 1], (tm, DN_DIM)) for h in range(2 * DN_HEADS)], axis=1)


def _gather_heads(x):
    tm = x.shape[0]
    lane = lax.broadcasted_iota(jnp.int32, (tm, LANE), 1)
    out = jnp.zeros((tm, LANE), F32)
    for h in range(2 * DN_HEADS):
        out = out + jnp.where(lane == h, x[:, DN_DIM * h:DN_DIM * h + 1], 0.0)
    return out


def _f_gdn_prep(i, x, halo, ab, w8, alog, dtb):
    tm = x.shape[0]
    act, _ = _gdn_act(i, x, halo, w8)
    outs = []
    for part in range(2):
        for h in range(DN_HEADS):
            t = act[:, DN_WIDTH * part + DN_DIM * h:DN_WIDTH * part + DN_DIM * (h + 1)]
            outs.append(t * lax.rsqrt(jnp.sum(t * t, axis=-1, keepdims=True) + EPS))
    q = jnp.concatenate(outs[:DN_HEADS], axis=1)
    k = jnp.concatenate(outs[DN_HEADS:], axis=1)
    v = act[:, 2 * DN_WIDTH:]
    abb = _spread_heads(ab)
    valid = _row_ids(i, tm) >= PAD
    g = jnp.where(valid, -jnp.exp(alog) * _softplus(abb[:, :DN_WIDTH] + dtb), 0.0)
    beta = jnp.where(valid, _sigmoid(abb[:, DN_WIDTH:]), 0.0)
    return (q, k, v, g, beta), ()


def _f_gdn_prep_bwd(i, x, x_prev, x_next, ab, dq, dq_next, dk, dk_next, dv, dv_next, dg, dbeta,
                    w8, alog, dtb, *, nt):
    tm = x.shape[0]
    x_prev = jnp.where(i > 0, x_prev, 0.0)
    more = i < nt - 1
    ext = lambda t, t_next: jnp.concatenate([t, jnp.where(more, t_next, 0.0)], axis=0)
    c = _conv_fwd(jnp.concatenate([x, x_next], axis=0), x_prev, w8, DN_CONV)
    act, dact = _silu_parts(c)
    douts = []
    for part, dd in enumerate((ext(dq, dq_next), ext(dk, dk_next))):
        for h in range(DN_HEADS):
            t = act[:, DN_WIDTH * part + DN_DIM * h:DN_WIDTH * part + DN_DIM * (h + 1)]
            r = lax.rsqrt(jnp.sum(t * t, axis=-1, keepdims=True) + EPS)
            y = t * r
            dy = dd[:, DN_DIM * h:DN_DIM * (h + 1)]
            douts.append(r * (dy - y * jnp.sum(dy * y, axis=-1, keepdims=True)))
    douts.append(ext(dv, dv_next))
    dc = jnp.concatenate(douts, axis=1) * dact
    dqkv = _conv_bwd_x(dc[:tm], dc[tm:], w8, DN_CONV)
    dconv_w = _conv_bwd_w(dc[:tm], x, x_prev, DN_CONV)
    abb = _spread_heads(ab)
    valid = _row_ids(i, tm) >= PAD
    pre = abb[:, :DN_WIDTH] + dtb
    ea = jnp.exp(alog)
    g = -ea * _softplus(pre)
    dg = jnp.where(valid, dg, 0.0)
    dbeta = jnp.where(valid, dbeta, 0.0)
    da = dg * (-ea) * _sigmoid(pre)
    beta = _sigmoid(abb[:, DN_WIDTH:])
    db = dbeta * beta * (1.0 - beta)
    dab = _gather_heads(jnp.concatenate([da, db], axis=1))
    return (dqkv, dab), (dconv_w, _rowsum(dg * g), _rowsum(da))


def _f_conv_bwd(i, dy, dy_next, x, x_prev, w8, *, width, nt):
    dy_next = jnp.where(i < nt - 1, dy_next, 0.0)
    x_prev = jnp.where(i > 0, x_prev, 0.0)
    return (_conv_bwd_x(dy, dy_next, w8, width),), (_conv_bwd_w(dy, x, x_prev, width),)


def _f_mix(i, o_mla, o_dn, z, w_mla, w_dn):
    tm = o_mla.shape[0]
    valid = _row_ids(i, tm) >= PAD
    outs = []
    for h in range(MLA_HEADS):
        y, _ = _rms_fwd(o_mla[:, V_HEAD * h:V_HEAD * (h + 1)], w_mla, V_HEAD)
        outs.append(jnp.where(valid, y, 0.0))
    for h in range(DN_HEADS):
        y, _ = _rms_fwd(o_dn[:, DN_DIM * h:DN_DIM * (h + 1)], w_dn, DN_DIM)
        outs.append(y * _silu_parts(z[:, DN_DIM * h:DN_DIM * (h + 1)])[0])
    return (jnp.concatenate(outs, axis=1),), ()


def _f_mix_bwd(i, o_mla, o_dn, z, dy_mla, dy_dn, w_mla, w_dn):
    tm = o_mla.shape[0]
    valid = _row_ids(i, tm) >= PAD
    d_mla, d_dn, d_z = [], [], []
    dw_mla = None
    dw_dn = None
    for h in range(MLA_HEADS):
        sl = slice(V_HEAD * h, V_HEAD * (h + 1))
        dx, dwr = _rms_bwd(o_mla[:, sl], w_mla, jnp.where(valid, dy_mla[:, sl], 0.0), V_HEAD)
        d_mla.append(dx)
        dw_mla = _rowsum(dwr) if dw_mla is None else dw_mla + _rowsum(dwr)
    for h in range(DN_HEADS):
        sl = slice(DN_DIM * h, DN_DIM * (h + 1))
        y, _ = _rms_fwd(o_dn[:, sl], w_dn, DN_DIM)
        sz, dsz = _silu_parts(z[:, sl])
        d_z.append(dy_dn[:, sl] * y * dsz)
        dx, dwr = _rms_bwd(o_dn[:, sl], w_dn, dy_dn[:, sl] * sz, DN_DIM)
        d_dn.append(dx)
        dw_dn = _rowsum(dwr) if dw_dn is None else dw_dn + _rowsum(dwr)
    return ((jnp.concatenate(d_mla, axis=1), jnp.concatenate(d_dn, axis=1), jnp.concatenate(d_z, axis=1)),
            (dw_mla, dw_dn))


def _f_ffn_act(i, gate_pre, halo, up, w8, b):
    halo = jnp.where(i > 0, halo, 0.0)
    gate = _conv_fwd(gate_pre, halo, w8, FFN_CONV) + b
    return (_silu_parts(gate)[0] * up,), ()


def _f_ffn_act_bwd(i, gp, gp_prev, gp_next, up, up_next, dact, dact_next, w8, b, *, nt):
    tm = gp.shape[0]
    gp_prev = jnp.where(i > 0, gp_prev, 0.0)
    dact_next = jnp.where(i < nt - 1, dact_next, 0.0)
    cat = lambda t, t_next: jnp.concatenate([t, t_next], axis=0)
    gate = _conv_fwd(cat(gp, gp_next), gp_prev, w8, FFN_CONV) + b
    sg, dsg = _silu_parts(gate)
    dact_e = cat(dact, dact_next)
    dgate = dact_e * cat(up, up_next) * dsg
    dgate_pre = _conv_bwd_x(dgate[:tm], dgate[tm:], w8, FFN_CONV)
    dup = dact * sg[:tm]
    return (dgate_pre, dup), (_conv_bwd_w(dgate[:tm], gp, gp_prev, FFN_CONV), _rowsum(dgate[:tm]))


def _f_loss(i, h3, tgt):
    tm = h3.shape[0]
    diff = jnp.where(_row_ids(i, tm) >= ROW0, h3 - tgt, 0.0)
    part = 0.5 * jnp.sum(diff * diff) * (1.0 / D_MODEL)
    return (diff * (1.0 / D_MODEL),), (jnp.full((1, LANE), part, F32),)


def _after(fn):
    return lambda i, *a: fn(i, *a[:-1])


def _local_step(h0, tgt, w, token, late_weights, grads_ready):
    tp = h0.shape[0]
    nt = tp // TM
    bf = (D_MODEL, _MXU)
    proj, u = _norm_mm("in_proj", h0, w["attn_norm_w"], w["w_in"], after=token)
    p_qkv = lambda kind="cur": _In(proj, 3 * DN_WIDTH, 0, kind)
    p_z = _In(proj, DN_WIDTH, C_Z // DN_WIDTH)
    p_ql = _In(proj, Q_LORA, C_QL // Q_LORA)
    p_kvl = _In(proj, KV_LORA, C_KVL // KV_LORA)
    p_kpe = _In(proj, LANE, C_KPE // LANE)
    p_ab = _In(proj, LANE, C_AB // LANE)
    cos, sin_s = _In(w["cos"]), _In(w["sin_s"])

    qf, qn = _norm_mm("mla_q_b", proj, w["q_a_norm_w"], w["w_q_b"], "nt", C_QL // Q_LORA)
    kvf, kvn = _norm_mm("mla_kv_b", proj, w["kv_a_norm_w"], w["w_kv_b"], "nn", C_KVL // KV_LORA)
    qk_w = [w["q_norm_w"], w["k_norm_w"]]
    q, k, v = _rows("mla_qk", _f_mla_qk, [_In(qf), _In(kvf), p_kpe, cos, sin_s], qk_w,
                    [(MLA_HEADS * HP, _MXU), (MLA_HEADS * HP, _MXU), (MLA_HEADS * V_HEAD, _MXU)])
    o_mla = _attn_fwd(q, k, v)

    dn_w = [w["dn_conv_w"], w["alog_b"], w["dtb_b"]]
    gq, gk, gv, gg, gb = _rows("gdn_prep", _f_gdn_prep, [p_qkv(), p_qkv("prev"), p_ab], dn_w,
                               [(DN_WIDTH, F32)] * 5)
    o_dn, s_all, t_all = _gdn_fwd(gq, gk, gv, gg, gb)

    out_w = [w["mla_out_norm_w"], w["dn_out_norm_w"]]
    mixed, = _rows("mix", _f_mix, [_In(o_mla), _In(o_dn), p_z], out_w, [bf])
    w = dict(w, **late_weights(mixed))
    h2 = _mm("out_proj", mixed, w["w_out"], "nn", resid=h0)

    gate_pre, hn = _norm_mm("ffn_gate", h2, w["ffn_norm_w"], w["w_gate"])
    ffn_w = [w["ffn_conv_w"], w["ffn_conv_b"]]
    act, up = _mm_rows(
        "ffn_up_act", hn, w["w_up"], "nt",
        lambda i, up_t, gp, gp_prev, w8, b: ((_f_ffn_act(i, gp, gp_prev, up_t, w8, b)[0][0], up_t), ()),
        [_In(gate_pre), _In(gate_pre, kind="prev")], ffn_w, [(D_FF, _MXU), (D_FF, F32)], tm_cap=288)
    dh3, loss = _mm_rows("ffn_down_loss", act, w["w_down"], "nn", lambda i, y, r, t: _f_loss(i, r + y, t),
                         [_In(h2), _In(tgt)], [], [(D_MODEL, F32)], [(1, LANE)])

    g = {}
    dact = _mm("ffn_down_dx", dh3, w["w_down"], "nt")
    g["w_down"] = _mm("ffn_down_dw", act, dh3, "tn", out_dtype=_MXU)
    dgate_pre, dup, g["ffn_conv_w"], g["ffn_conv_b"] = _rows(
        "ffn_act_bwd", functools.partial(_f_ffn_act_bwd, nt=nt),
        [_In(gate_pre), _In(gate_pre, kind="prev"), _In(gate_pre, kind="next"), _In(up), _In(up, kind="next"),
         _In(dact), _In(dact, kind="next")], ffn_w,
        [(D_FF, _MXU), (D_FF, _MXU)], [(8, D_FF), (1, D_FF)])
    g["w_gate"] = _mm("ffn_gate_dw", dgate_pre, hn, "tn", out_dtype=_MXU)
    g["w_up"] = _mm("ffn_up_dw", dup, hn, "tn", out_dtype=_MXU)
    tok = grads_ready(g, ("w_down", "w_gate", "w_up"))
    dhn = _mm("ffn_gate_dx", dgate_pre, w["w_gate"], "nn", after=tok)
    dh2, g["ffn_norm_w"] = _mm_rows(
        "ffn_up_dx_rms", dup, w["w_up"], "nn",
        lambda i, y, d1, x, dres, nw: _f_rms_bwd_add(i, x, d1 + y, dres, nw, mask_pad=True),
        [_In(dhn), _In(h2), _In(dh3)], [w["ffn_norm_w"]], [(D_MODEL, F32)], [(1, D_MODEL)])

    g["w_out"] = _mm("out_proj_dw", mixed, dh2, "tn", out_dtype=_MXU)
    half = MLA_HEADS * V_HEAD
    do_mla, do_dn, dz, g["mla_out_norm_w"], g["dn_out_norm_w"] = _mm_rows(
        "out_proj_dx_mix", dh2, w["w_out"], "nt",
        lambda i, dm, om, od, z, wm, wd: _f_mix_bwd(i, om, od, z, dm[:, :half], dm[:, half:], wm, wd),
        [_In(o_mla), _In(o_dn), p_z], out_w,
        [(half, F32), (DN_WIDTH, F32), (DN_WIDTH, _MXU)], [(1, V_HEAD), (1, DN_DIM)])

    dq, dk, dv = _attn_bwd(q, k, v, do_mla)
    dqf, dkvf, dkpe, g["q_norm_w"], g["k_norm_w"] = _rows(
        "mla_qk_bwd", _f_mla_qk_bwd, [_In(qf), _In(kvf), p_kpe, cos, sin_s, _In(dq), _In(dk), _In(dv)], qk_w,
        [(MLA_HEADS * HP, _MXU), (MLA_HEADS * HP, _MXU), (LANE, _MXU)], [(1, HP), (1, HP)])
    g["w_q_b"] = _mm("mla_q_b_dw", dqf, qn, "tn")
    g["w_kv_b"] = _mm("mla_kv_b_dw", kvn, dkvf, "tn")
    tok = grads_ready(g, ("w_out", "w_q_b", "w_kv_b"))

    def lat_bwd(n):
        def fn(i, dy, x, nw, _tok):
            dx, dwr = _rms_bwd(x, nw, dy, n)
            return (dx,), (_rowsum(dwr),)
        return fn

    dql, g["q_a_norm_w"] = _mm_rows("mla_q_b_dx", dqf, w["w_q_b"], "nn", lat_bwd(Q_LORA), [p_ql],
                                    [w["q_a_norm_w"], tok], [(Q_LORA, _MXU)], [(1, Q_LORA)])
    dkvl, g["kv_a_norm_w"] = _mm_rows("mla_kv_b_dx", dkvf, w["w_kv_b"], "nt", lat_bwd(KV_LORA), [p_kvl],
                                      [w["kv_a_norm_w"], tok], [(KV_LORA, _MXU)], [(1, KV_LORA)])

    dgq, dgk, dgv, dgg, dgb = _gdn_bwd(gq, gk, gv, gg, gb, s_all, t_all, do_dn)
    nxt = lambda a: _In(a, kind="next")
    dqkv, dab, g["dn_conv_w"], g["alog_b"], g["dtb_b"] = _rows(
        "gdn_prep_bwd", functools.partial(_f_gdn_prep_bwd, nt=nt),
        [p_qkv(), p_qkv("prev"), p_qkv("next"), p_ab, _In(dgq), nxt(dgq), _In(dgk), nxt(dgk), _In(dgv), nxt(dgv),
         _In(dgg), _In(dgb)], dn_w,
        [(3 * DN_WIDTH, _MXU), (LANE, _MXU)], [(8, 3 * DN_WIDTH), (1, DN_WIDTH), (1, DN_WIDTH)])

    dproj = jnp.concatenate([dqkv, dz, dql, dkvl, dkpe, dab], axis=1)
    g["w_in"] = _mm("in_proj_dw", dproj, u, "tn", out_dtype=_MXU)
    tok = grads_ready(g, ("w_in",))
    dh0, g["attn_norm_w"] = _mm_rows(
        "in_proj_dx_rms", dproj, w["w_in"], "nn",
        lambda i, du, x, dres, nw, _tok: _f_rms_bwd_add(i, x, du, dres, nw, mask_pad=False),
        [_In(h0), _In(dh2)], [w["attn_norm_w"], tok], [(D_MODEL, F32)], [(1, D_MODEL)])
    return loss, dh0, g


def _w_in_to_padded(w):
    c1, c2, c3 = Q_LORA, Q_LORA + KV_LORA, Q_LORA + KV_LORA + QK_ROPE
    c4 = c3 + 3 * DN_WIDTH
    c5 = c4 + DN_WIDTH
    z = lambda n: jnp.zeros((n, w.shape[1]), w.dtype)
    return jnp.concatenate([w[c3:c4], w[c4:c5], w[:c1], w[c1:c2], w[c2:c3], z(LANE - QK_ROPE),
                            w[c5:], z(LANE - 2 * DN_HEADS)], axis=0)


def _w_in_from_padded(g):
    return jnp.concatenate([g[C_QL:C_QL + Q_LORA], g[C_KVL:C_KVL + KV_LORA], g[C_KPE:C_KPE + QK_ROPE],
                            g[:C_Z + DN_WIDTH], g[C_AB:C_AB + 2 * DN_HEADS]], axis=0)


def _w_q_b_to_padded(w):
    r = w.shape[1]
    w = w.reshape(MLA_HEADS, QK_HEAD, r)
    return jnp.pad(w, ((0, 0), (0, HP - QK_HEAD), (0, 0))).reshape(MLA_HEADS * HP, r)


def _w_q_b_from_padded(g):
    r = g.shape[1]
    return g.reshape(MLA_HEADS, HP, r)[:, :QK_HEAD].reshape(MLA_HEADS * QK_HEAD, r)


def _pad_rows8(w):
    return jnp.pad(w, ((0, 8 - w.shape[0]), (0, 0)))


def _prepare(full, tp):
    w = {}
    mx = lambda a: a.astype(_MXU)
    w["attn_norm_w"] = full["attn_norm_w"]
    w["w_in"] = mx(_w_in_to_padded(full["w_in"]))
    w["q_a_norm_w"] = full["q_a_norm_w"]
    w["kv_a_norm_w"] = full["kv_a_norm_w"]
    w["w_q_b"] = mx(_w_q_b_to_padded(full["w_q_b"]))
    w["w_kv_b"] = mx(full["w_kv_b"])
    w["q_norm_w"] = jnp.pad(full["q_norm_w"], ((0, 0), (0, HP - QK_HEAD)))
    w["k_norm_w"] = jnp.pad(full["k_norm_w"], ((0, 0), (0, HP - QK_HEAD)))
    w["mla_out_norm_w"] = full["mla_out_norm_w"]
    w["dn_out_norm_w"] = full["dn_out_norm_w"]
    w["dn_conv_w"] = _pad_rows8(full["dn_conv_w"])
    w["alog_b"] = jnp.repeat(full["dn_A_log"], DN_DIM, axis=1)
    w["dtb_b"] = jnp.repeat(full["dn_dt_bias"], DN_DIM, axis=1)
    w["ffn_norm_w"] = full["ffn_norm_w"]
    w["ffn_conv_w"] = _pad_rows8(full["ffn_conv_w"])
    w["ffn_conv_b"] = full["ffn_conv_b"]
    for n in _LATE:
        if n in full:
            w[n] = mx(full[n])
    half = QK_ROPE // 2
    inv = ROPE_THETA ** (-jnp.arange(half, dtype=F32) / half)
    ang = (jnp.arange(tp, dtype=jnp.int32) - PAD).astype(F32)[:, None] * inv[None, :]
    zc = jnp.zeros((tp, LANE - QK_ROPE), F32)
    w["cos"] = jnp.concatenate([jnp.cos(ang), jnp.cos(ang), zc], axis=1)
    w["sin_s"] = jnp.concatenate([-jnp.sin(ang), jnp.sin(ang), zc], axis=1)
    return w


def _grads_to_natural(g):
    convert = {
        "w_in": ("w_in", _w_in_from_padded),
        "w_q_b": ("w_q_b", _w_q_b_from_padded),
        "q_norm_w": ("q_norm_w", lambda a: a[:, :QK_HEAD]),
        "k_norm_w": ("k_norm_w", lambda a: a[:, :QK_HEAD]),
        "dn_conv_w": ("dn_conv_w", lambda a: a[:DN_CONV]),
        "ffn_conv_w": ("ffn_conv_w", lambda a: a[:FFN_CONV]),
        "alog_b": ("dn_A_log", lambda a: a[:, ::DN_DIM]),
        "dtb_b": ("dn_dt_bias", lambda a: a[:, ::DN_DIM]),
    }
    n = {}
    for key, a in g.items():
        name, fn = convert.get(key, (key, lambda t: t))
        n[name] = fn(a)
    return n


_MESH = pl.DeviceIdType.MESH
_ANY = pl.BlockSpec(memory_space=pl.ANY)
_CHIP_FLIPS = ((1, 0), (0, 1), (1, 1))


def _me():
    return lax.axis_index("x"), lax.axis_index("y"), lax.axis_index("c")


def _all_gather(name, blk):
    def body(x_ref, out_ref, send_sems, recv_sems, local_sem):
        x, y, c = _me()
        me, sib = (x, y, c), (x, y, 1 - c)
        chips = [(x ^ fx, y ^ fy) for fx, fy in _CHIP_FLIPS]

        def slot(p):
            return out_ref.at[4 * p[0] + 2 * p[1] + p[2]]

        def copy(k, block, to, src=None):
            return pltpu.make_async_remote_copy(
                src_ref=slot(block) if src is None else src, dst_ref=slot(block),
                send_sem=send_sems.at[k], recv_sem=recv_sems.at[k], device_id=to, device_id_type=_MESH)

        mine = pltpu.make_async_copy(x_ref, slot(me), local_sem)
        mine.start()
        first = [copy(0, me, sib, src=x_ref)]
        first += [copy(1 + j, me, (*chip, c), src=x_ref) for j, chip in enumerate(chips)]
        for cp in first:
            cp.start()
        passed = [copy(4 + j, (*chip, c), sib) for j, chip in enumerate(chips)]
        for j, chip in enumerate(chips):
            copy(1 + j, (*chip, c), me).wait_recv()
            passed[j].start()
        copy(0, sib, me).wait_recv()
        for j, chip in enumerate(chips):
            copy(4 + j, (*chip, 1 - c), me).wait_recv()
        for cp in first + passed:
            cp.wait_send()
        mine.wait()

    return pl.pallas_call(
        body, name=name, in_specs=[_ANY], out_specs=_ANY,
        out_shape=jax.ShapeDtypeStruct((N_DEV,) + blk.shape, blk.dtype),
        scratch_shapes=[pltpu.SemaphoreType.DMA((7,)), pltpu.SemaphoreType.DMA((7,)), pltpu.SemaphoreType.DMA],
    )(blk)


def _rs_sibling(name, gb):
    def body(g_ref, out_ref, send_sems, recv_sems):
        x, y, c = _me()
        cps = []
        for j in range(4):
            cp = pltpu.make_async_remote_copy(
                src_ref=g_ref.at[2 * j + (1 - c)], dst_ref=out_ref.at[j], send_sem=send_sems.at[j],
                recv_sem=recv_sems.at[j], device_id=(x, y, 1 - c), device_id_type=_MESH)
            cp.start()
            cps.append(cp)
        for cp in cps:
            cp.wait()

    return pl.pallas_call(
        body, name=name, in_specs=[_ANY], out_specs=_ANY,
        out_shape=jax.ShapeDtypeStruct((4,) + gb.shape[1:], gb.dtype),
        scratch_shapes=[pltpu.SemaphoreType.DMA((4,)), pltpu.SemaphoreType.DMA((4,))],
    )(gb)


def _rs_chips(name, s1):
    def body(s_ref, out_ref, send_sems, recv_sems):
        x, y, c = _me()
        cps = []
        for k, (fx, fy) in enumerate(_CHIP_FLIPS):
            px, py = x ^ fx, y ^ fy
            cp = pltpu.make_async_remote_copy(
                src_ref=s_ref.at[2 * px + py], dst_ref=out_ref.at[k], send_sem=send_sems.at[k],
                recv_sem=recv_sems.at[k], device_id=(px, py, c), device_id_type=_MESH)
            cp.start()
            cps.append(cp)
        for cp in cps:
            cp.wait()

    return pl.pallas_call(
        body, name=name, in_specs=[_ANY], out_specs=_ANY,
        out_shape=jax.ShapeDtypeStruct((3,) + s1.shape[1:], s1.dtype),
        scratch_shapes=[pltpu.SemaphoreType.DMA((3,)), pltpu.SemaphoreType.DMA((3,))],
    )(s1)


def _row_tile(r):
    divs = [d for d in range(16, min(r, 512) + 1, 16) if r % d == 0]
    return divs[-1] if divs else r


def _pair_sum(name, gb, recv):
    _, r, cols = gb.shape
    tm = _row_tile(r)
    c = lax.axis_index("c").astype(jnp.int32).reshape(1)

    def body(c_ref, a_ref, b_ref, o_ref, ob_ref):
        s = a_ref[...] + b_ref[...]
        o_ref[...] = s
        ob_ref[...] = s.astype(BF16)

    blk = pl.BlockSpec((1, tm, cols), lambda j, i, c_ref: (j, i, 0))
    return pl.pallas_call(
        body, name=name,
        grid_spec=pltpu.PrefetchScalarGridSpec(
            num_scalar_prefetch=1, grid=(4, r // tm),
            in_specs=[pl.BlockSpec((1, tm, cols), lambda j, i, c_ref: (2 * j + c_ref[0], i, 0)), blk],
            out_specs=[blk, blk]),
        out_shape=[jax.ShapeDtypeStruct((4, r, cols), F32), jax.ShapeDtypeStruct((4, r, cols), BF16)],
        compiler_params=pltpu.CompilerParams(dimension_semantics=("parallel", "parallel")),
    )(c, gb, recv)


def _sum_parts(name, parts):
    _, r, cols = parts[0][0].shape
    tm = _row_tile(r)
    idx = jnp.stack([jnp.asarray(s, jnp.int32) for _, s in parts])
    n = len(parts)

    def body(idx_ref, *refs):
        g = refs[0][0].astype(F32)
        for p_ref in refs[1:n]:
            g = g + p_ref[0].astype(F32)
        refs[n][...] = g

    return pl.pallas_call(
        body, name=name,
        grid_spec=pltpu.PrefetchScalarGridSpec(
            num_scalar_prefetch=1, grid=(r // tm,),
            in_specs=[pl.BlockSpec((1, tm, cols), lambda i, idx_ref, p=p: (idx_ref[p], i, 0)) for p in range(n)],
            out_specs=pl.BlockSpec((tm, cols), lambda i, idx_ref: (i, 0))),
        out_shape=jax.ShapeDtypeStruct((r, cols), F32),
        compiler_params=pltpu.CompilerParams(dimension_semantics=("parallel",)),
    )(idx, *[a for a, _ in parts])


def _adam(name, parts, w, m, v):
    r, cols = w.shape
    tm = _row_tile(r)
    idx = jnp.stack([jnp.asarray(s, jnp.int32) for _, s in parts])
    n = len(parts)

    def body(idx_ref, *refs):
        g = refs[0][0].astype(F32)
        for p_ref in refs[1:n]:
            g = g + p_ref[0].astype(F32)
        w_ref, m_ref, v_ref, g_out, d_out, m_out, v_out = refs[n:]
        m_new = ADAM_B1 * m_ref[...] + (1.0 - ADAM_B1) * g
        v_new = ADAM_B2 * v_ref[...] + (1.0 - ADAM_B2) * (g * g)
        m_hat = m_new / (1.0 - ADAM_B1 ** ADAM_STEP)
        v_hat = v_new / (1.0 - ADAM_B2 ** ADAM_STEP)
        g_out[...] = g
        d_out[...] = -ADAM_LR * (m_hat / (jnp.sqrt(v_hat) + ADAM_EPS) + ADAM_WD * w_ref[...])
        m_out[...] = m_new
        v_out[...] = v_new

    part_specs = [pl.BlockSpec((1, tm, cols), lambda i, idx_ref, p=p: (idx_ref[p], i, 0)) for p in range(n)]
    flat = pl.BlockSpec((tm, cols), lambda i, idx_ref: (i, 0))
    return pl.pallas_call(
        body, name=name,
        grid_spec=pltpu.PrefetchScalarGridSpec(
            num_scalar_prefetch=1, grid=(r // tm,), in_specs=part_specs + [flat] * 3, out_specs=[flat] * 4),
        out_shape=[jax.ShapeDtypeStruct((r, cols), F32)] * 4,
        compiler_params=pltpu.CompilerParams(dimension_semantics=("parallel",)),
    )(idx, *[a for a, _ in parts], w, m, v)


def _all_gather_many(name, blks):
    n = len(blks)

    def body(*refs):
        x_refs, out_refs = refs[:n], refs[n:2 * n]
        send_sems, recv_sems, local_sems = refs[2 * n:]
        x, y, c = _me()
        me, sib = (x, y, c), (x, y, 1 - c)
        chips = [(x ^ fx, y ^ fy) for fx, fy in _CHIP_FLIPS]

        def slot(a, p):
            return out_refs[a].at[4 * p[0] + 2 * p[1] + p[2]]

        def copy(a, k, block, to, src=None):
            return pltpu.make_async_remote_copy(
                src_ref=slot(a, block) if src is None else src, dst_ref=slot(a, block),
                send_sem=send_sems.at[7 * a + k], recv_sem=recv_sems.at[7 * a + k], device_id=to,
                device_id_type=_MESH)

        mine = [pltpu.make_async_copy(x_refs[a], slot(a, me), local_sems.at[a]) for a in range(n)]
        first = []
        for a in range(n):
            mine[a].start()
            first.append(copy(a, 0, me, sib, src=x_refs[a]))
            first += [copy(a, 1 + j, me, (*chip, c), src=x_refs[a]) for j, chip in enumerate(chips)]
        for cp in first:
            cp.start()
        passed = []
        for j, chip in enumerate(chips):
            for a in range(n):
                copy(a, 1 + j, (*chip, c), me).wait_recv()
                cp = copy(a, 4 + j, (*chip, c), sib)
                cp.start()
                passed.append(cp)
        for a in range(n):
            copy(a, 0, sib, me).wait_recv()
            for j, chip in enumerate(chips):
                copy(a, 4 + j, (*chip, 1 - c), me).wait_recv()
        for cp in first + passed:
            cp.wait_send()
        for cp in mine:
            cp.wait()

    return pl.pallas_call(
        body, name=name, in_specs=[_ANY] * n, out_specs=[_ANY] * n,
        out_shape=[jax.ShapeDtypeStruct((N_DEV,) + b.shape, b.dtype) for b in blks],
        scratch_shapes=[pltpu.SemaphoreType.DMA((7 * n,)), pltpu.SemaphoreType.DMA((7 * n,)),
                        pltpu.SemaphoreType.DMA((n,))],
    )(*blks)


def _rs_sibling_many(name, gbs):
    n = len(gbs)

    def body(*refs):
        g_refs, out_refs = refs[:n], refs[n:2 * n]
        send_sems, recv_sems = refs[2 * n:]
        x, y, c = _me()
        cps = []
        for a in range(n):
            for j in range(4):
                cp = pltpu.make_async_remote_copy(
                    src_ref=g_refs[a].at[2 * j + (1 - c)], dst_ref=out_refs[a].at[j],
                    send_sem=send_sems.at[4 * a + j], recv_sem=recv_sems.at[4 * a + j],
                    device_id=(x, y, 1 - c), device_id_type=_MESH)
                cp.start()
                cps.append(cp)
        for cp in cps:
            cp.wait()

    return pl.pallas_call(
        body, name=name, in_specs=[_ANY] * n, out_specs=[_ANY] * n,
        out_shape=[jax.ShapeDtypeStruct((4,) + g.shape[1:], g.dtype) for g in gbs],
        scratch_shapes=[pltpu.SemaphoreType.DMA((4 * n,)), pltpu.SemaphoreType.DMA((4 * n,))],
    )(*gbs)


def _rs_chips_many(name, s1s):
    n = len(s1s)

    def body(*refs):
        s_refs, out_refs = refs[:n], refs[n:2 * n]
        send_sems, recv_sems = refs[2 * n:]
        x, y, c = _me()
        cps = []
        for a in range(n):
            for k, (fx, fy) in enumerate(_CHIP_FLIPS):
                px, py = x ^ fx, y ^ fy
                cp = pltpu.make_async_remote_copy(
                    src_ref=s_refs[a].at[2 * px + py], dst_ref=out_refs[a].at[k],
                    send_sem=send_sems.at[3 * a + k], recv_sem=recv_sems.at[3 * a + k],
                    device_id=(px, py, c), device_id_type=_MESH)
                cp.start()
                cps.append(cp)
        for cp in cps:
            cp.wait()

    return pl.pallas_call(
        body, name=name, in_specs=[_ANY] * n, out_specs=[_ANY] * n,
        out_shape=[jax.ShapeDtypeStruct((3,) + s.shape[1:], s.dtype) for s in s1s],
        scratch_shapes=[pltpu.SemaphoreType.DMA((3 * n,)), pltpu.SemaphoreType.DMA((3 * n,))],
    )(*s1s)


_HBM = pl.BlockSpec(memory_space=pltpu.HBM)
_SEM = pl.BlockSpec(memory_space=pltpu.SEMAPHORE)
_EFFECT = pltpu.SideEffectType.DATAFLOW_SIDE_EFFECTING


def _push_copies(src_refs, land_refs, send_sems, recv_sems, src_by_peer):
    x, y, c = _me()
    my_id = 4 * x + 2 * y + c
    out = []
    for a in range(len(src_refs)):
        for f in range(1, N_DEV):
            px, py, pc = x ^ (f >> 2), y ^ ((f >> 1) & 1), c ^ (f & 1)
            pid = 4 * px + 2 * py + pc
            src = src_refs[a].at[pid] if src_by_peer else src_refs[a]
            start = pltpu.make_async_remote_copy(
                src_ref=src, dst_ref=land_refs[a].at[my_id], send_sem=send_sems.at[7 * a + f - 1],
                recv_sem=recv_sems.at[7 * a + f - 1], device_id=(px, py, pc), device_id_type=_MESH)
            landed = pltpu.make_async_remote_copy(
                src_ref=src, dst_ref=land_refs[a].at[pid], send_sem=send_sems.at[7 * a + f - 1],
                recv_sem=recv_sems.at[7 * a + f - 1], device_id=(px, py, pc), device_id_type=_MESH)
            out.append((start, landed))
    return out


def _push_start(name, srcs, src_by_peer, after):
    n = len(srcs)
    lands = [jax.ShapeDtypeStruct((N_DEV,) + (s.shape[1:] if src_by_peer else s.shape), s.dtype) for s in srcs]

    def body(*refs):
        src_refs, land_refs = refs[:n], refs[n:2 * n]
        send_sems, recv_sems = refs[2 * n + 1], refs[2 * n + 2]
        token = refs[-1]
        for start, _ in _push_copies(src_refs, land_refs, send_sems, recv_sems, src_by_peer):
            start.start()
        token[...] = jnp.zeros_like(token)

    hbm = lambda a: pltpu.with_memory_space_constraint(a, pltpu.HBM)
    res = pl.pallas_call(
        body, name=name,
        out_shape=(pltpu.SemaphoreType.DMA((7 * n,)), pltpu.SemaphoreType.DMA((7 * n,)),
                   *[pltpu.HBM(s.shape, s.dtype) for s in srcs], *[pltpu.HBM(s.shape, s.dtype) for s in lands],
                   jax.ShapeDtypeStruct((8, LANE), F32)),
        in_specs=[_HBM] * (2 * n) + [_ANY],
        out_specs=(_SEM, _SEM, *[_HBM] * (2 * n), pl.BlockSpec(memory_space=pltpu.VMEM)),
        input_output_aliases={i: 2 + i for i in range(2 * n)},
        compiler_params=pltpu.CompilerParams(has_side_effects=_EFFECT),
    )(*[hbm(s) for s in srcs], *[hbm(lax.empty(s.shape, s.dtype)) for s in lands], after)
    return res[0], res[1], list(res[2:2 + n]), list(res[2 + n:2 + 2 * n]), res[-1]


def _push_wait(name, send_sems, recv_sems, srcs, lands, src_by_peer, after):
    n = len(srcs)

    def body(*refs):
        src_refs, land_refs = refs[:n], refs[n:2 * n]
        s_sems, r_sems = refs[2 * n], refs[2 * n + 1]
        for _, landed in _push_copies(src_refs, land_refs, s_sems, r_sems, src_by_peer):
            landed.wait_send()
            landed.wait_recv()

    res = pl.pallas_call(
        body, name=name,
        out_shape=tuple(pltpu.HBM(s.shape, s.dtype) for s in list(srcs) + list(lands)),
        in_specs=[_HBM] * (2 * n) + [_SEM, _SEM, _ANY],
        out_specs=tuple([_HBM] * (2 * n)),
        input_output_aliases={i: i for i in range(2 * n)},
        compiler_params=pltpu.CompilerParams(has_side_effects=_EFFECT),
    )(*srcs, *lands, send_sems, recv_sems, after)
    return list(res[:n]), list(res[n:])


_SHARDED = (
    ("meta_tokens", 1, (N_META, D_MODEL)),
    ("w_in", 1, (D_MODEL, IN_COLS)),
    ("w_q_b", 1, (Q_LORA, MLA_HEADS * QK_HEAD)),
    ("w_kv_b", 1, (KV_LORA, MLA_HEADS * (QK_NOPE + V_HEAD))),
    ("dn_conv_w", 1, (DN_CONV, 3 * DN_WIDTH)),
    ("w_out", 0, (2 * DN_WIDTH, D_MODEL)),
    ("w_gate", 1, (D_MODEL, D_FF)),
    ("w_up", 1, (D_MODEL, D_FF)),
    ("ffn_conv_w", 1, (FFN_CONV, D_FF)),
    ("w_down", 0, (D_FF, D_MODEL)),
)
_MXU_GATHERED = ("w_in", "w_q_b", "w_kv_b", "w_out", "w_gate", "w_up", "w_down")
_F32_GATHERED = ("meta_tokens", "dn_conv_w", "ffn_conv_w")
_EARLY = ("w_in", "w_q_b", "w_kv_b")
_LATE = ("w_out", "w_gate", "w_up", "w_down")
_TRANSPOSED = ("w_in", "w_q_b", "w_gate", "w_up")
_REPLICATED = (
    ("attn_norm_w", D_MODEL), ("q_a_norm_w", Q_LORA), ("kv_a_norm_w", KV_LORA), ("q_norm_w", QK_HEAD),
    ("k_norm_w", QK_HEAD), ("mla_out_norm_w", V_HEAD), ("dn_A_log", DN_HEADS), ("dn_dt_bias", DN_HEADS),
    ("dn_out_norm_w", DN_DIM), ("ffn_norm_w", D_MODEL), ("ffn_conv_b", D_FF),
)
_PACK_COLS = 1024
_PACK_ROW_MULT = 320
_SMALL_SHAPE = (8, 768)
_SMALL_BLOCK = (8, 512)


def _local_shape(dim, shape):
    return (shape[0] // N_DEV, shape[1]) if dim == 0 else (shape[0], shape[1] // N_DEV)


def _pack_rows(n, mult):
    rows = -(-n // _PACK_COLS)
    return -(-rows // mult) * mult


def _pack(flats, mult, axis=0):
    cat = jnp.concatenate(flats, axis=-1)
    n = cat.shape[-1]
    r = _pack_rows(n, mult)
    pad = [(0, 0)] * (cat.ndim - 1) + [(0, r * _PACK_COLS - n)]
    return jnp.pad(cat, pad).reshape(cat.shape[:-1] + (r, _PACK_COLS))


def _to_blocks(full, dim):
    r, c = full.shape
    if dim == 0:
        return full.reshape(N_DEV, (r // N_DEV) * c)
    return full.reshape(r, N_DEV, c // N_DEV).transpose(1, 0, 2).reshape(N_DEV, r * (c // N_DEV))


def _from_blocks(blocks, dim, shape):
    r, c = shape
    if dim == 0:
        return blocks.reshape(r, c)
    return blocks.reshape(N_DEV, r, c // N_DEV).transpose(1, 0, 2).reshape(r, c)


def _split(flat, sizes):
    out, o = [], 0
    for s in sizes:
        out.append(flat[..., o:o + s])
        o += s
    return out


def _gather_weights(local, names, dtype, mult):
    specs = [s for s in _SHARDED if s[0] in names]
    pack = _pack([local[n].astype(dtype).reshape(-1) for n, _, _ in specs], mult)
    got = _all_gather("gather_" + "_".join(n[:5] for n in names[:2]), pack)
    flat = got.reshape(N_DEV, -1)
    sizes = [math.prod(_local_shape(d, s)) for _, d, s in specs]
    return {n: _from_blocks(p, d, s) for (n, d, s), p in zip(specs, _split(flat, sizes))}


def kernel(x, meta_tokens, attn_norm_w, w_in, q_a_norm_w, w_q_b, kv_a_norm_w, w_kv_b, q_norm_w, k_norm_w, mla_out_norm_w, dn_conv_w, dn_A_log, dn_dt_bias, dn_out_norm_w, w_out, ffn_norm_w, w_gate, w_up, ffn_conv_w, ffn_conv_b, w_down, loss_target, m_meta_tokens, m_attn_norm_w, m_w_in, m_q_a_norm_w, m_w_q_b, m_kv_a_norm_w, m_w_kv_b, m_q_norm_w, m_k_norm_w, m_mla_out_norm_w, m_dn_conv_w, m_dn_A_log, m_dn_dt_bias, m_dn_out_norm_w, m_w_out, m_ffn_norm_w, m_w_gate, m_w_up, m_ffn_conv_w, m_ffn_conv_b, m_w_down, v_meta_tokens, v_attn_norm_w, v_w_in, v_q_a_norm_w, v_w_q_b, v_kv_a_norm_w, v_w_kv_b, v_q_norm_w, v_k_norm_w, v_mla_out_norm_w, v_dn_conv_w, v_dn_A_log, v_dn_dt_bias, v_dn_out_norm_w, v_w_out, v_ffn_norm_w, v_w_gate, v_w_up, v_ffn_conv_w, v_ffn_conv_b, v_w_down):
    names = [n for n, _, _ in _SHARDED] + [n for n, _ in _REPLICATED]
    given = dict(locals())
    two_d = lambda a: a.reshape(a.shape[-2:])
    view = lambda a, n: two_d(a).T if n in _TRANSPOSED else two_d(a)
    wl = {n: view(given[n], n) for n in names}
    ml = {n: view(given["m_" + n], n) for n in names}
    vl = {n: view(given["v_" + n], n) for n in names}
    out_shapes = {n: given[n].shape for n in names}

    spec = {n: (d, s) for n, d, s in _SHARDED}
    small_sizes = [math.prod(_local_shape(*spec[n])) for n in _F32_GATHERED]

    def small_block(d):
        cat = jnp.concatenate([d[n].reshape(d[n].shape[:-2] + (-1,)) for n in _F32_GATHERED], axis=-1)
        pad = [(0, 0)] * (cat.ndim - 1) + [(0, math.prod(_SMALL_BLOCK) - cat.shape[-1])]
        return jnp.pad(cat, pad).reshape(cat.shape[:-1] + _SMALL_BLOCK)

    def shard(n):
        return wl[n].astype(_MXU)

    def from_slots(n, blocks):
        d, s = spec[n]
        if d == 0 or n in _TRANSPOSED:
            return blocks.reshape(-1, blocks.shape[-1])
        return blocks.transpose(1, 0, 2).reshape(s)

    my_id = 4 * lax.axis_index("x") + 2 * lax.axis_index("y") + lax.axis_index("c")
    got = _all_gather_many("gather_early", [shard(n) for n in _EARLY] + [small_block(wl)])
    full = {n: a for n, a in wl.items() if n not in _LATE}
    for n, blocks in zip(_EARLY, got):
        full[n] = from_slots(n, blocks)
    for n, p in zip(_F32_GATHERED, _split(got[-1].reshape(N_DEV, -1), small_sizes)):
        full[n] = _from_blocks(p, *spec[n])
    late_own = [shard(n) for n in _LATE]
    l_send, l_recv, l_src, l_land, token = _push_start("gather_late_start", late_own, False, got[-1])

    def late_weights(after):
        _, lands = _push_wait("gather_late_wait", l_send, l_recv, l_src, l_land, False, after)
        out = {}
        for n, land, own in zip(_LATE, lands, late_own):
            out[n] = from_slots(n, lax.dynamic_update_slice(land, own[None], (my_id, 0, 0))).astype(_MXU)
        return out

    def dest_blocks(n, a):
        d, s = spec[n]
        r, c = _local_shape(d, s)
        if n in _TRANSPOSED:
            return a.reshape(N_DEV, c, r)
        return a.reshape(N_DEV, r, c) if d == 0 else a.reshape(r, N_DEV, c).transpose(1, 0, 2)

    pushed = []

    def grads_ready(g, names):
        nat = _grads_to_natural({n: g[n] for n in names})
        blocks = [dest_blocks(n, nat[n]).astype(_MXU) for n in names]
        sends, recvs, srcs, lands, tok = _push_start("rs_" + names[0] + "_start", blocks, True, token)
        pushed.append((names, sends, recvs, srcs, lands))
        return tok

    seq = x.shape[1]
    tp = ROW0 + seq
    h0 = jnp.concatenate([jnp.zeros((PAD, D_MODEL), F32), full["meta_tokens"], x[0]], axis=0)
    tgt = jnp.concatenate([jnp.zeros((ROW0, D_MODEL), F32), loss_target[0]], axis=0)
    loss, dh0, g = _local_step(h0, tgt, _prepare(full, tp), token, late_weights, grads_ready)
    g = _grads_to_natural(g)
    g["meta_tokens"] = dh0[PAD:ROW0]
    grad_x = dh0[ROW0:][None]

    big = [{}, {}, {}, {}]
    rep_names = [n for n, _ in _REPLICATED]
    pieces = [g[n].reshape(-1) for n in rep_names] + [loss[0, :1]] + [g[n].reshape(-1) for n in _F32_GATHERED]
    sizes = [p.shape[0] for p in pieces]
    cat = jnp.concatenate(pieces)
    cols = -(-cat.shape[0] // (8 * LANE)) * LANE
    mine = jnp.pad(cat, (0, 8 * cols - cat.shape[0])).reshape(8, cols)
    everyone = _all_gather("gather_small_grads", mine)
    total = _sum_parts("sum_small_grads", [(everyone, d) for d in range(N_DEV)])
    tot = dict(zip(rep_names + ["loss"] + list(_F32_GATHERED), _split(total.reshape(-1), sizes)))

    def small(d):
        cat = jnp.concatenate([d[n].reshape(-1) for n in rep_names])
        return jnp.pad(cat, (0, math.prod(_SMALL_SHAPE) - cat.shape[0])).reshape(_SMALL_SHAPE)

    sm = _adam("adam_replicated", [(small(tot)[None], 0)], small(wl), small(ml), small(vl))
    sm = [dict(zip(rep_names, _split(a.reshape(-1), [n for _, n in _REPLICATED]))) for a in sm]
    mine_of = {}
    for n in _F32_GATHERED:
        d, s = spec[n]
        r, c = _local_shape(d, s)
        mine_of[n] = lax.dynamic_slice(tot[n].reshape(s), (0, my_id * c), (r, c))
    res = _adam("adam_small_sharded", [(small_block(mine_of)[None], 0)], small_block(wl), small_block(ml),
                small_block(vl))
    for kind, a in enumerate(res):
        big[kind].update(zip(_F32_GATHERED, _split(a.reshape(-1), small_sizes)))

    for names, sends, recvs, srcs, lands in pushed:
        srcs, lands = _push_wait("rs_" + names[0] + "_wait", sends, recvs, srcs, lands, True, dh0)
        for n, src, land in zip(names, srcs, lands):
            parts = [(src, my_id)] + [(land, my_id ^ f) for f in range(1, N_DEV)]
            for kind, a in enumerate(_adam("adam_" + n, parts, wl[n], ml[n], vl[n])):
                big[kind][n] = a

    outs = [tot["loss"].reshape(()), grad_x]
    for kind in range(4):
        for n in ("meta_tokens", "attn_norm_w", "w_in", "q_a_norm_w", "w_q_b", "kv_a_norm_w", "w_kv_b", "q_norm_w",
                  "k_norm_w", "mla_out_norm_w", "dn_conv_w", "dn_A_log", "dn_dt_bias", "dn_out_norm_w", "w_out",
                  "ffn_norm_w", "w_gate", "w_up", "ffn_conv_w", "ffn_conv_b", "w_down"):
            src = big[kind] if n in big[kind] else sm[kind]
            a = src[n].T if n in _TRANSPOSED else src[n]
            outs.append(a.reshape(out_shapes[n]))
    return tuple(outs)
```

```python
import functools
import math

import jax
import jax.numpy as jnp
from jax import lax
from jax.experimental import pallas as pl
from jax.experimental.pallas import tpu as pltpu

F32 = jnp.float32
BF16 = jnp.bfloat16
_MXU = jnp.bfloat16
_HI = lax.Precision.HIGHEST

D_MODEL = 1024
N_META = 16
PAD = 112
ROW0 = PAD + N_META
MLA_HEADS = 4
QK_NOPE = 128
QK_ROPE = 64
QK_HEAD = QK_NOPE + QK_ROPE
V_HEAD = 128
Q_LORA = 256
KV_LORA = 256
ROPE_THETA = 10000.0
DN_HEADS = 4
DN_DIM = 128
DN_WIDTH = DN_HEADS * DN_DIM
DN_CONV = 4
DN_CHUNK = 64
D_FF = 2816
FFN_CONV = 3
EPS = 1e-6
HP = 256
C_QKV = 0
C_Z = 1536
C_QL = 2048
C_KVL = 2304
C_KPE = 2560
C_AB = 2688
IN_P = 2816
IN_COLS = 2632

ADAM_LR = 0.001
ADAM_B1 = 0.9
ADAM_B2 = 0.999
ADAM_EPS = 1e-08
ADAM_WD = 0.01
ADAM_STEP = 10

N_DEV = 8
TM = 128
LANE = 128
VMEM_LIMIT = 56 * 1024 * 1024
NEG = -1e30


def _dot(a, b, dims, hp=False):
    if hp:
        return lax.dot_general(a.astype(F32), b.astype(F32), (dims, ((), ())),
                               precision=lax.Precision.HIGH if hp == "3x" else _HI, preferred_element_type=F32)
    return lax.dot_general(a.astype(_MXU), b.astype(_MXU), (dims, ((), ())),
                           preferred_element_type=F32)


def _nn(a, b, hp=False):
    return _dot(a, b, ((1,), (0,)), hp)


def _nt(a, b, hp=False):
    return _dot(a, b, ((1,), (1,)), hp)


def _tn(a, b, hp=False):
    return _dot(a, b, ((0,), (0,)), hp)


def _sigmoid(x):
    return 1.0 / (1.0 + jnp.exp(-x))


def _rms_fwd(x, w, n):
    r = lax.rsqrt(jnp.sum(x * x, axis=-1, keepdims=True) * (1.0 / n) + EPS)
    return x * r * w, r


def _rms_bwd(x, w, dy, n):
    r = lax.rsqrt(jnp.sum(x * x, axis=-1, keepdims=True) * (1.0 / n) + EPS)
    xh = x * r
    gy = dy * w
    dx = r * (gy - xh * (jnp.sum(gy * xh, axis=-1, keepdims=True) * (1.0 / n)))
    return dx, dy * xh


def _rowsum(x):
    return jnp.sum(x, axis=0, keepdims=True)


def _row_ids(i, tm):
    return i * tm + lax.broadcasted_iota(jnp.int32, (tm, 1), 0)


def _shift_down(ext, s, tm):
    if s == 0:
        return ext[8:8 + tm]
    return pltpu.roll(ext, s, 0)[8:8 + tm]


def _shift_up(ext, s, tm):
    if s == 0:
        return ext[0:tm]
    return pltpu.roll(ext, tm + 8 - s, 0)[0:tm]


def _conv_fwd(x, halo_prev, w, width):
    tm = x.shape[0]
    ext = jnp.concatenate([halo_prev, x], axis=0)
    y = None
    for j in range(width):
        t = w[j:j + 1, :] * _shift_down(ext, width - 1 - j, tm)
        y = t if y is None else y + t
    return y


def _conv_bwd_x(dy, halo_next, w, width):
    tm = dy.shape[0]
    ext = jnp.concatenate([dy, halo_next], axis=0)
    dx = None
    for j in range(width):
        t = w[j:j + 1, :] * _shift_up(ext, width - 1 - j, tm)
        dx = t if dx is None else dx + t
    return dx


def _conv_bwd_w(dy, x, halo_prev, width):
    tm = dy.shape[0]
    ext = jnp.concatenate([halo_prev, x], axis=0)
    rows = [_rowsum(dy * _shift_down(ext, width - 1 - j, tm)) for j in range(width)]
    rows += [jnp.zeros_like(rows[0])] * (8 - width)
    return jnp.concatenate(rows, axis=0)


def _softplus(x):
    e = jnp.exp(-jnp.abs(x))
    u = 1.0 + e
    l1p = jnp.where(u == 1.0, e, jnp.log(u) * e / jnp.where(u == 1.0, 1.0, u - 1.0))
    return jnp.maximum(x, 0.0) + l1p


def _swap_halves(x):
    lane = lax.broadcasted_iota(jnp.int32, x.shape, 1)
    return jnp.where(lane < 32, pltpu.roll(x, 96, 1), jnp.where(lane < 64, pltpu.roll(x, 32, 1), 0.0))


class _In:
    def __init__(self, arr, width=None, cb=0, kind="cur"):
        self.arr, self.kind = arr, kind
        self.width = arr.shape[1] if width is None else width
        self.cb = cb


def _tile_spec(t, tm, tp):
    r8 = tm // 8
    if t.kind == "cur":
        return pl.BlockSpec((tm, t.width), lambda i, cb=t.cb: (i, cb))
    if t.kind == "prev":
        return pl.BlockSpec((8, t.width), lambda i, cb=t.cb: (jnp.maximum(i * r8 - 1, 0), cb))
    return pl.BlockSpec((8, t.width), lambda i, cb=t.cb: (jnp.minimum((i + 1) * r8, tp // 8 - 1), cb))


def _rows(name, fn, tiled, full, outs, accs=(), tm=TM):
    tp = tiled[0].arr.shape[0]
    nt = tp // tm
    r8 = tm // 8
    n_in = len(tiled) + len(full)
    n_out = len(outs)

    def body(*refs):
        i = pl.program_id(0)
        vals = [r[...] for r in refs[:n_in]]
        o_t, o_a = fn(i, *vals)
        for r, v in zip(refs[n_in:n_in + n_out], o_t):
            r[...] = v.astype(r.dtype)
        for r, v in zip(refs[n_in + n_out:], o_a):
            @pl.when(i == 0)
            def _():
                r[...] = v

            @pl.when(i > 0)
            def _():
                r[...] += v

    in_specs = [_tile_spec(t, tm, tp) for t in tiled]
    in_specs += [pl.BlockSpec(a.shape, lambda i, nd=a.ndim: (0,) * nd) for a in full]
    out_specs = [pl.BlockSpec((tm, w), lambda i: (i, 0)) for w, _ in outs]
    out_specs += [pl.BlockSpec((r, w), lambda i: (0, 0)) for r, w in accs]
    out_shape = [jax.ShapeDtypeStruct((tp, w), dt) for w, dt in outs]
    out_shape += [jax.ShapeDtypeStruct((r, w), F32) for r, w in accs]
    res = pl.pallas_call(
        body, name=name, grid=(nt,), in_specs=in_specs, out_specs=out_specs, out_shape=out_shape,
        compiler_params=pltpu.CompilerParams(dimension_semantics=("arbitrary",), vmem_limit_bytes=VMEM_LIMIT),
    )(*[t.arr for t in tiled], *full)
    return res


def _pick(n, cap, mult):
    best = None
    for d in range(mult, min(n, cap) + 1, mult):
        if n % d == 0:
            best = d
    assert best is not None, (n, cap, mult)
    return best


_ANY_SPEC = pl.BlockSpec(memory_space=pl.ANY)


def _mm(name, a, b, mode, out_dtype=F32, resid=None, after=None):
    if mode == "tn":
        m, k = a.shape
        n = b.shape[1]
        tk = _pick(k, 512, 128)
        tn = _pick(n, 1408, 128)

        def body_tn(a_ref, b_ref, o_ref):
            o_ref[...] = _tn(a_ref[...], b_ref[...]).astype(o_ref.dtype)

        return pl.pallas_call(
            body_tn, name=name, grid=(n // tn, k // tk),
            in_specs=[pl.BlockSpec((m, tk), lambda j, p: (0, p)),
                      pl.BlockSpec((m, tn), lambda j, p: (0, j))],
            out_specs=pl.BlockSpec((tk, tn), lambda j, p: (p, j)),
            out_shape=jax.ShapeDtypeStruct((k, n), out_dtype),
            compiler_params=pltpu.CompilerParams(
                dimension_semantics=("parallel", "parallel"), vmem_limit_bytes=VMEM_LIMIT),
        )(a, b)

    m, k = a.shape
    n = b.shape[1] if mode == "nn" else b.shape[0]
    tn = _pick(n, 1408, 128)
    tm = _pick(m, 1152, 16)
    dotf = _nn if mode == "nn" else _nt

    def body(*refs):
        a_ref, b_ref, o_ref = refs[0], refs[1], refs[-1]
        acc = dotf(a_ref[...], b_ref[...])
        if resid is not None:
            acc = refs[2][...] + acc
        o_ref[...] = acc.astype(o_ref.dtype)

    b_spec = (pl.BlockSpec((k, tn), lambda j, i: (0, j)) if mode == "nn"
              else pl.BlockSpec((tn, k), lambda j, i: (j, 0)))
    in_specs = [pl.BlockSpec((tm, k), lambda j, i: (i, 0)), b_spec]
    args = [a, b]
    if resid is not None:
        in_specs.append(pl.BlockSpec((tm, tn), lambda j, i: (i, j)))
        args.append(resid)
    if after is not None:
        in_specs.append(_ANY_SPEC)
        args.append(after)
    return pl.pallas_call(
        body, name=name, grid=(n // tn, m // tm), in_specs=in_specs,
        out_specs=pl.BlockSpec((tm, tn), lambda j, i: (i, j)),
        out_shape=jax.ShapeDtypeStruct((m, n), out_dtype),
        compiler_params=pltpu.CompilerParams(
            dimension_semantics=("parallel", "parallel"), vmem_limit_bytes=VMEM_LIMIT),
    )(*args)


def _norm_mm(name, x, norm_w, b, mode="nt", x_cb=0, after=None):
    m = x.shape[0]
    k = norm_w.shape[1]
    n = b.shape[0] if mode == "nt" else b.shape[1]
    tn = _pick(n, 1408, 128)
    tm = _pick(m, 1152, 16)
    dotf = _nt if mode == "nt" else _nn
    extra = [] if after is None else [after]

    def body(x_ref, w_ref, b_ref, *rest):
        o_ref, u_ref = rest[-2:]

        @pl.when(pl.program_id(1) == 0)
        def _():
            u_ref[...] = _rms_fwd(x_ref[...], w_ref[...], k)[0].astype(u_ref.dtype)

        o_ref[...] = dotf(u_ref[...], b_ref[...])

    b_spec = (pl.BlockSpec((tn, k), lambda i, j: (j, 0)) if mode == "nt"
              else pl.BlockSpec((k, tn), lambda i, j: (0, j)))
    return pl.pallas_call(
        body, name=name, grid=(m // tm, n // tn),
        in_specs=[pl.BlockSpec((tm, k), lambda i, j: (i, x_cb)), pl.BlockSpec((1, k), lambda i, j: (0, 0)),
                  b_spec] + [_ANY_SPEC] * len(extra),
        out_specs=[pl.BlockSpec((tm, tn), lambda i, j: (i, j)), pl.BlockSpec((tm, k), lambda i, j: (i, 0))],
        out_shape=[jax.ShapeDtypeStruct((m, n), F32), jax.ShapeDtypeStruct((m, k), _MXU)],
        compiler_params=pltpu.CompilerParams(
            dimension_semantics=("arbitrary", "arbitrary"), vmem_limit_bytes=VMEM_LIMIT),
    )(x, norm_w, b, *extra)


def _mm_rows(name, a, b, mode, fn, tiled, full, outs, accs=(), tm_cap=576):
    m = a.shape[0]
    tm = _pick(m, tm_cap, 16)
    dotf = _nn if mode == "nn" else _nt
    n_in = len(tiled) + len(full)
    n_out = len(outs)

    def body(*refs):
        i = pl.program_id(0)
        vals = [r[...] for r in refs[2:2 + n_in]]
        o_t, o_a = fn(i, dotf(refs[0][...], refs[1][...]), *vals)
        for r, v in zip(refs[2 + n_in:2 + n_in + n_out], o_t):
            r[...] = v.astype(r.dtype)
        for r, v in zip(refs[2 + n_in + n_out:], o_a):
            @pl.when(i == 0)
            def _():
                r[...] = v

            @pl.when(i > 0)
            def _():
                r[...] += v

    whole = lambda x: pl.BlockSpec(x.shape, lambda i, nd=x.ndim: (0,) * nd)
    in_specs = [pl.BlockSpec((tm, a.shape[1]), lambda i: (i, 0)), whole(b)]
    in_specs += [_tile_spec(t, tm, m) for t in tiled]
    in_specs += [whole(x) for x in full]
    out_specs = [pl.BlockSpec((tm, w), lambda i: (i, 0)) for w, _ in outs]
    out_specs += [pl.BlockSpec((r, w), lambda i: (0, 0)) for r, w in accs]
    out_shape = [jax.ShapeDtypeStruct((m, w), dt) for w, dt in outs]
    out_shape += [jax.ShapeDtypeStruct((r, w), F32) for r, w in accs]
    return pl.pallas_call(
        body, name=name, grid=(m // tm,), in_specs=in_specs, out_specs=out_specs, out_shape=out_shape,
        compiler_params=pltpu.CompilerParams(dimension_semantics=("arbitrary",), vmem_limit_bytes=VMEM_LIMIT),
    )(a, b, *[t.arr for t in tiled], *full)


ATTN_Q_TILES = 4


def _attn_probs(q, k, row0):
    tq, tp = q.shape[0], k.shape[0]
    s = _nt(q, k) * (1.0 / math.sqrt(QK_HEAD))
    row = row0 + lax.broadcasted_iota(jnp.int32, (tq, tp), 0)
    col = lax.broadcasted_iota(jnp.int32, (tq, tp), 1)
    ok = (col <= row) & (col >= PAD)
    s = jnp.where(ok, s, NEG)
    m = jnp.max(s, axis=-1, keepdims=True)
    e = jnp.exp(s - m)
    e = jnp.where(ok, e, 0.0)
    l = jnp.sum(e, axis=-1, keepdims=True)
    return e / jnp.maximum(l, 1e-30)


def _attn_fwd(q, k, v):
    tp = q.shape[0]
    tq = tp // ATTN_Q_TILES

    def body(q_ref, k_ref, v_ref, o_ref):
        for i in range(ATTN_Q_TILES):
            rows = slice(i * tq, (i + 1) * tq)
            keys = slice(0, (i + 1) * tq)
            p = _attn_probs(q_ref[rows, :], k_ref[keys, :], i * tq)
            o_ref[rows, :] = _nn(p, v_ref[keys, :])

    return pl.pallas_call(
        body, name="attn_fwd", grid=(MLA_HEADS,),
        in_specs=[pl.BlockSpec((tp, HP), lambda h: (0, h)),
                  pl.BlockSpec((tp, HP), lambda h: (0, h)),
                  pl.BlockSpec((tp, V_HEAD), lambda h: (0, h))],
        out_specs=pl.BlockSpec((tp, V_HEAD), lambda h: (0, h)),
        out_shape=jax.ShapeDtypeStruct((tp, MLA_HEADS * V_HEAD), F32),
        compiler_params=pltpu.CompilerParams(dimension_semantics=("parallel",), vmem_limit_bytes=VMEM_LIMIT),
    )(q, k, v)


def _attn_bwd(q, k, v, do):
    tp = q.shape[0]
    tq = tp // ATTN_Q_TILES

    def body(q_ref, k_ref, v_ref, do_ref, dq_ref, dk_ref, dv_ref):
        for i in reversed(range(ATTN_Q_TILES)):
            rows = slice(i * tq, (i + 1) * tq)
            keys = slice(0, (i + 1) * tq)
            qb = q_ref[rows, :]
            kk = k_ref[keys, :]
            dob = do_ref[rows, :]
            p = _attn_probs(qb, kk, i * tq)
            dp = _nt(dob, v_ref[keys, :])
            delta = jnp.sum(p * dp, axis=-1, keepdims=True)
            ds = p * (dp - delta) * (1.0 / math.sqrt(QK_HEAD))
            dq_ref[rows, :] = _nn(ds, kk)
            if i == ATTN_Q_TILES - 1:
                dk_ref[...] = _tn(ds, qb)
                dv_ref[...] = _tn(p, dob)
            else:
                dk_ref[keys, :] += _tn(ds, qb)
                dv_ref[keys, :] += _tn(p, dob)

    full = lambda w: pl.BlockSpec((tp, w), lambda h: (0, h))
    return pl.pallas_call(
        body, name="attn_bwd", grid=(MLA_HEADS,),
        in_specs=[full(HP), full(HP), full(V_HEAD), full(V_HEAD)],
        out_specs=[full(HP), full(HP), full(V_HEAD)],
        out_shape=[jax.ShapeDtypeStruct((tp, MLA_HEADS * HP), F32),
                   jax.ShapeDtypeStruct((tp, MLA_HEADS * HP), F32),
                   jax.ShapeDtypeStruct((tp, MLA_HEADS * V_HEAD), F32)],
        compiler_params=pltpu.CompilerParams(dimension_semantics=("parallel",), vmem_limit_bytes=VMEM_LIMIT),
    )(q, k, v, do)


def _gdn_consts():
    c = DN_CHUNK
    r = lax.broadcasted_iota(jnp.int32, (c, c), 0)
    cc = lax.broadcasted_iota(jnp.int32, (c, c), 1)
    incl = r >= cc
    strict = r > cc
    return incl, strict


def _cumsum_rows(x, reverse=False):
    c = x.shape[0]
    row = lax.broadcasted_iota(jnp.int32, x.shape, 0)
    s = 1
    while s < c:
        if reverse:
            x = x + jnp.where(row < c - s, pltpu.roll(x, c - s, 0), 0.0)
        else:
            x = x + jnp.where(row >= s, pltpu.roll(x, s, 0), 0.0)
        s *= 2
    return x


def _each(fn, *lists):
    return [fn(*a) for a in zip(*lists)]


def _interleave(chains):
    chains = list(chains)
    while chains:
        for ch in list(chains):
            try:
                next(ch)
            except StopIteration:
                chains.remove(ch)


def _gdn_chunk_common(q_ref, k_ref, v_ref, g_ref, b_ref):
    c = DN_CHUNK
    incl, strict = _gdn_consts()
    sls = [slice(DN_DIM * h, DN_DIM * (h + 1)) for h in range(DN_HEADS)]
    q = [q_ref[:, sl] * (1.0 / math.sqrt(DN_DIM)) for sl in sls]
    k = [k_ref[:, sl] for sl in sls]
    v = [v_ref[:, sl] for sl in sls]
    g = [g_ref[:, sl] for sl in sls]
    beta = [b_ref[:, sl] for sl in sls]
    gc = [_cumsum_rows(x) for x in g]
    grow = [x.T[:c, :] for x in gc]
    kb = _each(jnp.multiply, k, beta)
    kk = _each(_nt, kb, k)
    qk = _each(_nt, q, k)
    gam = [jnp.exp(x) for x in gc]
    g_last = [_rowsum(x) for x in g]
    dm = [jnp.exp(jnp.where(incl, x[:, :c] - y, NEG)) for x, y in zip(gc, grow)]
    vb = _each(jnp.multiply, v, beta)
    kbg = _each(jnp.multiply, kb, gam)
    ek = [jnp.exp(x - y) for x, y in zip(g_last, gc)]
    kd = _each(jnp.multiply, k, ek)
    return dict(q=q, k=k, v=v, beta=beta, gc=gc, gam=gam, g_last=g_last, dm=dm, kb=kb, vb=vb,
                kbg=kbg, kk=kk, ek=ek, kd=kd, qk=qk, incl=incl, strict=strict, sls=sls)


def _gdn_fwd(q, k, v, g, beta):
    tp = q.shape[0]
    c = DN_CHUNK
    nch = tp // c

    def body(q_ref, k_ref, v_ref, g_ref, b_ref, o_ref, s_ref, t_ref, s_scr):
        @pl.when(pl.program_id(0) == 0)
        def _():
            s_scr[...] = jnp.zeros_like(s_scr)

        eye = (lax.broadcasted_iota(jnp.int32, (c, c), 0) == lax.broadcasted_iota(jnp.int32, (c, c), 1)).astype(F32)
        x = _gdn_chunk_common(q_ref, k_ref, v_ref, g_ref, b_ref)
        heads = range(DN_HEADS)
        s = [s_scr[h] for h in heads]
        bp = [-jnp.where(x["strict"], kk * dm, 0.0) for kk, dm in zip(x["kk"], x["dm"])]
        t = [eye + b for b in bp]
        for _ in range(5):
            bp = [_nn(b, b, hp="3x") for b in bp]
            t = [tt + _nn(tt, b, hp="3x") for tt, b in zip(t, bp)]
        u = _each(_nn, t, x["vb"])
        w = _each(_nn, t, x["kbg"])
        v_new = [uu - _nn(ww, ss) for uu, ww, ss in zip(u, w, s)]
        o = [_nn(q * gam, ss) + _nn(qk * dm, vn)
             for q, gam, ss, qk, dm, vn in zip(x["q"], x["gam"], s, x["qk"], x["dm"], v_new)]
        s_new = [ss * jnp.exp(gl) + _tn(kd, vn) for ss, gl, kd, vn in zip(s, x["g_last"], x["kd"], v_new)]
        for h in heads:
            s_ref[h, 0] = s[h]
            t_ref[h, 0] = t[h]
            o_ref[:, x["sls"][h]] = o[h]
            s_scr[h] = s_new[h]

    rb = lambda n: (n, 0)
    return pl.pallas_call(
        body, name="gdn_fwd", grid=(nch,),
        in_specs=[pl.BlockSpec((c, DN_WIDTH), rb)] * 5,
        out_specs=[pl.BlockSpec((c, DN_WIDTH), rb),
                   pl.BlockSpec((DN_HEADS, 1, DN_DIM, DN_DIM), lambda n: (0, n, 0, 0)),
                   pl.BlockSpec((DN_HEADS, 1, c, c), lambda n: (0, n, 0, 0))],
        out_shape=[jax.ShapeDtypeStruct((tp, DN_WIDTH), F32),
                   jax.ShapeDtypeStruct((DN_HEADS, nch, DN_DIM, DN_DIM), F32),
                   jax.ShapeDtypeStruct((DN_HEADS, nch, c, c), F32)],
        scratch_shapes=[pltpu.VMEM((DN_HEADS, DN_DIM, DN_DIM), F32)],
        compiler_params=pltpu.CompilerParams(dimension_semantics=("arbitrary",), vmem_limit_bytes=VMEM_LIMIT),
    )(q, k, v, g, beta)


def _gdn_bwd(q, k, v, g, beta, s_all, t_all, do):
    tp = q.shape[0]
    c = DN_CHUNK
    nch = tp // c

    def body(q_ref, k_ref, v_ref, g_ref, b_ref, s_ref, t_ref, do_ref,
             dq_ref, dk_ref, dv_ref, dg_ref, db_ref, ds_scr):
        @pl.when(pl.program_id(0) == 0)
        def _():
            ds_scr[...] = jnp.zeros_like(ds_scr)

        xs = _gdn_chunk_common(q_ref, k_ref, v_ref, g_ref, b_ref)

        def chain(h):
            x = {key: (val[h] if isinstance(val, list) else val) for key, val in xs.items()}
            sl = x["sls"]
            qs, kx, vx, beta_, gam, dm = x["q"], x["k"], x["v"], x["beta"], x["gam"], x["dm"]
            kb, vb, kbg, kd, ek = x["kb"], x["vb"], x["kbg"], x["kd"], x["ek"]
            t = t_ref[h, 0]
            s = s_ref[h, 0]
            dsn = ds_scr[h]
            dob = do_ref[:, sl]
            eg_last = jnp.exp(x["g_last"])
            u = _nn(t, vb)
            w = _nn(t, kbg)
            mqk = x["qk"] * dm
            qd = qs * gam
            dqd = _nt(dob, s)
            dkd_pre = _nn(kd, dsn)
            yield
            v_new = u - _nn(w, s)
            dv_new = _tn(mqk, dob) + dkd_pre
            dq = dqd * gam
            dgam = jnp.sum(dqd * qs, axis=1, keepdims=True)
            yield
            ds_new = _tn(qd, dob) + eg_last * dsn - _tn(w, dv_new)
            dmm = jnp.where(x["incl"], _nt(dob, v_new), 0.0)
            dkd = _nt(v_new, dsn)
            dw = -_nt(dv_new, s)
            dvb = _tn(t, dv_new)
            dt = _nt(dv_new, vb)
            yield
            dqk = dmm * dm
            e_mat = dmm * mqk
            dq = dq + _nn(dqk, kx)
            dk = _tn(dqk, qs) + dkd * ek
            e1 = jnp.sum(dkd * kd, axis=1, keepdims=True)
            dgc = -e1
            dg_last = jnp.sum(e1) + eg_last * jnp.sum(s * dsn)
            dt = dt + _nt(dw, kbg)
            dkbg = _tn(t, dw)
            yield
            tdt = _tn(t, dt, hp="3x")
            yield
            da = jnp.where(x["strict"], -_nt(tdt, t, hp="3x"), 0.0)
            yield
            dkk = da * dm
            e_mat = e_mat + da * x["kk"] * dm
            dkb = _nn(dkk, kx) + dkbg * gam
            dk = dk + _tn(dkk, kb)
            dgam = dgam + jnp.sum(dkbg * kb, axis=1, keepdims=True)
            yield
            dk = dk + dkb * beta_
            dbeta = jnp.sum(dkb * kx, axis=1, keepdims=True) + jnp.sum(dvb * vx, axis=1, keepdims=True)
            dv = dvb * beta_
            dgc = dgc + jnp.sum(e_mat, axis=1, keepdims=True) + dgam * gam
            dgc = dgc - jnp.sum(e_mat.T, axis=1, keepdims=True)
            yield
            dg = _cumsum_rows(dgc, reverse=True) + dg_last
            yield
            ds_scr[h] = ds_new
            dq_ref[:, sl] = dq * (1.0 / math.sqrt(DN_DIM))
            dk_ref[:, sl] = dk
            dv_ref[:, sl] = dv
            dg_ref[:, sl] = dg
            db_ref[:, sl] = jnp.broadcast_to(dbeta, (c, LANE))

        _interleave([chain(h) for h in range(DN_HEADS)])

    rb = lambda n: (nch - 1 - n, 0)
    hs = lambda n: (0, nch - 1 - n, 0, 0)
    return pl.pallas_call(
        body, name="gdn_bwd", grid=(nch,),
        in_specs=[pl.BlockSpec((c, DN_WIDTH), rb)] * 5
        + [pl.BlockSpec((DN_HEADS, 1, DN_DIM, DN_DIM), hs), pl.BlockSpec((DN_HEADS, 1, c, c), hs),
           pl.BlockSpec((c, DN_WIDTH), rb)],
        out_specs=[pl.BlockSpec((c, DN_WIDTH), rb)] * 5,
        out_shape=[jax.ShapeDtypeStruct((tp, DN_WIDTH), F32)] * 5,
        scratch_shapes=[pltpu.VMEM((DN_HEADS, DN_DIM, DN_DIM), F32)],
        compiler_params=pltpu.CompilerParams(dimension_semantics=("arbitrary",), vmem_limit_bytes=VMEM_LIMIT),
    )(q, k, v, g, beta, s_all, t_all, do)


def _silu_parts(x):
    s = _sigmoid(x)
    return x * s, s * (1.0 + x * (1.0 - s))


def _f_rms_cast(i, x, w):
    y, _ = _rms_fwd(x, w, x.shape[1])
    return (y,), ()


def _f_rms_bwd_add(i, x, dy, dres, w, *, mask_pad):
    dx, dwr = _rms_bwd(x, w, dy, x.shape[1])
    out = dres + dx
    if mask_pad:
        out = jnp.where(_row_ids(i, x.shape[0]) >= PAD, out, 0.0)
    return (out,), (_rowsum(dwr),)


def _f_lat_norm(i, ql, kvl, qw, kvw):
    return (_rms_fwd(ql, qw, Q_LORA)[0], _rms_fwd(kvl, kvw, KV_LORA)[0]), ()


def _f_lat_norm_bwd(i, ql, kvl, dqn, dkvn, qw, kvw):
    dq, dqw = _rms_bwd(ql, qw, dqn, Q_LORA)
    dk, dkw = _rms_bwd(kvl, kvw, dkvn, KV_LORA)
    return (dq, dk), (_rowsum(dqw), _rowsum(dkw))


def _rope(x, cos, sin_s):
    return x * cos + _swap_halves(x) * sin_s


def _rope_t(dy, cos, sin_s):
    return dy * cos + _swap_halves(dy * sin_s)


def _f_mla_qk(i, qf, kvf, kpe, cos, sin_s, qw, kw):
    qs, ks, vs = [], [], []
    for h in range(MLA_HEADS):
        qn, _ = _rms_fwd(qf[:, HP * h:HP * (h + 1)], qw, QK_HEAD)
        qs += [qn[:, :QK_NOPE], _rope(qn[:, QK_NOPE:], cos, sin_s)]
        kh = jnp.concatenate([kvf[:, HP * h:HP * h + QK_NOPE], kpe], axis=1)
        kn, _ = _rms_fwd(kh, kw, QK_HEAD)
        ks += [kn[:, :QK_NOPE], _rope(kn[:, QK_NOPE:], cos, sin_s)]
        vs.append(kvf[:, HP * h + QK_NOPE:HP * (h + 1)])
    return (jnp.concatenate(qs, axis=1), jnp.concatenate(ks, axis=1), jnp.concatenate(vs, axis=1)), ()


def _f_mla_qk_bwd(i, qf, kvf, kpe, cos, sin_s, dq, dk, dv, qw, kw):
    dqf, dkvf = [], []
    dkpe = None
    dqw = None
    dkw = None
    for h in range(MLA_HEADS):
        dqh = dq[:, HP * h:HP * (h + 1)]
        dqn = jnp.concatenate([dqh[:, :QK_NOPE], _rope_t(dqh[:, QK_NOPE:], cos, sin_s)], axis=1)
        dx, dwr = _rms_bwd(qf[:, HP * h:HP * (h + 1)], qw, dqn, QK_HEAD)
        dqf.append(dx)
        dqw = _rowsum(dwr) if dqw is None else dqw + _rowsum(dwr)
        dkh = dk[:, HP * h:HP * (h + 1)]
        dkn = jnp.concatenate([dkh[:, :QK_NOPE], _rope_t(dkh[:, QK_NOPE:], cos, sin_s)], axis=1)
        kh = jnp.concatenate([kvf[:, HP * h:HP * h + QK_NOPE], kpe], axis=1)
        dx, dwr = _rms_bwd(kh, kw, dkn, QK_HEAD)
        dkvf += [dx[:, :QK_NOPE], dv[:, V_HEAD * h:V_HEAD * (h + 1)]]
        dkpe = dx[:, QK_NOPE:] if dkpe is None else dkpe + dx[:, QK_NOPE:]
        dkw = _rowsum(dwr) if dkw is None else dkw + _rowsum(dwr)
    return (jnp.concatenate(dqf, axis=1), jnp.concatenate(dkvf, axis=1), dkpe), (dqw, dkw)


def _gdn_act(i, x, halo, w8):
    tm = x.shape[0]
    halo = jnp.where(i > 0, halo, 0.0)
    c = _conv_fwd(x, halo, w8, DN_CONV)
    act, dact = _silu_parts(c)
    return act, dact


def _spread_heads(ab):
    tm = ab.shape[0]
    return jnp.concatenate([jnp.broadcast_to(ab[:, h:h + 1], (tm, DN_DIM)) for h in range(2 * DN_HEADS)], axis=1)


def _gather_heads(x):
    tm = x.shape[0]
    lane = lax.broadcasted_iota(jnp.int32, (tm, LANE), 1)
    out = jnp.zeros((tm, LANE), F32)
    for h in range(2 * DN_HEADS):
        out = out + jnp.where(lane == h, x[:, DN_DIM * h:DN_DIM * h + 1], 0.0)
    return out


def _f_gdn_prep(i, x, halo, ab, w8, alog, dtb):
    tm = x.shape[0]
    act, _ = _gdn_act(i, x, halo, w8)
    outs = []
    for part in range(2):
        for h in range(DN_HEADS):
            t = act[:, DN_WIDTH * part + DN_DIM * h:DN_WIDTH * part + DN_DIM * (h + 1)]
            outs.append(t * lax.rsqrt(jnp.sum(t * t, axis=-1, keepdims=True) + EPS))
    q = jnp.concatenate(outs[:DN_HEADS], axis=1)
    k = jnp.concatenate(outs[DN_HEADS:], axis=1)
    v = act[:, 2 * DN_WIDTH:]
    abb = _spread_heads(ab)
    valid = _row_ids(i, tm) >= PAD
    g = jnp.where(valid, -jnp.exp(alog) * _softplus(abb[:, :DN_WIDTH] + dtb), 0.0)
    beta = jnp.where(valid, _sigmoid(abb[:, DN_WIDTH:]), 0.0)
    return (q, k, v, g, beta), ()


def _f_gdn_prep_bwd(i, x, x_prev, x_next, ab, dq, dq_next, dk, dk_next, dv, dv_next, dg, dbeta,
                    w8, alog, dtb, *, nt):
    tm = x.shape[0]
    x_prev = jnp.where(i > 0, x_prev, 0.0)
    more = i < nt - 1
    ext = lambda t, t_next: jnp.concatenate([t, jnp.where(more, t_next, 0.0)], axis=0)
    c = _conv_fwd(jnp.concatenate([x, x_next], axis=0), x_prev, w8, DN_CONV)
    act, dact = _silu_parts(c)
    douts = []
    for part, dd in enumerate((ext(dq, dq_next), ext(dk, dk_next))):
        for h in range(DN_HEADS):
            t = act[:, DN_WIDTH * part + DN_DIM * h:DN_WIDTH * part + DN_DIM * (h + 1)]
            r = lax.rsqrt(jnp.sum(t * t, axis=-1, keepdims=True) + EPS)
            y = t * r
            dy = dd[:, DN_DIM * h:DN_DIM * (h + 1)]
            douts.append(r * (dy - y * jnp.sum(dy * y, axis=-1, keepdims=True)))
    douts.append(ext(dv, dv_next))
    dc = jnp.concatenate(douts, axis=1) * dact
    dqkv = _conv_bwd_x(dc[:tm], dc[tm:], w8, DN_CONV)
    dconv_w = _conv_bwd_w(dc[:tm], x, x_prev, DN_CONV)
    abb = _spread_heads(ab)
    valid = _row_ids(i, tm) >= PAD
    pre = abb[:, :DN_WIDTH] + dtb
    ea = jnp.exp(alog)
    g = -ea * _softplus(pre)
    dg = jnp.where(valid, dg, 0.0)
    dbeta = jnp.where(valid, dbeta, 0.0)
    da = dg * (-ea) * _sigmoid(pre)
    beta = _sigmoid(abb[:, DN_WIDTH:])
    db = dbeta * beta * (1.0 - beta)
    dab = _gather_heads(jnp.concatenate([da, db], axis=1))
    return (dqkv, dab), (dconv_w, _rowsum(dg * g), _rowsum(da))


def _f_conv_bwd(i, dy, dy_next, x, x_prev, w8, *, width, nt):
    dy_next = jnp.where(i < nt - 1, dy_next, 0.0)
    x_prev = jnp.where(i > 0, x_prev, 0.0)
    return (_conv_bwd_x(dy, dy_next, w8, width),), (_conv_bwd_w(dy, x, x_prev, width),)


def _f_mix(i, o_mla, o_dn, z, w_mla, w_dn):
    tm = o_mla.shape[0]
    valid = _row_ids(i, tm) >= PAD
    outs = []
    for h in range(MLA_HEADS):
        y, _ = _rms_fwd(o_mla[:, V_HEAD * h:V_HEAD * (h + 1)], w_mla, V_HEAD)
        outs.append(jnp.where(valid, y, 0.0))
    for h in range(DN_HEADS):
        y, _ = _rms_fwd(o_dn[:, DN_DIM * h:DN_DIM * (h + 1)], w_dn, DN_DIM)
        outs.append(y * _silu_parts(z[:, DN_DIM * h:DN_DIM * (h + 1)])[0])
    return (jnp.concatenate(outs, axis=1),), ()


def _f_mix_bwd(i, o_mla, o_dn, z, dy_mla, dy_dn, w_mla, w_dn):
    tm = o_mla.shape[0]
    valid = _row_ids(i, tm) >= PAD
    d_mla, d_dn, d_z = [], [], []
    dw_mla = None
    dw_dn = None
    for h in range(MLA_HEADS):
        sl = slice(V_HEAD * h, V_HEAD * (h + 1))
        dx, dwr = _rms_bwd(o_mla[:, sl], w_mla, jnp.where(valid, dy_mla[:, sl], 0.0), V_HEAD)
        d_mla.append(dx)
        dw_mla = _rowsum(dwr) if dw_mla is None else dw_mla + _rowsum(dwr)
    for h in range(DN_HEADS):
        sl = slice(DN_DIM * h, DN_DIM * (h + 1))
        y, _ = _rms_fwd(o_dn[:, sl], w_dn, DN_DIM)
        sz, dsz = _silu_parts(z[:, sl])
        d_z.append(dy_dn[:, sl] * y * dsz)
        dx, dwr = _rms_bwd(o_dn[:, sl], w_dn, dy_dn[:, sl] * sz, DN_DIM)
        d_dn.append(dx)
        dw_dn = _rowsum(dwr) if dw_dn is None else dw_dn + _rowsum(dwr)
    return ((jnp.concatenate(d_mla, axis=1), jnp.concatenate(d_dn, axis=1), jnp.concatenate(d_z, axis=1)),
            (dw_mla, dw_dn))


def _f_ffn_act(i, gate_pre, halo, up, w8, b):
    halo = jnp.where(i > 0, halo, 0.0)
    gate = _conv_fwd(gate_pre, halo, w8, FFN_CONV) + b
    return (_silu_parts(gate)[0] * up,), ()


def _f_ffn_act_bwd(i, gp, gp_prev, gp_next, up, up_next, dact, dact_next, w8, b, *, nt):
    tm = gp.shape[0]
    gp_prev = jnp.where(i > 0, gp_prev, 0.0)
    dact_next = jnp.where(i < nt - 1, dact_next, 0.0)
    cat = lambda t, t_next: jnp.concatenate([t, t_next], axis=0)
    gate = _conv_fwd(cat(gp, gp_next), gp_prev, w8, FFN_CONV) + b
    sg, dsg = _silu_parts(gate)
    dact_e = cat(dact, dact_next)
    dgate = dact_e * cat(up, up_next) * dsg
    dgate_pre = _conv_bwd_x(dgate[:tm], dgate[tm:], w8, FFN_CONV)
    dup = dact * sg[:tm]
    return (dgate_pre, dup), (_conv_bwd_w(dgate[:tm], gp, gp_prev, FFN_CONV), _rowsum(dgate[:tm]))


def _f_loss(i, h3, tgt):
    tm = h3.shape[0]
    diff = jnp.where(_row_ids(i, tm) >= ROW0, h3 - tgt, 0.0)
    part = 0.5 * jnp.sum(diff * diff) * (1.0 / D_MODEL)
    return (diff * (1.0 / D_MODEL),), (jnp.full((1, LANE), part, F32),)


def _after(fn):
    return lambda i, *a: fn(i, *a[:-1])


def _local_step(h0, tgt, w, token, late_weights, grads_ready):
    tp = h0.shape[0]
    nt = tp // TM
    bf = (D_MODEL, _MXU)
    proj, u = _norm_mm("in_proj", h0, w["attn_norm_w"], w["w_in"], after=token)
    p_qkv = lambda kind="cur": _In(proj, 3 * DN_WIDTH, 0, kind)
    p_z = _In(proj, DN_WIDTH, C_Z // DN_WIDTH)
    p_ql = _In(proj, Q_LORA, C_QL // Q_LORA)
    p_kvl = _In(proj, KV_LORA, C_KVL // KV_LORA)
    p_kpe = _In(proj, LANE, C_KPE // LANE)
    p_ab = _In(proj, LANE, C_AB // LANE)
    cos, sin_s = _In(w["cos"]), _In(w["sin_s"])

    qf, qn = _norm_mm("mla_q_b", proj, w["q_a_norm_w"], w["w_q_b"], "nt", C_QL // Q_LORA)
    kvf, kvn = _norm_mm("mla_kv_b", proj, w["kv_a_norm_w"], w["w_kv_b"], "nn", C_KVL // KV_LORA)
    qk_w = [w["q_norm_w"], w["k_norm_w"]]
    q, k, v = _rows("mla_qk", _f_mla_qk, [_In(qf), _In(kvf), p_kpe, cos, sin_s], qk_w,
                    [(MLA_HEADS * HP, _MXU), (MLA_HEADS * HP, _MXU), (MLA_HEADS * V_HEAD, _MXU)])
    o_mla = _attn_fwd(q, k, v)

    dn_w = [w["dn_conv_w"], w["alog_b"], w["dtb_b"]]
    gq, gk, gv, gg, gb = _rows("gdn_prep", _f_gdn_prep, [p_qkv(), p_qkv("prev"), p_ab], dn_w,
                               [(DN_WIDTH, F32)] * 5)
    o_dn, s_all, t_all = _gdn_fwd(gq, gk, gv, gg, gb)

    out_w = [w["mla_out_norm_w"], w["dn_out_norm_w"]]
    mixed, = _rows("mix", _f_mix, [_In(o_mla), _In(o_dn), p_z], out_w, [bf])
    w = dict(w, **late_weights(mixed))
    h2 = _mm("out_proj", mixed, w["w_out"], "nn", resid=h0)

    gate_pre, hn = _norm_mm("ffn_gate", h2, w["ffn_norm_w"], w["w_gate"])
    ffn_w = [w["ffn_conv_w"], w["ffn_conv_b"]]
    act, up = _mm_rows(
        "ffn_up_act", hn, w["w_up"], "nt",
        lambda i, up_t, gp, gp_prev, w8, b: ((_f_ffn_act(i, gp, gp_prev, up_t, w8, b)[0][0], up_t), ()),
        [_In(gate_pre), _In(gate_pre, kind="prev")], ffn_w, [(D_FF, _MXU), (D_FF, F32)], tm_cap=288)
    dh3, loss = _mm_rows("ffn_down_loss", act, w["w_down"], "nn", lambda i, y, r, t: _f_loss(i, r + y, t),
                         [_In(h2), _In(tgt)], [], [(D_MODEL, F32)], [(1, LANE)])

    g = {}
    dact = _mm("ffn_down_dx", dh3, w["w_down"], "nt")
    g["w_down"] = _mm("ffn_down_dw", act, dh3, "tn", out_dtype=_MXU)
    dgate_pre, dup, g["ffn_conv_w"], g["ffn_conv_b"] = _rows(
        "ffn_act_bwd", functools.partial(_f_ffn_act_bwd, nt=nt),
        [_In(gate_pre), _In(gate_pre, kind="prev"), _In(gate_pre, kind="next"), _In(up), _In(up, kind="next"),
         _In(dact), _In(dact, kind="next")], ffn_w,
        [(D_FF, _MXU), (D_FF, _MXU)], [(8, D_FF), (1, D_FF)])
    g["w_gate"] = _mm("ffn_gate_dw", dgate_pre, hn, "tn", out_dtype=_MXU)
    g["w_up"] = _mm("ffn_up_dw", dup, hn, "tn", out_dtype=_MXU)
    tok = grads_ready(g, ("w_down", "w_gate", "w_up"))
    dhn = _mm("ffn_gate_dx", dgate_pre, w["w_gate"], "nn", after=tok)
    dh2, g["ffn_norm_w"] = _mm_rows(
        "ffn_up_dx_rms", dup, w["w_up"], "nn",
        lambda i, y, d1, x, dres, nw: _f_rms_bwd_add(i, x, d1 + y, dres, nw, mask_pad=True),
        [_In(dhn), _In(h2), _In(dh3)], [w["ffn_norm_w"]], [(D_MODEL, F32)], [(1, D_MODEL)])

    g["w_out"] = _mm("out_proj_dw", mixed, dh2, "tn", out_dtype=_MXU)
    half = MLA_HEADS * V_HEAD
    do_mla, do_dn, dz, g["mla_out_norm_w"], g["dn_out_norm_w"] = _mm_rows(
        "out_proj_dx_mix", dh2, w["w_out"], "nt",
        lambda i, dm, om, od, z, wm, wd: _f_mix_bwd(i, om, od, z, dm[:, :half], dm[:, half:], wm, wd),
        [_In(o_mla), _In(o_dn), p_z], out_w,
        [(half, F32), (DN_WIDTH, F32), (DN_WIDTH, _MXU)], [(1, V_HEAD), (1, DN_DIM)])

    dq, dk, dv = _attn_bwd(q, k, v, do_mla)
    dqf, dkvf, dkpe, g["q_norm_w"], g["k_norm_w"] = _rows(
        "mla_qk_bwd", _f_mla_qk_bwd, [_In(qf), _In(kvf), p_kpe, cos, sin_s, _In(dq), _In(dk), _In(dv)], qk_w,
        [(MLA_HEADS * HP, _MXU), (MLA_HEADS * HP, _MXU), (LANE, _MXU)], [(1, HP), (1, HP)])
    g["w_q_b"] = _mm("mla_q_b_dw", dqf, qn, "tn")
    g["w_kv_b"] = _mm("mla_kv_b_dw", kvn, dkvf, "tn")
    tok = grads_ready(g, ("w_out", "w_q_b", "w_kv_b"))

    def lat_bwd(n):
        def fn(i, dy, x, nw, _tok):
            dx, dwr = _rms_bwd(x, nw, dy, n)
            return (dx,), (_rowsum(dwr),)
        return fn

    dql, g["q_a_norm_w"] = _mm_rows("mla_q_b_dx", dqf, w["w_q_b"], "nn", lat_bwd(Q_LORA), [p_ql],
                                    [w["q_a_norm_w"], tok], [(Q_LORA, _MXU)], [(1, Q_LORA)])
    dkvl, g["kv_a_norm_w"] = _mm_rows("mla_kv_b_dx", dkvf, w["w_kv_b"], "nt", lat_bwd(KV_LORA), [p_kvl],
                                      [w["kv_a_norm_w"], tok], [(KV_LORA, _MXU)], [(1, KV_LORA)])

    dgq, dgk, dgv, dgg, dgb = _gdn_bwd(gq, gk, gv, gg, gb, s_all, t_all, do_dn)
    nxt = lambda a: _In(a, kind="next")
    dqkv, dab, g["dn_conv_w"], g["alog_b"], g["dtb_b"] = _rows(
        "gdn_prep_bwd", functools.partial(_f_gdn_prep_bwd, nt=nt),
        [p_qkv(), p_qkv("prev"), p_qkv("next"), p_ab, _In(dgq), nxt(dgq), _In(dgk), nxt(dgk), _In(dgv), nxt(dgv),
         _In(dgg), _In(dgb)], dn_w,
        [(3 * DN_WIDTH, _MXU), (LANE, _MXU)], [(8, 3 * DN_WIDTH), (1, DN_WIDTH), (1, DN_WIDTH)])

    dproj = jnp.concatenate([dqkv, dz, dql, dkvl, dkpe, dab], axis=1)
    g["w_in"] = _mm("in_proj_dw", dproj, u, "tn", out_dtype=_MXU)
    tok = grads_ready(g, ("w_in",))
    dh0, g["attn_norm_w"] = _mm_rows(
        "in_proj_dx_rms", dproj, w["w_in"], "nn",
        lambda i, du, x, dres, nw, _tok: _f_rms_bwd_add(i, x, du, dres, nw, mask_pad=False),
        [_In(h0), _In(dh2)], [w["attn_norm_w"], tok], [(D_MODEL, F32)], [(1, D_MODEL)])
    return loss, dh0, g


def _w_in_to_padded(w):
    c1, c2, c3 = Q_LORA, Q_LORA + KV_LORA, Q_LORA + KV_LORA + QK_ROPE
    c4 = c3 + 3 * DN_WIDTH
    c5 = c4 + DN_WIDTH
    z = lambda n: jnp.zeros((n, w.shape[1]), w.dtype)
    return jnp.concatenate([w[c3:c4], w[c4:c5], w[:c1], w[c1:c2], w[c2:c3], z(LANE - QK_ROPE),
                            w[c5:], z(LANE - 2 * DN_HEADS)], axis=0)


def _w_in_from_padded(g):
    return jnp.concatenate([g[C_QL:C_QL + Q_LORA], g[C_KVL:C_KVL + KV_LORA], g[C_KPE:C_KPE + QK_ROPE],
                            g[:C_Z + DN_WIDTH], g[C_AB:C_AB + 2 * DN_HEADS]], axis=0)


def _w_q_b_to_padded(w):
    r = w.shape[1]
    w = w.reshape(MLA_HEADS, QK_HEAD, r)
    return jnp.pad(w, ((0, 0), (0, HP - QK_HEAD), (0, 0))).reshape(MLA_HEADS * HP, r)


def _w_q_b_from_padded(g):
    r = g.shape[1]
    return g.reshape(MLA_HEADS, HP, r)[:, :QK_HEAD].reshape(MLA_HEADS * QK_HEAD, r)


def _pad_rows8(w):
    return jnp.pad(w, ((0, 8 - w.shape[0]), (0, 0)))


def _prepare(full, tp):
    w = {}
    mx = lambda a: a.astype(_MXU)
    w["attn_norm_w"] = full["attn_norm_w"]
    w["w_in"] = mx(_w_in_to_padded(full["w_in"]))
    w["q_a_norm_w"] = full["q_a_norm_w"]
    w["kv_a_norm_w"] = full["kv_a_norm_w"]
    w["w_q_b"] = mx(_w_q_b_to_padded(full["w_q_b"]))
    w["w_kv_b"] = mx(full["w_kv_b"])
    w["q_norm_w"] = jnp.pad(full["q_norm_w"], ((0, 0), (0, HP - QK_HEAD)))
    w["k_norm_w"] = jnp.pad(full["k_norm_w"], ((0, 0), (0, HP - QK_HEAD)))
    w["mla_out_norm_w"] = full["mla_out_norm_w"]
    w["dn_out_norm_w"] = full["dn_out_norm_w"]
    w["dn_conv_w"] = _pad_rows8(full["dn_conv_w"])
    w["alog_b"] = jnp.repeat(full["dn_A_log"], DN_DIM, axis=1)
    w["dtb_b"] = jnp.repeat(full["dn_dt_bias"], DN_DIM, axis=1)
    w["ffn_norm_w"] = full["ffn_norm_w"]
    w["ffn_conv_w"] = _pad_rows8(full["ffn_conv_w"])
    w["ffn_conv_b"] = full["ffn_conv_b"]
    for n in _LATE:
        if n in full:
            w[n] = mx(full[n])
    half = QK_ROPE // 2
    inv = ROPE_THETA ** (-jnp.arange(half, dtype=F32) / half)
    ang = (jnp.arange(tp, dtype=jnp.int32) - PAD).astype(F32)[:, None] * inv[None, :]
    zc = jnp.zeros((tp, LANE - QK_ROPE), F32)
    w["cos"] = jnp.concatenate([jnp.cos(ang), jnp.cos(ang), zc], axis=1)
    w["sin_s"] = jnp.concatenate([-jnp.sin(ang), jnp.sin(ang), zc], axis=1)
    return w


def _grads_to_natural(g):
    convert = {
        "w_in": ("w_in", _w_in_from_padded),
        "w_q_b": ("w_q_b", _w_q_b_from_padded),
        "q_norm_w": ("q_norm_w", lambda a: a[:, :QK_HEAD]),
        "k_norm_w": ("k_norm_w", lambda a: a[:, :QK_HEAD]),
        "dn_conv_w": ("dn_conv_w", lambda a: a[:DN_CONV]),
        "ffn_conv_w": ("ffn_conv_w", lambda a: a[:FFN_CONV]),
        "alog_b": ("dn_A_log", lambda a: a[:, ::DN_DIM]),
        "dtb_b": ("dn_dt_bias", lambda a: a[:, ::DN_DIM]),
    }
    n = {}
    for key, a in g.items():
        name, fn = convert.get(key, (key, lambda t: t))
        n[name] = fn(a)
    return n


_MESH = pl.DeviceIdType.MESH
_ANY = pl.BlockSpec(memory_space=pl.ANY)
_CHIP_FLIPS = ((1, 0), (0, 1), (1, 1))


def _me():
    return lax.axis_index("x"), lax.axis_index("y"), lax.axis_index("c")


def _all_gather(name, blk):
    def body(x_ref, out_ref, send_sems, recv_sems, local_sem):
        x, y, c = _me()
        me, sib = (x, y, c), (x, y, 1 - c)
        chips = [(x ^ fx, y ^ fy) for fx, fy in _CHIP_FLIPS]

        def slot(p):
            return out_ref.at[4 * p[0] + 2 * p[1] + p[2]]

        def copy(k, block, to, src=None):
            return pltpu.make_async_remote_copy(
                src_ref=slot(block) if src is None else src, dst_ref=slot(block),
                send_sem=send_sems.at[k], recv_sem=recv_sems.at[k], device_id=to, device_id_type=_MESH)

        mine = pltpu.make_async_copy(x_ref, slot(me), local_sem)
        mine.start()
        first = [copy(0, me, sib, src=x_ref)]
        first += [copy(1 + j, me, (*chip, c), src=x_ref) for j, chip in enumerate(chips)]
        for cp in first:
            cp.start()
        passed = [copy(4 + j, (*chip, c), sib) for j, chip in enumerate(chips)]
        for j, chip in enumerate(chips):
            copy(1 + j, (*chip, c), me).wait_recv()
            passed[j].start()
        copy(0, sib, me).wait_recv()
        for j, chip in enumerate(chips):
            copy(4 + j, (*chip, 1 - c), me).wait_recv()
        for cp in first + passed:
            cp.wait_send()
        mine.wait()

    return pl.pallas_call(
        body, name=name, in_specs=[_ANY], out_specs=_ANY,
        out_shape=jax.ShapeDtypeStruct((N_DEV,) + blk.shape, blk.dtype),
        scratch_shapes=[pltpu.SemaphoreType.DMA((7,)), pltpu.SemaphoreType.DMA((7,)), pltpu.SemaphoreType.DMA],
    )(blk)


def _rs_sibling(name, gb):
    def body(g_ref, out_ref, send_sems, recv_sems):
        x, y, c = _me()
        cps = []
        for j in range(4):
            cp = pltpu.make_async_remote_copy(
                src_ref=g_ref.at[2 * j + (1 - c)], dst_ref=out_ref.at[j], send_sem=send_sems.at[j],
                recv_sem=recv_sems.at[j], device_id=(x, y, 1 - c), device_id_type=_MESH)
            cp.start()
            cps.append(cp)
        for cp in cps:
            cp.wait()

    return pl.pallas_call(
        body, name=name, in_specs=[_ANY], out_specs=_ANY,
        out_shape=jax.ShapeDtypeStruct((4,) + gb.shape[1:], gb.dtype),
        scratch_shapes=[pltpu.SemaphoreType.DMA((4,)), pltpu.SemaphoreType.DMA((4,))],
    )(gb)


def _rs_chips(name, s1):
    def body(s_ref, out_ref, send_sems, recv_sems):
        x, y, c = _me()
        cps = []
        for k, (fx, fy) in enumerate(_CHIP_FLIPS):
            px, py = x ^ fx, y ^ fy
            cp = pltpu.make_async_remote_copy(
                src_ref=s_ref.at[2 * px + py], dst_ref=out_ref.at[k], send_sem=send_sems.at[k],
                recv_sem=recv_sems.at[k], device_id=(px, py, c), device_id_type=_MESH)
            cp.start()
            cps.append(cp)
        for cp in cps:
            cp.wait()

    return pl.pallas_call(
        body, name=name, in_specs=[_ANY], out_specs=_ANY,
        out_shape=jax.ShapeDtypeStruct((3,) + s1.shape[1:], s1.dtype),
        scratch_shapes=[pltpu.SemaphoreType.DMA((3,)), pltpu.SemaphoreType.DMA((3,))],
    )(s1)


def _row_tile(r):
    divs = [d for d in range(16, min(r, 512) + 1, 16) if r % d == 0]
    return divs[-1] if divs else r


def _pair_sum(name, gb, recv):
    _, r, cols = gb.shape
    tm = _row_tile(r)
    c = lax.axis_index("c").astype(jnp.int32).reshape(1)

    def body(c_ref, a_ref, b_ref, o_ref, ob_ref):
        s = a_ref[...] + b_ref[...]
        o_ref[...] = s
        ob_ref[...] = s.astype(BF16)

    blk = pl.BlockSpec((1, tm, cols), lambda j, i, c_ref: (j, i, 0))
    return pl.pallas_call(
        body, name=name,
        grid_spec=pltpu.PrefetchScalarGridSpec(
            num_scalar_prefetch=1, grid=(4, r // tm),
            in_specs=[pl.BlockSpec((1, tm, cols), lambda j, i, c_ref: (2 * j + c_ref[0], i, 0)), blk],
            out_specs=[blk, blk]),
        out_shape=[jax.ShapeDtypeStruct((4, r, cols), F32), jax.ShapeDtypeStruct((4, r, cols), BF16)],
        compiler_params=pltpu.CompilerParams(dimension_semantics=("parallel", "parallel")),
    )(c, gb, recv)


def _adam_math(g, w, m, v):
    m_new = ADAM_B1 * m + (1.0 - ADAM_B1) * g
    v_new = ADAM_B2 * v + (1.0 - ADAM_B2) * (g * g)
    m_hat = m_new / (1.0 - ADAM_B1 ** ADAM_STEP)
    v_hat = v_new / (1.0 - ADAM_B2 ** ADAM_STEP)
    return -ADAM_LR * (m_hat / (jnp.sqrt(v_hat) + ADAM_EPS) + ADAM_WD * w), m_new, v_new


def _adam_vectors(name, row, items, ws, ms, vs):
    k = len(items)

    def body(row_ref, *refs):
        w_refs, m_refs, v_refs = refs[:k], refs[k:2 * k], refs[2 * k:3 * k]
        outs = refs[3 * k:]
        for idx, (off, n, per_head) in enumerate(items):
            if per_head:
                spread = row_ref[:, off:off + DN_WIDTH]
                lane = lax.broadcasted_iota(jnp.int32, (1, LANE), 1)
                g = jnp.zeros((1, LANE), F32)
                for h in range(DN_HEADS):
                    g = g + jnp.where(lane == h, spread[:, DN_DIM * h:DN_DIM * h + 1], 0.0)
            else:
                g = row_ref[:, off:off + n]
            d, m_new, v_new = _adam_math(g, w_refs[idx][...], m_refs[idx][...], v_refs[idx][...])
            for kind, val in enumerate((g, d, m_new, v_new)):
                outs[kind * k + idx][...] = val

    shapes = [jax.ShapeDtypeStruct((1, n), F32) for _, n, _ in items]
    res = pl.pallas_call(body, name=name, out_shape=shapes * 4)(row, *ws, *ms, *vs)
    return [list(res[kind * k:(kind + 1) * k]) for kind in range(4)]


def _sum_parts(name, parts):
    _, r, cols = parts[0][0].shape
    tm = _row_tile(r)
    idx = jnp.stack([jnp.asarray(s, jnp.int32) for _, s in parts])
    n = len(parts)

    def body(idx_ref, *refs):
        g = refs[0][0].astype(F32)
        for p_ref in refs[1:n]:
            g = g + p_ref[0].astype(F32)
        refs[n][...] = g

    return pl.pallas_call(
        body, name=name,
        grid_spec=pltpu.PrefetchScalarGridSpec(
            num_scalar_prefetch=1, grid=(r // tm,),
            in_specs=[pl.BlockSpec((1, tm, cols), lambda i, idx_ref, p=p: (idx_ref[p], i, 0)) for p in range(n)],
            out_specs=pl.BlockSpec((tm, cols), lambda i, idx_ref: (i, 0))),
        out_shape=jax.ShapeDtypeStruct((r, cols), F32),
        compiler_params=pltpu.CompilerParams(dimension_semantics=("parallel",)),
    )(idx, *[a for a, _ in parts])


def _adam(name, parts, w, m, v):
    r, cols = w.shape
    tm = _row_tile(r)
    idx = jnp.stack([jnp.asarray(s, jnp.int32) for _, s in parts])
    n = len(parts)

    def body(idx_ref, *refs):
        g = refs[0][0].astype(F32)
        for p_ref in refs[1:n]:
            g = g + p_ref[0].astype(F32)
        w_ref, m_ref, v_ref, g_out, d_out, m_out, v_out = refs[n:]
        g_out[...] = g
        d_out[...], m_out[...], v_out[...] = _adam_math(g, w_ref[...], m_ref[...], v_ref[...])

    part_specs = [pl.BlockSpec((1, tm, cols), lambda i, idx_ref, p=p: (idx_ref[p], i, 0)) for p in range(n)]
    flat = pl.BlockSpec((tm, cols), lambda i, idx_ref: (i, 0))
    return pl.pallas_call(
        body, name=name,
        grid_spec=pltpu.PrefetchScalarGridSpec(
            num_scalar_prefetch=1, grid=(r // tm,), in_specs=part_specs + [flat] * 3, out_specs=[flat] * 4),
        out_shape=[jax.ShapeDtypeStruct((r, cols), F32)] * 4,
        compiler_params=pltpu.CompilerParams(dimension_semantics=("parallel",)),
    )(idx, *[a for a, _ in parts], w, m, v)


def _all_gather_many(name, blks):
    n = len(blks)

    def body(*refs):
        x_refs, out_refs = refs[:n], refs[n:2 * n]
        send_sems, recv_sems, local_sems = refs[2 * n:]
        x, y, c = _me()
        me, sib = (x, y, c), (x, y, 1 - c)
        chips = [(x ^ fx, y ^ fy) for fx, fy in _CHIP_FLIPS]

        def slot(a, p):
            return out_refs[a].at[4 * p[0] + 2 * p[1] + p[2]]

        def copy(a, k, block, to, src=None):
            return pltpu.make_async_remote_copy(
                src_ref=slot(a, block) if src is None else src, dst_ref=slot(a, block),
                send_sem=send_sems.at[7 * a + k], recv_sem=recv_sems.at[7 * a + k], device_id=to,
                device_id_type=_MESH)

        mine = [pltpu.make_async_copy(x_refs[a], slot(a, me), local_sems.at[a]) for a in range(n)]
        first = []
        for a in range(n):
            mine[a].start()
            first.append(copy(a, 0, me, sib, src=x_refs[a]))
            first += [copy(a, 1 + j, me, (*chip, c), src=x_refs[a]) for j, chip in enumerate(chips)]
        for cp in first:
            cp.start()
        passed = []
        for j, chip in enumerate(chips):
            for a in range(n):
                copy(a, 1 + j, (*chip, c), me).wait_recv()
                cp = copy(a, 4 + j, (*chip, c), sib)
                cp.start()
                passed.append(cp)
        for a in range(n):
            copy(a, 0, sib, me).wait_recv()
            for j, chip in enumerate(chips):
                copy(a, 4 + j, (*chip, 1 - c), me).wait_recv()
        for cp in first + passed:
            cp.wait_send()
        for cp in mine:
            cp.wait()

    return pl.pallas_call(
        body, name=name, in_specs=[_ANY] * n, out_specs=[_ANY] * n,
        out_shape=[jax.ShapeDtypeStruct((N_DEV,) + b.shape, b.dtype) for b in blks],
        scratch_shapes=[pltpu.SemaphoreType.DMA((7 * n,)), pltpu.SemaphoreType.DMA((7 * n,)),
                        pltpu.SemaphoreType.DMA((n,))],
    )(*blks)


def _rs_sibling_many(name, gbs):
    n = len(gbs)

    def body(*refs):
        g_refs, out_refs = refs[:n], refs[n:2 * n]
        send_sems, recv_sems = refs[2 * n:]
        x, y, c = _me()
        cps = []
        for a in range(n):
            for j in range(4):
                cp = pltpu.make_async_remote_copy(
                    src_ref=g_refs[a].at[2 * j + (1 - c)], dst_ref=out_refs[a].at[j],
                    send_sem=send_sems.at[4 * a + j], recv_sem=recv_sems.at[4 * a + j],
                    device_id=(x, y, 1 - c), device_id_type=_MESH)
                cp.start()
                cps.append(cp)
        for cp in cps:
            cp.wait()

    return pl.pallas_call(
        body, name=name, in_specs=[_ANY] * n, out_specs=[_ANY] * n,
        out_shape=[jax.ShapeDtypeStruct((4,) + g.shape[1:], g.dtype) for g in gbs],
        scratch_shapes=[pltpu.SemaphoreType.DMA((4 * n,)), pltpu.SemaphoreType.DMA((4 * n,))],
    )(*gbs)


def _rs_chips_many(name, s1s):
    n = len(s1s)

    def body(*refs):
        s_refs, out_refs = refs[:n], refs[n:2 * n]
        send_sems, recv_sems = refs[2 * n:]
        x, y, c = _me()
        cps = []
        for a in range(n):
            for k, (fx, fy) in enumerate(_CHIP_FLIPS):
                px, py = x ^ fx, y ^ fy
                cp = pltpu.make_async_remote_copy(
                    src_ref=s_refs[a].at[2 * px + py], dst_ref=out_refs[a].at[k],
                    send_sem=send_sems.at[3 * a + k], recv_sem=recv_sems.at[3 * a + k],
                    device_id=(px, py, c), device_id_type=_MESH)
                cp.start()
                cps.append(cp)
        for cp in cps:
            cp.wait()

    return pl.pallas_call(
        body, name=name, in_specs=[_ANY] * n, out_specs=[_ANY] * n,
        out_shape=[jax.ShapeDtypeStruct((3,) + s.shape[1:], s.dtype) for s in s1s],
        scratch_shapes=[pltpu.SemaphoreType.DMA((3 * n,)), pltpu.SemaphoreType.DMA((3 * n,))],
    )(*s1s)


_HBM = pl.BlockSpec(memory_space=pltpu.HBM)
_SEM = pl.BlockSpec(memory_space=pltpu.SEMAPHORE)
_EFFECT = pltpu.SideEffectType.DATAFLOW_SIDE_EFFECTING


def _push_copies(src_refs, land_refs, send_sems, recv_sems, src_by_peer):
    x, y, c = _me()
    my_id = 4 * x + 2 * y + c
    out = []
    for a in range(len(src_refs)):
        for f in range(1, N_DEV):
            px, py, pc = x ^ (f >> 2), y ^ ((f >> 1) & 1), c ^ (f & 1)
            pid = 4 * px + 2 * py + pc
            src = src_refs[a].at[pid] if src_by_peer else src_refs[a]
            start = pltpu.make_async_remote_copy(
                src_ref=src, dst_ref=land_refs[a].at[my_id], send_sem=send_sems.at[7 * a + f - 1],
                recv_sem=recv_sems.at[7 * a + f - 1], device_id=(px, py, pc), device_id_type=_MESH)
            landed = pltpu.make_async_remote_copy(
                src_ref=src, dst_ref=land_refs[a].at[pid], send_sem=send_sems.at[7 * a + f - 1],
                recv_sem=recv_sems.at[7 * a + f - 1], device_id=(px, py, pc), device_id_type=_MESH)
            out.append((start, landed))
    return out


def _push_start(name, srcs, src_by_peer, after):
    n = len(srcs)
    lands = [jax.ShapeDtypeStruct((N_DEV,) + (s.shape[1:] if src_by_peer else s.shape), s.dtype) for s in srcs]

    def body(*refs):
        src_refs, land_refs = refs[:n], refs[n:2 * n]
        send_sems, recv_sems = refs[2 * n + 1], refs[2 * n + 2]
        token = refs[-1]
        for start, _ in _push_copies(src_refs, land_refs, send_sems, recv_sems, src_by_peer):
            start.start()
        token[...] = jnp.zeros_like(token)

    hbm = lambda a: pltpu.with_memory_space_constraint(a, pltpu.HBM)
    res = pl.pallas_call(
        body, name=name,
        out_shape=(pltpu.SemaphoreType.DMA((7 * n,)), pltpu.SemaphoreType.DMA((7 * n,)),
                   *[pltpu.HBM(s.shape, s.dtype) for s in srcs], *[pltpu.HBM(s.shape, s.dtype) for s in lands],
                   jax.ShapeDtypeStruct((8, LANE), F32)),
        in_specs=[_HBM] * (2 * n) + [_ANY],
        out_specs=(_SEM, _SEM, *[_HBM] * (2 * n), pl.BlockSpec(memory_space=pltpu.VMEM)),
        input_output_aliases={i: 2 + i for i in range(2 * n)},
        compiler_params=pltpu.CompilerParams(has_side_effects=_EFFECT),
    )(*[hbm(s) for s in srcs], *[hbm(lax.empty(s.shape, s.dtype)) for s in lands], after)
    return res[0], res[1], list(res[2:2 + n]), list(res[2 + n:2 + 2 * n]), res[-1]


def _push_wait(name, send_sems, recv_sems, srcs, lands, src_by_peer, after):
    n = len(srcs)

    def body(*refs):
        src_refs, land_refs = refs[:n], refs[n:2 * n]
        s_sems, r_sems = refs[2 * n], refs[2 * n + 1]
        for _, landed in _push_copies(src_refs, land_refs, s_sems, r_sems, src_by_peer):
            landed.wait_send()
            landed.wait_recv()

    res = pl.pallas_call(
        body, name=name,
        out_shape=tuple(pltpu.HBM(s.shape, s.dtype) for s in list(srcs) + list(lands)),
        in_specs=[_HBM] * (2 * n) + [_SEM, _SEM, _ANY],
        out_specs=tuple([_HBM] * (2 * n)),
        input_output_aliases={i: i for i in range(2 * n)},
        compiler_params=pltpu.CompilerParams(has_side_effects=_EFFECT),
    )(*srcs, *lands, send_sems, recv_sems, after)
    return list(res[:n]), list(res[n:])


_SHARDED = (
    ("meta_tokens", 1, (N_META, D_MODEL)),
    ("w_in", 1, (D_MODEL, IN_COLS)),
    ("w_q_b", 1, (Q_LORA, MLA_HEADS * QK_HEAD)),
    ("w_kv_b", 1, (KV_LORA, MLA_HEADS * (QK_NOPE + V_HEAD))),
    ("dn_conv_w", 1, (DN_CONV, 3 * DN_WIDTH)),
    ("w_out", 0, (2 * DN_WIDTH, D_MODEL)),
    ("w_gate", 1, (D_MODEL, D_FF)),
    ("w_up", 1, (D_MODEL, D_FF)),
    ("ffn_conv_w", 1, (FFN_CONV, D_FF)),
    ("w_down", 0, (D_FF, D_MODEL)),
)
_MXU_GATHERED = ("w_in", "w_q_b", "w_kv_b", "w_out", "w_gate", "w_up", "w_down")
_F32_GATHERED = ("meta_tokens", "dn_conv_w", "ffn_conv_w")
_EARLY = ("w_in", "w_q_b", "w_kv_b")
_LATE = ("w_out", "w_gate", "w_up", "w_down")
_TRANSPOSED = ("w_in", "w_q_b", "w_gate", "w_up")
_REPLICATED = (
    ("attn_norm_w", D_MODEL), ("q_a_norm_w", Q_LORA), ("kv_a_norm_w", KV_LORA), ("q_norm_w", QK_HEAD),
    ("k_norm_w", QK_HEAD), ("mla_out_norm_w", V_HEAD), ("dn_A_log", DN_HEADS), ("dn_dt_bias", DN_HEADS),
    ("dn_out_norm_w", DN_DIM), ("ffn_norm_w", D_MODEL), ("ffn_conv_b", D_FF),
)
_PACK_COLS = 1024
_PACK_ROW_MULT = 320
_SMALL_SHAPE = (8, 768)
_SMALL_BLOCK = (8, 512)


def _local_shape(dim, shape):
    return (shape[0] // N_DEV, shape[1]) if dim == 0 else (shape[0], shape[1] // N_DEV)


def _pack_rows(n, mult):
    rows = -(-n // _PACK_COLS)
    return -(-rows // mult) * mult


def _pack(flats, mult, axis=0):
    cat = jnp.concatenate(flats, axis=-1)
    n = cat.shape[-1]
    r = _pack_rows(n, mult)
    pad = [(0, 0)] * (cat.ndim - 1) + [(0, r * _PACK_COLS - n)]
    return jnp.pad(cat, pad).reshape(cat.shape[:-1] + (r, _PACK_COLS))


def _to_blocks(full, dim):
    r, c = full.shape
    if dim == 0:
        return full.reshape(N_DEV, (r // N_DEV) * c)
    return full.reshape(r, N_DEV, c // N_DEV).transpose(1, 0, 2).reshape(N_DEV, r * (c // N_DEV))


def _from_blocks(blocks, dim, shape):
    r, c = shape
    if dim == 0:
        return blocks.reshape(r, c)
    return blocks.reshape(N_DEV, r, c // N_DEV).transpose(1, 0, 2).reshape(r, c)


def _split(flat, sizes):
    out, o = [], 0
    for s in sizes:
        out.append(flat[..., o:o + s])
        o += s
    return out


def _gather_weights(local, names, dtype, mult):
    specs = [s for s in _SHARDED if s[0] in names]
    pack = _pack([local[n].astype(dtype).reshape(-1) for n, _, _ in specs], mult)
    got = _all_gather("gather_" + "_".join(n[:5] for n in names[:2]), pack)
    flat = got.reshape(N_DEV, -1)
    sizes = [math.prod(_local_shape(d, s)) for _, d, s in specs]
    return {n: _from_blocks(p, d, s) for (n, d, s), p in zip(specs, _split(flat, sizes))}


def kernel(x, meta_tokens, attn_norm_w, w_in, q_a_norm_w, w_q_b, kv_a_norm_w, w_kv_b, q_norm_w, k_norm_w, mla_out_norm_w, dn_conv_w, dn_A_log, dn_dt_bias, dn_out_norm_w, w_out, ffn_norm_w, w_gate, w_up, ffn_conv_w, ffn_conv_b, w_down, loss_target, m_meta_tokens, m_attn_norm_w, m_w_in, m_q_a_norm_w, m_w_q_b, m_kv_a_norm_w, m_w_kv_b, m_q_norm_w, m_k_norm_w, m_mla_out_norm_w, m_dn_conv_w, m_dn_A_log, m_dn_dt_bias, m_dn_out_norm_w, m_w_out, m_ffn_norm_w, m_w_gate, m_w_up, m_ffn_conv_w, m_ffn_conv_b, m_w_down, v_meta_tokens, v_attn_norm_w, v_w_in, v_q_a_norm_w, v_w_q_b, v_kv_a_norm_w, v_w_kv_b, v_q_norm_w, v_k_norm_w, v_mla_out_norm_w, v_dn_conv_w, v_dn_A_log, v_dn_dt_bias, v_dn_out_norm_w, v_w_out, v_ffn_norm_w, v_w_gate, v_w_up, v_ffn_conv_w, v_ffn_conv_b, v_w_down):
    names = [n for n, _, _ in _SHARDED] + [n for n, _ in _REPLICATED]
    given = dict(locals())
    two_d = lambda a: a.reshape(a.shape[-2:])
    view = lambda a, n: two_d(a).T if n in _TRANSPOSED else two_d(a)
    wl = {n: view(given[n], n) for n in names}
    ml = {n: view(given["m_" + n], n) for n in names}
    vl = {n: view(given["v_" + n], n) for n in names}
    out_shapes = {n: given[n].shape for n in names}

    spec = {n: (d, s) for n, d, s in _SHARDED}
    small_sizes = [math.prod(_local_shape(*spec[n])) for n in _F32_GATHERED]

    def small_block(d):
        cat = jnp.concatenate([d[n].reshape(d[n].shape[:-2] + (-1,)) for n in _F32_GATHERED], axis=-1)
        pad = [(0, 0)] * (cat.ndim - 1) + [(0, math.prod(_SMALL_BLOCK) - cat.shape[-1])]
        return jnp.pad(cat, pad).reshape(cat.shape[:-1] + _SMALL_BLOCK)

    def shard(n):
        return wl[n].astype(_MXU)

    def from_slots(n, blocks):
        d, s = spec[n]
        if d == 0 or n in _TRANSPOSED:
            return blocks.reshape(-1, blocks.shape[-1])
        return blocks.transpose(1, 0, 2).reshape(s)

    my_id = 4 * lax.axis_index("x") + 2 * lax.axis_index("y") + lax.axis_index("c")
    got = _all_gather_many("gather_early", [shard(n) for n in _EARLY] + [small_block(wl)])
    full = {n: a for n, a in wl.items() if n not in _LATE}
    for n, blocks in zip(_EARLY, got):
        full[n] = from_slots(n, blocks)
    for n, p in zip(_F32_GATHERED, _split(got[-1].reshape(N_DEV, -1), small_sizes)):
        full[n] = _from_blocks(p, *spec[n])
    late_own = [shard(n) for n in _LATE]
    l_send, l_recv, l_src, l_land, token = _push_start("gather_late_start", late_own, False, got[-1])

    def late_weights(after):
        _, lands = _push_wait("gather_late_wait", l_send, l_recv, l_src, l_land, False, after)
        out = {}
        for n, land, own in zip(_LATE, lands, late_own):
            out[n] = from_slots(n, lax.dynamic_update_slice(land, own[None], (my_id, 0, 0))).astype(_MXU)
        return out

    def dest_blocks(n, a):
        d, s = spec[n]
        r, c = _local_shape(d, s)
        if n in _TRANSPOSED:
            return a.reshape(N_DEV, c, r)
        return a.reshape(N_DEV, r, c) if d == 0 else a.reshape(r, N_DEV, c).transpose(1, 0, 2)

    pushed = []

    def grads_ready(g, names):
        nat = _grads_to_natural({n: g[n] for n in names})
        blocks = [dest_blocks(n, nat[n]).astype(_MXU) for n in names]
        sends, recvs, srcs, lands, tok = _push_start("rs_" + names[0] + "_start", blocks, True, token)
        pushed.append((names, sends, recvs, srcs, lands))
        return tok

    seq = x.shape[1]
    tp = ROW0 + seq
    h0 = jnp.concatenate([jnp.zeros((PAD, D_MODEL), F32), full["meta_tokens"], x[0]], axis=0)
    tgt = jnp.concatenate([jnp.zeros((ROW0, D_MODEL), F32), loss_target[0]], axis=0)
    loss, dh0, raw = _local_step(h0, tgt, _prepare(full, tp), token, late_weights, grads_ready)
    g = _grads_to_natural(raw)
    g["meta_tokens"] = dh0[PAD:ROW0]
    grad_x = dh0[ROW0:][None]

    big = [{}, {}, {}, {}]
    rep_names = [n for n, _ in _REPLICATED]
    raw_key = {"dn_A_log": "alog_b", "dn_dt_bias": "dtb_b"}
    pieces = [raw[raw_key.get(n, n)] for n in rep_names] + [loss]
    pieces += [g[n].reshape(1, -1) for n in _F32_GATHERED]
    widths = [p.shape[1] for p in pieces]
    offs = [sum(widths[:k]) for k in range(len(widths))]
    cat = jnp.concatenate(pieces, axis=1)
    cols = -(-cat.shape[1] // (8 * LANE)) * LANE
    mine = jnp.pad(cat, ((0, 0), (0, 8 * cols - cat.shape[1]))).reshape(8, cols)
    everyone = _all_gather("gather_small_grads", mine)
    total = _sum_parts("sum_small_grads", [(everyone, d) for d in range(N_DEV)]).reshape(1, 8 * cols)
    tot = {n: total[0, o:o + wd] for n, o, wd in zip(rep_names + ["loss"] + list(_F32_GATHERED), offs, widths)}
    lanes = lambda a: jnp.pad(a, ((0, 0), (0, -a.shape[1] % LANE)))
    items = [(o, -(-size // LANE) * LANE, n in raw_key) for (n, size), o in zip(_REPLICATED, offs)]
    sm = _adam_vectors("adam_replicated", total, items, [lanes(wl[n]) for n in rep_names],
                       [lanes(ml[n]) for n in rep_names], [lanes(vl[n]) for n in rep_names])
    sm = [{n: a[:, :size] for (n, size), a in zip(_REPLICATED, kind)} for kind in sm]
    mine_of = {}
    for n in _F32_GATHERED:
        d, s = spec[n]
        r, c = _local_shape(d, s)
        mine_of[n] = lax.dynamic_slice(tot[n].reshape(s), (0, my_id * c), (r, c))
    res = _adam("adam_small_sharded", [(small_block(mine_of)[None], 0)], small_block(wl), small_block(ml),
                small_block(vl))
    for kind, a in enumerate(res):
        big[kind].update(zip(_F32_GATHERED, _split(a.reshape(-1), small_sizes)))

    for names, sends, recvs, srcs, lands in pushed:
        srcs, lands = _push_wait("rs_" + names[0] + "_wait", sends, recvs, srcs, lands, True, dh0)
        for n, src, land in zip(names, srcs, lands):
            parts = [(src, my_id)] + [(land, my_id ^ f) for f in range(1, N_DEV)]
            for kind, a in enumerate(_adam("adam_" + n, parts, wl[n], ml[n], vl[n])):
                big[kind][n] = a

    outs = [tot["loss"][0], grad_x]
    for kind in range(4):
        for n in ("meta_tokens", "attn_norm_w", "w_in", "q_a_norm_w", "w_q_b", "kv_a_norm_w", "w_kv_b", "q_norm_w",
                  "k_norm_w", "mla_out_norm_w", "dn_conv_w", "dn_A_log", "dn_dt_bias", "dn_out_norm_w", "w_out",
                  "ffn_norm_w", "w_gate", "w_up", "ffn_conv_w", "ffn_conv_b", "w_down"):
            src = big[kind] if n in big[kind] else sm[kind]
            a = src[n].T if n in _TRANSPOSED else src[n]
            outs.append(a.reshape(out_shapes[n]))
    return tuple(outs)
```

```python
import functools
import math

import jax
import jax.numpy as jnp
from jax import lax
from jax.experimental import pallas as pl
from jax.experimental.pallas import tpu as pltpu

F32 = jnp.float32
BF16 = jnp.bfloat16
_MXU = jnp.bfloat16
_HI = lax.Precision.HIGHEST

D_MODEL = 1024
N_META = 16
PAD = 112
ROW0 = PAD + N_META
MLA_HEADS = 4
QK_NOPE = 128
QK_ROPE = 64
QK_HEAD = QK_NOPE + QK_ROPE
V_HEAD = 128
Q_LORA = 256
KV_LORA = 256
ROPE_THETA = 10000.0
DN_HEADS = 4
DN_DIM = 128
DN_WIDTH = DN_HEADS * DN_DIM
DN_CONV = 4
DN_CHUNK = 64
GDN_SUB_CHUNKS = 2
D_FF = 2816
FFN_CONV = 3
EPS = 1e-6
HP = 256
C_QKV = 0
C_Z = 1536
C_QL = 2048
C_KVL = 2304
C_KPE = 2560
C_AB = 2688
IN_P = 2816
IN_COLS = 2632

ADAM_LR = 0.001
ADAM_B1 = 0.9
ADAM_B2 = 0.999
ADAM_EPS = 1e-08
ADAM_WD = 0.01
ADAM_STEP = 10

N_DEV = 8
TM = 128
LANE = 128
VMEM_LIMIT = 56 * 1024 * 1024
NEG = -1e30


def _dot(a, b, dims, hp=False):
    if hp:
        return lax.dot_general(a.astype(F32), b.astype(F32), (dims, ((), ())),
                               precision=lax.Precision.HIGH if hp == "3x" else _HI, preferred_element_type=F32)
    return lax.dot_general(a.astype(_MXU), b.astype(_MXU), (dims, ((), ())),
                           preferred_element_type=F32)


def _nn(a, b, hp=False):
    return _dot(a, b, ((1,), (0,)), hp)


def _nt(a, b, hp=False):
    return _dot(a, b, ((1,), (1,)), hp)


def _tn(a, b, hp=False):
    return _dot(a, b, ((0,), (0,)), hp)


def _sigmoid(x):
    return 1.0 / (1.0 + jnp.exp(-x))


def _rms_fwd(x, w, n):
    r = lax.rsqrt(jnp.sum(x * x, axis=-1, keepdims=True) * (1.0 / n) + EPS)
    return x * r * w, r


def _rms_bwd(x, w, dy, n):
    r = lax.rsqrt(jnp.sum(x * x, axis=-1, keepdims=True) * (1.0 / n) + EPS)
    xh = x * r
    gy = dy * w
    dx = r * (gy - xh * (jnp.sum(gy * xh, axis=-1, keepdims=True) * (1.0 / n)))
    return dx, dy * xh


def _rowsum(x):
    return jnp.sum(x, axis=0, keepdims=True)


def _row_ids(i, tm):
    return i * tm + lax.broadcasted_iota(jnp.int32, (tm, 1), 0)


def _shift_down(ext, s, tm):
    if s == 0:
        return ext[8:8 + tm]
    return pltpu.roll(ext, s, 0)[8:8 + tm]


def _shift_up(ext, s, tm):
    if s == 0:
        return ext[0:tm]
    return pltpu.roll(ext, tm + 8 - s, 0)[0:tm]


def _conv_fwd(x, halo_prev, w, width):
    tm = x.shape[0]
    ext = jnp.concatenate([halo_prev, x], axis=0)
    y = None
    for j in range(width):
        t = w[j:j + 1, :] * _shift_down(ext, width - 1 - j, tm)
        y = t if y is None else y + t
    return y


def _conv_bwd_x(dy, halo_next, w, width):
    tm = dy.shape[0]
    ext = jnp.concatenate([dy, halo_next], axis=0)
    dx = None
    for j in range(width):
        t = w[j:j + 1, :] * _shift_up(ext, width - 1 - j, tm)
        dx = t if dx is None else dx + t
    return dx


def _conv_bwd_w(dy, x, halo_prev, width):
    tm = dy.shape[0]
    ext = jnp.concatenate([halo_prev, x], axis=0)
    rows = [_rowsum(dy * _shift_down(ext, width - 1 - j, tm)) for j in range(width)]
    rows += [jnp.zeros_like(rows[0])] * (8 - width)
    return jnp.concatenate(rows, axis=0)


def _softplus(x):
    e = jnp.exp(-jnp.abs(x))
    u = 1.0 + e
    l1p = jnp.where(u == 1.0, e, jnp.log(u) * e / jnp.where(u == 1.0, 1.0, u - 1.0))
    return jnp.maximum(x, 0.0) + l1p


def _swap_halves(x):
    lane = lax.broadcasted_iota(jnp.int32, x.shape, 1)
    return jnp.where(lane < 32, pltpu.roll(x, 96, 1), jnp.where(lane < 64, pltpu.roll(x, 32, 1), 0.0))


class _In:
    def __init__(self, arr, width=None, cb=0, kind="cur"):
        self.arr, self.kind = arr, kind
        self.width = arr.shape[1] if width is None else width
        self.cb = cb


def _tile_spec(t, tm, tp):
    r8 = tm // 8
    if t.kind == "cur":
        return pl.BlockSpec((tm, t.width), lambda i, cb=t.cb: (i, cb))
    if t.kind == "prev":
        return pl.BlockSpec((8, t.width), lambda i, cb=t.cb: (jnp.maximum(i * r8 - 1, 0), cb))
    return pl.BlockSpec((8, t.width), lambda i, cb=t.cb: (jnp.minimum((i + 1) * r8, tp // 8 - 1), cb))


def _rows(name, fn, tiled, full, outs, accs=(), tm=TM):
    tp = tiled[0].arr.shape[0]
    nt = tp // tm
    r8 = tm // 8
    n_in = len(tiled) + len(full)
    n_out = len(outs)

    def body(*refs):
        i = pl.program_id(0)
        vals = [r[...] for r in refs[:n_in]]
        o_t, o_a = fn(i, *vals)
        for r, v in zip(refs[n_in:n_in + n_out], o_t):
            r[...] = v.astype(r.dtype)
        for r, v in zip(refs[n_in + n_out:], o_a):
            @pl.when(i == 0)
            def _():
                r[...] = v

            @pl.when(i > 0)
            def _():
                r[...] += v

    in_specs = [_tile_spec(t, tm, tp) for t in tiled]
    in_specs += [pl.BlockSpec(a.shape, lambda i, nd=a.ndim: (0,) * nd) for a in full]
    out_specs = [pl.BlockSpec((tm, w), lambda i: (i, 0)) for w, _ in outs]
    out_specs += [pl.BlockSpec((r, w), lambda i: (0, 0)) for r, w in accs]
    out_shape = [jax.ShapeDtypeStruct((tp, w), dt) for w, dt in outs]
    out_shape += [jax.ShapeDtypeStruct((r, w), F32) for r, w in accs]
    res = pl.pallas_call(
        body, name=name, grid=(nt,), in_specs=in_specs, out_specs=out_specs, out_shape=out_shape,
        compiler_params=pltpu.CompilerParams(dimension_semantics=("arbitrary",), vmem_limit_bytes=VMEM_LIMIT),
    )(*[t.arr for t in tiled], *full)
    return res


def _pick(n, cap, mult):
    best = None
    for d in range(mult, min(n, cap) + 1, mult):
        if n % d == 0:
            best = d
    assert best is not None, (n, cap, mult)
    return best


_ANY_SPEC = pl.BlockSpec(memory_space=pl.ANY)


def _mm(name, a, b, mode, out_dtype=F32, resid=None, after=None):
    if mode == "tn":
        m, k = a.shape
        n = b.shape[1]
        tk = _pick(k, 512, 128)
        tn = _pick(n, 1408, 128)

        def body_tn(a_ref, b_ref, o_ref):
            o_ref[...] = _tn(a_ref[...], b_ref[...]).astype(o_ref.dtype)

        return pl.pallas_call(
            body_tn, name=name, grid=(n // tn, k // tk),
            in_specs=[pl.BlockSpec((m, tk), lambda j, p: (0, p)),
                      pl.BlockSpec((m, tn), lambda j, p: (0, j))],
            out_specs=pl.BlockSpec((tk, tn), lambda j, p: (p, j)),
            out_shape=jax.ShapeDtypeStruct((k, n), out_dtype),
            compiler_params=pltpu.CompilerParams(
                dimension_semantics=("parallel", "parallel"), vmem_limit_bytes=VMEM_LIMIT),
        )(a, b)

    m, k = a.shape
    n = b.shape[1] if mode == "nn" else b.shape[0]
    tn = _pick(n, 1408, 128)
    tm = _pick(m, 1152, 16)
    dotf = _nn if mode == "nn" else _nt

    def body(*refs):
        a_ref, b_ref, o_ref = refs[0], refs[1], refs[-1]
        acc = dotf(a_ref[...], b_ref[...])
        if resid is not None:
            acc = refs[2][...] + acc
        o_ref[...] = acc.astype(o_ref.dtype)

    b_spec = (pl.BlockSpec((k, tn), lambda j, i: (0, j)) if mode == "nn"
              else pl.BlockSpec((tn, k), lambda j, i: (j, 0)))
    in_specs = [pl.BlockSpec((tm, k), lambda j, i: (i, 0)), b_spec]
    args = [a, b]
    if resid is not None:
        in_specs.append(pl.BlockSpec((tm, tn), lambda j, i: (i, j)))
        args.append(resid)
    if after is not None:
        in_specs.append(_ANY_SPEC)
        args.append(after)
    return pl.pallas_call(
        body, name=name, grid=(n // tn, m // tm), in_specs=in_specs,
        out_specs=pl.BlockSpec((tm, tn), lambda j, i: (i, j)),
        out_shape=jax.ShapeDtypeStruct((m, n), out_dtype),
        compiler_params=pltpu.CompilerParams(
            dimension_semantics=("parallel", "parallel"), vmem_limit_bytes=VMEM_LIMIT),
    )(*args)


def _norm_mm(name, x, norm_w, b, mode="nt", x_cb=0, after=None):
    m = x.shape[0]
    k = norm_w.shape[1]
    n = b.shape[0] if mode == "nt" else b.shape[1]
    tn = _pick(n, 1408, 128)
    tm = _pick(m, 1152, 16)
    dotf = _nt if mode == "nt" else _nn
    extra = [] if after is None else [after]

    def body(x_ref, w_ref, b_ref, *rest):
        o_ref, u_ref = rest[-2:]

        @pl.when(pl.program_id(1) == 0)
        def _():
            u_ref[...] = _rms_fwd(x_ref[...], w_ref[...], k)[0].astype(u_ref.dtype)

        o_ref[...] = dotf(u_ref[...], b_ref[...])

    b_spec = (pl.BlockSpec((tn, k), lambda i, j: (j, 0)) if mode == "nt"
              else pl.BlockSpec((k, tn), lambda i, j: (0, j)))
    return pl.pallas_call(
        body, name=name, grid=(m // tm, n // tn),
        in_specs=[pl.BlockSpec((tm, k), lambda i, j: (i, x_cb)), pl.BlockSpec((1, k), lambda i, j: (0, 0)),
                  b_spec] + [_ANY_SPEC] * len(extra),
        out_specs=[pl.BlockSpec((tm, tn), lambda i, j: (i, j)), pl.BlockSpec((tm, k), lambda i, j: (i, 0))],
        out_shape=[jax.ShapeDtypeStruct((m, n), F32), jax.ShapeDtypeStruct((m, k), _MXU)],
        compiler_params=pltpu.CompilerParams(
            dimension_semantics=("arbitrary", "arbitrary"), vmem_limit_bytes=VMEM_LIMIT),
    )(x, norm_w, b, *extra)


def _mm_rows(name, a, b, mode, fn, tiled, full, outs, accs=(), tm_cap=576):
    m = a.shape[0]
    tm = _pick(m, tm_cap, 16)
    dotf = _nn if mode == "nn" else _nt
    n_in = len(tiled) + len(full)
    n_out = len(outs)

    def body(*refs):
        i = pl.program_id(0)
        vals = [r[...] for r in refs[2:2 + n_in]]
        o_t, o_a = fn(i, dotf(refs[0][...], refs[1][...]), *vals)
        for r, v in zip(refs[2 + n_in:2 + n_in + n_out], o_t):
            r[...] = v.astype(r.dtype)
        for r, v in zip(refs[2 + n_in + n_out:], o_a):
            @pl.when(i == 0)
            def _():
                r[...] = v

            @pl.when(i > 0)
            def _():
                r[...] += v

    whole = lambda x: pl.BlockSpec(x.shape, lambda i, nd=x.ndim: (0,) * nd)
    in_specs = [pl.BlockSpec((tm, a.shape[1]), lambda i: (i, 0)), whole(b)]
    in_specs += [_tile_spec(t, tm, m) for t in tiled]
    in_specs += [whole(x) for x in full]
    out_specs = [pl.BlockSpec((tm, w), lambda i: (i, 0)) for w, _ in outs]
    out_specs += [pl.BlockSpec((r, w), lambda i: (0, 0)) for r, w in accs]
    out_shape = [jax.ShapeDtypeStruct((m, w), dt) for w, dt in outs]
    out_shape += [jax.ShapeDtypeStruct((r, w), F32) for r, w in accs]
    return pl.pallas_call(
        body, name=name, grid=(m // tm,), in_specs=in_specs, out_specs=out_specs, out_shape=out_shape,
        compiler_params=pltpu.CompilerParams(dimension_semantics=("arbitrary",), vmem_limit_bytes=VMEM_LIMIT),
    )(a, b, *[t.arr for t in tiled], *full)


ATTN_Q_TILES = 4


def _attn_probs(q, k, row0):
    tq, tp = q.shape[0], k.shape[0]
    s = _nt(q, k) * (1.0 / math.sqrt(QK_HEAD))
    row = row0 + lax.broadcasted_iota(jnp.int32, (tq, tp), 0)
    col = lax.broadcasted_iota(jnp.int32, (tq, tp), 1)
    ok = (col <= row) & (col >= PAD)
    s = jnp.where(ok, s, NEG)
    m = jnp.max(s, axis=-1, keepdims=True)
    e = jnp.exp(s - m)
    return e * (1.0 / jnp.sum(e, axis=-1, keepdims=True))


def _attn_fwd(q, k, v):
    tp = q.shape[0]
    tq = tp // ATTN_Q_TILES

    def body(q_ref, k_ref, v_ref, o_ref):
        for i in range(ATTN_Q_TILES):
            rows = slice(i * tq, (i + 1) * tq)
            keys = slice(0, (i + 1) * tq)
            p = _attn_probs(q_ref[rows, :], k_ref[keys, :], i * tq)
            o_ref[rows, :] = _nn(p, v_ref[keys, :])

    return pl.pallas_call(
        body, name="attn_fwd", grid=(MLA_HEADS,),
        in_specs=[pl.BlockSpec((tp, HP), lambda h: (0, h)),
                  pl.BlockSpec((tp, HP), lambda h: (0, h)),
                  pl.BlockSpec((tp, V_HEAD), lambda h: (0, h))],
        out_specs=pl.BlockSpec((tp, V_HEAD), lambda h: (0, h)),
        out_shape=jax.ShapeDtypeStruct((tp, MLA_HEADS * V_HEAD), F32),
        compiler_params=pltpu.CompilerParams(dimension_semantics=("parallel",), vmem_limit_bytes=VMEM_LIMIT),
    )(q, k, v)


def _attn_bwd(q, k, v, do):
    tp = q.shape[0]
    tq = tp // ATTN_Q_TILES

    def body(q_ref, k_ref, v_ref, do_ref, dq_ref, dk_ref, dv_ref):
        for i in reversed(range(ATTN_Q_TILES)):
            rows = slice(i * tq, (i + 1) * tq)
            keys = slice(0, (i + 1) * tq)
            qb = q_ref[rows, :]
            kk = k_ref[keys, :]
            dob = do_ref[rows, :]
            p = _attn_probs(qb, kk, i * tq)
            dp = _nt(dob, v_ref[keys, :])
            delta = jnp.sum(p * dp, axis=-1, keepdims=True)
            ds = p * (dp - delta) * (1.0 / math.sqrt(QK_HEAD))
            dq_ref[rows, :] = _nn(ds, kk)
            if i == ATTN_Q_TILES - 1:
                dk_ref[...] = _tn(ds, qb)
                dv_ref[...] = _tn(p, dob)
            else:
                dk_ref[keys, :] += _tn(ds, qb)
                dv_ref[keys, :] += _tn(p, dob)

    full = lambda w: pl.BlockSpec((tp, w), lambda h: (0, h))
    return pl.pallas_call(
        body, name="attn_bwd", grid=(MLA_HEADS,),
        in_specs=[full(HP), full(HP), full(V_HEAD), full(V_HEAD)],
        out_specs=[full(HP), full(HP), full(V_HEAD)],
        out_shape=[jax.ShapeDtypeStruct((tp, MLA_HEADS * HP), F32),
                   jax.ShapeDtypeStruct((tp, MLA_HEADS * HP), F32),
                   jax.ShapeDtypeStruct((tp, MLA_HEADS * V_HEAD), F32)],
        compiler_params=pltpu.CompilerParams(dimension_semantics=("parallel",), vmem_limit_bytes=VMEM_LIMIT),
    )(q, k, v, do)


def _gdn_consts():
    c = DN_CHUNK
    r = lax.broadcasted_iota(jnp.int32, (c, c), 0)
    cc = lax.broadcasted_iota(jnp.int32, (c, c), 1)
    incl = r >= cc
    strict = r > cc
    return incl, strict


def _cumsum_rows(x, reverse=False):
    c = x.shape[0]
    row = lax.broadcasted_iota(jnp.int32, x.shape, 0)
    s = 1
    while s < c:
        if reverse:
            x = x + jnp.where(row < c - s, pltpu.roll(x, c - s, 0), 0.0)
        else:
            x = x + jnp.where(row >= s, pltpu.roll(x, s, 0), 0.0)
        s *= 2
    return x


def _each(fn, *lists):
    return [fn(*a) for a in zip(*lists)]


def _interleave(chains):
    chains = list(chains)
    while chains:
        for ch in list(chains):
            try:
                next(ch)
            except StopIteration:
                chains.remove(ch)


def _gdn_chunk_common(q_ref, k_ref, v_ref, g_ref, b_ref):
    c = DN_CHUNK
    incl, strict = _gdn_consts()
    sls = [(slice(c * sub, c * (sub + 1)), slice(DN_DIM * h, DN_DIM * (h + 1)))
           for sub in range(GDN_SUB_CHUNKS) for h in range(DN_HEADS)]
    q = [q_ref[sl] * (1.0 / math.sqrt(DN_DIM)) for sl in sls]
    k = [k_ref[sl] for sl in sls]
    v = [v_ref[sl] for sl in sls]
    g = [g_ref[sl] for sl in sls]
    beta = [b_ref[sl] for sl in sls]
    gc = [_cumsum_rows(x) for x in g]
    grow = [x.T[:c, :] for x in gc]
    kb = _each(jnp.multiply, k, beta)
    kk = _each(_nt, kb, k)
    qk = _each(_nt, q, k)
    gam = [jnp.exp(x) for x in gc]
    g_last = [_rowsum(x) for x in g]
    dm = [jnp.exp(jnp.where(incl, x[:, :c] - y, NEG)) for x, y in zip(gc, grow)]
    vb = _each(jnp.multiply, v, beta)
    kbg = _each(jnp.multiply, kb, gam)
    ek = [jnp.exp(x - y) for x, y in zip(g_last, gc)]
    kd = _each(jnp.multiply, k, ek)
    return dict(q=q, k=k, v=v, beta=beta, gc=gc, gam=gam, g_last=g_last, dm=dm, kb=kb, vb=vb,
                kbg=kbg, kk=kk, ek=ek, kd=kd, qk=qk, incl=incl, strict=strict, sls=sls)


def _gdn_fwd(q, k, v, g, beta):
    tp = q.shape[0]
    c = DN_CHUNK
    nch = tp // c

    def body(q_ref, k_ref, v_ref, g_ref, b_ref, o_ref, s_ref, t_ref, s_scr):
        @pl.when(pl.program_id(0) == 0)
        def _():
            s_scr[...] = jnp.zeros_like(s_scr)

        eye = (lax.broadcasted_iota(jnp.int32, (c, c), 0) == lax.broadcasted_iota(jnp.int32, (c, c), 1)).astype(F32)
        x = _gdn_chunk_common(q_ref, k_ref, v_ref, g_ref, b_ref)
        heads = range(DN_HEADS)
        bp = [-jnp.where(x["strict"], kk * dm, 0.0) for kk, dm in zip(x["kk"], x["dm"])]
        t = [eye + b for b in bp]
        for _ in range(5):
            bp = [_nn(b, b, hp="3x") for b in bp]
            t = [tt + _nn(tt, b, hp="3x") for tt, b in zip(t, bp)]
        u = _each(_nn, t, x["vb"])
        w = _each(_nn, t, x["kbg"])
        qg = _each(jnp.multiply, x["q"], x["gam"])
        mqk = _each(jnp.multiply, x["qk"], x["dm"])
        s = [s_scr[h] for h in heads]
        for sub in range(GDN_SUB_CHUNKS):
            e = [DN_HEADS * sub + h for h in heads]
            v_new = [u[i] - _nn(w[i], s[h]) for h, i in zip(heads, e)]
            o = [_nn(qg[i], s[h]) + _nn(mqk[i], v_new[h]) for h, i in zip(heads, e)]
            s_new = [s[h] * jnp.exp(x["g_last"][i]) + _tn(x["kd"][i], v_new[h]) for h, i in zip(heads, e)]
            for h, i in zip(heads, e):
                s_ref[h, sub] = s[h]
                t_ref[h, sub] = t[i]
                o_ref[x["sls"][i]] = o[h]
            s = s_new
        for h in heads:
            s_scr[h] = s[h]

    sub = GDN_SUB_CHUNKS
    rb = lambda n: (n, 0)
    return pl.pallas_call(
        body, name="gdn_fwd", grid=(nch // sub,),
        in_specs=[pl.BlockSpec((sub * c, DN_WIDTH), rb)] * 5,
        out_specs=[pl.BlockSpec((sub * c, DN_WIDTH), rb),
                   pl.BlockSpec((DN_HEADS, sub, DN_DIM, DN_DIM), lambda n: (0, n, 0, 0)),
                   pl.BlockSpec((DN_HEADS, sub, c, c), lambda n: (0, n, 0, 0))],
        out_shape=[jax.ShapeDtypeStruct((tp, DN_WIDTH), F32),
                   jax.ShapeDtypeStruct((DN_HEADS, nch, DN_DIM, DN_DIM), F32),
                   jax.ShapeDtypeStruct((DN_HEADS, nch, c, c), F32)],
        scratch_shapes=[pltpu.VMEM((DN_HEADS, DN_DIM, DN_DIM), F32)],
        compiler_params=pltpu.CompilerParams(dimension_semantics=("arbitrary",), vmem_limit_bytes=VMEM_LIMIT),
    )(q, k, v, g, beta)


def _gdn_bwd(q, k, v, g, beta, s_all, t_all, do):
    tp = q.shape[0]
    c = DN_CHUNK
    nch = tp // c

    def body(q_ref, k_ref, v_ref, g_ref, b_ref, s_ref, t_ref, do_ref,
             dq_ref, dk_ref, dv_ref, dg_ref, db_ref, ds_scr):
        @pl.when(pl.program_id(0) == 0)
        def _():
            ds_scr[...] = jnp.zeros_like(ds_scr)

        xs = _gdn_chunk_common(q_ref, k_ref, v_ref, g_ref, b_ref)

        ds_state = [ds_scr[h] for h in range(DN_HEADS)]

        def chain(sub, h):
            e = DN_HEADS * sub + h
            x = {key: (val[e] if isinstance(val, list) else val) for key, val in xs.items()}
            sl = x["sls"]
            qs, kx, vx, beta_, gam, dm = x["q"], x["k"], x["v"], x["beta"], x["gam"], x["dm"]
            kb, vb, kbg, kd, ek = x["kb"], x["vb"], x["kbg"], x["kd"], x["ek"]
            t = t_ref[h, sub]
            s = s_ref[h, sub]
            dsn = ds_state[h]
            dob = do_ref[sl]
            eg_last = jnp.exp(x["g_last"])
            u = _nn(t, vb)
            w = _nn(t, kbg)
            mqk = x["qk"] * dm
            qd = qs * gam
            dqd = _nt(dob, s)
            dkd_pre = _nn(kd, dsn)
            yield
            v_new = u - _nn(w, s)
            dv_new = _tn(mqk, dob) + dkd_pre
            dq = dqd * gam
            dgam = jnp.sum(dqd * qs, axis=1, keepdims=True)
            yield
            ds_state[h] = _tn(qd, dob) + eg_last * dsn - _tn(w, dv_new)
            dmm = jnp.where(x["incl"], _nt(dob, v_new), 0.0)
            dkd = _nt(v_new, dsn)
            dw = -_nt(dv_new, s)
            dvb = _tn(t, dv_new)
            dt = _nt(dv_new, vb)
            yield
            dqk = dmm * dm
            e_mat = dmm * mqk
            dq = dq + _nn(dqk, kx)
            dk = _tn(dqk, qs) + dkd * ek
            e1 = jnp.sum(dkd * kd, axis=1, keepdims=True)
            dgc = -e1
            dg_last = jnp.sum(e1) + eg_last * jnp.sum(s * dsn)
            dt = dt + _nt(dw, kbg)
            dkbg = _tn(t, dw)
            yield
            tdt = _tn(t, dt, hp="3x")
            yield
            da = jnp.where(x["strict"], -_nt(tdt, t, hp="3x"), 0.0)
            yield
            dkk = da * dm
            e_mat = e_mat + da * x["kk"] * dm
            dkb = _nn(dkk, kx) + dkbg * gam
            dk = dk + _tn(dkk, kb)
            dgam = dgam + jnp.sum(dkbg * kb, axis=1, keepdims=True)
            yield
            dk = dk + dkb * beta_
            dbeta = jnp.sum(dkb * kx, axis=1, keepdims=True) + jnp.sum(dvb * vx, axis=1, keepdims=True)
            dv = dvb * beta_
            dgc = dgc + jnp.sum(e_mat, axis=1, keepdims=True) + dgam * gam
            dgc = dgc - jnp.sum(e_mat.T, axis=1, keepdims=True)
            yield
            dg = _cumsum_rows(dgc, reverse=True) + dg_last
            yield
            dq_ref[sl] = dq * (1.0 / math.sqrt(DN_DIM))
            dk_ref[sl] = dk
            dv_ref[sl] = dv
            dg_ref[sl] = dg
            db_ref[sl] = jnp.broadcast_to(dbeta, (c, LANE))

        chains = []
        for sub in reversed(range(GDN_SUB_CHUNKS)):
            new = [chain(sub, h) for h in range(DN_HEADS)]
            for _ in range(3):
                for ch in new:
                    next(ch)
            chains += new
        _interleave(chains)
        for h in range(DN_HEADS):
            ds_scr[h] = ds_state[h]

    nblk = nch // GDN_SUB_CHUNKS
    sub = GDN_SUB_CHUNKS
    rb = lambda n: (nblk - 1 - n, 0)
    hs = lambda n: (0, nblk - 1 - n, 0, 0)
    return pl.pallas_call(
        body, name="gdn_bwd", grid=(nblk,),
        in_specs=[pl.BlockSpec((sub * c, DN_WIDTH), rb)] * 5
        + [pl.BlockSpec((DN_HEADS, sub, DN_DIM, DN_DIM), hs), pl.BlockSpec((DN_HEADS, sub, c, c), hs),
           pl.BlockSpec((sub * c, DN_WIDTH), rb)],
        out_specs=[pl.BlockSpec((sub * c, DN_WIDTH), rb)] * 5,
        out_shape=[jax.ShapeDtypeStruct((tp, DN_WIDTH), F32)] * 5,
        scratch_shapes=[pltpu.VMEM((DN_HEADS, DN_DIM, DN_DIM), F32)],
        compiler_params=pltpu.CompilerParams(dimension_semantics=("arbitrary",), vmem_limit_bytes=VMEM_LIMIT),
    )(q, k, v, g, beta, s_all, t_all, do)


def _silu_parts(x):
    s = _sigmoid(x)
    return x * s, s * (1.0 + x * (1.0 - s))


def _f_rms_cast(i, x, w):
    y, _ = _rms_fwd(x, w, x.shape[1])
    return (y,), ()


def _f_rms_bwd_add(i, x, dy, dres, w, *, mask_pad):
    dx, dwr = _rms_bwd(x, w, dy, x.shape[1])
    out = dres + dx
    if mask_pad:
        out = jnp.where(_row_ids(i, x.shape[0]) >= PAD, out, 0.0)
    return (out,), (_rowsum(dwr),)


def _f_lat_norm(i, ql, kvl, qw, kvw):
    return (_rms_fwd(ql, qw, Q_LORA)[0], _rms_fwd(kvl, kvw, KV_LORA)[0]), ()


def _f_lat_norm_bwd(i, ql, kvl, dqn, dkvn, qw, kvw):
    dq, dqw = _rms_bwd(ql, qw, dqn, Q_LORA)
    dk, dkw = _rms_bwd(kvl, kvw, dkvn, KV_LORA)
    return (dq, dk), (_rowsum(dqw), _rowsum(dkw))


def _rope(x, cos, sin_s):
    return x * cos + _swap_halves(x) * sin_s


def _rope_t(dy, cos, sin_s):
    return dy * cos + _swap_halves(dy * sin_s)


def _f_mla_qk(i, qf, kvf, kpe, cos, sin_s, qw, kw):
    qs, ks, vs = [], [], []
    for h in range(MLA_HEADS):
        qn, _ = _rms_fwd(qf[:, HP * h:HP * (h + 1)], qw, QK_HEAD)
        qs += [qn[:, :QK_NOPE], _rope(qn[:, QK_NOPE:], cos, sin_s)]
        kh = jnp.concatenate([kvf[:, HP * h:HP * h + QK_NOPE], kpe], axis=1)
        kn, _ = _rms_fwd(kh, kw, QK_HEAD)
        ks += [kn[:, :QK_NOPE], _rope(kn[:, QK_NOPE:], cos, sin_s)]
        vs.append(kvf[:, HP * h + QK_NOPE:HP * (h + 1)])
    return (jnp.concatenate(qs, axis=1), jnp.concatenate(ks, axis=1), jnp.concatenate(vs, axis=1)), ()


def _f_mla_qk_bwd(i, qf, kvf, kpe, cos, sin_s, dq, dk, dv, qw, kw):
    dqf, dkvf = [], []
    dkpe = None
    dqw = None
    dkw = None
    for h in range(MLA_HEADS):
        dqh = dq[:, HP * h:HP * (h + 1)]
        dqn = jnp.concatenate([dqh[:, :QK_NOPE], _rope_t(dqh[:, QK_NOPE:], cos, sin_s)], axis=1)
        dx, dwr = _rms_bwd(qf[:, HP * h:HP * (h + 1)], qw, dqn, QK_HEAD)
        dqf.append(dx)
        dqw = _rowsum(dwr) if dqw is None else dqw + _rowsum(dwr)
        dkh = dk[:, HP * h:HP * (h + 1)]
        dkn = jnp.concatenate([dkh[:, :QK_NOPE], _rope_t(dkh[:, QK_NOPE:], cos, sin_s)], axis=1)
        kh = jnp.concatenate([kvf[:, HP * h:HP * h + QK_NOPE], kpe], axis=1)
        dx, dwr = _rms_bwd(kh, kw, dkn, QK_HEAD)
        dkvf += [dx[:, :QK_NOPE], dv[:, V_HEAD * h:V_HEAD * (h + 1)]]
        dkpe = dx[:, QK_NOPE:] if dkpe is None else dkpe + dx[:, QK_NOPE:]
        dkw = _rowsum(dwr) if dkw is None else dkw + _rowsum(dwr)
    return (jnp.concatenate(dqf, axis=1), jnp.concatenate(dkvf, axis=1), dkpe), (dqw, dkw)


def _gdn_act(i, x, halo, w8):
    tm = x.shape[0]
    halo = jnp.where(i > 0, halo, 0.0)
    c = _conv_fwd(x, halo, w8, DN_CONV)
    act, dact = _silu_parts(c)
    return act, dact


def _spread_heads(ab):
    tm = ab.shape[0]
    return jnp.concatenate([jnp.broadcast_to(ab[:, h:h + 1], (tm, DN_DIM)) for h in range(2 * DN_HEADS)], axis=1)


def _gather_heads(x):
    tm = x.shape[0]
    lane = lax.broadcasted_iota(jnp.int32, (tm, LANE), 1)
    out = jnp.zeros((tm, LANE), F32)
    for h in range(2 * DN_HEADS):
        out = out + jnp.where(lane == h, x[:, DN_DIM * h:DN_DIM * h + 1], 0.0)
    return out


def _f_gdn_prep(i, x, halo, ab, w8, alog, dtb):
    tm = x.shape[0]
    act, _ = _gdn_act(i, x, halo, w8)
    outs = []
    for part in range(2):
        for h in range(DN_HEADS):
            t = act[:, DN_WIDTH * part + DN_DIM * h:DN_WIDTH * part + DN_DIM * (h + 1)]
            outs.append(t * lax.rsqrt(jnp.sum(t * t, axis=-1, keepdims=True) + EPS))
    q = jnp.concatenate(outs[:DN_HEADS], axis=1)
    k = jnp.concatenate(outs[DN_HEADS:], axis=1)
    v = act[:, 2 * DN_WIDTH:]
    abb = _spread_heads(ab)
    valid = _row_ids(i, tm) >= PAD
    g = jnp.where(valid, -jnp.exp(alog) * _softplus(abb[:, :DN_WIDTH] + dtb), 0.0)
    beta = jnp.where(valid, _sigmoid(abb[:, DN_WIDTH:]), 0.0)
    return (q, k, v, g, beta), ()


def _f_gdn_prep_bwd(i, x, x_prev, x_next, ab, dq, dq_next, dk, dk_next, dv, dv_next, dg, dbeta,
                    w8, alog, dtb, *, nt):
    tm = x.shape[0]
    x_prev = jnp.where(i > 0, x_prev, 0.0)
    more = i < nt - 1
    ext = lambda t, t_next: jnp.concatenate([t, jnp.where(more, t_next, 0.0)], axis=0)
    c = _conv_fwd(jnp.concatenate([x, x_next], axis=0), x_prev, w8, DN_CONV)
    act, dact = _silu_parts(c)
    douts = []
    for part, dd in enumerate((ext(dq, dq_next), ext(dk, dk_next))):
        for h in range(DN_HEADS):
            t = act[:, DN_WIDTH * part + DN_DIM * h:DN_WIDTH * part + DN_DIM * (h + 1)]
            r = lax.rsqrt(jnp.sum(t * t, axis=-1, keepdims=True) + EPS)
            y = t * r
            dy = dd[:, DN_DIM * h:DN_DIM * (h + 1)]
            douts.append(r * (dy - y * jnp.sum(dy * y, axis=-1, keepdims=True)))
    douts.append(ext(dv, dv_next))
    dc = jnp.concatenate(douts, axis=1) * dact
    dqkv = _conv_bwd_x(dc[:tm], dc[tm:], w8, DN_CONV)
    dconv_w = _conv_bwd_w(dc[:tm], x, x_prev, DN_CONV)
    abb = _spread_heads(ab)
    valid = _row_ids(i, tm) >= PAD
    pre = abb[:, :DN_WIDTH] + dtb
    ea = jnp.exp(alog)
    g = -ea * _softplus(pre)
    dg = jnp.where(valid, dg, 0.0)
    dbeta = jnp.where(valid, dbeta, 0.0)
    da = dg * (-ea) * _sigmoid(pre)
    beta = _sigmoid(abb[:, DN_WIDTH:])
    db = dbeta * beta * (1.0 - beta)
    dab = _gather_heads(jnp.concatenate([da, db], axis=1))
    return (dqkv, dab), (dconv_w, _rowsum(dg * g), _rowsum(da))


def _f_conv_bwd(i, dy, dy_next, x, x_prev, w8, *, width, nt):
    dy_next = jnp.where(i < nt - 1, dy_next, 0.0)
    x_prev = jnp.where(i > 0, x_prev, 0.0)
    return (_conv_bwd_x(dy, dy_next, w8, width),), (_conv_bwd_w(dy, x, x_prev, width),)


def _f_mix(i, o_mla, o_dn, z, w_mla, w_dn):
    tm = o_mla.shape[0]
    valid = _row_ids(i, tm) >= PAD
    outs = []
    for h in range(MLA_HEADS):
        y, _ = _rms_fwd(o_mla[:, V_HEAD * h:V_HEAD * (h + 1)], w_mla, V_HEAD)
        outs.append(jnp.where(valid, y, 0.0))
    for h in range(DN_HEADS):
        y, _ = _rms_fwd(o_dn[:, DN_DIM * h:DN_DIM * (h + 1)], w_dn, DN_DIM)
        outs.append(y * _silu_parts(z[:, DN_DIM * h:DN_DIM * (h + 1)])[0])
    return (jnp.concatenate(outs, axis=1),), ()


def _f_mix_bwd(i, o_mla, o_dn, z, dy_mla, dy_dn, w_mla, w_dn):
    tm = o_mla.shape[0]
    valid = _row_ids(i, tm) >= PAD
    d_mla, d_dn, d_z = [], [], []
    dw_mla = None
    dw_dn = None
    for h in range(MLA_HEADS):
        sl = slice(V_HEAD * h, V_HEAD * (h + 1))
        dx, dwr = _rms_bwd(o_mla[:, sl], w_mla, jnp.where(valid, dy_mla[:, sl], 0.0), V_HEAD)
        d_mla.append(dx)
        dw_mla = _rowsum(dwr) if dw_mla is None else dw_mla + _rowsum(dwr)
    for h in range(DN_HEADS):
        sl = slice(DN_DIM * h, DN_DIM * (h + 1))
        y, _ = _rms_fwd(o_dn[:, sl], w_dn, DN_DIM)
        sz, dsz = _silu_parts(z[:, sl])
        d_z.append(dy_dn[:, sl] * y * dsz)
        dx, dwr = _rms_bwd(o_dn[:, sl], w_dn, dy_dn[:, sl] * sz, DN_DIM)
        d_dn.append(dx)
        dw_dn = _rowsum(dwr) if dw_dn is None else dw_dn + _rowsum(dwr)
    return ((jnp.concatenate(d_mla, axis=1), jnp.concatenate(d_dn, axis=1), jnp.concatenate(d_z, axis=1)),
            (dw_mla, dw_dn))


def _f_ffn_act(i, gate_pre, halo, up, w8, b):
    halo = jnp.where(i > 0, halo, 0.0)
    gate = _conv_fwd(gate_pre, halo, w8, FFN_CONV) + b
    return (_silu_parts(gate)[0] * up,), ()


def _f_ffn_act_bwd(i, gp, gp_prev, gp_next, up, up_next, dact, dact_next, w8, b, *, nt):
    tm = gp.shape[0]
    gp_prev = jnp.where(i > 0, gp_prev, 0.0)
    dact_next = jnp.where(i < nt - 1, dact_next, 0.0)
    cat = lambda t, t_next: jnp.concatenate([t, t_next], axis=0)
    gate = _conv_fwd(cat(gp, gp_next), gp_prev, w8, FFN_CONV) + b
    sg, dsg = _silu_parts(gate)
    dact_e = cat(dact, dact_next)
    dgate = dact_e * cat(up, up_next) * dsg
    dgate_pre = _conv_bwd_x(dgate[:tm], dgate[tm:], w8, FFN_CONV)
    dup = dact * sg[:tm]
    return (dgate_pre, dup), (_conv_bwd_w(dgate[:tm], gp, gp_prev, FFN_CONV), _rowsum(dgate[:tm]))


def _f_loss(i, h3, tgt):
    tm = h3.shape[0]
    diff = jnp.where(_row_ids(i, tm) >= ROW0, h3 - tgt, 0.0)
    part = 0.5 * jnp.sum(diff * diff) * (1.0 / D_MODEL)
    return (diff * (1.0 / D_MODEL),), (jnp.full((1, LANE), part, F32),)


def _after(fn):
    return lambda i, *a: fn(i, *a[:-1])


def _local_step(h0, tgt, w, token, late_weights, grads_ready):
    tp = h0.shape[0]
    nt = tp // TM
    bf = (D_MODEL, _MXU)
    proj, u = _norm_mm("in_proj", h0, w["attn_norm_w"], w["w_in"], after=token)
    p_qkv = lambda kind="cur": _In(proj, 3 * DN_WIDTH, 0, kind)
    p_z = _In(proj, DN_WIDTH, C_Z // DN_WIDTH)
    p_ql = _In(proj, Q_LORA, C_QL // Q_LORA)
    p_kvl = _In(proj, KV_LORA, C_KVL // KV_LORA)
    p_kpe = _In(proj, LANE, C_KPE // LANE)
    p_ab = _In(proj, LANE, C_AB // LANE)
    cos, sin_s = _In(w["cos"]), _In(w["sin_s"])

    qf, qn = _norm_mm("mla_q_b", proj, w["q_a_norm_w"], w["w_q_b"], "nt", C_QL // Q_LORA)
    kvf, kvn = _norm_mm("mla_kv_b", proj, w["kv_a_norm_w"], w["w_kv_b"], "nn", C_KVL // KV_LORA)
    qk_w = [w["q_norm_w"], w["k_norm_w"]]
    q, k, v = _rows("mla_qk", _f_mla_qk, [_In(qf), _In(kvf), p_kpe, cos, sin_s], qk_w,
                    [(MLA_HEADS * HP, _MXU), (MLA_HEADS * HP, _MXU), (MLA_HEADS * V_HEAD, _MXU)])
    o_mla = _attn_fwd(q, k, v)

    dn_w = [w["dn_conv_w"], w["alog_b"], w["dtb_b"]]
    gq, gk, gv, gg, gb = _rows("gdn_prep", _f_gdn_prep, [p_qkv(), p_qkv("prev"), p_ab], dn_w,
                               [(DN_WIDTH, F32)] * 5)
    o_dn, s_all, t_all = _gdn_fwd(gq, gk, gv, gg, gb)

    out_w = [w["mla_out_norm_w"], w["dn_out_norm_w"]]
    mixed, = _rows("mix", _f_mix, [_In(o_mla), _In(o_dn), p_z], out_w, [bf])
    w = dict(w, **late_weights(mixed))
    h2 = _mm("out_proj", mixed, w["w_out"], "nn", resid=h0)

    gate_pre, hn = _norm_mm("ffn_gate", h2, w["ffn_norm_w"], w["w_gate"])
    ffn_w = [w["ffn_conv_w"], w["ffn_conv_b"]]
    act, up = _mm_rows(
        "ffn_up_act", hn, w["w_up"], "nt",
        lambda i, up_t, gp, gp_prev, w8, b: ((_f_ffn_act(i, gp, gp_prev, up_t, w8, b)[0][0], up_t), ()),
        [_In(gate_pre), _In(gate_pre, kind="prev")], ffn_w, [(D_FF, _MXU), (D_FF, F32)], tm_cap=288)
    dh3, loss = _mm_rows("ffn_down_loss", act, w["w_down"], "nn", lambda i, y, r, t: _f_loss(i, r + y, t),
                         [_In(h2), _In(tgt)], [], [(D_MODEL, F32)], [(1, LANE)])

    g = {}
    dact = _mm("ffn_down_dx", dh3, w["w_down"], "nt")
    g["w_down"] = _mm("ffn_down_dw", act, dh3, "tn", out_dtype=_MXU)
    dgate_pre, dup, g["ffn_conv_w"], g["ffn_conv_b"] = _rows(
        "ffn_act_bwd", functools.partial(_f_ffn_act_bwd, nt=nt),
        [_In(gate_pre), _In(gate_pre, kind="prev"), _In(gate_pre, kind="next"), _In(up), _In(up, kind="next"),
         _In(dact), _In(dact, kind="next")], ffn_w,
        [(D_FF, _MXU), (D_FF, _MXU)], [(8, D_FF), (1, D_FF)])
    g["w_gate"] = _mm("ffn_gate_dw", dgate_pre, hn, "tn", out_dtype=_MXU)
    g["w_up"] = _mm("ffn_up_dw", dup, hn, "tn", out_dtype=_MXU)
    tok = grads_ready(g, ("w_down", "w_gate", "w_up"))
    dhn = _mm("ffn_gate_dx", dgate_pre, w["w_gate"], "nn", after=tok)
    dh2, g["ffn_norm_w"] = _mm_rows(
        "ffn_up_dx_rms", dup, w["w_up"], "nn",
        lambda i, y, d1, x, dres, nw: _f_rms_bwd_add(i, x, d1 + y, dres, nw, mask_pad=True),
        [_In(dhn), _In(h2), _In(dh3)], [w["ffn_norm_w"]], [(D_MODEL, F32)], [(1, D_MODEL)])

    g["w_out"] = _mm("out_proj_dw", mixed, dh2, "tn", out_dtype=_MXU)
    half = MLA_HEADS * V_HEAD
    do_mla, do_dn, dz, g["mla_out_norm_w"], g["dn_out_norm_w"] = _mm_rows(
        "out_proj_dx_mix", dh2, w["w_out"], "nt",
        lambda i, dm, om, od, z, wm, wd: _f_mix_bwd(i, om, od, z, dm[:, :half], dm[:, half:], wm, wd),
        [_In(o_mla), _In(o_dn), p_z], out_w,
        [(half, F32), (DN_WIDTH, F32), (DN_WIDTH, _MXU)], [(1, V_HEAD), (1, DN_DIM)])

    dq, dk, dv = _attn_bwd(q, k, v, do_mla)
    dqf, dkvf, dkpe, g["q_norm_w"], g["k_norm_w"] = _rows(
        "mla_qk_bwd", _f_mla_qk_bwd, [_In(qf), _In(kvf), p_kpe, cos, sin_s, _In(dq), _In(dk), _In(dv)], qk_w,
        [(MLA_HEADS * HP, _MXU), (MLA_HEADS * HP, _MXU), (LANE, _MXU)], [(1, HP), (1, HP)])
    g["w_q_b"] = _mm("mla_q_b_dw", dqf, qn, "tn")
    g["w_kv_b"] = _mm("mla_kv_b_dw", kvn, dkvf, "tn")
    tok = grads_ready(g, ("w_out", "w_q_b", "w_kv_b"))

    def lat_bwd(n):
        def fn(i, dy, x, nw, _tok):
            dx, dwr = _rms_bwd(x, nw, dy, n)
            return (dx,), (_rowsum(dwr),)
        return fn

    dql, g["q_a_norm_w"] = _mm_rows("mla_q_b_dx", dqf, w["w_q_b"], "nn", lat_bwd(Q_LORA), [p_ql],
                                    [w["q_a_norm_w"], tok], [(Q_LORA, _MXU)], [(1, Q_LORA)])
    dkvl, g["kv_a_norm_w"] = _mm_rows("mla_kv_b_dx", dkvf, w["w_kv_b"], "nt", lat_bwd(KV_LORA), [p_kvl],
                                      [w["kv_a_norm_w"], tok], [(KV_LORA, _MXU)], [(1, KV_LORA)])

    dgq, dgk, dgv, dgg, dgb = _gdn_bwd(gq, gk, gv, gg, gb, s_all, t_all, do_dn)
    nxt = lambda a: _In(a, kind="next")
    dqkv, dab, g["dn_conv_w"], g["alog_b"], g["dtb_b"] = _rows(
        "gdn_prep_bwd", functools.partial(_f_gdn_prep_bwd, nt=nt),
        [p_qkv(), p_qkv("prev"), p_qkv("next"), p_ab, _In(dgq), nxt(dgq), _In(dgk), nxt(dgk), _In(dgv), nxt(dgv),
         _In(dgg), _In(dgb)], dn_w,
        [(3 * DN_WIDTH, _MXU), (LANE, _MXU)], [(8, 3 * DN_WIDTH), (1, DN_WIDTH), (1, DN_WIDTH)])

    dproj = jnp.concatenate([dqkv, dz, dql, dkvl, dkpe, dab], axis=1)
    g["w_in"] = _mm("in_proj_dw", dproj, u, "tn", out_dtype=_MXU)
    tok = grads_ready(g, ("w_in",))
    dh0, g["attn_norm_w"] = _mm_rows(
        "in_proj_dx_rms", dproj, w["w_in"], "nn",
        lambda i, du, x, dres, nw, _tok: _f_rms_bwd_add(i, x, du, dres, nw, mask_pad=False),
        [_In(h0), _In(dh2)], [w["attn_norm_w"], tok], [(D_MODEL, F32)], [(1, D_MODEL)])
    return loss, dh0, g


def _w_in_to_padded(w):
    c1, c2, c3 = Q_LORA, Q_LORA + KV_LORA, Q_LORA + KV_LORA + QK_ROPE
    c4 = c3 + 3 * DN_WIDTH
    c5 = c4 + DN_WIDTH
    z = lambda n: jnp.zeros((n, w.shape[1]), w.dtype)
    return jnp.concatenate([w[c3:c4], w[c4:c5], w[:c1], w[c1:c2], w[c2:c3], z(LANE - QK_ROPE),
                            w[c5:], z(LANE - 2 * DN_HEADS)], axis=0)


def _w_in_from_padded(g):
    return jnp.concatenate([g[C_QL:C_QL + Q_LORA], g[C_KVL:C_KVL + KV_LORA], g[C_KPE:C_KPE + QK_ROPE],
                            g[:C_Z + DN_WIDTH], g[C_AB:C_AB + 2 * DN_HEADS]], axis=0)


def _w_q_b_to_padded(w):
    r = w.shape[1]
    w = w.reshape(MLA_HEADS, QK_HEAD, r)
    return jnp.pad(w, ((0, 0), (0, HP - QK_HEAD), (0, 0))).reshape(MLA_HEADS * HP, r)


def _w_q_b_from_padded(g):
    r = g.shape[1]
    return g.reshape(MLA_HEADS, HP, r)[:, :QK_HEAD].reshape(MLA_HEADS * QK_HEAD, r)


def _pad_rows8(w):
    return jnp.pad(w, ((0, 8 - w.shape[0]), (0, 0)))


def _prepare(full, tp):
    w = {}
    mx = lambda a: a.astype(_MXU)
    w["attn_norm_w"] = full["attn_norm_w"]
    w["w_in"] = mx(_w_in_to_padded(full["w_in"]))
    w["q_a_norm_w"] = full["q_a_norm_w"]
    w["kv_a_norm_w"] = full["kv_a_norm_w"]
    w["w_q_b"] = mx(_w_q_b_to_padded(full["w_q_b"]))
    w["w_kv_b"] = mx(full["w_kv_b"])
    w["q_norm_w"] = jnp.pad(full["q_norm_w"], ((0, 0), (0, HP - QK_HEAD)))
    w["k_norm_w"] = jnp.pad(full["k_norm_w"], ((0, 0), (0, HP - QK_HEAD)))
    w["mla_out_norm_w"] = full["mla_out_norm_w"]
    w["dn_out_norm_w"] = full["dn_out_norm_w"]
    w["dn_conv_w"] = _pad_rows8(full["dn_conv_w"])
    w["alog_b"] = jnp.repeat(full["dn_A_log"], DN_DIM, axis=1)
    w["dtb_b"] = jnp.repeat(full["dn_dt_bias"], DN_DIM, axis=1)
    w["ffn_norm_w"] = full["ffn_norm_w"]
    w["ffn_conv_w"] = _pad_rows8(full["ffn_conv_w"])
    w["ffn_conv_b"] = full["ffn_conv_b"]
    for n in _LATE:
        if n in full:
            w[n] = mx(full[n])
    half = QK_ROPE // 2
    inv = ROPE_THETA ** (-jnp.arange(half, dtype=F32) / half)
    ang = (jnp.arange(tp, dtype=jnp.int32) - PAD).astype(F32)[:, None] * inv[None, :]
    zc = jnp.zeros((tp, LANE - QK_ROPE), F32)
    w["cos"] = jnp.concatenate([jnp.cos(ang), jnp.cos(ang), zc], axis=1)
    w["sin_s"] = jnp.concatenate([-jnp.sin(ang), jnp.sin(ang), zc], axis=1)
    return w


def _grads_to_natural(g):
    convert = {
        "w_in": ("w_in", _w_in_from_padded),
        "w_q_b": ("w_q_b", _w_q_b_from_padded),
        "q_norm_w": ("q_norm_w", lambda a: a[:, :QK_HEAD]),
        "k_norm_w": ("k_norm_w", lambda a: a[:, :QK_HEAD]),
        "dn_conv_w": ("dn_conv_w", lambda a: a[:DN_CONV]),
        "ffn_conv_w": ("ffn_conv_w", lambda a: a[:FFN_CONV]),
        "alog_b": ("dn_A_log", lambda a: a[:, ::DN_DIM]),
        "dtb_b": ("dn_dt_bias", lambda a: a[:, ::DN_DIM]),
    }
    n = {}
    for key, a in g.items():
        name, fn = convert.get(key, (key, lambda t: t))
        n[name] = fn(a)
    return n


_MESH = pl.DeviceIdType.MESH
_ANY = pl.BlockSpec(memory_space=pl.ANY)
_CHIP_FLIPS = ((1, 0), (0, 1), (1, 1))


def _me():
    return lax.axis_index("x"), lax.axis_index("y"), lax.axis_index("c")


def _all_gather(name, blk):
    def body(x_ref, out_ref, send_sems, recv_sems, local_sem):
        x, y, c = _me()
        me, sib = (x, y, c), (x, y, 1 - c)
        chips = [(x ^ fx, y ^ fy) for fx, fy in _CHIP_FLIPS]

        def slot(p):
            return out_ref.at[4 * p[0] + 2 * p[1] + p[2]]

        def copy(k, block, to, src=None):
            return pltpu.make_async_remote_copy(
                src_ref=slot(block) if src is None else src, dst_ref=slot(block),
                send_sem=send_sems.at[k], recv_sem=recv_sems.at[k], device_id=to, device_id_type=_MESH)

        mine = pltpu.make_async_copy(x_ref, slot(me), local_sem)
        mine.start()
        first = [copy(0, me, sib, src=x_ref)]
        first += [copy(1 + j, me, (*chip, c), src=x_ref) for j, chip in enumerate(chips)]
        for cp in first:
            cp.start()
        passed = [copy(4 + j, (*chip, c), sib) for j, chip in enumerate(chips)]
        for j, chip in enumerate(chips):
            copy(1 + j, (*chip, c), me).wait_recv()
            passed[j].start()
        copy(0, sib, me).wait_recv()
        for j, chip in enumerate(chips):
            copy(4 + j, (*chip, 1 - c), me).wait_recv()
        for cp in first + passed:
            cp.wait_send()
        mine.wait()

    return pl.pallas_call(
        body, name=name, in_specs=[_ANY], out_specs=_ANY,
        out_shape=jax.ShapeDtypeStruct((N_DEV,) + blk.shape, blk.dtype),
        scratch_shapes=[pltpu.SemaphoreType.DMA((7,)), pltpu.SemaphoreType.DMA((7,)), pltpu.SemaphoreType.DMA],
    )(blk)


def _rs_sibling(name, gb):
    def body(g_ref, out_ref, send_sems, recv_sems):
        x, y, c = _me()
        cps = []
        for j in range(4):
            cp = pltpu.make_async_remote_copy(
                src_ref=g_ref.at[2 * j + (1 - c)], dst_ref=out_ref.at[j], send_sem=send_sems.at[j],
                recv_sem=recv_sems.at[j], device_id=(x, y, 1 - c), device_id_type=_MESH)
            cp.start()
            cps.append(cp)
        for cp in cps:
            cp.wait()

    return pl.pallas_call(
        body, name=name, in_specs=[_ANY], out_specs=_ANY,
        out_shape=jax.ShapeDtypeStruct((4,) + gb.shape[1:], gb.dtype),
        scratch_shapes=[pltpu.SemaphoreType.DMA((4,)), pltpu.SemaphoreType.DMA((4,))],
    )(gb)


def _rs_chips(name, s1):
    def body(s_ref, out_ref, send_sems, recv_sems):
        x, y, c = _me()
        cps = []
        for k, (fx, fy) in enumerate(_CHIP_FLIPS):
            px, py = x ^ fx, y ^ fy
            cp = pltpu.make_async_remote_copy(
                src_ref=s_ref.at[2 * px + py], dst_ref=out_ref.at[k], send_sem=send_sems.at[k],
                recv_sem=recv_sems.at[k], device_id=(px, py, c), device_id_type=_MESH)
            cp.start()
            cps.append(cp)
        for cp in cps:
            cp.wait()

    return pl.pallas_call(
        body, name=name, in_specs=[_ANY], out_specs=_ANY,
        out_shape=jax.ShapeDtypeStruct((3,) + s1.shape[1:], s1.dtype),
        scratch_shapes=[pltpu.SemaphoreType.DMA((3,)), pltpu.SemaphoreType.DMA((3,))],
    )(s1)


def _row_tile(r):
    divs = [d for d in range(16, min(r, 512) + 1, 16) if r % d == 0]
    return divs[-1] if divs else r


def _pair_sum(name, gb, recv):
    _, r, cols = gb.shape
    tm = _row_tile(r)
    c = lax.axis_index("c").astype(jnp.int32).reshape(1)

    def body(c_ref, a_ref, b_ref, o_ref, ob_ref):
        s = a_ref[...] + b_ref[...]
        o_ref[...] = s
        ob_ref[...] = s.astype(BF16)

    blk = pl.BlockSpec((1, tm, cols), lambda j, i, c_ref: (j, i, 0))
    return pl.pallas_call(
        body, name=name,
        grid_spec=pltpu.PrefetchScalarGridSpec(
            num_scalar_prefetch=1, grid=(4, r // tm),
            in_specs=[pl.BlockSpec((1, tm, cols), lambda j, i, c_ref: (2 * j + c_ref[0], i, 0)), blk],
            out_specs=[blk, blk]),
        out_shape=[jax.ShapeDtypeStruct((4, r, cols), F32), jax.ShapeDtypeStruct((4, r, cols), BF16)],
        compiler_params=pltpu.CompilerParams(dimension_semantics=("parallel", "parallel")),
    )(c, gb, recv)


def _adam_math(g, w, m, v):
    m_new = ADAM_B1 * m + (1.0 - ADAM_B1) * g
    v_new = ADAM_B2 * v + (1.0 - ADAM_B2) * (g * g)
    m_hat = m_new / (1.0 - ADAM_B1 ** ADAM_STEP)
    v_hat = v_new / (1.0 - ADAM_B2 ** ADAM_STEP)
    return -ADAM_LR * (m_hat / (jnp.sqrt(v_hat) + ADAM_EPS) + ADAM_WD * w), m_new, v_new


def _adam_vectors(name, row, items, ws, ms, vs):
    k = len(items)

    def body(row_ref, *refs):
        w_refs, m_refs, v_refs = refs[:k], refs[k:2 * k], refs[2 * k:3 * k]
        outs = refs[3 * k:]
        for idx, (off, n, per_head) in enumerate(items):
            if per_head:
                spread = row_ref[:, off:off + DN_WIDTH]
                lane = lax.broadcasted_iota(jnp.int32, (1, LANE), 1)
                g = jnp.zeros((1, LANE), F32)
                for h in range(DN_HEADS):
                    g = g + jnp.where(lane == h, spread[:, DN_DIM * h:DN_DIM * h + 1], 0.0)
            else:
                g = row_ref[:, off:off + n]
            d, m_new, v_new = _adam_math(g, w_refs[idx][...], m_refs[idx][...], v_refs[idx][...])
            for kind, val in enumerate((g, d, m_new, v_new)):
                outs[kind * k + idx][...] = val

    shapes = [jax.ShapeDtypeStruct((1, n), F32) for _, n, _ in items]
    res = pl.pallas_call(body, name=name, out_shape=shapes * 4)(row, *ws, *ms, *vs)
    return [list(res[kind * k:(kind + 1) * k]) for kind in range(4)]


def _sum_parts(name, parts):
    _, r, cols = parts[0][0].shape
    tm = _row_tile(r)
    idx = jnp.stack([jnp.asarray(s, jnp.int32) for _, s in parts])
    n = len(parts)

    def body(idx_ref, *refs):
        g = refs[0][0].astype(F32)
        for p_ref in refs[1:n]:
            g = g + p_ref[0].astype(F32)
        refs[n][...] = g

    return pl.pallas_call(
        body, name=name,
        grid_spec=pltpu.PrefetchScalarGridSpec(
            num_scalar_prefetch=1, grid=(r // tm,),
            in_specs=[pl.BlockSpec((1, tm, cols), lambda i, idx_ref, p=p: (idx_ref[p], i, 0)) for p in range(n)],
            out_specs=pl.BlockSpec((tm, cols), lambda i, idx_ref: (i, 0))),
        out_shape=jax.ShapeDtypeStruct((r, cols), F32),
        compiler_params=pltpu.CompilerParams(dimension_semantics=("parallel",)),
    )(idx, *[a for a, _ in parts])


def _adam(name, parts, w, m, v):
    r, cols = w.shape
    tm = _row_tile(r)
    idx = jnp.stack([jnp.asarray(s, jnp.int32) for _, s in parts])
    n = len(parts)

    def body(idx_ref, *refs):
        g = refs[0][0].astype(F32)
        for p_ref in refs[1:n]:
            g = g + p_ref[0].astype(F32)
        w_ref, m_ref, v_ref, g_out, d_out, m_out, v_out = refs[n:]
        g_out[...] = g
        d_out[...], m_out[...], v_out[...] = _adam_math(g, w_ref[...], m_ref[...], v_ref[...])

    part_specs = [pl.BlockSpec((1, tm, cols), lambda i, idx_ref, p=p: (idx_ref[p], i, 0)) for p in range(n)]
    flat = pl.BlockSpec((tm, cols), lambda i, idx_ref: (i, 0))
    return pl.pallas_call(
        body, name=name,
        grid_spec=pltpu.PrefetchScalarGridSpec(
            num_scalar_prefetch=1, grid=(r // tm,), in_specs=part_specs + [flat] * 3, out_specs=[flat] * 4),
        out_shape=[jax.ShapeDtypeStruct((r, cols), F32)] * 4,
        compiler_params=pltpu.CompilerParams(dimension_semantics=("parallel",)),
    )(idx, *[a for a, _ in parts], w, m, v)


def _all_gather_many(name, blks):
    n = len(blks)

    def body(*refs):
        x_refs, out_refs = refs[:n], refs[n:2 * n]
        send_sems, recv_sems, local_sems = refs[2 * n:]
        x, y, c = _me()
        me, sib = (x, y, c), (x, y, 1 - c)
        chips = [(x ^ fx, y ^ fy) for fx, fy in _CHIP_FLIPS]

        def slot(a, p):
            return out_refs[a].at[4 * p[0] + 2 * p[1] + p[2]]

        def copy(a, k, block, to, src=None):
            return pltpu.make_async_remote_copy(
                src_ref=slot(a, block) if src is None else src, dst_ref=slot(a, block),
                send_sem=send_sems.at[7 * a + k], recv_sem=recv_sems.at[7 * a + k], device_id=to,
                device_id_type=_MESH)

        mine = [pltpu.make_async_copy(x_refs[a], slot(a, me), local_sems.at[a]) for a in range(n)]
        first = []
        for a in range(n):
            mine[a].start()
            first.append(copy(a, 0, me, sib, src=x_refs[a]))
            first += [copy(a, 1 + j, me, (*chip, c), src=x_refs[a]) for j, chip in enumerate(chips)]
        for cp in first:
            cp.start()
        passed = []
        for j, chip in enumerate(chips):
            for a in range(n):
                copy(a, 1 + j, (*chip, c), me).wait_recv()
                cp = copy(a, 4 + j, (*chip, c), sib)
                cp.start()
                passed.append(cp)
        for a in range(n):
            copy(a, 0, sib, me).wait_recv()
            for j, chip in enumerate(chips):
                copy(a, 4 + j, (*chip, 1 - c), me).wait_recv()
        for cp in first + passed:
            cp.wait_send()
        for cp in mine:
            cp.wait()

    return pl.pallas_call(
        body, name=name, in_specs=[_ANY] * n, out_specs=[_ANY] * n,
        out_shape=[jax.ShapeDtypeStruct((N_DEV,) + b.shape, b.dtype) for b in blks],
        scratch_shapes=[pltpu.SemaphoreType.DMA((7 * n,)), pltpu.SemaphoreType.DMA((7 * n,)),
                        pltpu.SemaphoreType.DMA((n,))],
    )(*blks)


def _rs_sibling_many(name, gbs):
    n = len(gbs)

    def body(*refs):
        g_refs, out_refs = refs[:n], refs[n:2 * n]
        send_sems, recv_sems = refs[2 * n:]
        x, y, c = _me()
        cps = []
        for a in range(n):
            for j in range(4):
                cp = pltpu.make_async_remote_copy(
                    src_ref=g_refs[a].at[2 * j + (1 - c)], dst_ref=out_refs[a].at[j],
                    send_sem=send_sems.at[4 * a + j], recv_sem=recv_sems.at[4 * a + j],
                    device_id=(x, y, 1 - c), device_id_type=_MESH)
                cp.start()
                cps.append(cp)
        for cp in cps:
            cp.wait()

    return pl.pallas_call(
        body, name=name, in_specs=[_ANY] * n, out_specs=[_ANY] * n,
        out_shape=[jax.ShapeDtypeStruct((4,) + g.shape[1:], g.dtype) for g in gbs],
        scratch_shapes=[pltpu.SemaphoreType.DMA((4 * n,)), pltpu.SemaphoreType.DMA((4 * n,))],
    )(*gbs)


def _rs_chips_many(name, s1s):
    n = len(s1s)

    def body(*refs):
        s_refs, out_refs = refs[:n], refs[n:2 * n]
        send_sems, recv_sems = refs[2 * n:]
        x, y, c = _me()
        cps = []
        for a in range(n):
            for k, (fx, fy) in enumerate(_CHIP_FLIPS):
                px, py = x ^ fx, y ^ fy
                cp = pltpu.make_async_remote_copy(
                    src_ref=s_refs[a].at[2 * px + py], dst_ref=out_refs[a].at[k],
                    send_sem=send_sems.at[3 * a + k], recv_sem=recv_sems.at[3 * a + k],
                    device_id=(px, py, c), device_id_type=_MESH)
                cp.start()
                cps.append(cp)
        for cp in cps:
            cp.wait()

    return pl.pallas_call(
        body, name=name, in_specs=[_ANY] * n, out_specs=[_ANY] * n,
        out_shape=[jax.ShapeDtypeStruct((3,) + s.shape[1:], s.dtype) for s in s1s],
        scratch_shapes=[pltpu.SemaphoreType.DMA((3 * n,)), pltpu.SemaphoreType.DMA((3 * n,))],
    )(*s1s)


_HBM = pl.BlockSpec(memory_space=pltpu.HBM)
_SEM = pl.BlockSpec(memory_space=pltpu.SEMAPHORE)
_EFFECT = pltpu.SideEffectType.DATAFLOW_SIDE_EFFECTING


def _push_copies(src_refs, land_refs, send_sems, recv_sems, src_by_peer):
    x, y, c = _me()
    my_id = 4 * x + 2 * y + c
    out = []
    for a in range(len(src_refs)):
        for f in range(1, N_DEV):
            px, py, pc = x ^ (f >> 2), y ^ ((f >> 1) & 1), c ^ (f & 1)
            pid = 4 * px + 2 * py + pc
            src = src_refs[a].at[pid] if src_by_peer else src_refs[a]
            start = pltpu.make_async_remote_copy(
                src_ref=src, dst_ref=land_refs[a].at[my_id], send_sem=send_sems.at[7 * a + f - 1],
                recv_sem=recv_sems.at[7 * a + f - 1], device_id=(px, py, pc), device_id_type=_MESH)
            landed = pltpu.make_async_remote_copy(
                src_ref=src, dst_ref=land_refs[a].at[pid], send_sem=send_sems.at[7 * a + f - 1],
                recv_sem=recv_sems.at[7 * a + f - 1], device_id=(px, py, pc), device_id_type=_MESH)
            out.append((start, landed))
    return out


def _push_start(name, srcs, src_by_peer, after):
    n = len(srcs)
    lands = [jax.ShapeDtypeStruct((N_DEV,) + (s.shape[1:] if src_by_peer else s.shape), s.dtype) for s in srcs]

    def body(*refs):
        src_refs, land_refs = refs[:n], refs[n:2 * n]
        send_sems, recv_sems = refs[2 * n + 1], refs[2 * n + 2]
        token = refs[-1]
        for start, _ in _push_copies(src_refs, land_refs, send_sems, recv_sems, src_by_peer):
            start.start()
        token[...] = jnp.zeros_like(token)

    hbm = lambda a: pltpu.with_memory_space_constraint(a, pltpu.HBM)
    res = pl.pallas_call(
        body, name=name,
        out_shape=(pltpu.SemaphoreType.DMA((7 * n,)), pltpu.SemaphoreType.DMA((7 * n,)),
                   *[pltpu.HBM(s.shape, s.dtype) for s in srcs], *[pltpu.HBM(s.shape, s.dtype) for s in lands],
                   jax.ShapeDtypeStruct((8, LANE), F32)),
        in_specs=[_HBM] * (2 * n) + [_ANY],
        out_specs=(_SEM, _SEM, *[_HBM] * (2 * n), pl.BlockSpec(memory_space=pltpu.VMEM)),
        input_output_aliases={i: 2 + i for i in range(2 * n)},
        compiler_params=pltpu.CompilerParams(has_side_effects=_EFFECT),
    )(*[hbm(s) for s in srcs], *[hbm(lax.empty(s.shape, s.dtype)) for s in lands], after)
    return res[0], res[1], list(res[2:2 + n]), list(res[2 + n:2 + 2 * n]), res[-1]


def _push_wait(name, send_sems, recv_sems, srcs, lands, src_by_peer, after):
    n = len(srcs)

    def body(*refs):
        src_refs, land_refs = refs[:n], refs[n:2 * n]
        s_sems, r_sems = refs[2 * n], refs[2 * n + 1]
        for _, landed in _push_copies(src_refs, land_refs, s_sems, r_sems, src_by_peer):
            landed.wait_send()
            landed.wait_recv()

    res = pl.pallas_call(
        body, name=name,
        out_shape=tuple(pltpu.HBM(s.shape, s.dtype) for s in list(srcs) + list(lands)),
        in_specs=[_HBM] * (2 * n) + [_SEM, _SEM, _ANY],
        out_specs=tuple([_HBM] * (2 * n)),
        input_output_aliases={i: i for i in range(2 * n)},
        compiler_params=pltpu.CompilerParams(has_side_effects=_EFFECT),
    )(*srcs, *lands, send_sems, recv_sems, after)
    return list(res[:n]), list(res[n:])


_SHARDED = (
    ("meta_tokens", 1, (N_META, D_MODEL)),
    ("w_in", 1, (D_MODEL, IN_COLS)),
    ("w_q_b", 1, (Q_LORA, MLA_HEADS * QK_HEAD)),
    ("w_kv_b", 1, (KV_LORA, MLA_HEADS * (QK_NOPE + V_HEAD))),
    ("dn_conv_w", 1, (DN_CONV, 3 * DN_WIDTH)),
    ("w_out", 0, (2 * DN_WIDTH, D_MODEL)),
    ("w_gate", 1, (D_MODEL, D_FF)),
    ("w_up", 1, (D_MODEL, D_FF)),
    ("ffn_conv_w", 1, (FFN_CONV, D_FF)),
    ("w_down", 0, (D_FF, D_MODEL)),
)
_MXU_GATHERED = ("w_in", "w_q_b", "w_kv_b", "w_out", "w_gate", "w_up", "w_down")
_F32_GATHERED = ("meta_tokens", "dn_conv_w", "ffn_conv_w")
_EARLY = ("w_in", "w_q_b", "w_kv_b")
_LATE = ("w_out", "w_gate", "w_up", "w_down")
_TRANSPOSED = ("w_in", "w_q_b", "w_gate", "w_up")
_REPLICATED = (
    ("attn_norm_w", D_MODEL), ("q_a_norm_w", Q_LORA), ("kv_a_norm_w", KV_LORA), ("q_norm_w", QK_HEAD),
    ("k_norm_w", QK_HEAD), ("mla_out_norm_w", V_HEAD), ("dn_A_log", DN_HEADS), ("dn_dt_bias", DN_HEADS),
    ("dn_out_norm_w", DN_DIM), ("ffn_norm_w", D_MODEL), ("ffn_conv_b", D_FF),
)
_PACK_COLS = 1024
_PACK_ROW_MULT = 320
_SMALL_SHAPE = (8, 768)
_SMALL_BLOCK = (8, 512)


def _local_shape(dim, shape):
    return (shape[0] // N_DEV, shape[1]) if dim == 0 else (shape[0], shape[1] // N_DEV)


def _pack_rows(n, mult):
    rows = -(-n // _PACK_COLS)
    return -(-rows // mult) * mult


def _pack(flats, mult, axis=0):
    cat = jnp.concatenate(flats, axis=-1)
    n = cat.shape[-1]
    r = _pack_rows(n, mult)
    pad = [(0, 0)] * (cat.ndim - 1) + [(0, r * _PACK_COLS - n)]
    return jnp.pad(cat, pad).reshape(cat.shape[:-1] + (r, _PACK_COLS))


def _to_blocks(full, dim):
    r, c = full.shape
    if dim == 0:
        return full.reshape(N_DEV, (r // N_DEV) * c)
    return full.reshape(r, N_DEV, c // N_DEV).transpose(1, 0, 2).reshape(N_DEV, r * (c // N_DEV))


def _from_blocks(blocks, dim, shape):
    r, c = shape
    if dim == 0:
        return blocks.reshape(r, c)
    return blocks.reshape(N_DEV, r, c // N_DEV).transpose(1, 0, 2).reshape(r, c)


def _split(flat, sizes):
    out, o = [], 0
    for s in sizes:
        out.append(flat[..., o:o + s])
        o += s
    return out


def _gather_weights(local, names, dtype, mult):
    specs = [s for s in _SHARDED if s[0] in names]
    pack = _pack([local[n].astype(dtype).reshape(-1) for n, _, _ in specs], mult)
    got = _all_gather("gather_" + "_".join(n[:5] for n in names[:2]), pack)
    flat = got.reshape(N_DEV, -1)
    sizes = [math.prod(_local_shape(d, s)) for _, d, s in specs]
    return {n: _from_blocks(p, d, s) for (n, d, s), p in zip(specs, _split(flat, sizes))}


def kernel(x, meta_tokens, attn_norm_w, w_in, q_a_norm_w, w_q_b, kv_a_norm_w, w_kv_b, q_norm_w, k_norm_w, mla_out_norm_w, dn_conv_w, dn_A_log, dn_dt_bias, dn_out_norm_w, w_out, ffn_norm_w, w_gate, w_up, ffn_conv_w, ffn_conv_b, w_down, loss_target, m_meta_tokens, m_attn_norm_w, m_w_in, m_q_a_norm_w, m_w_q_b, m_kv_a_norm_w, m_w_kv_b, m_q_norm_w, m_k_norm_w, m_mla_out_norm_w, m_dn_conv_w, m_dn_A_log, m_dn_dt_bias, m_dn_out_norm_w, m_w_out, m_ffn_norm_w, m_w_gate, m_w_up, m_ffn_conv_w, m_ffn_conv_b, m_w_down, v_meta_tokens, v_attn_norm_w, v_w_in, v_q_a_norm_w, v_w_q_b, v_kv_a_norm_w, v_w_kv_b, v_q_norm_w, v_k_norm_w, v_mla_out_norm_w, v_dn_conv_w, v_dn_A_log, v_dn_dt_bias, v_dn_out_norm_w, v_w_out, v_ffn_norm_w, v_w_gate, v_w_up, v_ffn_conv_w, v_ffn_conv_b, v_w_down):
    names = [n for n, _, _ in _SHARDED] + [n for n, _ in _REPLICATED]
    given = dict(locals())
    two_d = lambda a: a.reshape(a.shape[-2:])
    view = lambda a, n: two_d(a).T if n in _TRANSPOSED else two_d(a)
    wl = {n: view(given[n], n) for n in names}
    ml = {n: view(given["m_" + n], n) for n in names}
    vl = {n: view(given["v_" + n], n) for n in names}
    out_shapes = {n: given[n].shape for n in names}

    spec = {n: (d, s) for n, d, s in _SHARDED}
    small_sizes = [math.prod(_local_shape(*spec[n])) for n in _F32_GATHERED]

    def small_block(d):
        cat = jnp.concatenate([d[n].reshape(d[n].shape[:-2] + (-1,)) for n in _F32_GATHERED], axis=-1)
        pad = [(0, 0)] * (cat.ndim - 1) + [(0, math.prod(_SMALL_BLOCK) - cat.shape[-1])]
        return jnp.pad(cat, pad).reshape(cat.shape[:-1] + _SMALL_BLOCK)

    def shard(n):
        return wl[n].astype(_MXU)

    def from_slots(n, blocks):
        d, s = spec[n]
        if d == 0 or n in _TRANSPOSED:
            return blocks.reshape(-1, blocks.shape[-1])
        return blocks.transpose(1, 0, 2).reshape(s)

    my_id = 4 * lax.axis_index("x") + 2 * lax.axis_index("y") + lax.axis_index("c")
    got = _all_gather_many("gather_early", [shard(n) for n in _EARLY] + [small_block(wl)])
    full = {n: a for n, a in wl.items() if n not in _LATE}
    for n, blocks in zip(_EARLY, got):
        full[n] = from_slots(n, blocks)
    for n, p in zip(_F32_GATHERED, _split(got[-1].reshape(N_DEV, -1), small_sizes)):
        full[n] = _from_blocks(p, *spec[n])
    late_own = [shard(n) for n in _LATE]
    l_send, l_recv, l_src, l_land, token = _push_start("gather_late_start", late_own, False, got[-1])

    def late_weights(after):
        _, lands = _push_wait("gather_late_wait", l_send, l_recv, l_src, l_land, False, after)
        out = {}
        for n, land, own in zip(_LATE, lands, late_own):
            out[n] = from_slots(n, lax.dynamic_update_slice(land, own[None], (my_id, 0, 0))).astype(_MXU)
        return out

    def dest_blocks(n, a):
        d, s = spec[n]
        r, c = _local_shape(d, s)
        if n in _TRANSPOSED:
            return a.reshape(N_DEV, c, r)
        return a.reshape(N_DEV, r, c) if d == 0 else a.reshape(r, N_DEV, c).transpose(1, 0, 2)

    pushed = []

    def grads_ready(g, names):
        nat = _grads_to_natural({n: g[n] for n in names})
        blocks = [dest_blocks(n, nat[n]).astype(_MXU) for n in names]
        sends, recvs, srcs, lands, tok = _push_start("rs_" + names[0] + "_start", blocks, True, token)
        pushed.append((names, sends, recvs, srcs, lands))
        return tok

    seq = x.shape[1]
    tp = ROW0 + seq
    h0 = jnp.concatenate([jnp.zeros((PAD, D_MODEL), F32), full["meta_tokens"], x[0]], axis=0)
    tgt = jnp.concatenate([jnp.zeros((ROW0, D_MODEL), F32), loss_target[0]], axis=0)
    loss, dh0, raw = _local_step(h0, tgt, _prepare(full, tp), token, late_weights, grads_ready)
    g = _grads_to_natural(raw)
    g["meta_tokens"] = dh0[PAD:ROW0]
    grad_x = dh0[ROW0:][None]

    big = [{}, {}, {}, {}]
    rep_names = [n for n, _ in _REPLICATED]
    raw_key = {"dn_A_log": "alog_b", "dn_dt_bias": "dtb_b"}
    pieces = [raw[raw_key.get(n, n)] for n in rep_names] + [loss]
    pieces += [g[n].reshape(1, -1) for n in _F32_GATHERED]
    widths = [p.shape[1] for p in pieces]
    offs = [sum(widths[:k]) for k in range(len(widths))]
    cat = jnp.concatenate(pieces, axis=1)
    cols = -(-cat.shape[1] // (8 * LANE)) * LANE
    mine = jnp.pad(cat, ((0, 0), (0, 8 * cols - cat.shape[1]))).reshape(8, cols)
    everyone = _all_gather("gather_small_grads", mine)
    total = _sum_parts("sum_small_grads", [(everyone, d) for d in range(N_DEV)]).reshape(1, 8 * cols)
    tot = {n: total[0, o:o + wd] for n, o, wd in zip(rep_names + ["loss"] + list(_F32_GATHERED), offs, widths)}
    lanes = lambda a: jnp.pad(a, ((0, 0), (0, -a.shape[1] % LANE)))
    items = [(o, -(-size // LANE) * LANE, n in raw_key) for (n, size), o in zip(_REPLICATED, offs)]
    sm = _adam_vectors("adam_replicated", total, items, [lanes(wl[n]) for n in rep_names],
                       [lanes(ml[n]) for n in rep_names], [lanes(vl[n]) for n in rep_names])
    sm = [{n: a[:, :size] for (n, size), a in zip(_REPLICATED, kind)} for kind in sm]
    mine_of = {}
    for n in _F32_GATHERED:
        d, s = spec[n]
        r, c = _local_shape(d, s)
        mine_of[n] = lax.dynamic_slice(tot[n].reshape(s), (0, my_id * c), (r, c))
    res = _adam("adam_small_sharded", [(small_block(mine_of)[None], 0)], small_block(wl), small_block(ml),
                small_block(vl))
    for kind, a in enumerate(res):
        big[kind].update(zip(_F32_GATHERED, _split(a.reshape(-1), small_sizes)))

    for names, sends, recvs, srcs, lands in pushed:
        srcs, lands = _push_wait("rs_" + names[0] + "_wait", sends, recvs, srcs, lands, True, dh0)
        for n, src, land in zip(names, srcs, lands):
            parts = [(src, my_id)] + [(land, my_id ^ f) for f in range(1, N_DEV)]
            for kind, a in enumerate(_adam("adam_" + n, parts, wl[n], ml[n], vl[n])):
                big[kind][n] = a

    outs = [tot["loss"][0], grad_x]
    for kind in range(4):
        for n in ("meta_tokens", "attn_norm_w", "w_in", "q_a_norm_w", "w_q_b", "kv_a_norm_w", "w_kv_b", "q_norm_w",
                  "k_norm_w", "mla_out_norm_w", "dn_conv_w", "dn_A_log", "dn_dt_bias", "dn_out_norm_w", "w_out",
                  "ffn_norm_w", "w_gate", "w_up", "ffn_conv_w", "ffn_conv_b", "w_down"):
            src = big[kind] if n in big[kind] else sm[kind]
            a = src[n].T if n in _TRANSPOSED else src[n]
            outs.append(a.reshape(out_shapes[n]))
    return tuple(outs)
```

```python
import functools
import math

import jax
import jax.numpy as jnp
from jax import lax
from jax.experimental import pallas as pl
from jax.experimental.pallas import tpu as pltpu

F32 = jnp.float32
BF16 = jnp.bfloat16
_MXU = jnp.bfloat16
_HI = lax.Precision.HIGHEST

D_MODEL = 1024
N_META = 16
PAD = 112
ROW0 = PAD + N_META
MLA_HEADS = 4
QK_NOPE = 128
QK_ROPE = 64
QK_HEAD = QK_NOPE + QK_ROPE
V_HEAD = 128
Q_LORA = 256
KV_LORA = 256
ROPE_THETA = 10000.0
DN_HEADS = 4
DN_DIM = 128
DN_WIDTH = DN_HEADS * DN_DIM
DN_CONV = 4
DN_CHUNK = 64
GDN_SUB_CHUNKS = 2
D_FF = 2816
FFN_CONV = 3
EPS = 1e-6
HP = 256
C_QKV = 0
C_Z = 1536
C_QL = 2048
C_KVL = 2304
C_KPE = 2560
C_AB = 2688
IN_P = 2816
IN_COLS = 2632

ADAM_LR = 0.001
ADAM_B1 = 0.9
ADAM_B2 = 0.999
ADAM_EPS = 1e-08
ADAM_WD = 0.01
ADAM_STEP = 10

N_DEV = 8
TM = 128
LANE = 128
VMEM_LIMIT = 56 * 1024 * 1024
NEG = -1e30


def _dot(a, b, dims, hp=False):
    if hp:
        return lax.dot_general(a.astype(F32), b.astype(F32), (dims, ((), ())),
                               precision=lax.Precision.HIGH if hp == "3x" else _HI, preferred_element_type=F32)
    return lax.dot_general(a.astype(_MXU), b.astype(_MXU), (dims, ((), ())),
                           preferred_element_type=F32)


def _nn(a, b, hp=False):
    return _dot(a, b, ((1,), (0,)), hp)


def _nt(a, b, hp=False):
    return _dot(a, b, ((1,), (1,)), hp)


def _tn(a, b, hp=False):
    return _dot(a, b, ((0,), (0,)), hp)


def _sigmoid(x):
    return 1.0 / (1.0 + jnp.exp(-x))


def _rms_fwd(x, w, n):
    r = lax.rsqrt(jnp.sum(x * x, axis=-1, keepdims=True) * (1.0 / n) + EPS)
    return x * r * w, r


def _rms_bwd(x, w, dy, n):
    r = lax.rsqrt(jnp.sum(x * x, axis=-1, keepdims=True) * (1.0 / n) + EPS)
    xh = x * r
    gy = dy * w
    dx = r * (gy - xh * (jnp.sum(gy * xh, axis=-1, keepdims=True) * (1.0 / n)))
    return dx, dy * xh


def _rowsum(x):
    return jnp.sum(x, axis=0, keepdims=True)


def _row_ids(i, tm):
    return i * tm + lax.broadcasted_iota(jnp.int32, (tm, 1), 0)


def _shift_down(ext, s, tm):
    if s == 0:
        return ext[8:8 + tm]
    return pltpu.roll(ext, s, 0)[8:8 + tm]


def _shift_up(ext, s, tm):
    if s == 0:
        return ext[0:tm]
    return pltpu.roll(ext, tm + 8 - s, 0)[0:tm]


def _conv_fwd(x, halo_prev, w, width):
    tm = x.shape[0]
    ext = jnp.concatenate([halo_prev, x], axis=0)
    y = None
    for j in range(width):
        t = w[j:j + 1, :] * _shift_down(ext, width - 1 - j, tm)
        y = t if y is None else y + t
    return y


def _conv_bwd_x(dy, halo_next, w, width):
    tm = dy.shape[0]
    ext = jnp.concatenate([dy, halo_next], axis=0)
    dx = None
    for j in range(width):
        t = w[j:j + 1, :] * _shift_up(ext, width - 1 - j, tm)
        dx = t if dx is None else dx + t
    return dx


def _conv_bwd_w(dy, x, halo_prev, width):
    tm = dy.shape[0]
    ext = jnp.concatenate([halo_prev, x], axis=0)
    rows = [_rowsum(dy * _shift_down(ext, width - 1 - j, tm)) for j in range(width)]
    rows += [jnp.zeros_like(rows[0])] * (8 - width)
    return jnp.concatenate(rows, axis=0)


def _softplus(x):
    e = jnp.exp(-jnp.abs(x))
    u = 1.0 + e
    l1p = jnp.where(u == 1.0, e, jnp.log(u) * e / jnp.where(u == 1.0, 1.0, u - 1.0))
    return jnp.maximum(x, 0.0) + l1p


def _swap_halves(x):
    lane = lax.broadcasted_iota(jnp.int32, x.shape, 1)
    return jnp.where(lane < 32, pltpu.roll(x, 96, 1), jnp.where(lane < 64, pltpu.roll(x, 32, 1), 0.0))


class _In:
    def __init__(self, arr, width=None, cb=0, kind="cur"):
        self.arr, self.kind = arr, kind
        self.width = arr.shape[1] if width is None else width
        self.cb = cb


def _tile_spec(t, tm, tp):
    r8 = tm // 8
    if t.kind == "cur":
        return pl.BlockSpec((tm, t.width), lambda i, cb=t.cb: (i, cb))
    if t.kind == "prev":
        return pl.BlockSpec((8, t.width), lambda i, cb=t.cb: (jnp.maximum(i * r8 - 1, 0), cb))
    return pl.BlockSpec((8, t.width), lambda i, cb=t.cb: (jnp.minimum((i + 1) * r8, tp // 8 - 1), cb))


def _rows(name, fn, tiled, full, outs, accs=(), tm=TM):
    tp = tiled[0].arr.shape[0]
    nt = tp // tm
    r8 = tm // 8
    n_in = len(tiled) + len(full)
    n_out = len(outs)

    def body(*refs):
        i = pl.program_id(0)
        vals = [r[...] for r in refs[:n_in]]
        o_t, o_a = fn(i, *vals)
        for r, v in zip(refs[n_in:n_in + n_out], o_t):
            r[...] = v.astype(r.dtype)
        for r, v in zip(refs[n_in + n_out:], o_a):
            @pl.when(i == 0)
            def _():
                r[...] = v

            @pl.when(i > 0)
            def _():
                r[...] += v

    in_specs = [_tile_spec(t, tm, tp) for t in tiled]
    in_specs += [pl.BlockSpec(a.shape, lambda i, nd=a.ndim: (0,) * nd) for a in full]
    out_specs = [pl.BlockSpec((tm, w), lambda i: (i, 0)) for w, _ in outs]
    out_specs += [pl.BlockSpec((r, w), lambda i: (0, 0)) for r, w in accs]
    out_shape = [jax.ShapeDtypeStruct((tp, w), dt) for w, dt in outs]
    out_shape += [jax.ShapeDtypeStruct((r, w), F32) for r, w in accs]
    res = pl.pallas_call(
        body, name=name, grid=(nt,), in_specs=in_specs, out_specs=out_specs, out_shape=out_shape,
        compiler_params=pltpu.CompilerParams(dimension_semantics=("arbitrary",), vmem_limit_bytes=VMEM_LIMIT),
    )(*[t.arr for t in tiled], *full)
    return res


def _pick(n, cap, mult):
    best = None
    for d in range(mult, min(n, cap) + 1, mult):
        if n % d == 0:
            best = d
    assert best is not None, (n, cap, mult)
    return best


_ANY_SPEC = pl.BlockSpec(memory_space=pl.ANY)


def _mm(name, a, b, mode, out_dtype=F32, resid=None, after=None):
    if mode == "tn":
        m, k = a.shape
        n = b.shape[1]
        tk = _pick(k, 512, 128)
        tn = _pick(n, 1408, 128)

        def body_tn(a_ref, b_ref, o_ref):
            o_ref[...] = _tn(a_ref[...], b_ref[...]).astype(o_ref.dtype)

        return pl.pallas_call(
            body_tn, name=name, grid=(n // tn, k // tk),
            in_specs=[pl.BlockSpec((m, tk), lambda j, p: (0, p)),
                      pl.BlockSpec((m, tn), lambda j, p: (0, j))],
            out_specs=pl.BlockSpec((tk, tn), lambda j, p: (p, j)),
            out_shape=jax.ShapeDtypeStruct((k, n), out_dtype),
            compiler_params=pltpu.CompilerParams(
                dimension_semantics=("parallel", "parallel"), vmem_limit_bytes=VMEM_LIMIT),
        )(a, b)

    m, k = a.shape
    n = b.shape[1] if mode == "nn" else b.shape[0]
    tn = _pick(n, 1408, 128)
    tm = _pick(m, 1152, 16)
    dotf = _nn if mode == "nn" else _nt

    def body(*refs):
        a_ref, b_ref, o_ref = refs[0], refs[1], refs[-1]
        acc = dotf(a_ref[...], b_ref[...])
        if resid is not None:
            acc = refs[2][...] + acc
        o_ref[...] = acc.astype(o_ref.dtype)

    b_spec = (pl.BlockSpec((k, tn), lambda j, i: (0, j)) if mode == "nn"
              else pl.BlockSpec((tn, k), lambda j, i: (j, 0)))
    in_specs = [pl.BlockSpec((tm, k), lambda j, i: (i, 0)), b_spec]
    args = [a, b]
    if resid is not None:
        in_specs.append(pl.BlockSpec((tm, tn), lambda j, i: (i, j)))
        args.append(resid)
    if after is not None:
        in_specs.append(_ANY_SPEC)
        args.append(after)
    return pl.pallas_call(
        body, name=name, grid=(n // tn, m // tm), in_specs=in_specs,
        out_specs=pl.BlockSpec((tm, tn), lambda j, i: (i, j)),
        out_shape=jax.ShapeDtypeStruct((m, n), out_dtype),
        compiler_params=pltpu.CompilerParams(
            dimension_semantics=("parallel", "parallel"), vmem_limit_bytes=VMEM_LIMIT),
    )(*args)


def _mm_tn2(name, a1, a2, b, out_dtype=F32):
    m, k = a1.shape
    n = b.shape[1]
    tk = _pick(k, 512, 128)

    def body(a1_ref, a2_ref, b_ref, o1_ref, o2_ref):
        bb = b_ref[...]
        o1_ref[...] = _tn(a1_ref[...], bb).astype(o1_ref.dtype)
        o2_ref[...] = _tn(a2_ref[...], bb).astype(o2_ref.dtype)

    a_spec = pl.BlockSpec((m, tk), lambda p: (0, p))
    o_spec = pl.BlockSpec((tk, n), lambda p: (p, 0))
    return pl.pallas_call(
        body, name=name, grid=(k // tk,),
        in_specs=[a_spec, a_spec, pl.BlockSpec((m, n), lambda p: (0, 0))],
        out_specs=[o_spec, o_spec], out_shape=[jax.ShapeDtypeStruct((k, n), out_dtype)] * 2,
        compiler_params=pltpu.CompilerParams(dimension_semantics=("parallel",), vmem_limit_bytes=VMEM_LIMIT),
    )(a1, a2, b)


def _norm_mm(name, x, norm_w, b, mode="nt", x_cb=0, after=None):
    m = x.shape[0]
    k = norm_w.shape[1]
    n = b.shape[0] if mode == "nt" else b.shape[1]
    tn = _pick(n, 1408, 128)
    tm = _pick(m, 1152, 16)
    dotf = _nt if mode == "nt" else _nn
    extra = [] if after is None else [after]

    def body(x_ref, w_ref, b_ref, *rest):
        o_ref, u_ref = rest[-2:]

        @pl.when(pl.program_id(1) == 0)
        def _():
            u_ref[...] = _rms_fwd(x_ref[...], w_ref[...], k)[0].astype(u_ref.dtype)

        o_ref[...] = dotf(u_ref[...], b_ref[...])

    b_spec = (pl.BlockSpec((tn, k), lambda i, j: (j, 0)) if mode == "nt"
              else pl.BlockSpec((k, tn), lambda i, j: (0, j)))
    return pl.pallas_call(
        body, name=name, grid=(m // tm, n // tn),
        in_specs=[pl.BlockSpec((tm, k), lambda i, j: (i, x_cb)), pl.BlockSpec((1, k), lambda i, j: (0, 0)),
                  b_spec] + [_ANY_SPEC] * len(extra),
        out_specs=[pl.BlockSpec((tm, tn), lambda i, j: (i, j)), pl.BlockSpec((tm, k), lambda i, j: (i, 0))],
        out_shape=[jax.ShapeDtypeStruct((m, n), F32), jax.ShapeDtypeStruct((m, k), _MXU)],
        compiler_params=pltpu.CompilerParams(
            dimension_semantics=("arbitrary", "arbitrary"), vmem_limit_bytes=VMEM_LIMIT),
    )(x, norm_w, b, *extra)


def _mm_rows(name, a, b, mode, fn, tiled, full, outs, accs=(), tm_cap=576):
    a_list = list(a) if isinstance(a, (list, tuple)) else [a]
    b_list = list(b) if isinstance(b, (list, tuple)) else [b]
    na = len(a_list)
    m = a_list[0].shape[0]
    tm = _pick(m, tm_cap, 16)
    dotf = _nn if mode == "nn" else _nt
    n_in = len(tiled) + len(full)
    n_out = len(outs)
    first = 2 * na

    def body(*refs):
        i = pl.program_id(0)
        vals = [r[...] for r in refs[first:first + n_in]]
        acc = dotf(refs[0][...], refs[na][...])
        for p in range(1, na):
            acc = acc + dotf(refs[p][...], refs[na + p][...])
        o_t, o_a = fn(i, acc, *vals)
        for r, v in zip(refs[first + n_in:first + n_in + n_out], o_t):
            r[...] = v.astype(r.dtype)
        for r, v in zip(refs[first + n_in + n_out:], o_a):
            @pl.when(i == 0)
            def _():
                r[...] = v

            @pl.when(i > 0)
            def _():
                r[...] += v

    whole = lambda x: pl.BlockSpec(x.shape, lambda i, nd=x.ndim: (0,) * nd)
    in_specs = [pl.BlockSpec((tm, x.shape[1]), lambda i: (i, 0)) for x in a_list] + [whole(x) for x in b_list]
    in_specs += [_tile_spec(t, tm, m) for t in tiled]
    in_specs += [whole(x) for x in full]
    out_specs = [pl.BlockSpec((tm, w), lambda i: (i, 0)) for w, _ in outs]
    out_specs += [pl.BlockSpec((r, w), lambda i: (0, 0)) for r, w in accs]
    out_shape = [jax.ShapeDtypeStruct((m, w), dt) for w, dt in outs]
    out_shape += [jax.ShapeDtypeStruct((r, w), F32) for r, w in accs]
    return pl.pallas_call(
        body, name=name, grid=(m // tm,), in_specs=in_specs, out_specs=out_specs, out_shape=out_shape,
        compiler_params=pltpu.CompilerParams(dimension_semantics=("arbitrary",), vmem_limit_bytes=VMEM_LIMIT),
    )(*a_list, *b_list, *[t.arr for t in tiled], *full)


ATTN_Q_TILES = 4


def _attn_probs(q, k, row0):
    tq, tp = q.shape[0], k.shape[0]
    s = _nt(q, k) * (1.0 / math.sqrt(QK_HEAD))
    row = row0 + lax.broadcasted_iota(jnp.int32, (tq, tp), 0)
    col = lax.broadcasted_iota(jnp.int32, (tq, tp), 1)
    ok = (col <= row) & (col >= PAD)
    s = jnp.where(ok, s, NEG)
    m = jnp.max(s, axis=-1, keepdims=True)
    e = jnp.exp(s - m)
    return e * (1.0 / jnp.sum(e, axis=-1, keepdims=True))


def _attn_fwd(q, k, v):
    tp = q.shape[0]
    tq = tp // ATTN_Q_TILES

    def body(q_ref, k_ref, v_ref, o_ref):
        for i in range(ATTN_Q_TILES):
            rows = slice(i * tq, (i + 1) * tq)
            keys = slice(0, (i + 1) * tq)
            p = _attn_probs(q_ref[rows, :], k_ref[keys, :], i * tq)
            o_ref[rows, :] = _nn(p, v_ref[keys, :])

    return pl.pallas_call(
        body, name="attn_fwd", grid=(MLA_HEADS,),
        in_specs=[pl.BlockSpec((tp, HP), lambda h: (0, h)),
                  pl.BlockSpec((tp, HP), lambda h: (0, h)),
                  pl.BlockSpec((tp, V_HEAD), lambda h: (0, h))],
        out_specs=pl.BlockSpec((tp, V_HEAD), lambda h: (0, h)),
        out_shape=jax.ShapeDtypeStruct((tp, MLA_HEADS * V_HEAD), F32),
        compiler_params=pltpu.CompilerParams(dimension_semantics=("parallel",), vmem_limit_bytes=VMEM_LIMIT),
    )(q, k, v)


def _attn_bwd(q, k, v, do):
    tp = q.shape[0]
    tq = tp // ATTN_Q_TILES

    def body(q_ref, k_ref, v_ref, do_ref, dq_ref, dk_ref, dv_ref):
        for i in reversed(range(ATTN_Q_TILES)):
            rows = slice(i * tq, (i + 1) * tq)
            keys = slice(0, (i + 1) * tq)
            qb = q_ref[rows, :]
            kk = k_ref[keys, :]
            dob = do_ref[rows, :]
            p = _attn_probs(qb, kk, i * tq)
            dp = _nt(dob, v_ref[keys, :])
            delta = jnp.sum(p * dp, axis=-1, keepdims=True)
            ds = p * (dp - delta) * (1.0 / math.sqrt(QK_HEAD))
            dq_ref[rows, :] = _nn(ds, kk)
            if i == ATTN_Q_TILES - 1:
                dk_ref[...] = _tn(ds, qb)
                dv_ref[...] = _tn(p, dob)
            else:
                dk_ref[keys, :] += _tn(ds, qb)
                dv_ref[keys, :] += _tn(p, dob)

    full = lambda w: pl.BlockSpec((tp, w), lambda h: (0, h))
    return pl.pallas_call(
        body, name="attn_bwd", grid=(MLA_HEADS,),
        in_specs=[full(HP), full(HP), full(V_HEAD), full(V_HEAD)],
        out_specs=[full(HP), full(HP), full(V_HEAD)],
        out_shape=[jax.ShapeDtypeStruct((tp, MLA_HEADS * HP), F32),
                   jax.ShapeDtypeStruct((tp, MLA_HEADS * HP), F32),
                   jax.ShapeDtypeStruct((tp, MLA_HEADS * V_HEAD), F32)],
        compiler_params=pltpu.CompilerParams(dimension_semantics=("parallel",), vmem_limit_bytes=VMEM_LIMIT),
    )(q, k, v, do)


def _gdn_consts():
    c = DN_CHUNK
    r = lax.broadcasted_iota(jnp.int32, (c, c), 0)
    cc = lax.broadcasted_iota(jnp.int32, (c, c), 1)
    incl = r >= cc
    strict = r > cc
    return incl, strict


def _cumsum_rows(x, reverse=False):
    c = x.shape[0]
    row = lax.broadcasted_iota(jnp.int32, x.shape, 0)
    s = 1
    while s < c:
        if reverse:
            x = x + jnp.where(row < c - s, pltpu.roll(x, c - s, 0), 0.0)
        else:
            x = x + jnp.where(row >= s, pltpu.roll(x, s, 0), 0.0)
        s *= 2
    return x


def _each(fn, *lists):
    return [fn(*a) for a in zip(*lists)]


def _interleave(chains):
    chains = list(chains)
    while chains:
        for ch in list(chains):
            try:
                next(ch)
            except StopIteration:
                chains.remove(ch)


def _gdn_chunk_common(q_ref, k_ref, v_ref, g_ref, b_ref):
    c = DN_CHUNK
    incl, strict = _gdn_consts()
    sls = [(slice(c * sub, c * (sub + 1)), slice(DN_DIM * h, DN_DIM * (h + 1)))
           for sub in range(GDN_SUB_CHUNKS) for h in range(DN_HEADS)]
    q = [q_ref[sl] * (1.0 / math.sqrt(DN_DIM)) for sl in sls]
    k = [k_ref[sl] for sl in sls]
    v = [v_ref[sl] for sl in sls]
    g = [g_ref[sl] for sl in sls]
    beta = [b_ref[sl] for sl in sls]
    gc = [_cumsum_rows(x) for x in g]
    grow = [x.T[:c, :] for x in gc]
    kb = _each(jnp.multiply, k, beta)
    kk = _each(_nt, kb, k)
    qk = _each(_nt, q, k)
    gam = [jnp.exp(x) for x in gc]
    g_last = [_rowsum(x) for x in g]
    dm = [jnp.exp(jnp.where(incl, x[:, :c] - y, NEG)) for x, y in zip(gc, grow)]
    vb = _each(jnp.multiply, v, beta)
    kbg = _each(jnp.multiply, kb, gam)
    ek = [jnp.exp(x - y) for x, y in zip(g_last, gc)]
    kd = _each(jnp.multiply, k, ek)
    return dict(q=q, k=k, v=v, beta=beta, gc=gc, gam=gam, g_last=g_last, dm=dm, kb=kb, vb=vb,
                kbg=kbg, kk=kk, ek=ek, kd=kd, qk=qk, incl=incl, strict=strict, sls=sls)


def _gdn_fwd(q, k, v, g, beta):
    tp = q.shape[0]
    c = DN_CHUNK
    nch = tp // c

    def body(q_ref, k_ref, v_ref, g_ref, b_ref, o_ref, s_ref, t_ref, s_scr):
        @pl.when(pl.program_id(0) == 0)
        def _():
            s_scr[...] = jnp.zeros_like(s_scr)

        eye = (lax.broadcasted_iota(jnp.int32, (c, c), 0) == lax.broadcasted_iota(jnp.int32, (c, c), 1)).astype(F32)
        x = _gdn_chunk_common(q_ref, k_ref, v_ref, g_ref, b_ref)
        heads = range(DN_HEADS)
        bp = [-jnp.where(x["strict"], kk * dm, 0.0) for kk, dm in zip(x["kk"], x["dm"])]
        t = [eye + b for b in bp]
        for _ in range(5):
            bp = [_nn(b, b, hp="3x") for b in bp]
            t = [tt + _nn(tt, b, hp="3x") for tt, b in zip(t, bp)]
        u = _each(_nn, t, x["vb"])
        w = _each(_nn, t, x["kbg"])
        qg = _each(jnp.multiply, x["q"], x["gam"])
        mqk = _each(jnp.multiply, x["qk"], x["dm"])
        s = [s_scr[h] for h in heads]
        for sub in range(GDN_SUB_CHUNKS):
            e = [DN_HEADS * sub + h for h in heads]
            v_new = [u[i] - _nn(w[i], s[h]) for h, i in zip(heads, e)]
            o = [_nn(qg[i], s[h]) + _nn(mqk[i], v_new[h]) for h, i in zip(heads, e)]
            s_new = [s[h] * jnp.exp(x["g_last"][i]) + _tn(x["kd"][i], v_new[h]) for h, i in zip(heads, e)]
            for h, i in zip(heads, e):
                s_ref[h, sub] = s[h]
                t_ref[h, sub] = t[i]
                o_ref[x["sls"][i]] = o[h]
            s = s_new
        for h in heads:
            s_scr[h] = s[h]

    sub = GDN_SUB_CHUNKS
    rb = lambda n: (n, 0)
    return pl.pallas_call(
        body, name="gdn_fwd", grid=(nch // sub,),
        in_specs=[pl.BlockSpec((sub * c, DN_WIDTH), rb)] * 5,
        out_specs=[pl.BlockSpec((sub * c, DN_WIDTH), rb),
                   pl.BlockSpec((DN_HEADS, sub, DN_DIM, DN_DIM), lambda n: (0, n, 0, 0)),
                   pl.BlockSpec((DN_HEADS, sub, c, c), lambda n: (0, n, 0, 0))],
        out_shape=[jax.ShapeDtypeStruct((tp, DN_WIDTH), F32),
                   jax.ShapeDtypeStruct((DN_HEADS, nch, DN_DIM, DN_DIM), F32),
                   jax.ShapeDtypeStruct((DN_HEADS, nch, c, c), F32)],
        scratch_shapes=[pltpu.VMEM((DN_HEADS, DN_DIM, DN_DIM), F32)],
        compiler_params=pltpu.CompilerParams(dimension_semantics=("arbitrary",), vmem_limit_bytes=VMEM_LIMIT),
    )(q, k, v, g, beta)


def _gdn_bwd(q, k, v, g, beta, s_all, t_all, do):
    tp = q.shape[0]
    c = DN_CHUNK
    nch = tp // c

    def body(q_ref, k_ref, v_ref, g_ref, b_ref, s_ref, t_ref, do_ref,
             dq_ref, dk_ref, dv_ref, dg_ref, db_ref, ds_scr):
        @pl.when(pl.program_id(0) == 0)
        def _():
            ds_scr[...] = jnp.zeros_like(ds_scr)

        xs = _gdn_chunk_common(q_ref, k_ref, v_ref, g_ref, b_ref)

        ds_state = [ds_scr[h] for h in range(DN_HEADS)]

        def chain(sub, h):
            e = DN_HEADS * sub + h
            x = {key: (val[e] if isinstance(val, list) else val) for key, val in xs.items()}
            sl = x["sls"]
            qs, kx, vx, beta_, gam, dm = x["q"], x["k"], x["v"], x["beta"], x["gam"], x["dm"]
            kb, vb, kbg, kd, ek = x["kb"], x["vb"], x["kbg"], x["kd"], x["ek"]
            t = t_ref[h, sub]
            s = s_ref[h, sub]
            dsn = ds_state[h]
            dob = do_ref[sl]
            eg_last = jnp.exp(x["g_last"])
            u = _nn(t, vb)
            w = _nn(t, kbg)
            mqk = x["qk"] * dm
            qd = qs * gam
            dqd = _nt(dob, s)
            dkd_pre = _nn(kd, dsn)
            yield
            v_new = u - _nn(w, s)
            dv_new = _tn(mqk, dob) + dkd_pre
            dq = dqd * gam
            dgam = jnp.sum(dqd * qs, axis=1, keepdims=True)
            yield
            ds_state[h] = _tn(qd, dob) + eg_last * dsn - _tn(w, dv_new)
            dmm = jnp.where(x["incl"], _nt(dob, v_new), 0.0)
            dkd = _nt(v_new, dsn)
            dw = -_nt(dv_new, s)
            dvb = _tn(t, dv_new)
            dt = _nt(dv_new, vb)
            yield
            dqk = dmm * dm
            e_mat = dmm * mqk
            dq = dq + _nn(dqk, kx)
            dk = _tn(dqk, qs) + dkd * ek
            e1 = jnp.sum(dkd * kd, axis=1, keepdims=True)
            dgc = -e1
            dg_last = jnp.sum(e1) + eg_last * jnp.sum(s * dsn)
            dt = dt + _nt(dw, kbg)
            dkbg = _tn(t, dw)
            yield
            tdt = _tn(t, dt, hp="3x")
            yield
            da = jnp.where(x["strict"], -_nt(tdt, t, hp="3x"), 0.0)
            yield
            dkk = da * dm
            e_mat = e_mat + da * x["kk"] * dm
            dkb = _nn(dkk, kx) + dkbg * gam
            dk = dk + _tn(dkk, kb)
            dgam = dgam + jnp.sum(dkbg * kb, axis=1, keepdims=True)
            yield
            dk = dk + dkb * beta_
            dbeta = jnp.sum(dkb * kx, axis=1, keepdims=True) + jnp.sum(dvb * vx, axis=1, keepdims=True)
            dv = dvb * beta_
            dgc = dgc + jnp.sum(e_mat, axis=1, keepdims=True) + dgam * gam
            dgc = dgc - jnp.sum(e_mat.T, axis=1, keepdims=True)
            yield
            dg = _cumsum_rows(dgc, reverse=True) + dg_last
            yield
            dq_ref[sl] = dq * (1.0 / math.sqrt(DN_DIM))
            dk_ref[sl] = dk
            dv_ref[sl] = dv
            dg_ref[sl] = dg
            db_ref[sl] = jnp.broadcast_to(dbeta, (c, LANE))

        chains = []
        for sub in reversed(range(GDN_SUB_CHUNKS)):
            new = [chain(sub, h) for h in range(DN_HEADS)]
            for _ in range(3):
                for ch in new:
                    next(ch)
            chains += new
        _interleave(chains)
        for h in range(DN_HEADS):
            ds_scr[h] = ds_state[h]

    nblk = nch // GDN_SUB_CHUNKS
    sub = GDN_SUB_CHUNKS
    rb = lambda n: (nblk - 1 - n, 0)
    hs = lambda n: (0, nblk - 1 - n, 0, 0)
    return pl.pallas_call(
        body, name="gdn_bwd", grid=(nblk,),
        in_specs=[pl.BlockSpec((sub * c, DN_WIDTH), rb)] * 5
        + [pl.BlockSpec((DN_HEADS, sub, DN_DIM, DN_DIM), hs), pl.BlockSpec((DN_HEADS, sub, c, c), hs),
           pl.BlockSpec((sub * c, DN_WIDTH), rb)],
        out_specs=[pl.BlockSpec((sub * c, DN_WIDTH), rb)] * 5,
        out_shape=[jax.ShapeDtypeStruct((tp, DN_WIDTH), F32)] * 5,
        scratch_shapes=[pltpu.VMEM((DN_HEADS, DN_DIM, DN_DIM), F32)],
        compiler_params=pltpu.CompilerParams(dimension_semantics=("arbitrary",), vmem_limit_bytes=VMEM_LIMIT),
    )(q, k, v, g, beta, s_all, t_all, do)


def _silu_parts(x):
    s = _sigmoid(x)
    return x * s, s * (1.0 + x * (1.0 - s))


def _f_rms_cast(i, x, w):
    y, _ = _rms_fwd(x, w, x.shape[1])
    return (y,), ()


def _f_rms_bwd_add(i, x, dy, dres, w, *, mask_pad):
    dx, dwr = _rms_bwd(x, w, dy, x.shape[1])
    out = dres + dx
    if mask_pad:
        out = jnp.where(_row_ids(i, x.shape[0]) >= PAD, out, 0.0)
    return (out,), (_rowsum(dwr),)


def _f_lat_norm(i, ql, kvl, qw, kvw):
    return (_rms_fwd(ql, qw, Q_LORA)[0], _rms_fwd(kvl, kvw, KV_LORA)[0]), ()


def _f_lat_norm_bwd(i, ql, kvl, dqn, dkvn, qw, kvw):
    dq, dqw = _rms_bwd(ql, qw, dqn, Q_LORA)
    dk, dkw = _rms_bwd(kvl, kvw, dkvn, KV_LORA)
    return (dq, dk), (_rowsum(dqw), _rowsum(dkw))


def _rope(x, cos, sin_s):
    return x * cos + _swap_halves(x) * sin_s


def _rope_t(dy, cos, sin_s):
    return dy * cos + _swap_halves(dy * sin_s)


def _f_mla_qk(i, qf, kvf, kpe, cos, sin_s, qw, kw):
    qs, ks, vs = [], [], []
    for h in range(MLA_HEADS):
        qn, _ = _rms_fwd(qf[:, HP * h:HP * (h + 1)], qw, QK_HEAD)
        qs += [qn[:, :QK_NOPE], _rope(qn[:, QK_NOPE:], cos, sin_s)]
        kh = jnp.concatenate([kvf[:, HP * h:HP * h + QK_NOPE], kpe], axis=1)
        kn, _ = _rms_fwd(kh, kw, QK_HEAD)
        ks += [kn[:, :QK_NOPE], _rope(kn[:, QK_NOPE:], cos, sin_s)]
        vs.append(kvf[:, HP * h + QK_NOPE:HP * (h + 1)])
    return (jnp.concatenate(qs, axis=1), jnp.concatenate(ks, axis=1), jnp.concatenate(vs, axis=1)), ()


def _f_mla_qk_bwd(i, qf, kvf, kpe, cos, sin_s, dq, dk, dv, qw, kw):
    dqf, dkvf = [], []
    dkpe = None
    dqw = None
    dkw = None
    for h in range(MLA_HEADS):
        dqh = dq[:, HP * h:HP * (h + 1)]
        dqn = jnp.concatenate([dqh[:, :QK_NOPE], _rope_t(dqh[:, QK_NOPE:], cos, sin_s)], axis=1)
        dx, dwr = _rms_bwd(qf[:, HP * h:HP * (h + 1)], qw, dqn, QK_HEAD)
        dqf.append(dx)
        dqw = _rowsum(dwr) if dqw is None else dqw + _rowsum(dwr)
        dkh = dk[:, HP * h:HP * (h + 1)]
        dkn = jnp.concatenate([dkh[:, :QK_NOPE], _rope_t(dkh[:, QK_NOPE:], cos, sin_s)], axis=1)
        kh = jnp.concatenate([kvf[:, HP * h:HP * h + QK_NOPE], kpe], axis=1)
        dx, dwr = _rms_bwd(kh, kw, dkn, QK_HEAD)
        dkvf += [dx[:, :QK_NOPE], dv[:, V_HEAD * h:V_HEAD * (h + 1)]]
        dkpe = dx[:, QK_NOPE:] if dkpe is None else dkpe + dx[:, QK_NOPE:]
        dkw = _rowsum(dwr) if dkw is None else dkw + _rowsum(dwr)
    return (jnp.concatenate(dqf, axis=1), jnp.concatenate(dkvf, axis=1), dkpe), (dqw, dkw)


def _gdn_act(i, x, halo, w8):
    tm = x.shape[0]
    halo = jnp.where(i > 0, halo, 0.0)
    c = _conv_fwd(x, halo, w8, DN_CONV)
    act, dact = _silu_parts(c)
    return act, dact


def _spread_heads(ab):
    tm = ab.shape[0]
    return jnp.concatenate([jnp.broadcast_to(ab[:, h:h + 1], (tm, DN_DIM)) for h in range(2 * DN_HEADS)], axis=1)


def _gather_heads(x):
    tm = x.shape[0]
    lane = lax.broadcasted_iota(jnp.int32, (tm, LANE), 1)
    out = jnp.zeros((tm, LANE), F32)
    for h in range(2 * DN_HEADS):
        out = out + jnp.where(lane == h, x[:, DN_DIM * h:DN_DIM * h + 1], 0.0)
    return out


def _f_gdn_prep(i, x, halo, ab, w8, alog, dtb):
    tm = x.shape[0]
    act, _ = _gdn_act(i, x, halo, w8)
    outs = []
    for part in range(2):
        for h in range(DN_HEADS):
            t = act[:, DN_WIDTH * part + DN_DIM * h:DN_WIDTH * part + DN_DIM * (h + 1)]
            outs.append(t * lax.rsqrt(jnp.sum(t * t, axis=-1, keepdims=True) + EPS))
    q = jnp.concatenate(outs[:DN_HEADS], axis=1)
    k = jnp.concatenate(outs[DN_HEADS:], axis=1)
    v = act[:, 2 * DN_WIDTH:]
    abb = _spread_heads(ab)
    valid = _row_ids(i, tm) >= PAD
    g = jnp.where(valid, -jnp.exp(alog) * _softplus(abb[:, :DN_WIDTH] + dtb), 0.0)
    beta = jnp.where(valid, _sigmoid(abb[:, DN_WIDTH:]), 0.0)
    return (q, k, v, g, beta), ()


def _f_gdn_prep_bwd(i, x, x_prev, x_next, ab, dq, dq_next, dk, dk_next, dv, dv_next, dg, dbeta,
                    w8, alog, dtb, *, nt):
    tm = x.shape[0]
    x_prev = jnp.where(i > 0, x_prev, 0.0)
    more = i < nt - 1
    ext = lambda t, t_next: jnp.concatenate([t, jnp.where(more, t_next, 0.0)], axis=0)
    c = _conv_fwd(jnp.concatenate([x, x_next], axis=0), x_prev, w8, DN_CONV)
    act, dact = _silu_parts(c)
    douts = []
    for part, dd in enumerate((ext(dq, dq_next), ext(dk, dk_next))):
        for h in range(DN_HEADS):
            t = act[:, DN_WIDTH * part + DN_DIM * h:DN_WIDTH * part + DN_DIM * (h + 1)]
            r = lax.rsqrt(jnp.sum(t * t, axis=-1, keepdims=True) + EPS)
            y = t * r
            dy = dd[:, DN_DIM * h:DN_DIM * (h + 1)]
            douts.append(r * (dy - y * jnp.sum(dy * y, axis=-1, keepdims=True)))
    douts.append(ext(dv, dv_next))
    dc = jnp.concatenate(douts, axis=1) * dact
    dqkv = _conv_bwd_x(dc[:tm], dc[tm:], w8, DN_CONV)
    dconv_w = _conv_bwd_w(dc[:tm], x, x_prev, DN_CONV)
    abb = _spread_heads(ab)
    valid = _row_ids(i, tm) >= PAD
    pre = abb[:, :DN_WIDTH] + dtb
    ea = jnp.exp(alog)
    g = -ea * _softplus(pre)
    dg = jnp.where(valid, dg, 0.0)
    dbeta = jnp.where(valid, dbeta, 0.0)
    da = dg * (-ea) * _sigmoid(pre)
    beta = _sigmoid(abb[:, DN_WIDTH:])
    db = dbeta * beta * (1.0 - beta)
    dab = _gather_heads(jnp.concatenate([da, db], axis=1))
    return (dqkv, dab), (dconv_w, _rowsum(dg * g), _rowsum(da))


def _f_conv_bwd(i, dy, dy_next, x, x_prev, w8, *, width, nt):
    dy_next = jnp.where(i < nt - 1, dy_next, 0.0)
    x_prev = jnp.where(i > 0, x_prev, 0.0)
    return (_conv_bwd_x(dy, dy_next, w8, width),), (_conv_bwd_w(dy, x, x_prev, width),)


def _f_mix(i, o_mla, o_dn, z, w_mla, w_dn):
    tm = o_mla.shape[0]
    valid = _row_ids(i, tm) >= PAD
    outs = []
    for h in range(MLA_HEADS):
        y, _ = _rms_fwd(o_mla[:, V_HEAD * h:V_HEAD * (h + 1)], w_mla, V_HEAD)
        outs.append(jnp.where(valid, y, 0.0))
    for h in range(DN_HEADS):
        y, _ = _rms_fwd(o_dn[:, DN_DIM * h:DN_DIM * (h + 1)], w_dn, DN_DIM)
        outs.append(y * _silu_parts(z[:, DN_DIM * h:DN_DIM * (h + 1)])[0])
    return (jnp.concatenate(outs, axis=1),), ()


def _f_mix_bwd(i, o_mla, o_dn, z, dy_mla, dy_dn, w_mla, w_dn):
    tm = o_mla.shape[0]
    valid = _row_ids(i, tm) >= PAD
    d_mla, d_dn, d_z = [], [], []
    dw_mla = None
    dw_dn = None
    for h in range(MLA_HEADS):
        sl = slice(V_HEAD * h, V_HEAD * (h + 1))
        dx, dwr = _rms_bwd(o_mla[:, sl], w_mla, jnp.where(valid, dy_mla[:, sl], 0.0), V_HEAD)
        d_mla.append(dx)
        dw_mla = _rowsum(dwr) if dw_mla is None else dw_mla + _rowsum(dwr)
    for h in range(DN_HEADS):
        sl = slice(DN_DIM * h, DN_DIM * (h + 1))
        y, _ = _rms_fwd(o_dn[:, sl], w_dn, DN_DIM)
        sz, dsz = _silu_parts(z[:, sl])
        d_z.append(dy_dn[:, sl] * y * dsz)
        dx, dwr = _rms_bwd(o_dn[:, sl], w_dn, dy_dn[:, sl] * sz, DN_DIM)
        d_dn.append(dx)
        dw_dn = _rowsum(dwr) if dw_dn is None else dw_dn + _rowsum(dwr)
    return ((jnp.concatenate(d_mla, axis=1), jnp.concatenate(d_dn, axis=1), jnp.concatenate(d_z, axis=1)),
            (dw_mla, dw_dn))


def _f_ffn_act(i, gate_pre, halo, up, w8, b):
    halo = jnp.where(i > 0, halo, 0.0)
    gate = _conv_fwd(gate_pre, halo, w8, FFN_CONV) + b
    return (_silu_parts(gate)[0] * up,), ()


def _f_ffn_act_bwd(i, gp, gp_prev, gp_next, up, up_next, dact, dact_next, w8, b, *, nt):
    tm = gp.shape[0]
    gp_prev = jnp.where(i > 0, gp_prev, 0.0)
    dact_next = jnp.where(i < nt - 1, dact_next, 0.0)
    cat = lambda t, t_next: jnp.concatenate([t, t_next], axis=0)
    gate = _conv_fwd(cat(gp, gp_next), gp_prev, w8, FFN_CONV) + b
    sg, dsg = _silu_parts(gate)
    dact_e = cat(dact, dact_next)
    dgate = dact_e * cat(up, up_next) * dsg
    dgate_pre = _conv_bwd_x(dgate[:tm], dgate[tm:], w8, FFN_CONV)
    dup = dact * sg[:tm]
    return (dgate_pre, dup), (_conv_bwd_w(dgate[:tm], gp, gp_prev, FFN_CONV), _rowsum(dgate[:tm]))


def _f_loss(i, h3, tgt):
    tm = h3.shape[0]
    diff = jnp.where(_row_ids(i, tm) >= ROW0, h3 - tgt, 0.0)
    part = 0.5 * jnp.sum(diff * diff) * (1.0 / D_MODEL)
    return (diff * (1.0 / D_MODEL),), (jnp.full((1, LANE), part, F32),)


def _after(fn):
    return lambda i, *a: fn(i, *a[:-1])


def _local_step(h0, tgt, w, token, late_weights, grads_ready):
    tp = h0.shape[0]
    nt = tp // TM
    bf = (D_MODEL, _MXU)
    proj, u = _norm_mm("in_proj", h0, w["attn_norm_w"], w["w_in"], after=token)
    p_qkv = lambda kind="cur": _In(proj, 3 * DN_WIDTH, 0, kind)
    p_z = _In(proj, DN_WIDTH, C_Z // DN_WIDTH)
    p_ql = _In(proj, Q_LORA, C_QL // Q_LORA)
    p_kvl = _In(proj, KV_LORA, C_KVL // KV_LORA)
    p_kpe = _In(proj, LANE, C_KPE // LANE)
    p_ab = _In(proj, LANE, C_AB // LANE)
    cos, sin_s = _In(w["cos"]), _In(w["sin_s"])

    qf, qn = _norm_mm("mla_q_b", proj, w["q_a_norm_w"], w["w_q_b"], "nt", C_QL // Q_LORA)
    kvf, kvn = _norm_mm("mla_kv_b", proj, w["kv_a_norm_w"], w["w_kv_b"], "nn", C_KVL // KV_LORA)
    qk_w = [w["q_norm_w"], w["k_norm_w"]]
    q, k, v = _rows("mla_qk", _f_mla_qk, [_In(qf), _In(kvf), p_kpe, cos, sin_s], qk_w,
                    [(MLA_HEADS * HP, _MXU), (MLA_HEADS * HP, _MXU), (MLA_HEADS * V_HEAD, _MXU)])
    o_mla = _attn_fwd(q, k, v)

    dn_w = [w["dn_conv_w"], w["alog_b"], w["dtb_b"]]
    gq, gk, gv, gg, gb = _rows("gdn_prep", _f_gdn_prep, [p_qkv(), p_qkv("prev"), p_ab], dn_w,
                               [(DN_WIDTH, F32)] * 5)
    o_dn, s_all, t_all = _gdn_fwd(gq, gk, gv, gg, gb)

    out_w = [w["mla_out_norm_w"], w["dn_out_norm_w"]]
    mixed, = _rows("mix", _f_mix, [_In(o_mla), _In(o_dn), p_z], out_w, [bf])
    w = dict(w, **late_weights(mixed))
    h2 = _mm("out_proj", mixed, w["w_out"], "nn", resid=h0)

    gate_pre, hn = _norm_mm("ffn_gate", h2, w["ffn_norm_w"], w["w_gate"])
    ffn_w = [w["ffn_conv_w"], w["ffn_conv_b"]]
    act, up = _mm_rows(
        "ffn_up_act", hn, w["w_up"], "nt",
        lambda i, up_t, gp, gp_prev, w8, b: ((_f_ffn_act(i, gp, gp_prev, up_t, w8, b)[0][0], up_t), ()),
        [_In(gate_pre), _In(gate_pre, kind="prev")], ffn_w, [(D_FF, _MXU), (D_FF, F32)], tm_cap=288)
    dh3, loss = _mm_rows("ffn_down_loss", act, w["w_down"], "nn", lambda i, y, r, t: _f_loss(i, r + y, t),
                         [_In(h2), _In(tgt)], [], [(D_MODEL, F32)], [(1, LANE)])

    g = {}
    dact = _mm("ffn_down_dx", dh3, w["w_down"], "nt")
    g["w_down"] = _mm("ffn_down_dw", act, dh3, "tn", out_dtype=_MXU)
    dgate_pre, dup, g["ffn_conv_w"], g["ffn_conv_b"] = _rows(
        "ffn_act_bwd", functools.partial(_f_ffn_act_bwd, nt=nt),
        [_In(gate_pre), _In(gate_pre, kind="prev"), _In(gate_pre, kind="next"), _In(up), _In(up, kind="next"),
         _In(dact), _In(dact, kind="next")], ffn_w,
        [(D_FF, _MXU), (D_FF, _MXU)], [(8, D_FF), (1, D_FF)])
    g["w_gate"], g["w_up"] = _mm_tn2("ffn_gate_up_dw", dgate_pre, dup, hn, out_dtype=_MXU)
    tok = grads_ready(g, ("w_down", "w_gate", "w_up"))
    dh2, g["ffn_norm_w"] = _mm_rows(
        "ffn_gate_up_dx_rms", [dgate_pre, dup], [w["w_gate"], w["w_up"]], "nn",
        lambda i, dy, x, dres, nw, _tok: _f_rms_bwd_add(i, x, dy, dres, nw, mask_pad=True),
        [_In(h2), _In(dh3)], [w["ffn_norm_w"], tok], [(D_MODEL, F32)], [(1, D_MODEL)], tm_cap=288)

    g["w_out"] = _mm("out_proj_dw", mixed, dh2, "tn", out_dtype=_MXU)
    half = MLA_HEADS * V_HEAD
    do_mla, do_dn, dz, g["mla_out_norm_w"], g["dn_out_norm_w"] = _mm_rows(
        "out_proj_dx_mix", dh2, w["w_out"], "nt",
        lambda i, dm, om, od, z, wm, wd: _f_mix_bwd(i, om, od, z, dm[:, :half], dm[:, half:], wm, wd),
        [_In(o_mla), _In(o_dn), p_z], out_w,
        [(half, F32), (DN_WIDTH, F32), (DN_WIDTH, _MXU)], [(1, V_HEAD), (1, DN_DIM)])

    dq, dk, dv = _attn_bwd(q, k, v, do_mla)
    dqf, dkvf, dkpe, g["q_norm_w"], g["k_norm_w"] = _rows(
        "mla_qk_bwd", _f_mla_qk_bwd, [_In(qf), _In(kvf), p_kpe, cos, sin_s, _In(dq), _In(dk), _In(dv)], qk_w,
        [(MLA_HEADS * HP, _MXU), (MLA_HEADS * HP, _MXU), (LANE, _MXU)], [(1, HP), (1, HP)])
    g["w_q_b"] = _mm("mla_q_b_dw", dqf, qn, "tn")
    g["w_kv_b"] = _mm("mla_kv_b_dw", kvn, dkvf, "tn")
    tok = grads_ready(g, ("w_out", "w_q_b", "w_kv_b"))

    def lat_bwd(n):
        def fn(i, dy, x, nw, _tok):
            dx, dwr = _rms_bwd(x, nw, dy, n)
            return (dx,), (_rowsum(dwr),)
        return fn

    dql, g["q_a_norm_w"] = _mm_rows("mla_q_b_dx", dqf, w["w_q_b"], "nn", lat_bwd(Q_LORA), [p_ql],
                                    [w["q_a_norm_w"], tok], [(Q_LORA, _MXU)], [(1, Q_LORA)])
    dkvl, g["kv_a_norm_w"] = _mm_rows("mla_kv_b_dx", dkvf, w["w_kv_b"], "nt", lat_bwd(KV_LORA), [p_kvl],
                                      [w["kv_a_norm_w"], tok], [(KV_LORA, _MXU)], [(1, KV_LORA)])

    dgq, dgk, dgv, dgg, dgb = _gdn_bwd(gq, gk, gv, gg, gb, s_all, t_all, do_dn)
    nxt = lambda a: _In(a, kind="next")
    dqkv, dab, g["dn_conv_w"], g["alog_b"], g["dtb_b"] = _rows(
        "gdn_prep_bwd", functools.partial(_f_gdn_prep_bwd, nt=nt),
        [p_qkv(), p_qkv("prev"), p_qkv("next"), p_ab, _In(dgq), nxt(dgq), _In(dgk), nxt(dgk), _In(dgv), nxt(dgv),
         _In(dgg), _In(dgb)], dn_w,
        [(3 * DN_WIDTH, _MXU), (LANE, _MXU)], [(8, 3 * DN_WIDTH), (1, DN_WIDTH), (1, DN_WIDTH)])

    dproj = jnp.concatenate([dqkv, dz, dql, dkvl, dkpe, dab], axis=1)
    g["w_in"] = _mm("in_proj_dw", dproj, u, "tn", out_dtype=_MXU)
    tok = grads_ready(g, ("w_in",))
    dh0, g["attn_norm_w"] = _mm_rows(
        "in_proj_dx_rms", dproj, w["w_in"], "nn",
        lambda i, du, x, dres, nw, _tok: _f_rms_bwd_add(i, x, du, dres, nw, mask_pad=False),
        [_In(h0), _In(dh2)], [w["attn_norm_w"], tok], [(D_MODEL, F32)], [(1, D_MODEL)])
    return loss, dh0, g


def _w_in_to_padded(w):
    c1, c2, c3 = Q_LORA, Q_LORA + KV_LORA, Q_LORA + KV_LORA + QK_ROPE
    c4 = c3 + 3 * DN_WIDTH
    c5 = c4 + DN_WIDTH
    z = lambda n: jnp.zeros((n, w.shape[1]), w.dtype)
    return jnp.concatenate([w[c3:c4], w[c4:c5], w[:c1], w[c1:c2], w[c2:c3], z(LANE - QK_ROPE),
                            w[c5:], z(LANE - 2 * DN_HEADS)], axis=0)


def _w_in_from_padded(g):
    return jnp.concatenate([g[C_QL:C_QL + Q_LORA], g[C_KVL:C_KVL + KV_LORA], g[C_KPE:C_KPE + QK_ROPE],
                            g[:C_Z + DN_WIDTH], g[C_AB:C_AB + 2 * DN_HEADS]], axis=0)


def _w_q_b_to_padded(w):
    r = w.shape[1]
    w = w.reshape(MLA_HEADS, QK_HEAD, r)
    return jnp.pad(w, ((0, 0), (0, HP - QK_HEAD), (0, 0))).reshape(MLA_HEADS * HP, r)


def _w_q_b_from_padded(g):
    r = g.shape[1]
    return g.reshape(MLA_HEADS, HP, r)[:, :QK_HEAD].reshape(MLA_HEADS * QK_HEAD, r)


def _pad_rows8(w):
    return jnp.pad(w, ((0, 8 - w.shape[0]), (0, 0)))


def _prepare(full, tp):
    w = {}
    mx = lambda a: a.astype(_MXU)
    w["attn_norm_w"] = full["attn_norm_w"]
    w["w_in"] = mx(_w_in_to_padded(full["w_in"]))
    w["q_a_norm_w"] = full["q_a_norm_w"]
    w["kv_a_norm_w"] = full["kv_a_norm_w"]
    w["w_q_b"] = mx(_w_q_b_to_padded(full["w_q_b"]))
    w["w_kv_b"] = mx(full["w_kv_b"])
    w["q_norm_w"] = jnp.pad(full["q_norm_w"], ((0, 0), (0, HP - QK_HEAD)))
    w["k_norm_w"] = jnp.pad(full["k_norm_w"], ((0, 0), (0, HP - QK_HEAD)))
    w["mla_out_norm_w"] = full["mla_out_norm_w"]
    w["dn_out_norm_w"] = full["dn_out_norm_w"]
    w["dn_conv_w"] = _pad_rows8(full["dn_conv_w"])
    w["alog_b"] = jnp.repeat(full["dn_A_log"], DN_DIM, axis=1)
    w["dtb_b"] = jnp.repeat(full["dn_dt_bias"], DN_DIM, axis=1)
    w["ffn_norm_w"] = full["ffn_norm_w"]
    w["ffn_conv_w"] = _pad_rows8(full["ffn_conv_w"])
    w["ffn_conv_b"] = full["ffn_conv_b"]
    for n in _LATE:
        if n in full:
            w[n] = mx(full[n])
    half = QK_ROPE // 2
    inv = ROPE_THETA ** (-jnp.arange(half, dtype=F32) / half)
    ang = (jnp.arange(tp, dtype=jnp.int32) - PAD).astype(F32)[:, None] * inv[None, :]
    zc = jnp.zeros((tp, LANE - QK_ROPE), F32)
    w["cos"] = jnp.concatenate([jnp.cos(ang), jnp.cos(ang), zc], axis=1)
    w["sin_s"] = jnp.concatenate([-jnp.sin(ang), jnp.sin(ang), zc], axis=1)
    return w


def _grads_to_natural(g):
    convert = {
        "w_in": ("w_in", _w_in_from_padded),
        "w_q_b": ("w_q_b", _w_q_b_from_padded),
        "q_norm_w": ("q_norm_w", lambda a: a[:, :QK_HEAD]),
        "k_norm_w": ("k_norm_w", lambda a: a[:, :QK_HEAD]),
        "dn_conv_w": ("dn_conv_w", lambda a: a[:DN_CONV]),
        "ffn_conv_w": ("ffn_conv_w", lambda a: a[:FFN_CONV]),
        "alog_b": ("dn_A_log", lambda a: a[:, ::DN_DIM]),
        "dtb_b": ("dn_dt_bias", lambda a: a[:, ::DN_DIM]),
    }
    n = {}
    for key, a in g.items():
        name, fn = convert.get(key, (key, lambda t: t))
        n[name] = fn(a)
    return n


_MESH = pl.DeviceIdType.MESH
_ANY = pl.BlockSpec(memory_space=pl.ANY)
_CHIP_FLIPS = ((1, 0), (0, 1), (1, 1))


def _me():
    return lax.axis_index("x"), lax.axis_index("y"), lax.axis_index("c")


def _all_gather(name, blk):
    def body(x_ref, out_ref, send_sems, recv_sems, local_sem):
        x, y, c = _me()
        me, sib = (x, y, c), (x, y, 1 - c)
        chips = [(x ^ fx, y ^ fy) for fx, fy in _CHIP_FLIPS]

        def slot(p):
            return out_ref.at[4 * p[0] + 2 * p[1] + p[2]]

        def copy(k, block, to, src=None):
            return pltpu.make_async_remote_copy(
                src_ref=slot(block) if src is None else src, dst_ref=slot(block),
                send_sem=send_sems.at[k], recv_sem=recv_sems.at[k], device_id=to, device_id_type=_MESH)

        mine = pltpu.make_async_copy(x_ref, slot(me), local_sem)
        mine.start()
        first = [copy(0, me, sib, src=x_ref)]
        first += [copy(1 + j, me, (*chip, c), src=x_ref) for j, chip in enumerate(chips)]
        for cp in first:
            cp.start()
        passed = [copy(4 + j, (*chip, c), sib) for j, chip in enumerate(chips)]
        for j, chip in enumerate(chips):
            copy(1 + j, (*chip, c), me).wait_recv()
            passed[j].start()
        copy(0, sib, me).wait_recv()
        for j, chip in enumerate(chips):
            copy(4 + j, (*chip, 1 - c), me).wait_recv()
        for cp in first + passed:
            cp.wait_send()
        mine.wait()

    return pl.pallas_call(
        body, name=name, in_specs=[_ANY], out_specs=_ANY,
        out_shape=jax.ShapeDtypeStruct((N_DEV,) + blk.shape, blk.dtype),
        scratch_shapes=[pltpu.SemaphoreType.DMA((7,)), pltpu.SemaphoreType.DMA((7,)), pltpu.SemaphoreType.DMA],
    )(blk)


def _rs_sibling(name, gb):
    def body(g_ref, out_ref, send_sems, recv_sems):
        x, y, c = _me()
        cps = []
        for j in range(4):
            cp = pltpu.make_async_remote_copy(
                src_ref=g_ref.at[2 * j + (1 - c)], dst_ref=out_ref.at[j], send_sem=send_sems.at[j],
                recv_sem=recv_sems.at[j], device_id=(x, y, 1 - c), device_id_type=_MESH)
            cp.start()
            cps.append(cp)
        for cp in cps:
            cp.wait()

    return pl.pallas_call(
        body, name=name, in_specs=[_ANY], out_specs=_ANY,
        out_shape=jax.ShapeDtypeStruct((4,) + gb.shape[1:], gb.dtype),
        scratch_shapes=[pltpu.SemaphoreType.DMA((4,)), pltpu.SemaphoreType.DMA((4,))],
    )(gb)


def _rs_chips(name, s1):
    def body(s_ref, out_ref, send_sems, recv_sems):
        x, y, c = _me()
        cps = []
        for k, (fx, fy) in enumerate(_CHIP_FLIPS):
            px, py = x ^ fx, y ^ fy
            cp = pltpu.make_async_remote_copy(
                src_ref=s_ref.at[2 * px + py], dst_ref=out_ref.at[k], send_sem=send_sems.at[k],
                recv_sem=recv_sems.at[k], device_id=(px, py, c), device_id_type=_MESH)
            cp.start()
            cps.append(cp)
        for cp in cps:
            cp.wait()

    return pl.pallas_call(
        body, name=name, in_specs=[_ANY], out_specs=_ANY,
        out_shape=jax.ShapeDtypeStruct((3,) + s1.shape[1:], s1.dtype),
        scratch_shapes=[pltpu.SemaphoreType.DMA((3,)), pltpu.SemaphoreType.DMA((3,))],
    )(s1)


def _row_tile(r):
    divs = [d for d in range(16, min(r, 512) + 1, 16) if r % d == 0]
    return divs[-1] if divs else r


def _pair_sum(name, gb, recv):
    _, r, cols = gb.shape
    tm = _row_tile(r)
    c = lax.axis_index("c").astype(jnp.int32).reshape(1)

    def body(c_ref, a_ref, b_ref, o_ref, ob_ref):
        s = a_ref[...] + b_ref[...]
        o_ref[...] = s
        ob_ref[...] = s.astype(BF16)

    blk = pl.BlockSpec((1, tm, cols), lambda j, i, c_ref: (j, i, 0))
    return pl.pallas_call(
        body, name=name,
        grid_spec=pltpu.PrefetchScalarGridSpec(
            num_scalar_prefetch=1, grid=(4, r // tm),
            in_specs=[pl.BlockSpec((1, tm, cols), lambda j, i, c_ref: (2 * j + c_ref[0], i, 0)), blk],
            out_specs=[blk, blk]),
        out_shape=[jax.ShapeDtypeStruct((4, r, cols), F32), jax.ShapeDtypeStruct((4, r, cols), BF16)],
        compiler_params=pltpu.CompilerParams(dimension_semantics=("parallel", "parallel")),
    )(c, gb, recv)


def _adam_math(g, w, m, v):
    m_new = ADAM_B1 * m + (1.0 - ADAM_B1) * g
    v_new = ADAM_B2 * v + (1.0 - ADAM_B2) * (g * g)
    m_hat = m_new / (1.0 - ADAM_B1 ** ADAM_STEP)
    v_hat = v_new / (1.0 - ADAM_B2 ** ADAM_STEP)
    return -ADAM_LR * (m_hat / (jnp.sqrt(v_hat) + ADAM_EPS) + ADAM_WD * w), m_new, v_new


def _adam_vectors(name, row, items, ws, ms, vs):
    k = len(items)

    def body(row_ref, *refs):
        w_refs, m_refs, v_refs = refs[:k], refs[k:2 * k], refs[2 * k:3 * k]
        outs = refs[3 * k:]
        for idx, (off, n, per_head) in enumerate(items):
            if per_head:
                spread = row_ref[:, off:off + DN_WIDTH]
                lane = lax.broadcasted_iota(jnp.int32, (1, LANE), 1)
                g = jnp.zeros((1, LANE), F32)
                for h in range(DN_HEADS):
                    g = g + jnp.where(lane == h, spread[:, DN_DIM * h:DN_DIM * h + 1], 0.0)
            else:
                g = row_ref[:, off:off + n]
            d, m_new, v_new = _adam_math(g, w_refs[idx][...], m_refs[idx][...], v_refs[idx][...])
            for kind, val in enumerate((g, d, m_new, v_new)):
                outs[kind * k + idx][...] = val

    shapes = [jax.ShapeDtypeStruct((1, n), F32) for _, n, _ in items]
    res = pl.pallas_call(body, name=name, out_shape=shapes * 4)(row, *ws, *ms, *vs)
    return [list(res[kind * k:(kind + 1) * k]) for kind in range(4)]


def _sum_parts(name, parts):
    _, r, cols = parts[0][0].shape
    tm = _row_tile(r)
    idx = jnp.stack([jnp.asarray(s, jnp.int32) for _, s in parts])
    n = len(parts)

    def body(idx_ref, *refs):
        g = refs[0][0].astype(F32)
        for p_ref in refs[1:n]:
            g = g + p_ref[0].astype(F32)
        refs[n][...] = g

    return pl.pallas_call(
        body, name=name,
        grid_spec=pltpu.PrefetchScalarGridSpec(
            num_scalar_prefetch=1, grid=(r // tm,),
            in_specs=[pl.BlockSpec((1, tm, cols), lambda i, idx_ref, p=p: (idx_ref[p], i, 0)) for p in range(n)],
            out_specs=pl.BlockSpec((tm, cols), lambda i, idx_ref: (i, 0))),
        out_shape=jax.ShapeDtypeStruct((r, cols), F32),
        compiler_params=pltpu.CompilerParams(dimension_semantics=("parallel",)),
    )(idx, *[a for a, _ in parts])


def _adam(name, parts, w, m, v):
    r, cols = w.shape
    tm = _row_tile(r)
    idx = jnp.stack([jnp.asarray(s, jnp.int32) for _, s in parts])
    n = len(parts)

    def body(idx_ref, *refs):
        g = refs[0][0].astype(F32)
        for p_ref in refs[1:n]:
            g = g + p_ref[0].astype(F32)
        w_ref, m_ref, v_ref, g_out, d_out, m_out, v_out = refs[n:]
        g_out[...] = g
        d_out[...], m_out[...], v_out[...] = _adam_math(g, w_ref[...], m_ref[...], v_ref[...])

    part_specs = [pl.BlockSpec((1, tm, cols), lambda i, idx_ref, p=p: (idx_ref[p], i, 0)) for p in range(n)]
    flat = pl.BlockSpec((tm, cols), lambda i, idx_ref: (i, 0))
    return pl.pallas_call(
        body, name=name,
        grid_spec=pltpu.PrefetchScalarGridSpec(
            num_scalar_prefetch=1, grid=(r // tm,), in_specs=part_specs + [flat] * 3, out_specs=[flat] * 4),
        out_shape=[jax.ShapeDtypeStruct((r, cols), F32)] * 4,
        compiler_params=pltpu.CompilerParams(dimension_semantics=("parallel",)),
    )(idx, *[a for a, _ in parts], w, m, v)


def _all_gather_many(name, blks):
    n = len(blks)

    def body(*refs):
        x_refs, out_refs = refs[:n], refs[n:2 * n]
        send_sems, recv_sems, local_sems = refs[2 * n:]
        x, y, c = _me()
        me, sib = (x, y, c), (x, y, 1 - c)
        chips = [(x ^ fx, y ^ fy) for fx, fy in _CHIP_FLIPS]

        def slot(a, p):
            return out_refs[a].at[4 * p[0] + 2 * p[1] + p[2]]

        def copy(a, k, block, to, src=None):
            return pltpu.make_async_remote_copy(
                src_ref=slot(a, block) if src is None else src, dst_ref=slot(a, block),
                send_sem=send_sems.at[7 * a + k], recv_sem=recv_sems.at[7 * a + k], device_id=to,
                device_id_type=_MESH)

        mine = [pltpu.make_async_copy(x_refs[a], slot(a, me), local_sems.at[a]) for a in range(n)]
        first = []
        for a in range(n):
            mine[a].start()
            first.append(copy(a, 0, me, sib, src=x_refs[a]))
            first += [copy(a, 1 + j, me, (*chip, c), src=x_refs[a]) for j, chip in enumerate(chips)]
        for cp in first:
            cp.start()
        passed = []
        for j, chip in enumerate(chips):
            for a in range(n):
                copy(a, 1 + j, (*chip, c), me).wait_recv()
                cp = copy(a, 4 + j, (*chip, c), sib)
                cp.start()
                passed.append(cp)
        for a in range(n):
            copy(a, 0, sib, me).wait_recv()
            for j, chip in enumerate(chips):
                copy(a, 4 + j, (*chip, 1 - c), me).wait_recv()
        for cp in first + passed:
            cp.wait_send()
        for cp in mine:
            cp.wait()

    return pl.pallas_call(
        body, name=name, in_specs=[_ANY] * n, out_specs=[_ANY] * n,
        out_shape=[jax.ShapeDtypeStruct((N_DEV,) + b.shape, b.dtype) for b in blks],
        scratch_shapes=[pltpu.SemaphoreType.DMA((7 * n,)), pltpu.SemaphoreType.DMA((7 * n,)),
                        pltpu.SemaphoreType.DMA((n,))],
    )(*blks)


def _rs_sibling_many(name, gbs):
    n = len(gbs)

    def body(*refs):
        g_refs, out_refs = refs[:n], refs[n:2 * n]
        send_sems, recv_sems = refs[2 * n:]
        x, y, c = _me()
        cps = []
        for a in range(n):
            for j in range(4):
                cp = pltpu.make_async_remote_copy(
                    src_ref=g_refs[a].at[2 * j + (1 - c)], dst_ref=out_refs[a].at[j],
                    send_sem=send_sems.at[4 * a + j], recv_sem=recv_sems.at[4 * a + j],
                    device_id=(x, y, 1 - c), device_id_type=_MESH)
                cp.start()
                cps.append(cp)
        for cp in cps:
            cp.wait()

    return pl.pallas_call(
        body, name=name, in_specs=[_ANY] * n, out_specs=[_ANY] * n,
        out_shape=[jax.ShapeDtypeStruct((4,) + g.shape[1:], g.dtype) for g in gbs],
        scratch_shapes=[pltpu.SemaphoreType.DMA((4 * n,)), pltpu.SemaphoreType.DMA((4 * n,))],
    )(*gbs)


def _rs_chips_many(name, s1s):
    n = len(s1s)

    def body(*refs):
        s_refs, out_refs = refs[:n], refs[n:2 * n]
        send_sems, recv_sems = refs[2 * n:]
        x, y, c = _me()
        cps = []
        for a in range(n):
            for k, (fx, fy) in enumerate(_CHIP_FLIPS):
                px, py = x ^ fx, y ^ fy
                cp = pltpu.make_async_remote_copy(
                    src_ref=s_refs[a].at[2 * px + py], dst_ref=out_refs[a].at[k],
                    send_sem=send_sems.at[3 * a + k], recv_sem=recv_sems.at[3 * a + k],
                    device_id=(px, py, c), device_id_type=_MESH)
                cp.start()
                cps.append(cp)
        for cp in cps:
            cp.wait()

    return pl.pallas_call(
        body, name=name, in_specs=[_ANY] * n, out_specs=[_ANY] * n,
        out_shape=[jax.ShapeDtypeStruct((3,) + s.shape[1:], s.dtype) for s in s1s],
        scratch_shapes=[pltpu.SemaphoreType.DMA((3 * n,)), pltpu.SemaphoreType.DMA((3 * n,))],
    )(*s1s)


_HBM = pl.BlockSpec(memory_space=pltpu.HBM)
_SEM = pl.BlockSpec(memory_space=pltpu.SEMAPHORE)
_EFFECT = pltpu.SideEffectType.DATAFLOW_SIDE_EFFECTING


def _push_copies(src_refs, land_refs, send_sems, recv_sems, src_by_peer):
    x, y, c = _me()
    my_id = 4 * x + 2 * y + c
    out = []
    for a in range(len(src_refs)):
        for f in range(1, N_DEV):
            px, py, pc = x ^ (f >> 2), y ^ ((f >> 1) & 1), c ^ (f & 1)
            pid = 4 * px + 2 * py + pc
            src = src_refs[a].at[pid] if src_by_peer else src_refs[a]
            start = pltpu.make_async_remote_copy(
                src_ref=src, dst_ref=land_refs[a].at[my_id], send_sem=send_sems.at[7 * a + f - 1],
                recv_sem=recv_sems.at[7 * a + f - 1], device_id=(px, py, pc), device_id_type=_MESH)
            landed = pltpu.make_async_remote_copy(
                src_ref=src, dst_ref=land_refs[a].at[pid], send_sem=send_sems.at[7 * a + f - 1],
                recv_sem=recv_sems.at[7 * a + f - 1], device_id=(px, py, pc), device_id_type=_MESH)
            out.append((start, landed))
    return out


def _push_start(name, srcs, src_by_peer, after):
    n = len(srcs)
    lands = [jax.ShapeDtypeStruct((N_DEV,) + (s.shape[1:] if src_by_peer else s.shape), s.dtype) for s in srcs]

    def body(*refs):
        src_refs, land_refs = refs[:n], refs[n:2 * n]
        send_sems, recv_sems = refs[2 * n + 1], refs[2 * n + 2]
        token = refs[-1]
        for start, _ in _push_copies(src_refs, land_refs, send_sems, recv_sems, src_by_peer):
            start.start()
        token[...] = jnp.zeros_like(token)

    hbm = lambda a: pltpu.with_memory_space_constraint(a, pltpu.HBM)
    res = pl.pallas_call(
        body, name=name,
        out_shape=(pltpu.SemaphoreType.DMA((7 * n,)), pltpu.SemaphoreType.DMA((7 * n,)),
                   *[pltpu.HBM(s.shape, s.dtype) for s in srcs], *[pltpu.HBM(s.shape, s.dtype) for s in lands],
                   jax.ShapeDtypeStruct((8, LANE), F32)),
        in_specs=[_HBM] * (2 * n) + [_ANY],
        out_specs=(_SEM, _SEM, *[_HBM] * (2 * n), pl.BlockSpec(memory_space=pltpu.VMEM)),
        input_output_aliases={i: 2 + i for i in range(2 * n)},
        compiler_params=pltpu.CompilerParams(has_side_effects=_EFFECT),
    )(*[hbm(s) for s in srcs], *[hbm(lax.empty(s.shape, s.dtype)) for s in lands], after)
    return res[0], res[1], list(res[2:2 + n]), list(res[2 + n:2 + 2 * n]), res[-1]


def _push_wait(name, send_sems, recv_sems, srcs, lands, src_by_peer, after):
    n = len(srcs)

    def body(*refs):
        src_refs, land_refs = refs[:n], refs[n:2 * n]
        s_sems, r_sems = refs[2 * n], refs[2 * n + 1]
        for _, landed in _push_copies(src_refs, land_refs, s_sems, r_sems, src_by_peer):
            landed.wait_send()
            landed.wait_recv()

    res = pl.pallas_call(
        body, name=name,
        out_shape=tuple(pltpu.HBM(s.shape, s.dtype) for s in list(srcs) + list(lands)),
        in_specs=[_HBM] * (2 * n) + [_SEM, _SEM, _ANY],
        out_specs=tuple([_HBM] * (2 * n)),
        input_output_aliases={i: i for i in range(2 * n)},
        compiler_params=pltpu.CompilerParams(has_side_effects=_EFFECT),
    )(*srcs, *lands, send_sems, recv_sems, after)
    return list(res[:n]), list(res[n:])


_SHARDED = (
    ("meta_tokens", 1, (N_META, D_MODEL)),
    ("w_in", 1, (D_MODEL, IN_COLS)),
    ("w_q_b", 1, (Q_LORA, MLA_HEADS * QK_HEAD)),
    ("w_kv_b", 1, (KV_LORA, MLA_HEADS * (QK_NOPE + V_HEAD))),
    ("dn_conv_w", 1, (DN_CONV, 3 * DN_WIDTH)),
    ("w_out", 0, (2 * DN_WIDTH, D_MODEL)),
    ("w_gate", 1, (D_MODEL, D_FF)),
    ("w_up", 1, (D_MODEL, D_FF)),
    ("ffn_conv_w", 1, (FFN_CONV, D_FF)),
    ("w_down", 0, (D_FF, D_MODEL)),
)
_MXU_GATHERED = ("w_in", "w_q_b", "w_kv_b", "w_out", "w_gate", "w_up", "w_down")
_F32_GATHERED = ("meta_tokens", "dn_conv_w", "ffn_conv_w")
_EARLY = ("w_in", "w_q_b", "w_kv_b")
_LATE = ("w_out", "w_gate", "w_up", "w_down")
_TRANSPOSED = ("w_in", "w_q_b", "w_gate", "w_up")
_REPLICATED = (
    ("attn_norm_w", D_MODEL), ("q_a_norm_w", Q_LORA), ("kv_a_norm_w", KV_LORA), ("q_norm_w", QK_HEAD),
    ("k_norm_w", QK_HEAD), ("mla_out_norm_w", V_HEAD), ("dn_A_log", DN_HEADS), ("dn_dt_bias", DN_HEADS),
    ("dn_out_norm_w", DN_DIM), ("ffn_norm_w", D_MODEL), ("ffn_conv_b", D_FF),
)
_PACK_COLS = 1024
_PACK_ROW_MULT = 320
_SMALL_SHAPE = (8, 768)
_SMALL_BLOCK = (8, 512)


def _local_shape(dim, shape):
    return (shape[0] // N_DEV, shape[1]) if dim == 0 else (shape[0], shape[1] // N_DEV)


def _pack_rows(n, mult):
    rows = -(-n // _PACK_COLS)
    return -(-rows // mult) * mult


def _pack(flats, mult, axis=0):
    cat = jnp.concatenate(flats, axis=-1)
    n = cat.shape[-1]
    r = _pack_rows(n, mult)
    pad = [(0, 0)] * (cat.ndim - 1) + [(0, r * _PACK_COLS - n)]
    return jnp.pad(cat, pad).reshape(cat.shape[:-1] + (r, _PACK_COLS))


def _to_blocks(full, dim):
    r, c = full.shape
    if dim == 0:
        return full.reshape(N_DEV, (r // N_DEV) * c)
    return full.reshape(r, N_DEV, c // N_DEV).transpose(1, 0, 2).reshape(N_DEV, r * (c // N_DEV))


def _from_blocks(blocks, dim, shape):
    r, c = shape
    if dim == 0:
        return blocks.reshape(r, c)
    return blocks.reshape(N_DEV, r, c // N_DEV).transpose(1, 0, 2).reshape(r, c)


def _split(flat, sizes):
    out, o = [], 0
    for s in sizes:
        out.append(flat[..., o:o + s])
        o += s
    return out


def _gather_weights(local, names, dtype, mult):
    specs = [s for s in _SHARDED if s[0] in names]
    pack = _pack([local[n].astype(dtype).reshape(-1) for n, _, _ in specs], mult)
    got = _all_gather("gather_" + "_".join(n[:5] for n in names[:2]), pack)
    flat = got.reshape(N_DEV, -1)
    sizes = [math.prod(_local_shape(d, s)) for _, d, s in specs]
    return {n: _from_blocks(p, d, s) for (n, d, s), p in zip(specs, _split(flat, sizes))}


def kernel(x, meta_tokens, attn_norm_w, w_in, q_a_norm_w, w_q_b, kv_a_norm_w, w_kv_b, q_norm_w, k_norm_w, mla_out_norm_w, dn_conv_w, dn_A_log, dn_dt_bias, dn_out_norm_w, w_out, ffn_norm_w, w_gate, w_up, ffn_conv_w, ffn_conv_b, w_down, loss_target, m_meta_tokens, m_attn_norm_w, m_w_in, m_q_a_norm_w, m_w_q_b, m_kv_a_norm_w, m_w_kv_b, m_q_norm_w, m_k_norm_w, m_mla_out_norm_w, m_dn_conv_w, m_dn_A_log, m_dn_dt_bias, m_dn_out_norm_w, m_w_out, m_ffn_norm_w, m_w_gate, m_w_up, m_ffn_conv_w, m_ffn_conv_b, m_w_down, v_meta_tokens, v_attn_norm_w, v_w_in, v_q_a_norm_w, v_w_q_b, v_kv_a_norm_w, v_w_kv_b, v_q_norm_w, v_k_norm_w, v_mla_out_norm_w, v_dn_conv_w, v_dn_A_log, v_dn_dt_bias, v_dn_out_norm_w, v_w_out, v_ffn_norm_w, v_w_gate, v_w_up, v_ffn_conv_w, v_ffn_conv_b, v_w_down):
    names = [n for n, _, _ in _SHARDED] + [n for n, _ in _REPLICATED]
    given = dict(locals())
    two_d = lambda a: a.reshape(a.shape[-2:])
    view = lambda a, n: two_d(a).T if n in _TRANSPOSED else two_d(a)
    wl = {n: view(given[n], n) for n in names}
    ml = {n: view(given["m_" + n], n) for n in names}
    vl = {n: view(given["v_" + n], n) for n in names}
    out_shapes = {n: given[n].shape for n in names}

    spec = {n: (d, s) for n, d, s in _SHARDED}
    small_sizes = [math.prod(_local_shape(*spec[n])) for n in _F32_GATHERED]

    def small_block(d):
        cat = jnp.concatenate([d[n].reshape(d[n].shape[:-2] + (-1,)) for n in _F32_GATHERED], axis=-1)
        pad = [(0, 0)] * (cat.ndim - 1) + [(0, math.prod(_SMALL_BLOCK) - cat.shape[-1])]
        return jnp.pad(cat, pad).reshape(cat.shape[:-1] + _SMALL_BLOCK)

    def shard(n):
        return wl[n].astype(_MXU)

    def from_slots(n, blocks):
        d, s = spec[n]
        if d == 0 or n in _TRANSPOSED:
            return blocks.reshape(-1, blocks.shape[-1])
        return blocks.transpose(1, 0, 2).reshape(s)

    my_id = 4 * lax.axis_index("x") + 2 * lax.axis_index("y") + lax.axis_index("c")
    got = _all_gather_many("gather_early", [shard(n) for n in _EARLY] + [small_block(wl)])
    full = {n: a for n, a in wl.items() if n not in _LATE}
    for n, blocks in zip(_EARLY, got):
        full[n] = from_slots(n, blocks)
    for n, p in zip(_F32_GATHERED, _split(got[-1].reshape(N_DEV, -1), small_sizes)):
        full[n] = _from_blocks(p, *spec[n])
    late_own = [shard(n) for n in _LATE]
    l_send, l_recv, l_src, l_land, token = _push_start("gather_late_start", late_own, False, got[-1])

    def late_weights(after):
        _, lands = _push_wait("gather_late_wait", l_send, l_recv, l_src, l_land, False, after)
        out = {}
        for n, land, own in zip(_LATE, lands, late_own):
            out[n] = from_slots(n, lax.dynamic_update_slice(land, own[None], (my_id, 0, 0))).astype(_MXU)
        return out

    def dest_blocks(n, a):
        d, s = spec[n]
        r, c = _local_shape(d, s)
        if n in _TRANSPOSED:
            return a.reshape(N_DEV, c, r)
        return a.reshape(N_DEV, r, c) if d == 0 else a.reshape(r, N_DEV, c).transpose(1, 0, 2)

    pushed = []

    def grads_ready(g, names):
        nat = _grads_to_natural({n: g[n] for n in names})
        blocks = [dest_blocks(n, nat[n]).astype(_MXU) for n in names]
        sends, recvs, srcs, lands, tok = _push_start("rs_" + names[0] + "_start", blocks, True, token)
        pushed.append((names, sends, recvs, srcs, lands))
        return tok

    seq = x.shape[1]
    tp = ROW0 + seq
    h0 = jnp.concatenate([jnp.zeros((PAD, D_MODEL), F32), full["meta_tokens"], x[0]], axis=0)
    tgt = jnp.concatenate([jnp.zeros((ROW0, D_MODEL), F32), loss_target[0]], axis=0)
    loss, dh0, raw = _local_step(h0, tgt, _prepare(full, tp), token, late_weights, grads_ready)
    g = _grads_to_natural(raw)
    g["meta_tokens"] = dh0[PAD:ROW0]
    grad_x = dh0[ROW0:][None]

    big = [{}, {}, {}, {}]
    rep_names = [n for n, _ in _REPLICATED]
    raw_key = {"dn_A_log": "alog_b", "dn_dt_bias": "dtb_b"}
    pieces = [raw[raw_key.get(n, n)] for n in rep_names] + [loss]
    pieces += [g[n].reshape(1, -1) for n in _F32_GATHERED]
    widths = [p.shape[1] for p in pieces]
    offs = [sum(widths[:k]) for k in range(len(widths))]
    cat = jnp.concatenate(pieces, axis=1)
    cols = -(-cat.shape[1] // (8 * LANE)) * LANE
    mine = jnp.pad(cat, ((0, 0), (0, 8 * cols - cat.shape[1]))).reshape(8, cols)
    everyone = _all_gather("gather_small_grads", mine)
    total = _sum_parts("sum_small_grads", [(everyone, d) for d in range(N_DEV)]).reshape(1, 8 * cols)
    tot = {n: total[0, o:o + wd] for n, o, wd in zip(rep_names + ["loss"] + list(_F32_GATHERED), offs, widths)}
    lanes = lambda a: jnp.pad(a, ((0, 0), (0, -a.shape[1] % LANE)))
    items = [(o, -(-size // LANE) * LANE, n in raw_key) for (n, size), o in zip(_REPLICATED, offs)]
    sm = _adam_vectors("adam_replicated", total, items, [lanes(wl[n]) for n in rep_names],
                       [lanes(ml[n]) for n in rep_names], [lanes(vl[n]) for n in rep_names])
    sm = [{n: a[:, :size] for (n, size), a in zip(_REPLICATED, kind)} for kind in sm]
    mine_of = {}
    for n in _F32_GATHERED:
        d, s = spec[n]
        r, c = _local_shape(d, s)
        mine_of[n] = lax.dynamic_slice(tot[n].reshape(s), (0, my_id * c), (r, c))
    res = _adam("adam_small_sharded", [(small_block(mine_of)[None], 0)], small_block(wl), small_block(ml),
                small_block(vl))
    for kind, a in enumerate(res):
        big[kind].update(zip(_F32_GATHERED, _split(a.reshape(-1), small_sizes)))

    for names, sends, recvs, srcs, lands in pushed:
        srcs, lands = _push_wait("rs_" + names[0] + "_wait", sends, recvs, srcs, lands, True, dh0)
        for n, src, land in zip(names, srcs, lands):
            parts = [(src, my_id)] + [(land, my_id ^ f) for f in range(1, N_DEV)]
            for kind, a in enumerate(_adam("adam_" + n, parts, wl[n], ml[n], vl[n])):
                big[kind][n] = a

    outs = [tot["loss"][0], grad_x]
    for kind in range(4):
        for n in ("meta_tokens", "attn_norm_w", "w_in", "q_a_norm_w", "w_q_b", "kv_a_norm_w", "w_kv_b", "q_norm_w",
                  "k_norm_w", "mla_out_norm_w", "dn_conv_w", "dn_A_log", "dn_dt_bias", "dn_out_norm_w", "w_out",
                  "ffn_norm_w", "w_gate", "w_up", "ffn_conv_w", "ffn_conv_b", "w_down"):
            src = big[kind] if n in big[kind] else sm[kind]
            a = src[n].T if n in _TRANSPOSED else src[n]
            outs.append(a.reshape(out_shapes[n]))
    return tuple(outs)
```

```python
import functools
import math

import jax
import jax.numpy as jnp
from jax import lax
from jax.experimental import pallas as pl
from jax.experimental.pallas import tpu as pltpu

F32 = jnp.float32
BF16 = jnp.bfloat16
_MXU = jnp.bfloat16
_HI = lax.Precision.HIGHEST

D_MODEL = 1024
N_META = 16
PAD = 112
ROW0 = PAD + N_META
MLA_HEADS = 4
QK_NOPE = 128
QK_ROPE = 64
QK_HEAD = QK_NOPE + QK_ROPE
V_HEAD = 128
Q_LORA = 256
KV_LORA = 256
ROPE_THETA = 10000.0
DN_HEADS = 4
DN_DIM = 128
DN_WIDTH = DN_HEADS * DN_DIM
DN_CONV = 4
DN_CHUNK = 64
GDN_SUB_CHUNKS = 2
D_FF = 2816
FFN_CONV = 3
EPS = 1e-6
HP = 256
C_QKV = 0
C_Z = 1536
C_QL = 2048
C_KVL = 2304
C_KPE = 2560
C_AB = 2688
IN_P = 2816
IN_COLS = 2632

ADAM_LR = 0.001
ADAM_B1 = 0.9
ADAM_B2 = 0.999
ADAM_EPS = 1e-08
ADAM_WD = 0.01
ADAM_STEP = 10

N_DEV = 8
TM = 128
LANE = 128
VMEM_LIMIT = 56 * 1024 * 1024
NEG = -1e30


def _dot(a, b, dims, hp=False):
    if hp:
        return lax.dot_general(a.astype(F32), b.astype(F32), (dims, ((), ())),
                               precision=lax.Precision.HIGH if hp == "3x" else _HI, preferred_element_type=F32)
    return lax.dot_general(a.astype(_MXU), b.astype(_MXU), (dims, ((), ())),
                           preferred_element_type=F32)


def _nn(a, b, hp=False):
    return _dot(a, b, ((1,), (0,)), hp)


def _nt(a, b, hp=False):
    return _dot(a, b, ((1,), (1,)), hp)


def _tn(a, b, hp=False):
    return _dot(a, b, ((0,), (0,)), hp)


def _sigmoid(x):
    return 1.0 / (1.0 + jnp.exp(-x))


def _rms_fwd(x, w, n):
    r = lax.rsqrt(jnp.sum(x * x, axis=-1, keepdims=True) * (1.0 / n) + EPS)
    return x * r * w, r


def _rms_bwd(x, w, dy, n):
    r = lax.rsqrt(jnp.sum(x * x, axis=-1, keepdims=True) * (1.0 / n) + EPS)
    xh = x * r
    gy = dy * w
    dx = r * (gy - xh * (jnp.sum(gy * xh, axis=-1, keepdims=True) * (1.0 / n)))
    return dx, dy * xh


def _rowsum(x):
    return jnp.sum(x, axis=0, keepdims=True)


def _row_ids(i, tm):
    return i * tm + lax.broadcasted_iota(jnp.int32, (tm, 1), 0)


def _shift_down(ext, s, tm):
    if s == 0:
        return ext[8:8 + tm]
    return pltpu.roll(ext, s, 0)[8:8 + tm]


def _shift_up(ext, s, tm):
    if s == 0:
        return ext[0:tm]
    return pltpu.roll(ext, tm + 8 - s, 0)[0:tm]


def _conv_fwd(x, halo_prev, w, width):
    tm = x.shape[0]
    ext = jnp.concatenate([halo_prev, x], axis=0)
    y = None
    for j in range(width):
        t = w[j:j + 1, :] * _shift_down(ext, width - 1 - j, tm)
        y = t if y is None else y + t
    return y


def _conv_bwd_x(dy, halo_next, w, width):
    tm = dy.shape[0]
    ext = jnp.concatenate([dy, halo_next], axis=0)
    dx = None
    for j in range(width):
        t = w[j:j + 1, :] * _shift_up(ext, width - 1 - j, tm)
        dx = t if dx is None else dx + t
    return dx


def _conv_bwd_w(dy, x, halo_prev, width):
    tm = dy.shape[0]
    ext = jnp.concatenate([halo_prev, x], axis=0)
    rows = [_rowsum(dy * _shift_down(ext, width - 1 - j, tm)) for j in range(width)]
    rows += [jnp.zeros_like(rows[0])] * (8 - width)
    return jnp.concatenate(rows, axis=0)


def _softplus(x):
    e = jnp.exp(-jnp.abs(x))
    u = 1.0 + e
    l1p = jnp.where(u == 1.0, e, jnp.log(u) * e / jnp.where(u == 1.0, 1.0, u - 1.0))
    return jnp.maximum(x, 0.0) + l1p


def _swap_halves(x):
    lane = lax.broadcasted_iota(jnp.int32, x.shape, 1)
    return jnp.where(lane < 32, pltpu.roll(x, 96, 1), jnp.where(lane < 64, pltpu.roll(x, 32, 1), 0.0))


class _In:
    def __init__(self, arr, width=None, cb=0, kind="cur"):
        self.arr, self.kind = arr, kind
        self.width = arr.shape[1] if width is None else width
        self.cb = cb


def _whole_spec(x):
    return pl.BlockSpec(x.shape, lambda i, nd=x.ndim: (0,) * nd, pipeline_mode=pl.Buffered(1))


def _tile_spec(t, tm, tp):
    r8 = tm // 8
    if t.kind == "cur":
        return pl.BlockSpec((tm, t.width), lambda i, cb=t.cb: (i, cb))
    if t.kind == "prev":
        return pl.BlockSpec((8, t.width), lambda i, cb=t.cb: (jnp.maximum(i * r8 - 1, 0), cb))
    return pl.BlockSpec((8, t.width), lambda i, cb=t.cb: (jnp.minimum((i + 1) * r8, tp // 8 - 1), cb))


def _rows(name, fn, tiled, full, outs, accs=(), tm=TM):
    tp = tiled[0].arr.shape[0]
    nt = tp // tm
    r8 = tm // 8
    n_in = len(tiled) + len(full)
    n_out = len(outs)

    def body(*refs):
        i = pl.program_id(0)
        vals = [r[...] for r in refs[:n_in]]
        o_t, o_a = fn(i, *vals)
        for r, v in zip(refs[n_in:n_in + n_out], o_t):
            r[...] = v.astype(r.dtype)
        for r, v in zip(refs[n_in + n_out:], o_a):
            @pl.when(i == 0)
            def _():
                r[...] = v

            @pl.when(i > 0)
            def _():
                r[...] += v

    in_specs = [_tile_spec(t, tm, tp) for t in tiled]
    in_specs += [pl.BlockSpec(a.shape, lambda i, nd=a.ndim: (0,) * nd) for a in full]
    out_specs = [pl.BlockSpec((tm, w), lambda i: (i, 0)) for w, _ in outs]
    out_specs += [pl.BlockSpec((r, w), lambda i: (0, 0)) for r, w in accs]
    out_shape = [jax.ShapeDtypeStruct((tp, w), dt) for w, dt in outs]
    out_shape += [jax.ShapeDtypeStruct((r, w), F32) for r, w in accs]
    res = pl.pallas_call(
        body, name=name, grid=(nt,), in_specs=in_specs, out_specs=out_specs, out_shape=out_shape,
        compiler_params=pltpu.CompilerParams(dimension_semantics=("arbitrary",), vmem_limit_bytes=VMEM_LIMIT),
    )(*[t.arr for t in tiled], *full)
    return res


def _pick(n, cap, mult):
    best = None
    for d in range(mult, min(n, cap) + 1, mult):
        if n % d == 0:
            best = d
    assert best is not None, (n, cap, mult)
    return best


_ANY_SPEC = pl.BlockSpec(memory_space=pl.ANY)


def _mm(name, a, b, mode, out_dtype=F32, resid=None, after=None):
    if mode == "tn":
        m, k = a.shape
        n = b.shape[1]
        tk = _pick(k, 512, 128)
        tn = _pick(n, 1408, 128)

        def body_tn(a_ref, b_ref, o_ref):
            o_ref[...] = _tn(a_ref[...], b_ref[...]).astype(o_ref.dtype)

        return pl.pallas_call(
            body_tn, name=name, grid=(n // tn, k // tk),
            in_specs=[pl.BlockSpec((m, tk), lambda j, p: (0, p)),
                      pl.BlockSpec((m, tn), lambda j, p: (0, j))],
            out_specs=pl.BlockSpec((tk, tn), lambda j, p: (p, j)),
            out_shape=jax.ShapeDtypeStruct((k, n), out_dtype),
            compiler_params=pltpu.CompilerParams(
                dimension_semantics=("parallel", "parallel"), vmem_limit_bytes=VMEM_LIMIT),
        )(a, b)

    m, k = a.shape
    n = b.shape[1] if mode == "nn" else b.shape[0]
    tn = _pick(n, 1408, 128)
    tm = _pick(m, 1152, 16)
    dotf = _nn if mode == "nn" else _nt

    def body(*refs):
        a_ref, b_ref, o_ref = refs[0], refs[1], refs[-1]
        acc = dotf(a_ref[...], b_ref[...])
        if resid is not None:
            acc = refs[2][...] + acc
        o_ref[...] = acc.astype(o_ref.dtype)

    b_spec = (pl.BlockSpec((k, tn), lambda j, i: (0, j)) if mode == "nn"
              else pl.BlockSpec((tn, k), lambda j, i: (j, 0)))
    in_specs = [pl.BlockSpec((tm, k), lambda j, i: (i, 0)), b_spec]
    args = [a, b]
    if resid is not None:
        in_specs.append(pl.BlockSpec((tm, tn), lambda j, i: (i, j)))
        args.append(resid)
    if after is not None:
        in_specs.append(_ANY_SPEC)
        args.append(after)
    return pl.pallas_call(
        body, name=name, grid=(n // tn, m // tm), in_specs=in_specs,
        out_specs=pl.BlockSpec((tm, tn), lambda j, i: (i, j)),
        out_shape=jax.ShapeDtypeStruct((m, n), out_dtype),
        compiler_params=pltpu.CompilerParams(
            dimension_semantics=("parallel", "parallel"), vmem_limit_bytes=VMEM_LIMIT),
    )(*args)


def _mm_tn2(name, a1, a2, b, out_dtype=F32):
    m, k = a1.shape
    n = b.shape[1]
    tk = _pick(k, 512, 128)

    def body(a1_ref, a2_ref, b_ref, o1_ref, o2_ref):
        bb = b_ref[...]
        o1_ref[...] = _tn(a1_ref[...], bb).astype(o1_ref.dtype)
        o2_ref[...] = _tn(a2_ref[...], bb).astype(o2_ref.dtype)

    a_spec = pl.BlockSpec((m, tk), lambda p: (0, p))
    o_spec = pl.BlockSpec((tk, n), lambda p: (p, 0))
    return pl.pallas_call(
        body, name=name, grid=(k // tk,),
        in_specs=[a_spec, a_spec, pl.BlockSpec((m, n), lambda p: (0, 0))],
        out_specs=[o_spec, o_spec], out_shape=[jax.ShapeDtypeStruct((k, n), out_dtype)] * 2,
        compiler_params=pltpu.CompilerParams(dimension_semantics=("parallel",), vmem_limit_bytes=VMEM_LIMIT),
    )(a1, a2, b)


def _norm_mm(name, x, norm_w, b, mode="nt", x_cb=0, after=None):
    m = x.shape[0]
    k = norm_w.shape[1]
    n = b.shape[0] if mode == "nt" else b.shape[1]
    tn = _pick(n, 1408, 128)
    tm = _pick(m, 1152, 16)
    dotf = _nt if mode == "nt" else _nn
    extra = [] if after is None else [after]

    def body(x_ref, w_ref, b_ref, *rest):
        o_ref, u_ref = rest[-2:]

        @pl.when(pl.program_id(1) == 0)
        def _():
            u_ref[...] = _rms_fwd(x_ref[...], w_ref[...], k)[0].astype(u_ref.dtype)

        o_ref[...] = dotf(u_ref[...], b_ref[...])

    b_spec = (pl.BlockSpec((tn, k), lambda i, j: (j, 0)) if mode == "nt"
              else pl.BlockSpec((k, tn), lambda i, j: (0, j)))
    return pl.pallas_call(
        body, name=name, grid=(m // tm, n // tn),
        in_specs=[pl.BlockSpec((tm, k), lambda i, j: (i, x_cb)), pl.BlockSpec((1, k), lambda i, j: (0, 0)),
                  b_spec] + [_ANY_SPEC] * len(extra),
        out_specs=[pl.BlockSpec((tm, tn), lambda i, j: (i, j)), pl.BlockSpec((tm, k), lambda i, j: (i, 0))],
        out_shape=[jax.ShapeDtypeStruct((m, n), F32), jax.ShapeDtypeStruct((m, k), _MXU)],
        compiler_params=pltpu.CompilerParams(
            dimension_semantics=("arbitrary", "arbitrary"), vmem_limit_bytes=VMEM_LIMIT),
    )(x, norm_w, b, *extra)


def _pro_mm(name, fn, tiled, full, k, b, resid):
    m = resid.shape[0]
    n = b.shape[1]
    tm = _pick(m, 576, 16)
    n_in = len(tiled) + len(full)

    def body(*refs):
        i = pl.program_id(0)
        u = fn(i, *[r[...] for r in refs[:n_in]]).astype(_MXU)
        b_ref, r_ref, o_ref, u_ref = refs[n_in:]
        u_ref[...] = u
        o_ref[...] = r_ref[...] + _nn(u, b_ref[...])

    row = lambda w: pl.BlockSpec((tm, w), lambda i: (i, 0))
    in_specs = [_tile_spec(t, tm, m) for t in tiled]
    in_specs += [_whole_spec(x) for x in full] + [_whole_spec(b), row(n)]
    return pl.pallas_call(
        body, name=name, grid=(m // tm,), in_specs=in_specs, out_specs=[row(n), row(k)],
        out_shape=[jax.ShapeDtypeStruct((m, n), F32), jax.ShapeDtypeStruct((m, k), _MXU)],
        compiler_params=pltpu.CompilerParams(dimension_semantics=("parallel",), vmem_limit_bytes=VMEM_LIMIT),
    )(*[t.arr for t in tiled], *full, b, resid)


def _ffn_in(h2, norm_w, w_gate_t, w_up_t, conv_w8, conv_b):
    m, k = h2.shape
    n = w_gate_t.shape[0]
    tm = _pick(m, 288, 16)

    def body(x_ref, xp_ref, nw_ref, wg_ref, wu_ref, cw_ref, cb_ref, hn_ref, gp_ref, up_ref, act_ref):
        i = pl.program_id(0)
        nw = nw_ref[...]
        hn = _rms_fwd(x_ref[...], nw, k)[0].astype(_MXU)
        hn_prev = _rms_fwd(xp_ref[...], nw, k)[0].astype(_MXU)
        wg = wg_ref[...]
        gp = _nt(hn, wg)
        gp_prev = jnp.where(i > 0, _nt(hn_prev, wg), 0.0)
        up = _nt(hn, wu_ref[...])
        gate = _conv_fwd(gp, gp_prev, cw_ref[...], FFN_CONV) + cb_ref[...]
        hn_ref[...] = hn
        gp_ref[...] = gp
        up_ref[...] = up
        act_ref[...] = (_silu_parts(gate)[0] * up).astype(act_ref.dtype)

    row = lambda w: pl.BlockSpec((tm, w), lambda i: (i, 0))
    r8 = tm // 8
    return pl.pallas_call(
        body, name="ffn_in", grid=(m // tm,),
        in_specs=[row(k), pl.BlockSpec((8, k), lambda i: (jnp.maximum(i * r8 - 1, 0), 0)), _whole_spec(norm_w),
                  _whole_spec(w_gate_t), _whole_spec(w_up_t), _whole_spec(conv_w8), _whole_spec(conv_b)],
        out_specs=[row(k), row(n), row(n), row(n)],
        out_shape=[jax.ShapeDtypeStruct((m, k), _MXU), jax.ShapeDtypeStruct((m, n), F32),
                   jax.ShapeDtypeStruct((m, n), F32), jax.ShapeDtypeStruct((m, n), _MXU)],
        compiler_params=pltpu.CompilerParams(dimension_semantics=("parallel",), vmem_limit_bytes=VMEM_LIMIT),
    )(h2, h2, norm_w, w_gate_t, w_up_t, conv_w8, conv_b)


def _mm_rows(name, a, b, mode, fn, tiled, full, outs, accs=(), tm_cap=576):
    a_list = list(a) if isinstance(a, (list, tuple)) else [a]
    b_list = list(b) if isinstance(b, (list, tuple)) else [b]
    na = len(a_list)
    m = a_list[0].shape[0]
    tm = _pick(m, tm_cap, 16)
    dotf = _nn if mode == "nn" else _nt
    n_in = len(tiled) + len(full)
    n_out = len(outs)
    first = 2 * na

    def body(*refs):
        i = pl.program_id(0)
        vals = [r[...] for r in refs[first:first + n_in]]
        acc = dotf(refs[0][...], refs[na][...])
        for p in range(1, na):
            acc = acc + dotf(refs[p][...], refs[na + p][...])
        o_t, o_a = fn(i, acc, *vals)
        for r, v in zip(refs[first + n_in:first + n_in + n_out], o_t):
            r[...] = v.astype(r.dtype)
        for r, v in zip(refs[first + n_in + n_out:], o_a):
            @pl.when(i == 0)
            def _():
                r[...] = v

            @pl.when(i > 0)
            def _():
                r[...] += v

    whole = lambda x: pl.BlockSpec(x.shape, lambda i, nd=x.ndim: (0,) * nd)
    in_specs = [pl.BlockSpec((tm, x.shape[1]), lambda i: (i, 0)) for x in a_list] + [_whole_spec(x) for x in b_list]
    in_specs += [_tile_spec(t, tm, m) for t in tiled]
    in_specs += [whole(x) for x in full]
    out_specs = [pl.BlockSpec((tm, w), lambda i: (i, 0)) for w, _ in outs]
    out_specs += [pl.BlockSpec((r, w), lambda i: (0, 0)) for r, w in accs]
    out_shape = [jax.ShapeDtypeStruct((m, w), dt) for w, dt in outs]
    out_shape += [jax.ShapeDtypeStruct((r, w), F32) for r, w in accs]
    return pl.pallas_call(
        body, name=name, grid=(m // tm,), in_specs=in_specs, out_specs=out_specs, out_shape=out_shape,
        compiler_params=pltpu.CompilerParams(dimension_semantics=("arbitrary",), vmem_limit_bytes=VMEM_LIMIT),
    )(*a_list, *b_list, *[t.arr for t in tiled], *full)


ATTN_Q_TILES = 4


def _attn_probs(q, k, row0):
    tq, tp = q.shape[0], k.shape[0]
    s = _nt(q, k) * (1.0 / math.sqrt(QK_HEAD))
    row = row0 + lax.broadcasted_iota(jnp.int32, (tq, tp), 0)
    col = lax.broadcasted_iota(jnp.int32, (tq, tp), 1)
    ok = (col <= row) & (col >= PAD)
    s = jnp.where(ok, s, NEG)
    m = jnp.max(s, axis=-1, keepdims=True)
    e = jnp.exp(s - m)
    return e * (1.0 / jnp.sum(e, axis=-1, keepdims=True))


def _attn_fwd(q, k, v):
    tp = q.shape[0]
    tq = tp // ATTN_Q_TILES

    def body(q_ref, k_ref, v_ref, o_ref):
        for i in range(ATTN_Q_TILES):
            rows = slice(i * tq, (i + 1) * tq)
            keys = slice(0, (i + 1) * tq)
            p = _attn_probs(q_ref[rows, :], k_ref[keys, :], i * tq)
            o_ref[rows, :] = _nn(p, v_ref[keys, :])

    return pl.pallas_call(
        body, name="attn_fwd", grid=(MLA_HEADS,),
        in_specs=[pl.BlockSpec((tp, HP), lambda h: (0, h)),
                  pl.BlockSpec((tp, HP), lambda h: (0, h)),
                  pl.BlockSpec((tp, V_HEAD), lambda h: (0, h))],
        out_specs=pl.BlockSpec((tp, V_HEAD), lambda h: (0, h)),
        out_shape=jax.ShapeDtypeStruct((tp, MLA_HEADS * V_HEAD), F32),
        compiler_params=pltpu.CompilerParams(dimension_semantics=("parallel",), vmem_limit_bytes=VMEM_LIMIT),
    )(q, k, v)


def _attn_bwd(q, k, v, do):
    tp = q.shape[0]
    tq = tp // ATTN_Q_TILES

    def body(q_ref, k_ref, v_ref, do_ref, dq_ref, dk_ref, dv_ref):
        for i in reversed(range(ATTN_Q_TILES)):
            rows = slice(i * tq, (i + 1) * tq)
            keys = slice(0, (i + 1) * tq)
            qb = q_ref[rows, :]
            kk = k_ref[keys, :]
            dob = do_ref[rows, :]
            p = _attn_probs(qb, kk, i * tq)
            dp = _nt(dob, v_ref[keys, :])
            delta = jnp.sum(p * dp, axis=-1, keepdims=True)
            ds = p * (dp - delta) * (1.0 / math.sqrt(QK_HEAD))
            dq_ref[rows, :] = _nn(ds, kk)
            if i == ATTN_Q_TILES - 1:
                dk_ref[...] = _tn(ds, qb)
                dv_ref[...] = _tn(p, dob)
            else:
                dk_ref[keys, :] += _tn(ds, qb)
                dv_ref[keys, :] += _tn(p, dob)

    full = lambda w: pl.BlockSpec((tp, w), lambda h: (0, h))
    return pl.pallas_call(
        body, name="attn_bwd", grid=(MLA_HEADS,),
        in_specs=[full(HP), full(HP), full(V_HEAD), full(V_HEAD)],
        out_specs=[full(HP), full(HP), full(V_HEAD)],
        out_shape=[jax.ShapeDtypeStruct((tp, MLA_HEADS * HP), F32),
                   jax.ShapeDtypeStruct((tp, MLA_HEADS * HP), F32),
                   jax.ShapeDtypeStruct((tp, MLA_HEADS * V_HEAD), F32)],
        compiler_params=pltpu.CompilerParams(dimension_semantics=("parallel",), vmem_limit_bytes=VMEM_LIMIT),
    )(q, k, v, do)


def _gdn_consts():
    c = DN_CHUNK
    r = lax.broadcasted_iota(jnp.int32, (c, c), 0)
    cc = lax.broadcasted_iota(jnp.int32, (c, c), 1)
    incl = r >= cc
    strict = r > cc
    return incl, strict


def _cumsum_rows(x, reverse=False):
    c = x.shape[0]
    row = lax.broadcasted_iota(jnp.int32, x.shape, 0)
    s = 1
    while s < c:
        if reverse:
            x = x + jnp.where(row < c - s, pltpu.roll(x, c - s, 0), 0.0)
        else:
            x = x + jnp.where(row >= s, pltpu.roll(x, s, 0), 0.0)
        s *= 2
    return x


def _each(fn, *lists):
    return [fn(*a) for a in zip(*lists)]


def _interleave(chains):
    chains = list(chains)
    while chains:
        for ch in list(chains):
            try:
                next(ch)
            except StopIteration:
                chains.remove(ch)


def _gdn_chunk_common(q_ref, k_ref, v_ref, g_ref, b_ref):
    c = DN_CHUNK
    incl, strict = _gdn_consts()
    sls = [(slice(c * sub, c * (sub + 1)), slice(DN_DIM * h, DN_DIM * (h + 1)))
           for sub in range(GDN_SUB_CHUNKS) for h in range(DN_HEADS)]
    q = [q_ref[sl] * (1.0 / math.sqrt(DN_DIM)) for sl in sls]
    k = [k_ref[sl] for sl in sls]
    v = [v_ref[sl] for sl in sls]
    g = [g_ref[sl] for sl in sls]
    beta = [b_ref[sl] for sl in sls]
    gc = [_cumsum_rows(x) for x in g]
    grow = [x.T[:c, :] for x in gc]
    kb = _each(jnp.multiply, k, beta)
    kk = _each(_nt, kb, k)
    qk = _each(_nt, q, k)
    gam = [jnp.exp(x) for x in gc]
    g_last = [_rowsum(x) for x in g]
    dm = [jnp.exp(jnp.where(incl, x[:, :c] - y, NEG)) for x, y in zip(gc, grow)]
    vb = _each(jnp.multiply, v, beta)
    kbg = _each(jnp.multiply, kb, gam)
    ek = [jnp.exp(x - y) for x, y in zip(g_last, gc)]
    kd = _each(jnp.multiply, k, ek)
    return dict(q=q, k=k, v=v, beta=beta, gc=gc, gam=gam, g_last=g_last, dm=dm, kb=kb, vb=vb,
                kbg=kbg, kk=kk, ek=ek, kd=kd, qk=qk, incl=incl, strict=strict, sls=sls)


def _gdn_fwd(q, k, v, g, beta):
    tp = q.shape[0]
    c = DN_CHUNK
    nch = tp // c

    def body(q_ref, k_ref, v_ref, g_ref, b_ref, o_ref, s_ref, t_ref, s_scr):
        @pl.when(pl.program_id(0) == 0)
        def _():
            s_scr[...] = jnp.zeros_like(s_scr)

        eye = (lax.broadcasted_iota(jnp.int32, (c, c), 0) == lax.broadcasted_iota(jnp.int32, (c, c), 1)).astype(F32)
        x = _gdn_chunk_common(q_ref, k_ref, v_ref, g_ref, b_ref)
        heads = range(DN_HEADS)
        bp = [-jnp.where(x["strict"], kk * dm, 0.0) for kk, dm in zip(x["kk"], x["dm"])]
        t = [eye + b for b in bp]
        for _ in range(5):
            bp = [_nn(b, b, hp="3x") for b in bp]
            t = [tt + _nn(tt, b, hp="3x") for tt, b in zip(t, bp)]
        u = _each(_nn, t, x["vb"])
        w = _each(_nn, t, x["kbg"])
        qg = _each(jnp.multiply, x["q"], x["gam"])
        mqk = _each(jnp.multiply, x["qk"], x["dm"])
        s = [s_scr[h] for h in heads]
        for sub in range(GDN_SUB_CHUNKS):
            e = [DN_HEADS * sub + h for h in heads]
            v_new = [u[i] - _nn(w[i], s[h]) for h, i in zip(heads, e)]
            o = [_nn(qg[i], s[h]) + _nn(mqk[i], v_new[h]) for h, i in zip(heads, e)]
            s_new = [s[h] * jnp.exp(x["g_last"][i]) + _tn(x["kd"][i], v_new[h]) for h, i in zip(heads, e)]
            for h, i in zip(heads, e):
                s_ref[h, sub] = s[h]
                t_ref[h, sub] = t[i]
                o_ref[x["sls"][i]] = o[h]
            s = s_new
        for h in heads:
            s_scr[h] = s[h]

    sub = GDN_SUB_CHUNKS
    rb = lambda n: (n, 0)
    return pl.pallas_call(
        body, name="gdn_fwd", grid=(nch // sub,),
        in_specs=[pl.BlockSpec((sub * c, DN_WIDTH), rb)] * 5,
        out_specs=[pl.BlockSpec((sub * c, DN_WIDTH), rb),
                   pl.BlockSpec((DN_HEADS, sub, DN_DIM, DN_DIM), lambda n: (0, n, 0, 0)),
                   pl.BlockSpec((DN_HEADS, sub, c, c), lambda n: (0, n, 0, 0))],
        out_shape=[jax.ShapeDtypeStruct((tp, DN_WIDTH), F32),
                   jax.ShapeDtypeStruct((DN_HEADS, nch, DN_DIM, DN_DIM), F32),
                   jax.ShapeDtypeStruct((DN_HEADS, nch, c, c), F32)],
        scratch_shapes=[pltpu.VMEM((DN_HEADS, DN_DIM, DN_DIM), F32)],
        compiler_params=pltpu.CompilerParams(dimension_semantics=("arbitrary",), vmem_limit_bytes=VMEM_LIMIT),
    )(q, k, v, g, beta)


def _gdn_bwd(q, k, v, g, beta, s_all, t_all, do):
    tp = q.shape[0]
    c = DN_CHUNK
    nch = tp // c

    def body(q_ref, k_ref, v_ref, g_ref, b_ref, s_ref, t_ref, do_ref,
             dq_ref, dk_ref, dv_ref, dg_ref, db_ref, ds_scr):
        @pl.when(pl.program_id(0) == 0)
        def _():
            ds_scr[...] = jnp.zeros_like(ds_scr)

        xs = _gdn_chunk_common(q_ref, k_ref, v_ref, g_ref, b_ref)

        ds_state = [ds_scr[h] for h in range(DN_HEADS)]

        def chain(sub, h):
            e = DN_HEADS * sub + h
            x = {key: (val[e] if isinstance(val, list) else val) for key, val in xs.items()}
            sl = x["sls"]
            qs, kx, vx, beta_, gam, dm = x["q"], x["k"], x["v"], x["beta"], x["gam"], x["dm"]
            kb, vb, kbg, kd, ek = x["kb"], x["vb"], x["kbg"], x["kd"], x["ek"]
            t = t_ref[h, sub]
            s = s_ref[h, sub]
            dsn = ds_state[h]
            dob = do_ref[sl]
            eg_last = jnp.exp(x["g_last"])
            u = _nn(t, vb)
            w = _nn(t, kbg)
            mqk = x["qk"] * dm
            qd = qs * gam
            dqd = _nt(dob, s)
            dkd_pre = _nn(kd, dsn)
            yield
            v_new = u - _nn(w, s)
            dv_new = _tn(mqk, dob) + dkd_pre
            dq = dqd * gam
            dgam = jnp.sum(dqd * qs, axis=1, keepdims=True)
            yield
            ds_state[h] = _tn(qd, dob) + eg_last * dsn - _tn(w, dv_new)
            dmm = jnp.where(x["incl"], _nt(dob, v_new), 0.0)
            dkd = _nt(v_new, dsn)
            dw = -_nt(dv_new, s)
            dvb = _tn(t, dv_new)
            dt = _nt(dv_new, vb)
            yield
            dqk = dmm * dm
            e_mat = dmm * mqk
            dq = dq + _nn(dqk, kx)
            dk = _tn(dqk, qs) + dkd * ek
            e1 = jnp.sum(dkd * kd, axis=1, keepdims=True)
            dgc = -e1
            dg_last = jnp.sum(e1) + eg_last * jnp.sum(s * dsn)
            dt = dt + _nt(dw, kbg)
            dkbg = _tn(t, dw)
            yield
            tdt = _tn(t, dt, hp="3x")
            yield
            da = jnp.where(x["strict"], -_nt(tdt, t, hp="3x"), 0.0)
            yield
            dkk = da * dm
            e_mat = e_mat + da * x["kk"] * dm
            dkb = _nn(dkk, kx) + dkbg * gam
            dk = dk + _tn(dkk, kb)
            dgam = dgam + jnp.sum(dkbg * kb, axis=1, keepdims=True)
            yield
            dk = dk + dkb * beta_
            dbeta = jnp.sum(dkb * kx, axis=1, keepdims=True) + jnp.sum(dvb * vx, axis=1, keepdims=True)
            dv = dvb * beta_
            dgc = dgc + jnp.sum(e_mat, axis=1, keepdims=True) + dgam * gam
            dgc = dgc - jnp.sum(e_mat.T, axis=1, keepdims=True)
            yield
            dg = _cumsum_rows(dgc, reverse=True) + dg_last
            yield
            dq_ref[sl] = dq * (1.0 / math.sqrt(DN_DIM))
            dk_ref[sl] = dk
            dv_ref[sl] = dv
            dg_ref[sl] = dg
            db_ref[sl] = jnp.broadcast_to(dbeta, (c, LANE))

        chains = []
        for sub in reversed(range(GDN_SUB_CHUNKS)):
            new = [chain(sub, h) for h in range(DN_HEADS)]
            for _ in range(3):
                for ch in new:
                    next(ch)
            chains += new
        _interleave(chains)
        for h in range(DN_HEADS):
            ds_scr[h] = ds_state[h]

    nblk = nch // GDN_SUB_CHUNKS
    sub = GDN_SUB_CHUNKS
    rb = lambda n: (nblk - 1 - n, 0)
    hs = lambda n: (0, nblk - 1 - n, 0, 0)
    return pl.pallas_call(
        body, name="gdn_bwd", grid=(nblk,),
        in_specs=[pl.BlockSpec((sub * c, DN_WIDTH), rb)] * 5
        + [pl.BlockSpec((DN_HEADS, sub, DN_DIM, DN_DIM), hs), pl.BlockSpec((DN_HEADS, sub, c, c), hs),
           pl.BlockSpec((sub * c, DN_WIDTH), rb)],
        out_specs=[pl.BlockSpec((sub * c, DN_WIDTH), rb)] * 5,
        out_shape=[jax.ShapeDtypeStruct((tp, DN_WIDTH), F32)] * 5,
        scratch_shapes=[pltpu.VMEM((DN_HEADS, DN_DIM, DN_DIM), F32)],
        compiler_params=pltpu.CompilerParams(dimension_semantics=("arbitrary",), vmem_limit_bytes=VMEM_LIMIT),
    )(q, k, v, g, beta, s_all, t_all, do)


def _silu_parts(x):
    s = _sigmoid(x)
    return x * s, s * (1.0 + x * (1.0 - s))


def _f_rms_cast(i, x, w):
    y, _ = _rms_fwd(x, w, x.shape[1])
    return (y,), ()


def _f_rms_bwd_add(i, x, dy, dres, w, *, mask_pad):
    dx, dwr = _rms_bwd(x, w, dy, x.shape[1])
    out = dres + dx
    if mask_pad:
        out = jnp.where(_row_ids(i, x.shape[0]) >= PAD, out, 0.0)
    return (out,), (_rowsum(dwr),)


def _f_lat_norm(i, ql, kvl, qw, kvw):
    return (_rms_fwd(ql, qw, Q_LORA)[0], _rms_fwd(kvl, kvw, KV_LORA)[0]), ()


def _f_lat_norm_bwd(i, ql, kvl, dqn, dkvn, qw, kvw):
    dq, dqw = _rms_bwd(ql, qw, dqn, Q_LORA)
    dk, dkw = _rms_bwd(kvl, kvw, dkvn, KV_LORA)
    return (dq, dk), (_rowsum(dqw), _rowsum(dkw))


def _rope(x, cos, sin_s):
    return x * cos + _swap_halves(x) * sin_s


def _rope_t(dy, cos, sin_s):
    return dy * cos + _swap_halves(dy * sin_s)


def _f_mla_qk(i, qf, kvf, kpe, cos, sin_s, qw, kw):
    qs, ks, vs = [], [], []
    for h in range(MLA_HEADS):
        qn, _ = _rms_fwd(qf[:, HP * h:HP * (h + 1)], qw, QK_HEAD)
        qs += [qn[:, :QK_NOPE], _rope(qn[:, QK_NOPE:], cos, sin_s)]
        kh = jnp.concatenate([kvf[:, HP * h:HP * h + QK_NOPE], kpe], axis=1)
        kn, _ = _rms_fwd(kh, kw, QK_HEAD)
        ks += [kn[:, :QK_NOPE], _rope(kn[:, QK_NOPE:], cos, sin_s)]
        vs.append(kvf[:, HP * h + QK_NOPE:HP * (h + 1)])
    return (jnp.concatenate(qs, axis=1), jnp.concatenate(ks, axis=1), jnp.concatenate(vs, axis=1)), ()


def _f_mla_qk_bwd(i, qf, kvf, kpe, cos, sin_s, dq, dk, dv, qw, kw):
    dqf, dkvf = [], []
    dkpe = None
    dqw = None
    dkw = None
    for h in range(MLA_HEADS):
        dqh = dq[:, HP * h:HP * (h + 1)]
        dqn = jnp.concatenate([dqh[:, :QK_NOPE], _rope_t(dqh[:, QK_NOPE:], cos, sin_s)], axis=1)
        dx, dwr = _rms_bwd(qf[:, HP * h:HP * (h + 1)], qw, dqn, QK_HEAD)
        dqf.append(dx)
        dqw = _rowsum(dwr) if dqw is None else dqw + _rowsum(dwr)
        dkh = dk[:, HP * h:HP * (h + 1)]
        dkn = jnp.concatenate([dkh[:, :QK_NOPE], _rope_t(dkh[:, QK_NOPE:], cos, sin_s)], axis=1)
        kh = jnp.concatenate([kvf[:, HP * h:HP * h + QK_NOPE], kpe], axis=1)
        dx, dwr = _rms_bwd(kh, kw, dkn, QK_HEAD)
        dkvf += [dx[:, :QK_NOPE], dv[:, V_HEAD * h:V_HEAD * (h + 1)]]
        dkpe = dx[:, QK_NOPE:] if dkpe is None else dkpe + dx[:, QK_NOPE:]
        dkw = _rowsum(dwr) if dkw is None else dkw + _rowsum(dwr)
    return (jnp.concatenate(dqf, axis=1), jnp.concatenate(dkvf, axis=1), dkpe), (dqw, dkw)


def _gdn_act(i, x, halo, w8):
    tm = x.shape[0]
    halo = jnp.where(i > 0, halo, 0.0)
    c = _conv_fwd(x, halo, w8, DN_CONV)
    act, dact = _silu_parts(c)
    return act, dact


def _spread_heads(ab):
    tm = ab.shape[0]
    return jnp.concatenate([jnp.broadcast_to(ab[:, h:h + 1], (tm, DN_DIM)) for h in range(2 * DN_HEADS)], axis=1)


def _gather_heads(x):
    tm = x.shape[0]
    lane = lax.broadcasted_iota(jnp.int32, (tm, LANE), 1)
    out = jnp.zeros((tm, LANE), F32)
    for h in range(2 * DN_HEADS):
        out = out + jnp.where(lane == h, x[:, DN_DIM * h:DN_DIM * h + 1], 0.0)
    return out


def _f_gdn_prep(i, x, halo, ab, w8, alog, dtb):
    tm = x.shape[0]
    act, _ = _gdn_act(i, x, halo, w8)
    outs = []
    for part in range(2):
        for h in range(DN_HEADS):
            t = act[:, DN_WIDTH * part + DN_DIM * h:DN_WIDTH * part + DN_DIM * (h + 1)]
            outs.append(t * lax.rsqrt(jnp.sum(t * t, axis=-1, keepdims=True) + EPS))
    q = jnp.concatenate(outs[:DN_HEADS], axis=1)
    k = jnp.concatenate(outs[DN_HEADS:], axis=1)
    v = act[:, 2 * DN_WIDTH:]
    abb = _spread_heads(ab)
    valid = _row_ids(i, tm) >= PAD
    g = jnp.where(valid, -jnp.exp(alog) * _softplus(abb[:, :DN_WIDTH] + dtb), 0.0)
    beta = jnp.where(valid, _sigmoid(abb[:, DN_WIDTH:]), 0.0)
    return (q, k, v, g, beta), ()


def _f_gdn_prep_bwd(i, x, x_prev, x_next, ab, dq, dq_next, dk, dk_next, dv, dv_next, dg, dbeta,
                    w8, alog, dtb, *, nt):
    tm = x.shape[0]
    x_prev = jnp.where(i > 0, x_prev, 0.0)
    more = i < nt - 1
    ext = lambda t, t_next: jnp.concatenate([t, jnp.where(more, t_next, 0.0)], axis=0)
    c = _conv_fwd(jnp.concatenate([x, x_next], axis=0), x_prev, w8, DN_CONV)
    act, dact = _silu_parts(c)
    douts = []
    for part, dd in enumerate((ext(dq, dq_next), ext(dk, dk_next))):
        for h in range(DN_HEADS):
            t = act[:, DN_WIDTH * part + DN_DIM * h:DN_WIDTH * part + DN_DIM * (h + 1)]
            r = lax.rsqrt(jnp.sum(t * t, axis=-1, keepdims=True) + EPS)
            y = t * r
            dy = dd[:, DN_DIM * h:DN_DIM * (h + 1)]
            douts.append(r * (dy - y * jnp.sum(dy * y, axis=-1, keepdims=True)))
    douts.append(ext(dv, dv_next))
    dc = jnp.concatenate(douts, axis=1) * dact
    dqkv = _conv_bwd_x(dc[:tm], dc[tm:], w8, DN_CONV)
    dconv_w = _conv_bwd_w(dc[:tm], x, x_prev, DN_CONV)
    abb = _spread_heads(ab)
    valid = _row_ids(i, tm) >= PAD
    pre = abb[:, :DN_WIDTH] + dtb
    ea = jnp.exp(alog)
    g = -ea * _softplus(pre)
    dg = jnp.where(valid, dg, 0.0)
    dbeta = jnp.where(valid, dbeta, 0.0)
    da = dg * (-ea) * _sigmoid(pre)
    beta = _sigmoid(abb[:, DN_WIDTH:])
    db = dbeta * beta * (1.0 - beta)
    dab = _gather_heads(jnp.concatenate([da, db], axis=1))
    return (dqkv, dab), (dconv_w, _rowsum(dg * g), _rowsum(da))


def _f_conv_bwd(i, dy, dy_next, x, x_prev, w8, *, width, nt):
    dy_next = jnp.where(i < nt - 1, dy_next, 0.0)
    x_prev = jnp.where(i > 0, x_prev, 0.0)
    return (_conv_bwd_x(dy, dy_next, w8, width),), (_conv_bwd_w(dy, x, x_prev, width),)


def _f_mix(i, o_mla, o_dn, z, w_mla, w_dn):
    tm = o_mla.shape[0]
    valid = _row_ids(i, tm) >= PAD
    outs = []
    for h in range(MLA_HEADS):
        y, _ = _rms_fwd(o_mla[:, V_HEAD * h:V_HEAD * (h + 1)], w_mla, V_HEAD)
        outs.append(jnp.where(valid, y, 0.0))
    for h in range(DN_HEADS):
        y, _ = _rms_fwd(o_dn[:, DN_DIM * h:DN_DIM * (h + 1)], w_dn, DN_DIM)
        outs.append(y * _silu_parts(z[:, DN_DIM * h:DN_DIM * (h + 1)])[0])
    return (jnp.concatenate(outs, axis=1),), ()


def _f_mix_bwd(i, o_mla, o_dn, z, dy_mla, dy_dn, w_mla, w_dn):
    tm = o_mla.shape[0]
    valid = _row_ids(i, tm) >= PAD
    d_mla, d_dn, d_z = [], [], []
    dw_mla = None
    dw_dn = None
    for h in range(MLA_HEADS):
        sl = slice(V_HEAD * h, V_HEAD * (h + 1))
        dx, dwr = _rms_bwd(o_mla[:, sl], w_mla, jnp.where(valid, dy_mla[:, sl], 0.0), V_HEAD)
        d_mla.append(dx)
        dw_mla = _rowsum(dwr) if dw_mla is None else dw_mla + _rowsum(dwr)
    for h in range(DN_HEADS):
        sl = slice(DN_DIM * h, DN_DIM * (h + 1))
        y, _ = _rms_fwd(o_dn[:, sl], w_dn, DN_DIM)
        sz, dsz = _silu_parts(z[:, sl])
        d_z.append(dy_dn[:, sl] * y * dsz)
        dx, dwr = _rms_bwd(o_dn[:, sl], w_dn, dy_dn[:, sl] * sz, DN_DIM)
        d_dn.append(dx)
        dw_dn = _rowsum(dwr) if dw_dn is None else dw_dn + _rowsum(dwr)
    return ((jnp.concatenate(d_mla, axis=1), jnp.concatenate(d_dn, axis=1), jnp.concatenate(d_z, axis=1)),
            (dw_mla, dw_dn))


def _f_ffn_act(i, gate_pre, halo, up, w8, b):
    halo = jnp.where(i > 0, halo, 0.0)
    gate = _conv_fwd(gate_pre, halo, w8, FFN_CONV) + b
    return (_silu_parts(gate)[0] * up,), ()


def _f_ffn_act_bwd(i, gp, gp_prev, gp_next, up, up_next, dact, dact_next, w8, b, *, nt):
    tm = gp.shape[0]
    gp_prev = jnp.where(i > 0, gp_prev, 0.0)
    dact_next = jnp.where(i < nt - 1, dact_next, 0.0)
    cat = lambda t, t_next: jnp.concatenate([t, t_next], axis=0)
    gate = _conv_fwd(cat(gp, gp_next), gp_prev, w8, FFN_CONV) + b
    sg, dsg = _silu_parts(gate)
    dact_e = cat(dact, dact_next)
    dgate = dact_e * cat(up, up_next) * dsg
    dgate_pre = _conv_bwd_x(dgate[:tm], dgate[tm:], w8, FFN_CONV)
    dup = dact * sg[:tm]
    return (dgate_pre, dup), (_conv_bwd_w(dgate[:tm], gp, gp_prev, FFN_CONV), _rowsum(dgate[:tm]))


def _f_loss(i, h3, tgt):
    tm = h3.shape[0]
    diff = jnp.where(_row_ids(i, tm) >= ROW0, h3 - tgt, 0.0)
    part = 0.5 * jnp.sum(diff * diff) * (1.0 / D_MODEL)
    return (diff * (1.0 / D_MODEL),), (jnp.full((1, LANE), part, F32),)


def _after(fn):
    return lambda i, *a: fn(i, *a[:-1])


def _local_step(h0, tgt, w, token, late_weights, grads_ready):
    tp = h0.shape[0]
    nt = tp // TM
    bf = (D_MODEL, _MXU)
    proj, u = _norm_mm("in_proj", h0, w["attn_norm_w"], w["w_in"], after=token)
    p_qkv = lambda kind="cur": _In(proj, 3 * DN_WIDTH, 0, kind)
    p_z = _In(proj, DN_WIDTH, C_Z // DN_WIDTH)
    p_ql = _In(proj, Q_LORA, C_QL // Q_LORA)
    p_kvl = _In(proj, KV_LORA, C_KVL // KV_LORA)
    p_kpe = _In(proj, LANE, C_KPE // LANE)
    p_ab = _In(proj, LANE, C_AB // LANE)
    cos, sin_s = _In(w["cos"]), _In(w["sin_s"])

    qf, qn = _norm_mm("mla_q_b", proj, w["q_a_norm_w"], w["w_q_b"], "nt", C_QL // Q_LORA)
    kvf, kvn = _norm_mm("mla_kv_b", proj, w["kv_a_norm_w"], w["w_kv_b"], "nn", C_KVL // KV_LORA)
    qk_w = [w["q_norm_w"], w["k_norm_w"]]
    q, k, v = _rows("mla_qk", _f_mla_qk, [_In(qf), _In(kvf), p_kpe, cos, sin_s], qk_w,
                    [(MLA_HEADS * HP, _MXU), (MLA_HEADS * HP, _MXU), (MLA_HEADS * V_HEAD, _MXU)])
    o_mla = _attn_fwd(q, k, v)

    dn_w = [w["dn_conv_w"], w["alog_b"], w["dtb_b"]]
    gq, gk, gv, gg, gb = _rows("gdn_prep", _f_gdn_prep, [p_qkv(), p_qkv("prev"), p_ab], dn_w,
                               [(DN_WIDTH, F32)] * 5)
    o_dn, s_all, t_all = _gdn_fwd(gq, gk, gv, gg, gb)

    out_w = [w["mla_out_norm_w"], w["dn_out_norm_w"]]
    w = dict(w, **late_weights(o_dn))
    h2, mixed = _pro_mm("mix_out_proj", lambda i, *t: _f_mix(i, *t)[0][0], [_In(o_mla), _In(o_dn), p_z], out_w,
                        D_MODEL, w["w_out"], h0)

    ffn_w = [w["ffn_conv_w"], w["ffn_conv_b"]]
    hn, gate_pre, up, act = _ffn_in(h2, w["ffn_norm_w"], w["w_gate"], w["w_up"], *ffn_w)
    dh3, loss = _mm_rows("ffn_down_loss", act, w["w_down"], "nn", lambda i, y, r, t: _f_loss(i, r + y, t),
                         [_In(h2), _In(tgt)], [], [(D_MODEL, F32)], [(1, LANE)])

    g = {}
    dact = _mm("ffn_down_dx", dh3, w["w_down"], "nt")
    g["w_down"] = _mm("ffn_down_dw", act, dh3, "tn", out_dtype=_MXU)
    dgate_pre, dup, g["ffn_conv_w"], g["ffn_conv_b"] = _rows(
        "ffn_act_bwd", functools.partial(_f_ffn_act_bwd, nt=nt),
        [_In(gate_pre), _In(gate_pre, kind="prev"), _In(gate_pre, kind="next"), _In(up), _In(up, kind="next"),
         _In(dact), _In(dact, kind="next")], ffn_w,
        [(D_FF, _MXU), (D_FF, _MXU)], [(8, D_FF), (1, D_FF)])
    g["w_gate"], g["w_up"] = _mm_tn2("ffn_gate_up_dw", dgate_pre, dup, hn, out_dtype=_MXU)
    tok = grads_ready(g, ("w_down", "w_gate", "w_up"))
    dh2, g["ffn_norm_w"] = _mm_rows(
        "ffn_gate_up_dx_rms", [dgate_pre, dup], [w["w_gate"], w["w_up"]], "nn",
        lambda i, dy, x, dres, nw, _tok: _f_rms_bwd_add(i, x, dy, dres, nw, mask_pad=True),
        [_In(h2), _In(dh3)], [w["ffn_norm_w"], tok], [(D_MODEL, F32)], [(1, D_MODEL)])

    g["w_out"] = _mm("out_proj_dw", mixed, dh2, "tn", out_dtype=_MXU)
    half = MLA_HEADS * V_HEAD
    do_mla, do_dn, dz, g["mla_out_norm_w"], g["dn_out_norm_w"] = _mm_rows(
        "out_proj_dx_mix", dh2, w["w_out"], "nt",
        lambda i, dm, om, od, z, wm, wd: _f_mix_bwd(i, om, od, z, dm[:, :half], dm[:, half:], wm, wd),
        [_In(o_mla), _In(o_dn), p_z], out_w,
        [(half, F32), (DN_WIDTH, F32), (DN_WIDTH, _MXU)], [(1, V_HEAD), (1, DN_DIM)])

    dq, dk, dv = _attn_bwd(q, k, v, do_mla)
    dqf, dkvf, dkpe, g["q_norm_w"], g["k_norm_w"] = _rows(
        "mla_qk_bwd", _f_mla_qk_bwd, [_In(qf), _In(kvf), p_kpe, cos, sin_s, _In(dq), _In(dk), _In(dv)], qk_w,
        [(MLA_HEADS * HP, _MXU), (MLA_HEADS * HP, _MXU), (LANE, _MXU)], [(1, HP), (1, HP)])
    g["w_q_b"] = _mm("mla_q_b_dw", dqf, qn, "tn")
    g["w_kv_b"] = _mm("mla_kv_b_dw", kvn, dkvf, "tn")
    tok = grads_ready(g, ("w_out", "w_q_b", "w_kv_b"))

    def lat_bwd(n):
        def fn(i, dy, x, nw, _tok):
            dx, dwr = _rms_bwd(x, nw, dy, n)
            return (dx,), (_rowsum(dwr),)
        return fn

    dql, g["q_a_norm_w"] = _mm_rows("mla_q_b_dx", dqf, w["w_q_b"], "nn", lat_bwd(Q_LORA), [p_ql],
                                    [w["q_a_norm_w"], tok], [(Q_LORA, _MXU)], [(1, Q_LORA)])
    dkvl, g["kv_a_norm_w"] = _mm_rows("mla_kv_b_dx", dkvf, w["w_kv_b"], "nt", lat_bwd(KV_LORA), [p_kvl],
                                      [w["kv_a_norm_w"], tok], [(KV_LORA, _MXU)], [(1, KV_LORA)])

    dgq, dgk, dgv, dgg, dgb = _gdn_bwd(gq, gk, gv, gg, gb, s_all, t_all, do_dn)
    nxt = lambda a: _In(a, kind="next")
    dqkv, dab, g["dn_conv_w"], g["alog_b"], g["dtb_b"] = _rows(
        "gdn_prep_bwd", functools.partial(_f_gdn_prep_bwd, nt=nt),
        [p_qkv(), p_qkv("prev"), p_qkv("next"), p_ab, _In(dgq), nxt(dgq), _In(dgk), nxt(dgk), _In(dgv), nxt(dgv),
         _In(dgg), _In(dgb)], dn_w,
        [(3 * DN_WIDTH, _MXU), (LANE, _MXU)], [(8, 3 * DN_WIDTH), (1, DN_WIDTH), (1, DN_WIDTH)])

    dproj = jnp.concatenate([dqkv, dz, dql, dkvl, dkpe, dab], axis=1)
    g["w_in"] = _mm("in_proj_dw", dproj, u, "tn", out_dtype=_MXU)
    tok = grads_ready(g, ("w_in",))
    dh0, g["attn_norm_w"] = _mm_rows(
        "in_proj_dx_rms", dproj, w["w_in"], "nn",
        lambda i, du, x, dres, nw, _tok: _f_rms_bwd_add(i, x, du, dres, nw, mask_pad=False),
        [_In(h0), _In(dh2)], [w["attn_norm_w"], tok], [(D_MODEL, F32)], [(1, D_MODEL)])
    return loss, dh0, g


def _w_in_to_padded(w):
    c1, c2, c3 = Q_LORA, Q_LORA + KV_LORA, Q_LORA + KV_LORA + QK_ROPE
    c4 = c3 + 3 * DN_WIDTH
    c5 = c4 + DN_WIDTH
    z = lambda n: jnp.zeros((n, w.shape[1]), w.dtype)
    return jnp.concatenate([w[c3:c4], w[c4:c5], w[:c1], w[c1:c2], w[c2:c3], z(LANE - QK_ROPE),
                            w[c5:], z(LANE - 2 * DN_HEADS)], axis=0)


def _w_in_from_padded(g):
    return jnp.concatenate([g[C_QL:C_QL + Q_LORA], g[C_KVL:C_KVL + KV_LORA], g[C_KPE:C_KPE + QK_ROPE],
                            g[:C_Z + DN_WIDTH], g[C_AB:C_AB + 2 * DN_HEADS]], axis=0)


def _w_q_b_to_padded(w):
    r = w.shape[1]
    w = w.reshape(MLA_HEADS, QK_HEAD, r)
    return jnp.pad(w, ((0, 0), (0, HP - QK_HEAD), (0, 0))).reshape(MLA_HEADS * HP, r)


def _w_q_b_from_padded(g):
    r = g.shape[1]
    return g.reshape(MLA_HEADS, HP, r)[:, :QK_HEAD].reshape(MLA_HEADS * QK_HEAD, r)


def _pad_rows8(w):
    return jnp.pad(w, ((0, 8 - w.shape[0]), (0, 0)))


def _prepare(full, tp):
    w = {}
    mx = lambda a: a.astype(_MXU)
    w["attn_norm_w"] = full["attn_norm_w"]
    w["w_in"] = mx(_w_in_to_padded(full["w_in"]))
    w["q_a_norm_w"] = full["q_a_norm_w"]
    w["kv_a_norm_w"] = full["kv_a_norm_w"]
    w["w_q_b"] = mx(_w_q_b_to_padded(full["w_q_b"]))
    w["w_kv_b"] = mx(full["w_kv_b"])
    w["q_norm_w"] = jnp.pad(full["q_norm_w"], ((0, 0), (0, HP - QK_HEAD)))
    w["k_norm_w"] = jnp.pad(full["k_norm_w"], ((0, 0), (0, HP - QK_HEAD)))
    w["mla_out_norm_w"] = full["mla_out_norm_w"]
    w["dn_out_norm_w"] = full["dn_out_norm_w"]
    w["dn_conv_w"] = _pad_rows8(full["dn_conv_w"])
    w["alog_b"] = jnp.repeat(full["dn_A_log"], DN_DIM, axis=1)
    w["dtb_b"] = jnp.repeat(full["dn_dt_bias"], DN_DIM, axis=1)
    w["ffn_norm_w"] = full["ffn_norm_w"]
    w["ffn_conv_w"] = _pad_rows8(full["ffn_conv_w"])
    w["ffn_conv_b"] = full["ffn_conv_b"]
    for n in _LATE:
        if n in full:
            w[n] = mx(full[n])
    half = QK_ROPE // 2
    inv = ROPE_THETA ** (-jnp.arange(half, dtype=F32) / half)
    ang = (jnp.arange(tp, dtype=jnp.int32) - PAD).astype(F32)[:, None] * inv[None, :]
    zc = jnp.zeros((tp, LANE - QK_ROPE), F32)
    w["cos"] = jnp.concatenate([jnp.cos(ang), jnp.cos(ang), zc], axis=1)
    w["sin_s"] = jnp.concatenate([-jnp.sin(ang), jnp.sin(ang), zc], axis=1)
    return w


def _grads_to_natural(g):
    convert = {
        "w_in": ("w_in", _w_in_from_padded),
        "w_q_b": ("w_q_b", _w_q_b_from_padded),
        "q_norm_w": ("q_norm_w", lambda a: a[:, :QK_HEAD]),
        "k_norm_w": ("k_norm_w", lambda a: a[:, :QK_HEAD]),
        "dn_conv_w": ("dn_conv_w", lambda a: a[:DN_CONV]),
        "ffn_conv_w": ("ffn_conv_w", lambda a: a[:FFN_CONV]),
        "alog_b": ("dn_A_log", lambda a: a[:, ::DN_DIM]),
        "dtb_b": ("dn_dt_bias", lambda a: a[:, ::DN_DIM]),
    }
    n = {}
    for key, a in g.items():
        name, fn = convert.get(key, (key, lambda t: t))
        n[name] = fn(a)
    return n


_MESH = pl.DeviceIdType.MESH
_ANY = pl.BlockSpec(memory_space=pl.ANY)
_CHIP_FLIPS = ((1, 0), (0, 1), (1, 1))


def _me():
    return lax.axis_index("x"), lax.axis_index("y"), lax.axis_index("c")


def _all_gather(name, blk):
    def body(x_ref, out_ref, send_sems, recv_sems, local_sem):
        x, y, c = _me()
        me, sib = (x, y, c), (x, y, 1 - c)
        chips = [(x ^ fx, y ^ fy) for fx, fy in _CHIP_FLIPS]

        def slot(p):
            return out_ref.at[4 * p[0] + 2 * p[1] + p[2]]

        def copy(k, block, to, src=None):
            return pltpu.make_async_remote_copy(
                src_ref=slot(block) if src is None else src, dst_ref=slot(block),
                send_sem=send_sems.at[k], recv_sem=recv_sems.at[k], device_id=to, device_id_type=_MESH)

        mine = pltpu.make_async_copy(x_ref, slot(me), local_sem)
        mine.start()
        first = [copy(0, me, sib, src=x_ref)]
        first += [copy(1 + j, me, (*chip, c), src=x_ref) for j, chip in enumerate(chips)]
        for cp in first:
            cp.start()
        passed = [copy(4 + j, (*chip, c), sib) for j, chip in enumerate(chips)]
        for j, chip in enumerate(chips):
            copy(1 + j, (*chip, c), me).wait_recv()
            passed[j].start()
        copy(0, sib, me).wait_recv()
        for j, chip in enumerate(chips):
            copy(4 + j, (*chip, 1 - c), me).wait_recv()
        for cp in first + passed:
            cp.wait_send()
        mine.wait()

    return pl.pallas_call(
        body, name=name, in_specs=[_ANY], out_specs=_ANY,
        out_shape=jax.ShapeDtypeStruct((N_DEV,) + blk.shape, blk.dtype),
        scratch_shapes=[pltpu.SemaphoreType.DMA((7,)), pltpu.SemaphoreType.DMA((7,)), pltpu.SemaphoreType.DMA],
    )(blk)


def _rs_sibling(name, gb):
    def body(g_ref, out_ref, send_sems, recv_sems):
        x, y, c = _me()
        cps = []
        for j in range(4):
            cp = pltpu.make_async_remote_copy(
                src_ref=g_ref.at[2 * j + (1 - c)], dst_ref=out_ref.at[j], send_sem=send_sems.at[j],
                recv_sem=recv_sems.at[j], device_id=(x, y, 1 - c), device_id_type=_MESH)
            cp.start()
            cps.append(cp)
        for cp in cps:
            cp.wait()

    return pl.pallas_call(
        body, name=name, in_specs=[_ANY], out_specs=_ANY,
        out_shape=jax.ShapeDtypeStruct((4,) + gb.shape[1:], gb.dtype),
        scratch_shapes=[pltpu.SemaphoreType.DMA((4,)), pltpu.SemaphoreType.DMA((4,))],
    )(gb)


def _rs_chips(name, s1):
    def body(s_ref, out_ref, send_sems, recv_sems):
        x, y, c = _me()
        cps = []
        for k, (fx, fy) in enumerate(_CHIP_FLIPS):
            px, py = x ^ fx, y ^ fy
            cp = pltpu.make_async_remote_copy(
                src_ref=s_ref.at[2 * px + py], dst_ref=out_ref.at[k], send_sem=send_sems.at[k],
                recv_sem=recv_sems.at[k], device_id=(px, py, c), device_id_type=_MESH)
            cp.start()
            cps.append(cp)
        for cp in cps:
            cp.wait()

    return pl.pallas_call(
        body, name=name, in_specs=[_ANY], out_specs=_ANY,
        out_shape=jax.ShapeDtypeStruct((3,) + s1.shape[1:], s1.dtype),
        scratch_shapes=[pltpu.SemaphoreType.DMA((3,)), pltpu.SemaphoreType.DMA((3,))],
    )(s1)


def _row_tile(r):
    divs = [d for d in range(16, min(r, 512) + 1, 16) if r % d == 0]
    return divs[-1] if divs else r


def _pair_sum(name, gb, recv):
    _, r, cols = gb.shape
    tm = _row_tile(r)
    c = lax.axis_index("c").astype(jnp.int32).reshape(1)

    def body(c_ref, a_ref, b_ref, o_ref, ob_ref):
        s = a_ref[...] + b_ref[...]
        o_ref[...] = s
        ob_ref[...] = s.astype(BF16)

    blk = pl.BlockSpec((1, tm, cols), lambda j, i, c_ref: (j, i, 0))
    return pl.pallas_call(
        body, name=name,
        grid_spec=pltpu.PrefetchScalarGridSpec(
            num_scalar_prefetch=1, grid=(4, r // tm),
            in_specs=[pl.BlockSpec((1, tm, cols), lambda j, i, c_ref: (2 * j + c_ref[0], i, 0)), blk],
            out_specs=[blk, blk]),
        out_shape=[jax.ShapeDtypeStruct((4, r, cols), F32), jax.ShapeDtypeStruct((4, r, cols), BF16)],
        compiler_params=pltpu.CompilerParams(dimension_semantics=("parallel", "parallel")),
    )(c, gb, recv)


def _adam_math(g, w, m, v):
    m_new = ADAM_B1 * m + (1.0 - ADAM_B1) * g
    v_new = ADAM_B2 * v + (1.0 - ADAM_B2) * (g * g)
    m_hat = m_new / (1.0 - ADAM_B1 ** ADAM_STEP)
    v_hat = v_new / (1.0 - ADAM_B2 ** ADAM_STEP)
    return -ADAM_LR * (m_hat / (jnp.sqrt(v_hat) + ADAM_EPS) + ADAM_WD * w), m_new, v_new


def _adam_vectors(name, row, items, ws, ms, vs):
    k = len(items)

    def body(row_ref, *refs):
        w_refs, m_refs, v_refs = refs[:k], refs[k:2 * k], refs[2 * k:3 * k]
        outs = refs[3 * k:]
        for idx, (off, n, per_head) in enumerate(items):
            if per_head:
                spread = row_ref[:, off:off + DN_WIDTH]
                lane = lax.broadcasted_iota(jnp.int32, (1, LANE), 1)
                g = jnp.zeros((1, LANE), F32)
                for h in range(DN_HEADS):
                    g = g + jnp.where(lane == h, spread[:, DN_DIM * h:DN_DIM * h + 1], 0.0)
            else:
                g = row_ref[:, off:off + n]
            d, m_new, v_new = _adam_math(g, w_refs[idx][...], m_refs[idx][...], v_refs[idx][...])
            for kind, val in enumerate((g, d, m_new, v_new)):
                outs[kind * k + idx][...] = val

    shapes = [jax.ShapeDtypeStruct((1, n), F32) for _, n, _ in items]
    res = pl.pallas_call(body, name=name, out_shape=shapes * 4)(row, *ws, *ms, *vs)
    return [list(res[kind * k:(kind + 1) * k]) for kind in range(4)]


def _sum_parts(name, parts):
    _, r, cols = parts[0][0].shape
    tm = _row_tile(r)
    idx = jnp.stack([jnp.asarray(s, jnp.int32) for _, s in parts])
    n = len(parts)

    def body(idx_ref, *refs):
        g = refs[0][0].astype(F32)
        for p_ref in refs[1:n]:
            g = g + p_ref[0].astype(F32)
        refs[n][...] = g

    return pl.pallas_call(
        body, name=name,
        grid_spec=pltpu.PrefetchScalarGridSpec(
            num_scalar_prefetch=1, grid=(r // tm,),
            in_specs=[pl.BlockSpec((1, tm, cols), lambda i, idx_ref, p=p: (idx_ref[p], i, 0)) for p in range(n)],
            out_specs=pl.BlockSpec((tm, cols), lambda i, idx_ref: (i, 0))),
        out_shape=jax.ShapeDtypeStruct((r, cols), F32),
        compiler_params=pltpu.CompilerParams(dimension_semantics=("parallel",)),
    )(idx, *[a for a, _ in parts])


def _adam(name, parts, w, m, v):
    r, cols = w.shape
    tm = _row_tile(r)
    idx = jnp.stack([jnp.asarray(s, jnp.int32) for _, s in parts])
    n = len(parts)

    def body(idx_ref, *refs):
        g = refs[0][0].astype(F32)
        for p_ref in refs[1:n]:
            g = g + p_ref[0].astype(F32)
        w_ref, m_ref, v_ref, g_out, d_out, m_out, v_out = refs[n:]
        g_out[...] = g
        d_out[...], m_out[...], v_out[...] = _adam_math(g, w_ref[...], m_ref[...], v_ref[...])

    part_specs = [pl.BlockSpec((1, tm, cols), lambda i, idx_ref, p=p: (idx_ref[p], i, 0)) for p in range(n)]
    flat = pl.BlockSpec((tm, cols), lambda i, idx_ref: (i, 0))
    return pl.pallas_call(
        body, name=name,
        grid_spec=pltpu.PrefetchScalarGridSpec(
            num_scalar_prefetch=1, grid=(r // tm,), in_specs=part_specs + [flat] * 3, out_specs=[flat] * 4),
        out_shape=[jax.ShapeDtypeStruct((r, cols), F32)] * 4,
        compiler_params=pltpu.CompilerParams(dimension_semantics=("parallel",)),
    )(idx, *[a for a, _ in parts], w, m, v)


def _all_gather_many(name, blks):
    n = len(blks)

    def body(*refs):
        x_refs, out_refs = refs[:n], refs[n:2 * n]
        send_sems, recv_sems, local_sems = refs[2 * n:]
        x, y, c = _me()
        me, sib = (x, y, c), (x, y, 1 - c)
        chips = [(x ^ fx, y ^ fy) for fx, fy in _CHIP_FLIPS]

        def slot(a, p):
            return out_refs[a].at[4 * p[0] + 2 * p[1] + p[2]]

        def copy(a, k, block, to, src=None):
            return pltpu.make_async_remote_copy(
                src_ref=slot(a, block) if src is None else src, dst_ref=slot(a, block),
                send_sem=send_sems.at[7 * a + k], recv_sem=recv_sems.at[7 * a + k], device_id=to,
                device_id_type=_MESH)

        mine = [pltpu.make_async_copy(x_refs[a], slot(a, me), local_sems.at[a]) for a in range(n)]
        first = []
        for a in range(n):
            mine[a].start()
            first.append(copy(a, 0, me, sib, src=x_refs[a]))
            first += [copy(a, 1 + j, me, (*chip, c), src=x_refs[a]) for j, chip in enumerate(chips)]
        for cp in first:
            cp.start()
        passed = []
        for j, chip in enumerate(chips):
            for a in range(n):
                copy(a, 1 + j, (*chip, c), me).wait_recv()
                cp = copy(a, 4 + j, (*chip, c), sib)
                cp.start()
                passed.append(cp)
        for a in range(n):
            copy(a, 0, sib, me).wait_recv()
            for j, chip in enumerate(chips):
                copy(a, 4 + j, (*chip, 1 - c), me).wait_recv()
        for cp in first + passed:
            cp.wait_send()
        for cp in mine:
            cp.wait()

    return pl.pallas_call(
        body, name=name, in_specs=[_ANY] * n, out_specs=[_ANY] * n,
        out_shape=[jax.ShapeDtypeStruct((N_DEV,) + b.shape, b.dtype) for b in blks],
        scratch_shapes=[pltpu.SemaphoreType.DMA((7 * n,)), pltpu.SemaphoreType.DMA((7 * n,)),
                        pltpu.SemaphoreType.DMA((n,))],
    )(*blks)


def _rs_sibling_many(name, gbs):
    n = len(gbs)

    def body(*refs):
        g_refs, out_refs = refs[:n], refs[n:2 * n]
        send_sems, recv_sems = refs[2 * n:]
        x, y, c = _me()
        cps = []
        for a in range(n):
            for j in range(4):
                cp = pltpu.make_async_remote_copy(
                    src_ref=g_refs[a].at[2 * j + (1 - c)], dst_ref=out_refs[a].at[j],
                    send_sem=send_sems.at[4 * a + j], recv_sem=recv_sems.at[4 * a + j],
                    device_id=(x, y, 1 - c), device_id_type=_MESH)
                cp.start()
                cps.append(cp)
        for cp in cps:
            cp.wait()

    return pl.pallas_call(
        body, name=name, in_specs=[_ANY] * n, out_specs=[_ANY] * n,
        out_shape=[jax.ShapeDtypeStruct((4,) + g.shape[1:], g.dtype) for g in gbs],
        scratch_shapes=[pltpu.SemaphoreType.DMA((4 * n,)), pltpu.SemaphoreType.DMA((4 * n,))],
    )(*gbs)


def _rs_chips_many(name, s1s):
    n = len(s1s)

    def body(*refs):
        s_refs, out_refs = refs[:n], refs[n:2 * n]
        send_sems, recv_sems = refs[2 * n:]
        x, y, c = _me()
        cps = []
        for a in range(n):
            for k, (fx, fy) in enumerate(_CHIP_FLIPS):
                px, py = x ^ fx, y ^ fy
                cp = pltpu.make_async_remote_copy(
                    src_ref=s_refs[a].at[2 * px + py], dst_ref=out_refs[a].at[k],
                    send_sem=send_sems.at[3 * a + k], recv_sem=recv_sems.at[3 * a + k],
                    device_id=(px, py, c), device_id_type=_MESH)
                cp.start()
                cps.append(cp)
        for cp in cps:
            cp.wait()

    return pl.pallas_call(
        body, name=name, in_specs=[_ANY] * n, out_specs=[_ANY] * n,
        out_shape=[jax.ShapeDtypeStruct((3,) + s.shape[1:], s.dtype) for s in s1s],
        scratch_shapes=[pltpu.SemaphoreType.DMA((3 * n,)), pltpu.SemaphoreType.DMA((3 * n,))],
    )(*s1s)


_HBM = pl.BlockSpec(memory_space=pltpu.HBM)
_SEM = pl.BlockSpec(memory_space=pltpu.SEMAPHORE)
_EFFECT = pltpu.SideEffectType.DATAFLOW_SIDE_EFFECTING


def _push_copies(src_refs, land_refs, send_sems, recv_sems, src_by_peer):
    x, y, c = _me()
    my_id = 4 * x + 2 * y + c
    out = []
    for a in range(len(src_refs)):
        for f in range(1, N_DEV):
            px, py, pc = x ^ (f >> 2), y ^ ((f >> 1) & 1), c ^ (f & 1)
            pid = 4 * px + 2 * py + pc
            src = src_refs[a].at[pid] if src_by_peer else src_refs[a]
            start = pltpu.make_async_remote_copy(
                src_ref=src, dst_ref=land_refs[a].at[my_id], send_sem=send_sems.at[7 * a + f - 1],
                recv_sem=recv_sems.at[7 * a + f - 1], device_id=(px, py, pc), device_id_type=_MESH)
            landed = pltpu.make_async_remote_copy(
                src_ref=src, dst_ref=land_refs[a].at[pid], send_sem=send_sems.at[7 * a + f - 1],
                recv_sem=recv_sems.at[7 * a + f - 1], device_id=(px, py, pc), device_id_type=_MESH)
            out.append((start, landed))
    return out


def _push_start(name, srcs, src_by_peer, after):
    n = len(srcs)
    lands = [jax.ShapeDtypeStruct((N_DEV,) + (s.shape[1:] if src_by_peer else s.shape), s.dtype) for s in srcs]

    def body(*refs):
        src_refs, land_refs = refs[:n], refs[n:2 * n]
        send_sems, recv_sems = refs[2 * n + 1], refs[2 * n + 2]
        token = refs[-1]
        for start, _ in _push_copies(src_refs, land_refs, send_sems, recv_sems, src_by_peer):
            start.start()
        token[...] = jnp.zeros_like(token)

    hbm = lambda a: pltpu.with_memory_space_constraint(a, pltpu.HBM)
    res = pl.pallas_call(
        body, name=name,
        out_shape=(pltpu.SemaphoreType.DMA((7 * n,)), pltpu.SemaphoreType.DMA((7 * n,)),
                   *[pltpu.HBM(s.shape, s.dtype) for s in srcs], *[pltpu.HBM(s.shape, s.dtype) for s in lands],
                   jax.ShapeDtypeStruct((8, LANE), F32)),
        in_specs=[_HBM] * (2 * n) + [_ANY],
        out_specs=(_SEM, _SEM, *[_HBM] * (2 * n), pl.BlockSpec(memory_space=pltpu.VMEM)),
        input_output_aliases={i: 2 + i for i in range(2 * n)},
        compiler_params=pltpu.CompilerParams(has_side_effects=_EFFECT),
    )(*[hbm(s) for s in srcs], *[hbm(lax.empty(s.shape, s.dtype)) for s in lands], after)
    return res[0], res[1], list(res[2:2 + n]), list(res[2 + n:2 + 2 * n]), res[-1]


def _push_wait(name, send_sems, recv_sems, srcs, lands, src_by_peer, after):
    n = len(srcs)

    def body(*refs):
        src_refs, land_refs = refs[:n], refs[n:2 * n]
        s_sems, r_sems = refs[2 * n], refs[2 * n + 1]
        for _, landed in _push_copies(src_refs, land_refs, s_sems, r_sems, src_by_peer):
            landed.wait_send()
            landed.wait_recv()

    res = pl.pallas_call(
        body, name=name,
        out_shape=tuple(pltpu.HBM(s.shape, s.dtype) for s in list(srcs) + list(lands)),
        in_specs=[_HBM] * (2 * n) + [_SEM, _SEM, _ANY],
        out_specs=tuple([_HBM] * (2 * n)),
        input_output_aliases={i: i for i in range(2 * n)},
        compiler_params=pltpu.CompilerParams(has_side_effects=_EFFECT),
    )(*srcs, *lands, send_sems, recv_sems, after)
    return list(res[:n]), list(res[n:])


_SHARDED = (
    ("meta_tokens", 1, (N_META, D_MODEL)),
    ("w_in", 1, (D_MODEL, IN_COLS)),
    ("w_q_b", 1, (Q_LORA, MLA_HEADS * QK_HEAD)),
    ("w_kv_b", 1, (KV_LORA, MLA_HEADS * (QK_NOPE + V_HEAD))),
    ("dn_conv_w", 1, (DN_CONV, 3 * DN_WIDTH)),
    ("w_out", 0, (2 * DN_WIDTH, D_MODEL)),
    ("w_gate", 1, (D_MODEL, D_FF)),
    ("w_up", 1, (D_MODEL, D_FF)),
    ("ffn_conv_w", 1, (FFN_CONV, D_FF)),
    ("w_down", 0, (D_FF, D_MODEL)),
)
_MXU_GATHERED = ("w_in", "w_q_b", "w_kv_b", "w_out", "w_gate", "w_up", "w_down")
_F32_GATHERED = ("meta_tokens", "dn_conv_w", "ffn_conv_w")
_EARLY = ("w_in", "w_q_b", "w_kv_b")
_LATE = ("w_out", "w_gate", "w_up", "w_down")
_TRANSPOSED = ("w_in", "w_q_b", "w_gate", "w_up")
_REPLICATED = (
    ("attn_norm_w", D_MODEL), ("q_a_norm_w", Q_LORA), ("kv_a_norm_w", KV_LORA), ("q_norm_w", QK_HEAD),
    ("k_norm_w", QK_HEAD), ("mla_out_norm_w", V_HEAD), ("dn_A_log", DN_HEADS), ("dn_dt_bias", DN_HEADS),
    ("dn_out_norm_w", DN_DIM), ("ffn_norm_w", D_MODEL), ("ffn_conv_b", D_FF),
)
_PACK_COLS = 1024
_PACK_ROW_MULT = 320
_SMALL_SHAPE = (8, 768)
_SMALL_BLOCK = (8, 512)


def _local_shape(dim, shape):
    return (shape[0] // N_DEV, shape[1]) if dim == 0 else (shape[0], shape[1] // N_DEV)


def _pack_rows(n, mult):
    rows = -(-n // _PACK_COLS)
    return -(-rows // mult) * mult


def _pack(flats, mult, axis=0):
    cat = jnp.concatenate(flats, axis=-1)
    n = cat.shape[-1]
    r = _pack_rows(n, mult)
    pad = [(0, 0)] * (cat.ndim - 1) + [(0, r * _PACK_COLS - n)]
    return jnp.pad(cat, pad).reshape(cat.shape[:-1] + (r, _PACK_COLS))


def _to_blocks(full, dim):
    r, c = full.shape
    if dim == 0:
        return full.reshape(N_DEV, (r // N_DEV) * c)
    return full.reshape(r, N_DEV, c // N_DEV).transpose(1, 0, 2).reshape(N_DEV, r * (c // N_DEV))


def _from_blocks(blocks, dim, shape):
    r, c = shape
    if dim == 0:
        return blocks.reshape(r, c)
    return blocks.reshape(N_DEV, r, c // N_DEV).transpose(1, 0, 2).reshape(r, c)


def _split(flat, sizes):
    out, o = [], 0
    for s in sizes:
        out.append(flat[..., o:o + s])
        o += s
    return out


def _gather_weights(local, names, dtype, mult):
    specs = [s for s in _SHARDED if s[0] in names]
    pack = _pack([local[n].astype(dtype).reshape(-1) for n, _, _ in specs], mult)
    got = _all_gather("gather_" + "_".join(n[:5] for n in names[:2]), pack)
    flat = got.reshape(N_DEV, -1)
    sizes = [math.prod(_local_shape(d, s)) for _, d, s in specs]
    return {n: _from_blocks(p, d, s) for (n, d, s), p in zip(specs, _split(flat, sizes))}


def kernel(x, meta_tokens, attn_norm_w, w_in, q_a_norm_w, w_q_b, kv_a_norm_w, w_kv_b, q_norm_w, k_norm_w, mla_out_norm_w, dn_conv_w, dn_A_log, dn_dt_bias, dn_out_norm_w, w_out, ffn_norm_w, w_gate, w_up, ffn_conv_w, ffn_conv_b, w_down, loss_target, m_meta_tokens, m_attn_norm_w, m_w_in, m_q_a_norm_w, m_w_q_b, m_kv_a_norm_w, m_w_kv_b, m_q_norm_w, m_k_norm_w, m_mla_out_norm_w, m_dn_conv_w, m_dn_A_log, m_dn_dt_bias, m_dn_out_norm_w, m_w_out, m_ffn_norm_w, m_w_gate, m_w_up, m_ffn_conv_w, m_ffn_conv_b, m_w_down, v_meta_tokens, v_attn_norm_w, v_w_in, v_q_a_norm_w, v_w_q_b, v_kv_a_norm_w, v_w_kv_b, v_q_norm_w, v_k_norm_w, v_mla_out_norm_w, v_dn_conv_w, v_dn_A_log, v_dn_dt_bias, v_dn_out_norm_w, v_w_out, v_ffn_norm_w, v_w_gate, v_w_up, v_ffn_conv_w, v_ffn_conv_b, v_w_down):
    names = [n for n, _, _ in _SHARDED] + [n for n, _ in _REPLICATED]
    given = dict(locals())
    two_d = lambda a: a.reshape(a.shape[-2:])
    view = lambda a, n: two_d(a).T if n in _TRANSPOSED else two_d(a)
    wl = {n: view(given[n], n) for n in names}
    ml = {n: view(given["m_" + n], n) for n in names}
    vl = {n: view(given["v_" + n], n) for n in names}
    out_shapes = {n: given[n].shape for n in names}

    spec = {n: (d, s) for n, d, s in _SHARDED}
    small_sizes = [math.prod(_local_shape(*spec[n])) for n in _F32_GATHERED]

    def small_block(d):
        cat = jnp.concatenate([d[n].reshape(d[n].shape[:-2] + (-1,)) for n in _F32_GATHERED], axis=-1)
        pad = [(0, 0)] * (cat.ndim - 1) + [(0, math.prod(_SMALL_BLOCK) - cat.shape[-1])]
        return jnp.pad(cat, pad).reshape(cat.shape[:-1] + _SMALL_BLOCK)

    def shard(n):
        return wl[n].astype(_MXU)

    def from_slots(n, blocks):
        d, s = spec[n]
        if d == 0 or n in _TRANSPOSED:
            return blocks.reshape(-1, blocks.shape[-1])
        return blocks.transpose(1, 0, 2).reshape(s)

    my_id = 4 * lax.axis_index("x") + 2 * lax.axis_index("y") + lax.axis_index("c")
    got = _all_gather_many("gather_early", [shard(n) for n in _EARLY] + [small_block(wl)])
    full = {n: a for n, a in wl.items() if n not in _LATE}
    for n, blocks in zip(_EARLY, got):
        full[n] = from_slots(n, blocks)
    for n, p in zip(_F32_GATHERED, _split(got[-1].reshape(N_DEV, -1), small_sizes)):
        full[n] = _from_blocks(p, *spec[n])
    late_own = [shard(n) for n in _LATE]
    l_send, l_recv, l_src, l_land, token = _push_start("gather_late_start", late_own, False, got[-1])

    def late_weights(after):
        _, lands = _push_wait("gather_late_wait", l_send, l_recv, l_src, l_land, False, after)
        out = {}
        for n, land, own in zip(_LATE, lands, late_own):
            out[n] = from_slots(n, lax.dynamic_update_slice(land, own[None], (my_id, 0, 0))).astype(_MXU)
        return out

    def dest_blocks(n, a):
        d, s = spec[n]
        r, c = _local_shape(d, s)
        if n in _TRANSPOSED:
            return a.reshape(N_DEV, c, r)
        return a.reshape(N_DEV, r, c) if d == 0 else a.reshape(r, N_DEV, c).transpose(1, 0, 2)

    pushed = []

    def grads_ready(g, names):
        nat = _grads_to_natural({n: g[n] for n in names})
        blocks = [dest_blocks(n, nat[n]).astype(_MXU) for n in names]
        sends, recvs, srcs, lands, tok = _push_start("rs_" + names[0] + "_start", blocks, True, token)
        pushed.append((names, sends, recvs, srcs, lands))
        return tok

    seq = x.shape[1]
    tp = ROW0 + seq
    h0 = jnp.concatenate([jnp.zeros((PAD, D_MODEL), F32), full["meta_tokens"], x[0]], axis=0)
    tgt = jnp.concatenate([jnp.zeros((ROW0, D_MODEL), F32), loss_target[0]], axis=0)
    loss, dh0, raw = _local_step(h0, tgt, _prepare(full, tp), token, late_weights, grads_ready)
    g = _grads_to_natural(raw)
    g["meta_tokens"] = dh0[PAD:ROW0]
    grad_x = dh0[ROW0:][None]

    big = [{}, {}, {}, {}]
    rep_names = [n for n, _ in _REPLICATED]
    raw_key = {"dn_A_log": "alog_b", "dn_dt_bias": "dtb_b"}
    pieces = [raw[raw_key.get(n, n)] for n in rep_names] + [loss]
    pieces += [g[n].reshape(1, -1) for n in _F32_GATHERED]
    widths = [p.shape[1] for p in pieces]
    offs = [sum(widths[:k]) for k in range(len(widths))]
    cat = jnp.concatenate(pieces, axis=1)
    cols = -(-cat.shape[1] // (8 * LANE)) * LANE
    mine = jnp.pad(cat, ((0, 0), (0, 8 * cols - cat.shape[1]))).reshape(8, cols)
    everyone = _all_gather("gather_small_grads", mine)
    total = _sum_parts("sum_small_grads", [(everyone, d) for d in range(N_DEV)]).reshape(1, 8 * cols)
    tot = {n: total[0, o:o + wd] for n, o, wd in zip(rep_names + ["loss"] + list(_F32_GATHERED), offs, widths)}
    lanes = lambda a: jnp.pad(a, ((0, 0), (0, -a.shape[1] % LANE)))
    items = [(o, -(-size // LANE) * LANE, n in raw_key) for (n, size), o in zip(_REPLICATED, offs)]
    sm = _adam_vectors("adam_replicated", total, items, [lanes(wl[n]) for n in rep_names],
                       [lanes(ml[n]) for n in rep_names], [lanes(vl[n]) for n in rep_names])
    sm = [{n: a[:, :size] for (n, size), a in zip(_REPLICATED, kind)} for kind in sm]
    mine_of = {}
    for n in _F32_GATHERED:
        d, s = spec[n]
        r, c = _local_shape(d, s)
        mine_of[n] = lax.dynamic_slice(tot[n].reshape(s), (0, my_id * c), (r, c))
    res = _adam("adam_small_sharded", [(small_block(mine_of)[None], 0)], small_block(wl), small_block(ml),
                small_block(vl))
    for kind, a in enumerate(res):
        big[kind].update(zip(_F32_GATHERED, _split(a.reshape(-1), small_sizes)))

    for names, sends, recvs, srcs, lands in pushed:
        srcs, lands = _push_wait("rs_" + names[0] + "_wait", sends, recvs, srcs, lands, True, dh0)
        for n, src, land in zip(names, srcs, lands):
            parts = [(src, my_id)] + [(land, my_id ^ f) for f in range(1, N_DEV)]
            for kind, a in enumerate(_adam("adam_" + n, parts, wl[n], ml[n], vl[n])):
                big[kind][n] = a

    outs = [tot["loss"][0], grad_x]
    for kind in range(4):
        for n in ("meta_tokens", "attn_norm_w", "w_in", "q_a_norm_w", "w_q_b", "kv_a_norm_w", "w_kv_b", "q_norm_w",
                  "k_norm_w", "mla_out_norm_w", "dn_conv_w", "dn_A_log", "dn_dt_bias", "dn_out_norm_w", "w_out",
                  "ffn_norm_w", "w_gate", "w_up", "ffn_conv_w", "ffn_conv_b", "w_down"):
            src = big[kind] if n in big[kind] else sm[kind]
            a = src[n].T if n in _TRANSPOSED else src[n]
            outs.append(a.reshape(out_shapes[n]))
    return tuple(outs)
```

```python
import functools
import math

import jax
import jax.numpy as jnp
from jax import lax
from jax.experimental import pallas as pl
from jax.experimental.pallas import tpu as pltpu

F32 = jnp.float32
BF16 = jnp.bfloat16
_MXU = jnp.bfloat16
_HI = lax.Precision.HIGHEST

D_MODEL = 1024
N_META = 16
PAD = 112
ROW0 = PAD + N_META
MLA_HEADS = 4
QK_NOPE = 128
QK_ROPE = 64
QK_HEAD = QK_NOPE + QK_ROPE
V_HEAD = 128
Q_LORA = 256
KV_LORA = 256
ROPE_THETA = 10000.0
DN_HEADS = 4
DN_DIM = 128
DN_WIDTH = DN_HEADS * DN_DIM
DN_CONV = 4
DN_CHUNK = 64
GDN_SUB_CHUNKS = 2
D_FF = 2816
FFN_CONV = 3
EPS = 1e-6
HP = 256
C_QKV = 0
C_Z = 1536
C_QL = 2048
C_KVL = 2304
C_KPE = 2560
C_AB = 2688
IN_P = 2816
IN_COLS = 2632

ADAM_LR = 0.001
ADAM_B1 = 0.9
ADAM_B2 = 0.999
ADAM_EPS = 1e-08
ADAM_WD = 0.01
ADAM_STEP = 10

N_DEV = 8
TM = 128
LANE = 128
VMEM_LIMIT = 56 * 1024 * 1024
NEG = -1e30


def _dot(a, b, dims, hp=False):
    if hp:
        return lax.dot_general(a.astype(F32), b.astype(F32), (dims, ((), ())),
                               precision=lax.Precision.HIGH if hp == "3x" else _HI, preferred_element_type=F32)
    return lax.dot_general(a.astype(_MXU), b.astype(_MXU), (dims, ((), ())),
                           preferred_element_type=F32)


def _nn(a, b, hp=False):
    return _dot(a, b, ((1,), (0,)), hp)


def _nt(a, b, hp=False):
    return _dot(a, b, ((1,), (1,)), hp)


def _tn(a, b, hp=False):
    return _dot(a, b, ((0,), (0,)), hp)


def _sigmoid(x):
    return 1.0 / (1.0 + jnp.exp(-x))


def _rms_fwd(x, w, n):
    r = lax.rsqrt(jnp.sum(x * x, axis=-1, keepdims=True) * (1.0 / n) + EPS)
    return x * r * w, r


def _rms_bwd(x, w, dy, n):
    r = lax.rsqrt(jnp.sum(x * x, axis=-1, keepdims=True) * (1.0 / n) + EPS)
    xh = x * r
    gy = dy * w
    dx = r * (gy - xh * (jnp.sum(gy * xh, axis=-1, keepdims=True) * (1.0 / n)))
    return dx, dy * xh


def _rowsum(x):
    return jnp.sum(x, axis=0, keepdims=True)


def _row_ids(i, tm):
    return i * tm + lax.broadcasted_iota(jnp.int32, (tm, 1), 0)


def _shift_down(ext, s, tm):
    if s == 0:
        return ext[8:8 + tm]
    return pltpu.roll(ext, s, 0)[8:8 + tm]


def _shift_up(ext, s, tm):
    if s == 0:
        return ext[0:tm]
    return pltpu.roll(ext, tm + 8 - s, 0)[0:tm]


def _conv_fwd(x, halo_prev, w, width):
    tm = x.shape[0]
    ext = jnp.concatenate([halo_prev, x], axis=0)
    y = None
    for j in range(width):
        t = w[j:j + 1, :] * _shift_down(ext, width - 1 - j, tm)
        y = t if y is None else y + t
    return y


def _conv_bwd_x(dy, halo_next, w, width):
    tm = dy.shape[0]
    ext = jnp.concatenate([dy, halo_next], axis=0)
    dx = None
    for j in range(width):
        t = w[j:j + 1, :] * _shift_up(ext, width - 1 - j, tm)
        dx = t if dx is None else dx + t
    return dx


def _conv_bwd_w(dy, x, halo_prev, width):
    tm = dy.shape[0]
    ext = jnp.concatenate([halo_prev, x], axis=0)
    rows = [_rowsum(dy * _shift_down(ext, width - 1 - j, tm)) for j in range(width)]
    rows += [jnp.zeros_like(rows[0])] * (8 - width)
    return jnp.concatenate(rows, axis=0)


def _softplus(x):
    e = jnp.exp(-jnp.abs(x))
    u = 1.0 + e
    l1p = jnp.where(u == 1.0, e, jnp.log(u) * e / jnp.where(u == 1.0, 1.0, u - 1.0))
    return jnp.maximum(x, 0.0) + l1p


def _swap_halves(x):
    lane = lax.broadcasted_iota(jnp.int32, x.shape, 1)
    return jnp.where(lane < 32, pltpu.roll(x, 96, 1), jnp.where(lane < 64, pltpu.roll(x, 32, 1), 0.0))


class _In:
    def __init__(self, arr, width=None, cb=0, kind="cur"):
        self.arr, self.kind = arr, kind
        self.width = arr.shape[1] if width is None else width
        self.cb = cb


def _whole_spec(x):
    return pl.BlockSpec(x.shape, lambda i, nd=x.ndim: (0,) * nd, pipeline_mode=pl.Buffered(1))


def _tile_spec(t, tm, tp):
    r8 = tm // 8
    if t.kind == "cur":
        return pl.BlockSpec((tm, t.width), lambda i, cb=t.cb: (i, cb))
    if t.kind == "prev":
        return pl.BlockSpec((8, t.width), lambda i, cb=t.cb: (jnp.maximum(i * r8 - 1, 0), cb))
    return pl.BlockSpec((8, t.width), lambda i, cb=t.cb: (jnp.minimum((i + 1) * r8, tp // 8 - 1), cb))


def _rows(name, fn, tiled, full, outs, accs=(), tm=TM):
    tp = tiled[0].arr.shape[0]
    nt = tp // tm
    r8 = tm // 8
    n_in = len(tiled) + len(full)
    n_out = len(outs)

    def body(*refs):
        i = pl.program_id(0)
        vals = [r[...] for r in refs[:n_in]]
        o_t, o_a = fn(i, *vals)
        for r, v in zip(refs[n_in:n_in + n_out], o_t):
            r[...] = v.astype(r.dtype)
        for r, v in zip(refs[n_in + n_out:], o_a):
            @pl.when(i == 0)
            def _():
                r[...] = v

            @pl.when(i > 0)
            def _():
                r[...] += v

    in_specs = [_tile_spec(t, tm, tp) for t in tiled]
    in_specs += [pl.BlockSpec(a.shape, lambda i, nd=a.ndim: (0,) * nd) for a in full]
    out_specs = [pl.BlockSpec((tm, w), lambda i: (i, 0)) for w, _ in outs]
    out_specs += [pl.BlockSpec((r, w), lambda i: (0, 0)) for r, w in accs]
    out_shape = [jax.ShapeDtypeStruct((tp, w), dt) for w, dt in outs]
    out_shape += [jax.ShapeDtypeStruct((r, w), F32) for r, w in accs]
    res = pl.pallas_call(
        body, name=name, grid=(nt,), in_specs=in_specs, out_specs=out_specs, out_shape=out_shape,
        compiler_params=pltpu.CompilerParams(dimension_semantics=("arbitrary",), vmem_limit_bytes=VMEM_LIMIT),
    )(*[t.arr for t in tiled], *full)
    return res


def _pick(n, cap, mult):
    best = None
    for d in range(mult, min(n, cap) + 1, mult):
        if n % d == 0:
            best = d
    assert best is not None, (n, cap, mult)
    return best


_ANY_SPEC = pl.BlockSpec(memory_space=pl.ANY)


def _mm(name, a, b, mode, out_dtype=F32, resid=None, after=None):
    if mode == "tn":
        m, k = a.shape
        n = b.shape[1]
        tk = _pick(k, 512, 128)
        tn = _pick(n, 1408, 128)

        def body_tn(a_ref, b_ref, o_ref):
            o_ref[...] = _tn(a_ref[...], b_ref[...]).astype(o_ref.dtype)

        return pl.pallas_call(
            body_tn, name=name, grid=(n // tn, k // tk),
            in_specs=[pl.BlockSpec((m, tk), lambda j, p: (0, p)),
                      pl.BlockSpec((m, tn), lambda j, p: (0, j))],
            out_specs=pl.BlockSpec((tk, tn), lambda j, p: (p, j)),
            out_shape=jax.ShapeDtypeStruct((k, n), out_dtype),
            compiler_params=pltpu.CompilerParams(
                dimension_semantics=("parallel", "parallel"), vmem_limit_bytes=VMEM_LIMIT),
        )(a, b)

    m, k = a.shape
    n = b.shape[1] if mode == "nn" else b.shape[0]
    tn = _pick(n, 1408, 128)
    tm = _pick(m, 1152, 16)
    dotf = _nn if mode == "nn" else _nt

    def body(*refs):
        a_ref, b_ref, o_ref = refs[0], refs[1], refs[-1]
        acc = dotf(a_ref[...], b_ref[...])
        if resid is not None:
            acc = refs[2][...] + acc
        o_ref[...] = acc.astype(o_ref.dtype)

    b_spec = (pl.BlockSpec((k, tn), lambda j, i: (0, j)) if mode == "nn"
              else pl.BlockSpec((tn, k), lambda j, i: (j, 0)))
    in_specs = [pl.BlockSpec((tm, k), lambda j, i: (i, 0)), b_spec]
    args = [a, b]
    if resid is not None:
        in_specs.append(pl.BlockSpec((tm, tn), lambda j, i: (i, j)))
        args.append(resid)
    if after is not None:
        in_specs.append(_ANY_SPEC)
        args.append(after)
    return pl.pallas_call(
        body, name=name, grid=(n // tn, m // tm), in_specs=in_specs,
        out_specs=pl.BlockSpec((tm, tn), lambda j, i: (i, j)),
        out_shape=jax.ShapeDtypeStruct((m, n), out_dtype),
        compiler_params=pltpu.CompilerParams(
            dimension_semantics=("parallel", "parallel"), vmem_limit_bytes=VMEM_LIMIT),
    )(*args)


def _mm_tn2(name, a1, a2, b, out_dtype=F32):
    m, k = a1.shape
    n = b.shape[1]
    tk = _pick(k, 512, 128)

    def body(a1_ref, a2_ref, b_ref, o1_ref, o2_ref):
        bb = b_ref[...]
        o1_ref[...] = _tn(a1_ref[...], bb).astype(o1_ref.dtype)
        o2_ref[...] = _tn(a2_ref[...], bb).astype(o2_ref.dtype)

    a_spec = pl.BlockSpec((m, tk), lambda p: (0, p))
    o_spec = pl.BlockSpec((tk, n), lambda p: (p, 0))
    return pl.pallas_call(
        body, name=name, grid=(k // tk,),
        in_specs=[a_spec, a_spec, pl.BlockSpec((m, n), lambda p: (0, 0))],
        out_specs=[o_spec, o_spec], out_shape=[jax.ShapeDtypeStruct((k, n), out_dtype)] * 2,
        compiler_params=pltpu.CompilerParams(dimension_semantics=("parallel",), vmem_limit_bytes=VMEM_LIMIT),
    )(a1, a2, b)


def _norm_mm(name, x, norm_w, b, mode="nt", x_cb=0, after=None):
    m = x.shape[0]
    k = norm_w.shape[1]
    n = b.shape[0] if mode == "nt" else b.shape[1]
    tn = _pick(n, 1408, 128)
    tm = _pick(m, 1152, 16)
    dotf = _nt if mode == "nt" else _nn
    extra = [] if after is None else [after]

    def body(x_ref, w_ref, b_ref, *rest):
        o_ref, u_ref = rest[-2:]

        @pl.when(pl.program_id(1) == 0)
        def _():
            u_ref[...] = _rms_fwd(x_ref[...], w_ref[...], k)[0].astype(u_ref.dtype)

        o_ref[...] = dotf(u_ref[...], b_ref[...])

    b_spec = (pl.BlockSpec((tn, k), lambda i, j: (j, 0)) if mode == "nt"
              else pl.BlockSpec((k, tn), lambda i, j: (0, j)))
    return pl.pallas_call(
        body, name=name, grid=(m // tm, n // tn),
        in_specs=[pl.BlockSpec((tm, k), lambda i, j: (i, x_cb)), pl.BlockSpec((1, k), lambda i, j: (0, 0)),
                  b_spec] + [_ANY_SPEC] * len(extra),
        out_specs=[pl.BlockSpec((tm, tn), lambda i, j: (i, j)), pl.BlockSpec((tm, k), lambda i, j: (i, 0))],
        out_shape=[jax.ShapeDtypeStruct((m, n), F32), jax.ShapeDtypeStruct((m, k), _MXU)],
        compiler_params=pltpu.CompilerParams(
            dimension_semantics=("arbitrary", "arbitrary"), vmem_limit_bytes=VMEM_LIMIT),
    )(x, norm_w, b, *extra)


def _pro_mm(name, fn, tiled, full, k, b, resid):
    m = resid.shape[0]
    n = b.shape[1]
    tm = _pick(m, 576, 16)
    n_in = len(tiled) + len(full)

    def body(*refs):
        i = pl.program_id(0)
        u = fn(i, *[r[...] for r in refs[:n_in]]).astype(_MXU)
        b_ref, r_ref, o_ref, u_ref = refs[n_in:]
        u_ref[...] = u
        o_ref[...] = r_ref[...] + _nn(u, b_ref[...])

    row = lambda w: pl.BlockSpec((tm, w), lambda i: (i, 0))
    in_specs = [_tile_spec(t, tm, m) for t in tiled]
    in_specs += [_whole_spec(x) for x in full] + [_whole_spec(b), row(n)]
    return pl.pallas_call(
        body, name=name, grid=(m // tm,), in_specs=in_specs, out_specs=[row(n), row(k)],
        out_shape=[jax.ShapeDtypeStruct((m, n), F32), jax.ShapeDtypeStruct((m, k), _MXU)],
        compiler_params=pltpu.CompilerParams(dimension_semantics=("parallel",), vmem_limit_bytes=VMEM_LIMIT),
    )(*[t.arr for t in tiled], *full, b, resid)


def _ffn_in(h2, norm_w, w_gate_t, w_up_t, conv_w8, conv_b):
    m, k = h2.shape
    n = w_gate_t.shape[0]
    tm = _pick(m, 288, 16)

    def body(x_ref, xp_ref, nw_ref, wg_ref, wu_ref, cw_ref, cb_ref, hn_ref, gp_ref, up_ref, act_ref):
        i = pl.program_id(0)
        nw = nw_ref[...]
        hn = _rms_fwd(x_ref[...], nw, k)[0].astype(_MXU)
        hn_prev = _rms_fwd(xp_ref[...], nw, k)[0].astype(_MXU)
        wg = wg_ref[...]
        gp = _nt(hn, wg)
        gp_prev = jnp.where(i > 0, _nt(hn_prev, wg), 0.0)
        up = _nt(hn, wu_ref[...])
        gate = _conv_fwd(gp, gp_prev, cw_ref[...], FFN_CONV) + cb_ref[...]
        hn_ref[...] = hn
        gp_ref[...] = gp
        up_ref[...] = up
        act_ref[...] = (_silu_parts(gate)[0] * up).astype(act_ref.dtype)

    row = lambda w: pl.BlockSpec((tm, w), lambda i: (i, 0))
    r8 = tm // 8
    return pl.pallas_call(
        body, name="ffn_in", grid=(m // tm,),
        in_specs=[row(k), pl.BlockSpec((8, k), lambda i: (jnp.maximum(i * r8 - 1, 0), 0)), _whole_spec(norm_w),
                  _whole_spec(w_gate_t), _whole_spec(w_up_t), _whole_spec(conv_w8), _whole_spec(conv_b)],
        out_specs=[row(k), row(n), row(n), row(n)],
        out_shape=[jax.ShapeDtypeStruct((m, k), _MXU), jax.ShapeDtypeStruct((m, n), F32),
                   jax.ShapeDtypeStruct((m, n), F32), jax.ShapeDtypeStruct((m, n), _MXU)],
        compiler_params=pltpu.CompilerParams(dimension_semantics=("parallel",), vmem_limit_bytes=VMEM_LIMIT),
    )(h2, h2, norm_w, w_gate_t, w_up_t, conv_w8, conv_b)


def _mm_rows(name, a, b, mode, fn, tiled, full, outs, accs=(), tm_cap=576):
    a_list = list(a) if isinstance(a, (list, tuple)) else [a]
    b_list = list(b) if isinstance(b, (list, tuple)) else [b]
    na = len(a_list)
    m = a_list[0].shape[0]
    tm = _pick(m, tm_cap, 16)
    dotf = _nn if mode == "nn" else _nt
    n_in = len(tiled) + len(full)
    n_out = len(outs)
    first = 2 * na

    def body(*refs):
        i = pl.program_id(0)
        vals = [r[...] for r in refs[first:first + n_in]]
        acc = dotf(refs[0][...], refs[na][...])
        for p in range(1, na):
            acc = acc + dotf(refs[p][...], refs[na + p][...])
        o_t, o_a = fn(i, acc, *vals)
        for r, v in zip(refs[first + n_in:first + n_in + n_out], o_t):
            r[...] = v.astype(r.dtype)
        for r, v in zip(refs[first + n_in + n_out:], o_a):
            @pl.when(i == 0)
            def _():
                r[...] = v

            @pl.when(i > 0)
            def _():
                r[...] += v

    whole = lambda x: pl.BlockSpec(x.shape, lambda i, nd=x.ndim: (0,) * nd)
    in_specs = [pl.BlockSpec((tm, x.shape[1]), lambda i: (i, 0)) for x in a_list] + [_whole_spec(x) for x in b_list]
    in_specs += [_tile_spec(t, tm, m) for t in tiled]
    in_specs += [whole(x) for x in full]
    out_specs = [pl.BlockSpec((tm, w), lambda i: (i, 0)) for w, _ in outs]
    out_specs += [pl.BlockSpec((r, w), lambda i: (0, 0)) for r, w in accs]
    out_shape = [jax.ShapeDtypeStruct((m, w), dt) for w, dt in outs]
    out_shape += [jax.ShapeDtypeStruct((r, w), F32) for r, w in accs]
    return pl.pallas_call(
        body, name=name, grid=(m // tm,), in_specs=in_specs, out_specs=out_specs, out_shape=out_shape,
        compiler_params=pltpu.CompilerParams(dimension_semantics=("arbitrary",), vmem_limit_bytes=VMEM_LIMIT),
    )(*a_list, *b_list, *[t.arr for t in tiled], *full)


ATTN_Q_TILES = 4


def _attn_probs(q, k, row0):
    tq, tp = q.shape[0], k.shape[0]
    s = _nt(q, k) * (1.0 / math.sqrt(QK_HEAD))
    row = row0 + lax.broadcasted_iota(jnp.int32, (tq, tp), 0)
    col = lax.broadcasted_iota(jnp.int32, (tq, tp), 1)
    ok = (col <= row) & (col >= PAD)
    s = jnp.where(ok, s, NEG)
    m = jnp.max(s, axis=-1, keepdims=True)
    e = jnp.exp(s - m)
    return e * (1.0 / jnp.sum(e, axis=-1, keepdims=True))


def _attn_fwd(q, k, v):
    tp = q.shape[0]
    tq = tp // ATTN_Q_TILES

    def body(q_ref, k_ref, v_ref, o_ref):
        for i in range(ATTN_Q_TILES):
            rows = slice(i * tq, (i + 1) * tq)
            keys = slice(0, (i + 1) * tq)
            p = _attn_probs(q_ref[rows, :], k_ref[keys, :], i * tq)
            o_ref[rows, :] = _nn(p, v_ref[keys, :])

    return pl.pallas_call(
        body, name="attn_fwd", grid=(MLA_HEADS,),
        in_specs=[pl.BlockSpec((tp, HP), lambda h: (0, h)),
                  pl.BlockSpec((tp, HP), lambda h: (0, h)),
                  pl.BlockSpec((tp, V_HEAD), lambda h: (0, h))],
        out_specs=pl.BlockSpec((tp, V_HEAD), lambda h: (0, h)),
        out_shape=jax.ShapeDtypeStruct((tp, MLA_HEADS * V_HEAD), F32),
        compiler_params=pltpu.CompilerParams(dimension_semantics=("parallel",), vmem_limit_bytes=VMEM_LIMIT),
    )(q, k, v)


def _attn_bwd(q, k, v, do):
    tp = q.shape[0]
    tq = tp // ATTN_Q_TILES

    def body(q_ref, k_ref, v_ref, do_ref, dq_ref, dk_ref, dv_ref):
        for i in reversed(range(ATTN_Q_TILES)):
            rows = slice(i * tq, (i + 1) * tq)
            keys = slice(0, (i + 1) * tq)
            qb = q_ref[rows, :]
            kk = k_ref[keys, :]
            dob = do_ref[rows, :]
            p = _attn_probs(qb, kk, i * tq)
            dp = _nt(dob, v_ref[keys, :])
            delta = jnp.sum(p * dp, axis=-1, keepdims=True)
            ds = p * (dp - delta) * (1.0 / math.sqrt(QK_HEAD))
            dq_ref[rows, :] = _nn(ds, kk)
            if i == ATTN_Q_TILES - 1:
                dk_ref[...] = _tn(ds, qb)
                dv_ref[...] = _tn(p, dob)
            else:
                dk_ref[keys, :] += _tn(ds, qb)
                dv_ref[keys, :] += _tn(p, dob)

    full = lambda w: pl.BlockSpec((tp, w), lambda h: (0, h))
    return pl.pallas_call(
        body, name="attn_bwd", grid=(MLA_HEADS,),
        in_specs=[full(HP), full(HP), full(V_HEAD), full(V_HEAD)],
        out_specs=[full(HP), full(HP), full(V_HEAD)],
        out_shape=[jax.ShapeDtypeStruct((tp, MLA_HEADS * HP), F32),
                   jax.ShapeDtypeStruct((tp, MLA_HEADS * HP), F32),
                   jax.ShapeDtypeStruct((tp, MLA_HEADS * V_HEAD), F32)],
        compiler_params=pltpu.CompilerParams(dimension_semantics=("parallel",), vmem_limit_bytes=VMEM_LIMIT),
    )(q, k, v, do)


def _gdn_consts():
    c = DN_CHUNK
    r = lax.broadcasted_iota(jnp.int32, (c, c), 0)
    cc = lax.broadcasted_iota(jnp.int32, (c, c), 1)
    incl = r >= cc
    strict = r > cc
    return incl, strict


def _cumsum_rows(x, reverse=False):
    c = x.shape[0]
    row = lax.broadcasted_iota(jnp.int32, x.shape, 0)
    s = 1
    while s < c:
        if reverse:
            x = x + jnp.where(row < c - s, pltpu.roll(x, c - s, 0), 0.0)
        else:
            x = x + jnp.where(row >= s, pltpu.roll(x, s, 0), 0.0)
        s *= 2
    return x


def _each(fn, *lists):
    return [fn(*a) for a in zip(*lists)]


def _interleave(chains):
    chains = list(chains)
    while chains:
        for ch in list(chains):
            try:
                next(ch)
            except StopIteration:
                chains.remove(ch)


def _gdn_chunk_common(q_ref, k_ref, v_ref, g_ref, b_ref):
    c = DN_CHUNK
    incl, strict = _gdn_consts()
    sls = [(slice(c * sub, c * (sub + 1)), slice(DN_DIM * h, DN_DIM * (h + 1)))
           for sub in range(GDN_SUB_CHUNKS) for h in range(DN_HEADS)]
    q = [q_ref[sl] * (1.0 / math.sqrt(DN_DIM)) for sl in sls]
    k = [k_ref[sl] for sl in sls]
    v = [v_ref[sl] for sl in sls]
    g = [g_ref[sl] for sl in sls]
    beta = [b_ref[sl] for sl in sls]
    gc = [_cumsum_rows(x) for x in g]
    grow = [x.T[:c, :] for x in gc]
    kb = _each(jnp.multiply, k, beta)
    kk = _each(_nt, kb, k)
    qk = _each(_nt, q, k)
    gam = [jnp.exp(x) for x in gc]
    g_last = [_rowsum(x) for x in g]
    dm = [jnp.exp(jnp.where(incl, x[:, :c] - y, NEG)) for x, y in zip(gc, grow)]
    vb = _each(jnp.multiply, v, beta)
    kbg = _each(jnp.multiply, kb, gam)
    ek = [jnp.exp(x - y) for x, y in zip(g_last, gc)]
    kd = _each(jnp.multiply, k, ek)
    return dict(q=q, k=k, v=v, beta=beta, gc=gc, gam=gam, g_last=g_last, dm=dm, kb=kb, vb=vb,
                kbg=kbg, kk=kk, ek=ek, kd=kd, qk=qk, incl=incl, strict=strict, sls=sls)


def _gdn_fwd(q, k, v, g, beta):
    tp = q.shape[0]
    c = DN_CHUNK
    nch = tp // c

    def body(q_ref, k_ref, v_ref, g_ref, b_ref, o_ref, s_ref, t_ref, s_scr):
        @pl.when(pl.program_id(0) == 0)
        def _():
            s_scr[...] = jnp.zeros_like(s_scr)

        eye = (lax.broadcasted_iota(jnp.int32, (c, c), 0) == lax.broadcasted_iota(jnp.int32, (c, c), 1)).astype(F32)
        x = _gdn_chunk_common(q_ref, k_ref, v_ref, g_ref, b_ref)
        heads = range(DN_HEADS)
        bp = [-jnp.where(x["strict"], kk * dm, 0.0) for kk, dm in zip(x["kk"], x["dm"])]
        t = [eye + b for b in bp]
        for _ in range(5):
            bp = [_nn(b, b, hp="3x") for b in bp]
            t = [tt + _nn(tt, b, hp="3x") for tt, b in zip(t, bp)]
        u = _each(_nn, t, x["vb"])
        w = _each(_nn, t, x["kbg"])
        qg = _each(jnp.multiply, x["q"], x["gam"])
        mqk = _each(jnp.multiply, x["qk"], x["dm"])
        s = [s_scr[h] for h in heads]
        for sub in range(GDN_SUB_CHUNKS):
            e = [DN_HEADS * sub + h for h in heads]
            v_new = [u[i] - _nn(w[i], s[h]) for h, i in zip(heads, e)]
            o = [_nn(qg[i], s[h]) + _nn(mqk[i], v_new[h]) for h, i in zip(heads, e)]
            s_new = [s[h] * jnp.exp(x["g_last"][i]) + _tn(x["kd"][i], v_new[h]) for h, i in zip(heads, e)]
            for h, i in zip(heads, e):
                s_ref[h, sub] = s[h]
                t_ref[h, sub] = t[i]
                o_ref[x["sls"][i]] = o[h]
            s = s_new
        for h in heads:
            s_scr[h] = s[h]

    sub = GDN_SUB_CHUNKS
    rb = lambda n: (n, 0)
    return pl.pallas_call(
        body, name="gdn_fwd", grid=(nch // sub,),
        in_specs=[pl.BlockSpec((sub * c, DN_WIDTH), rb)] * 5,
        out_specs=[pl.BlockSpec((sub * c, DN_WIDTH), rb),
                   pl.BlockSpec((DN_HEADS, sub, DN_DIM, DN_DIM), lambda n: (0, n, 0, 0)),
                   pl.BlockSpec((DN_HEADS, sub, c, c), lambda n: (0, n, 0, 0))],
        out_shape=[jax.ShapeDtypeStruct((tp, DN_WIDTH), F32),
                   jax.ShapeDtypeStruct((DN_HEADS, nch, DN_DIM, DN_DIM), F32),
                   jax.ShapeDtypeStruct((DN_HEADS, nch, c, c), F32)],
        scratch_shapes=[pltpu.VMEM((DN_HEADS, DN_DIM, DN_DIM), F32)],
        compiler_params=pltpu.CompilerParams(dimension_semantics=("arbitrary",), vmem_limit_bytes=VMEM_LIMIT),
    )(q, k, v, g, beta)


def _gdn_bwd(q, k, v, g, beta, s_all, t_all, do):
    tp = q.shape[0]
    c = DN_CHUNK
    nch = tp // c

    def body(q_ref, k_ref, v_ref, g_ref, b_ref, s_ref, t_ref, do_ref,
             dq_ref, dk_ref, dv_ref, dg_ref, db_ref, ds_scr):
        @pl.when(pl.program_id(0) == 0)
        def _():
            ds_scr[...] = jnp.zeros_like(ds_scr)

        xs = _gdn_chunk_common(q_ref, k_ref, v_ref, g_ref, b_ref)

        ds_state = [ds_scr[h] for h in range(DN_HEADS)]

        def chain(sub, h):
            e = DN_HEADS * sub + h
            x = {key: (val[e] if isinstance(val, list) else val) for key, val in xs.items()}
            sl = x["sls"]
            qs, kx, vx, beta_, gam, dm = x["q"], x["k"], x["v"], x["beta"], x["gam"], x["dm"]
            kb, vb, kbg, kd, ek = x["kb"], x["vb"], x["kbg"], x["kd"], x["ek"]
            t = t_ref[h, sub]
            s = s_ref[h, sub]
            dsn = ds_state[h]
            dob = do_ref[sl]
            eg_last = jnp.exp(x["g_last"])
            u = _nn(t, vb)
            w = _nn(t, kbg)
            mqk = x["qk"] * dm
            qd = qs * gam
            dqd = _nt(dob, s)
            dkd_pre = _nn(kd, dsn)
            yield
            v_new = u - _nn(w, s)
            dv_new = _tn(mqk, dob) + dkd_pre
            dq = dqd * gam
            dgam = jnp.sum(dqd * qs, axis=1, keepdims=True)
            yield
            ds_state[h] = _tn(qd, dob) + eg_last * dsn - _tn(w, dv_new)
            dmm = jnp.where(x["incl"], _nt(dob, v_new), 0.0)
            dkd = _nt(v_new, dsn)
            dw = -_nt(dv_new, s)
            dvb = _tn(t, dv_new)
            dt = _nt(dv_new, vb)
            yield
            dqk = dmm * dm
            e_mat = dmm * mqk
            dq = dq + _nn(dqk, kx)
            dk = _tn(dqk, qs) + dkd * ek
            e1 = jnp.sum(dkd * kd, axis=1, keepdims=True)
            dgc = -e1
            dg_last = jnp.sum(e1) + eg_last * jnp.sum(s * dsn)
            dt = dt + _nt(dw, kbg)
            dkbg = _tn(t, dw)
            yield
            tdt = _tn(t, dt, hp="3x")
            yield
            da = jnp.where(x["strict"], -_nt(tdt, t, hp="3x"), 0.0)
            yield
            dkk = da * dm
            e_mat = e_mat + da * x["kk"] * dm
            dkb = _nn(dkk, kx) + dkbg * gam
            dk = dk + _tn(dkk, kb)
            dgam = dgam + jnp.sum(dkbg * kb, axis=1, keepdims=True)
            yield
            dk = dk + dkb * beta_
            dbeta = jnp.sum(dkb * kx, axis=1, keepdims=True) + jnp.sum(dvb * vx, axis=1, keepdims=True)
            dv = dvb * beta_
            dgc = dgc + jnp.sum(e_mat, axis=1, keepdims=True) + dgam * gam
            dgc = dgc - jnp.sum(e_mat.T, axis=1, keepdims=True)
            yield
            dg = _cumsum_rows(dgc, reverse=True) + dg_last
            yield
            dq_ref[sl] = dq * (1.0 / math.sqrt(DN_DIM))
            dk_ref[sl] = dk
            dv_ref[sl] = dv
            dg_ref[sl] = dg
            db_ref[sl] = jnp.broadcast_to(dbeta, (c, LANE))

        chains = []
        for sub in reversed(range(GDN_SUB_CHUNKS)):
            new = [chain(sub, h) for h in range(DN_HEADS)]
            for _ in range(3):
                for ch in new:
                    next(ch)
            chains += new
        _interleave(chains)
        for h in range(DN_HEADS):
            ds_scr[h] = ds_state[h]

    nblk = nch // GDN_SUB_CHUNKS
    sub = GDN_SUB_CHUNKS
    rb = lambda n: (nblk - 1 - n, 0)
    hs = lambda n: (0, nblk - 1 - n, 0, 0)
    return pl.pallas_call(
        body, name="gdn_bwd", grid=(nblk,),
        in_specs=[pl.BlockSpec((sub * c, DN_WIDTH), rb)] * 5
        + [pl.BlockSpec((DN_HEADS, sub, DN_DIM, DN_DIM), hs), pl.BlockSpec((DN_HEADS, sub, c, c), hs),
           pl.BlockSpec((sub * c, DN_WIDTH), rb)],
        out_specs=[pl.BlockSpec((sub * c, DN_WIDTH), rb)] * 5,
        out_shape=[jax.ShapeDtypeStruct((tp, DN_WIDTH), F32)] * 5,
        scratch_shapes=[pltpu.VMEM((DN_HEADS, DN_DIM, DN_DIM), F32)],
        compiler_params=pltpu.CompilerParams(dimension_semantics=("arbitrary",), vmem_limit_bytes=VMEM_LIMIT),
    )(q, k, v, g, beta, s_all, t_all, do)


def _silu_parts(x):
    s = _sigmoid(x)
    return x * s, s * (1.0 + x * (1.0 - s))


def _f_rms_cast(i, x, w):
    y, _ = _rms_fwd(x, w, x.shape[1])
    return (y,), ()


def _f_rms_bwd_add(i, x, dy, dres, w, *, mask_pad):
    dx, dwr = _rms_bwd(x, w, dy, x.shape[1])
    out = dres + dx
    if mask_pad:
        out = jnp.where(_row_ids(i, x.shape[0]) >= PAD, out, 0.0)
    return (out,), (_rowsum(dwr),)


def _f_lat_norm(i, ql, kvl, qw, kvw):
    return (_rms_fwd(ql, qw, Q_LORA)[0], _rms_fwd(kvl, kvw, KV_LORA)[0]), ()


def _f_lat_norm_bwd(i, ql, kvl, dqn, dkvn, qw, kvw):
    dq, dqw = _rms_bwd(ql, qw, dqn, Q_LORA)
    dk, dkw = _rms_bwd(kvl, kvw, dkvn, KV_LORA)
    return (dq, dk), (_rowsum(dqw), _rowsum(dkw))


def _rope(x, cos, sin_s):
    return x * cos + _swap_halves(x) * sin_s


def _rope_t(dy, cos, sin_s):
    return dy * cos + _swap_halves(dy * sin_s)


def _f_mla_qk(i, qf, kvf, kpe, cos, sin_s, qw, kw):
    qs, ks, vs = [], [], []
    for h in range(MLA_HEADS):
        qn, _ = _rms_fwd(qf[:, HP * h:HP * (h + 1)], qw, QK_HEAD)
        qs += [qn[:, :QK_NOPE], _rope(qn[:, QK_NOPE:], cos, sin_s)]
        kh = jnp.concatenate([kvf[:, HP * h:HP * h + QK_NOPE], kpe], axis=1)
        kn, _ = _rms_fwd(kh, kw, QK_HEAD)
        ks += [kn[:, :QK_NOPE], _rope(kn[:, QK_NOPE:], cos, sin_s)]
        vs.append(kvf[:, HP * h + QK_NOPE:HP * (h + 1)])
    return (jnp.concatenate(qs, axis=1), jnp.concatenate(ks, axis=1), jnp.concatenate(vs, axis=1)), ()


def _f_mla_front(i, ql, kvl, kpe, cos, sin_s, qaw, kvaw, wq_t, wkv, qw, kw):
    qn = _rms_fwd(ql, qaw, Q_LORA)[0].astype(_MXU)
    kvn = _rms_fwd(kvl, kvaw, KV_LORA)[0].astype(_MXU)
    qf = _nt(qn, wq_t)
    kvf = _nn(kvn, wkv)
    (q, k, v), _ = _f_mla_qk(i, qf, kvf, kpe, cos, sin_s, qw, kw)
    return (qn, kvn, qf, kvf, q, k, v), ()


def _f_mla_back(i, qf, kvf, kpe, cos, sin_s, dq, dk, dv, ql, kvl, qaw, kvaw, wq_t, wkv, qw, kw):
    (dqf, dkvf, dkpe), (dqw, dkw) = _f_mla_qk_bwd(i, qf, kvf, kpe, cos, sin_s, dq, dk, dv, qw, kw)
    dqf = dqf.astype(_MXU)
    dkvf = dkvf.astype(_MXU)
    dql, dqaw = _rms_bwd(ql, qaw, _nn(dqf, wq_t), Q_LORA)
    dkvl, dkvaw = _rms_bwd(kvl, kvaw, _nt(dkvf, wkv), KV_LORA)
    return (dqf, dkvf, dkpe, dql, dkvl), (dqw, dkw, _rowsum(dqaw), _rowsum(dkvaw))


def _f_mla_qk_bwd(i, qf, kvf, kpe, cos, sin_s, dq, dk, dv, qw, kw):
    dqf, dkvf = [], []
    dkpe = None
    dqw = None
    dkw = None
    for h in range(MLA_HEADS):
        dqh = dq[:, HP * h:HP * (h + 1)]
        dqn = jnp.concatenate([dqh[:, :QK_NOPE], _rope_t(dqh[:, QK_NOPE:], cos, sin_s)], axis=1)
        dx, dwr = _rms_bwd(qf[:, HP * h:HP * (h + 1)], qw, dqn, QK_HEAD)
        dqf.append(dx)
        dqw = _rowsum(dwr) if dqw is None else dqw + _rowsum(dwr)
        dkh = dk[:, HP * h:HP * (h + 1)]
        dkn = jnp.concatenate([dkh[:, :QK_NOPE], _rope_t(dkh[:, QK_NOPE:], cos, sin_s)], axis=1)
        kh = jnp.concatenate([kvf[:, HP * h:HP * h + QK_NOPE], kpe], axis=1)
        dx, dwr = _rms_bwd(kh, kw, dkn, QK_HEAD)
        dkvf += [dx[:, :QK_NOPE], dv[:, V_HEAD * h:V_HEAD * (h + 1)]]
        dkpe = dx[:, QK_NOPE:] if dkpe is None else dkpe + dx[:, QK_NOPE:]
        dkw = _rowsum(dwr) if dkw is None else dkw + _rowsum(dwr)
    return (jnp.concatenate(dqf, axis=1), jnp.concatenate(dkvf, axis=1), dkpe), (dqw, dkw)


def _gdn_act(i, x, halo, w8):
    tm = x.shape[0]
    halo = jnp.where(i > 0, halo, 0.0)
    c = _conv_fwd(x, halo, w8, DN_CONV)
    act, dact = _silu_parts(c)
    return act, dact


def _spread_heads(ab):
    tm = ab.shape[0]
    return jnp.concatenate([jnp.broadcast_to(ab[:, h:h + 1], (tm, DN_DIM)) for h in range(2 * DN_HEADS)], axis=1)


def _gather_heads(x):
    tm = x.shape[0]
    lane = lax.broadcasted_iota(jnp.int32, (tm, LANE), 1)
    out = jnp.zeros((tm, LANE), F32)
    for h in range(2 * DN_HEADS):
        out = out + jnp.where(lane == h, x[:, DN_DIM * h:DN_DIM * h + 1], 0.0)
    return out


def _f_gdn_prep(i, x, halo, ab, w8, alog, dtb):
    tm = x.shape[0]
    act, _ = _gdn_act(i, x, halo, w8)
    outs = []
    for part in range(2):
        for h in range(DN_HEADS):
            t = act[:, DN_WIDTH * part + DN_DIM * h:DN_WIDTH * part + DN_DIM * (h + 1)]
            outs.append(t * lax.rsqrt(jnp.sum(t * t, axis=-1, keepdims=True) + EPS))
    q = jnp.concatenate(outs[:DN_HEADS], axis=1)
    k = jnp.concatenate(outs[DN_HEADS:], axis=1)
    v = act[:, 2 * DN_WIDTH:]
    abb = _spread_heads(ab)
    valid = _row_ids(i, tm) >= PAD
    g = jnp.where(valid, -jnp.exp(alog) * _softplus(abb[:, :DN_WIDTH] + dtb), 0.0)
    beta = jnp.where(valid, _sigmoid(abb[:, DN_WIDTH:]), 0.0)
    return (q, k, v, g, beta), ()


def _f_gdn_prep_bwd(i, x, x_prev, x_next, ab, dq, dq_next, dk, dk_next, dv, dv_next, dg, dbeta,
                    w8, alog, dtb, *, nt):
    tm = x.shape[0]
    x_prev = jnp.where(i > 0, x_prev, 0.0)
    more = i < nt - 1
    ext = lambda t, t_next: jnp.concatenate([t, jnp.where(more, t_next, 0.0)], axis=0)
    c = _conv_fwd(jnp.concatenate([x, x_next], axis=0), x_prev, w8, DN_CONV)
    act, dact = _silu_parts(c)
    douts = []
    for part, dd in enumerate((ext(dq, dq_next), ext(dk, dk_next))):
        for h in range(DN_HEADS):
            t = act[:, DN_WIDTH * part + DN_DIM * h:DN_WIDTH * part + DN_DIM * (h + 1)]
            r = lax.rsqrt(jnp.sum(t * t, axis=-1, keepdims=True) + EPS)
            y = t * r
            dy = dd[:, DN_DIM * h:DN_DIM * (h + 1)]
            douts.append(r * (dy - y * jnp.sum(dy * y, axis=-1, keepdims=True)))
    douts.append(ext(dv, dv_next))
    dc = jnp.concatenate(douts, axis=1) * dact
    dqkv = _conv_bwd_x(dc[:tm], dc[tm:], w8, DN_CONV)
    dconv_w = _conv_bwd_w(dc[:tm], x, x_prev, DN_CONV)
    abb = _spread_heads(ab)
    valid = _row_ids(i, tm) >= PAD
    pre = abb[:, :DN_WIDTH] + dtb
    ea = jnp.exp(alog)
    g = -ea * _softplus(pre)
    dg = jnp.where(valid, dg, 0.0)
    dbeta = jnp.where(valid, dbeta, 0.0)
    da = dg * (-ea) * _sigmoid(pre)
    beta = _sigmoid(abb[:, DN_WIDTH:])
    db = dbeta * beta * (1.0 - beta)
    dab = _gather_heads(jnp.concatenate([da, db], axis=1))
    return (dqkv, dab), (dconv_w, _rowsum(dg * g), _rowsum(da))


def _f_conv_bwd(i, dy, dy_next, x, x_prev, w8, *, width, nt):
    dy_next = jnp.where(i < nt - 1, dy_next, 0.0)
    x_prev = jnp.where(i > 0, x_prev, 0.0)
    return (_conv_bwd_x(dy, dy_next, w8, width),), (_conv_bwd_w(dy, x, x_prev, width),)


def _f_mix(i, o_mla, o_dn, z, w_mla, w_dn):
    tm = o_mla.shape[0]
    valid = _row_ids(i, tm) >= PAD
    outs = []
    for h in range(MLA_HEADS):
        y, _ = _rms_fwd(o_mla[:, V_HEAD * h:V_HEAD * (h + 1)], w_mla, V_HEAD)
        outs.append(jnp.where(valid, y, 0.0))
    for h in range(DN_HEADS):
        y, _ = _rms_fwd(o_dn[:, DN_DIM * h:DN_DIM * (h + 1)], w_dn, DN_DIM)
        outs.append(y * _silu_parts(z[:, DN_DIM * h:DN_DIM * (h + 1)])[0])
    return (jnp.concatenate(outs, axis=1),), ()


def _f_mix_bwd(i, o_mla, o_dn, z, dy_mla, dy_dn, w_mla, w_dn):
    tm = o_mla.shape[0]
    valid = _row_ids(i, tm) >= PAD
    d_mla, d_dn, d_z = [], [], []
    dw_mla = None
    dw_dn = None
    for h in range(MLA_HEADS):
        sl = slice(V_HEAD * h, V_HEAD * (h + 1))
        dx, dwr = _rms_bwd(o_mla[:, sl], w_mla, jnp.where(valid, dy_mla[:, sl], 0.0), V_HEAD)
        d_mla.append(dx)
        dw_mla = _rowsum(dwr) if dw_mla is None else dw_mla + _rowsum(dwr)
    for h in range(DN_HEADS):
        sl = slice(DN_DIM * h, DN_DIM * (h + 1))
        y, _ = _rms_fwd(o_dn[:, sl], w_dn, DN_DIM)
        sz, dsz = _silu_parts(z[:, sl])
        d_z.append(dy_dn[:, sl] * y * dsz)
        dx, dwr = _rms_bwd(o_dn[:, sl], w_dn, dy_dn[:, sl] * sz, DN_DIM)
        d_dn.append(dx)
        dw_dn = _rowsum(dwr) if dw_dn is None else dw_dn + _rowsum(dwr)
    return ((jnp.concatenate(d_mla, axis=1), jnp.concatenate(d_dn, axis=1), jnp.concatenate(d_z, axis=1)),
            (dw_mla, dw_dn))


def _f_ffn_act(i, gate_pre, halo, up, w8, b):
    halo = jnp.where(i > 0, halo, 0.0)
    gate = _conv_fwd(gate_pre, halo, w8, FFN_CONV) + b
    return (_silu_parts(gate)[0] * up,), ()


def _f_ffn_act_bwd(i, gp, gp_prev, gp_next, up, up_next, dact, dact_next, w8, b, *, nt):
    tm = gp.shape[0]
    gp_prev = jnp.where(i > 0, gp_prev, 0.0)
    dact_next = jnp.where(i < nt - 1, dact_next, 0.0)
    cat = lambda t, t_next: jnp.concatenate([t, t_next], axis=0)
    gate = _conv_fwd(cat(gp, gp_next), gp_prev, w8, FFN_CONV) + b
    sg, dsg = _silu_parts(gate)
    dact_e = cat(dact, dact_next)
    dgate = dact_e * cat(up, up_next) * dsg
    dgate_pre = _conv_bwd_x(dgate[:tm], dgate[tm:], w8, FFN_CONV)
    dup = dact * sg[:tm]
    return (dgate_pre, dup), (_conv_bwd_w(dgate[:tm], gp, gp_prev, FFN_CONV), _rowsum(dgate[:tm]))


def _f_loss(i, h3, tgt):
    tm = h3.shape[0]
    diff = jnp.where(_row_ids(i, tm) >= ROW0, h3 - tgt, 0.0)
    part = 0.5 * jnp.sum(diff * diff) * (1.0 / D_MODEL)
    return (diff * (1.0 / D_MODEL),), (jnp.full((1, LANE), part, F32),)


def _after(fn):
    return lambda i, *a: fn(i, *a[:-1])


def _local_step(h0, tgt, w, token, late_weights, grads_ready):
    tp = h0.shape[0]
    nt = tp // TM
    bf = (D_MODEL, _MXU)
    proj, u = _norm_mm("in_proj", h0, w["attn_norm_w"], w["w_in"], after=token)
    p_qkv = lambda kind="cur": _In(proj, 3 * DN_WIDTH, 0, kind)
    p_z = _In(proj, DN_WIDTH, C_Z // DN_WIDTH)
    p_ql = _In(proj, Q_LORA, C_QL // Q_LORA)
    p_kvl = _In(proj, KV_LORA, C_KVL // KV_LORA)
    p_kpe = _In(proj, LANE, C_KPE // LANE)
    p_ab = _In(proj, LANE, C_AB // LANE)
    cos, sin_s = _In(w["cos"]), _In(w["sin_s"])

    mla_w = [w["q_a_norm_w"], w["kv_a_norm_w"], w["w_q_b"], w["w_kv_b"], w["q_norm_w"], w["k_norm_w"]]
    tm_mla = _pick(tp, 288, 16)
    wide = MLA_HEADS * HP
    qn, kvn, qf, kvf, q, k, v = _rows(
        "mla_front", _f_mla_front, [p_ql, p_kvl, p_kpe, cos, sin_s], mla_w,
        [(Q_LORA, _MXU), (KV_LORA, _MXU), (wide, F32), (wide, F32), (wide, _MXU), (wide, _MXU),
         (MLA_HEADS * V_HEAD, _MXU)], tm=tm_mla)
    o_mla = _attn_fwd(q, k, v)

    dn_w = [w["dn_conv_w"], w["alog_b"], w["dtb_b"]]
    gq, gk, gv, gg, gb = _rows("gdn_prep", _f_gdn_prep, [p_qkv(), p_qkv("prev"), p_ab], dn_w,
                               [(DN_WIDTH, F32)] * 5)
    o_dn, s_all, t_all = _gdn_fwd(gq, gk, gv, gg, gb)

    out_w = [w["mla_out_norm_w"], w["dn_out_norm_w"]]
    w = dict(w, **late_weights(o_dn))
    h2, mixed = _pro_mm("mix_out_proj", lambda i, *t: _f_mix(i, *t)[0][0], [_In(o_mla), _In(o_dn), p_z], out_w,
                        D_MODEL, w["w_out"], h0)

    ffn_w = [w["ffn_conv_w"], w["ffn_conv_b"]]
    hn, gate_pre, up, act = _ffn_in(h2, w["ffn_norm_w"], w["w_gate"], w["w_up"], *ffn_w)
    dh3, loss = _mm_rows("ffn_down_loss", act, w["w_down"], "nn", lambda i, y, r, t: _f_loss(i, r + y, t),
                         [_In(h2), _In(tgt)], [], [(D_MODEL, F32)], [(1, LANE)])

    g = {}
    dact = _mm("ffn_down_dx", dh3, w["w_down"], "nt")
    g["w_down"] = _mm("ffn_down_dw", act, dh3, "tn", out_dtype=_MXU)
    dgate_pre, dup, g["ffn_conv_w"], g["ffn_conv_b"] = _rows(
        "ffn_act_bwd", functools.partial(_f_ffn_act_bwd, nt=nt),
        [_In(gate_pre), _In(gate_pre, kind="prev"), _In(gate_pre, kind="next"), _In(up), _In(up, kind="next"),
         _In(dact), _In(dact, kind="next")], ffn_w,
        [(D_FF, _MXU), (D_FF, _MXU)], [(8, D_FF), (1, D_FF)])
    g["w_gate"], g["w_up"] = _mm_tn2("ffn_gate_up_dw", dgate_pre, dup, hn, out_dtype=_MXU)
    tok = grads_ready(g, ("w_down", "w_gate", "w_up"))
    dh2, g["ffn_norm_w"] = _mm_rows(
        "ffn_gate_up_dx_rms", [dgate_pre, dup], [w["w_gate"], w["w_up"]], "nn",
        lambda i, dy, x, dres, nw, _tok: _f_rms_bwd_add(i, x, dy, dres, nw, mask_pad=True),
        [_In(h2), _In(dh3)], [w["ffn_norm_w"], tok], [(D_MODEL, F32)], [(1, D_MODEL)])

    g["w_out"] = _mm("out_proj_dw", mixed, dh2, "tn", out_dtype=_MXU)
    half = MLA_HEADS * V_HEAD
    do_mla, do_dn, dz, g["mla_out_norm_w"], g["dn_out_norm_w"] = _mm_rows(
        "out_proj_dx_mix", dh2, w["w_out"], "nt",
        lambda i, dm, om, od, z, wm, wd: _f_mix_bwd(i, om, od, z, dm[:, :half], dm[:, half:], wm, wd),
        [_In(o_mla), _In(o_dn), p_z], out_w,
        [(half, F32), (DN_WIDTH, F32), (DN_WIDTH, _MXU)], [(1, V_HEAD), (1, DN_DIM)])

    dq, dk, dv = _attn_bwd(q, k, v, do_mla)
    dqf, dkvf, dkpe, dql, dkvl, g["q_norm_w"], g["k_norm_w"], g["q_a_norm_w"], g["kv_a_norm_w"] = _rows(
        "mla_back", _f_mla_back,
        [_In(qf), _In(kvf), p_kpe, cos, sin_s, _In(dq), _In(dk), _In(dv), p_ql, p_kvl], mla_w,
        [(wide, _MXU), (wide, _MXU), (LANE, _MXU), (Q_LORA, _MXU), (KV_LORA, _MXU)],
        [(1, HP), (1, HP), (1, Q_LORA), (1, KV_LORA)], tm=tm_mla)
    g["w_q_b"] = _mm("mla_q_b_dw", dqf, qn, "tn")
    g["w_kv_b"] = _mm("mla_kv_b_dw", kvn, dkvf, "tn")
    tok = grads_ready(g, ("w_out", "w_q_b", "w_kv_b"))

    dgq, dgk, dgv, dgg, dgb = _gdn_bwd(gq, gk, gv, gg, gb, s_all, t_all, do_dn)
    nxt = lambda a: _In(a, kind="next")
    dqkv, dab, g["dn_conv_w"], g["alog_b"], g["dtb_b"] = _rows(
        "gdn_prep_bwd", _after(functools.partial(_f_gdn_prep_bwd, nt=nt)),
        [p_qkv(), p_qkv("prev"), p_qkv("next"), p_ab, _In(dgq), nxt(dgq), _In(dgk), nxt(dgk), _In(dgv), nxt(dgv),
         _In(dgg), _In(dgb)], dn_w + [tok],
        [(3 * DN_WIDTH, _MXU), (LANE, _MXU)], [(8, 3 * DN_WIDTH), (1, DN_WIDTH), (1, DN_WIDTH)])

    dproj = jnp.concatenate([dqkv, dz, dql, dkvl, dkpe, dab], axis=1)
    g["w_in"] = _mm("in_proj_dw", dproj, u, "tn", out_dtype=_MXU)
    tok = grads_ready(g, ("w_in",))
    dh0, g["attn_norm_w"] = _mm_rows(
        "in_proj_dx_rms", dproj, w["w_in"], "nn",
        lambda i, du, x, dres, nw, _tok: _f_rms_bwd_add(i, x, du, dres, nw, mask_pad=False),
        [_In(h0), _In(dh2)], [w["attn_norm_w"], tok], [(D_MODEL, F32)], [(1, D_MODEL)])
    return loss, dh0, g


def _w_in_to_padded(w):
    c1, c2, c3 = Q_LORA, Q_LORA + KV_LORA, Q_LORA + KV_LORA + QK_ROPE
    c4 = c3 + 3 * DN_WIDTH
    c5 = c4 + DN_WIDTH
    z = lambda n: jnp.zeros((n, w.shape[1]), w.dtype)
    return jnp.concatenate([w[c3:c4], w[c4:c5], w[:c1], w[c1:c2], w[c2:c3], z(LANE - QK_ROPE),
                            w[c5:], z(LANE - 2 * DN_HEADS)], axis=0)


def _w_in_from_padded(g):
    return jnp.concatenate([g[C_QL:C_QL + Q_LORA], g[C_KVL:C_KVL + KV_LORA], g[C_KPE:C_KPE + QK_ROPE],
                            g[:C_Z + DN_WIDTH], g[C_AB:C_AB + 2 * DN_HEADS]], axis=0)


def _w_q_b_to_padded(w):
    r = w.shape[1]
    w = w.reshape(MLA_HEADS, QK_HEAD, r)
    return jnp.pad(w, ((0, 0), (0, HP - QK_HEAD), (0, 0))).reshape(MLA_HEADS * HP, r)


def _w_q_b_from_padded(g):
    r = g.shape[1]
    return g.reshape(MLA_HEADS, HP, r)[:, :QK_HEAD].reshape(MLA_HEADS * QK_HEAD, r)


def _pad_rows8(w):
    return jnp.pad(w, ((0, 8 - w.shape[0]), (0, 0)))


def _prepare(full, tp):
    w = {}
    mx = lambda a: a.astype(_MXU)
    w["attn_norm_w"] = full["attn_norm_w"]
    w["w_in"] = mx(_w_in_to_padded(full["w_in"]))
    w["q_a_norm_w"] = full["q_a_norm_w"]
    w["kv_a_norm_w"] = full["kv_a_norm_w"]
    w["w_q_b"] = mx(_w_q_b_to_padded(full["w_q_b"]))
    w["w_kv_b"] = mx(full["w_kv_b"])
    w["q_norm_w"] = jnp.pad(full["q_norm_w"], ((0, 0), (0, HP - QK_HEAD)))
    w["k_norm_w"] = jnp.pad(full["k_norm_w"], ((0, 0), (0, HP - QK_HEAD)))
    w["mla_out_norm_w"] = full["mla_out_norm_w"]
    w["dn_out_norm_w"] = full["dn_out_norm_w"]
    w["dn_conv_w"] = _pad_rows8(full["dn_conv_w"])
    w["alog_b"] = jnp.repeat(full["dn_A_log"], DN_DIM, axis=1)
    w["dtb_b"] = jnp.repeat(full["dn_dt_bias"], DN_DIM, axis=1)
    w["ffn_norm_w"] = full["ffn_norm_w"]
    w["ffn_conv_w"] = _pad_rows8(full["ffn_conv_w"])
    w["ffn_conv_b"] = full["ffn_conv_b"]
    for n in _LATE:
        if n in full:
            w[n] = mx(full[n])
    half = QK_ROPE // 2
    inv = ROPE_THETA ** (-jnp.arange(half, dtype=F32) / half)
    ang = (jnp.arange(tp, dtype=jnp.int32) - PAD).astype(F32)[:, None] * inv[None, :]
    zc = jnp.zeros((tp, LANE - QK_ROPE), F32)
    w["cos"] = jnp.concatenate([jnp.cos(ang), jnp.cos(ang), zc], axis=1)
    w["sin_s"] = jnp.concatenate([-jnp.sin(ang), jnp.sin(ang), zc], axis=1)
    return w


def _grads_to_natural(g):
    convert = {
        "w_in": ("w_in", _w_in_from_padded),
        "w_q_b": ("w_q_b", _w_q_b_from_padded),
        "q_norm_w": ("q_norm_w", lambda a: a[:, :QK_HEAD]),
        "k_norm_w": ("k_norm_w", lambda a: a[:, :QK_HEAD]),
        "dn_conv_w": ("dn_conv_w", lambda a: a[:DN_CONV]),
        "ffn_conv_w": ("ffn_conv_w", lambda a: a[:FFN_CONV]),
        "alog_b": ("dn_A_log", lambda a: a[:, ::DN_DIM]),
        "dtb_b": ("dn_dt_bias", lambda a: a[:, ::DN_DIM]),
    }
    n = {}
    for key, a in g.items():
        name, fn = convert.get(key, (key, lambda t: t))
        n[name] = fn(a)
    return n


_MESH = pl.DeviceIdType.MESH
_ANY = pl.BlockSpec(memory_space=pl.ANY)
_CHIP_FLIPS = ((1, 0), (0, 1), (1, 1))


def _me():
    return lax.axis_index("x"), lax.axis_index("y"), lax.axis_index("c")


def _all_gather(name, blk):
    def body(x_ref, out_ref, send_sems, recv_sems, local_sem):
        x, y, c = _me()
        me, sib = (x, y, c), (x, y, 1 - c)
        chips = [(x ^ fx, y ^ fy) for fx, fy in _CHIP_FLIPS]

        def slot(p):
            return out_ref.at[4 * p[0] + 2 * p[1] + p[2]]

        def copy(k, block, to, src=None):
            return pltpu.make_async_remote_copy(
                src_ref=slot(block) if src is None else src, dst_ref=slot(block),
                send_sem=send_sems.at[k], recv_sem=recv_sems.at[k], device_id=to, device_id_type=_MESH)

        mine = pltpu.make_async_copy(x_ref, slot(me), local_sem)
        mine.start()
        first = [copy(0, me, sib, src=x_ref)]
        first += [copy(1 + j, me, (*chip, c), src=x_ref) for j, chip in enumerate(chips)]
        for cp in first:
            cp.start()
        passed = [copy(4 + j, (*chip, c), sib) for j, chip in enumerate(chips)]
        for j, chip in enumerate(chips):
            copy(1 + j, (*chip, c), me).wait_recv()
            passed[j].start()
        copy(0, sib, me).wait_recv()
        for j, chip in enumerate(chips):
            copy(4 + j, (*chip, 1 - c), me).wait_recv()
        for cp in first + passed:
            cp.wait_send()
        mine.wait()

    return pl.pallas_call(
        body, name=name, in_specs=[_ANY], out_specs=_ANY,
        out_shape=jax.ShapeDtypeStruct((N_DEV,) + blk.shape, blk.dtype),
        scratch_shapes=[pltpu.SemaphoreType.DMA((7,)), pltpu.SemaphoreType.DMA((7,)), pltpu.SemaphoreType.DMA],
    )(blk)


def _rs_sibling(name, gb):
    def body(g_ref, out_ref, send_sems, recv_sems):
        x, y, c = _me()
        cps = []
        for j in range(4):
            cp = pltpu.make_async_remote_copy(
                src_ref=g_ref.at[2 * j + (1 - c)], dst_ref=out_ref.at[j], send_sem=send_sems.at[j],
                recv_sem=recv_sems.at[j], device_id=(x, y, 1 - c), device_id_type=_MESH)
            cp.start()
            cps.append(cp)
        for cp in cps:
            cp.wait()

    return pl.pallas_call(
        body, name=name, in_specs=[_ANY], out_specs=_ANY,
        out_shape=jax.ShapeDtypeStruct((4,) + gb.shape[1:], gb.dtype),
        scratch_shapes=[pltpu.SemaphoreType.DMA((4,)), pltpu.SemaphoreType.DMA((4,))],
    )(gb)


def _rs_chips(name, s1):
    def body(s_ref, out_ref, send_sems, recv_sems):
        x, y, c = _me()
        cps = []
        for k, (fx, fy) in enumerate(_CHIP_FLIPS):
            px, py = x ^ fx, y ^ fy
            cp = pltpu.make_async_remote_copy(
                src_ref=s_ref.at[2 * px + py], dst_ref=out_ref.at[k], send_sem=send_sems.at[k],
                recv_sem=recv_sems.at[k], device_id=(px, py, c), device_id_type=_MESH)
            cp.start()
            cps.append(cp)
        for cp in cps:
            cp.wait()

    return pl.pallas_call(
        body, name=name, in_specs=[_ANY], out_specs=_ANY,
        out_shape=jax.ShapeDtypeStruct((3,) + s1.shape[1:], s1.dtype),
        scratch_shapes=[pltpu.SemaphoreType.DMA((3,)), pltpu.SemaphoreType.DMA((3,))],
    )(s1)


def _row_tile(r):
    divs = [d for d in range(16, min(r, 512) + 1, 16) if r % d == 0]
    return divs[-1] if divs else r


def _pair_sum(name, gb, recv):
    _, r, cols = gb.shape
    tm = _row_tile(r)
    c = lax.axis_index("c").astype(jnp.int32).reshape(1)

    def body(c_ref, a_ref, b_ref, o_ref, ob_ref):
        s = a_ref[...] + b_ref[...]
        o_ref[...] = s
        ob_ref[...] = s.astype(BF16)

    blk = pl.BlockSpec((1, tm, cols), lambda j, i, c_ref: (j, i, 0))
    return pl.pallas_call(
        body, name=name,
        grid_spec=pltpu.PrefetchScalarGridSpec(
            num_scalar_prefetch=1, grid=(4, r // tm),
            in_specs=[pl.BlockSpec((1, tm, cols), lambda j, i, c_ref: (2 * j + c_ref[0], i, 0)), blk],
            out_specs=[blk, blk]),
        out_shape=[jax.ShapeDtypeStruct((4, r, cols), F32), jax.ShapeDtypeStruct((4, r, cols), BF16)],
        compiler_params=pltpu.CompilerParams(dimension_semantics=("parallel", "parallel")),
    )(c, gb, recv)


def _adam_math(g, w, m, v):
    m_new = ADAM_B1 * m + (1.0 - ADAM_B1) * g
    v_new = ADAM_B2 * v + (1.0 - ADAM_B2) * (g * g)
    m_hat = m_new / (1.0 - ADAM_B1 ** ADAM_STEP)
    v_hat = v_new / (1.0 - ADAM_B2 ** ADAM_STEP)
    return -ADAM_LR * (m_hat / (jnp.sqrt(v_hat) + ADAM_EPS) + ADAM_WD * w), m_new, v_new


def _adam_vectors(name, row, items, ws, ms, vs):
    k = len(items)

    def body(row_ref, *refs):
        w_refs, m_refs, v_refs = refs[:k], refs[k:2 * k], refs[2 * k:3 * k]
        outs = refs[3 * k:]
        for idx, (off, n, per_head) in enumerate(items):
            if per_head:
                spread = row_ref[:, off:off + DN_WIDTH]
                lane = lax.broadcasted_iota(jnp.int32, (1, LANE), 1)
                g = jnp.zeros((1, LANE), F32)
                for h in range(DN_HEADS):
                    g = g + jnp.where(lane == h, spread[:, DN_DIM * h:DN_DIM * h + 1], 0.0)
            else:
                g = row_ref[:, off:off + n]
            d, m_new, v_new = _adam_math(g, w_refs[idx][...], m_refs[idx][...], v_refs[idx][...])
            for kind, val in enumerate((g, d, m_new, v_new)):
                outs[kind * k + idx][...] = val

    shapes = [jax.ShapeDtypeStruct((1, n), F32) for _, n, _ in items]
    res = pl.pallas_call(body, name=name, out_shape=shapes * 4)(row, *ws, *ms, *vs)
    return [list(res[kind * k:(kind + 1) * k]) for kind in range(4)]


def _sum_parts(name, parts):
    _, r, cols = parts[0][0].shape
    tm = _row_tile(r)
    idx = jnp.stack([jnp.asarray(s, jnp.int32) for _, s in parts])
    n = len(parts)

    def body(idx_ref, *refs):
        g = refs[0][0].astype(F32)
        for p_ref in refs[1:n]:
            g = g + p_ref[0].astype(F32)
        refs[n][...] = g

    return pl.pallas_call(
        body, name=name,
        grid_spec=pltpu.PrefetchScalarGridSpec(
            num_scalar_prefetch=1, grid=(r // tm,),
            in_specs=[pl.BlockSpec((1, tm, cols), lambda i, idx_ref, p=p: (idx_ref[p], i, 0)) for p in range(n)],
            out_specs=pl.BlockSpec((tm, cols), lambda i, idx_ref: (i, 0))),
        out_shape=jax.ShapeDtypeStruct((r, cols), F32),
        compiler_params=pltpu.CompilerParams(dimension_semantics=("parallel",)),
    )(idx, *[a for a, _ in parts])


def _adam(name, parts, w, m, v):
    r, cols = w.shape
    tm = _row_tile(r)
    idx = jnp.stack([jnp.asarray(s, jnp.int32) for _, s in parts])
    n = len(parts)

    def body(idx_ref, *refs):
        g = refs[0][0].astype(F32)
        for p_ref in refs[1:n]:
            g = g + p_ref[0].astype(F32)
        w_ref, m_ref, v_ref, g_out, d_out, m_out, v_out = refs[n:]
        g_out[...] = g
        d_out[...], m_out[...], v_out[...] = _adam_math(g, w_ref[...], m_ref[...], v_ref[...])

    part_specs = [pl.BlockSpec((1, tm, cols), lambda i, idx_ref, p=p: (idx_ref[p], i, 0)) for p in range(n)]
    flat = pl.BlockSpec((tm, cols), lambda i, idx_ref: (i, 0))
    return pl.pallas_call(
        body, name=name,
        grid_spec=pltpu.PrefetchScalarGridSpec(
            num_scalar_prefetch=1, grid=(r // tm,), in_specs=part_specs + [flat] * 3, out_specs=[flat] * 4),
        out_shape=[jax.ShapeDtypeStruct((r, cols), F32)] * 4,
        compiler_params=pltpu.CompilerParams(dimension_semantics=("parallel",)),
    )(idx, *[a for a, _ in parts], w, m, v)


def _all_gather_many(name, blks):
    n = len(blks)

    def body(*refs):
        x_refs, out_refs = refs[:n], refs[n:2 * n]
        send_sems, recv_sems, local_sems = refs[2 * n:]
        x, y, c = _me()
        me, sib = (x, y, c), (x, y, 1 - c)
        chips = [(x ^ fx, y ^ fy) for fx, fy in _CHIP_FLIPS]

        def slot(a, p):
            return out_refs[a].at[4 * p[0] + 2 * p[1] + p[2]]

        def copy(a, k, block, to, src=None):
            return pltpu.make_async_remote_copy(
                src_ref=slot(a, block) if src is None else src, dst_ref=slot(a, block),
                send_sem=send_sems.at[7 * a + k], recv_sem=recv_sems.at[7 * a + k], device_id=to,
                device_id_type=_MESH)

        mine = [pltpu.make_async_copy(x_refs[a], slot(a, me), local_sems.at[a]) for a in range(n)]
        first = []
        for a in range(n):
            mine[a].start()
            first.append(copy(a, 0, me, sib, src=x_refs[a]))
            first += [copy(a, 1 + j, me, (*chip, c), src=x_refs[a]) for j, chip in enumerate(chips)]
        for cp in first:
            cp.start()
        passed = []
        for j, chip in enumerate(chips):
            for a in range(n):
                copy(a, 1 + j, (*chip, c), me).wait_recv()
                cp = copy(a, 4 + j, (*chip, c), sib)
                cp.start()
                passed.append(cp)
        for a in range(n):
            copy(a, 0, sib, me).wait_recv()
            for j, chip in enumerate(chips):
                copy(a, 4 + j, (*chip, 1 - c), me).wait_recv()
        for cp in first + passed:
            cp.wait_send()
        for cp in mine:
            cp.wait()

    return pl.pallas_call(
        body, name=name, in_specs=[_ANY] * n, out_specs=[_ANY] * n,
        out_shape=[jax.ShapeDtypeStruct((N_DEV,) + b.shape, b.dtype) for b in blks],
        scratch_shapes=[pltpu.SemaphoreType.DMA((7 * n,)), pltpu.SemaphoreType.DMA((7 * n,)),
                        pltpu.SemaphoreType.DMA((n,))],
    )(*blks)


def _rs_sibling_many(name, gbs):
    n = len(gbs)

    def body(*refs):
        g_refs, out_refs = refs[:n], refs[n:2 * n]
        send_sems, recv_sems = refs[2 * n:]
        x, y, c = _me()
        cps = []
        for a in range(n):
            for j in range(4):
                cp = pltpu.make_async_remote_copy(
                    src_ref=g_refs[a].at[2 * j + (1 - c)], dst_ref=out_refs[a].at[j],
                    send_sem=send_sems.at[4 * a + j], recv_sem=recv_sems.at[4 * a + j],
                    device_id=(x, y, 1 - c), device_id_type=_MESH)
                cp.start()
                cps.append(cp)
        for cp in cps:
            cp.wait()

    return pl.pallas_call(
        body, name=name, in_specs=[_ANY] * n, out_specs=[_ANY] * n,
        out_shape=[jax.ShapeDtypeStruct((4,) + g.shape[1:], g.dtype) for g in gbs],
        scratch_shapes=[pltpu.SemaphoreType.DMA((4 * n,)), pltpu.SemaphoreType.DMA((4 * n,))],
    )(*gbs)


def _rs_chips_many(name, s1s):
    n = len(s1s)

    def body(*refs):
        s_refs, out_refs = refs[:n], refs[n:2 * n]
        send_sems, recv_sems = refs[2 * n:]
        x, y, c = _me()
        cps = []
        for a in range(n):
            for k, (fx, fy) in enumerate(_CHIP_FLIPS):
                px, py = x ^ fx, y ^ fy
                cp = pltpu.make_async_remote_copy(
                    src_ref=s_refs[a].at[2 * px + py], dst_ref=out_refs[a].at[k],
                    send_sem=send_sems.at[3 * a + k], recv_sem=recv_sems.at[3 * a + k],
                    device_id=(px, py, c), device_id_type=_MESH)
                cp.start()
                cps.append(cp)
        for cp in cps:
            cp.wait()

    return pl.pallas_call(
        body, name=name, in_specs=[_ANY] * n, out_specs=[_ANY] * n,
        out_shape=[jax.ShapeDtypeStruct((3,) + s.shape[1:], s.dtype) for s in s1s],
        scratch_shapes=[pltpu.SemaphoreType.DMA((3 * n,)), pltpu.SemaphoreType.DMA((3 * n,))],
    )(*s1s)


_HBM = pl.BlockSpec(memory_space=pltpu.HBM)
_SEM = pl.BlockSpec(memory_space=pltpu.SEMAPHORE)
_EFFECT = pltpu.SideEffectType.DATAFLOW_SIDE_EFFECTING


def _push_copies(src_refs, land_refs, send_sems, recv_sems, src_by_peer):
    x, y, c = _me()
    my_id = 4 * x + 2 * y + c
    out = []
    for a in range(len(src_refs)):
        for f in range(1, N_DEV):
            px, py, pc = x ^ (f >> 2), y ^ ((f >> 1) & 1), c ^ (f & 1)
            pid = 4 * px + 2 * py + pc
            src = src_refs[a].at[pid] if src_by_peer else src_refs[a]
            start = pltpu.make_async_remote_copy(
                src_ref=src, dst_ref=land_refs[a].at[my_id], send_sem=send_sems.at[7 * a + f - 1],
                recv_sem=recv_sems.at[7 * a + f - 1], device_id=(px, py, pc), device_id_type=_MESH)
            landed = pltpu.make_async_remote_copy(
                src_ref=src, dst_ref=land_refs[a].at[pid], send_sem=send_sems.at[7 * a + f - 1],
                recv_sem=recv_sems.at[7 * a + f - 1], device_id=(px, py, pc), device_id_type=_MESH)
            out.append((start, landed))
    return out


def _push_start(name, srcs, src_by_peer, after):
    n = len(srcs)
    lands = [jax.ShapeDtypeStruct((N_DEV,) + (s.shape[1:] if src_by_peer else s.shape), s.dtype) for s in srcs]

    def body(*refs):
        src_refs, land_refs = refs[:n], refs[n:2 * n]
        send_sems, recv_sems = refs[2 * n + 1], refs[2 * n + 2]
        token = refs[-1]
        for start, _ in _push_copies(src_refs, land_refs, send_sems, recv_sems, src_by_peer):
            start.start()
        token[...] = jnp.zeros_like(token)

    hbm = lambda a: pltpu.with_memory_space_constraint(a, pltpu.HBM)
    res = pl.pallas_call(
        body, name=name,
        out_shape=(pltpu.SemaphoreType.DMA((7 * n,)), pltpu.SemaphoreType.DMA((7 * n,)),
                   *[pltpu.HBM(s.shape, s.dtype) for s in srcs], *[pltpu.HBM(s.shape, s.dtype) for s in lands],
                   jax.ShapeDtypeStruct((8, LANE), F32)),
        in_specs=[_HBM] * (2 * n) + [_ANY],
        out_specs=(_SEM, _SEM, *[_HBM] * (2 * n), pl.BlockSpec(memory_space=pltpu.VMEM)),
        input_output_aliases={i: 2 + i for i in range(2 * n)},
        compiler_params=pltpu.CompilerParams(has_side_effects=_EFFECT),
    )(*[hbm(s) for s in srcs], *[hbm(lax.empty(s.shape, s.dtype)) for s in lands], after)
    return res[0], res[1], list(res[2:2 + n]), list(res[2 + n:2 + 2 * n]), res[-1]


def _push_wait(name, send_sems, recv_sems, srcs, lands, src_by_peer, after):
    n = len(srcs)

    def body(*refs):
        src_refs, land_refs = refs[:n], refs[n:2 * n]
        s_sems, r_sems = refs[2 * n], refs[2 * n + 1]
        for _, landed in _push_copies(src_refs, land_refs, s_sems, r_sems, src_by_peer):
            landed.wait_send()
            landed.wait_recv()

    res = pl.pallas_call(
        body, name=name,
        out_shape=tuple(pltpu.HBM(s.shape, s.dtype) for s in list(srcs) + list(lands)),
        in_specs=[_HBM] * (2 * n) + [_SEM, _SEM, _ANY],
        out_specs=tuple([_HBM] * (2 * n)),
        input_output_aliases={i: i for i in range(2 * n)},
        compiler_params=pltpu.CompilerParams(has_side_effects=_EFFECT),
    )(*srcs, *lands, send_sems, recv_sems, after)
    return list(res[:n]), list(res[n:])


_SHARDED = (
    ("meta_tokens", 1, (N_META, D_MODEL)),
    ("w_in", 1, (D_MODEL, IN_COLS)),
    ("w_q_b", 1, (Q_LORA, MLA_HEADS * QK_HEAD)),
    ("w_kv_b", 1, (KV_LORA, MLA_HEADS * (QK_NOPE + V_HEAD))),
    ("dn_conv_w", 1, (DN_CONV, 3 * DN_WIDTH)),
    ("w_out", 0, (2 * DN_WIDTH, D_MODEL)),
    ("w_gate", 1, (D_MODEL, D_FF)),
    ("w_up", 1, (D_MODEL, D_FF)),
    ("ffn_conv_w", 1, (FFN_CONV, D_FF)),
    ("w_down", 0, (D_FF, D_MODEL)),
)
_MXU_GATHERED = ("w_in", "w_q_b", "w_kv_b", "w_out", "w_gate", "w_up", "w_down")
_F32_GATHERED = ("meta_tokens", "dn_conv_w", "ffn_conv_w")
_EARLY = ("w_in", "w_q_b", "w_kv_b")
_LATE = ("w_out", "w_gate", "w_up", "w_down")
_TRANSPOSED = ("w_in", "w_q_b", "w_gate", "w_up")
_REPLICATED = (
    ("attn_norm_w", D_MODEL), ("q_a_norm_w", Q_LORA), ("kv_a_norm_w", KV_LORA), ("q_norm_w", QK_HEAD),
    ("k_norm_w", QK_HEAD), ("mla_out_norm_w", V_HEAD), ("dn_A_log", DN_HEADS), ("dn_dt_bias", DN_HEADS),
    ("dn_out_norm_w", DN_DIM), ("ffn_norm_w", D_MODEL), ("ffn_conv_b", D_FF),
)
_PACK_COLS = 1024
_PACK_ROW_MULT = 320
_SMALL_SHAPE = (8, 768)
_SMALL_BLOCK = (8, 512)


def _local_shape(dim, shape):
    return (shape[0] // N_DEV, shape[1]) if dim == 0 else (shape[0], shape[1] // N_DEV)


def _pack_rows(n, mult):
    rows = -(-n // _PACK_COLS)
    return -(-rows // mult) * mult


def _pack(flats, mult, axis=0):
    cat = jnp.concatenate(flats, axis=-1)
    n = cat.shape[-1]
    r = _pack_rows(n, mult)
    pad = [(0, 0)] * (cat.ndim - 1) + [(0, r * _PACK_COLS - n)]
    return jnp.pad(cat, pad).reshape(cat.shape[:-1] + (r, _PACK_COLS))


def _to_blocks(full, dim):
    r, c = full.shape
    if dim == 0:
        return full.reshape(N_DEV, (r // N_DEV) * c)
    return full.reshape(r, N_DEV, c // N_DEV).transpose(1, 0, 2).reshape(N_DEV, r * (c // N_DEV))


def _from_blocks(blocks, dim, shape):
    r, c = shape
    if dim == 0:
        return blocks.reshape(r, c)
    return blocks.reshape(N_DEV, r, c // N_DEV).transpose(1, 0, 2).reshape(r, c)


def _split(flat, sizes):
    out, o = [], 0
    for s in sizes:
        out.append(flat[..., o:o + s])
        o += s
    return out


def _gather_weights(local, names, dtype, mult):
    specs = [s for s in _SHARDED if s[0] in names]
    pack = _pack([local[n].astype(dtype).reshape(-1) for n, _, _ in specs], mult)
    got = _all_gather("gather_" + "_".join(n[:5] for n in names[:2]), pack)
    flat = got.reshape(N_DEV, -1)
    sizes = [math.prod(_local_shape(d, s)) for _, d, s in specs]
    return {n: _from_blocks(p, d, s) for (n, d, s), p in zip(specs, _split(flat, sizes))}


def kernel(x, meta_tokens, attn_norm_w, w_in, q_a_norm_w, w_q_b, kv_a_norm_w, w_kv_b, q_norm_w, k_norm_w, mla_out_norm_w, dn_conv_w, dn_A_log, dn_dt_bias, dn_out_norm_w, w_out, ffn_norm_w, w_gate, w_up, ffn_conv_w, ffn_conv_b, w_down, loss_target, m_meta_tokens, m_attn_norm_w, m_w_in, m_q_a_norm_w, m_w_q_b, m_kv_a_norm_w, m_w_kv_b, m_q_norm_w, m_k_norm_w, m_mla_out_norm_w, m_dn_conv_w, m_dn_A_log, m_dn_dt_bias, m_dn_out_norm_w, m_w_out, m_ffn_norm_w, m_w_gate, m_w_up, m_ffn_conv_w, m_ffn_conv_b, m_w_down, v_meta_tokens, v_attn_norm_w, v_w_in, v_q_a_norm_w, v_w_q_b, v_kv_a_norm_w, v_w_kv_b, v_q_norm_w, v_k_norm_w, v_mla_out_norm_w, v_dn_conv_w, v_dn_A_log, v_dn_dt_bias, v_dn_out_norm_w, v_w_out, v_ffn_norm_w, v_w_gate, v_w_up, v_ffn_conv_w, v_ffn_conv_b, v_w_down):
    names = [n for n, _, _ in _SHARDED] + [n for n, _ in _REPLICATED]
    given = dict(locals())
    two_d = lambda a: a.reshape(a.shape[-2:])
    view = lambda a, n: two_d(a).T if n in _TRANSPOSED else two_d(a)
    wl = {n: view(given[n], n) for n in names}
    ml = {n: view(given["m_" + n], n) for n in names}
    vl = {n: view(given["v_" + n], n) for n in names}
    out_shapes = {n: given[n].shape for n in names}

    spec = {n: (d, s) for n, d, s in _SHARDED}
    small_sizes = [math.prod(_local_shape(*spec[n])) for n in _F32_GATHERED]

    def small_block(d):
        cat = jnp.concatenate([d[n].reshape(d[n].shape[:-2] + (-1,)) for n in _F32_GATHERED], axis=-1)
        pad = [(0, 0)] * (cat.ndim - 1) + [(0, math.prod(_SMALL_BLOCK) - cat.shape[-1])]
        return jnp.pad(cat, pad).reshape(cat.shape[:-1] + _SMALL_BLOCK)

    def shard(n):
        return wl[n].astype(_MXU)

    def from_slots(n, blocks):
        d, s = spec[n]
        if d == 0 or n in _TRANSPOSED:
            return blocks.reshape(-1, blocks.shape[-1])
        return blocks.transpose(1, 0, 2).reshape(s)

    my_id = 4 * lax.axis_index("x") + 2 * lax.axis_index("y") + lax.axis_index("c")
    got = _all_gather_many("gather_early", [shard(n) for n in _EARLY] + [small_block(wl)])
    full = {n: a for n, a in wl.items() if n not in _LATE}
    for n, blocks in zip(_EARLY, got):
        full[n] = from_slots(n, blocks)
    for n, p in zip(_F32_GATHERED, _split(got[-1].reshape(N_DEV, -1), small_sizes)):
        full[n] = _from_blocks(p, *spec[n])
    late_own = [shard(n) for n in _LATE]
    l_send, l_recv, l_src, l_land, token = _push_start("gather_late_start", late_own, False, got[-1])

    def late_weights(after):
        _, lands = _push_wait("gather_late_wait", l_send, l_recv, l_src, l_land, False, after)
        out = {}
        for n, land, own in zip(_LATE, lands, late_own):
            out[n] = from_slots(n, lax.dynamic_update_slice(land, own[None], (my_id, 0, 0))).astype(_MXU)
        return out

    def dest_blocks(n, a):
        d, s = spec[n]
        r, c = _local_shape(d, s)
        if n in _TRANSPOSED:
            return a.reshape(N_DEV, c, r)
        return a.reshape(N_DEV, r, c) if d == 0 else a.reshape(r, N_DEV, c).transpose(1, 0, 2)

    pushed = []

    def grads_ready(g, names):
        nat = _grads_to_natural({n: g[n] for n in names})
        blocks = [dest_blocks(n, nat[n]).astype(_MXU) for n in names]
        sends, recvs, srcs, lands, tok = _push_start("rs_" + names[0] + "_start", blocks, True, token)
        pushed.append((names, sends, recvs, srcs, lands))
        return tok

    seq = x.shape[1]
    tp = ROW0 + seq
    h0 = jnp.concatenate([jnp.zeros((PAD, D_MODEL), F32), full["meta_tokens"], x[0]], axis=0)
    tgt = jnp.concatenate([jnp.zeros((ROW0, D_MODEL), F32), loss_target[0]], axis=0)
    loss, dh0, raw = _local_step(h0, tgt, _prepare(full, tp), token, late_weights, grads_ready)
    g = _grads_to_natural(raw)
    g["meta_tokens"] = dh0[PAD:ROW0]
    grad_x = dh0[ROW0:][None]

    big = [{}, {}, {}, {}]
    rep_names = [n for n, _ in _REPLICATED]
    raw_key = {"dn_A_log": "alog_b", "dn_dt_bias": "dtb_b"}
    pieces = [raw[raw_key.get(n, n)] for n in rep_names] + [loss]
    pieces += [g[n].reshape(1, -1) for n in _F32_GATHERED]
    widths = [p.shape[1] for p in pieces]
    offs = [sum(widths[:k]) for k in range(len(widths))]
    cat = jnp.concatenate(pieces, axis=1)
    cols = -(-cat.shape[1] // (8 * LANE)) * LANE
    mine = jnp.pad(cat, ((0, 0), (0, 8 * cols - cat.shape[1]))).reshape(8, cols)
    everyone = _all_gather("gather_small_grads", mine)
    total = _sum_parts("sum_small_grads", [(everyone, d) for d in range(N_DEV)]).reshape(1, 8 * cols)
    tot = {n: total[0, o:o + wd] for n, o, wd in zip(rep_names + ["loss"] + list(_F32_GATHERED), offs, widths)}
    lanes = lambda a: jnp.pad(a, ((0, 0), (0, -a.shape[1] % LANE)))
    items = [(o, -(-size // LANE) * LANE, n in raw_key) for (n, size), o in zip(_REPLICATED, offs)]
    sm = _adam_vectors("adam_replicated", total, items, [lanes(wl[n]) for n in rep_names],
                       [lanes(ml[n]) for n in rep_names], [lanes(vl[n]) for n in rep_names])
    sm = [{n: a[:, :size] for (n, size), a in zip(_REPLICATED, kind)} for kind in sm]
    mine_of = {}
    for n in _F32_GATHERED:
        d, s = spec[n]
        r, c = _local_shape(d, s)
        mine_of[n] = lax.dynamic_slice(tot[n].reshape(s), (0, my_id * c), (r, c))
    res = _adam("adam_small_sharded", [(small_block(mine_of)[None], 0)], small_block(wl), small_block(ml),
                small_block(vl))
    for kind, a in enumerate(res):
        big[kind].update(zip(_F32_GATHERED, _split(a.reshape(-1), small_sizes)))

    for names, sends, recvs, srcs, lands in pushed:
        srcs, lands = _push_wait("rs_" + names[0] + "_wait", sends, recvs, srcs, lands, True, dh0)
        for n, src, land in zip(names, srcs, lands):
            parts = [(src, my_id)] + [(land, my_id ^ f) for f in range(1, N_DEV)]
            for kind, a in enumerate(_adam("adam_" + n, parts, wl[n], ml[n], vl[n])):
                big[kind][n] = a

    outs = [tot["loss"][0], grad_x]
    for kind in range(4):
        for n in ("meta_tokens", "attn_norm_w", "w_in", "q_a_norm_w", "w_q_b", "kv_a_norm_w", "w_kv_b", "q_norm_w",
                  "k_norm_w", "mla_out_norm_w", "dn_conv_w", "dn_A_log", "dn_dt_bias", "dn_out_norm_w", "w_out",
                  "ffn_norm_w", "w_gate", "w_up", "ffn_conv_w", "ffn_conv_b", "w_down"):
            src = big[kind] if n in big[kind] else sm[kind]
            a = src[n].T if n in _TRANSPOSED else src[n]
            outs.append(a.reshape(out_shapes[n]))
    return tuple(outs)
```

```python
import functools
import math

import jax
import jax.numpy as jnp
from jax import lax
from jax.experimental import pallas as pl
from jax.experimental.pallas import tpu as pltpu

F32 = jnp.float32
BF16 = jnp.bfloat16
_MXU = jnp.bfloat16
_HI = lax.Precision.HIGHEST

D_MODEL = 1024
N_META = 16
PAD = 112
ROW0 = PAD + N_META
MLA_HEADS = 4
QK_NOPE = 128
QK_ROPE = 64
QK_HEAD = QK_NOPE + QK_ROPE
V_HEAD = 128
Q_LORA = 256
KV_LORA = 256
ROPE_THETA = 10000.0
DN_HEADS = 4
DN_DIM = 128
DN_WIDTH = DN_HEADS * DN_DIM
DN_CONV = 4
DN_CHUNK = 64
GDN_SUB_CHUNKS = 2
D_FF = 2816
FFN_CONV = 3
EPS = 1e-6
HP = 256
C_QKV = 0
C_Z = 1536
C_QL = 2048
C_KVL = 2304
C_KPE = 2560
C_AB = 2688
IN_P = 2816
IN_COLS = 2632

ADAM_LR = 0.001
ADAM_B1 = 0.9
ADAM_B2 = 0.999
ADAM_EPS = 1e-08
ADAM_WD = 0.01
ADAM_STEP = 10

N_DEV = 8
TM = 128
LANE = 128
VMEM_LIMIT = 56 * 1024 * 1024
NEG = -1e30


def _dot(a, b, dims, hp=False):
    if hp:
        return lax.dot_general(a.astype(F32), b.astype(F32), (dims, ((), ())),
                               precision=lax.Precision.HIGH if hp == "3x" else _HI, preferred_element_type=F32)
    return lax.dot_general(a.astype(_MXU), b.astype(_MXU), (dims, ((), ())),
                           preferred_element_type=F32)


def _nn(a, b, hp=False):
    return _dot(a, b, ((1,), (0,)), hp)


def _nt(a, b, hp=False):
    return _dot(a, b, ((1,), (1,)), hp)


def _tn(a, b, hp=False):
    return _dot(a, b, ((0,), (0,)), hp)


def _sigmoid(x):
    return 1.0 / (1.0 + jnp.exp(-x))


def _rms_fwd(x, w, n):
    r = lax.rsqrt(jnp.sum(x * x, axis=-1, keepdims=True) * (1.0 / n) + EPS)
    return x * r * w, r


def _rms_bwd(x, w, dy, n):
    r = lax.rsqrt(jnp.sum(x * x, axis=-1, keepdims=True) * (1.0 / n) + EPS)
    xh = x * r
    gy = dy * w
    dx = r * (gy - xh * (jnp.sum(gy * xh, axis=-1, keepdims=True) * (1.0 / n)))
    return dx, dy * xh


def _rowsum(x):
    return jnp.sum(x, axis=0, keepdims=True)


def _row_ids(i, tm):
    return i * tm + lax.broadcasted_iota(jnp.int32, (tm, 1), 0)


def _shift_down(ext, s, tm):
    if s == 0:
        return ext[8:8 + tm]
    return pltpu.roll(ext, s, 0)[8:8 + tm]


def _shift_up(ext, s, tm):
    if s == 0:
        return ext[0:tm]
    return pltpu.roll(ext, tm + 8 - s, 0)[0:tm]


def _conv_fwd(x, halo_prev, w, width):
    tm = x.shape[0]
    ext = jnp.concatenate([halo_prev, x], axis=0)
    y = None
    for j in range(width):
        t = w[j:j + 1, :] * _shift_down(ext, width - 1 - j, tm)
        y = t if y is None else y + t
    return y


def _conv_bwd_x(dy, halo_next, w, width):
    tm = dy.shape[0]
    ext = jnp.concatenate([dy, halo_next], axis=0)
    dx = None
    for j in range(width):
        t = w[j:j + 1, :] * _shift_up(ext, width - 1 - j, tm)
        dx = t if dx is None else dx + t
    return dx


def _conv_bwd_w(dy, x, halo_prev, width):
    tm = dy.shape[0]
    ext = jnp.concatenate([halo_prev, x], axis=0)
    rows = [_rowsum(dy * _shift_down(ext, width - 1 - j, tm)) for j in range(width)]
    rows += [jnp.zeros_like(rows[0])] * (8 - width)
    return jnp.concatenate(rows, axis=0)


def _softplus(x):
    e = jnp.exp(-jnp.abs(x))
    u = 1.0 + e
    l1p = jnp.where(u == 1.0, e, jnp.log(u) * e / jnp.where(u == 1.0, 1.0, u - 1.0))
    return jnp.maximum(x, 0.0) + l1p


def _swap_halves(x):
    lane = lax.broadcasted_iota(jnp.int32, x.shape, 1)
    return jnp.where(lane < 32, pltpu.roll(x, 96, 1), jnp.where(lane < 64, pltpu.roll(x, 32, 1), 0.0))


class _In:
    def __init__(self, arr, width=None, cb=0, kind="cur"):
        self.arr, self.kind = arr, kind
        self.width = arr.shape[1] if width is None else width
        self.cb = cb


def _whole_spec(x):
    return pl.BlockSpec(x.shape, lambda i, nd=x.ndim: (0,) * nd, pipeline_mode=pl.Buffered(1))


def _tile_spec(t, tm, tp):
    r8 = tm // 8
    if t.kind == "cur":
        return pl.BlockSpec((tm, t.width), lambda i, cb=t.cb: (i, cb))
    if t.kind == "prev":
        return pl.BlockSpec((8, t.width), lambda i, cb=t.cb: (jnp.maximum(i * r8 - 1, 0), cb))
    return pl.BlockSpec((8, t.width), lambda i, cb=t.cb: (jnp.minimum((i + 1) * r8, tp // 8 - 1), cb))


def _rows(name, fn, tiled, full, outs, accs=(), tm=TM):
    tp = tiled[0].arr.shape[0]
    nt = tp // tm
    r8 = tm // 8
    n_in = len(tiled) + len(full)
    n_out = len(outs)

    def body(*refs):
        i = pl.program_id(0)
        vals = [r[...] for r in refs[:n_in]]
        o_t, o_a = fn(i, *vals)
        for r, v in zip(refs[n_in:n_in + n_out], o_t):
            r[...] = v.astype(r.dtype)
        for r, v in zip(refs[n_in + n_out:], o_a):
            @pl.when(i == 0)
            def _():
                r[...] = v

            @pl.when(i > 0)
            def _():
                r[...] += v

    in_specs = [_tile_spec(t, tm, tp) for t in tiled]
    in_specs += [pl.BlockSpec(a.shape, lambda i, nd=a.ndim: (0,) * nd) for a in full]
    out_specs = [pl.BlockSpec((tm, w), lambda i: (i, 0)) for w, _ in outs]
    out_specs += [pl.BlockSpec((r, w), lambda i: (0, 0)) for r, w in accs]
    out_shape = [jax.ShapeDtypeStruct((tp, w), dt) for w, dt in outs]
    out_shape += [jax.ShapeDtypeStruct((r, w), F32) for r, w in accs]
    res = pl.pallas_call(
        body, name=name, grid=(nt,), in_specs=in_specs, out_specs=out_specs, out_shape=out_shape,
        compiler_params=pltpu.CompilerParams(dimension_semantics=("arbitrary",), vmem_limit_bytes=VMEM_LIMIT),
    )(*[t.arr for t in tiled], *full)
    return res


def _pick(n, cap, mult):
    best = None
    for d in range(mult, min(n, cap) + 1, mult):
        if n % d == 0:
            best = d
    assert best is not None, (n, cap, mult)
    return best


_ANY_SPEC = pl.BlockSpec(memory_space=pl.ANY)


def _mm(name, a, b, mode, out_dtype=F32, resid=None, after=None):
    if mode == "tn":
        m, k = a.shape
        n = b.shape[1]
        tk = _pick(k, 512, 128)
        tn = _pick(n, 1408, 128)

        def body_tn(a_ref, b_ref, o_ref):
            o_ref[...] = _tn(a_ref[...], b_ref[...]).astype(o_ref.dtype)

        return pl.pallas_call(
            body_tn, name=name, grid=(n // tn, k // tk),
            in_specs=[pl.BlockSpec((m, tk), lambda j, p: (0, p)),
                      pl.BlockSpec((m, tn), lambda j, p: (0, j))],
            out_specs=pl.BlockSpec((tk, tn), lambda j, p: (p, j)),
            out_shape=jax.ShapeDtypeStruct((k, n), out_dtype),
            compiler_params=pltpu.CompilerParams(
                dimension_semantics=("parallel", "parallel"), vmem_limit_bytes=VMEM_LIMIT),
        )(a, b)

    m, k = a.shape
    n = b.shape[1] if mode == "nn" else b.shape[0]
    tn = _pick(n, 1408, 128)
    tm = _pick(m, 1152, 16)
    dotf = _nn if mode == "nn" else _nt

    def body(*refs):
        a_ref, b_ref, o_ref = refs[0], refs[1], refs[-1]
        acc = dotf(a_ref[...], b_ref[...])
        if resid is not None:
            acc = refs[2][...] + acc
        o_ref[...] = acc.astype(o_ref.dtype)

    b_spec = (pl.BlockSpec((k, tn), lambda j, i: (0, j)) if mode == "nn"
              else pl.BlockSpec((tn, k), lambda j, i: (j, 0)))
    in_specs = [pl.BlockSpec((tm, k), lambda j, i: (i, 0)), b_spec]
    args = [a, b]
    if resid is not None:
        in_specs.append(pl.BlockSpec((tm, tn), lambda j, i: (i, j)))
        args.append(resid)
    if after is not None:
        in_specs.append(_ANY_SPEC)
        args.append(after)
    return pl.pallas_call(
        body, name=name, grid=(n // tn, m // tm), in_specs=in_specs,
        out_specs=pl.BlockSpec((tm, tn), lambda j, i: (i, j)),
        out_shape=jax.ShapeDtypeStruct((m, n), out_dtype),
        compiler_params=pltpu.CompilerParams(
            dimension_semantics=("parallel", "parallel"), vmem_limit_bytes=VMEM_LIMIT),
    )(*args)


def _mm_tn2(name, a1, a2, b, out_dtype=F32):
    m, k = a1.shape
    n = b.shape[1]
    tk = _pick(k, 512, 128)

    def body(a1_ref, a2_ref, b_ref, o1_ref, o2_ref):
        bb = b_ref[...]
        o1_ref[...] = _tn(a1_ref[...], bb).astype(o1_ref.dtype)
        o2_ref[...] = _tn(a2_ref[...], bb).astype(o2_ref.dtype)

    a_spec = pl.BlockSpec((m, tk), lambda p: (0, p))
    o_spec = pl.BlockSpec((tk, n), lambda p: (p, 0))
    return pl.pallas_call(
        body, name=name, grid=(k // tk,),
        in_specs=[a_spec, a_spec, pl.BlockSpec((m, n), lambda p: (0, 0))],
        out_specs=[o_spec, o_spec], out_shape=[jax.ShapeDtypeStruct((k, n), out_dtype)] * 2,
        compiler_params=pltpu.CompilerParams(dimension_semantics=("parallel",), vmem_limit_bytes=VMEM_LIMIT),
    )(a1, a2, b)


def _norm_mm(name, x, norm_w, b, mode="nt", x_cb=0, after=None):
    m = x.shape[0]
    k = norm_w.shape[1]
    n = b.shape[0] if mode == "nt" else b.shape[1]
    tn = _pick(n, 1408, 128)
    tm = _pick(m, 1152, 16)
    dotf = _nt if mode == "nt" else _nn
    extra = [] if after is None else [after]

    def body(x_ref, w_ref, b_ref, *rest):
        o_ref, u_ref = rest[-2:]

        @pl.when(pl.program_id(1) == 0)
        def _():
            u_ref[...] = _rms_fwd(x_ref[...], w_ref[...], k)[0].astype(u_ref.dtype)

        o_ref[...] = dotf(u_ref[...], b_ref[...])

    b_spec = (pl.BlockSpec((tn, k), lambda i, j: (j, 0)) if mode == "nt"
              else pl.BlockSpec((k, tn), lambda i, j: (0, j)))
    return pl.pallas_call(
        body, name=name, grid=(m // tm, n // tn),
        in_specs=[pl.BlockSpec((tm, k), lambda i, j: (i, x_cb)), pl.BlockSpec((1, k), lambda i, j: (0, 0)),
                  b_spec] + [_ANY_SPEC] * len(extra),
        out_specs=[pl.BlockSpec((tm, tn), lambda i, j: (i, j)), pl.BlockSpec((tm, k), lambda i, j: (i, 0))],
        out_shape=[jax.ShapeDtypeStruct((m, n), F32), jax.ShapeDtypeStruct((m, k), _MXU)],
        compiler_params=pltpu.CompilerParams(
            dimension_semantics=("arbitrary", "arbitrary"), vmem_limit_bytes=VMEM_LIMIT),
    )(x, norm_w, b, *extra)


def _pro_mm(name, fn, tiled, full, k, b, resid):
    m = resid.shape[0]
    n = b.shape[1]
    tm = _pick(m, 576, 16)
    n_in = len(tiled) + len(full)

    def body(*refs):
        i = pl.program_id(0)
        u = fn(i, *[r[...] for r in refs[:n_in]]).astype(_MXU)
        b_ref, r_ref, o_ref, u_ref = refs[n_in:]
        u_ref[...] = u
        o_ref[...] = r_ref[...] + _nn(u, b_ref[...])

    row = lambda w: pl.BlockSpec((tm, w), lambda i: (i, 0))
    in_specs = [_tile_spec(t, tm, m) for t in tiled]
    in_specs += [_whole_spec(x) for x in full] + [_whole_spec(b), row(n)]
    return pl.pallas_call(
        body, name=name, grid=(m // tm,), in_specs=in_specs, out_specs=[row(n), row(k)],
        out_shape=[jax.ShapeDtypeStruct((m, n), F32), jax.ShapeDtypeStruct((m, k), _MXU)],
        compiler_params=pltpu.CompilerParams(dimension_semantics=("parallel",), vmem_limit_bytes=VMEM_LIMIT),
    )(*[t.arr for t in tiled], *full, b, resid)


def _ffn_in(h2, norm_w, w_gate_t, w_up_t, conv_w8, conv_b):
    m, k = h2.shape
    n = w_gate_t.shape[0]
    tm = _pick(m, 288, 16)

    def body(x_ref, xp_ref, nw_ref, wg_ref, wu_ref, cw_ref, cb_ref, hn_ref, gp_ref, up_ref, act_ref):
        i = pl.program_id(0)
        nw = nw_ref[...]
        hn = _rms_fwd(x_ref[...], nw, k)[0].astype(_MXU)
        hn_prev = _rms_fwd(xp_ref[...], nw, k)[0].astype(_MXU)
        wg = wg_ref[...]
        gp = _nt(hn, wg)
        gp_prev = jnp.where(i > 0, _nt(hn_prev, wg), 0.0)
        up = _nt(hn, wu_ref[...])
        gate = _conv_fwd(gp, gp_prev, cw_ref[...], FFN_CONV) + cb_ref[...]
        hn_ref[...] = hn
        gp_ref[...] = gp
        up_ref[...] = up
        act_ref[...] = (_silu_parts(gate)[0] * up).astype(act_ref.dtype)

    row = lambda w: pl.BlockSpec((tm, w), lambda i: (i, 0))
    r8 = tm // 8
    return pl.pallas_call(
        body, name="ffn_in", grid=(m // tm,),
        in_specs=[row(k), pl.BlockSpec((8, k), lambda i: (jnp.maximum(i * r8 - 1, 0), 0)), _whole_spec(norm_w),
                  _whole_spec(w_gate_t), _whole_spec(w_up_t), _whole_spec(conv_w8), _whole_spec(conv_b)],
        out_specs=[row(k), row(n), row(n), row(n)],
        out_shape=[jax.ShapeDtypeStruct((m, k), _MXU), jax.ShapeDtypeStruct((m, n), F32),
                   jax.ShapeDtypeStruct((m, n), F32), jax.ShapeDtypeStruct((m, n), _MXU)],
        compiler_params=pltpu.CompilerParams(dimension_semantics=("parallel",), vmem_limit_bytes=VMEM_LIMIT),
    )(h2, h2, norm_w, w_gate_t, w_up_t, conv_w8, conv_b)


def _mm_rows(name, a, b, mode, fn, tiled, full, outs, accs=(), tm_cap=576):
    a_list = list(a) if isinstance(a, (list, tuple)) else [a]
    b_list = list(b) if isinstance(b, (list, tuple)) else [b]
    na = len(a_list)
    m = a_list[0].shape[0]
    tm = _pick(m, tm_cap, 16)
    dotf = _nn if mode == "nn" else _nt
    n_in = len(tiled) + len(full)
    n_out = len(outs)
    first = 2 * na

    def body(*refs):
        i = pl.program_id(0)
        vals = [r[...] for r in refs[first:first + n_in]]
        acc = dotf(refs[0][...], refs[na][...])
        for p in range(1, na):
            acc = acc + dotf(refs[p][...], refs[na + p][...])
        o_t, o_a = fn(i, acc, *vals)
        for r, v in zip(refs[first + n_in:first + n_in + n_out], o_t):
            r[...] = v.astype(r.dtype)
        for r, v in zip(refs[first + n_in + n_out:], o_a):
            @pl.when(i == 0)
            def _():
                r[...] = v

            @pl.when(i > 0)
            def _():
                r[...] += v

    whole = lambda x: pl.BlockSpec(x.shape, lambda i, nd=x.ndim: (0,) * nd)
    in_specs = [pl.BlockSpec((tm, x.shape[1]), lambda i: (i, 0)) for x in a_list] + [_whole_spec(x) for x in b_list]
    in_specs += [_tile_spec(t, tm, m) for t in tiled]
    in_specs += [whole(x) for x in full]
    out_specs = [pl.BlockSpec((tm, w), lambda i: (i, 0)) for w, _ in outs]
    out_specs += [pl.BlockSpec((r, w), lambda i: (0, 0)) for r, w in accs]
    out_shape = [jax.ShapeDtypeStruct((m, w), dt) for w, dt in outs]
    out_shape += [jax.ShapeDtypeStruct((r, w), F32) for r, w in accs]
    return pl.pallas_call(
        body, name=name, grid=(m // tm,), in_specs=in_specs, out_specs=out_specs, out_shape=out_shape,
        compiler_params=pltpu.CompilerParams(dimension_semantics=("arbitrary",), vmem_limit_bytes=VMEM_LIMIT),
    )(*a_list, *b_list, *[t.arr for t in tiled], *full)


ATTN_Q_TILES = 4


def _attn_probs(q, k, row0):
    tq, tp = q.shape[0], k.shape[0]
    s = _nt(q, k) * (1.0 / math.sqrt(QK_HEAD))
    row = row0 + lax.broadcasted_iota(jnp.int32, (tq, tp), 0)
    col = lax.broadcasted_iota(jnp.int32, (tq, tp), 1)
    ok = (col <= row) & (col >= PAD)
    s = jnp.where(ok, s, NEG)
    m = jnp.max(s, axis=-1, keepdims=True)
    e = jnp.exp(s - m)
    return e * (1.0 / jnp.sum(e, axis=-1, keepdims=True))


def _attn_fwd(q, k, v):
    tp = q.shape[0]
    tq = tp // ATTN_Q_TILES

    def body(q_ref, k_ref, v_ref, o_ref):
        for i in range(ATTN_Q_TILES):
            rows = slice(i * tq, (i + 1) * tq)
            keys = slice(0, (i + 1) * tq)
            p = _attn_probs(q_ref[rows, :], k_ref[keys, :], i * tq)
            o_ref[rows, :] = _nn(p, v_ref[keys, :])

    return pl.pallas_call(
        body, name="attn_fwd", grid=(MLA_HEADS,),
        in_specs=[pl.BlockSpec((tp, HP), lambda h: (0, h)),
                  pl.BlockSpec((tp, HP), lambda h: (0, h)),
                  pl.BlockSpec((tp, V_HEAD), lambda h: (0, h))],
        out_specs=pl.BlockSpec((tp, V_HEAD), lambda h: (0, h)),
        out_shape=jax.ShapeDtypeStruct((tp, MLA_HEADS * V_HEAD), F32),
        compiler_params=pltpu.CompilerParams(dimension_semantics=("parallel",), vmem_limit_bytes=VMEM_LIMIT),
    )(q, k, v)


def _attn_bwd(q, k, v, do):
    tp = q.shape[0]
    tq = tp // ATTN_Q_TILES

    def body(q_ref, k_ref, v_ref, do_ref, dq_ref, dk_ref, dv_ref):
        for i in reversed(range(ATTN_Q_TILES)):
            rows = slice(i * tq, (i + 1) * tq)
            keys = slice(0, (i + 1) * tq)
            qb = q_ref[rows, :]
            kk = k_ref[keys, :]
            dob = do_ref[rows, :]
            p = _attn_probs(qb, kk, i * tq)
            dp = _nt(dob, v_ref[keys, :])
            delta = jnp.sum(p * dp, axis=-1, keepdims=True)
            ds = p * (dp - delta) * (1.0 / math.sqrt(QK_HEAD))
            dq_ref[rows, :] = _nn(ds, kk)
            if i == ATTN_Q_TILES - 1:
                dk_ref[...] = _tn(ds, qb)
                dv_ref[...] = _tn(p, dob)
            else:
                dk_ref[keys, :] += _tn(ds, qb)
                dv_ref[keys, :] += _tn(p, dob)

    full = lambda w: pl.BlockSpec((tp, w), lambda h: (0, h))
    return pl.pallas_call(
        body, name="attn_bwd", grid=(MLA_HEADS,),
        in_specs=[full(HP), full(HP), full(V_HEAD), full(V_HEAD)],
        out_specs=[full(HP), full(HP), full(V_HEAD)],
        out_shape=[jax.ShapeDtypeStruct((tp, MLA_HEADS * HP), F32),
                   jax.ShapeDtypeStruct((tp, MLA_HEADS * HP), F32),
                   jax.ShapeDtypeStruct((tp, MLA_HEADS * V_HEAD), F32)],
        compiler_params=pltpu.CompilerParams(dimension_semantics=("parallel",), vmem_limit_bytes=VMEM_LIMIT),
    )(q, k, v, do)


def _gdn_consts():
    c = DN_CHUNK
    r = lax.broadcasted_iota(jnp.int32, (c, c), 0)
    cc = lax.broadcasted_iota(jnp.int32, (c, c), 1)
    incl = r >= cc
    strict = r > cc
    return incl, strict


def _cumsum_rows(x, reverse=False):
    c = x.shape[0]
    row = lax.broadcasted_iota(jnp.int32, x.shape, 0)
    s = 1
    while s < c:
        if reverse:
            x = x + jnp.where(row < c - s, pltpu.roll(x, c - s, 0), 0.0)
        else:
            x = x + jnp.where(row >= s, pltpu.roll(x, s, 0), 0.0)
        s *= 2
    return x


def _each(fn, *lists):
    return [fn(*a) for a in zip(*lists)]


def _interleave(chains):
    chains = list(chains)
    while chains:
        for ch in list(chains):
            try:
                next(ch)
            except StopIteration:
                chains.remove(ch)


def _gdn_chunk_common(q_ref, k_ref, v_ref, g_ref, b_ref):
    c = DN_CHUNK
    incl, strict = _gdn_consts()
    sls = [(slice(c * sub, c * (sub + 1)), slice(DN_DIM * h, DN_DIM * (h + 1)))
           for sub in range(GDN_SUB_CHUNKS) for h in range(DN_HEADS)]
    q = [q_ref[sl] * (1.0 / math.sqrt(DN_DIM)) for sl in sls]
    k = [k_ref[sl] for sl in sls]
    v = [v_ref[sl] for sl in sls]
    g = [g_ref[sl] for sl in sls]
    beta = [b_ref[sl] for sl in sls]
    gc = [_cumsum_rows(x) for x in g]
    grow = [x.T[:c, :] for x in gc]
    kb = _each(jnp.multiply, k, beta)
    kk = _each(_nt, kb, k)
    qk = _each(_nt, q, k)
    gam = [jnp.exp(x) for x in gc]
    g_last = [_rowsum(x) for x in g]
    dm = [jnp.exp(jnp.where(incl, x[:, :c] - y, NEG)) for x, y in zip(gc, grow)]
    vb = _each(jnp.multiply, v, beta)
    kbg = _each(jnp.multiply, kb, gam)
    ek = [jnp.exp(x - y) for x, y in zip(g_last, gc)]
    kd = _each(jnp.multiply, k, ek)
    return dict(q=q, k=k, v=v, beta=beta, gc=gc, gam=gam, g_last=g_last, dm=dm, kb=kb, vb=vb,
                kbg=kbg, kk=kk, ek=ek, kd=kd, qk=qk, incl=incl, strict=strict, sls=sls)


def _gdn_fwd(q, k, v, g, beta):
    tp = q.shape[0]
    c = DN_CHUNK
    nch = tp // c

    def body(q_ref, k_ref, v_ref, g_ref, b_ref, o_ref, s_ref, t_ref, s_scr):
        @pl.when(pl.program_id(0) == 0)
        def _():
            s_scr[...] = jnp.zeros_like(s_scr)

        eye = (lax.broadcasted_iota(jnp.int32, (c, c), 0) == lax.broadcasted_iota(jnp.int32, (c, c), 1)).astype(F32)
        x = _gdn_chunk_common(q_ref, k_ref, v_ref, g_ref, b_ref)
        heads = range(DN_HEADS)
        bp = [-jnp.where(x["strict"], kk * dm, 0.0) for kk, dm in zip(x["kk"], x["dm"])]
        t = [eye + b for b in bp]
        for _ in range(5):
            bp = [_nn(b, b, hp="3x") for b in bp]
            t = [tt + _nn(tt, b, hp="3x") for tt, b in zip(t, bp)]
        u = _each(_nn, t, x["vb"])
        w = _each(_nn, t, x["kbg"])
        qg = _each(jnp.multiply, x["q"], x["gam"])
        mqk = _each(jnp.multiply, x["qk"], x["dm"])
        s = [s_scr[h] for h in heads]
        for sub in range(GDN_SUB_CHUNKS):
            e = [DN_HEADS * sub + h for h in heads]
            v_new = [u[i] - _nn(w[i], s[h]) for h, i in zip(heads, e)]
            o = [_nn(qg[i], s[h]) + _nn(mqk[i], v_new[h]) for h, i in zip(heads, e)]
            s_new = [s[h] * jnp.exp(x["g_last"][i]) + _tn(x["kd"][i], v_new[h]) for h, i in zip(heads, e)]
            for h, i in zip(heads, e):
                s_ref[h, sub] = s[h]
                t_ref[h, sub] = t[i]
                o_ref[x["sls"][i]] = o[h]
            s = s_new
        for h in heads:
            s_scr[h] = s[h]

    sub = GDN_SUB_CHUNKS
    rb = lambda n: (n, 0)
    return pl.pallas_call(
        body, name="gdn_fwd", grid=(nch // sub,),
        in_specs=[pl.BlockSpec((sub * c, DN_WIDTH), rb)] * 5,
        out_specs=[pl.BlockSpec((sub * c, DN_WIDTH), rb),
                   pl.BlockSpec((DN_HEADS, sub, DN_DIM, DN_DIM), lambda n: (0, n, 0, 0)),
                   pl.BlockSpec((DN_HEADS, sub, c, c), lambda n: (0, n, 0, 0))],
        out_shape=[jax.ShapeDtypeStruct((tp, DN_WIDTH), F32),
                   jax.ShapeDtypeStruct((DN_HEADS, nch, DN_DIM, DN_DIM), F32),
                   jax.ShapeDtypeStruct((DN_HEADS, nch, c, c), F32)],
        scratch_shapes=[pltpu.VMEM((DN_HEADS, DN_DIM, DN_DIM), F32)],
        compiler_params=pltpu.CompilerParams(dimension_semantics=("arbitrary",), vmem_limit_bytes=VMEM_LIMIT),
    )(q, k, v, g, beta)


def _gdn_bwd(q, k, v, g, beta, s_all, t_all, do):
    tp = q.shape[0]
    c = DN_CHUNK
    nch = tp // c

    def body(q_ref, k_ref, v_ref, g_ref, b_ref, s_ref, t_ref, do_ref,
             dq_ref, dk_ref, dv_ref, dg_ref, db_ref, ds_scr):
        @pl.when(pl.program_id(0) == 0)
        def _():
            ds_scr[...] = jnp.zeros_like(ds_scr)

        xs = _gdn_chunk_common(q_ref, k_ref, v_ref, g_ref, b_ref)

        ds_state = [ds_scr[h] for h in range(DN_HEADS)]

        def chain(sub, h):
            e = DN_HEADS * sub + h
            x = {key: (val[e] if isinstance(val, list) else val) for key, val in xs.items()}
            sl = x["sls"]
            qs, kx, vx, beta_, gam, dm = x["q"], x["k"], x["v"], x["beta"], x["gam"], x["dm"]
            kb, vb, kbg, kd, ek = x["kb"], x["vb"], x["kbg"], x["kd"], x["ek"]
            t = t_ref[h, sub]
            s = s_ref[h, sub]
            dsn = ds_state[h]
            dob = do_ref[sl]
            eg_last = jnp.exp(x["g_last"])
            u = _nn(t, vb)
            w = _nn(t, kbg)
            mqk = x["qk"] * dm
            qd = qs * gam
            dqd = _nt(dob, s)
            dkd_pre = _nn(kd, dsn)
            yield
            v_new = u - _nn(w, s)
            dv_new = _tn(mqk, dob) + dkd_pre
            dq = dqd * gam
            dgam = jnp.sum(dqd * qs, axis=1, keepdims=True)
            yield
            ds_state[h] = _tn(qd, dob) + eg_last * dsn - _tn(w, dv_new)
            dmm = jnp.where(x["incl"], _nt(dob, v_new), 0.0)
            dkd = _nt(v_new, dsn)
            dw = -_nt(dv_new, s)
            dvb = _tn(t, dv_new)
            dt = _nt(dv_new, vb)
            yield
            dqk = dmm * dm
            e_mat = dmm * mqk
            dq = dq + _nn(dqk, kx)
            dk = _tn(dqk, qs) + dkd * ek
            e1 = jnp.sum(dkd * kd, axis=1, keepdims=True)
            dgc = -e1
            dg_last = jnp.sum(e1) + eg_last * jnp.sum(s * dsn)
            dt = dt + _nt(dw, kbg)
            dkbg = _tn(t, dw)
            yield
            tdt = _tn(t, dt, hp="3x")
            yield
            da = jnp.where(x["strict"], -_nt(tdt, t, hp="3x"), 0.0)
            yield
            dkk = da * dm
            e_mat = e_mat + da * x["kk"] * dm
            dkb = _nn(dkk, kx) + dkbg * gam
            dk = dk + _tn(dkk, kb)
            dgam = dgam + jnp.sum(dkbg * kb, axis=1, keepdims=True)
            yield
            dk = dk + dkb * beta_
            dbeta = jnp.sum(dkb * kx, axis=1, keepdims=True) + jnp.sum(dvb * vx, axis=1, keepdims=True)
            dv = dvb * beta_
            dgc = dgc + jnp.sum(e_mat, axis=1, keepdims=True) + dgam * gam
            dgc = dgc - jnp.sum(e_mat.T, axis=1, keepdims=True)
            yield
            dg = _cumsum_rows(dgc, reverse=True) + dg_last
            yield
            dq_ref[sl] = dq * (1.0 / math.sqrt(DN_DIM))
            dk_ref[sl] = dk
            dv_ref[sl] = dv
            dg_ref[sl] = dg
            db_ref[sl] = jnp.broadcast_to(dbeta, (c, LANE))

        chains = []
        for sub in reversed(range(GDN_SUB_CHUNKS)):
            new = [chain(sub, h) for h in range(DN_HEADS)]
            for _ in range(3):
                for ch in new:
                    next(ch)
            chains += new
        _interleave(chains)
        for h in range(DN_HEADS):
            ds_scr[h] = ds_state[h]

    nblk = nch // GDN_SUB_CHUNKS
    sub = GDN_SUB_CHUNKS
    rb = lambda n: (nblk - 1 - n, 0)
    hs = lambda n: (0, nblk - 1 - n, 0, 0)
    return pl.pallas_call(
        body, name="gdn_bwd", grid=(nblk,),
        in_specs=[pl.BlockSpec((sub * c, DN_WIDTH), rb)] * 5
        + [pl.BlockSpec((DN_HEADS, sub, DN_DIM, DN_DIM), hs), pl.BlockSpec((DN_HEADS, sub, c, c), hs),
           pl.BlockSpec((sub * c, DN_WIDTH), rb)],
        out_specs=[pl.BlockSpec((sub * c, DN_WIDTH), rb)] * 5,
        out_shape=[jax.ShapeDtypeStruct((tp, DN_WIDTH), F32)] * 5,
        scratch_shapes=[pltpu.VMEM((DN_HEADS, DN_DIM, DN_DIM), F32)],
        compiler_params=pltpu.CompilerParams(dimension_semantics=("arbitrary",), vmem_limit_bytes=VMEM_LIMIT),
    )(q, k, v, g, beta, s_all, t_all, do)


def _silu_parts(x):
    s = _sigmoid(x)
    return x * s, s * (1.0 + x * (1.0 - s))


def _f_rms_cast(i, x, w):
    y, _ = _rms_fwd(x, w, x.shape[1])
    return (y,), ()


def _f_rms_bwd_add(i, x, dy, dres, w, *, mask_pad):
    dx, dwr = _rms_bwd(x, w, dy, x.shape[1])
    out = dres + dx
    if mask_pad:
        out = jnp.where(_row_ids(i, x.shape[0]) >= PAD, out, 0.0)
    return (out,), (_rowsum(dwr),)


def _f_lat_norm(i, ql, kvl, qw, kvw):
    return (_rms_fwd(ql, qw, Q_LORA)[0], _rms_fwd(kvl, kvw, KV_LORA)[0]), ()


def _f_lat_norm_bwd(i, ql, kvl, dqn, dkvn, qw, kvw):
    dq, dqw = _rms_bwd(ql, qw, dqn, Q_LORA)
    dk, dkw = _rms_bwd(kvl, kvw, dkvn, KV_LORA)
    return (dq, dk), (_rowsum(dqw), _rowsum(dkw))


def _rope(x, cos, sin_s):
    return x * cos + _swap_halves(x) * sin_s


def _rope_t(dy, cos, sin_s):
    return dy * cos + _swap_halves(dy * sin_s)


def _f_mla_qk(i, qf, kvf, kpe, cos, sin_s, qw, kw):
    qs, ks, vs = [], [], []
    for h in range(MLA_HEADS):
        qn, _ = _rms_fwd(qf[:, HP * h:HP * (h + 1)], qw, QK_HEAD)
        qs += [qn[:, :QK_NOPE], _rope(qn[:, QK_NOPE:], cos, sin_s)]
        kh = jnp.concatenate([kvf[:, HP * h:HP * h + QK_NOPE], kpe], axis=1)
        kn, _ = _rms_fwd(kh, kw, QK_HEAD)
        ks += [kn[:, :QK_NOPE], _rope(kn[:, QK_NOPE:], cos, sin_s)]
        vs.append(kvf[:, HP * h + QK_NOPE:HP * (h + 1)])
    return (jnp.concatenate(qs, axis=1), jnp.concatenate(ks, axis=1), jnp.concatenate(vs, axis=1)), ()


def _f_mla_front(i, ql, kvl, kpe, cos, sin_s, qaw, kvaw, wq_t, wkv, qw, kw):
    qn = _rms_fwd(ql, qaw, Q_LORA)[0].astype(_MXU)
    kvn = _rms_fwd(kvl, kvaw, KV_LORA)[0].astype(_MXU)
    qf = _nt(qn, wq_t)
    kvf = _nn(kvn, wkv)
    (q, k, v), _ = _f_mla_qk(i, qf, kvf, kpe, cos, sin_s, qw, kw)
    return (qn, kvn, qf, kvf, q, k, v), ()


def _f_mla_back(i, qf, kvf, kpe, cos, sin_s, dq, dk, dv, ql, kvl, qaw, kvaw, wq_t, wkv, qw, kw):
    (dqf, dkvf, dkpe), (dqw, dkw) = _f_mla_qk_bwd(i, qf, kvf, kpe, cos, sin_s, dq, dk, dv, qw, kw)
    dqf = dqf.astype(_MXU)
    dkvf = dkvf.astype(_MXU)
    dql, dqaw = _rms_bwd(ql, qaw, _nn(dqf, wq_t), Q_LORA)
    dkvl, dkvaw = _rms_bwd(kvl, kvaw, _nt(dkvf, wkv), KV_LORA)
    return (dqf, dkvf, dkpe, dql, dkvl), (dqw, dkw, _rowsum(dqaw), _rowsum(dkvaw))


def _f_mla_qk_bwd(i, qf, kvf, kpe, cos, sin_s, dq, dk, dv, qw, kw):
    dqf, dkvf = [], []
    dkpe = None
    dqw = None
    dkw = None
    for h in range(MLA_HEADS):
        dqh = dq[:, HP * h:HP * (h + 1)]
        dqn = jnp.concatenate([dqh[:, :QK_NOPE], _rope_t(dqh[:, QK_NOPE:], cos, sin_s)], axis=1)
        dx, dwr = _rms_bwd(qf[:, HP * h:HP * (h + 1)], qw, dqn, QK_HEAD)
        dqf.append(dx)
        dqw = _rowsum(dwr) if dqw is None else dqw + _rowsum(dwr)
        dkh = dk[:, HP * h:HP * (h + 1)]
        dkn = jnp.concatenate([dkh[:, :QK_NOPE], _rope_t(dkh[:, QK_NOPE:], cos, sin_s)], axis=1)
        kh = jnp.concatenate([kvf[:, HP * h:HP * h + QK_NOPE], kpe], axis=1)
        dx, dwr = _rms_bwd(kh, kw, dkn, QK_HEAD)
        dkvf += [dx[:, :QK_NOPE], dv[:, V_HEAD * h:V_HEAD * (h + 1)]]
        dkpe = dx[:, QK_NOPE:] if dkpe is None else dkpe + dx[:, QK_NOPE:]
        dkw = _rowsum(dwr) if dkw is None else dkw + _rowsum(dwr)
    return (jnp.concatenate(dqf, axis=1), jnp.concatenate(dkvf, axis=1), dkpe), (dqw, dkw)


def _gdn_act(i, x, halo, w8):
    tm = x.shape[0]
    halo = jnp.where(i > 0, halo, 0.0)
    c = _conv_fwd(x, halo, w8, DN_CONV)
    act, dact = _silu_parts(c)
    return act, dact


def _spread_heads(ab):
    tm = ab.shape[0]
    return jnp.concatenate([jnp.broadcast_to(ab[:, h:h + 1], (tm, DN_DIM)) for h in range(2 * DN_HEADS)], axis=1)


def _gather_heads(x):
    tm = x.shape[0]
    lane = lax.broadcasted_iota(jnp.int32, (tm, LANE), 1)
    out = jnp.zeros((tm, LANE), F32)
    for h in range(2 * DN_HEADS):
        out = out + jnp.where(lane == h, x[:, DN_DIM * h:DN_DIM * h + 1], 0.0)
    return out


def _f_gdn_prep(i, x, halo, ab, w8, alog, dtb):
    tm = x.shape[0]
    act, _ = _gdn_act(i, x, halo, w8)
    outs = []
    for part in range(2):
        for h in range(DN_HEADS):
            t = act[:, DN_WIDTH * part + DN_DIM * h:DN_WIDTH * part + DN_DIM * (h + 1)]
            outs.append(t * lax.rsqrt(jnp.sum(t * t, axis=-1, keepdims=True) + EPS))
    q = jnp.concatenate(outs[:DN_HEADS], axis=1)
    k = jnp.concatenate(outs[DN_HEADS:], axis=1)
    v = act[:, 2 * DN_WIDTH:]
    abb = _spread_heads(ab)
    valid = _row_ids(i, tm) >= PAD
    g = jnp.where(valid, -jnp.exp(alog) * _softplus(abb[:, :DN_WIDTH] + dtb), 0.0)
    beta = jnp.where(valid, _sigmoid(abb[:, DN_WIDTH:]), 0.0)
    return (q, k, v, g, beta), ()


def _f_gdn_prep_bwd(i, x, x_prev, x_next, ab, dq, dq_next, dk, dk_next, dv, dv_next, dg, dbeta,
                    w8, alog, dtb, *, nt):
    tm = x.shape[0]
    x_prev = jnp.where(i > 0, x_prev, 0.0)
    more = i < nt - 1
    ext = lambda t, t_next: jnp.concatenate([t, jnp.where(more, t_next, 0.0)], axis=0)
    c = _conv_fwd(jnp.concatenate([x, x_next], axis=0), x_prev, w8, DN_CONV)
    act, dact = _silu_parts(c)
    douts = []
    for part, dd in enumerate((ext(dq, dq_next), ext(dk, dk_next))):
        for h in range(DN_HEADS):
            t = act[:, DN_WIDTH * part + DN_DIM * h:DN_WIDTH * part + DN_DIM * (h + 1)]
            r = lax.rsqrt(jnp.sum(t * t, axis=-1, keepdims=True) + EPS)
            y = t * r
            dy = dd[:, DN_DIM * h:DN_DIM * (h + 1)]
            douts.append(r * (dy - y * jnp.sum(dy * y, axis=-1, keepdims=True)))
    douts.append(ext(dv, dv_next))
    dc = jnp.concatenate(douts, axis=1) * dact
    dqkv = _conv_bwd_x(dc[:tm], dc[tm:], w8, DN_CONV)
    dconv_w = _conv_bwd_w(dc[:tm], x, x_prev, DN_CONV)
    abb = _spread_heads(ab)
    valid = _row_ids(i, tm) >= PAD
    pre = abb[:, :DN_WIDTH] + dtb
    ea = jnp.exp(alog)
    g = -ea * _softplus(pre)
    dg = jnp.where(valid, dg, 0.0)
    dbeta = jnp.where(valid, dbeta, 0.0)
    da = dg * (-ea) * _sigmoid(pre)
    beta = _sigmoid(abb[:, DN_WIDTH:])
    db = dbeta * beta * (1.0 - beta)
    dab = _gather_heads(jnp.concatenate([da, db], axis=1))
    return (dqkv, dab), (dconv_w, _rowsum(dg * g), _rowsum(da))


def _f_conv_bwd(i, dy, dy_next, x, x_prev, w8, *, width, nt):
    dy_next = jnp.where(i < nt - 1, dy_next, 0.0)
    x_prev = jnp.where(i > 0, x_prev, 0.0)
    return (_conv_bwd_x(dy, dy_next, w8, width),), (_conv_bwd_w(dy, x, x_prev, width),)


def _f_mix(i, o_mla, o_dn, z, w_mla, w_dn):
    tm = o_mla.shape[0]
    valid = _row_ids(i, tm) >= PAD
    outs = []
    for h in range(MLA_HEADS):
        y, _ = _rms_fwd(o_mla[:, V_HEAD * h:V_HEAD * (h + 1)], w_mla, V_HEAD)
        outs.append(jnp.where(valid, y, 0.0))
    for h in range(DN_HEADS):
        y, _ = _rms_fwd(o_dn[:, DN_DIM * h:DN_DIM * (h + 1)], w_dn, DN_DIM)
        outs.append(y * _silu_parts(z[:, DN_DIM * h:DN_DIM * (h + 1)])[0])
    return (jnp.concatenate(outs, axis=1),), ()


def _f_mix_bwd(i, o_mla, o_dn, z, dy_mla, dy_dn, w_mla, w_dn):
    tm = o_mla.shape[0]
    valid = _row_ids(i, tm) >= PAD
    d_mla, d_dn, d_z = [], [], []
    dw_mla = None
    dw_dn = None
    for h in range(MLA_HEADS):
        sl = slice(V_HEAD * h, V_HEAD * (h + 1))
        dx, dwr = _rms_bwd(o_mla[:, sl], w_mla, jnp.where(valid, dy_mla[:, sl], 0.0), V_HEAD)
        d_mla.append(dx)
        dw_mla = _rowsum(dwr) if dw_mla is None else dw_mla + _rowsum(dwr)
    for h in range(DN_HEADS):
        sl = slice(DN_DIM * h, DN_DIM * (h + 1))
        y, _ = _rms_fwd(o_dn[:, sl], w_dn, DN_DIM)
        sz, dsz = _silu_parts(z[:, sl])
        d_z.append(dy_dn[:, sl] * y * dsz)
        dx, dwr = _rms_bwd(o_dn[:, sl], w_dn, dy_dn[:, sl] * sz, DN_DIM)
        d_dn.append(dx)
        dw_dn = _rowsum(dwr) if dw_dn is None else dw_dn + _rowsum(dwr)
    return ((jnp.concatenate(d_mla, axis=1), jnp.concatenate(d_dn, axis=1), jnp.concatenate(d_z, axis=1)),
            (dw_mla, dw_dn))


def _f_ffn_act(i, gate_pre, halo, up, w8, b):
    halo = jnp.where(i > 0, halo, 0.0)
    gate = _conv_fwd(gate_pre, halo, w8, FFN_CONV) + b
    return (_silu_parts(gate)[0] * up,), ()


def _f_ffn_act_bwd(i, gp, gp_prev, gp_next, up, up_next, dact, dact_next, w8, b, *, nt):
    tm = gp.shape[0]
    gp_prev = jnp.where(i > 0, gp_prev, 0.0)
    dact_next = jnp.where(i < nt - 1, dact_next, 0.0)
    cat = lambda t, t_next: jnp.concatenate([t, t_next], axis=0)
    gate = _conv_fwd(cat(gp, gp_next), gp_prev, w8, FFN_CONV) + b
    sg, dsg = _silu_parts(gate)
    dact_e = cat(dact, dact_next)
    dgate = dact_e * cat(up, up_next) * dsg
    dgate_pre = _conv_bwd_x(dgate[:tm], dgate[tm:], w8, FFN_CONV)
    dup = dact * sg[:tm]
    return (dgate_pre, dup), (_conv_bwd_w(dgate[:tm], gp, gp_prev, FFN_CONV), _rowsum(dgate[:tm]))


def _f_loss(i, h3, tgt):
    tm = h3.shape[0]
    diff = jnp.where(_row_ids(i, tm) >= ROW0, h3 - tgt, 0.0)
    part = 0.5 * jnp.sum(diff * diff) * (1.0 / D_MODEL)
    return (diff * (1.0 / D_MODEL),), (jnp.full((1, LANE), part, F32),)


def _after(fn):
    return lambda i, *a: fn(i, *a[:-1])


def _local_step(h0, tgt, w, token, late_weights, grads_ready):
    tp = h0.shape[0]
    nt = tp // TM
    bf = (D_MODEL, _MXU)
    proj, u = _norm_mm("in_proj", h0, w["attn_norm_w"], w["w_in"], after=token)
    p_qkv = lambda kind="cur": _In(proj, 3 * DN_WIDTH, 0, kind)
    p_z = _In(proj, DN_WIDTH, C_Z // DN_WIDTH)
    p_ql = _In(proj, Q_LORA, C_QL // Q_LORA)
    p_kvl = _In(proj, KV_LORA, C_KVL // KV_LORA)
    p_kpe = _In(proj, LANE, C_KPE // LANE)
    p_ab = _In(proj, LANE, C_AB // LANE)
    cos, sin_s = _In(w["cos"]), _In(w["sin_s"])

    mla_w = [w["q_a_norm_w"], w["kv_a_norm_w"], w["w_q_b"], w["w_kv_b"], w["q_norm_w"], w["k_norm_w"]]
    tm_mla = _pick(tp, 288, 16)
    wide = MLA_HEADS * HP
    qn, kvn, qf, kvf, q, k, v = _rows(
        "mla_front", _f_mla_front, [p_ql, p_kvl, p_kpe, cos, sin_s], mla_w,
        [(Q_LORA, _MXU), (KV_LORA, _MXU), (wide, F32), (wide, F32), (wide, _MXU), (wide, _MXU),
         (MLA_HEADS * V_HEAD, _MXU)], tm=tm_mla)
    o_mla = _attn_fwd(q, k, v)

    dn_w = [w["dn_conv_w"], w["alog_b"], w["dtb_b"]]
    gq, gk, gv, gg, gb = _rows("gdn_prep", _f_gdn_prep, [p_qkv(), p_qkv("prev"), p_ab], dn_w,
                               [(DN_WIDTH, F32)] * 5)
    o_dn, s_all, t_all = _gdn_fwd(gq, gk, gv, gg, gb)

    out_w = [w["mla_out_norm_w"], w["dn_out_norm_w"]]
    w = dict(w, **late_weights((o_mla, o_dn)))
    h2, mixed = _pro_mm("mix_out_proj", lambda i, *t: _f_mix(i, *t)[0][0], [_In(o_mla), _In(o_dn), p_z], out_w,
                        D_MODEL, w["w_out"], h0)

    ffn_w = [w["ffn_conv_w"], w["ffn_conv_b"]]
    hn, gate_pre, up, act = _ffn_in(h2, w["ffn_norm_w"], w["w_gate"], w["w_up"], *ffn_w)
    dh3, loss = _mm_rows("ffn_down_loss", act, w["w_down"], "nn", lambda i, y, r, t: _f_loss(i, r + y, t),
                         [_In(h2), _In(tgt)], [], [(D_MODEL, F32)], [(1, LANE)])

    g = {}
    dact = _mm("ffn_down_dx", dh3, w["w_down"], "nt")
    g["w_down"] = _mm("ffn_down_dw", act, dh3, "tn", out_dtype=_MXU)
    dgate_pre, dup, g["ffn_conv_w"], g["ffn_conv_b"] = _rows(
        "ffn_act_bwd", functools.partial(_f_ffn_act_bwd, nt=nt),
        [_In(gate_pre), _In(gate_pre, kind="prev"), _In(gate_pre, kind="next"), _In(up), _In(up, kind="next"),
         _In(dact), _In(dact, kind="next")], ffn_w,
        [(D_FF, _MXU), (D_FF, _MXU)], [(8, D_FF), (1, D_FF)])
    g["w_gate"], g["w_up"] = _mm_tn2("ffn_gate_up_dw", dgate_pre, dup, hn, out_dtype=_MXU)
    tok = grads_ready(g, ("w_down", "w_gate", "w_up"))
    dh2, g["ffn_norm_w"] = _mm_rows(
        "ffn_gate_up_dx_rms", [dgate_pre, dup], [w["w_gate"], w["w_up"]], "nn",
        lambda i, dy, x, dres, nw, _tok: _f_rms_bwd_add(i, x, dy, dres, nw, mask_pad=True),
        [_In(h2), _In(dh3)], [w["ffn_norm_w"], tok], [(D_MODEL, F32)], [(1, D_MODEL)])

    g["w_out"] = _mm("out_proj_dw", mixed, dh2, "tn", out_dtype=_MXU)
    half = MLA_HEADS * V_HEAD
    do_mla, do_dn, dz, g["mla_out_norm_w"], g["dn_out_norm_w"] = _mm_rows(
        "out_proj_dx_mix", dh2, w["w_out"], "nt",
        lambda i, dm, om, od, z, wm, wd: _f_mix_bwd(i, om, od, z, dm[:, :half], dm[:, half:], wm, wd),
        [_In(o_mla), _In(o_dn), p_z], out_w,
        [(half, F32), (DN_WIDTH, F32), (DN_WIDTH, _MXU)], [(1, V_HEAD), (1, DN_DIM)])

    dq, dk, dv = _attn_bwd(q, k, v, do_mla)
    dqf, dkvf, dkpe, dql, dkvl, g["q_norm_w"], g["k_norm_w"], g["q_a_norm_w"], g["kv_a_norm_w"] = _rows(
        "mla_back", _f_mla_back,
        [_In(qf), _In(kvf), p_kpe, cos, sin_s, _In(dq), _In(dk), _In(dv), p_ql, p_kvl], mla_w,
        [(wide, _MXU), (wide, _MXU), (LANE, _MXU), (Q_LORA, _MXU), (KV_LORA, _MXU)],
        [(1, HP), (1, HP), (1, Q_LORA), (1, KV_LORA)], tm=tm_mla)
    g["w_q_b"] = _mm("mla_q_b_dw", dqf, qn, "tn")
    g["w_kv_b"] = _mm("mla_kv_b_dw", kvn, dkvf, "tn")
    tok = grads_ready(g, ("w_out", "w_q_b", "w_kv_b"))

    dgq, dgk, dgv, dgg, dgb = _gdn_bwd(gq, gk, gv, gg, gb, s_all, t_all, do_dn)
    nxt = lambda a: _In(a, kind="next")
    dqkv, dab, g["dn_conv_w"], g["alog_b"], g["dtb_b"] = _rows(
        "gdn_prep_bwd", _after(functools.partial(_f_gdn_prep_bwd, nt=nt)),
        [p_qkv(), p_qkv("prev"), p_qkv("next"), p_ab, _In(dgq), nxt(dgq), _In(dgk), nxt(dgk), _In(dgv), nxt(dgv),
         _In(dgg), _In(dgb)], dn_w + [tok],
        [(3 * DN_WIDTH, _MXU), (LANE, _MXU)], [(8, 3 * DN_WIDTH), (1, DN_WIDTH), (1, DN_WIDTH)])

    dproj = jnp.concatenate([dqkv, dz, dql, dkvl, dkpe, dab], axis=1)
    g["w_in"] = _mm("in_proj_dw", dproj, u, "tn", out_dtype=_MXU)
    tok = grads_ready(g, ("w_in",))
    dh0, g["attn_norm_w"] = _mm_rows(
        "in_proj_dx_rms", dproj, w["w_in"], "nn",
        lambda i, du, x, dres, nw, _tok: _f_rms_bwd_add(i, x, du, dres, nw, mask_pad=False),
        [_In(h0), _In(dh2)], [w["attn_norm_w"], tok], [(D_MODEL, F32)], [(1, D_MODEL)])
    return loss, dh0, g


def _w_in_to_padded(w):
    c1, c2, c3 = Q_LORA, Q_LORA + KV_LORA, Q_LORA + KV_LORA + QK_ROPE
    c4 = c3 + 3 * DN_WIDTH
    c5 = c4 + DN_WIDTH
    z = lambda n: jnp.zeros((n, w.shape[1]), w.dtype)
    return jnp.concatenate([w[c3:c4], w[c4:c5], w[:c1], w[c1:c2], w[c2:c3], z(LANE - QK_ROPE),
                            w[c5:], z(LANE - 2 * DN_HEADS)], axis=0)


def _w_in_from_padded(g):
    return jnp.concatenate([g[C_QL:C_QL + Q_LORA], g[C_KVL:C_KVL + KV_LORA], g[C_KPE:C_KPE + QK_ROPE],
                            g[:C_Z + DN_WIDTH], g[C_AB:C_AB + 2 * DN_HEADS]], axis=0)


def _w_q_b_to_padded(w):
    r = w.shape[1]
    w = w.reshape(MLA_HEADS, QK_HEAD, r)
    return jnp.pad(w, ((0, 0), (0, HP - QK_HEAD), (0, 0))).reshape(MLA_HEADS * HP, r)


def _w_q_b_from_padded(g):
    r = g.shape[1]
    return g.reshape(MLA_HEADS, HP, r)[:, :QK_HEAD].reshape(MLA_HEADS * QK_HEAD, r)


def _pad_rows8(w):
    return jnp.pad(w, ((0, 8 - w.shape[0]), (0, 0)))


def _prepare(full, tp):
    w = {}
    mx = lambda a: a.astype(_MXU)
    w["attn_norm_w"] = full["attn_norm_w"]
    w["w_in"] = mx(_w_in_to_padded(full["w_in"]))
    w["q_a_norm_w"] = full["q_a_norm_w"]
    w["kv_a_norm_w"] = full["kv_a_norm_w"]
    w["w_q_b"] = mx(_w_q_b_to_padded(full["w_q_b"]))
    w["w_kv_b"] = mx(full["w_kv_b"])
    w["q_norm_w"] = jnp.pad(full["q_norm_w"], ((0, 0), (0, HP - QK_HEAD)))
    w["k_norm_w"] = jnp.pad(full["k_norm_w"], ((0, 0), (0, HP - QK_HEAD)))
    w["mla_out_norm_w"] = full["mla_out_norm_w"]
    w["dn_out_norm_w"] = full["dn_out_norm_w"]
    w["dn_conv_w"] = _pad_rows8(full["dn_conv_w"])
    w["alog_b"] = jnp.repeat(full["dn_A_log"], DN_DIM, axis=1)
    w["dtb_b"] = jnp.repeat(full["dn_dt_bias"], DN_DIM, axis=1)
    w["ffn_norm_w"] = full["ffn_norm_w"]
    w["ffn_conv_w"] = _pad_rows8(full["ffn_conv_w"])
    w["ffn_conv_b"] = full["ffn_conv_b"]
    for n in _LATE:
        if n in full:
            w[n] = mx(full[n])
    half = QK_ROPE // 2
    inv = ROPE_THETA ** (-jnp.arange(half, dtype=F32) / half)
    ang = (jnp.arange(tp, dtype=jnp.int32) - PAD).astype(F32)[:, None] * inv[None, :]
    zc = jnp.zeros((tp, LANE - QK_ROPE), F32)
    w["cos"] = jnp.concatenate([jnp.cos(ang), jnp.cos(ang), zc], axis=1)
    w["sin_s"] = jnp.concatenate([-jnp.sin(ang), jnp.sin(ang), zc], axis=1)
    return w


def _grads_to_natural(g):
    convert = {
        "w_in": ("w_in", _w_in_from_padded),
        "w_q_b": ("w_q_b", _w_q_b_from_padded),
        "q_norm_w": ("q_norm_w", lambda a: a[:, :QK_HEAD]),
        "k_norm_w": ("k_norm_w", lambda a: a[:, :QK_HEAD]),
        "dn_conv_w": ("dn_conv_w", lambda a: a[:DN_CONV]),
        "ffn_conv_w": ("ffn_conv_w", lambda a: a[:FFN_CONV]),
        "alog_b": ("dn_A_log", lambda a: a[:, ::DN_DIM]),
        "dtb_b": ("dn_dt_bias", lambda a: a[:, ::DN_DIM]),
    }
    n = {}
    for key, a in g.items():
        name, fn = convert.get(key, (key, lambda t: t))
        n[name] = fn(a)
    return n


_MESH = pl.DeviceIdType.MESH
_ANY = pl.BlockSpec(memory_space=pl.ANY)
_CHIP_FLIPS = ((1, 0), (0, 1), (1, 1))


def _me():
    return lax.axis_index("x"), lax.axis_index("y"), lax.axis_index("c")


def _all_gather(name, blk):
    def body(x_ref, out_ref, send_sems, recv_sems, local_sem):
        x, y, c = _me()
        me, sib = (x, y, c), (x, y, 1 - c)
        chips = [(x ^ fx, y ^ fy) for fx, fy in _CHIP_FLIPS]

        def slot(p):
            return out_ref.at[4 * p[0] + 2 * p[1] + p[2]]

        def copy(k, block, to, src=None):
            return pltpu.make_async_remote_copy(
                src_ref=slot(block) if src is None else src, dst_ref=slot(block),
                send_sem=send_sems.at[k], recv_sem=recv_sems.at[k], device_id=to, device_id_type=_MESH)

        mine = pltpu.make_async_copy(x_ref, slot(me), local_sem)
        mine.start()
        first = [copy(0, me, sib, src=x_ref)]
        first += [copy(1 + j, me, (*chip, c), src=x_ref) for j, chip in enumerate(chips)]
        for cp in first:
            cp.start()
        passed = [copy(4 + j, (*chip, c), sib) for j, chip in enumerate(chips)]
        for j, chip in enumerate(chips):
            copy(1 + j, (*chip, c), me).wait_recv()
            passed[j].start()
        copy(0, sib, me).wait_recv()
        for j, chip in enumerate(chips):
            copy(4 + j, (*chip, 1 - c), me).wait_recv()
        for cp in first + passed:
            cp.wait_send()
        mine.wait()

    return pl.pallas_call(
        body, name=name, in_specs=[_ANY], out_specs=_ANY,
        out_shape=jax.ShapeDtypeStruct((N_DEV,) + blk.shape, blk.dtype),
        scratch_shapes=[pltpu.SemaphoreType.DMA((7,)), pltpu.SemaphoreType.DMA((7,)), pltpu.SemaphoreType.DMA],
    )(blk)


def _rs_sibling(name, gb):
    def body(g_ref, out_ref, send_sems, recv_sems):
        x, y, c = _me()
        cps = []
        for j in range(4):
            cp = pltpu.make_async_remote_copy(
                src_ref=g_ref.at[2 * j + (1 - c)], dst_ref=out_ref.at[j], send_sem=send_sems.at[j],
                recv_sem=recv_sems.at[j], device_id=(x, y, 1 - c), device_id_type=_MESH)
            cp.start()
            cps.append(cp)
        for cp in cps:
            cp.wait()

    return pl.pallas_call(
        body, name=name, in_specs=[_ANY], out_specs=_ANY,
        out_shape=jax.ShapeDtypeStruct((4,) + gb.shape[1:], gb.dtype),
        scratch_shapes=[pltpu.SemaphoreType.DMA((4,)), pltpu.SemaphoreType.DMA((4,))],
    )(gb)


def _rs_chips(name, s1):
    def body(s_ref, out_ref, send_sems, recv_sems):
        x, y, c = _me()
        cps = []
        for k, (fx, fy) in enumerate(_CHIP_FLIPS):
            px, py = x ^ fx, y ^ fy
            cp = pltpu.make_async_remote_copy(
                src_ref=s_ref.at[2 * px + py], dst_ref=out_ref.at[k], send_sem=send_sems.at[k],
                recv_sem=recv_sems.at[k], device_id=(px, py, c), device_id_type=_MESH)
            cp.start()
            cps.append(cp)
        for cp in cps:
            cp.wait()

    return pl.pallas_call(
        body, name=name, in_specs=[_ANY], out_specs=_ANY,
        out_shape=jax.ShapeDtypeStruct((3,) + s1.shape[1:], s1.dtype),
        scratch_shapes=[pltpu.SemaphoreType.DMA((3,)), pltpu.SemaphoreType.DMA((3,))],
    )(s1)


def _row_tile(r):
    divs = [d for d in range(16, min(r, 512) + 1, 16) if r % d == 0]
    return divs[-1] if divs else r


def _pair_sum(name, gb, recv):
    _, r, cols = gb.shape
    tm = _row_tile(r)
    c = lax.axis_index("c").astype(jnp.int32).reshape(1)

    def body(c_ref, a_ref, b_ref, o_ref, ob_ref):
        s = a_ref[...] + b_ref[...]
        o_ref[...] = s
        ob_ref[...] = s.astype(BF16)

    blk = pl.BlockSpec((1, tm, cols), lambda j, i, c_ref: (j, i, 0))
    return pl.pallas_call(
        body, name=name,
        grid_spec=pltpu.PrefetchScalarGridSpec(
            num_scalar_prefetch=1, grid=(4, r // tm),
            in_specs=[pl.BlockSpec((1, tm, cols), lambda j, i, c_ref: (2 * j + c_ref[0], i, 0)), blk],
            out_specs=[blk, blk]),
        out_shape=[jax.ShapeDtypeStruct((4, r, cols), F32), jax.ShapeDtypeStruct((4, r, cols), BF16)],
        compiler_params=pltpu.CompilerParams(dimension_semantics=("parallel", "parallel")),
    )(c, gb, recv)


def _adam_math(g, w, m, v):
    m_new = ADAM_B1 * m + (1.0 - ADAM_B1) * g
    v_new = ADAM_B2 * v + (1.0 - ADAM_B2) * (g * g)
    m_hat = m_new / (1.0 - ADAM_B1 ** ADAM_STEP)
    v_hat = v_new / (1.0 - ADAM_B2 ** ADAM_STEP)
    return -ADAM_LR * (m_hat / (jnp.sqrt(v_hat) + ADAM_EPS) + ADAM_WD * w), m_new, v_new


def _adam_vectors(name, row, items, ws, ms, vs):
    k = len(items)

    def body(row_ref, *refs):
        w_refs, m_refs, v_refs = refs[:k], refs[k:2 * k], refs[2 * k:3 * k]
        outs = refs[3 * k:]
        for idx, (off, n, per_head) in enumerate(items):
            if per_head:
                spread = row_ref[:, off:off + DN_WIDTH]
                lane = lax.broadcasted_iota(jnp.int32, (1, LANE), 1)
                g = jnp.zeros((1, LANE), F32)
                for h in range(DN_HEADS):
                    g = g + jnp.where(lane == h, spread[:, DN_DIM * h:DN_DIM * h + 1], 0.0)
            else:
                g = row_ref[:, off:off + n]
            d, m_new, v_new = _adam_math(g, w_refs[idx][...], m_refs[idx][...], v_refs[idx][...])
            for kind, val in enumerate((g, d, m_new, v_new)):
                outs[kind * k + idx][...] = val

    shapes = [jax.ShapeDtypeStruct((1, n), F32) for _, n, _ in items]
    res = pl.pallas_call(body, name=name, out_shape=shapes * 4)(row, *ws, *ms, *vs)
    return [list(res[kind * k:(kind + 1) * k]) for kind in range(4)]


def _sum_parts(name, parts):
    _, r, cols = parts[0][0].shape
    tm = _row_tile(r)
    idx = jnp.stack([jnp.asarray(s, jnp.int32) for _, s in parts])
    n = len(parts)

    def body(idx_ref, *refs):
        g = refs[0][0].astype(F32)
        for p_ref in refs[1:n]:
            g = g + p_ref[0].astype(F32)
        refs[n][...] = g

    return pl.pallas_call(
        body, name=name,
        grid_spec=pltpu.PrefetchScalarGridSpec(
            num_scalar_prefetch=1, grid=(r // tm,),
            in_specs=[pl.BlockSpec((1, tm, cols), lambda i, idx_ref, p=p: (idx_ref[p], i, 0)) for p in range(n)],
            out_specs=pl.BlockSpec((tm, cols), lambda i, idx_ref: (i, 0))),
        out_shape=jax.ShapeDtypeStruct((r, cols), F32),
        compiler_params=pltpu.CompilerParams(dimension_semantics=("parallel",)),
    )(idx, *[a for a, _ in parts])


def _adam(name, parts, w, m, v):
    r, cols = w.shape
    tm = _row_tile(r)
    idx = jnp.stack([jnp.asarray(s, jnp.int32) for _, s in parts])
    n = len(parts)

    def body(idx_ref, *refs):
        g = refs[0][0].astype(F32)
        for p_ref in refs[1:n]:
            g = g + p_ref[0].astype(F32)
        w_ref, m_ref, v_ref, g_out, d_out, m_out, v_out = refs[n:]
        g_out[...] = g
        d_out[...], m_out[...], v_out[...] = _adam_math(g, w_ref[...], m_ref[...], v_ref[...])

    part_specs = [pl.BlockSpec((1, tm, cols), lambda i, idx_ref, p=p: (idx_ref[p], i, 0)) for p in range(n)]
    flat = pl.BlockSpec((tm, cols), lambda i, idx_ref: (i, 0))
    return pl.pallas_call(
        body, name=name,
        grid_spec=pltpu.PrefetchScalarGridSpec(
            num_scalar_prefetch=1, grid=(r // tm,), in_specs=part_specs + [flat] * 3, out_specs=[flat] * 4),
        out_shape=[jax.ShapeDtypeStruct((r, cols), F32)] * 4,
        compiler_params=pltpu.CompilerParams(dimension_semantics=("parallel",)),
    )(idx, *[a for a, _ in parts], w, m, v)


def _all_gather_many(name, blks):
    n = len(blks)

    def body(*refs):
        x_refs, out_refs = refs[:n], refs[n:2 * n]
        send_sems, recv_sems, local_sems = refs[2 * n:]
        x, y, c = _me()
        me, sib = (x, y, c), (x, y, 1 - c)
        chips = [(x ^ fx, y ^ fy) for fx, fy in _CHIP_FLIPS]

        def slot(a, p):
            return out_refs[a].at[4 * p[0] + 2 * p[1] + p[2]]

        def copy(a, k, block, to, src=None):
            return pltpu.make_async_remote_copy(
                src_ref=slot(a, block) if src is None else src, dst_ref=slot(a, block),
                send_sem=send_sems.at[7 * a + k], recv_sem=recv_sems.at[7 * a + k], device_id=to,
                device_id_type=_MESH)

        mine = [pltpu.make_async_copy(x_refs[a], slot(a, me), local_sems.at[a]) for a in range(n)]
        first = []
        for a in range(n):
            mine[a].start()
            first.append(copy(a, 0, me, sib, src=x_refs[a]))
            first += [copy(a, 1 + j, me, (*chip, c), src=x_refs[a]) for j, chip in enumerate(chips)]
        for cp in first:
            cp.start()
        passed = []
        for j, chip in enumerate(chips):
            for a in range(n):
                copy(a, 1 + j, (*chip, c), me).wait_recv()
                cp = copy(a, 4 + j, (*chip, c), sib)
                cp.start()
                passed.append(cp)
        for a in range(n):
            copy(a, 0, sib, me).wait_recv()
            for j, chip in enumerate(chips):
                copy(a, 4 + j, (*chip, 1 - c), me).wait_recv()
        for cp in first + passed:
            cp.wait_send()
        for cp in mine:
            cp.wait()

    return pl.pallas_call(
        body, name=name, in_specs=[_ANY] * n, out_specs=[_ANY] * n,
        out_shape=[jax.ShapeDtypeStruct((N_DEV,) + b.shape, b.dtype) for b in blks],
        scratch_shapes=[pltpu.SemaphoreType.DMA((7 * n,)), pltpu.SemaphoreType.DMA((7 * n,)),
                        pltpu.SemaphoreType.DMA((n,))],
    )(*blks)


def _rs_sibling_many(name, gbs):
    n = len(gbs)

    def body(*refs):
        g_refs, out_refs = refs[:n], refs[n:2 * n]
        send_sems, recv_sems = refs[2 * n:]
        x, y, c = _me()
        cps = []
        for a in range(n):
            for j in range(4):
                cp = pltpu.make_async_remote_copy(
                    src_ref=g_refs[a].at[2 * j + (1 - c)], dst_ref=out_refs[a].at[j],
                    send_sem=send_sems.at[4 * a + j], recv_sem=recv_sems.at[4 * a + j],
                    device_id=(x, y, 1 - c), device_id_type=_MESH)
                cp.start()
                cps.append(cp)
        for cp in cps:
            cp.wait()

    return pl.pallas_call(
        body, name=name, in_specs=[_ANY] * n, out_specs=[_ANY] * n,
        out_shape=[jax.ShapeDtypeStruct((4,) + g.shape[1:], g.dtype) for g in gbs],
        scratch_shapes=[pltpu.SemaphoreType.DMA((4 * n,)), pltpu.SemaphoreType.DMA((4 * n,))],
    )(*gbs)


def _rs_chips_many(name, s1s):
    n = len(s1s)

    def body(*refs):
        s_refs, out_refs = refs[:n], refs[n:2 * n]
        send_sems, recv_sems = refs[2 * n:]
        x, y, c = _me()
        cps = []
        for a in range(n):
            for k, (fx, fy) in enumerate(_CHIP_FLIPS):
                px, py = x ^ fx, y ^ fy
                cp = pltpu.make_async_remote_copy(
                    src_ref=s_refs[a].at[2 * px + py], dst_ref=out_refs[a].at[k],
                    send_sem=send_sems.at[3 * a + k], recv_sem=recv_sems.at[3 * a + k],
                    device_id=(px, py, c), device_id_type=_MESH)
                cp.start()
                cps.append(cp)
        for cp in cps:
            cp.wait()

    return pl.pallas_call(
        body, name=name, in_specs=[_ANY] * n, out_specs=[_ANY] * n,
        out_shape=[jax.ShapeDtypeStruct((3,) + s.shape[1:], s.dtype) for s in s1s],
        scratch_shapes=[pltpu.SemaphoreType.DMA((3 * n,)), pltpu.SemaphoreType.DMA((3 * n,))],
    )(*s1s)


_HBM = pl.BlockSpec(memory_space=pltpu.HBM)
_SEM = pl.BlockSpec(memory_space=pltpu.SEMAPHORE)
_EFFECT = pltpu.SideEffectType.DATAFLOW_SIDE_EFFECTING


def _push_copies(src_refs, land_refs, send_sems, recv_sems, src_by_peer):
    x, y, c = _me()
    my_id = 4 * x + 2 * y + c
    out = []
    for a in range(len(src_refs)):
        for f in range(1, N_DEV):
            px, py, pc = x ^ (f >> 2), y ^ ((f >> 1) & 1), c ^ (f & 1)
            pid = 4 * px + 2 * py + pc
            src = src_refs[a].at[pid] if src_by_peer else src_refs[a]
            start = pltpu.make_async_remote_copy(
                src_ref=src, dst_ref=land_refs[a].at[my_id], send_sem=send_sems.at[7 * a + f - 1],
                recv_sem=recv_sems.at[7 * a + f - 1], device_id=(px, py, pc), device_id_type=_MESH)
            landed = pltpu.make_async_remote_copy(
                src_ref=src, dst_ref=land_refs[a].at[pid], send_sem=send_sems.at[7 * a + f - 1],
                recv_sem=recv_sems.at[7 * a + f - 1], device_id=(px, py, pc), device_id_type=_MESH)
            out.append((start, landed))
    return out


def _push_start(name, srcs, src_by_peer, after):
    n = len(srcs)
    lands = [jax.ShapeDtypeStruct((N_DEV,) + (s.shape[1:] if src_by_peer else s.shape), s.dtype) for s in srcs]

    def body(*refs):
        src_refs, land_refs = refs[:n], refs[n:2 * n]
        send_sems, recv_sems = refs[2 * n + 1], refs[2 * n + 2]
        token = refs[-1]
        for start, _ in _push_copies(src_refs, land_refs, send_sems, recv_sems, src_by_peer):
            start.start()
        token[...] = jnp.zeros_like(token)

    hbm = lambda a: pltpu.with_memory_space_constraint(a, pltpu.HBM)
    res = pl.pallas_call(
        body, name=name,
        out_shape=(pltpu.SemaphoreType.DMA((7 * n,)), pltpu.SemaphoreType.DMA((7 * n,)),
                   *[pltpu.HBM(s.shape, s.dtype) for s in srcs], *[pltpu.HBM(s.shape, s.dtype) for s in lands],
                   jax.ShapeDtypeStruct((8, LANE), F32)),
        in_specs=[_HBM] * (2 * n) + [_ANY],
        out_specs=(_SEM, _SEM, *[_HBM] * (2 * n), pl.BlockSpec(memory_space=pltpu.VMEM)),
        input_output_aliases={i: 2 + i for i in range(2 * n)},
        compiler_params=pltpu.CompilerParams(has_side_effects=_EFFECT),
    )(*[hbm(s) for s in srcs], *[hbm(lax.empty(s.shape, s.dtype)) for s in lands], after)
    return res[0], res[1], list(res[2:2 + n]), list(res[2 + n:2 + 2 * n]), res[-1]


def _push_wait(name, send_sems, recv_sems, srcs, lands, src_by_peer, after):
    n = len(srcs)
    after = list(after) if isinstance(after, (list, tuple)) else [after]

    def body(*refs):
        src_refs, land_refs = refs[:n], refs[n:2 * n]
        s_sems, r_sems = refs[2 * n], refs[2 * n + 1]
        for _, landed in _push_copies(src_refs, land_refs, s_sems, r_sems, src_by_peer):
            landed.wait_send()
            landed.wait_recv()

    res = pl.pallas_call(
        body, name=name,
        out_shape=tuple(pltpu.HBM(s.shape, s.dtype) for s in list(srcs) + list(lands)),
        in_specs=[_HBM] * (2 * n) + [_SEM, _SEM] + [_ANY] * len(after),
        out_specs=tuple([_HBM] * (2 * n)),
        input_output_aliases={i: i for i in range(2 * n)},
        compiler_params=pltpu.CompilerParams(has_side_effects=_EFFECT),
    )(*srcs, *lands, send_sems, recv_sems, *after)
    return list(res[:n]), list(res[n:])


_SHARDED = (
    ("meta_tokens", 1, (N_META, D_MODEL)),
    ("w_in", 1, (D_MODEL, IN_COLS)),
    ("w_q_b", 1, (Q_LORA, MLA_HEADS * QK_HEAD)),
    ("w_kv_b", 1, (KV_LORA, MLA_HEADS * (QK_NOPE + V_HEAD))),
    ("dn_conv_w", 1, (DN_CONV, 3 * DN_WIDTH)),
    ("w_out", 0, (2 * DN_WIDTH, D_MODEL)),
    ("w_gate", 1, (D_MODEL, D_FF)),
    ("w_up", 1, (D_MODEL, D_FF)),
    ("ffn_conv_w", 1, (FFN_CONV, D_FF)),
    ("w_down", 0, (D_FF, D_MODEL)),
)
_MXU_GATHERED = ("w_in", "w_q_b", "w_kv_b", "w_out", "w_gate", "w_up", "w_down")
_F32_GATHERED = ("meta_tokens", "dn_conv_w", "ffn_conv_w")
_EARLY = ("w_in", "w_q_b", "w_kv_b")
_LATE = ("w_out", "w_gate", "w_up", "w_down")
_TRANSPOSED = ("w_in", "w_q_b", "w_gate", "w_up")
_REPLICATED = (
    ("attn_norm_w", D_MODEL), ("q_a_norm_w", Q_LORA), ("kv_a_norm_w", KV_LORA), ("q_norm_w", QK_HEAD),
    ("k_norm_w", QK_HEAD), ("mla_out_norm_w", V_HEAD), ("dn_A_log", DN_HEADS), ("dn_dt_bias", DN_HEADS),
    ("dn_out_norm_w", DN_DIM), ("ffn_norm_w", D_MODEL), ("ffn_conv_b", D_FF),
)
_PACK_COLS = 1024
_PACK_ROW_MULT = 320
_SMALL_SHAPE = (8, 768)
_SMALL_BLOCK = (8, 512)


def _local_shape(dim, shape):
    return (shape[0] // N_DEV, shape[1]) if dim == 0 else (shape[0], shape[1] // N_DEV)


def _pack_rows(n, mult):
    rows = -(-n // _PACK_COLS)
    return -(-rows // mult) * mult


def _pack(flats, mult, axis=0):
    cat = jnp.concatenate(flats, axis=-1)
    n = cat.shape[-1]
    r = _pack_rows(n, mult)
    pad = [(0, 0)] * (cat.ndim - 1) + [(0, r * _PACK_COLS - n)]
    return jnp.pad(cat, pad).reshape(cat.shape[:-1] + (r, _PACK_COLS))


def _to_blocks(full, dim):
    r, c = full.shape
    if dim == 0:
        return full.reshape(N_DEV, (r // N_DEV) * c)
    return full.reshape(r, N_DEV, c // N_DEV).transpose(1, 0, 2).reshape(N_DEV, r * (c // N_DEV))


def _from_blocks(blocks, dim, shape):
    r, c = shape
    if dim == 0:
        return blocks.reshape(r, c)
    return blocks.reshape(N_DEV, r, c // N_DEV).transpose(1, 0, 2).reshape(r, c)


def _split(flat, sizes):
    out, o = [], 0
    for s in sizes:
        out.append(flat[..., o:o + s])
        o += s
    return out


def _gather_weights(local, names, dtype, mult):
    specs = [s for s in _SHARDED if s[0] in names]
    pack = _pack([local[n].astype(dtype).reshape(-1) for n, _, _ in specs], mult)
    got = _all_gather("gather_" + "_".join(n[:5] for n in names[:2]), pack)
    flat = got.reshape(N_DEV, -1)
    sizes = [math.prod(_local_shape(d, s)) for _, d, s in specs]
    return {n: _from_blocks(p, d, s) for (n, d, s), p in zip(specs, _split(flat, sizes))}


def kernel(x, meta_tokens, attn_norm_w, w_in, q_a_norm_w, w_q_b, kv_a_norm_w, w_kv_b, q_norm_w, k_norm_w, mla_out_norm_w, dn_conv_w, dn_A_log, dn_dt_bias, dn_out_norm_w, w_out, ffn_norm_w, w_gate, w_up, ffn_conv_w, ffn_conv_b, w_down, loss_target, m_meta_tokens, m_attn_norm_w, m_w_in, m_q_a_norm_w, m_w_q_b, m_kv_a_norm_w, m_w_kv_b, m_q_norm_w, m_k_norm_w, m_mla_out_norm_w, m_dn_conv_w, m_dn_A_log, m_dn_dt_bias, m_dn_out_norm_w, m_w_out, m_ffn_norm_w, m_w_gate, m_w_up, m_ffn_conv_w, m_ffn_conv_b, m_w_down, v_meta_tokens, v_attn_norm_w, v_w_in, v_q_a_norm_w, v_w_q_b, v_kv_a_norm_w, v_w_kv_b, v_q_norm_w, v_k_norm_w, v_mla_out_norm_w, v_dn_conv_w, v_dn_A_log, v_dn_dt_bias, v_dn_out_norm_w, v_w_out, v_ffn_norm_w, v_w_gate, v_w_up, v_ffn_conv_w, v_ffn_conv_b, v_w_down):
    names = [n for n, _, _ in _SHARDED] + [n for n, _ in _REPLICATED]
    given = dict(locals())
    two_d = lambda a: a.reshape(a.shape[-2:])
    view = lambda a, n: two_d(a).T if n in _TRANSPOSED else two_d(a)
    wl = {n: view(given[n], n) for n in names}
    ml = {n: view(given["m_" + n], n) for n in names}
    vl = {n: view(given["v_" + n], n) for n in names}
    out_shapes = {n: given[n].shape for n in names}

    spec = {n: (d, s) for n, d, s in _SHARDED}
    small_sizes = [math.prod(_local_shape(*spec[n])) for n in _F32_GATHERED]

    def small_block(d):
        cat = jnp.concatenate([d[n].reshape(d[n].shape[:-2] + (-1,)) for n in _F32_GATHERED], axis=-1)
        pad = [(0, 0)] * (cat.ndim - 1) + [(0, math.prod(_SMALL_BLOCK) - cat.shape[-1])]
        return jnp.pad(cat, pad).reshape(cat.shape[:-1] + _SMALL_BLOCK)

    def shard(n):
        return wl[n].astype(_MXU)

    def from_slots(n, blocks):
        d, s = spec[n]
        if d == 0 or n in _TRANSPOSED:
            return blocks.reshape(-1, blocks.shape[-1])
        return blocks.transpose(1, 0, 2).reshape(s)

    my_id = 4 * lax.axis_index("x") + 2 * lax.axis_index("y") + lax.axis_index("c")
    got = _all_gather_many("gather_early", [shard(n) for n in _EARLY] + [small_block(wl)])
    full = {n: a for n, a in wl.items() if n not in _LATE}
    for n, blocks in zip(_EARLY, got):
        full[n] = from_slots(n, blocks)
    for n, p in zip(_F32_GATHERED, _split(got[-1].reshape(N_DEV, -1), small_sizes)):
        full[n] = _from_blocks(p, *spec[n])
    late_own = [shard(n) for n in _LATE]
    l_send, l_recv, l_src, l_land, token = _push_start("gather_late_start", late_own, False, got[-1])

    def late_weights(after):
        _, lands = _push_wait("gather_late_wait", l_send, l_recv, l_src, l_land, False, after)
        out = {}
        for n, land, own in zip(_LATE, lands, late_own):
            out[n] = from_slots(n, lax.dynamic_update_slice(land, own[None], (my_id, 0, 0))).astype(_MXU)
        return out

    def dest_blocks(n, a):
        d, s = spec[n]
        r, c = _local_shape(d, s)
        if n in _TRANSPOSED:
            return a.reshape(N_DEV, c, r)
        return a.reshape(N_DEV, r, c) if d == 0 else a.reshape(r, N_DEV, c).transpose(1, 0, 2)

    pushed = []

    def grads_ready(g, names):
        nat = _grads_to_natural({n: g[n] for n in names})
        blocks = [dest_blocks(n, nat[n]).astype(_MXU) for n in names]
        sends, recvs, srcs, lands, tok = _push_start("rs_" + names[0] + "_start", blocks, True, token)
        pushed.append((names, sends, recvs, srcs, lands))
        return tok

    seq = x.shape[1]
    tp = ROW0 + seq
    h0 = jnp.concatenate([jnp.zeros((PAD, D_MODEL), F32), full["meta_tokens"], x[0]], axis=0)
    tgt = jnp.concatenate([jnp.zeros((ROW0, D_MODEL), F32), loss_target[0]], axis=0)
    loss, dh0, raw = _local_step(h0, tgt, _prepare(full, tp), token, late_weights, grads_ready)
    g = _grads_to_natural(raw)
    g["meta_tokens"] = dh0[PAD:ROW0]
    grad_x = dh0[ROW0:][None]

    big = [{}, {}, {}, {}]
    rep_names = [n for n, _ in _REPLICATED]
    raw_key = {"dn_A_log": "alog_b", "dn_dt_bias": "dtb_b"}
    pieces = [raw[raw_key.get(n, n)] for n in rep_names] + [loss]
    pieces += [g[n].reshape(1, -1) for n in _F32_GATHERED]
    widths = [p.shape[1] for p in pieces]
    offs = [sum(widths[:k]) for k in range(len(widths))]
    cat = jnp.concatenate(pieces, axis=1)
    cols = -(-cat.shape[1] // (8 * LANE)) * LANE
    mine = jnp.pad(cat, ((0, 0), (0, 8 * cols - cat.shape[1]))).reshape(8, cols)
    everyone = _all_gather("gather_small_grads", mine)
    total = _sum_parts("sum_small_grads", [(everyone, d) for d in range(N_DEV)]).reshape(1, 8 * cols)
    tot = {n: total[0, o:o + wd] for n, o, wd in zip(rep_names + ["loss"] + list(_F32_GATHERED), offs, widths)}
    lanes = lambda a: jnp.pad(a, ((0, 0), (0, -a.shape[1] % LANE)))
    items = [(o, -(-size // LANE) * LANE, n in raw_key) for (n, size), o in zip(_REPLICATED, offs)]
    sm = _adam_vectors("adam_replicated", total, items, [lanes(wl[n]) for n in rep_names],
                       [lanes(ml[n]) for n in rep_names], [lanes(vl[n]) for n in rep_names])
    sm = [{n: a[:, :size] for (n, size), a in zip(_REPLICATED, kind)} for kind in sm]
    mine_of = {}
    for n in _F32_GATHERED:
        d, s = spec[n]
        r, c = _local_shape(d, s)
        mine_of[n] = lax.dynamic_slice(tot[n].reshape(s), (0, my_id * c), (r, c))
    res = _adam("adam_small_sharded", [(small_block(mine_of)[None], 0)], small_block(wl), small_block(ml),
                small_block(vl))
    for kind, a in enumerate(res):
        big[kind].update(zip(_F32_GATHERED, _split(a.reshape(-1), small_sizes)))

    for names, sends, recvs, srcs, lands in pushed:
        srcs, lands = _push_wait("rs_" + names[0] + "_wait", sends, recvs, srcs, lands, True, dh0)
        for n, src, land in zip(names, srcs, lands):
            parts = [(src, my_id)] + [(land, my_id ^ f) for f in range(1, N_DEV)]
            for kind, a in enumerate(_adam("adam_" + n, parts, wl[n], ml[n], vl[n])):
                big[kind][n] = a

    outs = [tot["loss"][0], grad_x]
    for kind in range(4):
        for n in ("meta_tokens", "attn_norm_w", "w_in", "q_a_norm_w", "w_q_b", "kv_a_norm_w", "w_kv_b", "q_norm_w",
                  "k_norm_w", "mla_out_norm_w", "dn_conv_w", "dn_A_log", "dn_dt_bias", "dn_out_norm_w", "w_out",
                  "ffn_norm_w", "w_gate", "w_up", "ffn_conv_w", "ffn_conv_b", "w_down"):
            src = big[kind] if n in big[kind] else sm[kind]
            a = src[n].T if n in _TRANSPOSED else src[n]
            outs.append(a.reshape(out_shapes[n]))
    return tuple(outs)
```

```python
import functools
import math

import jax
import jax.numpy as jnp
from jax import lax
from jax.experimental import pallas as pl
from jax.experimental.pallas import tpu as pltpu

F32 = jnp.float32
BF16 = jnp.bfloat16
_MXU = jnp.bfloat16
_HI = lax.Precision.HIGHEST

D_MODEL = 1024
N_META = 16
PAD = 112
ROW0 = PAD + N_META
MLA_HEADS = 4
QK_NOPE = 128
QK_ROPE = 64
QK_HEAD = QK_NOPE + QK_ROPE
V_HEAD = 128
Q_LORA = 256
KV_LORA = 256
ROPE_THETA = 10000.0
DN_HEADS = 4
DN_DIM = 128
DN_WIDTH = DN_HEADS * DN_DIM
DN_CONV = 4
DN_CHUNK = 64
GDN_SUB_CHUNKS = 2
D_FF = 2816
FFN_CONV = 3
EPS = 1e-6
HP = 256
C_QKV = 0
C_Z = 1536
C_QL = 2048
C_KVL = 2304
C_KPE = 2560
C_AB = 2688
IN_P = 2816
IN_COLS = 2632

ADAM_LR = 0.001
ADAM_B1 = 0.9
ADAM_B2 = 0.999
ADAM_EPS = 1e-08
ADAM_WD = 0.01
ADAM_STEP = 10

N_DEV = 8
TM = 128
LANE = 128
VMEM_LIMIT = 56 * 1024 * 1024
NEG = -1e30


def _dot(a, b, dims, hp=False):
    if hp:
        return lax.dot_general(a.astype(F32), b.astype(F32), (dims, ((), ())),
                               precision=lax.Precision.HIGH if hp == "3x" else _HI, preferred_element_type=F32)
    return lax.dot_general(a.astype(_MXU), b.astype(_MXU), (dims, ((), ())),
                           preferred_element_type=F32)


def _nn(a, b, hp=False):
    return _dot(a, b, ((1,), (0,)), hp)


def _nt(a, b, hp=False):
    return _dot(a, b, ((1,), (1,)), hp)


def _tn(a, b, hp=False):
    return _dot(a, b, ((0,), (0,)), hp)


def _sigmoid(x):
    return 1.0 / (1.0 + jnp.exp(-x))


def _rms_fwd(x, w, n):
    r = lax.rsqrt(jnp.sum(x * x, axis=-1, keepdims=True) * (1.0 / n) + EPS)
    return x * r * w, r


def _rms_bwd(x, w, dy, n):
    r = lax.rsqrt(jnp.sum(x * x, axis=-1, keepdims=True) * (1.0 / n) + EPS)
    xh = x * r
    gy = dy * w
    dx = r * (gy - xh * (jnp.sum(gy * xh, axis=-1, keepdims=True) * (1.0 / n)))
    return dx, dy * xh


def _rowsum(x):
    return jnp.sum(x, axis=0, keepdims=True)


def _row_ids(i, tm):
    return i * tm + lax.broadcasted_iota(jnp.int32, (tm, 1), 0)


def _shift_down(ext, s, tm):
    if s == 0:
        return ext[8:8 + tm]
    return pltpu.roll(ext, s, 0)[8:8 + tm]


def _shift_up(ext, s, tm):
    if s == 0:
        return ext[0:tm]
    return pltpu.roll(ext, tm + 8 - s, 0)[0:tm]


def _conv_fwd(x, halo_prev, w, width):
    tm = x.shape[0]
    ext = jnp.concatenate([halo_prev, x], axis=0)
    y = None
    for j in range(width):
        t = w[j:j + 1, :] * _shift_down(ext, width - 1 - j, tm)
        y = t if y is None else y + t
    return y


def _conv_bwd_x(dy, halo_next, w, width):
    tm = dy.shape[0]
    ext = jnp.concatenate([dy, halo_next], axis=0)
    dx = None
    for j in range(width):
        t = w[j:j + 1, :] * _shift_up(ext, width - 1 - j, tm)
        dx = t if dx is None else dx + t
    return dx


def _conv_bwd_w(dy, x, halo_prev, width):
    tm = dy.shape[0]
    ext = jnp.concatenate([halo_prev, x], axis=0)
    rows = [_rowsum(dy * _shift_down(ext, width - 1 - j, tm)) for j in range(width)]
    rows += [jnp.zeros_like(rows[0])] * (8 - width)
    return jnp.concatenate(rows, axis=0)


def _softplus(x):
    e = jnp.exp(-jnp.abs(x))
    u = 1.0 + e
    l1p = jnp.where(u == 1.0, e, jnp.log(u) * e / jnp.where(u == 1.0, 1.0, u - 1.0))
    return jnp.maximum(x, 0.0) + l1p


def _swap_halves(x):
    lane = lax.broadcasted_iota(jnp.int32, x.shape, 1)
    return jnp.where(lane < 32, pltpu.roll(x, 96, 1), jnp.where(lane < 64, pltpu.roll(x, 32, 1), 0.0))


class _In:
    def __init__(self, arr, width=None, cb=0, kind="cur"):
        self.arr, self.kind = arr, kind
        self.width = arr.shape[1] if width is None else width
        self.cb = cb


def _whole_spec(x):
    return pl.BlockSpec(x.shape, lambda i, nd=x.ndim: (0,) * nd, pipeline_mode=pl.Buffered(1))


def _tile_spec(t, tm, tp):
    r8 = tm // 8
    if t.kind == "cur":
        return pl.BlockSpec((tm, t.width), lambda i, cb=t.cb: (i, cb))
    if t.kind == "prev":
        return pl.BlockSpec((8, t.width), lambda i, cb=t.cb: (jnp.maximum(i * r8 - 1, 0), cb))
    return pl.BlockSpec((8, t.width), lambda i, cb=t.cb: (jnp.minimum((i + 1) * r8, tp // 8 - 1), cb))


def _rows(name, fn, tiled, full, outs, accs=(), tm=TM):
    tp = tiled[0].arr.shape[0]
    nt = tp // tm
    r8 = tm // 8
    n_in = len(tiled) + len(full)
    n_out = len(outs)

    def body(*refs):
        i = pl.program_id(0)
        vals = [r[...] for r in refs[:n_in]]
        o_t, o_a = fn(i, *vals)
        for r, v in zip(refs[n_in:n_in + n_out], o_t):
            r[...] = v.astype(r.dtype)
        for r, v in zip(refs[n_in + n_out:], o_a):
            @pl.when(i == 0)
            def _():
                r[...] = v

            @pl.when(i > 0)
            def _():
                r[...] += v

    in_specs = [_tile_spec(t, tm, tp) for t in tiled]
    in_specs += [pl.BlockSpec(a.shape, lambda i, nd=a.ndim: (0,) * nd) for a in full]
    out_specs = [pl.BlockSpec((tm, w), lambda i: (i, 0)) for w, _ in outs]
    out_specs += [pl.BlockSpec((r, w), lambda i: (0, 0)) for r, w in accs]
    out_shape = [jax.ShapeDtypeStruct((tp, w), dt) for w, dt in outs]
    out_shape += [jax.ShapeDtypeStruct((r, w), F32) for r, w in accs]
    res = pl.pallas_call(
        body, name=name, grid=(nt,), in_specs=in_specs, out_specs=out_specs, out_shape=out_shape,
        compiler_params=pltpu.CompilerParams(dimension_semantics=("arbitrary",), vmem_limit_bytes=VMEM_LIMIT),
    )(*[t.arr for t in tiled], *full)
    return res


def _pick(n, cap, mult):
    best = None
    for d in range(mult, min(n, cap) + 1, mult):
        if n % d == 0:
            best = d
    assert best is not None, (n, cap, mult)
    return best


_ANY_SPEC = pl.BlockSpec(memory_space=pl.ANY)


def _mm(name, a, b, mode, out_dtype=F32, resid=None, after=None):
    if mode == "tn":
        m, k = a.shape
        n = b.shape[1]
        tk = _pick(k, 512, 128)
        tn = _pick(n, 1408, 128)

        def body_tn(a_ref, b_ref, o_ref):
            o_ref[...] = _tn(a_ref[...], b_ref[...]).astype(o_ref.dtype)

        return pl.pallas_call(
            body_tn, name=name, grid=(n // tn, k // tk),
            in_specs=[pl.BlockSpec((m, tk), lambda j, p: (0, p)),
                      pl.BlockSpec((m, tn), lambda j, p: (0, j))],
            out_specs=pl.BlockSpec((tk, tn), lambda j, p: (p, j)),
            out_shape=jax.ShapeDtypeStruct((k, n), out_dtype),
            compiler_params=pltpu.CompilerParams(
                dimension_semantics=("parallel", "parallel"), vmem_limit_bytes=VMEM_LIMIT),
        )(a, b)

    m, k = a.shape
    n = b.shape[1] if mode == "nn" else b.shape[0]
    tn = _pick(n, 1408, 128)
    tm = _pick(m, 1152, 16)
    dotf = _nn if mode == "nn" else _nt

    def body(*refs):
        a_ref, b_ref, o_ref = refs[0], refs[1], refs[-1]
        acc = dotf(a_ref[...], b_ref[...])
        if resid is not None:
            acc = refs[2][...] + acc
        o_ref[...] = acc.astype(o_ref.dtype)

    b_spec = (pl.BlockSpec((k, tn), lambda j, i: (0, j)) if mode == "nn"
              else pl.BlockSpec((tn, k), lambda j, i: (j, 0)))
    in_specs = [pl.BlockSpec((tm, k), lambda j, i: (i, 0)), b_spec]
    args = [a, b]
    if resid is not None:
        in_specs.append(pl.BlockSpec((tm, tn), lambda j, i: (i, j)))
        args.append(resid)
    if after is not None:
        in_specs.append(_ANY_SPEC)
        args.append(after)
    return pl.pallas_call(
        body, name=name, grid=(n // tn, m // tm), in_specs=in_specs,
        out_specs=pl.BlockSpec((tm, tn), lambda j, i: (i, j)),
        out_shape=jax.ShapeDtypeStruct((m, n), out_dtype),
        compiler_params=pltpu.CompilerParams(
            dimension_semantics=("parallel", "parallel"), vmem_limit_bytes=VMEM_LIMIT),
    )(*args)


def _mm_tn2(name, a1, a2, b, out_dtype=F32):
    m, k = a1.shape
    n = b.shape[1]
    tk = _pick(k, 512, 128)

    def body(a1_ref, a2_ref, b_ref, o1_ref, o2_ref):
        bb = b_ref[...]
        o1_ref[...] = _tn(a1_ref[...], bb).astype(o1_ref.dtype)
        o2_ref[...] = _tn(a2_ref[...], bb).astype(o2_ref.dtype)

    a_spec = pl.BlockSpec((m, tk), lambda p: (0, p))
    o_spec = pl.BlockSpec((tk, n), lambda p: (p, 0))
    return pl.pallas_call(
        body, name=name, grid=(k // tk,),
        in_specs=[a_spec, a_spec, pl.BlockSpec((m, n), lambda p: (0, 0))],
        out_specs=[o_spec, o_spec], out_shape=[jax.ShapeDtypeStruct((k, n), out_dtype)] * 2,
        compiler_params=pltpu.CompilerParams(dimension_semantics=("parallel",), vmem_limit_bytes=VMEM_LIMIT),
    )(a1, a2, b)


def _norm_mm(name, x, norm_w, b, mode="nt", x_cb=0, after=None):
    m = x.shape[0]
    k = norm_w.shape[1]
    n = b.shape[0] if mode == "nt" else b.shape[1]
    tn = _pick(n, 1408, 128)
    tm = _pick(m, 1152, 16)
    dotf = _nt if mode == "nt" else _nn
    extra = [] if after is None else [after]

    def body(x_ref, w_ref, b_ref, *rest):
        o_ref, u_ref = rest[-2:]

        @pl.when(pl.program_id(1) == 0)
        def _():
            u_ref[...] = _rms_fwd(x_ref[...], w_ref[...], k)[0].astype(u_ref.dtype)

        o_ref[...] = dotf(u_ref[...], b_ref[...])

    b_spec = (pl.BlockSpec((tn, k), lambda i, j: (j, 0)) if mode == "nt"
              else pl.BlockSpec((k, tn), lambda i, j: (0, j)))
    return pl.pallas_call(
        body, name=name, grid=(m // tm, n // tn),
        in_specs=[pl.BlockSpec((tm, k), lambda i, j: (i, x_cb)), pl.BlockSpec((1, k), lambda i, j: (0, 0)),
                  b_spec] + [_ANY_SPEC] * len(extra),
        out_specs=[pl.BlockSpec((tm, tn), lambda i, j: (i, j)), pl.BlockSpec((tm, k), lambda i, j: (i, 0))],
        out_shape=[jax.ShapeDtypeStruct((m, n), F32), jax.ShapeDtypeStruct((m, k), _MXU)],
        compiler_params=pltpu.CompilerParams(
            dimension_semantics=("arbitrary", "arbitrary"), vmem_limit_bytes=VMEM_LIMIT),
    )(x, norm_w, b, *extra)


def _pro_mm(name, fn, tiled, full, k, b, resid):
    m = resid.shape[0]
    n = b.shape[1]
    tm = _pick(m, 576, 16)
    n_in = len(tiled) + len(full)

    def body(*refs):
        i = pl.program_id(0)
        u = fn(i, *[r[...] for r in refs[:n_in]]).astype(_MXU)
        b_ref, r_ref, o_ref, u_ref = refs[n_in:]
        u_ref[...] = u
        o_ref[...] = r_ref[...] + _nn(u, b_ref[...])

    row = lambda w: pl.BlockSpec((tm, w), lambda i: (i, 0))
    in_specs = [_tile_spec(t, tm, m) for t in tiled]
    in_specs += [_whole_spec(x) for x in full] + [_whole_spec(b), row(n)]
    return pl.pallas_call(
        body, name=name, grid=(m // tm,), in_specs=in_specs, out_specs=[row(n), row(k)],
        out_shape=[jax.ShapeDtypeStruct((m, n), F32), jax.ShapeDtypeStruct((m, k), _MXU)],
        compiler_params=pltpu.CompilerParams(dimension_semantics=("parallel",), vmem_limit_bytes=VMEM_LIMIT),
    )(*[t.arr for t in tiled], *full, b, resid)


def _ffn_in(h2, norm_w, w_gate_t, w_up_t, conv_w8, conv_b):
    m, k = h2.shape
    n = w_gate_t.shape[0]
    tm = _pick(m, 288, 16)

    def body(x_ref, xp_ref, nw_ref, wg_ref, wu_ref, cw_ref, cb_ref, hn_ref, gp_ref, up_ref, act_ref):
        i = pl.program_id(0)
        nw = nw_ref[...]
        hn = _rms_fwd(x_ref[...], nw, k)[0].astype(_MXU)
        hn_prev = _rms_fwd(xp_ref[...], nw, k)[0].astype(_MXU)
        wg = wg_ref[...]
        gp = _nt(hn, wg)
        gp_prev = jnp.where(i > 0, _nt(hn_prev, wg), 0.0)
        up = _nt(hn, wu_ref[...])
        gate = _conv_fwd(gp, gp_prev, cw_ref[...], FFN_CONV) + cb_ref[...]
        hn_ref[...] = hn
        gp_ref[...] = gp
        up_ref[...] = up
        act_ref[...] = (_silu_parts(gate)[0] * up).astype(act_ref.dtype)

    row = lambda w: pl.BlockSpec((tm, w), lambda i: (i, 0))
    r8 = tm // 8
    return pl.pallas_call(
        body, name="ffn_in", grid=(m // tm,),
        in_specs=[row(k), pl.BlockSpec((8, k), lambda i: (jnp.maximum(i * r8 - 1, 0), 0)), _whole_spec(norm_w),
                  _whole_spec(w_gate_t), _whole_spec(w_up_t), _whole_spec(conv_w8), _whole_spec(conv_b)],
        out_specs=[row(k), row(n), row(n), row(n)],
        out_shape=[jax.ShapeDtypeStruct((m, k), _MXU), jax.ShapeDtypeStruct((m, n), F32),
                   jax.ShapeDtypeStruct((m, n), F32), jax.ShapeDtypeStruct((m, n), _MXU)],
        compiler_params=pltpu.CompilerParams(dimension_semantics=("parallel",), vmem_limit_bytes=VMEM_LIMIT),
    )(h2, h2, norm_w, w_gate_t, w_up_t, conv_w8, conv_b)


def _mm_rows(name, a, b, mode, fn, tiled, full, outs, accs=(), tm_cap=576):
    a_list = list(a) if isinstance(a, (list, tuple)) else [a]
    b_list = list(b) if isinstance(b, (list, tuple)) else [b]
    na = len(a_list)
    m = a_list[0].shape[0]
    tm = _pick(m, tm_cap, 16)
    dotf = _nn if mode == "nn" else _nt
    n_in = len(tiled) + len(full)
    n_out = len(outs)
    first = 2 * na

    def body(*refs):
        i = pl.program_id(0)
        vals = [r[...] for r in refs[first:first + n_in]]
        acc = dotf(refs[0][...], refs[na][...])
        for p in range(1, na):
            acc = acc + dotf(refs[p][...], refs[na + p][...])
        o_t, o_a = fn(i, acc, *vals)
        for r, v in zip(refs[first + n_in:first + n_in + n_out], o_t):
            r[...] = v.astype(r.dtype)
        for r, v in zip(refs[first + n_in + n_out:], o_a):
            @pl.when(i == 0)
            def _():
                r[...] = v

            @pl.when(i > 0)
            def _():
                r[...] += v

    whole = lambda x: pl.BlockSpec(x.shape, lambda i, nd=x.ndim: (0,) * nd)
    in_specs = [pl.BlockSpec((tm, x.shape[1]), lambda i: (i, 0)) for x in a_list] + [_whole_spec(x) for x in b_list]
    in_specs += [_tile_spec(t, tm, m) for t in tiled]
    in_specs += [whole(x) for x in full]
    out_specs = [pl.BlockSpec((tm, w), lambda i: (i, 0)) for w, _ in outs]
    out_specs += [pl.BlockSpec((r, w), lambda i: (0, 0)) for r, w in accs]
    out_shape = [jax.ShapeDtypeStruct((m, w), dt) for w, dt in outs]
    out_shape += [jax.ShapeDtypeStruct((r, w), F32) for r, w in accs]
    return pl.pallas_call(
        body, name=name, grid=(m // tm,), in_specs=in_specs, out_specs=out_specs, out_shape=out_shape,
        compiler_params=pltpu.CompilerParams(dimension_semantics=("arbitrary",), vmem_limit_bytes=VMEM_LIMIT),
    )(*a_list, *b_list, *[t.arr for t in tiled], *full)


ATTN_Q_TILES = 4


def _attn_probs(q, k, row0):
    tq, tp = q.shape[0], k.shape[0]
    s = _nt(q, k) * (1.0 / math.sqrt(QK_HEAD))
    row = row0 + lax.broadcasted_iota(jnp.int32, (tq, tp), 0)
    col = lax.broadcasted_iota(jnp.int32, (tq, tp), 1)
    ok = (col <= row) & (col >= PAD)
    s = jnp.where(ok, s, NEG)
    m = jnp.max(s, axis=-1, keepdims=True)
    e = jnp.exp(s - m)
    return e * (1.0 / jnp.sum(e, axis=-1, keepdims=True))


def _attn_fwd(q, k, v):
    tp = q.shape[0]
    tq = tp // ATTN_Q_TILES

    def body(q_ref, k_ref, v_ref, o_ref):
        for i in range(ATTN_Q_TILES):
            rows = slice(i * tq, (i + 1) * tq)
            keys = slice(0, (i + 1) * tq)
            p = _attn_probs(q_ref[rows, :], k_ref[keys, :], i * tq)
            o_ref[rows, :] = _nn(p, v_ref[keys, :])

    return pl.pallas_call(
        body, name="attn_fwd", grid=(MLA_HEADS,),
        in_specs=[pl.BlockSpec((tp, HP), lambda h: (0, h)),
                  pl.BlockSpec((tp, HP), lambda h: (0, h)),
                  pl.BlockSpec((tp, V_HEAD), lambda h: (0, h))],
        out_specs=pl.BlockSpec((tp, V_HEAD), lambda h: (0, h)),
        out_shape=jax.ShapeDtypeStruct((tp, MLA_HEADS * V_HEAD), F32),
        compiler_params=pltpu.CompilerParams(dimension_semantics=("parallel",), vmem_limit_bytes=VMEM_LIMIT),
    )(q, k, v)


def _attn_bwd(q, k, v, do):
    tp = q.shape[0]
    tq = tp // ATTN_Q_TILES

    def body(q_ref, k_ref, v_ref, do_ref, dq_ref, dk_ref, dv_ref):
        for i in reversed(range(ATTN_Q_TILES)):
            rows = slice(i * tq, (i + 1) * tq)
            keys = slice(0, (i + 1) * tq)
            qb = q_ref[rows, :]
            kk = k_ref[keys, :]
            dob = do_ref[rows, :]
            p = _attn_probs(qb, kk, i * tq)
            dp = _nt(dob, v_ref[keys, :])
            delta = jnp.sum(p * dp, axis=-1, keepdims=True)
            ds = p * (dp - delta) * (1.0 / math.sqrt(QK_HEAD))
            dq_ref[rows, :] = _nn(ds, kk)
            if i == ATTN_Q_TILES - 1:
                dk_ref[...] = _tn(ds, qb)
                dv_ref[...] = _tn(p, dob)
            else:
                dk_ref[keys, :] += _tn(ds, qb)
                dv_ref[keys, :] += _tn(p, dob)

    full = lambda w: pl.BlockSpec((tp, w), lambda h: (0, h))
    return pl.pallas_call(
        body, name="attn_bwd", grid=(MLA_HEADS,),
        in_specs=[full(HP), full(HP), full(V_HEAD), full(V_HEAD)],
        out_specs=[full(HP), full(HP), full(V_HEAD)],
        out_shape=[jax.ShapeDtypeStruct((tp, MLA_HEADS * HP), F32),
                   jax.ShapeDtypeStruct((tp, MLA_HEADS * HP), F32),
                   jax.ShapeDtypeStruct((tp, MLA_HEADS * V_HEAD), F32)],
        compiler_params=pltpu.CompilerParams(dimension_semantics=("parallel",), vmem_limit_bytes=VMEM_LIMIT),
    )(q, k, v, do)


def _gdn_consts():
    c = DN_CHUNK
    r = lax.broadcasted_iota(jnp.int32, (c, c), 0)
    cc = lax.broadcasted_iota(jnp.int32, (c, c), 1)
    incl = r >= cc
    strict = r > cc
    return incl, strict


def _cumsum_rows(x, reverse=False):
    c = x.shape[0]
    row = lax.broadcasted_iota(jnp.int32, x.shape, 0)
    s = 1
    while s < c:
        if reverse:
            x = x + jnp.where(row < c - s, pltpu.roll(x, c - s, 0), 0.0)
        else:
            x = x + jnp.where(row >= s, pltpu.roll(x, s, 0), 0.0)
        s *= 2
    return x


def _each(fn, *lists):
    return [fn(*a) for a in zip(*lists)]


def _interleave(chains):
    chains = list(chains)
    while chains:
        for ch in list(chains):
            try:
                next(ch)
            except StopIteration:
                chains.remove(ch)


def _gdn_chunk_common(q_ref, k_ref, v_ref, g_ref, b_ref):
    c = DN_CHUNK
    incl, strict = _gdn_consts()
    sls = [(slice(c * sub, c * (sub + 1)), slice(DN_DIM * h, DN_DIM * (h + 1)))
           for sub in range(GDN_SUB_CHUNKS) for h in range(DN_HEADS)]
    q = [q_ref[sl] * (1.0 / math.sqrt(DN_DIM)) for sl in sls]
    k = [k_ref[sl] for sl in sls]
    v = [v_ref[sl] for sl in sls]
    g = [g_ref[sl] for sl in sls]
    beta = [b_ref[sl] for sl in sls]
    gc = [_cumsum_rows(x) for x in g]
    grow = [x.T[:c, :] for x in gc]
    kb = _each(jnp.multiply, k, beta)
    kk = _each(_nt, kb, k)
    qk = _each(_nt, q, k)
    gam = [jnp.exp(x) for x in gc]
    g_last = [_rowsum(x) for x in g]
    dm = [jnp.exp(jnp.where(incl, x[:, :c] - y, NEG)) for x, y in zip(gc, grow)]
    vb = _each(jnp.multiply, v, beta)
    kbg = _each(jnp.multiply, kb, gam)
    ek = [jnp.exp(x - y) for x, y in zip(g_last, gc)]
    kd = _each(jnp.multiply, k, ek)
    return dict(q=q, k=k, v=v, beta=beta, gc=gc, gam=gam, g_last=g_last, dm=dm, kb=kb, vb=vb,
                kbg=kbg, kk=kk, ek=ek, kd=kd, qk=qk, incl=incl, strict=strict, sls=sls)


def _gdn_fwd(q, k, v, g, beta):
    tp = q.shape[0]
    c = DN_CHUNK
    nch = tp // c

    def body(q_ref, k_ref, v_ref, g_ref, b_ref, o_ref, s_ref, t_ref, s_scr):
        @pl.when(pl.program_id(0) == 0)
        def _():
            s_scr[...] = jnp.zeros_like(s_scr)

        eye = (lax.broadcasted_iota(jnp.int32, (c, c), 0) == lax.broadcasted_iota(jnp.int32, (c, c), 1)).astype(F32)
        x = _gdn_chunk_common(q_ref, k_ref, v_ref, g_ref, b_ref)
        heads = range(DN_HEADS)
        bp = [-jnp.where(x["strict"], kk * dm, 0.0) for kk, dm in zip(x["kk"], x["dm"])]
        t = [eye + b for b in bp]
        for _ in range(5):
            bp = [_nn(b, b, hp="3x") for b in bp]
            t = [tt + _nn(tt, b, hp="3x") for tt, b in zip(t, bp)]
        u = _each(_nn, t, x["vb"])
        w = _each(_nn, t, x["kbg"])
        qg = _each(jnp.multiply, x["q"], x["gam"])
        mqk = _each(jnp.multiply, x["qk"], x["dm"])
        s = [s_scr[h] for h in heads]
        for sub in range(GDN_SUB_CHUNKS):
            e = [DN_HEADS * sub + h for h in heads]
            v_new = [u[i] - _nn(w[i], s[h]) for h, i in zip(heads, e)]
            o = [_nn(qg[i], s[h]) + _nn(mqk[i], v_new[h]) for h, i in zip(heads, e)]
            s_new = [s[h] * jnp.exp(x["g_last"][i]) + _tn(x["kd"][i], v_new[h]) for h, i in zip(heads, e)]
            for h, i in zip(heads, e):
                s_ref[h, sub] = s[h]
                t_ref[h, sub] = t[i]
                o_ref[x["sls"][i]] = o[h]
            s = s_new
        for h in heads:
            s_scr[h] = s[h]

    sub = GDN_SUB_CHUNKS
    rb = lambda n: (n, 0)
    return pl.pallas_call(
        body, name="gdn_fwd", grid=(nch // sub,),
        in_specs=[pl.BlockSpec((sub * c, DN_WIDTH), rb)] * 5,
        out_specs=[pl.BlockSpec((sub * c, DN_WIDTH), rb),
                   pl.BlockSpec((DN_HEADS, sub, DN_DIM, DN_DIM), lambda n: (0, n, 0, 0)),
                   pl.BlockSpec((DN_HEADS, sub, c, c), lambda n: (0, n, 0, 0))],
        out_shape=[jax.ShapeDtypeStruct((tp, DN_WIDTH), F32),
                   jax.ShapeDtypeStruct((DN_HEADS, nch, DN_DIM, DN_DIM), F32),
                   jax.ShapeDtypeStruct((DN_HEADS, nch, c, c), F32)],
        scratch_shapes=[pltpu.VMEM((DN_HEADS, DN_DIM, DN_DIM), F32)],
        compiler_params=pltpu.CompilerParams(dimension_semantics=("arbitrary",), vmem_limit_bytes=VMEM_LIMIT),
    )(q, k, v, g, beta)


def _gdn_bwd(q, k, v, g, beta, s_all, t_all, do):
    tp = q.shape[0]
    c = DN_CHUNK
    nch = tp // c

    def body(q_ref, k_ref, v_ref, g_ref, b_ref, s_ref, t_ref, do_ref,
             dq_ref, dk_ref, dv_ref, dg_ref, db_ref, ds_scr):
        @pl.when(pl.program_id(0) == 0)
        def _():
            ds_scr[...] = jnp.zeros_like(ds_scr)

        xs = _gdn_chunk_common(q_ref, k_ref, v_ref, g_ref, b_ref)

        ds_state = [ds_scr[h] for h in range(DN_HEADS)]

        def chain(sub, h):
            e = DN_HEADS * sub + h
            x = {key: (val[e] if isinstance(val, list) else val) for key, val in xs.items()}
            sl = x["sls"]
            qs, kx, vx, beta_, gam, dm = x["q"], x["k"], x["v"], x["beta"], x["gam"], x["dm"]
            kb, vb, kbg, kd, ek = x["kb"], x["vb"], x["kbg"], x["kd"], x["ek"]
            t = t_ref[h, sub]
            s = s_ref[h, sub]
            dsn = ds_state[h]
            dob = do_ref[sl]
            eg_last = jnp.exp(x["g_last"])
            u = _nn(t, vb)
            w = _nn(t, kbg)
            mqk = x["qk"] * dm
            qd = qs * gam
            dqd = _nt(dob, s)
            dkd_pre = _nn(kd, dsn)
            yield
            v_new = u - _nn(w, s)
            dv_new = _tn(mqk, dob) + dkd_pre
            dq = dqd * gam
            dgam = jnp.sum(dqd * qs, axis=1, keepdims=True)
            yield
            ds_state[h] = _tn(qd, dob) + eg_last * dsn - _tn(w, dv_new)
            dmm = jnp.where(x["incl"], _nt(dob, v_new), 0.0)
            dkd = _nt(v_new, dsn)
            dw = -_nt(dv_new, s)
            dvb = _tn(t, dv_new)
            dt = _nt(dv_new, vb)
            yield
            dqk = dmm * dm
            e_mat = dmm * mqk
            dq = dq + _nn(dqk, kx)
            dk = _tn(dqk, qs) + dkd * ek
            e1 = jnp.sum(dkd * kd, axis=1, keepdims=True)
            dgc = -e1
            dg_last = jnp.sum(e1) + eg_last * jnp.sum(s * dsn)
            dt = dt + _nt(dw, kbg)
            dkbg = _tn(t, dw)
            yield
            tdt = _tn(t, dt, hp="3x")
            yield
            da = jnp.where(x["strict"], -_nt(tdt, t, hp="3x"), 0.0)
            yield
            dkk = da * dm
            e_mat = e_mat + da * x["kk"] * dm
            dkb = _nn(dkk, kx) + dkbg * gam
            dk = dk + _tn(dkk, kb)
            dgam = dgam + jnp.sum(dkbg * kb, axis=1, keepdims=True)
            yield
            dk = dk + dkb * beta_
            dbeta = jnp.sum(dkb * kx, axis=1, keepdims=True) + jnp.sum(dvb * vx, axis=1, keepdims=True)
            dv = dvb * beta_
            dgc = dgc + jnp.sum(e_mat, axis=1, keepdims=True) + dgam * gam
            dgc = dgc - jnp.sum(e_mat.T, axis=1, keepdims=True)
            yield
            dg = _cumsum_rows(dgc, reverse=True) + dg_last
            yield
            dq_ref[sl] = dq * (1.0 / math.sqrt(DN_DIM))
            dk_ref[sl] = dk
            dv_ref[sl] = dv
            dg_ref[sl] = dg
            db_ref[sl] = jnp.broadcast_to(dbeta, (c, LANE))

        chains = []
        for sub in reversed(range(GDN_SUB_CHUNKS)):
            new = [chain(sub, h) for h in range(DN_HEADS)]
            for _ in range(3):
                for ch in new:
                    next(ch)
            chains += new
        _interleave(chains)
        for h in range(DN_HEADS):
            ds_scr[h] = ds_state[h]

    nblk = nch // GDN_SUB_CHUNKS
    sub = GDN_SUB_CHUNKS
    rb = lambda n: (nblk - 1 - n, 0)
    hs = lambda n: (0, nblk - 1 - n, 0, 0)
    return pl.pallas_call(
        body, name="gdn_bwd", grid=(nblk,),
        in_specs=[pl.BlockSpec((sub * c, DN_WIDTH), rb)] * 5
        + [pl.BlockSpec((DN_HEADS, sub, DN_DIM, DN_DIM), hs), pl.BlockSpec((DN_HEADS, sub, c, c), hs),
           pl.BlockSpec((sub * c, DN_WIDTH), rb)],
        out_specs=[pl.BlockSpec((sub * c, DN_WIDTH), rb)] * 5,
        out_shape=[jax.ShapeDtypeStruct((tp, DN_WIDTH), F32)] * 5,
        scratch_shapes=[pltpu.VMEM((DN_HEADS, DN_DIM, DN_DIM), F32)],
        compiler_params=pltpu.CompilerParams(dimension_semantics=("arbitrary",), vmem_limit_bytes=VMEM_LIMIT),
    )(q, k, v, g, beta, s_all, t_all, do)


def _silu_parts(x):
    s = _sigmoid(x)
    return x * s, s * (1.0 + x * (1.0 - s))


def _f_rms_cast(i, x, w):
    y, _ = _rms_fwd(x, w, x.shape[1])
    return (y,), ()


def _f_rms_bwd_add(i, x, dy, dres, w, *, mask_pad):
    dx, dwr = _rms_bwd(x, w, dy, x.shape[1])
    out = dres + dx
    if mask_pad:
        out = jnp.where(_row_ids(i, x.shape[0]) >= PAD, out, 0.0)
    return (out,), (_rowsum(dwr),)


def _f_lat_norm(i, ql, kvl, qw, kvw):
    return (_rms_fwd(ql, qw, Q_LORA)[0], _rms_fwd(kvl, kvw, KV_LORA)[0]), ()


def _f_lat_norm_bwd(i, ql, kvl, dqn, dkvn, qw, kvw):
    dq, dqw = _rms_bwd(ql, qw, dqn, Q_LORA)
    dk, dkw = _rms_bwd(kvl, kvw, dkvn, KV_LORA)
    return (dq, dk), (_rowsum(dqw), _rowsum(dkw))


def _rope(x, cos, sin_s):
    return x * cos + _swap_halves(x) * sin_s


def _rope_t(dy, cos, sin_s):
    return dy * cos + _swap_halves(dy * sin_s)


def _f_mla_qk(i, qf, kvf, kpe, cos, sin_s, qw, kw):
    qs, ks, vs = [], [], []
    for h in range(MLA_HEADS):
        qn, _ = _rms_fwd(qf[:, HP * h:HP * (h + 1)], qw, QK_HEAD)
        qs += [qn[:, :QK_NOPE], _rope(qn[:, QK_NOPE:], cos, sin_s)]
        kh = jnp.concatenate([kvf[:, HP * h:HP * h + QK_NOPE], kpe], axis=1)
        kn, _ = _rms_fwd(kh, kw, QK_HEAD)
        ks += [kn[:, :QK_NOPE], _rope(kn[:, QK_NOPE:], cos, sin_s)]
        vs.append(kvf[:, HP * h + QK_NOPE:HP * (h + 1)])
    return (jnp.concatenate(qs, axis=1), jnp.concatenate(ks, axis=1), jnp.concatenate(vs, axis=1)), ()


def _f_mla_front(i, ql, kvl, kpe, cos, sin_s, qaw, kvaw, wq_t, wkv, qw, kw):
    qn = _rms_fwd(ql, qaw, Q_LORA)[0].astype(_MXU)
    kvn = _rms_fwd(kvl, kvaw, KV_LORA)[0].astype(_MXU)
    qf = _nt(qn, wq_t)
    kvf = _nn(kvn, wkv)
    (q, k, v), _ = _f_mla_qk(i, qf, kvf, kpe, cos, sin_s, qw, kw)
    return (qn, kvn, qf, kvf, q, k, v), ()


def _f_mla_back(i, qf, kvf, kpe, cos, sin_s, dq, dk, dv, ql, kvl, qaw, kvaw, wq_t, wkv, qw, kw):
    (dqf, dkvf, dkpe), (dqw, dkw) = _f_mla_qk_bwd(i, qf, kvf, kpe, cos, sin_s, dq, dk, dv, qw, kw)
    dqf = dqf.astype(_MXU)
    dkvf = dkvf.astype(_MXU)
    dql, dqaw = _rms_bwd(ql, qaw, _nn(dqf, wq_t), Q_LORA)
    dkvl, dkvaw = _rms_bwd(kvl, kvaw, _nt(dkvf, wkv), KV_LORA)
    return (dqf, dkvf, dkpe, dql, dkvl), (dqw, dkw, _rowsum(dqaw), _rowsum(dkvaw))


def _f_mla_qk_bwd(i, qf, kvf, kpe, cos, sin_s, dq, dk, dv, qw, kw):
    dqf, dkvf = [], []
    dkpe = None
    dqw = None
    dkw = None
    for h in range(MLA_HEADS):
        dqh = dq[:, HP * h:HP * (h + 1)]
        dqn = jnp.concatenate([dqh[:, :QK_NOPE], _rope_t(dqh[:, QK_NOPE:], cos, sin_s)], axis=1)
        dx, dwr = _rms_bwd(qf[:, HP * h:HP * (h + 1)], qw, dqn, QK_HEAD)
        dqf.append(dx)
        dqw = _rowsum(dwr) if dqw is None else dqw + _rowsum(dwr)
        dkh = dk[:, HP * h:HP * (h + 1)]
        dkn = jnp.concatenate([dkh[:, :QK_NOPE], _rope_t(dkh[:, QK_NOPE:], cos, sin_s)], axis=1)
        kh = jnp.concatenate([kvf[:, HP * h:HP * h + QK_NOPE], kpe], axis=1)
        dx, dwr = _rms_bwd(kh, kw, dkn, QK_HEAD)
        dkvf += [dx[:, :QK_NOPE], dv[:, V_HEAD * h:V_HEAD * (h + 1)]]
        dkpe = dx[:, QK_NOPE:] if dkpe is None else dkpe + dx[:, QK_NOPE:]
        dkw = _rowsum(dwr) if dkw is None else dkw + _rowsum(dwr)
    return (jnp.concatenate(dqf, axis=1), jnp.concatenate(dkvf, axis=1), dkpe), (dqw, dkw)


def _gdn_act(i, x, halo, w8):
    tm = x.shape[0]
    halo = jnp.where(i > 0, halo, 0.0)
    c = _conv_fwd(x, halo, w8, DN_CONV)
    act, dact = _silu_parts(c)
    return act, dact


def _spread_heads(ab):
    tm = ab.shape[0]
    return jnp.concatenate([jnp.broadcast_to(ab[:, h:h + 1], (tm, DN_DIM)) for h in range(2 * DN_HEADS)], axis=1)


def _gather_heads(x):
    tm = x.shape[0]
    lane = lax.broadcasted_iota(jnp.int32, (tm, LANE), 1)
    out = jnp.zeros((tm, LANE), F32)
    for h in range(2 * DN_HEADS):
        out = out + jnp.where(lane == h, x[:, DN_DIM * h:DN_DIM * h + 1], 0.0)
    return out


def _f_gdn_prep(i, x, halo, ab, w8, alog, dtb):
    tm = x.shape[0]
    act, _ = _gdn_act(i, x, halo, w8)
    outs = []
    for part in range(2):
        for h in range(DN_HEADS):
            t = act[:, DN_WIDTH * part + DN_DIM * h:DN_WIDTH * part + DN_DIM * (h + 1)]
            outs.append(t * lax.rsqrt(jnp.sum(t * t, axis=-1, keepdims=True) + EPS))
    q = jnp.concatenate(outs[:DN_HEADS], axis=1)
    k = jnp.concatenate(outs[DN_HEADS:], axis=1)
    v = act[:, 2 * DN_WIDTH:]
    abb = _spread_heads(ab)
    valid = _row_ids(i, tm) >= PAD
    g = jnp.where(valid, -jnp.exp(alog) * _softplus(abb[:, :DN_WIDTH] + dtb), 0.0)
    beta = jnp.where(valid, _sigmoid(abb[:, DN_WIDTH:]), 0.0)
    return (q, k, v, g, beta), ()


def _f_gdn_prep_bwd(i, x, x_prev, x_next, ab, dq, dq_next, dk, dk_next, dv, dv_next, dg, dbeta,
                    w8, alog, dtb, *, nt):
    tm = x.shape[0]
    x_prev = jnp.where(i > 0, x_prev, 0.0)
    more = i < nt - 1
    ext = lambda t, t_next: jnp.concatenate([t, jnp.where(more, t_next, 0.0)], axis=0)
    c = _conv_fwd(jnp.concatenate([x, x_next], axis=0), x_prev, w8, DN_CONV)
    act, dact = _silu_parts(c)
    douts = []
    for part, dd in enumerate((ext(dq, dq_next), ext(dk, dk_next))):
        for h in range(DN_HEADS):
            t = act[:, DN_WIDTH * part + DN_DIM * h:DN_WIDTH * part + DN_DIM * (h + 1)]
            r = lax.rsqrt(jnp.sum(t * t, axis=-1, keepdims=True) + EPS)
            y = t * r
            dy = dd[:, DN_DIM * h:DN_DIM * (h + 1)]
            douts.append(r * (dy - y * jnp.sum(dy * y, axis=-1, keepdims=True)))
    douts.append(ext(dv, dv_next))
    dc = jnp.concatenate(douts, axis=1) * dact
    dqkv = _conv_bwd_x(dc[:tm], dc[tm:], w8, DN_CONV)
    dconv_w = _conv_bwd_w(dc[:tm], x, x_prev, DN_CONV)
    abb = _spread_heads(ab)
    valid = _row_ids(i, tm) >= PAD
    pre = abb[:, :DN_WIDTH] + dtb
    ea = jnp.exp(alog)
    g = -ea * _softplus(pre)
    dg = jnp.where(valid, dg, 0.0)
    dbeta = jnp.where(valid, dbeta, 0.0)
    da = dg * (-ea) * _sigmoid(pre)
    beta = _sigmoid(abb[:, DN_WIDTH:])
    db = dbeta * beta * (1.0 - beta)
    dab = _gather_heads(jnp.concatenate([da, db], axis=1))
    return (dqkv, dab), (dconv_w, _rowsum(dg * g), _rowsum(da))


def _f_conv_bwd(i, dy, dy_next, x, x_prev, w8, *, width, nt):
    dy_next = jnp.where(i < nt - 1, dy_next, 0.0)
    x_prev = jnp.where(i > 0, x_prev, 0.0)
    return (_conv_bwd_x(dy, dy_next, w8, width),), (_conv_bwd_w(dy, x, x_prev, width),)


def _f_mix(i, o_mla, o_dn, z, w_mla, w_dn):
    tm = o_mla.shape[0]
    valid = _row_ids(i, tm) >= PAD
    outs = []
    for h in range(MLA_HEADS):
        y, _ = _rms_fwd(o_mla[:, V_HEAD * h:V_HEAD * (h + 1)], w_mla, V_HEAD)
        outs.append(jnp.where(valid, y, 0.0))
    for h in range(DN_HEADS):
        y, _ = _rms_fwd(o_dn[:, DN_DIM * h:DN_DIM * (h + 1)], w_dn, DN_DIM)
        outs.append(y * _silu_parts(z[:, DN_DIM * h:DN_DIM * (h + 1)])[0])
    return (jnp.concatenate(outs, axis=1),), ()


def _f_mix_bwd(i, o_mla, o_dn, z, dy_mla, dy_dn, w_mla, w_dn):
    tm = o_mla.shape[0]
    valid = _row_ids(i, tm) >= PAD
    d_mla, d_dn, d_z = [], [], []
    dw_mla = None
    dw_dn = None
    for h in range(MLA_HEADS):
        sl = slice(V_HEAD * h, V_HEAD * (h + 1))
        dx, dwr = _rms_bwd(o_mla[:, sl], w_mla, jnp.where(valid, dy_mla[:, sl], 0.0), V_HEAD)
        d_mla.append(dx)
        dw_mla = _rowsum(dwr) if dw_mla is None else dw_mla + _rowsum(dwr)
    for h in range(DN_HEADS):
        sl = slice(DN_DIM * h, DN_DIM * (h + 1))
        y, _ = _rms_fwd(o_dn[:, sl], w_dn, DN_DIM)
        sz, dsz = _silu_parts(z[:, sl])
        d_z.append(dy_dn[:, sl] * y * dsz)
        dx, dwr = _rms_bwd(o_dn[:, sl], w_dn, dy_dn[:, sl] * sz, DN_DIM)
        d_dn.append(dx)
        dw_dn = _rowsum(dwr) if dw_dn is None else dw_dn + _rowsum(dwr)
    return ((jnp.concatenate(d_mla, axis=1), jnp.concatenate(d_dn, axis=1), jnp.concatenate(d_z, axis=1)),
            (dw_mla, dw_dn))


def _f_ffn_act(i, gate_pre, halo, up, w8, b):
    halo = jnp.where(i > 0, halo, 0.0)
    gate = _conv_fwd(gate_pre, halo, w8, FFN_CONV) + b
    return (_silu_parts(gate)[0] * up,), ()


def _f_ffn_act_bwd(i, gp, gp_prev, gp_next, up, up_next, dact, dact_next, w8, b, *, nt):
    tm = gp.shape[0]
    gp_prev = jnp.where(i > 0, gp_prev, 0.0)
    dact_next = jnp.where(i < nt - 1, dact_next, 0.0)
    cat = lambda t, t_next: jnp.concatenate([t, t_next], axis=0)
    gate = _conv_fwd(cat(gp, gp_next), gp_prev, w8, FFN_CONV) + b
    sg, dsg = _silu_parts(gate)
    dact_e = cat(dact, dact_next)
    dgate = dact_e * cat(up, up_next) * dsg
    dgate_pre = _conv_bwd_x(dgate[:tm], dgate[tm:], w8, FFN_CONV)
    dup = dact * sg[:tm]
    return (dgate_pre, dup), (_conv_bwd_w(dgate[:tm], gp, gp_prev, FFN_CONV), _rowsum(dgate[:tm]))


def _f_loss(i, h3, tgt):
    tm = h3.shape[0]
    diff = jnp.where(_row_ids(i, tm) >= ROW0, h3 - tgt, 0.0)
    part = 0.5 * jnp.sum(diff * diff) * (1.0 / D_MODEL)
    return (diff * (1.0 / D_MODEL),), (jnp.full((1, LANE), part, F32),)


def _after(fn):
    return lambda i, *a: fn(i, *a[:-1])


def _local_step(h0, tgt, w, token, late_weights, grads_ready):
    tp = h0.shape[0]
    nt = tp // TM
    bf = (D_MODEL, _MXU)
    proj, u = _norm_mm("in_proj", h0, w["attn_norm_w"], w["w_in"], after=token)
    p_qkv = lambda kind="cur": _In(proj, 3 * DN_WIDTH, 0, kind)
    p_z = _In(proj, DN_WIDTH, C_Z // DN_WIDTH)
    p_ql = _In(proj, Q_LORA, C_QL // Q_LORA)
    p_kvl = _In(proj, KV_LORA, C_KVL // KV_LORA)
    p_kpe = _In(proj, LANE, C_KPE // LANE)
    p_ab = _In(proj, LANE, C_AB // LANE)
    cos, sin_s = _In(w["cos"]), _In(w["sin_s"])

    mla_w = [w["q_a_norm_w"], w["kv_a_norm_w"], w["w_q_b"], w["w_kv_b"], w["q_norm_w"], w["k_norm_w"]]
    tm_mla = _pick(tp, 288, 16)
    wide = MLA_HEADS * HP
    qn, kvn, qf, kvf, q, k, v = _rows(
        "mla_front", _f_mla_front, [p_ql, p_kvl, p_kpe, cos, sin_s], mla_w,
        [(Q_LORA, _MXU), (KV_LORA, _MXU), (wide, F32), (wide, F32), (wide, _MXU), (wide, _MXU),
         (MLA_HEADS * V_HEAD, _MXU)], tm=tm_mla)
    o_mla = _attn_fwd(q, k, v)

    dn_w = [w["dn_conv_w"], w["alog_b"], w["dtb_b"]]
    gq, gk, gv, gg, gb = _rows("gdn_prep", _f_gdn_prep, [p_qkv(), p_qkv("prev"), p_ab], dn_w,
                               [(DN_WIDTH, F32)] * 5)
    o_dn, s_all, t_all = _gdn_fwd(gq, gk, gv, gg, gb)

    out_w = [w["mla_out_norm_w"], w["dn_out_norm_w"]]
    w = dict(w, **late_weights((o_mla, o_dn), _LATE[:3]))
    h2, mixed = _pro_mm("mix_out_proj", lambda i, *t: _f_mix(i, *t)[0][0], [_In(o_mla), _In(o_dn), p_z], out_w,
                        D_MODEL, w["w_out"], h0)

    ffn_w = [w["ffn_conv_w"], w["ffn_conv_b"]]
    hn, gate_pre, up, act = _ffn_in(h2, w["ffn_norm_w"], w["w_gate"], w["w_up"], *ffn_w)
    w = dict(w, **late_weights(act, _LATE[3:]))
    dh3, loss = _mm_rows("ffn_down_loss", act, w["w_down"], "nn", lambda i, y, r, t: _f_loss(i, r + y, t),
                         [_In(h2), _In(tgt)], [], [(D_MODEL, F32)], [(1, LANE)])

    g = {}
    dact = _mm("ffn_down_dx", dh3, w["w_down"], "nt")
    g["w_down"] = _mm("ffn_down_dw", act, dh3, "tn", out_dtype=_MXU)
    dgate_pre, dup, g["ffn_conv_w"], g["ffn_conv_b"] = _rows(
        "ffn_act_bwd", functools.partial(_f_ffn_act_bwd, nt=nt),
        [_In(gate_pre), _In(gate_pre, kind="prev"), _In(gate_pre, kind="next"), _In(up), _In(up, kind="next"),
         _In(dact), _In(dact, kind="next")], ffn_w,
        [(D_FF, _MXU), (D_FF, _MXU)], [(8, D_FF), (1, D_FF)])
    g["w_gate"], g["w_up"] = _mm_tn2("ffn_gate_up_dw", dgate_pre, dup, hn, out_dtype=_MXU)
    tok = grads_ready(g, ("w_down", "w_gate", "w_up"))
    dh2, g["ffn_norm_w"] = _mm_rows(
        "ffn_gate_up_dx_rms", [dgate_pre, dup], [w["w_gate"], w["w_up"]], "nn",
        lambda i, dy, x, dres, nw, _tok: _f_rms_bwd_add(i, x, dy, dres, nw, mask_pad=True),
        [_In(h2), _In(dh3)], [w["ffn_norm_w"], tok], [(D_MODEL, F32)], [(1, D_MODEL)])

    g["w_out"] = _mm("out_proj_dw", mixed, dh2, "tn", out_dtype=_MXU)
    half = MLA_HEADS * V_HEAD
    do_mla, do_dn, dz, g["mla_out_norm_w"], g["dn_out_norm_w"] = _mm_rows(
        "out_proj_dx_mix", dh2, w["w_out"], "nt",
        lambda i, dm, om, od, z, wm, wd: _f_mix_bwd(i, om, od, z, dm[:, :half], dm[:, half:], wm, wd),
        [_In(o_mla), _In(o_dn), p_z], out_w,
        [(half, F32), (DN_WIDTH, F32), (DN_WIDTH, _MXU)], [(1, V_HEAD), (1, DN_DIM)])

    dq, dk, dv = _attn_bwd(q, k, v, do_mla)
    dqf, dkvf, dkpe, dql, dkvl, g["q_norm_w"], g["k_norm_w"], g["q_a_norm_w"], g["kv_a_norm_w"] = _rows(
        "mla_back", _f_mla_back,
        [_In(qf), _In(kvf), p_kpe, cos, sin_s, _In(dq), _In(dk), _In(dv), p_ql, p_kvl], mla_w,
        [(wide, _MXU), (wide, _MXU), (LANE, _MXU), (Q_LORA, _MXU), (KV_LORA, _MXU)],
        [(1, HP), (1, HP), (1, Q_LORA), (1, KV_LORA)], tm=tm_mla)
    g["w_q_b"] = _mm("mla_q_b_dw", dqf, qn, "tn")
    g["w_kv_b"] = _mm("mla_kv_b_dw", kvn, dkvf, "tn")
    tok = grads_ready(g, ("w_out", "w_q_b", "w_kv_b"))

    dgq, dgk, dgv, dgg, dgb = _gdn_bwd(gq, gk, gv, gg, gb, s_all, t_all, do_dn)
    nxt = lambda a: _In(a, kind="next")
    dqkv, dab, g["dn_conv_w"], g["alog_b"], g["dtb_b"] = _rows(
        "gdn_prep_bwd", _after(functools.partial(_f_gdn_prep_bwd, nt=nt)),
        [p_qkv(), p_qkv("prev"), p_qkv("next"), p_ab, _In(dgq), nxt(dgq), _In(dgk), nxt(dgk), _In(dgv), nxt(dgv),
         _In(dgg), _In(dgb)], dn_w + [tok],
        [(3 * DN_WIDTH, _MXU), (LANE, _MXU)], [(8, 3 * DN_WIDTH), (1, DN_WIDTH), (1, DN_WIDTH)])

    dproj = jnp.concatenate([dqkv, dz, dql, dkvl, dkpe, dab], axis=1)
    g["w_in"] = _mm("in_proj_dw", dproj, u, "tn", out_dtype=_MXU)
    tok = grads_ready(g, ("w_in",))
    dh0, g["attn_norm_w"] = _mm_rows(
        "in_proj_dx_rms", dproj, w["w_in"], "nn",
        lambda i, du, x, dres, nw, _tok: _f_rms_bwd_add(i, x, du, dres, nw, mask_pad=False),
        [_In(h0), _In(dh2)], [w["attn_norm_w"], tok], [(D_MODEL, F32)], [(1, D_MODEL)])
    return loss, dh0, g


def _w_in_to_padded(w):
    c1, c2, c3 = Q_LORA, Q_LORA + KV_LORA, Q_LORA + KV_LORA + QK_ROPE
    c4 = c3 + 3 * DN_WIDTH
    c5 = c4 + DN_WIDTH
    z = lambda n: jnp.zeros((n, w.shape[1]), w.dtype)
    return jnp.concatenate([w[c3:c4], w[c4:c5], w[:c1], w[c1:c2], w[c2:c3], z(LANE - QK_ROPE),
                            w[c5:], z(LANE - 2 * DN_HEADS)], axis=0)


def _w_in_from_padded(g):
    return jnp.concatenate([g[C_QL:C_QL + Q_LORA], g[C_KVL:C_KVL + KV_LORA], g[C_KPE:C_KPE + QK_ROPE],
                            g[:C_Z + DN_WIDTH], g[C_AB:C_AB + 2 * DN_HEADS]], axis=0)


def _w_q_b_to_padded(w):
    r = w.shape[1]
    w = w.reshape(MLA_HEADS, QK_HEAD, r)
    return jnp.pad(w, ((0, 0), (0, HP - QK_HEAD), (0, 0))).reshape(MLA_HEADS * HP, r)


def _w_q_b_from_padded(g):
    r = g.shape[1]
    return g.reshape(MLA_HEADS, HP, r)[:, :QK_HEAD].reshape(MLA_HEADS * QK_HEAD, r)


def _pad_rows8(w):
    return jnp.pad(w, ((0, 8 - w.shape[0]), (0, 0)))


def _prepare(full, tp):
    w = {}
    mx = lambda a: a.astype(_MXU)
    w["attn_norm_w"] = full["attn_norm_w"]
    w["w_in"] = mx(_w_in_to_padded(full["w_in"]))
    w["q_a_norm_w"] = full["q_a_norm_w"]
    w["kv_a_norm_w"] = full["kv_a_norm_w"]
    w["w_q_b"] = mx(_w_q_b_to_padded(full["w_q_b"]))
    w["w_kv_b"] = mx(full["w_kv_b"])
    w["q_norm_w"] = jnp.pad(full["q_norm_w"], ((0, 0), (0, HP - QK_HEAD)))
    w["k_norm_w"] = jnp.pad(full["k_norm_w"], ((0, 0), (0, HP - QK_HEAD)))
    w["mla_out_norm_w"] = full["mla_out_norm_w"]
    w["dn_out_norm_w"] = full["dn_out_norm_w"]
    w["dn_conv_w"] = _pad_rows8(full["dn_conv_w"])
    w["alog_b"] = jnp.repeat(full["dn_A_log"], DN_DIM, axis=1)
    w["dtb_b"] = jnp.repeat(full["dn_dt_bias"], DN_DIM, axis=1)
    w["ffn_norm_w"] = full["ffn_norm_w"]
    w["ffn_conv_w"] = _pad_rows8(full["ffn_conv_w"])
    w["ffn_conv_b"] = full["ffn_conv_b"]
    for n in _LATE:
        if n in full:
            w[n] = mx(full[n])
    half = QK_ROPE // 2
    inv = ROPE_THETA ** (-jnp.arange(half, dtype=F32) / half)
    ang = (jnp.arange(tp, dtype=jnp.int32) - PAD).astype(F32)[:, None] * inv[None, :]
    zc = jnp.zeros((tp, LANE - QK_ROPE), F32)
    w["cos"] = jnp.concatenate([jnp.cos(ang), jnp.cos(ang), zc], axis=1)
    w["sin_s"] = jnp.concatenate([-jnp.sin(ang), jnp.sin(ang), zc], axis=1)
    return w


def _grads_to_natural(g):
    convert = {
        "w_in": ("w_in", _w_in_from_padded),
        "w_q_b": ("w_q_b", _w_q_b_from_padded),
        "q_norm_w": ("q_norm_w", lambda a: a[:, :QK_HEAD]),
        "k_norm_w": ("k_norm_w", lambda a: a[:, :QK_HEAD]),
        "dn_conv_w": ("dn_conv_w", lambda a: a[:DN_CONV]),
        "ffn_conv_w": ("ffn_conv_w", lambda a: a[:FFN_CONV]),
        "alog_b": ("dn_A_log", lambda a: a[:, ::DN_DIM]),
        "dtb_b": ("dn_dt_bias", lambda a: a[:, ::DN_DIM]),
    }
    n = {}
    for key, a in g.items():
        name, fn = convert.get(key, (key, lambda t: t))
        n[name] = fn(a)
    return n


_MESH = pl.DeviceIdType.MESH
_ANY = pl.BlockSpec(memory_space=pl.ANY)
_CHIP_FLIPS = ((1, 0), (0, 1), (1, 1))


def _me():
    return lax.axis_index("x"), lax.axis_index("y"), lax.axis_index("c")


def _all_gather(name, blk):
    def body(x_ref, out_ref, send_sems, recv_sems, local_sem):
        x, y, c = _me()
        me, sib = (x, y, c), (x, y, 1 - c)
        chips = [(x ^ fx, y ^ fy) for fx, fy in _CHIP_FLIPS]

        def slot(p):
            return out_ref.at[4 * p[0] + 2 * p[1] + p[2]]

        def copy(k, block, to, src=None):
            return pltpu.make_async_remote_copy(
                src_ref=slot(block) if src is None else src, dst_ref=slot(block),
                send_sem=send_sems.at[k], recv_sem=recv_sems.at[k], device_id=to, device_id_type=_MESH)

        mine = pltpu.make_async_copy(x_ref, slot(me), local_sem)
        mine.start()
        first = [copy(0, me, sib, src=x_ref)]
        first += [copy(1 + j, me, (*chip, c), src=x_ref) for j, chip in enumerate(chips)]
        for cp in first:
            cp.start()
        passed = [copy(4 + j, (*chip, c), sib) for j, chip in enumerate(chips)]
        for j, chip in enumerate(chips):
            copy(1 + j, (*chip, c), me).wait_recv()
            passed[j].start()
        copy(0, sib, me).wait_recv()
        for j, chip in enumerate(chips):
            copy(4 + j, (*chip, 1 - c), me).wait_recv()
        for cp in first + passed:
            cp.wait_send()
        mine.wait()

    return pl.pallas_call(
        body, name=name, in_specs=[_ANY], out_specs=_ANY,
        out_shape=jax.ShapeDtypeStruct((N_DEV,) + blk.shape, blk.dtype),
        scratch_shapes=[pltpu.SemaphoreType.DMA((7,)), pltpu.SemaphoreType.DMA((7,)), pltpu.SemaphoreType.DMA],
    )(blk)


def _rs_sibling(name, gb):
    def body(g_ref, out_ref, send_sems, recv_sems):
        x, y, c = _me()
        cps = []
        for j in range(4):
            cp = pltpu.make_async_remote_copy(
                src_ref=g_ref.at[2 * j + (1 - c)], dst_ref=out_ref.at[j], send_sem=send_sems.at[j],
                recv_sem=recv_sems.at[j], device_id=(x, y, 1 - c), device_id_type=_MESH)
            cp.start()
            cps.append(cp)
        for cp in cps:
            cp.wait()

    return pl.pallas_call(
        body, name=name, in_specs=[_ANY], out_specs=_ANY,
        out_shape=jax.ShapeDtypeStruct((4,) + gb.shape[1:], gb.dtype),
        scratch_shapes=[pltpu.SemaphoreType.DMA((4,)), pltpu.SemaphoreType.DMA((4,))],
    )(gb)


def _rs_chips(name, s1):
    def body(s_ref, out_ref, send_sems, recv_sems):
        x, y, c = _me()
        cps = []
        for k, (fx, fy) in enumerate(_CHIP_FLIPS):
            px, py = x ^ fx, y ^ fy
            cp = pltpu.make_async_remote_copy(
                src_ref=s_ref.at[2 * px + py], dst_ref=out_ref.at[k], send_sem=send_sems.at[k],
                recv_sem=recv_sems.at[k], device_id=(px, py, c), device_id_type=_MESH)
            cp.start()
            cps.append(cp)
        for cp in cps:
            cp.wait()

    return pl.pallas_call(
        body, name=name, in_specs=[_ANY], out_specs=_ANY,
        out_shape=jax.ShapeDtypeStruct((3,) + s1.shape[1:], s1.dtype),
        scratch_shapes=[pltpu.SemaphoreType.DMA((3,)), pltpu.SemaphoreType.DMA((3,))],
    )(s1)


def _row_tile(r):
    divs = [d for d in range(16, min(r, 512) + 1, 16) if r % d == 0]
    return divs[-1] if divs else r


def _pair_sum(name, gb, recv):
    _, r, cols = gb.shape
    tm = _row_tile(r)
    c = lax.axis_index("c").astype(jnp.int32).reshape(1)

    def body(c_ref, a_ref, b_ref, o_ref, ob_ref):
        s = a_ref[...] + b_ref[...]
        o_ref[...] = s
        ob_ref[...] = s.astype(BF16)

    blk = pl.BlockSpec((1, tm, cols), lambda j, i, c_ref: (j, i, 0))
    return pl.pallas_call(
        body, name=name,
        grid_spec=pltpu.PrefetchScalarGridSpec(
            num_scalar_prefetch=1, grid=(4, r // tm),
            in_specs=[pl.BlockSpec((1, tm, cols), lambda j, i, c_ref: (2 * j + c_ref[0], i, 0)), blk],
            out_specs=[blk, blk]),
        out_shape=[jax.ShapeDtypeStruct((4, r, cols), F32), jax.ShapeDtypeStruct((4, r, cols), BF16)],
        compiler_params=pltpu.CompilerParams(dimension_semantics=("parallel", "parallel")),
    )(c, gb, recv)


def _adam_math(g, w, m, v):
    m_new = ADAM_B1 * m + (1.0 - ADAM_B1) * g
    v_new = ADAM_B2 * v + (1.0 - ADAM_B2) * (g * g)
    m_hat = m_new / (1.0 - ADAM_B1 ** ADAM_STEP)
    v_hat = v_new / (1.0 - ADAM_B2 ** ADAM_STEP)
    return -ADAM_LR * (m_hat / (jnp.sqrt(v_hat) + ADAM_EPS) + ADAM_WD * w), m_new, v_new


def _adam_vectors(name, row, items, ws, ms, vs):
    k = len(items)

    def body(row_ref, *refs):
        w_refs, m_refs, v_refs = refs[:k], refs[k:2 * k], refs[2 * k:3 * k]
        outs = refs[3 * k:]
        for idx, (off, n, per_head) in enumerate(items):
            if per_head:
                spread = row_ref[:, off:off + DN_WIDTH]
                lane = lax.broadcasted_iota(jnp.int32, (1, LANE), 1)
                g = jnp.zeros((1, LANE), F32)
                for h in range(DN_HEADS):
                    g = g + jnp.where(lane == h, spread[:, DN_DIM * h:DN_DIM * h + 1], 0.0)
            else:
                g = row_ref[:, off:off + n]
            d, m_new, v_new = _adam_math(g, w_refs[idx][...], m_refs[idx][...], v_refs[idx][...])
            for kind, val in enumerate((g, d, m_new, v_new)):
                outs[kind * k + idx][...] = val

    shapes = [jax.ShapeDtypeStruct((1, n), F32) for _, n, _ in items]
    res = pl.pallas_call(body, name=name, out_shape=shapes * 4)(row, *ws, *ms, *vs)
    return [list(res[kind * k:(kind + 1) * k]) for kind in range(4)]


def _sum_parts(name, parts):
    _, r, cols = parts[0][0].shape
    tm = _row_tile(r)
    idx = jnp.stack([jnp.asarray(s, jnp.int32) for _, s in parts])
    n = len(parts)

    def body(idx_ref, *refs):
        g = refs[0][0].astype(F32)
        for p_ref in refs[1:n]:
            g = g + p_ref[0].astype(F32)
        refs[n][...] = g

    return pl.pallas_call(
        body, name=name,
        grid_spec=pltpu.PrefetchScalarGridSpec(
            num_scalar_prefetch=1, grid=(r // tm,),
            in_specs=[pl.BlockSpec((1, tm, cols), lambda i, idx_ref, p=p: (idx_ref[p], i, 0)) for p in range(n)],
            out_specs=pl.BlockSpec((tm, cols), lambda i, idx_ref: (i, 0))),
        out_shape=jax.ShapeDtypeStruct((r, cols), F32),
        compiler_params=pltpu.CompilerParams(dimension_semantics=("parallel",)),
    )(idx, *[a for a, _ in parts])


def _adam(name, parts, w, m, v):
    r, cols = w.shape
    tm = _row_tile(r)
    idx = jnp.stack([jnp.asarray(s, jnp.int32) for _, s in parts])
    n = len(parts)

    def body(idx_ref, *refs):
        g = refs[0][0].astype(F32)
        for p_ref in refs[1:n]:
            g = g + p_ref[0].astype(F32)
        w_ref, m_ref, v_ref, g_out, d_out, m_out, v_out = refs[n:]
        g_out[...] = g
        d_out[...], m_out[...], v_out[...] = _adam_math(g, w_ref[...], m_ref[...], v_ref[...])

    part_specs = [pl.BlockSpec((1, tm, cols), lambda i, idx_ref, p=p: (idx_ref[p], i, 0)) for p in range(n)]
    flat = pl.BlockSpec((tm, cols), lambda i, idx_ref: (i, 0))
    return pl.pallas_call(
        body, name=name,
        grid_spec=pltpu.PrefetchScalarGridSpec(
            num_scalar_prefetch=1, grid=(r // tm,), in_specs=part_specs + [flat] * 3, out_specs=[flat] * 4),
        out_shape=[jax.ShapeDtypeStruct((r, cols), F32)] * 4,
        compiler_params=pltpu.CompilerParams(dimension_semantics=("parallel",)),
    )(idx, *[a for a, _ in parts], w, m, v)


def _all_gather_many(name, blks):
    n = len(blks)

    def body(*refs):
        x_refs, out_refs = refs[:n], refs[n:2 * n]
        send_sems, recv_sems, local_sems = refs[2 * n:]
        x, y, c = _me()
        me, sib = (x, y, c), (x, y, 1 - c)
        chips = [(x ^ fx, y ^ fy) for fx, fy in _CHIP_FLIPS]

        def slot(a, p):
            return out_refs[a].at[4 * p[0] + 2 * p[1] + p[2]]

        def copy(a, k, block, to, src=None):
            return pltpu.make_async_remote_copy(
                src_ref=slot(a, block) if src is None else src, dst_ref=slot(a, block),
                send_sem=send_sems.at[7 * a + k], recv_sem=recv_sems.at[7 * a + k], device_id=to,
                device_id_type=_MESH)

        mine = [pltpu.make_async_copy(x_refs[a], slot(a, me), local_sems.at[a]) for a in range(n)]
        first = []
        for a in range(n):
            mine[a].start()
            first.append(copy(a, 0, me, sib, src=x_refs[a]))
            first += [copy(a, 1 + j, me, (*chip, c), src=x_refs[a]) for j, chip in enumerate(chips)]
        for cp in first:
            cp.start()
        passed = []
        for j, chip in enumerate(chips):
            for a in range(n):
                copy(a, 1 + j, (*chip, c), me).wait_recv()
                cp = copy(a, 4 + j, (*chip, c), sib)
                cp.start()
                passed.append(cp)
        for a in range(n):
            copy(a, 0, sib, me).wait_recv()
            for j, chip in enumerate(chips):
                copy(a, 4 + j, (*chip, 1 - c), me).wait_recv()
        for cp in first + passed:
            cp.wait_send()
        for cp in mine:
            cp.wait()

    return pl.pallas_call(
        body, name=name, in_specs=[_ANY] * n, out_specs=[_ANY] * n,
        out_shape=[jax.ShapeDtypeStruct((N_DEV,) + b.shape, b.dtype) for b in blks],
        scratch_shapes=[pltpu.SemaphoreType.DMA((7 * n,)), pltpu.SemaphoreType.DMA((7 * n,)),
                        pltpu.SemaphoreType.DMA((n,))],
    )(*blks)


def _rs_sibling_many(name, gbs):
    n = len(gbs)

    def body(*refs):
        g_refs, out_refs = refs[:n], refs[n:2 * n]
        send_sems, recv_sems = refs[2 * n:]
        x, y, c = _me()
        cps = []
        for a in range(n):
            for j in range(4):
                cp = pltpu.make_async_remote_copy(
                    src_ref=g_refs[a].at[2 * j + (1 - c)], dst_ref=out_refs[a].at[j],
                    send_sem=send_sems.at[4 * a + j], recv_sem=recv_sems.at[4 * a + j],
                    device_id=(x, y, 1 - c), device_id_type=_MESH)
                cp.start()
                cps.append(cp)
        for cp in cps:
            cp.wait()

    return pl.pallas_call(
        body, name=name, in_specs=[_ANY] * n, out_specs=[_ANY] * n,
        out_shape=[jax.ShapeDtypeStruct((4,) + g.shape[1:], g.dtype) for g in gbs],
        scratch_shapes=[pltpu.SemaphoreType.DMA((4 * n,)), pltpu.SemaphoreType.DMA((4 * n,))],
    )(*gbs)


def _rs_chips_many(name, s1s):
    n = len(s1s)

    def body(*refs):
        s_refs, out_refs = refs[:n], refs[n:2 * n]
        send_sems, recv_sems = refs[2 * n:]
        x, y, c = _me()
        cps = []
        for a in range(n):
            for k, (fx, fy) in enumerate(_CHIP_FLIPS):
                px, py = x ^ fx, y ^ fy
                cp = pltpu.make_async_remote_copy(
                    src_ref=s_refs[a].at[2 * px + py], dst_ref=out_refs[a].at[k],
                    send_sem=send_sems.at[3 * a + k], recv_sem=recv_sems.at[3 * a + k],
                    device_id=(px, py, c), device_id_type=_MESH)
                cp.start()
                cps.append(cp)
        for cp in cps:
            cp.wait()

    return pl.pallas_call(
        body, name=name, in_specs=[_ANY] * n, out_specs=[_ANY] * n,
        out_shape=[jax.ShapeDtypeStruct((3,) + s.shape[1:], s.dtype) for s in s1s],
        scratch_shapes=[pltpu.SemaphoreType.DMA((3 * n,)), pltpu.SemaphoreType.DMA((3 * n,))],
    )(*s1s)


_HBM = pl.BlockSpec(memory_space=pltpu.HBM)
_SEM = pl.BlockSpec(memory_space=pltpu.SEMAPHORE)
_EFFECT = pltpu.SideEffectType.DATAFLOW_SIDE_EFFECTING


def _push_copies(src_refs, land_refs, send_sems, recv_sems, src_by_peer, first=0):
    x, y, c = _me()
    my_id = 4 * x + 2 * y + c
    out = []
    for k in range(len(src_refs)):
        a = first + k
        for f in range(1, N_DEV):
            px, py, pc = x ^ (f >> 2), y ^ ((f >> 1) & 1), c ^ (f & 1)
            pid = 4 * px + 2 * py + pc
            src = src_refs[k].at[pid] if src_by_peer else src_refs[k]
            start = pltpu.make_async_remote_copy(
                src_ref=src, dst_ref=land_refs[k].at[my_id], send_sem=send_sems.at[7 * a + f - 1],
                recv_sem=recv_sems.at[7 * a + f - 1], device_id=(px, py, pc), device_id_type=_MESH)
            landed = pltpu.make_async_remote_copy(
                src_ref=src, dst_ref=land_refs[k].at[pid], send_sem=send_sems.at[7 * a + f - 1],
                recv_sem=recv_sems.at[7 * a + f - 1], device_id=(px, py, pc), device_id_type=_MESH)
            out.append((start, landed))
    return out


def _push_start(name, srcs, src_by_peer, after):
    n = len(srcs)
    lands = [jax.ShapeDtypeStruct((N_DEV,) + (s.shape[1:] if src_by_peer else s.shape), s.dtype) for s in srcs]

    def body(*refs):
        src_refs, land_refs = refs[:n], refs[n:2 * n]
        send_sems, recv_sems = refs[2 * n + 1], refs[2 * n + 2]
        token = refs[-1]
        for start, _ in _push_copies(src_refs, land_refs, send_sems, recv_sems, src_by_peer):
            start.start()
        token[...] = jnp.zeros_like(token)

    hbm = lambda a: pltpu.with_memory_space_constraint(a, pltpu.HBM)
    res = pl.pallas_call(
        body, name=name,
        out_shape=(pltpu.SemaphoreType.DMA((7 * n,)), pltpu.SemaphoreType.DMA((7 * n,)),
                   *[pltpu.HBM(s.shape, s.dtype) for s in srcs], *[pltpu.HBM(s.shape, s.dtype) for s in lands],
                   jax.ShapeDtypeStruct((8, LANE), F32)),
        in_specs=[_HBM] * (2 * n) + [_ANY],
        out_specs=(_SEM, _SEM, *[_HBM] * (2 * n), pl.BlockSpec(memory_space=pltpu.VMEM)),
        input_output_aliases={i: 2 + i for i in range(2 * n)},
        compiler_params=pltpu.CompilerParams(has_side_effects=_EFFECT),
    )(*[hbm(s) for s in srcs], *[hbm(lax.empty(s.shape, s.dtype)) for s in lands], after)
    return res[0], res[1], list(res[2:2 + n]), list(res[2 + n:2 + 2 * n]), res[-1]


def _push_wait(name, send_sems, recv_sems, srcs, lands, src_by_peer, after, first=0):
    n = len(srcs)
    after = list(after) if isinstance(after, (list, tuple)) else [after]

    def body(*refs):
        src_refs, land_refs = refs[:n], refs[n:2 * n]
        s_sems, r_sems = refs[2 * n], refs[2 * n + 1]
        for _, landed in _push_copies(src_refs, land_refs, s_sems, r_sems, src_by_peer, first):
            landed.wait_send()
            landed.wait_recv()

    res = pl.pallas_call(
        body, name=name,
        out_shape=tuple(pltpu.HBM(s.shape, s.dtype) for s in list(srcs) + list(lands)),
        in_specs=[_HBM] * (2 * n) + [_SEM, _SEM] + [_ANY] * len(after),
        out_specs=tuple([_HBM] * (2 * n)),
        input_output_aliases={i: i for i in range(2 * n)},
        compiler_params=pltpu.CompilerParams(has_side_effects=_EFFECT),
    )(*srcs, *lands, send_sems, recv_sems, *after)
    return list(res[:n]), list(res[n:])


_SHARDED = (
    ("meta_tokens", 1, (N_META, D_MODEL)),
    ("w_in", 1, (D_MODEL, IN_COLS)),
    ("w_q_b", 1, (Q_LORA, MLA_HEADS * QK_HEAD)),
    ("w_kv_b", 1, (KV_LORA, MLA_HEADS * (QK_NOPE + V_HEAD))),
    ("dn_conv_w", 1, (DN_CONV, 3 * DN_WIDTH)),
    ("w_out", 0, (2 * DN_WIDTH, D_MODEL)),
    ("w_gate", 1, (D_MODEL, D_FF)),
    ("w_up", 1, (D_MODEL, D_FF)),
    ("ffn_conv_w", 1, (FFN_CONV, D_FF)),
    ("w_down", 0, (D_FF, D_MODEL)),
)
_MXU_GATHERED = ("w_in", "w_q_b", "w_kv_b", "w_out", "w_gate", "w_up", "w_down")
_F32_GATHERED = ("meta_tokens", "dn_conv_w", "ffn_conv_w")
_EARLY = ("w_in", "w_q_b", "w_kv_b")
_LATE = ("w_out", "w_gate", "w_up", "w_down")
_TRANSPOSED = ("w_in", "w_q_b", "w_gate", "w_up")
_REPLICATED = (
    ("attn_norm_w", D_MODEL), ("q_a_norm_w", Q_LORA), ("kv_a_norm_w", KV_LORA), ("q_norm_w", QK_HEAD),
    ("k_norm_w", QK_HEAD), ("mla_out_norm_w", V_HEAD), ("dn_A_log", DN_HEADS), ("dn_dt_bias", DN_HEADS),
    ("dn_out_norm_w", DN_DIM), ("ffn_norm_w", D_MODEL), ("ffn_conv_b", D_FF),
)
_PACK_COLS = 1024
_PACK_ROW_MULT = 320
_SMALL_SHAPE = (8, 768)
_SMALL_BLOCK = (8, 512)


def _local_shape(dim, shape):
    return (shape[0] // N_DEV, shape[1]) if dim == 0 else (shape[0], shape[1] // N_DEV)


def _pack_rows(n, mult):
    rows = -(-n // _PACK_COLS)
    return -(-rows // mult) * mult


def _pack(flats, mult, axis=0):
    cat = jnp.concatenate(flats, axis=-1)
    n = cat.shape[-1]
    r = _pack_rows(n, mult)
    pad = [(0, 0)] * (cat.ndim - 1) + [(0, r * _PACK_COLS - n)]
    return jnp.pad(cat, pad).reshape(cat.shape[:-1] + (r, _PACK_COLS))


def _to_blocks(full, dim):
    r, c = full.shape
    if dim == 0:
        return full.reshape(N_DEV, (r // N_DEV) * c)
    return full.reshape(r, N_DEV, c // N_DEV).transpose(1, 0, 2).reshape(N_DEV, r * (c // N_DEV))


def _from_blocks(blocks, dim, shape):
    r, c = shape
    if dim == 0:
        return blocks.reshape(r, c)
    return blocks.reshape(N_DEV, r, c // N_DEV).transpose(1, 0, 2).reshape(r, c)


def _split(flat, sizes):
    out, o = [], 0
    for s in sizes:
        out.append(flat[..., o:o + s])
        o += s
    return out


def _gather_weights(local, names, dtype, mult):
    specs = [s for s in _SHARDED if s[0] in names]
    pack = _pack([local[n].astype(dtype).reshape(-1) for n, _, _ in specs], mult)
    got = _all_gather("gather_" + "_".join(n[:5] for n in names[:2]), pack)
    flat = got.reshape(N_DEV, -1)
    sizes = [math.prod(_local_shape(d, s)) for _, d, s in specs]
    return {n: _from_blocks(p, d, s) for (n, d, s), p in zip(specs, _split(flat, sizes))}


def kernel(x, meta_tokens, attn_norm_w, w_in, q_a_norm_w, w_q_b, kv_a_norm_w, w_kv_b, q_norm_w, k_norm_w, mla_out_norm_w, dn_conv_w, dn_A_log, dn_dt_bias, dn_out_norm_w, w_out, ffn_norm_w, w_gate, w_up, ffn_conv_w, ffn_conv_b, w_down, loss_target, m_meta_tokens, m_attn_norm_w, m_w_in, m_q_a_norm_w, m_w_q_b, m_kv_a_norm_w, m_w_kv_b, m_q_norm_w, m_k_norm_w, m_mla_out_norm_w, m_dn_conv_w, m_dn_A_log, m_dn_dt_bias, m_dn_out_norm_w, m_w_out, m_ffn_norm_w, m_w_gate, m_w_up, m_ffn_conv_w, m_ffn_conv_b, m_w_down, v_meta_tokens, v_attn_norm_w, v_w_in, v_q_a_norm_w, v_w_q_b, v_kv_a_norm_w, v_w_kv_b, v_q_norm_w, v_k_norm_w, v_mla_out_norm_w, v_dn_conv_w, v_dn_A_log, v_dn_dt_bias, v_dn_out_norm_w, v_w_out, v_ffn_norm_w, v_w_gate, v_w_up, v_ffn_conv_w, v_ffn_conv_b, v_w_down):
    names = [n for n, _, _ in _SHARDED] + [n for n, _ in _REPLICATED]
    given = dict(locals())
    two_d = lambda a: a.reshape(a.shape[-2:])
    view = lambda a, n: two_d(a).T if n in _TRANSPOSED else two_d(a)
    wl = {n: view(given[n], n) for n in names}
    ml = {n: view(given["m_" + n], n) for n in names}
    vl = {n: view(given["v_" + n], n) for n in names}
    out_shapes = {n: given[n].shape for n in names}

    spec = {n: (d, s) for n, d, s in _SHARDED}
    small_sizes = [math.prod(_local_shape(*spec[n])) for n in _F32_GATHERED]

    def small_block(d):
        cat = jnp.concatenate([d[n].reshape(d[n].shape[:-2] + (-1,)) for n in _F32_GATHERED], axis=-1)
        pad = [(0, 0)] * (cat.ndim - 1) + [(0, math.prod(_SMALL_BLOCK) - cat.shape[-1])]
        return jnp.pad(cat, pad).reshape(cat.shape[:-1] + _SMALL_BLOCK)

    def shard(n):
        return wl[n].astype(_MXU)

    def from_slots(n, blocks):
        d, s = spec[n]
        if d == 0 or n in _TRANSPOSED:
            return blocks.reshape(-1, blocks.shape[-1])
        return blocks.transpose(1, 0, 2).reshape(s)

    my_id = 4 * lax.axis_index("x") + 2 * lax.axis_index("y") + lax.axis_index("c")
    got = _all_gather_many("gather_early", [shard(n) for n in _EARLY] + [small_block(wl)])
    full = {n: a for n, a in wl.items() if n not in _LATE}
    for n, blocks in zip(_EARLY, got):
        full[n] = from_slots(n, blocks)
    for n, p in zip(_F32_GATHERED, _split(got[-1].reshape(N_DEV, -1), small_sizes)):
        full[n] = _from_blocks(p, *spec[n])
    late_own = [shard(n) for n in _LATE]
    l_send, l_recv, l_src, l_land, token = _push_start("gather_late_start", late_own, False, got[-1])

    def late_weights(after, names):
        first = _LATE.index(names[0])
        sl = slice(first, first + len(names))
        _, lands = _push_wait("gather_late_wait_" + names[0], l_send, l_recv, l_src[sl], l_land[sl], False,
                              after, first)
        out = {}
        for n, land, own in zip(names, lands, late_own[sl]):
            out[n] = from_slots(n, lax.dynamic_update_slice(land, own[None], (my_id, 0, 0))).astype(_MXU)
        return out

    def dest_blocks(n, a):
        d, s = spec[n]
        r, c = _local_shape(d, s)
        if n in _TRANSPOSED:
            return a.reshape(N_DEV, c, r)
        return a.reshape(N_DEV, r, c) if d == 0 else a.reshape(r, N_DEV, c).transpose(1, 0, 2)

    pushed = []

    def grads_ready(g, names):
        nat = _grads_to_natural({n: g[n] for n in names})
        blocks = [dest_blocks(n, nat[n]).astype(_MXU) for n in names]
        sends, recvs, srcs, lands, tok = _push_start("rs_" + names[0] + "_start", blocks, True, token)
        pushed.append((names, sends, recvs, srcs, lands))
        return tok

    seq = x.shape[1]
    tp = ROW0 + seq
    h0 = jnp.concatenate([jnp.zeros((PAD, D_MODEL), F32), full["meta_tokens"], x[0]], axis=0)
    tgt = jnp.concatenate([jnp.zeros((ROW0, D_MODEL), F32), loss_target[0]], axis=0)
    loss, dh0, raw = _local_step(h0, tgt, _prepare(full, tp), token, late_weights, grads_ready)
    g = _grads_to_natural(raw)
    g["meta_tokens"] = dh0[PAD:ROW0]
    grad_x = dh0[ROW0:][None]

    big = [{}, {}, {}, {}]
    rep_names = [n for n, _ in _REPLICATED]
    raw_key = {"dn_A_log": "alog_b", "dn_dt_bias": "dtb_b"}
    pieces = [raw[raw_key.get(n, n)] for n in rep_names] + [loss]
    pieces += [g[n].reshape(1, -1) for n in _F32_GATHERED]
    widths = [p.shape[1] for p in pieces]
    offs = [sum(widths[:k]) for k in range(len(widths))]
    cat = jnp.concatenate(pieces, axis=1)
    cols = -(-cat.shape[1] // (8 * LANE)) * LANE
    mine = jnp.pad(cat, ((0, 0), (0, 8 * cols - cat.shape[1]))).reshape(8, cols)
    everyone = _all_gather("gather_small_grads", mine)
    total = _sum_parts("sum_small_grads", [(everyone, d) for d in range(N_DEV)]).reshape(1, 8 * cols)
    tot = {n: total[0, o:o + wd] for n, o, wd in zip(rep_names + ["loss"] + list(_F32_GATHERED), offs, widths)}
    lanes = lambda a: jnp.pad(a, ((0, 0), (0, -a.shape[1] % LANE)))
    items = [(o, -(-size // LANE) * LANE, n in raw_key) for (n, size), o in zip(_REPLICATED, offs)]
    sm = _adam_vectors("adam_replicated", total, items, [lanes(wl[n]) for n in rep_names],
                       [lanes(ml[n]) for n in rep_names], [lanes(vl[n]) for n in rep_names])
    sm = [{n: a[:, :size] for (n, size), a in zip(_REPLICATED, kind)} for kind in sm]
    mine_of = {}
    for n in _F32_GATHERED:
        d, s = spec[n]
        r, c = _local_shape(d, s)
        mine_of[n] = lax.dynamic_slice(tot[n].reshape(s), (0, my_id * c), (r, c))
    res = _adam("adam_small_sharded", [(small_block(mine_of)[None], 0)], small_block(wl), small_block(ml),
                small_block(vl))
    for kind, a in enumerate(res):
        big[kind].update(zip(_F32_GATHERED, _split(a.reshape(-1), small_sizes)))

    for names, sends, recvs, srcs, lands in pushed:
        srcs, lands = _push_wait("rs_" + names[0] + "_wait", sends, recvs, srcs, lands, True, dh0)
        for n, src, land in zip(names, srcs, lands):
            parts = [(src, my_id)] + [(land, my_id ^ f) for f in range(1, N_DEV)]
            for kind, a in enumerate(_adam("adam_" + n, parts, wl[n], ml[n], vl[n])):
                big[kind][n] = a

    outs = [tot["loss"][0], grad_x]
    for kind in range(4):
        for n in ("meta_tokens", "attn_norm_w", "w_in", "q_a_norm_w", "w_q_b", "kv_a_norm_w", "w_kv_b", "q_norm_w",
                  "k_norm_w", "mla_out_norm_w", "dn_conv_w", "dn_A_log", "dn_dt_bias", "dn_out_norm_w", "w_out",
                  "ffn_norm_w", "w_gate", "w_up", "ffn_conv_w", "ffn_conv_b", "w_down"):
            src = big[kind] if n in big[kind] else sm[kind]
            a = src[n].T if n in _TRANSPOSED else src[n]
            outs.append(a.reshape(out_shapes[n]))
    return tuple(outs)
```

```python
import functools
import math

import jax
import jax.numpy as jnp
from jax import lax
from jax.experimental import pallas as pl
from jax.experimental.pallas import tpu as pltpu

F32 = jnp.float32
_MXU = jnp.bfloat16
_HI = lax.Precision.HIGHEST

D_MODEL = 1024
N_META = 16
PAD = 112
ROW0 = PAD + N_META
MLA_HEADS = 4
QK_NOPE = 128
QK_ROPE = 64
QK_HEAD = QK_NOPE + QK_ROPE
V_HEAD = 128
Q_LORA = 256
KV_LORA = 256
ROPE_THETA = 10000.0
DN_HEADS = 4
DN_DIM = 128
DN_WIDTH = DN_HEADS * DN_DIM
DN_CONV = 4
DN_CHUNK = 64
GDN_SUB_CHUNKS = 2
D_FF = 2816
FFN_CONV = 3
EPS = 1e-6
HP = 256
C_Z = 1536
C_QL = 2048
C_KVL = 2304
C_KPE = 2560
C_AB = 2688
IN_COLS = 2632

ADAM_LR = 0.001
ADAM_B1 = 0.9
ADAM_B2 = 0.999
ADAM_EPS = 1e-08
ADAM_WD = 0.01
ADAM_STEP = 10

N_DEV = 8
TM = 128
LANE = 128
VMEM_LIMIT = 56 * 1024 * 1024
NEG = -1e30


def _dot(a, b, dims, hp=False):
    if hp:
        return lax.dot_general(a.astype(F32), b.astype(F32), (dims, ((), ())),
                               precision=lax.Precision.HIGH if hp == "3x" else _HI, preferred_element_type=F32)
    return lax.dot_general(a.astype(_MXU), b.astype(_MXU), (dims, ((), ())),
                           preferred_element_type=F32)


def _nn(a, b, hp=False):
    return _dot(a, b, ((1,), (0,)), hp)


def _nt(a, b, hp=False):
    return _dot(a, b, ((1,), (1,)), hp)


def _tn(a, b, hp=False):
    return _dot(a, b, ((0,), (0,)), hp)


def _sigmoid(x):
    return 1.0 / (1.0 + jnp.exp(-x))


def _rms_fwd(x, w, n):
    r = lax.rsqrt(jnp.sum(x * x, axis=-1, keepdims=True) * (1.0 / n) + EPS)
    return x * r * w, r


def _rms_bwd(x, w, dy, n):
    r = lax.rsqrt(jnp.sum(x * x, axis=-1, keepdims=True) * (1.0 / n) + EPS)
    xh = x * r
    gy = dy * w
    dx = r * (gy - xh * (jnp.sum(gy * xh, axis=-1, keepdims=True) * (1.0 / n)))
    return dx, dy * xh


def _rowsum(x):
    return jnp.sum(x, axis=0, keepdims=True)


def _row_ids(i, tm):
    return i * tm + lax.broadcasted_iota(jnp.int32, (tm, 1), 0)


def _shift_down(ext, s, tm):
    if s == 0:
        return ext[8:8 + tm]
    return pltpu.roll(ext, s, 0)[8:8 + tm]


def _shift_up(ext, s, tm):
    if s == 0:
        return ext[0:tm]
    return pltpu.roll(ext, tm + 8 - s, 0)[0:tm]


def _conv_fwd(x, halo_prev, w, width):
    tm = x.shape[0]
    ext = jnp.concatenate([halo_prev, x], axis=0)
    y = None
    for j in range(width):
        t = w[j:j + 1, :] * _shift_down(ext, width - 1 - j, tm)
        y = t if y is None else y + t
    return y


def _conv_bwd_x(dy, halo_next, w, width):
    tm = dy.shape[0]
    ext = jnp.concatenate([dy, halo_next], axis=0)
    dx = None
    for j in range(width):
        t = w[j:j + 1, :] * _shift_up(ext, width - 1 - j, tm)
        dx = t if dx is None else dx + t
    return dx


def _conv_bwd_w(dy, x, halo_prev, width):
    tm = dy.shape[0]
    ext = jnp.concatenate([halo_prev, x], axis=0)
    rows = [_rowsum(dy * _shift_down(ext, width - 1 - j, tm)) for j in range(width)]
    rows += [jnp.zeros_like(rows[0])] * (8 - width)
    return jnp.concatenate(rows, axis=0)


def _softplus(x):
    e = jnp.exp(-jnp.abs(x))
    u = 1.0 + e
    l1p = jnp.where(u == 1.0, e, jnp.log(u) * e / jnp.where(u == 1.0, 1.0, u - 1.0))
    return jnp.maximum(x, 0.0) + l1p


def _swap_halves(x):
    lane = lax.broadcasted_iota(jnp.int32, x.shape, 1)
    return jnp.where(lane < 32, pltpu.roll(x, 96, 1), jnp.where(lane < 64, pltpu.roll(x, 32, 1), 0.0))


class _In:
    def __init__(self, arr, width=None, cb=0, kind="cur"):
        self.arr, self.kind = arr, kind
        self.width = arr.shape[1] if width is None else width
        self.cb = cb


def _whole_spec(x):
    return pl.BlockSpec(x.shape, lambda i, nd=x.ndim: (0,) * nd, pipeline_mode=pl.Buffered(1))


def _tile_spec(t, tm, tp):
    r8 = tm // 8
    if t.kind == "cur":
        return pl.BlockSpec((tm, t.width), lambda i, cb=t.cb: (i, cb))
    if t.kind == "prev":
        return pl.BlockSpec((8, t.width), lambda i, cb=t.cb: (jnp.maximum(i * r8 - 1, 0), cb))
    return pl.BlockSpec((8, t.width), lambda i, cb=t.cb: (jnp.minimum((i + 1) * r8, tp // 8 - 1), cb))


def _rows(name, fn, tiled, full, outs, accs=(), tm=TM):
    tp = tiled[0].arr.shape[0]
    nt = tp // tm
    n_in = len(tiled) + len(full)
    n_out = len(outs)

    def body(*refs):
        i = pl.program_id(0)
        vals = [r[...] for r in refs[:n_in]]
        o_t, o_a = fn(i, *vals)
        for r, v in zip(refs[n_in:n_in + n_out], o_t):
            r[...] = v.astype(r.dtype)
        for r, v in zip(refs[n_in + n_out:], o_a):
            @pl.when(i == 0)
            def _():
                r[...] = v

            @pl.when(i > 0)
            def _():
                r[...] += v

    in_specs = [_tile_spec(t, tm, tp) for t in tiled]
    in_specs += [pl.BlockSpec(a.shape, lambda i, nd=a.ndim: (0,) * nd) for a in full]
    out_specs = [pl.BlockSpec((tm, w), lambda i: (i, 0)) for w, _ in outs]
    out_specs += [pl.BlockSpec((r, w), lambda i: (0, 0)) for r, w in accs]
    out_shape = [jax.ShapeDtypeStruct((tp, w), dt) for w, dt in outs]
    out_shape += [jax.ShapeDtypeStruct((r, w), F32) for r, w in accs]
    res = pl.pallas_call(
        body, name=name, grid=(nt,), in_specs=in_specs, out_specs=out_specs, out_shape=out_shape,
        compiler_params=pltpu.CompilerParams(dimension_semantics=("arbitrary",), vmem_limit_bytes=VMEM_LIMIT),
    )(*[t.arr for t in tiled], *full)
    return res


def _pick(n, cap, mult):
    best = None
    for d in range(mult, min(n, cap) + 1, mult):
        if n % d == 0:
            best = d
    assert best is not None, (n, cap, mult)
    return best


_ANY_SPEC = pl.BlockSpec(memory_space=pl.ANY)


def _mm(name, a, b, mode, out_dtype=F32, resid=None, after=None):
    if mode == "tn":
        m, k = a.shape
        n = b.shape[1]
        tk = _pick(k, 512, 128)
        tn = _pick(n, 1408, 128)

        def body_tn(a_ref, b_ref, o_ref):
            o_ref[...] = _tn(a_ref[...], b_ref[...]).astype(o_ref.dtype)

        return pl.pallas_call(
            body_tn, name=name, grid=(n // tn, k // tk),
            in_specs=[pl.BlockSpec((m, tk), lambda j, p: (0, p)),
                      pl.BlockSpec((m, tn), lambda j, p: (0, j))],
            out_specs=pl.BlockSpec((tk, tn), lambda j, p: (p, j)),
            out_shape=jax.ShapeDtypeStruct((k, n), out_dtype),
            compiler_params=pltpu.CompilerParams(
                dimension_semantics=("parallel", "parallel"), vmem_limit_bytes=VMEM_LIMIT),
        )(a, b)

    m, k = a.shape
    n = b.shape[1] if mode == "nn" else b.shape[0]
    tn = _pick(n, 1408, 128)
    tm = _pick(m, 1152, 16)
    dotf = _nn if mode == "nn" else _nt

    def body(*refs):
        a_ref, b_ref, o_ref = refs[0], refs[1], refs[-1]
        acc = dotf(a_ref[...], b_ref[...])
        if resid is not None:
            acc = refs[2][...] + acc
        o_ref[...] = acc.astype(o_ref.dtype)

    b_spec = (pl.BlockSpec((k, tn), lambda j, i: (0, j)) if mode == "nn"
              else pl.BlockSpec((tn, k), lambda j, i: (j, 0)))
    in_specs = [pl.BlockSpec((tm, k), lambda j, i: (i, 0)), b_spec]
    args = [a, b]
    if resid is not None:
        in_specs.append(pl.BlockSpec((tm, tn), lambda j, i: (i, j)))
        args.append(resid)
    if after is not None:
        in_specs.append(_ANY_SPEC)
        args.append(after)
    return pl.pallas_call(
        body, name=name, grid=(n // tn, m // tm), in_specs=in_specs,
        out_specs=pl.BlockSpec((tm, tn), lambda j, i: (i, j)),
        out_shape=jax.ShapeDtypeStruct((m, n), out_dtype),
        compiler_params=pltpu.CompilerParams(
            dimension_semantics=("parallel", "parallel"), vmem_limit_bytes=VMEM_LIMIT),
    )(*args)


def _mm_tn2(name, a1, a2, b, out_dtype=F32):
    m, k = a1.shape
    n = b.shape[1]
    tk = _pick(k, 512, 128)

    def body(a1_ref, a2_ref, b_ref, o1_ref, o2_ref):
        bb = b_ref[...]
        o1_ref[...] = _tn(a1_ref[...], bb).astype(o1_ref.dtype)
        o2_ref[...] = _tn(a2_ref[...], bb).astype(o2_ref.dtype)

    a_spec = pl.BlockSpec((m, tk), lambda p: (0, p))
    o_spec = pl.BlockSpec((tk, n), lambda p: (p, 0))
    return pl.pallas_call(
        body, name=name, grid=(k // tk,),
        in_specs=[a_spec, a_spec, pl.BlockSpec((m, n), lambda p: (0, 0))],
        out_specs=[o_spec, o_spec], out_shape=[jax.ShapeDtypeStruct((k, n), out_dtype)] * 2,
        compiler_params=pltpu.CompilerParams(dimension_semantics=("parallel",), vmem_limit_bytes=VMEM_LIMIT),
    )(a1, a2, b)


def _norm_mm(name, x, norm_w, b, mode="nt", x_cb=0, after=None):
    m = x.shape[0]
    k = norm_w.shape[1]
    n = b.shape[0] if mode == "nt" else b.shape[1]
    tn = _pick(n, 1408, 128)
    tm = _pick(m, 1152, 16)
    dotf = _nt if mode == "nt" else _nn
    extra = [] if after is None else [after]

    def body(x_ref, w_ref, b_ref, *rest):
        o_ref, u_ref = rest[-2:]

        @pl.when(pl.program_id(1) == 0)
        def _():
            u_ref[...] = _rms_fwd(x_ref[...], w_ref[...], k)[0].astype(u_ref.dtype)

        o_ref[...] = dotf(u_ref[...], b_ref[...])

    b_spec = (pl.BlockSpec((tn, k), lambda i, j: (j, 0)) if mode == "nt"
              else pl.BlockSpec((k, tn), lambda i, j: (0, j)))
    return pl.pallas_call(
        body, name=name, grid=(m // tm, n // tn),
        in_specs=[pl.BlockSpec((tm, k), lambda i, j: (i, x_cb)), pl.BlockSpec((1, k), lambda i, j: (0, 0)),
                  b_spec] + [_ANY_SPEC] * len(extra),
        out_specs=[pl.BlockSpec((tm, tn), lambda i, j: (i, j)), pl.BlockSpec((tm, k), lambda i, j: (i, 0))],
        out_shape=[jax.ShapeDtypeStruct((m, n), F32), jax.ShapeDtypeStruct((m, k), _MXU)],
        compiler_params=pltpu.CompilerParams(
            dimension_semantics=("arbitrary", "arbitrary"), vmem_limit_bytes=VMEM_LIMIT),
    )(x, norm_w, b, *extra)


def _pro_mm(name, fn, tiled, full, k, b, resid):
    m = resid.shape[0]
    n = b.shape[1]
    tm = _pick(m, 576, 16)
    n_in = len(tiled) + len(full)

    def body(*refs):
        i = pl.program_id(0)
        u = fn(i, *[r[...] for r in refs[:n_in]]).astype(_MXU)
        b_ref, r_ref, o_ref, u_ref = refs[n_in:]
        u_ref[...] = u
        o_ref[...] = r_ref[...] + _nn(u, b_ref[...])

    row = lambda w: pl.BlockSpec((tm, w), lambda i: (i, 0))
    in_specs = [_tile_spec(t, tm, m) for t in tiled]
    in_specs += [_whole_spec(x) for x in full] + [_whole_spec(b), row(n)]
    return pl.pallas_call(
        body, name=name, grid=(m // tm,), in_specs=in_specs, out_specs=[row(n), row(k)],
        out_shape=[jax.ShapeDtypeStruct((m, n), F32), jax.ShapeDtypeStruct((m, k), _MXU)],
        compiler_params=pltpu.CompilerParams(dimension_semantics=("parallel",), vmem_limit_bytes=VMEM_LIMIT),
    )(*[t.arr for t in tiled], *full, b, resid)


def _ffn_in(h2, norm_w, w_gate_t, w_up_t, conv_w8, conv_b):
    m, k = h2.shape
    n = w_gate_t.shape[0]
    tm = _pick(m, 288, 16)

    def body(x_ref, xp_ref, nw_ref, wg_ref, wu_ref, cw_ref, cb_ref, hn_ref, gp_ref, up_ref, act_ref):
        i = pl.program_id(0)
        nw = nw_ref[...]
        hn = _rms_fwd(x_ref[...], nw, k)[0].astype(_MXU)
        hn_prev = _rms_fwd(xp_ref[...], nw, k)[0].astype(_MXU)
        wg = wg_ref[...]
        gp = _nt(hn, wg)
        gp_prev = jnp.where(i > 0, _nt(hn_prev, wg), 0.0)
        up = _nt(hn, wu_ref[...])
        gate = _conv_fwd(gp, gp_prev, cw_ref[...], FFN_CONV) + cb_ref[...]
        hn_ref[...] = hn
        gp_ref[...] = gp
        up_ref[...] = up
        act_ref[...] = (_silu_parts(gate)[0] * up).astype(act_ref.dtype)

    row = lambda w: pl.BlockSpec((tm, w), lambda i: (i, 0))
    r8 = tm // 8
    return pl.pallas_call(
        body, name="ffn_in", grid=(m // tm,),
        in_specs=[row(k), pl.BlockSpec((8, k), lambda i: (jnp.maximum(i * r8 - 1, 0), 0)), _whole_spec(norm_w),
                  _whole_spec(w_gate_t), _whole_spec(w_up_t), _whole_spec(conv_w8), _whole_spec(conv_b)],
        out_specs=[row(k), row(n), row(n), row(n)],
        out_shape=[jax.ShapeDtypeStruct((m, k), _MXU), jax.ShapeDtypeStruct((m, n), F32),
                   jax.ShapeDtypeStruct((m, n), F32), jax.ShapeDtypeStruct((m, n), _MXU)],
        compiler_params=pltpu.CompilerParams(dimension_semantics=("parallel",), vmem_limit_bytes=VMEM_LIMIT),
    )(h2, h2, norm_w, w_gate_t, w_up_t, conv_w8, conv_b)


def _mm_rows(name, a, b, mode, fn, tiled, full, outs, accs=(), tm_cap=576):
    a_list = list(a) if isinstance(a, (list, tuple)) else [a]
    b_list = list(b) if isinstance(b, (list, tuple)) else [b]
    na = len(a_list)
    m = a_list[0].shape[0]
    tm = _pick(m, tm_cap, 16)
    dotf = _nn if mode == "nn" else _nt
    n_in = len(tiled) + len(full)
    n_out = len(outs)
    first = 2 * na

    def body(*refs):
        i = pl.program_id(0)
        vals = [r[...] for r in refs[first:first + n_in]]
        acc = dotf(refs[0][...], refs[na][...])
        for p in range(1, na):
            acc = acc + dotf(refs[p][...], refs[na + p][...])
        o_t, o_a = fn(i, acc, *vals)
        for r, v in zip(refs[first + n_in:first + n_in + n_out], o_t):
            r[...] = v.astype(r.dtype)
        for r, v in zip(refs[first + n_in + n_out:], o_a):
            @pl.when(i == 0)
            def _():
                r[...] = v

            @pl.when(i > 0)
            def _():
                r[...] += v

    whole = lambda x: pl.BlockSpec(x.shape, lambda i, nd=x.ndim: (0,) * nd)
    in_specs = [pl.BlockSpec((tm, x.shape[1]), lambda i: (i, 0)) for x in a_list] + [_whole_spec(x) for x in b_list]
    in_specs += [_tile_spec(t, tm, m) for t in tiled]
    in_specs += [whole(x) for x in full]
    out_specs = [pl.BlockSpec((tm, w), lambda i: (i, 0)) for w, _ in outs]
    out_specs += [pl.BlockSpec((r, w), lambda i: (0, 0)) for r, w in accs]
    out_shape = [jax.ShapeDtypeStruct((m, w), dt) for w, dt in outs]
    out_shape += [jax.ShapeDtypeStruct((r, w), F32) for r, w in accs]
    return pl.pallas_call(
        body, name=name, grid=(m // tm,), in_specs=in_specs, out_specs=out_specs, out_shape=out_shape,
        compiler_params=pltpu.CompilerParams(dimension_semantics=("arbitrary",), vmem_limit_bytes=VMEM_LIMIT),
    )(*a_list, *b_list, *[t.arr for t in tiled], *full)


ATTN_Q_TILES = 4


def _attn_probs(q, k, row0):
    tq, tp = q.shape[0], k.shape[0]
    s = _nt(q, k) * (1.0 / math.sqrt(QK_HEAD))
    row = row0 + lax.broadcasted_iota(jnp.int32, (tq, tp), 0)
    col = lax.broadcasted_iota(jnp.int32, (tq, tp), 1)
    ok = (col <= row) & (col >= PAD)
    s = jnp.where(ok, s, NEG)
    m = jnp.max(s, axis=-1, keepdims=True)
    e = jnp.exp(s - m)
    return e * (1.0 / jnp.sum(e, axis=-1, keepdims=True))


def _attn_fwd(q, k, v):
    tp = q.shape[0]
    tq = tp // ATTN_Q_TILES

    def body(q_ref, k_ref, v_ref, o_ref):
        for i in range(ATTN_Q_TILES):
            rows = slice(i * tq, (i + 1) * tq)
            keys = slice(0, (i + 1) * tq)
            p = _attn_probs(q_ref[rows, :], k_ref[keys, :], i * tq)
            o_ref[rows, :] = _nn(p, v_ref[keys, :])

    return pl.pallas_call(
        body, name="attn_fwd", grid=(MLA_HEADS,),
        in_specs=[pl.BlockSpec((tp, HP), lambda h: (0, h)),
                  pl.BlockSpec((tp, HP), lambda h: (0, h)),
                  pl.BlockSpec((tp, V_HEAD), lambda h: (0, h))],
        out_specs=pl.BlockSpec((tp, V_HEAD), lambda h: (0, h)),
        out_shape=jax.ShapeDtypeStruct((tp, MLA_HEADS * V_HEAD), F32),
        compiler_params=pltpu.CompilerParams(dimension_semantics=("parallel",), vmem_limit_bytes=VMEM_LIMIT),
    )(q, k, v)


def _attn_bwd(q, k, v, do):
    tp = q.shape[0]
    tq = tp // ATTN_Q_TILES

    def body(q_ref, k_ref, v_ref, do_ref, dq_ref, dk_ref, dv_ref):
        for i in reversed(range(ATTN_Q_TILES)):
            rows = slice(i * tq, (i + 1) * tq)
            keys = slice(0, (i + 1) * tq)
            qb = q_ref[rows, :]
            kk = k_ref[keys, :]
            dob = do_ref[rows, :]
            p = _attn_probs(qb, kk, i * tq)
            dp = _nt(dob, v_ref[keys, :])
            delta = jnp.sum(p * dp, axis=-1, keepdims=True)
            ds = p * (dp - delta) * (1.0 / math.sqrt(QK_HEAD))
            dq_ref[rows, :] = _nn(ds, kk)
            if i == ATTN_Q_TILES - 1:
                dk_ref[...] = _tn(ds, qb)
                dv_ref[...] = _tn(p, dob)
            else:
                dk_ref[keys, :] += _tn(ds, qb)
                dv_ref[keys, :] += _tn(p, dob)

    full = lambda w: pl.BlockSpec((tp, w), lambda h: (0, h))
    return pl.pallas_call(
        body, name="attn_bwd", grid=(MLA_HEADS,),
        in_specs=[full(HP), full(HP), full(V_HEAD), full(V_HEAD)],
        out_specs=[full(HP), full(HP), full(V_HEAD)],
        out_shape=[jax.ShapeDtypeStruct((tp, MLA_HEADS * HP), F32),
                   jax.ShapeDtypeStruct((tp, MLA_HEADS * HP), F32),
                   jax.ShapeDtypeStruct((tp, MLA_HEADS * V_HEAD), F32)],
        compiler_params=pltpu.CompilerParams(dimension_semantics=("parallel",), vmem_limit_bytes=VMEM_LIMIT),
    )(q, k, v, do)


def _gdn_consts():
    c = DN_CHUNK
    r = lax.broadcasted_iota(jnp.int32, (c, c), 0)
    cc = lax.broadcasted_iota(jnp.int32, (c, c), 1)
    incl = r >= cc
    strict = r > cc
    return incl, strict


def _cumsum_rows(x, reverse=False):
    c = x.shape[0]
    row = lax.broadcasted_iota(jnp.int32, x.shape, 0)
    s = 1
    while s < c:
        if reverse:
            x = x + jnp.where(row < c - s, pltpu.roll(x, c - s, 0), 0.0)
        else:
            x = x + jnp.where(row >= s, pltpu.roll(x, s, 0), 0.0)
        s *= 2
    return x


def _each(fn, *lists):
    return [fn(*a) for a in zip(*lists)]


def _interleave(chains):
    chains = list(chains)
    while chains:
        for ch in list(chains):
            try:
                next(ch)
            except StopIteration:
                chains.remove(ch)


def _gdn_chunk_common(q_ref, k_ref, v_ref, g_ref, b_ref):
    c = DN_CHUNK
    incl, strict = _gdn_consts()
    sls = [(slice(c * sub, c * (sub + 1)), slice(DN_DIM * h, DN_DIM * (h + 1)))
           for sub in range(GDN_SUB_CHUNKS) for h in range(DN_HEADS)]
    q = [q_ref[sl] * (1.0 / math.sqrt(DN_DIM)) for sl in sls]
    k = [k_ref[sl] for sl in sls]
    v = [v_ref[sl] for sl in sls]
    g = [g_ref[sl] for sl in sls]
    beta = [b_ref[sl] for sl in sls]
    gc = [_cumsum_rows(x) for x in g]
    grow = [x.T[:c, :] for x in gc]
    kb = _each(jnp.multiply, k, beta)
    kk = _each(_nt, kb, k)
    qk = _each(_nt, q, k)
    gam = [jnp.exp(x) for x in gc]
    g_last = [_rowsum(x) for x in g]
    dm = [jnp.exp(jnp.where(incl, x[:, :c] - y, NEG)) for x, y in zip(gc, grow)]
    vb = _each(jnp.multiply, v, beta)
    kbg = _each(jnp.multiply, kb, gam)
    ek = [jnp.exp(x - y) for x, y in zip(g_last, gc)]
    kd = _each(jnp.multiply, k, ek)
    return dict(q=q, k=k, v=v, beta=beta, gc=gc, gam=gam, g_last=g_last, dm=dm, kb=kb, vb=vb,
                kbg=kbg, kk=kk, ek=ek, kd=kd, qk=qk, incl=incl, strict=strict, sls=sls)


def _gdn_fwd(q, k, v, g, beta):
    tp = q.shape[0]
    c = DN_CHUNK
    nch = tp // c

    def body(q_ref, k_ref, v_ref, g_ref, b_ref, o_ref, s_ref, t_ref, s_scr):
        @pl.when(pl.program_id(0) == 0)
        def _():
            s_scr[...] = jnp.zeros_like(s_scr)

        eye = (lax.broadcasted_iota(jnp.int32, (c, c), 0) == lax.broadcasted_iota(jnp.int32, (c, c), 1)).astype(F32)
        x = _gdn_chunk_common(q_ref, k_ref, v_ref, g_ref, b_ref)
        heads = range(DN_HEADS)
        bp = [-jnp.where(x["strict"], kk * dm, 0.0) for kk, dm in zip(x["kk"], x["dm"])]
        t = [eye + b for b in bp]
        for _ in range(5):
            bp = [_nn(b, b, hp="3x") for b in bp]
            t = [tt + _nn(tt, b, hp="3x") for tt, b in zip(t, bp)]
        u = _each(_nn, t, x["vb"])
        w = _each(_nn, t, x["kbg"])
        qg = _each(jnp.multiply, x["q"], x["gam"])
        mqk = _each(jnp.multiply, x["qk"], x["dm"])
        s = [s_scr[h] for h in heads]
        for sub in range(GDN_SUB_CHUNKS):
            e = [DN_HEADS * sub + h for h in heads]
            v_new = [u[i] - _nn(w[i], s[h]) for h, i in zip(heads, e)]
            o = [_nn(qg[i], s[h]) + _nn(mqk[i], v_new[h]) for h, i in zip(heads, e)]
            s_new = [s[h] * jnp.exp(x["g_last"][i]) + _tn(x["kd"][i], v_new[h]) for h, i in zip(heads, e)]
            for h, i in zip(heads, e):
                s_ref[h, sub] = s[h]
                t_ref[h, sub] = t[i]
                o_ref[x["sls"][i]] = o[h]
            s = s_new
        for h in heads:
            s_scr[h] = s[h]

    sub = GDN_SUB_CHUNKS
    rb = lambda n: (n, 0)
    return pl.pallas_call(
        body, name="gdn_fwd", grid=(nch // sub,),
        in_specs=[pl.BlockSpec((sub * c, DN_WIDTH), rb)] * 5,
        out_specs=[pl.BlockSpec((sub * c, DN_WIDTH), rb),
                   pl.BlockSpec((DN_HEADS, sub, DN_DIM, DN_DIM), lambda n: (0, n, 0, 0)),
                   pl.BlockSpec((DN_HEADS, sub, c, c), lambda n: (0, n, 0, 0))],
        out_shape=[jax.ShapeDtypeStruct((tp, DN_WIDTH), F32),
                   jax.ShapeDtypeStruct((DN_HEADS, nch, DN_DIM, DN_DIM), F32),
                   jax.ShapeDtypeStruct((DN_HEADS, nch, c, c), F32)],
        scratch_shapes=[pltpu.VMEM((DN_HEADS, DN_DIM, DN_DIM), F32)],
        compiler_params=pltpu.CompilerParams(dimension_semantics=("arbitrary",), vmem_limit_bytes=VMEM_LIMIT),
    )(q, k, v, g, beta)


def _gdn_bwd(q, k, v, g, beta, s_all, t_all, do):
    tp = q.shape[0]
    c = DN_CHUNK
    nch = tp // c

    def body(q_ref, k_ref, v_ref, g_ref, b_ref, s_ref, t_ref, do_ref,
             dq_ref, dk_ref, dv_ref, dg_ref, db_ref, ds_scr):
        @pl.when(pl.program_id(0) == 0)
        def _():
            ds_scr[...] = jnp.zeros_like(ds_scr)

        xs = _gdn_chunk_common(q_ref, k_ref, v_ref, g_ref, b_ref)

        ds_state = [ds_scr[h] for h in range(DN_HEADS)]

        def chain(sub, h):
            e = DN_HEADS * sub + h
            x = {key: (val[e] if isinstance(val, list) else val) for key, val in xs.items()}
            sl = x["sls"]
            qs, kx, vx, beta_, gam, dm = x["q"], x["k"], x["v"], x["beta"], x["gam"], x["dm"]
            kb, vb, kbg, kd, ek = x["kb"], x["vb"], x["kbg"], x["kd"], x["ek"]
            t = t_ref[h, sub]
            s = s_ref[h, sub]
            dsn = ds_state[h]
            dob = do_ref[sl]
            eg_last = jnp.exp(x["g_last"])
            u = _nn(t, vb)
            w = _nn(t, kbg)
            mqk = x["qk"] * dm
            qd = qs * gam
            dqd = _nt(dob, s)
            dkd_pre = _nn(kd, dsn)
            yield
            v_new = u - _nn(w, s)
            dv_new = _tn(mqk, dob) + dkd_pre
            dq = dqd * gam
            dgam = jnp.sum(dqd * qs, axis=1, keepdims=True)
            yield
            ds_state[h] = _tn(qd, dob) + eg_last * dsn - _tn(w, dv_new)
            dmm = jnp.where(x["incl"], _nt(dob, v_new), 0.0)
            dkd = _nt(v_new, dsn)
            dw = -_nt(dv_new, s)
            dvb = _tn(t, dv_new)
            dt = _nt(dv_new, vb)
            yield
            dqk = dmm * dm
            e_mat = dmm * mqk
            dq = dq + _nn(dqk, kx)
            dk = _tn(dqk, qs) + dkd * ek
            e1 = jnp.sum(dkd * kd, axis=1, keepdims=True)
            dgc = -e1
            dg_last = jnp.sum(e1) + eg_last * jnp.sum(s * dsn)
            dt = dt + _nt(dw, kbg)
            dkbg = _tn(t, dw)
            yield
            tdt = _tn(t, dt, hp="3x")
            yield
            da = jnp.where(x["strict"], -_nt(tdt, t, hp="3x"), 0.0)
            yield
            dkk = da * dm
            e_mat = e_mat + da * x["kk"] * dm
            dkb = _nn(dkk, kx) + dkbg * gam
            dk = dk + _tn(dkk, kb)
            dgam = dgam + jnp.sum(dkbg * kb, axis=1, keepdims=True)
            yield
            dk = dk + dkb * beta_
            dbeta = jnp.sum(dkb * kx, axis=1, keepdims=True) + jnp.sum(dvb * vx, axis=1, keepdims=True)
            dv = dvb * beta_
            dgc = dgc + jnp.sum(e_mat, axis=1, keepdims=True) + dgam * gam
            dgc = dgc - jnp.sum(e_mat.T, axis=1, keepdims=True)
            yield
            dg = _cumsum_rows(dgc, reverse=True) + dg_last
            yield
            dq_ref[sl] = dq * (1.0 / math.sqrt(DN_DIM))
            dk_ref[sl] = dk
            dv_ref[sl] = dv
            dg_ref[sl] = dg
            db_ref[sl] = jnp.broadcast_to(dbeta, (c, LANE))

        chains = []
        for sub in reversed(range(GDN_SUB_CHUNKS)):
            new = [chain(sub, h) for h in range(DN_HEADS)]
            for _ in range(3):
                for ch in new:
                    next(ch)
            chains += new
        _interleave(chains)
        for h in range(DN_HEADS):
            ds_scr[h] = ds_state[h]

    nblk = nch // GDN_SUB_CHUNKS
    sub = GDN_SUB_CHUNKS
    rb = lambda n: (nblk - 1 - n, 0)
    hs = lambda n: (0, nblk - 1 - n, 0, 0)
    return pl.pallas_call(
        body, name="gdn_bwd", grid=(nblk,),
        in_specs=[pl.BlockSpec((sub * c, DN_WIDTH), rb)] * 5
        + [pl.BlockSpec((DN_HEADS, sub, DN_DIM, DN_DIM), hs), pl.BlockSpec((DN_HEADS, sub, c, c), hs),
           pl.BlockSpec((sub * c, DN_WIDTH), rb)],
        out_specs=[pl.BlockSpec((sub * c, DN_WIDTH), rb)] * 5,
        out_shape=[jax.ShapeDtypeStruct((tp, DN_WIDTH), F32)] * 5,
        scratch_shapes=[pltpu.VMEM((DN_HEADS, DN_DIM, DN_DIM), F32)],
        compiler_params=pltpu.CompilerParams(dimension_semantics=("arbitrary",), vmem_limit_bytes=VMEM_LIMIT),
    )(q, k, v, g, beta, s_all, t_all, do)


def _silu_parts(x):
    s = _sigmoid(x)
    return x * s, s * (1.0 + x * (1.0 - s))


def _f_rms_bwd_add(i, x, dy, dres, w, *, mask_pad):
    dx, dwr = _rms_bwd(x, w, dy, x.shape[1])
    out = dres + dx
    if mask_pad:
        out = jnp.where(_row_ids(i, x.shape[0]) >= PAD, out, 0.0)
    return (out,), (_rowsum(dwr),)


def _rope(x, cos, sin_s):
    return x * cos + _swap_halves(x) * sin_s


def _rope_t(dy, cos, sin_s):
    return dy * cos + _swap_halves(dy * sin_s)


def _f_mla_qk(i, qf, kvf, kpe, cos, sin_s, qw, kw):
    qs, ks, vs = [], [], []
    for h in range(MLA_HEADS):
        qn, _ = _rms_fwd(qf[:, HP * h:HP * (h + 1)], qw, QK_HEAD)
        qs += [qn[:, :QK_NOPE], _rope(qn[:, QK_NOPE:], cos, sin_s)]
        kh = jnp.concatenate([kvf[:, HP * h:HP * h + QK_NOPE], kpe], axis=1)
        kn, _ = _rms_fwd(kh, kw, QK_HEAD)
        ks += [kn[:, :QK_NOPE], _rope(kn[:, QK_NOPE:], cos, sin_s)]
        vs.append(kvf[:, HP * h + QK_NOPE:HP * (h + 1)])
    return (jnp.concatenate(qs, axis=1), jnp.concatenate(ks, axis=1), jnp.concatenate(vs, axis=1)), ()


def _f_mla_front(i, ql, kvl, kpe, cos, sin_s, qaw, kvaw, wq_t, wkv, qw, kw):
    qn = _rms_fwd(ql, qaw, Q_LORA)[0].astype(_MXU)
    kvn = _rms_fwd(kvl, kvaw, KV_LORA)[0].astype(_MXU)
    qf = _nt(qn, wq_t)
    kvf = _nn(kvn, wkv)
    (q, k, v), _ = _f_mla_qk(i, qf, kvf, kpe, cos, sin_s, qw, kw)
    return (qn, kvn, qf, kvf, q, k, v), ()


def _f_mla_back(i, qf, kvf, kpe, cos, sin_s, dq, dk, dv, ql, kvl, qaw, kvaw, wq_t, wkv, qw, kw):
    (dqf, dkvf, dkpe), (dqw, dkw) = _f_mla_qk_bwd(i, qf, kvf, kpe, cos, sin_s, dq, dk, dv, qw, kw)
    dqf = dqf.astype(_MXU)
    dkvf = dkvf.astype(_MXU)
    dql, dqaw = _rms_bwd(ql, qaw, _nn(dqf, wq_t), Q_LORA)
    dkvl, dkvaw = _rms_bwd(kvl, kvaw, _nt(dkvf, wkv), KV_LORA)
    return (dqf, dkvf, dkpe, dql, dkvl), (dqw, dkw, _rowsum(dqaw), _rowsum(dkvaw))


def _f_mla_qk_bwd(i, qf, kvf, kpe, cos, sin_s, dq, dk, dv, qw, kw):
    dqf, dkvf = [], []
    dkpe = None
    dqw = None
    dkw = None
    for h in range(MLA_HEADS):
        dqh = dq[:, HP * h:HP * (h + 1)]
        dqn = jnp.concatenate([dqh[:, :QK_NOPE], _rope_t(dqh[:, QK_NOPE:], cos, sin_s)], axis=1)
        dx, dwr = _rms_bwd(qf[:, HP * h:HP * (h + 1)], qw, dqn, QK_HEAD)
        dqf.append(dx)
        dqw = _rowsum(dwr) if dqw is None else dqw + _rowsum(dwr)
        dkh = dk[:, HP * h:HP * (h + 1)]
        dkn = jnp.concatenate([dkh[:, :QK_NOPE], _rope_t(dkh[:, QK_NOPE:], cos, sin_s)], axis=1)
        kh = jnp.concatenate([kvf[:, HP * h:HP * h + QK_NOPE], kpe], axis=1)
        dx, dwr = _rms_bwd(kh, kw, dkn, QK_HEAD)
        dkvf += [dx[:, :QK_NOPE], dv[:, V_HEAD * h:V_HEAD * (h + 1)]]
        dkpe = dx[:, QK_NOPE:] if dkpe is None else dkpe + dx[:, QK_NOPE:]
        dkw = _rowsum(dwr) if dkw is None else dkw + _rowsum(dwr)
    return (jnp.concatenate(dqf, axis=1), jnp.concatenate(dkvf, axis=1), dkpe), (dqw, dkw)


def _gdn_act(i, x, halo, w8):
    halo = jnp.where(i > 0, halo, 0.0)
    c = _conv_fwd(x, halo, w8, DN_CONV)
    act, dact = _silu_parts(c)
    return act, dact


def _spread_heads(ab):
    tm = ab.shape[0]
    return jnp.concatenate([jnp.broadcast_to(ab[:, h:h + 1], (tm, DN_DIM)) for h in range(2 * DN_HEADS)], axis=1)


def _gather_heads(x):
    tm = x.shape[0]
    lane = lax.broadcasted_iota(jnp.int32, (tm, LANE), 1)
    out = jnp.zeros((tm, LANE), F32)
    for h in range(2 * DN_HEADS):
        out = out + jnp.where(lane == h, x[:, DN_DIM * h:DN_DIM * h + 1], 0.0)
    return out


def _f_gdn_prep(i, x, halo, ab, w8, alog, dtb):
    tm = x.shape[0]
    act, _ = _gdn_act(i, x, halo, w8)
    outs = []
    for part in range(2):
        for h in range(DN_HEADS):
            t = act[:, DN_WIDTH * part + DN_DIM * h:DN_WIDTH * part + DN_DIM * (h + 1)]
            outs.append(t * lax.rsqrt(jnp.sum(t * t, axis=-1, keepdims=True) + EPS))
    q = jnp.concatenate(outs[:DN_HEADS], axis=1)
    k = jnp.concatenate(outs[DN_HEADS:], axis=1)
    v = act[:, 2 * DN_WIDTH:]
    abb = _spread_heads(ab)
    valid = _row_ids(i, tm) >= PAD
    g = jnp.where(valid, -jnp.exp(alog) * _softplus(abb[:, :DN_WIDTH] + dtb), 0.0)
    beta = jnp.where(valid, _sigmoid(abb[:, DN_WIDTH:]), 0.0)
    return (q, k, v, g, beta), ()


def _f_gdn_prep_bwd(i, x, x_prev, x_next, ab, dq, dq_next, dk, dk_next, dv, dv_next, dg, dbeta,
                    w8, alog, dtb, *, nt):
    tm = x.shape[0]
    x_prev = jnp.where(i > 0, x_prev, 0.0)
    more = i < nt - 1
    ext = lambda t, t_next: jnp.concatenate([t, jnp.where(more, t_next, 0.0)], axis=0)
    c = _conv_fwd(jnp.concatenate([x, x_next], axis=0), x_prev, w8, DN_CONV)
    act, dact = _silu_parts(c)
    douts = []
    for part, dd in enumerate((ext(dq, dq_next), ext(dk, dk_next))):
        for h in range(DN_HEADS):
            t = act[:, DN_WIDTH * part + DN_DIM * h:DN_WIDTH * part + DN_DIM * (h + 1)]
            r = lax.rsqrt(jnp.sum(t * t, axis=-1, keepdims=True) + EPS)
            y = t * r
            dy = dd[:, DN_DIM * h:DN_DIM * (h + 1)]
            douts.append(r * (dy - y * jnp.sum(dy * y, axis=-1, keepdims=True)))
    douts.append(ext(dv, dv_next))
    dc = jnp.concatenate(douts, axis=1) * dact
    dqkv = _conv_bwd_x(dc[:tm], dc[tm:], w8, DN_CONV)
    dconv_w = _conv_bwd_w(dc[:tm], x, x_prev, DN_CONV)
    abb = _spread_heads(ab)
    valid = _row_ids(i, tm) >= PAD
    pre = abb[:, :DN_WIDTH] + dtb
    ea = jnp.exp(alog)
    g = -ea * _softplus(pre)
    dg = jnp.where(valid, dg, 0.0)
    dbeta = jnp.where(valid, dbeta, 0.0)
    da = dg * (-ea) * _sigmoid(pre)
    beta = _sigmoid(abb[:, DN_WIDTH:])
    db = dbeta * beta * (1.0 - beta)
    dab = _gather_heads(jnp.concatenate([da, db], axis=1))
    return (dqkv, dab), (dconv_w, _rowsum(dg * g), _rowsum(da))


def _f_mix(i, o_mla, o_dn, z, w_mla, w_dn):
    tm = o_mla.shape[0]
    valid = _row_ids(i, tm) >= PAD
    outs = []
    for h in range(MLA_HEADS):
        y, _ = _rms_fwd(o_mla[:, V_HEAD * h:V_HEAD * (h + 1)], w_mla, V_HEAD)
        outs.append(jnp.where(valid, y, 0.0))
    for h in range(DN_HEADS):
        y, _ = _rms_fwd(o_dn[:, DN_DIM * h:DN_DIM * (h + 1)], w_dn, DN_DIM)
        outs.append(y * _silu_parts(z[:, DN_DIM * h:DN_DIM * (h + 1)])[0])
    return (jnp.concatenate(outs, axis=1),), ()


def _f_mix_bwd(i, o_mla, o_dn, z, dy_mla, dy_dn, w_mla, w_dn):
    tm = o_mla.shape[0]
    valid = _row_ids(i, tm) >= PAD
    d_mla, d_dn, d_z = [], [], []
    dw_mla = None
    dw_dn = None
    for h in range(MLA_HEADS):
        sl = slice(V_HEAD * h, V_HEAD * (h + 1))
        dx, dwr = _rms_bwd(o_mla[:, sl], w_mla, jnp.where(valid, dy_mla[:, sl], 0.0), V_HEAD)
        d_mla.append(dx)
        dw_mla = _rowsum(dwr) if dw_mla is None else dw_mla + _rowsum(dwr)
    for h in range(DN_HEADS):
        sl = slice(DN_DIM * h, DN_DIM * (h + 1))
        y, _ = _rms_fwd(o_dn[:, sl], w_dn, DN_DIM)
        sz, dsz = _silu_parts(z[:, sl])
        d_z.append(dy_dn[:, sl] * y * dsz)
        dx, dwr = _rms_bwd(o_dn[:, sl], w_dn, dy_dn[:, sl] * sz, DN_DIM)
        d_dn.append(dx)
        dw_dn = _rowsum(dwr) if dw_dn is None else dw_dn + _rowsum(dwr)
    return ((jnp.concatenate(d_mla, axis=1), jnp.concatenate(d_dn, axis=1), jnp.concatenate(d_z, axis=1)),
            (dw_mla, dw_dn))


def _f_ffn_act_bwd(i, gp, gp_prev, gp_next, up, up_next, dact, dact_next, w8, b, *, nt):
    tm = gp.shape[0]
    gp_prev = jnp.where(i > 0, gp_prev, 0.0)
    dact_next = jnp.where(i < nt - 1, dact_next, 0.0)
    cat = lambda t, t_next: jnp.concatenate([t, t_next], axis=0)
    gate = _conv_fwd(cat(gp, gp_next), gp_prev, w8, FFN_CONV) + b
    sg, dsg = _silu_parts(gate)
    dact_e = cat(dact, dact_next)
    dgate = dact_e * cat(up, up_next) * dsg
    dgate_pre = _conv_bwd_x(dgate[:tm], dgate[tm:], w8, FFN_CONV)
    dup = dact * sg[:tm]
    return (dgate_pre, dup), (_conv_bwd_w(dgate[:tm], gp, gp_prev, FFN_CONV), _rowsum(dgate[:tm]))


def _f_loss(i, h3, tgt):
    tm = h3.shape[0]
    diff = jnp.where(_row_ids(i, tm) >= ROW0, h3 - tgt, 0.0)
    part = 0.5 * jnp.sum(diff * diff) * (1.0 / D_MODEL)
    return (diff * (1.0 / D_MODEL),), (jnp.full((1, LANE), part, F32),)


def _after(fn):
    return lambda i, *a: fn(i, *a[:-1])


def _local_step(h0, tgt, w, token, late_weights, grads_ready):
    tp = h0.shape[0]
    nt = tp // TM
    proj, u = _norm_mm("in_proj", h0, w["attn_norm_w"], w["w_in"], after=token)
    p_qkv = lambda kind="cur": _In(proj, 3 * DN_WIDTH, 0, kind)
    p_z = _In(proj, DN_WIDTH, C_Z // DN_WIDTH)
    p_ql = _In(proj, Q_LORA, C_QL // Q_LORA)
    p_kvl = _In(proj, KV_LORA, C_KVL // KV_LORA)
    p_kpe = _In(proj, LANE, C_KPE // LANE)
    p_ab = _In(proj, LANE, C_AB // LANE)
    cos, sin_s = _In(w["cos"]), _In(w["sin_s"])

    mla_w = [w["q_a_norm_w"], w["kv_a_norm_w"], w["w_q_b"], w["w_kv_b"], w["q_norm_w"], w["k_norm_w"]]
    tm_mla = _pick(tp, 288, 16)
    wide = MLA_HEADS * HP
    qn, kvn, qf, kvf, q, k, v = _rows(
        "mla_front", _f_mla_front, [p_ql, p_kvl, p_kpe, cos, sin_s], mla_w,
        [(Q_LORA, _MXU), (KV_LORA, _MXU), (wide, F32), (wide, F32), (wide, _MXU), (wide, _MXU),
         (MLA_HEADS * V_HEAD, _MXU)], tm=tm_mla)
    o_mla = _attn_fwd(q, k, v)

    dn_w = [w["dn_conv_w"], w["alog_b"], w["dtb_b"]]
    gq, gk, gv, gg, gb = _rows("gdn_prep", _f_gdn_prep, [p_qkv(), p_qkv("prev"), p_ab], dn_w,
                               [(DN_WIDTH, F32)] * 5)
    o_dn, s_all, t_all = _gdn_fwd(gq, gk, gv, gg, gb)

    out_w = [w["mla_out_norm_w"], w["dn_out_norm_w"]]
    w = dict(w, **late_weights((o_mla, o_dn), _LATE[:3]))
    h2, mixed = _pro_mm("mix_out_proj", lambda i, *t: _f_mix(i, *t)[0][0], [_In(o_mla), _In(o_dn), p_z], out_w,
                        D_MODEL, w["w_out"], h0)

    ffn_w = [w["ffn_conv_w"], w["ffn_conv_b"]]
    hn, gate_pre, up, act = _ffn_in(h2, w["ffn_norm_w"], w["w_gate"], w["w_up"], *ffn_w)
    w = dict(w, **late_weights(act, _LATE[3:]))
    dh3, loss = _mm_rows("ffn_down_loss", act, w["w_down"], "nn", lambda i, y, r, t: _f_loss(i, r + y, t),
                         [_In(h2), _In(tgt)], [], [(D_MODEL, F32)], [(1, LANE)])

    g = {}
    dact = _mm("ffn_down_dx", dh3, w["w_down"], "nt")
    g["w_down"] = _mm("ffn_down_dw", act, dh3, "tn", out_dtype=_MXU)
    dgate_pre, dup, g["ffn_conv_w"], g["ffn_conv_b"] = _rows(
        "ffn_act_bwd", functools.partial(_f_ffn_act_bwd, nt=nt),
        [_In(gate_pre), _In(gate_pre, kind="prev"), _In(gate_pre, kind="next"), _In(up), _In(up, kind="next"),
         _In(dact), _In(dact, kind="next")], ffn_w,
        [(D_FF, _MXU), (D_FF, _MXU)], [(8, D_FF), (1, D_FF)])
    g["w_gate"], g["w_up"] = _mm_tn2("ffn_gate_up_dw", dgate_pre, dup, hn, out_dtype=_MXU)
    tok = grads_ready(g, ("w_down", "w_gate", "w_up"))
    dh2, g["ffn_norm_w"] = _mm_rows(
        "ffn_gate_up_dx_rms", [dgate_pre, dup], [w["w_gate"], w["w_up"]], "nn",
        lambda i, dy, x, dres, nw, _tok: _f_rms_bwd_add(i, x, dy, dres, nw, mask_pad=True),
        [_In(h2), _In(dh3)], [w["ffn_norm_w"], tok], [(D_MODEL, F32)], [(1, D_MODEL)])

    g["w_out"] = _mm("out_proj_dw", mixed, dh2, "tn", out_dtype=_MXU)
    half = MLA_HEADS * V_HEAD
    do_mla, do_dn, dz, g["mla_out_norm_w"], g["dn_out_norm_w"] = _mm_rows(
        "out_proj_dx_mix", dh2, w["w_out"], "nt",
        lambda i, dm, om, od, z, wm, wd: _f_mix_bwd(i, om, od, z, dm[:, :half], dm[:, half:], wm, wd),
        [_In(o_mla), _In(o_dn), p_z], out_w,
        [(half, F32), (DN_WIDTH, F32), (DN_WIDTH, _MXU)], [(1, V_HEAD), (1, DN_DIM)])

    dq, dk, dv = _attn_bwd(q, k, v, do_mla)
    dqf, dkvf, dkpe, dql, dkvl, g["q_norm_w"], g["k_norm_w"], g["q_a_norm_w"], g["kv_a_norm_w"] = _rows(
        "mla_back", _f_mla_back,
        [_In(qf), _In(kvf), p_kpe, cos, sin_s, _In(dq), _In(dk), _In(dv), p_ql, p_kvl], mla_w,
        [(wide, _MXU), (wide, _MXU), (LANE, _MXU), (Q_LORA, _MXU), (KV_LORA, _MXU)],
        [(1, HP), (1, HP), (1, Q_LORA), (1, KV_LORA)], tm=tm_mla)
    g["w_q_b"] = _mm("mla_q_b_dw", dqf, qn, "tn")
    g["w_kv_b"] = _mm("mla_kv_b_dw", kvn, dkvf, "tn")
    tok = grads_ready(g, ("w_out", "w_q_b", "w_kv_b"))

    dgq, dgk, dgv, dgg, dgb = _gdn_bwd(gq, gk, gv, gg, gb, s_all, t_all, do_dn)
    nxt = lambda a: _In(a, kind="next")
    dqkv, dab, g["dn_conv_w"], g["alog_b"], g["dtb_b"] = _rows(
        "gdn_prep_bwd", _after(functools.partial(_f_gdn_prep_bwd, nt=nt)),
        [p_qkv(), p_qkv("prev"), p_qkv("next"), p_ab, _In(dgq), nxt(dgq), _In(dgk), nxt(dgk), _In(dgv), nxt(dgv),
         _In(dgg), _In(dgb)], dn_w + [tok],
        [(3 * DN_WIDTH, _MXU), (LANE, _MXU)], [(8, 3 * DN_WIDTH), (1, DN_WIDTH), (1, DN_WIDTH)])

    dproj = jnp.concatenate([dqkv, dz, dql, dkvl, dkpe, dab], axis=1)
    g["w_in"] = _mm("in_proj_dw", dproj, u, "tn", out_dtype=_MXU)
    tok = grads_ready(g, ("w_in",))
    dh0, g["attn_norm_w"] = _mm_rows(
        "in_proj_dx_rms", dproj, w["w_in"], "nn",
        lambda i, du, x, dres, nw, _tok: _f_rms_bwd_add(i, x, du, dres, nw, mask_pad=False),
        [_In(h0), _In(dh2)], [w["attn_norm_w"], tok], [(D_MODEL, F32)], [(1, D_MODEL)])
    return loss, dh0, g


def _w_in_to_padded(w):
    c1, c2, c3 = Q_LORA, Q_LORA + KV_LORA, Q_LORA + KV_LORA + QK_ROPE
    c4 = c3 + 3 * DN_WIDTH
    c5 = c4 + DN_WIDTH
    z = lambda n: jnp.zeros((n, w.shape[1]), w.dtype)
    return jnp.concatenate([w[c3:c4], w[c4:c5], w[:c1], w[c1:c2], w[c2:c3], z(LANE - QK_ROPE),
                            w[c5:], z(LANE - 2 * DN_HEADS)], axis=0)


def _w_in_from_padded(g):
    return jnp.concatenate([g[C_QL:C_QL + Q_LORA], g[C_KVL:C_KVL + KV_LORA], g[C_KPE:C_KPE + QK_ROPE],
                            g[:C_Z + DN_WIDTH], g[C_AB:C_AB + 2 * DN_HEADS]], axis=0)


def _w_q_b_to_padded(w):
    r = w.shape[1]
    w = w.reshape(MLA_HEADS, QK_HEAD, r)
    return jnp.pad(w, ((0, 0), (0, HP - QK_HEAD), (0, 0))).reshape(MLA_HEADS * HP, r)


def _w_q_b_from_padded(g):
    r = g.shape[1]
    return g.reshape(MLA_HEADS, HP, r)[:, :QK_HEAD].reshape(MLA_HEADS * QK_HEAD, r)


def _pad_rows8(w):
    return jnp.pad(w, ((0, 8 - w.shape[0]), (0, 0)))


def _prepare(full, tp):
    w = {}
    mx = lambda a: a.astype(_MXU)
    w["attn_norm_w"] = full["attn_norm_w"]
    w["w_in"] = mx(_w_in_to_padded(full["w_in"]))
    w["q_a_norm_w"] = full["q_a_norm_w"]
    w["kv_a_norm_w"] = full["kv_a_norm_w"]
    w["w_q_b"] = mx(_w_q_b_to_padded(full["w_q_b"]))
    w["w_kv_b"] = mx(full["w_kv_b"])
    w["q_norm_w"] = jnp.pad(full["q_norm_w"], ((0, 0), (0, HP - QK_HEAD)))
    w["k_norm_w"] = jnp.pad(full["k_norm_w"], ((0, 0), (0, HP - QK_HEAD)))
    w["mla_out_norm_w"] = full["mla_out_norm_w"]
    w["dn_out_norm_w"] = full["dn_out_norm_w"]
    w["dn_conv_w"] = _pad_rows8(full["dn_conv_w"])
    w["alog_b"] = jnp.repeat(full["dn_A_log"], DN_DIM, axis=1)
    w["dtb_b"] = jnp.repeat(full["dn_dt_bias"], DN_DIM, axis=1)
    w["ffn_norm_w"] = full["ffn_norm_w"]
    w["ffn_conv_w"] = _pad_rows8(full["ffn_conv_w"])
    w["ffn_conv_b"] = full["ffn_conv_b"]
    for n in _LATE:
        if n in full:
            w[n] = mx(full[n])
    half = QK_ROPE // 2
    inv = ROPE_THETA ** (-jnp.arange(half, dtype=F32) / half)
    ang = (jnp.arange(tp, dtype=jnp.int32) - PAD).astype(F32)[:, None] * inv[None, :]
    zc = jnp.zeros((tp, LANE - QK_ROPE), F32)
    w["cos"] = jnp.concatenate([jnp.cos(ang), jnp.cos(ang), zc], axis=1)
    w["sin_s"] = jnp.concatenate([-jnp.sin(ang), jnp.sin(ang), zc], axis=1)
    return w


def _grads_to_natural(g):
    convert = {
        "w_in": ("w_in", _w_in_from_padded),
        "w_q_b": ("w_q_b", _w_q_b_from_padded),
        "q_norm_w": ("q_norm_w", lambda a: a[:, :QK_HEAD]),
        "k_norm_w": ("k_norm_w", lambda a: a[:, :QK_HEAD]),
        "dn_conv_w": ("dn_conv_w", lambda a: a[:DN_CONV]),
        "ffn_conv_w": ("ffn_conv_w", lambda a: a[:FFN_CONV]),
        "alog_b": ("dn_A_log", lambda a: a[:, ::DN_DIM]),
        "dtb_b": ("dn_dt_bias", lambda a: a[:, ::DN_DIM]),
    }
    n = {}
    for key, a in g.items():
        name, fn = convert.get(key, (key, lambda t: t))
        n[name] = fn(a)
    return n


_MESH = pl.DeviceIdType.MESH
_ANY = pl.BlockSpec(memory_space=pl.ANY)
_CHIP_FLIPS = ((1, 0), (0, 1), (1, 1))


def _me():
    return lax.axis_index("x"), lax.axis_index("y"), lax.axis_index("c")


def _all_gather(name, blk):
    def body(x_ref, out_ref, send_sems, recv_sems, local_sem):
        x, y, c = _me()
        me, sib = (x, y, c), (x, y, 1 - c)
        chips = [(x ^ fx, y ^ fy) for fx, fy in _CHIP_FLIPS]

        def slot(p):
            return out_ref.at[4 * p[0] + 2 * p[1] + p[2]]

        def copy(k, block, to, src=None):
            return pltpu.make_async_remote_copy(
                src_ref=slot(block) if src is None else src, dst_ref=slot(block),
                send_sem=send_sems.at[k], recv_sem=recv_sems.at[k], device_id=to, device_id_type=_MESH)

        mine = pltpu.make_async_copy(x_ref, slot(me), local_sem)
        mine.start()
        first = [copy(0, me, sib, src=x_ref)]
        first += [copy(1 + j, me, (*chip, c), src=x_ref) for j, chip in enumerate(chips)]
        for cp in first:
            cp.start()
        passed = [copy(4 + j, (*chip, c), sib) for j, chip in enumerate(chips)]
        for j, chip in enumerate(chips):
            copy(1 + j, (*chip, c), me).wait_recv()
            passed[j].start()
        copy(0, sib, me).wait_recv()
        for j, chip in enumerate(chips):
            copy(4 + j, (*chip, 1 - c), me).wait_recv()
        for cp in first + passed:
            cp.wait_send()
        mine.wait()

    return pl.pallas_call(
        body, name=name, in_specs=[_ANY], out_specs=_ANY,
        out_shape=jax.ShapeDtypeStruct((N_DEV,) + blk.shape, blk.dtype),
        scratch_shapes=[pltpu.SemaphoreType.DMA((7,)), pltpu.SemaphoreType.DMA((7,)), pltpu.SemaphoreType.DMA],
    )(blk)


def _row_tile(r):
    divs = [d for d in range(16, min(r, 512) + 1, 16) if r % d == 0]
    return divs[-1] if divs else r


def _adam_math(g, w, m, v):
    m_new = ADAM_B1 * m + (1.0 - ADAM_B1) * g
    v_new = ADAM_B2 * v + (1.0 - ADAM_B2) * (g * g)
    m_hat = m_new / (1.0 - ADAM_B1 ** ADAM_STEP)
    v_hat = v_new / (1.0 - ADAM_B2 ** ADAM_STEP)
    return -ADAM_LR * (m_hat / (jnp.sqrt(v_hat) + ADAM_EPS) + ADAM_WD * w), m_new, v_new


def _adam_vectors(name, row, items, ws, ms, vs):
    k = len(items)

    def body(row_ref, *refs):
        w_refs, m_refs, v_refs = refs[:k], refs[k:2 * k], refs[2 * k:3 * k]
        outs = refs[3 * k:]
        for idx, (off, n, per_head) in enumerate(items):
            if per_head:
                spread = row_ref[:, off:off + DN_WIDTH]
                lane = lax.broadcasted_iota(jnp.int32, (1, LANE), 1)
                g = jnp.zeros((1, LANE), F32)
                for h in range(DN_HEADS):
                    g = g + jnp.where(lane == h, spread[:, DN_DIM * h:DN_DIM * h + 1], 0.0)
            else:
                g = row_ref[:, off:off + n]
            d, m_new, v_new = _adam_math(g, w_refs[idx][...], m_refs[idx][...], v_refs[idx][...])
            for kind, val in enumerate((g, d, m_new, v_new)):
                outs[kind * k + idx][...] = val

    shapes = [jax.ShapeDtypeStruct((1, n), F32) for _, n, _ in items]
    res = pl.pallas_call(body, name=name, out_shape=shapes * 4)(row, *ws, *ms, *vs)
    return [list(res[kind * k:(kind + 1) * k]) for kind in range(4)]


def _sum_parts(name, parts):
    _, r, cols = parts[0][0].shape
    tm = _row_tile(r)
    idx = jnp.stack([jnp.asarray(s, jnp.int32) for _, s in parts])
    n = len(parts)

    def body(idx_ref, *refs):
        g = refs[0][0].astype(F32)
        for p_ref in refs[1:n]:
            g = g + p_ref[0].astype(F32)
        refs[n][...] = g

    return pl.pallas_call(
        body, name=name,
        grid_spec=pltpu.PrefetchScalarGridSpec(
            num_scalar_prefetch=1, grid=(r // tm,),
            in_specs=[pl.BlockSpec((1, tm, cols), lambda i, idx_ref, p=p: (idx_ref[p], i, 0)) for p in range(n)],
            out_specs=pl.BlockSpec((tm, cols), lambda i, idx_ref: (i, 0))),
        out_shape=jax.ShapeDtypeStruct((r, cols), F32),
        compiler_params=pltpu.CompilerParams(dimension_semantics=("parallel",)),
    )(idx, *[a for a, _ in parts])


def _adam(name, parts, w, m, v):
    r, cols = w.shape
    tm = _row_tile(r)
    tc = cols // 4 if (r // tm < 4 and cols % (4 * LANE) == 0) else cols
    idx = jnp.stack([jnp.asarray(s, jnp.int32) for _, s in parts])
    n = len(parts)

    def body(idx_ref, *refs):
        g = refs[0][0].astype(F32)
        for p_ref in refs[1:n]:
            g = g + p_ref[0].astype(F32)
        w_ref, m_ref, v_ref, g_out, d_out, m_out, v_out = refs[n:]
        g_out[...] = g
        d_out[...], m_out[...], v_out[...] = _adam_math(g, w_ref[...], m_ref[...], v_ref[...])

    part_specs = [pl.BlockSpec((1, tm, tc), lambda i, j, idx_ref, p=p: (idx_ref[p], i, j)) for p in range(n)]
    flat = pl.BlockSpec((tm, tc), lambda i, j, idx_ref: (i, j))
    return pl.pallas_call(
        body, name=name,
        grid_spec=pltpu.PrefetchScalarGridSpec(
            num_scalar_prefetch=1, grid=(r // tm, cols // tc), in_specs=part_specs + [flat] * 3,
            out_specs=[flat] * 4),
        out_shape=[jax.ShapeDtypeStruct((r, cols), F32)] * 4,
        compiler_params=pltpu.CompilerParams(dimension_semantics=("parallel", "parallel")),
    )(idx, *[a for a, _ in parts], w, m, v)


def _all_gather_many(name, blks):
    n = len(blks)

    def body(*refs):
        x_refs, out_refs = refs[:n], refs[n:2 * n]
        send_sems, recv_sems, local_sems = refs[2 * n:]
        x, y, c = _me()
        me, sib = (x, y, c), (x, y, 1 - c)
        chips = [(x ^ fx, y ^ fy) for fx, fy in _CHIP_FLIPS]

        def slot(a, p):
            return out_refs[a].at[4 * p[0] + 2 * p[1] + p[2]]

        def copy(a, k, block, to, src=None):
            return pltpu.make_async_remote_copy(
                src_ref=slot(a, block) if src is None else src, dst_ref=slot(a, block),
                send_sem=send_sems.at[7 * a + k], recv_sem=recv_sems.at[7 * a + k], device_id=to,
                device_id_type=_MESH)

        mine = [pltpu.make_async_copy(x_refs[a], slot(a, me), local_sems.at[a]) for a in range(n)]
        first = []
        for a in range(n):
            mine[a].start()
            first.append(copy(a, 0, me, sib, src=x_refs[a]))
            first += [copy(a, 1 + j, me, (*chip, c), src=x_refs[a]) for j, chip in enumerate(chips)]
        for cp in first:
            cp.start()
        passed = []
        for j, chip in enumerate(chips):
            for a in range(n):
                copy(a, 1 + j, (*chip, c), me).wait_recv()
                cp = copy(a, 4 + j, (*chip, c), sib)
                cp.start()
                passed.append(cp)
        for a in range(n):
            copy(a, 0, sib, me).wait_recv()
            for j, chip in enumerate(chips):
                copy(a, 4 + j, (*chip, 1 - c), me).wait_recv()
        for cp in first + passed:
            cp.wait_send()
        for cp in mine:
            cp.wait()

    return pl.pallas_call(
        body, name=name, in_specs=[_ANY] * n, out_specs=[_ANY] * n,
        out_shape=[jax.ShapeDtypeStruct((N_DEV,) + b.shape, b.dtype) for b in blks],
        scratch_shapes=[pltpu.SemaphoreType.DMA((7 * n,)), pltpu.SemaphoreType.DMA((7 * n,)),
                        pltpu.SemaphoreType.DMA((n,))],
    )(*blks)


_HBM = pl.BlockSpec(memory_space=pltpu.HBM)
_SEM = pl.BlockSpec(memory_space=pltpu.SEMAPHORE)
_EFFECT = pltpu.SideEffectType.DATAFLOW_SIDE_EFFECTING


def _push_copies(src_refs, land_refs, send_sems, recv_sems, src_by_peer, first=0):
    x, y, c = _me()
    my_id = 4 * x + 2 * y + c
    out = []
    for k in range(len(src_refs)):
        a = first + k
        for f in range(1, N_DEV):
            px, py, pc = x ^ (f >> 2), y ^ ((f >> 1) & 1), c ^ (f & 1)
            pid = 4 * px + 2 * py + pc
            src = src_refs[k].at[pid] if src_by_peer else src_refs[k]
            start = pltpu.make_async_remote_copy(
                src_ref=src, dst_ref=land_refs[k].at[my_id], send_sem=send_sems.at[7 * a + f - 1],
                recv_sem=recv_sems.at[7 * a + f - 1], device_id=(px, py, pc), device_id_type=_MESH)
            landed = pltpu.make_async_remote_copy(
                src_ref=src, dst_ref=land_refs[k].at[pid], send_sem=send_sems.at[7 * a + f - 1],
                recv_sem=recv_sems.at[7 * a + f - 1], device_id=(px, py, pc), device_id_type=_MESH)
            out.append((start, landed))
    return out


def _push_start(name, srcs, src_by_peer, after):
    n = len(srcs)
    lands = [jax.ShapeDtypeStruct((N_DEV,) + (s.shape[1:] if src_by_peer else s.shape), s.dtype) for s in srcs]

    def body(*refs):
        src_refs, land_refs = refs[:n], refs[n:2 * n]
        send_sems, recv_sems = refs[2 * n + 1], refs[2 * n + 2]
        token = refs[-1]
        for start, _ in _push_copies(src_refs, land_refs, send_sems, recv_sems, src_by_peer):
            start.start()
        token[...] = jnp.zeros_like(token)

    hbm = lambda a: pltpu.with_memory_space_constraint(a, pltpu.HBM)
    res = pl.pallas_call(
        body, name=name,
        out_shape=(pltpu.SemaphoreType.DMA((7 * n,)), pltpu.SemaphoreType.DMA((7 * n,)),
                   *[pltpu.HBM(s.shape, s.dtype) for s in srcs], *[pltpu.HBM(s.shape, s.dtype) for s in lands],
                   jax.ShapeDtypeStruct((8, LANE), F32)),
        in_specs=[_HBM] * (2 * n) + [_ANY],
        out_specs=(_SEM, _SEM, *[_HBM] * (2 * n), pl.BlockSpec(memory_space=pltpu.VMEM)),
        input_output_aliases={i: 2 + i for i in range(2 * n)},
        compiler_params=pltpu.CompilerParams(has_side_effects=_EFFECT),
    )(*[hbm(s) for s in srcs], *[hbm(lax.empty(s.shape, s.dtype)) for s in lands], after)
    return res[0], res[1], list(res[2:2 + n]), list(res[2 + n:2 + 2 * n]), res[-1]


def _push_wait(name, send_sems, recv_sems, srcs, lands, src_by_peer, after, first=0):
    n = len(srcs)
    after = list(after) if isinstance(after, (list, tuple)) else [after]

    def body(*refs):
        src_refs, land_refs = refs[:n], refs[n:2 * n]
        s_sems, r_sems = refs[2 * n], refs[2 * n + 1]
        for _, landed in _push_copies(src_refs, land_refs, s_sems, r_sems, src_by_peer, first):
            landed.wait_send()
            landed.wait_recv()

    res = pl.pallas_call(
        body, name=name,
        out_shape=tuple(pltpu.HBM(s.shape, s.dtype) for s in list(srcs) + list(lands)),
        in_specs=[_HBM] * (2 * n) + [_SEM, _SEM] + [_ANY] * len(after),
        out_specs=tuple([_HBM] * (2 * n)),
        input_output_aliases={i: i for i in range(2 * n)},
        compiler_params=pltpu.CompilerParams(has_side_effects=_EFFECT),
    )(*srcs, *lands, send_sems, recv_sems, *after)
    return list(res[:n]), list(res[n:])


_SHARDED = (
    ("meta_tokens", 1, (N_META, D_MODEL)),
    ("w_in", 1, (D_MODEL, IN_COLS)),
    ("w_q_b", 1, (Q_LORA, MLA_HEADS * QK_HEAD)),
    ("w_kv_b", 1, (KV_LORA, MLA_HEADS * (QK_NOPE + V_HEAD))),
    ("dn_conv_w", 1, (DN_CONV, 3 * DN_WIDTH)),
    ("w_out", 0, (2 * DN_WIDTH, D_MODEL)),
    ("w_gate", 1, (D_MODEL, D_FF)),
    ("w_up", 1, (D_MODEL, D_FF)),
    ("ffn_conv_w", 1, (FFN_CONV, D_FF)),
    ("w_down", 0, (D_FF, D_MODEL)),
)
_F32_GATHERED = ("meta_tokens", "dn_conv_w", "ffn_conv_w")
_EARLY = ("w_in", "w_q_b", "w_kv_b")
_LATE = ("w_out", "w_gate", "w_up", "w_down")
_TRANSPOSED = ("w_in", "w_q_b", "w_gate", "w_up")
_REPLICATED = (
    ("attn_norm_w", D_MODEL), ("q_a_norm_w", Q_LORA), ("kv_a_norm_w", KV_LORA), ("q_norm_w", QK_HEAD),
    ("k_norm_w", QK_HEAD), ("mla_out_norm_w", V_HEAD), ("dn_A_log", DN_HEADS), ("dn_dt_bias", DN_HEADS),
    ("dn_out_norm_w", DN_DIM), ("ffn_norm_w", D_MODEL), ("ffn_conv_b", D_FF),
)
_SMALL_BLOCK = (8, 512)


def _local_shape(dim, shape):
    return (shape[0] // N_DEV, shape[1]) if dim == 0 else (shape[0], shape[1] // N_DEV)


def _from_blocks(blocks, dim, shape):
    r, c = shape
    if dim == 0:
        return blocks.reshape(r, c)
    return blocks.reshape(N_DEV, r, c // N_DEV).transpose(1, 0, 2).reshape(r, c)


def _split(flat, sizes):
    out, o = [], 0
    for s in sizes:
        out.append(flat[..., o:o + s])
        o += s
    return out


def kernel(x, meta_tokens, attn_norm_w, w_in, q_a_norm_w, w_q_b, kv_a_norm_w, w_kv_b, q_norm_w, k_norm_w, mla_out_norm_w, dn_conv_w, dn_A_log, dn_dt_bias, dn_out_norm_w, w_out, ffn_norm_w, w_gate, w_up, ffn_conv_w, ffn_conv_b, w_down, loss_target, m_meta_tokens, m_attn_norm_w, m_w_in, m_q_a_norm_w, m_w_q_b, m_kv_a_norm_w, m_w_kv_b, m_q_norm_w, m_k_norm_w, m_mla_out_norm_w, m_dn_conv_w, m_dn_A_log, m_dn_dt_bias, m_dn_out_norm_w, m_w_out, m_ffn_norm_w, m_w_gate, m_w_up, m_ffn_conv_w, m_ffn_conv_b, m_w_down, v_meta_tokens, v_attn_norm_w, v_w_in, v_q_a_norm_w, v_w_q_b, v_kv_a_norm_w, v_w_kv_b, v_q_norm_w, v_k_norm_w, v_mla_out_norm_w, v_dn_conv_w, v_dn_A_log, v_dn_dt_bias, v_dn_out_norm_w, v_w_out, v_ffn_norm_w, v_w_gate, v_w_up, v_ffn_conv_w, v_ffn_conv_b, v_w_down):
    names = [n for n, _, _ in _SHARDED] + [n for n, _ in _REPLICATED]
    given = dict(locals())
    two_d = lambda a: a.reshape(a.shape[-2:])
    view = lambda a, n: two_d(a).T if n in _TRANSPOSED else two_d(a)
    wl = {n: view(given[n], n) for n in names}
    ml = {n: view(given["m_" + n], n) for n in names}
    vl = {n: view(given["v_" + n], n) for n in names}
    out_shapes = {n: given[n].shape for n in names}

    spec = {n: (d, s) for n, d, s in _SHARDED}
    small_sizes = [math.prod(_local_shape(*spec[n])) for n in _F32_GATHERED]

    def small_block(d):
        cat = jnp.concatenate([d[n].reshape(d[n].shape[:-2] + (-1,)) for n in _F32_GATHERED], axis=-1)
        pad = [(0, 0)] * (cat.ndim - 1) + [(0, math.prod(_SMALL_BLOCK) - cat.shape[-1])]
        return jnp.pad(cat, pad).reshape(cat.shape[:-1] + _SMALL_BLOCK)

    def shard(n):
        return wl[n].astype(_MXU)

    def from_slots(n, blocks):
        d, s = spec[n]
        if d == 0 or n in _TRANSPOSED:
            return blocks.reshape(-1, blocks.shape[-1])
        return blocks.transpose(1, 0, 2).reshape(s)

    my_id = 4 * lax.axis_index("x") + 2 * lax.axis_index("y") + lax.axis_index("c")
    got = _all_gather_many("gather_early", [shard(n) for n in _EARLY] + [small_block(wl)])
    full = {n: a for n, a in wl.items() if n not in _LATE}
    for n, blocks in zip(_EARLY, got):
        full[n] = from_slots(n, blocks)
    for n, p in zip(_F32_GATHERED, _split(got[-1].reshape(N_DEV, -1), small_sizes)):
        full[n] = _from_blocks(p, *spec[n])
    late_own = [shard(n) for n in _LATE]
    l_send, l_recv, l_src, l_land, token = _push_start("gather_late_start", late_own, False, got[-1])

    def late_weights(after, names):
        first = _LATE.index(names[0])
        sl = slice(first, first + len(names))
        _, lands = _push_wait("gather_late_wait_" + names[0], l_send, l_recv, l_src[sl], l_land[sl], False,
                              after, first)
        out = {}
        for n, land, own in zip(names, lands, late_own[sl]):
            out[n] = from_slots(n, lax.dynamic_update_slice(land, own[None], (my_id, 0, 0))).astype(_MXU)
        return out

    def dest_blocks(n, a):
        d, s = spec[n]
        r, c = _local_shape(d, s)
        if n in _TRANSPOSED:
            return a.reshape(N_DEV, c, r)
        return a.reshape(N_DEV, r, c) if d == 0 else a.reshape(r, N_DEV, c).transpose(1, 0, 2)

    pushed = []

    def grads_ready(g, names):
        nat = _grads_to_natural({n: g[n] for n in names})
        blocks = [dest_blocks(n, nat[n]).astype(_MXU) for n in names]
        sends, recvs, srcs, lands, tok = _push_start("rs_" + names[0] + "_start", blocks, True, token)
        pushed.append((names, sends, recvs, srcs, lands))
        return tok

    seq = x.shape[1]
    tp = ROW0 + seq
    h0 = jnp.concatenate([jnp.zeros((PAD, D_MODEL), F32), full["meta_tokens"], x[0]], axis=0)
    tgt = jnp.concatenate([jnp.zeros((ROW0, D_MODEL), F32), loss_target[0]], axis=0)
    loss, dh0, raw = _local_step(h0, tgt, _prepare(full, tp), token, late_weights, grads_ready)
    g = _grads_to_natural(raw)
    g["meta_tokens"] = dh0[PAD:ROW0]
    grad_x = dh0[ROW0:][None]

    big = [{}, {}, {}, {}]
    rep_names = [n for n, _ in _REPLICATED]
    raw_key = {"dn_A_log": "alog_b", "dn_dt_bias": "dtb_b"}
    pieces = [raw[raw_key.get(n, n)] for n in rep_names] + [loss]
    pieces += [g[n].reshape(1, -1) for n in _F32_GATHERED]
    widths = [p.shape[1] for p in pieces]
    offs = [sum(widths[:k]) for k in range(len(widths))]
    cat = jnp.concatenate(pieces, axis=1)
    cols = -(-cat.shape[1] // (8 * LANE)) * LANE
    mine = jnp.pad(cat, ((0, 0), (0, 8 * cols - cat.shape[1]))).reshape(8, cols)
    everyone = _all_gather("gather_small_grads", mine)
    total = _sum_parts("sum_small_grads", [(everyone, d) for d in range(N_DEV)]).reshape(1, 8 * cols)
    tot = {n: total[0, o:o + wd] for n, o, wd in zip(rep_names + ["loss"] + list(_F32_GATHERED), offs, widths)}
    lanes = lambda a: jnp.pad(a, ((0, 0), (0, -a.shape[1] % LANE)))
    items = [(o, -(-size // LANE) * LANE, n in raw_key) for (n, size), o in zip(_REPLICATED, offs)]
    sm = _adam_vectors("adam_replicated", total, items, [lanes(wl[n]) for n in rep_names],
                       [lanes(ml[n]) for n in rep_names], [lanes(vl[n]) for n in rep_names])
    sm = [{n: a[:, :size] for (n, size), a in zip(_REPLICATED, kind)} for kind in sm]
    mine_of = {}
    for n in _F32_GATHERED:
        d, s = spec[n]
        r, c = _local_shape(d, s)
        mine_of[n] = lax.dynamic_slice(tot[n].reshape(s), (0, my_id * c), (r, c))
    res = _adam("adam_small_sharded", [(small_block(mine_of)[None], 0)], small_block(wl), small_block(ml),
                small_block(vl))
    for kind, a in enumerate(res):
        big[kind].update(zip(_F32_GATHERED, _split(a.reshape(-1), small_sizes)))

    for names, sends, recvs, srcs, lands in pushed:
        srcs, lands = _push_wait("rs_" + names[0] + "_wait", sends, recvs, srcs, lands, True, dh0)
        for n, src, land in zip(names, srcs, lands):
            parts = [(src, my_id)] + [(land, my_id ^ f) for f in range(1, N_DEV)]
            for kind, a in enumerate(_adam("adam_" + n, parts, wl[n], ml[n], vl[n])):
                big[kind][n] = a

    outs = [tot["loss"][0], grad_x]
    for kind in range(4):
        for n in ("meta_tokens", "attn_norm_w", "w_in", "q_a_norm_w", "w_q_b", "kv_a_norm_w", "w_kv_b", "q_norm_w",
                  "k_norm_w", "mla_out_norm_w", "dn_conv_w", "dn_A_log", "dn_dt_bias", "dn_out_norm_w", "w_out",
                  "ffn_norm_w", "w_gate", "w_up", "ffn_conv_w", "ffn_conv_b", "w_down"):
            src = big[kind] if n in big[kind] else sm[kind]
            a = src[n].T if n in _TRANSPOSED else src[n]
            outs.append(a.reshape(out_shapes[n]))
    return tuple(outs)
```

```python
import functools
import math

import jax
import jax.numpy as jnp
from jax import lax
from jax.experimental import pallas as pl
from jax.experimental.pallas import tpu as pltpu

F32 = jnp.float32
_MXU = jnp.bfloat16
_HI = lax.Precision.HIGHEST

D_MODEL = 1024
N_META = 16
PAD = 112
ROW0 = PAD + N_META
MLA_HEADS = 4
QK_NOPE = 128
QK_ROPE = 64
QK_HEAD = QK_NOPE + QK_ROPE
V_HEAD = 128
Q_LORA = 256
KV_LORA = 256
ROPE_THETA = 10000.0
DN_HEADS = 4
DN_DIM = 128
DN_WIDTH = DN_HEADS * DN_DIM
DN_CONV = 4
DN_CHUNK = 64
GDN_SUB_CHUNKS = 2
D_FF = 2816
FFN_CONV = 3
EPS = 1e-6
HP = 256
C_Z = 1536
C_QL = 2048
C_KVL = 2304
C_KPE = 2560
C_AB = 2688
IN_COLS = 2632

ADAM_LR = 0.001
ADAM_B1 = 0.9
ADAM_B2 = 0.999
ADAM_EPS = 1e-08
ADAM_WD = 0.01
ADAM_STEP = 10

N_DEV = 8
TM = 128
LANE = 128
VMEM_LIMIT = 56 * 1024 * 1024
NEG = -1e30


def _dot(a, b, dims, hp=False):
    if hp:
        return lax.dot_general(a.astype(F32), b.astype(F32), (dims, ((), ())),
                               precision=lax.Precision.HIGH if hp == "3x" else _HI, preferred_element_type=F32)
    return lax.dot_general(a.astype(_MXU), b.astype(_MXU), (dims, ((), ())),
                           preferred_element_type=F32)


def _nn(a, b, hp=False):
    return _dot(a, b, ((1,), (0,)), hp)


def _nt(a, b, hp=False):
    return _dot(a, b, ((1,), (1,)), hp)


def _tn(a, b, hp=False):
    return _dot(a, b, ((0,), (0,)), hp)


def _sigmoid(x):
    return 1.0 / (1.0 + jnp.exp(-x))


def _rms_fwd(x, w, n):
    r = lax.rsqrt(jnp.sum(x * x, axis=-1, keepdims=True) * (1.0 / n) + EPS)
    return x * r * w, r


def _rms_bwd(x, w, dy, n):
    r = lax.rsqrt(jnp.sum(x * x, axis=-1, keepdims=True) * (1.0 / n) + EPS)
    xh = x * r
    gy = dy * w
    dx = r * (gy - xh * (jnp.sum(gy * xh, axis=-1, keepdims=True) * (1.0 / n)))
    return dx, dy * xh


def _rowsum(x):
    return jnp.sum(x, axis=0, keepdims=True)


def _row_ids(i, tm):
    return i * tm + lax.broadcasted_iota(jnp.int32, (tm, 1), 0)


def _shift_down(ext, s, tm):
    if s == 0:
        return ext[8:8 + tm]
    return pltpu.roll(ext, s, 0)[8:8 + tm]


def _shift_up(ext, s, tm):
    if s == 0:
        return ext[0:tm]
    return pltpu.roll(ext, tm + 8 - s, 0)[0:tm]


def _conv_fwd(x, halo_prev, w, width):
    tm = x.shape[0]
    ext = jnp.concatenate([halo_prev, x], axis=0)
    y = None
    for j in range(width):
        t = w[j:j + 1, :] * _shift_down(ext, width - 1 - j, tm)
        y = t if y is None else y + t
    return y


def _conv_bwd_x(dy, halo_next, w, width):
    tm = dy.shape[0]
    ext = jnp.concatenate([dy, halo_next], axis=0)
    dx = None
    for j in range(width):
        t = w[j:j + 1, :] * _shift_up(ext, width - 1 - j, tm)
        dx = t if dx is None else dx + t
    return dx


def _conv_bwd_w(dy, x, halo_prev, width):
    tm = dy.shape[0]
    ext = jnp.concatenate([halo_prev, x], axis=0)
    rows = [_rowsum(dy * _shift_down(ext, width - 1 - j, tm)) for j in range(width)]
    rows += [jnp.zeros_like(rows[0])] * (8 - width)
    return jnp.concatenate(rows, axis=0)


def _softplus(x):
    e = jnp.exp(-jnp.abs(x))
    u = 1.0 + e
    l1p = jnp.where(u == 1.0, e, jnp.log(u) * e / jnp.where(u == 1.0, 1.0, u - 1.0))
    return jnp.maximum(x, 0.0) + l1p


def _swap_halves(x):
    lane = lax.broadcasted_iota(jnp.int32, x.shape, 1)
    return jnp.where(lane < 32, pltpu.roll(x, 96, 1), jnp.where(lane < 64, pltpu.roll(x, 32, 1), 0.0))


class _In:
    def __init__(self, arr, width=None, cb=0, kind="cur"):
        self.arr, self.kind = arr, kind
        self.width = arr.shape[1] if width is None else width
        self.cb = cb


def _whole_spec(x):
    return pl.BlockSpec(x.shape, lambda i, nd=x.ndim: (0,) * nd, pipeline_mode=pl.Buffered(1))


def _tile_spec(t, tm, tp):
    r8 = tm // 8
    if t.kind == "cur":
        return pl.BlockSpec((tm, t.width), lambda i, cb=t.cb: (i, cb))
    if t.kind == "prev":
        return pl.BlockSpec((8, t.width), lambda i, cb=t.cb: (jnp.maximum(i * r8 - 1, 0), cb))
    return pl.BlockSpec((8, t.width), lambda i, cb=t.cb: (jnp.minimum((i + 1) * r8, tp // 8 - 1), cb))


def _rows(name, fn, tiled, full, outs, accs=(), tm=TM):
    tp = tiled[0].arr.shape[0]
    nt = tp // tm
    n_in = len(tiled) + len(full)
    n_out = len(outs)

    def body(*refs):
        i = pl.program_id(0)
        vals = [r[...] for r in refs[:n_in]]
        o_t, o_a = fn(i, *vals)
        for r, v in zip(refs[n_in:n_in + n_out], o_t):
            r[...] = v.astype(r.dtype)
        for r, v in zip(refs[n_in + n_out:], o_a):
            @pl.when(i == 0)
            def _():
                r[...] = v

            @pl.when(i > 0)
            def _():
                r[...] += v

    in_specs = [_tile_spec(t, tm, tp) for t in tiled]
    in_specs += [pl.BlockSpec(a.shape, lambda i, nd=a.ndim: (0,) * nd) for a in full]
    out_specs = [pl.BlockSpec((tm, w), lambda i: (i, 0)) for w, _ in outs]
    out_specs += [pl.BlockSpec((r, w), lambda i: (0, 0)) for r, w in accs]
    out_shape = [jax.ShapeDtypeStruct((tp, w), dt) for w, dt in outs]
    out_shape += [jax.ShapeDtypeStruct((r, w), F32) for r, w in accs]
    res = pl.pallas_call(
        body, name=name, grid=(nt,), in_specs=in_specs, out_specs=out_specs, out_shape=out_shape,
        compiler_params=pltpu.CompilerParams(dimension_semantics=("arbitrary",), vmem_limit_bytes=VMEM_LIMIT),
    )(*[t.arr for t in tiled], *full)
    return res


def _pick(n, cap, mult):
    best = None
    for d in range(mult, min(n, cap) + 1, mult):
        if n % d == 0:
            best = d
    assert best is not None, (n, cap, mult)
    return best


_ANY_SPEC = pl.BlockSpec(memory_space=pl.ANY)


def _mm(name, a, b, mode, out_dtype=F32, resid=None, after=None):
    if mode == "tn":
        m, k = a.shape
        n = b.shape[1]
        tk = _pick(k, 512, 128)
        tn = _pick(n, 1408, 128)

        def body_tn(a_ref, b_ref, o_ref):
            o_ref[...] = _tn(a_ref[...], b_ref[...]).astype(o_ref.dtype)

        return pl.pallas_call(
            body_tn, name=name, grid=(n // tn, k // tk),
            in_specs=[pl.BlockSpec((m, tk), lambda j, p: (0, p)),
                      pl.BlockSpec((m, tn), lambda j, p: (0, j))],
            out_specs=pl.BlockSpec((tk, tn), lambda j, p: (p, j)),
            out_shape=jax.ShapeDtypeStruct((k, n), out_dtype),
            compiler_params=pltpu.CompilerParams(
                dimension_semantics=("parallel", "parallel"), vmem_limit_bytes=VMEM_LIMIT),
        )(a, b)

    m, k = a.shape
    n = b.shape[1] if mode == "nn" else b.shape[0]
    tn = _pick(n, 1408, 128)
    tm = _pick(m, 1152, 16)
    dotf = _nn if mode == "nn" else _nt

    def body(*refs):
        a_ref, b_ref, o_ref = refs[0], refs[1], refs[-1]
        acc = dotf(a_ref[...], b_ref[...])
        if resid is not None:
            acc = refs[2][...] + acc
        o_ref[...] = acc.astype(o_ref.dtype)

    b_spec = (pl.BlockSpec((k, tn), lambda j, i: (0, j)) if mode == "nn"
              else pl.BlockSpec((tn, k), lambda j, i: (j, 0)))
    in_specs = [pl.BlockSpec((tm, k), lambda j, i: (i, 0)), b_spec]
    args = [a, b]
    if resid is not None:
        in_specs.append(pl.BlockSpec((tm, tn), lambda j, i: (i, j)))
        args.append(resid)
    if after is not None:
        in_specs.append(_ANY_SPEC)
        args.append(after)
    return pl.pallas_call(
        body, name=name, grid=(n // tn, m // tm), in_specs=in_specs,
        out_specs=pl.BlockSpec((tm, tn), lambda j, i: (i, j)),
        out_shape=jax.ShapeDtypeStruct((m, n), out_dtype),
        compiler_params=pltpu.CompilerParams(
            dimension_semantics=("parallel", "parallel"), vmem_limit_bytes=VMEM_LIMIT),
    )(*args)


def _mm_tn2(name, a1, a2, b, out_dtype=F32):
    m, k = a1.shape
    n = b.shape[1]
    tk = _pick(k, 512, 128)

    def body(a1_ref, a2_ref, b_ref, o1_ref, o2_ref):
        bb = b_ref[...]
        o1_ref[...] = _tn(a1_ref[...], bb).astype(o1_ref.dtype)
        o2_ref[...] = _tn(a2_ref[...], bb).astype(o2_ref.dtype)

    a_spec = pl.BlockSpec((m, tk), lambda p: (0, p))
    o_spec = pl.BlockSpec((tk, n), lambda p: (p, 0))
    return pl.pallas_call(
        body, name=name, grid=(k // tk,),
        in_specs=[a_spec, a_spec, pl.BlockSpec((m, n), lambda p: (0, 0))],
        out_specs=[o_spec, o_spec], out_shape=[jax.ShapeDtypeStruct((k, n), out_dtype)] * 2,
        compiler_params=pltpu.CompilerParams(dimension_semantics=("parallel",), vmem_limit_bytes=VMEM_LIMIT),
    )(a1, a2, b)


def _norm_mm(name, x, norm_w, b, mode="nt", x_cb=0, after=None):
    m = x.shape[0]
    k = norm_w.shape[1]
    n = b.shape[0] if mode == "nt" else b.shape[1]
    tn = _pick(n, 1408, 128)
    tm = _pick(m, 1152, 16)
    dotf = _nt if mode == "nt" else _nn
    extra = [] if after is None else [after]

    def body(x_ref, w_ref, b_ref, *rest):
        o_ref, u_ref = rest[-2:]

        @pl.when(pl.program_id(1) == 0)
        def _():
            u_ref[...] = _rms_fwd(x_ref[...], w_ref[...], k)[0].astype(u_ref.dtype)

        o_ref[...] = dotf(u_ref[...], b_ref[...])

    b_spec = (pl.BlockSpec((tn, k), lambda i, j: (j, 0)) if mode == "nt"
              else pl.BlockSpec((k, tn), lambda i, j: (0, j)))
    return pl.pallas_call(
        body, name=name, grid=(m // tm, n // tn),
        in_specs=[pl.BlockSpec((tm, k), lambda i, j: (i, x_cb)), pl.BlockSpec((1, k), lambda i, j: (0, 0)),
                  b_spec] + [_ANY_SPEC] * len(extra),
        out_specs=[pl.BlockSpec((tm, tn), lambda i, j: (i, j)), pl.BlockSpec((tm, k), lambda i, j: (i, 0))],
        out_shape=[jax.ShapeDtypeStruct((m, n), F32), jax.ShapeDtypeStruct((m, k), _MXU)],
        compiler_params=pltpu.CompilerParams(
            dimension_semantics=("arbitrary", "arbitrary"), vmem_limit_bytes=VMEM_LIMIT),
    )(x, norm_w, b, *extra)


def _pro_mm(name, fn, tiled, full, k, b, resid):
    m = resid.shape[0]
    n = b.shape[1]
    tm = _pick(m, 576, 16)
    n_in = len(tiled) + len(full)

    def body(*refs):
        i = pl.program_id(0)
        u = fn(i, *[r[...] for r in refs[:n_in]]).astype(_MXU)
        b_ref, r_ref, o_ref, u_ref = refs[n_in:]
        u_ref[...] = u
        o_ref[...] = r_ref[...] + _nn(u, b_ref[...])

    row = lambda w: pl.BlockSpec((tm, w), lambda i: (i, 0))
    in_specs = [_tile_spec(t, tm, m) for t in tiled]
    in_specs += [_whole_spec(x) for x in full] + [_whole_spec(b), row(n)]
    return pl.pallas_call(
        body, name=name, grid=(m // tm,), in_specs=in_specs, out_specs=[row(n), row(k)],
        out_shape=[jax.ShapeDtypeStruct((m, n), F32), jax.ShapeDtypeStruct((m, k), _MXU)],
        compiler_params=pltpu.CompilerParams(dimension_semantics=("parallel",), vmem_limit_bytes=VMEM_LIMIT),
    )(*[t.arr for t in tiled], *full, b, resid)


def _ffn_in(h2, norm_w, w_gate_t, w_up_t, conv_w8, conv_b):
    m, k = h2.shape
    n = w_gate_t.shape[0]
    tm = _pick(m, 288, 16)

    def body(x_ref, xp_ref, nw_ref, wg_ref, wu_ref, cw_ref, cb_ref, hn_ref, gp_ref, up_ref, act_ref):
        i = pl.program_id(0)
        nw = nw_ref[...]
        hn = _rms_fwd(x_ref[...], nw, k)[0].astype(_MXU)
        hn_prev = _rms_fwd(xp_ref[...], nw, k)[0].astype(_MXU)
        wg = wg_ref[...]
        gp = _nt(hn, wg)
        gp_prev = jnp.where(i > 0, _nt(hn_prev, wg), 0.0)
        up = _nt(hn, wu_ref[...])
        gate = _conv_fwd(gp, gp_prev, cw_ref[...], FFN_CONV) + cb_ref[...]
        hn_ref[...] = hn
        gp_ref[...] = gp
        up_ref[...] = up
        act_ref[...] = (_silu_parts(gate)[0] * up).astype(act_ref.dtype)

    row = lambda w: pl.BlockSpec((tm, w), lambda i: (i, 0))
    r8 = tm // 8
    return pl.pallas_call(
        body, name="ffn_in", grid=(m // tm,),
        in_specs=[row(k), pl.BlockSpec((8, k), lambda i: (jnp.maximum(i * r8 - 1, 0), 0)), _whole_spec(norm_w),
                  _whole_spec(w_gate_t), _whole_spec(w_up_t), _whole_spec(conv_w8), _whole_spec(conv_b)],
        out_specs=[row(k), row(n), row(n), row(n)],
        out_shape=[jax.ShapeDtypeStruct((m, k), _MXU), jax.ShapeDtypeStruct((m, n), F32),
                   jax.ShapeDtypeStruct((m, n), F32), jax.ShapeDtypeStruct((m, n), _MXU)],
        compiler_params=pltpu.CompilerParams(dimension_semantics=("parallel",), vmem_limit_bytes=VMEM_LIMIT),
    )(h2, h2, norm_w, w_gate_t, w_up_t, conv_w8, conv_b)


def _mm_rows(name, a, b, mode, fn, tiled, full, outs, accs=(), tm_cap=576):
    a_list = list(a) if isinstance(a, (list, tuple)) else [a]
    b_list = list(b) if isinstance(b, (list, tuple)) else [b]
    na = len(a_list)
    m = a_list[0].shape[0]
    tm = _pick(m, tm_cap, 16)
    dotf = _nn if mode == "nn" else _nt
    n_in = len(tiled) + len(full)
    n_out = len(outs)
    first = 2 * na

    def body(*refs):
        i = pl.program_id(0)
        vals = [r[...] for r in refs[first:first + n_in]]
        acc = dotf(refs[0][...], refs[na][...])
        for p in range(1, na):
            acc = acc + dotf(refs[p][...], refs[na + p][...])
        o_t, o_a = fn(i, acc, *vals)
        for r, v in zip(refs[first + n_in:first + n_in + n_out], o_t):
            r[...] = v.astype(r.dtype)
        for r, v in zip(refs[first + n_in + n_out:], o_a):
            @pl.when(i == 0)
            def _():
                r[...] = v

            @pl.when(i > 0)
            def _():
                r[...] += v

    whole = lambda x: pl.BlockSpec(x.shape, lambda i, nd=x.ndim: (0,) * nd)
    in_specs = [pl.BlockSpec((tm, x.shape[1]), lambda i: (i, 0)) for x in a_list] + [_whole_spec(x) for x in b_list]
    in_specs += [_tile_spec(t, tm, m) for t in tiled]
    in_specs += [whole(x) for x in full]
    out_specs = [pl.BlockSpec((tm, w), lambda i: (i, 0)) for w, _ in outs]
    out_specs += [pl.BlockSpec((r, w), lambda i: (0, 0)) for r, w in accs]
    out_shape = [jax.ShapeDtypeStruct((m, w), dt) for w, dt in outs]
    out_shape += [jax.ShapeDtypeStruct((r, w), F32) for r, w in accs]
    return pl.pallas_call(
        body, name=name, grid=(m // tm,), in_specs=in_specs, out_specs=out_specs, out_shape=out_shape,
        compiler_params=pltpu.CompilerParams(dimension_semantics=("arbitrary",), vmem_limit_bytes=VMEM_LIMIT),
    )(*a_list, *b_list, *[t.arr for t in tiled], *full)


ATTN_Q_TILES = 4


def _attn_probs(q, k, row0):
    tq, tp = q.shape[0], k.shape[0]
    s = _nt(q, k) * (1.0 / math.sqrt(QK_HEAD))
    row = row0 + lax.broadcasted_iota(jnp.int32, (tq, tp), 0)
    col = lax.broadcasted_iota(jnp.int32, (tq, tp), 1)
    ok = (col <= row) & (col >= PAD)
    s = jnp.where(ok, s, NEG)
    m = jnp.max(s, axis=-1, keepdims=True)
    e = jnp.exp(s - m)
    return e * (1.0 / jnp.sum(e, axis=-1, keepdims=True))


def _attn_fwd(q, k, v):
    tp = q.shape[0]
    tq = tp // ATTN_Q_TILES

    def body(q_ref, k_ref, v_ref, o_ref):
        for i in range(ATTN_Q_TILES):
            rows = slice(i * tq, (i + 1) * tq)
            keys = slice(0, (i + 1) * tq)
            p = _attn_probs(q_ref[rows, :], k_ref[keys, :], i * tq)
            o_ref[rows, :] = _nn(p, v_ref[keys, :])

    return pl.pallas_call(
        body, name="attn_fwd", grid=(MLA_HEADS,),
        in_specs=[pl.BlockSpec((tp, HP), lambda h: (0, h)),
                  pl.BlockSpec((tp, HP), lambda h: (0, h)),
                  pl.BlockSpec((tp, V_HEAD), lambda h: (0, h))],
        out_specs=pl.BlockSpec((tp, V_HEAD), lambda h: (0, h)),
        out_shape=jax.ShapeDtypeStruct((tp, MLA_HEADS * V_HEAD), F32),
        compiler_params=pltpu.CompilerParams(dimension_semantics=("parallel",), vmem_limit_bytes=VMEM_LIMIT),
    )(q, k, v)


def _attn_bwd(q, k, v, do):
    tp = q.shape[0]
    tq = tp // ATTN_Q_TILES

    def body(q_ref, k_ref, v_ref, do_ref, dq_ref, dk_ref, dv_ref):
        for i in reversed(range(ATTN_Q_TILES)):
            rows = slice(i * tq, (i + 1) * tq)
            keys = slice(0, (i + 1) * tq)
            qb = q_ref[rows, :]
            kk = k_ref[keys, :]
            dob = do_ref[rows, :]
            p = _attn_probs(qb, kk, i * tq)
            dp = _nt(dob, v_ref[keys, :])
            delta = jnp.sum(p * dp, axis=-1, keepdims=True)
            ds = p * (dp - delta) * (1.0 / math.sqrt(QK_HEAD))
            dq_ref[rows, :] = _nn(ds, kk)
            if i == ATTN_Q_TILES - 1:
                dk_ref[...] = _tn(ds, qb)
                dv_ref[...] = _tn(p, dob)
            else:
                dk_ref[keys, :] += _tn(ds, qb)
                dv_ref[keys, :] += _tn(p, dob)

    full = lambda w: pl.BlockSpec((tp, w), lambda h: (0, h))
    return pl.pallas_call(
        body, name="attn_bwd", grid=(MLA_HEADS,),
        in_specs=[full(HP), full(HP), full(V_HEAD), full(V_HEAD)],
        out_specs=[full(HP), full(HP), full(V_HEAD)],
        out_shape=[jax.ShapeDtypeStruct((tp, MLA_HEADS * HP), F32),
                   jax.ShapeDtypeStruct((tp, MLA_HEADS * HP), F32),
                   jax.ShapeDtypeStruct((tp, MLA_HEADS * V_HEAD), F32)],
        compiler_params=pltpu.CompilerParams(dimension_semantics=("parallel",), vmem_limit_bytes=VMEM_LIMIT),
    )(q, k, v, do)


def _gdn_consts():
    c = DN_CHUNK
    r = lax.broadcasted_iota(jnp.int32, (c, c), 0)
    cc = lax.broadcasted_iota(jnp.int32, (c, c), 1)
    incl = r >= cc
    strict = r > cc
    return incl, strict


def _cumsum_rows(x, reverse=False):
    c = x.shape[0]
    row = lax.broadcasted_iota(jnp.int32, x.shape, 0)
    s = 1
    while s < c:
        if reverse:
            x = x + jnp.where(row < c - s, pltpu.roll(x, c - s, 0), 0.0)
        else:
            x = x + jnp.where(row >= s, pltpu.roll(x, s, 0), 0.0)
        s *= 2
    return x


def _each(fn, *lists):
    return [fn(*a) for a in zip(*lists)]


def _interleave(chains):
    chains = list(chains)
    while chains:
        for ch in list(chains):
            try:
                next(ch)
            except StopIteration:
                chains.remove(ch)


def _gdn_chunk_common(q_ref, k_ref, v_ref, g_ref, b_ref):
    c = DN_CHUNK
    incl, strict = _gdn_consts()
    sls = [(slice(c * sub, c * (sub + 1)), slice(DN_DIM * h, DN_DIM * (h + 1)))
           for sub in range(GDN_SUB_CHUNKS) for h in range(DN_HEADS)]
    q = [q_ref[sl] * (1.0 / math.sqrt(DN_DIM)) for sl in sls]
    k = [k_ref[sl] for sl in sls]
    v = [v_ref[sl] for sl in sls]
    g = [g_ref[sl] for sl in sls]
    beta = [b_ref[sl] for sl in sls]
    gc = [_cumsum_rows(x) for x in g]
    grow = [x.T[:c, :] for x in gc]
    kb = _each(jnp.multiply, k, beta)
    kk = _each(_nt, kb, k)
    qk = _each(_nt, q, k)
    gam = [jnp.exp(x) for x in gc]
    g_last = [_rowsum(x) for x in g]
    dm = [jnp.exp(jnp.where(incl, x[:, :c] - y, NEG)) for x, y in zip(gc, grow)]
    vb = _each(jnp.multiply, v, beta)
    kbg = _each(jnp.multiply, kb, gam)
    ek = [jnp.exp(x - y) for x, y in zip(g_last, gc)]
    kd = _each(jnp.multiply, k, ek)
    return dict(q=q, k=k, v=v, beta=beta, gc=gc, gam=gam, g_last=g_last, dm=dm, kb=kb, vb=vb,
                kbg=kbg, kk=kk, ek=ek, kd=kd, qk=qk, incl=incl, strict=strict, sls=sls)


def _gdn_fwd(proj, conv_w8, alog, dtb):
    tp = proj.shape[0]
    c = DN_CHUNK
    nch = tp // c
    blk = GDN_SUB_CHUNKS * c

    def body(x_ref, xp_ref, ab_ref, w8_ref, alog_ref, dtb_ref,
             o_ref, s_ref, t_ref, q_ref, k_ref, v_ref, g_ref, b_ref, s_scr):
        @pl.when(pl.program_id(0) == 0)
        def _():
            s_scr[...] = jnp.zeros_like(s_scr)

        staged, _ = _f_gdn_prep(pl.program_id(0), x_ref[...], xp_ref[...], ab_ref[...], w8_ref[...],
                                alog_ref[...], dtb_ref[...])
        for ref, val in zip((q_ref, k_ref, v_ref, g_ref, b_ref), staged):
            ref[...] = val
        eye = (lax.broadcasted_iota(jnp.int32, (c, c), 0) == lax.broadcasted_iota(jnp.int32, (c, c), 1)).astype(F32)
        x = _gdn_chunk_common(q_ref, k_ref, v_ref, g_ref, b_ref)
        heads = range(DN_HEADS)
        bp = [-jnp.where(x["strict"], kk * dm, 0.0) for kk, dm in zip(x["kk"], x["dm"])]
        t = [eye + b for b in bp]
        for _ in range(5):
            bp = [_nn(b, b, hp="3x") for b in bp]
            t = [tt + _nn(tt, b, hp="3x") for tt, b in zip(t, bp)]
        u = _each(_nn, t, x["vb"])
        w = _each(_nn, t, x["kbg"])
        qg = _each(jnp.multiply, x["q"], x["gam"])
        mqk = _each(jnp.multiply, x["qk"], x["dm"])
        s = [s_scr[h] for h in heads]
        for sub in range(GDN_SUB_CHUNKS):
            e = [DN_HEADS * sub + h for h in heads]
            v_new = [u[i] - _nn(w[i], s[h]) for h, i in zip(heads, e)]
            o = [_nn(qg[i], s[h]) + _nn(mqk[i], v_new[h]) for h, i in zip(heads, e)]
            s_new = [s[h] * jnp.exp(x["g_last"][i]) + _tn(x["kd"][i], v_new[h]) for h, i in zip(heads, e)]
            for h, i in zip(heads, e):
                s_ref[h, sub] = s[h]
                t_ref[h, sub] = t[i]
                o_ref[x["sls"][i]] = o[h]
            s = s_new
        for h in heads:
            s_scr[h] = s[h]

    sub = GDN_SUB_CHUNKS
    rb = lambda n: (n, 0)
    rows = pl.BlockSpec((blk, DN_WIDTH), rb)
    whole = lambda a: pl.BlockSpec(a.shape, lambda n: (0, 0))
    return pl.pallas_call(
        body, name="gdn_fwd", grid=(nch // sub,),
        in_specs=[pl.BlockSpec((blk, 3 * DN_WIDTH), rb),
                  pl.BlockSpec((8, 3 * DN_WIDTH), lambda n: (jnp.maximum(n * (blk // 8) - 1, 0), 0)),
                  pl.BlockSpec((blk, LANE), lambda n: (n, C_AB // LANE)),
                  whole(conv_w8), whole(alog), whole(dtb)],
        out_specs=[rows,
                   pl.BlockSpec((DN_HEADS, sub, DN_DIM, DN_DIM), lambda n: (0, n, 0, 0)),
                   pl.BlockSpec((DN_HEADS, sub, c, c), lambda n: (0, n, 0, 0))] + [rows] * 5,
        out_shape=[jax.ShapeDtypeStruct((tp, DN_WIDTH), F32),
                   jax.ShapeDtypeStruct((DN_HEADS, nch, DN_DIM, DN_DIM), F32),
                   jax.ShapeDtypeStruct((DN_HEADS, nch, c, c), F32)] + [jax.ShapeDtypeStruct((tp, DN_WIDTH), F32)] * 5,
        scratch_shapes=[pltpu.VMEM((DN_HEADS, DN_DIM, DN_DIM), F32)],
        compiler_params=pltpu.CompilerParams(dimension_semantics=("arbitrary",), vmem_limit_bytes=VMEM_LIMIT),
    )(proj, proj, proj, conv_w8, alog, dtb)


def _gdn_bwd(q, k, v, g, beta, s_all, t_all, do, proj, conv_w8, alog, dtb, after):
    tp = q.shape[0]
    c = DN_CHUNK
    nch = tp // c
    nblk = nch // GDN_SUB_CHUNKS
    blk = GDN_SUB_CHUNKS * c

    def body(q_ref, k_ref, v_ref, g_ref, b_ref, s_ref, t_ref, do_ref, x_ref, xp_ref, xn_ref, ab_ref,
             w8_ref, alog_ref, dtb_ref, _after_ref, dqkv_ref, dab_ref, dcw_ref, dalog_ref, ddtb_ref,
             ds_scr, dq_ref, dk_ref, dv_ref, dg_ref, db_ref, nxt_scr):
        step = pl.program_id(0)

        @pl.when(step == 0)
        def _():
            ds_scr[...] = jnp.zeros_like(ds_scr)
            nxt_scr[...] = jnp.zeros_like(nxt_scr)

        xs = _gdn_chunk_common(q_ref, k_ref, v_ref, g_ref, b_ref)

        ds_state = [ds_scr[h] for h in range(DN_HEADS)]

        def chain(sub, h):
            e = DN_HEADS * sub + h
            x = {key: (val[e] if isinstance(val, list) else val) for key, val in xs.items()}
            sl = x["sls"]
            qs, kx, vx, beta_, gam, dm = x["q"], x["k"], x["v"], x["beta"], x["gam"], x["dm"]
            kb, vb, kbg, kd, ek = x["kb"], x["vb"], x["kbg"], x["kd"], x["ek"]
            t = t_ref[h, sub]
            s = s_ref[h, sub]
            dsn = ds_state[h]
            dob = do_ref[sl]
            eg_last = jnp.exp(x["g_last"])
            u = _nn(t, vb)
            w = _nn(t, kbg)
            mqk = x["qk"] * dm
            qd = qs * gam
            dqd = _nt(dob, s)
            dkd_pre = _nn(kd, dsn)
            yield
            v_new = u - _nn(w, s)
            dv_new = _tn(mqk, dob) + dkd_pre
            dq = dqd * gam
            dgam = jnp.sum(dqd * qs, axis=1, keepdims=True)
            yield
            ds_state[h] = _tn(qd, dob) + eg_last * dsn - _tn(w, dv_new)
            dmm = jnp.where(x["incl"], _nt(dob, v_new), 0.0)
            dkd = _nt(v_new, dsn)
            dw = -_nt(dv_new, s)
            dvb = _tn(t, dv_new)
            dt = _nt(dv_new, vb)
            yield
            dqk = dmm * dm
            e_mat = dmm * mqk
            dq = dq + _nn(dqk, kx)
            dk = _tn(dqk, qs) + dkd * ek
            e1 = jnp.sum(dkd * kd, axis=1, keepdims=True)
            dgc = -e1
            dg_last = jnp.sum(e1) + eg_last * jnp.sum(s * dsn)
            dt = dt + _nt(dw, kbg)
            dkbg = _tn(t, dw)
            yield
            tdt = _tn(t, dt, hp="3x")
            yield
            da = jnp.where(x["strict"], -_nt(tdt, t, hp="3x"), 0.0)
            yield
            dkk = da * dm
            e_mat = e_mat + da * x["kk"] * dm
            dkb = _nn(dkk, kx) + dkbg * gam
            dk = dk + _tn(dkk, kb)
            dgam = dgam + jnp.sum(dkbg * kb, axis=1, keepdims=True)
            yield
            dk = dk + dkb * beta_
            dbeta = jnp.sum(dkb * kx, axis=1, keepdims=True) + jnp.sum(dvb * vx, axis=1, keepdims=True)
            dv = dvb * beta_
            dgc = dgc + jnp.sum(e_mat, axis=1, keepdims=True) + dgam * gam
            dgc = dgc - jnp.sum(e_mat.T, axis=1, keepdims=True)
            yield
            dg = _cumsum_rows(dgc, reverse=True) + dg_last
            yield
            dq_ref[sl] = dq * (1.0 / math.sqrt(DN_DIM))
            dk_ref[sl] = dk
            dv_ref[sl] = dv
            dg_ref[sl] = dg
            db_ref[sl] = jnp.broadcast_to(dbeta, (c, LANE))

        chains = []
        for sub in reversed(range(GDN_SUB_CHUNKS)):
            new = [chain(sub, h) for h in range(DN_HEADS)]
            for _ in range(3):
                for ch in new:
                    next(ch)
            chains += new
        _interleave(chains)
        for h in range(DN_HEADS):
            ds_scr[h] = ds_state[h]

        dq, dk, dv = dq_ref[...], dk_ref[...], dv_ref[...]
        outs, accs = _f_gdn_prep_bwd(
            nblk - 1 - step, x_ref[...], xp_ref[...], xn_ref[...], ab_ref[...], dq, nxt_scr[0], dk, nxt_scr[1],
            dv, nxt_scr[2], dg_ref[...], db_ref[...], w8_ref[...], alog_ref[...], dtb_ref[...], nt=nblk)
        nxt_scr[0] = dq[:8]
        nxt_scr[1] = dk[:8]
        nxt_scr[2] = dv[:8]
        dqkv_ref[...] = outs[0].astype(dqkv_ref.dtype)
        dab_ref[...] = outs[1].astype(dab_ref.dtype)
        for ref, val in zip((dcw_ref, dalog_ref, ddtb_ref), accs):
            @pl.when(step == 0)
            def _():
                ref[...] = val

            @pl.when(step > 0)
            def _():
                ref[...] += val

    sub = GDN_SUB_CHUNKS
    r8 = blk // 8
    rb = lambda n: (nblk - 1 - n, 0)
    hs = lambda n: (0, nblk - 1 - n, 0, 0)
    rows = pl.BlockSpec((blk, DN_WIDTH), rb)
    whole = lambda a: pl.BlockSpec(a.shape, lambda n: (0,) * a.ndim)
    wide = 3 * DN_WIDTH
    return pl.pallas_call(
        body, name="gdn_bwd", grid=(nblk,),
        in_specs=[rows] * 5
        + [pl.BlockSpec((DN_HEADS, sub, DN_DIM, DN_DIM), hs), pl.BlockSpec((DN_HEADS, sub, c, c), hs), rows,
           pl.BlockSpec((blk, wide), rb),
           pl.BlockSpec((8, wide), lambda n: (jnp.maximum((nblk - 1 - n) * r8 - 1, 0), 0)),
           pl.BlockSpec((8, wide), lambda n: (jnp.minimum((nblk - n) * r8, tp // 8 - 1), 0)),
           pl.BlockSpec((blk, LANE), lambda n: (nblk - 1 - n, C_AB // LANE)),
           whole(conv_w8), whole(alog), whole(dtb), _ANY_SPEC],
        out_specs=[pl.BlockSpec((blk, wide), rb), pl.BlockSpec((blk, LANE), rb),
                   whole(conv_w8), whole(alog), whole(dtb)],
        out_shape=[jax.ShapeDtypeStruct((tp, wide), _MXU), jax.ShapeDtypeStruct((tp, LANE), _MXU),
                   jax.ShapeDtypeStruct(conv_w8.shape, F32), jax.ShapeDtypeStruct(alog.shape, F32),
                   jax.ShapeDtypeStruct(dtb.shape, F32)],
        scratch_shapes=[pltpu.VMEM((DN_HEADS, DN_DIM, DN_DIM), F32)] + [pltpu.VMEM((blk, DN_WIDTH), F32)] * 5
        + [pltpu.VMEM((3, 8, DN_WIDTH), F32)],
        compiler_params=pltpu.CompilerParams(dimension_semantics=("arbitrary",), vmem_limit_bytes=VMEM_LIMIT),
    )(q, k, v, g, beta, s_all, t_all, do, proj, proj, proj, proj, conv_w8, alog, dtb, after)


def _silu_parts(x):
    s = _sigmoid(x)
    return x * s, s * (1.0 + x * (1.0 - s))


def _f_rms_bwd_add(i, x, dy, dres, w, *, mask_pad):
    dx, dwr = _rms_bwd(x, w, dy, x.shape[1])
    out = dres + dx
    if mask_pad:
        out = jnp.where(_row_ids(i, x.shape[0]) >= PAD, out, 0.0)
    return (out,), (_rowsum(dwr),)


def _rope(x, cos, sin_s):
    return x * cos + _swap_halves(x) * sin_s


def _rope_t(dy, cos, sin_s):
    return dy * cos + _swap_halves(dy * sin_s)


def _f_mla_qk(i, qf, kvf, kpe, cos, sin_s, qw, kw):
    qs, ks, vs = [], [], []
    for h in range(MLA_HEADS):
        qn, _ = _rms_fwd(qf[:, HP * h:HP * (h + 1)], qw, QK_HEAD)
        qs += [qn[:, :QK_NOPE], _rope(qn[:, QK_NOPE:], cos, sin_s)]
        kh = jnp.concatenate([kvf[:, HP * h:HP * h + QK_NOPE], kpe], axis=1)
        kn, _ = _rms_fwd(kh, kw, QK_HEAD)
        ks += [kn[:, :QK_NOPE], _rope(kn[:, QK_NOPE:], cos, sin_s)]
        vs.append(kvf[:, HP * h + QK_NOPE:HP * (h + 1)])
    return (jnp.concatenate(qs, axis=1), jnp.concatenate(ks, axis=1), jnp.concatenate(vs, axis=1)), ()


def _f_mla_front(i, ql, kvl, kpe, cos, sin_s, qaw, kvaw, wq_t, wkv, qw, kw):
    qn = _rms_fwd(ql, qaw, Q_LORA)[0].astype(_MXU)
    kvn = _rms_fwd(kvl, kvaw, KV_LORA)[0].astype(_MXU)
    qf = _nt(qn, wq_t)
    kvf = _nn(kvn, wkv)
    (q, k, v), _ = _f_mla_qk(i, qf, kvf, kpe, cos, sin_s, qw, kw)
    return (qn, kvn, qf, kvf, q, k, v), ()


def _f_mla_back(i, qf, kvf, kpe, cos, sin_s, dq, dk, dv, ql, kvl, qaw, kvaw, wq_t, wkv, qw, kw):
    (dqf, dkvf, dkpe), (dqw, dkw) = _f_mla_qk_bwd(i, qf, kvf, kpe, cos, sin_s, dq, dk, dv, qw, kw)
    dqf = dqf.astype(_MXU)
    dkvf = dkvf.astype(_MXU)
    dql, dqaw = _rms_bwd(ql, qaw, _nn(dqf, wq_t), Q_LORA)
    dkvl, dkvaw = _rms_bwd(kvl, kvaw, _nt(dkvf, wkv), KV_LORA)
    return (dqf, dkvf, dkpe, dql, dkvl), (dqw, dkw, _rowsum(dqaw), _rowsum(dkvaw))


def _f_mla_qk_bwd(i, qf, kvf, kpe, cos, sin_s, dq, dk, dv, qw, kw):
    dqf, dkvf = [], []
    dkpe = None
    dqw = None
    dkw = None
    for h in range(MLA_HEADS):
        dqh = dq[:, HP * h:HP * (h + 1)]
        dqn = jnp.concatenate([dqh[:, :QK_NOPE], _rope_t(dqh[:, QK_NOPE:], cos, sin_s)], axis=1)
        dx, dwr = _rms_bwd(qf[:, HP * h:HP * (h + 1)], qw, dqn, QK_HEAD)
        dqf.append(dx)
        dqw = _rowsum(dwr) if dqw is None else dqw + _rowsum(dwr)
        dkh = dk[:, HP * h:HP * (h + 1)]
        dkn = jnp.concatenate([dkh[:, :QK_NOPE], _rope_t(dkh[:, QK_NOPE:], cos, sin_s)], axis=1)
        kh = jnp.concatenate([kvf[:, HP * h:HP * h + QK_NOPE], kpe], axis=1)
        dx, dwr = _rms_bwd(kh, kw, dkn, QK_HEAD)
        dkvf += [dx[:, :QK_NOPE], dv[:, V_HEAD * h:V_HEAD * (h + 1)]]
        dkpe = dx[:, QK_NOPE:] if dkpe is None else dkpe + dx[:, QK_NOPE:]
        dkw = _rowsum(dwr) if dkw is None else dkw + _rowsum(dwr)
    return (jnp.concatenate(dqf, axis=1), jnp.concatenate(dkvf, axis=1), dkpe), (dqw, dkw)


def _gdn_act(i, x, halo, w8):
    halo = jnp.where(i > 0, halo, 0.0)
    c = _conv_fwd(x, halo, w8, DN_CONV)
    act, dact = _silu_parts(c)
    return act, dact


def _spread_heads(ab):
    tm = ab.shape[0]
    return jnp.concatenate([jnp.broadcast_to(ab[:, h:h + 1], (tm, DN_DIM)) for h in range(2 * DN_HEADS)], axis=1)


def _gather_heads(x):
    tm = x.shape[0]
    lane = lax.broadcasted_iota(jnp.int32, (tm, LANE), 1)
    out = jnp.zeros((tm, LANE), F32)
    for h in range(2 * DN_HEADS):
        out = out + jnp.where(lane == h, x[:, DN_DIM * h:DN_DIM * h + 1], 0.0)
    return out


def _f_gdn_prep(i, x, halo, ab, w8, alog, dtb):
    tm = x.shape[0]
    act, _ = _gdn_act(i, x, halo, w8)
    outs = []
    for part in range(2):
        for h in range(DN_HEADS):
            t = act[:, DN_WIDTH * part + DN_DIM * h:DN_WIDTH * part + DN_DIM * (h + 1)]
            outs.append(t * lax.rsqrt(jnp.sum(t * t, axis=-1, keepdims=True) + EPS))
    q = jnp.concatenate(outs[:DN_HEADS], axis=1)
    k = jnp.concatenate(outs[DN_HEADS:], axis=1)
    v = act[:, 2 * DN_WIDTH:]
    abb = _spread_heads(ab)
    valid = _row_ids(i, tm) >= PAD
    g = jnp.where(valid, -jnp.exp(alog) * _softplus(abb[:, :DN_WIDTH] + dtb), 0.0)
    beta = jnp.where(valid, _sigmoid(abb[:, DN_WIDTH:]), 0.0)
    return (q, k, v, g, beta), ()


def _f_gdn_prep_bwd(i, x, x_prev, x_next, ab, dq, dq_next, dk, dk_next, dv, dv_next, dg, dbeta,
                    w8, alog, dtb, *, nt):
    tm = x.shape[0]
    x_prev = jnp.where(i > 0, x_prev, 0.0)
    more = i < nt - 1
    ext = lambda t, t_next: jnp.concatenate([t, jnp.where(more, t_next, 0.0)], axis=0)
    c = _conv_fwd(jnp.concatenate([x, x_next], axis=0), x_prev, w8, DN_CONV)
    act, dact = _silu_parts(c)
    douts = []
    for part, dd in enumerate((ext(dq, dq_next), ext(dk, dk_next))):
        for h in range(DN_HEADS):
            t = act[:, DN_WIDTH * part + DN_DIM * h:DN_WIDTH * part + DN_DIM * (h + 1)]
            r = lax.rsqrt(jnp.sum(t * t, axis=-1, keepdims=True) + EPS)
            y = t * r
            dy = dd[:, DN_DIM * h:DN_DIM * (h + 1)]
            douts.append(r * (dy - y * jnp.sum(dy * y, axis=-1, keepdims=True)))
    douts.append(ext(dv, dv_next))
    dc = jnp.concatenate(douts, axis=1) * dact
    dqkv = _conv_bwd_x(dc[:tm], dc[tm:], w8, DN_CONV)
    dconv_w = _conv_bwd_w(dc[:tm], x, x_prev, DN_CONV)
    abb = _spread_heads(ab)
    valid = _row_ids(i, tm) >= PAD
    pre = abb[:, :DN_WIDTH] + dtb
    ea = jnp.exp(alog)
    g = -ea * _softplus(pre)
    dg = jnp.where(valid, dg, 0.0)
    dbeta = jnp.where(valid, dbeta, 0.0)
    da = dg * (-ea) * _sigmoid(pre)
    beta = _sigmoid(abb[:, DN_WIDTH:])
    db = dbeta * beta * (1.0 - beta)
    dab = _gather_heads(jnp.concatenate([da, db], axis=1))
    return (dqkv, dab), (dconv_w, _rowsum(dg * g), _rowsum(da))


def _f_mix(i, o_mla, o_dn, z, w_mla, w_dn):
    tm = o_mla.shape[0]
    valid = _row_ids(i, tm) >= PAD
    outs = []
    for h in range(MLA_HEADS):
        y, _ = _rms_fwd(o_mla[:, V_HEAD * h:V_HEAD * (h + 1)], w_mla, V_HEAD)
        outs.append(jnp.where(valid, y, 0.0))
    for h in range(DN_HEADS):
        y, _ = _rms_fwd(o_dn[:, DN_DIM * h:DN_DIM * (h + 1)], w_dn, DN_DIM)
        outs.append(y * _silu_parts(z[:, DN_DIM * h:DN_DIM * (h + 1)])[0])
    return (jnp.concatenate(outs, axis=1),), ()


def _f_mix_bwd(i, o_mla, o_dn, z, dy_mla, dy_dn, w_mla, w_dn):
    tm = o_mla.shape[0]
    valid = _row_ids(i, tm) >= PAD
    d_mla, d_dn, d_z = [], [], []
    dw_mla = None
    dw_dn = None
    for h in range(MLA_HEADS):
        sl = slice(V_HEAD * h, V_HEAD * (h + 1))
        dx, dwr = _rms_bwd(o_mla[:, sl], w_mla, jnp.where(valid, dy_mla[:, sl], 0.0), V_HEAD)
        d_mla.append(dx)
        dw_mla = _rowsum(dwr) if dw_mla is None else dw_mla + _rowsum(dwr)
    for h in range(DN_HEADS):
        sl = slice(DN_DIM * h, DN_DIM * (h + 1))
        y, _ = _rms_fwd(o_dn[:, sl], w_dn, DN_DIM)
        sz, dsz = _silu_parts(z[:, sl])
        d_z.append(dy_dn[:, sl] * y * dsz)
        dx, dwr = _rms_bwd(o_dn[:, sl], w_dn, dy_dn[:, sl] * sz, DN_DIM)
        d_dn.append(dx)
        dw_dn = _rowsum(dwr) if dw_dn is None else dw_dn + _rowsum(dwr)
    return ((jnp.concatenate(d_mla, axis=1), jnp.concatenate(d_dn, axis=1), jnp.concatenate(d_z, axis=1)),
            (dw_mla, dw_dn))


def _f_ffn_act_bwd(i, gp, gp_prev, gp_next, up, up_next, dact, dact_next, w8, b, *, nt):
    tm = gp.shape[0]
    gp_prev = jnp.where(i > 0, gp_prev, 0.0)
    dact_next = jnp.where(i < nt - 1, dact_next, 0.0)
    cat = lambda t, t_next: jnp.concatenate([t, t_next], axis=0)
    gate = _conv_fwd(cat(gp, gp_next), gp_prev, w8, FFN_CONV) + b
    sg, dsg = _silu_parts(gate)
    dact_e = cat(dact, dact_next)
    dgate = dact_e * cat(up, up_next) * dsg
    dgate_pre = _conv_bwd_x(dgate[:tm], dgate[tm:], w8, FFN_CONV)
    dup = dact * sg[:tm]
    return (dgate_pre, dup), (_conv_bwd_w(dgate[:tm], gp, gp_prev, FFN_CONV), _rowsum(dgate[:tm]))


def _f_loss(i, h3, tgt):
    tm = h3.shape[0]
    diff = jnp.where(_row_ids(i, tm) >= ROW0, h3 - tgt, 0.0)
    part = 0.5 * jnp.sum(diff * diff) * (1.0 / D_MODEL)
    return (diff * (1.0 / D_MODEL),), (jnp.full((1, LANE), part, F32),)


def _local_step(h0, tgt, w, token, late_weights, grads_ready):
    tp = h0.shape[0]
    nt = tp // TM
    proj, u = _norm_mm("in_proj", h0, w["attn_norm_w"], w["w_in"], after=token)
    p_qkv = lambda kind="cur": _In(proj, 3 * DN_WIDTH, 0, kind)
    p_z = _In(proj, DN_WIDTH, C_Z // DN_WIDTH)
    p_ql = _In(proj, Q_LORA, C_QL // Q_LORA)
    p_kvl = _In(proj, KV_LORA, C_KVL // KV_LORA)
    p_kpe = _In(proj, LANE, C_KPE // LANE)
    p_ab = _In(proj, LANE, C_AB // LANE)
    cos, sin_s = _In(w["cos"]), _In(w["sin_s"])

    mla_w = [w["q_a_norm_w"], w["kv_a_norm_w"], w["w_q_b"], w["w_kv_b"], w["q_norm_w"], w["k_norm_w"]]
    tm_mla = _pick(tp, 288, 16)
    wide = MLA_HEADS * HP
    qn, kvn, qf, kvf, q, k, v = _rows(
        "mla_front", _f_mla_front, [p_ql, p_kvl, p_kpe, cos, sin_s], mla_w,
        [(Q_LORA, _MXU), (KV_LORA, _MXU), (wide, F32), (wide, F32), (wide, _MXU), (wide, _MXU),
         (MLA_HEADS * V_HEAD, _MXU)], tm=tm_mla)
    o_mla = _attn_fwd(q, k, v)

    dn_w = [w["dn_conv_w"], w["alog_b"], w["dtb_b"]]
    o_dn, s_all, t_all, gq, gk, gv, gg, gb = _gdn_fwd(proj, *dn_w)

    out_w = [w["mla_out_norm_w"], w["dn_out_norm_w"]]
    w = dict(w, **late_weights((o_mla, o_dn), _LATE[:3]))
    h2, mixed = _pro_mm("mix_out_proj", lambda i, *t: _f_mix(i, *t)[0][0], [_In(o_mla), _In(o_dn), p_z], out_w,
                        D_MODEL, w["w_out"], h0)

    ffn_w = [w["ffn_conv_w"], w["ffn_conv_b"]]
    hn, gate_pre, up, act = _ffn_in(h2, w["ffn_norm_w"], w["w_gate"], w["w_up"], *ffn_w)
    w = dict(w, **late_weights(act, _LATE[3:]))
    dh3, loss = _mm_rows("ffn_down_loss", act, w["w_down"], "nn", lambda i, y, r, t: _f_loss(i, r + y, t),
                         [_In(h2), _In(tgt)], [], [(D_MODEL, F32)], [(1, LANE)])

    g = {}
    dact = _mm("ffn_down_dx", dh3, w["w_down"], "nt")
    g["w_down"] = _mm("ffn_down_dw", act, dh3, "tn", out_dtype=_MXU)
    dgate_pre, dup, g["ffn_conv_w"], g["ffn_conv_b"] = _rows(
        "ffn_act_bwd", functools.partial(_f_ffn_act_bwd, nt=nt),
        [_In(gate_pre), _In(gate_pre, kind="prev"), _In(gate_pre, kind="next"), _In(up), _In(up, kind="next"),
         _In(dact), _In(dact, kind="next")], ffn_w,
        [(D_FF, _MXU), (D_FF, _MXU)], [(8, D_FF), (1, D_FF)])
    g["w_gate"], g["w_up"] = _mm_tn2("ffn_gate_up_dw", dgate_pre, dup, hn, out_dtype=_MXU)
    tok = grads_ready(g, ("w_down", "w_gate", "w_up"))
    dh2, g["ffn_norm_w"] = _mm_rows(
        "ffn_gate_up_dx_rms", [dgate_pre, dup], [w["w_gate"], w["w_up"]], "nn",
        lambda i, dy, x, dres, nw, _tok: _f_rms_bwd_add(i, x, dy, dres, nw, mask_pad=True),
        [_In(h2), _In(dh3)], [w["ffn_norm_w"], tok], [(D_MODEL, F32)], [(1, D_MODEL)])

    g["w_out"] = _mm("out_proj_dw", mixed, dh2, "tn", out_dtype=_MXU)
    half = MLA_HEADS * V_HEAD
    do_mla, do_dn, dz, g["mla_out_norm_w"], g["dn_out_norm_w"] = _mm_rows(
        "out_proj_dx_mix", dh2, w["w_out"], "nt",
        lambda i, dm, om, od, z, wm, wd: _f_mix_bwd(i, om, od, z, dm[:, :half], dm[:, half:], wm, wd),
        [_In(o_mla), _In(o_dn), p_z], out_w,
        [(half, F32), (DN_WIDTH, F32), (DN_WIDTH, _MXU)], [(1, V_HEAD), (1, DN_DIM)])

    dq, dk, dv = _attn_bwd(q, k, v, do_mla)
    dqf, dkvf, dkpe, dql, dkvl, g["q_norm_w"], g["k_norm_w"], g["q_a_norm_w"], g["kv_a_norm_w"] = _rows(
        "mla_back", _f_mla_back,
        [_In(qf), _In(kvf), p_kpe, cos, sin_s, _In(dq), _In(dk), _In(dv), p_ql, p_kvl], mla_w,
        [(wide, _MXU), (wide, _MXU), (LANE, _MXU), (Q_LORA, _MXU), (KV_LORA, _MXU)],
        [(1, HP), (1, HP), (1, Q_LORA), (1, KV_LORA)], tm=tm_mla)
    g["w_q_b"] = _mm("mla_q_b_dw", dqf, qn, "tn")
    g["w_kv_b"] = _mm("mla_kv_b_dw", kvn, dkvf, "tn")
    tok = grads_ready(g, ("w_out", "w_q_b", "w_kv_b"))

    dqkv, dab, g["dn_conv_w"], g["alog_b"], g["dtb_b"] = _gdn_bwd(
        gq, gk, gv, gg, gb, s_all, t_all, do_dn, proj, *dn_w, tok)

    dproj = jnp.concatenate([dqkv, dz, dql, dkvl, dkpe, dab], axis=1)
    g["w_in"] = _mm("in_proj_dw", dproj, u, "tn", out_dtype=_MXU)
    tok = grads_ready(g, ("w_in",))
    dh0, g["attn_norm_w"] = _mm_rows(
        "in_proj_dx_rms", dproj, w["w_in"], "nn",
        lambda i, du, x, dres, nw, _tok: _f_rms_bwd_add(i, x, du, dres, nw, mask_pad=False),
        [_In(h0), _In(dh2)], [w["attn_norm_w"], tok], [(D_MODEL, F32)], [(1, D_MODEL)])
    return loss, dh0, g


def _w_in_to_padded(w):
    c1, c2, c3 = Q_LORA, Q_LORA + KV_LORA, Q_LORA + KV_LORA + QK_ROPE
    c4 = c3 + 3 * DN_WIDTH
    c5 = c4 + DN_WIDTH
    z = lambda n: jnp.zeros((n, w.shape[1]), w.dtype)
    return jnp.concatenate([w[c3:c4], w[c4:c5], w[:c1], w[c1:c2], w[c2:c3], z(LANE - QK_ROPE),
                            w[c5:], z(LANE - 2 * DN_HEADS)], axis=0)


def _w_in_from_padded(g):
    return jnp.concatenate([g[C_QL:C_QL + Q_LORA], g[C_KVL:C_KVL + KV_LORA], g[C_KPE:C_KPE + QK_ROPE],
                            g[:C_Z + DN_WIDTH], g[C_AB:C_AB + 2 * DN_HEADS]], axis=0)


def _w_q_b_to_padded(w):
    r = w.shape[1]
    w = w.reshape(MLA_HEADS, QK_HEAD, r)
    return jnp.pad(w, ((0, 0), (0, HP - QK_HEAD), (0, 0))).reshape(MLA_HEADS * HP, r)


def _w_q_b_from_padded(g):
    r = g.shape[1]
    return g.reshape(MLA_HEADS, HP, r)[:, :QK_HEAD].reshape(MLA_HEADS * QK_HEAD, r)


def _pad_rows8(w):
    return jnp.pad(w, ((0, 8 - w.shape[0]), (0, 0)))


def _prepare(full, tp):
    w = {}
    mx = lambda a: a.astype(_MXU)
    w["attn_norm_w"] = full["attn_norm_w"]
    w["w_in"] = mx(_w_in_to_padded(full["w_in"]))
    w["q_a_norm_w"] = full["q_a_norm_w"]
    w["kv_a_norm_w"] = full["kv_a_norm_w"]
    w["w_q_b"] = mx(_w_q_b_to_padded(full["w_q_b"]))
    w["w_kv_b"] = mx(full["w_kv_b"])
    w["q_norm_w"] = jnp.pad(full["q_norm_w"], ((0, 0), (0, HP - QK_HEAD)))
    w["k_norm_w"] = jnp.pad(full["k_norm_w"], ((0, 0), (0, HP - QK_HEAD)))
    w["mla_out_norm_w"] = full["mla_out_norm_w"]
    w["dn_out_norm_w"] = full["dn_out_norm_w"]
    w["dn_conv_w"] = _pad_rows8(full["dn_conv_w"])
    w["alog_b"] = jnp.repeat(full["dn_A_log"], DN_DIM, axis=1)
    w["dtb_b"] = jnp.repeat(full["dn_dt_bias"], DN_DIM, axis=1)
    w["ffn_norm_w"] = full["ffn_norm_w"]
    w["ffn_conv_w"] = _pad_rows8(full["ffn_conv_w"])
    w["ffn_conv_b"] = full["ffn_conv_b"]
    for n in _LATE:
        if n in full:
            w[n] = mx(full[n])
    half = QK_ROPE // 2
    inv = ROPE_THETA ** (-jnp.arange(half, dtype=F32) / half)
    ang = (jnp.arange(tp, dtype=jnp.int32) - PAD).astype(F32)[:, None] * inv[None, :]
    zc = jnp.zeros((tp, LANE - QK_ROPE), F32)
    w["cos"] = jnp.concatenate([jnp.cos(ang), jnp.cos(ang), zc], axis=1)
    w["sin_s"] = jnp.concatenate([-jnp.sin(ang), jnp.sin(ang), zc], axis=1)
    return w


def _grads_to_natural(g):
    convert = {
        "w_in": ("w_in", _w_in_from_padded),
        "w_q_b": ("w_q_b", _w_q_b_from_padded),
        "q_norm_w": ("q_norm_w", lambda a: a[:, :QK_HEAD]),
        "k_norm_w": ("k_norm_w", lambda a: a[:, :QK_HEAD]),
        "dn_conv_w": ("dn_conv_w", lambda a: a[:DN_CONV]),
        "ffn_conv_w": ("ffn_conv_w", lambda a: a[:FFN_CONV]),
        "alog_b": ("dn_A_log", lambda a: a[:, ::DN_DIM]),
        "dtb_b": ("dn_dt_bias", lambda a: a[:, ::DN_DIM]),
    }
    n = {}
    for key, a in g.items():
        name, fn = convert.get(key, (key, lambda t: t))
        n[name] = fn(a)
    return n


_MESH = pl.DeviceIdType.MESH
_ANY = pl.BlockSpec(memory_space=pl.ANY)
_CHIP_FLIPS = ((1, 0), (0, 1), (1, 1))


def _me():
    return lax.axis_index("x"), lax.axis_index("y"), lax.axis_index("c")


def _all_gather(name, blk):
    def body(x_ref, out_ref, send_sems, recv_sems, local_sem):
        x, y, c = _me()
        me, sib = (x, y, c), (x, y, 1 - c)
        chips = [(x ^ fx, y ^ fy) for fx, fy in _CHIP_FLIPS]

        def slot(p):
            return out_ref.at[4 * p[0] + 2 * p[1] + p[2]]

        def copy(k, block, to, src=None):
            return pltpu.make_async_remote_copy(
                src_ref=slot(block) if src is None else src, dst_ref=slot(block),
                send_sem=send_sems.at[k], recv_sem=recv_sems.at[k], device_id=to, device_id_type=_MESH)

        mine = pltpu.make_async_copy(x_ref, slot(me), local_sem)
        mine.start()
        first = [copy(0, me, sib, src=x_ref)]
        first += [copy(1 + j, me, (*chip, c), src=x_ref) for j, chip in enumerate(chips)]
        for cp in first:
            cp.start()
        passed = [copy(4 + j, (*chip, c), sib) for j, chip in enumerate(chips)]
        for j, chip in enumerate(chips):
            copy(1 + j, (*chip, c), me).wait_recv()
            passed[j].start()
        copy(0, sib, me).wait_recv()
        for j, chip in enumerate(chips):
            copy(4 + j, (*chip, 1 - c), me).wait_recv()
        for cp in first + passed:
            cp.wait_send()
        mine.wait()

    return pl.pallas_call(
        body, name=name, in_specs=[_ANY], out_specs=_ANY,
        out_shape=jax.ShapeDtypeStruct((N_DEV,) + blk.shape, blk.dtype),
        scratch_shapes=[pltpu.SemaphoreType.DMA((7,)), pltpu.SemaphoreType.DMA((7,)), pltpu.SemaphoreType.DMA],
    )(blk)


def _row_tile(r):
    divs = [d for d in range(16, min(r, 512) + 1, 16) if r % d == 0]
    return divs[-1] if divs else r


def _adam_math(g, w, m, v):
    m_new = ADAM_B1 * m + (1.0 - ADAM_B1) * g
    v_new = ADAM_B2 * v + (1.0 - ADAM_B2) * (g * g)
    m_hat = m_new / (1.0 - ADAM_B1 ** ADAM_STEP)
    v_hat = v_new / (1.0 - ADAM_B2 ** ADAM_STEP)
    return -ADAM_LR * (m_hat / (jnp.sqrt(v_hat) + ADAM_EPS) + ADAM_WD * w), m_new, v_new


def _adam_vectors(name, row, items, ws, ms, vs):
    k = len(items)

    def body(row_ref, *refs):
        w_refs, m_refs, v_refs = refs[:k], refs[k:2 * k], refs[2 * k:3 * k]
        outs = refs[3 * k:]
        for idx, (off, n, per_head) in enumerate(items):
            if per_head:
                spread = row_ref[:, off:off + DN_WIDTH]
                lane = lax.broadcasted_iota(jnp.int32, (1, LANE), 1)
                g = jnp.zeros((1, LANE), F32)
                for h in range(DN_HEADS):
                    g = g + jnp.where(lane == h, spread[:, DN_DIM * h:DN_DIM * h + 1], 0.0)
            else:
                g = row_ref[:, off:off + n]
            d, m_new, v_new = _adam_math(g, w_refs[idx][...], m_refs[idx][...], v_refs[idx][...])
            for kind, val in enumerate((g, d, m_new, v_new)):
                outs[kind * k + idx][...] = val

    shapes = [jax.ShapeDtypeStruct((1, n), F32) for _, n, _ in items]
    res = pl.pallas_call(body, name=name, out_shape=shapes * 4)(row, *ws, *ms, *vs)
    return [list(res[kind * k:(kind + 1) * k]) for kind in range(4)]


def _sum_parts(name, parts):
    _, r, cols = parts[0][0].shape
    tm = _row_tile(r)
    idx = jnp.stack([jnp.asarray(s, jnp.int32) for _, s in parts])
    n = len(parts)

    def body(idx_ref, *refs):
        g = refs[0][0].astype(F32)
        for p_ref in refs[1:n]:
            g = g + p_ref[0].astype(F32)
        refs[n][...] = g

    return pl.pallas_call(
        body, name=name,
        grid_spec=pltpu.PrefetchScalarGridSpec(
            num_scalar_prefetch=1, grid=(r // tm,),
            in_specs=[pl.BlockSpec((1, tm, cols), lambda i, idx_ref, p=p: (idx_ref[p], i, 0)) for p in range(n)],
            out_specs=pl.BlockSpec((tm, cols), lambda i, idx_ref: (i, 0))),
        out_shape=jax.ShapeDtypeStruct((r, cols), F32),
        compiler_params=pltpu.CompilerParams(dimension_semantics=("parallel",)),
    )(idx, *[a for a, _ in parts])


def _adam(name, parts, w, m, v):
    r, cols = w.shape
    tm = _row_tile(r)
    tc = cols // 4 if (r // tm < 4 and cols % (4 * LANE) == 0) else cols
    idx = jnp.stack([jnp.asarray(s, jnp.int32) for _, s in parts])
    n = len(parts)

    def body(idx_ref, *refs):
        g = refs[0][0].astype(F32)
        for p_ref in refs[1:n]:
            g = g + p_ref[0].astype(F32)
        w_ref, m_ref, v_ref, g_out, d_out, m_out, v_out = refs[n:]
        g_out[...] = g
        d_out[...], m_out[...], v_out[...] = _adam_math(g, w_ref[...], m_ref[...], v_ref[...])

    part_specs = [pl.BlockSpec((1, tm, tc), lambda i, j, idx_ref, p=p: (idx_ref[p], i, j)) for p in range(n)]
    flat = pl.BlockSpec((tm, tc), lambda i, j, idx_ref: (i, j))
    return pl.pallas_call(
        body, name=name,
        grid_spec=pltpu.PrefetchScalarGridSpec(
            num_scalar_prefetch=1, grid=(r // tm, cols // tc), in_specs=part_specs + [flat] * 3,
            out_specs=[flat] * 4),
        out_shape=[jax.ShapeDtypeStruct((r, cols), F32)] * 4,
        compiler_params=pltpu.CompilerParams(dimension_semantics=("parallel", "parallel")),
    )(idx, *[a for a, _ in parts], w, m, v)


def _all_gather_many(name, blks):
    n = len(blks)

    def body(*refs):
        x_refs, out_refs = refs[:n], refs[n:2 * n]
        send_sems, recv_sems, local_sems = refs[2 * n:]
        x, y, c = _me()
        me, sib = (x, y, c), (x, y, 1 - c)
        chips = [(x ^ fx, y ^ fy) for fx, fy in _CHIP_FLIPS]

        def slot(a, p):
            return out_refs[a].at[4 * p[0] + 2 * p[1] + p[2]]

        def copy(a, k, block, to, src=None):
            return pltpu.make_async_remote_copy(
                src_ref=slot(a, block) if src is None else src, dst_ref=slot(a, block),
                send_sem=send_sems.at[7 * a + k], recv_sem=recv_sems.at[7 * a + k], device_id=to,
                device_id_type=_MESH)

        mine = [pltpu.make_async_copy(x_refs[a], slot(a, me), local_sems.at[a]) for a in range(n)]
        first = []
        for a in range(n):
            mine[a].start()
            first.append(copy(a, 0, me, sib, src=x_refs[a]))
            first += [copy(a, 1 + j, me, (*chip, c), src=x_refs[a]) for j, chip in enumerate(chips)]
        for cp in first:
            cp.start()
        passed = []
        for j, chip in enumerate(chips):
            for a in range(n):
                copy(a, 1 + j, (*chip, c), me).wait_recv()
                cp = copy(a, 4 + j, (*chip, c), sib)
                cp.start()
                passed.append(cp)
        for a in range(n):
            copy(a, 0, sib, me).wait_recv()
            for j, chip in enumerate(chips):
                copy(a, 4 + j, (*chip, 1 - c), me).wait_recv()
        for cp in first + passed:
            cp.wait_send()
        for cp in mine:
            cp.wait()

    return pl.pallas_call(
        body, name=name, in_specs=[_ANY] * n, out_specs=[_ANY] * n,
        out_shape=[jax.ShapeDtypeStruct((N_DEV,) + b.shape, b.dtype) for b in blks],
        scratch_shapes=[pltpu.SemaphoreType.DMA((7 * n,)), pltpu.SemaphoreType.DMA((7 * n,)),
                        pltpu.SemaphoreType.DMA((n,))],
    )(*blks)


_HBM = pl.BlockSpec(memory_space=pltpu.HBM)
_SEM = pl.BlockSpec(memory_space=pltpu.SEMAPHORE)
_EFFECT = pltpu.SideEffectType.DATAFLOW_SIDE_EFFECTING


def _push_copies(src_refs, land_refs, send_sems, recv_sems, src_by_peer, first=0):
    x, y, c = _me()
    my_id = 4 * x + 2 * y + c
    out = []
    for k in range(len(src_refs)):
        a = first + k
        for f in range(1, N_DEV):
            px, py, pc = x ^ (f >> 2), y ^ ((f >> 1) & 1), c ^ (f & 1)
            pid = 4 * px + 2 * py + pc
            src = src_refs[k].at[pid] if src_by_peer else src_refs[k]
            start = pltpu.make_async_remote_copy(
                src_ref=src, dst_ref=land_refs[k].at[my_id], send_sem=send_sems.at[7 * a + f - 1],
                recv_sem=recv_sems.at[7 * a + f - 1], device_id=(px, py, pc), device_id_type=_MESH)
            landed = pltpu.make_async_remote_copy(
                src_ref=src, dst_ref=land_refs[k].at[pid], send_sem=send_sems.at[7 * a + f - 1],
                recv_sem=recv_sems.at[7 * a + f - 1], device_id=(px, py, pc), device_id_type=_MESH)
            out.append((start, landed))
    return out


def _push_start(name, srcs, src_by_peer, after):
    n = len(srcs)
    lands = [jax.ShapeDtypeStruct((N_DEV,) + (s.shape[1:] if src_by_peer else s.shape), s.dtype) for s in srcs]

    def body(*refs):
        src_refs, land_refs = refs[:n], refs[n:2 * n]
        send_sems, recv_sems = refs[2 * n + 1], refs[2 * n + 2]
        token = refs[-1]
        for start, _ in _push_copies(src_refs, land_refs, send_sems, recv_sems, src_by_peer):
            start.start()
        token[...] = jnp.zeros_like(token)

    hbm = lambda a: pltpu.with_memory_space_constraint(a, pltpu.HBM)
    res = pl.pallas_call(
        body, name=name,
        out_shape=(pltpu.SemaphoreType.DMA((7 * n,)), pltpu.SemaphoreType.DMA((7 * n,)),
                   *[pltpu.HBM(s.shape, s.dtype) for s in srcs], *[pltpu.HBM(s.shape, s.dtype) for s in lands],
                   jax.ShapeDtypeStruct((8, LANE), F32)),
        in_specs=[_HBM] * (2 * n) + [_ANY],
        out_specs=(_SEM, _SEM, *[_HBM] * (2 * n), pl.BlockSpec(memory_space=pltpu.VMEM)),
        input_output_aliases={i: 2 + i for i in range(2 * n)},
        compiler_params=pltpu.CompilerParams(has_side_effects=_EFFECT),
    )(*[hbm(s) for s in srcs], *[hbm(lax.empty(s.shape, s.dtype)) for s in lands], after)
    return res[0], res[1], list(res[2:2 + n]), list(res[2 + n:2 + 2 * n]), res[-1]


def _push_wait(name, send_sems, recv_sems, srcs, lands, src_by_peer, after, first=0):
    n = len(srcs)
    after = list(after) if isinstance(after, (list, tuple)) else [after]

    def body(*refs):
        src_refs, land_refs = refs[:n], refs[n:2 * n]
        s_sems, r_sems = refs[2 * n], refs[2 * n + 1]
        for _, landed in _push_copies(src_refs, land_refs, s_sems, r_sems, src_by_peer, first):
            landed.wait_send()
            landed.wait_recv()

    res = pl.pallas_call(
        body, name=name,
        out_shape=tuple(pltpu.HBM(s.shape, s.dtype) for s in list(srcs) + list(lands)),
        in_specs=[_HBM] * (2 * n) + [_SEM, _SEM] + [_ANY] * len(after),
        out_specs=tuple([_HBM] * (2 * n)),
        input_output_aliases={i: i for i in range(2 * n)},
        compiler_params=pltpu.CompilerParams(has_side_effects=_EFFECT),
    )(*srcs, *lands, send_sems, recv_sems, *after)
    return list(res[:n]), list(res[n:])


_SHARDED = (
    ("meta_tokens", 1, (N_META, D_MODEL)),
    ("w_in", 1, (D_MODEL, IN_COLS)),
    ("w_q_b", 1, (Q_LORA, MLA_HEADS * QK_HEAD)),
    ("w_kv_b", 1, (KV_LORA, MLA_HEADS * (QK_NOPE + V_HEAD))),
    ("dn_conv_w", 1, (DN_CONV, 3 * DN_WIDTH)),
    ("w_out", 0, (2 * DN_WIDTH, D_MODEL)),
    ("w_gate", 1, (D_MODEL, D_FF)),
    ("w_up", 1, (D_MODEL, D_FF)),
    ("ffn_conv_w", 1, (FFN_CONV, D_FF)),
    ("w_down", 0, (D_FF, D_MODEL)),
)
_F32_GATHERED = ("meta_tokens", "dn_conv_w", "ffn_conv_w")
_EARLY = ("w_in", "w_q_b", "w_kv_b")
_LATE = ("w_out", "w_gate", "w_up", "w_down")
_TRANSPOSED = ("w_in", "w_q_b", "w_gate", "w_up")
_REPLICATED = (
    ("attn_norm_w", D_MODEL), ("q_a_norm_w", Q_LORA), ("kv_a_norm_w", KV_LORA), ("q_norm_w", QK_HEAD),
    ("k_norm_w", QK_HEAD), ("mla_out_norm_w", V_HEAD), ("dn_A_log", DN_HEADS), ("dn_dt_bias", DN_HEADS),
    ("dn_out_norm_w", DN_DIM), ("ffn_norm_w", D_MODEL), ("ffn_conv_b", D_FF),
)
_SMALL_BLOCK = (8, 512)


def _local_shape(dim, shape):
    return (shape[0] // N_DEV, shape[1]) if dim == 0 else (shape[0], shape[1] // N_DEV)


def _from_blocks(blocks, dim, shape):
    r, c = shape
    if dim == 0:
        return blocks.reshape(r, c)
    return blocks.reshape(N_DEV, r, c // N_DEV).transpose(1, 0, 2).reshape(r, c)


def _split(flat, sizes):
    out, o = [], 0
    for s in sizes:
        out.append(flat[..., o:o + s])
        o += s
    return out


def kernel(x, meta_tokens, attn_norm_w, w_in, q_a_norm_w, w_q_b, kv_a_norm_w, w_kv_b, q_norm_w, k_norm_w, mla_out_norm_w, dn_conv_w, dn_A_log, dn_dt_bias, dn_out_norm_w, w_out, ffn_norm_w, w_gate, w_up, ffn_conv_w, ffn_conv_b, w_down, loss_target, m_meta_tokens, m_attn_norm_w, m_w_in, m_q_a_norm_w, m_w_q_b, m_kv_a_norm_w, m_w_kv_b, m_q_norm_w, m_k_norm_w, m_mla_out_norm_w, m_dn_conv_w, m_dn_A_log, m_dn_dt_bias, m_dn_out_norm_w, m_w_out, m_ffn_norm_w, m_w_gate, m_w_up, m_ffn_conv_w, m_ffn_conv_b, m_w_down, v_meta_tokens, v_attn_norm_w, v_w_in, v_q_a_norm_w, v_w_q_b, v_kv_a_norm_w, v_w_kv_b, v_q_norm_w, v_k_norm_w, v_mla_out_norm_w, v_dn_conv_w, v_dn_A_log, v_dn_dt_bias, v_dn_out_norm_w, v_w_out, v_ffn_norm_w, v_w_gate, v_w_up, v_ffn_conv_w, v_ffn_conv_b, v_w_down):
    names = [n for n, _, _ in _SHARDED] + [n for n, _ in _REPLICATED]
    given = dict(locals())
    two_d = lambda a: a.reshape(a.shape[-2:])
    view = lambda a, n: two_d(a).T if n in _TRANSPOSED else two_d(a)
    wl = {n: view(given[n], n) for n in names}
    ml = {n: view(given["m_" + n], n) for n in names}
    vl = {n: view(given["v_" + n], n) for n in names}
    out_shapes = {n: given[n].shape for n in names}

    spec = {n: (d, s) for n, d, s in _SHARDED}
    small_sizes = [math.prod(_local_shape(*spec[n])) for n in _F32_GATHERED]

    def small_block(d):
        cat = jnp.concatenate([d[n].reshape(d[n].shape[:-2] + (-1,)) for n in _F32_GATHERED], axis=-1)
        pad = [(0, 0)] * (cat.ndim - 1) + [(0, math.prod(_SMALL_BLOCK) - cat.shape[-1])]
        return jnp.pad(cat, pad).reshape(cat.shape[:-1] + _SMALL_BLOCK)

    def shard(n):
        return wl[n].astype(_MXU)

    def from_slots(n, blocks):
        d, s = spec[n]
        if d == 0 or n in _TRANSPOSED:
            return blocks.reshape(-1, blocks.shape[-1])
        return blocks.transpose(1, 0, 2).reshape(s)

    my_id = 4 * lax.axis_index("x") + 2 * lax.axis_index("y") + lax.axis_index("c")
    got = _all_gather_many("gather_early", [shard(n) for n in _EARLY] + [small_block(wl)])
    full = {n: a for n, a in wl.items() if n not in _LATE}
    for n, blocks in zip(_EARLY, got):
        full[n] = from_slots(n, blocks)
    for n, p in zip(_F32_GATHERED, _split(got[-1].reshape(N_DEV, -1), small_sizes)):
        full[n] = _from_blocks(p, *spec[n])
    late_own = [shard(n) for n in _LATE]
    l_send, l_recv, l_src, l_land, token = _push_start("gather_late_start", late_own, False, got[-1])

    def late_weights(after, names):
        first = _LATE.index(names[0])
        sl = slice(first, first + len(names))
        _, lands = _push_wait("gather_late_wait_" + names[0], l_send, l_recv, l_src[sl], l_land[sl], False,
                              after, first)
        out = {}
        for n, land, own in zip(names, lands, late_own[sl]):
            out[n] = from_slots(n, lax.dynamic_update_slice(land, own[None], (my_id, 0, 0))).astype(_MXU)
        return out

    def dest_blocks(n, a):
        d, s = spec[n]
        r, c = _local_shape(d, s)
        if n in _TRANSPOSED:
            return a.reshape(N_DEV, c, r)
        return a.reshape(N_DEV, r, c) if d == 0 else a.reshape(r, N_DEV, c).transpose(1, 0, 2)

    pushed = []

    def grads_ready(g, names):
        nat = _grads_to_natural({n: g[n] for n in names})
        blocks = [dest_blocks(n, nat[n]).astype(_MXU) for n in names]
        sends, recvs, srcs, lands, tok = _push_start("rs_" + names[0] + "_start", blocks, True, token)
        pushed.append((names, sends, recvs, srcs, lands))
        return tok

    seq = x.shape[1]
    tp = ROW0 + seq
    h0 = jnp.concatenate([jnp.zeros((PAD, D_MODEL), F32), full["meta_tokens"], x[0]], axis=0)
    tgt = jnp.concatenate([jnp.zeros((ROW0, D_MODEL), F32), loss_target[0]], axis=0)
    loss, dh0, raw = _local_step(h0, tgt, _prepare(full, tp), token, late_weights, grads_ready)
    g = _grads_to_natural(raw)
    g["meta_tokens"] = dh0[PAD:ROW0]
    grad_x = dh0[ROW0:][None]

    big = [{}, {}, {}, {}]
    rep_names = [n for n, _ in _REPLICATED]
    raw_key = {"dn_A_log": "alog_b", "dn_dt_bias": "dtb_b"}
    pieces = [raw[raw_key.get(n, n)] for n in rep_names] + [loss]
    pieces += [g[n].reshape(1, -1) for n in _F32_GATHERED]
    widths = [p.shape[1] for p in pieces]
    offs = [sum(widths[:k]) for k in range(len(widths))]
    cat = jnp.concatenate(pieces, axis=1)
    cols = -(-cat.shape[1] // (8 * LANE)) * LANE
    mine = jnp.pad(cat, ((0, 0), (0, 8 * cols - cat.shape[1]))).reshape(8, cols)
    everyone = _all_gather("gather_small_grads", mine)
    total = _sum_parts("sum_small_grads", [(everyone, d) for d in range(N_DEV)]).reshape(1, 8 * cols)
    tot = {n: total[0, o:o + wd] for n, o, wd in zip(rep_names + ["loss"] + list(_F32_GATHERED), offs, widths)}
    lanes = lambda a: jnp.pad(a, ((0, 0), (0, -a.shape[1] % LANE)))
    items = [(o, -(-size // LANE) * LANE, n in raw_key) for (n, size), o in zip(_REPLICATED, offs)]
    sm = _adam_vectors("adam_replicated", total, items, [lanes(wl[n]) for n in rep_names],
                       [lanes(ml[n]) for n in rep_names], [lanes(vl[n]) for n in rep_names])
    sm = [{n: a[:, :size] for (n, size), a in zip(_REPLICATED, kind)} for kind in sm]
    mine_of = {}
    for n in _F32_GATHERED:
        d, s = spec[n]
        r, c = _local_shape(d, s)
        mine_of[n] = lax.dynamic_slice(tot[n].reshape(s), (0, my_id * c), (r, c))
    res = _adam("adam_small_sharded", [(small_block(mine_of)[None], 0)], small_block(wl), small_block(ml),
                small_block(vl))
    for kind, a in enumerate(res):
        big[kind].update(zip(_F32_GATHERED, _split(a.reshape(-1), small_sizes)))

    for names, sends, recvs, srcs, lands in pushed:
        srcs, lands = _push_wait("rs_" + names[0] + "_wait", sends, recvs, srcs, lands, True, dh0)
        for n, src, land in zip(names, srcs, lands):
            parts = [(src, my_id)] + [(land, my_id ^ f) for f in range(1, N_DEV)]
            for kind, a in enumerate(_adam("adam_" + n, parts, wl[n], ml[n], vl[n])):
                big[kind][n] = a

    outs = [tot["loss"][0], grad_x]
    for kind in range(4):
        for n in ("meta_tokens", "attn_norm_w", "w_in", "q_a_norm_w", "w_q_b", "kv_a_norm_w", "w_kv_b", "q_norm_w",
                  "k_norm_w", "mla_out_norm_w", "dn_conv_w", "dn_A_log", "dn_dt_bias", "dn_out_norm_w", "w_out",
                  "ffn_norm_w", "w_gate", "w_up", "ffn_conv_w", "ffn_conv_b", "w_down"):
            src = big[kind] if n in big[kind] else sm[kind]
            a = src[n].T if n in _TRANSPOSED else src[n]
            outs.append(a.reshape(out_shapes[n]))
    return tuple(outs)
```

```python
import functools
import math

import jax
import jax.numpy as jnp
from jax import lax
from jax.experimental import pallas as pl
from jax.experimental.pallas import tpu as pltpu

F32 = jnp.float32
_MXU = jnp.bfloat16
_HI = lax.Precision.HIGHEST

D_MODEL = 1024
N_META = 16
PAD = 112
ROW0 = PAD + N_META
MLA_HEADS = 4
QK_NOPE = 128
QK_ROPE = 64
QK_HEAD = QK_NOPE + QK_ROPE
V_HEAD = 128
Q_LORA = 256
KV_LORA = 256
ROPE_THETA = 10000.0
DN_HEADS = 4
DN_DIM = 128
DN_WIDTH = DN_HEADS * DN_DIM
DN_CONV = 4
DN_CHUNK = 64
GDN_SUB_CHUNKS = 2
D_FF = 2816
FFN_CONV = 3
EPS = 1e-6
HP = 256
C_Z = 1536
C_QL = 2048
C_KVL = 2304
C_KPE = 2560
C_AB = 2688
IN_COLS = 2632

ADAM_LR = 0.001
ADAM_B1 = 0.9
ADAM_B2 = 0.999
ADAM_EPS = 1e-08
ADAM_WD = 0.01
ADAM_STEP = 10

N_DEV = 8
TM = 128
LANE = 128
VMEM_LIMIT = 56 * 1024 * 1024
NEG = -1e30


def _dot(a, b, dims, hp=False):
    if hp:
        return lax.dot_general(a.astype(F32), b.astype(F32), (dims, ((), ())),
                               precision=lax.Precision.HIGH if hp == "3x" else _HI, preferred_element_type=F32)
    return lax.dot_general(a.astype(_MXU), b.astype(_MXU), (dims, ((), ())),
                           preferred_element_type=F32)


def _nn(a, b, hp=False):
    return _dot(a, b, ((1,), (0,)), hp)


def _nt(a, b, hp=False):
    return _dot(a, b, ((1,), (1,)), hp)


def _tn(a, b, hp=False):
    return _dot(a, b, ((0,), (0,)), hp)


def _sigmoid(x):
    return 1.0 / (1.0 + jnp.exp(-x))


def _rms_fwd(x, w, n):
    r = lax.rsqrt(jnp.sum(x * x, axis=-1, keepdims=True) * (1.0 / n) + EPS)
    return x * r * w, r


def _rms_bwd(x, w, dy, n):
    r = lax.rsqrt(jnp.sum(x * x, axis=-1, keepdims=True) * (1.0 / n) + EPS)
    xh = x * r
    gy = dy * w
    dx = r * (gy - xh * (jnp.sum(gy * xh, axis=-1, keepdims=True) * (1.0 / n)))
    return dx, dy * xh


def _rowsum(x):
    return jnp.sum(x, axis=0, keepdims=True)


def _row_ids(i, tm):
    return i * tm + lax.broadcasted_iota(jnp.int32, (tm, 1), 0)


def _shift_down(ext, s, tm):
    if s == 0:
        return ext[8:8 + tm]
    return pltpu.roll(ext, s, 0)[8:8 + tm]


def _shift_up(ext, s, tm):
    if s == 0:
        return ext[0:tm]
    return pltpu.roll(ext, tm + 8 - s, 0)[0:tm]


def _conv_fwd(x, halo_prev, w, width):
    tm = x.shape[0]
    ext = jnp.concatenate([halo_prev, x], axis=0)
    y = None
    for j in range(width):
        t = w[j:j + 1, :] * _shift_down(ext, width - 1 - j, tm)
        y = t if y is None else y + t
    return y


def _conv_bwd_x(dy, halo_next, w, width):
    tm = dy.shape[0]
    ext = jnp.concatenate([dy, halo_next], axis=0)
    dx = None
    for j in range(width):
        t = w[j:j + 1, :] * _shift_up(ext, width - 1 - j, tm)
        dx = t if dx is None else dx + t
    return dx


def _conv_bwd_w(dy, x, halo_prev, width):
    tm = dy.shape[0]
    ext = jnp.concatenate([halo_prev, x], axis=0)
    rows = [_rowsum(dy * _shift_down(ext, width - 1 - j, tm)) for j in range(width)]
    rows += [jnp.zeros_like(rows[0])] * (8 - width)
    return jnp.concatenate(rows, axis=0)


def _softplus(x):
    e = jnp.exp(-jnp.abs(x))
    u = 1.0 + e
    l1p = jnp.where(u == 1.0, e, jnp.log(u) * e / jnp.where(u == 1.0, 1.0, u - 1.0))
    return jnp.maximum(x, 0.0) + l1p


def _swap_halves(x):
    lane = lax.broadcasted_iota(jnp.int32, x.shape, 1)
    return jnp.where(lane < 32, pltpu.roll(x, 96, 1), jnp.where(lane < 64, pltpu.roll(x, 32, 1), 0.0))


class _In:
    def __init__(self, arr, width=None, cb=0, kind="cur"):
        self.arr, self.kind = arr, kind
        self.width = arr.shape[1] if width is None else width
        self.cb = cb


def _whole_spec(x):
    return pl.BlockSpec(x.shape, lambda i, nd=x.ndim: (0,) * nd, pipeline_mode=pl.Buffered(1))


def _tile_spec(t, tm, tp):
    r8 = tm // 8
    if t.kind == "cur":
        return pl.BlockSpec((tm, t.width), lambda i, cb=t.cb: (i, cb))
    if t.kind == "prev":
        return pl.BlockSpec((8, t.width), lambda i, cb=t.cb: (jnp.maximum(i * r8 - 1, 0), cb))
    return pl.BlockSpec((8, t.width), lambda i, cb=t.cb: (jnp.minimum((i + 1) * r8, tp // 8 - 1), cb))


def _rows(name, fn, tiled, full, outs, accs=(), tm=TM):
    tp = tiled[0].arr.shape[0]
    nt = tp // tm
    n_in = len(tiled) + len(full)
    n_out = len(outs)

    def body(*refs):
        i = pl.program_id(0)
        vals = [r[...] for r in refs[:n_in]]
        o_t, o_a = fn(i, *vals)
        for r, v in zip(refs[n_in:n_in + n_out], o_t):
            r[...] = v.astype(r.dtype)
        for r, v in zip(refs[n_in + n_out:], o_a):
            @pl.when(i == 0)
            def _():
                r[...] = v

            @pl.when(i > 0)
            def _():
                r[...] += v

    in_specs = [_tile_spec(t, tm, tp) for t in tiled]
    in_specs += [pl.BlockSpec(a.shape, lambda i, nd=a.ndim: (0,) * nd) for a in full]
    out_specs = [pl.BlockSpec((tm, w), lambda i: (i, 0)) for w, _ in outs]
    out_specs += [pl.BlockSpec((r, w), lambda i: (0, 0)) for r, w in accs]
    out_shape = [jax.ShapeDtypeStruct((tp, w), dt) for w, dt in outs]
    out_shape += [jax.ShapeDtypeStruct((r, w), F32) for r, w in accs]
    res = pl.pallas_call(
        body, name=name, grid=(nt,), in_specs=in_specs, out_specs=out_specs, out_shape=out_shape,
        compiler_params=pltpu.CompilerParams(dimension_semantics=("arbitrary",), vmem_limit_bytes=VMEM_LIMIT),
    )(*[t.arr for t in tiled], *full)
    return res


def _pick(n, cap, mult):
    best = None
    for d in range(mult, min(n, cap) + 1, mult):
        if n % d == 0:
            best = d
    assert best is not None, (n, cap, mult)
    return best


_ANY_SPEC = pl.BlockSpec(memory_space=pl.ANY)


def _mm(name, a, b, mode, out_dtype=F32, resid=None, after=None):
    if mode == "tn":
        m, k = a.shape
        n = b.shape[1]
        tk = _pick(k, 512, 128)
        tn = _pick(n, 1408, 128)

        def body_tn(a_ref, b_ref, o_ref):
            o_ref[...] = _tn(a_ref[...], b_ref[...]).astype(o_ref.dtype)

        return pl.pallas_call(
            body_tn, name=name, grid=(n // tn, k // tk),
            in_specs=[pl.BlockSpec((m, tk), lambda j, p: (0, p)),
                      pl.BlockSpec((m, tn), lambda j, p: (0, j))],
            out_specs=pl.BlockSpec((tk, tn), lambda j, p: (p, j)),
            out_shape=jax.ShapeDtypeStruct((k, n), out_dtype),
            compiler_params=pltpu.CompilerParams(
                dimension_semantics=("parallel", "parallel"), vmem_limit_bytes=VMEM_LIMIT),
        )(a, b)

    m, k = a.shape
    n = b.shape[1] if mode == "nn" else b.shape[0]
    tn = _pick(n, 1408, 128)
    tm = _pick(m, 1152, 16)
    dotf = _nn if mode == "nn" else _nt

    def body(*refs):
        a_ref, b_ref, o_ref = refs[0], refs[1], refs[-1]
        acc = dotf(a_ref[...], b_ref[...])
        if resid is not None:
            acc = refs[2][...] + acc
        o_ref[...] = acc.astype(o_ref.dtype)

    b_spec = (pl.BlockSpec((k, tn), lambda j, i: (0, j)) if mode == "nn"
              else pl.BlockSpec((tn, k), lambda j, i: (j, 0)))
    in_specs = [pl.BlockSpec((tm, k), lambda j, i: (i, 0)), b_spec]
    args = [a, b]
    if resid is not None:
        in_specs.append(pl.BlockSpec((tm, tn), lambda j, i: (i, j)))
        args.append(resid)
    if after is not None:
        in_specs.append(_ANY_SPEC)
        args.append(after)
    return pl.pallas_call(
        body, name=name, grid=(n // tn, m // tm), in_specs=in_specs,
        out_specs=pl.BlockSpec((tm, tn), lambda j, i: (i, j)),
        out_shape=jax.ShapeDtypeStruct((m, n), out_dtype),
        compiler_params=pltpu.CompilerParams(
            dimension_semantics=("parallel", "parallel"), vmem_limit_bytes=VMEM_LIMIT),
    )(*args)


def _mm_tn2(name, a1, a2, b, out_dtype=F32):
    m, k = a1.shape
    n = b.shape[1]
    tk = _pick(k, 512, 128)

    def body(a1_ref, a2_ref, b_ref, o1_ref, o2_ref):
        bb = b_ref[...]
        o1_ref[...] = _tn(a1_ref[...], bb).astype(o1_ref.dtype)
        o2_ref[...] = _tn(a2_ref[...], bb).astype(o2_ref.dtype)

    a_spec = pl.BlockSpec((m, tk), lambda p: (0, p))
    o_spec = pl.BlockSpec((tk, n), lambda p: (p, 0))
    return pl.pallas_call(
        body, name=name, grid=(k // tk,),
        in_specs=[a_spec, a_spec, pl.BlockSpec((m, n), lambda p: (0, 0))],
        out_specs=[o_spec, o_spec], out_shape=[jax.ShapeDtypeStruct((k, n), out_dtype)] * 2,
        compiler_params=pltpu.CompilerParams(dimension_semantics=("parallel",), vmem_limit_bytes=VMEM_LIMIT),
    )(a1, a2, b)


def _norm_mm(name, x, norm_w, b, mode="nt", x_cb=0, after=None):
    m = x.shape[0]
    k = norm_w.shape[1]
    n = b.shape[0] if mode == "nt" else b.shape[1]
    tn = _pick(n, 1408, 128)
    tm = _pick(m, 1152, 16)
    dotf = _nt if mode == "nt" else _nn
    extra = [] if after is None else [after]

    def body(x_ref, w_ref, b_ref, *rest):
        o_ref, u_ref = rest[-2:]

        @pl.when(pl.program_id(1) == 0)
        def _():
            u_ref[...] = _rms_fwd(x_ref[...], w_ref[...], k)[0].astype(u_ref.dtype)

        o_ref[...] = dotf(u_ref[...], b_ref[...])

    b_spec = (pl.BlockSpec((tn, k), lambda i, j: (j, 0)) if mode == "nt"
              else pl.BlockSpec((k, tn), lambda i, j: (0, j)))
    return pl.pallas_call(
        body, name=name, grid=(m // tm, n // tn),
        in_specs=[pl.BlockSpec((tm, k), lambda i, j: (i, x_cb)), pl.BlockSpec((1, k), lambda i, j: (0, 0)),
                  b_spec] + [_ANY_SPEC] * len(extra),
        out_specs=[pl.BlockSpec((tm, tn), lambda i, j: (i, j)), pl.BlockSpec((tm, k), lambda i, j: (i, 0))],
        out_shape=[jax.ShapeDtypeStruct((m, n), F32), jax.ShapeDtypeStruct((m, k), _MXU)],
        compiler_params=pltpu.CompilerParams(
            dimension_semantics=("arbitrary", "arbitrary"), vmem_limit_bytes=VMEM_LIMIT),
    )(x, norm_w, b, *extra)


def _pro_mm(name, fn, tiled, full, k, b, resid):
    m = resid.shape[0]
    n = b.shape[1]
    tm = _pick(m, 576, 16)
    n_in = len(tiled) + len(full)

    def body(*refs):
        i = pl.program_id(0)
        u = fn(i, *[r[...] for r in refs[:n_in]]).astype(_MXU)
        b_ref, r_ref, o_ref, u_ref = refs[n_in:]
        u_ref[...] = u
        o_ref[...] = r_ref[...] + _nn(u, b_ref[...])

    row = lambda w: pl.BlockSpec((tm, w), lambda i: (i, 0))
    in_specs = [_tile_spec(t, tm, m) for t in tiled]
    in_specs += [_whole_spec(x) for x in full] + [_whole_spec(b), row(n)]
    return pl.pallas_call(
        body, name=name, grid=(m // tm,), in_specs=in_specs, out_specs=[row(n), row(k)],
        out_shape=[jax.ShapeDtypeStruct((m, n), F32), jax.ShapeDtypeStruct((m, k), _MXU)],
        compiler_params=pltpu.CompilerParams(dimension_semantics=("parallel",), vmem_limit_bytes=VMEM_LIMIT),
    )(*[t.arr for t in tiled], *full, b, resid)


def _ffn_in(h2, norm_w, w_gate_t, w_up_t, conv_w8, conv_b):
    m, k = h2.shape
    n = w_gate_t.shape[0]
    tm = _pick(m, 288, 16)

    def body(x_ref, xp_ref, nw_ref, wg_ref, wu_ref, cw_ref, cb_ref, hn_ref, gp_ref, up_ref, act_ref):
        i = pl.program_id(0)
        nw = nw_ref[...]
        hn = _rms_fwd(x_ref[...], nw, k)[0].astype(_MXU)
        hn_prev = _rms_fwd(xp_ref[...], nw, k)[0].astype(_MXU)
        wg = wg_ref[...]
        gp = _nt(hn, wg)
        gp_prev = jnp.where(i > 0, _nt(hn_prev, wg), 0.0)
        up = _nt(hn, wu_ref[...])
        gate = _conv_fwd(gp, gp_prev, cw_ref[...], FFN_CONV) + cb_ref[...]
        hn_ref[...] = hn
        gp_ref[...] = gp
        up_ref[...] = up
        act_ref[...] = (_silu_parts(gate)[0] * up).astype(act_ref.dtype)

    row = lambda w: pl.BlockSpec((tm, w), lambda i: (i, 0))
    r8 = tm // 8
    return pl.pallas_call(
        body, name="ffn_in", grid=(m // tm,),
        in_specs=[row(k), pl.BlockSpec((8, k), lambda i: (jnp.maximum(i * r8 - 1, 0), 0)), _whole_spec(norm_w),
                  _whole_spec(w_gate_t), _whole_spec(w_up_t), _whole_spec(conv_w8), _whole_spec(conv_b)],
        out_specs=[row(k), row(n), row(n), row(n)],
        out_shape=[jax.ShapeDtypeStruct((m, k), _MXU), jax.ShapeDtypeStruct((m, n), F32),
                   jax.ShapeDtypeStruct((m, n), F32), jax.ShapeDtypeStruct((m, n), _MXU)],
        compiler_params=pltpu.CompilerParams(dimension_semantics=("parallel",), vmem_limit_bytes=VMEM_LIMIT),
    )(h2, h2, norm_w, w_gate_t, w_up_t, conv_w8, conv_b)


def _mm_rows(name, a, b, mode, fn, tiled, full, outs, accs=(), tm_cap=576):
    a_list = list(a) if isinstance(a, (list, tuple)) else [a]
    b_list = list(b) if isinstance(b, (list, tuple)) else [b]
    na = len(a_list)
    m = a_list[0].shape[0]
    tm = _pick(m, tm_cap, 16)
    dotf = _nn if mode == "nn" else _nt
    n_in = len(tiled) + len(full)
    n_out = len(outs)
    first = 2 * na

    def body(*refs):
        i = pl.program_id(0)
        vals = [r[...] for r in refs[first:first + n_in]]
        acc = dotf(refs[0][...], refs[na][...])
        for p in range(1, na):
            acc = acc + dotf(refs[p][...], refs[na + p][...])
        o_t, o_a = fn(i, acc, *vals)
        for r, v in zip(refs[first + n_in:first + n_in + n_out], o_t):
            r[...] = v.astype(r.dtype)
        for r, v in zip(refs[first + n_in + n_out:], o_a):
            @pl.when(i == 0)
            def _():
                r[...] = v

            @pl.when(i > 0)
            def _():
                r[...] += v

    whole = lambda x: pl.BlockSpec(x.shape, lambda i, nd=x.ndim: (0,) * nd)
    in_specs = [pl.BlockSpec((tm, x.shape[1]), lambda i: (i, 0)) for x in a_list] + [_whole_spec(x) for x in b_list]
    in_specs += [_tile_spec(t, tm, m) for t in tiled]
    in_specs += [whole(x) for x in full]
    out_specs = [pl.BlockSpec((tm, w), lambda i: (i, 0)) for w, _ in outs]
    out_specs += [pl.BlockSpec((r, w), lambda i: (0, 0)) for r, w in accs]
    out_shape = [jax.ShapeDtypeStruct((m, w), dt) for w, dt in outs]
    out_shape += [jax.ShapeDtypeStruct((r, w), F32) for r, w in accs]
    return pl.pallas_call(
        body, name=name, grid=(m // tm,), in_specs=in_specs, out_specs=out_specs, out_shape=out_shape,
        compiler_params=pltpu.CompilerParams(dimension_semantics=("arbitrary",), vmem_limit_bytes=VMEM_LIMIT),
    )(*a_list, *b_list, *[t.arr for t in tiled], *full)


ATTN_Q_TILES = 4


def _attn_probs(q, k, row0):
    tq, tp = q.shape[0], k.shape[0]
    s = _nt(q, k) * (1.0 / math.sqrt(QK_HEAD))
    row = row0 + lax.broadcasted_iota(jnp.int32, (tq, tp), 0)
    col = lax.broadcasted_iota(jnp.int32, (tq, tp), 1)
    ok = (col <= row) & (col >= PAD)
    s = jnp.where(ok, s, NEG)
    m = jnp.max(s, axis=-1, keepdims=True)
    e = jnp.exp(s - m)
    return e * (1.0 / jnp.sum(e, axis=-1, keepdims=True))


def _attn_fwd(q, k, v):
    tp = q.shape[0]
    tq = tp // ATTN_Q_TILES

    def body(q_ref, k_ref, v_ref, o_ref):
        for i in range(ATTN_Q_TILES):
            rows = slice(i * tq, (i + 1) * tq)
            keys = slice(0, (i + 1) * tq)
            p = _attn_probs(q_ref[rows, :], k_ref[keys, :], i * tq)
            o_ref[rows, :] = _nn(p, v_ref[keys, :])

    return pl.pallas_call(
        body, name="attn_fwd", grid=(MLA_HEADS,),
        in_specs=[pl.BlockSpec((tp, HP), lambda h: (0, h)),
                  pl.BlockSpec((tp, HP), lambda h: (0, h)),
                  pl.BlockSpec((tp, V_HEAD), lambda h: (0, h))],
        out_specs=pl.BlockSpec((tp, V_HEAD), lambda h: (0, h)),
        out_shape=jax.ShapeDtypeStruct((tp, MLA_HEADS * V_HEAD), F32),
        compiler_params=pltpu.CompilerParams(dimension_semantics=("parallel",), vmem_limit_bytes=VMEM_LIMIT),
    )(q, k, v)


def _attn_bwd(q, k, v, do):
    tp = q.shape[0]
    tq = tp // ATTN_Q_TILES

    def body(q_ref, k_ref, v_ref, do_ref, dq_ref, dk_ref, dv_ref):
        for i in reversed(range(ATTN_Q_TILES)):
            rows = slice(i * tq, (i + 1) * tq)
            keys = slice(0, (i + 1) * tq)
            qb = q_ref[rows, :]
            kk = k_ref[keys, :]
            dob = do_ref[rows, :]
            p = _attn_probs(qb, kk, i * tq)
            dp = _nt(dob, v_ref[keys, :])
            delta = jnp.sum(p * dp, axis=-1, keepdims=True)
            ds = p * (dp - delta) * (1.0 / math.sqrt(QK_HEAD))
            dq_ref[rows, :] = _nn(ds, kk)
            if i == ATTN_Q_TILES - 1:
                dk_ref[...] = _tn(ds, qb)
                dv_ref[...] = _tn(p, dob)
            else:
                dk_ref[keys, :] += _tn(ds, qb)
                dv_ref[keys, :] += _tn(p, dob)

    full = lambda w: pl.BlockSpec((tp, w), lambda h: (0, h))
    return pl.pallas_call(
        body, name="attn_bwd", grid=(MLA_HEADS,),
        in_specs=[full(HP), full(HP), full(V_HEAD), full(V_HEAD)],
        out_specs=[full(HP), full(HP), full(V_HEAD)],
        out_shape=[jax.ShapeDtypeStruct((tp, MLA_HEADS * HP), F32),
                   jax.ShapeDtypeStruct((tp, MLA_HEADS * HP), F32),
                   jax.ShapeDtypeStruct((tp, MLA_HEADS * V_HEAD), F32)],
        compiler_params=pltpu.CompilerParams(dimension_semantics=("parallel",), vmem_limit_bytes=VMEM_LIMIT),
    )(q, k, v, do)


def _gdn_consts():
    c = DN_CHUNK
    r = lax.broadcasted_iota(jnp.int32, (c, c), 0)
    cc = lax.broadcasted_iota(jnp.int32, (c, c), 1)
    incl = r >= cc
    strict = r > cc
    return incl, strict


def _cumsum_rows(x, reverse=False):
    c = x.shape[0]
    row = lax.broadcasted_iota(jnp.int32, x.shape, 0)
    s = 1
    while s < c:
        if reverse:
            x = x + jnp.where(row < c - s, pltpu.roll(x, c - s, 0), 0.0)
        else:
            x = x + jnp.where(row >= s, pltpu.roll(x, s, 0), 0.0)
        s *= 2
    return x


def _each(fn, *lists):
    return [fn(*a) for a in zip(*lists)]


def _interleave(chains):
    chains = list(chains)
    while chains:
        for ch in list(chains):
            try:
                next(ch)
            except StopIteration:
                chains.remove(ch)


def _gdn_chunk_common(q_ref, k_ref, v_ref, g_ref, b_ref):
    c = DN_CHUNK
    incl, strict = _gdn_consts()
    sls = [(slice(c * sub, c * (sub + 1)), slice(DN_DIM * h, DN_DIM * (h + 1)))
           for sub in range(GDN_SUB_CHUNKS) for h in range(DN_HEADS)]
    q = [q_ref[sl] * (1.0 / math.sqrt(DN_DIM)) for sl in sls]
    k = [k_ref[sl] for sl in sls]
    v = [v_ref[sl] for sl in sls]
    g = [g_ref[sl] for sl in sls]
    beta = [b_ref[sl] for sl in sls]
    gc = [_cumsum_rows(x) for x in g]
    grow = [x.T[:c, :] for x in gc]
    kb = _each(jnp.multiply, k, beta)
    kk = _each(_nt, kb, k)
    qk = _each(_nt, q, k)
    gam = [jnp.exp(x) for x in gc]
    g_last = [_rowsum(x) for x in g]
    dm = [jnp.exp(jnp.where(incl, x[:, :c] - y, NEG)) for x, y in zip(gc, grow)]
    vb = _each(jnp.multiply, v, beta)
    kbg = _each(jnp.multiply, kb, gam)
    ek = [jnp.exp(x - y) for x, y in zip(g_last, gc)]
    kd = _each(jnp.multiply, k, ek)
    return dict(q=q, k=k, v=v, beta=beta, gc=gc, gam=gam, g_last=g_last, dm=dm, kb=kb, vb=vb,
                kbg=kbg, kk=kk, ek=ek, kd=kd, qk=qk, incl=incl, strict=strict, sls=sls)


def _gdn_fwd(proj, conv_w8, alog, dtb):
    tp = proj.shape[0]
    c = DN_CHUNK
    nch = tp // c
    blk = GDN_SUB_CHUNKS * c

    def body(x_ref, xp_ref, ab_ref, w8_ref, alog_ref, dtb_ref,
             o_ref, s_ref, t_ref, q_ref, k_ref, v_ref, g_ref, b_ref, s_scr):
        @pl.when(pl.program_id(0) == 0)
        def _():
            s_scr[...] = jnp.zeros_like(s_scr)

        staged, _ = _f_gdn_prep(pl.program_id(0), x_ref[...], xp_ref[...], ab_ref[...], w8_ref[...],
                                alog_ref[...], dtb_ref[...])
        for ref, val in zip((q_ref, k_ref, v_ref, g_ref, b_ref), staged):
            ref[...] = val
        eye = (lax.broadcasted_iota(jnp.int32, (c, c), 0) == lax.broadcasted_iota(jnp.int32, (c, c), 1)).astype(F32)
        x = _gdn_chunk_common(q_ref, k_ref, v_ref, g_ref, b_ref)
        heads = range(DN_HEADS)
        bp = [-jnp.where(x["strict"], kk * dm, 0.0) for kk, dm in zip(x["kk"], x["dm"])]
        t = [eye + b for b in bp]
        for _ in range(5):
            bp = [_nn(b, b, hp="3x") for b in bp]
            t = [tt + _nn(tt, b, hp="3x") for tt, b in zip(t, bp)]
        u = _each(_nn, t, x["vb"])
        w = _each(_nn, t, x["kbg"])
        qg = _each(jnp.multiply, x["q"], x["gam"])
        mqk = _each(jnp.multiply, x["qk"], x["dm"])
        s = [s_scr[h] for h in heads]
        for sub in range(GDN_SUB_CHUNKS):
            e = [DN_HEADS * sub + h for h in heads]
            v_new = [u[i] - _nn(w[i], s[h]) for h, i in zip(heads, e)]
            o = [_nn(qg[i], s[h]) + _nn(mqk[i], v_new[h]) for h, i in zip(heads, e)]
            s_new = [s[h] * jnp.exp(x["g_last"][i]) + _tn(x["kd"][i], v_new[h]) for h, i in zip(heads, e)]
            for h, i in zip(heads, e):
                s_ref[h, sub] = s[h]
                t_ref[h, sub] = t[i]
                o_ref[x["sls"][i]] = o[h]
            s = s_new
        for h in heads:
            s_scr[h] = s[h]

    sub = GDN_SUB_CHUNKS
    rb = lambda n: (n, 0)
    rows = pl.BlockSpec((blk, DN_WIDTH), rb)
    whole = lambda a: pl.BlockSpec(a.shape, lambda n: (0, 0))
    return pl.pallas_call(
        body, name="gdn_fwd", grid=(nch // sub,),
        in_specs=[pl.BlockSpec((blk, 3 * DN_WIDTH), rb),
                  pl.BlockSpec((8, 3 * DN_WIDTH), lambda n: (jnp.maximum(n * (blk // 8) - 1, 0), 0)),
                  pl.BlockSpec((blk, LANE), lambda n: (n, C_AB // LANE)),
                  whole(conv_w8), whole(alog), whole(dtb)],
        out_specs=[rows,
                   pl.BlockSpec((DN_HEADS, sub, DN_DIM, DN_DIM), lambda n: (0, n, 0, 0)),
                   pl.BlockSpec((DN_HEADS, sub, c, c), lambda n: (0, n, 0, 0))] + [rows] * 5,
        out_shape=[jax.ShapeDtypeStruct((tp, DN_WIDTH), F32),
                   jax.ShapeDtypeStruct((DN_HEADS, nch, DN_DIM, DN_DIM), F32),
                   jax.ShapeDtypeStruct((DN_HEADS, nch, c, c), F32)] + [jax.ShapeDtypeStruct((tp, DN_WIDTH), F32)] * 5,
        scratch_shapes=[pltpu.VMEM((DN_HEADS, DN_DIM, DN_DIM), F32)],
        compiler_params=pltpu.CompilerParams(dimension_semantics=("arbitrary",), vmem_limit_bytes=VMEM_LIMIT),
    )(proj, proj, proj, conv_w8, alog, dtb)


def _gdn_bwd(q, k, v, g, beta, s_all, t_all, do, proj, conv_w8, alog, dtb, after):
    tp = q.shape[0]
    c = DN_CHUNK
    nch = tp // c
    nblk = nch // GDN_SUB_CHUNKS
    blk = GDN_SUB_CHUNKS * c

    def body(q_ref, k_ref, v_ref, g_ref, b_ref, s_ref, t_ref, do_ref, x_ref, xp_ref, xn_ref, ab_ref,
             w8_ref, alog_ref, dtb_ref, _after_ref, dqkv_ref, dab_ref, dcw_ref, dalog_ref, ddtb_ref,
             ds_scr, dq_ref, dk_ref, dv_ref, dg_ref, db_ref, nxt_scr):
        step = pl.program_id(0)

        @pl.when(step == 0)
        def _():
            ds_scr[...] = jnp.zeros_like(ds_scr)
            nxt_scr[...] = jnp.zeros_like(nxt_scr)

        xs = _gdn_chunk_common(q_ref, k_ref, v_ref, g_ref, b_ref)

        ds_state = [ds_scr[h] for h in range(DN_HEADS)]

        def chain(sub, h):
            e = DN_HEADS * sub + h
            x = {key: (val[e] if isinstance(val, list) else val) for key, val in xs.items()}
            sl = x["sls"]
            qs, kx, vx, beta_, gam, dm = x["q"], x["k"], x["v"], x["beta"], x["gam"], x["dm"]
            kb, vb, kbg, kd, ek = x["kb"], x["vb"], x["kbg"], x["kd"], x["ek"]
            t = t_ref[h, sub]
            s = s_ref[h, sub]
            dsn = ds_state[h]
            dob = do_ref[sl]
            eg_last = jnp.exp(x["g_last"])
            u = _nn(t, vb)
            w = _nn(t, kbg)
            mqk = x["qk"] * dm
            qd = qs * gam
            dqd = _nt(dob, s)
            dkd_pre = _nn(kd, dsn)
            yield
            v_new = u - _nn(w, s)
            dv_new = _tn(mqk, dob) + dkd_pre
            dq = dqd * gam
            dgam = jnp.sum(dqd * qs, axis=1, keepdims=True)
            yield
            ds_state[h] = _tn(qd, dob) + eg_last * dsn - _tn(w, dv_new)
            dmm = jnp.where(x["incl"], _nt(dob, v_new), 0.0)
            dkd = _nt(v_new, dsn)
            dw = -_nt(dv_new, s)
            dvb = _tn(t, dv_new)
            dt = _nt(dv_new, vb)
            yield
            dqk = dmm * dm
            e_mat = dmm * mqk
            dq = dq + _nn(dqk, kx)
            dk = _tn(dqk, qs) + dkd * ek
            e1 = jnp.sum(dkd * kd, axis=1, keepdims=True)
            dgc = -e1
            dg_last = jnp.sum(e1) + eg_last * jnp.sum(s * dsn)
            dt = dt + _nt(dw, kbg)
            dkbg = _tn(t, dw)
            yield
            tdt = _tn(t, dt, hp="3x")
            yield
            da = jnp.where(x["strict"], -_nt(tdt, t, hp="3x"), 0.0)
            yield
            dkk = da * dm
            e_mat = e_mat + da * x["kk"] * dm
            dkb = _nn(dkk, kx) + dkbg * gam
            dk = dk + _tn(dkk, kb)
            dgam = dgam + jnp.sum(dkbg * kb, axis=1, keepdims=True)
            yield
            dk = dk + dkb * beta_
            dbeta = jnp.sum(dkb * kx, axis=1, keepdims=True) + jnp.sum(dvb * vx, axis=1, keepdims=True)
            dv = dvb * beta_
            dgc = dgc + jnp.sum(e_mat, axis=1, keepdims=True) + dgam * gam
            dgc = dgc - jnp.sum(e_mat.T, axis=1, keepdims=True)
            yield
            dg = _cumsum_rows(dgc, reverse=True) + dg_last
            yield
            dq_ref[sl] = dq * (1.0 / math.sqrt(DN_DIM))
            dk_ref[sl] = dk
            dv_ref[sl] = dv
            dg_ref[sl] = dg
            db_ref[sl] = jnp.broadcast_to(dbeta, (c, LANE))

        chains = []
        for sub in reversed(range(GDN_SUB_CHUNKS)):
            new = [chain(sub, h) for h in range(DN_HEADS)]
            for _ in range(3):
                for ch in new:
                    next(ch)
            chains += new
        _interleave(chains)
        for h in range(DN_HEADS):
            ds_scr[h] = ds_state[h]

        dq, dk, dv = dq_ref[...], dk_ref[...], dv_ref[...]
        outs, accs = _f_gdn_prep_bwd(
            nblk - 1 - step, x_ref[...], xp_ref[...], xn_ref[...], ab_ref[...], dq, nxt_scr[0], dk, nxt_scr[1],
            dv, nxt_scr[2], dg_ref[...], db_ref[...], w8_ref[...], alog_ref[...], dtb_ref[...], nt=nblk)
        nxt_scr[0] = dq[:8]
        nxt_scr[1] = dk[:8]
        nxt_scr[2] = dv[:8]
        dqkv_ref[...] = outs[0].astype(dqkv_ref.dtype)
        dab_ref[...] = outs[1].astype(dab_ref.dtype)
        for ref, val in zip((dcw_ref, dalog_ref, ddtb_ref), accs):
            @pl.when(step == 0)
            def _():
                ref[...] = val

            @pl.when(step > 0)
            def _():
                ref[...] += val

    sub = GDN_SUB_CHUNKS
    r8 = blk // 8
    rb = lambda n: (nblk - 1 - n, 0)
    hs = lambda n: (0, nblk - 1 - n, 0, 0)
    rows = pl.BlockSpec((blk, DN_WIDTH), rb)
    whole = lambda a: pl.BlockSpec(a.shape, lambda n: (0,) * a.ndim)
    wide = 3 * DN_WIDTH
    return pl.pallas_call(
        body, name="gdn_bwd", grid=(nblk,),
        in_specs=[rows] * 5
        + [pl.BlockSpec((DN_HEADS, sub, DN_DIM, DN_DIM), hs), pl.BlockSpec((DN_HEADS, sub, c, c), hs), rows,
           pl.BlockSpec((blk, wide), rb),
           pl.BlockSpec((8, wide), lambda n: (jnp.maximum((nblk - 1 - n) * r8 - 1, 0), 0)),
           pl.BlockSpec((8, wide), lambda n: (jnp.minimum((nblk - n) * r8, tp // 8 - 1), 0)),
           pl.BlockSpec((blk, LANE), lambda n: (nblk - 1 - n, C_AB // LANE)),
           whole(conv_w8), whole(alog), whole(dtb), _ANY_SPEC],
        out_specs=[pl.BlockSpec((blk, wide), rb), pl.BlockSpec((blk, LANE), rb),
                   whole(conv_w8), whole(alog), whole(dtb)],
        out_shape=[jax.ShapeDtypeStruct((tp, wide), _MXU), jax.ShapeDtypeStruct((tp, LANE), _MXU),
                   jax.ShapeDtypeStruct(conv_w8.shape, F32), jax.ShapeDtypeStruct(alog.shape, F32),
                   jax.ShapeDtypeStruct(dtb.shape, F32)],
        scratch_shapes=[pltpu.VMEM((DN_HEADS, DN_DIM, DN_DIM), F32)] + [pltpu.VMEM((blk, DN_WIDTH), F32)] * 5
        + [pltpu.VMEM((3, 8, DN_WIDTH), F32)],
        compiler_params=pltpu.CompilerParams(dimension_semantics=("arbitrary",), vmem_limit_bytes=VMEM_LIMIT),
    )(q, k, v, g, beta, s_all, t_all, do, proj, proj, proj, proj, conv_w8, alog, dtb, after)


def _silu_parts(x):
    s = _sigmoid(x)
    return x * s, s * (1.0 + x * (1.0 - s))


def _f_rms_bwd_add(i, x, dy, dres, w, *, mask_pad):
    dx, dwr = _rms_bwd(x, w, dy, x.shape[1])
    out = dres + dx
    if mask_pad:
        out = jnp.where(_row_ids(i, x.shape[0]) >= PAD, out, 0.0)
    return (out,), (_rowsum(dwr),)


def _rope(x, cos, sin_s):
    return x * cos + _swap_halves(x) * sin_s


def _rope_t(dy, cos, sin_s):
    return dy * cos + _swap_halves(dy * sin_s)


def _f_mla_qk(i, qf, kvf, kpe, cos, sin_s, qw, kw):
    qs, ks, vs = [], [], []
    for h in range(MLA_HEADS):
        qn, _ = _rms_fwd(qf[:, HP * h:HP * (h + 1)], qw, QK_HEAD)
        qs += [qn[:, :QK_NOPE], _rope(qn[:, QK_NOPE:], cos, sin_s)]
        kh = jnp.concatenate([kvf[:, HP * h:HP * h + QK_NOPE], kpe], axis=1)
        kn, _ = _rms_fwd(kh, kw, QK_HEAD)
        ks += [kn[:, :QK_NOPE], _rope(kn[:, QK_NOPE:], cos, sin_s)]
        vs.append(kvf[:, HP * h + QK_NOPE:HP * (h + 1)])
    return (jnp.concatenate(qs, axis=1), jnp.concatenate(ks, axis=1), jnp.concatenate(vs, axis=1)), ()


def _f_mla_front(i, ql, kvl, kpe, cos, sin_s, qaw, kvaw, wq_t, wkv, qw, kw):
    qn = _rms_fwd(ql, qaw, Q_LORA)[0].astype(_MXU)
    kvn = _rms_fwd(kvl, kvaw, KV_LORA)[0].astype(_MXU)
    qf = _nt(qn, wq_t)
    kvf = _nn(kvn, wkv)
    (q, k, v), _ = _f_mla_qk(i, qf, kvf, kpe, cos, sin_s, qw, kw)
    return (qn, kvn, qf, kvf, q, k, v), ()


def _f_mla_back(i, qf, kvf, kpe, cos, sin_s, dq, dk, dv, ql, kvl, qaw, kvaw, wq_t, wkv, qw, kw):
    (dqf, dkvf, dkpe), (dqw, dkw) = _f_mla_qk_bwd(i, qf, kvf, kpe, cos, sin_s, dq, dk, dv, qw, kw)
    dqf = dqf.astype(_MXU)
    dkvf = dkvf.astype(_MXU)
    dql, dqaw = _rms_bwd(ql, qaw, _nn(dqf, wq_t), Q_LORA)
    dkvl, dkvaw = _rms_bwd(kvl, kvaw, _nt(dkvf, wkv), KV_LORA)
    return (dqf, dkvf, dkpe, dql, dkvl), (dqw, dkw, _rowsum(dqaw), _rowsum(dkvaw))


def _f_mla_qk_bwd(i, qf, kvf, kpe, cos, sin_s, dq, dk, dv, qw, kw):
    dqf, dkvf = [], []
    dkpe = None
    dqw = None
    dkw = None
    for h in range(MLA_HEADS):
        dqh = dq[:, HP * h:HP * (h + 1)]
        dqn = jnp.concatenate([dqh[:, :QK_NOPE], _rope_t(dqh[:, QK_NOPE:], cos, sin_s)], axis=1)
        dx, dwr = _rms_bwd(qf[:, HP * h:HP * (h + 1)], qw, dqn, QK_HEAD)
        dqf.append(dx)
        dqw = _rowsum(dwr) if dqw is None else dqw + _rowsum(dwr)
        dkh = dk[:, HP * h:HP * (h + 1)]
        dkn = jnp.concatenate([dkh[:, :QK_NOPE], _rope_t(dkh[:, QK_NOPE:], cos, sin_s)], axis=1)
        kh = jnp.concatenate([kvf[:, HP * h:HP * h + QK_NOPE], kpe], axis=1)
        dx, dwr = _rms_bwd(kh, kw, dkn, QK_HEAD)
        dkvf += [dx[:, :QK_NOPE], dv[:, V_HEAD * h:V_HEAD * (h + 1)]]
        dkpe = dx[:, QK_NOPE:] if dkpe is None else dkpe + dx[:, QK_NOPE:]
        dkw = _rowsum(dwr) if dkw is None else dkw + _rowsum(dwr)
    return (jnp.concatenate(dqf, axis=1), jnp.concatenate(dkvf, axis=1), dkpe), (dqw, dkw)


def _gdn_act(i, x, halo, w8):
    halo = jnp.where(i > 0, halo, 0.0)
    c = _conv_fwd(x, halo, w8, DN_CONV)
    act, dact = _silu_parts(c)
    return act, dact


def _spread_heads(ab):
    tm = ab.shape[0]
    return jnp.concatenate([jnp.broadcast_to(ab[:, h:h + 1], (tm, DN_DIM)) for h in range(2 * DN_HEADS)], axis=1)


def _gather_heads(x):
    tm = x.shape[0]
    lane = lax.broadcasted_iota(jnp.int32, (tm, LANE), 1)
    out = jnp.zeros((tm, LANE), F32)
    for h in range(2 * DN_HEADS):
        out = out + jnp.where(lane == h, x[:, DN_DIM * h:DN_DIM * h + 1], 0.0)
    return out


def _f_gdn_prep(i, x, halo, ab, w8, alog, dtb):
    tm = x.shape[0]
    act, _ = _gdn_act(i, x, halo, w8)
    outs = []
    for part in range(2):
        for h in range(DN_HEADS):
            t = act[:, DN_WIDTH * part + DN_DIM * h:DN_WIDTH * part + DN_DIM * (h + 1)]
            outs.append(t * lax.rsqrt(jnp.sum(t * t, axis=-1, keepdims=True) + EPS))
    q = jnp.concatenate(outs[:DN_HEADS], axis=1)
    k = jnp.concatenate(outs[DN_HEADS:], axis=1)
    v = act[:, 2 * DN_WIDTH:]
    abb = _spread_heads(ab)
    valid = _row_ids(i, tm) >= PAD
    g = jnp.where(valid, -jnp.exp(alog) * _softplus(abb[:, :DN_WIDTH] + dtb), 0.0)
    beta = jnp.where(valid, _sigmoid(abb[:, DN_WIDTH:]), 0.0)
    return (q, k, v, g, beta), ()


def _f_gdn_prep_bwd(i, x, x_prev, x_next, ab, dq, dq_next, dk, dk_next, dv, dv_next, dg, dbeta,
                    w8, alog, dtb, *, nt):
    tm = x.shape[0]
    x_prev = jnp.where(i > 0, x_prev, 0.0)
    more = i < nt - 1
    ext = lambda t, t_next: jnp.concatenate([t, jnp.where(more, t_next, 0.0)], axis=0)
    c = _conv_fwd(jnp.concatenate([x, x_next], axis=0), x_prev, w8, DN_CONV)
    act, dact = _silu_parts(c)
    douts = []
    for part, dd in enumerate((ext(dq, dq_next), ext(dk, dk_next))):
        for h in range(DN_HEADS):
            t = act[:, DN_WIDTH * part + DN_DIM * h:DN_WIDTH * part + DN_DIM * (h + 1)]
            r = lax.rsqrt(jnp.sum(t * t, axis=-1, keepdims=True) + EPS)
            y = t * r
            dy = dd[:, DN_DIM * h:DN_DIM * (h + 1)]
            douts.append(r * (dy - y * jnp.sum(dy * y, axis=-1, keepdims=True)))
    douts.append(ext(dv, dv_next))
    dc = jnp.concatenate(douts, axis=1) * dact
    dqkv = _conv_bwd_x(dc[:tm], dc[tm:], w8, DN_CONV)
    dconv_w = _conv_bwd_w(dc[:tm], x, x_prev, DN_CONV)
    abb = _spread_heads(ab)
    valid = _row_ids(i, tm) >= PAD
    pre = abb[:, :DN_WIDTH] + dtb
    ea = jnp.exp(alog)
    g = -ea * _softplus(pre)
    dg = jnp.where(valid, dg, 0.0)
    dbeta = jnp.where(valid, dbeta, 0.0)
    da = dg * (-ea) * _sigmoid(pre)
    beta = _sigmoid(abb[:, DN_WIDTH:])
    db = dbeta * beta * (1.0 - beta)
    dab = _gather_heads(jnp.concatenate([da, db], axis=1))
    return (dqkv, dab), (dconv_w, _rowsum(dg * g), _rowsum(da))


def _f_mix(i, o_mla, o_dn, z, w_mla, w_dn):
    tm = o_mla.shape[0]
    valid = _row_ids(i, tm) >= PAD
    outs = []
    for h in range(MLA_HEADS):
        y, _ = _rms_fwd(o_mla[:, V_HEAD * h:V_HEAD * (h + 1)], w_mla, V_HEAD)
        outs.append(jnp.where(valid, y, 0.0))
    for h in range(DN_HEADS):
        y, _ = _rms_fwd(o_dn[:, DN_DIM * h:DN_DIM * (h + 1)], w_dn, DN_DIM)
        outs.append(y * _silu_parts(z[:, DN_DIM * h:DN_DIM * (h + 1)])[0])
    return (jnp.concatenate(outs, axis=1),), ()


def _f_mix_bwd(i, o_mla, o_dn, z, dy_mla, dy_dn, w_mla, w_dn):
    tm = o_mla.shape[0]
    valid = _row_ids(i, tm) >= PAD
    d_mla, d_dn, d_z = [], [], []
    dw_mla = None
    dw_dn = None
    for h in range(MLA_HEADS):
        sl = slice(V_HEAD * h, V_HEAD * (h + 1))
        dx, dwr = _rms_bwd(o_mla[:, sl], w_mla, jnp.where(valid, dy_mla[:, sl], 0.0), V_HEAD)
        d_mla.append(dx)
        dw_mla = _rowsum(dwr) if dw_mla is None else dw_mla + _rowsum(dwr)
    for h in range(DN_HEADS):
        sl = slice(DN_DIM * h, DN_DIM * (h + 1))
        y, _ = _rms_fwd(o_dn[:, sl], w_dn, DN_DIM)
        sz, dsz = _silu_parts(z[:, sl])
        d_z.append(dy_dn[:, sl] * y * dsz)
        dx, dwr = _rms_bwd(o_dn[:, sl], w_dn, dy_dn[:, sl] * sz, DN_DIM)
        d_dn.append(dx)
        dw_dn = _rowsum(dwr) if dw_dn is None else dw_dn + _rowsum(dwr)
    return ((jnp.concatenate(d_mla, axis=1), jnp.concatenate(d_dn, axis=1), jnp.concatenate(d_z, axis=1)),
            (dw_mla, dw_dn))


def _f_ffn_act_bwd(i, gp, gp_prev, gp_next, up, up_next, dact, dact_next, w8, b, *, nt):
    tm = gp.shape[0]
    gp_prev = jnp.where(i > 0, gp_prev, 0.0)
    dact_next = jnp.where(i < nt - 1, dact_next, 0.0)
    cat = lambda t, t_next: jnp.concatenate([t, t_next], axis=0)
    gate = _conv_fwd(cat(gp, gp_next), gp_prev, w8, FFN_CONV) + b
    sg, dsg = _silu_parts(gate)
    dact_e = cat(dact, dact_next)
    dgate = dact_e * cat(up, up_next) * dsg
    dgate_pre = _conv_bwd_x(dgate[:tm], dgate[tm:], w8, FFN_CONV)
    dup = dact * sg[:tm]
    return (dgate_pre, dup), (_conv_bwd_w(dgate[:tm], gp, gp_prev, FFN_CONV), _rowsum(dgate[:tm]))


def _f_loss(i, h3, tgt):
    tm = h3.shape[0]
    diff = jnp.where(_row_ids(i, tm) >= ROW0, h3 - tgt, 0.0)
    part = 0.5 * jnp.sum(diff * diff) * (1.0 / D_MODEL)
    return (diff * (1.0 / D_MODEL),), (jnp.full((1, LANE), part, F32),)


def _local_step(h0, tgt, w, token, late_weights, grads_ready):
    tp = h0.shape[0]
    nt = tp // TM
    proj, u = _norm_mm("in_proj", h0, w["attn_norm_w"], w["w_in"], after=token)
    p_qkv = lambda kind="cur": _In(proj, 3 * DN_WIDTH, 0, kind)
    p_z = _In(proj, DN_WIDTH, C_Z // DN_WIDTH)
    p_ql = _In(proj, Q_LORA, C_QL // Q_LORA)
    p_kvl = _In(proj, KV_LORA, C_KVL // KV_LORA)
    p_kpe = _In(proj, LANE, C_KPE // LANE)
    p_ab = _In(proj, LANE, C_AB // LANE)
    cos, sin_s = _In(w["cos"]), _In(w["sin_s"])

    mla_w = [w["q_a_norm_w"], w["kv_a_norm_w"], w["w_q_b"], w["w_kv_b"], w["q_norm_w"], w["k_norm_w"]]
    tm_mla = _pick(tp, 288, 16)
    wide = MLA_HEADS * HP
    qn, kvn, qf, kvf, q, k, v = _rows(
        "mla_front", _f_mla_front, [p_ql, p_kvl, p_kpe, cos, sin_s], mla_w,
        [(Q_LORA, _MXU), (KV_LORA, _MXU), (wide, F32), (wide, F32), (wide, _MXU), (wide, _MXU),
         (MLA_HEADS * V_HEAD, _MXU)], tm=tm_mla)
    o_mla = _attn_fwd(q, k, v)

    dn_w = [w["dn_conv_w"], w["alog_b"], w["dtb_b"]]
    o_dn, s_all, t_all, gq, gk, gv, gg, gb = _gdn_fwd(proj, *dn_w)

    out_w = [w["mla_out_norm_w"], w["dn_out_norm_w"]]
    w = dict(w, **late_weights((o_mla, o_dn), _LATE[:3]))
    h2, mixed = _pro_mm("mix_out_proj", lambda i, *t: _f_mix(i, *t)[0][0], [_In(o_mla), _In(o_dn), p_z], out_w,
                        D_MODEL, w["w_out"], h0)

    ffn_w = [w["ffn_conv_w"], w["ffn_conv_b"]]
    hn, gate_pre, up, act = _ffn_in(h2, w["ffn_norm_w"], w["w_gate"], w["w_up"], *ffn_w)
    w = dict(w, **late_weights(act, _LATE[3:]))
    dh3, loss = _mm_rows("ffn_down_loss", act, w["w_down"], "nn", lambda i, y, r, t: _f_loss(i, r + y, t),
                         [_In(h2), _In(tgt)], [], [(D_MODEL, F32)], [(1, LANE)])

    g = {}
    dact = _mm("ffn_down_dx", dh3, w["w_down"], "nt")
    g["w_down"] = _mm("ffn_down_dw", act, dh3, "tn", out_dtype=_MXU)
    dgate_pre, dup, g["ffn_conv_w"], g["ffn_conv_b"] = _rows(
        "ffn_act_bwd", functools.partial(_f_ffn_act_bwd, nt=nt),
        [_In(gate_pre), _In(gate_pre, kind="prev"), _In(gate_pre, kind="next"), _In(up), _In(up, kind="next"),
         _In(dact), _In(dact, kind="next")], ffn_w,
        [(D_FF, _MXU), (D_FF, _MXU)], [(8, D_FF), (1, D_FF)])
    g["w_gate"], g["w_up"] = _mm_tn2("ffn_gate_up_dw", dgate_pre, dup, hn, out_dtype=_MXU)
    tok = grads_ready(g, ("w_down", "w_gate", "w_up"))
    dh2, g["ffn_norm_w"] = _mm_rows(
        "ffn_gate_up_dx_rms", [dgate_pre, dup], [w["w_gate"], w["w_up"]], "nn",
        lambda i, dy, x, dres, nw, _tok: _f_rms_bwd_add(i, x, dy, dres, nw, mask_pad=True),
        [_In(h2), _In(dh3)], [w["ffn_norm_w"], tok], [(D_MODEL, F32)], [(1, D_MODEL)])

    g["w_out"] = _mm("out_proj_dw", mixed, dh2, "tn", out_dtype=_MXU)
    half = MLA_HEADS * V_HEAD
    do_mla, do_dn, dz, g["mla_out_norm_w"], g["dn_out_norm_w"] = _mm_rows(
        "out_proj_dx_mix", dh2, w["w_out"], "nt",
        lambda i, dm, om, od, z, wm, wd: _f_mix_bwd(i, om, od, z, dm[:, :half], dm[:, half:], wm, wd),
        [_In(o_mla), _In(o_dn), p_z], out_w,
        [(half, F32), (DN_WIDTH, F32), (DN_WIDTH, _MXU)], [(1, V_HEAD), (1, DN_DIM)])

    dq, dk, dv = _attn_bwd(q, k, v, do_mla)
    dqf, dkvf, dkpe, dql, dkvl, g["q_norm_w"], g["k_norm_w"], g["q_a_norm_w"], g["kv_a_norm_w"] = _rows(
        "mla_back", _f_mla_back,
        [_In(qf), _In(kvf), p_kpe, cos, sin_s, _In(dq), _In(dk), _In(dv), p_ql, p_kvl], mla_w,
        [(wide, _MXU), (wide, _MXU), (LANE, _MXU), (Q_LORA, _MXU), (KV_LORA, _MXU)],
        [(1, HP), (1, HP), (1, Q_LORA), (1, KV_LORA)], tm=tm_mla)
    g["w_q_b"] = _mm("mla_q_b_dw", dqf, qn, "tn")
    g["w_kv_b"] = _mm("mla_kv_b_dw", kvn, dkvf, "tn")
    tok = grads_ready(g, ("w_out", "w_q_b", "w_kv_b"))

    dqkv, dab, g["dn_conv_w"], g["alog_b"], g["dtb_b"] = _gdn_bwd(
        gq, gk, gv, gg, gb, s_all, t_all, do_dn, proj, *dn_w, tok)

    dproj = jnp.concatenate([dqkv, dz, dql, dkvl, dkpe, dab], axis=1)
    g["w_in"] = _mm("in_proj_dw", dproj, u, "tn", out_dtype=_MXU)
    tok = grads_ready(g, ("w_in",))
    dh0, g["attn_norm_w"] = _mm_rows(
        "in_proj_dx_rms", dproj, w["w_in"], "nn",
        lambda i, du, x, dres, nw, _tok: _f_rms_bwd_add(i, x, du, dres, nw, mask_pad=False),
        [_In(h0), _In(dh2)], [w["attn_norm_w"], tok], [(D_MODEL, F32)], [(1, D_MODEL)])
    return loss, dh0, g


def _w_in_to_padded(w):
    c1, c2, c3 = Q_LORA, Q_LORA + KV_LORA, Q_LORA + KV_LORA + QK_ROPE
    c4 = c3 + 3 * DN_WIDTH
    c5 = c4 + DN_WIDTH
    z = lambda n: jnp.zeros((n, w.shape[1]), w.dtype)
    return jnp.concatenate([w[c3:c4], w[c4:c5], w[:c1], w[c1:c2], w[c2:c3], z(LANE - QK_ROPE),
                            w[c5:], z(LANE - 2 * DN_HEADS)], axis=0)


def _w_in_from_padded(g):
    return jnp.concatenate([g[C_QL:C_QL + Q_LORA], g[C_KVL:C_KVL + KV_LORA], g[C_KPE:C_KPE + QK_ROPE],
                            g[:C_Z + DN_WIDTH], g[C_AB:C_AB + 2 * DN_HEADS]], axis=0)


def _w_q_b_to_padded(w):
    r = w.shape[1]
    w = w.reshape(MLA_HEADS, QK_HEAD, r)
    return jnp.pad(w, ((0, 0), (0, HP - QK_HEAD), (0, 0))).reshape(MLA_HEADS * HP, r)


def _w_q_b_from_padded(g):
    r = g.shape[1]
    return g.reshape(MLA_HEADS, HP, r)[:, :QK_HEAD].reshape(MLA_HEADS * QK_HEAD, r)


def _pad_rows8(w):
    return jnp.pad(w, ((0, 8 - w.shape[0]), (0, 0)))


def _prepare(full, tp):
    w = {}
    mx = lambda a: a.astype(_MXU)
    w["attn_norm_w"] = full["attn_norm_w"]
    w["w_in"] = mx(_w_in_to_padded(full["w_in"]))
    w["q_a_norm_w"] = full["q_a_norm_w"]
    w["kv_a_norm_w"] = full["kv_a_norm_w"]
    w["w_q_b"] = mx(_w_q_b_to_padded(full["w_q_b"]))
    w["w_kv_b"] = mx(full["w_kv_b"])
    w["q_norm_w"] = jnp.pad(full["q_norm_w"], ((0, 0), (0, HP - QK_HEAD)))
    w["k_norm_w"] = jnp.pad(full["k_norm_w"], ((0, 0), (0, HP - QK_HEAD)))
    w["mla_out_norm_w"] = full["mla_out_norm_w"]
    w["dn_out_norm_w"] = full["dn_out_norm_w"]
    w["dn_conv_w"] = _pad_rows8(full["dn_conv_w"])
    w["alog_b"] = jnp.repeat(full["dn_A_log"], DN_DIM, axis=1)
    w["dtb_b"] = jnp.repeat(full["dn_dt_bias"], DN_DIM, axis=1)
    w["ffn_norm_w"] = full["ffn_norm_w"]
    w["ffn_conv_w"] = _pad_rows8(full["ffn_conv_w"])
    w["ffn_conv_b"] = full["ffn_conv_b"]
    for n in _LATE:
        if n in full:
            w[n] = mx(full[n])
    half = QK_ROPE // 2
    inv = ROPE_THETA ** (-jnp.arange(half, dtype=F32) / half)
    ang = (jnp.arange(tp, dtype=jnp.int32) - PAD).astype(F32)[:, None] * inv[None, :]
    zc = jnp.zeros((tp, LANE - QK_ROPE), F32)
    w["cos"] = jnp.concatenate([jnp.cos(ang), jnp.cos(ang), zc], axis=1)
    w["sin_s"] = jnp.concatenate([-jnp.sin(ang), jnp.sin(ang), zc], axis=1)
    return w


def _grads_to_natural(g):
    convert = {
        "w_in": ("w_in", _w_in_from_padded),
        "w_q_b": ("w_q_b", _w_q_b_from_padded),
        "q_norm_w": ("q_norm_w", lambda a: a[:, :QK_HEAD]),
        "k_norm_w": ("k_norm_w", lambda a: a[:, :QK_HEAD]),
        "dn_conv_w": ("dn_conv_w", lambda a: a[:DN_CONV]),
        "ffn_conv_w": ("ffn_conv_w", lambda a: a[:FFN_CONV]),
        "alog_b": ("dn_A_log", lambda a: a[:, ::DN_DIM]),
        "dtb_b": ("dn_dt_bias", lambda a: a[:, ::DN_DIM]),
    }
    n = {}
    for key, a in g.items():
        name, fn = convert.get(key, (key, lambda t: t))
        n[name] = fn(a)
    return n


_MESH = pl.DeviceIdType.MESH
_ANY = pl.BlockSpec(memory_space=pl.ANY)
_CHIP_FLIPS = ((1, 0), (0, 1), (1, 1))


def _me():
    return lax.axis_index("x"), lax.axis_index("y"), lax.axis_index("c")


def _all_gather(name, blk):
    def body(x_ref, out_ref, send_sems, recv_sems, local_sem):
        x, y, c = _me()
        me, sib = (x, y, c), (x, y, 1 - c)
        chips = [(x ^ fx, y ^ fy) for fx, fy in _CHIP_FLIPS]

        def slot(p):
            return out_ref.at[4 * p[0] + 2 * p[1] + p[2]]

        def copy(k, block, to, src=None):
            return pltpu.make_async_remote_copy(
                src_ref=slot(block) if src is None else src, dst_ref=slot(block),
                send_sem=send_sems.at[k], recv_sem=recv_sems.at[k], device_id=to, device_id_type=_MESH)

        mine = pltpu.make_async_copy(x_ref, slot(me), local_sem)
        mine.start()
        first = [copy(0, me, sib, src=x_ref)]
        first += [copy(1 + j, me, (*chip, c), src=x_ref) for j, chip in enumerate(chips)]
        for cp in first:
            cp.start()
        passed = [copy(4 + j, (*chip, c), sib) for j, chip in enumerate(chips)]
        for j, chip in enumerate(chips):
            copy(1 + j, (*chip, c), me).wait_recv()
            passed[j].start()
        copy(0, sib, me).wait_recv()
        for j, chip in enumerate(chips):
            copy(4 + j, (*chip, 1 - c), me).wait_recv()
        for cp in first + passed:
            cp.wait_send()
        mine.wait()

    return pl.pallas_call(
        body, name=name, in_specs=[_ANY], out_specs=_ANY,
        out_shape=jax.ShapeDtypeStruct((N_DEV,) + blk.shape, blk.dtype),
        scratch_shapes=[pltpu.SemaphoreType.DMA((7,)), pltpu.SemaphoreType.DMA((7,)), pltpu.SemaphoreType.DMA],
    )(blk)


def _row_tile(r):
    divs = [d for d in range(16, min(r, 512) + 1, 16) if r % d == 0]
    return divs[-1] if divs else r


def _adam_math(g, w, m, v):
    m_new = ADAM_B1 * m + (1.0 - ADAM_B1) * g
    v_new = ADAM_B2 * v + (1.0 - ADAM_B2) * (g * g)
    m_hat = m_new / (1.0 - ADAM_B1 ** ADAM_STEP)
    v_hat = v_new / (1.0 - ADAM_B2 ** ADAM_STEP)
    return -ADAM_LR * (m_hat / (jnp.sqrt(v_hat) + ADAM_EPS) + ADAM_WD * w), m_new, v_new


def _adam_vectors(name, row, items, ws, ms, vs):
    k = len(items)

    def body(row_ref, *refs):
        w_refs, m_refs, v_refs = refs[:k], refs[k:2 * k], refs[2 * k:3 * k]
        outs = refs[3 * k:]
        for idx, (off, n, per_head) in enumerate(items):
            if per_head:
                spread = row_ref[:, off:off + DN_WIDTH]
                lane = lax.broadcasted_iota(jnp.int32, (1, LANE), 1)
                g = jnp.zeros((1, LANE), F32)
                for h in range(DN_HEADS):
                    g = g + jnp.where(lane == h, spread[:, DN_DIM * h:DN_DIM * h + 1], 0.0)
                g = g[:, :n]
            else:
                g = row_ref[:, off:off + n]
            d, m_new, v_new = _adam_math(g, w_refs[idx][...], m_refs[idx][...], v_refs[idx][...])
            for kind, val in enumerate((g, d, m_new, v_new)):
                outs[kind * k + idx][...] = val

    shapes = [jax.ShapeDtypeStruct((1, n), F32) for _, n, _ in items]
    res = pl.pallas_call(body, name=name, out_shape=shapes * 4)(row, *ws, *ms, *vs)
    return [list(res[kind * k:(kind + 1) * k]) for kind in range(4)]


def _adam_arrays(name, gs, ws, ms, vs):
    k = len(gs)

    def body(*refs):
        outs = refs[4 * k:]
        for idx in range(k):
            res = _adam_math(refs[idx][...], refs[k + idx][...], refs[2 * k + idx][...], refs[3 * k + idx][...])
            for kind, val in enumerate(res):
                outs[kind * k + idx][...] = val

    shapes = [jax.ShapeDtypeStruct(w.shape, F32) for w in ws]
    res = pl.pallas_call(body, name=name, out_shape=shapes * 3)(*gs, *ws, *ms, *vs)
    return [list(res[kind * k:(kind + 1) * k]) for kind in range(3)]


def _sum_parts(name, parts):
    _, r, cols = parts[0][0].shape
    tm = _row_tile(r)
    idx = jnp.stack([jnp.asarray(s, jnp.int32) for _, s in parts])
    n = len(parts)

    def body(idx_ref, *refs):
        g = refs[0][0].astype(F32)
        for p_ref in refs[1:n]:
            g = g + p_ref[0].astype(F32)
        refs[n][...] = g

    return pl.pallas_call(
        body, name=name,
        grid_spec=pltpu.PrefetchScalarGridSpec(
            num_scalar_prefetch=1, grid=(r // tm,),
            in_specs=[pl.BlockSpec((1, tm, cols), lambda i, idx_ref, p=p: (idx_ref[p], i, 0)) for p in range(n)],
            out_specs=pl.BlockSpec((tm, cols), lambda i, idx_ref: (i, 0))),
        out_shape=jax.ShapeDtypeStruct((r, cols), F32),
        compiler_params=pltpu.CompilerParams(dimension_semantics=("parallel",)),
    )(idx, *[a for a, _ in parts])


def _adam(name, parts, w, m, v):
    r, cols = w.shape
    tm = _row_tile(r)
    tc = cols // 4 if (r // tm < 4 and cols % (4 * LANE) == 0) else cols
    idx = jnp.stack([jnp.asarray(s, jnp.int32) for _, s in parts])
    n = len(parts)

    def body(idx_ref, *refs):
        g = refs[0][0].astype(F32)
        for p_ref in refs[1:n]:
            g = g + p_ref[0].astype(F32)
        w_ref, m_ref, v_ref, g_out, d_out, m_out, v_out = refs[n:]
        g_out[...] = g
        d_out[...], m_out[...], v_out[...] = _adam_math(g, w_ref[...], m_ref[...], v_ref[...])

    part_specs = [pl.BlockSpec((1, tm, tc), lambda i, j, idx_ref, p=p: (idx_ref[p], i, j)) for p in range(n)]
    flat = pl.BlockSpec((tm, tc), lambda i, j, idx_ref: (i, j))
    return pl.pallas_call(
        body, name=name,
        grid_spec=pltpu.PrefetchScalarGridSpec(
            num_scalar_prefetch=1, grid=(r // tm, cols // tc), in_specs=part_specs + [flat] * 3,
            out_specs=[flat] * 4),
        out_shape=[jax.ShapeDtypeStruct((r, cols), F32)] * 4,
        compiler_params=pltpu.CompilerParams(dimension_semantics=("parallel", "parallel")),
    )(idx, *[a for a, _ in parts], w, m, v)


def _all_gather_many(name, blks):
    n = len(blks)

    def body(*refs):
        x_refs, out_refs = refs[:n], refs[n:2 * n]
        send_sems, recv_sems, local_sems = refs[2 * n:]
        x, y, c = _me()
        me, sib = (x, y, c), (x, y, 1 - c)
        chips = [(x ^ fx, y ^ fy) for fx, fy in _CHIP_FLIPS]

        def slot(a, p):
            return out_refs[a].at[4 * p[0] + 2 * p[1] + p[2]]

        def copy(a, k, block, to, src=None):
            return pltpu.make_async_remote_copy(
                src_ref=slot(a, block) if src is None else src, dst_ref=slot(a, block),
                send_sem=send_sems.at[7 * a + k], recv_sem=recv_sems.at[7 * a + k], device_id=to,
                device_id_type=_MESH)

        mine = [pltpu.make_async_copy(x_refs[a], slot(a, me), local_sems.at[a]) for a in range(n)]
        first = []
        for a in range(n):
            mine[a].start()
            first.append(copy(a, 0, me, sib, src=x_refs[a]))
            first += [copy(a, 1 + j, me, (*chip, c), src=x_refs[a]) for j, chip in enumerate(chips)]
        for cp in first:
            cp.start()
        passed = []
        for j, chip in enumerate(chips):
            for a in range(n):
                copy(a, 1 + j, (*chip, c), me).wait_recv()
                cp = copy(a, 4 + j, (*chip, c), sib)
                cp.start()
                passed.append(cp)
        for a in range(n):
            copy(a, 0, sib, me).wait_recv()
            for j, chip in enumerate(chips):
                copy(a, 4 + j, (*chip, 1 - c), me).wait_recv()
        for cp in first + passed:
            cp.wait_send()
        for cp in mine:
            cp.wait()

    return pl.pallas_call(
        body, name=name, in_specs=[_ANY] * n, out_specs=[_ANY] * n,
        out_shape=[jax.ShapeDtypeStruct((N_DEV,) + b.shape, b.dtype) for b in blks],
        scratch_shapes=[pltpu.SemaphoreType.DMA((7 * n,)), pltpu.SemaphoreType.DMA((7 * n,)),
                        pltpu.SemaphoreType.DMA((n,))],
    )(*blks)


_HBM = pl.BlockSpec(memory_space=pltpu.HBM)
_SEM = pl.BlockSpec(memory_space=pltpu.SEMAPHORE)
_EFFECT = pltpu.SideEffectType.DATAFLOW_SIDE_EFFECTING


def _push_copies(src_refs, land_refs, send_sems, recv_sems, src_by_peer, first=0):
    x, y, c = _me()
    my_id = 4 * x + 2 * y + c
    out = []
    for k in range(len(src_refs)):
        a = first + k
        for f in range(1, N_DEV):
            px, py, pc = x ^ (f >> 2), y ^ ((f >> 1) & 1), c ^ (f & 1)
            pid = 4 * px + 2 * py + pc
            src = src_refs[k].at[pid] if src_by_peer else src_refs[k]
            start = pltpu.make_async_remote_copy(
                src_ref=src, dst_ref=land_refs[k].at[my_id], send_sem=send_sems.at[7 * a + f - 1],
                recv_sem=recv_sems.at[7 * a + f - 1], device_id=(px, py, pc), device_id_type=_MESH)
            landed = pltpu.make_async_remote_copy(
                src_ref=src, dst_ref=land_refs[k].at[pid], send_sem=send_sems.at[7 * a + f - 1],
                recv_sem=recv_sems.at[7 * a + f - 1], device_id=(px, py, pc), device_id_type=_MESH)
            out.append((start, landed))
    return out


def _push_start(name, srcs, src_by_peer, after):
    n = len(srcs)
    lands = [jax.ShapeDtypeStruct((N_DEV,) + (s.shape[1:] if src_by_peer else s.shape), s.dtype) for s in srcs]

    def body(*refs):
        src_refs, land_refs = refs[:n], refs[n:2 * n]
        send_sems, recv_sems = refs[2 * n + 1], refs[2 * n + 2]
        token = refs[-1]
        for start, _ in _push_copies(src_refs, land_refs, send_sems, recv_sems, src_by_peer):
            start.start()
        token[...] = jnp.zeros_like(token)

    hbm = lambda a: pltpu.with_memory_space_constraint(a, pltpu.HBM)
    res = pl.pallas_call(
        body, name=name,
        out_shape=(pltpu.SemaphoreType.DMA((7 * n,)), pltpu.SemaphoreType.DMA((7 * n,)),
                   *[pltpu.HBM(s.shape, s.dtype) for s in srcs], *[pltpu.HBM(s.shape, s.dtype) for s in lands],
                   jax.ShapeDtypeStruct((8, LANE), F32)),
        in_specs=[_HBM] * (2 * n) + [_ANY],
        out_specs=(_SEM, _SEM, *[_HBM] * (2 * n), pl.BlockSpec(memory_space=pltpu.VMEM)),
        input_output_aliases={i: 2 + i for i in range(2 * n)},
        compiler_params=pltpu.CompilerParams(has_side_effects=_EFFECT),
    )(*[hbm(s) for s in srcs], *[hbm(lax.empty(s.shape, s.dtype)) for s in lands], after)
    return res[0], res[1], list(res[2:2 + n]), list(res[2 + n:2 + 2 * n]), res[-1]


def _push_wait(name, send_sems, recv_sems, srcs, lands, src_by_peer, after, first=0):
    n = len(srcs)
    after = list(after) if isinstance(after, (list, tuple)) else [after]

    def body(*refs):
        src_refs, land_refs = refs[:n], refs[n:2 * n]
        s_sems, r_sems = refs[2 * n], refs[2 * n + 1]
        for _, landed in _push_copies(src_refs, land_refs, s_sems, r_sems, src_by_peer, first):
            landed.wait_send()
            landed.wait_recv()

    res = pl.pallas_call(
        body, name=name,
        out_shape=tuple(pltpu.HBM(s.shape, s.dtype) for s in list(srcs) + list(lands)),
        in_specs=[_HBM] * (2 * n) + [_SEM, _SEM] + [_ANY] * len(after),
        out_specs=tuple([_HBM] * (2 * n)),
        input_output_aliases={i: i for i in range(2 * n)},
        compiler_params=pltpu.CompilerParams(has_side_effects=_EFFECT),
    )(*srcs, *lands, send_sems, recv_sems, *after)
    return list(res[:n]), list(res[n:])


_SHARDED = (
    ("meta_tokens", 1, (N_META, D_MODEL)),
    ("w_in", 1, (D_MODEL, IN_COLS)),
    ("w_q_b", 1, (Q_LORA, MLA_HEADS * QK_HEAD)),
    ("w_kv_b", 1, (KV_LORA, MLA_HEADS * (QK_NOPE + V_HEAD))),
    ("dn_conv_w", 1, (DN_CONV, 3 * DN_WIDTH)),
    ("w_out", 0, (2 * DN_WIDTH, D_MODEL)),
    ("w_gate", 1, (D_MODEL, D_FF)),
    ("w_up", 1, (D_MODEL, D_FF)),
    ("ffn_conv_w", 1, (FFN_CONV, D_FF)),
    ("w_down", 0, (D_FF, D_MODEL)),
)
_F32_GATHERED = ("meta_tokens", "dn_conv_w", "ffn_conv_w")
_EARLY = ("w_in", "w_q_b", "w_kv_b")
_LATE = ("w_out", "w_gate", "w_up", "w_down")
_TRANSPOSED = ("w_in", "w_q_b", "w_gate", "w_up")
_REPLICATED = (
    ("attn_norm_w", D_MODEL), ("q_a_norm_w", Q_LORA), ("kv_a_norm_w", KV_LORA), ("q_norm_w", QK_HEAD),
    ("k_norm_w", QK_HEAD), ("mla_out_norm_w", V_HEAD), ("dn_A_log", DN_HEADS), ("dn_dt_bias", DN_HEADS),
    ("dn_out_norm_w", DN_DIM), ("ffn_norm_w", D_MODEL), ("ffn_conv_b", D_FF),
)
_SMALL_BLOCK = (8, 512)


def _local_shape(dim, shape):
    return (shape[0] // N_DEV, shape[1]) if dim == 0 else (shape[0], shape[1] // N_DEV)


def _from_blocks(blocks, dim, shape):
    r, c = shape
    if dim == 0:
        return blocks.reshape(r, c)
    return blocks.reshape(N_DEV, r, c // N_DEV).transpose(1, 0, 2).reshape(r, c)


def _split(flat, sizes):
    out, o = [], 0
    for s in sizes:
        out.append(flat[..., o:o + s])
        o += s
    return out


def kernel(x, meta_tokens, attn_norm_w, w_in, q_a_norm_w, w_q_b, kv_a_norm_w, w_kv_b, q_norm_w, k_norm_w, mla_out_norm_w, dn_conv_w, dn_A_log, dn_dt_bias, dn_out_norm_w, w_out, ffn_norm_w, w_gate, w_up, ffn_conv_w, ffn_conv_b, w_down, loss_target, m_meta_tokens, m_attn_norm_w, m_w_in, m_q_a_norm_w, m_w_q_b, m_kv_a_norm_w, m_w_kv_b, m_q_norm_w, m_k_norm_w, m_mla_out_norm_w, m_dn_conv_w, m_dn_A_log, m_dn_dt_bias, m_dn_out_norm_w, m_w_out, m_ffn_norm_w, m_w_gate, m_w_up, m_ffn_conv_w, m_ffn_conv_b, m_w_down, v_meta_tokens, v_attn_norm_w, v_w_in, v_q_a_norm_w, v_w_q_b, v_kv_a_norm_w, v_w_kv_b, v_q_norm_w, v_k_norm_w, v_mla_out_norm_w, v_dn_conv_w, v_dn_A_log, v_dn_dt_bias, v_dn_out_norm_w, v_w_out, v_ffn_norm_w, v_w_gate, v_w_up, v_ffn_conv_w, v_ffn_conv_b, v_w_down):
    names = [n for n, _, _ in _SHARDED] + [n for n, _ in _REPLICATED]
    given = dict(locals())
    two_d = lambda a: a.reshape(a.shape[-2:])
    view = lambda a, n: two_d(a).T if n in _TRANSPOSED else two_d(a)
    wl = {n: view(given[n], n) for n in names}
    ml = {n: view(given["m_" + n], n) for n in names}
    vl = {n: view(given["v_" + n], n) for n in names}
    out_shapes = {n: given[n].shape for n in names}

    spec = {n: (d, s) for n, d, s in _SHARDED}
    small_sizes = [math.prod(_local_shape(*spec[n])) for n in _F32_GATHERED]

    def small_block(d):
        cat = jnp.concatenate([d[n].reshape(d[n].shape[:-2] + (-1,)) for n in _F32_GATHERED], axis=-1)
        pad = [(0, 0)] * (cat.ndim - 1) + [(0, math.prod(_SMALL_BLOCK) - cat.shape[-1])]
        return jnp.pad(cat, pad).reshape(cat.shape[:-1] + _SMALL_BLOCK)

    def shard(n):
        return wl[n].astype(_MXU)

    def from_slots(n, blocks):
        d, s = spec[n]
        if d == 0 or n in _TRANSPOSED:
            return blocks.reshape(-1, blocks.shape[-1])
        return blocks.transpose(1, 0, 2).reshape(s)

    my_id = 4 * lax.axis_index("x") + 2 * lax.axis_index("y") + lax.axis_index("c")
    got = _all_gather_many("gather_early", [shard(n) for n in _EARLY] + [small_block(wl)])
    full = {n: a for n, a in wl.items() if n not in _LATE}
    for n, blocks in zip(_EARLY, got):
        full[n] = from_slots(n, blocks)
    for n, p in zip(_F32_GATHERED, _split(got[-1].reshape(N_DEV, -1), small_sizes)):
        full[n] = _from_blocks(p, *spec[n])
    late_own = [shard(n) for n in _LATE]
    l_send, l_recv, l_src, l_land, token = _push_start("gather_late_start", late_own, False, got[-1])

    def late_weights(after, names):
        first = _LATE.index(names[0])
        sl = slice(first, first + len(names))
        _, lands = _push_wait("gather_late_wait_" + names[0], l_send, l_recv, l_src[sl], l_land[sl], False,
                              after, first)
        out = {}
        for n, land, own in zip(names, lands, late_own[sl]):
            out[n] = from_slots(n, lax.dynamic_update_slice(land, own[None], (my_id, 0, 0))).astype(_MXU)
        return out

    def dest_blocks(n, a):
        d, s = spec[n]
        r, c = _local_shape(d, s)
        if n in _TRANSPOSED:
            return a.reshape(N_DEV, c, r)
        return a.reshape(N_DEV, r, c) if d == 0 else a.reshape(r, N_DEV, c).transpose(1, 0, 2)

    pushed = []

    def grads_ready(g, names):
        nat = _grads_to_natural({n: g[n] for n in names})
        blocks = [dest_blocks(n, nat[n]).astype(_MXU) for n in names]
        sends, recvs, srcs, lands, tok = _push_start("rs_" + names[0] + "_start", blocks, True, token)
        pushed.append((names, sends, recvs, srcs, lands))
        return tok

    seq = x.shape[1]
    tp = ROW0 + seq
    h0 = jnp.concatenate([jnp.zeros((PAD, D_MODEL), F32), full["meta_tokens"], x[0]], axis=0)
    tgt = jnp.concatenate([jnp.zeros((ROW0, D_MODEL), F32), loss_target[0]], axis=0)
    loss, dh0, raw = _local_step(h0, tgt, _prepare(full, tp), token, late_weights, grads_ready)
    g = _grads_to_natural(raw)
    g["meta_tokens"] = dh0[PAD:ROW0]
    grad_x = dh0[ROW0:][None]

    big = [{}, {}, {}, {}]
    rep_names = [n for n, _ in _REPLICATED]
    raw_key = {"dn_A_log": "alog_b", "dn_dt_bias": "dtb_b"}
    pieces = [raw[raw_key.get(n, n)] for n in rep_names] + [loss]
    pieces += [g[n].reshape(1, -1) for n in _F32_GATHERED]
    widths = [p.shape[1] for p in pieces]
    offs = [sum(widths[:k]) for k in range(len(widths))]
    cat = jnp.concatenate(pieces, axis=1)
    cols = -(-cat.shape[1] // (8 * LANE)) * LANE
    mine = jnp.pad(cat, ((0, 0), (0, 8 * cols - cat.shape[1]))).reshape(8, cols)
    everyone = _all_gather("gather_small_grads", mine)
    total = _sum_parts("sum_small_grads", [(everyone, d) for d in range(N_DEV)]).reshape(1, 8 * cols)
    tot = {n: total[0, o:o + wd] for n, o, wd in zip(rep_names + ["loss"] + list(_F32_GATHERED), offs, widths)}
    items = [(o, size, n in raw_key) for (n, size), o in zip(_REPLICATED, offs)]
    sm = _adam_vectors("adam_replicated", total, items, [wl[n] for n in rep_names], [ml[n] for n in rep_names],
                       [vl[n] for n in rep_names])
    sm = [dict(zip(rep_names, kind)) for kind in sm]
    mine_of = {}
    for n in _F32_GATHERED:
        d, s = spec[n]
        r, c = _local_shape(d, s)
        mine_of[n] = lax.dynamic_slice(tot[n].reshape(s), (0, my_id * c), (r, c))
    res = _adam_arrays("adam_small_sharded", [mine_of[n] for n in _F32_GATHERED], [wl[n] for n in _F32_GATHERED],
                       [ml[n] for n in _F32_GATHERED], [vl[n] for n in _F32_GATHERED])
    for kind, arrays in enumerate([[mine_of[n] for n in _F32_GATHERED]] + res):
        big[kind].update(zip(_F32_GATHERED, arrays))

    for names, sends, recvs, srcs, lands in pushed:
        srcs, lands = _push_wait("rs_" + names[0] + "_wait", sends, recvs, srcs, lands, True, dh0)
        for n, src, land in zip(names, srcs, lands):
            parts = [(src, my_id)] + [(land, my_id ^ f) for f in range(1, N_DEV)]
            for kind, a in enumerate(_adam("adam_" + n, parts, wl[n], ml[n], vl[n])):
                big[kind][n] = a

    outs = [tot["loss"][0], grad_x]
    for kind in range(4):
        for n in ("meta_tokens", "attn_norm_w", "w_in", "q_a_norm_w", "w_q_b", "kv_a_norm_w", "w_kv_b", "q_norm_w",
                  "k_norm_w", "mla_out_norm_w", "dn_conv_w", "dn_A_log", "dn_dt_bias", "dn_out_norm_w", "w_out",
                  "ffn_norm_w", "w_gate", "w_up", "ffn_conv_w", "ffn_conv_b", "w_down"):
            src = big[kind] if n in big[kind] else sm[kind]
            a = src[n].T if n in _TRANSPOSED else src[n]
            outs.append(a.reshape(out_shapes[n]))
    return tuple(outs)
```

```python
import functools
import math

import jax
import jax.numpy as jnp
from jax import lax
from jax.experimental import pallas as pl
from jax.experimental.pallas import tpu as pltpu

F32 = jnp.float32
_MXU = jnp.bfloat16
_HI = lax.Precision.HIGHEST

D_MODEL = 1024
N_META = 16
PAD = 112
ROW0 = PAD + N_META
MLA_HEADS = 4
QK_NOPE = 128
QK_ROPE = 64
QK_HEAD = QK_NOPE + QK_ROPE
V_HEAD = 128
Q_LORA = 256
KV_LORA = 256
ROPE_THETA = 10000.0
DN_HEADS = 4
DN_DIM = 128
DN_WIDTH = DN_HEADS * DN_DIM
DN_CONV = 4
DN_CHUNK = 64
GDN_SUB_CHUNKS = 2
D_FF = 2816
FFN_CONV = 3
EPS = 1e-6
HP = 256
C_Z = 1536
C_QL = 2048
C_KVL = 2304
C_KPE = 2560
C_AB = 2688
IN_COLS = 2632

ADAM_LR = 0.001
ADAM_B1 = 0.9
ADAM_B2 = 0.999
ADAM_EPS = 1e-08
ADAM_WD = 0.01
ADAM_STEP = 10

N_DEV = 8
TM = 128
LANE = 128
VMEM_LIMIT = 56 * 1024 * 1024
NEG = -1e30


def _dot(a, b, dims, hp=False):
    if hp:
        return lax.dot_general(a.astype(F32), b.astype(F32), (dims, ((), ())),
                               precision=lax.Precision.HIGH if hp == "3x" else _HI, preferred_element_type=F32)
    return lax.dot_general(a.astype(_MXU), b.astype(_MXU), (dims, ((), ())),
                           preferred_element_type=F32)


def _nn(a, b, hp=False):
    return _dot(a, b, ((1,), (0,)), hp)


def _nt(a, b, hp=False):
    return _dot(a, b, ((1,), (1,)), hp)


def _tn(a, b, hp=False):
    return _dot(a, b, ((0,), (0,)), hp)


def _sigmoid(x):
    return 1.0 / (1.0 + jnp.exp(-x))


def _rms_fwd(x, w, n):
    r = lax.rsqrt(jnp.sum(x * x, axis=-1, keepdims=True) * (1.0 / n) + EPS)
    return x * r * w, r


def _rms_bwd(x, w, dy, n):
    r = lax.rsqrt(jnp.sum(x * x, axis=-1, keepdims=True) * (1.0 / n) + EPS)
    xh = x * r
    gy = dy * w
    dx = r * (gy - xh * (jnp.sum(gy * xh, axis=-1, keepdims=True) * (1.0 / n)))
    return dx, dy * xh


def _rowsum(x):
    return jnp.sum(x, axis=0, keepdims=True)


def _row_ids(i, tm):
    return i * tm + lax.broadcasted_iota(jnp.int32, (tm, 1), 0)


def _shift_down(ext, s, tm):
    if s == 0:
        return ext[8:8 + tm]
    return pltpu.roll(ext, s, 0)[8:8 + tm]


def _shift_up(ext, s, tm):
    if s == 0:
        return ext[0:tm]
    return pltpu.roll(ext, tm + 8 - s, 0)[0:tm]


def _conv_fwd(x, halo_prev, w, width):
    tm = x.shape[0]
    ext = jnp.concatenate([halo_prev, x], axis=0)
    y = None
    for j in range(width):
        t = w[j:j + 1, :] * _shift_down(ext, width - 1 - j, tm)
        y = t if y is None else y + t
    return y


def _conv_bwd_x(dy, halo_next, w, width):
    tm = dy.shape[0]
    ext = jnp.concatenate([dy, halo_next], axis=0)
    dx = None
    for j in range(width):
        t = w[j:j + 1, :] * _shift_up(ext, width - 1 - j, tm)
        dx = t if dx is None else dx + t
    return dx


def _conv_bwd_w(dy, x, halo_prev, width):
    tm = dy.shape[0]
    ext = jnp.concatenate([halo_prev, x], axis=0)
    rows = [_rowsum(dy * _shift_down(ext, width - 1 - j, tm)) for j in range(width)]
    rows += [jnp.zeros_like(rows[0])] * (8 - width)
    return jnp.concatenate(rows, axis=0)


def _softplus(x):
    e = jnp.exp(-jnp.abs(x))
    u = 1.0 + e
    l1p = jnp.where(u == 1.0, e, jnp.log(u) * e / jnp.where(u == 1.0, 1.0, u - 1.0))
    return jnp.maximum(x, 0.0) + l1p


def _swap_halves(x):
    lane = lax.broadcasted_iota(jnp.int32, x.shape, 1)
    return jnp.where(lane < 32, pltpu.roll(x, 96, 1), jnp.where(lane < 64, pltpu.roll(x, 32, 1), 0.0))


class _In:
    def __init__(self, arr, width=None, cb=0, kind="cur"):
        self.arr, self.kind = arr, kind
        self.width = arr.shape[1] if width is None else width
        self.cb = cb


def _whole_spec(x):
    return pl.BlockSpec(x.shape, lambda i, nd=x.ndim: (0,) * nd, pipeline_mode=pl.Buffered(1))


def _tile_spec(t, tm, tp):
    r8 = tm // 8
    if t.kind == "cur":
        return pl.BlockSpec((tm, t.width), lambda i, cb=t.cb: (i, cb))
    if t.kind == "prev":
        return pl.BlockSpec((8, t.width), lambda i, cb=t.cb: (jnp.maximum(i * r8 - 1, 0), cb))
    return pl.BlockSpec((8, t.width), lambda i, cb=t.cb: (jnp.minimum((i + 1) * r8, tp // 8 - 1), cb))


def _rows(name, fn, tiled, full, outs, accs=(), tm=TM):
    tp = tiled[0].arr.shape[0]
    nt = tp // tm
    n_in = len(tiled) + len(full)
    n_out = len(outs)

    def body(*refs):
        i = pl.program_id(0)
        vals = [r[...] for r in refs[:n_in]]
        o_t, o_a = fn(i, *vals)
        for r, v in zip(refs[n_in:n_in + n_out], o_t):
            r[...] = v.astype(r.dtype)
        for r, v in zip(refs[n_in + n_out:], o_a):
            @pl.when(i == 0)
            def _():
                r[...] = v

            @pl.when(i > 0)
            def _():
                r[...] += v

    in_specs = [_tile_spec(t, tm, tp) for t in tiled]
    in_specs += [pl.BlockSpec(a.shape, lambda i, nd=a.ndim: (0,) * nd) for a in full]
    out_specs = [pl.BlockSpec((tm, w), lambda i: (i, 0)) for w, _ in outs]
    out_specs += [pl.BlockSpec((r, w), lambda i: (0, 0)) for r, w in accs]
    out_shape = [jax.ShapeDtypeStruct((tp, w), dt) for w, dt in outs]
    out_shape += [jax.ShapeDtypeStruct((r, w), F32) for r, w in accs]
    res = pl.pallas_call(
        body, name=name, grid=(nt,), in_specs=in_specs, out_specs=out_specs, out_shape=out_shape,
        compiler_params=pltpu.CompilerParams(dimension_semantics=("arbitrary",), vmem_limit_bytes=VMEM_LIMIT),
    )(*[t.arr for t in tiled], *full)
    return res


def _pick(n, cap, mult):
    best = None
    for d in range(mult, min(n, cap) + 1, mult):
        if n % d == 0:
            best = d
    assert best is not None, (n, cap, mult)
    return best


_ANY_SPEC = pl.BlockSpec(memory_space=pl.ANY)


def _mm(name, a, b, mode, out_dtype=F32, resid=None, after=None):
    if mode == "tn":
        m, k = a.shape
        n = b.shape[1]
        tk = _pick(k, 512, 128)
        tn = _pick(n, 1408, 128)

        def body_tn(a_ref, b_ref, o_ref):
            o_ref[...] = _tn(a_ref[...], b_ref[...]).astype(o_ref.dtype)

        return pl.pallas_call(
            body_tn, name=name, grid=(n // tn, k // tk),
            in_specs=[pl.BlockSpec((m, tk), lambda j, p: (0, p)),
                      pl.BlockSpec((m, tn), lambda j, p: (0, j))],
            out_specs=pl.BlockSpec((tk, tn), lambda j, p: (p, j)),
            out_shape=jax.ShapeDtypeStruct((k, n), out_dtype),
            compiler_params=pltpu.CompilerParams(
                dimension_semantics=("parallel", "parallel"), vmem_limit_bytes=VMEM_LIMIT),
        )(a, b)

    m, k = a.shape
    n = b.shape[1] if mode == "nn" else b.shape[0]
    tn = _pick(n, 1408, 128)
    tm = _pick(m, 1152, 16)
    dotf = _nn if mode == "nn" else _nt

    def body(*refs):
        a_ref, b_ref, o_ref = refs[0], refs[1], refs[-1]
        acc = dotf(a_ref[...], b_ref[...])
        if resid is not None:
            acc = refs[2][...] + acc
        o_ref[...] = acc.astype(o_ref.dtype)

    b_spec = (pl.BlockSpec((k, tn), lambda j, i: (0, j)) if mode == "nn"
              else pl.BlockSpec((tn, k), lambda j, i: (j, 0)))
    in_specs = [pl.BlockSpec((tm, k), lambda j, i: (i, 0)), b_spec]
    args = [a, b]
    if resid is not None:
        in_specs.append(pl.BlockSpec((tm, tn), lambda j, i: (i, j)))
        args.append(resid)
    if after is not None:
        in_specs.append(_ANY_SPEC)
        args.append(after)
    return pl.pallas_call(
        body, name=name, grid=(n // tn, m // tm), in_specs=in_specs,
        out_specs=pl.BlockSpec((tm, tn), lambda j, i: (i, j)),
        out_shape=jax.ShapeDtypeStruct((m, n), out_dtype),
        compiler_params=pltpu.CompilerParams(
            dimension_semantics=("parallel", "parallel"), vmem_limit_bytes=VMEM_LIMIT),
    )(*args)


def _mm_tn2(name, a1, a2, b, out_dtype=F32):
    m, k = a1.shape
    n = b.shape[1]
    tk = _pick(k, 512, 128)

    def body(a1_ref, a2_ref, b_ref, o1_ref, o2_ref):
        bb = b_ref[...]
        o1_ref[...] = _tn(a1_ref[...], bb).astype(o1_ref.dtype)
        o2_ref[...] = _tn(a2_ref[...], bb).astype(o2_ref.dtype)

    a_spec = pl.BlockSpec((m, tk), lambda p: (0, p))
    o_spec = pl.BlockSpec((tk, n), lambda p: (p, 0))
    return pl.pallas_call(
        body, name=name, grid=(k // tk,),
        in_specs=[a_spec, a_spec, pl.BlockSpec((m, n), lambda p: (0, 0))],
        out_specs=[o_spec, o_spec], out_shape=[jax.ShapeDtypeStruct((k, n), out_dtype)] * 2,
        compiler_params=pltpu.CompilerParams(dimension_semantics=("parallel",), vmem_limit_bytes=VMEM_LIMIT),
    )(a1, a2, b)


def _norm_mm(name, x, norm_w, b, mode="nt", x_cb=0, after=None):
    m = x.shape[0]
    k = norm_w.shape[1]
    n = b.shape[0] if mode == "nt" else b.shape[1]
    tn = _pick(n, 1408, 128)
    tm = _pick(m, 1152, 16)
    dotf = _nt if mode == "nt" else _nn
    extra = [] if after is None else [after]

    def body(x_ref, w_ref, b_ref, *rest):
        o_ref, u_ref = rest[-2:]

        @pl.when(pl.program_id(1) == 0)
        def _():
            u_ref[...] = _rms_fwd(x_ref[...], w_ref[...], k)[0].astype(u_ref.dtype)

        o_ref[...] = dotf(u_ref[...], b_ref[...])

    b_spec = (pl.BlockSpec((tn, k), lambda i, j: (j, 0)) if mode == "nt"
              else pl.BlockSpec((k, tn), lambda i, j: (0, j)))
    return pl.pallas_call(
        body, name=name, grid=(m // tm, n // tn),
        in_specs=[pl.BlockSpec((tm, k), lambda i, j: (i, x_cb)), pl.BlockSpec((1, k), lambda i, j: (0, 0)),
                  b_spec] + [_ANY_SPEC] * len(extra),
        out_specs=[pl.BlockSpec((tm, tn), lambda i, j: (i, j)), pl.BlockSpec((tm, k), lambda i, j: (i, 0))],
        out_shape=[jax.ShapeDtypeStruct((m, n), F32), jax.ShapeDtypeStruct((m, k), _MXU)],
        compiler_params=pltpu.CompilerParams(
            dimension_semantics=("arbitrary", "arbitrary"), vmem_limit_bytes=VMEM_LIMIT),
    )(x, norm_w, b, *extra)


def _pro_mm(name, fn, tiled, full, k, b, resid):
    m = resid.shape[0]
    n = b.shape[1]
    tm = _pick(m, 576, 16)
    n_in = len(tiled) + len(full)

    def body(*refs):
        i = pl.program_id(0)
        u = fn(i, *[r[...] for r in refs[:n_in]]).astype(_MXU)
        b_ref, r_ref, o_ref, u_ref = refs[n_in:]
        u_ref[...] = u
        o_ref[...] = r_ref[...] + _nn(u, b_ref[...])

    row = lambda w: pl.BlockSpec((tm, w), lambda i: (i, 0))
    in_specs = [_tile_spec(t, tm, m) for t in tiled]
    in_specs += [_whole_spec(x) for x in full] + [_whole_spec(b), row(n)]
    return pl.pallas_call(
        body, name=name, grid=(m // tm,), in_specs=in_specs, out_specs=[row(n), row(k)],
        out_shape=[jax.ShapeDtypeStruct((m, n), F32), jax.ShapeDtypeStruct((m, k), _MXU)],
        compiler_params=pltpu.CompilerParams(dimension_semantics=("parallel",), vmem_limit_bytes=VMEM_LIMIT),
    )(*[t.arr for t in tiled], *full, b, resid)


def _ffn_in(h2, norm_w, w_gate_t, w_up_t, conv_w8, conv_b):
    m, k = h2.shape
    n = w_gate_t.shape[0]
    tm = _pick(m, 288, 16)

    def body(x_ref, xp_ref, nw_ref, wg_ref, wu_ref, cw_ref, cb_ref, hn_ref, gp_ref, up_ref, act_ref):
        i = pl.program_id(0)
        nw = nw_ref[...]
        hn = _rms_fwd(x_ref[...], nw, k)[0].astype(_MXU)
        hn_prev = _rms_fwd(xp_ref[...], nw, k)[0].astype(_MXU)
        wg = wg_ref[...]
        gp = _nt(hn, wg)
        gp_prev = jnp.where(i > 0, _nt(hn_prev, wg), 0.0)
        up = _nt(hn, wu_ref[...])
        gate = _conv_fwd(gp, gp_prev, cw_ref[...], FFN_CONV) + cb_ref[...]
        hn_ref[...] = hn
        gp_ref[...] = gp
        up_ref[...] = up
        act_ref[...] = (_silu_parts(gate)[0] * up).astype(act_ref.dtype)

    row = lambda w: pl.BlockSpec((tm, w), lambda i: (i, 0))
    r8 = tm // 8
    return pl.pallas_call(
        body, name="ffn_in", grid=(m // tm,),
        in_specs=[row(k), pl.BlockSpec((8, k), lambda i: (jnp.maximum(i * r8 - 1, 0), 0)), _whole_spec(norm_w),
                  _whole_spec(w_gate_t), _whole_spec(w_up_t), _whole_spec(conv_w8), _whole_spec(conv_b)],
        out_specs=[row(k), row(n), row(n), row(n)],
        out_shape=[jax.ShapeDtypeStruct((m, k), _MXU), jax.ShapeDtypeStruct((m, n), F32),
                   jax.ShapeDtypeStruct((m, n), F32), jax.ShapeDtypeStruct((m, n), _MXU)],
        compiler_params=pltpu.CompilerParams(dimension_semantics=("parallel",), vmem_limit_bytes=VMEM_LIMIT),
    )(h2, h2, norm_w, w_gate_t, w_up_t, conv_w8, conv_b)


def _mm_rows(name, a, b, mode, fn, tiled, full, outs, accs=(), tm_cap=576):
    a_list = list(a) if isinstance(a, (list, tuple)) else [a]
    b_list = list(b) if isinstance(b, (list, tuple)) else [b]
    na = len(a_list)
    m = a_list[0].shape[0]
    tm = _pick(m, tm_cap, 16)
    dotf = _nn if mode == "nn" else _nt
    n_in = len(tiled) + len(full)
    n_out = len(outs)
    first = 2 * na

    def body(*refs):
        i = pl.program_id(0)
        vals = [r[...] for r in refs[first:first + n_in]]
        acc = dotf(refs[0][...], refs[na][...])
        for p in range(1, na):
            acc = acc + dotf(refs[p][...], refs[na + p][...])
        o_t, o_a = fn(i, acc, *vals)
        for r, v in zip(refs[first + n_in:first + n_in + n_out], o_t):
            r[...] = v.astype(r.dtype)
        for r, v in zip(refs[first + n_in + n_out:], o_a):
            @pl.when(i == 0)
            def _():
                r[...] = v

            @pl.when(i > 0)
            def _():
                r[...] += v

    whole = lambda x: pl.BlockSpec(x.shape, lambda i, nd=x.ndim: (0,) * nd)
    in_specs = [pl.BlockSpec((tm, x.shape[1]), lambda i: (i, 0)) for x in a_list] + [_whole_spec(x) for x in b_list]
    in_specs += [_tile_spec(t, tm, m) for t in tiled]
    in_specs += [whole(x) for x in full]
    out_specs = [pl.BlockSpec((tm, w), lambda i: (i, 0)) for w, _ in outs]
    out_specs += [pl.BlockSpec((r, w), lambda i: (0, 0)) for r, w in accs]
    out_shape = [jax.ShapeDtypeStruct((m, w), dt) for w, dt in outs]
    out_shape += [jax.ShapeDtypeStruct((r, w), F32) for r, w in accs]
    return pl.pallas_call(
        body, name=name, grid=(m // tm,), in_specs=in_specs, out_specs=out_specs, out_shape=out_shape,
        compiler_params=pltpu.CompilerParams(dimension_semantics=("arbitrary",), vmem_limit_bytes=VMEM_LIMIT),
    )(*a_list, *b_list, *[t.arr for t in tiled], *full)


ATTN_Q_TILES = 4


def _attn_probs(q, k, row0):
    tq, tp = q.shape[0], k.shape[0]
    s = _nt(q, k) * (1.0 / math.sqrt(QK_HEAD))
    row = row0 + lax.broadcasted_iota(jnp.int32, (tq, tp), 0)
    col = lax.broadcasted_iota(jnp.int32, (tq, tp), 1)
    ok = (col <= row) & (col >= PAD)
    s = jnp.where(ok, s, NEG)
    m = jnp.max(s, axis=-1, keepdims=True)
    e = jnp.exp(s - m)
    return e * (1.0 / jnp.sum(e, axis=-1, keepdims=True))


def _attn_fwd(q, k, v):
    tp = q.shape[0]
    tq = tp // ATTN_Q_TILES

    def body(q_ref, k_ref, v_ref, o_ref):
        for i in range(ATTN_Q_TILES):
            rows = slice(i * tq, (i + 1) * tq)
            keys = slice(0, (i + 1) * tq)
            p = _attn_probs(q_ref[rows, :], k_ref[keys, :], i * tq)
            o_ref[rows, :] = _nn(p, v_ref[keys, :])

    return pl.pallas_call(
        body, name="attn_fwd", grid=(MLA_HEADS,),
        in_specs=[pl.BlockSpec((tp, HP), lambda h: (0, h)),
                  pl.BlockSpec((tp, HP), lambda h: (0, h)),
                  pl.BlockSpec((tp, V_HEAD), lambda h: (0, h))],
        out_specs=pl.BlockSpec((tp, V_HEAD), lambda h: (0, h)),
        out_shape=jax.ShapeDtypeStruct((tp, MLA_HEADS * V_HEAD), F32),
        compiler_params=pltpu.CompilerParams(dimension_semantics=("parallel",), vmem_limit_bytes=VMEM_LIMIT),
    )(q, k, v)


def _attn_bwd(q, k, v, do):
    tp = q.shape[0]
    tq = tp // ATTN_Q_TILES

    def body(q_ref, k_ref, v_ref, do_ref, dq_ref, dk_ref, dv_ref):
        for i in reversed(range(ATTN_Q_TILES)):
            rows = slice(i * tq, (i + 1) * tq)
            keys = slice(0, (i + 1) * tq)
            qb = q_ref[rows, :]
            kk = k_ref[keys, :]
            dob = do_ref[rows, :]
            p = _attn_probs(qb, kk, i * tq)
            dp = _nt(dob, v_ref[keys, :])
            delta = jnp.sum(p * dp, axis=-1, keepdims=True)
            ds = p * (dp - delta) * (1.0 / math.sqrt(QK_HEAD))
            dq_ref[rows, :] = _nn(ds, kk)
            if i == ATTN_Q_TILES - 1:
                dk_ref[...] = _tn(ds, qb)
                dv_ref[...] = _tn(p, dob)
            else:
                dk_ref[keys, :] += _tn(ds, qb)
                dv_ref[keys, :] += _tn(p, dob)

    full = lambda w: pl.BlockSpec((tp, w), lambda h: (0, h))
    return pl.pallas_call(
        body, name="attn_bwd", grid=(MLA_HEADS,),
        in_specs=[full(HP), full(HP), full(V_HEAD), full(V_HEAD)],
        out_specs=[full(HP), full(HP), full(V_HEAD)],
        out_shape=[jax.ShapeDtypeStruct((tp, MLA_HEADS * HP), F32),
                   jax.ShapeDtypeStruct((tp, MLA_HEADS * HP), F32),
                   jax.ShapeDtypeStruct((tp, MLA_HEADS * V_HEAD), F32)],
        compiler_params=pltpu.CompilerParams(dimension_semantics=("parallel",), vmem_limit_bytes=VMEM_LIMIT),
    )(q, k, v, do)


def _gdn_consts():
    c = DN_CHUNK
    r = lax.broadcasted_iota(jnp.int32, (c, c), 0)
    cc = lax.broadcasted_iota(jnp.int32, (c, c), 1)
    incl = r >= cc
    strict = r > cc
    return incl, strict


def _cumsum_rows(x, reverse=False):
    c = x.shape[0]
    row = lax.broadcasted_iota(jnp.int32, x.shape, 0)
    s = 1
    while s < c:
        if reverse:
            x = x + jnp.where(row < c - s, pltpu.roll(x, c - s, 0), 0.0)
        else:
            x = x + jnp.where(row >= s, pltpu.roll(x, s, 0), 0.0)
        s *= 2
    return x


def _each(fn, *lists):
    return [fn(*a) for a in zip(*lists)]


def _interleave(chains):
    chains = list(chains)
    while chains:
        for ch in list(chains):
            try:
                next(ch)
            except StopIteration:
                chains.remove(ch)


def _gdn_chunk_common(q_ref, k_ref, v_ref, g_ref, b_ref):
    c = DN_CHUNK
    incl, strict = _gdn_consts()
    sls = [(slice(c * sub, c * (sub + 1)), slice(DN_DIM * h, DN_DIM * (h + 1)))
           for sub in range(GDN_SUB_CHUNKS) for h in range(DN_HEADS)]
    q = [q_ref[sl] * (1.0 / math.sqrt(DN_DIM)) for sl in sls]
    k = [k_ref[sl] for sl in sls]
    v = [v_ref[sl] for sl in sls]
    g = [g_ref[sl] for sl in sls]
    beta = [b_ref[sl] for sl in sls]
    gc = [_cumsum_rows(x) for x in g]
    grow = [x.T[:c, :] for x in gc]
    kb = _each(jnp.multiply, k, beta)
    kk = _each(_nt, kb, k)
    qk = _each(_nt, q, k)
    gam = [jnp.exp(x) for x in gc]
    g_last = [_rowsum(x) for x in g]
    dm = [jnp.exp(jnp.where(incl, x[:, :c] - y, NEG)) for x, y in zip(gc, grow)]
    vb = _each(jnp.multiply, v, beta)
    kbg = _each(jnp.multiply, kb, gam)
    ek = [jnp.exp(x - y) for x, y in zip(g_last, gc)]
    kd = _each(jnp.multiply, k, ek)
    return dict(q=q, k=k, v=v, beta=beta, gc=gc, gam=gam, g_last=g_last, dm=dm, kb=kb, vb=vb,
                kbg=kbg, kk=kk, ek=ek, kd=kd, qk=qk, incl=incl, strict=strict, sls=sls)


def _gdn_fwd(proj, conv_w8, alog, dtb):
    tp = proj.shape[0]
    c = DN_CHUNK
    nch = tp // c
    blk = GDN_SUB_CHUNKS * c

    def body(x_ref, xp_ref, ab_ref, w8_ref, alog_ref, dtb_ref,
             o_ref, s_ref, t_ref, q_ref, k_ref, v_ref, g_ref, b_ref, s_scr):
        @pl.when(pl.program_id(0) == 0)
        def _():
            s_scr[...] = jnp.zeros_like(s_scr)

        staged, _ = _f_gdn_prep(pl.program_id(0), x_ref[...], xp_ref[...], ab_ref[...], w8_ref[...],
                                alog_ref[...], dtb_ref[...])
        for ref, val in zip((q_ref, k_ref, v_ref, g_ref, b_ref), staged):
            ref[...] = val
        eye = (lax.broadcasted_iota(jnp.int32, (c, c), 0) == lax.broadcasted_iota(jnp.int32, (c, c), 1)).astype(F32)
        x = _gdn_chunk_common(q_ref, k_ref, v_ref, g_ref, b_ref)
        heads = range(DN_HEADS)
        bp = [-jnp.where(x["strict"], kk * dm, 0.0) for kk, dm in zip(x["kk"], x["dm"])]
        t = [eye + b for b in bp]
        for _ in range(5):
            bp = [_nn(b, b, hp="3x") for b in bp]
            t = [tt + _nn(tt, b, hp="3x") for tt, b in zip(t, bp)]
        u = _each(_nn, t, x["vb"])
        w = _each(_nn, t, x["kbg"])
        qg = _each(jnp.multiply, x["q"], x["gam"])
        mqk = _each(jnp.multiply, x["qk"], x["dm"])
        s = [s_scr[h] for h in heads]
        for sub in range(GDN_SUB_CHUNKS):
            e = [DN_HEADS * sub + h for h in heads]
            v_new = [u[i] - _nn(w[i], s[h]) for h, i in zip(heads, e)]
            o = [_nn(qg[i], s[h]) + _nn(mqk[i], v_new[h]) for h, i in zip(heads, e)]
            s_new = [s[h] * jnp.exp(x["g_last"][i]) + _tn(x["kd"][i], v_new[h]) for h, i in zip(heads, e)]
            for h, i in zip(heads, e):
                s_ref[h, sub] = s[h]
                t_ref[h, sub] = t[i]
                o_ref[x["sls"][i]] = o[h]
            s = s_new
        for h in heads:
            s_scr[h] = s[h]

    sub = GDN_SUB_CHUNKS
    rb = lambda n: (n, 0)
    rows = pl.BlockSpec((blk, DN_WIDTH), rb)
    whole = lambda a: pl.BlockSpec(a.shape, lambda n: (0, 0))
    return pl.pallas_call(
        body, name="gdn_fwd", grid=(nch // sub,),
        in_specs=[pl.BlockSpec((blk, 3 * DN_WIDTH), rb),
                  pl.BlockSpec((8, 3 * DN_WIDTH), lambda n: (jnp.maximum(n * (blk // 8) - 1, 0), 0)),
                  pl.BlockSpec((blk, LANE), lambda n: (n, C_AB // LANE)),
                  whole(conv_w8), whole(alog), whole(dtb)],
        out_specs=[rows,
                   pl.BlockSpec((DN_HEADS, sub, DN_DIM, DN_DIM), lambda n: (0, n, 0, 0)),
                   pl.BlockSpec((DN_HEADS, sub, c, c), lambda n: (0, n, 0, 0))] + [rows] * 5,
        out_shape=[jax.ShapeDtypeStruct((tp, DN_WIDTH), F32),
                   jax.ShapeDtypeStruct((DN_HEADS, nch, DN_DIM, DN_DIM), F32),
                   jax.ShapeDtypeStruct((DN_HEADS, nch, c, c), F32)] + [jax.ShapeDtypeStruct((tp, DN_WIDTH), F32)] * 5,
        scratch_shapes=[pltpu.VMEM((DN_HEADS, DN_DIM, DN_DIM), F32)],
        compiler_params=pltpu.CompilerParams(dimension_semantics=("arbitrary",), vmem_limit_bytes=VMEM_LIMIT),
    )(proj, proj, proj, conv_w8, alog, dtb)


def _gdn_bwd(q, k, v, g, beta, s_all, t_all, do, proj, conv_w8, alog, dtb, after):
    tp = q.shape[0]
    c = DN_CHUNK
    nch = tp // c
    nblk = nch // GDN_SUB_CHUNKS
    blk = GDN_SUB_CHUNKS * c

    def body(q_ref, k_ref, v_ref, g_ref, b_ref, s_ref, t_ref, do_ref, x_ref, xp_ref, xn_ref, ab_ref,
             w8_ref, alog_ref, dtb_ref, _after_ref, dqkv_ref, dab_ref, dcw_ref, dalog_ref, ddtb_ref,
             ds_scr, dq_ref, dk_ref, dv_ref, dg_ref, db_ref, nxt_scr):
        step = pl.program_id(0)

        @pl.when(step == 0)
        def _():
            ds_scr[...] = jnp.zeros_like(ds_scr)
            nxt_scr[...] = jnp.zeros_like(nxt_scr)

        xs = _gdn_chunk_common(q_ref, k_ref, v_ref, g_ref, b_ref)

        ds_state = [ds_scr[h] for h in range(DN_HEADS)]

        def chain(sub, h):
            e = DN_HEADS * sub + h
            x = {key: (val[e] if isinstance(val, list) else val) for key, val in xs.items()}
            sl = x["sls"]
            qs, kx, vx, beta_, gam, dm = x["q"], x["k"], x["v"], x["beta"], x["gam"], x["dm"]
            kb, vb, kbg, kd, ek = x["kb"], x["vb"], x["kbg"], x["kd"], x["ek"]
            t = t_ref[h, sub]
            s = s_ref[h, sub]
            dsn = ds_state[h]
            dob = do_ref[sl]
            eg_last = jnp.exp(x["g_last"])
            u = _nn(t, vb)
            w = _nn(t, kbg)
            mqk = x["qk"] * dm
            qd = qs * gam
            dqd = _nt(dob, s)
            dkd_pre = _nn(kd, dsn)
            yield
            v_new = u - _nn(w, s)
            dv_new = _tn(mqk, dob) + dkd_pre
            dq = dqd * gam
            dgam = jnp.sum(dqd * qs, axis=1, keepdims=True)
            yield
            ds_state[h] = _tn(qd, dob) + eg_last * dsn - _tn(w, dv_new)
            dmm = jnp.where(x["incl"], _nt(dob, v_new), 0.0)
            dkd = _nt(v_new, dsn)
            dw = -_nt(dv_new, s)
            dvb = _tn(t, dv_new)
            dt = _nt(dv_new, vb)
            yield
            dqk = dmm * dm
            e_mat = dmm * mqk
            dq = dq + _nn(dqk, kx)
            dk = _tn(dqk, qs) + dkd * ek
            e1 = jnp.sum(dkd * kd, axis=1, keepdims=True)
            dgc = -e1
            dg_last = jnp.sum(e1) + eg_last * jnp.sum(s * dsn)
            dt = dt + _nt(dw, kbg)
            dkbg = _tn(t, dw)
            yield
            tdt = _tn(t, dt, hp="3x")
            yield
            da = jnp.where(x["strict"], -_nt(tdt, t, hp="3x"), 0.0)
            yield
            dkk = da * dm
            e_mat = e_mat + da * x["kk"] * dm
            dkb = _nn(dkk, kx) + dkbg * gam
            dk = dk + _tn(dkk, kb)
            dgam = dgam + jnp.sum(dkbg * kb, axis=1, keepdims=True)
            yield
            dk = dk + dkb * beta_
            dbeta = jnp.sum(dkb * kx, axis=1, keepdims=True) + jnp.sum(dvb * vx, axis=1, keepdims=True)
            dv = dvb * beta_
            dgc = dgc + jnp.sum(e_mat, axis=1, keepdims=True) + dgam * gam
            dgc = dgc - jnp.sum(e_mat.T, axis=1, keepdims=True)
            yield
            dg = _cumsum_rows(dgc, reverse=True) + dg_last
            yield
            dq_ref[sl] = dq * (1.0 / math.sqrt(DN_DIM))
            dk_ref[sl] = dk
            dv_ref[sl] = dv
            dg_ref[sl] = dg
            db_ref[sl] = jnp.broadcast_to(dbeta, (c, LANE))

        chains = []
        for sub in reversed(range(GDN_SUB_CHUNKS)):
            new = [chain(sub, h) for h in range(DN_HEADS)]
            for _ in range(3):
                for ch in new:
                    next(ch)
            chains += new
        _interleave(chains)
        for h in range(DN_HEADS):
            ds_scr[h] = ds_state[h]

        dq, dk, dv = dq_ref[...], dk_ref[...], dv_ref[...]
        outs, accs = _f_gdn_prep_bwd(
            nblk - 1 - step, x_ref[...], xp_ref[...], xn_ref[...], ab_ref[...], dq, nxt_scr[0], dk, nxt_scr[1],
            dv, nxt_scr[2], dg_ref[...], db_ref[...], w8_ref[...], alog_ref[...], dtb_ref[...], nt=nblk)
        nxt_scr[0] = dq[:8]
        nxt_scr[1] = dk[:8]
        nxt_scr[2] = dv[:8]
        dqkv_ref[...] = outs[0].astype(dqkv_ref.dtype)
        dab_ref[...] = outs[1].astype(dab_ref.dtype)
        for ref, val in zip((dcw_ref, dalog_ref, ddtb_ref), accs):
            @pl.when(step == 0)
            def _():
                ref[...] = val

            @pl.when(step > 0)
            def _():
                ref[...] += val

    sub = GDN_SUB_CHUNKS
    r8 = blk // 8
    rb = lambda n: (nblk - 1 - n, 0)
    hs = lambda n: (0, nblk - 1 - n, 0, 0)
    rows = pl.BlockSpec((blk, DN_WIDTH), rb)
    whole = lambda a: pl.BlockSpec(a.shape, lambda n: (0,) * a.ndim)
    wide = 3 * DN_WIDTH
    return pl.pallas_call(
        body, name="gdn_bwd", grid=(nblk,),
        in_specs=[rows] * 5
        + [pl.BlockSpec((DN_HEADS, sub, DN_DIM, DN_DIM), hs), pl.BlockSpec((DN_HEADS, sub, c, c), hs), rows,
           pl.BlockSpec((blk, wide), rb),
           pl.BlockSpec((8, wide), lambda n: (jnp.maximum((nblk - 1 - n) * r8 - 1, 0), 0)),
           pl.BlockSpec((8, wide), lambda n: (jnp.minimum((nblk - n) * r8, tp // 8 - 1), 0)),
           pl.BlockSpec((blk, LANE), lambda n: (nblk - 1 - n, C_AB // LANE)),
           whole(conv_w8), whole(alog), whole(dtb), _ANY_SPEC],
        out_specs=[pl.BlockSpec((blk, wide), rb), pl.BlockSpec((blk, LANE), rb),
                   whole(conv_w8), whole(alog), whole(dtb)],
        out_shape=[jax.ShapeDtypeStruct((tp, wide), _MXU), jax.ShapeDtypeStruct((tp, LANE), _MXU),
                   jax.ShapeDtypeStruct(conv_w8.shape, F32), jax.ShapeDtypeStruct(alog.shape, F32),
                   jax.ShapeDtypeStruct(dtb.shape, F32)],
        scratch_shapes=[pltpu.VMEM((DN_HEADS, DN_DIM, DN_DIM), F32)] + [pltpu.VMEM((blk, DN_WIDTH), F32)] * 5
        + [pltpu.VMEM((3, 8, DN_WIDTH), F32)],
        compiler_params=pltpu.CompilerParams(dimension_semantics=("arbitrary",), vmem_limit_bytes=VMEM_LIMIT),
    )(q, k, v, g, beta, s_all, t_all, do, proj, proj, proj, proj, conv_w8, alog, dtb, after)


def _silu_parts(x):
    s = _sigmoid(x)
    return x * s, s * (1.0 + x * (1.0 - s))


def _f_rms_bwd_add(i, x, dy, dres, w, *, mask_pad):
    dx, dwr = _rms_bwd(x, w, dy, x.shape[1])
    out = dres + dx
    if mask_pad:
        out = jnp.where(_row_ids(i, x.shape[0]) >= PAD, out, 0.0)
    return (out,), (_rowsum(dwr),)


def _rope(x, cos, sin_s):
    return x * cos + _swap_halves(x) * sin_s


def _rope_t(dy, cos, sin_s):
    return dy * cos + _swap_halves(dy * sin_s)


def _f_mla_qk(i, qf, kvf, kpe, cos, sin_s, qw, kw):
    qs, ks, vs = [], [], []
    for h in range(MLA_HEADS):
        qn, _ = _rms_fwd(qf[:, HP * h:HP * (h + 1)], qw, QK_HEAD)
        qs += [qn[:, :QK_NOPE], _rope(qn[:, QK_NOPE:], cos, sin_s)]
        kh = jnp.concatenate([kvf[:, HP * h:HP * h + QK_NOPE], kpe], axis=1)
        kn, _ = _rms_fwd(kh, kw, QK_HEAD)
        ks += [kn[:, :QK_NOPE], _rope(kn[:, QK_NOPE:], cos, sin_s)]
        vs.append(kvf[:, HP * h + QK_NOPE:HP * (h + 1)])
    return (jnp.concatenate(qs, axis=1), jnp.concatenate(ks, axis=1), jnp.concatenate(vs, axis=1)), ()


def _f_mla_front(i, ql, kvl, kpe, cos, sin_s, qaw, kvaw, wq_t, wkv, qw, kw):
    qn = _rms_fwd(ql, qaw, Q_LORA)[0].astype(_MXU)
    kvn = _rms_fwd(kvl, kvaw, KV_LORA)[0].astype(_MXU)
    qf = _nt(qn, wq_t)
    kvf = _nn(kvn, wkv)
    (q, k, v), _ = _f_mla_qk(i, qf, kvf, kpe, cos, sin_s, qw, kw)
    return (qn, kvn, qf, kvf, q, k, v), ()


def _f_mla_back(i, qf, kvf, kpe, cos, sin_s, dq, dk, dv, ql, kvl, qaw, kvaw, wq_t, wkv, qw, kw):
    (dqf, dkvf, dkpe), (dqw, dkw) = _f_mla_qk_bwd(i, qf, kvf, kpe, cos, sin_s, dq, dk, dv, qw, kw)
    dqf = dqf.astype(_MXU)
    dkvf = dkvf.astype(_MXU)
    dql, dqaw = _rms_bwd(ql, qaw, _nn(dqf, wq_t), Q_LORA)
    dkvl, dkvaw = _rms_bwd(kvl, kvaw, _nt(dkvf, wkv), KV_LORA)
    return (dqf, dkvf, dkpe, dql, dkvl), (dqw, dkw, _rowsum(dqaw), _rowsum(dkvaw))


def _f_mla_qk_bwd(i, qf, kvf, kpe, cos, sin_s, dq, dk, dv, qw, kw):
    dqf, dkvf = [], []
    dkpe = None
    dqw = None
    dkw = None
    for h in range(MLA_HEADS):
        dqh = dq[:, HP * h:HP * (h + 1)]
        dqn = jnp.concatenate([dqh[:, :QK_NOPE], _rope_t(dqh[:, QK_NOPE:], cos, sin_s)], axis=1)
        dx, dwr = _rms_bwd(qf[:, HP * h:HP * (h + 1)], qw, dqn, QK_HEAD)
        dqf.append(dx)
        dqw = _rowsum(dwr) if dqw is None else dqw + _rowsum(dwr)
        dkh = dk[:, HP * h:HP * (h + 1)]
        dkn = jnp.concatenate([dkh[:, :QK_NOPE], _rope_t(dkh[:, QK_NOPE:], cos, sin_s)], axis=1)
        kh = jnp.concatenate([kvf[:, HP * h:HP * h + QK_NOPE], kpe], axis=1)
        dx, dwr = _rms_bwd(kh, kw, dkn, QK_HEAD)
        dkvf += [dx[:, :QK_NOPE], dv[:, V_HEAD * h:V_HEAD * (h + 1)]]
        dkpe = dx[:, QK_NOPE:] if dkpe is None else dkpe + dx[:, QK_NOPE:]
        dkw = _rowsum(dwr) if dkw is None else dkw + _rowsum(dwr)
    return (jnp.concatenate(dqf, axis=1), jnp.concatenate(dkvf, axis=1), dkpe), (dqw, dkw)


def _gdn_act(i, x, halo, w8):
    halo = jnp.where(i > 0, halo, 0.0)
    c = _conv_fwd(x, halo, w8, DN_CONV)
    act, dact = _silu_parts(c)
    return act, dact


def _spread_heads(ab):
    tm = ab.shape[0]
    return jnp.concatenate([jnp.broadcast_to(ab[:, h:h + 1], (tm, DN_DIM)) for h in range(2 * DN_HEADS)], axis=1)


def _gather_heads(x):
    tm = x.shape[0]
    lane = lax.broadcasted_iota(jnp.int32, (tm, LANE), 1)
    out = jnp.zeros((tm, LANE), F32)
    for h in range(2 * DN_HEADS):
        out = out + jnp.where(lane == h, x[:, DN_DIM * h:DN_DIM * h + 1], 0.0)
    return out


def _f_gdn_prep(i, x, halo, ab, w8, alog, dtb):
    tm = x.shape[0]
    act, _ = _gdn_act(i, x, halo, w8)
    outs = []
    for part in range(2):
        for h in range(DN_HEADS):
            t = act[:, DN_WIDTH * part + DN_DIM * h:DN_WIDTH * part + DN_DIM * (h + 1)]
            outs.append(t * lax.rsqrt(jnp.sum(t * t, axis=-1, keepdims=True) + EPS))
    q = jnp.concatenate(outs[:DN_HEADS], axis=1)
    k = jnp.concatenate(outs[DN_HEADS:], axis=1)
    v = act[:, 2 * DN_WIDTH:]
    abb = _spread_heads(ab)
    valid = _row_ids(i, tm) >= PAD
    g = jnp.where(valid, -jnp.exp(alog) * _softplus(abb[:, :DN_WIDTH] + dtb), 0.0)
    beta = jnp.where(valid, _sigmoid(abb[:, DN_WIDTH:]), 0.0)
    return (q, k, v, g, beta), ()


def _f_gdn_prep_bwd(i, x, x_prev, x_next, ab, dq, dq_next, dk, dk_next, dv, dv_next, dg, dbeta,
                    w8, alog, dtb, *, nt):
    tm = x.shape[0]
    x_prev = jnp.where(i > 0, x_prev, 0.0)
    more = i < nt - 1
    ext = lambda t, t_next: jnp.concatenate([t, jnp.where(more, t_next, 0.0)], axis=0)
    c = _conv_fwd(jnp.concatenate([x, x_next], axis=0), x_prev, w8, DN_CONV)
    act, dact = _silu_parts(c)
    douts = []
    for part, dd in enumerate((ext(dq, dq_next), ext(dk, dk_next))):
        for h in range(DN_HEADS):
            t = act[:, DN_WIDTH * part + DN_DIM * h:DN_WIDTH * part + DN_DIM * (h + 1)]
            r = lax.rsqrt(jnp.sum(t * t, axis=-1, keepdims=True) + EPS)
            y = t * r
            dy = dd[:, DN_DIM * h:DN_DIM * (h + 1)]
            douts.append(r * (dy - y * jnp.sum(dy * y, axis=-1, keepdims=True)))
    douts.append(ext(dv, dv_next))
    dc = jnp.concatenate(douts, axis=1) * dact
    dqkv = _conv_bwd_x(dc[:tm], dc[tm:], w8, DN_CONV)
    dconv_w = _conv_bwd_w(dc[:tm], x, x_prev, DN_CONV)
    abb = _spread_heads(ab)
    valid = _row_ids(i, tm) >= PAD
    pre = abb[:, :DN_WIDTH] + dtb
    ea = jnp.exp(alog)
    g = -ea * _softplus(pre)
    dg = jnp.where(valid, dg, 0.0)
    dbeta = jnp.where(valid, dbeta, 0.0)
    da = dg * (-ea) * _sigmoid(pre)
    beta = _sigmoid(abb[:, DN_WIDTH:])
    db = dbeta * beta * (1.0 - beta)
    dab = _gather_heads(jnp.concatenate([da, db], axis=1))
    return (dqkv, dab), (dconv_w, _rowsum(dg * g), _rowsum(da))


def _f_mix(i, o_mla, o_dn, z, w_mla, w_dn):
    tm = o_mla.shape[0]
    valid = _row_ids(i, tm) >= PAD
    outs = []
    for h in range(MLA_HEADS):
        y, _ = _rms_fwd(o_mla[:, V_HEAD * h:V_HEAD * (h + 1)], w_mla, V_HEAD)
        outs.append(jnp.where(valid, y, 0.0))
    for h in range(DN_HEADS):
        y, _ = _rms_fwd(o_dn[:, DN_DIM * h:DN_DIM * (h + 1)], w_dn, DN_DIM)
        outs.append(y * _silu_parts(z[:, DN_DIM * h:DN_DIM * (h + 1)])[0])
    return (jnp.concatenate(outs, axis=1),), ()


def _f_mix_bwd(i, o_mla, o_dn, z, dy_mla, dy_dn, w_mla, w_dn):
    tm = o_mla.shape[0]
    valid = _row_ids(i, tm) >= PAD
    d_mla, d_dn, d_z = [], [], []
    dw_mla = None
    dw_dn = None
    for h in range(MLA_HEADS):
        sl = slice(V_HEAD * h, V_HEAD * (h + 1))
        dx, dwr = _rms_bwd(o_mla[:, sl], w_mla, jnp.where(valid, dy_mla[:, sl], 0.0), V_HEAD)
        d_mla.append(dx)
        dw_mla = _rowsum(dwr) if dw_mla is None else dw_mla + _rowsum(dwr)
    for h in range(DN_HEADS):
        sl = slice(DN_DIM * h, DN_DIM * (h + 1))
        y, _ = _rms_fwd(o_dn[:, sl], w_dn, DN_DIM)
        sz, dsz = _silu_parts(z[:, sl])
        d_z.append(dy_dn[:, sl] * y * dsz)
        dx, dwr = _rms_bwd(o_dn[:, sl], w_dn, dy_dn[:, sl] * sz, DN_DIM)
        d_dn.append(dx)
        dw_dn = _rowsum(dwr) if dw_dn is None else dw_dn + _rowsum(dwr)
    return ((jnp.concatenate(d_mla, axis=1), jnp.concatenate(d_dn, axis=1), jnp.concatenate(d_z, axis=1)),
            (dw_mla, dw_dn))


def _f_ffn_act_bwd(i, gp, gp_prev, gp_next, up, up_next, dact, dact_next, w8, b, *, nt):
    tm = gp.shape[0]
    gp_prev = jnp.where(i > 0, gp_prev, 0.0)
    dact_next = jnp.where(i < nt - 1, dact_next, 0.0)
    cat = lambda t, t_next: jnp.concatenate([t, t_next], axis=0)
    gate = _conv_fwd(cat(gp, gp_next), gp_prev, w8, FFN_CONV) + b
    sg, dsg = _silu_parts(gate)
    dact_e = cat(dact, dact_next)
    dgate = dact_e * cat(up, up_next) * dsg
    dgate_pre = _conv_bwd_x(dgate[:tm], dgate[tm:], w8, FFN_CONV)
    dup = dact * sg[:tm]
    return (dgate_pre, dup), (_conv_bwd_w(dgate[:tm], gp, gp_prev, FFN_CONV), _rowsum(dgate[:tm]))


def _f_loss(i, h3, tgt):
    tm = h3.shape[0]
    diff = jnp.where(_row_ids(i, tm) >= ROW0, h3 - tgt, 0.0)
    part = 0.5 * jnp.sum(diff * diff) * (1.0 / D_MODEL)
    return (diff * (1.0 / D_MODEL),), (jnp.full((1, LANE), part, F32),)


def _local_step(h0, tgt, w, token, late_weights, grads_ready):
    tp = h0.shape[0]
    nt = tp // TM
    proj, u = _norm_mm("in_proj", h0, w["attn_norm_w"], w["w_in"], after=token)
    p_qkv = lambda kind="cur": _In(proj, 3 * DN_WIDTH, 0, kind)
    p_z = _In(proj, DN_WIDTH, C_Z // DN_WIDTH)
    p_ql = _In(proj, Q_LORA, C_QL // Q_LORA)
    p_kvl = _In(proj, KV_LORA, C_KVL // KV_LORA)
    p_kpe = _In(proj, LANE, C_KPE // LANE)
    p_ab = _In(proj, LANE, C_AB // LANE)
    cos, sin_s = _In(w["cos"]), _In(w["sin_s"])

    mla_w = [w["q_a_norm_w"], w["kv_a_norm_w"], w["w_q_b"], w["w_kv_b"], w["q_norm_w"], w["k_norm_w"]]
    tm_mla = _pick(tp, 288, 16)
    wide = MLA_HEADS * HP
    qn, kvn, qf, kvf, q, k, v = _rows(
        "mla_front", _f_mla_front, [p_ql, p_kvl, p_kpe, cos, sin_s], mla_w,
        [(Q_LORA, _MXU), (KV_LORA, _MXU), (wide, F32), (wide, F32), (wide, _MXU), (wide, _MXU),
         (MLA_HEADS * V_HEAD, _MXU)], tm=tm_mla)
    o_mla = _attn_fwd(q, k, v)

    dn_w = [w["dn_conv_w"], w["alog_b"], w["dtb_b"]]
    o_dn, s_all, t_all, gq, gk, gv, gg, gb = _gdn_fwd(proj, *dn_w)

    out_w = [w["mla_out_norm_w"], w["dn_out_norm_w"]]
    w = dict(w, **late_weights((o_mla, o_dn), _LATE[:3]))
    h2, mixed = _pro_mm("mix_out_proj", lambda i, *t: _f_mix(i, *t)[0][0], [_In(o_mla), _In(o_dn), p_z], out_w,
                        D_MODEL, w["w_out"], h0)

    ffn_w = [w["ffn_conv_w"], w["ffn_conv_b"]]
    hn, gate_pre, up, act = _ffn_in(h2, w["ffn_norm_w"], w["w_gate"], w["w_up"], *ffn_w)
    w = dict(w, **late_weights(act, _LATE[3:]))
    dh3, loss = _mm_rows("ffn_down_loss", act, w["w_down"], "nn", lambda i, y, r, t: _f_loss(i, r + y, t),
                         [_In(h2), _In(tgt)], [], [(D_MODEL, F32)], [(1, LANE)])

    g = {}
    dact = _mm("ffn_down_dx", dh3, w["w_down"], "nt")
    g["w_down"] = _mm("ffn_down_dw", act, dh3, "tn", out_dtype=_MXU)
    dgate_pre, dup, g["ffn_conv_w"], g["ffn_conv_b"] = _rows(
        "ffn_act_bwd", functools.partial(_f_ffn_act_bwd, nt=nt),
        [_In(gate_pre), _In(gate_pre, kind="prev"), _In(gate_pre, kind="next"), _In(up), _In(up, kind="next"),
         _In(dact), _In(dact, kind="next")], ffn_w,
        [(D_FF, _MXU), (D_FF, _MXU)], [(8, D_FF), (1, D_FF)])
    g["w_gate"], g["w_up"] = _mm_tn2("ffn_gate_up_dw", dgate_pre, dup, hn, out_dtype=_MXU)
    tok = grads_ready(g, ("w_down", "w_gate", "w_up"))
    dh2, g["ffn_norm_w"] = _mm_rows(
        "ffn_gate_up_dx_rms", [dgate_pre, dup], [w["w_gate"], w["w_up"]], "nn",
        lambda i, dy, x, dres, nw, _tok: _f_rms_bwd_add(i, x, dy, dres, nw, mask_pad=True),
        [_In(h2), _In(dh3)], [w["ffn_norm_w"], tok], [(D_MODEL, F32)], [(1, D_MODEL)])

    g["w_out"] = _mm("out_proj_dw", mixed, dh2, "tn", out_dtype=_MXU)
    half = MLA_HEADS * V_HEAD
    do_mla, do_dn, dz, g["mla_out_norm_w"], g["dn_out_norm_w"] = _mm_rows(
        "out_proj_dx_mix", dh2, w["w_out"], "nt",
        lambda i, dm, om, od, z, wm, wd: _f_mix_bwd(i, om, od, z, dm[:, :half], dm[:, half:], wm, wd),
        [_In(o_mla), _In(o_dn), p_z], out_w,
        [(half, F32), (DN_WIDTH, F32), (DN_WIDTH, _MXU)], [(1, V_HEAD), (1, DN_DIM)])

    dq, dk, dv = _attn_bwd(q, k, v, do_mla)
    dqf, dkvf, dkpe, dql, dkvl, g["q_norm_w"], g["k_norm_w"], g["q_a_norm_w"], g["kv_a_norm_w"] = _rows(
        "mla_back", _f_mla_back,
        [_In(qf), _In(kvf), p_kpe, cos, sin_s, _In(dq), _In(dk), _In(dv), p_ql, p_kvl], mla_w,
        [(wide, _MXU), (wide, _MXU), (LANE, _MXU), (Q_LORA, _MXU), (KV_LORA, _MXU)],
        [(1, HP), (1, HP), (1, Q_LORA), (1, KV_LORA)], tm=tm_mla)
    g["w_q_b"] = _mm("mla_q_b_dw", dqf, qn, "tn")
    g["w_kv_b"] = _mm("mla_kv_b_dw", kvn, dkvf, "tn")
    tok = grads_ready(g, ("w_out", "w_q_b", "w_kv_b"))

    dqkv, dab, g["dn_conv_w"], g["alog_b"], g["dtb_b"] = _gdn_bwd(
        gq, gk, gv, gg, gb, s_all, t_all, do_dn, proj, *dn_w, tok)

    dproj = jnp.concatenate([dqkv, dz, dql, dkvl, dkpe, dab], axis=1)
    g["w_in"] = _mm("in_proj_dw", dproj, u, "tn", out_dtype=_MXU)
    tok = grads_ready(g, ("w_in",))
    dh0, g["attn_norm_w"] = _mm_rows(
        "in_proj_dx_rms", dproj, w["w_in"], "nn",
        lambda i, du, x, dres, nw, _tok: _f_rms_bwd_add(i, x, du, dres, nw, mask_pad=False),
        [_In(h0), _In(dh2)], [w["attn_norm_w"], tok], [(D_MODEL, F32)], [(1, D_MODEL)])
    return loss, dh0, g


def _w_in_to_padded(w):
    c1, c2, c3 = Q_LORA, Q_LORA + KV_LORA, Q_LORA + KV_LORA + QK_ROPE
    c4 = c3 + 3 * DN_WIDTH
    c5 = c4 + DN_WIDTH
    z = lambda n: jnp.zeros((n, w.shape[1]), w.dtype)
    return jnp.concatenate([w[c3:c4], w[c4:c5], w[:c1], w[c1:c2], w[c2:c3], z(LANE - QK_ROPE),
                            w[c5:], z(LANE - 2 * DN_HEADS)], axis=0)


def _w_in_from_padded(g):
    return jnp.concatenate([g[C_QL:C_QL + Q_LORA], g[C_KVL:C_KVL + KV_LORA], g[C_KPE:C_KPE + QK_ROPE],
                            g[:C_Z + DN_WIDTH], g[C_AB:C_AB + 2 * DN_HEADS]], axis=0)


def _w_q_b_to_padded(w):
    r = w.shape[1]
    w = w.reshape(MLA_HEADS, QK_HEAD, r)
    return jnp.pad(w, ((0, 0), (0, HP - QK_HEAD), (0, 0))).reshape(MLA_HEADS * HP, r)


def _w_q_b_from_padded(g):
    r = g.shape[1]
    return g.reshape(MLA_HEADS, HP, r)[:, :QK_HEAD].reshape(MLA_HEADS * QK_HEAD, r)


def _pad_rows8(w):
    return jnp.pad(w, ((0, 8 - w.shape[0]), (0, 0)))


def _prepare(full, tp):
    w = {}
    mx = lambda a: a.astype(_MXU)
    w["attn_norm_w"] = full["attn_norm_w"]
    w["w_in"] = mx(_w_in_to_padded(full["w_in"]))
    w["q_a_norm_w"] = full["q_a_norm_w"]
    w["kv_a_norm_w"] = full["kv_a_norm_w"]
    w["w_q_b"] = mx(_w_q_b_to_padded(full["w_q_b"]))
    w["w_kv_b"] = mx(full["w_kv_b"])
    w["q_norm_w"] = jnp.pad(full["q_norm_w"], ((0, 0), (0, HP - QK_HEAD)))
    w["k_norm_w"] = jnp.pad(full["k_norm_w"], ((0, 0), (0, HP - QK_HEAD)))
    w["mla_out_norm_w"] = full["mla_out_norm_w"]
    w["dn_out_norm_w"] = full["dn_out_norm_w"]
    w["dn_conv_w"] = _pad_rows8(full["dn_conv_w"])
    w["alog_b"] = jnp.repeat(full["dn_A_log"], DN_DIM, axis=1)
    w["dtb_b"] = jnp.repeat(full["dn_dt_bias"], DN_DIM, axis=1)
    w["ffn_norm_w"] = full["ffn_norm_w"]
    w["ffn_conv_w"] = _pad_rows8(full["ffn_conv_w"])
    w["ffn_conv_b"] = full["ffn_conv_b"]
    for n in _LATE:
        if n in full:
            w[n] = mx(full[n])
    half = QK_ROPE // 2
    inv = ROPE_THETA ** (-jnp.arange(half, dtype=F32) / half)
    ang = (jnp.arange(tp, dtype=jnp.int32) - PAD).astype(F32)[:, None] * inv[None, :]
    zc = jnp.zeros((tp, LANE - QK_ROPE), F32)
    w["cos"] = jnp.concatenate([jnp.cos(ang), jnp.cos(ang), zc], axis=1)
    w["sin_s"] = jnp.concatenate([-jnp.sin(ang), jnp.sin(ang), zc], axis=1)
    return w


def _grads_to_natural(g):
    convert = {
        "w_in": ("w_in", _w_in_from_padded),
        "w_q_b": ("w_q_b", _w_q_b_from_padded),
        "q_norm_w": ("q_norm_w", lambda a: a[:, :QK_HEAD]),
        "k_norm_w": ("k_norm_w", lambda a: a[:, :QK_HEAD]),
        "dn_conv_w": ("dn_conv_w", lambda a: a[:DN_CONV]),
        "ffn_conv_w": ("ffn_conv_w", lambda a: a[:FFN_CONV]),
        "alog_b": ("dn_A_log", lambda a: a[:, ::DN_DIM]),
        "dtb_b": ("dn_dt_bias", lambda a: a[:, ::DN_DIM]),
    }
    n = {}
    for key, a in g.items():
        name, fn = convert.get(key, (key, lambda t: t))
        n[name] = fn(a)
    return n


_MESH = pl.DeviceIdType.MESH
_ANY = pl.BlockSpec(memory_space=pl.ANY)
_CHIP_FLIPS = ((1, 0), (0, 1), (1, 1))


def _me():
    return lax.axis_index("x"), lax.axis_index("y"), lax.axis_index("c")


def _all_gather(name, blk, after):
    def body(x_ref, _after_ref, out_ref, send_sems, recv_sems, local_sem):
        x, y, c = _me()
        me, sib = (x, y, c), (x, y, 1 - c)
        chips = [(x ^ fx, y ^ fy) for fx, fy in _CHIP_FLIPS]

        def slot(p):
            return out_ref.at[4 * p[0] + 2 * p[1] + p[2]]

        def copy(k, block, to, src=None):
            return pltpu.make_async_remote_copy(
                src_ref=slot(block) if src is None else src, dst_ref=slot(block),
                send_sem=send_sems.at[k], recv_sem=recv_sems.at[k], device_id=to, device_id_type=_MESH)

        mine = pltpu.make_async_copy(x_ref, slot(me), local_sem)
        mine.start()
        first = [copy(0, me, sib, src=x_ref)]
        first += [copy(1 + j, me, (*chip, c), src=x_ref) for j, chip in enumerate(chips)]
        for cp in first:
            cp.start()
        passed = [copy(4 + j, (*chip, c), sib) for j, chip in enumerate(chips)]
        for j, chip in enumerate(chips):
            copy(1 + j, (*chip, c), me).wait_recv()
            passed[j].start()
        copy(0, sib, me).wait_recv()
        for j, chip in enumerate(chips):
            copy(4 + j, (*chip, 1 - c), me).wait_recv()
        for cp in first + passed:
            cp.wait_send()
        mine.wait()

    return pl.pallas_call(
        body, name=name, in_specs=[_ANY, _ANY], out_specs=_ANY,
        out_shape=jax.ShapeDtypeStruct((N_DEV,) + blk.shape, blk.dtype),
        scratch_shapes=[pltpu.SemaphoreType.DMA((7,)), pltpu.SemaphoreType.DMA((7,)), pltpu.SemaphoreType.DMA],
    )(blk, after)


def _row_tile(r):
    divs = [d for d in range(16, min(r, 512) + 1, 16) if r % d == 0]
    return divs[-1] if divs else r


def _adam_math(g, w, m, v):
    m_new = ADAM_B1 * m + (1.0 - ADAM_B1) * g
    v_new = ADAM_B2 * v + (1.0 - ADAM_B2) * (g * g)
    m_hat = m_new / (1.0 - ADAM_B1 ** ADAM_STEP)
    v_hat = v_new / (1.0 - ADAM_B2 ** ADAM_STEP)
    return -ADAM_LR * (m_hat / (jnp.sqrt(v_hat) + ADAM_EPS) + ADAM_WD * w), m_new, v_new


def _adam_vectors(name, row, items, ws, ms, vs):
    k = len(items)

    def body(row_ref, *refs):
        w_refs, m_refs, v_refs = refs[:k], refs[k:2 * k], refs[2 * k:3 * k]
        outs = refs[3 * k:]
        for idx, (off, n, per_head) in enumerate(items):
            if per_head:
                spread = row_ref[:, off:off + DN_WIDTH]
                lane = lax.broadcasted_iota(jnp.int32, (1, LANE), 1)
                g = jnp.zeros((1, LANE), F32)
                for h in range(DN_HEADS):
                    g = g + jnp.where(lane == h, spread[:, DN_DIM * h:DN_DIM * h + 1], 0.0)
                g = g[:, :n]
            else:
                g = row_ref[:, off:off + n]
            d, m_new, v_new = _adam_math(g, w_refs[idx][...], m_refs[idx][...], v_refs[idx][...])
            for kind, val in enumerate((g, d, m_new, v_new)):
                outs[kind * k + idx][...] = val

    shapes = [jax.ShapeDtypeStruct((1, n), F32) for _, n, _ in items]
    res = pl.pallas_call(body, name=name, out_shape=shapes * 4)(row, *ws, *ms, *vs)
    return [list(res[kind * k:(kind + 1) * k]) for kind in range(4)]


def _adam_arrays(name, gs, ws, ms, vs):
    k = len(gs)

    def body(*refs):
        outs = refs[4 * k:]
        for idx in range(k):
            res = _adam_math(refs[idx][...], refs[k + idx][...], refs[2 * k + idx][...], refs[3 * k + idx][...])
            for kind, val in enumerate(res):
                outs[kind * k + idx][...] = val

    shapes = [jax.ShapeDtypeStruct(w.shape, F32) for w in ws]
    res = pl.pallas_call(body, name=name, out_shape=shapes * 3)(*gs, *ws, *ms, *vs)
    return [list(res[kind * k:(kind + 1) * k]) for kind in range(3)]


def _sum_parts(name, parts):
    _, r, cols = parts[0][0].shape
    tm = _row_tile(r)
    idx = jnp.stack([jnp.asarray(s, jnp.int32) for _, s in parts])
    n = len(parts)

    def body(idx_ref, *refs):
        g = refs[0][0].astype(F32)
        for p_ref in refs[1:n]:
            g = g + p_ref[0].astype(F32)
        refs[n][...] = g

    return pl.pallas_call(
        body, name=name,
        grid_spec=pltpu.PrefetchScalarGridSpec(
            num_scalar_prefetch=1, grid=(r // tm,),
            in_specs=[pl.BlockSpec((1, tm, cols), lambda i, idx_ref, p=p: (idx_ref[p], i, 0)) for p in range(n)],
            out_specs=pl.BlockSpec((tm, cols), lambda i, idx_ref: (i, 0))),
        out_shape=jax.ShapeDtypeStruct((r, cols), F32),
        compiler_params=pltpu.CompilerParams(dimension_semantics=("parallel",)),
    )(idx, *[a for a, _ in parts])


def _adam(name, parts, w, m, v):
    r, cols = w.shape
    tm = _row_tile(r)
    tc = cols // 4 if (r // tm < 4 and cols % (4 * LANE) == 0) else cols
    idx = jnp.stack([jnp.asarray(s, jnp.int32) for _, s in parts])
    n = len(parts)

    def body(idx_ref, *refs):
        g = refs[0][0].astype(F32)
        for p_ref in refs[1:n]:
            g = g + p_ref[0].astype(F32)
        w_ref, m_ref, v_ref, g_out, d_out, m_out, v_out = refs[n:]
        g_out[...] = g
        d_out[...], m_out[...], v_out[...] = _adam_math(g, w_ref[...], m_ref[...], v_ref[...])

    part_specs = [pl.BlockSpec((1, tm, tc), lambda i, j, idx_ref, p=p: (idx_ref[p], i, j)) for p in range(n)]
    flat = pl.BlockSpec((tm, tc), lambda i, j, idx_ref: (i, j))
    return pl.pallas_call(
        body, name=name,
        grid_spec=pltpu.PrefetchScalarGridSpec(
            num_scalar_prefetch=1, grid=(r // tm, cols // tc), in_specs=part_specs + [flat] * 3,
            out_specs=[flat] * 4),
        out_shape=[jax.ShapeDtypeStruct((r, cols), F32)] * 4,
        compiler_params=pltpu.CompilerParams(dimension_semantics=("parallel", "parallel")),
    )(idx, *[a for a, _ in parts], w, m, v)


def _all_gather_many(name, blks):
    n = len(blks)

    def body(*refs):
        x_refs, out_refs = refs[:n], refs[n:2 * n]
        send_sems, recv_sems, local_sems = refs[2 * n:]
        x, y, c = _me()
        me, sib = (x, y, c), (x, y, 1 - c)
        chips = [(x ^ fx, y ^ fy) for fx, fy in _CHIP_FLIPS]

        def slot(a, p):
            return out_refs[a].at[4 * p[0] + 2 * p[1] + p[2]]

        def copy(a, k, block, to, src=None):
            return pltpu.make_async_remote_copy(
                src_ref=slot(a, block) if src is None else src, dst_ref=slot(a, block),
                send_sem=send_sems.at[7 * a + k], recv_sem=recv_sems.at[7 * a + k], device_id=to,
                device_id_type=_MESH)

        mine = [pltpu.make_async_copy(x_refs[a], slot(a, me), local_sems.at[a]) for a in range(n)]
        first = []
        for a in range(n):
            mine[a].start()
            first.append(copy(a, 0, me, sib, src=x_refs[a]))
            first += [copy(a, 1 + j, me, (*chip, c), src=x_refs[a]) for j, chip in enumerate(chips)]
        for cp in first:
            cp.start()
        passed = []
        for j, chip in enumerate(chips):
            for a in range(n):
                copy(a, 1 + j, (*chip, c), me).wait_recv()
                cp = copy(a, 4 + j, (*chip, c), sib)
                cp.start()
                passed.append(cp)
        for a in range(n):
            copy(a, 0, sib, me).wait_recv()
            for j, chip in enumerate(chips):
                copy(a, 4 + j, (*chip, 1 - c), me).wait_recv()
        for cp in first + passed:
            cp.wait_send()
        for cp in mine:
            cp.wait()

    return pl.pallas_call(
        body, name=name, in_specs=[_ANY] * n, out_specs=[_ANY] * n,
        out_shape=[jax.ShapeDtypeStruct((N_DEV,) + b.shape, b.dtype) for b in blks],
        scratch_shapes=[pltpu.SemaphoreType.DMA((7 * n,)), pltpu.SemaphoreType.DMA((7 * n,)),
                        pltpu.SemaphoreType.DMA((n,))],
    )(*blks)


_HBM = pl.BlockSpec(memory_space=pltpu.HBM)
_SEM = pl.BlockSpec(memory_space=pltpu.SEMAPHORE)
_EFFECT = pltpu.SideEffectType.DATAFLOW_SIDE_EFFECTING


def _push_copies(src_refs, land_refs, send_sems, recv_sems, src_by_peer, first=0):
    x, y, c = _me()
    my_id = 4 * x + 2 * y + c
    out = []
    for k in range(len(src_refs)):
        a = first + k
        for f in range(1, N_DEV):
            px, py, pc = x ^ (f >> 2), y ^ ((f >> 1) & 1), c ^ (f & 1)
            pid = 4 * px + 2 * py + pc
            src = src_refs[k].at[pid] if src_by_peer else src_refs[k]
            start = pltpu.make_async_remote_copy(
                src_ref=src, dst_ref=land_refs[k].at[my_id], send_sem=send_sems.at[7 * a + f - 1],
                recv_sem=recv_sems.at[7 * a + f - 1], device_id=(px, py, pc), device_id_type=_MESH)
            landed = pltpu.make_async_remote_copy(
                src_ref=src, dst_ref=land_refs[k].at[pid], send_sem=send_sems.at[7 * a + f - 1],
                recv_sem=recv_sems.at[7 * a + f - 1], device_id=(px, py, pc), device_id_type=_MESH)
            out.append((start, landed))
    return out


def _push_start(name, srcs, src_by_peer, after):
    n = len(srcs)
    lands = [jax.ShapeDtypeStruct((N_DEV,) + (s.shape[1:] if src_by_peer else s.shape), s.dtype) for s in srcs]

    def body(*refs):
        src_refs, land_refs = refs[:n], refs[n:2 * n]
        send_sems, recv_sems = refs[2 * n + 1], refs[2 * n + 2]
        token = refs[-1]
        for start, _ in _push_copies(src_refs, land_refs, send_sems, recv_sems, src_by_peer):
            start.start()
        token[...] = jnp.zeros_like(token)

    hbm = lambda a: pltpu.with_memory_space_constraint(a, pltpu.HBM)
    res = pl.pallas_call(
        body, name=name,
        out_shape=(pltpu.SemaphoreType.DMA((7 * n,)), pltpu.SemaphoreType.DMA((7 * n,)),
                   *[pltpu.HBM(s.shape, s.dtype) for s in srcs], *[pltpu.HBM(s.shape, s.dtype) for s in lands],
                   jax.ShapeDtypeStruct((8, LANE), F32)),
        in_specs=[_HBM] * (2 * n) + [_ANY],
        out_specs=(_SEM, _SEM, *[_HBM] * (2 * n), pl.BlockSpec(memory_space=pltpu.VMEM)),
        input_output_aliases={i: 2 + i for i in range(2 * n)},
        compiler_params=pltpu.CompilerParams(has_side_effects=_EFFECT),
    )(*[hbm(s) for s in srcs], *[hbm(lax.empty(s.shape, s.dtype)) for s in lands], after)
    return res[0], res[1], list(res[2:2 + n]), list(res[2 + n:2 + 2 * n]), res[-1]


def _push_wait(name, send_sems, recv_sems, srcs, lands, src_by_peer, after, first=0):
    n = len(srcs)
    after = list(after) if isinstance(after, (list, tuple)) else [after]

    def body(*refs):
        src_refs, land_refs = refs[:n], refs[n:2 * n]
        s_sems, r_sems = refs[2 * n], refs[2 * n + 1]
        for _, landed in _push_copies(src_refs, land_refs, s_sems, r_sems, src_by_peer, first):
            landed.wait_send()
            landed.wait_recv()

    res = pl.pallas_call(
        body, name=name,
        out_shape=tuple(pltpu.HBM(s.shape, s.dtype) for s in list(srcs) + list(lands)),
        in_specs=[_HBM] * (2 * n) + [_SEM, _SEM] + [_ANY] * len(after),
        out_specs=tuple([_HBM] * (2 * n)),
        input_output_aliases={i: i for i in range(2 * n)},
        compiler_params=pltpu.CompilerParams(has_side_effects=_EFFECT),
    )(*srcs, *lands, send_sems, recv_sems, *after)
    return list(res[:n]), list(res[n:])


_SHARDED = (
    ("meta_tokens", 1, (N_META, D_MODEL)),
    ("w_in", 1, (D_MODEL, IN_COLS)),
    ("w_q_b", 1, (Q_LORA, MLA_HEADS * QK_HEAD)),
    ("w_kv_b", 1, (KV_LORA, MLA_HEADS * (QK_NOPE + V_HEAD))),
    ("dn_conv_w", 1, (DN_CONV, 3 * DN_WIDTH)),
    ("w_out", 0, (2 * DN_WIDTH, D_MODEL)),
    ("w_gate", 1, (D_MODEL, D_FF)),
    ("w_up", 1, (D_MODEL, D_FF)),
    ("ffn_conv_w", 1, (FFN_CONV, D_FF)),
    ("w_down", 0, (D_FF, D_MODEL)),
)
_F32_GATHERED = ("meta_tokens", "dn_conv_w", "ffn_conv_w")
_EARLY = ("w_in", "w_q_b", "w_kv_b")
_LATE = ("w_out", "w_gate", "w_up", "w_down")
_TRANSPOSED = ("w_in", "w_q_b", "w_gate", "w_up")
_REPLICATED = (
    ("attn_norm_w", D_MODEL), ("q_a_norm_w", Q_LORA), ("kv_a_norm_w", KV_LORA), ("q_norm_w", QK_HEAD),
    ("k_norm_w", QK_HEAD), ("mla_out_norm_w", V_HEAD), ("dn_A_log", DN_HEADS), ("dn_dt_bias", DN_HEADS),
    ("dn_out_norm_w", DN_DIM), ("ffn_norm_w", D_MODEL), ("ffn_conv_b", D_FF),
)
_SMALL_BLOCK = (8, 512)


def _local_shape(dim, shape):
    return (shape[0] // N_DEV, shape[1]) if dim == 0 else (shape[0], shape[1] // N_DEV)


def _from_blocks(blocks, dim, shape):
    r, c = shape
    if dim == 0:
        return blocks.reshape(r, c)
    return blocks.reshape(N_DEV, r, c // N_DEV).transpose(1, 0, 2).reshape(r, c)


def _split(flat, sizes):
    out, o = [], 0
    for s in sizes:
        out.append(flat[..., o:o + s])
        o += s
    return out


def kernel(x, meta_tokens, attn_norm_w, w_in, q_a_norm_w, w_q_b, kv_a_norm_w, w_kv_b, q_norm_w, k_norm_w, mla_out_norm_w, dn_conv_w, dn_A_log, dn_dt_bias, dn_out_norm_w, w_out, ffn_norm_w, w_gate, w_up, ffn_conv_w, ffn_conv_b, w_down, loss_target, m_meta_tokens, m_attn_norm_w, m_w_in, m_q_a_norm_w, m_w_q_b, m_kv_a_norm_w, m_w_kv_b, m_q_norm_w, m_k_norm_w, m_mla_out_norm_w, m_dn_conv_w, m_dn_A_log, m_dn_dt_bias, m_dn_out_norm_w, m_w_out, m_ffn_norm_w, m_w_gate, m_w_up, m_ffn_conv_w, m_ffn_conv_b, m_w_down, v_meta_tokens, v_attn_norm_w, v_w_in, v_q_a_norm_w, v_w_q_b, v_kv_a_norm_w, v_w_kv_b, v_q_norm_w, v_k_norm_w, v_mla_out_norm_w, v_dn_conv_w, v_dn_A_log, v_dn_dt_bias, v_dn_out_norm_w, v_w_out, v_ffn_norm_w, v_w_gate, v_w_up, v_ffn_conv_w, v_ffn_conv_b, v_w_down):
    names = [n for n, _, _ in _SHARDED] + [n for n, _ in _REPLICATED]
    given = dict(locals())
    two_d = lambda a: a.reshape(a.shape[-2:])
    view = lambda a, n: two_d(a).T if n in _TRANSPOSED else two_d(a)
    wl = {n: view(given[n], n) for n in names}
    ml = {n: view(given["m_" + n], n) for n in names}
    vl = {n: view(given["v_" + n], n) for n in names}
    out_shapes = {n: given[n].shape for n in names}

    spec = {n: (d, s) for n, d, s in _SHARDED}
    small_sizes = [math.prod(_local_shape(*spec[n])) for n in _F32_GATHERED]

    def small_block(d):
        cat = jnp.concatenate([d[n].reshape(d[n].shape[:-2] + (-1,)) for n in _F32_GATHERED], axis=-1)
        pad = [(0, 0)] * (cat.ndim - 1) + [(0, math.prod(_SMALL_BLOCK) - cat.shape[-1])]
        return jnp.pad(cat, pad).reshape(cat.shape[:-1] + _SMALL_BLOCK)

    def shard(n):
        return wl[n].astype(_MXU)

    def from_slots(n, blocks):
        d, s = spec[n]
        if d == 0 or n in _TRANSPOSED:
            return blocks.reshape(-1, blocks.shape[-1])
        return blocks.transpose(1, 0, 2).reshape(s)

    my_id = 4 * lax.axis_index("x") + 2 * lax.axis_index("y") + lax.axis_index("c")
    got = _all_gather_many("gather_early", [shard(n) for n in _EARLY] + [small_block(wl)])
    full = {n: a for n, a in wl.items() if n not in _LATE}
    for n, blocks in zip(_EARLY, got):
        full[n] = from_slots(n, blocks)
    for n, p in zip(_F32_GATHERED, _split(got[-1].reshape(N_DEV, -1), small_sizes)):
        full[n] = _from_blocks(p, *spec[n])
    late_own = [shard(n) for n in _LATE]
    l_send, l_recv, l_src, l_land, token = _push_start("gather_late_start", late_own, False, got[-1])

    def late_weights(after, names):
        first = _LATE.index(names[0])
        sl = slice(first, first + len(names))
        _, lands = _push_wait("gather_late_wait_" + names[0], l_send, l_recv, l_src[sl], l_land[sl], False,
                              after, first)
        out = {}
        for n, land, own in zip(names, lands, late_own[sl]):
            out[n] = from_slots(n, lax.dynamic_update_slice(land, own[None], (my_id, 0, 0))).astype(_MXU)
        return out

    def dest_blocks(n, a):
        d, s = spec[n]
        r, c = _local_shape(d, s)
        if n in _TRANSPOSED:
            return a.reshape(N_DEV, c, r)
        return a.reshape(N_DEV, r, c) if d == 0 else a.reshape(r, N_DEV, c).transpose(1, 0, 2)

    pushed = []

    def grads_ready(g, names):
        nat = _grads_to_natural({n: g[n] for n in names})
        blocks = [dest_blocks(n, nat[n]).astype(_MXU) for n in names]
        sends, recvs, srcs, lands, tok = _push_start("rs_" + names[0] + "_start", blocks, True, token)
        pushed.append((names, sends, recvs, srcs, lands))
        return tok

    seq = x.shape[1]
    tp = ROW0 + seq
    h0 = jnp.concatenate([jnp.zeros((PAD, D_MODEL), F32), full["meta_tokens"], x[0]], axis=0)
    tgt = jnp.concatenate([jnp.zeros((ROW0, D_MODEL), F32), loss_target[0]], axis=0)
    loss, dh0, raw = _local_step(h0, tgt, _prepare(full, tp), token, late_weights, grads_ready)
    g = _grads_to_natural(raw)
    g["meta_tokens"] = dh0[PAD:ROW0]
    grad_x = dh0[ROW0:][None]

    big = [{}, {}, {}, {}]

    def finish(group):
        names, sends, recvs, srcs, lands = group
        srcs, lands = _push_wait("rs_" + names[0] + "_wait", sends, recvs, srcs, lands, True, dh0)
        for n, src, land in zip(names, srcs, lands):
            parts = [(src, my_id)] + [(land, my_id ^ f) for f in range(1, N_DEV)]
            for kind, a in enumerate(_adam("adam_" + n, parts, wl[n], ml[n], vl[n])):
                big[kind][n] = a

    for group in pushed[:-1]:
        finish(group)
    rep_names = [n for n, _ in _REPLICATED]
    raw_key = {"dn_A_log": "alog_b", "dn_dt_bias": "dtb_b"}
    pieces = [raw[raw_key.get(n, n)] for n in rep_names] + [loss]
    pieces += [g[n].reshape(1, -1) for n in _F32_GATHERED]
    widths = [p.shape[1] for p in pieces]
    offs = [sum(widths[:k]) for k in range(len(widths))]
    cat = jnp.concatenate(pieces, axis=1)
    cols = -(-cat.shape[1] // (8 * LANE)) * LANE
    mine = jnp.pad(cat, ((0, 0), (0, 8 * cols - cat.shape[1]))).reshape(8, cols)
    everyone = _all_gather("gather_small_grads", mine, big[1][pushed[-2][0][-1]])
    total = _sum_parts("sum_small_grads", [(everyone, d) for d in range(N_DEV)]).reshape(1, 8 * cols)
    tot = {n: total[0, o:o + wd] for n, o, wd in zip(rep_names + ["loss"] + list(_F32_GATHERED), offs, widths)}
    items = [(o, size, n in raw_key) for (n, size), o in zip(_REPLICATED, offs)]
    sm = _adam_vectors("adam_replicated", total, items, [wl[n] for n in rep_names], [ml[n] for n in rep_names],
                       [vl[n] for n in rep_names])
    sm = [dict(zip(rep_names, kind)) for kind in sm]
    mine_of = {}
    for n in _F32_GATHERED:
        d, s = spec[n]
        r, c = _local_shape(d, s)
        mine_of[n] = lax.dynamic_slice(tot[n].reshape(s), (0, my_id * c), (r, c))
    res = _adam_arrays("adam_small_sharded", [mine_of[n] for n in _F32_GATHERED], [wl[n] for n in _F32_GATHERED],
                       [ml[n] for n in _F32_GATHERED], [vl[n] for n in _F32_GATHERED])
    for kind, arrays in enumerate([[mine_of[n] for n in _F32_GATHERED]] + res):
        big[kind].update(zip(_F32_GATHERED, arrays))

    finish(pushed[-1])

    outs = [tot["loss"][0], grad_x]
    for kind in range(4):
        for n in ("meta_tokens", "attn_norm_w", "w_in", "q_a_norm_w", "w_q_b", "kv_a_norm_w", "w_kv_b", "q_norm_w",
                  "k_norm_w", "mla_out_norm_w", "dn_conv_w", "dn_A_log", "dn_dt_bias", "dn_out_norm_w", "w_out",
                  "ffn_norm_w", "w_gate", "w_up", "ffn_conv_w", "ffn_conv_b", "w_down"):
            src = big[kind] if n in big[kind] else sm[kind]
            a = src[n].T if n in _TRANSPOSED else src[n]
            outs.append(a.reshape(out_shapes[n]))
    return tuple(outs)
```

```python
import functools
import math

import jax
import jax.numpy as jnp
from jax import lax
from jax.experimental import pallas as pl
from jax.experimental.pallas import tpu as pltpu

F32 = jnp.float32
_MXU = jnp.bfloat16
_HI = lax.Precision.HIGHEST

D_MODEL = 1024
N_META = 16
PAD = 112
ROW0 = PAD + N_META
MLA_HEADS = 4
QK_NOPE = 128
QK_ROPE = 64
QK_HEAD = QK_NOPE + QK_ROPE
V_HEAD = 128
Q_LORA = 256
KV_LORA = 256
ROPE_THETA = 10000.0
DN_HEADS = 4
DN_DIM = 128
DN_WIDTH = DN_HEADS * DN_DIM
DN_CONV = 4
DN_CHUNK = 64
GDN_SUB_CHUNKS = 2
D_FF = 2816
FFN_CONV = 3
EPS = 1e-6
HP = 256
C_Z = 1536
C_QL = 2048
C_KVL = 2304
C_KPE = 2560
C_AB = 2688
IN_COLS = 2632

ADAM_LR = 0.001
ADAM_B1 = 0.9
ADAM_B2 = 0.999
ADAM_EPS = 1e-08
ADAM_WD = 0.01
ADAM_STEP = 10

N_DEV = 8
TM = 128
LANE = 128
VMEM_LIMIT = 56 * 1024 * 1024
NEG = -1e30


def _dot(a, b, dims, hp=False):
    if hp:
        return lax.dot_general(a.astype(F32), b.astype(F32), (dims, ((), ())),
                               precision=lax.Precision.HIGH if hp == "3x" else _HI, preferred_element_type=F32)
    return lax.dot_general(a.astype(_MXU), b.astype(_MXU), (dims, ((), ())),
                           preferred_element_type=F32)


def _nn(a, b, hp=False):
    return _dot(a, b, ((1,), (0,)), hp)


def _nt(a, b, hp=False):
    return _dot(a, b, ((1,), (1,)), hp)


def _tn(a, b, hp=False):
    return _dot(a, b, ((0,), (0,)), hp)


def _sigmoid(x):
    return 1.0 / (1.0 + jnp.exp(-x))


def _rms_fwd(x, w, n):
    r = lax.rsqrt(jnp.sum(x * x, axis=-1, keepdims=True) * (1.0 / n) + EPS)
    return x * r * w, r


def _rms_bwd(x, w, dy, n):
    r = lax.rsqrt(jnp.sum(x * x, axis=-1, keepdims=True) * (1.0 / n) + EPS)
    xh = x * r
    gy = dy * w
    dx = r * (gy - xh * (jnp.sum(gy * xh, axis=-1, keepdims=True) * (1.0 / n)))
    return dx, dy * xh


def _rowsum(x):
    return jnp.sum(x, axis=0, keepdims=True)


def _row_ids(i, tm):
    return i * tm + lax.broadcasted_iota(jnp.int32, (tm, 1), 0)


def _shift_down(ext, s, tm):
    if s == 0:
        return ext[8:8 + tm]
    return pltpu.roll(ext, s, 0)[8:8 + tm]


def _shift_up(ext, s, tm):
    if s == 0:
        return ext[0:tm]
    return pltpu.roll(ext, tm + 8 - s, 0)[0:tm]


def _conv_fwd(x, halo_prev, w, width):
    tm = x.shape[0]
    ext = jnp.concatenate([halo_prev, x], axis=0)
    y = None
    for j in range(width):
        t = w[j:j + 1, :] * _shift_down(ext, width - 1 - j, tm)
        y = t if y is None else y + t
    return y


def _conv_bwd_x(dy, halo_next, w, width):
    tm = dy.shape[0]
    ext = jnp.concatenate([dy, halo_next], axis=0)
    dx = None
    for j in range(width):
        t = w[j:j + 1, :] * _shift_up(ext, width - 1 - j, tm)
        dx = t if dx is None else dx + t
    return dx


def _conv_bwd_w(dy, x, halo_prev, width):
    tm = dy.shape[0]
    ext = jnp.concatenate([halo_prev, x], axis=0)
    rows = [_rowsum(dy * _shift_down(ext, width - 1 - j, tm)) for j in range(width)]
    rows += [jnp.zeros_like(rows[0])] * (8 - width)
    return jnp.concatenate(rows, axis=0)


def _softplus(x):
    e = jnp.exp(-jnp.abs(x))
    u = 1.0 + e
    l1p = jnp.where(u == 1.0, e, jnp.log(u) * e / jnp.where(u == 1.0, 1.0, u - 1.0))
    return jnp.maximum(x, 0.0) + l1p


def _swap_halves(x):
    lane = lax.broadcasted_iota(jnp.int32, x.shape, 1)
    return jnp.where(lane < 32, pltpu.roll(x, 96, 1), jnp.where(lane < 64, pltpu.roll(x, 32, 1), 0.0))


class _In:
    def __init__(self, arr, width=None, cb=0, kind="cur"):
        self.arr, self.kind = arr, kind
        self.width = arr.shape[1] if width is None else width
        self.cb = cb


def _whole_spec(x):
    return pl.BlockSpec(x.shape, lambda i, nd=x.ndim: (0,) * nd, pipeline_mode=pl.Buffered(1))


def _tile_spec(t, tm, tp):
    r8 = tm // 8
    if t.kind == "cur":
        return pl.BlockSpec((tm, t.width), lambda i, cb=t.cb: (i, cb))
    if t.kind == "prev":
        return pl.BlockSpec((8, t.width), lambda i, cb=t.cb: (jnp.maximum(i * r8 - 1, 0), cb))
    return pl.BlockSpec((8, t.width), lambda i, cb=t.cb: (jnp.minimum((i + 1) * r8, tp // 8 - 1), cb))


def _rows(name, fn, tiled, full, outs, accs=(), tm=TM):
    tp = tiled[0].arr.shape[0]
    nt = tp // tm
    n_in = len(tiled) + len(full)
    n_out = len(outs)

    def body(*refs):
        i = pl.program_id(0)
        vals = [r[...] for r in refs[:n_in]]
        o_t, o_a = fn(i, *vals)
        for r, v in zip(refs[n_in:n_in + n_out], o_t):
            r[...] = v.astype(r.dtype)
        for r, v in zip(refs[n_in + n_out:], o_a):
            @pl.when(i == 0)
            def _():
                r[...] = v

            @pl.when(i > 0)
            def _():
                r[...] += v

    in_specs = [_tile_spec(t, tm, tp) for t in tiled]
    in_specs += [pl.BlockSpec(a.shape, lambda i, nd=a.ndim: (0,) * nd) for a in full]
    out_specs = [pl.BlockSpec((tm, w), lambda i: (i, 0)) for w, _ in outs]
    out_specs += [pl.BlockSpec((r, w), lambda i: (0, 0)) for r, w in accs]
    out_shape = [jax.ShapeDtypeStruct((tp, w), dt) for w, dt in outs]
    out_shape += [jax.ShapeDtypeStruct((r, w), F32) for r, w in accs]
    res = pl.pallas_call(
        body, name=name, grid=(nt,), in_specs=in_specs, out_specs=out_specs, out_shape=out_shape,
        compiler_params=pltpu.CompilerParams(dimension_semantics=("arbitrary",), vmem_limit_bytes=VMEM_LIMIT),
    )(*[t.arr for t in tiled], *full)
    return res


def _pick(n, cap, mult):
    best = None
    for d in range(mult, min(n, cap) + 1, mult):
        if n % d == 0:
            best = d
    assert best is not None, (n, cap, mult)
    return best


_ANY_SPEC = pl.BlockSpec(memory_space=pl.ANY)


def _mm(name, a, b, mode, out_dtype=F32, resid=None, after=None):
    if mode == "tn":
        m, k = a.shape
        n = b.shape[1]
        tk = _pick(k, 512, 128)
        tn = _pick(n, 1408, 128)

        def body_tn(a_ref, b_ref, o_ref):
            o_ref[...] = _tn(a_ref[...], b_ref[...]).astype(o_ref.dtype)

        return pl.pallas_call(
            body_tn, name=name, grid=(n // tn, k // tk),
            in_specs=[pl.BlockSpec((m, tk), lambda j, p: (0, p)),
                      pl.BlockSpec((m, tn), lambda j, p: (0, j))],
            out_specs=pl.BlockSpec((tk, tn), lambda j, p: (p, j)),
            out_shape=jax.ShapeDtypeStruct((k, n), out_dtype),
            compiler_params=pltpu.CompilerParams(
                dimension_semantics=("parallel", "parallel"), vmem_limit_bytes=VMEM_LIMIT),
        )(a, b)

    m, k = a.shape
    n = b.shape[1] if mode == "nn" else b.shape[0]
    tn = _pick(n, 1408, 128)
    tm = _pick(m, 1152, 16)
    dotf = _nn if mode == "nn" else _nt

    def body(*refs):
        a_ref, b_ref, o_ref = refs[0], refs[1], refs[-1]
        acc = dotf(a_ref[...], b_ref[...])
        if resid is not None:
            acc = refs[2][...] + acc
        o_ref[...] = acc.astype(o_ref.dtype)

    b_spec = (pl.BlockSpec((k, tn), lambda j, i: (0, j)) if mode == "nn"
              else pl.BlockSpec((tn, k), lambda j, i: (j, 0)))
    in_specs = [pl.BlockSpec((tm, k), lambda j, i: (i, 0)), b_spec]
    args = [a, b]
    if resid is not None:
        in_specs.append(pl.BlockSpec((tm, tn), lambda j, i: (i, j)))
        args.append(resid)
    if after is not None:
        in_specs.append(_ANY_SPEC)
        args.append(after)
    return pl.pallas_call(
        body, name=name, grid=(n // tn, m // tm), in_specs=in_specs,
        out_specs=pl.BlockSpec((tm, tn), lambda j, i: (i, j)),
        out_shape=jax.ShapeDtypeStruct((m, n), out_dtype),
        compiler_params=pltpu.CompilerParams(
            dimension_semantics=("parallel", "parallel"), vmem_limit_bytes=VMEM_LIMIT),
    )(*args)


def _mm_tn2(name, a1, a2, b, out_dtype=F32):
    m, k = a1.shape
    n = b.shape[1]
    tk = _pick(k, 512, 128)

    def body(a1_ref, a2_ref, b_ref, o1_ref, o2_ref):
        bb = b_ref[...]
        o1_ref[...] = _tn(a1_ref[...], bb).astype(o1_ref.dtype)
        o2_ref[...] = _tn(a2_ref[...], bb).astype(o2_ref.dtype)

    a_spec = pl.BlockSpec((m, tk), lambda p: (0, p))
    o_spec = pl.BlockSpec((tk, n), lambda p: (p, 0))
    return pl.pallas_call(
        body, name=name, grid=(k // tk,),
        in_specs=[a_spec, a_spec, pl.BlockSpec((m, n), lambda p: (0, 0))],
        out_specs=[o_spec, o_spec], out_shape=[jax.ShapeDtypeStruct((k, n), out_dtype)] * 2,
        compiler_params=pltpu.CompilerParams(dimension_semantics=("parallel",), vmem_limit_bytes=VMEM_LIMIT),
    )(a1, a2, b)


def _norm_mm(name, x, norm_w, b, mode="nt", x_cb=0, after=None):
    m = x.shape[0]
    k = norm_w.shape[1]
    n = b.shape[0] if mode == "nt" else b.shape[1]
    tn = _pick(n, 1408, 128)
    tm = _pick(m, 1152, 16)
    dotf = _nt if mode == "nt" else _nn
    extra = [] if after is None else [after]

    def body(x_ref, w_ref, b_ref, *rest):
        o_ref, u_ref = rest[-2:]

        @pl.when(pl.program_id(1) == 0)
        def _():
            u_ref[...] = _rms_fwd(x_ref[...], w_ref[...], k)[0].astype(u_ref.dtype)

        o_ref[...] = dotf(u_ref[...], b_ref[...])

    b_spec = (pl.BlockSpec((tn, k), lambda i, j: (j, 0)) if mode == "nt"
              else pl.BlockSpec((k, tn), lambda i, j: (0, j)))
    return pl.pallas_call(
        body, name=name, grid=(m // tm, n // tn),
        in_specs=[pl.BlockSpec((tm, k), lambda i, j: (i, x_cb)), pl.BlockSpec((1, k), lambda i, j: (0, 0)),
                  b_spec] + [_ANY_SPEC] * len(extra),
        out_specs=[pl.BlockSpec((tm, tn), lambda i, j: (i, j)), pl.BlockSpec((tm, k), lambda i, j: (i, 0))],
        out_shape=[jax.ShapeDtypeStruct((m, n), F32), jax.ShapeDtypeStruct((m, k), _MXU)],
        compiler_params=pltpu.CompilerParams(
            dimension_semantics=("arbitrary", "arbitrary"), vmem_limit_bytes=VMEM_LIMIT),
    )(x, norm_w, b, *extra)


def _pro_mm(name, fn, tiled, full, k, b, resid):
    m = resid.shape[0]
    n = b.shape[1]
    tm = _pick(m, 576, 16)
    n_in = len(tiled) + len(full)

    def body(*refs):
        i = pl.program_id(0)
        u = fn(i, *[r[...] for r in refs[:n_in]]).astype(_MXU)
        b_ref, r_ref, o_ref, u_ref = refs[n_in:]
        u_ref[...] = u
        o_ref[...] = r_ref[...] + _nn(u, b_ref[...])

    row = lambda w: pl.BlockSpec((tm, w), lambda i: (i, 0))
    in_specs = [_tile_spec(t, tm, m) for t in tiled]
    in_specs += [_whole_spec(x) for x in full] + [_whole_spec(b), row(n)]
    return pl.pallas_call(
        body, name=name, grid=(m // tm,), in_specs=in_specs, out_specs=[row(n), row(k)],
        out_shape=[jax.ShapeDtypeStruct((m, n), F32), jax.ShapeDtypeStruct((m, k), _MXU)],
        compiler_params=pltpu.CompilerParams(dimension_semantics=("parallel",), vmem_limit_bytes=VMEM_LIMIT),
    )(*[t.arr for t in tiled], *full, b, resid)


def _ffn_in(h2, norm_w, w_gate_t, w_up_t, conv_w8, conv_b):
    m, k = h2.shape
    n = w_gate_t.shape[0]
    tm = _pick(m, 288, 16)

    def body(x_ref, xp_ref, nw_ref, wg_ref, wu_ref, cw_ref, cb_ref, hn_ref, gp_ref, up_ref, act_ref):
        i = pl.program_id(0)
        nw = nw_ref[...]
        hn = _rms_fwd(x_ref[...], nw, k)[0].astype(_MXU)
        hn_prev = _rms_fwd(xp_ref[...], nw, k)[0].astype(_MXU)
        wg = wg_ref[...]
        gp = _nt(hn, wg)
        gp_prev = jnp.where(i > 0, _nt(hn_prev, wg), 0.0)
        up = _nt(hn, wu_ref[...])
        gate = _conv_fwd(gp, gp_prev, cw_ref[...], FFN_CONV) + cb_ref[...]
        hn_ref[...] = hn
        gp_ref[...] = gp
        up_ref[...] = up
        act_ref[...] = (_silu_parts(gate)[0] * up).astype(act_ref.dtype)

    row = lambda w: pl.BlockSpec((tm, w), lambda i: (i, 0))
    r8 = tm // 8
    return pl.pallas_call(
        body, name="ffn_in", grid=(m // tm,),
        in_specs=[row(k), pl.BlockSpec((8, k), lambda i: (jnp.maximum(i * r8 - 1, 0), 0)), _whole_spec(norm_w),
                  _whole_spec(w_gate_t), _whole_spec(w_up_t), _whole_spec(conv_w8), _whole_spec(conv_b)],
        out_specs=[row(k), row(n), row(n), row(n)],
        out_shape=[jax.ShapeDtypeStruct((m, k), _MXU), jax.ShapeDtypeStruct((m, n), F32),
                   jax.ShapeDtypeStruct((m, n), F32), jax.ShapeDtypeStruct((m, n), _MXU)],
        compiler_params=pltpu.CompilerParams(dimension_semantics=("parallel",), vmem_limit_bytes=VMEM_LIMIT),
    )(h2, h2, norm_w, w_gate_t, w_up_t, conv_w8, conv_b)


def _mm_rows(name, a, b, mode, fn, tiled, full, outs, accs=(), tm_cap=576):
    a_list = list(a) if isinstance(a, (list, tuple)) else [a]
    b_list = list(b) if isinstance(b, (list, tuple)) else [b]
    na = len(a_list)
    m = a_list[0].shape[0]
    tm = _pick(m, tm_cap, 16)
    dotf = _nn if mode == "nn" else _nt
    n_in = len(tiled) + len(full)
    n_out = len(outs)
    first = 2 * na

    def body(*refs):
        i = pl.program_id(0)
        vals = [r[...] for r in refs[first:first + n_in]]
        acc = dotf(refs[0][...], refs[na][...])
        for p in range(1, na):
            acc = acc + dotf(refs[p][...], refs[na + p][...])
        o_t, o_a = fn(i, acc, *vals)
        for r, v in zip(refs[first + n_in:first + n_in + n_out], o_t):
            r[...] = v.astype(r.dtype)
        for r, v in zip(refs[first + n_in + n_out:], o_a):
            @pl.when(i == 0)
            def _():
                r[...] = v

            @pl.when(i > 0)
            def _():
                r[...] += v

    whole = lambda x: pl.BlockSpec(x.shape, lambda i, nd=x.ndim: (0,) * nd)
    in_specs = [pl.BlockSpec((tm, x.shape[1]), lambda i: (i, 0)) for x in a_list] + [_whole_spec(x) for x in b_list]
    in_specs += [_tile_spec(t, tm, m) for t in tiled]
    in_specs += [whole(x) for x in full]
    out_specs = [pl.BlockSpec((tm, w), lambda i: (i, 0)) for w, _ in outs]
    out_specs += [pl.BlockSpec((r, w), lambda i: (0, 0)) for r, w in accs]
    out_shape = [jax.ShapeDtypeStruct((m, w), dt) for w, dt in outs]
    out_shape += [jax.ShapeDtypeStruct((r, w), F32) for r, w in accs]
    return pl.pallas_call(
        body, name=name, grid=(m // tm,), in_specs=in_specs, out_specs=out_specs, out_shape=out_shape,
        compiler_params=pltpu.CompilerParams(dimension_semantics=("arbitrary",), vmem_limit_bytes=VMEM_LIMIT),
    )(*a_list, *b_list, *[t.arr for t in tiled], *full)


ATTN_Q_TILES = 4


def _attn_probs(q, k, row0):
    tq, tp = q.shape[0], k.shape[0]
    s = _nt(q, k) * (1.0 / math.sqrt(QK_HEAD))
    row = row0 + lax.broadcasted_iota(jnp.int32, (tq, tp), 0)
    col = lax.broadcasted_iota(jnp.int32, (tq, tp), 1)
    ok = (col <= row) & (col >= PAD)
    s = jnp.where(ok, s, NEG)
    m = jnp.max(s, axis=-1, keepdims=True)
    e = jnp.exp(s - m)
    return e * (1.0 / jnp.sum(e, axis=-1, keepdims=True))


def _attn_fwd(q, k, v):
    tp = q.shape[0]
    tq = tp // ATTN_Q_TILES

    def body(q_ref, k_ref, v_ref, o_ref):
        for i in range(ATTN_Q_TILES):
            rows = slice(i * tq, (i + 1) * tq)
            keys = slice(0, (i + 1) * tq)
            p = _attn_probs(q_ref[rows, :], k_ref[keys, :], i * tq)
            o_ref[rows, :] = _nn(p, v_ref[keys, :])

    return pl.pallas_call(
        body, name="attn_fwd", grid=(MLA_HEADS,),
        in_specs=[pl.BlockSpec((tp, HP), lambda h: (0, h)),
                  pl.BlockSpec((tp, HP), lambda h: (0, h)),
                  pl.BlockSpec((tp, V_HEAD), lambda h: (0, h))],
        out_specs=pl.BlockSpec((tp, V_HEAD), lambda h: (0, h)),
        out_shape=jax.ShapeDtypeStruct((tp, MLA_HEADS * V_HEAD), F32),
        compiler_params=pltpu.CompilerParams(dimension_semantics=("parallel",), vmem_limit_bytes=VMEM_LIMIT),
    )(q, k, v)


def _attn_bwd(q, k, v, do):
    tp = q.shape[0]
    tq = tp // ATTN_Q_TILES

    def body(q_ref, k_ref, v_ref, do_ref, dq_ref, dk_ref, dv_ref):
        for i in reversed(range(ATTN_Q_TILES)):
            rows = slice(i * tq, (i + 1) * tq)
            keys = slice(0, (i + 1) * tq)
            qb = q_ref[rows, :]
            kk = k_ref[keys, :]
            dob = do_ref[rows, :]
            p = _attn_probs(qb, kk, i * tq)
            dp = _nt(dob, v_ref[keys, :])
            delta = jnp.sum(p * dp, axis=-1, keepdims=True)
            ds = p * (dp - delta) * (1.0 / math.sqrt(QK_HEAD))
            dq_ref[rows, :] = _nn(ds, kk)
            if i == ATTN_Q_TILES - 1:
                dk_ref[...] = _tn(ds, qb)
                dv_ref[...] = _tn(p, dob)
            else:
                dk_ref[keys, :] += _tn(ds, qb)
                dv_ref[keys, :] += _tn(p, dob)

    full = lambda w: pl.BlockSpec((tp, w), lambda h: (0, h))
    return pl.pallas_call(
        body, name="attn_bwd", grid=(MLA_HEADS,),
        in_specs=[full(HP), full(HP), full(V_HEAD), full(V_HEAD)],
        out_specs=[full(HP), full(HP), full(V_HEAD)],
        out_shape=[jax.ShapeDtypeStruct((tp, MLA_HEADS * HP), F32),
                   jax.ShapeDtypeStruct((tp, MLA_HEADS * HP), F32),
                   jax.ShapeDtypeStruct((tp, MLA_HEADS * V_HEAD), F32)],
        compiler_params=pltpu.CompilerParams(dimension_semantics=("parallel",), vmem_limit_bytes=VMEM_LIMIT),
    )(q, k, v, do)


def _gdn_consts():
    c = DN_CHUNK
    r = lax.broadcasted_iota(jnp.int32, (c, c), 0)
    cc = lax.broadcasted_iota(jnp.int32, (c, c), 1)
    incl = r >= cc
    strict = r > cc
    return incl, strict


def _cumsum_rows(x, reverse=False):
    c = x.shape[0]
    row = lax.broadcasted_iota(jnp.int32, x.shape, 0)
    s = 1
    while s < c:
        if reverse:
            x = x + jnp.where(row < c - s, pltpu.roll(x, c - s, 0), 0.0)
        else:
            x = x + jnp.where(row >= s, pltpu.roll(x, s, 0), 0.0)
        s *= 2
    return x


def _each(fn, *lists):
    return [fn(*a) for a in zip(*lists)]


def _interleave(chains):
    chains = list(chains)
    while chains:
        for ch in list(chains):
            try:
                next(ch)
            except StopIteration:
                chains.remove(ch)


def _gdn_chunk_common(q_ref, k_ref, v_ref, g_ref, b_ref):
    c = DN_CHUNK
    incl, strict = _gdn_consts()
    sls = [(slice(c * sub, c * (sub + 1)), slice(DN_DIM * h, DN_DIM * (h + 1)))
           for sub in range(GDN_SUB_CHUNKS) for h in range(DN_HEADS)]
    q = [q_ref[sl] * (1.0 / math.sqrt(DN_DIM)) for sl in sls]
    k = [k_ref[sl] for sl in sls]
    v = [v_ref[sl] for sl in sls]
    g = [g_ref[sl] for sl in sls]
    beta = [b_ref[sl] for sl in sls]
    gc = [_cumsum_rows(x) for x in g]
    grow = [x.T[:c, :] for x in gc]
    kb = _each(jnp.multiply, k, beta)
    kk = _each(_nt, kb, k)
    qk = _each(_nt, q, k)
    gam = [jnp.exp(x) for x in gc]
    g_last = [_rowsum(x) for x in g]
    dm = [jnp.exp(jnp.where(incl, x[:, :c] - y, NEG)) for x, y in zip(gc, grow)]
    vb = _each(jnp.multiply, v, beta)
    kbg = _each(jnp.multiply, kb, gam)
    ek = [jnp.exp(x - y) for x, y in zip(g_last, gc)]
    kd = _each(jnp.multiply, k, ek)
    return dict(q=q, k=k, v=v, beta=beta, gc=gc, gam=gam, g_last=g_last, dm=dm, kb=kb, vb=vb,
                kbg=kbg, kk=kk, ek=ek, kd=kd, qk=qk, incl=incl, strict=strict, sls=sls)


def _gdn_fwd(proj, conv_w8, alog, dtb):
    tp = proj.shape[0]
    c = DN_CHUNK
    nch = tp // c
    blk = GDN_SUB_CHUNKS * c

    def body(x_ref, xp_ref, ab_ref, w8_ref, alog_ref, dtb_ref,
             o_ref, s_ref, t_ref, q_ref, k_ref, v_ref, g_ref, b_ref, s_scr):
        @pl.when(pl.program_id(0) == 0)
        def _():
            s_scr[...] = jnp.zeros_like(s_scr)

        staged, _ = _f_gdn_prep(pl.program_id(0), x_ref[...], xp_ref[...], ab_ref[...], w8_ref[...],
                                alog_ref[...], dtb_ref[...])
        for ref, val in zip((q_ref, k_ref, v_ref, g_ref, b_ref), staged):
            ref[...] = val
        eye = (lax.broadcasted_iota(jnp.int32, (c, c), 0) == lax.broadcasted_iota(jnp.int32, (c, c), 1)).astype(F32)
        x = _gdn_chunk_common(q_ref, k_ref, v_ref, g_ref, b_ref)
        heads = range(DN_HEADS)
        bp = [-jnp.where(x["strict"], kk * dm, 0.0) for kk, dm in zip(x["kk"], x["dm"])]
        t = [eye + b for b in bp]
        for _ in range(5):
            bp = [_nn(b, b, hp="3x") for b in bp]
            t = [tt + _nn(tt, b, hp="3x") for tt, b in zip(t, bp)]
        u = _each(_nn, t, x["vb"])
        w = _each(_nn, t, x["kbg"])
        qg = _each(jnp.multiply, x["q"], x["gam"])
        mqk = _each(jnp.multiply, x["qk"], x["dm"])
        s = [s_scr[h] for h in heads]
        for sub in range(GDN_SUB_CHUNKS):
            e = [DN_HEADS * sub + h for h in heads]
            v_new = [u[i] - _nn(w[i], s[h]) for h, i in zip(heads, e)]
            o = [_nn(qg[i], s[h]) + _nn(mqk[i], v_new[h]) for h, i in zip(heads, e)]
            s_new = [s[h] * jnp.exp(x["g_last"][i]) + _tn(x["kd"][i], v_new[h]) for h, i in zip(heads, e)]
            for h, i in zip(heads, e):
                s_ref[h, sub] = s[h]
                t_ref[h, sub] = t[i]
                o_ref[x["sls"][i]] = o[h]
            s = s_new
        for h in heads:
            s_scr[h] = s[h]

    sub = GDN_SUB_CHUNKS
    rb = lambda n: (n, 0)
    rows = pl.BlockSpec((blk, DN_WIDTH), rb)
    whole = lambda a: pl.BlockSpec(a.shape, lambda n: (0, 0))
    return pl.pallas_call(
        body, name="gdn_fwd", grid=(nch // sub,),
        in_specs=[pl.BlockSpec((blk, 3 * DN_WIDTH), rb),
                  pl.BlockSpec((8, 3 * DN_WIDTH), lambda n: (jnp.maximum(n * (blk // 8) - 1, 0), 0)),
                  pl.BlockSpec((blk, LANE), lambda n: (n, C_AB // LANE)),
                  whole(conv_w8), whole(alog), whole(dtb)],
        out_specs=[rows,
                   pl.BlockSpec((DN_HEADS, sub, DN_DIM, DN_DIM), lambda n: (0, n, 0, 0)),
                   pl.BlockSpec((DN_HEADS, sub, c, c), lambda n: (0, n, 0, 0))] + [rows] * 5,
        out_shape=[jax.ShapeDtypeStruct((tp, DN_WIDTH), F32),
                   jax.ShapeDtypeStruct((DN_HEADS, nch, DN_DIM, DN_DIM), F32),
                   jax.ShapeDtypeStruct((DN_HEADS, nch, c, c), F32)] + [jax.ShapeDtypeStruct((tp, DN_WIDTH), F32)] * 5,
        scratch_shapes=[pltpu.VMEM((DN_HEADS, DN_DIM, DN_DIM), F32)],
        compiler_params=pltpu.CompilerParams(dimension_semantics=("arbitrary",), vmem_limit_bytes=VMEM_LIMIT),
    )(proj, proj, proj, conv_w8, alog, dtb)


def _gdn_bwd(q, k, v, g, beta, s_all, t_all, do, proj, conv_w8, alog, dtb, after):
    tp = q.shape[0]
    c = DN_CHUNK
    nch = tp // c
    nblk = nch // GDN_SUB_CHUNKS
    blk = GDN_SUB_CHUNKS * c

    def body(q_ref, k_ref, v_ref, g_ref, b_ref, s_ref, t_ref, do_ref, x_ref, xp_ref, xn_ref, ab_ref,
             w8_ref, alog_ref, dtb_ref, _after_ref, dqkv_ref, dab_ref, dcw_ref, dalog_ref, ddtb_ref,
             ds_scr, dq_ref, dk_ref, dv_ref, dg_ref, db_ref, nxt_scr):
        step = pl.program_id(0)

        @pl.when(step == 0)
        def _():
            ds_scr[...] = jnp.zeros_like(ds_scr)
            nxt_scr[...] = jnp.zeros_like(nxt_scr)

        xs = _gdn_chunk_common(q_ref, k_ref, v_ref, g_ref, b_ref)

        ds_state = [ds_scr[h] for h in range(DN_HEADS)]

        def chain(sub, h):
            e = DN_HEADS * sub + h
            x = {key: (val[e] if isinstance(val, list) else val) for key, val in xs.items()}
            sl = x["sls"]
            qs, kx, vx, beta_, gam, dm = x["q"], x["k"], x["v"], x["beta"], x["gam"], x["dm"]
            kb, vb, kbg, kd, ek = x["kb"], x["vb"], x["kbg"], x["kd"], x["ek"]
            t = t_ref[h, sub]
            s = s_ref[h, sub]
            dsn = ds_state[h]
            dob = do_ref[sl]
            eg_last = jnp.exp(x["g_last"])
            u = _nn(t, vb)
            w = _nn(t, kbg)
            mqk = x["qk"] * dm
            qd = qs * gam
            dqd = _nt(dob, s)
            dkd_pre = _nn(kd, dsn)
            yield
            v_new = u - _nn(w, s)
            dv_new = _tn(mqk, dob) + dkd_pre
            dq = dqd * gam
            dgam = jnp.sum(dqd * qs, axis=1, keepdims=True)
            yield
            ds_state[h] = _tn(qd, dob) + eg_last * dsn - _tn(w, dv_new)
            dmm = jnp.where(x["incl"], _nt(dob, v_new), 0.0)
            dkd = _nt(v_new, dsn)
            dw = -_nt(dv_new, s)
            dvb = _tn(t, dv_new)
            dt = _nt(dv_new, vb)
            yield
            dqk = dmm * dm
            e_mat = dmm * mqk
            dq = dq + _nn(dqk, kx)
            dk = _tn(dqk, qs) + dkd * ek
            e1 = jnp.sum(dkd * kd, axis=1, keepdims=True)
            dgc = -e1
            dg_last = jnp.sum(e1) + eg_last * jnp.sum(s * dsn)
            dt = dt + _nt(dw, kbg)
            dkbg = _tn(t, dw)
            yield
            tdt = _tn(t, dt, hp="3x")
            yield
            da = jnp.where(x["strict"], -_nt(tdt, t, hp="3x"), 0.0)
            yield
            dkk = da * dm
            e_mat = e_mat + da * x["kk"] * dm
            dkb = _nn(dkk, kx) + dkbg * gam
            dk = dk + _tn(dkk, kb)
            dgam = dgam + jnp.sum(dkbg * kb, axis=1, keepdims=True)
            yield
            dk = dk + dkb * beta_
            dbeta = jnp.sum(dkb * kx, axis=1, keepdims=True) + jnp.sum(dvb * vx, axis=1, keepdims=True)
            dv = dvb * beta_
            dgc = dgc + jnp.sum(e_mat, axis=1, keepdims=True) + dgam * gam
            dgc = dgc - jnp.sum(e_mat.T, axis=1, keepdims=True)
            yield
            dg = _cumsum_rows(dgc, reverse=True) + dg_last
            yield
            dq_ref[sl] = dq * (1.0 / math.sqrt(DN_DIM))
            dk_ref[sl] = dk
            dv_ref[sl] = dv
            dg_ref[sl] = dg
            db_ref[sl] = jnp.broadcast_to(dbeta, (c, LANE))

        chains = []
        for sub in reversed(range(GDN_SUB_CHUNKS)):
            new = [chain(sub, h) for h in range(DN_HEADS)]
            for _ in range(3):
                for ch in new:
                    next(ch)
            chains += new
        _interleave(chains)
        for h in range(DN_HEADS):
            ds_scr[h] = ds_state[h]

        dq, dk, dv = dq_ref[...], dk_ref[...], dv_ref[...]
        outs, accs = _f_gdn_prep_bwd(
            nblk - 1 - step, x_ref[...], xp_ref[...], xn_ref[...], ab_ref[...], dq, nxt_scr[0], dk, nxt_scr[1],
            dv, nxt_scr[2], dg_ref[...], db_ref[...], w8_ref[...], alog_ref[...], dtb_ref[...], nt=nblk)
        nxt_scr[0] = dq[:8]
        nxt_scr[1] = dk[:8]
        nxt_scr[2] = dv[:8]
        dqkv_ref[...] = outs[0].astype(dqkv_ref.dtype)
        dab_ref[...] = outs[1].astype(dab_ref.dtype)
        for ref, val in zip((dcw_ref, dalog_ref, ddtb_ref), accs):
            @pl.when(step == 0)
            def _():
                ref[...] = val

            @pl.when(step > 0)
            def _():
                ref[...] += val

    sub = GDN_SUB_CHUNKS
    r8 = blk // 8
    rb = lambda n: (nblk - 1 - n, 0)
    hs = lambda n: (0, nblk - 1 - n, 0, 0)
    rows = pl.BlockSpec((blk, DN_WIDTH), rb)
    whole = lambda a: pl.BlockSpec(a.shape, lambda n: (0,) * a.ndim)
    wide = 3 * DN_WIDTH
    return pl.pallas_call(
        body, name="gdn_bwd", grid=(nblk,),
        in_specs=[rows] * 5
        + [pl.BlockSpec((DN_HEADS, sub, DN_DIM, DN_DIM), hs), pl.BlockSpec((DN_HEADS, sub, c, c), hs), rows,
           pl.BlockSpec((blk, wide), rb),
           pl.BlockSpec((8, wide), lambda n: (jnp.maximum((nblk - 1 - n) * r8 - 1, 0), 0)),
           pl.BlockSpec((8, wide), lambda n: (jnp.minimum((nblk - n) * r8, tp // 8 - 1), 0)),
           pl.BlockSpec((blk, LANE), lambda n: (nblk - 1 - n, C_AB // LANE)),
           whole(conv_w8), whole(alog), whole(dtb), _ANY_SPEC],
        out_specs=[pl.BlockSpec((blk, wide), rb), pl.BlockSpec((blk, LANE), rb),
                   whole(conv_w8), whole(alog), whole(dtb)],
        out_shape=[jax.ShapeDtypeStruct((tp, wide), _MXU), jax.ShapeDtypeStruct((tp, LANE), _MXU),
                   jax.ShapeDtypeStruct(conv_w8.shape, F32), jax.ShapeDtypeStruct(alog.shape, F32),
                   jax.ShapeDtypeStruct(dtb.shape, F32)],
        scratch_shapes=[pltpu.VMEM((DN_HEADS, DN_DIM, DN_DIM), F32)] + [pltpu.VMEM((blk, DN_WIDTH), F32)] * 5
        + [pltpu.VMEM((3, 8, DN_WIDTH), F32)],
        compiler_params=pltpu.CompilerParams(dimension_semantics=("arbitrary",), vmem_limit_bytes=VMEM_LIMIT),
    )(q, k, v, g, beta, s_all, t_all, do, proj, proj, proj, proj, conv_w8, alog, dtb, after)


def _silu_parts(x):
    s = _sigmoid(x)
    return x * s, s * (1.0 + x * (1.0 - s))


def _f_rms_bwd_add(i, x, dy, dres, w, *, mask_pad):
    dx, dwr = _rms_bwd(x, w, dy, x.shape[1])
    out = dres + dx
    if mask_pad:
        out = jnp.where(_row_ids(i, x.shape[0]) >= PAD, out, 0.0)
    return (out,), (_rowsum(dwr),)


def _rope(x, cos, sin_s):
    return x * cos + _swap_halves(x) * sin_s


def _rope_t(dy, cos, sin_s):
    return dy * cos + _swap_halves(dy * sin_s)


def _f_mla_qk(i, qf, kvf, kpe, cos, sin_s, qw, kw):
    qs, ks, vs = [], [], []
    for h in range(MLA_HEADS):
        qn, _ = _rms_fwd(qf[:, HP * h:HP * (h + 1)], qw, QK_HEAD)
        qs += [qn[:, :QK_NOPE], _rope(qn[:, QK_NOPE:], cos, sin_s)]
        kh = jnp.concatenate([kvf[:, HP * h:HP * h + QK_NOPE], kpe], axis=1)
        kn, _ = _rms_fwd(kh, kw, QK_HEAD)
        ks += [kn[:, :QK_NOPE], _rope(kn[:, QK_NOPE:], cos, sin_s)]
        vs.append(kvf[:, HP * h + QK_NOPE:HP * (h + 1)])
    return (jnp.concatenate(qs, axis=1), jnp.concatenate(ks, axis=1), jnp.concatenate(vs, axis=1)), ()


def _f_mla_front(i, ql, kvl, kpe, cos, sin_s, qaw, kvaw, wq_t, wkv, qw, kw):
    qn = _rms_fwd(ql, qaw, Q_LORA)[0].astype(_MXU)
    kvn = _rms_fwd(kvl, kvaw, KV_LORA)[0].astype(_MXU)
    qf = _nt(qn, wq_t)
    kvf = _nn(kvn, wkv)
    (q, k, v), _ = _f_mla_qk(i, qf, kvf, kpe, cos, sin_s, qw, kw)
    return (qn, kvn, qf, kvf, q, k, v), ()


def _f_mla_back(i, qf, kvf, kpe, cos, sin_s, dq, dk, dv, ql, kvl, qaw, kvaw, wq_t, wkv, qw, kw):
    (dqf, dkvf, dkpe), (dqw, dkw) = _f_mla_qk_bwd(i, qf, kvf, kpe, cos, sin_s, dq, dk, dv, qw, kw)
    dqf = dqf.astype(_MXU)
    dkvf = dkvf.astype(_MXU)
    dql, dqaw = _rms_bwd(ql, qaw, _nn(dqf, wq_t), Q_LORA)
    dkvl, dkvaw = _rms_bwd(kvl, kvaw, _nt(dkvf, wkv), KV_LORA)
    return (dqf, dkvf, dkpe, dql, dkvl), (dqw, dkw, _rowsum(dqaw), _rowsum(dkvaw))


def _f_mla_qk_bwd(i, qf, kvf, kpe, cos, sin_s, dq, dk, dv, qw, kw):
    dqf, dkvf = [], []
    dkpe = None
    dqw = None
    dkw = None
    for h in range(MLA_HEADS):
        dqh = dq[:, HP * h:HP * (h + 1)]
        dqn = jnp.concatenate([dqh[:, :QK_NOPE], _rope_t(dqh[:, QK_NOPE:], cos, sin_s)], axis=1)
        dx, dwr = _rms_bwd(qf[:, HP * h:HP * (h + 1)], qw, dqn, QK_HEAD)
        dqf.append(dx)
        dqw = _rowsum(dwr) if dqw is None else dqw + _rowsum(dwr)
        dkh = dk[:, HP * h:HP * (h + 1)]
        dkn = jnp.concatenate([dkh[:, :QK_NOPE], _rope_t(dkh[:, QK_NOPE:], cos, sin_s)], axis=1)
        kh = jnp.concatenate([kvf[:, HP * h:HP * h + QK_NOPE], kpe], axis=1)
        dx, dwr = _rms_bwd(kh, kw, dkn, QK_HEAD)
        dkvf += [dx[:, :QK_NOPE], dv[:, V_HEAD * h:V_HEAD * (h + 1)]]
        dkpe = dx[:, QK_NOPE:] if dkpe is None else dkpe + dx[:, QK_NOPE:]
        dkw = _rowsum(dwr) if dkw is None else dkw + _rowsum(dwr)
    return (jnp.concatenate(dqf, axis=1), jnp.concatenate(dkvf, axis=1), dkpe), (dqw, dkw)


def _gdn_act(i, x, halo, w8):
    halo = jnp.where(i > 0, halo, 0.0)
    c = _conv_fwd(x, halo, w8, DN_CONV)
    act, dact = _silu_parts(c)
    return act, dact


def _spread_heads(ab):
    tm = ab.shape[0]
    return jnp.concatenate([jnp.broadcast_to(ab[:, h:h + 1], (tm, DN_DIM)) for h in range(2 * DN_HEADS)], axis=1)


def _gather_heads(x):
    tm = x.shape[0]
    lane = lax.broadcasted_iota(jnp.int32, (tm, LANE), 1)
    out = jnp.zeros((tm, LANE), F32)
    for h in range(2 * DN_HEADS):
        out = out + jnp.where(lane == h, x[:, DN_DIM * h:DN_DIM * h + 1], 0.0)
    return out


def _f_gdn_prep(i, x, halo, ab, w8, alog, dtb):
    tm = x.shape[0]
    act, _ = _gdn_act(i, x, halo, w8)
    outs = []
    for part in range(2):
        for h in range(DN_HEADS):
            t = act[:, DN_WIDTH * part + DN_DIM * h:DN_WIDTH * part + DN_DIM * (h + 1)]
            outs.append(t * lax.rsqrt(jnp.sum(t * t, axis=-1, keepdims=True) + EPS))
    q = jnp.concatenate(outs[:DN_HEADS], axis=1)
    k = jnp.concatenate(outs[DN_HEADS:], axis=1)
    v = act[:, 2 * DN_WIDTH:]
    abb = _spread_heads(ab)
    valid = _row_ids(i, tm) >= PAD
    g = jnp.where(valid, -jnp.exp(alog) * _softplus(abb[:, :DN_WIDTH] + dtb), 0.0)
    beta = jnp.where(valid, _sigmoid(abb[:, DN_WIDTH:]), 0.0)
    return (q, k, v, g, beta), ()


def _f_gdn_prep_bwd(i, x, x_prev, x_next, ab, dq, dq_next, dk, dk_next, dv, dv_next, dg, dbeta,
                    w8, alog, dtb, *, nt):
    tm = x.shape[0]
    x_prev = jnp.where(i > 0, x_prev, 0.0)
    more = i < nt - 1
    ext = lambda t, t_next: jnp.concatenate([t, jnp.where(more, t_next, 0.0)], axis=0)
    c = _conv_fwd(jnp.concatenate([x, x_next], axis=0), x_prev, w8, DN_CONV)
    act, dact = _silu_parts(c)
    douts = []
    for part, dd in enumerate((ext(dq, dq_next), ext(dk, dk_next))):
        for h in range(DN_HEADS):
            t = act[:, DN_WIDTH * part + DN_DIM * h:DN_WIDTH * part + DN_DIM * (h + 1)]
            r = lax.rsqrt(jnp.sum(t * t, axis=-1, keepdims=True) + EPS)
            y = t * r
            dy = dd[:, DN_DIM * h:DN_DIM * (h + 1)]
            douts.append(r * (dy - y * jnp.sum(dy * y, axis=-1, keepdims=True)))
    douts.append(ext(dv, dv_next))
    dc = jnp.concatenate(douts, axis=1) * dact
    dqkv = _conv_bwd_x(dc[:tm], dc[tm:], w8, DN_CONV)
    dconv_w = _conv_bwd_w(dc[:tm], x, x_prev, DN_CONV)
    abb = _spread_heads(ab)
    valid = _row_ids(i, tm) >= PAD
    pre = abb[:, :DN_WIDTH] + dtb
    ea = jnp.exp(alog)
    g = -ea * _softplus(pre)
    dg = jnp.where(valid, dg, 0.0)
    dbeta = jnp.where(valid, dbeta, 0.0)
    da = dg * (-ea) * _sigmoid(pre)
    beta = _sigmoid(abb[:, DN_WIDTH:])
    db = dbeta * beta * (1.0 - beta)
    dab = _gather_heads(jnp.concatenate([da, db], axis=1))
    return (dqkv, dab), (dconv_w, _rowsum(dg * g), _rowsum(da))


def _f_mix(i, o_mla, o_dn, z, w_mla, w_dn):
    tm = o_mla.shape[0]
    valid = _row_ids(i, tm) >= PAD
    outs = []
    for h in range(MLA_HEADS):
        y, _ = _rms_fwd(o_mla[:, V_HEAD * h:V_HEAD * (h + 1)], w_mla, V_HEAD)
        outs.append(jnp.where(valid, y, 0.0))
    for h in range(DN_HEADS):
        y, _ = _rms_fwd(o_dn[:, DN_DIM * h:DN_DIM * (h + 1)], w_dn, DN_DIM)
        outs.append(y * _silu_parts(z[:, DN_DIM * h:DN_DIM * (h + 1)])[0])
    return (jnp.concatenate(outs, axis=1),), ()


def _f_mix_bwd(i, o_mla, o_dn, z, dy_mla, dy_dn, w_mla, w_dn):
    tm = o_mla.shape[0]
    valid = _row_ids(i, tm) >= PAD
    d_mla, d_dn, d_z = [], [], []
    dw_mla = None
    dw_dn = None
    for h in range(MLA_HEADS):
        sl = slice(V_HEAD * h, V_HEAD * (h + 1))
        dx, dwr = _rms_bwd(o_mla[:, sl], w_mla, jnp.where(valid, dy_mla[:, sl], 0.0), V_HEAD)
        d_mla.append(dx)
        dw_mla = _rowsum(dwr) if dw_mla is None else dw_mla + _rowsum(dwr)
    for h in range(DN_HEADS):
        sl = slice(DN_DIM * h, DN_DIM * (h + 1))
        y, _ = _rms_fwd(o_dn[:, sl], w_dn, DN_DIM)
        sz, dsz = _silu_parts(z[:, sl])
        d_z.append(dy_dn[:, sl] * y * dsz)
        dx, dwr = _rms_bwd(o_dn[:, sl], w_dn, dy_dn[:, sl] * sz, DN_DIM)
        d_dn.append(dx)
        dw_dn = _rowsum(dwr) if dw_dn is None else dw_dn + _rowsum(dwr)
    return ((jnp.concatenate(d_mla, axis=1), jnp.concatenate(d_dn, axis=1), jnp.concatenate(d_z, axis=1)),
            (dw_mla, dw_dn))


def _f_ffn_act_bwd(i, gp, gp_prev, gp_next, up, up_next, dact, dact_next, w8, b, *, nt):
    tm = gp.shape[0]
    gp_prev = jnp.where(i > 0, gp_prev, 0.0)
    dact_next = jnp.where(i < nt - 1, dact_next, 0.0)
    cat = lambda t, t_next: jnp.concatenate([t, t_next], axis=0)
    gate = _conv_fwd(cat(gp, gp_next), gp_prev, w8, FFN_CONV) + b
    sg, dsg = _silu_parts(gate)
    dact_e = cat(dact, dact_next)
    dgate = dact_e * cat(up, up_next) * dsg
    dgate_pre = _conv_bwd_x(dgate[:tm], dgate[tm:], w8, FFN_CONV)
    dup = dact * sg[:tm]
    return (dgate_pre, dup), (_conv_bwd_w(dgate[:tm], gp, gp_prev, FFN_CONV), _rowsum(dgate[:tm]))


def _f_loss(i, h3, tgt):
    tm = h3.shape[0]
    diff = jnp.where(_row_ids(i, tm) >= ROW0, h3 - tgt, 0.0)
    part = 0.5 * jnp.sum(diff * diff) * (1.0 / D_MODEL)
    return (diff * (1.0 / D_MODEL),), (jnp.full((1, LANE), part, F32),)


def _local_step(h0, tgt, w, token, late_weights, grads_ready):
    tp = h0.shape[0]
    nt = tp // TM
    proj, u = _norm_mm("in_proj", h0, w["attn_norm_w"], w["w_in"], after=token)
    p_qkv = lambda kind="cur": _In(proj, 3 * DN_WIDTH, 0, kind)
    p_z = _In(proj, DN_WIDTH, C_Z // DN_WIDTH)
    p_ql = _In(proj, Q_LORA, C_QL // Q_LORA)
    p_kvl = _In(proj, KV_LORA, C_KVL // KV_LORA)
    p_kpe = _In(proj, LANE, C_KPE // LANE)
    p_ab = _In(proj, LANE, C_AB // LANE)
    cos, sin_s = _In(w["cos"]), _In(w["sin_s"])

    mla_w = [w["q_a_norm_w"], w["kv_a_norm_w"], w["w_q_b"], w["w_kv_b"], w["q_norm_w"], w["k_norm_w"]]
    tm_mla = _pick(tp, 288, 16)
    wide = MLA_HEADS * HP
    qn, kvn, qf, kvf, q, k, v = _rows(
        "mla_front", _f_mla_front, [p_ql, p_kvl, p_kpe, cos, sin_s], mla_w,
        [(Q_LORA, _MXU), (KV_LORA, _MXU), (wide, F32), (wide, F32), (wide, _MXU), (wide, _MXU),
         (MLA_HEADS * V_HEAD, _MXU)], tm=tm_mla)
    o_mla = _attn_fwd(q, k, v)

    dn_w = [w["dn_conv_w"], w["alog_b"], w["dtb_b"]]
    o_dn, s_all, t_all, gq, gk, gv, gg, gb = _gdn_fwd(proj, *dn_w)

    out_w = [w["mla_out_norm_w"], w["dn_out_norm_w"]]
    w = dict(w, **late_weights((o_mla, o_dn), _LATE[:3]))
    h2, mixed = _pro_mm("mix_out_proj", lambda i, *t: _f_mix(i, *t)[0][0], [_In(o_mla), _In(o_dn), p_z], out_w,
                        D_MODEL, w["w_out"], h0)

    ffn_w = [w["ffn_conv_w"], w["ffn_conv_b"]]
    hn, gate_pre, up, act = _ffn_in(h2, w["ffn_norm_w"], w["w_gate"], w["w_up"], *ffn_w)
    w = dict(w, **late_weights(act, _LATE[3:]))
    dh3, loss = _mm_rows("ffn_down_loss", act, w["w_down"], "nn", lambda i, y, r, t: _f_loss(i, r + y, t),
                         [_In(h2), _In(tgt)], [], [(D_MODEL, F32)], [(1, LANE)])

    g = {}
    dact = _mm("ffn_down_dx", dh3, w["w_down"], "nt")
    g["w_down"] = _mm("ffn_down_dw", act, dh3, "tn", out_dtype=_MXU)
    dgate_pre, dup, g["ffn_conv_w"], g["ffn_conv_b"] = _rows(
        "ffn_act_bwd", functools.partial(_f_ffn_act_bwd, nt=nt),
        [_In(gate_pre), _In(gate_pre, kind="prev"), _In(gate_pre, kind="next"), _In(up), _In(up, kind="next"),
         _In(dact), _In(dact, kind="next")], ffn_w,
        [(D_FF, _MXU), (D_FF, _MXU)], [(8, D_FF), (1, D_FF)])
    g["w_gate"], g["w_up"] = _mm_tn2("ffn_gate_up_dw", dgate_pre, dup, hn, out_dtype=_MXU)
    tok = grads_ready(g, ("w_down", "w_gate", "w_up"))
    dh2, g["ffn_norm_w"] = _mm_rows(
        "ffn_gate_up_dx_rms", [dgate_pre, dup], [w["w_gate"], w["w_up"]], "nn",
        lambda i, dy, x, dres, nw, _tok: _f_rms_bwd_add(i, x, dy, dres, nw, mask_pad=True),
        [_In(h2), _In(dh3)], [w["ffn_norm_w"], tok], [(D_MODEL, F32)], [(1, D_MODEL)])

    g["w_out"] = _mm("out_proj_dw", mixed, dh2, "tn", out_dtype=_MXU)
    half = MLA_HEADS * V_HEAD
    do_mla, do_dn, dz, g["mla_out_norm_w"], g["dn_out_norm_w"] = _mm_rows(
        "out_proj_dx_mix", dh2, w["w_out"], "nt",
        lambda i, dm, om, od, z, wm, wd: _f_mix_bwd(i, om, od, z, dm[:, :half], dm[:, half:], wm, wd),
        [_In(o_mla), _In(o_dn), p_z], out_w,
        [(half, F32), (DN_WIDTH, F32), (DN_WIDTH, _MXU)], [(1, V_HEAD), (1, DN_DIM)])

    dq, dk, dv = _attn_bwd(q, k, v, do_mla)
    dqf, dkvf, dkpe, dql, dkvl, g["q_norm_w"], g["k_norm_w"], g["q_a_norm_w"], g["kv_a_norm_w"] = _rows(
        "mla_back", _f_mla_back,
        [_In(qf), _In(kvf), p_kpe, cos, sin_s, _In(dq), _In(dk), _In(dv), p_ql, p_kvl], mla_w,
        [(wide, _MXU), (wide, _MXU), (LANE, _MXU), (Q_LORA, _MXU), (KV_LORA, _MXU)],
        [(1, HP), (1, HP), (1, Q_LORA), (1, KV_LORA)], tm=tm_mla)
    g["w_q_b"] = _mm("mla_q_b_dw", dqf, qn, "tn")
    g["w_kv_b"] = _mm("mla_kv_b_dw", kvn, dkvf, "tn")
    tok = grads_ready(g, ("w_out", "w_q_b", "w_kv_b"))

    dqkv, dab, g["dn_conv_w"], g["alog_b"], g["dtb_b"] = _gdn_bwd(
        gq, gk, gv, gg, gb, s_all, t_all, do_dn, proj, *dn_w, tok)

    dproj = jnp.concatenate([dqkv, dz, dql, dkvl, dkpe, dab], axis=1)
    g["w_in"] = _mm("in_proj_dw", dproj, u, "tn", out_dtype=_MXU)
    tok = grads_ready(g, ("w_in",))
    dh0, g["attn_norm_w"] = _mm_rows(
        "in_proj_dx_rms", dproj, w["w_in"], "nn",
        lambda i, du, x, dres, nw, _tok: _f_rms_bwd_add(i, x, du, dres, nw, mask_pad=False),
        [_In(h0), _In(dh2)], [w["attn_norm_w"], tok], [(D_MODEL, F32)], [(1, D_MODEL)])
    return loss, dh0, g


def _w_in_to_padded(w):
    c1, c2, c3 = Q_LORA, Q_LORA + KV_LORA, Q_LORA + KV_LORA + QK_ROPE
    c4 = c3 + 3 * DN_WIDTH
    c5 = c4 + DN_WIDTH
    z = lambda n: jnp.zeros((n, w.shape[1]), w.dtype)
    return jnp.concatenate([w[c3:c4], w[c4:c5], w[:c1], w[c1:c2], w[c2:c3], z(LANE - QK_ROPE),
                            w[c5:], z(LANE - 2 * DN_HEADS)], axis=0)


def _w_in_from_padded(g):
    return jnp.concatenate([g[C_QL:C_QL + Q_LORA], g[C_KVL:C_KVL + KV_LORA], g[C_KPE:C_KPE + QK_ROPE],
                            g[:C_Z + DN_WIDTH], g[C_AB:C_AB + 2 * DN_HEADS]], axis=0)


def _w_q_b_to_padded(w):
    r = w.shape[1]
    w = w.reshape(MLA_HEADS, QK_HEAD, r)
    return jnp.pad(w, ((0, 0), (0, HP - QK_HEAD), (0, 0))).reshape(MLA_HEADS * HP, r)


def _w_q_b_from_padded(g):
    r = g.shape[1]
    return g.reshape(MLA_HEADS, HP, r)[:, :QK_HEAD].reshape(MLA_HEADS * QK_HEAD, r)


def _pad_rows8(w):
    return jnp.pad(w, ((0, 8 - w.shape[0]), (0, 0)))


def _prepare(full, tp):
    w = {}
    mx = lambda a: a.astype(_MXU)
    w["attn_norm_w"] = full["attn_norm_w"]
    w["w_in"] = mx(_w_in_to_padded(full["w_in"]))
    w["q_a_norm_w"] = full["q_a_norm_w"]
    w["kv_a_norm_w"] = full["kv_a_norm_w"]
    w["w_q_b"] = mx(_w_q_b_to_padded(full["w_q_b"]))
    w["w_kv_b"] = mx(full["w_kv_b"])
    w["q_norm_w"] = jnp.pad(full["q_norm_w"], ((0, 0), (0, HP - QK_HEAD)))
    w["k_norm_w"] = jnp.pad(full["k_norm_w"], ((0, 0), (0, HP - QK_HEAD)))
    w["mla_out_norm_w"] = full["mla_out_norm_w"]
    w["dn_out_norm_w"] = full["dn_out_norm_w"]
    w["dn_conv_w"] = _pad_rows8(full["dn_conv_w"])
    w["alog_b"] = jnp.repeat(full["dn_A_log"], DN_DIM, axis=1)
    w["dtb_b"] = jnp.repeat(full["dn_dt_bias"], DN_DIM, axis=1)
    w["ffn_norm_w"] = full["ffn_norm_w"]
    w["ffn_conv_w"] = _pad_rows8(full["ffn_conv_w"])
    w["ffn_conv_b"] = full["ffn_conv_b"]
    for n in _LATE:
        if n in full:
            w[n] = mx(full[n])
    half = QK_ROPE // 2
    inv = ROPE_THETA ** (-jnp.arange(half, dtype=F32) / half)
    ang = (jnp.arange(tp, dtype=jnp.int32) - PAD).astype(F32)[:, None] * inv[None, :]
    zc = jnp.zeros((tp, LANE - QK_ROPE), F32)
    w["cos"] = jnp.concatenate([jnp.cos(ang), jnp.cos(ang), zc], axis=1)
    w["sin_s"] = jnp.concatenate([-jnp.sin(ang), jnp.sin(ang), zc], axis=1)
    return w


def _grads_to_natural(g):
    convert = {
        "w_in": ("w_in", _w_in_from_padded),
        "w_q_b": ("w_q_b", _w_q_b_from_padded),
        "q_norm_w": ("q_norm_w", lambda a: a[:, :QK_HEAD]),
        "k_norm_w": ("k_norm_w", lambda a: a[:, :QK_HEAD]),
        "dn_conv_w": ("dn_conv_w", lambda a: a[:DN_CONV]),
        "ffn_conv_w": ("ffn_conv_w", lambda a: a[:FFN_CONV]),
        "alog_b": ("dn_A_log", lambda a: a[:, ::DN_DIM]),
        "dtb_b": ("dn_dt_bias", lambda a: a[:, ::DN_DIM]),
    }
    n = {}
    for key, a in g.items():
        name, fn = convert.get(key, (key, lambda t: t))
        n[name] = fn(a)
    return n


_MESH = pl.DeviceIdType.MESH
_ANY = pl.BlockSpec(memory_space=pl.ANY)
_CHIP_FLIPS = ((1, 0), (0, 1), (1, 1))


def _me():
    return lax.axis_index("x"), lax.axis_index("y"), lax.axis_index("c")


def _all_gather(name, blk, after):
    after = list(after)

    def body(x_ref, *rest):
        out_ref, send_sems, recv_sems, local_sem = rest[len(after):]
        x, y, c = _me()
        me, sib = (x, y, c), (x, y, 1 - c)
        chips = [(x ^ fx, y ^ fy) for fx, fy in _CHIP_FLIPS]

        def slot(p):
            return out_ref.at[4 * p[0] + 2 * p[1] + p[2]]

        def copy(k, block, to, src=None):
            return pltpu.make_async_remote_copy(
                src_ref=slot(block) if src is None else src, dst_ref=slot(block),
                send_sem=send_sems.at[k], recv_sem=recv_sems.at[k], device_id=to, device_id_type=_MESH)

        mine = pltpu.make_async_copy(x_ref, slot(me), local_sem)
        mine.start()
        first = [copy(0, me, sib, src=x_ref)]
        first += [copy(1 + j, me, (*chip, c), src=x_ref) for j, chip in enumerate(chips)]
        for cp in first:
            cp.start()
        passed = [copy(4 + j, (*chip, c), sib) for j, chip in enumerate(chips)]
        for j, chip in enumerate(chips):
            copy(1 + j, (*chip, c), me).wait_recv()
            passed[j].start()
        copy(0, sib, me).wait_recv()
        for j, chip in enumerate(chips):
            copy(4 + j, (*chip, 1 - c), me).wait_recv()
        for cp in first + passed:
            cp.wait_send()
        mine.wait()

    return pl.pallas_call(
        body, name=name, in_specs=[_ANY] * (1 + len(after)), out_specs=_ANY,
        out_shape=jax.ShapeDtypeStruct((N_DEV,) + blk.shape, blk.dtype),
        scratch_shapes=[pltpu.SemaphoreType.DMA((7,)), pltpu.SemaphoreType.DMA((7,)), pltpu.SemaphoreType.DMA],
    )(blk, *after)


def _row_tile(r):
    divs = [d for d in range(16, min(r, 512) + 1, 16) if r % d == 0]
    return divs[-1] if divs else r


def _adam_math(g, w, m, v):
    m_new = ADAM_B1 * m + (1.0 - ADAM_B1) * g
    v_new = ADAM_B2 * v + (1.0 - ADAM_B2) * (g * g)
    m_hat = m_new / (1.0 - ADAM_B1 ** ADAM_STEP)
    v_hat = v_new / (1.0 - ADAM_B2 ** ADAM_STEP)
    return -ADAM_LR * (m_hat / (jnp.sqrt(v_hat) + ADAM_EPS) + ADAM_WD * w), m_new, v_new


def _adam_vectors(name, row, items, ws, ms, vs):
    k = len(items)

    def body(row_ref, *refs):
        w_refs, m_refs, v_refs = refs[:k], refs[k:2 * k], refs[2 * k:3 * k]
        outs = refs[3 * k:]
        for idx, (off, n, per_head) in enumerate(items):
            if per_head:
                spread = row_ref[:, off:off + DN_WIDTH]
                lane = lax.broadcasted_iota(jnp.int32, (1, LANE), 1)
                g = jnp.zeros((1, LANE), F32)
                for h in range(DN_HEADS):
                    g = g + jnp.where(lane == h, spread[:, DN_DIM * h:DN_DIM * h + 1], 0.0)
                g = g[:, :n]
            else:
                g = row_ref[:, off:off + n]
            d, m_new, v_new = _adam_math(g, w_refs[idx][...], m_refs[idx][...], v_refs[idx][...])
            for kind, val in enumerate((g, d, m_new, v_new)):
                outs[kind * k + idx][...] = val

    shapes = [jax.ShapeDtypeStruct((1, n), F32) for _, n, _ in items]
    res = pl.pallas_call(body, name=name, out_shape=shapes * 4)(row, *ws, *ms, *vs)
    return [list(res[kind * k:(kind + 1) * k]) for kind in range(4)]


def _adam_arrays(name, gs, ws, ms, vs):
    k = len(gs)

    def body(*refs):
        outs = refs[4 * k:]
        for idx in range(k):
            res = _adam_math(refs[idx][...], refs[k + idx][...], refs[2 * k + idx][...], refs[3 * k + idx][...])
            for kind, val in enumerate(res):
                outs[kind * k + idx][...] = val

    shapes = [jax.ShapeDtypeStruct(w.shape, F32) for w in ws]
    res = pl.pallas_call(body, name=name, out_shape=shapes * 3)(*gs, *ws, *ms, *vs)
    return [list(res[kind * k:(kind + 1) * k]) for kind in range(3)]


def _sum_parts(name, parts):
    _, r, cols = parts[0][0].shape
    tm = _row_tile(r)
    idx = jnp.stack([jnp.asarray(s, jnp.int32) for _, s in parts])
    n = len(parts)

    def body(idx_ref, *refs):
        g = refs[0][0].astype(F32)
        for p_ref in refs[1:n]:
            g = g + p_ref[0].astype(F32)
        refs[n][...] = g

    return pl.pallas_call(
        body, name=name,
        grid_spec=pltpu.PrefetchScalarGridSpec(
            num_scalar_prefetch=1, grid=(r // tm,),
            in_specs=[pl.BlockSpec((1, tm, cols), lambda i, idx_ref, p=p: (idx_ref[p], i, 0)) for p in range(n)],
            out_specs=pl.BlockSpec((tm, cols), lambda i, idx_ref: (i, 0))),
        out_shape=jax.ShapeDtypeStruct((r, cols), F32),
        compiler_params=pltpu.CompilerParams(dimension_semantics=("parallel",)),
    )(idx, *[a for a, _ in parts])


def _adam(name, parts, w, m, v):
    r, cols = w.shape
    tm = _row_tile(r)
    tc = cols // 4 if (r // tm < 4 and cols % (4 * LANE) == 0) else cols
    idx = jnp.stack([jnp.asarray(s, jnp.int32) for _, s in parts])
    n = len(parts)

    def body(idx_ref, *refs):
        g = refs[0][0].astype(F32)
        for p_ref in refs[1:n]:
            g = g + p_ref[0].astype(F32)
        w_ref, m_ref, v_ref, g_out, d_out, m_out, v_out = refs[n:]
        g_out[...] = g
        d_out[...], m_out[...], v_out[...] = _adam_math(g, w_ref[...], m_ref[...], v_ref[...])

    part_specs = [pl.BlockSpec((1, tm, tc), lambda i, j, idx_ref, p=p: (idx_ref[p], i, j)) for p in range(n)]
    flat = pl.BlockSpec((tm, tc), lambda i, j, idx_ref: (i, j))
    return pl.pallas_call(
        body, name=name,
        grid_spec=pltpu.PrefetchScalarGridSpec(
            num_scalar_prefetch=1, grid=(r // tm, cols // tc), in_specs=part_specs + [flat] * 3,
            out_specs=[flat] * 4),
        out_shape=[jax.ShapeDtypeStruct((r, cols), F32)] * 4,
        compiler_params=pltpu.CompilerParams(dimension_semantics=("parallel", "parallel")),
    )(idx, *[a for a, _ in parts], w, m, v)


def _all_gather_many(name, blks):
    n = len(blks)

    def body(*refs):
        x_refs, out_refs = refs[:n], refs[n:2 * n]
        send_sems, recv_sems, local_sems = refs[2 * n:]
        x, y, c = _me()
        me, sib = (x, y, c), (x, y, 1 - c)
        chips = [(x ^ fx, y ^ fy) for fx, fy in _CHIP_FLIPS]

        def slot(a, p):
            return out_refs[a].at[4 * p[0] + 2 * p[1] + p[2]]

        def copy(a, k, block, to, src=None):
            return pltpu.make_async_remote_copy(
                src_ref=slot(a, block) if src is None else src, dst_ref=slot(a, block),
                send_sem=send_sems.at[7 * a + k], recv_sem=recv_sems.at[7 * a + k], device_id=to,
                device_id_type=_MESH)

        mine = [pltpu.make_async_copy(x_refs[a], slot(a, me), local_sems.at[a]) for a in range(n)]
        first = []
        for a in range(n):
            mine[a].start()
            first.append(copy(a, 0, me, sib, src=x_refs[a]))
            first += [copy(a, 1 + j, me, (*chip, c), src=x_refs[a]) for j, chip in enumerate(chips)]
        for cp in first:
            cp.start()
        passed = []
        for j, chip in enumerate(chips):
            for a in range(n):
                copy(a, 1 + j, (*chip, c), me).wait_recv()
                cp = copy(a, 4 + j, (*chip, c), sib)
                cp.start()
                passed.append(cp)
        for a in range(n):
            copy(a, 0, sib, me).wait_recv()
            for j, chip in enumerate(chips):
                copy(a, 4 + j, (*chip, 1 - c), me).wait_recv()
        for cp in first + passed:
            cp.wait_send()
        for cp in mine:
            cp.wait()

    return pl.pallas_call(
        body, name=name, in_specs=[_ANY] * n, out_specs=[_ANY] * n,
        out_shape=[jax.ShapeDtypeStruct((N_DEV,) + b.shape, b.dtype) for b in blks],
        scratch_shapes=[pltpu.SemaphoreType.DMA((7 * n,)), pltpu.SemaphoreType.DMA((7 * n,)),
                        pltpu.SemaphoreType.DMA((n,))],
    )(*blks)


_HBM = pl.BlockSpec(memory_space=pltpu.HBM)
_SEM = pl.BlockSpec(memory_space=pltpu.SEMAPHORE)
_EFFECT = pltpu.SideEffectType.DATAFLOW_SIDE_EFFECTING


def _push_copies(src_refs, land_refs, send_sems, recv_sems, src_by_peer, first=0):
    x, y, c = _me()
    my_id = 4 * x + 2 * y + c
    out = []
    for k in range(len(src_refs)):
        a = first + k
        for f in range(1, N_DEV):
            px, py, pc = x ^ (f >> 2), y ^ ((f >> 1) & 1), c ^ (f & 1)
            pid = 4 * px + 2 * py + pc
            src = src_refs[k].at[pid] if src_by_peer else src_refs[k]
            start = pltpu.make_async_remote_copy(
                src_ref=src, dst_ref=land_refs[k].at[my_id], send_sem=send_sems.at[7 * a + f - 1],
                recv_sem=recv_sems.at[7 * a + f - 1], device_id=(px, py, pc), device_id_type=_MESH)
            landed = pltpu.make_async_remote_copy(
                src_ref=src, dst_ref=land_refs[k].at[pid], send_sem=send_sems.at[7 * a + f - 1],
                recv_sem=recv_sems.at[7 * a + f - 1], device_id=(px, py, pc), device_id_type=_MESH)
            out.append((start, landed))
    return out


def _push_start(name, srcs, src_by_peer, after):
    n = len(srcs)
    lands = [jax.ShapeDtypeStruct((N_DEV,) + (s.shape[1:] if src_by_peer else s.shape), s.dtype) for s in srcs]

    def body(*refs):
        src_refs, land_refs = refs[:n], refs[n:2 * n]
        send_sems, recv_sems = refs[2 * n + 1], refs[2 * n + 2]
        token = refs[-1]
        for start, _ in _push_copies(src_refs, land_refs, send_sems, recv_sems, src_by_peer):
            start.start()
        token[...] = jnp.zeros_like(token)

    hbm = lambda a: pltpu.with_memory_space_constraint(a, pltpu.HBM)
    res = pl.pallas_call(
        body, name=name,
        out_shape=(pltpu.SemaphoreType.DMA((7 * n,)), pltpu.SemaphoreType.DMA((7 * n,)),
                   *[pltpu.HBM(s.shape, s.dtype) for s in srcs], *[pltpu.HBM(s.shape, s.dtype) for s in lands],
                   jax.ShapeDtypeStruct((8, LANE), F32)),
        in_specs=[_HBM] * (2 * n) + [_ANY],
        out_specs=(_SEM, _SEM, *[_HBM] * (2 * n), pl.BlockSpec(memory_space=pltpu.VMEM)),
        input_output_aliases={i: 2 + i for i in range(2 * n)},
        compiler_params=pltpu.CompilerParams(has_side_effects=_EFFECT),
    )(*[hbm(s) for s in srcs], *[hbm(lax.empty(s.shape, s.dtype)) for s in lands], after)
    return res[0], res[1], list(res[2:2 + n]), list(res[2 + n:2 + 2 * n]), res[-1]


def _push_wait(name, send_sems, recv_sems, srcs, lands, src_by_peer, after, first=0):
    n = len(srcs)
    after = list(after) if isinstance(after, (list, tuple)) else [after]

    def body(*refs):
        src_refs, land_refs = refs[:n], refs[n:2 * n]
        s_sems, r_sems = refs[2 * n], refs[2 * n + 1]
        for _, landed in _push_copies(src_refs, land_refs, s_sems, r_sems, src_by_peer, first):
            landed.wait_send()
            landed.wait_recv()

    res = pl.pallas_call(
        body, name=name,
        out_shape=tuple(pltpu.HBM(s.shape, s.dtype) for s in list(srcs) + list(lands)),
        in_specs=[_HBM] * (2 * n) + [_SEM, _SEM] + [_ANY] * len(after),
        out_specs=tuple([_HBM] * (2 * n)),
        input_output_aliases={i: i for i in range(2 * n)},
        compiler_params=pltpu.CompilerParams(has_side_effects=_EFFECT),
    )(*srcs, *lands, send_sems, recv_sems, *after)
    return list(res[:n]), list(res[n:])


_SHARDED = (
    ("meta_tokens", 1, (N_META, D_MODEL)),
    ("w_in", 1, (D_MODEL, IN_COLS)),
    ("w_q_b", 1, (Q_LORA, MLA_HEADS * QK_HEAD)),
    ("w_kv_b", 1, (KV_LORA, MLA_HEADS * (QK_NOPE + V_HEAD))),
    ("dn_conv_w", 1, (DN_CONV, 3 * DN_WIDTH)),
    ("w_out", 0, (2 * DN_WIDTH, D_MODEL)),
    ("w_gate", 1, (D_MODEL, D_FF)),
    ("w_up", 1, (D_MODEL, D_FF)),
    ("ffn_conv_w", 1, (FFN_CONV, D_FF)),
    ("w_down", 0, (D_FF, D_MODEL)),
)
_F32_GATHERED = ("meta_tokens", "dn_conv_w", "ffn_conv_w")
_EARLY = ("w_in", "w_q_b", "w_kv_b")
_LATE = ("w_out", "w_gate", "w_up", "w_down")
_TRANSPOSED = ("w_in", "w_q_b", "w_gate", "w_up")
_REPLICATED = (
    ("attn_norm_w", D_MODEL), ("q_a_norm_w", Q_LORA), ("kv_a_norm_w", KV_LORA), ("q_norm_w", QK_HEAD),
    ("k_norm_w", QK_HEAD), ("mla_out_norm_w", V_HEAD), ("dn_A_log", DN_HEADS), ("dn_dt_bias", DN_HEADS),
    ("dn_out_norm_w", DN_DIM), ("ffn_norm_w", D_MODEL), ("ffn_conv_b", D_FF),
)
_SMALL_BLOCK = (8, 512)


def _local_shape(dim, shape):
    return (shape[0] // N_DEV, shape[1]) if dim == 0 else (shape[0], shape[1] // N_DEV)


def _from_blocks(blocks, dim, shape):
    r, c = shape
    if dim == 0:
        return blocks.reshape(r, c)
    return blocks.reshape(N_DEV, r, c // N_DEV).transpose(1, 0, 2).reshape(r, c)


def _split(flat, sizes):
    out, o = [], 0
    for s in sizes:
        out.append(flat[..., o:o + s])
        o += s
    return out


def kernel(x, meta_tokens, attn_norm_w, w_in, q_a_norm_w, w_q_b, kv_a_norm_w, w_kv_b, q_norm_w, k_norm_w, mla_out_norm_w, dn_conv_w, dn_A_log, dn_dt_bias, dn_out_norm_w, w_out, ffn_norm_w, w_gate, w_up, ffn_conv_w, ffn_conv_b, w_down, loss_target, m_meta_tokens, m_attn_norm_w, m_w_in, m_q_a_norm_w, m_w_q_b, m_kv_a_norm_w, m_w_kv_b, m_q_norm_w, m_k_norm_w, m_mla_out_norm_w, m_dn_conv_w, m_dn_A_log, m_dn_dt_bias, m_dn_out_norm_w, m_w_out, m_ffn_norm_w, m_w_gate, m_w_up, m_ffn_conv_w, m_ffn_conv_b, m_w_down, v_meta_tokens, v_attn_norm_w, v_w_in, v_q_a_norm_w, v_w_q_b, v_kv_a_norm_w, v_w_kv_b, v_q_norm_w, v_k_norm_w, v_mla_out_norm_w, v_dn_conv_w, v_dn_A_log, v_dn_dt_bias, v_dn_out_norm_w, v_w_out, v_ffn_norm_w, v_w_gate, v_w_up, v_ffn_conv_w, v_ffn_conv_b, v_w_down):
    names = [n for n, _, _ in _SHARDED] + [n for n, _ in _REPLICATED]
    given = dict(locals())
    two_d = lambda a: a.reshape(a.shape[-2:])
    view = lambda a, n: two_d(a).T if n in _TRANSPOSED else two_d(a)
    wl = {n: view(given[n], n) for n in names}
    ml = {n: view(given["m_" + n], n) for n in names}
    vl = {n: view(given["v_" + n], n) for n in names}
    out_shapes = {n: given[n].shape for n in names}

    spec = {n: (d, s) for n, d, s in _SHARDED}
    small_sizes = [math.prod(_local_shape(*spec[n])) for n in _F32_GATHERED]

    def small_block(d):
        cat = jnp.concatenate([d[n].reshape(d[n].shape[:-2] + (-1,)) for n in _F32_GATHERED], axis=-1)
        pad = [(0, 0)] * (cat.ndim - 1) + [(0, math.prod(_SMALL_BLOCK) - cat.shape[-1])]
        return jnp.pad(cat, pad).reshape(cat.shape[:-1] + _SMALL_BLOCK)

    def shard(n):
        return wl[n].astype(_MXU)

    def from_slots(n, blocks):
        d, s = spec[n]
        if d == 0 or n in _TRANSPOSED:
            return blocks.reshape(-1, blocks.shape[-1])
        return blocks.transpose(1, 0, 2).reshape(s)

    my_id = 4 * lax.axis_index("x") + 2 * lax.axis_index("y") + lax.axis_index("c")
    got = _all_gather_many("gather_early", [shard(n) for n in _EARLY] + [small_block(wl)])
    full = {n: a for n, a in wl.items() if n not in _LATE}
    for n, blocks in zip(_EARLY, got):
        full[n] = from_slots(n, blocks)
    for n, p in zip(_F32_GATHERED, _split(got[-1].reshape(N_DEV, -1), small_sizes)):
        full[n] = _from_blocks(p, *spec[n])
    late_own = [shard(n) for n in _LATE]
    l_send, l_recv, l_src, l_land, token = _push_start("gather_late_start", late_own, False, got[-1])

    def late_weights(after, names):
        first = _LATE.index(names[0])
        sl = slice(first, first + len(names))
        _, lands = _push_wait("gather_late_wait_" + names[0], l_send, l_recv, l_src[sl], l_land[sl], False,
                              after, first)
        out = {}
        for n, land, own in zip(names, lands, late_own[sl]):
            out[n] = from_slots(n, lax.dynamic_update_slice(land, own[None], (my_id, 0, 0))).astype(_MXU)
        return out

    def dest_blocks(n, a):
        d, s = spec[n]
        r, c = _local_shape(d, s)
        if n in _TRANSPOSED:
            return a.reshape(N_DEV, c, r)
        return a.reshape(N_DEV, r, c) if d == 0 else a.reshape(r, N_DEV, c).transpose(1, 0, 2)

    pushed = []

    def grads_ready(g, names):
        nat = _grads_to_natural({n: g[n] for n in names})
        blocks = [dest_blocks(n, nat[n]).astype(_MXU) for n in names]
        sends, recvs, srcs, lands, tok = _push_start("rs_" + names[0] + "_start", blocks, True, token)
        pushed.append((names, sends, recvs, srcs, lands))
        return tok

    seq = x.shape[1]
    tp = ROW0 + seq
    h0 = jnp.concatenate([jnp.zeros((PAD, D_MODEL), F32), full["meta_tokens"], x[0]], axis=0)
    tgt = jnp.concatenate([jnp.zeros((ROW0, D_MODEL), F32), loss_target[0]], axis=0)
    loss, dh0, raw = _local_step(h0, tgt, _prepare(full, tp), token, late_weights, grads_ready)
    g = _grads_to_natural(raw)
    g["meta_tokens"] = dh0[PAD:ROW0]
    grad_x = dh0[ROW0:][None]

    big = [{}, {}, {}, {}]

    def finish(group):
        names, sends, recvs, srcs, lands = group
        srcs, lands = _push_wait("rs_" + names[0] + "_wait", sends, recvs, srcs, lands, True, dh0)
        for n, src, land in zip(names, srcs, lands):
            parts = [(src, my_id)] + [(land, my_id ^ f) for f in range(1, N_DEV)]
            for kind, a in enumerate(_adam("adam_" + n, parts, wl[n], ml[n], vl[n])):
                big[kind][n] = a

    for group in pushed[:-1]:
        finish(group)
    rep_names = [n for n, _ in _REPLICATED]
    raw_key = {"dn_A_log": "alog_b", "dn_dt_bias": "dtb_b"}
    pieces = [raw[raw_key.get(n, n)] for n in rep_names] + [loss]
    pieces += [g[n].reshape(1, -1) for n in _F32_GATHERED]
    widths = [p.shape[1] for p in pieces]
    offs = [sum(widths[:k]) for k in range(len(widths))]
    cat = jnp.concatenate(pieces, axis=1)
    cols = -(-cat.shape[1] // (8 * LANE)) * LANE
    mine = jnp.pad(cat, ((0, 0), (0, 8 * cols - cat.shape[1]))).reshape(8, cols)
    everyone = _all_gather("gather_small_grads", mine, [big[1][n] for group in pushed[:-1] for n in group[0]])
    total = _sum_parts("sum_small_grads", [(everyone, d) for d in range(N_DEV)]).reshape(1, 8 * cols)
    tot = {n: total[0, o:o + wd] for n, o, wd in zip(rep_names + ["loss"] + list(_F32_GATHERED), offs, widths)}
    items = [(o, size, n in raw_key) for (n, size), o in zip(_REPLICATED, offs)]
    sm = _adam_vectors("adam_replicated", total, items, [wl[n] for n in rep_names], [ml[n] for n in rep_names],
                       [vl[n] for n in rep_names])
    sm = [dict(zip(rep_names, kind)) for kind in sm]
    mine_of = {}
    for n in _F32_GATHERED:
        d, s = spec[n]
        r, c = _local_shape(d, s)
        mine_of[n] = lax.dynamic_slice(tot[n].reshape(s), (0, my_id * c), (r, c))
    res = _adam_arrays("adam_small_sharded", [mine_of[n] for n in _F32_GATHERED], [wl[n] for n in _F32_GATHERED],
                       [ml[n] for n in _F32_GATHERED], [vl[n] for n in _F32_GATHERED])
    for kind, arrays in enumerate([[mine_of[n] for n in _F32_GATHERED]] + res):
        big[kind].update(zip(_F32_GATHERED, arrays))

    finish(pushed[-1])

    outs = [tot["loss"][0], grad_x]
    for kind in range(4):
        for n in ("meta_tokens", "attn_norm_w", "w_in", "q_a_norm_w", "w_q_b", "kv_a_norm_w", "w_kv_b", "q_norm_w",
                  "k_norm_w", "mla_out_norm_w", "dn_conv_w", "dn_A_log", "dn_dt_bias", "dn_out_norm_w", "w_out",
                  "ffn_norm_w", "w_gate", "w_up", "ffn_conv_w", "ffn_conv_b", "w_down"):
            src = big[kind] if n in big[kind] else sm[kind]
            a = src[n].T if n in _TRANSPOSED else src[n]
            outs.append(a.reshape(out_shapes[n]))
    return tuple(outs)
```

```python
import functools
import math

import jax
import jax.numpy as jnp
from jax import lax
from jax.experimental import pallas as pl
from jax.experimental.pallas import tpu as pltpu

F32 = jnp.float32
_MXU = jnp.bfloat16
_HI = lax.Precision.HIGHEST

D_MODEL = 1024
N_META = 16
PAD = 112
ROW0 = PAD + N_META
MLA_HEADS = 4
QK_NOPE = 128
QK_ROPE = 64
QK_HEAD = QK_NOPE + QK_ROPE
V_HEAD = 128
Q_LORA = 256
KV_LORA = 256
ROPE_THETA = 10000.0
DN_HEADS = 4
DN_DIM = 128
DN_WIDTH = DN_HEADS * DN_DIM
DN_CONV = 4
DN_CHUNK = 64
GDN_SUB_CHUNKS = 2
D_FF = 2816
FFN_CONV = 3
EPS = 1e-6
HP = 256
C_Z = 1536
C_QL = 2048
C_KVL = 2304
C_KPE = 2560
C_AB = 2688
IN_COLS = 2632

ADAM_LR = 0.001
ADAM_B1 = 0.9
ADAM_B2 = 0.999
ADAM_EPS = 1e-08
ADAM_WD = 0.01
ADAM_STEP = 10

N_DEV = 8
TM = 128
LANE = 128
VMEM_LIMIT = 56 * 1024 * 1024
NEG = -1e30


def _dot(a, b, dims, hp=False):
    if hp:
        return lax.dot_general(a.astype(F32), b.astype(F32), (dims, ((), ())),
                               precision=lax.Precision.HIGH if hp == "3x" else _HI, preferred_element_type=F32)
    return lax.dot_general(a.astype(_MXU), b.astype(_MXU), (dims, ((), ())),
                           preferred_element_type=F32)


def _nn(a, b, hp=False):
    return _dot(a, b, ((1,), (0,)), hp)


def _nt(a, b, hp=False):
    return _dot(a, b, ((1,), (1,)), hp)


def _tn(a, b, hp=False):
    return _dot(a, b, ((0,), (0,)), hp)


def _sigmoid(x):
    return 1.0 / (1.0 + jnp.exp(-x))


def _rms_fwd(x, w, n):
    r = lax.rsqrt(jnp.sum(x * x, axis=-1, keepdims=True) * (1.0 / n) + EPS)
    return x * r * w, r


def _rms_bwd(x, w, dy, n):
    r = lax.rsqrt(jnp.sum(x * x, axis=-1, keepdims=True) * (1.0 / n) + EPS)
    xh = x * r
    gy = dy * w
    dx = r * (gy - xh * (jnp.sum(gy * xh, axis=-1, keepdims=True) * (1.0 / n)))
    return dx, dy * xh


def _rowsum(x):
    return jnp.sum(x, axis=0, keepdims=True)


def _row_ids(i, tm):
    return i * tm + lax.broadcasted_iota(jnp.int32, (tm, 1), 0)


def _shift_down(ext, s, tm):
    if s == 0:
        return ext[8:8 + tm]
    return pltpu.roll(ext, s, 0)[8:8 + tm]


def _shift_up(ext, s, tm):
    if s == 0:
        return ext[0:tm]
    return pltpu.roll(ext, tm + 8 - s, 0)[0:tm]


def _conv_taps(x, halo_prev, width):
    tm = x.shape[0]
    ext = jnp.concatenate([halo_prev, x], axis=0)
    return [_shift_down(ext, width - 1 - j, tm) for j in range(width)]


def _conv_from_taps(taps, w):
    y = None
    for j, tap in enumerate(taps):
        t = w[j:j + 1, :] * tap
        y = t if y is None else y + t
    return y


def _conv_fwd(x, halo_prev, w, width):
    return _conv_from_taps(_conv_taps(x, halo_prev, width), w)


def _conv_bwd_w_taps(dy, taps):
    tm = dy.shape[0]
    rows = [_rowsum(dy * tap[:tm]) for tap in taps]
    rows += [jnp.zeros_like(rows[0])] * (8 - len(taps))
    return jnp.concatenate(rows, axis=0)


def _conv_bwd_x(dy, halo_next, w, width):
    tm = dy.shape[0]
    ext = jnp.concatenate([dy, halo_next], axis=0)
    dx = None
    for j in range(width):
        t = w[j:j + 1, :] * _shift_up(ext, width - 1 - j, tm)
        dx = t if dx is None else dx + t
    return dx


def _softplus(x):
    e = jnp.exp(-jnp.abs(x))
    u = 1.0 + e
    l1p = jnp.where(u == 1.0, e, jnp.log(u) * e / jnp.where(u == 1.0, 1.0, u - 1.0))
    return jnp.maximum(x, 0.0) + l1p


def _swap_halves(x):
    lane = lax.broadcasted_iota(jnp.int32, x.shape, 1)
    return jnp.where(lane < 32, pltpu.roll(x, 96, 1), jnp.where(lane < 64, pltpu.roll(x, 32, 1), 0.0))


class _In:
    def __init__(self, arr, width=None, cb=0, kind="cur"):
        self.arr, self.kind = arr, kind
        self.width = arr.shape[1] if width is None else width
        self.cb = cb


def _whole_spec(x):
    return pl.BlockSpec(x.shape, lambda i, nd=x.ndim: (0,) * nd, pipeline_mode=pl.Buffered(1))


def _tile_spec(t, tm, tp):
    r8 = tm // 8
    if t.kind == "cur":
        return pl.BlockSpec((tm, t.width), lambda i, cb=t.cb: (i, cb))
    if t.kind == "prev":
        return pl.BlockSpec((8, t.width), lambda i, cb=t.cb: (jnp.maximum(i * r8 - 1, 0), cb))
    return pl.BlockSpec((8, t.width), lambda i, cb=t.cb: (jnp.minimum((i + 1) * r8, tp // 8 - 1), cb))


def _rows(name, fn, tiled, full, outs, accs=(), tm=TM):
    tp = tiled[0].arr.shape[0]
    nt = tp // tm
    n_in = len(tiled) + len(full)
    n_out = len(outs)

    def body(*refs):
        i = pl.program_id(0)
        vals = [r[...] for r in refs[:n_in]]
        o_t, o_a = fn(i, *vals)
        for r, v in zip(refs[n_in:n_in + n_out], o_t):
            r[...] = v.astype(r.dtype)
        for r, v in zip(refs[n_in + n_out:], o_a):
            @pl.when(i == 0)
            def _():
                r[...] = v

            @pl.when(i > 0)
            def _():
                r[...] += v

    in_specs = [_tile_spec(t, tm, tp) for t in tiled]
    in_specs += [pl.BlockSpec(a.shape, lambda i, nd=a.ndim: (0,) * nd) for a in full]
    out_specs = [pl.BlockSpec((tm, w), lambda i: (i, 0)) for w, _ in outs]
    out_specs += [pl.BlockSpec((r, w), lambda i: (0, 0)) for r, w in accs]
    out_shape = [jax.ShapeDtypeStruct((tp, w), dt) for w, dt in outs]
    out_shape += [jax.ShapeDtypeStruct((r, w), F32) for r, w in accs]
    res = pl.pallas_call(
        body, name=name, grid=(nt,), in_specs=in_specs, out_specs=out_specs, out_shape=out_shape,
        compiler_params=pltpu.CompilerParams(dimension_semantics=("arbitrary",), vmem_limit_bytes=VMEM_LIMIT),
    )(*[t.arr for t in tiled], *full)
    return res


def _pick(n, cap, mult):
    best = None
    for d in range(mult, min(n, cap) + 1, mult):
        if n % d == 0:
            best = d
    assert best is not None, (n, cap, mult)
    return best


_ANY_SPEC = pl.BlockSpec(memory_space=pl.ANY)


def _mm(name, a, b, mode, out_dtype=F32, resid=None, after=None):
    if mode == "tn":
        m, k = a.shape
        n = b.shape[1]
        tk = _pick(k, 512, 128)
        tn = _pick(n, 1408, 128)

        def body_tn(a_ref, b_ref, o_ref):
            o_ref[...] = _tn(a_ref[...], b_ref[...]).astype(o_ref.dtype)

        return pl.pallas_call(
            body_tn, name=name, grid=(n // tn, k // tk),
            in_specs=[pl.BlockSpec((m, tk), lambda j, p: (0, p)),
                      pl.BlockSpec((m, tn), lambda j, p: (0, j))],
            out_specs=pl.BlockSpec((tk, tn), lambda j, p: (p, j)),
            out_shape=jax.ShapeDtypeStruct((k, n), out_dtype),
            compiler_params=pltpu.CompilerParams(
                dimension_semantics=("parallel", "parallel"), vmem_limit_bytes=VMEM_LIMIT),
        )(a, b)

    m, k = a.shape
    n = b.shape[1] if mode == "nn" else b.shape[0]
    tn = _pick(n, 1408, 128)
    tm = _pick(m, 1152, 16)
    dotf = _nn if mode == "nn" else _nt

    def body(*refs):
        a_ref, b_ref, o_ref = refs[0], refs[1], refs[-1]
        acc = dotf(a_ref[...], b_ref[...])
        if resid is not None:
            acc = refs[2][...] + acc
        o_ref[...] = acc.astype(o_ref.dtype)

    b_spec = (pl.BlockSpec((k, tn), lambda j, i: (0, j)) if mode == "nn"
              else pl.BlockSpec((tn, k), lambda j, i: (j, 0)))
    in_specs = [pl.BlockSpec((tm, k), lambda j, i: (i, 0)), b_spec]
    args = [a, b]
    if resid is not None:
        in_specs.append(pl.BlockSpec((tm, tn), lambda j, i: (i, j)))
        args.append(resid)
    if after is not None:
        in_specs.append(_ANY_SPEC)
        args.append(after)
    return pl.pallas_call(
        body, name=name, grid=(n // tn, m // tm), in_specs=in_specs,
        out_specs=pl.BlockSpec((tm, tn), lambda j, i: (i, j)),
        out_shape=jax.ShapeDtypeStruct((m, n), out_dtype),
        compiler_params=pltpu.CompilerParams(
            dimension_semantics=("parallel", "parallel"), vmem_limit_bytes=VMEM_LIMIT),
    )(*args)


def _mm_tn2(name, a1, a2, b, out_dtype=F32):
    m, k = a1.shape
    n = b.shape[1]
    tk = _pick(k, 512, 128)

    def body(a1_ref, a2_ref, b_ref, o1_ref, o2_ref):
        bb = b_ref[...]
        o1_ref[...] = _tn(a1_ref[...], bb).astype(o1_ref.dtype)
        o2_ref[...] = _tn(a2_ref[...], bb).astype(o2_ref.dtype)

    a_spec = pl.BlockSpec((m, tk), lambda p: (0, p))
    o_spec = pl.BlockSpec((tk, n), lambda p: (p, 0))
    return pl.pallas_call(
        body, name=name, grid=(k // tk,),
        in_specs=[a_spec, a_spec, pl.BlockSpec((m, n), lambda p: (0, 0))],
        out_specs=[o_spec, o_spec], out_shape=[jax.ShapeDtypeStruct((k, n), out_dtype)] * 2,
        compiler_params=pltpu.CompilerParams(dimension_semantics=("parallel",), vmem_limit_bytes=VMEM_LIMIT),
    )(a1, a2, b)


def _mm_tn_pair(name, a1, b1, a2, b2):
    def body(a1_ref, b1_ref, a2_ref, b2_ref, o1_ref, o2_ref):
        o1_ref[...] = _tn(a1_ref[...], b1_ref[...])
        o2_ref[...] = _tn(a2_ref[...], b2_ref[...])

    return pl.pallas_call(
        body, name=name,
        out_shape=[jax.ShapeDtypeStruct((a1.shape[1], b1.shape[1]), F32),
                   jax.ShapeDtypeStruct((a2.shape[1], b2.shape[1]), F32)],
        compiler_params=pltpu.CompilerParams(vmem_limit_bytes=VMEM_LIMIT),
    )(a1, b1, a2, b2)


def _norm_mm(name, x, norm_w, b, mode="nt", x_cb=0, after=None):
    m = x.shape[0]
    k = norm_w.shape[1]
    n = b.shape[0] if mode == "nt" else b.shape[1]
    tn = _pick(n, 1408, 128)
    tm = _pick(m, 1152, 16)
    dotf = _nt if mode == "nt" else _nn
    extra = [] if after is None else [after]

    def body(x_ref, w_ref, b_ref, *rest):
        o_ref, u_ref = rest[-2:]

        @pl.when(pl.program_id(1) == 0)
        def _():
            u_ref[...] = _rms_fwd(x_ref[...], w_ref[...], k)[0].astype(u_ref.dtype)

        o_ref[...] = dotf(u_ref[...], b_ref[...])

    b_spec = (pl.BlockSpec((tn, k), lambda i, j: (j, 0)) if mode == "nt"
              else pl.BlockSpec((k, tn), lambda i, j: (0, j)))
    return pl.pallas_call(
        body, name=name, grid=(m // tm, n // tn),
        in_specs=[pl.BlockSpec((tm, k), lambda i, j: (i, x_cb)), pl.BlockSpec((1, k), lambda i, j: (0, 0)),
                  b_spec] + [_ANY_SPEC] * len(extra),
        out_specs=[pl.BlockSpec((tm, tn), lambda i, j: (i, j)), pl.BlockSpec((tm, k), lambda i, j: (i, 0))],
        out_shape=[jax.ShapeDtypeStruct((m, n), F32), jax.ShapeDtypeStruct((m, k), _MXU)],
        compiler_params=pltpu.CompilerParams(
            dimension_semantics=("arbitrary", "arbitrary"), vmem_limit_bytes=VMEM_LIMIT),
    )(x, norm_w, b, *extra)


def _pro_mm(name, fn, tiled, full, k, b, resid):
    m = resid.shape[0]
    n = b.shape[1]
    tm = _pick(m, 576, 16)
    n_in = len(tiled) + len(full)

    def body(*refs):
        i = pl.program_id(0)
        u = fn(i, *[r[...] for r in refs[:n_in]]).astype(_MXU)
        b_ref, r_ref, o_ref, u_ref = refs[n_in:]
        u_ref[...] = u
        o_ref[...] = r_ref[...] + _nn(u, b_ref[...])

    row = lambda w: pl.BlockSpec((tm, w), lambda i: (i, 0))
    in_specs = [_tile_spec(t, tm, m) for t in tiled]
    in_specs += [_whole_spec(x) for x in full] + [_whole_spec(b), row(n)]
    return pl.pallas_call(
        body, name=name, grid=(m // tm,), in_specs=in_specs, out_specs=[row(n), row(k)],
        out_shape=[jax.ShapeDtypeStruct((m, n), F32), jax.ShapeDtypeStruct((m, k), _MXU)],
        compiler_params=pltpu.CompilerParams(dimension_semantics=("parallel",), vmem_limit_bytes=VMEM_LIMIT),
    )(*[t.arr for t in tiled], *full, b, resid)


def _ffn_in(h2, norm_w, w_gate_t, w_up_t, conv_w8, conv_b):
    m, k = h2.shape
    n = w_gate_t.shape[0]
    tm = _pick(m, 288, 16)

    def body(x_ref, xp_ref, nw_ref, wg_ref, wu_ref, cw_ref, cb_ref, hn_ref, gp_ref, up_ref, act_ref):
        i = pl.program_id(0)
        nw = nw_ref[...]
        hn = _rms_fwd(x_ref[...], nw, k)[0].astype(_MXU)
        hn_prev = _rms_fwd(xp_ref[...], nw, k)[0].astype(_MXU)
        wg = wg_ref[...]
        gp = _nt(hn, wg)
        gp_prev = jnp.where(i > 0, _nt(hn_prev, wg), 0.0)
        up = _nt(hn, wu_ref[...])
        gate = _conv_fwd(gp, gp_prev, cw_ref[...], FFN_CONV) + cb_ref[...]
        hn_ref[...] = hn
        gp_ref[...] = gp
        up_ref[...] = up
        act_ref[...] = (_silu_parts(gate)[0] * up).astype(act_ref.dtype)

    row = lambda w: pl.BlockSpec((tm, w), lambda i: (i, 0))
    r8 = tm // 8
    return pl.pallas_call(
        body, name="ffn_in", grid=(m // tm,),
        in_specs=[row(k), pl.BlockSpec((8, k), lambda i: (jnp.maximum(i * r8 - 1, 0), 0)), _whole_spec(norm_w),
                  _whole_spec(w_gate_t), _whole_spec(w_up_t), _whole_spec(conv_w8), _whole_spec(conv_b)],
        out_specs=[row(k), row(n), row(n), row(n)],
        out_shape=[jax.ShapeDtypeStruct((m, k), _MXU), jax.ShapeDtypeStruct((m, n), F32),
                   jax.ShapeDtypeStruct((m, n), F32), jax.ShapeDtypeStruct((m, n), _MXU)],
        compiler_params=pltpu.CompilerParams(dimension_semantics=("parallel",), vmem_limit_bytes=VMEM_LIMIT),
    )(h2, h2, norm_w, w_gate_t, w_up_t, conv_w8, conv_b)


def _mm_rows(name, a, b, mode, fn, tiled, full, outs, accs=(), tm_cap=576):
    a_list = list(a) if isinstance(a, (list, tuple)) else [a]
    b_list = list(b) if isinstance(b, (list, tuple)) else [b]
    na = len(a_list)
    m = a_list[0].shape[0]
    tm = _pick(m, tm_cap, 16)
    dotf = _nn if mode == "nn" else _nt
    n_in = len(tiled) + len(full)
    n_out = len(outs)
    first = 2 * na

    def body(*refs):
        i = pl.program_id(0)
        vals = [r[...] for r in refs[first:first + n_in]]
        acc = dotf(refs[0][...], refs[na][...])
        for p in range(1, na):
            acc = acc + dotf(refs[p][...], refs[na + p][...])
        o_t, o_a = fn(i, acc, *vals)
        for r, v in zip(refs[first + n_in:first + n_in + n_out], o_t):
            r[...] = v.astype(r.dtype)
        for r, v in zip(refs[first + n_in + n_out:], o_a):
            @pl.when(i == 0)
            def _():
                r[...] = v

            @pl.when(i > 0)
            def _():
                r[...] += v

    whole = lambda x: pl.BlockSpec(x.shape, lambda i, nd=x.ndim: (0,) * nd)
    in_specs = [pl.BlockSpec((tm, x.shape[1]), lambda i: (i, 0)) for x in a_list] + [_whole_spec(x) for x in b_list]
    in_specs += [_tile_spec(t, tm, m) for t in tiled]
    in_specs += [whole(x) for x in full]
    out_specs = [pl.BlockSpec((tm, w), lambda i: (i, 0)) for w, _ in outs]
    out_specs += [pl.BlockSpec((r, w), lambda i: (0, 0)) for r, w in accs]
    out_shape = [jax.ShapeDtypeStruct((m, w), dt) for w, dt in outs]
    out_shape += [jax.ShapeDtypeStruct((r, w), F32) for r, w in accs]
    return pl.pallas_call(
        body, name=name, grid=(m // tm,), in_specs=in_specs, out_specs=out_specs, out_shape=out_shape,
        compiler_params=pltpu.CompilerParams(dimension_semantics=("arbitrary",), vmem_limit_bytes=VMEM_LIMIT),
    )(*a_list, *b_list, *[t.arr for t in tiled], *full)


ATTN_Q_TILES = 4


def _attn_probs(q, k, row0):
    tq, tp = q.shape[0], k.shape[0]
    s = _nt(q, k) * (1.0 / math.sqrt(QK_HEAD))
    row = row0 + lax.broadcasted_iota(jnp.int32, (tq, tp), 0)
    col = lax.broadcasted_iota(jnp.int32, (tq, tp), 1)
    ok = (col <= row) & (col >= PAD)
    s = jnp.where(ok, s, NEG)
    m = jnp.max(s, axis=-1, keepdims=True)
    e = jnp.exp(s - m)
    return e * (1.0 / jnp.sum(e, axis=-1, keepdims=True))


def _attn_fwd(q, k, v):
    tp = q.shape[0]
    tq = tp // ATTN_Q_TILES

    def body(q_ref, k_ref, v_ref, o_ref):
        for i in range(ATTN_Q_TILES):
            rows = slice(i * tq, (i + 1) * tq)
            keys = slice(0, (i + 1) * tq)
            p = _attn_probs(q_ref[rows, :], k_ref[keys, :], i * tq)
            o_ref[rows, :] = _nn(p, v_ref[keys, :])

    return pl.pallas_call(
        body, name="attn_fwd", grid=(MLA_HEADS,),
        in_specs=[pl.BlockSpec((tp, HP), lambda h: (0, h)),
                  pl.BlockSpec((tp, HP), lambda h: (0, h)),
                  pl.BlockSpec((tp, V_HEAD), lambda h: (0, h))],
        out_specs=pl.BlockSpec((tp, V_HEAD), lambda h: (0, h)),
        out_shape=jax.ShapeDtypeStruct((tp, MLA_HEADS * V_HEAD), F32),
        compiler_params=pltpu.CompilerParams(dimension_semantics=("parallel",), vmem_limit_bytes=VMEM_LIMIT),
    )(q, k, v)


def _attn_bwd(q, k, v, do):
    tp = q.shape[0]
    tq = tp // ATTN_Q_TILES

    def body(q_ref, k_ref, v_ref, do_ref, dq_ref, dk_ref, dv_ref):
        for i in reversed(range(ATTN_Q_TILES)):
            rows = slice(i * tq, (i + 1) * tq)
            keys = slice(0, (i + 1) * tq)
            qb = q_ref[rows, :]
            kk = k_ref[keys, :]
            dob = do_ref[rows, :]
            p = _attn_probs(qb, kk, i * tq)
            dp = _nt(dob, v_ref[keys, :])
            delta = jnp.sum(p * dp, axis=-1, keepdims=True)
            ds = p * (dp - delta) * (1.0 / math.sqrt(QK_HEAD))
            dq_ref[rows, :] = _nn(ds, kk)
            if i == ATTN_Q_TILES - 1:
                dk_ref[...] = _tn(ds, qb)
                dv_ref[...] = _tn(p, dob)
            else:
                dk_ref[keys, :] += _tn(ds, qb)
                dv_ref[keys, :] += _tn(p, dob)

    full = lambda w: pl.BlockSpec((tp, w), lambda h: (0, h))
    return pl.pallas_call(
        body, name="attn_bwd", grid=(MLA_HEADS,),
        in_specs=[full(HP), full(HP), full(V_HEAD), full(V_HEAD)],
        out_specs=[full(HP), full(HP), full(V_HEAD)],
        out_shape=[jax.ShapeDtypeStruct((tp, MLA_HEADS * HP), F32),
                   jax.ShapeDtypeStruct((tp, MLA_HEADS * HP), F32),
                   jax.ShapeDtypeStruct((tp, MLA_HEADS * V_HEAD), F32)],
        compiler_params=pltpu.CompilerParams(dimension_semantics=("parallel",), vmem_limit_bytes=VMEM_LIMIT),
    )(q, k, v, do)


def _gdn_consts():
    c = DN_CHUNK
    r = lax.broadcasted_iota(jnp.int32, (c, c), 0)
    cc = lax.broadcasted_iota(jnp.int32, (c, c), 1)
    incl = r >= cc
    strict = r > cc
    return incl, strict


def _cumsum_rows(x, reverse=False):
    c = x.shape[0]
    row = lax.broadcasted_iota(jnp.int32, x.shape, 0)
    s = 1
    while s < c:
        if reverse:
            x = x + jnp.where(row < c - s, pltpu.roll(x, c - s, 0), 0.0)
        else:
            x = x + jnp.where(row >= s, pltpu.roll(x, s, 0), 0.0)
        s *= 2
    return x


def _each(fn, *lists):
    return [fn(*a) for a in zip(*lists)]


def _interleave(chains):
    chains = list(chains)
    while chains:
        for ch in list(chains):
            try:
                next(ch)
            except StopIteration:
                chains.remove(ch)


def _gdn_chunk_common(q_ref, k_ref, v_ref, g_ref, b_ref):
    c = DN_CHUNK
    incl, strict = _gdn_consts()
    sls = [(slice(c * sub, c * (sub + 1)), slice(DN_DIM * h, DN_DIM * (h + 1)))
           for sub in range(GDN_SUB_CHUNKS) for h in range(DN_HEADS)]
    q = [q_ref[sl] * (1.0 / math.sqrt(DN_DIM)) for sl in sls]
    k = [k_ref[sl] for sl in sls]
    v = [v_ref[sl] for sl in sls]
    g = [g_ref[sl] for sl in sls]
    beta = [b_ref[sl] for sl in sls]
    gc = [_cumsum_rows(x) for x in g]
    grow = [x.T[:c, :] for x in gc]
    kb = _each(jnp.multiply, k, beta)
    kk = _each(_nt, kb, k)
    qk = _each(_nt, q, k)
    gam = [jnp.exp(x) for x in gc]
    g_last = [_rowsum(x) for x in g]
    dm = [jnp.exp(jnp.where(incl, x[:, :c] - y, NEG)) for x, y in zip(gc, grow)]
    vb = _each(jnp.multiply, v, beta)
    kbg = _each(jnp.multiply, kb, gam)
    ek = [jnp.exp(x - y) for x, y in zip(g_last, gc)]
    kd = _each(jnp.multiply, k, ek)
    return dict(q=q, k=k, v=v, beta=beta, gc=gc, gam=gam, g_last=g_last, dm=dm, kb=kb, vb=vb,
                kbg=kbg, kk=kk, ek=ek, kd=kd, qk=qk, incl=incl, strict=strict, sls=sls)


def _gdn_fwd(proj, conv_w8, alog, dtb):
    tp = proj.shape[0]
    c = DN_CHUNK
    nch = tp // c
    blk = GDN_SUB_CHUNKS * c

    def body(x_ref, xp_ref, ab_ref, w8_ref, alog_ref, dtb_ref,
             o_ref, s_ref, t_ref, q_ref, k_ref, v_ref, g_ref, b_ref, s_scr):
        @pl.when(pl.program_id(0) == 0)
        def _():
            s_scr[...] = jnp.zeros_like(s_scr)

        staged, _ = _f_gdn_prep(pl.program_id(0), x_ref[...], xp_ref[...], ab_ref[...], w8_ref[...],
                                alog_ref[...], dtb_ref[...])
        for ref, val in zip((q_ref, k_ref, v_ref, g_ref, b_ref), staged):
            ref[...] = val
        eye = (lax.broadcasted_iota(jnp.int32, (c, c), 0) == lax.broadcasted_iota(jnp.int32, (c, c), 1)).astype(F32)
        x = _gdn_chunk_common(q_ref, k_ref, v_ref, g_ref, b_ref)
        heads = range(DN_HEADS)
        bp = [-jnp.where(x["strict"], kk * dm, 0.0) for kk, dm in zip(x["kk"], x["dm"])]
        t = [eye + b for b in bp]
        for _ in range(5):
            bp = [_nn(b, b, hp="3x") for b in bp]
            t = [tt + _nn(tt, b, hp="3x") for tt, b in zip(t, bp)]
        u = _each(_nn, t, x["vb"])
        w = _each(_nn, t, x["kbg"])
        qg = _each(jnp.multiply, x["q"], x["gam"])
        mqk = _each(jnp.multiply, x["qk"], x["dm"])
        s = [s_scr[h] for h in heads]
        for sub in range(GDN_SUB_CHUNKS):
            e = [DN_HEADS * sub + h for h in heads]
            v_new = [u[i] - _nn(w[i], s[h]) for h, i in zip(heads, e)]
            o = [_nn(qg[i], s[h]) + _nn(mqk[i], v_new[h]) for h, i in zip(heads, e)]
            s_new = [s[h] * jnp.exp(x["g_last"][i]) + _tn(x["kd"][i], v_new[h]) for h, i in zip(heads, e)]
            for h, i in zip(heads, e):
                s_ref[h, sub] = s[h]
                t_ref[h, sub] = t[i]
                o_ref[x["sls"][i]] = o[h]
            s = s_new
        for h in heads:
            s_scr[h] = s[h]

    sub = GDN_SUB_CHUNKS
    rb = lambda n: (n, 0)
    rows = pl.BlockSpec((blk, DN_WIDTH), rb)
    whole = lambda a: pl.BlockSpec(a.shape, lambda n: (0, 0))
    return pl.pallas_call(
        body, name="gdn_fwd", grid=(nch // sub,),
        in_specs=[pl.BlockSpec((blk, 3 * DN_WIDTH), rb),
                  pl.BlockSpec((8, 3 * DN_WIDTH), lambda n: (jnp.maximum(n * (blk // 8) - 1, 0), 0)),
                  pl.BlockSpec((blk, LANE), lambda n: (n, C_AB // LANE)),
                  whole(conv_w8), whole(alog), whole(dtb)],
        out_specs=[rows,
                   pl.BlockSpec((DN_HEADS, sub, DN_DIM, DN_DIM), lambda n: (0, n, 0, 0)),
                   pl.BlockSpec((DN_HEADS, sub, c, c), lambda n: (0, n, 0, 0))] + [rows] * 5,
        out_shape=[jax.ShapeDtypeStruct((tp, DN_WIDTH), F32),
                   jax.ShapeDtypeStruct((DN_HEADS, nch, DN_DIM, DN_DIM), F32),
                   jax.ShapeDtypeStruct((DN_HEADS, nch, c, c), F32)] + [jax.ShapeDtypeStruct((tp, DN_WIDTH), F32)] * 5,
        scratch_shapes=[pltpu.VMEM((DN_HEADS, DN_DIM, DN_DIM), F32)],
        compiler_params=pltpu.CompilerParams(dimension_semantics=("arbitrary",), vmem_limit_bytes=VMEM_LIMIT),
    )(proj, proj, proj, conv_w8, alog, dtb)


def _gdn_bwd(q, k, v, g, beta, s_all, t_all, do, proj, conv_w8, alog, dtb, after):
    tp = q.shape[0]
    c = DN_CHUNK
    nch = tp // c
    nblk = nch // GDN_SUB_CHUNKS
    blk = GDN_SUB_CHUNKS * c

    def body(q_ref, k_ref, v_ref, g_ref, b_ref, s_ref, t_ref, do_ref, x_ref, xp_ref, xn_ref, ab_ref,
             w8_ref, alog_ref, dtb_ref, _after_ref, dqkv_ref, dab_ref, dcw_ref, dalog_ref, ddtb_ref,
             ds_scr, dq_ref, dk_ref, dv_ref, dg_ref, db_ref, nxt_scr):
        step = pl.program_id(0)

        @pl.when(step == 0)
        def _():
            ds_scr[...] = jnp.zeros_like(ds_scr)
            nxt_scr[...] = jnp.zeros_like(nxt_scr)

        xs = _gdn_chunk_common(q_ref, k_ref, v_ref, g_ref, b_ref)

        ds_state = [ds_scr[h] for h in range(DN_HEADS)]

        def chain(sub, h):
            e = DN_HEADS * sub + h
            x = {key: (val[e] if isinstance(val, list) else val) for key, val in xs.items()}
            sl = x["sls"]
            qs, kx, vx, beta_, gam, dm = x["q"], x["k"], x["v"], x["beta"], x["gam"], x["dm"]
            kb, vb, kbg, kd, ek = x["kb"], x["vb"], x["kbg"], x["kd"], x["ek"]
            t = t_ref[h, sub]
            s = s_ref[h, sub]
            dsn = ds_state[h]
            dob = do_ref[sl]
            eg_last = jnp.exp(x["g_last"])
            u = _nn(t, vb)
            w = _nn(t, kbg)
            mqk = x["qk"] * dm
            qd = qs * gam
            dqd = _nt(dob, s)
            dkd_pre = _nn(kd, dsn)
            yield
            v_new = u - _nn(w, s)
            dv_new = _tn(mqk, dob) + dkd_pre
            dq = dqd * gam
            dgam = jnp.sum(dqd * qs, axis=1, keepdims=True)
            yield
            ds_state[h] = _tn(qd, dob) + eg_last * dsn - _tn(w, dv_new)
            dmm = jnp.where(x["incl"], _nt(dob, v_new), 0.0)
            dkd = _nt(v_new, dsn)
            dw = -_nt(dv_new, s)
            dvb = _tn(t, dv_new)
            dt = _nt(dv_new, vb)
            yield
            dqk = dmm * dm
            e_mat = dmm * mqk
            dq = dq + _nn(dqk, kx)
            dk = _tn(dqk, qs) + dkd * ek
            e1 = jnp.sum(dkd * kd, axis=1, keepdims=True)
            dgc = -e1
            dg_last = jnp.sum(e1) + eg_last * jnp.sum(s * dsn)
            dt = dt + _nt(dw, kbg)
            dkbg = _tn(t, dw)
            yield
            tdt = _tn(t, dt, hp="3x")
            yield
            da = jnp.where(x["strict"], -_nt(tdt, t, hp="3x"), 0.0)
            yield
            dkk = da * dm
            e_mat = e_mat + da * x["kk"] * dm
            dkb = _nn(dkk, kx) + dkbg * gam
            dk = dk + _tn(dkk, kb)
            dgam = dgam + jnp.sum(dkbg * kb, axis=1, keepdims=True)
            yield
            dk = dk + dkb * beta_
            dbeta = jnp.sum(dkb * kx, axis=1, keepdims=True) + jnp.sum(dvb * vx, axis=1, keepdims=True)
            dv = dvb * beta_
            dgc = dgc + jnp.sum(e_mat, axis=1, keepdims=True) + dgam * gam
            dgc = dgc - jnp.sum(e_mat.T, axis=1, keepdims=True)
            yield
            dg = _cumsum_rows(dgc, reverse=True) + dg_last
            yield
            dq_ref[sl] = dq * (1.0 / math.sqrt(DN_DIM))
            dk_ref[sl] = dk
            dv_ref[sl] = dv
            dg_ref[sl] = dg
            db_ref[sl] = jnp.broadcast_to(dbeta, (c, LANE))

        chains = []
        for sub in reversed(range(GDN_SUB_CHUNKS)):
            new = [chain(sub, h) for h in range(DN_HEADS)]
            for _ in range(3):
                for ch in new:
                    next(ch)
            chains += new
        _interleave(chains)
        for h in range(DN_HEADS):
            ds_scr[h] = ds_state[h]

        dq, dk, dv = dq_ref[...], dk_ref[...], dv_ref[...]
        outs, accs = _f_gdn_prep_bwd(
            nblk - 1 - step, x_ref[...], xp_ref[...], xn_ref[...], ab_ref[...], dq, nxt_scr[0], dk, nxt_scr[1],
            dv, nxt_scr[2], dg_ref[...], db_ref[...], w8_ref[...], alog_ref[...], dtb_ref[...], nt=nblk)
        nxt_scr[0] = dq[:8]
        nxt_scr[1] = dk[:8]
        nxt_scr[2] = dv[:8]
        dqkv_ref[...] = outs[0].astype(dqkv_ref.dtype)
        dab_ref[...] = outs[1].astype(dab_ref.dtype)
        for ref, val in zip((dcw_ref, dalog_ref, ddtb_ref), accs):
            @pl.when(step == 0)
            def _():
                ref[...] = val

            @pl.when(step > 0)
            def _():
                ref[...] += val

    sub = GDN_SUB_CHUNKS
    r8 = blk // 8
    rb = lambda n: (nblk - 1 - n, 0)
    hs = lambda n: (0, nblk - 1 - n, 0, 0)
    rows = pl.BlockSpec((blk, DN_WIDTH), rb)
    whole = lambda a: pl.BlockSpec(a.shape, lambda n: (0,) * a.ndim)
    wide = 3 * DN_WIDTH
    return pl.pallas_call(
        body, name="gdn_bwd", grid=(nblk,),
        in_specs=[rows] * 5
        + [pl.BlockSpec((DN_HEADS, sub, DN_DIM, DN_DIM), hs), pl.BlockSpec((DN_HEADS, sub, c, c), hs), rows,
           pl.BlockSpec((blk, wide), rb),
           pl.BlockSpec((8, wide), lambda n: (jnp.maximum((nblk - 1 - n) * r8 - 1, 0), 0)),
           pl.BlockSpec((8, wide), lambda n: (jnp.minimum((nblk - n) * r8, tp // 8 - 1), 0)),
           pl.BlockSpec((blk, LANE), lambda n: (nblk - 1 - n, C_AB // LANE)),
           whole(conv_w8), whole(alog), whole(dtb), _ANY_SPEC],
        out_specs=[pl.BlockSpec((blk, wide), rb), pl.BlockSpec((blk, LANE), rb),
                   whole(conv_w8), whole(alog), whole(dtb)],
        out_shape=[jax.ShapeDtypeStruct((tp, wide), _MXU), jax.ShapeDtypeStruct((tp, LANE), _MXU),
                   jax.ShapeDtypeStruct(conv_w8.shape, F32), jax.ShapeDtypeStruct(alog.shape, F32),
                   jax.ShapeDtypeStruct(dtb.shape, F32)],
        scratch_shapes=[pltpu.VMEM((DN_HEADS, DN_DIM, DN_DIM), F32)] + [pltpu.VMEM((blk, DN_WIDTH), F32)] * 5
        + [pltpu.VMEM((3, 8, DN_WIDTH), F32)],
        compiler_params=pltpu.CompilerParams(dimension_semantics=("arbitrary",), vmem_limit_bytes=VMEM_LIMIT),
    )(q, k, v, g, beta, s_all, t_all, do, proj, proj, proj, proj, conv_w8, alog, dtb, after)


def _silu_parts(x):
    s = _sigmoid(x)
    return x * s, s * (1.0 + x * (1.0 - s))


def _f_rms_bwd_add(i, x, dy, dres, w, *, mask_pad):
    dx, dwr = _rms_bwd(x, w, dy, x.shape[1])
    out = dres + dx
    if mask_pad:
        out = jnp.where(_row_ids(i, x.shape[0]) >= PAD, out, 0.0)
    return (out,), (_rowsum(dwr),)


def _rope(x, cos, sin_s):
    return x * cos + _swap_halves(x) * sin_s


def _rope_t(dy, cos, sin_s):
    return dy * cos + _swap_halves(dy * sin_s)


def _f_mla_qk(i, qf, kvf, kpe, cos, sin_s, qw, kw):
    qs, ks, vs = [], [], []
    for h in range(MLA_HEADS):
        qn, _ = _rms_fwd(qf[:, HP * h:HP * (h + 1)], qw, QK_HEAD)
        qs += [qn[:, :QK_NOPE], _rope(qn[:, QK_NOPE:], cos, sin_s)]
        kh = jnp.concatenate([kvf[:, HP * h:HP * h + QK_NOPE], kpe], axis=1)
        kn, _ = _rms_fwd(kh, kw, QK_HEAD)
        ks += [kn[:, :QK_NOPE], _rope(kn[:, QK_NOPE:], cos, sin_s)]
        vs.append(kvf[:, HP * h + QK_NOPE:HP * (h + 1)])
    return (jnp.concatenate(qs, axis=1), jnp.concatenate(ks, axis=1), jnp.concatenate(vs, axis=1)), ()


def _f_mla_front(i, ql, kvl, kpe, cos, sin_s, qaw, kvaw, wq_t, wkv, qw, kw):
    qn = _rms_fwd(ql, qaw, Q_LORA)[0].astype(_MXU)
    kvn = _rms_fwd(kvl, kvaw, KV_LORA)[0].astype(_MXU)
    qf = _nt(qn, wq_t)
    kvf = _nn(kvn, wkv)
    (q, k, v), _ = _f_mla_qk(i, qf, kvf, kpe, cos, sin_s, qw, kw)
    return (qn, kvn, qf, kvf, q, k, v), ()


def _f_mla_back(i, qf, kvf, kpe, cos, sin_s, dq, dk, dv, ql, kvl, qaw, kvaw, wq_t, wkv, qw, kw):
    (dqf, dkvf, dkpe), (dqw, dkw) = _f_mla_qk_bwd(i, qf, kvf, kpe, cos, sin_s, dq, dk, dv, qw, kw)
    dqf = dqf.astype(_MXU)
    dkvf = dkvf.astype(_MXU)
    dql, dqaw = _rms_bwd(ql, qaw, _nn(dqf, wq_t), Q_LORA)
    dkvl, dkvaw = _rms_bwd(kvl, kvaw, _nt(dkvf, wkv), KV_LORA)
    return (dqf, dkvf, dkpe, dql, dkvl), (dqw, dkw, _rowsum(dqaw), _rowsum(dkvaw))


def _f_mla_qk_bwd(i, qf, kvf, kpe, cos, sin_s, dq, dk, dv, qw, kw):
    dqf, dkvf = [], []
    dkpe = None
    dqw = None
    dkw = None
    for h in range(MLA_HEADS):
        dqh = dq[:, HP * h:HP * (h + 1)]
        dqn = jnp.concatenate([dqh[:, :QK_NOPE], _rope_t(dqh[:, QK_NOPE:], cos, sin_s)], axis=1)
        dx, dwr = _rms_bwd(qf[:, HP * h:HP * (h + 1)], qw, dqn, QK_HEAD)
        dqf.append(dx)
        dqw = _rowsum(dwr) if dqw is None else dqw + _rowsum(dwr)
        dkh = dk[:, HP * h:HP * (h + 1)]
        dkn = jnp.concatenate([dkh[:, :QK_NOPE], _rope_t(dkh[:, QK_NOPE:], cos, sin_s)], axis=1)
        kh = jnp.concatenate([kvf[:, HP * h:HP * h + QK_NOPE], kpe], axis=1)
        dx, dwr = _rms_bwd(kh, kw, dkn, QK_HEAD)
        dkvf += [dx[:, :QK_NOPE], dv[:, V_HEAD * h:V_HEAD * (h + 1)]]
        dkpe = dx[:, QK_NOPE:] if dkpe is None else dkpe + dx[:, QK_NOPE:]
        dkw = _rowsum(dwr) if dkw is None else dkw + _rowsum(dwr)
    return (jnp.concatenate(dqf, axis=1), jnp.concatenate(dkvf, axis=1), dkpe), (dqw, dkw)


def _gdn_act(i, x, halo, w8):
    halo = jnp.where(i > 0, halo, 0.0)
    c = _conv_fwd(x, halo, w8, DN_CONV)
    act, dact = _silu_parts(c)
    return act, dact


def _spread_heads(ab):
    tm = ab.shape[0]
    return jnp.concatenate([jnp.broadcast_to(ab[:, h:h + 1], (tm, DN_DIM)) for h in range(2 * DN_HEADS)], axis=1)


def _gather_heads(x):
    tm = x.shape[0]
    lane = lax.broadcasted_iota(jnp.int32, (tm, LANE), 1)
    out = jnp.zeros((tm, LANE), F32)
    for h in range(2 * DN_HEADS):
        out = out + jnp.where(lane == h, x[:, DN_DIM * h:DN_DIM * h + 1], 0.0)
    return out


def _f_gdn_prep(i, x, halo, ab, w8, alog, dtb):
    tm = x.shape[0]
    act, _ = _gdn_act(i, x, halo, w8)
    outs = []
    for part in range(2):
        for h in range(DN_HEADS):
            t = act[:, DN_WIDTH * part + DN_DIM * h:DN_WIDTH * part + DN_DIM * (h + 1)]
            outs.append(t * lax.rsqrt(jnp.sum(t * t, axis=-1, keepdims=True) + EPS))
    q = jnp.concatenate(outs[:DN_HEADS], axis=1)
    k = jnp.concatenate(outs[DN_HEADS:], axis=1)
    v = act[:, 2 * DN_WIDTH:]
    abb = _spread_heads(ab)
    valid = _row_ids(i, tm) >= PAD
    g = jnp.where(valid, -jnp.exp(alog) * _softplus(abb[:, :DN_WIDTH] + dtb), 0.0)
    beta = jnp.where(valid, _sigmoid(abb[:, DN_WIDTH:]), 0.0)
    return (q, k, v, g, beta), ()


def _f_gdn_prep_bwd(i, x, x_prev, x_next, ab, dq, dq_next, dk, dk_next, dv, dv_next, dg, dbeta,
                    w8, alog, dtb, *, nt):
    tm = x.shape[0]
    x_prev = jnp.where(i > 0, x_prev, 0.0)
    more = i < nt - 1
    ext = lambda t, t_next: jnp.concatenate([t, jnp.where(more, t_next, 0.0)], axis=0)
    taps = _conv_taps(jnp.concatenate([x, x_next], axis=0), x_prev, DN_CONV)
    c = _conv_from_taps(taps, w8)
    act, dact = _silu_parts(c)
    douts = []
    for part, dd in enumerate((ext(dq, dq_next), ext(dk, dk_next))):
        for h in range(DN_HEADS):
            t = act[:, DN_WIDTH * part + DN_DIM * h:DN_WIDTH * part + DN_DIM * (h + 1)]
            r = lax.rsqrt(jnp.sum(t * t, axis=-1, keepdims=True) + EPS)
            y = t * r
            dy = dd[:, DN_DIM * h:DN_DIM * (h + 1)]
            douts.append(r * (dy - y * jnp.sum(dy * y, axis=-1, keepdims=True)))
    douts.append(ext(dv, dv_next))
    dc = jnp.concatenate(douts, axis=1) * dact
    dqkv = _conv_bwd_x(dc[:tm], dc[tm:], w8, DN_CONV)
    dconv_w = _conv_bwd_w_taps(dc[:tm], taps)
    abb = _spread_heads(ab)
    valid = _row_ids(i, tm) >= PAD
    pre = abb[:, :DN_WIDTH] + dtb
    ea = jnp.exp(alog)
    g = -ea * _softplus(pre)
    dg = jnp.where(valid, dg, 0.0)
    dbeta = jnp.where(valid, dbeta, 0.0)
    da = dg * (-ea) * _sigmoid(pre)
    beta = _sigmoid(abb[:, DN_WIDTH:])
    db = dbeta * beta * (1.0 - beta)
    dab = _gather_heads(jnp.concatenate([da, db], axis=1))
    return (dqkv, dab), (dconv_w, _rowsum(dg * g), _rowsum(da))


def _f_mix(i, o_mla, o_dn, z, w_mla, w_dn):
    tm = o_mla.shape[0]
    valid = _row_ids(i, tm) >= PAD
    outs = []
    for h in range(MLA_HEADS):
        y, _ = _rms_fwd(o_mla[:, V_HEAD * h:V_HEAD * (h + 1)], w_mla, V_HEAD)
        outs.append(jnp.where(valid, y, 0.0))
    for h in range(DN_HEADS):
        y, _ = _rms_fwd(o_dn[:, DN_DIM * h:DN_DIM * (h + 1)], w_dn, DN_DIM)
        outs.append(y * _silu_parts(z[:, DN_DIM * h:DN_DIM * (h + 1)])[0])
    return (jnp.concatenate(outs, axis=1),), ()


def _f_mix_bwd(i, o_mla, o_dn, z, dy_mla, dy_dn, w_mla, w_dn):
    tm = o_mla.shape[0]
    valid = _row_ids(i, tm) >= PAD
    d_mla, d_dn, d_z = [], [], []
    dw_mla = None
    dw_dn = None
    for h in range(MLA_HEADS):
        sl = slice(V_HEAD * h, V_HEAD * (h + 1))
        dx, dwr = _rms_bwd(o_mla[:, sl], w_mla, jnp.where(valid, dy_mla[:, sl], 0.0), V_HEAD)
        d_mla.append(dx)
        dw_mla = _rowsum(dwr) if dw_mla is None else dw_mla + _rowsum(dwr)
    for h in range(DN_HEADS):
        sl = slice(DN_DIM * h, DN_DIM * (h + 1))
        y, _ = _rms_fwd(o_dn[:, sl], w_dn, DN_DIM)
        sz, dsz = _silu_parts(z[:, sl])
        d_z.append(dy_dn[:, sl] * y * dsz)
        dx, dwr = _rms_bwd(o_dn[:, sl], w_dn, dy_dn[:, sl] * sz, DN_DIM)
        d_dn.append(dx)
        dw_dn = _rowsum(dwr) if dw_dn is None else dw_dn + _rowsum(dwr)
    return ((jnp.concatenate(d_mla, axis=1), jnp.concatenate(d_dn, axis=1), jnp.concatenate(d_z, axis=1)),
            (dw_mla, dw_dn))


def _f_ffn_act_bwd(i, gp, gp_prev, gp_next, up, up_next, dact, dact_next, w8, b, *, nt):
    tm = gp.shape[0]
    gp_prev = jnp.where(i > 0, gp_prev, 0.0)
    dact_next = jnp.where(i < nt - 1, dact_next, 0.0)
    cat = lambda t, t_next: jnp.concatenate([t, t_next], axis=0)
    taps = _conv_taps(cat(gp, gp_next), gp_prev, FFN_CONV)
    gate = _conv_from_taps(taps, w8) + b
    sg, dsg = _silu_parts(gate)
    dact_e = cat(dact, dact_next)
    dgate = dact_e * cat(up, up_next) * dsg
    dgate_pre = _conv_bwd_x(dgate[:tm], dgate[tm:], w8, FFN_CONV)
    dup = dact * sg[:tm]
    return (dgate_pre, dup), (_conv_bwd_w_taps(dgate[:tm], taps), _rowsum(dgate[:tm]))


def _f_loss(i, h3, tgt):
    tm = h3.shape[0]
    diff = jnp.where(_row_ids(i, tm) >= ROW0, h3 - tgt, 0.0)
    part = 0.5 * jnp.sum(diff * diff) * (1.0 / D_MODEL)
    return (diff * (1.0 / D_MODEL),), (jnp.full((1, LANE), part, F32),)


def _local_step(h0, tgt, w, token, late_weights, grads_ready):
    tp = h0.shape[0]
    nt = tp // TM
    proj, u = _norm_mm("in_proj", h0, w["attn_norm_w"], w["w_in"], after=token)
    p_qkv = lambda kind="cur": _In(proj, 3 * DN_WIDTH, 0, kind)
    p_z = _In(proj, DN_WIDTH, C_Z // DN_WIDTH)
    p_ql = _In(proj, Q_LORA, C_QL // Q_LORA)
    p_kvl = _In(proj, KV_LORA, C_KVL // KV_LORA)
    p_kpe = _In(proj, LANE, C_KPE // LANE)
    p_ab = _In(proj, LANE, C_AB // LANE)
    cos, sin_s = _In(w["cos"]), _In(w["sin_s"])

    mla_w = [w["q_a_norm_w"], w["kv_a_norm_w"], w["w_q_b"], w["w_kv_b"], w["q_norm_w"], w["k_norm_w"]]
    tm_mla = _pick(tp, 288, 16)
    wide = MLA_HEADS * HP
    qn, kvn, qf, kvf, q, k, v = _rows(
        "mla_front", _f_mla_front, [p_ql, p_kvl, p_kpe, cos, sin_s], mla_w,
        [(Q_LORA, _MXU), (KV_LORA, _MXU), (wide, F32), (wide, F32), (wide, _MXU), (wide, _MXU),
         (MLA_HEADS * V_HEAD, _MXU)], tm=tm_mla)
    o_mla = _attn_fwd(q, k, v)

    dn_w = [w["dn_conv_w"], w["alog_b"], w["dtb_b"]]
    o_dn, s_all, t_all, gq, gk, gv, gg, gb = _gdn_fwd(proj, *dn_w)

    out_w = [w["mla_out_norm_w"], w["dn_out_norm_w"]]
    w = dict(w, **late_weights((o_mla, o_dn), _LATE[:3]))
    h2, mixed = _pro_mm("mix_out_proj", lambda i, *t: _f_mix(i, *t)[0][0], [_In(o_mla), _In(o_dn), p_z], out_w,
                        D_MODEL, w["w_out"], h0)

    ffn_w = [w["ffn_conv_w"], w["ffn_conv_b"]]
    hn, gate_pre, up, act = _ffn_in(h2, w["ffn_norm_w"], w["w_gate"], w["w_up"], *ffn_w)
    w = dict(w, **late_weights(act, _LATE[3:]))
    dh3, loss = _mm_rows("ffn_down_loss", act, w["w_down"], "nn", lambda i, y, r, t: _f_loss(i, r + y, t),
                         [_In(h2), _In(tgt)], [], [(D_MODEL, F32)], [(1, LANE)])

    g = {}
    dact = _mm("ffn_down_dx", dh3, w["w_down"], "nt")
    g["w_down"] = _mm("ffn_down_dw", act, dh3, "tn", out_dtype=_MXU)
    dgate_pre, dup, g["ffn_conv_w"], g["ffn_conv_b"] = _rows(
        "ffn_act_bwd", functools.partial(_f_ffn_act_bwd, nt=nt),
        [_In(gate_pre), _In(gate_pre, kind="prev"), _In(gate_pre, kind="next"), _In(up), _In(up, kind="next"),
         _In(dact), _In(dact, kind="next")], ffn_w,
        [(D_FF, _MXU), (D_FF, _MXU)], [(8, D_FF), (1, D_FF)])
    g["w_gate"], g["w_up"] = _mm_tn2("ffn_gate_up_dw", dgate_pre, dup, hn, out_dtype=_MXU)
    tok = grads_ready(g, ("w_down", "w_gate", "w_up"))
    dh2, g["ffn_norm_w"] = _mm_rows(
        "ffn_gate_up_dx_rms", [dgate_pre, dup], [w["w_gate"], w["w_up"]], "nn",
        lambda i, dy, x, dres, nw, _tok: _f_rms_bwd_add(i, x, dy, dres, nw, mask_pad=True),
        [_In(h2), _In(dh3)], [w["ffn_norm_w"], tok], [(D_MODEL, F32)], [(1, D_MODEL)])

    g["w_out"] = _mm("out_proj_dw", mixed, dh2, "tn", out_dtype=_MXU)
    half = MLA_HEADS * V_HEAD
    do_mla, do_dn, dz, g["mla_out_norm_w"], g["dn_out_norm_w"] = _mm_rows(
        "out_proj_dx_mix", dh2, w["w_out"], "nt",
        lambda i, dm, om, od, z, wm, wd: _f_mix_bwd(i, om, od, z, dm[:, :half], dm[:, half:], wm, wd),
        [_In(o_mla), _In(o_dn), p_z], out_w,
        [(half, F32), (DN_WIDTH, F32), (DN_WIDTH, _MXU)], [(1, V_HEAD), (1, DN_DIM)])

    dq, dk, dv = _attn_bwd(q, k, v, do_mla)
    dqf, dkvf, dkpe, dql, dkvl, g["q_norm_w"], g["k_norm_w"], g["q_a_norm_w"], g["kv_a_norm_w"] = _rows(
        "mla_back", _f_mla_back,
        [_In(qf), _In(kvf), p_kpe, cos, sin_s, _In(dq), _In(dk), _In(dv), p_ql, p_kvl], mla_w,
        [(wide, _MXU), (wide, _MXU), (LANE, _MXU), (Q_LORA, _MXU), (KV_LORA, _MXU)],
        [(1, HP), (1, HP), (1, Q_LORA), (1, KV_LORA)], tm=tm_mla)
    g["w_q_b"], g["w_kv_b"] = _mm_tn_pair("mla_b_dw", dqf, qn, kvn, dkvf)
    tok = grads_ready(g, ("w_out", "w_q_b", "w_kv_b"))

    dqkv, dab, g["dn_conv_w"], g["alog_b"], g["dtb_b"] = _gdn_bwd(
        gq, gk, gv, gg, gb, s_all, t_all, do_dn, proj, *dn_w, tok)

    dproj = jnp.concatenate([dqkv, dz, dql, dkvl, dkpe, dab], axis=1)
    g["w_in"] = _mm("in_proj_dw", dproj, u, "tn", out_dtype=_MXU)
    tok = grads_ready(g, ("w_in",))
    dh0, g["attn_norm_w"] = _mm_rows(
        "in_proj_dx_rms", dproj, w["w_in"], "nn",
        lambda i, du, x, dres, nw, _tok: _f_rms_bwd_add(i, x, du, dres, nw, mask_pad=False),
        [_In(h0), _In(dh2)], [w["attn_norm_w"], tok], [(D_MODEL, F32)], [(1, D_MODEL)])
    return loss, dh0, g


def _w_in_to_padded(w):
    c1, c2, c3 = Q_LORA, Q_LORA + KV_LORA, Q_LORA + KV_LORA + QK_ROPE
    c4 = c3 + 3 * DN_WIDTH
    c5 = c4 + DN_WIDTH
    z = lambda n: jnp.zeros((n, w.shape[1]), w.dtype)
    return jnp.concatenate([w[c3:c4], w[c4:c5], w[:c1], w[c1:c2], w[c2:c3], z(LANE - QK_ROPE),
                            w[c5:], z(LANE - 2 * DN_HEADS)], axis=0)


def _w_in_from_padded(g):
    return jnp.concatenate([g[C_QL:C_QL + Q_LORA], g[C_KVL:C_KVL + KV_LORA], g[C_KPE:C_KPE + QK_ROPE],
                            g[:C_Z + DN_WIDTH], g[C_AB:C_AB + 2 * DN_HEADS]], axis=0)


def _w_q_b_to_padded(w):
    r = w.shape[1]
    w = w.reshape(MLA_HEADS, QK_HEAD, r)
    return jnp.pad(w, ((0, 0), (0, HP - QK_HEAD), (0, 0))).reshape(MLA_HEADS * HP, r)


def _w_q_b_from_padded(g):
    r = g.shape[1]
    return g.reshape(MLA_HEADS, HP, r)[:, :QK_HEAD].reshape(MLA_HEADS * QK_HEAD, r)


def _pad_rows8(w):
    return jnp.pad(w, ((0, 8 - w.shape[0]), (0, 0)))


def _prepare(full, tp):
    w = {}
    mx = lambda a: a.astype(_MXU)
    w["attn_norm_w"] = full["attn_norm_w"]
    w["w_in"] = mx(_w_in_to_padded(full["w_in"]))
    w["q_a_norm_w"] = full["q_a_norm_w"]
    w["kv_a_norm_w"] = full["kv_a_norm_w"]
    w["w_q_b"] = mx(_w_q_b_to_padded(full["w_q_b"]))
    w["w_kv_b"] = mx(full["w_kv_b"])
    w["q_norm_w"] = jnp.pad(full["q_norm_w"], ((0, 0), (0, HP - QK_HEAD)))
    w["k_norm_w"] = jnp.pad(full["k_norm_w"], ((0, 0), (0, HP - QK_HEAD)))
    w["mla_out_norm_w"] = full["mla_out_norm_w"]
    w["dn_out_norm_w"] = full["dn_out_norm_w"]
    w["dn_conv_w"] = _pad_rows8(full["dn_conv_w"])
    w["alog_b"] = jnp.repeat(full["dn_A_log"], DN_DIM, axis=1)
    w["dtb_b"] = jnp.repeat(full["dn_dt_bias"], DN_DIM, axis=1)
    w["ffn_norm_w"] = full["ffn_norm_w"]
    w["ffn_conv_w"] = _pad_rows8(full["ffn_conv_w"])
    w["ffn_conv_b"] = full["ffn_conv_b"]
    for n in _LATE:
        if n in full:
            w[n] = mx(full[n])
    half = QK_ROPE // 2
    inv = ROPE_THETA ** (-jnp.arange(half, dtype=F32) / half)
    ang = (jnp.arange(tp, dtype=jnp.int32) - PAD).astype(F32)[:, None] * inv[None, :]
    zc = jnp.zeros((tp, LANE - QK_ROPE), F32)
    w["cos"] = jnp.concatenate([jnp.cos(ang), jnp.cos(ang), zc], axis=1)
    w["sin_s"] = jnp.concatenate([-jnp.sin(ang), jnp.sin(ang), zc], axis=1)
    return w


def _grads_to_natural(g):
    convert = {
        "w_in": ("w_in", _w_in_from_padded),
        "w_q_b": ("w_q_b", _w_q_b_from_padded),
        "q_norm_w": ("q_norm_w", lambda a: a[:, :QK_HEAD]),
        "k_norm_w": ("k_norm_w", lambda a: a[:, :QK_HEAD]),
        "dn_conv_w": ("dn_conv_w", lambda a: a[:DN_CONV]),
        "ffn_conv_w": ("ffn_conv_w", lambda a: a[:FFN_CONV]),
        "alog_b": ("dn_A_log", lambda a: a[:, ::DN_DIM]),
        "dtb_b": ("dn_dt_bias", lambda a: a[:, ::DN_DIM]),
    }
    n = {}
    for key, a in g.items():
        name, fn = convert.get(key, (key, lambda t: t))
        n[name] = fn(a)
    return n


_MESH = pl.DeviceIdType.MESH
_ANY = pl.BlockSpec(memory_space=pl.ANY)
_CHIP_FLIPS = ((1, 0), (0, 1), (1, 1))


def _me():
    return lax.axis_index("x"), lax.axis_index("y"), lax.axis_index("c")


def _all_gather(name, blk, after):
    after = list(after)

    def body(x_ref, *rest):
        out_ref, send_sems, recv_sems, local_sem = rest[len(after):]
        x, y, c = _me()
        me, sib = (x, y, c), (x, y, 1 - c)
        chips = [(x ^ fx, y ^ fy) for fx, fy in _CHIP_FLIPS]

        def slot(p):
            return out_ref.at[4 * p[0] + 2 * p[1] + p[2]]

        def copy(k, block, to, src=None):
            return pltpu.make_async_remote_copy(
                src_ref=slot(block) if src is None else src, dst_ref=slot(block),
                send_sem=send_sems.at[k], recv_sem=recv_sems.at[k], device_id=to, device_id_type=_MESH)

        mine = pltpu.make_async_copy(x_ref, slot(me), local_sem)
        mine.start()
        first = [copy(0, me, sib, src=x_ref)]
        first += [copy(1 + j, me, (*chip, c), src=x_ref) for j, chip in enumerate(chips)]
        for cp in first:
            cp.start()
        passed = [copy(4 + j, (*chip, c), sib) for j, chip in enumerate(chips)]
        for j, chip in enumerate(chips):
            copy(1 + j, (*chip, c), me).wait_recv()
            passed[j].start()
        copy(0, sib, me).wait_recv()
        for j, chip in enumerate(chips):
            copy(4 + j, (*chip, 1 - c), me).wait_recv()
        for cp in first + passed:
            cp.wait_send()
        mine.wait()

    return pl.pallas_call(
        body, name=name, in_specs=[_ANY] * (1 + len(after)), out_specs=_ANY,
        out_shape=jax.ShapeDtypeStruct((N_DEV,) + blk.shape, blk.dtype),
        scratch_shapes=[pltpu.SemaphoreType.DMA((7,)), pltpu.SemaphoreType.DMA((7,)), pltpu.SemaphoreType.DMA],
    )(blk, *after)


def _row_tile(r):
    divs = [d for d in range(16, min(r, 512) + 1, 16) if r % d == 0]
    return divs[-1] if divs else r


def _adam_math(g, w, m, v):
    m_new = ADAM_B1 * m + (1.0 - ADAM_B1) * g
    v_new = ADAM_B2 * v + (1.0 - ADAM_B2) * (g * g)
    m_hat = m_new / (1.0 - ADAM_B1 ** ADAM_STEP)
    v_hat = v_new / (1.0 - ADAM_B2 ** ADAM_STEP)
    return -ADAM_LR * (m_hat / (jnp.sqrt(v_hat) + ADAM_EPS) + ADAM_WD * w), m_new, v_new


def _adam_vectors(name, row, items, ws, ms, vs):
    k = len(items)

    def body(row_ref, *refs):
        w_refs, m_refs, v_refs = refs[:k], refs[k:2 * k], refs[2 * k:3 * k]
        outs = refs[3 * k:]
        for idx, (off, n, per_head) in enumerate(items):
            if per_head:
                spread = row_ref[:, off:off + DN_WIDTH]
                lane = lax.broadcasted_iota(jnp.int32, (1, LANE), 1)
                g = jnp.zeros((1, LANE), F32)
                for h in range(DN_HEADS):
                    g = g + jnp.where(lane == h, spread[:, DN_DIM * h:DN_DIM * h + 1], 0.0)
                g = g[:, :n]
            else:
                g = row_ref[:, off:off + n]
            d, m_new, v_new = _adam_math(g, w_refs[idx][...], m_refs[idx][...], v_refs[idx][...])
            for kind, val in enumerate((g, d, m_new, v_new)):
                outs[kind * k + idx][...] = val

    shapes = [jax.ShapeDtypeStruct((1, n), F32) for _, n, _ in items]
    res = pl.pallas_call(body, name=name, out_shape=shapes * 4)(row, *ws, *ms, *vs)
    return [list(res[kind * k:(kind + 1) * k]) for kind in range(4)]


def _adam_arrays(name, gs, ws, ms, vs):
    k = len(gs)

    def body(*refs):
        outs = refs[4 * k:]
        for idx in range(k):
            res = _adam_math(refs[idx][...], refs[k + idx][...], refs[2 * k + idx][...], refs[3 * k + idx][...])
            for kind, val in enumerate(res):
                outs[kind * k + idx][...] = val

    shapes = [jax.ShapeDtypeStruct(w.shape, F32) for w in ws]
    res = pl.pallas_call(body, name=name, out_shape=shapes * 3)(*gs, *ws, *ms, *vs)
    return [list(res[kind * k:(kind + 1) * k]) for kind in range(3)]


def _sum_parts(name, parts):
    _, r, cols = parts[0][0].shape
    tm = _row_tile(r)
    idx = jnp.stack([jnp.asarray(s, jnp.int32) for _, s in parts])
    n = len(parts)

    def body(idx_ref, *refs):
        g = refs[0][0].astype(F32)
        for p_ref in refs[1:n]:
            g = g + p_ref[0].astype(F32)
        refs[n][...] = g

    return pl.pallas_call(
        body, name=name,
        grid_spec=pltpu.PrefetchScalarGridSpec(
            num_scalar_prefetch=1, grid=(r // tm,),
            in_specs=[pl.BlockSpec((1, tm, cols), lambda i, idx_ref, p=p: (idx_ref[p], i, 0)) for p in range(n)],
            out_specs=pl.BlockSpec((tm, cols), lambda i, idx_ref: (i, 0))),
        out_shape=jax.ShapeDtypeStruct((r, cols), F32),
        compiler_params=pltpu.CompilerParams(dimension_semantics=("parallel",)),
    )(idx, *[a for a, _ in parts])


def _adam(name, parts, w, m, v):
    r, cols = w.shape
    tm = _row_tile(r)
    tc = cols // 4 if (r // tm < 4 and cols % (4 * LANE) == 0) else cols
    idx = jnp.stack([jnp.asarray(s, jnp.int32) for _, s in parts])
    n = len(parts)

    def body(idx_ref, *refs):
        g = refs[0][0].astype(F32)
        for p_ref in refs[1:n]:
            g = g + p_ref[0].astype(F32)
        w_ref, m_ref, v_ref, g_out, d_out, m_out, v_out = refs[n:]
        g_out[...] = g
        d_out[...], m_out[...], v_out[...] = _adam_math(g, w_ref[...], m_ref[...], v_ref[...])

    part_specs = [pl.BlockSpec((1, tm, tc), lambda i, j, idx_ref, p=p: (idx_ref[p], i, j)) for p in range(n)]
    flat = pl.BlockSpec((tm, tc), lambda i, j, idx_ref: (i, j))
    return pl.pallas_call(
        body, name=name,
        grid_spec=pltpu.PrefetchScalarGridSpec(
            num_scalar_prefetch=1, grid=(r // tm, cols // tc), in_specs=part_specs + [flat] * 3,
            out_specs=[flat] * 4),
        out_shape=[jax.ShapeDtypeStruct((r, cols), F32)] * 4,
        compiler_params=pltpu.CompilerParams(dimension_semantics=("parallel", "parallel")),
    )(idx, *[a for a, _ in parts], w, m, v)


def _all_gather_many(name, blks):
    n = len(blks)

    def body(*refs):
        x_refs, out_refs = refs[:n], refs[n:2 * n]
        send_sems, recv_sems, local_sems = refs[2 * n:]
        x, y, c = _me()
        me, sib = (x, y, c), (x, y, 1 - c)
        chips = [(x ^ fx, y ^ fy) for fx, fy in _CHIP_FLIPS]

        def slot(a, p):
            return out_refs[a].at[4 * p[0] + 2 * p[1] + p[2]]

        def copy(a, k, block, to, src=None):
            return pltpu.make_async_remote_copy(
                src_ref=slot(a, block) if src is None else src, dst_ref=slot(a, block),
                send_sem=send_sems.at[7 * a + k], recv_sem=recv_sems.at[7 * a + k], device_id=to,
                device_id_type=_MESH)

        mine = [pltpu.make_async_copy(x_refs[a], slot(a, me), local_sems.at[a]) for a in range(n)]
        first = []
        for a in range(n):
            mine[a].start()
            first.append(copy(a, 0, me, sib, src=x_refs[a]))
            first += [copy(a, 1 + j, me, (*chip, c), src=x_refs[a]) for j, chip in enumerate(chips)]
        for cp in first:
            cp.start()
        passed = []
        for j, chip in enumerate(chips):
            for a in range(n):
                copy(a, 1 + j, (*chip, c), me).wait_recv()
                cp = copy(a, 4 + j, (*chip, c), sib)
                cp.start()
                passed.append(cp)
        for a in range(n):
            copy(a, 0, sib, me).wait_recv()
            for j, chip in enumerate(chips):
                copy(a, 4 + j, (*chip, 1 - c), me).wait_recv()
        for cp in first + passed:
            cp.wait_send()
        for cp in mine:
            cp.wait()

    return pl.pallas_call(
        body, name=name, in_specs=[_ANY] * n, out_specs=[_ANY] * n,
        out_shape=[jax.ShapeDtypeStruct((N_DEV,) + b.shape, b.dtype) for b in blks],
        scratch_shapes=[pltpu.SemaphoreType.DMA((7 * n,)), pltpu.SemaphoreType.DMA((7 * n,)),
                        pltpu.SemaphoreType.DMA((n,))],
    )(*blks)


_HBM = pl.BlockSpec(memory_space=pltpu.HBM)
_SEM = pl.BlockSpec(memory_space=pltpu.SEMAPHORE)
_EFFECT = pltpu.SideEffectType.DATAFLOW_SIDE_EFFECTING


def _push_copies(src_refs, land_refs, send_sems, recv_sems, src_by_peer, first=0):
    x, y, c = _me()
    my_id = 4 * x + 2 * y + c
    out = []
    for k in range(len(src_refs)):
        a = first + k
        for f in range(1, N_DEV):
            px, py, pc = x ^ (f >> 2), y ^ ((f >> 1) & 1), c ^ (f & 1)
            pid = 4 * px + 2 * py + pc
            src = src_refs[k].at[pid] if src_by_peer else src_refs[k]
            start = pltpu.make_async_remote_copy(
                src_ref=src, dst_ref=land_refs[k].at[my_id], send_sem=send_sems.at[7 * a + f - 1],
                recv_sem=recv_sems.at[7 * a + f - 1], device_id=(px, py, pc), device_id_type=_MESH)
            landed = pltpu.make_async_remote_copy(
                src_ref=src, dst_ref=land_refs[k].at[pid], send_sem=send_sems.at[7 * a + f - 1],
                recv_sem=recv_sems.at[7 * a + f - 1], device_id=(px, py, pc), device_id_type=_MESH)
            out.append((start, landed))
    return out


def _push_start(name, srcs, src_by_peer, after):
    n = len(srcs)
    lands = [jax.ShapeDtypeStruct((N_DEV,) + (s.shape[1:] if src_by_peer else s.shape), s.dtype) for s in srcs]

    def body(*refs):
        src_refs, land_refs = refs[:n], refs[n:2 * n]
        send_sems, recv_sems = refs[2 * n + 1], refs[2 * n + 2]
        token = refs[-1]
        for start, _ in _push_copies(src_refs, land_refs, send_sems, recv_sems, src_by_peer):
            start.start()
        token[...] = jnp.zeros_like(token)

    hbm = lambda a: pltpu.with_memory_space_constraint(a, pltpu.HBM)
    res = pl.pallas_call(
        body, name=name,
        out_shape=(pltpu.SemaphoreType.DMA((7 * n,)), pltpu.SemaphoreType.DMA((7 * n,)),
                   *[pltpu.HBM(s.shape, s.dtype) for s in srcs], *[pltpu.HBM(s.shape, s.dtype) for s in lands],
                   jax.ShapeDtypeStruct((8, LANE), F32)),
        in_specs=[_HBM] * (2 * n) + [_ANY],
        out_specs=(_SEM, _SEM, *[_HBM] * (2 * n), pl.BlockSpec(memory_space=pltpu.VMEM)),
        input_output_aliases={i: 2 + i for i in range(2 * n)},
        compiler_params=pltpu.CompilerParams(has_side_effects=_EFFECT),
    )(*[hbm(s) for s in srcs], *[hbm(lax.empty(s.shape, s.dtype)) for s in lands], after)
    return res[0], res[1], list(res[2:2 + n]), list(res[2 + n:2 + 2 * n]), res[-1]


def _push_wait(name, send_sems, recv_sems, srcs, lands, src_by_peer, after, first=0):
    n = len(srcs)
    after = list(after) if isinstance(after, (list, tuple)) else [after]

    def body(*refs):
        src_refs, land_refs = refs[:n], refs[n:2 * n]
        s_sems, r_sems = refs[2 * n], refs[2 * n + 1]
        for _, landed in _push_copies(src_refs, land_refs, s_sems, r_sems, src_by_peer, first):
            landed.wait_send()
            landed.wait_recv()

    res = pl.pallas_call(
        body, name=name,
        out_shape=tuple(pltpu.HBM(s.shape, s.dtype) for s in list(srcs) + list(lands)),
        in_specs=[_HBM] * (2 * n) + [_SEM, _SEM] + [_ANY] * len(after),
        out_specs=tuple([_HBM] * (2 * n)),
        input_output_aliases={i: i for i in range(2 * n)},
        compiler_params=pltpu.CompilerParams(has_side_effects=_EFFECT),
    )(*srcs, *lands, send_sems, recv_sems, *after)
    return list(res[:n]), list(res[n:])


_SHARDED = (
    ("meta_tokens", 1, (N_META, D_MODEL)),
    ("w_in", 1, (D_MODEL, IN_COLS)),
    ("w_q_b", 1, (Q_LORA, MLA_HEADS * QK_HEAD)),
    ("w_kv_b", 1, (KV_LORA, MLA_HEADS * (QK_NOPE + V_HEAD))),
    ("dn_conv_w", 1, (DN_CONV, 3 * DN_WIDTH)),
    ("w_out", 0, (2 * DN_WIDTH, D_MODEL)),
    ("w_gate", 1, (D_MODEL, D_FF)),
    ("w_up", 1, (D_MODEL, D_FF)),
    ("ffn_conv_w", 1, (FFN_CONV, D_FF)),
    ("w_down", 0, (D_FF, D_MODEL)),
)
_F32_GATHERED = ("meta_tokens", "dn_conv_w", "ffn_conv_w")
_EARLY = ("w_in", "w_q_b", "w_kv_b")
_LATE = ("w_out", "w_gate", "w_up", "w_down")
_TRANSPOSED = ("w_in", "w_q_b", "w_gate", "w_up")
_REPLICATED = (
    ("attn_norm_w", D_MODEL), ("q_a_norm_w", Q_LORA), ("kv_a_norm_w", KV_LORA), ("q_norm_w", QK_HEAD),
    ("k_norm_w", QK_HEAD), ("mla_out_norm_w", V_HEAD), ("dn_A_log", DN_HEADS), ("dn_dt_bias", DN_HEADS),
    ("dn_out_norm_w", DN_DIM), ("ffn_norm_w", D_MODEL), ("ffn_conv_b", D_FF),
)
_SMALL_BLOCK = (8, 512)


def _local_shape(dim, shape):
    return (shape[0] // N_DEV, shape[1]) if dim == 0 else (shape[0], shape[1] // N_DEV)


def _from_blocks(blocks, dim, shape):
    r, c = shape
    if dim == 0:
        return blocks.reshape(r, c)
    return blocks.reshape(N_DEV, r, c // N_DEV).transpose(1, 0, 2).reshape(r, c)


def _split(flat, sizes):
    out, o = [], 0
    for s in sizes:
        out.append(flat[..., o:o + s])
        o += s
    return out


def kernel(x, meta_tokens, attn_norm_w, w_in, q_a_norm_w, w_q_b, kv_a_norm_w, w_kv_b, q_norm_w, k_norm_w, mla_out_norm_w, dn_conv_w, dn_A_log, dn_dt_bias, dn_out_norm_w, w_out, ffn_norm_w, w_gate, w_up, ffn_conv_w, ffn_conv_b, w_down, loss_target, m_meta_tokens, m_attn_norm_w, m_w_in, m_q_a_norm_w, m_w_q_b, m_kv_a_norm_w, m_w_kv_b, m_q_norm_w, m_k_norm_w, m_mla_out_norm_w, m_dn_conv_w, m_dn_A_log, m_dn_dt_bias, m_dn_out_norm_w, m_w_out, m_ffn_norm_w, m_w_gate, m_w_up, m_ffn_conv_w, m_ffn_conv_b, m_w_down, v_meta_tokens, v_attn_norm_w, v_w_in, v_q_a_norm_w, v_w_q_b, v_kv_a_norm_w, v_w_kv_b, v_q_norm_w, v_k_norm_w, v_mla_out_norm_w, v_dn_conv_w, v_dn_A_log, v_dn_dt_bias, v_dn_out_norm_w, v_w_out, v_ffn_norm_w, v_w_gate, v_w_up, v_ffn_conv_w, v_ffn_conv_b, v_w_down):
    names = [n for n, _, _ in _SHARDED] + [n for n, _ in _REPLICATED]
    given = dict(locals())
    two_d = lambda a: a.reshape(a.shape[-2:])
    view = lambda a, n: two_d(a).T if n in _TRANSPOSED else two_d(a)
    wl = {n: view(given[n], n) for n in names}
    ml = {n: view(given["m_" + n], n) for n in names}
    vl = {n: view(given["v_" + n], n) for n in names}
    out_shapes = {n: given[n].shape for n in names}

    spec = {n: (d, s) for n, d, s in _SHARDED}
    small_sizes = [math.prod(_local_shape(*spec[n])) for n in _F32_GATHERED]

    def small_block(d):
        cat = jnp.concatenate([d[n].reshape(d[n].shape[:-2] + (-1,)) for n in _F32_GATHERED], axis=-1)
        pad = [(0, 0)] * (cat.ndim - 1) + [(0, math.prod(_SMALL_BLOCK) - cat.shape[-1])]
        return jnp.pad(cat, pad).reshape(cat.shape[:-1] + _SMALL_BLOCK)

    def shard(n):
        return wl[n].astype(_MXU)

    def from_slots(n, blocks):
        d, s = spec[n]
        if d == 0 or n in _TRANSPOSED:
            return blocks.reshape(-1, blocks.shape[-1])
        return blocks.transpose(1, 0, 2).reshape(s)

    my_id = 4 * lax.axis_index("x") + 2 * lax.axis_index("y") + lax.axis_index("c")
    got = _all_gather_many("gather_early", [shard(n) for n in _EARLY] + [small_block(wl)])
    full = {n: a for n, a in wl.items() if n not in _LATE}
    for n, blocks in zip(_EARLY, got):
        full[n] = from_slots(n, blocks)
    for n, p in zip(_F32_GATHERED, _split(got[-1].reshape(N_DEV, -1), small_sizes)):
        full[n] = _from_blocks(p, *spec[n])
    late_own = [shard(n) for n in _LATE]
    l_send, l_recv, l_src, l_land, token = _push_start("gather_late_start", late_own, False, got[-1])

    def late_weights(after, names):
        first = _LATE.index(names[0])
        sl = slice(first, first + len(names))
        _, lands = _push_wait("gather_late_wait_" + names[0], l_send, l_recv, l_src[sl], l_land[sl], False,
                              after, first)
        out = {}
        for n, land, own in zip(names, lands, late_own[sl]):
            out[n] = from_slots(n, lax.dynamic_update_slice(land, own[None], (my_id, 0, 0))).astype(_MXU)
        return out

    def dest_blocks(n, a):
        d, s = spec[n]
        r, c = _local_shape(d, s)
        if n in _TRANSPOSED:
            return a.reshape(N_DEV, c, r)
        return a.reshape(N_DEV, r, c) if d == 0 else a.reshape(r, N_DEV, c).transpose(1, 0, 2)

    pushed = []

    def grads_ready(g, names):
        nat = _grads_to_natural({n: g[n] for n in names})
        blocks = [dest_blocks(n, nat[n]).astype(_MXU) for n in names]
        sends, recvs, srcs, lands, tok = _push_start("rs_" + names[0] + "_start", blocks, True, token)
        pushed.append((names, sends, recvs, srcs, lands))
        return tok

    seq = x.shape[1]
    tp = ROW0 + seq
    h0 = jnp.concatenate([jnp.zeros((PAD, D_MODEL), F32), full["meta_tokens"], x[0]], axis=0)
    tgt = jnp.concatenate([jnp.zeros((ROW0, D_MODEL), F32), loss_target[0]], axis=0)
    loss, dh0, raw = _local_step(h0, tgt, _prepare(full, tp), token, late_weights, grads_ready)
    g = _grads_to_natural(raw)
    g["meta_tokens"] = dh0[PAD:ROW0]
    grad_x = dh0[ROW0:][None]

    big = [{}, {}, {}, {}]

    def finish(group):
        names, sends, recvs, srcs, lands = group
        srcs, lands = _push_wait("rs_" + names[0] + "_wait", sends, recvs, srcs, lands, True, dh0)
        for n, src, land in zip(names, srcs, lands):
            parts = [(src, my_id)] + [(land, my_id ^ f) for f in range(1, N_DEV)]
            for kind, a in enumerate(_adam("adam_" + n, parts, wl[n], ml[n], vl[n])):
                big[kind][n] = a

    for group in pushed[:-1]:
        finish(group)
    rep_names = [n for n, _ in _REPLICATED]
    raw_key = {"dn_A_log": "alog_b", "dn_dt_bias": "dtb_b"}
    pieces = [raw[raw_key.get(n, n)] for n in rep_names] + [loss]
    pieces += [g[n].reshape(1, -1) for n in _F32_GATHERED]
    widths = [p.shape[1] for p in pieces]
    offs = [sum(widths[:k]) for k in range(len(widths))]
    cat = jnp.concatenate(pieces, axis=1)
    cols = -(-cat.shape[1] // (8 * LANE)) * LANE
    mine = jnp.pad(cat, ((0, 0), (0, 8 * cols - cat.shape[1]))).reshape(8, cols)
    everyone = _all_gather("gather_small_grads", mine, [big[1][n] for group in pushed[:-1] for n in group[0]])
    total = _sum_parts("sum_small_grads", [(everyone, d) for d in range(N_DEV)]).reshape(1, 8 * cols)
    tot = {n: total[0, o:o + wd] for n, o, wd in zip(rep_names + ["loss"] + list(_F32_GATHERED), offs, widths)}
    items = [(o, size, n in raw_key) for (n, size), o in zip(_REPLICATED, offs)]
    sm = _adam_vectors("adam_replicated", total, items, [wl[n] for n in rep_names], [ml[n] for n in rep_names],
                       [vl[n] for n in rep_names])
    sm = [dict(zip(rep_names, kind)) for kind in sm]
    mine_of = {}
    for n in _F32_GATHERED:
        d, s = spec[n]
        r, c = _local_shape(d, s)
        mine_of[n] = lax.dynamic_slice(tot[n].reshape(s), (0, my_id * c), (r, c))
    res = _adam_arrays("adam_small_sharded", [mine_of[n] for n in _F32_GATHERED], [wl[n] for n in _F32_GATHERED],
                       [ml[n] for n in _F32_GATHERED], [vl[n] for n in _F32_GATHERED])
    for kind, arrays in enumerate([[mine_of[n] for n in _F32_GATHERED]] + res):
        big[kind].update(zip(_F32_GATHERED, arrays))

    finish(pushed[-1])

    outs = [tot["loss"][0], grad_x]
    for kind in range(4):
        for n in ("meta_tokens", "attn_norm_w", "w_in", "q_a_norm_w", "w_q_b", "kv_a_norm_w", "w_kv_b", "q_norm_w",
                  "k_norm_w", "mla_out_norm_w", "dn_conv_w", "dn_A_log", "dn_dt_bias", "dn_out_norm_w", "w_out",
                  "ffn_norm_w", "w_gate", "w_up", "ffn_conv_w", "ffn_conv_b", "w_down"):
            src = big[kind] if n in big[kind] else sm[kind]
            a = src[n].T if n in _TRANSPOSED else src[n]
            outs.append(a.reshape(out_shapes[n]))
    return tuple(outs)
```

```python
import functools
import math

import jax
import jax.numpy as jnp
from jax import lax
from jax.experimental import pallas as pl
from jax.experimental.pallas import tpu as pltpu

F32 = jnp.float32
_MXU = jnp.bfloat16
_HI = lax.Precision.HIGHEST

D_MODEL = 1024
N_META = 16
PAD = 112
ROW0 = PAD + N_META
MLA_HEADS = 4
QK_NOPE = 128
QK_ROPE = 64
QK_HEAD = QK_NOPE + QK_ROPE
V_HEAD = 128
Q_LORA = 256
KV_LORA = 256
ROPE_THETA = 10000.0
DN_HEADS = 4
DN_DIM = 128
DN_WIDTH = DN_HEADS * DN_DIM
DN_CONV = 4
DN_CHUNK = 64
GDN_SUB_CHUNKS = 2
D_FF = 2816
FFN_CONV = 3
EPS = 1e-6
HP = 256
C_Z = 1536
C_QL = 2048
C_KVL = 2304
C_KPE = 2560
C_AB = 2688
IN_COLS = 2632

ADAM_LR = 0.001
ADAM_B1 = 0.9
ADAM_B2 = 0.999
ADAM_EPS = 1e-08
ADAM_WD = 0.01
ADAM_STEP = 10

N_DEV = 8
TM = 128
LANE = 128
VMEM_LIMIT = 56 * 1024 * 1024
NEG = -1e30


def _dot(a, b, dims, hp=False):
    if hp:
        return lax.dot_general(a.astype(F32), b.astype(F32), (dims, ((), ())),
                               precision=lax.Precision.HIGH if hp == "3x" else _HI, preferred_element_type=F32)
    return lax.dot_general(a.astype(_MXU), b.astype(_MXU), (dims, ((), ())),
                           preferred_element_type=F32)


def _nn(a, b, hp=False):
    return _dot(a, b, ((1,), (0,)), hp)


def _nt(a, b, hp=False):
    return _dot(a, b, ((1,), (1,)), hp)


def _tn(a, b, hp=False):
    return _dot(a, b, ((0,), (0,)), hp)


def _sigmoid(x):
    return 1.0 / (1.0 + jnp.exp(-x))


def _rms_fwd(x, w, n):
    r = lax.rsqrt(jnp.sum(x * x, axis=-1, keepdims=True) * (1.0 / n) + EPS)
    return x * r * w, r


def _rms_bwd(x, w, dy, n):
    r = lax.rsqrt(jnp.sum(x * x, axis=-1, keepdims=True) * (1.0 / n) + EPS)
    xh = x * r
    gy = dy * w
    dx = r * (gy - xh * (jnp.sum(gy * xh, axis=-1, keepdims=True) * (1.0 / n)))
    return dx, dy * xh


def _rowsum(x):
    return jnp.sum(x, axis=0, keepdims=True)


def _row_ids(i, tm):
    return i * tm + lax.broadcasted_iota(jnp.int32, (tm, 1), 0)


def _shift_down(ext, s, tm):
    if s == 0:
        return ext[8:8 + tm]
    return pltpu.roll(ext, s, 0)[8:8 + tm]


def _shift_up(ext, s, tm):
    if s == 0:
        return ext[0:tm]
    return pltpu.roll(ext, tm + 8 - s, 0)[0:tm]


def _conv_taps(x, halo_prev, width):
    tm = x.shape[0]
    ext = jnp.concatenate([halo_prev, x], axis=0)
    return [_shift_down(ext, width - 1 - j, tm) for j in range(width)]


def _conv_from_taps(taps, w):
    y = None
    for j, tap in enumerate(taps):
        t = w[j:j + 1, :] * tap
        y = t if y is None else y + t
    return y


def _conv_fwd(x, halo_prev, w, width):
    return _conv_from_taps(_conv_taps(x, halo_prev, width), w)


def _conv_bwd_w_taps(dy, taps):
    tm = dy.shape[0]
    rows = [_rowsum(dy * tap[:tm]) for tap in taps]
    rows += [jnp.zeros_like(rows[0])] * (8 - len(taps))
    return jnp.concatenate(rows, axis=0)


def _conv_bwd_x(dy, halo_next, w, width):
    tm = dy.shape[0]
    ext = jnp.concatenate([dy, halo_next], axis=0)
    dx = None
    for j in range(width):
        t = w[j:j + 1, :] * _shift_up(ext, width - 1 - j, tm)
        dx = t if dx is None else dx + t
    return dx


def _softplus(x):
    e = jnp.exp(-jnp.abs(x))
    u = 1.0 + e
    l1p = jnp.where(u == 1.0, e, jnp.log(u) * e / jnp.where(u == 1.0, 1.0, u - 1.0))
    return jnp.maximum(x, 0.0) + l1p


def _swap_halves(x):
    lane = lax.broadcasted_iota(jnp.int32, x.shape, 1)
    return jnp.where(lane < 32, pltpu.roll(x, 96, 1), jnp.where(lane < 64, pltpu.roll(x, 32, 1), 0.0))


class _In:
    def __init__(self, arr, width=None, cb=0, kind="cur"):
        self.arr, self.kind = arr, kind
        self.width = arr.shape[1] if width is None else width
        self.cb = cb


def _whole_spec(x):
    return pl.BlockSpec(x.shape, lambda i, nd=x.ndim: (0,) * nd, pipeline_mode=pl.Buffered(1))


def _tile_spec(t, tm, tp):
    r8 = tm // 8
    if t.kind == "cur":
        return pl.BlockSpec((tm, t.width), lambda i, cb=t.cb: (i, cb))
    if t.kind == "prev":
        return pl.BlockSpec((8, t.width), lambda i, cb=t.cb: (jnp.maximum(i * r8 - 1, 0), cb))
    return pl.BlockSpec((8, t.width), lambda i, cb=t.cb: (jnp.minimum((i + 1) * r8, tp // 8 - 1), cb))


def _rows(name, fn, tiled, full, outs, accs=(), tm=TM):
    tp = tiled[0].arr.shape[0]
    nt = tp // tm
    n_in = len(tiled) + len(full)
    n_out = len(outs)

    def body(*refs):
        i = pl.program_id(0)
        vals = [r[...] for r in refs[:n_in]]
        o_t, o_a = fn(i, *vals)
        for r, v in zip(refs[n_in:n_in + n_out], o_t):
            r[...] = v.astype(r.dtype)
        for r, v in zip(refs[n_in + n_out:], o_a):
            @pl.when(i == 0)
            def _():
                r[...] = v

            @pl.when(i > 0)
            def _():
                r[...] += v

    in_specs = [_tile_spec(t, tm, tp) for t in tiled]
    in_specs += [pl.BlockSpec(a.shape, lambda i, nd=a.ndim: (0,) * nd) for a in full]
    out_specs = [pl.BlockSpec((tm, w), lambda i: (i, 0)) for w, _ in outs]
    out_specs += [pl.BlockSpec((r, w), lambda i: (0, 0)) for r, w in accs]
    out_shape = [jax.ShapeDtypeStruct((tp, w), dt) for w, dt in outs]
    out_shape += [jax.ShapeDtypeStruct((r, w), F32) for r, w in accs]
    res = pl.pallas_call(
        body, name=name, grid=(nt,), in_specs=in_specs, out_specs=out_specs, out_shape=out_shape,
        compiler_params=pltpu.CompilerParams(dimension_semantics=("arbitrary",), vmem_limit_bytes=VMEM_LIMIT),
    )(*[t.arr for t in tiled], *full)
    return res


def _pick(n, cap, mult):
    best = None
    for d in range(mult, min(n, cap) + 1, mult):
        if n % d == 0:
            best = d
    assert best is not None, (n, cap, mult)
    return best


_ANY_SPEC = pl.BlockSpec(memory_space=pl.ANY)


def _mm(name, a, b, mode, out_dtype=F32, resid=None, after=None):
    if mode == "tn":
        m, k = a.shape
        n = b.shape[1]
        tk = _pick(k, 512, 128)
        tn = _pick(n, 1408, 128)

        def body_tn(a_ref, b_ref, o_ref):
            o_ref[...] = _tn(a_ref[...], b_ref[...]).astype(o_ref.dtype)

        return pl.pallas_call(
            body_tn, name=name, grid=(n // tn, k // tk),
            in_specs=[pl.BlockSpec((m, tk), lambda j, p: (0, p)),
                      pl.BlockSpec((m, tn), lambda j, p: (0, j))],
            out_specs=pl.BlockSpec((tk, tn), lambda j, p: (p, j)),
            out_shape=jax.ShapeDtypeStruct((k, n), out_dtype),
            compiler_params=pltpu.CompilerParams(
                dimension_semantics=("parallel", "parallel"), vmem_limit_bytes=VMEM_LIMIT),
        )(a, b)

    m, k = a.shape
    n = b.shape[1] if mode == "nn" else b.shape[0]
    tn = _pick(n, 1408, 128)
    tm = _pick(m, 1152, 16)
    dotf = _nn if mode == "nn" else _nt

    def body(*refs):
        a_ref, b_ref, o_ref = refs[0], refs[1], refs[-1]
        acc = dotf(a_ref[...], b_ref[...])
        if resid is not None:
            acc = refs[2][...] + acc
        o_ref[...] = acc.astype(o_ref.dtype)

    b_spec = (pl.BlockSpec((k, tn), lambda j, i: (0, j)) if mode == "nn"
              else pl.BlockSpec((tn, k), lambda j, i: (j, 0)))
    in_specs = [pl.BlockSpec((tm, k), lambda j, i: (i, 0)), b_spec]
    args = [a, b]
    if resid is not None:
        in_specs.append(pl.BlockSpec((tm, tn), lambda j, i: (i, j)))
        args.append(resid)
    if after is not None:
        in_specs.append(_ANY_SPEC)
        args.append(after)
    return pl.pallas_call(
        body, name=name, grid=(n // tn, m // tm), in_specs=in_specs,
        out_specs=pl.BlockSpec((tm, tn), lambda j, i: (i, j)),
        out_shape=jax.ShapeDtypeStruct((m, n), out_dtype),
        compiler_params=pltpu.CompilerParams(
            dimension_semantics=("parallel", "parallel"), vmem_limit_bytes=VMEM_LIMIT),
    )(*args)


def _mm_tn2(name, a1, a2, b, out_dtype=F32):
    m, k = a1.shape
    n = b.shape[1]
    tk = _pick(k, 512, 128)

    def body(a1_ref, a2_ref, b_ref, o1_ref, o2_ref):
        bb = b_ref[...]
        o1_ref[...] = _tn(a1_ref[...], bb).astype(o1_ref.dtype)
        o2_ref[...] = _tn(a2_ref[...], bb).astype(o2_ref.dtype)

    a_spec = pl.BlockSpec((m, tk), lambda p: (0, p))
    o_spec = pl.BlockSpec((tk, n), lambda p: (p, 0))
    return pl.pallas_call(
        body, name=name, grid=(k // tk,),
        in_specs=[a_spec, a_spec, pl.BlockSpec((m, n), lambda p: (0, 0))],
        out_specs=[o_spec, o_spec], out_shape=[jax.ShapeDtypeStruct((k, n), out_dtype)] * 2,
        compiler_params=pltpu.CompilerParams(dimension_semantics=("parallel",), vmem_limit_bytes=VMEM_LIMIT),
    )(a1, a2, b)


def _mm_tn_pair(name, a1, b1, a2, b2):
    def body(a1_ref, b1_ref, a2_ref, b2_ref, o1_ref, o2_ref):
        o1_ref[...] = _tn(a1_ref[...], b1_ref[...])
        o2_ref[...] = _tn(a2_ref[...], b2_ref[...])

    return pl.pallas_call(
        body, name=name,
        out_shape=[jax.ShapeDtypeStruct((a1.shape[1], b1.shape[1]), F32),
                   jax.ShapeDtypeStruct((a2.shape[1], b2.shape[1]), F32)],
        compiler_params=pltpu.CompilerParams(vmem_limit_bytes=VMEM_LIMIT),
    )(a1, b1, a2, b2)


def _norm_mm(name, x, norm_w, b, mode="nt", x_cb=0, after=None):
    m = x.shape[0]
    k = norm_w.shape[1]
    n = b.shape[0] if mode == "nt" else b.shape[1]
    tn = _pick(n, 1408, 128)
    tm = _pick(m, 1152, 16)
    dotf = _nt if mode == "nt" else _nn
    extra = [] if after is None else [after]

    def body(x_ref, w_ref, b_ref, *rest):
        o_ref, u_ref = rest[-2:]

        @pl.when(pl.program_id(1) == 0)
        def _():
            u_ref[...] = _rms_fwd(x_ref[...], w_ref[...], k)[0].astype(u_ref.dtype)

        o_ref[...] = dotf(u_ref[...], b_ref[...])

    b_spec = (pl.BlockSpec((tn, k), lambda i, j: (j, 0)) if mode == "nt"
              else pl.BlockSpec((k, tn), lambda i, j: (0, j)))
    return pl.pallas_call(
        body, name=name, grid=(m // tm, n // tn),
        in_specs=[pl.BlockSpec((tm, k), lambda i, j: (i, x_cb)), pl.BlockSpec((1, k), lambda i, j: (0, 0)),
                  b_spec] + [_ANY_SPEC] * len(extra),
        out_specs=[pl.BlockSpec((tm, tn), lambda i, j: (i, j)), pl.BlockSpec((tm, k), lambda i, j: (i, 0))],
        out_shape=[jax.ShapeDtypeStruct((m, n), F32), jax.ShapeDtypeStruct((m, k), _MXU)],
        compiler_params=pltpu.CompilerParams(
            dimension_semantics=("arbitrary", "arbitrary"), vmem_limit_bytes=VMEM_LIMIT),
    )(x, norm_w, b, *extra)


def _pro_mm(name, fn, tiled, full, k, b, resid):
    m = resid.shape[0]
    n = b.shape[1]
    tm = _pick(m, 576, 16)
    n_in = len(tiled) + len(full)

    def body(*refs):
        i = pl.program_id(0)
        u = fn(i, *[r[...] for r in refs[:n_in]]).astype(_MXU)
        b_ref, r_ref, o_ref, u_ref = refs[n_in:]
        u_ref[...] = u
        o_ref[...] = r_ref[...] + _nn(u, b_ref[...])

    row = lambda w: pl.BlockSpec((tm, w), lambda i: (i, 0))
    in_specs = [_tile_spec(t, tm, m) for t in tiled]
    in_specs += [_whole_spec(x) for x in full] + [_whole_spec(b), row(n)]
    return pl.pallas_call(
        body, name=name, grid=(m // tm,), in_specs=in_specs, out_specs=[row(n), row(k)],
        out_shape=[jax.ShapeDtypeStruct((m, n), F32), jax.ShapeDtypeStruct((m, k), _MXU)],
        compiler_params=pltpu.CompilerParams(dimension_semantics=("parallel",), vmem_limit_bytes=VMEM_LIMIT),
    )(*[t.arr for t in tiled], *full, b, resid)


def _ffn_in(h2, norm_w, w_gate_t, w_up_t, conv_w8, conv_b):
    m, k = h2.shape
    n = w_gate_t.shape[0]
    tm = _pick(m, 288, 16)

    def body(x_ref, xp_ref, nw_ref, wg_ref, wu_ref, cw_ref, cb_ref, hn_ref, gp_ref, up_ref, act_ref):
        i = pl.program_id(0)
        nw = nw_ref[...]
        hn = _rms_fwd(x_ref[...], nw, k)[0].astype(_MXU)
        hn_prev = _rms_fwd(xp_ref[...], nw, k)[0].astype(_MXU)
        wg = wg_ref[...]
        gp = _nt(hn, wg)
        gp_prev = jnp.where(i > 0, _nt(hn_prev, wg), 0.0)
        up = _nt(hn, wu_ref[...])
        gate = _conv_fwd(gp, gp_prev, cw_ref[...], FFN_CONV) + cb_ref[...]
        hn_ref[...] = hn
        gp_ref[...] = gp
        up_ref[...] = up
        act_ref[...] = (_silu_parts(gate)[0] * up).astype(act_ref.dtype)

    row = lambda w: pl.BlockSpec((tm, w), lambda i: (i, 0))
    r8 = tm // 8
    return pl.pallas_call(
        body, name="ffn_in", grid=(m // tm,),
        in_specs=[row(k), pl.BlockSpec((8, k), lambda i: (jnp.maximum(i * r8 - 1, 0), 0)), _whole_spec(norm_w),
                  _whole_spec(w_gate_t), _whole_spec(w_up_t), _whole_spec(conv_w8), _whole_spec(conv_b)],
        out_specs=[row(k), row(n), row(n), row(n)],
        out_shape=[jax.ShapeDtypeStruct((m, k), _MXU), jax.ShapeDtypeStruct((m, n), F32),
                   jax.ShapeDtypeStruct((m, n), F32), jax.ShapeDtypeStruct((m, n), _MXU)],
        compiler_params=pltpu.CompilerParams(dimension_semantics=("parallel",), vmem_limit_bytes=VMEM_LIMIT),
    )(h2, h2, norm_w, w_gate_t, w_up_t, conv_w8, conv_b)


def _mm_rows(name, a, b, mode, fn, tiled, full, outs, accs=(), tm_cap=576):
    a_list = list(a) if isinstance(a, (list, tuple)) else [a]
    b_list = list(b) if isinstance(b, (list, tuple)) else [b]
    na = len(a_list)
    m = a_list[0].shape[0]
    tm = _pick(m, tm_cap, 16)
    dotf = _nn if mode == "nn" else _nt
    n_in = len(tiled) + len(full)
    n_out = len(outs)
    first = 2 * na

    def body(*refs):
        i = pl.program_id(0)
        vals = [r[...] for r in refs[first:first + n_in]]
        acc = dotf(refs[0][...], refs[na][...])
        for p in range(1, na):
            acc = acc + dotf(refs[p][...], refs[na + p][...])
        o_t, o_a = fn(i, acc, *vals)
        for r, v in zip(refs[first + n_in:first + n_in + n_out], o_t):
            r[...] = v.astype(r.dtype)
        for r, v in zip(refs[first + n_in + n_out:], o_a):
            @pl.when(i == 0)
            def _():
                r[...] = v

            @pl.when(i > 0)
            def _():
                r[...] += v

    whole = lambda x: pl.BlockSpec(x.shape, lambda i, nd=x.ndim: (0,) * nd)
    in_specs = [pl.BlockSpec((tm, x.shape[1]), lambda i: (i, 0)) for x in a_list] + [_whole_spec(x) for x in b_list]
    in_specs += [_tile_spec(t, tm, m) for t in tiled]
    in_specs += [whole(x) for x in full]
    out_specs = [pl.BlockSpec((tm, w), lambda i: (i, 0)) for w, _ in outs]
    out_specs += [pl.BlockSpec((r, w), lambda i: (0, 0)) for r, w in accs]
    out_shape = [jax.ShapeDtypeStruct((m, w), dt) for w, dt in outs]
    out_shape += [jax.ShapeDtypeStruct((r, w), F32) for r, w in accs]
    return pl.pallas_call(
        body, name=name, grid=(m // tm,), in_specs=in_specs, out_specs=out_specs, out_shape=out_shape,
        compiler_params=pltpu.CompilerParams(dimension_semantics=("arbitrary",), vmem_limit_bytes=VMEM_LIMIT),
    )(*a_list, *b_list, *[t.arr for t in tiled], *full)


ATTN_Q_TILES = 4


def _attn_probs(q, k, row0):
    tq, tp = q.shape[0], k.shape[0]
    s = _nt(q, k) * (1.0 / math.sqrt(QK_HEAD))
    row = row0 + lax.broadcasted_iota(jnp.int32, (tq, tp), 0)
    col = lax.broadcasted_iota(jnp.int32, (tq, tp), 1)
    ok = (col <= row) & (col >= PAD)
    s = jnp.where(ok, s, NEG)
    m = jnp.max(s, axis=-1, keepdims=True)
    e = jnp.exp(s - m)
    return e * (1.0 / jnp.sum(e, axis=-1, keepdims=True))


def _attn_fwd(q, k, v):
    tp = q.shape[0]
    tq = tp // ATTN_Q_TILES

    def body(q_ref, k_ref, v_ref, o_ref):
        for i in range(ATTN_Q_TILES):
            rows = slice(i * tq, (i + 1) * tq)
            keys = slice(0, (i + 1) * tq)
            p = _attn_probs(q_ref[rows, :], k_ref[keys, :], i * tq)
            o_ref[rows, :] = _nn(p, v_ref[keys, :])

    return pl.pallas_call(
        body, name="attn_fwd", grid=(MLA_HEADS,),
        in_specs=[pl.BlockSpec((tp, HP), lambda h: (0, h)),
                  pl.BlockSpec((tp, HP), lambda h: (0, h)),
                  pl.BlockSpec((tp, V_HEAD), lambda h: (0, h))],
        out_specs=pl.BlockSpec((tp, V_HEAD), lambda h: (0, h)),
        out_shape=jax.ShapeDtypeStruct((tp, MLA_HEADS * V_HEAD), F32),
        compiler_params=pltpu.CompilerParams(dimension_semantics=("parallel",), vmem_limit_bytes=VMEM_LIMIT),
    )(q, k, v)


def _attn_bwd(q, k, v, do):
    tp = q.shape[0]
    tq = tp // ATTN_Q_TILES

    def body(q_ref, k_ref, v_ref, do_ref, dq_ref, dk_ref, dv_ref):
        for i in reversed(range(ATTN_Q_TILES)):
            rows = slice(i * tq, (i + 1) * tq)
            keys = slice(0, (i + 1) * tq)
            qb = q_ref[rows, :]
            kk = k_ref[keys, :]
            dob = do_ref[rows, :]
            p = _attn_probs(qb, kk, i * tq)
            dp = _nt(dob, v_ref[keys, :])
            delta = jnp.sum(p * dp, axis=-1, keepdims=True)
            ds = p * (dp - delta) * (1.0 / math.sqrt(QK_HEAD))
            dq_ref[rows, :] = _nn(ds, kk)
            if i == ATTN_Q_TILES - 1:
                dk_ref[...] = _tn(ds, qb)
                dv_ref[...] = _tn(p, dob)
            else:
                dk_ref[keys, :] += _tn(ds, qb)
                dv_ref[keys, :] += _tn(p, dob)

    full = lambda w: pl.BlockSpec((tp, w), lambda h: (0, h))
    return pl.pallas_call(
        body, name="attn_bwd", grid=(MLA_HEADS,),
        in_specs=[full(HP), full(HP), full(V_HEAD), full(V_HEAD)],
        out_specs=[full(HP), full(HP), full(V_HEAD)],
        out_shape=[jax.ShapeDtypeStruct((tp, MLA_HEADS * HP), F32),
                   jax.ShapeDtypeStruct((tp, MLA_HEADS * HP), F32),
                   jax.ShapeDtypeStruct((tp, MLA_HEADS * V_HEAD), F32)],
        compiler_params=pltpu.CompilerParams(dimension_semantics=("parallel",), vmem_limit_bytes=VMEM_LIMIT),
    )(q, k, v, do)


def _gdn_consts():
    c = DN_CHUNK
    r = lax.broadcasted_iota(jnp.int32, (c, c), 0)
    cc = lax.broadcasted_iota(jnp.int32, (c, c), 1)
    incl = r >= cc
    strict = r > cc
    return incl, strict


def _cumsum_rows(x, reverse=False):
    c = x.shape[0]
    row = lax.broadcasted_iota(jnp.int32, x.shape, 0)
    s = 1
    while s < c:
        if reverse:
            x = x + jnp.where(row < c - s, pltpu.roll(x, c - s, 0), 0.0)
        else:
            x = x + jnp.where(row >= s, pltpu.roll(x, s, 0), 0.0)
        s *= 2
    return x


def _each(fn, *lists):
    return [fn(*a) for a in zip(*lists)]


def _interleave(chains):
    chains = list(chains)
    while chains:
        for ch in list(chains):
            try:
                next(ch)
            except StopIteration:
                chains.remove(ch)


def _gdn_chunk_common(q_ref, k_ref, v_ref, g_ref, b_ref):
    c = DN_CHUNK
    incl, strict = _gdn_consts()
    sls = [(slice(c * sub, c * (sub + 1)), slice(DN_DIM * h, DN_DIM * (h + 1)))
           for sub in range(GDN_SUB_CHUNKS) for h in range(DN_HEADS)]
    q = [q_ref[sl] * (1.0 / math.sqrt(DN_DIM)) for sl in sls]
    k = [k_ref[sl] for sl in sls]
    v = [v_ref[sl] for sl in sls]
    g = [g_ref[sl] for sl in sls]
    beta = [b_ref[sl] for sl in sls]
    gc = [_cumsum_rows(x) for x in g]
    grow = [x.T[:c, :] for x in gc]
    kb = _each(jnp.multiply, k, beta)
    kk = _each(_nt, kb, k)
    qk = _each(_nt, q, k)
    gam = [jnp.exp(x) for x in gc]
    g_last = [_rowsum(x) for x in g]
    dm = [jnp.exp(jnp.where(incl, x[:, :c] - y, NEG)) for x, y in zip(gc, grow)]
    vb = _each(jnp.multiply, v, beta)
    kbg = _each(jnp.multiply, kb, gam)
    ek = [jnp.exp(x - y) for x, y in zip(g_last, gc)]
    kd = _each(jnp.multiply, k, ek)
    return dict(q=q, k=k, v=v, beta=beta, gc=gc, gam=gam, g_last=g_last, dm=dm, kb=kb, vb=vb,
                kbg=kbg, kk=kk, ek=ek, kd=kd, qk=qk, incl=incl, strict=strict, sls=sls)


def _gdn_fwd(proj, conv_w8, alog, dtb):
    tp = proj.shape[0]
    c = DN_CHUNK
    nch = tp // c
    blk = GDN_SUB_CHUNKS * c

    def body(x_ref, xp_ref, ab_ref, w8_ref, alog_ref, dtb_ref,
             o_ref, s_ref, t_ref, q_ref, k_ref, v_ref, g_ref, b_ref, s_scr):
        @pl.when(pl.program_id(0) == 0)
        def _():
            s_scr[...] = jnp.zeros_like(s_scr)

        staged, _ = _f_gdn_prep(pl.program_id(0), x_ref[...], xp_ref[...], ab_ref[...], w8_ref[...],
                                alog_ref[...], dtb_ref[...])
        for ref, val in zip((q_ref, k_ref, v_ref, g_ref, b_ref), staged):
            ref[...] = val
        eye = (lax.broadcasted_iota(jnp.int32, (c, c), 0) == lax.broadcasted_iota(jnp.int32, (c, c), 1)).astype(F32)
        x = _gdn_chunk_common(q_ref, k_ref, v_ref, g_ref, b_ref)
        heads = range(DN_HEADS)
        bp = [-jnp.where(x["strict"], kk * dm, 0.0) for kk, dm in zip(x["kk"], x["dm"])]
        t = [eye + b for b in bp]
        for _ in range(5):
            bp = [_nn(b, b, hp="3x") for b in bp]
            t = [tt + _nn(tt, b, hp="3x") for tt, b in zip(t, bp)]
        u = _each(_nn, t, x["vb"])
        w = _each(_nn, t, x["kbg"])
        qg = _each(jnp.multiply, x["q"], x["gam"])
        mqk = _each(jnp.multiply, x["qk"], x["dm"])
        s = [s_scr[h] for h in heads]
        for sub in range(GDN_SUB_CHUNKS):
            e = [DN_HEADS * sub + h for h in heads]
            v_new = [u[i] - _nn(w[i], s[h]) for h, i in zip(heads, e)]
            o = [_nn(qg[i], s[h]) + _nn(mqk[i], v_new[h]) for h, i in zip(heads, e)]
            s_new = [s[h] * jnp.exp(x["g_last"][i]) + _tn(x["kd"][i], v_new[h]) for h, i in zip(heads, e)]
            for h, i in zip(heads, e):
                s_ref[h, sub] = s[h]
                t_ref[h, sub] = t[i]
                o_ref[x["sls"][i]] = o[h]
            s = s_new
        for h in heads:
            s_scr[h] = s[h]

    sub = GDN_SUB_CHUNKS
    rb = lambda n: (n, 0)
    rows = pl.BlockSpec((blk, DN_WIDTH), rb)
    whole = lambda a: pl.BlockSpec(a.shape, lambda n: (0, 0))
    return pl.pallas_call(
        body, name="gdn_fwd", grid=(nch // sub,),
        in_specs=[pl.BlockSpec((blk, 3 * DN_WIDTH), rb),
                  pl.BlockSpec((8, 3 * DN_WIDTH), lambda n: (jnp.maximum(n * (blk // 8) - 1, 0), 0)),
                  pl.BlockSpec((blk, LANE), lambda n: (n, C_AB // LANE)),
                  whole(conv_w8), whole(alog), whole(dtb)],
        out_specs=[rows,
                   pl.BlockSpec((DN_HEADS, sub, DN_DIM, DN_DIM), lambda n: (0, n, 0, 0)),
                   pl.BlockSpec((DN_HEADS, sub, c, c), lambda n: (0, n, 0, 0))] + [rows] * 5,
        out_shape=[jax.ShapeDtypeStruct((tp, DN_WIDTH), F32),
                   jax.ShapeDtypeStruct((DN_HEADS, nch, DN_DIM, DN_DIM), F32),
                   jax.ShapeDtypeStruct((DN_HEADS, nch, c, c), F32)] + [jax.ShapeDtypeStruct((tp, DN_WIDTH), F32)] * 5,
        scratch_shapes=[pltpu.VMEM((DN_HEADS, DN_DIM, DN_DIM), F32)],
        compiler_params=pltpu.CompilerParams(dimension_semantics=("arbitrary",), vmem_limit_bytes=VMEM_LIMIT),
    )(proj, proj, proj, conv_w8, alog, dtb)


def _gdn_bwd(q, k, v, g, beta, s_all, t_all, do, proj, conv_w8, alog, dtb, after):
    tp = q.shape[0]
    c = DN_CHUNK
    nch = tp // c
    nblk = nch // GDN_SUB_CHUNKS
    blk = GDN_SUB_CHUNKS * c

    def body(q_ref, k_ref, v_ref, g_ref, b_ref, s_ref, t_ref, do_ref, x_ref, xp_ref, xn_ref, ab_ref,
             w8_ref, alog_ref, dtb_ref, _after_ref, dqkv_ref, dab_ref, dcw_ref, dalog_ref, ddtb_ref,
             ds_scr, dq_ref, dk_ref, dv_ref, dg_ref, db_ref, nxt_scr):
        step = pl.program_id(0)

        @pl.when(step == 0)
        def _():
            ds_scr[...] = jnp.zeros_like(ds_scr)
            nxt_scr[...] = jnp.zeros_like(nxt_scr)

        xs = _gdn_chunk_common(q_ref, k_ref, v_ref, g_ref, b_ref)

        ds_state = [ds_scr[h] for h in range(DN_HEADS)]

        def chain(sub, h):
            e = DN_HEADS * sub + h
            x = {key: (val[e] if isinstance(val, list) else val) for key, val in xs.items()}
            sl = x["sls"]
            qs, kx, vx, beta_, gam, dm = x["q"], x["k"], x["v"], x["beta"], x["gam"], x["dm"]
            kb, vb, kbg, kd, ek = x["kb"], x["vb"], x["kbg"], x["kd"], x["ek"]
            t = t_ref[h, sub]
            s = s_ref[h, sub]
            dsn = ds_state[h]
            dob = do_ref[sl]
            eg_last = jnp.exp(x["g_last"])
            u = _nn(t, vb)
            w = _nn(t, kbg)
            mqk = x["qk"] * dm
            qd = qs * gam
            dqd = _nt(dob, s)
            dkd_pre = _nn(kd, dsn)
            yield
            v_new = u - _nn(w, s)
            dv_new = _tn(mqk, dob) + dkd_pre
            dq = dqd * gam
            dgam = jnp.sum(dqd * qs, axis=1, keepdims=True)
            yield
            ds_state[h] = _tn(qd, dob) + eg_last * dsn - _tn(w, dv_new)
            dmm = jnp.where(x["incl"], _nt(dob, v_new), 0.0)
            dkd = _nt(v_new, dsn)
            dw = -_nt(dv_new, s)
            dvb = _tn(t, dv_new)
            dt = _nt(dv_new, vb)
            yield
            dqk = dmm * dm
            e_mat = dmm * mqk
            dq = dq + _nn(dqk, kx)
            dk = _tn(dqk, qs) + dkd * ek
            e1 = jnp.sum(dkd * kd, axis=1, keepdims=True)
            dgc = -e1
            dg_last = jnp.sum(e1) + eg_last * jnp.sum(s * dsn)
            dt = dt + _nt(dw, kbg)
            dkbg = _tn(t, dw)
            yield
            tdt = _tn(t, dt, hp="3x")
            yield
            da = jnp.where(x["strict"], -_nt(tdt, t, hp="3x"), 0.0)
            yield
            dkk = da * dm
            e_mat = e_mat + da * x["kk"] * dm
            dkb = _nn(dkk, kx) + dkbg * gam
            dk = dk + _tn(dkk, kb)
            dgam = dgam + jnp.sum(dkbg * kb, axis=1, keepdims=True)
            yield
            dk = dk + dkb * beta_
            dbeta = jnp.sum(dkb * kx, axis=1, keepdims=True) + jnp.sum(dvb * vx, axis=1, keepdims=True)
            dv = dvb * beta_
            dgc = dgc + jnp.sum(e_mat, axis=1, keepdims=True) + dgam * gam
            dgc = dgc - jnp.sum(e_mat.T, axis=1, keepdims=True)
            yield
            dg = _cumsum_rows(dgc, reverse=True) + dg_last
            yield
            dq_ref[sl] = dq * (1.0 / math.sqrt(DN_DIM))
            dk_ref[sl] = dk
            dv_ref[sl] = dv
            dg_ref[sl] = dg
            db_ref[sl] = jnp.broadcast_to(dbeta, (c, LANE))

        chains = []
        for sub in reversed(range(GDN_SUB_CHUNKS)):
            new = [chain(sub, h) for h in range(DN_HEADS)]
            for _ in range(3):
                for ch in new:
                    next(ch)
            chains += new
        _interleave(chains)
        for h in range(DN_HEADS):
            ds_scr[h] = ds_state[h]

        dq, dk, dv = dq_ref[...], dk_ref[...], dv_ref[...]
        outs, accs = _f_gdn_prep_bwd(
            nblk - 1 - step, x_ref[...], xp_ref[...], xn_ref[...], ab_ref[...], dq, nxt_scr[0], dk, nxt_scr[1],
            dv, nxt_scr[2], dg_ref[...], db_ref[...], w8_ref[...], alog_ref[...], dtb_ref[...], nt=nblk)
        nxt_scr[0] = dq[:8]
        nxt_scr[1] = dk[:8]
        nxt_scr[2] = dv[:8]
        dqkv_ref[...] = outs[0].astype(dqkv_ref.dtype)
        dab_ref[...] = outs[1].astype(dab_ref.dtype)
        for ref, val in zip((dcw_ref, dalog_ref, ddtb_ref), accs):
            @pl.when(step == 0)
            def _():
                ref[...] = val

            @pl.when(step > 0)
            def _():
                ref[...] += val

    sub = GDN_SUB_CHUNKS
    r8 = blk // 8
    rb = lambda n: (nblk - 1 - n, 0)
    hs = lambda n: (0, nblk - 1 - n, 0, 0)
    rows = pl.BlockSpec((blk, DN_WIDTH), rb)
    whole = lambda a: pl.BlockSpec(a.shape, lambda n: (0,) * a.ndim)
    wide = 3 * DN_WIDTH
    return pl.pallas_call(
        body, name="gdn_bwd", grid=(nblk,),
        in_specs=[rows] * 5
        + [pl.BlockSpec((DN_HEADS, sub, DN_DIM, DN_DIM), hs), pl.BlockSpec((DN_HEADS, sub, c, c), hs), rows,
           pl.BlockSpec((blk, wide), rb),
           pl.BlockSpec((8, wide), lambda n: (jnp.maximum((nblk - 1 - n) * r8 - 1, 0), 0)),
           pl.BlockSpec((8, wide), lambda n: (jnp.minimum((nblk - n) * r8, tp // 8 - 1), 0)),
           pl.BlockSpec((blk, LANE), lambda n: (nblk - 1 - n, C_AB // LANE)),
           whole(conv_w8), whole(alog), whole(dtb), _ANY_SPEC],
        out_specs=[pl.BlockSpec((blk, wide), rb), pl.BlockSpec((blk, LANE), rb),
                   whole(conv_w8), whole(alog), whole(dtb)],
        out_shape=[jax.ShapeDtypeStruct((tp, wide), _MXU), jax.ShapeDtypeStruct((tp, LANE), _MXU),
                   jax.ShapeDtypeStruct(conv_w8.shape, F32), jax.ShapeDtypeStruct(alog.shape, F32),
                   jax.ShapeDtypeStruct(dtb.shape, F32)],
        scratch_shapes=[pltpu.VMEM((DN_HEADS, DN_DIM, DN_DIM), F32)] + [pltpu.VMEM((blk, DN_WIDTH), F32)] * 5
        + [pltpu.VMEM((3, 8, DN_WIDTH), F32)],
        compiler_params=pltpu.CompilerParams(dimension_semantics=("arbitrary",), vmem_limit_bytes=VMEM_LIMIT),
    )(q, k, v, g, beta, s_all, t_all, do, proj, proj, proj, proj, conv_w8, alog, dtb, after)


def _silu_parts(x):
    s = _sigmoid(x)
    return x * s, s * (1.0 + x * (1.0 - s))


def _f_rms_bwd_add(i, x, dy, dres, w, *, mask_pad):
    dx, dwr = _rms_bwd(x, w, dy, x.shape[1])
    out = dres + dx
    if mask_pad:
        out = jnp.where(_row_ids(i, x.shape[0]) >= PAD, out, 0.0)
    return (out,), (_rowsum(dwr),)


def _rope(x, cos, sin_s):
    return x * cos + _swap_halves(x) * sin_s


def _rope_t(dy, cos, sin_s):
    return dy * cos + _swap_halves(dy * sin_s)


def _f_mla_qk(i, qf, kvf, kpe, cos, sin_s, qw, kw):
    qs, ks, vs = [], [], []
    for h in range(MLA_HEADS):
        qn, _ = _rms_fwd(qf[:, HP * h:HP * (h + 1)], qw, QK_HEAD)
        qs += [qn[:, :QK_NOPE], _rope(qn[:, QK_NOPE:], cos, sin_s)]
        kh = jnp.concatenate([kvf[:, HP * h:HP * h + QK_NOPE], kpe], axis=1)
        kn, _ = _rms_fwd(kh, kw, QK_HEAD)
        ks += [kn[:, :QK_NOPE], _rope(kn[:, QK_NOPE:], cos, sin_s)]
        vs.append(kvf[:, HP * h + QK_NOPE:HP * (h + 1)])
    return (jnp.concatenate(qs, axis=1), jnp.concatenate(ks, axis=1), jnp.concatenate(vs, axis=1)), ()


def _f_mla_front(i, ql, kvl, kpe, cos, sin_s, qaw, kvaw, wq_t, wkv, qw, kw):
    qn = _rms_fwd(ql, qaw, Q_LORA)[0].astype(_MXU)
    kvn = _rms_fwd(kvl, kvaw, KV_LORA)[0].astype(_MXU)
    qf = _nt(qn, wq_t)
    kvf = _nn(kvn, wkv)
    (q, k, v), _ = _f_mla_qk(i, qf, kvf, kpe, cos, sin_s, qw, kw)
    return (qn, kvn, qf, kvf, q, k, v), ()


def _f_mla_back(i, qf, kvf, kpe, cos, sin_s, dq, dk, dv, ql, kvl, qaw, kvaw, wq_t, wkv, qw, kw):
    (dqf, dkvf, dkpe), (dqw, dkw) = _f_mla_qk_bwd(i, qf, kvf, kpe, cos, sin_s, dq, dk, dv, qw, kw)
    dqf = dqf.astype(_MXU)
    dkvf = dkvf.astype(_MXU)
    dql, dqaw = _rms_bwd(ql, qaw, _nn(dqf, wq_t), Q_LORA)
    dkvl, dkvaw = _rms_bwd(kvl, kvaw, _nt(dkvf, wkv), KV_LORA)
    return (dqf, dkvf, dkpe, dql, dkvl), (dqw, dkw, _rowsum(dqaw), _rowsum(dkvaw))


def _f_mla_qk_bwd(i, qf, kvf, kpe, cos, sin_s, dq, dk, dv, qw, kw):
    dqf, dkvf = [], []
    dkpe = None
    dqw = None
    dkw = None
    for h in range(MLA_HEADS):
        dqh = dq[:, HP * h:HP * (h + 1)]
        dqn = jnp.concatenate([dqh[:, :QK_NOPE], _rope_t(dqh[:, QK_NOPE:], cos, sin_s)], axis=1)
        dx, dwr = _rms_bwd(qf[:, HP * h:HP * (h + 1)], qw, dqn, QK_HEAD)
        dqf.append(dx)
        dqw = _rowsum(dwr) if dqw is None else dqw + _rowsum(dwr)
        dkh = dk[:, HP * h:HP * (h + 1)]
        dkn = jnp.concatenate([dkh[:, :QK_NOPE], _rope_t(dkh[:, QK_NOPE:], cos, sin_s)], axis=1)
        kh = jnp.concatenate([kvf[:, HP * h:HP * h + QK_NOPE], kpe], axis=1)
        dx, dwr = _rms_bwd(kh, kw, dkn, QK_HEAD)
        dkvf += [dx[:, :QK_NOPE], dv[:, V_HEAD * h:V_HEAD * (h + 1)]]
        dkpe = dx[:, QK_NOPE:] if dkpe is None else dkpe + dx[:, QK_NOPE:]
        dkw = _rowsum(dwr) if dkw is None else dkw + _rowsum(dwr)
    return (jnp.concatenate(dqf, axis=1), jnp.concatenate(dkvf, axis=1), dkpe), (dqw, dkw)


def _gdn_act(i, x, halo, w8):
    halo = jnp.where(i > 0, halo, 0.0)
    c = _conv_fwd(x, halo, w8, DN_CONV)
    act, dact = _silu_parts(c)
    return act, dact


def _spread_heads(ab):
    tm = ab.shape[0]
    return jnp.concatenate([jnp.broadcast_to(ab[:, h:h + 1], (tm, DN_DIM)) for h in range(2 * DN_HEADS)], axis=1)


def _gather_heads(x):
    tm = x.shape[0]
    lane = lax.broadcasted_iota(jnp.int32, (tm, LANE), 1)
    out = jnp.zeros((tm, LANE), F32)
    for h in range(2 * DN_HEADS):
        out = out + jnp.where(lane == h, x[:, DN_DIM * h:DN_DIM * h + 1], 0.0)
    return out


def _f_gdn_prep(i, x, halo, ab, w8, alog, dtb):
    tm = x.shape[0]
    act, _ = _gdn_act(i, x, halo, w8)
    outs = []
    for part in range(2):
        for h in range(DN_HEADS):
            t = act[:, DN_WIDTH * part + DN_DIM * h:DN_WIDTH * part + DN_DIM * (h + 1)]
            outs.append(t * lax.rsqrt(jnp.sum(t * t, axis=-1, keepdims=True) + EPS))
    q = jnp.concatenate(outs[:DN_HEADS], axis=1)
    k = jnp.concatenate(outs[DN_HEADS:], axis=1)
    v = act[:, 2 * DN_WIDTH:]
    abb = _spread_heads(ab)
    valid = _row_ids(i, tm) >= PAD
    g = jnp.where(valid, -jnp.exp(alog) * _softplus(abb[:, :DN_WIDTH] + dtb), 0.0)
    beta = jnp.where(valid, _sigmoid(abb[:, DN_WIDTH:]), 0.0)
    return (q, k, v, g, beta), ()


def _f_gdn_prep_bwd(i, x, x_prev, x_next, ab, dq, dq_next, dk, dk_next, dv, dv_next, dg, dbeta,
                    w8, alog, dtb, *, nt):
    tm = x.shape[0]
    x_prev = jnp.where(i > 0, x_prev, 0.0)
    more = i < nt - 1
    ext = lambda t, t_next: jnp.concatenate([t, jnp.where(more, t_next, 0.0)], axis=0)
    taps = _conv_taps(jnp.concatenate([x, x_next], axis=0), x_prev, DN_CONV)
    c = _conv_from_taps(taps, w8)
    act, dact = _silu_parts(c)
    douts = []
    for part, dd in enumerate((ext(dq, dq_next), ext(dk, dk_next))):
        for h in range(DN_HEADS):
            t = act[:, DN_WIDTH * part + DN_DIM * h:DN_WIDTH * part + DN_DIM * (h + 1)]
            r = lax.rsqrt(jnp.sum(t * t, axis=-1, keepdims=True) + EPS)
            y = t * r
            dy = dd[:, DN_DIM * h:DN_DIM * (h + 1)]
            douts.append(r * (dy - y * jnp.sum(dy * y, axis=-1, keepdims=True)))
    douts.append(ext(dv, dv_next))
    dc = jnp.concatenate(douts, axis=1) * dact
    dqkv = _conv_bwd_x(dc[:tm], dc[tm:], w8, DN_CONV)
    dconv_w = _conv_bwd_w_taps(dc[:tm], taps)
    abb = _spread_heads(ab)
    valid = _row_ids(i, tm) >= PAD
    pre = abb[:, :DN_WIDTH] + dtb
    ea = jnp.exp(alog)
    g = -ea * _softplus(pre)
    dg = jnp.where(valid, dg, 0.0)
    dbeta = jnp.where(valid, dbeta, 0.0)
    da = dg * (-ea) * _sigmoid(pre)
    beta = _sigmoid(abb[:, DN_WIDTH:])
    db = dbeta * beta * (1.0 - beta)
    dab = _gather_heads(jnp.concatenate([da, db], axis=1))
    return (dqkv, dab), (dconv_w, _rowsum(dg * g), _rowsum(da))


def _f_mix(i, o_mla, o_dn, z, w_mla, w_dn):
    tm = o_mla.shape[0]
    valid = _row_ids(i, tm) >= PAD
    outs = []
    for h in range(MLA_HEADS):
        y, _ = _rms_fwd(o_mla[:, V_HEAD * h:V_HEAD * (h + 1)], w_mla, V_HEAD)
        outs.append(jnp.where(valid, y, 0.0))
    for h in range(DN_HEADS):
        y, _ = _rms_fwd(o_dn[:, DN_DIM * h:DN_DIM * (h + 1)], w_dn, DN_DIM)
        outs.append(y * _silu_parts(z[:, DN_DIM * h:DN_DIM * (h + 1)])[0])
    return (jnp.concatenate(outs, axis=1),), ()


def _f_mix_bwd(i, o_mla, o_dn, z, dy_mla, dy_dn, w_mla, w_dn):
    tm = o_mla.shape[0]
    valid = _row_ids(i, tm) >= PAD
    d_mla, d_dn, d_z = [], [], []
    dw_mla = None
    dw_dn = None
    for h in range(MLA_HEADS):
        sl = slice(V_HEAD * h, V_HEAD * (h + 1))
        dx, dwr = _rms_bwd(o_mla[:, sl], w_mla, jnp.where(valid, dy_mla[:, sl], 0.0), V_HEAD)
        d_mla.append(dx)
        dw_mla = _rowsum(dwr) if dw_mla is None else dw_mla + _rowsum(dwr)
    for h in range(DN_HEADS):
        sl = slice(DN_DIM * h, DN_DIM * (h + 1))
        y, _ = _rms_fwd(o_dn[:, sl], w_dn, DN_DIM)
        sz, dsz = _silu_parts(z[:, sl])
        d_z.append(dy_dn[:, sl] * y * dsz)
        dx, dwr = _rms_bwd(o_dn[:, sl], w_dn, dy_dn[:, sl] * sz, DN_DIM)
        d_dn.append(dx)
        dw_dn = _rowsum(dwr) if dw_dn is None else dw_dn + _rowsum(dwr)
    return ((jnp.concatenate(d_mla, axis=1), jnp.concatenate(d_dn, axis=1), jnp.concatenate(d_z, axis=1)),
            (dw_mla, dw_dn))


def _f_ffn_act_bwd(i, gp, gp_prev, gp_next, up, up_next, dact, dact_next, w8, b, *, nt):
    tm = gp.shape[0]
    gp_prev = jnp.where(i > 0, gp_prev, 0.0)
    dact_next = jnp.where(i < nt - 1, dact_next, 0.0)
    cat = lambda t, t_next: jnp.concatenate([t, t_next], axis=0)
    taps = _conv_taps(cat(gp, gp_next), gp_prev, FFN_CONV)
    gate = _conv_from_taps(taps, w8) + b
    sg, dsg = _silu_parts(gate)
    dact_e = cat(dact, dact_next)
    dgate = dact_e * cat(up, up_next) * dsg
    dgate_pre = _conv_bwd_x(dgate[:tm], dgate[tm:], w8, FFN_CONV)
    dup = dact * sg[:tm]
    return (dgate_pre, dup), (_conv_bwd_w_taps(dgate[:tm], taps), _rowsum(dgate[:tm]))


def _f_loss(i, h3, tgt):
    tm = h3.shape[0]
    diff = jnp.where(_row_ids(i, tm) >= ROW0, h3 - tgt, 0.0)
    part = 0.5 * jnp.sum(diff * diff) * (1.0 / D_MODEL)
    return (diff * (1.0 / D_MODEL),), (jnp.full((1, LANE), part, F32),)


def _local_step(h0, tgt, w, token, late_weights, grads_ready):
    tp = h0.shape[0]
    nt = tp // TM
    proj, u = _norm_mm("in_proj", h0, w["attn_norm_w"], w["w_in"], after=token)
    p_qkv = lambda kind="cur": _In(proj, 3 * DN_WIDTH, 0, kind)
    p_z = _In(proj, DN_WIDTH, C_Z // DN_WIDTH)
    p_ql = _In(proj, Q_LORA, C_QL // Q_LORA)
    p_kvl = _In(proj, KV_LORA, C_KVL // KV_LORA)
    p_kpe = _In(proj, LANE, C_KPE // LANE)
    p_ab = _In(proj, LANE, C_AB // LANE)
    cos, sin_s = _In(w["cos"]), _In(w["sin_s"])

    mla_w = [w["q_a_norm_w"], w["kv_a_norm_w"], w["w_q_b"], w["w_kv_b"], w["q_norm_w"], w["k_norm_w"]]
    tm_mla = _pick(tp, 288, 16)
    wide = MLA_HEADS * HP
    qn, kvn, qf, kvf, q, k, v = _rows(
        "mla_front", _f_mla_front, [p_ql, p_kvl, p_kpe, cos, sin_s], mla_w,
        [(Q_LORA, _MXU), (KV_LORA, _MXU), (wide, F32), (wide, F32), (wide, _MXU), (wide, _MXU),
         (MLA_HEADS * V_HEAD, _MXU)], tm=tm_mla)
    o_mla = _attn_fwd(q, k, v)

    dn_w = [w["dn_conv_w"], w["alog_b"], w["dtb_b"]]
    o_dn, s_all, t_all, gq, gk, gv, gg, gb = _gdn_fwd(proj, *dn_w)

    out_w = [w["mla_out_norm_w"], w["dn_out_norm_w"]]
    w = dict(w, **late_weights((o_mla, o_dn), _LATE[:3]))
    h2, mixed = _pro_mm("mix_out_proj", lambda i, *t: _f_mix(i, *t)[0][0], [_In(o_mla), _In(o_dn), p_z], out_w,
                        D_MODEL, w["w_out"], h0)

    ffn_w = [w["ffn_conv_w"], w["ffn_conv_b"]]
    hn, gate_pre, up, act = _ffn_in(h2, w["ffn_norm_w"], w["w_gate"], w["w_up"], *ffn_w)
    w = dict(w, **late_weights(act, _LATE[3:]))
    dh3, loss = _mm_rows("ffn_down_loss", act, w["w_down"], "nn", lambda i, y, r, t: _f_loss(i, r + y, t),
                         [_In(h2), _In(tgt)], [], [(D_MODEL, F32)], [(1, LANE)])

    g = {}
    dact = _mm("ffn_down_dx", dh3, w["w_down"], "nt")
    g["w_down"] = _mm("ffn_down_dw", act, dh3, "tn", out_dtype=_MXU)
    dgate_pre, dup, g["ffn_conv_w"], g["ffn_conv_b"] = _rows(
        "ffn_act_bwd", functools.partial(_f_ffn_act_bwd, nt=nt),
        [_In(gate_pre), _In(gate_pre, kind="prev"), _In(gate_pre, kind="next"), _In(up), _In(up, kind="next"),
         _In(dact), _In(dact, kind="next")], ffn_w,
        [(D_FF, _MXU), (D_FF, _MXU)], [(8, D_FF), (1, D_FF)])
    g["w_gate"], g["w_up"] = _mm_tn2("ffn_gate_up_dw", dgate_pre, dup, hn, out_dtype=_MXU)
    tok = grads_ready(g, ("w_down", "w_gate", "w_up"))
    dh2, g["ffn_norm_w"] = _mm_rows(
        "ffn_gate_up_dx_rms", [dgate_pre, dup], [w["w_gate"], w["w_up"]], "nn",
        lambda i, dy, x, dres, nw, _tok: _f_rms_bwd_add(i, x, dy, dres, nw, mask_pad=True),
        [_In(h2), _In(dh3)], [w["ffn_norm_w"], tok], [(D_MODEL, F32)], [(1, D_MODEL)], tm_cap=288)

    g["w_out"] = _mm("out_proj_dw", mixed, dh2, "tn", out_dtype=_MXU)
    half = MLA_HEADS * V_HEAD
    do_mla, do_dn, dz, g["mla_out_norm_w"], g["dn_out_norm_w"] = _mm_rows(
        "out_proj_dx_mix", dh2, w["w_out"], "nt",
        lambda i, dm, om, od, z, wm, wd: _f_mix_bwd(i, om, od, z, dm[:, :half], dm[:, half:], wm, wd),
        [_In(o_mla), _In(o_dn), p_z], out_w,
        [(half, F32), (DN_WIDTH, F32), (DN_WIDTH, _MXU)], [(1, V_HEAD), (1, DN_DIM)])

    dq, dk, dv = _attn_bwd(q, k, v, do_mla)
    dqf, dkvf, dkpe, dql, dkvl, g["q_norm_w"], g["k_norm_w"], g["q_a_norm_w"], g["kv_a_norm_w"] = _rows(
        "mla_back", _f_mla_back,
        [_In(qf), _In(kvf), p_kpe, cos, sin_s, _In(dq), _In(dk), _In(dv), p_ql, p_kvl], mla_w,
        [(wide, _MXU), (wide, _MXU), (LANE, _MXU), (Q_LORA, _MXU), (KV_LORA, _MXU)],
        [(1, HP), (1, HP), (1, Q_LORA), (1, KV_LORA)], tm=tm_mla)
    g["w_q_b"], g["w_kv_b"] = _mm_tn_pair("mla_b_dw", dqf, qn, kvn, dkvf)
    tok = grads_ready(g, ("w_out", "w_q_b", "w_kv_b"))

    dqkv, dab, g["dn_conv_w"], g["alog_b"], g["dtb_b"] = _gdn_bwd(
        gq, gk, gv, gg, gb, s_all, t_all, do_dn, proj, *dn_w, tok)

    dproj = jnp.concatenate([dqkv, dz, dql, dkvl, dkpe, dab], axis=1)
    g["w_in"] = _mm("in_proj_dw", dproj, u, "tn", out_dtype=_MXU)
    tok = grads_ready(g, ("w_in",))
    dh0, g["attn_norm_w"] = _mm_rows(
        "in_proj_dx_rms", dproj, w["w_in"], "nn",
        lambda i, du, x, dres, nw, _tok: _f_rms_bwd_add(i, x, du, dres, nw, mask_pad=False),
        [_In(h0), _In(dh2)], [w["attn_norm_w"], tok], [(D_MODEL, F32)], [(1, D_MODEL)])
    return loss, dh0, g


def _w_in_to_padded(w):
    c1, c2, c3 = Q_LORA, Q_LORA + KV_LORA, Q_LORA + KV_LORA + QK_ROPE
    c4 = c3 + 3 * DN_WIDTH
    c5 = c4 + DN_WIDTH
    z = lambda n: jnp.zeros((n, w.shape[1]), w.dtype)
    return jnp.concatenate([w[c3:c4], w[c4:c5], w[:c1], w[c1:c2], w[c2:c3], z(LANE - QK_ROPE),
                            w[c5:], z(LANE - 2 * DN_HEADS)], axis=0)


def _w_in_from_padded(g):
    return jnp.concatenate([g[C_QL:C_QL + Q_LORA], g[C_KVL:C_KVL + KV_LORA], g[C_KPE:C_KPE + QK_ROPE],
                            g[:C_Z + DN_WIDTH], g[C_AB:C_AB + 2 * DN_HEADS]], axis=0)


def _w_q_b_to_padded(w):
    r = w.shape[1]
    w = w.reshape(MLA_HEADS, QK_HEAD, r)
    return jnp.pad(w, ((0, 0), (0, HP - QK_HEAD), (0, 0))).reshape(MLA_HEADS * HP, r)


def _w_q_b_from_padded(g):
    r = g.shape[1]
    return g.reshape(MLA_HEADS, HP, r)[:, :QK_HEAD].reshape(MLA_HEADS * QK_HEAD, r)


def _pad_rows8(w):
    return jnp.pad(w, ((0, 8 - w.shape[0]), (0, 0)))


def _prepare(full, tp):
    w = {}
    mx = lambda a: a.astype(_MXU)
    w["attn_norm_w"] = full["attn_norm_w"]
    w["w_in"] = mx(_w_in_to_padded(full["w_in"]))
    w["q_a_norm_w"] = full["q_a_norm_w"]
    w["kv_a_norm_w"] = full["kv_a_norm_w"]
    w["w_q_b"] = mx(_w_q_b_to_padded(full["w_q_b"]))
    w["w_kv_b"] = mx(full["w_kv_b"])
    w["q_norm_w"] = jnp.pad(full["q_norm_w"], ((0, 0), (0, HP - QK_HEAD)))
    w["k_norm_w"] = jnp.pad(full["k_norm_w"], ((0, 0), (0, HP - QK_HEAD)))
    w["mla_out_norm_w"] = full["mla_out_norm_w"]
    w["dn_out_norm_w"] = full["dn_out_norm_w"]
    w["dn_conv_w"] = _pad_rows8(full["dn_conv_w"])
    w["alog_b"] = jnp.repeat(full["dn_A_log"], DN_DIM, axis=1)
    w["dtb_b"] = jnp.repeat(full["dn_dt_bias"], DN_DIM, axis=1)
    w["ffn_norm_w"] = full["ffn_norm_w"]
    w["ffn_conv_w"] = _pad_rows8(full["ffn_conv_w"])
    w["ffn_conv_b"] = full["ffn_conv_b"]
    for n in _LATE:
        if n in full:
            w[n] = mx(full[n])
    half = QK_ROPE // 2
    inv = ROPE_THETA ** (-jnp.arange(half, dtype=F32) / half)
    ang = (jnp.arange(tp, dtype=jnp.int32) - PAD).astype(F32)[:, None] * inv[None, :]
    zc = jnp.zeros((tp, LANE - QK_ROPE), F32)
    w["cos"] = jnp.concatenate([jnp.cos(ang), jnp.cos(ang), zc], axis=1)
    w["sin_s"] = jnp.concatenate([-jnp.sin(ang), jnp.sin(ang), zc], axis=1)
    return w


def _grads_to_natural(g):
    convert = {
        "w_in": ("w_in", _w_in_from_padded),
        "w_q_b": ("w_q_b", _w_q_b_from_padded),
        "q_norm_w": ("q_norm_w", lambda a: a[:, :QK_HEAD]),
        "k_norm_w": ("k_norm_w", lambda a: a[:, :QK_HEAD]),
        "dn_conv_w": ("dn_conv_w", lambda a: a[:DN_CONV]),
        "ffn_conv_w": ("ffn_conv_w", lambda a: a[:FFN_CONV]),
        "alog_b": ("dn_A_log", lambda a: a[:, ::DN_DIM]),
        "dtb_b": ("dn_dt_bias", lambda a: a[:, ::DN_DIM]),
    }
    n = {}
    for key, a in g.items():
        name, fn = convert.get(key, (key, lambda t: t))
        n[name] = fn(a)
    return n


_MESH = pl.DeviceIdType.MESH
_ANY = pl.BlockSpec(memory_space=pl.ANY)
_CHIP_FLIPS = ((1, 0), (0, 1), (1, 1))


def _me():
    return lax.axis_index("x"), lax.axis_index("y"), lax.axis_index("c")


def _all_gather(name, blk, after):
    after = list(after)

    def body(x_ref, *rest):
        out_ref, send_sems, recv_sems, local_sem = rest[len(after):]
        x, y, c = _me()
        me, sib = (x, y, c), (x, y, 1 - c)
        chips = [(x ^ fx, y ^ fy) for fx, fy in _CHIP_FLIPS]

        def slot(p):
            return out_ref.at[4 * p[0] + 2 * p[1] + p[2]]

        def copy(k, block, to, src=None):
            return pltpu.make_async_remote_copy(
                src_ref=slot(block) if src is None else src, dst_ref=slot(block),
                send_sem=send_sems.at[k], recv_sem=recv_sems.at[k], device_id=to, device_id_type=_MESH)

        mine = pltpu.make_async_copy(x_ref, slot(me), local_sem)
        mine.start()
        first = [copy(0, me, sib, src=x_ref)]
        first += [copy(1 + j, me, (*chip, c), src=x_ref) for j, chip in enumerate(chips)]
        for cp in first:
            cp.start()
        passed = [copy(4 + j, (*chip, c), sib) for j, chip in enumerate(chips)]
        for j, chip in enumerate(chips):
            copy(1 + j, (*chip, c), me).wait_recv()
            passed[j].start()
        copy(0, sib, me).wait_recv()
        for j, chip in enumerate(chips):
            copy(4 + j, (*chip, 1 - c), me).wait_recv()
        for cp in first + passed:
            cp.wait_send()
        mine.wait()

    return pl.pallas_call(
        body, name=name, in_specs=[_ANY] * (1 + len(after)), out_specs=_ANY,
        out_shape=jax.ShapeDtypeStruct((N_DEV,) + blk.shape, blk.dtype),
        scratch_shapes=[pltpu.SemaphoreType.DMA((7,)), pltpu.SemaphoreType.DMA((7,)), pltpu.SemaphoreType.DMA],
    )(blk, *after)


def _row_tile(r):
    divs = [d for d in range(16, min(r, 512) + 1, 16) if r % d == 0]
    return divs[-1] if divs else r


def _adam_math(g, w, m, v):
    m_new = ADAM_B1 * m + (1.0 - ADAM_B1) * g
    v_new = ADAM_B2 * v + (1.0 - ADAM_B2) * (g * g)
    m_hat = m_new / (1.0 - ADAM_B1 ** ADAM_STEP)
    v_hat = v_new / (1.0 - ADAM_B2 ** ADAM_STEP)
    return -ADAM_LR * (m_hat / (jnp.sqrt(v_hat) + ADAM_EPS) + ADAM_WD * w), m_new, v_new


def _adam_vectors(name, row, items, ws, ms, vs):
    k = len(items)

    def body(row_ref, *refs):
        w_refs, m_refs, v_refs = refs[:k], refs[k:2 * k], refs[2 * k:3 * k]
        outs = refs[3 * k:]
        for idx, (off, n, per_head) in enumerate(items):
            if per_head:
                spread = row_ref[:, off:off + DN_WIDTH]
                lane = lax.broadcasted_iota(jnp.int32, (1, LANE), 1)
                g = jnp.zeros((1, LANE), F32)
                for h in range(DN_HEADS):
                    g = g + jnp.where(lane == h, spread[:, DN_DIM * h:DN_DIM * h + 1], 0.0)
                g = g[:, :n]
            else:
                g = row_ref[:, off:off + n]
            d, m_new, v_new = _adam_math(g, w_refs[idx][...], m_refs[idx][...], v_refs[idx][...])
            for kind, val in enumerate((g, d, m_new, v_new)):
                outs[kind * k + idx][...] = val

    shapes = [jax.ShapeDtypeStruct((1, n), F32) for _, n, _ in items]
    res = pl.pallas_call(body, name=name, out_shape=shapes * 4)(row, *ws, *ms, *vs)
    return [list(res[kind * k:(kind + 1) * k]) for kind in range(4)]


def _adam_arrays(name, gs, ws, ms, vs):
    k = len(gs)

    def body(*refs):
        outs = refs[4 * k:]
        for idx in range(k):
            res = _adam_math(refs[idx][...], refs[k + idx][...], refs[2 * k + idx][...], refs[3 * k + idx][...])
            for kind, val in enumerate(res):
                outs[kind * k + idx][...] = val

    shapes = [jax.ShapeDtypeStruct(w.shape, F32) for w in ws]
    res = pl.pallas_call(body, name=name, out_shape=shapes * 3)(*gs, *ws, *ms, *vs)
    return [list(res[kind * k:(kind + 1) * k]) for kind in range(3)]


def _sum_parts(name, parts):
    _, r, cols = parts[0][0].shape
    tm = _row_tile(r)
    idx = jnp.stack([jnp.asarray(s, jnp.int32) for _, s in parts])
    n = len(parts)

    def body(idx_ref, *refs):
        g = refs[0][0].astype(F32)
        for p_ref in refs[1:n]:
            g = g + p_ref[0].astype(F32)
        refs[n][...] = g

    return pl.pallas_call(
        body, name=name,
        grid_spec=pltpu.PrefetchScalarGridSpec(
            num_scalar_prefetch=1, grid=(r // tm,),
            in_specs=[pl.BlockSpec((1, tm, cols), lambda i, idx_ref, p=p: (idx_ref[p], i, 0)) for p in range(n)],
            out_specs=pl.BlockSpec((tm, cols), lambda i, idx_ref: (i, 0))),
        out_shape=jax.ShapeDtypeStruct((r, cols), F32),
        compiler_params=pltpu.CompilerParams(dimension_semantics=("parallel",)),
    )(idx, *[a for a, _ in parts])


def _adam(name, parts, w, m, v):
    r, cols = w.shape
    tm = _row_tile(r)
    tc = cols // 4 if (r // tm < 4 and cols % (4 * LANE) == 0) else cols
    idx = jnp.stack([jnp.asarray(s, jnp.int32) for _, s in parts])
    n = len(parts)

    def body(idx_ref, *refs):
        g = refs[0][0].astype(F32)
        for p_ref in refs[1:n]:
            g = g + p_ref[0].astype(F32)
        w_ref, m_ref, v_ref, g_out, d_out, m_out, v_out = refs[n:]
        g_out[...] = g
        d_out[...], m_out[...], v_out[...] = _adam_math(g, w_ref[...], m_ref[...], v_ref[...])

    part_specs = [pl.BlockSpec((1, tm, tc), lambda i, j, idx_ref, p=p: (idx_ref[p], i, j)) for p in range(n)]
    flat = pl.BlockSpec((tm, tc), lambda i, j, idx_ref: (i, j))
    return pl.pallas_call(
        body, name=name,
        grid_spec=pltpu.PrefetchScalarGridSpec(
            num_scalar_prefetch=1, grid=(r // tm, cols // tc), in_specs=part_specs + [flat] * 3,
            out_specs=[flat] * 4),
        out_shape=[jax.ShapeDtypeStruct((r, cols), F32)] * 4,
        compiler_params=pltpu.CompilerParams(dimension_semantics=("parallel", "parallel")),
    )(idx, *[a for a, _ in parts], w, m, v)


def _all_gather_many(name, blks):
    n = len(blks)

    def body(*refs):
        x_refs, out_refs = refs[:n], refs[n:2 * n]
        send_sems, recv_sems, local_sems = refs[2 * n:]
        x, y, c = _me()
        me, sib = (x, y, c), (x, y, 1 - c)
        chips = [(x ^ fx, y ^ fy) for fx, fy in _CHIP_FLIPS]

        def slot(a, p):
            return out_refs[a].at[4 * p[0] + 2 * p[1] + p[2]]

        def copy(a, k, block, to, src=None):
            return pltpu.make_async_remote_copy(
                src_ref=slot(a, block) if src is None else src, dst_ref=slot(a, block),
                send_sem=send_sems.at[7 * a + k], recv_sem=recv_sems.at[7 * a + k], device_id=to,
                device_id_type=_MESH)

        mine = [pltpu.make_async_copy(x_refs[a], slot(a, me), local_sems.at[a]) for a in range(n)]
        first = []
        for a in range(n):
            mine[a].start()
            first.append(copy(a, 0, me, sib, src=x_refs[a]))
            first += [copy(a, 1 + j, me, (*chip, c), src=x_refs[a]) for j, chip in enumerate(chips)]
        for cp in first:
            cp.start()
        passed = []
        for j, chip in enumerate(chips):
            for a in range(n):
                copy(a, 1 + j, (*chip, c), me).wait_recv()
                cp = copy(a, 4 + j, (*chip, c), sib)
                cp.start()
                passed.append(cp)
        for a in range(n):
            copy(a, 0, sib, me).wait_recv()
            for j, chip in enumerate(chips):
                copy(a, 4 + j, (*chip, 1 - c), me).wait_recv()
        for cp in first + passed:
            cp.wait_send()
        for cp in mine:
            cp.wait()

    return pl.pallas_call(
        body, name=name, in_specs=[_ANY] * n, out_specs=[_ANY] * n,
        out_shape=[jax.ShapeDtypeStruct((N_DEV,) + b.shape, b.dtype) for b in blks],
        scratch_shapes=[pltpu.SemaphoreType.DMA((7 * n,)), pltpu.SemaphoreType.DMA((7 * n,)),
                        pltpu.SemaphoreType.DMA((n,))],
    )(*blks)


_HBM = pl.BlockSpec(memory_space=pltpu.HBM)
_SEM = pl.BlockSpec(memory_space=pltpu.SEMAPHORE)
_EFFECT = pltpu.SideEffectType.DATAFLOW_SIDE_EFFECTING


def _push_copies(src_refs, land_refs, send_sems, recv_sems, src_by_peer, first=0):
    x, y, c = _me()
    my_id = 4 * x + 2 * y + c
    out = []
    for k in range(len(src_refs)):
        a = first + k
        for f in range(1, N_DEV):
            px, py, pc = x ^ (f >> 2), y ^ ((f >> 1) & 1), c ^ (f & 1)
            pid = 4 * px + 2 * py + pc
            src = src_refs[k].at[pid] if src_by_peer else src_refs[k]
            start = pltpu.make_async_remote_copy(
                src_ref=src, dst_ref=land_refs[k].at[my_id], send_sem=send_sems.at[7 * a + f - 1],
                recv_sem=recv_sems.at[7 * a + f - 1], device_id=(px, py, pc), device_id_type=_MESH)
            landed = pltpu.make_async_remote_copy(
                src_ref=src, dst_ref=land_refs[k].at[pid], send_sem=send_sems.at[7 * a + f - 1],
                recv_sem=recv_sems.at[7 * a + f - 1], device_id=(px, py, pc), device_id_type=_MESH)
            out.append((start, landed))
    return out


def _push_start(name, srcs, src_by_peer, after):
    n = len(srcs)
    lands = [jax.ShapeDtypeStruct((N_DEV,) + (s.shape[1:] if src_by_peer else s.shape), s.dtype) for s in srcs]

    def body(*refs):
        src_refs, land_refs = refs[:n], refs[n:2 * n]
        send_sems, recv_sems = refs[2 * n + 1], refs[2 * n + 2]
        token = refs[-1]
        for start, _ in _push_copies(src_refs, land_refs, send_sems, recv_sems, src_by_peer):
            start.start()
        token[...] = jnp.zeros_like(token)

    hbm = lambda a: pltpu.with_memory_space_constraint(a, pltpu.HBM)
    res = pl.pallas_call(
        body, name=name,
        out_shape=(pltpu.SemaphoreType.DMA((7 * n,)), pltpu.SemaphoreType.DMA((7 * n,)),
                   *[pltpu.HBM(s.shape, s.dtype) for s in srcs], *[pltpu.HBM(s.shape, s.dtype) for s in lands],
                   jax.ShapeDtypeStruct((8, LANE), F32)),
        in_specs=[_HBM] * (2 * n) + [_ANY],
        out_specs=(_SEM, _SEM, *[_HBM] * (2 * n), pl.BlockSpec(memory_space=pltpu.VMEM)),
        input_output_aliases={i: 2 + i for i in range(2 * n)},
        compiler_params=pltpu.CompilerParams(has_side_effects=_EFFECT),
    )(*[hbm(s) for s in srcs], *[hbm(lax.empty(s.shape, s.dtype)) for s in lands], after)
    return res[0], res[1], list(res[2:2 + n]), list(res[2 + n:2 + 2 * n]), res[-1]


def _push_wait(name, send_sems, recv_sems, srcs, lands, src_by_peer, after, first=0):
    n = len(srcs)
    after = list(after) if isinstance(after, (list, tuple)) else [after]

    def body(*refs):
        src_refs, land_refs = refs[:n], refs[n:2 * n]
        s_sems, r_sems = refs[2 * n], refs[2 * n + 1]
        for _, landed in _push_copies(src_refs, land_refs, s_sems, r_sems, src_by_peer, first):
            landed.wait_send()
            landed.wait_recv()

    res = pl.pallas_call(
        body, name=name,
        out_shape=tuple(pltpu.HBM(s.shape, s.dtype) for s in list(srcs) + list(lands)),
        in_specs=[_HBM] * (2 * n) + [_SEM, _SEM] + [_ANY] * len(after),
        out_specs=tuple([_HBM] * (2 * n)),
        input_output_aliases={i: i for i in range(2 * n)},
        compiler_params=pltpu.CompilerParams(has_side_effects=_EFFECT),
    )(*srcs, *lands, send_sems, recv_sems, *after)
    return list(res[:n]), list(res[n:])


_SHARDED = (
    ("meta_tokens", 1, (N_META, D_MODEL)),
    ("w_in", 1, (D_MODEL, IN_COLS)),
    ("w_q_b", 1, (Q_LORA, MLA_HEADS * QK_HEAD)),
    ("w_kv_b", 1, (KV_LORA, MLA_HEADS * (QK_NOPE + V_HEAD))),
    ("dn_conv_w", 1, (DN_CONV, 3 * DN_WIDTH)),
    ("w_out", 0, (2 * DN_WIDTH, D_MODEL)),
    ("w_gate", 1, (D_MODEL, D_FF)),
    ("w_up", 1, (D_MODEL, D_FF)),
    ("ffn_conv_w", 1, (FFN_CONV, D_FF)),
    ("w_down", 0, (D_FF, D_MODEL)),
)
_F32_GATHERED = ("meta_tokens", "dn_conv_w", "ffn_conv_w")
_EARLY = ("w_in", "w_q_b", "w_kv_b")
_LATE = ("w_out", "w_gate", "w_up", "w_down")
_TRANSPOSED = ("w_in", "w_q_b", "w_gate", "w_up")
_REPLICATED = (
    ("attn_norm_w", D_MODEL), ("q_a_norm_w", Q_LORA), ("kv_a_norm_w", KV_LORA), ("q_norm_w", QK_HEAD),
    ("k_norm_w", QK_HEAD), ("mla_out_norm_w", V_HEAD), ("dn_A_log", DN_HEADS), ("dn_dt_bias", DN_HEADS),
    ("dn_out_norm_w", DN_DIM), ("ffn_norm_w", D_MODEL), ("ffn_conv_b", D_FF),
)
_SMALL_BLOCK = (8, 512)


def _local_shape(dim, shape):
    return (shape[0] // N_DEV, shape[1]) if dim == 0 else (shape[0], shape[1] // N_DEV)


def _from_blocks(blocks, dim, shape):
    r, c = shape
    if dim == 0:
        return blocks.reshape(r, c)
    return blocks.reshape(N_DEV, r, c // N_DEV).transpose(1, 0, 2).reshape(r, c)


def _split(flat, sizes):
    out, o = [], 0
    for s in sizes:
        out.append(flat[..., o:o + s])
        o += s
    return out


def kernel(x, meta_tokens, attn_norm_w, w_in, q_a_norm_w, w_q_b, kv_a_norm_w, w_kv_b, q_norm_w, k_norm_w, mla_out_norm_w, dn_conv_w, dn_A_log, dn_dt_bias, dn_out_norm_w, w_out, ffn_norm_w, w_gate, w_up, ffn_conv_w, ffn_conv_b, w_down, loss_target, m_meta_tokens, m_attn_norm_w, m_w_in, m_q_a_norm_w, m_w_q_b, m_kv_a_norm_w, m_w_kv_b, m_q_norm_w, m_k_norm_w, m_mla_out_norm_w, m_dn_conv_w, m_dn_A_log, m_dn_dt_bias, m_dn_out_norm_w, m_w_out, m_ffn_norm_w, m_w_gate, m_w_up, m_ffn_conv_w, m_ffn_conv_b, m_w_down, v_meta_tokens, v_attn_norm_w, v_w_in, v_q_a_norm_w, v_w_q_b, v_kv_a_norm_w, v_w_kv_b, v_q_norm_w, v_k_norm_w, v_mla_out_norm_w, v_dn_conv_w, v_dn_A_log, v_dn_dt_bias, v_dn_out_norm_w, v_w_out, v_ffn_norm_w, v_w_gate, v_w_up, v_ffn_conv_w, v_ffn_conv_b, v_w_down):
    names = [n for n, _, _ in _SHARDED] + [n for n, _ in _REPLICATED]
    given = dict(locals())
    two_d = lambda a: a.reshape(a.shape[-2:])
    view = lambda a, n: two_d(a).T if n in _TRANSPOSED else two_d(a)
    wl = {n: view(given[n], n) for n in names}
    ml = {n: view(given["m_" + n], n) for n in names}
    vl = {n: view(given["v_" + n], n) for n in names}
    out_shapes = {n: given[n].shape for n in names}

    spec = {n: (d, s) for n, d, s in _SHARDED}
    small_sizes = [math.prod(_local_shape(*spec[n])) for n in _F32_GATHERED]

    def small_block(d):
        cat = jnp.concatenate([d[n].reshape(d[n].shape[:-2] + (-1,)) for n in _F32_GATHERED], axis=-1)
        pad = [(0, 0)] * (cat.ndim - 1) + [(0, math.prod(_SMALL_BLOCK) - cat.shape[-1])]
        return jnp.pad(cat, pad).reshape(cat.shape[:-1] + _SMALL_BLOCK)

    def shard(n):
        return wl[n].astype(_MXU)

    def from_slots(n, blocks):
        d, s = spec[n]
        if d == 0 or n in _TRANSPOSED:
            return blocks.reshape(-1, blocks.shape[-1])
        return blocks.transpose(1, 0, 2).reshape(s)

    my_id = 4 * lax.axis_index("x") + 2 * lax.axis_index("y") + lax.axis_index("c")
    got = _all_gather_many("gather_early", [shard(n) for n in _EARLY] + [small_block(wl)])
    full = {n: a for n, a in wl.items() if n not in _LATE}
    for n, blocks in zip(_EARLY, got):
        full[n] = from_slots(n, blocks)
    for n, p in zip(_F32_GATHERED, _split(got[-1].reshape(N_DEV, -1), small_sizes)):
        full[n] = _from_blocks(p, *spec[n])
    late_own = [shard(n) for n in _LATE]
    l_send, l_recv, l_src, l_land, token = _push_start("gather_late_start", late_own, False, got[-1])

    def late_weights(after, names):
        first = _LATE.index(names[0])
        sl = slice(first, first + len(names))
        _, lands = _push_wait("gather_late_wait_" + names[0], l_send, l_recv, l_src[sl], l_land[sl], False,
                              after, first)
        out = {}
        for n, land, own in zip(names, lands, late_own[sl]):
            out[n] = from_slots(n, lax.dynamic_update_slice(land, own[None], (my_id, 0, 0))).astype(_MXU)
        return out

    def dest_blocks(n, a):
        d, s = spec[n]
        r, c = _local_shape(d, s)
        if n in _TRANSPOSED:
            return a.reshape(N_DEV, c, r)
        return a.reshape(N_DEV, r, c) if d == 0 else a.reshape(r, N_DEV, c).transpose(1, 0, 2)

    pushed = []

    def grads_ready(g, names):
        nat = _grads_to_natural({n: g[n] for n in names})
        blocks = [dest_blocks(n, nat[n]).astype(_MXU) for n in names]
        sends, recvs, srcs, lands, tok = _push_start("rs_" + names[0] + "_start", blocks, True, token)
        pushed.append((names, sends, recvs, srcs, lands))
        return tok

    seq = x.shape[1]
    tp = ROW0 + seq
    h0 = jnp.concatenate([jnp.zeros((PAD, D_MODEL), F32), full["meta_tokens"], x[0]], axis=0)
    tgt = jnp.concatenate([jnp.zeros((ROW0, D_MODEL), F32), loss_target[0]], axis=0)
    loss, dh0, raw = _local_step(h0, tgt, _prepare(full, tp), token, late_weights, grads_ready)
    g = _grads_to_natural(raw)
    g["meta_tokens"] = dh0[PAD:ROW0]
    grad_x = dh0[ROW0:][None]

    big = [{}, {}, {}, {}]

    def finish(group):
        names, sends, recvs, srcs, lands = group
        srcs, lands = _push_wait("rs_" + names[0] + "_wait", sends, recvs, srcs, lands, True, dh0)
        for n, src, land in zip(names, srcs, lands):
            parts = [(src, my_id)] + [(land, my_id ^ f) for f in range(1, N_DEV)]
            for kind, a in enumerate(_adam("adam_" + n, parts, wl[n], ml[n], vl[n])):
                big[kind][n] = a

    for group in pushed[:-1]:
        finish(group)
    rep_names = [n for n, _ in _REPLICATED]
    raw_key = {"dn_A_log": "alog_b", "dn_dt_bias": "dtb_b"}
    pieces = [raw[raw_key.get(n, n)] for n in rep_names] + [loss]
    pieces += [g[n].reshape(1, -1) for n in _F32_GATHERED]
    widths = [p.shape[1] for p in pieces]
    offs = [sum(widths[:k]) for k in range(len(widths))]
    cat = jnp.concatenate(pieces, axis=1)
    cols = -(-cat.shape[1] // (8 * LANE)) * LANE
    mine = jnp.pad(cat, ((0, 0), (0, 8 * cols - cat.shape[1]))).reshape(8, cols)
    everyone = _all_gather("gather_small_grads", mine, [big[1][n] for group in pushed[:-1] for n in group[0]])
    total = _sum_parts("sum_small_grads", [(everyone, d) for d in range(N_DEV)]).reshape(1, 8 * cols)
    tot = {n: total[0, o:o + wd] for n, o, wd in zip(rep_names + ["loss"] + list(_F32_GATHERED), offs, widths)}
    items = [(o, size, n in raw_key) for (n, size), o in zip(_REPLICATED, offs)]
    sm = _adam_vectors("adam_replicated", total, items, [wl[n] for n in rep_names], [ml[n] for n in rep_names],
                       [vl[n] for n in rep_names])
    sm = [dict(zip(rep_names, kind)) for kind in sm]
    mine_of = {}
    for n in _F32_GATHERED:
        d, s = spec[n]
        r, c = _local_shape(d, s)
        mine_of[n] = lax.dynamic_slice(tot[n].reshape(s), (0, my_id * c), (r, c))
    res = _adam_arrays("adam_small_sharded", [mine_of[n] for n in _F32_GATHERED], [wl[n] for n in _F32_GATHERED],
                       [ml[n] for n in _F32_GATHERED], [vl[n] for n in _F32_GATHERED])
    for kind, arrays in enumerate([[mine_of[n] for n in _F32_GATHERED]] + res):
        big[kind].update(zip(_F32_GATHERED, arrays))

    finish(pushed[-1])

    outs = [tot["loss"][0], grad_x]
    for kind in range(4):
        for n in ("meta_tokens", "attn_norm_w", "w_in", "q_a_norm_w", "w_q_b", "kv_a_norm_w", "w_kv_b", "q_norm_w",
                  "k_norm_w", "mla_out_norm_w", "dn_conv_w", "dn_A_log", "dn_dt_bias", "dn_out_norm_w", "w_out",
                  "ffn_norm_w", "w_gate", "w_up", "ffn_conv_w", "ffn_conv_b", "w_down"):
            src = big[kind] if n in big[kind] else sm[kind]
            a = src[n].T if n in _TRANSPOSED else src[n]
            outs.append(a.reshape(out_shapes[n]))
    return tuple(outs)
```

```python
import functools
import math

import jax
import jax.numpy as jnp
from jax import lax
from jax.experimental import pallas as pl
from jax.experimental.pallas import tpu as pltpu

F32 = jnp.float32
_MXU = jnp.bfloat16
_HI = lax.Precision.HIGHEST

D_MODEL = 1024
N_META = 16
PAD = 112
ROW0 = PAD + N_META
MLA_HEADS = 4
QK_NOPE = 128
QK_ROPE = 64
QK_HEAD = QK_NOPE + QK_ROPE
V_HEAD = 128
Q_LORA = 256
KV_LORA = 256
ROPE_THETA = 10000.0
DN_HEADS = 4
DN_DIM = 128
DN_WIDTH = DN_HEADS * DN_DIM
DN_CONV = 4
DN_CHUNK = 64
GDN_SUB_CHUNKS = 2
D_FF = 2816
FFN_CONV = 3
EPS = 1e-6
HP = 256
C_Z = 1536
C_QL = 2048
C_KVL = 2304
C_KPE = 2560
C_AB = 2688
IN_COLS = 2632

ADAM_LR = 0.001
ADAM_B1 = 0.9
ADAM_B2 = 0.999
ADAM_EPS = 1e-08
ADAM_WD = 0.01
ADAM_STEP = 10

N_DEV = 8
TM = 128
LANE = 128
VMEM_LIMIT = 56 * 1024 * 1024
NEG = -1e30


def _dot(a, b, dims, hp=False):
    if hp:
        return lax.dot_general(a.astype(F32), b.astype(F32), (dims, ((), ())),
                               precision=lax.Precision.HIGH if hp == "3x" else _HI, preferred_element_type=F32)
    return lax.dot_general(a.astype(_MXU), b.astype(_MXU), (dims, ((), ())),
                           preferred_element_type=F32)


def _nn(a, b, hp=False):
    return _dot(a, b, ((1,), (0,)), hp)


def _nt(a, b, hp=False):
    return _dot(a, b, ((1,), (1,)), hp)


def _tn(a, b, hp=False):
    return _dot(a, b, ((0,), (0,)), hp)


def _sigmoid(x):
    return 1.0 / (1.0 + jnp.exp(-x))


def _rms_fwd(x, w, n):
    r = lax.rsqrt(jnp.sum(x * x, axis=-1, keepdims=True) * (1.0 / n) + EPS)
    return x * r * w, r


def _rms_bwd(x, w, dy, n):
    r = lax.rsqrt(jnp.sum(x * x, axis=-1, keepdims=True) * (1.0 / n) + EPS)
    xh = x * r
    gy = dy * w
    dx = r * (gy - xh * (jnp.sum(gy * xh, axis=-1, keepdims=True) * (1.0 / n)))
    return dx, dy * xh


def _rowsum(x):
    return jnp.sum(x, axis=0, keepdims=True)


def _row_ids(i, tm):
    return i * tm + lax.broadcasted_iota(jnp.int32, (tm, 1), 0)


def _shift_down(ext, s, tm):
    if s == 0:
        return ext[8:8 + tm]
    return pltpu.roll(ext, s, 0)[8:8 + tm]


def _shift_up(ext, s, tm):
    if s == 0:
        return ext[0:tm]
    return pltpu.roll(ext, tm + 8 - s, 0)[0:tm]


def _conv_taps(x, halo_prev, width):
    tm = x.shape[0]
    ext = jnp.concatenate([halo_prev, x], axis=0)
    return [_shift_down(ext, width - 1 - j, tm) for j in range(width)]


def _conv_from_taps(taps, w):
    y = None
    for j, tap in enumerate(taps):
        t = w[j:j + 1, :] * tap
        y = t if y is None else y + t
    return y


def _conv_fwd(x, halo_prev, w, width):
    return _conv_from_taps(_conv_taps(x, halo_prev, width), w)


def _conv_bwd_w_taps(dy, taps):
    tm = dy.shape[0]
    rows = [_rowsum(dy * tap[:tm]) for tap in taps]
    rows += [jnp.zeros_like(rows[0])] * (8 - len(taps))
    return jnp.concatenate(rows, axis=0)


def _conv_bwd_x(dy, halo_next, w, width):
    tm = dy.shape[0]
    ext = jnp.concatenate([dy, halo_next], axis=0)
    dx = None
    for j in range(width):
        t = w[j:j + 1, :] * _shift_up(ext, width - 1 - j, tm)
        dx = t if dx is None else dx + t
    return dx


def _softplus(x):
    e = jnp.exp(-jnp.abs(x))
    u = 1.0 + e
    l1p = jnp.where(u == 1.0, e, jnp.log(u) * e / jnp.where(u == 1.0, 1.0, u - 1.0))
    return jnp.maximum(x, 0.0) + l1p


def _swap_halves(x):
    lane = lax.broadcasted_iota(jnp.int32, x.shape, 1)
    return jnp.where(lane < 32, pltpu.roll(x, 96, 1), jnp.where(lane < 64, pltpu.roll(x, 32, 1), 0.0))


class _In:
    def __init__(self, arr, width=None, cb=0, kind="cur"):
        self.arr, self.kind = arr, kind
        self.width = arr.shape[1] if width is None else width
        self.cb = cb


def _whole_spec(x):
    return pl.BlockSpec(x.shape, lambda i, nd=x.ndim: (0,) * nd, pipeline_mode=pl.Buffered(1))


def _tile_spec(t, tm, tp):
    r8 = tm // 8
    if t.kind == "cur":
        return pl.BlockSpec((tm, t.width), lambda i, cb=t.cb: (i, cb))
    if t.kind == "prev":
        return pl.BlockSpec((8, t.width), lambda i, cb=t.cb: (jnp.maximum(i * r8 - 1, 0), cb))
    return pl.BlockSpec((8, t.width), lambda i, cb=t.cb: (jnp.minimum((i + 1) * r8, tp // 8 - 1), cb))


def _rows(name, fn, tiled, full, outs, accs=(), tm=TM):
    tp = tiled[0].arr.shape[0]
    nt = tp // tm
    n_in = len(tiled) + len(full)
    n_out = len(outs)

    def body(*refs):
        i = pl.program_id(0)
        vals = [r[...] for r in refs[:n_in]]
        o_t, o_a = fn(i, *vals)
        for r, v in zip(refs[n_in:n_in + n_out], o_t):
            r[...] = v.astype(r.dtype)
        for r, v in zip(refs[n_in + n_out:], o_a):
            @pl.when(i == 0)
            def _():
                r[...] = v

            @pl.when(i > 0)
            def _():
                r[...] += v

    in_specs = [_tile_spec(t, tm, tp) for t in tiled]
    in_specs += [pl.BlockSpec(a.shape, lambda i, nd=a.ndim: (0,) * nd) for a in full]
    out_specs = [pl.BlockSpec((tm, w), lambda i: (i, 0)) for w, _ in outs]
    out_specs += [pl.BlockSpec((r, w), lambda i: (0, 0)) for r, w in accs]
    out_shape = [jax.ShapeDtypeStruct((tp, w), dt) for w, dt in outs]
    out_shape += [jax.ShapeDtypeStruct((r, w), F32) for r, w in accs]
    res = pl.pallas_call(
        body, name=name, grid=(nt,), in_specs=in_specs, out_specs=out_specs, out_shape=out_shape,
        compiler_params=pltpu.CompilerParams(dimension_semantics=("arbitrary",), vmem_limit_bytes=VMEM_LIMIT),
    )(*[t.arr for t in tiled], *full)
    return res


def _pick(n, cap, mult):
    best = None
    for d in range(mult, min(n, cap) + 1, mult):
        if n % d == 0:
            best = d
    assert best is not None, (n, cap, mult)
    return best


_ANY_SPEC = pl.BlockSpec(memory_space=pl.ANY)


def _mm(name, a, b, mode, out_dtype=F32, resid=None, after=None):
    if mode == "tn":
        m, k = a.shape
        n = b.shape[1]
        tk = _pick(k, 512, 128)
        tn = _pick(n, 1408, 128)

        def body_tn(a_ref, b_ref, o_ref):
            o_ref[...] = _tn(a_ref[...], b_ref[...]).astype(o_ref.dtype)

        return pl.pallas_call(
            body_tn, name=name, grid=(n // tn, k // tk),
            in_specs=[pl.BlockSpec((m, tk), lambda j, p: (0, p)),
                      pl.BlockSpec((m, tn), lambda j, p: (0, j))],
            out_specs=pl.BlockSpec((tk, tn), lambda j, p: (p, j)),
            out_shape=jax.ShapeDtypeStruct((k, n), out_dtype),
            compiler_params=pltpu.CompilerParams(
                dimension_semantics=("parallel", "parallel"), vmem_limit_bytes=VMEM_LIMIT),
        )(a, b)

    m, k = a.shape
    n = b.shape[1] if mode == "nn" else b.shape[0]
    tn = _pick(n, 1408, 128)
    tm = _pick(m, 1152, 16)
    dotf = _nn if mode == "nn" else _nt

    def body(*refs):
        a_ref, b_ref, o_ref = refs[0], refs[1], refs[-1]
        acc = dotf(a_ref[...], b_ref[...])
        if resid is not None:
            acc = refs[2][...] + acc
        o_ref[...] = acc.astype(o_ref.dtype)

    b_spec = (pl.BlockSpec((k, tn), lambda j, i: (0, j)) if mode == "nn"
              else pl.BlockSpec((tn, k), lambda j, i: (j, 0)))
    in_specs = [pl.BlockSpec((tm, k), lambda j, i: (i, 0)), b_spec]
    args = [a, b]
    if resid is not None:
        in_specs.append(pl.BlockSpec((tm, tn), lambda j, i: (i, j)))
        args.append(resid)
    if after is not None:
        in_specs.append(_ANY_SPEC)
        args.append(after)
    return pl.pallas_call(
        body, name=name, grid=(n // tn, m // tm), in_specs=in_specs,
        out_specs=pl.BlockSpec((tm, tn), lambda j, i: (i, j)),
        out_shape=jax.ShapeDtypeStruct((m, n), out_dtype),
        compiler_params=pltpu.CompilerParams(
            dimension_semantics=("parallel", "parallel"), vmem_limit_bytes=VMEM_LIMIT),
    )(*args)


def _mm_tn2(name, a1, a2, b, out_dtype=F32):
    m, k = a1.shape
    n = b.shape[1]
    tk = _pick(k, 512, 128)

    def body(a1_ref, a2_ref, b_ref, o1_ref, o2_ref):
        bb = b_ref[...]
        o1_ref[...] = _tn(a1_ref[...], bb).astype(o1_ref.dtype)
        o2_ref[...] = _tn(a2_ref[...], bb).astype(o2_ref.dtype)

    a_spec = pl.BlockSpec((m, tk), lambda p: (0, p))
    o_spec = pl.BlockSpec((tk, n), lambda p: (p, 0))
    return pl.pallas_call(
        body, name=name, grid=(k // tk,),
        in_specs=[a_spec, a_spec, pl.BlockSpec((m, n), lambda p: (0, 0))],
        out_specs=[o_spec, o_spec], out_shape=[jax.ShapeDtypeStruct((k, n), out_dtype)] * 2,
        compiler_params=pltpu.CompilerParams(dimension_semantics=("parallel",), vmem_limit_bytes=VMEM_LIMIT),
    )(a1, a2, b)


def _mm_tn_pair(name, a1, b1, a2, b2):
    def body(a1_ref, b1_ref, a2_ref, b2_ref, o1_ref, o2_ref):
        o1_ref[...] = _tn(a1_ref[...], b1_ref[...])
        o2_ref[...] = _tn(a2_ref[...], b2_ref[...])

    return pl.pallas_call(
        body, name=name,
        out_shape=[jax.ShapeDtypeStruct((a1.shape[1], b1.shape[1]), F32),
                   jax.ShapeDtypeStruct((a2.shape[1], b2.shape[1]), F32)],
        compiler_params=pltpu.CompilerParams(vmem_limit_bytes=VMEM_LIMIT),
    )(a1, b1, a2, b2)


def _norm_mm(name, x, norm_w, b, mode="nt", x_cb=0, after=None):
    m = x.shape[0]
    k = norm_w.shape[1]
    n = b.shape[0] if mode == "nt" else b.shape[1]
    tn = _pick(n, 1408, 128)
    tm = _pick(m, 1152, 16)
    dotf = _nt if mode == "nt" else _nn
    extra = [] if after is None else [after]

    def body(x_ref, w_ref, b_ref, *rest):
        o_ref, u_ref = rest[-2:]

        @pl.when(pl.program_id(1) == 0)
        def _():
            u_ref[...] = _rms_fwd(x_ref[...], w_ref[...], k)[0].astype(u_ref.dtype)

        o_ref[...] = dotf(u_ref[...], b_ref[...])

    b_spec = (pl.BlockSpec((tn, k), lambda i, j: (j, 0)) if mode == "nt"
              else pl.BlockSpec((k, tn), lambda i, j: (0, j)))
    return pl.pallas_call(
        body, name=name, grid=(m // tm, n // tn),
        in_specs=[pl.BlockSpec((tm, k), lambda i, j: (i, x_cb)), pl.BlockSpec((1, k), lambda i, j: (0, 0)),
                  b_spec] + [_ANY_SPEC] * len(extra),
        out_specs=[pl.BlockSpec((tm, tn), lambda i, j: (i, j)), pl.BlockSpec((tm, k), lambda i, j: (i, 0))],
        out_shape=[jax.ShapeDtypeStruct((m, n), F32), jax.ShapeDtypeStruct((m, k), _MXU)],
        compiler_params=pltpu.CompilerParams(
            dimension_semantics=("arbitrary", "arbitrary"), vmem_limit_bytes=VMEM_LIMIT),
    )(x, norm_w, b, *extra)


def _pro_mm(name, fn, tiled, full, k, b, resid):
    m = resid.shape[0]
    n = b.shape[1]
    tm = _pick(m, 576, 16)
    n_in = len(tiled) + len(full)

    def body(*refs):
        i = pl.program_id(0)
        u = fn(i, *[r[...] for r in refs[:n_in]]).astype(_MXU)
        b_ref, r_ref, o_ref, u_ref = refs[n_in:]
        u_ref[...] = u
        o_ref[...] = r_ref[...] + _nn(u, b_ref[...])

    row = lambda w: pl.BlockSpec((tm, w), lambda i: (i, 0))
    in_specs = [_tile_spec(t, tm, m) for t in tiled]
    in_specs += [_whole_spec(x) for x in full] + [_whole_spec(b), row(n)]
    return pl.pallas_call(
        body, name=name, grid=(m // tm,), in_specs=in_specs, out_specs=[row(n), row(k)],
        out_shape=[jax.ShapeDtypeStruct((m, n), F32), jax.ShapeDtypeStruct((m, k), _MXU)],
        compiler_params=pltpu.CompilerParams(dimension_semantics=("parallel",), vmem_limit_bytes=VMEM_LIMIT),
    )(*[t.arr for t in tiled], *full, b, resid)


def _ffn_in(h2, norm_w, w_gate_t, w_up_t, conv_w8, conv_b):
    m, k = h2.shape
    n = w_gate_t.shape[0]
    tm = _pick(m, 288, 16)

    def body(x_ref, xp_ref, nw_ref, wg_ref, wu_ref, cw_ref, cb_ref, hn_ref, gp_ref, up_ref, act_ref):
        i = pl.program_id(0)
        nw = nw_ref[...]
        hn = _rms_fwd(x_ref[...], nw, k)[0].astype(_MXU)
        hn_prev = _rms_fwd(xp_ref[...], nw, k)[0].astype(_MXU)
        wg = wg_ref[...]
        gp = _nt(hn, wg)
        gp_prev = jnp.where(i > 0, _nt(hn_prev, wg), 0.0)
        up = _nt(hn, wu_ref[...])
        gate = _conv_fwd(gp, gp_prev, cw_ref[...], FFN_CONV) + cb_ref[...]
        hn_ref[...] = hn
        gp_ref[...] = gp
        up_ref[...] = up
        act_ref[...] = (_silu_parts(gate)[0] * up).astype(act_ref.dtype)

    row = lambda w: pl.BlockSpec((tm, w), lambda i: (i, 0))
    r8 = tm // 8
    return pl.pallas_call(
        body, name="ffn_in", grid=(m // tm,),
        in_specs=[row(k), pl.BlockSpec((8, k), lambda i: (jnp.maximum(i * r8 - 1, 0), 0)), _whole_spec(norm_w),
                  _whole_spec(w_gate_t), _whole_spec(w_up_t), _whole_spec(conv_w8), _whole_spec(conv_b)],
        out_specs=[row(k), row(n), row(n), row(n)],
        out_shape=[jax.ShapeDtypeStruct((m, k), _MXU), jax.ShapeDtypeStruct((m, n), F32),
                   jax.ShapeDtypeStruct((m, n), F32), jax.ShapeDtypeStruct((m, n), _MXU)],
        compiler_params=pltpu.CompilerParams(dimension_semantics=("parallel",), vmem_limit_bytes=VMEM_LIMIT),
    )(h2, h2, norm_w, w_gate_t, w_up_t, conv_w8, conv_b)


def _mm_rows(name, a, b, mode, fn, tiled, full, outs, accs=(), tm_cap=576):
    a_list = list(a) if isinstance(a, (list, tuple)) else [a]
    b_list = list(b) if isinstance(b, (list, tuple)) else [b]
    na = len(a_list)
    m = a_list[0].shape[0]
    tm = _pick(m, tm_cap, 16)
    dotf = _nn if mode == "nn" else _nt
    n_in = len(tiled) + len(full)
    n_out = len(outs)
    first = 2 * na

    def body(*refs):
        i = pl.program_id(0)
        vals = [r[...] for r in refs[first:first + n_in]]
        acc = dotf(refs[0][...], refs[na][...])
        for p in range(1, na):
            acc = acc + dotf(refs[p][...], refs[na + p][...])
        o_t, o_a = fn(i, acc, *vals)
        for r, v in zip(refs[first + n_in:first + n_in + n_out], o_t):
            r[...] = v.astype(r.dtype)
        for r, v in zip(refs[first + n_in + n_out:], o_a):
            @pl.when(i == 0)
            def _():
                r[...] = v

            @pl.when(i > 0)
            def _():
                r[...] += v

    whole = lambda x: pl.BlockSpec(x.shape, lambda i, nd=x.ndim: (0,) * nd)
    in_specs = [pl.BlockSpec((tm, x.shape[1]), lambda i: (i, 0)) for x in a_list] + [_whole_spec(x) for x in b_list]
    in_specs += [_tile_spec(t, tm, m) for t in tiled]
    in_specs += [whole(x) for x in full]
    out_specs = [pl.BlockSpec((tm, w), lambda i: (i, 0)) for w, _ in outs]
    out_specs += [pl.BlockSpec((r, w), lambda i: (0, 0)) for r, w in accs]
    out_shape = [jax.ShapeDtypeStruct((m, w), dt) for w, dt in outs]
    out_shape += [jax.ShapeDtypeStruct((r, w), F32) for r, w in accs]
    return pl.pallas_call(
        body, name=name, grid=(m // tm,), in_specs=in_specs, out_specs=out_specs, out_shape=out_shape,
        compiler_params=pltpu.CompilerParams(dimension_semantics=("arbitrary",), vmem_limit_bytes=VMEM_LIMIT),
    )(*a_list, *b_list, *[t.arr for t in tiled], *full)


ATTN_Q_TILES = 4


def _attn_probs(q, k, row0):
    tq, tp = q.shape[0], k.shape[0]
    s = _nt(q, k) * (1.0 / math.sqrt(QK_HEAD))
    row = row0 + lax.broadcasted_iota(jnp.int32, (tq, tp), 0)
    col = lax.broadcasted_iota(jnp.int32, (tq, tp), 1)
    ok = (col <= row) & (col >= PAD)
    s = jnp.where(ok, s, NEG)
    m = jnp.max(s, axis=-1, keepdims=True)
    e = jnp.exp(s - m)
    return e * (1.0 / jnp.sum(e, axis=-1, keepdims=True))


def _attn_fwd(q, k, v):
    tp = q.shape[0]
    tq = tp // ATTN_Q_TILES

    def body(q_ref, k_ref, v_ref, o_ref):
        for i in range(ATTN_Q_TILES):
            rows = slice(i * tq, (i + 1) * tq)
            keys = slice(0, (i + 1) * tq)
            p = _attn_probs(q_ref[rows, :], k_ref[keys, :], i * tq)
            o_ref[rows, :] = _nn(p, v_ref[keys, :])

    return pl.pallas_call(
        body, name="attn_fwd", grid=(MLA_HEADS,),
        in_specs=[pl.BlockSpec((tp, HP), lambda h: (0, h)),
                  pl.BlockSpec((tp, HP), lambda h: (0, h)),
                  pl.BlockSpec((tp, V_HEAD), lambda h: (0, h))],
        out_specs=pl.BlockSpec((tp, V_HEAD), lambda h: (0, h)),
        out_shape=jax.ShapeDtypeStruct((tp, MLA_HEADS * V_HEAD), F32),
        compiler_params=pltpu.CompilerParams(dimension_semantics=("parallel",), vmem_limit_bytes=VMEM_LIMIT),
    )(q, k, v)


def _attn_bwd(q, k, v, do):
    tp = q.shape[0]
    tq = tp // ATTN_Q_TILES

    def body(q_ref, k_ref, v_ref, do_ref, dq_ref, dk_ref, dv_ref):
        for i in reversed(range(ATTN_Q_TILES)):
            rows = slice(i * tq, (i + 1) * tq)
            keys = slice(0, (i + 1) * tq)
            qb = q_ref[rows, :]
            kk = k_ref[keys, :]
            dob = do_ref[rows, :]
            p = _attn_probs(qb, kk, i * tq)
            dp = _nt(dob, v_ref[keys, :])
            delta = jnp.sum(p * dp, axis=-1, keepdims=True)
            ds = p * (dp - delta) * (1.0 / math.sqrt(QK_HEAD))
            dq_ref[rows, :] = _nn(ds, kk)
            if i == ATTN_Q_TILES - 1:
                dk_ref[...] = _tn(ds, qb)
                dv_ref[...] = _tn(p, dob)
            else:
                dk_ref[keys, :] += _tn(ds, qb)
                dv_ref[keys, :] += _tn(p, dob)

    full = lambda w: pl.BlockSpec((tp, w), lambda h: (0, h))
    return pl.pallas_call(
        body, name="attn_bwd", grid=(MLA_HEADS,),
        in_specs=[full(HP), full(HP), full(V_HEAD), full(V_HEAD)],
        out_specs=[full(HP), full(HP), full(V_HEAD)],
        out_shape=[jax.ShapeDtypeStruct((tp, MLA_HEADS * HP), F32),
                   jax.ShapeDtypeStruct((tp, MLA_HEADS * HP), F32),
                   jax.ShapeDtypeStruct((tp, MLA_HEADS * V_HEAD), F32)],
        compiler_params=pltpu.CompilerParams(dimension_semantics=("parallel",), vmem_limit_bytes=VMEM_LIMIT),
    )(q, k, v, do)


def _gdn_consts():
    c = DN_CHUNK
    r = lax.broadcasted_iota(jnp.int32, (c, c), 0)
    cc = lax.broadcasted_iota(jnp.int32, (c, c), 1)
    incl = r >= cc
    strict = r > cc
    return incl, strict


def _cumsum_rows(x, reverse=False):
    c = x.shape[0]
    row = lax.broadcasted_iota(jnp.int32, x.shape, 0)
    s = 1
    while s < c:
        if reverse:
            x = x + jnp.where(row < c - s, pltpu.roll(x, c - s, 0), 0.0)
        else:
            x = x + jnp.where(row >= s, pltpu.roll(x, s, 0), 0.0)
        s *= 2
    return x


def _each(fn, *lists):
    return [fn(*a) for a in zip(*lists)]


def _interleave(chains):
    chains = list(chains)
    while chains:
        for ch in list(chains):
            try:
                next(ch)
            except StopIteration:
                chains.remove(ch)


def _gdn_chunk_common(q_ref, k_ref, v_ref, g_ref, b_ref):
    c = DN_CHUNK
    incl, strict = _gdn_consts()
    sls = [(slice(c * sub, c * (sub + 1)), slice(DN_DIM * h, DN_DIM * (h + 1)))
           for sub in range(GDN_SUB_CHUNKS) for h in range(DN_HEADS)]
    q = [q_ref[sl] * (1.0 / math.sqrt(DN_DIM)) for sl in sls]
    k = [k_ref[sl] for sl in sls]
    v = [v_ref[sl] for sl in sls]
    g = [g_ref[sl] for sl in sls]
    beta = [b_ref[sl] for sl in sls]
    gc = [_cumsum_rows(x) for x in g]
    grow = [x.T[:c, :] for x in gc]
    kb = _each(jnp.multiply, k, beta)
    kk = _each(_nt, kb, k)
    qk = _each(_nt, q, k)
    gam = [jnp.exp(x) for x in gc]
    g_last = [_rowsum(x) for x in g]
    dm = [jnp.exp(jnp.where(incl, x[:, :c] - y, NEG)) for x, y in zip(gc, grow)]
    vb = _each(jnp.multiply, v, beta)
    kbg = _each(jnp.multiply, kb, gam)
    ek = [jnp.exp(x - y) for x, y in zip(g_last, gc)]
    kd = _each(jnp.multiply, k, ek)
    return dict(q=q, k=k, v=v, beta=beta, gc=gc, gam=gam, g_last=g_last, dm=dm, kb=kb, vb=vb,
                kbg=kbg, kk=kk, ek=ek, kd=kd, qk=qk, incl=incl, strict=strict, sls=sls)


def _gdn_fwd(proj, conv_w8, alog, dtb):
    tp = proj.shape[0]
    c = DN_CHUNK
    nch = tp // c
    blk = GDN_SUB_CHUNKS * c

    def body(x_ref, xp_ref, ab_ref, w8_ref, alog_ref, dtb_ref,
             o_ref, s_ref, t_ref, q_ref, k_ref, v_ref, g_ref, b_ref, s_scr):
        @pl.when(pl.program_id(0) == 0)
        def _():
            s_scr[...] = jnp.zeros_like(s_scr)

        staged, _ = _f_gdn_prep(pl.program_id(0), x_ref[...], xp_ref[...], ab_ref[...], w8_ref[...],
                                alog_ref[...], dtb_ref[...])
        for ref, val in zip((q_ref, k_ref, v_ref, g_ref, b_ref), staged):
            ref[...] = val
        eye = (lax.broadcasted_iota(jnp.int32, (c, c), 0) == lax.broadcasted_iota(jnp.int32, (c, c), 1)).astype(F32)
        x = _gdn_chunk_common(q_ref, k_ref, v_ref, g_ref, b_ref)
        heads = range(DN_HEADS)
        bp = [-jnp.where(x["strict"], kk * dm, 0.0) for kk, dm in zip(x["kk"], x["dm"])]
        t = [eye + b for b in bp]
        for _ in range(5):
            bp = [_nn(b, b, hp="3x") for b in bp]
            t = [tt + _nn(tt, b, hp="3x") for tt, b in zip(t, bp)]
        u = _each(_nn, t, x["vb"])
        w = _each(_nn, t, x["kbg"])
        qg = _each(jnp.multiply, x["q"], x["gam"])
        mqk = _each(jnp.multiply, x["qk"], x["dm"])
        s = [s_scr[h] for h in heads]
        for sub in range(GDN_SUB_CHUNKS):
            e = [DN_HEADS * sub + h for h in heads]
            v_new = [u[i] - _nn(w[i], s[h]) for h, i in zip(heads, e)]
            o = [_nn(qg[i], s[h]) + _nn(mqk[i], v_new[h]) for h, i in zip(heads, e)]
            s_new = [s[h] * jnp.exp(x["g_last"][i]) + _tn(x["kd"][i], v_new[h]) for h, i in zip(heads, e)]
            for h, i in zip(heads, e):
                s_ref[h, sub] = s[h]
                t_ref[h, sub] = t[i]
                o_ref[x["sls"][i]] = o[h]
            s = s_new
        for h in heads:
            s_scr[h] = s[h]

    sub = GDN_SUB_CHUNKS
    rb = lambda n: (n, 0)
    rows = pl.BlockSpec((blk, DN_WIDTH), rb)
    whole = lambda a: pl.BlockSpec(a.shape, lambda n: (0, 0))
    return pl.pallas_call(
        body, name="gdn_fwd", grid=(nch // sub,),
        in_specs=[pl.BlockSpec((blk, 3 * DN_WIDTH), rb),
                  pl.BlockSpec((8, 3 * DN_WIDTH), lambda n: (jnp.maximum(n * (blk // 8) - 1, 0), 0)),
                  pl.BlockSpec((blk, LANE), lambda n: (n, C_AB // LANE)),
                  whole(conv_w8), whole(alog), whole(dtb)],
        out_specs=[rows,
                   pl.BlockSpec((DN_HEADS, sub, DN_DIM, DN_DIM), lambda n: (0, n, 0, 0)),
                   pl.BlockSpec((DN_HEADS, sub, c, c), lambda n: (0, n, 0, 0))] + [rows] * 5,
        out_shape=[jax.ShapeDtypeStruct((tp, DN_WIDTH), F32),
                   jax.ShapeDtypeStruct((DN_HEADS, nch, DN_DIM, DN_DIM), F32),
                   jax.ShapeDtypeStruct((DN_HEADS, nch, c, c), F32)] + [jax.ShapeDtypeStruct((tp, DN_WIDTH), F32)] * 5,
        scratch_shapes=[pltpu.VMEM((DN_HEADS, DN_DIM, DN_DIM), F32)],
        compiler_params=pltpu.CompilerParams(dimension_semantics=("arbitrary",), vmem_limit_bytes=VMEM_LIMIT),
    )(proj, proj, proj, conv_w8, alog, dtb)


def _gdn_bwd(q, k, v, g, beta, s_all, t_all, do, proj, conv_w8, alog, dtb, after):
    tp = q.shape[0]
    c = DN_CHUNK
    nch = tp // c
    nblk = nch // GDN_SUB_CHUNKS
    blk = GDN_SUB_CHUNKS * c

    def body(q_ref, k_ref, v_ref, g_ref, b_ref, s_ref, t_ref, do_ref, x_ref, xp_ref, xn_ref, ab_ref,
             w8_ref, alog_ref, dtb_ref, _after_ref, dqkv_ref, dab_ref, dcw_ref, dalog_ref, ddtb_ref,
             ds_scr, dq_ref, dk_ref, dv_ref, dg_ref, db_ref, nxt_scr):
        step = pl.program_id(0)

        @pl.when(step == 0)
        def _():
            ds_scr[...] = jnp.zeros_like(ds_scr)
            nxt_scr[...] = jnp.zeros_like(nxt_scr)

        xs = _gdn_chunk_common(q_ref, k_ref, v_ref, g_ref, b_ref)

        ds_state = [ds_scr[h] for h in range(DN_HEADS)]

        def chain(sub, h):
            e = DN_HEADS * sub + h
            x = {key: (val[e] if isinstance(val, list) else val) for key, val in xs.items()}
            sl = x["sls"]
            qs, kx, vx, beta_, gam, dm = x["q"], x["k"], x["v"], x["beta"], x["gam"], x["dm"]
            kb, vb, kbg, kd, ek = x["kb"], x["vb"], x["kbg"], x["kd"], x["ek"]
            t = t_ref[h, sub]
            s = s_ref[h, sub]
            dsn = ds_state[h]
            dob = do_ref[sl]
            eg_last = jnp.exp(x["g_last"])
            u = _nn(t, vb)
            w = _nn(t, kbg)
            mqk = x["qk"] * dm
            qd = qs * gam
            dqd = _nt(dob, s)
            dkd_pre = _nn(kd, dsn)
            yield
            v_new = u - _nn(w, s)
            dv_new = _tn(mqk, dob) + dkd_pre
            dq = dqd * gam
            dgam = jnp.sum(dqd * qs, axis=1, keepdims=True)
            yield
            ds_state[h] = _tn(qd, dob) + eg_last * dsn - _tn(w, dv_new)
            dmm = jnp.where(x["incl"], _nt(dob, v_new), 0.0)
            dkd = _nt(v_new, dsn)
            dw = -_nt(dv_new, s)
            dvb = _tn(t, dv_new)
            dt = _nt(dv_new, vb)
            yield
            dqk = dmm * dm
            e_mat = dmm * mqk
            dq = dq + _nn(dqk, kx)
            dk = _tn(dqk, qs) + dkd * ek
            e1 = jnp.sum(dkd * kd, axis=1, keepdims=True)
            dgc = -e1
            dg_last = jnp.sum(e1) + eg_last * jnp.sum(s * dsn)
            dt = dt + _nt(dw, kbg)
            dkbg = _tn(t, dw)
            yield
            tdt = _tn(t, dt, hp="3x")
            yield
            da = jnp.where(x["strict"], -_nt(tdt, t, hp="3x"), 0.0)
            yield
            dkk = da * dm
            e_mat = e_mat + da * x["kk"] * dm
            dkb = _nn(dkk, kx) + dkbg * gam
            dk = dk + _tn(dkk, kb)
            dgam = dgam + jnp.sum(dkbg * kb, axis=1, keepdims=True)
            yield
            dk = dk + dkb * beta_
            dbeta = jnp.sum(dkb * kx, axis=1, keepdims=True) + jnp.sum(dvb * vx, axis=1, keepdims=True)
            dv = dvb * beta_
            dgc = dgc + jnp.sum(e_mat, axis=1, keepdims=True) + dgam * gam
            dgc = dgc - jnp.sum(e_mat.T, axis=1, keepdims=True)
            yield
            dg = _cumsum_rows(dgc, reverse=True) + dg_last
            yield
            dq_ref[sl] = dq * (1.0 / math.sqrt(DN_DIM))
            dk_ref[sl] = dk
            dv_ref[sl] = dv
            dg_ref[sl] = dg
            db_ref[sl] = jnp.broadcast_to(dbeta, (c, LANE))

        chains = []
        for sub in reversed(range(GDN_SUB_CHUNKS)):
            new = [chain(sub, h) for h in range(DN_HEADS)]
            for _ in range(3):
                for ch in new:
                    next(ch)
            chains += new
        _interleave(chains)
        for h in range(DN_HEADS):
            ds_scr[h] = ds_state[h]

        dq, dk, dv = dq_ref[...], dk_ref[...], dv_ref[...]
        outs, accs = _f_gdn_prep_bwd(
            nblk - 1 - step, x_ref[...], xp_ref[...], xn_ref[...], ab_ref[...], dq, nxt_scr[0], dk, nxt_scr[1],
            dv, nxt_scr[2], dg_ref[...], db_ref[...], w8_ref[...], alog_ref[...], dtb_ref[...], nt=nblk)
        nxt_scr[0] = dq[:8]
        nxt_scr[1] = dk[:8]
        nxt_scr[2] = dv[:8]
        dqkv_ref[...] = outs[0].astype(dqkv_ref.dtype)
        dab_ref[...] = outs[1].astype(dab_ref.dtype)
        for ref, val in zip((dcw_ref, dalog_ref, ddtb_ref), accs):
            @pl.when(step == 0)
            def _():
                ref[...] = val

            @pl.when(step > 0)
            def _():
                ref[...] += val

    sub = GDN_SUB_CHUNKS
    r8 = blk // 8
    rb = lambda n: (nblk - 1 - n, 0)
    hs = lambda n: (0, nblk - 1 - n, 0, 0)
    rows = pl.BlockSpec((blk, DN_WIDTH), rb)
    whole = lambda a: pl.BlockSpec(a.shape, lambda n: (0,) * a.ndim)
    wide = 3 * DN_WIDTH
    return pl.pallas_call(
        body, name="gdn_bwd", grid=(nblk,),
        in_specs=[rows] * 5
        + [pl.BlockSpec((DN_HEADS, sub, DN_DIM, DN_DIM), hs), pl.BlockSpec((DN_HEADS, sub, c, c), hs), rows,
           pl.BlockSpec((blk, wide), rb),
           pl.BlockSpec((8, wide), lambda n: (jnp.maximum((nblk - 1 - n) * r8 - 1, 0), 0)),
           pl.BlockSpec((8, wide), lambda n: (jnp.minimum((nblk - n) * r8, tp // 8 - 1), 0)),
           pl.BlockSpec((blk, LANE), lambda n: (nblk - 1 - n, C_AB // LANE)),
           whole(conv_w8), whole(alog), whole(dtb), _ANY_SPEC],
        out_specs=[pl.BlockSpec((blk, wide), rb), pl.BlockSpec((blk, LANE), rb),
                   whole(conv_w8), whole(alog), whole(dtb)],
        out_shape=[jax.ShapeDtypeStruct((tp, wide), _MXU), jax.ShapeDtypeStruct((tp, LANE), _MXU),
                   jax.ShapeDtypeStruct(conv_w8.shape, F32), jax.ShapeDtypeStruct(alog.shape, F32),
                   jax.ShapeDtypeStruct(dtb.shape, F32)],
        scratch_shapes=[pltpu.VMEM((DN_HEADS, DN_DIM, DN_DIM), F32)] + [pltpu.VMEM((blk, DN_WIDTH), F32)] * 5
        + [pltpu.VMEM((3, 8, DN_WIDTH), F32)],
        compiler_params=pltpu.CompilerParams(dimension_semantics=("arbitrary",), vmem_limit_bytes=VMEM_LIMIT),
    )(q, k, v, g, beta, s_all, t_all, do, proj, proj, proj, proj, conv_w8, alog, dtb, after)


def _silu_parts(x):
    s = _sigmoid(x)
    return x * s, s * (1.0 + x * (1.0 - s))


def _f_rms_bwd_add(i, x, dy, dres, w, *, mask_pad):
    dx, dwr = _rms_bwd(x, w, dy, x.shape[1])
    out = dres + dx
    if mask_pad:
        out = jnp.where(_row_ids(i, x.shape[0]) >= PAD, out, 0.0)
    return (out,), (_rowsum(dwr),)


def _rope(x, cos, sin_s):
    return x * cos + _swap_halves(x) * sin_s


def _rope_t(dy, cos, sin_s):
    return dy * cos + _swap_halves(dy * sin_s)


def _f_mla_qk(i, qf, kvf, kpe, cos, sin_s, qw, kw):
    qs, ks, vs = [], [], []
    for h in range(MLA_HEADS):
        qn, _ = _rms_fwd(qf[:, HP * h:HP * (h + 1)], qw, QK_HEAD)
        qs += [qn[:, :QK_NOPE], _rope(qn[:, QK_NOPE:], cos, sin_s)]
        kh = jnp.concatenate([kvf[:, HP * h:HP * h + QK_NOPE], kpe], axis=1)
        kn, _ = _rms_fwd(kh, kw, QK_HEAD)
        ks += [kn[:, :QK_NOPE], _rope(kn[:, QK_NOPE:], cos, sin_s)]
        vs.append(kvf[:, HP * h + QK_NOPE:HP * (h + 1)])
    return (jnp.concatenate(qs, axis=1), jnp.concatenate(ks, axis=1), jnp.concatenate(vs, axis=1)), ()


def _f_mla_front(i, ql, kvl, kpe, cos, sin_s, qaw, kvaw, wq_t, wkv, qw, kw):
    qn = _rms_fwd(ql, qaw, Q_LORA)[0].astype(_MXU)
    kvn = _rms_fwd(kvl, kvaw, KV_LORA)[0].astype(_MXU)
    qf = _nt(qn, wq_t)
    kvf = _nn(kvn, wkv)
    (q, k, v), _ = _f_mla_qk(i, qf, kvf, kpe, cos, sin_s, qw, kw)
    return (qn, kvn, qf, kvf, q, k, v), ()


def _f_mla_back(i, qf, kvf, kpe, cos, sin_s, dq, dk, dv, ql, kvl, qaw, kvaw, wq_t, wkv, qw, kw):
    (dqf, dkvf, dkpe), (dqw, dkw) = _f_mla_qk_bwd(i, qf, kvf, kpe, cos, sin_s, dq, dk, dv, qw, kw)
    dqf = dqf.astype(_MXU)
    dkvf = dkvf.astype(_MXU)
    dql, dqaw = _rms_bwd(ql, qaw, _nn(dqf, wq_t), Q_LORA)
    dkvl, dkvaw = _rms_bwd(kvl, kvaw, _nt(dkvf, wkv), KV_LORA)
    return (dqf, dkvf, dkpe, dql, dkvl), (dqw, dkw, _rowsum(dqaw), _rowsum(dkvaw))


def _f_mla_qk_bwd(i, qf, kvf, kpe, cos, sin_s, dq, dk, dv, qw, kw):
    dqf, dkvf = [], []
    dkpe = None
    dqw = None
    dkw = None
    for h in range(MLA_HEADS):
        dqh = dq[:, HP * h:HP * (h + 1)]
        dqn = jnp.concatenate([dqh[:, :QK_NOPE], _rope_t(dqh[:, QK_NOPE:], cos, sin_s)], axis=1)
        dx, dwr = _rms_bwd(qf[:, HP * h:HP * (h + 1)], qw, dqn, QK_HEAD)
        dqf.append(dx)
        dqw = _rowsum(dwr) if dqw is None else dqw + _rowsum(dwr)
        dkh = dk[:, HP * h:HP * (h + 1)]
        dkn = jnp.concatenate([dkh[:, :QK_NOPE], _rope_t(dkh[:, QK_NOPE:], cos, sin_s)], axis=1)
        kh = jnp.concatenate([kvf[:, HP * h:HP * h + QK_NOPE], kpe], axis=1)
        dx, dwr = _rms_bwd(kh, kw, dkn, QK_HEAD)
        dkvf += [dx[:, :QK_NOPE], dv[:, V_HEAD * h:V_HEAD * (h + 1)]]
        dkpe = dx[:, QK_NOPE:] if dkpe is None else dkpe + dx[:, QK_NOPE:]
        dkw = _rowsum(dwr) if dkw is None else dkw + _rowsum(dwr)
    return (jnp.concatenate(dqf, axis=1), jnp.concatenate(dkvf, axis=1), dkpe), (dqw, dkw)


def _gdn_act(i, x, halo, w8):
    halo = jnp.where(i > 0, halo, 0.0)
    c = _conv_fwd(x, halo, w8, DN_CONV)
    act, dact = _silu_parts(c)
    return act, dact


def _spread_heads(ab):
    tm = ab.shape[0]
    return jnp.concatenate([jnp.broadcast_to(ab[:, h:h + 1], (tm, DN_DIM)) for h in range(2 * DN_HEADS)], axis=1)


def _gather_heads(x):
    tm = x.shape[0]
    lane = lax.broadcasted_iota(jnp.int32, (tm, LANE), 1)
    out = jnp.zeros((tm, LANE), F32)
    for h in range(2 * DN_HEADS):
        out = out + jnp.where(lane == h, x[:, DN_DIM * h:DN_DIM * h + 1], 0.0)
    return out


def _gate_parts(ab, dtb):
    lane1 = lax.broadcasted_iota(jnp.int32, (1, LANE), 1)
    dtb_c = jnp.zeros((1, LANE), F32)
    for h in range(DN_HEADS):
        dtb_c = dtb_c + jnp.where(lane1 == h, dtb[:, DN_DIM * h:DN_DIM * h + 1], 0.0)
    pre = ab + dtb_c
    sig = _sigmoid(pre)
    lane = lax.broadcasted_iota(jnp.int32, ab.shape, 1)
    return jnp.where(lane < DN_HEADS, _softplus(pre), sig), sig


def _f_gdn_prep(i, x, halo, ab, w8, alog, dtb):
    tm = x.shape[0]
    act, _ = _gdn_act(i, x, halo, w8)
    outs = []
    for part in range(2):
        for h in range(DN_HEADS):
            t = act[:, DN_WIDTH * part + DN_DIM * h:DN_WIDTH * part + DN_DIM * (h + 1)]
            outs.append(t * lax.rsqrt(jnp.sum(t * t, axis=-1, keepdims=True) + EPS))
    q = jnp.concatenate(outs[:DN_HEADS], axis=1)
    k = jnp.concatenate(outs[DN_HEADS:], axis=1)
    v = act[:, 2 * DN_WIDTH:]
    abb = _spread_heads(ab)
    valid = _row_ids(i, tm) >= PAD
    g = jnp.where(valid, -jnp.exp(alog) * _softplus(abb[:, :DN_WIDTH] + dtb), 0.0)
    beta = jnp.where(valid, _sigmoid(abb[:, DN_WIDTH:]), 0.0)
    return (q, k, v, g, beta), ()


def _f_gdn_prep_bwd(i, x, x_prev, x_next, ab, dq, dq_next, dk, dk_next, dv, dv_next, dg, dbeta,
                    w8, alog, dtb, *, nt):
    tm = x.shape[0]
    x_prev = jnp.where(i > 0, x_prev, 0.0)
    more = i < nt - 1
    ext = lambda t, t_next: jnp.concatenate([t, jnp.where(more, t_next, 0.0)], axis=0)
    taps = _conv_taps(jnp.concatenate([x, x_next], axis=0), x_prev, DN_CONV)
    c = _conv_from_taps(taps, w8)
    act, dact = _silu_parts(c)
    douts = []
    for part, dd in enumerate((ext(dq, dq_next), ext(dk, dk_next))):
        for h in range(DN_HEADS):
            t = act[:, DN_WIDTH * part + DN_DIM * h:DN_WIDTH * part + DN_DIM * (h + 1)]
            r = lax.rsqrt(jnp.sum(t * t, axis=-1, keepdims=True) + EPS)
            y = t * r
            dy = dd[:, DN_DIM * h:DN_DIM * (h + 1)]
            douts.append(r * (dy - y * jnp.sum(dy * y, axis=-1, keepdims=True)))
    douts.append(ext(dv, dv_next))
    dc = jnp.concatenate(douts, axis=1) * dact
    dqkv = _conv_bwd_x(dc[:tm], dc[tm:], w8, DN_CONV)
    dconv_w = _conv_bwd_w_taps(dc[:tm], taps)
    sp_beta, sig = _gate_parts(ab, dtb)
    spread = _spread_heads(sp_beta)
    valid = _row_ids(i, tm) >= PAD
    ea = jnp.exp(alog)
    g = -ea * spread[:, :DN_WIDTH]
    dg = jnp.where(valid, dg, 0.0)
    dbeta = jnp.where(valid, dbeta, 0.0)
    da = dg * (-ea) * _spread_heads(sig)[:, :DN_WIDTH]
    beta = spread[:, DN_WIDTH:]
    db = dbeta * beta * (1.0 - beta)
    dab = _gather_heads(jnp.concatenate([da, db], axis=1))
    return (dqkv, dab), (dconv_w, _rowsum(dg * g), _rowsum(da))


def _f_mix(i, o_mla, o_dn, z, w_mla, w_dn):
    tm = o_mla.shape[0]
    valid = _row_ids(i, tm) >= PAD
    outs = []
    for h in range(MLA_HEADS):
        y, _ = _rms_fwd(o_mla[:, V_HEAD * h:V_HEAD * (h + 1)], w_mla, V_HEAD)
        outs.append(jnp.where(valid, y, 0.0))
    for h in range(DN_HEADS):
        y, _ = _rms_fwd(o_dn[:, DN_DIM * h:DN_DIM * (h + 1)], w_dn, DN_DIM)
        outs.append(y * _silu_parts(z[:, DN_DIM * h:DN_DIM * (h + 1)])[0])
    return (jnp.concatenate(outs, axis=1),), ()


def _f_mix_bwd(i, o_mla, o_dn, z, dy_mla, dy_dn, w_mla, w_dn):
    tm = o_mla.shape[0]
    valid = _row_ids(i, tm) >= PAD
    d_mla, d_dn, d_z = [], [], []
    dw_mla = None
    dw_dn = None
    for h in range(MLA_HEADS):
        sl = slice(V_HEAD * h, V_HEAD * (h + 1))
        dx, dwr = _rms_bwd(o_mla[:, sl], w_mla, jnp.where(valid, dy_mla[:, sl], 0.0), V_HEAD)
        d_mla.append(dx)
        dw_mla = _rowsum(dwr) if dw_mla is None else dw_mla + _rowsum(dwr)
    for h in range(DN_HEADS):
        sl = slice(DN_DIM * h, DN_DIM * (h + 1))
        y, _ = _rms_fwd(o_dn[:, sl], w_dn, DN_DIM)
        sz, dsz = _silu_parts(z[:, sl])
        d_z.append(dy_dn[:, sl] * y * dsz)
        dx, dwr = _rms_bwd(o_dn[:, sl], w_dn, dy_dn[:, sl] * sz, DN_DIM)
        d_dn.append(dx)
        dw_dn = _rowsum(dwr) if dw_dn is None else dw_dn + _rowsum(dwr)
    return ((jnp.concatenate(d_mla, axis=1), jnp.concatenate(d_dn, axis=1), jnp.concatenate(d_z, axis=1)),
            (dw_mla, dw_dn))


def _f_ffn_act_bwd(i, gp, gp_prev, gp_next, up, up_next, dact, dact_next, w8, b, *, nt):
    tm = gp.shape[0]
    gp_prev = jnp.where(i > 0, gp_prev, 0.0)
    dact_next = jnp.where(i < nt - 1, dact_next, 0.0)
    cat = lambda t, t_next: jnp.concatenate([t, t_next], axis=0)
    taps = _conv_taps(cat(gp, gp_next), gp_prev, FFN_CONV)
    gate = _conv_from_taps(taps, w8) + b
    sg, dsg = _silu_parts(gate)
    dact_e = cat(dact, dact_next)
    dgate = dact_e * cat(up, up_next) * dsg
    dgate_pre = _conv_bwd_x(dgate[:tm], dgate[tm:], w8, FFN_CONV)
    dup = dact * sg[:tm]
    return (dgate_pre, dup), (_conv_bwd_w_taps(dgate[:tm], taps), _rowsum(dgate[:tm]))


def _f_loss(i, h3, tgt):
    tm = h3.shape[0]
    diff = jnp.where(_row_ids(i, tm) >= ROW0, h3 - tgt, 0.0)
    part = 0.5 * jnp.sum(diff * diff) * (1.0 / D_MODEL)
    return (diff * (1.0 / D_MODEL),), (jnp.full((1, LANE), part, F32),)


def _local_step(h0, tgt, w, token, late_weights, grads_ready):
    tp = h0.shape[0]
    nt = tp // TM
    proj, u = _norm_mm("in_proj", h0, w["attn_norm_w"], w["w_in"], after=token)
    p_qkv = lambda kind="cur": _In(proj, 3 * DN_WIDTH, 0, kind)
    p_z = _In(proj, DN_WIDTH, C_Z // DN_WIDTH)
    p_ql = _In(proj, Q_LORA, C_QL // Q_LORA)
    p_kvl = _In(proj, KV_LORA, C_KVL // KV_LORA)
    p_kpe = _In(proj, LANE, C_KPE // LANE)
    p_ab = _In(proj, LANE, C_AB // LANE)
    cos, sin_s = _In(w["cos"]), _In(w["sin_s"])

    mla_w = [w["q_a_norm_w"], w["kv_a_norm_w"], w["w_q_b"], w["w_kv_b"], w["q_norm_w"], w["k_norm_w"]]
    tm_mla = _pick(tp, 288, 16)
    wide = MLA_HEADS * HP
    qn, kvn, qf, kvf, q, k, v = _rows(
        "mla_front", _f_mla_front, [p_ql, p_kvl, p_kpe, cos, sin_s], mla_w,
        [(Q_LORA, _MXU), (KV_LORA, _MXU), (wide, F32), (wide, F32), (wide, _MXU), (wide, _MXU),
         (MLA_HEADS * V_HEAD, _MXU)], tm=tm_mla)
    o_mla = _attn_fwd(q, k, v)

    dn_w = [w["dn_conv_w"], w["alog_b"], w["dtb_b"]]
    o_dn, s_all, t_all, gq, gk, gv, gg, gb = _gdn_fwd(proj, *dn_w)

    out_w = [w["mla_out_norm_w"], w["dn_out_norm_w"]]
    w = dict(w, **late_weights((o_mla, o_dn), _LATE[:3]))
    h2, mixed = _pro_mm("mix_out_proj", lambda i, *t: _f_mix(i, *t)[0][0], [_In(o_mla), _In(o_dn), p_z], out_w,
                        D_MODEL, w["w_out"], h0)

    ffn_w = [w["ffn_conv_w"], w["ffn_conv_b"]]
    hn, gate_pre, up, act = _ffn_in(h2, w["ffn_norm_w"], w["w_gate"], w["w_up"], *ffn_w)
    w = dict(w, **late_weights(act, _LATE[3:]))
    dh3, loss = _mm_rows("ffn_down_loss", act, w["w_down"], "nn", lambda i, y, r, t: _f_loss(i, r + y, t),
                         [_In(h2), _In(tgt)], [], [(D_MODEL, F32)], [(1, LANE)])

    g = {}
    dact = _mm("ffn_down_dx", dh3, w["w_down"], "nt")
    g["w_down"] = _mm("ffn_down_dw", act, dh3, "tn", out_dtype=_MXU)
    dgate_pre, dup, g["ffn_conv_w"], g["ffn_conv_b"] = _rows(
        "ffn_act_bwd", functools.partial(_f_ffn_act_bwd, nt=nt),
        [_In(gate_pre), _In(gate_pre, kind="prev"), _In(gate_pre, kind="next"), _In(up), _In(up, kind="next"),
         _In(dact), _In(dact, kind="next")], ffn_w,
        [(D_FF, _MXU), (D_FF, _MXU)], [(8, D_FF), (1, D_FF)])
    g["w_gate"], g["w_up"] = _mm_tn2("ffn_gate_up_dw", dgate_pre, dup, hn, out_dtype=_MXU)
    tok = grads_ready(g, ("w_down", "w_gate", "w_up"))
    dh2, g["ffn_norm_w"] = _mm_rows(
        "ffn_gate_up_dx_rms", [dgate_pre, dup], [w["w_gate"], w["w_up"]], "nn",
        lambda i, dy, x, dres, nw, _tok: _f_rms_bwd_add(i, x, dy, dres, nw, mask_pad=True),
        [_In(h2), _In(dh3)], [w["ffn_norm_w"], tok], [(D_MODEL, F32)], [(1, D_MODEL)], tm_cap=288)

    g["w_out"] = _mm("out_proj_dw", mixed, dh2, "tn", out_dtype=_MXU)
    half = MLA_HEADS * V_HEAD
    do_mla, do_dn, dz, g["mla_out_norm_w"], g["dn_out_norm_w"] = _mm_rows(
        "out_proj_dx_mix", dh2, w["w_out"], "nt",
        lambda i, dm, om, od, z, wm, wd: _f_mix_bwd(i, om, od, z, dm[:, :half], dm[:, half:], wm, wd),
        [_In(o_mla), _In(o_dn), p_z], out_w,
        [(half, F32), (DN_WIDTH, F32), (DN_WIDTH, _MXU)], [(1, V_HEAD), (1, DN_DIM)])

    dq, dk, dv = _attn_bwd(q, k, v, do_mla)
    dqf, dkvf, dkpe, dql, dkvl, g["q_norm_w"], g["k_norm_w"], g["q_a_norm_w"], g["kv_a_norm_w"] = _rows(
        "mla_back", _f_mla_back,
        [_In(qf), _In(kvf), p_kpe, cos, sin_s, _In(dq), _In(dk), _In(dv), p_ql, p_kvl], mla_w,
        [(wide, _MXU), (wide, _MXU), (LANE, _MXU), (Q_LORA, _MXU), (KV_LORA, _MXU)],
        [(1, HP), (1, HP), (1, Q_LORA), (1, KV_LORA)], tm=tm_mla)
    g["w_q_b"], g["w_kv_b"] = _mm_tn_pair("mla_b_dw", dqf, qn, kvn, dkvf)
    tok = grads_ready(g, ("w_out", "w_q_b", "w_kv_b"))

    dqkv, dab, g["dn_conv_w"], g["alog_b"], g["dtb_b"] = _gdn_bwd(
        gq, gk, gv, gg, gb, s_all, t_all, do_dn, proj, *dn_w, tok)

    dproj = jnp.concatenate([dqkv, dz, dql, dkvl, dkpe, dab], axis=1)
    g["w_in"] = _mm("in_proj_dw", dproj, u, "tn", out_dtype=_MXU)
    tok = grads_ready(g, ("w_in",))
    dh0, g["attn_norm_w"] = _mm_rows(
        "in_proj_dx_rms", dproj, w["w_in"], "nn",
        lambda i, du, x, dres, nw, _tok: _f_rms_bwd_add(i, x, du, dres, nw, mask_pad=False),
        [_In(h0), _In(dh2)], [w["attn_norm_w"], tok], [(D_MODEL, F32)], [(1, D_MODEL)])
    return loss, dh0, g


def _w_in_to_padded(w):
    c1, c2, c3 = Q_LORA, Q_LORA + KV_LORA, Q_LORA + KV_LORA + QK_ROPE
    c4 = c3 + 3 * DN_WIDTH
    c5 = c4 + DN_WIDTH
    z = lambda n: jnp.zeros((n, w.shape[1]), w.dtype)
    return jnp.concatenate([w[c3:c4], w[c4:c5], w[:c1], w[c1:c2], w[c2:c3], z(LANE - QK_ROPE),
                            w[c5:], z(LANE - 2 * DN_HEADS)], axis=0)


def _w_in_from_padded(g):
    return jnp.concatenate([g[C_QL:C_QL + Q_LORA], g[C_KVL:C_KVL + KV_LORA], g[C_KPE:C_KPE + QK_ROPE],
                            g[:C_Z + DN_WIDTH], g[C_AB:C_AB + 2 * DN_HEADS]], axis=0)


def _w_q_b_to_padded(w):
    r = w.shape[1]
    w = w.reshape(MLA_HEADS, QK_HEAD, r)
    return jnp.pad(w, ((0, 0), (0, HP - QK_HEAD), (0, 0))).reshape(MLA_HEADS * HP, r)


def _w_q_b_from_padded(g):
    r = g.shape[1]
    return g.reshape(MLA_HEADS, HP, r)[:, :QK_HEAD].reshape(MLA_HEADS * QK_HEAD, r)


def _pad_rows8(w):
    return jnp.pad(w, ((0, 8 - w.shape[0]), (0, 0)))


def _prepare(full, tp):
    w = {}
    mx = lambda a: a.astype(_MXU)
    w["attn_norm_w"] = full["attn_norm_w"]
    w["w_in"] = mx(_w_in_to_padded(full["w_in"]))
    w["q_a_norm_w"] = full["q_a_norm_w"]
    w["kv_a_norm_w"] = full["kv_a_norm_w"]
    w["w_q_b"] = mx(_w_q_b_to_padded(full["w_q_b"]))
    w["w_kv_b"] = mx(full["w_kv_b"])
    w["q_norm_w"] = jnp.pad(full["q_norm_w"], ((0, 0), (0, HP - QK_HEAD)))
    w["k_norm_w"] = jnp.pad(full["k_norm_w"], ((0, 0), (0, HP - QK_HEAD)))
    w["mla_out_norm_w"] = full["mla_out_norm_w"]
    w["dn_out_norm_w"] = full["dn_out_norm_w"]
    w["dn_conv_w"] = _pad_rows8(full["dn_conv_w"])
    w["alog_b"] = jnp.repeat(full["dn_A_log"], DN_DIM, axis=1)
    w["dtb_b"] = jnp.repeat(full["dn_dt_bias"], DN_DIM, axis=1)
    w["ffn_norm_w"] = full["ffn_norm_w"]
    w["ffn_conv_w"] = _pad_rows8(full["ffn_conv_w"])
    w["ffn_conv_b"] = full["ffn_conv_b"]
    for n in _LATE:
        if n in full:
            w[n] = mx(full[n])
    half = QK_ROPE // 2
    inv = ROPE_THETA ** (-jnp.arange(half, dtype=F32) / half)
    ang = (jnp.arange(tp, dtype=jnp.int32) - PAD).astype(F32)[:, None] * inv[None, :]
    zc = jnp.zeros((tp, LANE - QK_ROPE), F32)
    w["cos"] = jnp.concatenate([jnp.cos(ang), jnp.cos(ang), zc], axis=1)
    w["sin_s"] = jnp.concatenate([-jnp.sin(ang), jnp.sin(ang), zc], axis=1)
    return w


def _grads_to_natural(g):
    convert = {
        "w_in": ("w_in", _w_in_from_padded),
        "w_q_b": ("w_q_b", _w_q_b_from_padded),
        "q_norm_w": ("q_norm_w", lambda a: a[:, :QK_HEAD]),
        "k_norm_w": ("k_norm_w", lambda a: a[:, :QK_HEAD]),
        "dn_conv_w": ("dn_conv_w", lambda a: a[:DN_CONV]),
        "ffn_conv_w": ("ffn_conv_w", lambda a: a[:FFN_CONV]),
        "alog_b": ("dn_A_log", lambda a: a[:, ::DN_DIM]),
        "dtb_b": ("dn_dt_bias", lambda a: a[:, ::DN_DIM]),
    }
    n = {}
    for key, a in g.items():
        name, fn = convert.get(key, (key, lambda t: t))
        n[name] = fn(a)
    return n


_MESH = pl.DeviceIdType.MESH
_ANY = pl.BlockSpec(memory_space=pl.ANY)
_CHIP_FLIPS = ((1, 0), (0, 1), (1, 1))


def _me():
    return lax.axis_index("x"), lax.axis_index("y"), lax.axis_index("c")


def _all_gather(name, blk, after):
    after = list(after)

    def body(x_ref, *rest):
        out_ref, send_sems, recv_sems, local_sem = rest[len(after):]
        x, y, c = _me()
        me, sib = (x, y, c), (x, y, 1 - c)
        chips = [(x ^ fx, y ^ fy) for fx, fy in _CHIP_FLIPS]

        def slot(p):
            return out_ref.at[4 * p[0] + 2 * p[1] + p[2]]

        def copy(k, block, to, src=None):
            return pltpu.make_async_remote_copy(
                src_ref=slot(block) if src is None else src, dst_ref=slot(block),
                send_sem=send_sems.at[k], recv_sem=recv_sems.at[k], device_id=to, device_id_type=_MESH)

        mine = pltpu.make_async_copy(x_ref, slot(me), local_sem)
        mine.start()
        first = [copy(0, me, sib, src=x_ref)]
        first += [copy(1 + j, me, (*chip, c), src=x_ref) for j, chip in enumerate(chips)]
        for cp in first:
            cp.start()
        passed = [copy(4 + j, (*chip, c), sib) for j, chip in enumerate(chips)]
        for j, chip in enumerate(chips):
            copy(1 + j, (*chip, c), me).wait_recv()
            passed[j].start()
        copy(0, sib, me).wait_recv()
        for j, chip in enumerate(chips):
            copy(4 + j, (*chip, 1 - c), me).wait_recv()
        for cp in first + passed:
            cp.wait_send()
        mine.wait()

    return pl.pallas_call(
        body, name=name, in_specs=[_ANY] * (1 + len(after)), out_specs=_ANY,
        out_shape=jax.ShapeDtypeStruct((N_DEV,) + blk.shape, blk.dtype),
        scratch_shapes=[pltpu.SemaphoreType.DMA((7,)), pltpu.SemaphoreType.DMA((7,)), pltpu.SemaphoreType.DMA],
    )(blk, *after)


def _row_tile(r):
    divs = [d for d in range(16, min(r, 512) + 1, 16) if r % d == 0]
    return divs[-1] if divs else r


def _adam_math(g, w, m, v):
    m_new = ADAM_B1 * m + (1.0 - ADAM_B1) * g
    v_new = ADAM_B2 * v + (1.0 - ADAM_B2) * (g * g)
    m_hat = m_new / (1.0 - ADAM_B1 ** ADAM_STEP)
    v_hat = v_new / (1.0 - ADAM_B2 ** ADAM_STEP)
    return -ADAM_LR * (m_hat / (jnp.sqrt(v_hat) + ADAM_EPS) + ADAM_WD * w), m_new, v_new


def _adam_vectors(name, row, items, ws, ms, vs):
    k = len(items)

    def body(row_ref, *refs):
        w_refs, m_refs, v_refs = refs[:k], refs[k:2 * k], refs[2 * k:3 * k]
        outs = refs[3 * k:]
        for idx, (off, n, per_head) in enumerate(items):
            if per_head:
                spread = row_ref[:, off:off + DN_WIDTH]
                lane = lax.broadcasted_iota(jnp.int32, (1, LANE), 1)
                g = jnp.zeros((1, LANE), F32)
                for h in range(DN_HEADS):
                    g = g + jnp.where(lane == h, spread[:, DN_DIM * h:DN_DIM * h + 1], 0.0)
                g = g[:, :n]
            else:
                g = row_ref[:, off:off + n]
            d, m_new, v_new = _adam_math(g, w_refs[idx][...], m_refs[idx][...], v_refs[idx][...])
            for kind, val in enumerate((g, d, m_new, v_new)):
                outs[kind * k + idx][...] = val

    shapes = [jax.ShapeDtypeStruct((1, n), F32) for _, n, _ in items]
    res = pl.pallas_call(body, name=name, out_shape=shapes * 4)(row, *ws, *ms, *vs)
    return [list(res[kind * k:(kind + 1) * k]) for kind in range(4)]


def _adam_arrays(name, gs, ws, ms, vs):
    k = len(gs)

    def body(*refs):
        outs = refs[4 * k:]
        for idx in range(k):
            res = _adam_math(refs[idx][...], refs[k + idx][...], refs[2 * k + idx][...], refs[3 * k + idx][...])
            for kind, val in enumerate(res):
                outs[kind * k + idx][...] = val

    shapes = [jax.ShapeDtypeStruct(w.shape, F32) for w in ws]
    res = pl.pallas_call(body, name=name, out_shape=shapes * 3)(*gs, *ws, *ms, *vs)
    return [list(res[kind * k:(kind + 1) * k]) for kind in range(3)]


def _sum_parts(name, parts):
    _, r, cols = parts[0][0].shape
    tm = _row_tile(r)
    idx = jnp.stack([jnp.asarray(s, jnp.int32) for _, s in parts])
    n = len(parts)

    def body(idx_ref, *refs):
        g = refs[0][0].astype(F32)
        for p_ref in refs[1:n]:
            g = g + p_ref[0].astype(F32)
        refs[n][...] = g

    return pl.pallas_call(
        body, name=name,
        grid_spec=pltpu.PrefetchScalarGridSpec(
            num_scalar_prefetch=1, grid=(r // tm,),
            in_specs=[pl.BlockSpec((1, tm, cols), lambda i, idx_ref, p=p: (idx_ref[p], i, 0)) for p in range(n)],
            out_specs=pl.BlockSpec((tm, cols), lambda i, idx_ref: (i, 0))),
        out_shape=jax.ShapeDtypeStruct((r, cols), F32),
        compiler_params=pltpu.CompilerParams(dimension_semantics=("parallel",)),
    )(idx, *[a for a, _ in parts])


def _adam(name, parts, w, m, v):
    r, cols = w.shape
    tm = _row_tile(r)
    tc = cols // 4 if (r // tm < 4 and cols % (4 * LANE) == 0) else cols
    idx = jnp.stack([jnp.asarray(s, jnp.int32) for _, s in parts])
    n = len(parts)

    def body(idx_ref, *refs):
        g = refs[0][0].astype(F32)
        for p_ref in refs[1:n]:
            g = g + p_ref[0].astype(F32)
        w_ref, m_ref, v_ref, g_out, d_out, m_out, v_out = refs[n:]
        g_out[...] = g
        d_out[...], m_out[...], v_out[...] = _adam_math(g, w_ref[...], m_ref[...], v_ref[...])

    part_specs = [pl.BlockSpec((1, tm, tc), lambda i, j, idx_ref, p=p: (idx_ref[p], i, j)) for p in range(n)]
    flat = pl.BlockSpec((tm, tc), lambda i, j, idx_ref: (i, j))
    return pl.pallas_call(
        body, name=name,
        grid_spec=pltpu.PrefetchScalarGridSpec(
            num_scalar_prefetch=1, grid=(r // tm, cols // tc), in_specs=part_specs + [flat] * 3,
            out_specs=[flat] * 4),
        out_shape=[jax.ShapeDtypeStruct((r, cols), F32)] * 4,
        compiler_params=pltpu.CompilerParams(dimension_semantics=("parallel", "parallel")),
    )(idx, *[a for a, _ in parts], w, m, v)


def _all_gather_many(name, blks):
    n = len(blks)

    def body(*refs):
        x_refs, out_refs = refs[:n], refs[n:2 * n]
        send_sems, recv_sems, local_sems = refs[2 * n:]
        x, y, c = _me()
        me, sib = (x, y, c), (x, y, 1 - c)
        chips = [(x ^ fx, y ^ fy) for fx, fy in _CHIP_FLIPS]

        def slot(a, p):
            return out_refs[a].at[4 * p[0] + 2 * p[1] + p[2]]

        def copy(a, k, block, to, src=None):
            return pltpu.make_async_remote_copy(
                src_ref=slot(a, block) if src is None else src, dst_ref=slot(a, block),
                send_sem=send_sems.at[7 * a + k], recv_sem=recv_sems.at[7 * a + k], device_id=to,
                device_id_type=_MESH)

        mine = [pltpu.make_async_copy(x_refs[a], slot(a, me), local_sems.at[a]) for a in range(n)]
        first = []
        for a in range(n):
            mine[a].start()
            first.append(copy(a, 0, me, sib, src=x_refs[a]))
            first += [copy(a, 1 + j, me, (*chip, c), src=x_refs[a]) for j, chip in enumerate(chips)]
        for cp in first:
            cp.start()
        passed = []
        for j, chip in enumerate(chips):
            for a in range(n):
                copy(a, 1 + j, (*chip, c), me).wait_recv()
                cp = copy(a, 4 + j, (*chip, c), sib)
                cp.start()
                passed.append(cp)
        for a in range(n):
            copy(a, 0, sib, me).wait_recv()
            for j, chip in enumerate(chips):
                copy(a, 4 + j, (*chip, 1 - c), me).wait_recv()
        for cp in first + passed:
            cp.wait_send()
        for cp in mine:
            cp.wait()

    return pl.pallas_call(
        body, name=name, in_specs=[_ANY] * n, out_specs=[_ANY] * n,
        out_shape=[jax.ShapeDtypeStruct((N_DEV,) + b.shape, b.dtype) for b in blks],
        scratch_shapes=[pltpu.SemaphoreType.DMA((7 * n,)), pltpu.SemaphoreType.DMA((7 * n,)),
                        pltpu.SemaphoreType.DMA((n,))],
    )(*blks)


_HBM = pl.BlockSpec(memory_space=pltpu.HBM)
_SEM = pl.BlockSpec(memory_space=pltpu.SEMAPHORE)
_EFFECT = pltpu.SideEffectType.DATAFLOW_SIDE_EFFECTING


def _push_copies(src_refs, land_refs, send_sems, recv_sems, src_by_peer, first=0):
    x, y, c = _me()
    my_id = 4 * x + 2 * y + c
    out = []
    for k in range(len(src_refs)):
        a = first + k
        for f in range(1, N_DEV):
            px, py, pc = x ^ (f >> 2), y ^ ((f >> 1) & 1), c ^ (f & 1)
            pid = 4 * px + 2 * py + pc
            src = src_refs[k].at[pid] if src_by_peer else src_refs[k]
            start = pltpu.make_async_remote_copy(
                src_ref=src, dst_ref=land_refs[k].at[my_id], send_sem=send_sems.at[7 * a + f - 1],
                recv_sem=recv_sems.at[7 * a + f - 1], device_id=(px, py, pc), device_id_type=_MESH)
            landed = pltpu.make_async_remote_copy(
                src_ref=src, dst_ref=land_refs[k].at[pid], send_sem=send_sems.at[7 * a + f - 1],
                recv_sem=recv_sems.at[7 * a + f - 1], device_id=(px, py, pc), device_id_type=_MESH)
            out.append((start, landed))
    return out


def _push_start(name, srcs, src_by_peer, after):
    n = len(srcs)
    lands = [jax.ShapeDtypeStruct((N_DEV,) + (s.shape[1:] if src_by_peer else s.shape), s.dtype) for s in srcs]

    def body(*refs):
        src_refs, land_refs = refs[:n], refs[n:2 * n]
        send_sems, recv_sems = refs[2 * n + 1], refs[2 * n + 2]
        token = refs[-1]
        for start, _ in _push_copies(src_refs, land_refs, send_sems, recv_sems, src_by_peer):
            start.start()
        token[...] = jnp.zeros_like(token)

    hbm = lambda a: pltpu.with_memory_space_constraint(a, pltpu.HBM)
    res = pl.pallas_call(
        body, name=name,
        out_shape=(pltpu.SemaphoreType.DMA((7 * n,)), pltpu.SemaphoreType.DMA((7 * n,)),
                   *[pltpu.HBM(s.shape, s.dtype) for s in srcs], *[pltpu.HBM(s.shape, s.dtype) for s in lands],
                   jax.ShapeDtypeStruct((8, LANE), F32)),
        in_specs=[_HBM] * (2 * n) + [_ANY],
        out_specs=(_SEM, _SEM, *[_HBM] * (2 * n), pl.BlockSpec(memory_space=pltpu.VMEM)),
        input_output_aliases={i: 2 + i for i in range(2 * n)},
        compiler_params=pltpu.CompilerParams(has_side_effects=_EFFECT),
    )(*[hbm(s) for s in srcs], *[hbm(lax.empty(s.shape, s.dtype)) for s in lands], after)
    return res[0], res[1], list(res[2:2 + n]), list(res[2 + n:2 + 2 * n]), res[-1]


def _push_wait(name, send_sems, recv_sems, srcs, lands, src_by_peer, after, first=0):
    n = len(srcs)
    after = list(after) if isinstance(after, (list, tuple)) else [after]

    def body(*refs):
        src_refs, land_refs = refs[:n], refs[n:2 * n]
        s_sems, r_sems = refs[2 * n], refs[2 * n + 1]
        for _, landed in _push_copies(src_refs, land_refs, s_sems, r_sems, src_by_peer, first):
            landed.wait_send()
            landed.wait_recv()

    res = pl.pallas_call(
        body, name=name,
        out_shape=tuple(pltpu.HBM(s.shape, s.dtype) for s in list(srcs) + list(lands)),
        in_specs=[_HBM] * (2 * n) + [_SEM, _SEM] + [_ANY] * len(after),
        out_specs=tuple([_HBM] * (2 * n)),
        input_output_aliases={i: i for i in range(2 * n)},
        compiler_params=pltpu.CompilerParams(has_side_effects=_EFFECT),
    )(*srcs, *lands, send_sems, recv_sems, *after)
    return list(res[:n]), list(res[n:])


_SHARDED = (
    ("meta_tokens", 1, (N_META, D_MODEL)),
    ("w_in", 1, (D_MODEL, IN_COLS)),
    ("w_q_b", 1, (Q_LORA, MLA_HEADS * QK_HEAD)),
    ("w_kv_b", 1, (KV_LORA, MLA_HEADS * (QK_NOPE + V_HEAD))),
    ("dn_conv_w", 1, (DN_CONV, 3 * DN_WIDTH)),
    ("w_out", 0, (2 * DN_WIDTH, D_MODEL)),
    ("w_gate", 1, (D_MODEL, D_FF)),
    ("w_up", 1, (D_MODEL, D_FF)),
    ("ffn_conv_w", 1, (FFN_CONV, D_FF)),
    ("w_down", 0, (D_FF, D_MODEL)),
)
_F32_GATHERED = ("meta_tokens", "dn_conv_w", "ffn_conv_w")
_EARLY = ("w_in", "w_q_b", "w_kv_b")
_LATE = ("w_out", "w_gate", "w_up", "w_down")
_TRANSPOSED = ("w_in", "w_q_b", "w_gate", "w_up")
_REPLICATED = (
    ("attn_norm_w", D_MODEL), ("q_a_norm_w", Q_LORA), ("kv_a_norm_w", KV_LORA), ("q_norm_w", QK_HEAD),
    ("k_norm_w", QK_HEAD), ("mla_out_norm_w", V_HEAD), ("dn_A_log", DN_HEADS), ("dn_dt_bias", DN_HEADS),
    ("dn_out_norm_w", DN_DIM), ("ffn_norm_w", D_MODEL), ("ffn_conv_b", D_FF),
)
_SMALL_BLOCK = (8, 512)


def _local_shape(dim, shape):
    return (shape[0] // N_DEV, shape[1]) if dim == 0 else (shape[0], shape[1] // N_DEV)


def _from_blocks(blocks, dim, shape):
    r, c = shape
    if dim == 0:
        return blocks.reshape(r, c)
    return blocks.reshape(N_DEV, r, c // N_DEV).transpose(1, 0, 2).reshape(r, c)


def _split(flat, sizes):
    out, o = [], 0
    for s in sizes:
        out.append(flat[..., o:o + s])
        o += s
    return out


def kernel(x, meta_tokens, attn_norm_w, w_in, q_a_norm_w, w_q_b, kv_a_norm_w, w_kv_b, q_norm_w, k_norm_w, mla_out_norm_w, dn_conv_w, dn_A_log, dn_dt_bias, dn_out_norm_w, w_out, ffn_norm_w, w_gate, w_up, ffn_conv_w, ffn_conv_b, w_down, loss_target, m_meta_tokens, m_attn_norm_w, m_w_in, m_q_a_norm_w, m_w_q_b, m_kv_a_norm_w, m_w_kv_b, m_q_norm_w, m_k_norm_w, m_mla_out_norm_w, m_dn_conv_w, m_dn_A_log, m_dn_dt_bias, m_dn_out_norm_w, m_w_out, m_ffn_norm_w, m_w_gate, m_w_up, m_ffn_conv_w, m_ffn_conv_b, m_w_down, v_meta_tokens, v_attn_norm_w, v_w_in, v_q_a_norm_w, v_w_q_b, v_kv_a_norm_w, v_w_kv_b, v_q_norm_w, v_k_norm_w, v_mla_out_norm_w, v_dn_conv_w, v_dn_A_log, v_dn_dt_bias, v_dn_out_norm_w, v_w_out, v_ffn_norm_w, v_w_gate, v_w_up, v_ffn_conv_w, v_ffn_conv_b, v_w_down):
    names = [n for n, _, _ in _SHARDED] + [n for n, _ in _REPLICATED]
    given = dict(locals())
    two_d = lambda a: a.reshape(a.shape[-2:])
    view = lambda a, n: two_d(a).T if n in _TRANSPOSED else two_d(a)
    wl = {n: view(given[n], n) for n in names}
    ml = {n: view(given["m_" + n], n) for n in names}
    vl = {n: view(given["v_" + n], n) for n in names}
    out_shapes = {n: given[n].shape for n in names}

    spec = {n: (d, s) for n, d, s in _SHARDED}
    small_sizes = [math.prod(_local_shape(*spec[n])) for n in _F32_GATHERED]

    def small_block(d):
        cat = jnp.concatenate([d[n].reshape(d[n].shape[:-2] + (-1,)) for n in _F32_GATHERED], axis=-1)
        pad = [(0, 0)] * (cat.ndim - 1) + [(0, math.prod(_SMALL_BLOCK) - cat.shape[-1])]
        return jnp.pad(cat, pad).reshape(cat.shape[:-1] + _SMALL_BLOCK)

    def shard(n):
        return wl[n].astype(_MXU)

    def from_slots(n, blocks):
        d, s = spec[n]
        if d == 0 or n in _TRANSPOSED:
            return blocks.reshape(-1, blocks.shape[-1])
        return blocks.transpose(1, 0, 2).reshape(s)

    my_id = 4 * lax.axis_index("x") + 2 * lax.axis_index("y") + lax.axis_index("c")
    got = _all_gather_many("gather_early", [shard(n) for n in _EARLY] + [small_block(wl)])
    full = {n: a for n, a in wl.items() if n not in _LATE}
    for n, blocks in zip(_EARLY, got):
        full[n] = from_slots(n, blocks)
    for n, p in zip(_F32_GATHERED, _split(got[-1].reshape(N_DEV, -1), small_sizes)):
        full[n] = _from_blocks(p, *spec[n])
    late_own = [shard(n) for n in _LATE]
    l_send, l_recv, l_src, l_land, token = _push_start("gather_late_start", late_own, False, got[-1])

    def late_weights(after, names):
        first = _LATE.index(names[0])
        sl = slice(first, first + len(names))
        _, lands = _push_wait("gather_late_wait_" + names[0], l_send, l_recv, l_src[sl], l_land[sl], False,
                              after, first)
        out = {}
        for n, land, own in zip(names, lands, late_own[sl]):
            out[n] = from_slots(n, lax.dynamic_update_slice(land, own[None], (my_id, 0, 0))).astype(_MXU)
        return out

    def dest_blocks(n, a):
        d, s = spec[n]
        r, c = _local_shape(d, s)
        if n in _TRANSPOSED:
            return a.reshape(N_DEV, c, r)
        return a.reshape(N_DEV, r, c) if d == 0 else a.reshape(r, N_DEV, c).transpose(1, 0, 2)

    pushed = []

    def grads_ready(g, names):
        nat = _grads_to_natural({n: g[n] for n in names})
        blocks = [dest_blocks(n, nat[n]).astype(_MXU) for n in names]
        sends, recvs, srcs, lands, tok = _push_start("rs_" + names[0] + "_start", blocks, True, token)
        pushed.append((names, sends, recvs, srcs, lands))
        return tok

    seq = x.shape[1]
    tp = ROW0 + seq
    h0 = jnp.concatenate([jnp.zeros((PAD, D_MODEL), F32), full["meta_tokens"], x[0]], axis=0)
    tgt = jnp.concatenate([jnp.zeros((ROW0, D_MODEL), F32), loss_target[0]], axis=0)
    loss, dh0, raw = _local_step(h0, tgt, _prepare(full, tp), token, late_weights, grads_ready)
    g = _grads_to_natural(raw)
    g["meta_tokens"] = dh0[PAD:ROW0]
    grad_x = dh0[ROW0:][None]

    big = [{}, {}, {}, {}]

    def finish(group):
        names, sends, recvs, srcs, lands = group
        srcs, lands = _push_wait("rs_" + names[0] + "_wait", sends, recvs, srcs, lands, True, dh0)
        for n, src, land in zip(names, srcs, lands):
            parts = [(src, my_id)] + [(land, my_id ^ f) for f in range(1, N_DEV)]
            for kind, a in enumerate(_adam("adam_" + n, parts, wl[n], ml[n], vl[n])):
                big[kind][n] = a

    for group in pushed[:-1]:
        finish(group)
    rep_names = [n for n, _ in _REPLICATED]
    raw_key = {"dn_A_log": "alog_b", "dn_dt_bias": "dtb_b"}
    pieces = [raw[raw_key.get(n, n)] for n in rep_names] + [loss]
    pieces += [g[n].reshape(1, -1) for n in _F32_GATHERED]
    widths = [p.shape[1] for p in pieces]
    offs = [sum(widths[:k]) for k in range(len(widths))]
    cat = jnp.concatenate(pieces, axis=1)
    cols = -(-cat.shape[1] // (8 * LANE)) * LANE
    mine = jnp.pad(cat, ((0, 0), (0, 8 * cols - cat.shape[1]))).reshape(8, cols)
    everyone = _all_gather("gather_small_grads", mine, [big[1][n] for group in pushed[:-1] for n in group[0]])
    total = _sum_parts("sum_small_grads", [(everyone, d) for d in range(N_DEV)]).reshape(1, 8 * cols)
    tot = {n: total[0, o:o + wd] for n, o, wd in zip(rep_names + ["loss"] + list(_F32_GATHERED), offs, widths)}
    items = [(o, size, n in raw_key) for (n, size), o in zip(_REPLICATED, offs)]
    sm = _adam_vectors("adam_replicated", total, items, [wl[n] for n in rep_names], [ml[n] for n in rep_names],
                       [vl[n] for n in rep_names])
    sm = [dict(zip(rep_names, kind)) for kind in sm]
    mine_of = {}
    for n in _F32_GATHERED:
        d, s = spec[n]
        r, c = _local_shape(d, s)
        mine_of[n] = lax.dynamic_slice(tot[n].reshape(s), (0, my_id * c), (r, c))
    res = _adam_arrays("adam_small_sharded", [mine_of[n] for n in _F32_GATHERED], [wl[n] for n in _F32_GATHERED],
                       [ml[n] for n in _F32_GATHERED], [vl[n] for n in _F32_GATHERED])
    for kind, arrays in enumerate([[mine_of[n] for n in _F32_GATHERED]] + res):
        big[kind].update(zip(_F32_GATHERED, arrays))

    finish(pushed[-1])

    outs = [tot["loss"][0], grad_x]
    for kind in range(4):
        for n in ("meta_tokens", "attn_norm_w", "w_in", "q_a_norm_w", "w_q_b", "kv_a_norm_w", "w_kv_b", "q_norm_w",
                  "k_norm_w", "mla_out_norm_w", "dn_conv_w", "dn_A_log", "dn_dt_bias", "dn_out_norm_w", "w_out",
                  "ffn_norm_w", "w_gate", "w_up", "ffn_conv_w", "ffn_conv_b", "w_down"):
            src = big[kind] if n in big[kind] else sm[kind]
            a = src[n].T if n in _TRANSPOSED else src[n]
            outs.append(a.reshape(out_shapes[n]))
    return tuple(outs)
```

```python
import functools
import math

import jax
import jax.numpy as jnp
from jax import lax
from jax.experimental import pallas as pl
from jax.experimental.pallas import tpu as pltpu

F32 = jnp.float32
_MXU = jnp.bfloat16
_HI = lax.Precision.HIGHEST

D_MODEL = 1024
N_META = 16
PAD = 112
ROW0 = PAD + N_META
MLA_HEADS = 4
QK_NOPE = 128
QK_ROPE = 64
QK_HEAD = QK_NOPE + QK_ROPE
V_HEAD = 128
Q_LORA = 256
KV_LORA = 256
ROPE_THETA = 10000.0
DN_HEADS = 4
DN_DIM = 128
DN_WIDTH = DN_HEADS * DN_DIM
DN_CONV = 4
DN_CHUNK = 64
GDN_SUB_CHUNKS = 2
D_FF = 2816
FFN_CONV = 3
EPS = 1e-6
HP = 256
C_Z = 1536
C_QL = 2048
C_KVL = 2304
C_KPE = 2560
C_AB = 2688
IN_COLS = 2632

ADAM_LR = 0.001
ADAM_B1 = 0.9
ADAM_B2 = 0.999
ADAM_EPS = 1e-08
ADAM_WD = 0.01
ADAM_STEP = 10

N_DEV = 8
TM = 128
LANE = 128
VMEM_LIMIT = 56 * 1024 * 1024
NEG = -1e30


def _dot(a, b, dims, hp=False):
    if hp:
        return lax.dot_general(a.astype(F32), b.astype(F32), (dims, ((), ())),
                               precision=lax.Precision.HIGH if hp == "3x" else _HI, preferred_element_type=F32)
    return lax.dot_general(a.astype(_MXU), b.astype(_MXU), (dims, ((), ())),
                           preferred_element_type=F32)


def _nn(a, b, hp=False):
    return _dot(a, b, ((1,), (0,)), hp)


def _nt(a, b, hp=False):
    return _dot(a, b, ((1,), (1,)), hp)


def _tn(a, b, hp=False):
    return _dot(a, b, ((0,), (0,)), hp)


def _sigmoid(x):
    return 1.0 / (1.0 + jnp.exp(-x))


def _rms_fwd(x, w, n):
    r = lax.rsqrt(jnp.sum(x * x, axis=-1, keepdims=True) * (1.0 / n) + EPS)
    return x * r * w, r


def _rms_bwd(x, w, dy, n):
    r = lax.rsqrt(jnp.sum(x * x, axis=-1, keepdims=True) * (1.0 / n) + EPS)
    xh = x * r
    gy = dy * w
    dx = r * (gy - xh * (jnp.sum(gy * xh, axis=-1, keepdims=True) * (1.0 / n)))
    return dx, dy * xh


def _rowsum(x):
    return jnp.sum(x, axis=0, keepdims=True)


def _row_ids(i, tm):
    return i * tm + lax.broadcasted_iota(jnp.int32, (tm, 1), 0)


def _shift_down(ext, s, tm):
    if s == 0:
        return ext[8:8 + tm]
    return pltpu.roll(ext, s, 0)[8:8 + tm]


def _shift_up(ext, s, tm):
    if s == 0:
        return ext[0:tm]
    return pltpu.roll(ext, tm + 8 - s, 0)[0:tm]


def _conv_taps(x, halo_prev, width):
    tm = x.shape[0]
    ext = jnp.concatenate([halo_prev, x], axis=0)
    return [_shift_down(ext, width - 1 - j, tm) for j in range(width)]


def _conv_from_taps(taps, w):
    y = None
    for j, tap in enumerate(taps):
        t = w[j:j + 1, :] * tap
        y = t if y is None else y + t
    return y


def _conv_fwd(x, halo_prev, w, width):
    return _conv_from_taps(_conv_taps(x, halo_prev, width), w)


def _conv_bwd_w_taps(dy, taps):
    tm = dy.shape[0]
    rows = [_rowsum(dy * tap[:tm]) for tap in taps]
    rows += [jnp.zeros_like(rows[0])] * (8 - len(taps))
    return jnp.concatenate(rows, axis=0)


def _conv_bwd_x(dy, halo_next, w, width):
    tm = dy.shape[0]
    ext = jnp.concatenate([dy, halo_next], axis=0)
    dx = None
    for j in range(width):
        t = w[j:j + 1, :] * _shift_up(ext, width - 1 - j, tm)
        dx = t if dx is None else dx + t
    return dx


def _softplus(x):
    e = jnp.exp(-jnp.abs(x))
    u = 1.0 + e
    l1p = jnp.where(u == 1.0, e, jnp.log(u) * e / jnp.where(u == 1.0, 1.0, u - 1.0))
    return jnp.maximum(x, 0.0) + l1p


def _swap_halves(x):
    lane = lax.broadcasted_iota(jnp.int32, x.shape, 1)
    return jnp.where(lane < 32, pltpu.roll(x, 96, 1), jnp.where(lane < 64, pltpu.roll(x, 32, 1), 0.0))


class _In:
    def __init__(self, arr, width=None, cb=0, kind="cur"):
        self.arr, self.kind = arr, kind
        self.width = arr.shape[1] if width is None else width
        self.cb = cb


def _whole_spec(x):
    return pl.BlockSpec(x.shape, lambda i, nd=x.ndim: (0,) * nd, pipeline_mode=pl.Buffered(1))


def _tile_spec(t, tm, tp):
    r8 = tm // 8
    if t.kind == "cur":
        return pl.BlockSpec((tm, t.width), lambda i, cb=t.cb: (i, cb))
    if t.kind == "prev":
        return pl.BlockSpec((8, t.width), lambda i, cb=t.cb: (jnp.maximum(i * r8 - 1, 0), cb))
    return pl.BlockSpec((8, t.width), lambda i, cb=t.cb: (jnp.minimum((i + 1) * r8, tp // 8 - 1), cb))


def _rows(name, fn, tiled, full, outs, accs=(), tm=TM):
    tp = tiled[0].arr.shape[0]
    nt = tp // tm
    n_in = len(tiled) + len(full)
    n_out = len(outs)

    def body(*refs):
        i = pl.program_id(0)
        vals = [r[...] for r in refs[:n_in]]
        o_t, o_a = fn(i, *vals)
        for r, v in zip(refs[n_in:n_in + n_out], o_t):
            r[...] = v.astype(r.dtype)
        for r, v in zip(refs[n_in + n_out:], o_a):
            @pl.when(i == 0)
            def _():
                r[...] = v

            @pl.when(i > 0)
            def _():
                r[...] += v

    in_specs = [_tile_spec(t, tm, tp) for t in tiled]
    in_specs += [pl.BlockSpec(a.shape, lambda i, nd=a.ndim: (0,) * nd) for a in full]
    out_specs = [pl.BlockSpec((tm, w), lambda i: (i, 0)) for w, _ in outs]
    out_specs += [pl.BlockSpec((r, w), lambda i: (0, 0)) for r, w in accs]
    out_shape = [jax.ShapeDtypeStruct((tp, w), dt) for w, dt in outs]
    out_shape += [jax.ShapeDtypeStruct((r, w), F32) for r, w in accs]
    res = pl.pallas_call(
        body, name=name, grid=(nt,), in_specs=in_specs, out_specs=out_specs, out_shape=out_shape,
        compiler_params=pltpu.CompilerParams(dimension_semantics=("arbitrary",), vmem_limit_bytes=VMEM_LIMIT),
    )(*[t.arr for t in tiled], *full)
    return res


def _pick(n, cap, mult):
    best = None
    for d in range(mult, min(n, cap) + 1, mult):
        if n % d == 0:
            best = d
    assert best is not None, (n, cap, mult)
    return best


_ANY_SPEC = pl.BlockSpec(memory_space=pl.ANY)


def _mm(name, a, b, mode, out_dtype=F32, resid=None, after=None):
    if mode == "tn":
        m, k = a.shape
        n = b.shape[1]
        tk = _pick(k, 512, 128)
        tn = _pick(n, 1408, 128)

        def body_tn(a_ref, b_ref, o_ref):
            o_ref[...] = _tn(a_ref[...], b_ref[...]).astype(o_ref.dtype)

        return pl.pallas_call(
            body_tn, name=name, grid=(n // tn, k // tk),
            in_specs=[pl.BlockSpec((m, tk), lambda j, p: (0, p)),
                      pl.BlockSpec((m, tn), lambda j, p: (0, j))],
            out_specs=pl.BlockSpec((tk, tn), lambda j, p: (p, j)),
            out_shape=jax.ShapeDtypeStruct((k, n), out_dtype),
            compiler_params=pltpu.CompilerParams(
                dimension_semantics=("parallel", "parallel"), vmem_limit_bytes=VMEM_LIMIT),
        )(a, b)

    m, k = a.shape
    n = b.shape[1] if mode == "nn" else b.shape[0]
    tn = _pick(n, 1408, 128)
    tm = _pick(m, 1152, 16)
    dotf = _nn if mode == "nn" else _nt

    def body(*refs):
        a_ref, b_ref, o_ref = refs[0], refs[1], refs[-1]
        acc = dotf(a_ref[...], b_ref[...])
        if resid is not None:
            acc = refs[2][...] + acc
        o_ref[...] = acc.astype(o_ref.dtype)

    b_spec = (pl.BlockSpec((k, tn), lambda j, i: (0, j)) if mode == "nn"
              else pl.BlockSpec((tn, k), lambda j, i: (j, 0)))
    in_specs = [pl.BlockSpec((tm, k), lambda j, i: (i, 0)), b_spec]
    args = [a, b]
    if resid is not None:
        in_specs.append(pl.BlockSpec((tm, tn), lambda j, i: (i, j)))
        args.append(resid)
    if after is not None:
        in_specs.append(_ANY_SPEC)
        args.append(after)
    return pl.pallas_call(
        body, name=name, grid=(n // tn, m // tm), in_specs=in_specs,
        out_specs=pl.BlockSpec((tm, tn), lambda j, i: (i, j)),
        out_shape=jax.ShapeDtypeStruct((m, n), out_dtype),
        compiler_params=pltpu.CompilerParams(
            dimension_semantics=("parallel", "parallel"), vmem_limit_bytes=VMEM_LIMIT),
    )(*args)


def _mm_tn2(name, a1, a2, b, out_dtype=F32):
    m, k = a1.shape
    n = b.shape[1]
    tk = _pick(k, 512, 128)

    def body(a1_ref, a2_ref, b_ref, o1_ref, o2_ref):
        bb = b_ref[...]
        o1_ref[...] = _tn(a1_ref[...], bb).astype(o1_ref.dtype)
        o2_ref[...] = _tn(a2_ref[...], bb).astype(o2_ref.dtype)

    a_spec = pl.BlockSpec((m, tk), lambda p: (0, p))
    o_spec = pl.BlockSpec((tk, n), lambda p: (p, 0))
    return pl.pallas_call(
        body, name=name, grid=(k // tk,),
        in_specs=[a_spec, a_spec, pl.BlockSpec((m, n), lambda p: (0, 0))],
        out_specs=[o_spec, o_spec], out_shape=[jax.ShapeDtypeStruct((k, n), out_dtype)] * 2,
        compiler_params=pltpu.CompilerParams(dimension_semantics=("parallel",), vmem_limit_bytes=VMEM_LIMIT),
    )(a1, a2, b)


def _mm_tn_pair(name, a1, b1, a2, b2):
    def body(a1_ref, b1_ref, a2_ref, b2_ref, o1_ref, o2_ref):
        o1_ref[...] = _tn(a1_ref[...], b1_ref[...])
        o2_ref[...] = _tn(a2_ref[...], b2_ref[...])

    return pl.pallas_call(
        body, name=name,
        out_shape=[jax.ShapeDtypeStruct((a1.shape[1], b1.shape[1]), F32),
                   jax.ShapeDtypeStruct((a2.shape[1], b2.shape[1]), F32)],
        compiler_params=pltpu.CompilerParams(vmem_limit_bytes=VMEM_LIMIT),
    )(a1, b1, a2, b2)


def _norm_mm(name, x, norm_w, b, mode="nt", x_cb=0, after=None):
    m = x.shape[0]
    k = norm_w.shape[1]
    n = b.shape[0] if mode == "nt" else b.shape[1]
    tn = _pick(n, 1408, 128)
    tm = _pick(m, 1152, 16)
    dotf = _nt if mode == "nt" else _nn
    extra = [] if after is None else [after]

    def body(x_ref, w_ref, b_ref, *rest):
        o_ref, u_ref = rest[-2:]

        @pl.when(pl.program_id(1) == 0)
        def _():
            u_ref[...] = _rms_fwd(x_ref[...], w_ref[...], k)[0].astype(u_ref.dtype)

        o_ref[...] = dotf(u_ref[...], b_ref[...])

    b_spec = (pl.BlockSpec((tn, k), lambda i, j: (j, 0)) if mode == "nt"
              else pl.BlockSpec((k, tn), lambda i, j: (0, j)))
    return pl.pallas_call(
        body, name=name, grid=(m // tm, n // tn),
        in_specs=[pl.BlockSpec((tm, k), lambda i, j: (i, x_cb)), pl.BlockSpec((1, k), lambda i, j: (0, 0)),
                  b_spec] + [_ANY_SPEC] * len(extra),
        out_specs=[pl.BlockSpec((tm, tn), lambda i, j: (i, j)), pl.BlockSpec((tm, k), lambda i, j: (i, 0))],
        out_shape=[jax.ShapeDtypeStruct((m, n), F32), jax.ShapeDtypeStruct((m, k), _MXU)],
        compiler_params=pltpu.CompilerParams(
            dimension_semantics=("arbitrary", "arbitrary"), vmem_limit_bytes=VMEM_LIMIT),
    )(x, norm_w, b, *extra)


def _pro_mm(name, fn, tiled, full, k, b, resid):
    m = resid.shape[0]
    n = b.shape[1]
    tm = _pick(m, 576, 16)
    n_in = len(tiled) + len(full)

    def body(*refs):
        i = pl.program_id(0)
        u = fn(i, *[r[...] for r in refs[:n_in]]).astype(_MXU)
        b_ref, r_ref, o_ref, u_ref = refs[n_in:]
        u_ref[...] = u
        o_ref[...] = r_ref[...] + _nn(u, b_ref[...])

    row = lambda w: pl.BlockSpec((tm, w), lambda i: (i, 0))
    in_specs = [_tile_spec(t, tm, m) for t in tiled]
    in_specs += [_whole_spec(x) for x in full] + [_whole_spec(b), row(n)]
    return pl.pallas_call(
        body, name=name, grid=(m // tm,), in_specs=in_specs, out_specs=[row(n), row(k)],
        out_shape=[jax.ShapeDtypeStruct((m, n), F32), jax.ShapeDtypeStruct((m, k), _MXU)],
        compiler_params=pltpu.CompilerParams(dimension_semantics=("parallel",), vmem_limit_bytes=VMEM_LIMIT),
    )(*[t.arr for t in tiled], *full, b, resid)


def _ffn_in(h2, norm_w, w_gate_t, w_up_t, conv_w8, conv_b):
    m, k = h2.shape
    n = w_gate_t.shape[0]
    tm = _pick(m, 576, 16)
    tn = _pick(n, 1408, 128)

    def body(x_ref, xp_ref, nw_ref, wg_ref, wu_ref, cw_ref, cb_ref, hn_ref, gp_ref, up_ref, act_ref):
        i = pl.program_id(0)
        nw = nw_ref[...]

        @pl.when(pl.program_id(1) == 0)
        def _():
            hn_ref[...] = _rms_fwd(x_ref[...], nw, k)[0].astype(hn_ref.dtype)

        hn = hn_ref[...]
        hn_prev = _rms_fwd(xp_ref[...], nw, k)[0].astype(_MXU)
        wg = wg_ref[...]
        gp = _nt(hn, wg)
        gp_prev = jnp.where(i > 0, _nt(hn_prev, wg), 0.0)
        up = _nt(hn, wu_ref[...])
        gate = _conv_fwd(gp, gp_prev, cw_ref[...], FFN_CONV) + cb_ref[...]
        gp_ref[...] = gp
        up_ref[...] = up
        act_ref[...] = (_silu_parts(gate)[0] * up).astype(act_ref.dtype)

    r8 = tm // 8
    tile = pl.BlockSpec((tm, tn), lambda i, j: (i, j))
    wblk = pl.BlockSpec((tn, k), lambda i, j: (j, 0))
    return pl.pallas_call(
        body, name="ffn_in", grid=(m // tm, n // tn),
        in_specs=[pl.BlockSpec((tm, k), lambda i, j: (i, 0)),
                  pl.BlockSpec((8, k), lambda i, j: (jnp.maximum(i * r8 - 1, 0), 0)),
                  pl.BlockSpec((1, k), lambda i, j: (0, 0)), wblk, wblk,
                  pl.BlockSpec((8, tn), lambda i, j: (0, j)), pl.BlockSpec((1, tn), lambda i, j: (0, j))],
        out_specs=[pl.BlockSpec((tm, k), lambda i, j: (i, 0)), tile, tile, tile],
        out_shape=[jax.ShapeDtypeStruct((m, k), _MXU), jax.ShapeDtypeStruct((m, n), F32),
                   jax.ShapeDtypeStruct((m, n), F32), jax.ShapeDtypeStruct((m, n), _MXU)],
        compiler_params=pltpu.CompilerParams(
            dimension_semantics=("arbitrary", "arbitrary"), vmem_limit_bytes=VMEM_LIMIT),
    )(h2, h2, norm_w, w_gate_t, w_up_t, conv_w8, conv_b)


def _mm_rows(name, a, b, mode, fn, tiled, full, outs, accs=(), tm_cap=576):
    a_list = list(a) if isinstance(a, (list, tuple)) else [a]
    b_list = list(b) if isinstance(b, (list, tuple)) else [b]
    na = len(a_list)
    m = a_list[0].shape[0]
    tm = _pick(m, tm_cap, 16)
    dotf = _nn if mode == "nn" else _nt
    n_in = len(tiled) + len(full)
    n_out = len(outs)
    first = 2 * na

    def body(*refs):
        i = pl.program_id(0)
        vals = [r[...] for r in refs[first:first + n_in]]
        acc = dotf(refs[0][...], refs[na][...])
        for p in range(1, na):
            acc = acc + dotf(refs[p][...], refs[na + p][...])
        o_t, o_a = fn(i, acc, *vals)
        for r, v in zip(refs[first + n_in:first + n_in + n_out], o_t):
            r[...] = v.astype(r.dtype)
        for r, v in zip(refs[first + n_in + n_out:], o_a):
            @pl.when(i == 0)
            def _():
                r[...] = v

            @pl.when(i > 0)
            def _():
                r[...] += v

    whole = lambda x: pl.BlockSpec(x.shape, lambda i, nd=x.ndim: (0,) * nd)
    in_specs = [pl.BlockSpec((tm, x.shape[1]), lambda i: (i, 0)) for x in a_list] + [_whole_spec(x) for x in b_list]
    in_specs += [_tile_spec(t, tm, m) for t in tiled]
    in_specs += [whole(x) for x in full]
    out_specs = [pl.BlockSpec((tm, w), lambda i: (i, 0)) for w, _ in outs]
    out_specs += [pl.BlockSpec((r, w), lambda i: (0, 0)) for r, w in accs]
    out_shape = [jax.ShapeDtypeStruct((m, w), dt) for w, dt in outs]
    out_shape += [jax.ShapeDtypeStruct((r, w), F32) for r, w in accs]
    return pl.pallas_call(
        body, name=name, grid=(m // tm,), in_specs=in_specs, out_specs=out_specs, out_shape=out_shape,
        compiler_params=pltpu.CompilerParams(dimension_semantics=("arbitrary",), vmem_limit_bytes=VMEM_LIMIT),
    )(*a_list, *b_list, *[t.arr for t in tiled], *full)


ATTN_Q_TILES = 4


def _attn_probs(q, k, row0):
    tq, tp = q.shape[0], k.shape[0]
    s = _nt(q, k) * (1.0 / math.sqrt(QK_HEAD))
    row = row0 + lax.broadcasted_iota(jnp.int32, (tq, tp), 0)
    col = lax.broadcasted_iota(jnp.int32, (tq, tp), 1)
    ok = (col <= row) & (col >= PAD)
    s = jnp.where(ok, s, NEG)
    m = jnp.max(s, axis=-1, keepdims=True)
    e = jnp.exp(s - m)
    return e * (1.0 / jnp.sum(e, axis=-1, keepdims=True))


def _attn_fwd(q, k, v):
    tp = q.shape[0]
    tq = tp // ATTN_Q_TILES

    def body(q_ref, k_ref, v_ref, o_ref):
        for i in range(ATTN_Q_TILES):
            rows = slice(i * tq, (i + 1) * tq)
            keys = slice(0, (i + 1) * tq)
            p = _attn_probs(q_ref[rows, :], k_ref[keys, :], i * tq)
            o_ref[rows, :] = _nn(p, v_ref[keys, :])

    return pl.pallas_call(
        body, name="attn_fwd", grid=(MLA_HEADS,),
        in_specs=[pl.BlockSpec((tp, HP), lambda h: (0, h)),
                  pl.BlockSpec((tp, HP), lambda h: (0, h)),
                  pl.BlockSpec((tp, V_HEAD), lambda h: (0, h))],
        out_specs=pl.BlockSpec((tp, V_HEAD), lambda h: (0, h)),
        out_shape=jax.ShapeDtypeStruct((tp, MLA_HEADS * V_HEAD), F32),
        compiler_params=pltpu.CompilerParams(dimension_semantics=("parallel",), vmem_limit_bytes=VMEM_LIMIT),
    )(q, k, v)


def _attn_bwd(q, k, v, do):
    tp = q.shape[0]
    tq = tp // ATTN_Q_TILES

    def body(q_ref, k_ref, v_ref, do_ref, dq_ref, dk_ref, dv_ref):
        for i in reversed(range(ATTN_Q_TILES)):
            rows = slice(i * tq, (i + 1) * tq)
            keys = slice(0, (i + 1) * tq)
            qb = q_ref[rows, :]
            kk = k_ref[keys, :]
            dob = do_ref[rows, :]
            p = _attn_probs(qb, kk, i * tq)
            dp = _nt(dob, v_ref[keys, :])
            delta = jnp.sum(p * dp, axis=-1, keepdims=True)
            ds = p * (dp - delta) * (1.0 / math.sqrt(QK_HEAD))
            dq_ref[rows, :] = _nn(ds, kk)
            if i == ATTN_Q_TILES - 1:
                dk_ref[...] = _tn(ds, qb)
                dv_ref[...] = _tn(p, dob)
            else:
                dk_ref[keys, :] += _tn(ds, qb)
                dv_ref[keys, :] += _tn(p, dob)

    full = lambda w: pl.BlockSpec((tp, w), lambda h: (0, h))
    return pl.pallas_call(
        body, name="attn_bwd", grid=(MLA_HEADS,),
        in_specs=[full(HP), full(HP), full(V_HEAD), full(V_HEAD)],
        out_specs=[full(HP), full(HP), full(V_HEAD)],
        out_shape=[jax.ShapeDtypeStruct((tp, MLA_HEADS * HP), F32),
                   jax.ShapeDtypeStruct((tp, MLA_HEADS * HP), F32),
                   jax.ShapeDtypeStruct((tp, MLA_HEADS * V_HEAD), F32)],
        compiler_params=pltpu.CompilerParams(dimension_semantics=("parallel",), vmem_limit_bytes=VMEM_LIMIT),
    )(q, k, v, do)


def _gdn_consts():
    c = DN_CHUNK
    r = lax.broadcasted_iota(jnp.int32, (c, c), 0)
    cc = lax.broadcasted_iota(jnp.int32, (c, c), 1)
    incl = r >= cc
    strict = r > cc
    return incl, strict


def _cumsum_rows(x, reverse=False):
    c = x.shape[0]
    row = lax.broadcasted_iota(jnp.int32, x.shape, 0)
    s = 1
    while s < c:
        if reverse:
            x = x + jnp.where(row < c - s, pltpu.roll(x, c - s, 0), 0.0)
        else:
            x = x + jnp.where(row >= s, pltpu.roll(x, s, 0), 0.0)
        s *= 2
    return x


def _each(fn, *lists):
    return [fn(*a) for a in zip(*lists)]


def _interleave(chains):
    chains = list(chains)
    while chains:
        for ch in list(chains):
            try:
                next(ch)
            except StopIteration:
                chains.remove(ch)


def _gdn_chunk_common(q_ref, k_ref, v_ref, g_ref, b_ref):
    c = DN_CHUNK
    incl, strict = _gdn_consts()
    sls = [(slice(c * sub, c * (sub + 1)), slice(DN_DIM * h, DN_DIM * (h + 1)))
           for sub in range(GDN_SUB_CHUNKS) for h in range(DN_HEADS)]
    q = [q_ref[sl] * (1.0 / math.sqrt(DN_DIM)) for sl in sls]
    k = [k_ref[sl] for sl in sls]
    v = [v_ref[sl] for sl in sls]
    g = [g_ref[sl] for sl in sls]
    beta = [b_ref[sl] for sl in sls]
    gc = [_cumsum_rows(x) for x in g]
    grow = [x.T[:c, :] for x in gc]
    kb = _each(jnp.multiply, k, beta)
    kk = _each(_nt, kb, k)
    qk = _each(_nt, q, k)
    gam = [jnp.exp(x) for x in gc]
    g_last = [_rowsum(x) for x in g]
    dm = [jnp.exp(jnp.where(incl, x[:, :c] - y, NEG)) for x, y in zip(gc, grow)]
    vb = _each(jnp.multiply, v, beta)
    kbg = _each(jnp.multiply, kb, gam)
    ek = [jnp.exp(x - y) for x, y in zip(g_last, gc)]
    kd = _each(jnp.multiply, k, ek)
    return dict(q=q, k=k, v=v, beta=beta, gc=gc, gam=gam, g_last=g_last, dm=dm, kb=kb, vb=vb,
                kbg=kbg, kk=kk, ek=ek, kd=kd, qk=qk, incl=incl, strict=strict, sls=sls)


def _gdn_fwd(proj, conv_w8, alog, dtb):
    tp = proj.shape[0]
    c = DN_CHUNK
    nch = tp // c
    blk = GDN_SUB_CHUNKS * c

    def body(x_ref, xp_ref, ab_ref, w8_ref, alog_ref, dtb_ref,
             o_ref, s_ref, t_ref, q_ref, k_ref, v_ref, g_ref, b_ref, s_scr):
        @pl.when(pl.program_id(0) == 0)
        def _():
            s_scr[...] = jnp.zeros_like(s_scr)

        staged, _ = _f_gdn_prep(pl.program_id(0), x_ref[...], xp_ref[...], ab_ref[...], w8_ref[...],
                                alog_ref[...], dtb_ref[...])
        for ref, val in zip((q_ref, k_ref, v_ref, g_ref, b_ref), staged):
            ref[...] = val
        eye = (lax.broadcasted_iota(jnp.int32, (c, c), 0) == lax.broadcasted_iota(jnp.int32, (c, c), 1)).astype(F32)
        x = _gdn_chunk_common(q_ref, k_ref, v_ref, g_ref, b_ref)
        heads = range(DN_HEADS)
        bp = [-jnp.where(x["strict"], kk * dm, 0.0) for kk, dm in zip(x["kk"], x["dm"])]
        t = [eye + b for b in bp]
        for _ in range(5):
            bp = [_nn(b, b, hp="3x") for b in bp]
            t = [tt + _nn(tt, b, hp="3x") for tt, b in zip(t, bp)]
        u = _each(_nn, t, x["vb"])
        w = _each(_nn, t, x["kbg"])
        qg = _each(jnp.multiply, x["q"], x["gam"])
        mqk = _each(jnp.multiply, x["qk"], x["dm"])
        s = [s_scr[h] for h in heads]
        for sub in range(GDN_SUB_CHUNKS):
            e = [DN_HEADS * sub + h for h in heads]
            v_new = [u[i] - _nn(w[i], s[h]) for h, i in zip(heads, e)]
            o = [_nn(qg[i], s[h]) + _nn(mqk[i], v_new[h]) for h, i in zip(heads, e)]
            s_new = [s[h] * jnp.exp(x["g_last"][i]) + _tn(x["kd"][i], v_new[h]) for h, i in zip(heads, e)]
            for h, i in zip(heads, e):
                s_ref[h, sub] = s[h]
                t_ref[h, sub] = t[i]
                o_ref[x["sls"][i]] = o[h]
            s = s_new
        for h in heads:
            s_scr[h] = s[h]

    sub = GDN_SUB_CHUNKS
    rb = lambda n: (n, 0)
    rows = pl.BlockSpec((blk, DN_WIDTH), rb)
    whole = lambda a: pl.BlockSpec(a.shape, lambda n: (0, 0))
    return pl.pallas_call(
        body, name="gdn_fwd", grid=(nch // sub,),
        in_specs=[pl.BlockSpec((blk, 3 * DN_WIDTH), rb),
                  pl.BlockSpec((8, 3 * DN_WIDTH), lambda n: (jnp.maximum(n * (blk // 8) - 1, 0), 0)),
                  pl.BlockSpec((blk, LANE), lambda n: (n, C_AB // LANE)),
                  whole(conv_w8), whole(alog), whole(dtb)],
        out_specs=[rows,
                   pl.BlockSpec((DN_HEADS, sub, DN_DIM, DN_DIM), lambda n: (0, n, 0, 0)),
                   pl.BlockSpec((DN_HEADS, sub, c, c), lambda n: (0, n, 0, 0))] + [rows] * 5,
        out_shape=[jax.ShapeDtypeStruct((tp, DN_WIDTH), F32),
                   jax.ShapeDtypeStruct((DN_HEADS, nch, DN_DIM, DN_DIM), F32),
                   jax.ShapeDtypeStruct((DN_HEADS, nch, c, c), F32)] + [jax.ShapeDtypeStruct((tp, DN_WIDTH), F32)] * 5,
        scratch_shapes=[pltpu.VMEM((DN_HEADS, DN_DIM, DN_DIM), F32)],
        compiler_params=pltpu.CompilerParams(dimension_semantics=("arbitrary",), vmem_limit_bytes=VMEM_LIMIT),
    )(proj, proj, proj, conv_w8, alog, dtb)


def _gdn_bwd(q, k, v, g, beta, s_all, t_all, do, proj, conv_w8, alog, dtb, after):
    tp = q.shape[0]
    c = DN_CHUNK
    nch = tp // c
    nblk = nch // GDN_SUB_CHUNKS
    blk = GDN_SUB_CHUNKS * c

    def body(q_ref, k_ref, v_ref, g_ref, b_ref, s_ref, t_ref, do_ref, x_ref, xp_ref, xn_ref, ab_ref,
             w8_ref, alog_ref, dtb_ref, _after_ref, dqkv_ref, dab_ref, dcw_ref, dalog_ref, ddtb_ref,
             ds_scr, dq_ref, dk_ref, dv_ref, dg_ref, db_ref, nxt_scr):
        step = pl.program_id(0)

        @pl.when(step == 0)
        def _():
            ds_scr[...] = jnp.zeros_like(ds_scr)
            nxt_scr[...] = jnp.zeros_like(nxt_scr)

        xs = _gdn_chunk_common(q_ref, k_ref, v_ref, g_ref, b_ref)

        ds_state = [ds_scr[h] for h in range(DN_HEADS)]

        def chain(sub, h):
            e = DN_HEADS * sub + h
            x = {key: (val[e] if isinstance(val, list) else val) for key, val in xs.items()}
            sl = x["sls"]
            qs, kx, vx, beta_, gam, dm = x["q"], x["k"], x["v"], x["beta"], x["gam"], x["dm"]
            kb, vb, kbg, kd, ek = x["kb"], x["vb"], x["kbg"], x["kd"], x["ek"]
            t = t_ref[h, sub]
            s = s_ref[h, sub]
            dsn = ds_state[h]
            dob = do_ref[sl]
            eg_last = jnp.exp(x["g_last"])
            u = _nn(t, vb)
            w = _nn(t, kbg)
            mqk = x["qk"] * dm
            qd = qs * gam
            dqd = _nt(dob, s)
            dkd_pre = _nn(kd, dsn)
            yield
            v_new = u - _nn(w, s)
            dv_new = _tn(mqk, dob) + dkd_pre
            dq = dqd * gam
            dgam = jnp.sum(dqd * qs, axis=1, keepdims=True)
            yield
            ds_state[h] = _tn(qd, dob) + eg_last * dsn - _tn(w, dv_new)
            dmm = jnp.where(x["incl"], _nt(dob, v_new), 0.0)
            dkd = _nt(v_new, dsn)
            dw = -_nt(dv_new, s)
            dvb = _tn(t, dv_new)
            dt = _nt(dv_new, vb)
            yield
            dqk = dmm * dm
            e_mat = dmm * mqk
            dq = dq + _nn(dqk, kx)
            dk = _tn(dqk, qs) + dkd * ek
            e1 = jnp.sum(dkd * kd, axis=1, keepdims=True)
            dgc = -e1
            dg_last = jnp.sum(e1) + eg_last * jnp.sum(s * dsn)
            dt = dt + _nt(dw, kbg)
            dkbg = _tn(t, dw)
            yield
            tdt = _tn(t, dt, hp="3x")
            yield
            da = jnp.where(x["strict"], -_nt(tdt, t, hp="3x"), 0.0)
            yield
            dkk = da * dm
            e_mat = e_mat + da * x["kk"] * dm
            dkb = _nn(dkk, kx) + dkbg * gam
            dk = dk + _tn(dkk, kb)
            dgam = dgam + jnp.sum(dkbg * kb, axis=1, keepdims=True)
            yield
            dk = dk + dkb * beta_
            dbeta = jnp.sum(dkb * kx, axis=1, keepdims=True) + jnp.sum(dvb * vx, axis=1, keepdims=True)
            dv = dvb * beta_
            dgc = dgc + jnp.sum(e_mat, axis=1, keepdims=True) + dgam * gam
            dgc = dgc - jnp.sum(e_mat.T, axis=1, keepdims=True)
            yield
            dg = _cumsum_rows(dgc, reverse=True) + dg_last
            yield
            dq_ref[sl] = dq * (1.0 / math.sqrt(DN_DIM))
            dk_ref[sl] = dk
            dv_ref[sl] = dv
            dg_ref[sl] = dg
            db_ref[sl] = jnp.broadcast_to(dbeta, (c, LANE))

        chains = []
        for sub in reversed(range(GDN_SUB_CHUNKS)):
            new = [chain(sub, h) for h in range(DN_HEADS)]
            for _ in range(3):
                for ch in new:
                    next(ch)
            chains += new
        _interleave(chains)
        for h in range(DN_HEADS):
            ds_scr[h] = ds_state[h]

        dq, dk, dv = dq_ref[...], dk_ref[...], dv_ref[...]
        outs, accs = _f_gdn_prep_bwd(
            nblk - 1 - step, x_ref[...], xp_ref[...], xn_ref[...], ab_ref[...], dq, nxt_scr[0], dk, nxt_scr[1],
            dv, nxt_scr[2], dg_ref[...], db_ref[...], w8_ref[...], alog_ref[...], dtb_ref[...], nt=nblk)
        nxt_scr[0] = dq[:8]
        nxt_scr[1] = dk[:8]
        nxt_scr[2] = dv[:8]
        dqkv_ref[...] = outs[0].astype(dqkv_ref.dtype)
        dab_ref[...] = outs[1].astype(dab_ref.dtype)
        for ref, val in zip((dcw_ref, dalog_ref, ddtb_ref), accs):
            @pl.when(step == 0)
            def _():
                ref[...] = val

            @pl.when(step > 0)
            def _():
                ref[...] += val

    sub = GDN_SUB_CHUNKS
    r8 = blk // 8
    rb = lambda n: (nblk - 1 - n, 0)
    hs = lambda n: (0, nblk - 1 - n, 0, 0)
    rows = pl.BlockSpec((blk, DN_WIDTH), rb)
    whole = lambda a: pl.BlockSpec(a.shape, lambda n: (0,) * a.ndim)
    wide = 3 * DN_WIDTH
    return pl.pallas_call(
        body, name="gdn_bwd", grid=(nblk,),
        in_specs=[rows] * 5
        + [pl.BlockSpec((DN_HEADS, sub, DN_DIM, DN_DIM), hs), pl.BlockSpec((DN_HEADS, sub, c, c), hs), rows,
           pl.BlockSpec((blk, wide), rb),
           pl.BlockSpec((8, wide), lambda n: (jnp.maximum((nblk - 1 - n) * r8 - 1, 0), 0)),
           pl.BlockSpec((8, wide), lambda n: (jnp.minimum((nblk - n) * r8, tp // 8 - 1), 0)),
           pl.BlockSpec((blk, LANE), lambda n: (nblk - 1 - n, C_AB // LANE)),
           whole(conv_w8), whole(alog), whole(dtb), _ANY_SPEC],
        out_specs=[pl.BlockSpec((blk, wide), rb), pl.BlockSpec((blk, LANE), rb),
                   whole(conv_w8), whole(alog), whole(dtb)],
        out_shape=[jax.ShapeDtypeStruct((tp, wide), _MXU), jax.ShapeDtypeStruct((tp, LANE), _MXU),
                   jax.ShapeDtypeStruct(conv_w8.shape, F32), jax.ShapeDtypeStruct(alog.shape, F32),
                   jax.ShapeDtypeStruct(dtb.shape, F32)],
        scratch_shapes=[pltpu.VMEM((DN_HEADS, DN_DIM, DN_DIM), F32)] + [pltpu.VMEM((blk, DN_WIDTH), F32)] * 5
        + [pltpu.VMEM((3, 8, DN_WIDTH), F32)],
        compiler_params=pltpu.CompilerParams(dimension_semantics=("arbitrary",), vmem_limit_bytes=VMEM_LIMIT),
    )(q, k, v, g, beta, s_all, t_all, do, proj, proj, proj, proj, conv_w8, alog, dtb, after)


def _silu_parts(x):
    s = _sigmoid(x)
    return x * s, s * (1.0 + x * (1.0 - s))


def _f_rms_bwd_add(i, x, dy, dres, w, *, mask_pad):
    dx, dwr = _rms_bwd(x, w, dy, x.shape[1])
    out = dres + dx
    if mask_pad:
        out = jnp.where(_row_ids(i, x.shape[0]) >= PAD, out, 0.0)
    return (out,), (_rowsum(dwr),)


def _rope(x, cos, sin_s):
    return x * cos + _swap_halves(x) * sin_s


def _rope_t(dy, cos, sin_s):
    return dy * cos + _swap_halves(dy * sin_s)


def _f_mla_qk(i, qf, kvf, kpe, cos, sin_s, qw, kw):
    qs, ks, vs = [], [], []
    for h in range(MLA_HEADS):
        qn, _ = _rms_fwd(qf[:, HP * h:HP * (h + 1)], qw, QK_HEAD)
        qs += [qn[:, :QK_NOPE], _rope(qn[:, QK_NOPE:], cos, sin_s)]
        kh = jnp.concatenate([kvf[:, HP * h:HP * h + QK_NOPE], kpe], axis=1)
        kn, _ = _rms_fwd(kh, kw, QK_HEAD)
        ks += [kn[:, :QK_NOPE], _rope(kn[:, QK_NOPE:], cos, sin_s)]
        vs.append(kvf[:, HP * h + QK_NOPE:HP * (h + 1)])
    return (jnp.concatenate(qs, axis=1), jnp.concatenate(ks, axis=1), jnp.concatenate(vs, axis=1)), ()


def _f_mla_front(i, ql, kvl, kpe, cos, sin_s, qaw, kvaw, wq_t, wkv, qw, kw):
    qn = _rms_fwd(ql, qaw, Q_LORA)[0].astype(_MXU)
    kvn = _rms_fwd(kvl, kvaw, KV_LORA)[0].astype(_MXU)
    qf = _nt(qn, wq_t)
    kvf = _nn(kvn, wkv)
    (q, k, v), _ = _f_mla_qk(i, qf, kvf, kpe, cos, sin_s, qw, kw)
    return (qn, kvn, qf, kvf, q, k, v), ()


def _f_mla_back(i, qf, kvf, kpe, cos, sin_s, dq, dk, dv, ql, kvl, qaw, kvaw, wq_t, wkv, qw, kw):
    (dqf, dkvf, dkpe), (dqw, dkw) = _f_mla_qk_bwd(i, qf, kvf, kpe, cos, sin_s, dq, dk, dv, qw, kw)
    dqf = dqf.astype(_MXU)
    dkvf = dkvf.astype(_MXU)
    dql, dqaw = _rms_bwd(ql, qaw, _nn(dqf, wq_t), Q_LORA)
    dkvl, dkvaw = _rms_bwd(kvl, kvaw, _nt(dkvf, wkv), KV_LORA)
    return (dqf, dkvf, dkpe, dql, dkvl), (dqw, dkw, _rowsum(dqaw), _rowsum(dkvaw))


def _f_mla_qk_bwd(i, qf, kvf, kpe, cos, sin_s, dq, dk, dv, qw, kw):
    dqf, dkvf = [], []
    dkpe = None
    dqw = None
    dkw = None
    for h in range(MLA_HEADS):
        dqh = dq[:, HP * h:HP * (h + 1)]
        dqn = jnp.concatenate([dqh[:, :QK_NOPE], _rope_t(dqh[:, QK_NOPE:], cos, sin_s)], axis=1)
        dx, dwr = _rms_bwd(qf[:, HP * h:HP * (h + 1)], qw, dqn, QK_HEAD)
        dqf.append(dx)
        dqw = _rowsum(dwr) if dqw is None else dqw + _rowsum(dwr)
        dkh = dk[:, HP * h:HP * (h + 1)]
        dkn = jnp.concatenate([dkh[:, :QK_NOPE], _rope_t(dkh[:, QK_NOPE:], cos, sin_s)], axis=1)
        kh = jnp.concatenate([kvf[:, HP * h:HP * h + QK_NOPE], kpe], axis=1)
        dx, dwr = _rms_bwd(kh, kw, dkn, QK_HEAD)
        dkvf += [dx[:, :QK_NOPE], dv[:, V_HEAD * h:V_HEAD * (h + 1)]]
        dkpe = dx[:, QK_NOPE:] if dkpe is None else dkpe + dx[:, QK_NOPE:]
        dkw = _rowsum(dwr) if dkw is None else dkw + _rowsum(dwr)
    return (jnp.concatenate(dqf, axis=1), jnp.concatenate(dkvf, axis=1), dkpe), (dqw, dkw)


def _gdn_act(i, x, halo, w8):
    halo = jnp.where(i > 0, halo, 0.0)
    c = _conv_fwd(x, halo, w8, DN_CONV)
    act, dact = _silu_parts(c)
    return act, dact


def _spread_heads(ab):
    tm = ab.shape[0]
    return jnp.concatenate([jnp.broadcast_to(ab[:, h:h + 1], (tm, DN_DIM)) for h in range(2 * DN_HEADS)], axis=1)


def _gather_heads(x):
    tm = x.shape[0]
    lane = lax.broadcasted_iota(jnp.int32, (tm, LANE), 1)
    out = jnp.zeros((tm, LANE), F32)
    for h in range(2 * DN_HEADS):
        out = out + jnp.where(lane == h, x[:, DN_DIM * h:DN_DIM * h + 1], 0.0)
    return out


def _gate_parts(ab, dtb):
    lane1 = lax.broadcasted_iota(jnp.int32, (1, LANE), 1)
    dtb_c = jnp.zeros((1, LANE), F32)
    for h in range(DN_HEADS):
        dtb_c = dtb_c + jnp.where(lane1 == h, dtb[:, DN_DIM * h:DN_DIM * h + 1], 0.0)
    pre = ab + dtb_c
    sig = _sigmoid(pre)
    lane = lax.broadcasted_iota(jnp.int32, ab.shape, 1)
    return jnp.where(lane < DN_HEADS, _softplus(pre), sig), sig


def _f_gdn_prep(i, x, halo, ab, w8, alog, dtb):
    tm = x.shape[0]
    act, _ = _gdn_act(i, x, halo, w8)
    outs = []
    for part in range(2):
        for h in range(DN_HEADS):
            t = act[:, DN_WIDTH * part + DN_DIM * h:DN_WIDTH * part + DN_DIM * (h + 1)]
            outs.append(t * lax.rsqrt(jnp.sum(t * t, axis=-1, keepdims=True) + EPS))
    q = jnp.concatenate(outs[:DN_HEADS], axis=1)
    k = jnp.concatenate(outs[DN_HEADS:], axis=1)
    v = act[:, 2 * DN_WIDTH:]
    abb = _spread_heads(ab)
    valid = _row_ids(i, tm) >= PAD
    g = jnp.where(valid, -jnp.exp(alog) * _softplus(abb[:, :DN_WIDTH] + dtb), 0.0)
    beta = jnp.where(valid, _sigmoid(abb[:, DN_WIDTH:]), 0.0)
    return (q, k, v, g, beta), ()


def _f_gdn_prep_bwd(i, x, x_prev, x_next, ab, dq, dq_next, dk, dk_next, dv, dv_next, dg, dbeta,
                    w8, alog, dtb, *, nt):
    tm = x.shape[0]
    x_prev = jnp.where(i > 0, x_prev, 0.0)
    more = i < nt - 1
    ext = lambda t, t_next: jnp.concatenate([t, jnp.where(more, t_next, 0.0)], axis=0)
    taps = _conv_taps(jnp.concatenate([x, x_next], axis=0), x_prev, DN_CONV)
    c = _conv_from_taps(taps, w8)
    act, dact = _silu_parts(c)
    douts = []
    for part, dd in enumerate((ext(dq, dq_next), ext(dk, dk_next))):
        for h in range(DN_HEADS):
            t = act[:, DN_WIDTH * part + DN_DIM * h:DN_WIDTH * part + DN_DIM * (h + 1)]
            r = lax.rsqrt(jnp.sum(t * t, axis=-1, keepdims=True) + EPS)
            y = t * r
            dy = dd[:, DN_DIM * h:DN_DIM * (h + 1)]
            douts.append(r * (dy - y * jnp.sum(dy * y, axis=-1, keepdims=True)))
    douts.append(ext(dv, dv_next))
    dc = jnp.concatenate(douts, axis=1) * dact
    dqkv = _conv_bwd_x(dc[:tm], dc[tm:], w8, DN_CONV)
    dconv_w = _conv_bwd_w_taps(dc[:tm], taps)
    sp_beta, sig = _gate_parts(ab, dtb)
    spread = _spread_heads(sp_beta)
    valid = _row_ids(i, tm) >= PAD
    ea = jnp.exp(alog)
    g = -ea * spread[:, :DN_WIDTH]
    dg = jnp.where(valid, dg, 0.0)
    dbeta = jnp.where(valid, dbeta, 0.0)
    da = dg * (-ea) * _spread_heads(sig)[:, :DN_WIDTH]
    beta = spread[:, DN_WIDTH:]
    db = dbeta * beta * (1.0 - beta)
    dab = _gather_heads(jnp.concatenate([da, db], axis=1))
    return (dqkv, dab), (dconv_w, _rowsum(dg * g), _rowsum(da))


def _f_mix(i, o_mla, o_dn, z, w_mla, w_dn):
    tm = o_mla.shape[0]
    valid = _row_ids(i, tm) >= PAD
    outs = []
    for h in range(MLA_HEADS):
        y, _ = _rms_fwd(o_mla[:, V_HEAD * h:V_HEAD * (h + 1)], w_mla, V_HEAD)
        outs.append(jnp.where(valid, y, 0.0))
    for h in range(DN_HEADS):
        y, _ = _rms_fwd(o_dn[:, DN_DIM * h:DN_DIM * (h + 1)], w_dn, DN_DIM)
        outs.append(y * _silu_parts(z[:, DN_DIM * h:DN_DIM * (h + 1)])[0])
    return (jnp.concatenate(outs, axis=1),), ()


def _f_mix_bwd(i, o_mla, o_dn, z, dy_mla, dy_dn, w_mla, w_dn):
    tm = o_mla.shape[0]
    valid = _row_ids(i, tm) >= PAD
    d_mla, d_dn, d_z = [], [], []
    dw_mla = None
    dw_dn = None
    for h in range(MLA_HEADS):
        sl = slice(V_HEAD * h, V_HEAD * (h + 1))
        dx, dwr = _rms_bwd(o_mla[:, sl], w_mla, jnp.where(valid, dy_mla[:, sl], 0.0), V_HEAD)
        d_mla.append(dx)
        dw_mla = _rowsum(dwr) if dw_mla is None else dw_mla + _rowsum(dwr)
    for h in range(DN_HEADS):
        sl = slice(DN_DIM * h, DN_DIM * (h + 1))
        y, _ = _rms_fwd(o_dn[:, sl], w_dn, DN_DIM)
        sz, dsz = _silu_parts(z[:, sl])
        d_z.append(dy_dn[:, sl] * y * dsz)
        dx, dwr = _rms_bwd(o_dn[:, sl], w_dn, dy_dn[:, sl] * sz, DN_DIM)
        d_dn.append(dx)
        dw_dn = _rowsum(dwr) if dw_dn is None else dw_dn + _rowsum(dwr)
    return ((jnp.concatenate(d_mla, axis=1), jnp.concatenate(d_dn, axis=1), jnp.concatenate(d_z, axis=1)),
            (dw_mla, dw_dn))


def _f_ffn_act_bwd(i, gp, gp_prev, gp_next, up, up_next, dact, dact_next, w8, b, *, nt):
    tm = gp.shape[0]
    gp_prev = jnp.where(i > 0, gp_prev, 0.0)
    dact_next = jnp.where(i < nt - 1, dact_next, 0.0)
    cat = lambda t, t_next: jnp.concatenate([t, t_next], axis=0)
    taps = _conv_taps(cat(gp, gp_next), gp_prev, FFN_CONV)
    gate = _conv_from_taps(taps, w8) + b
    sg, dsg = _silu_parts(gate)
    dact_e = cat(dact, dact_next)
    dgate = dact_e * cat(up, up_next) * dsg
    dgate_pre = _conv_bwd_x(dgate[:tm], dgate[tm:], w8, FFN_CONV)
    dup = dact * sg[:tm]
    return (dgate_pre, dup), (_conv_bwd_w_taps(dgate[:tm], taps), _rowsum(dgate[:tm]))


def _f_loss(i, h3, tgt):
    tm = h3.shape[0]
    diff = jnp.where(_row_ids(i, tm) >= ROW0, h3 - tgt, 0.0)
    part = 0.5 * jnp.sum(diff * diff) * (1.0 / D_MODEL)
    return (diff * (1.0 / D_MODEL),), (jnp.full((1, LANE), part, F32),)


def _local_step(h0, tgt, w, token, late_weights, grads_ready):
    tp = h0.shape[0]
    nt = tp // TM
    proj, u = _norm_mm("in_proj", h0, w["attn_norm_w"], w["w_in"], after=token)
    p_qkv = lambda kind="cur": _In(proj, 3 * DN_WIDTH, 0, kind)
    p_z = _In(proj, DN_WIDTH, C_Z // DN_WIDTH)
    p_ql = _In(proj, Q_LORA, C_QL // Q_LORA)
    p_kvl = _In(proj, KV_LORA, C_KVL // KV_LORA)
    p_kpe = _In(proj, LANE, C_KPE // LANE)
    p_ab = _In(proj, LANE, C_AB // LANE)
    cos, sin_s = _In(w["cos"]), _In(w["sin_s"])

    mla_w = [w["q_a_norm_w"], w["kv_a_norm_w"], w["w_q_b"], w["w_kv_b"], w["q_norm_w"], w["k_norm_w"]]
    tm_mla = _pick(tp, 288, 16)
    wide = MLA_HEADS * HP
    qn, kvn, qf, kvf, q, k, v = _rows(
        "mla_front", _f_mla_front, [p_ql, p_kvl, p_kpe, cos, sin_s], mla_w,
        [(Q_LORA, _MXU), (KV_LORA, _MXU), (wide, F32), (wide, F32), (wide, _MXU), (wide, _MXU),
         (MLA_HEADS * V_HEAD, _MXU)], tm=tm_mla)
    o_mla = _attn_fwd(q, k, v)

    dn_w = [w["dn_conv_w"], w["alog_b"], w["dtb_b"]]
    o_dn, s_all, t_all, gq, gk, gv, gg, gb = _gdn_fwd(proj, *dn_w)

    out_w = [w["mla_out_norm_w"], w["dn_out_norm_w"]]
    w = dict(w, **late_weights((o_mla, o_dn), _LATE[:3]))
    h2, mixed = _pro_mm("mix_out_proj", lambda i, *t: _f_mix(i, *t)[0][0], [_In(o_mla), _In(o_dn), p_z], out_w,
                        D_MODEL, w["w_out"], h0)

    ffn_w = [w["ffn_conv_w"], w["ffn_conv_b"]]
    hn, gate_pre, up, act = _ffn_in(h2, w["ffn_norm_w"], w["w_gate"], w["w_up"], *ffn_w)
    w = dict(w, **late_weights(act, _LATE[3:]))
    dh3, loss = _mm_rows("ffn_down_loss", act, w["w_down"], "nn", lambda i, y, r, t: _f_loss(i, r + y, t),
                         [_In(h2), _In(tgt)], [], [(D_MODEL, F32)], [(1, LANE)])

    g = {}
    dact = _mm("ffn_down_dx", dh3, w["w_down"], "nt")
    g["w_down"] = _mm("ffn_down_dw", act, dh3, "tn", out_dtype=_MXU)
    dgate_pre, dup, g["ffn_conv_w"], g["ffn_conv_b"] = _rows(
        "ffn_act_bwd", functools.partial(_f_ffn_act_bwd, nt=nt),
        [_In(gate_pre), _In(gate_pre, kind="prev"), _In(gate_pre, kind="next"), _In(up), _In(up, kind="next"),
         _In(dact), _In(dact, kind="next")], ffn_w,
        [(D_FF, _MXU), (D_FF, _MXU)], [(8, D_FF), (1, D_FF)])
    g["w_gate"], g["w_up"] = _mm_tn2("ffn_gate_up_dw", dgate_pre, dup, hn, out_dtype=_MXU)
    tok = grads_ready(g, ("w_down", "w_gate", "w_up"))
    dh2, g["ffn_norm_w"] = _mm_rows(
        "ffn_gate_up_dx_rms", [dgate_pre, dup], [w["w_gate"], w["w_up"]], "nn",
        lambda i, dy, x, dres, nw, _tok: _f_rms_bwd_add(i, x, dy, dres, nw, mask_pad=True),
        [_In(h2), _In(dh3)], [w["ffn_norm_w"], tok], [(D_MODEL, F32)], [(1, D_MODEL)], tm_cap=288)

    g["w_out"] = _mm("out_proj_dw", mixed, dh2, "tn", out_dtype=_MXU)
    half = MLA_HEADS * V_HEAD
    do_mla, do_dn, dz, g["mla_out_norm_w"], g["dn_out_norm_w"] = _mm_rows(
        "out_proj_dx_mix", dh2, w["w_out"], "nt",
        lambda i, dm, om, od, z, wm, wd: _f_mix_bwd(i, om, od, z, dm[:, :half], dm[:, half:], wm, wd),
        [_In(o_mla), _In(o_dn), p_z], out_w,
        [(half, F32), (DN_WIDTH, F32), (DN_WIDTH, _MXU)], [(1, V_HEAD), (1, DN_DIM)])

    dq, dk, dv = _attn_bwd(q, k, v, do_mla)
    dqf, dkvf, dkpe, dql, dkvl, g["q_norm_w"], g["k_norm_w"], g["q_a_norm_w"], g["kv_a_norm_w"] = _rows(
        "mla_back", _f_mla_back,
        [_In(qf), _In(kvf), p_kpe, cos, sin_s, _In(dq), _In(dk), _In(dv), p_ql, p_kvl], mla_w,
        [(wide, _MXU), (wide, _MXU), (LANE, _MXU), (Q_LORA, _MXU), (KV_LORA, _MXU)],
        [(1, HP), (1, HP), (1, Q_LORA), (1, KV_LORA)], tm=tm_mla)
    g["w_q_b"], g["w_kv_b"] = _mm_tn_pair("mla_b_dw", dqf, qn, kvn, dkvf)
    tok = grads_ready(g, ("w_out", "w_q_b", "w_kv_b"))

    dqkv, dab, g["dn_conv_w"], g["alog_b"], g["dtb_b"] = _gdn_bwd(
        gq, gk, gv, gg, gb, s_all, t_all, do_dn, proj, *dn_w, tok)

    dproj = jnp.concatenate([dqkv, dz, dql, dkvl, dkpe, dab], axis=1)
    g["w_in"] = _mm("in_proj_dw", dproj, u, "tn", out_dtype=_MXU)
    tok = grads_ready(g, ("w_in",))
    dh0, g["attn_norm_w"] = _mm_rows(
        "in_proj_dx_rms", dproj, w["w_in"], "nn",
        lambda i, du, x, dres, nw, _tok: _f_rms_bwd_add(i, x, du, dres, nw, mask_pad=False),
        [_In(h0), _In(dh2)], [w["attn_norm_w"], tok], [(D_MODEL, F32)], [(1, D_MODEL)])
    return loss, dh0, g


def _w_in_to_padded(w):
    c1, c2, c3 = Q_LORA, Q_LORA + KV_LORA, Q_LORA + KV_LORA + QK_ROPE
    c4 = c3 + 3 * DN_WIDTH
    c5 = c4 + DN_WIDTH
    z = lambda n: jnp.zeros((n, w.shape[1]), w.dtype)
    return jnp.concatenate([w[c3:c4], w[c4:c5], w[:c1], w[c1:c2], w[c2:c3], z(LANE - QK_ROPE),
                            w[c5:], z(LANE - 2 * DN_HEADS)], axis=0)


def _w_in_from_padded(g):
    return jnp.concatenate([g[C_QL:C_QL + Q_LORA], g[C_KVL:C_KVL + KV_LORA], g[C_KPE:C_KPE + QK_ROPE],
                            g[:C_Z + DN_WIDTH], g[C_AB:C_AB + 2 * DN_HEADS]], axis=0)


def _w_q_b_to_padded(w):
    r = w.shape[1]
    w = w.reshape(MLA_HEADS, QK_HEAD, r)
    return jnp.pad(w, ((0, 0), (0, HP - QK_HEAD), (0, 0))).reshape(MLA_HEADS * HP, r)


def _w_q_b_from_padded(g):
    r = g.shape[1]
    return g.reshape(MLA_HEADS, HP, r)[:, :QK_HEAD].reshape(MLA_HEADS * QK_HEAD, r)


def _pad_rows8(w):
    return jnp.pad(w, ((0, 8 - w.shape[0]), (0, 0)))


def _prepare(full, tp):
    w = {}
    mx = lambda a: a.astype(_MXU)
    w["attn_norm_w"] = full["attn_norm_w"]
    w["w_in"] = mx(_w_in_to_padded(full["w_in"]))
    w["q_a_norm_w"] = full["q_a_norm_w"]
    w["kv_a_norm_w"] = full["kv_a_norm_w"]
    w["w_q_b"] = mx(_w_q_b_to_padded(full["w_q_b"]))
    w["w_kv_b"] = mx(full["w_kv_b"])
    w["q_norm_w"] = jnp.pad(full["q_norm_w"], ((0, 0), (0, HP - QK_HEAD)))
    w["k_norm_w"] = jnp.pad(full["k_norm_w"], ((0, 0), (0, HP - QK_HEAD)))
    w["mla_out_norm_w"] = full["mla_out_norm_w"]
    w["dn_out_norm_w"] = full["dn_out_norm_w"]
    w["dn_conv_w"] = _pad_rows8(full["dn_conv_w"])
    w["alog_b"] = jnp.repeat(full["dn_A_log"], DN_DIM, axis=1)
    w["dtb_b"] = jnp.repeat(full["dn_dt_bias"], DN_DIM, axis=1)
    w["ffn_norm_w"] = full["ffn_norm_w"]
    w["ffn_conv_w"] = _pad_rows8(full["ffn_conv_w"])
    w["ffn_conv_b"] = full["ffn_conv_b"]
    for n in _LATE:
        if n in full:
            w[n] = mx(full[n])
    half = QK_ROPE // 2
    inv = ROPE_THETA ** (-jnp.arange(half, dtype=F32) / half)
    ang = (jnp.arange(tp, dtype=jnp.int32) - PAD).astype(F32)[:, None] * inv[None, :]
    zc = jnp.zeros((tp, LANE - QK_ROPE), F32)
    w["cos"] = jnp.concatenate([jnp.cos(ang), jnp.cos(ang), zc], axis=1)
    w["sin_s"] = jnp.concatenate([-jnp.sin(ang), jnp.sin(ang), zc], axis=1)
    return w


def _grads_to_natural(g):
    convert = {
        "w_in": ("w_in", _w_in_from_padded),
        "w_q_b": ("w_q_b", _w_q_b_from_padded),
        "q_norm_w": ("q_norm_w", lambda a: a[:, :QK_HEAD]),
        "k_norm_w": ("k_norm_w", lambda a: a[:, :QK_HEAD]),
        "dn_conv_w": ("dn_conv_w", lambda a: a[:DN_CONV]),
        "ffn_conv_w": ("ffn_conv_w", lambda a: a[:FFN_CONV]),
        "alog_b": ("dn_A_log", lambda a: a[:, ::DN_DIM]),
        "dtb_b": ("dn_dt_bias", lambda a: a[:, ::DN_DIM]),
    }
    n = {}
    for key, a in g.items():
        name, fn = convert.get(key, (key, lambda t: t))
        n[name] = fn(a)
    return n


_MESH = pl.DeviceIdType.MESH
_ANY = pl.BlockSpec(memory_space=pl.ANY)
_CHIP_FLIPS = ((1, 0), (0, 1), (1, 1))


def _me():
    return lax.axis_index("x"), lax.axis_index("y"), lax.axis_index("c")


def _all_gather(name, blk, after):
    after = list(after)

    def body(x_ref, *rest):
        out_ref, send_sems, recv_sems, local_sem = rest[len(after):]
        x, y, c = _me()
        me, sib = (x, y, c), (x, y, 1 - c)
        chips = [(x ^ fx, y ^ fy) for fx, fy in _CHIP_FLIPS]

        def slot(p):
            return out_ref.at[4 * p[0] + 2 * p[1] + p[2]]

        def copy(k, block, to, src=None):
            return pltpu.make_async_remote_copy(
                src_ref=slot(block) if src is None else src, dst_ref=slot(block),
                send_sem=send_sems.at[k], recv_sem=recv_sems.at[k], device_id=to, device_id_type=_MESH)

        mine = pltpu.make_async_copy(x_ref, slot(me), local_sem)
        mine.start()
        first = [copy(0, me, sib, src=x_ref)]
        first += [copy(1 + j, me, (*chip, c), src=x_ref) for j, chip in enumerate(chips)]
        for cp in first:
            cp.start()
        passed = [copy(4 + j, (*chip, c), sib) for j, chip in enumerate(chips)]
        for j, chip in enumerate(chips):
            copy(1 + j, (*chip, c), me).wait_recv()
            passed[j].start()
        copy(0, sib, me).wait_recv()
        for j, chip in enumerate(chips):
            copy(4 + j, (*chip, 1 - c), me).wait_recv()
        for cp in first + passed:
            cp.wait_send()
        mine.wait()

    return pl.pallas_call(
        body, name=name, in_specs=[_ANY] * (1 + len(after)), out_specs=_ANY,
        out_shape=jax.ShapeDtypeStruct((N_DEV,) + blk.shape, blk.dtype),
        scratch_shapes=[pltpu.SemaphoreType.DMA((7,)), pltpu.SemaphoreType.DMA((7,)), pltpu.SemaphoreType.DMA],
    )(blk, *after)


def _row_tile(r):
    divs = [d for d in range(16, min(r, 512) + 1, 16) if r % d == 0]
    return divs[-1] if divs else r


def _adam_math(g, w, m, v):
    m_new = ADAM_B1 * m + (1.0 - ADAM_B1) * g
    v_new = ADAM_B2 * v + (1.0 - ADAM_B2) * (g * g)
    m_hat = m_new / (1.0 - ADAM_B1 ** ADAM_STEP)
    v_hat = v_new / (1.0 - ADAM_B2 ** ADAM_STEP)
    return -ADAM_LR * (m_hat / (jnp.sqrt(v_hat) + ADAM_EPS) + ADAM_WD * w), m_new, v_new


def _adam_vectors(name, row, items, ws, ms, vs):
    k = len(items)

    def body(row_ref, *refs):
        w_refs, m_refs, v_refs = refs[:k], refs[k:2 * k], refs[2 * k:3 * k]
        outs = refs[3 * k:]
        for idx, (off, n, per_head) in enumerate(items):
            if per_head:
                spread = row_ref[:, off:off + DN_WIDTH]
                lane = lax.broadcasted_iota(jnp.int32, (1, LANE), 1)
                g = jnp.zeros((1, LANE), F32)
                for h in range(DN_HEADS):
                    g = g + jnp.where(lane == h, spread[:, DN_DIM * h:DN_DIM * h + 1], 0.0)
                g = g[:, :n]
            else:
                g = row_ref[:, off:off + n]
            d, m_new, v_new = _adam_math(g, w_refs[idx][...], m_refs[idx][...], v_refs[idx][...])
            for kind, val in enumerate((g, d, m_new, v_new)):
                outs[kind * k + idx][...] = val

    shapes = [jax.ShapeDtypeStruct((1, n), F32) for _, n, _ in items]
    res = pl.pallas_call(body, name=name, out_shape=shapes * 4)(row, *ws, *ms, *vs)
    return [list(res[kind * k:(kind + 1) * k]) for kind in range(4)]


def _adam_arrays(name, gs, ws, ms, vs):
    k = len(gs)

    def body(*refs):
        outs = refs[4 * k:]
        for idx in range(k):
            res = _adam_math(refs[idx][...], refs[k + idx][...], refs[2 * k + idx][...], refs[3 * k + idx][...])
            for kind, val in enumerate(res):
                outs[kind * k + idx][...] = val

    shapes = [jax.ShapeDtypeStruct(w.shape, F32) for w in ws]
    res = pl.pallas_call(body, name=name, out_shape=shapes * 3)(*gs, *ws, *ms, *vs)
    return [list(res[kind * k:(kind + 1) * k]) for kind in range(3)]


def _sum_parts(name, parts):
    _, r, cols = parts[0][0].shape
    tm = _row_tile(r)
    idx = jnp.stack([jnp.asarray(s, jnp.int32) for _, s in parts])
    n = len(parts)

    def body(idx_ref, *refs):
        g = refs[0][0].astype(F32)
        for p_ref in refs[1:n]:
            g = g + p_ref[0].astype(F32)
        refs[n][...] = g

    return pl.pallas_call(
        body, name=name,
        grid_spec=pltpu.PrefetchScalarGridSpec(
            num_scalar_prefetch=1, grid=(r // tm,),
            in_specs=[pl.BlockSpec((1, tm, cols), lambda i, idx_ref, p=p: (idx_ref[p], i, 0)) for p in range(n)],
            out_specs=pl.BlockSpec((tm, cols), lambda i, idx_ref: (i, 0))),
        out_shape=jax.ShapeDtypeStruct((r, cols), F32),
        compiler_params=pltpu.CompilerParams(dimension_semantics=("parallel",)),
    )(idx, *[a for a, _ in parts])


def _adam(name, parts, w, m, v):
    r, cols = w.shape
    tm = _row_tile(r)
    tc = cols // 4 if (r // tm < 4 and cols % (4 * LANE) == 0) else cols
    idx = jnp.stack([jnp.asarray(s, jnp.int32) for _, s in parts])
    n = len(parts)

    def body(idx_ref, *refs):
        g = refs[0][0].astype(F32)
        for p_ref in refs[1:n]:
            g = g + p_ref[0].astype(F32)
        w_ref, m_ref, v_ref, g_out, d_out, m_out, v_out = refs[n:]
        g_out[...] = g
        d_out[...], m_out[...], v_out[...] = _adam_math(g, w_ref[...], m_ref[...], v_ref[...])

    part_specs = [pl.BlockSpec((1, tm, tc), lambda i, j, idx_ref, p=p: (idx_ref[p], i, j)) for p in range(n)]
    flat = pl.BlockSpec((tm, tc), lambda i, j, idx_ref: (i, j))
    return pl.pallas_call(
        body, name=name,
        grid_spec=pltpu.PrefetchScalarGridSpec(
            num_scalar_prefetch=1, grid=(r // tm, cols // tc), in_specs=part_specs + [flat] * 3,
            out_specs=[flat] * 4),
        out_shape=[jax.ShapeDtypeStruct((r, cols), F32)] * 4,
        compiler_params=pltpu.CompilerParams(dimension_semantics=("parallel", "parallel")),
    )(idx, *[a for a, _ in parts], w, m, v)


def _all_gather_many(name, blks):
    n = len(blks)

    def body(*refs):
        x_refs, out_refs = refs[:n], refs[n:2 * n]
        send_sems, recv_sems, local_sems = refs[2 * n:]
        x, y, c = _me()
        me, sib = (x, y, c), (x, y, 1 - c)
        chips = [(x ^ fx, y ^ fy) for fx, fy in _CHIP_FLIPS]

        def slot(a, p):
            return out_refs[a].at[4 * p[0] + 2 * p[1] + p[2]]

        def copy(a, k, block, to, src=None):
            return pltpu.make_async_remote_copy(
                src_ref=slot(a, block) if src is None else src, dst_ref=slot(a, block),
                send_sem=send_sems.at[7 * a + k], recv_sem=recv_sems.at[7 * a + k], device_id=to,
                device_id_type=_MESH)

        mine = [pltpu.make_async_copy(x_refs[a], slot(a, me), local_sems.at[a]) for a in range(n)]
        first = []
        for a in range(n):
            mine[a].start()
            first.append(copy(a, 0, me, sib, src=x_refs[a]))
            first += [copy(a, 1 + j, me, (*chip, c), src=x_refs[a]) for j, chip in enumerate(chips)]
        for cp in first:
            cp.start()
        passed = []
        for j, chip in enumerate(chips):
            for a in range(n):
                copy(a, 1 + j, (*chip, c), me).wait_recv()
                cp = copy(a, 4 + j, (*chip, c), sib)
                cp.start()
                passed.append(cp)
        for a in range(n):
            copy(a, 0, sib, me).wait_recv()
            for j, chip in enumerate(chips):
                copy(a, 4 + j, (*chip, 1 - c), me).wait_recv()
        for cp in first + passed:
            cp.wait_send()
        for cp in mine:
            cp.wait()

    return pl.pallas_call(
        body, name=name, in_specs=[_ANY] * n, out_specs=[_ANY] * n,
        out_shape=[jax.ShapeDtypeStruct((N_DEV,) + b.shape, b.dtype) for b in blks],
        scratch_shapes=[pltpu.SemaphoreType.DMA((7 * n,)), pltpu.SemaphoreType.DMA((7 * n,)),
                        pltpu.SemaphoreType.DMA((n,))],
    )(*blks)


_HBM = pl.BlockSpec(memory_space=pltpu.HBM)
_SEM = pl.BlockSpec(memory_space=pltpu.SEMAPHORE)
_EFFECT = pltpu.SideEffectType.DATAFLOW_SIDE_EFFECTING


def _push_copies(src_refs, land_refs, send_sems, recv_sems, src_by_peer, first=0):
    x, y, c = _me()
    my_id = 4 * x + 2 * y + c
    out = []
    for k in range(len(src_refs)):
        a = first + k
        for f in range(1, N_DEV):
            px, py, pc = x ^ (f >> 2), y ^ ((f >> 1) & 1), c ^ (f & 1)
            pid = 4 * px + 2 * py + pc
            src = src_refs[k].at[pid] if src_by_peer else src_refs[k]
            start = pltpu.make_async_remote_copy(
                src_ref=src, dst_ref=land_refs[k].at[my_id], send_sem=send_sems.at[7 * a + f - 1],
                recv_sem=recv_sems.at[7 * a + f - 1], device_id=(px, py, pc), device_id_type=_MESH)
            landed = pltpu.make_async_remote_copy(
                src_ref=src, dst_ref=land_refs[k].at[pid], send_sem=send_sems.at[7 * a + f - 1],
                recv_sem=recv_sems.at[7 * a + f - 1], device_id=(px, py, pc), device_id_type=_MESH)
            out.append((start, landed))
    return out


def _push_start(name, srcs, src_by_peer, after):
    n = len(srcs)
    lands = [jax.ShapeDtypeStruct((N_DEV,) + (s.shape[1:] if src_by_peer else s.shape), s.dtype) for s in srcs]

    def body(*refs):
        src_refs, land_refs = refs[:n], refs[n:2 * n]
        send_sems, recv_sems = refs[2 * n + 1], refs[2 * n + 2]
        token = refs[-1]
        for start, _ in _push_copies(src_refs, land_refs, send_sems, recv_sems, src_by_peer):
            start.start()
        token[...] = jnp.zeros_like(token)

    hbm = lambda a: pltpu.with_memory_space_constraint(a, pltpu.HBM)
    res = pl.pallas_call(
        body, name=name,
        out_shape=(pltpu.SemaphoreType.DMA((7 * n,)), pltpu.SemaphoreType.DMA((7 * n,)),
                   *[pltpu.HBM(s.shape, s.dtype) for s in srcs], *[pltpu.HBM(s.shape, s.dtype) for s in lands],
                   jax.ShapeDtypeStruct((8, LANE), F32)),
        in_specs=[_HBM] * (2 * n) + [_ANY],
        out_specs=(_SEM, _SEM, *[_HBM] * (2 * n), pl.BlockSpec(memory_space=pltpu.VMEM)),
        input_output_aliases={i: 2 + i for i in range(2 * n)},
        compiler_params=pltpu.CompilerParams(has_side_effects=_EFFECT),
    )(*[hbm(s) for s in srcs], *[hbm(lax.empty(s.shape, s.dtype)) for s in lands], after)
    return res[0], res[1], list(res[2:2 + n]), list(res[2 + n:2 + 2 * n]), res[-1]


def _push_wait(name, send_sems, recv_sems, srcs, lands, src_by_peer, after, first=0):
    n = len(srcs)
    after = list(after) if isinstance(after, (list, tuple)) else [after]

    def body(*refs):
        src_refs, land_refs = refs[:n], refs[n:2 * n]
        s_sems, r_sems = refs[2 * n], refs[2 * n + 1]
        for _, landed in _push_copies(src_refs, land_refs, s_sems, r_sems, src_by_peer, first):
            landed.wait_send()
            landed.wait_recv()

    res = pl.pallas_call(
        body, name=name,
        out_shape=tuple(pltpu.HBM(s.shape, s.dtype) for s in list(srcs) + list(lands)),
        in_specs=[_HBM] * (2 * n) + [_SEM, _SEM] + [_ANY] * len(after),
        out_specs=tuple([_HBM] * (2 * n)),
        input_output_aliases={i: i for i in range(2 * n)},
        compiler_params=pltpu.CompilerParams(has_side_effects=_EFFECT),
    )(*srcs, *lands, send_sems, recv_sems, *after)
    return list(res[:n]), list(res[n:])


_SHARDED = (
    ("meta_tokens", 1, (N_META, D_MODEL)),
    ("w_in", 1, (D_MODEL, IN_COLS)),
    ("w_q_b", 1, (Q_LORA, MLA_HEADS * QK_HEAD)),
    ("w_kv_b", 1, (KV_LORA, MLA_HEADS * (QK_NOPE + V_HEAD))),
    ("dn_conv_w", 1, (DN_CONV, 3 * DN_WIDTH)),
    ("w_out", 0, (2 * DN_WIDTH, D_MODEL)),
    ("w_gate", 1, (D_MODEL, D_FF)),
    ("w_up", 1, (D_MODEL, D_FF)),
    ("ffn_conv_w", 1, (FFN_CONV, D_FF)),
    ("w_down", 0, (D_FF, D_MODEL)),
)
_F32_GATHERED = ("meta_tokens", "dn_conv_w", "ffn_conv_w")
_EARLY = ("w_in", "w_q_b", "w_kv_b")
_LATE = ("w_out", "w_gate", "w_up", "w_down")
_TRANSPOSED = ("w_in", "w_q_b", "w_gate", "w_up")
_REPLICATED = (
    ("attn_norm_w", D_MODEL), ("q_a_norm_w", Q_LORA), ("kv_a_norm_w", KV_LORA), ("q_norm_w", QK_HEAD),
    ("k_norm_w", QK_HEAD), ("mla_out_norm_w", V_HEAD), ("dn_A_log", DN_HEADS), ("dn_dt_bias", DN_HEADS),
    ("dn_out_norm_w", DN_DIM), ("ffn_norm_w", D_MODEL), ("ffn_conv_b", D_FF),
)
_SMALL_BLOCK = (8, 512)


def _local_shape(dim, shape):
    return (shape[0] // N_DEV, shape[1]) if dim == 0 else (shape[0], shape[1] // N_DEV)


def _from_blocks(blocks, dim, shape):
    r, c = shape
    if dim == 0:
        return blocks.reshape(r, c)
    return blocks.reshape(N_DEV, r, c // N_DEV).transpose(1, 0, 2).reshape(r, c)


def _split(flat, sizes):
    out, o = [], 0
    for s in sizes:
        out.append(flat[..., o:o + s])
        o += s
    return out


def kernel(x, meta_tokens, attn_norm_w, w_in, q_a_norm_w, w_q_b, kv_a_norm_w, w_kv_b, q_norm_w, k_norm_w, mla_out_norm_w, dn_conv_w, dn_A_log, dn_dt_bias, dn_out_norm_w, w_out, ffn_norm_w, w_gate, w_up, ffn_conv_w, ffn_conv_b, w_down, loss_target, m_meta_tokens, m_attn_norm_w, m_w_in, m_q_a_norm_w, m_w_q_b, m_kv_a_norm_w, m_w_kv_b, m_q_norm_w, m_k_norm_w, m_mla_out_norm_w, m_dn_conv_w, m_dn_A_log, m_dn_dt_bias, m_dn_out_norm_w, m_w_out, m_ffn_norm_w, m_w_gate, m_w_up, m_ffn_conv_w, m_ffn_conv_b, m_w_down, v_meta_tokens, v_attn_norm_w, v_w_in, v_q_a_norm_w, v_w_q_b, v_kv_a_norm_w, v_w_kv_b, v_q_norm_w, v_k_norm_w, v_mla_out_norm_w, v_dn_conv_w, v_dn_A_log, v_dn_dt_bias, v_dn_out_norm_w, v_w_out, v_ffn_norm_w, v_w_gate, v_w_up, v_ffn_conv_w, v_ffn_conv_b, v_w_down):
    names = [n for n, _, _ in _SHARDED] + [n for n, _ in _REPLICATED]
    given = dict(locals())
    two_d = lambda a: a.reshape(a.shape[-2:])
    view = lambda a, n: two_d(a).T if n in _TRANSPOSED else two_d(a)
    wl = {n: view(given[n], n) for n in names}
    ml = {n: view(given["m_" + n], n) for n in names}
    vl = {n: view(given["v_" + n], n) for n in names}
    out_shapes = {n: given[n].shape for n in names}

    spec = {n: (d, s) for n, d, s in _SHARDED}
    small_sizes = [math.prod(_local_shape(*spec[n])) for n in _F32_GATHERED]

    def small_block(d):
        cat = jnp.concatenate([d[n].reshape(d[n].shape[:-2] + (-1,)) for n in _F32_GATHERED], axis=-1)
        pad = [(0, 0)] * (cat.ndim - 1) + [(0, math.prod(_SMALL_BLOCK) - cat.shape[-1])]
        return jnp.pad(cat, pad).reshape(cat.shape[:-1] + _SMALL_BLOCK)

    def shard(n):
        return wl[n].astype(_MXU)

    def from_slots(n, blocks):
        d, s = spec[n]
        if d == 0 or n in _TRANSPOSED:
            return blocks.reshape(-1, blocks.shape[-1])
        return blocks.transpose(1, 0, 2).reshape(s)

    my_id = 4 * lax.axis_index("x") + 2 * lax.axis_index("y") + lax.axis_index("c")
    got = _all_gather_many("gather_early", [shard(n) for n in _EARLY] + [small_block(wl)])
    full = {n: a for n, a in wl.items() if n not in _LATE}
    for n, blocks in zip(_EARLY, got):
        full[n] = from_slots(n, blocks)
    for n, p in zip(_F32_GATHERED, _split(got[-1].reshape(N_DEV, -1), small_sizes)):
        full[n] = _from_blocks(p, *spec[n])
    late_own = [shard(n) for n in _LATE]
    l_send, l_recv, l_src, l_land, token = _push_start("gather_late_start", late_own, False, got[-1])

    def late_weights(after, names):
        first = _LATE.index(names[0])
        sl = slice(first, first + len(names))
        _, lands = _push_wait("gather_late_wait_" + names[0], l_send, l_recv, l_src[sl], l_land[sl], False,
                              after, first)
        out = {}
        for n, land, own in zip(names, lands, late_own[sl]):
            out[n] = from_slots(n, lax.dynamic_update_slice(land, own[None], (my_id, 0, 0))).astype(_MXU)
        return out

    def dest_blocks(n, a):
        d, s = spec[n]
        r, c = _local_shape(d, s)
        if n in _TRANSPOSED:
            return a.reshape(N_DEV, c, r)
        return a.reshape(N_DEV, r, c) if d == 0 else a.reshape(r, N_DEV, c).transpose(1, 0, 2)

    pushed = []

    def grads_ready(g, names):
        nat = _grads_to_natural({n: g[n] for n in names})
        blocks = [dest_blocks(n, nat[n]).astype(_MXU) for n in names]
        sends, recvs, srcs, lands, tok = _push_start("rs_" + names[0] + "_start", blocks, True, token)
        pushed.append((names, sends, recvs, srcs, lands))
        return tok

    seq = x.shape[1]
    tp = ROW0 + seq
    h0 = jnp.concatenate([jnp.zeros((PAD, D_MODEL), F32), full["meta_tokens"], x[0]], axis=0)
    tgt = jnp.concatenate([jnp.zeros((ROW0, D_MODEL), F32), loss_target[0]], axis=0)
    loss, dh0, raw = _local_step(h0, tgt, _prepare(full, tp), token, late_weights, grads_ready)
    g = _grads_to_natural(raw)
    g["meta_tokens"] = dh0[PAD:ROW0]
    grad_x = dh0[ROW0:][None]

    big = [{}, {}, {}, {}]

    def finish(group):
        names, sends, recvs, srcs, lands = group
        srcs, lands = _push_wait("rs_" + names[0] + "_wait", sends, recvs, srcs, lands, True, dh0)
        for n, src, land in zip(names, srcs, lands):
            parts = [(src, my_id)] + [(land, my_id ^ f) for f in range(1, N_DEV)]
            for kind, a in enumerate(_adam("adam_" + n, parts, wl[n], ml[n], vl[n])):
                big[kind][n] = a

    for group in pushed[:-1]:
        finish(group)
    rep_names = [n for n, _ in _REPLICATED]
    raw_key = {"dn_A_log": "alog_b", "dn_dt_bias": "dtb_b"}
    pieces = [raw[raw_key.get(n, n)] for n in rep_names] + [loss]
    pieces += [g[n].reshape(1, -1) for n in _F32_GATHERED]
    widths = [p.shape[1] for p in pieces]
    offs = [sum(widths[:k]) for k in range(len(widths))]
    cat = jnp.concatenate(pieces, axis=1)
    cols = -(-cat.shape[1] // (8 * LANE)) * LANE
    mine = jnp.pad(cat, ((0, 0), (0, 8 * cols - cat.shape[1]))).reshape(8, cols)
    everyone = _all_gather("gather_small_grads", mine, [big[1][n] for group in pushed[:-1] for n in group[0]])
    total = _sum_parts("sum_small_grads", [(everyone, d) for d in range(N_DEV)]).reshape(1, 8 * cols)
    tot = {n: total[0, o:o + wd] for n, o, wd in zip(rep_names + ["loss"] + list(_F32_GATHERED), offs, widths)}
    items = [(o, size, n in raw_key) for (n, size), o in zip(_REPLICATED, offs)]
    sm = _adam_vectors("adam_replicated", total, items, [wl[n] for n in rep_names], [ml[n] for n in rep_names],
                       [vl[n] for n in rep_names])
    sm = [dict(zip(rep_names, kind)) for kind in sm]
    mine_of = {}
    for n in _F32_GATHERED:
        d, s = spec[n]
        r, c = _local_shape(d, s)
        mine_of[n] = lax.dynamic_slice(tot[n].reshape(s), (0, my_id * c), (r, c))
    res = _adam_arrays("adam_small_sharded", [mine_of[n] for n in _F32_GATHERED], [wl[n] for n in _F32_GATHERED],
                       [ml[n] for n in _F32_GATHERED], [vl[n] for n in _F32_GATHERED])
    for kind, arrays in enumerate([[mine_of[n] for n in _F32_GATHERED]] + res):
        big[kind].update(zip(_F32_GATHERED, arrays))

    finish(pushed[-1])

    outs = [tot["loss"][0], grad_x]
    for kind in range(4):
        for n in ("meta_tokens", "attn_norm_w", "w_in", "q_a_norm_w", "w_q_b", "kv_a_norm_w", "w_kv_b", "q_norm_w",
                  "k_norm_w", "mla_out_norm_w", "dn_conv_w", "dn_A_log", "dn_dt_bias", "dn_out_norm_w", "w_out",
                  "ffn_norm_w", "w_gate", "w_up", "ffn_conv_w", "ffn_conv_b", "w_down"):
            src = big[kind] if n in big[kind] else sm[kind]
            a = src[n].T if n in _TRANSPOSED else src[n]
            outs.append(a.reshape(out_shapes[n]))
    return tuple(outs)
```

```python
import functools
import math

import jax
import jax.numpy as jnp
from jax import lax
from jax.experimental import pallas as pl
from jax.experimental.pallas import tpu as pltpu

F32 = jnp.float32
_MXU = jnp.bfloat16
_HI = lax.Precision.HIGHEST

D_MODEL = 1024
N_META = 16
PAD = 112
ROW0 = PAD + N_META
MLA_HEADS = 4
QK_NOPE = 128
QK_ROPE = 64
QK_HEAD = QK_NOPE + QK_ROPE
V_HEAD = 128
Q_LORA = 256
KV_LORA = 256
ROPE_THETA = 10000.0
DN_HEADS = 4
DN_DIM = 128
DN_WIDTH = DN_HEADS * DN_DIM
DN_CONV = 4
DN_CHUNK = 64
GDN_SUB_CHUNKS = 2
D_FF = 2816
FFN_CONV = 3
EPS = 1e-6
HP = 256
C_Z = 1536
C_QL = 2048
C_KVL = 2304
C_KPE = 2560
C_AB = 2688
IN_COLS = 2632

ADAM_LR = 0.001
ADAM_B1 = 0.9
ADAM_B2 = 0.999
ADAM_EPS = 1e-08
ADAM_WD = 0.01
ADAM_STEP = 10

N_DEV = 8
TM = 128
LANE = 128
VMEM_LIMIT = 56 * 1024 * 1024
NEG = -1e30


def _dot(a, b, dims, hp=False):
    if hp:
        return lax.dot_general(a.astype(F32), b.astype(F32), (dims, ((), ())),
                               precision=lax.Precision.HIGH if hp == "3x" else _HI, preferred_element_type=F32)
    return lax.dot_general(a.astype(_MXU), b.astype(_MXU), (dims, ((), ())),
                           preferred_element_type=F32)


def _nn(a, b, hp=False):
    return _dot(a, b, ((1,), (0,)), hp)


def _nt(a, b, hp=False):
    return _dot(a, b, ((1,), (1,)), hp)


def _tn(a, b, hp=False):
    return _dot(a, b, ((0,), (0,)), hp)


def _sigmoid(x):
    return 1.0 / (1.0 + jnp.exp(-x))


def _rms_fwd(x, w, n):
    r = lax.rsqrt(jnp.sum(x * x, axis=-1, keepdims=True) * (1.0 / n) + EPS)
    return x * r * w, r


def _rms_bwd(x, w, dy, n):
    r = lax.rsqrt(jnp.sum(x * x, axis=-1, keepdims=True) * (1.0 / n) + EPS)
    xh = x * r
    gy = dy * w
    dx = r * (gy - xh * (jnp.sum(gy * xh, axis=-1, keepdims=True) * (1.0 / n)))
    return dx, dy * xh


def _rowsum(x):
    return jnp.sum(x, axis=0, keepdims=True)


def _row_ids(i, tm):
    return i * tm + lax.broadcasted_iota(jnp.int32, (tm, 1), 0)


def _shift_down(ext, s, tm):
    if s == 0:
        return ext[8:8 + tm]
    return pltpu.roll(ext, s, 0)[8:8 + tm]


def _shift_up(ext, s, tm):
    if s == 0:
        return ext[0:tm]
    return pltpu.roll(ext, tm + 8 - s, 0)[0:tm]


def _conv_taps(x, halo_prev, width):
    tm = x.shape[0]
    ext = jnp.concatenate([halo_prev, x], axis=0)
    return [_shift_down(ext, width - 1 - j, tm) for j in range(width)]


def _conv_from_taps(taps, w):
    y = None
    for j, tap in enumerate(taps):
        t = w[j:j + 1, :] * tap
        y = t if y is None else y + t
    return y


def _conv_fwd(x, halo_prev, w, width):
    return _conv_from_taps(_conv_taps(x, halo_prev, width), w)


def _conv_bwd_w_taps(dy, taps):
    tm = dy.shape[0]
    rows = [_rowsum(dy * tap[:tm]) for tap in taps]
    rows += [jnp.zeros_like(rows[0])] * (8 - len(taps))
    return jnp.concatenate(rows, axis=0)


def _conv_bwd_x(dy, halo_next, w, width):
    tm = dy.shape[0]
    ext = jnp.concatenate([dy, halo_next], axis=0)
    dx = None
    for j in range(width):
        t = w[j:j + 1, :] * _shift_up(ext, width - 1 - j, tm)
        dx = t if dx is None else dx + t
    return dx


def _softplus(x):
    e = jnp.exp(-jnp.abs(x))
    u = 1.0 + e
    l1p = jnp.where(u == 1.0, e, jnp.log(u) * e / jnp.where(u == 1.0, 1.0, u - 1.0))
    return jnp.maximum(x, 0.0) + l1p


def _swap_halves(x):
    lane = lax.broadcasted_iota(jnp.int32, x.shape, 1)
    return jnp.where(lane < 32, pltpu.roll(x, 96, 1), jnp.where(lane < 64, pltpu.roll(x, 32, 1), 0.0))


class _In:
    def __init__(self, arr, width=None, cb=0, kind="cur"):
        self.arr, self.kind = arr, kind
        self.width = arr.shape[1] if width is None else width
        self.cb = cb


def _whole_spec(x):
    return pl.BlockSpec(x.shape, lambda i, nd=x.ndim: (0,) * nd, pipeline_mode=pl.Buffered(1))


def _tile_spec(t, tm, tp):
    r8 = tm // 8
    if t.kind == "cur":
        return pl.BlockSpec((tm, t.width), lambda i, cb=t.cb: (i, cb))
    if t.kind == "prev":
        return pl.BlockSpec((8, t.width), lambda i, cb=t.cb: (jnp.maximum(i * r8 - 1, 0), cb))
    return pl.BlockSpec((8, t.width), lambda i, cb=t.cb: (jnp.minimum((i + 1) * r8, tp // 8 - 1), cb))


def _rows(name, fn, tiled, full, outs, accs=(), tm=TM):
    tp = tiled[0].arr.shape[0]
    nt = tp // tm
    n_in = len(tiled) + len(full)
    n_out = len(outs)

    def body(*refs):
        i = pl.program_id(0)
        vals = [r[...] for r in refs[:n_in]]
        o_t, o_a = fn(i, *vals)
        for r, v in zip(refs[n_in:n_in + n_out], o_t):
            r[...] = v.astype(r.dtype)
        for r, v in zip(refs[n_in + n_out:], o_a):
            @pl.when(i == 0)
            def _():
                r[...] = v

            @pl.when(i > 0)
            def _():
                r[...] += v

    in_specs = [_tile_spec(t, tm, tp) for t in tiled]
    in_specs += [pl.BlockSpec(a.shape, lambda i, nd=a.ndim: (0,) * nd) for a in full]
    out_specs = [pl.BlockSpec((tm, w), lambda i: (i, 0)) for w, _ in outs]
    out_specs += [pl.BlockSpec((r, w), lambda i: (0, 0)) for r, w in accs]
    out_shape = [jax.ShapeDtypeStruct((tp, w), dt) for w, dt in outs]
    out_shape += [jax.ShapeDtypeStruct((r, w), F32) for r, w in accs]
    res = pl.pallas_call(
        body, name=name, grid=(nt,), in_specs=in_specs, out_specs=out_specs, out_shape=out_shape,
        compiler_params=pltpu.CompilerParams(dimension_semantics=("arbitrary",), vmem_limit_bytes=VMEM_LIMIT),
    )(*[t.arr for t in tiled], *full)
    return res


def _pick(n, cap, mult):
    best = None
    for d in range(mult, min(n, cap) + 1, mult):
        if n % d == 0:
            best = d
    assert best is not None, (n, cap, mult)
    return best


_ANY_SPEC = pl.BlockSpec(memory_space=pl.ANY)


def _mm(name, a, b, mode, out_dtype=F32, resid=None, after=None):
    if mode == "tn":
        m, k = a.shape
        n = b.shape[1]
        tk = _pick(k, 512, 128)
        tn = _pick(n, 1408, 128)

        def body_tn(a_ref, b_ref, o_ref):
            o_ref[...] = _tn(a_ref[...], b_ref[...]).astype(o_ref.dtype)

        return pl.pallas_call(
            body_tn, name=name, grid=(n // tn, k // tk),
            in_specs=[pl.BlockSpec((m, tk), lambda j, p: (0, p)),
                      pl.BlockSpec((m, tn), lambda j, p: (0, j))],
            out_specs=pl.BlockSpec((tk, tn), lambda j, p: (p, j)),
            out_shape=jax.ShapeDtypeStruct((k, n), out_dtype),
            compiler_params=pltpu.CompilerParams(
                dimension_semantics=("parallel", "parallel"), vmem_limit_bytes=VMEM_LIMIT),
        )(a, b)

    m, k = a.shape
    n = b.shape[1] if mode == "nn" else b.shape[0]
    tn = _pick(n, 1408, 128)
    tm = _pick(m, 1152, 16)
    dotf = _nn if mode == "nn" else _nt

    def body(*refs):
        a_ref, b_ref, o_ref = refs[0], refs[1], refs[-1]
        acc = dotf(a_ref[...], b_ref[...])
        if resid is not None:
            acc = refs[2][...] + acc
        o_ref[...] = acc.astype(o_ref.dtype)

    b_spec = (pl.BlockSpec((k, tn), lambda j, i: (0, j)) if mode == "nn"
              else pl.BlockSpec((tn, k), lambda j, i: (j, 0)))
    in_specs = [pl.BlockSpec((tm, k), lambda j, i: (i, 0)), b_spec]
    args = [a, b]
    if resid is not None:
        in_specs.append(pl.BlockSpec((tm, tn), lambda j, i: (i, j)))
        args.append(resid)
    if after is not None:
        in_specs.append(_ANY_SPEC)
        args.append(after)
    return pl.pallas_call(
        body, name=name, grid=(n // tn, m // tm), in_specs=in_specs,
        out_specs=pl.BlockSpec((tm, tn), lambda j, i: (i, j)),
        out_shape=jax.ShapeDtypeStruct((m, n), out_dtype),
        compiler_params=pltpu.CompilerParams(
            dimension_semantics=("parallel", "parallel"), vmem_limit_bytes=VMEM_LIMIT),
    )(*args)


def _mm_tn2(name, a1, a2, b, out_dtype=F32):
    m, k = a1.shape
    n = b.shape[1]
    tk = _pick(k, 512, 128)

    def body(a1_ref, a2_ref, b_ref, o1_ref, o2_ref):
        bb = b_ref[...]
        o1_ref[...] = _tn(a1_ref[...], bb).astype(o1_ref.dtype)
        o2_ref[...] = _tn(a2_ref[...], bb).astype(o2_ref.dtype)

    a_spec = pl.BlockSpec((m, tk), lambda p: (0, p))
    o_spec = pl.BlockSpec((tk, n), lambda p: (p, 0))
    return pl.pallas_call(
        body, name=name, grid=(k // tk,),
        in_specs=[a_spec, a_spec, pl.BlockSpec((m, n), lambda p: (0, 0))],
        out_specs=[o_spec, o_spec], out_shape=[jax.ShapeDtypeStruct((k, n), out_dtype)] * 2,
        compiler_params=pltpu.CompilerParams(dimension_semantics=("parallel",), vmem_limit_bytes=VMEM_LIMIT),
    )(a1, a2, b)


def _mm_tn_pair(name, a1, b1, a2, b2):
    def body(a1_ref, b1_ref, a2_ref, b2_ref, o1_ref, o2_ref):
        o1_ref[...] = _tn(a1_ref[...], b1_ref[...])
        o2_ref[...] = _tn(a2_ref[...], b2_ref[...])

    return pl.pallas_call(
        body, name=name,
        out_shape=[jax.ShapeDtypeStruct((a1.shape[1], b1.shape[1]), F32),
                   jax.ShapeDtypeStruct((a2.shape[1], b2.shape[1]), F32)],
        compiler_params=pltpu.CompilerParams(vmem_limit_bytes=VMEM_LIMIT),
    )(a1, b1, a2, b2)


def _norm_mm(name, x, norm_w, b, mode="nt", x_cb=0, after=None):
    m = x.shape[0]
    k = norm_w.shape[1]
    n = b.shape[0] if mode == "nt" else b.shape[1]
    tn = _pick(n, 1408, 128)
    tm = _pick(m, 1152, 16)
    dotf = _nt if mode == "nt" else _nn
    extra = [] if after is None else [after]

    def body(x_ref, w_ref, b_ref, *rest):
        o_ref, u_ref = rest[-2:]

        @pl.when(pl.program_id(1) == 0)
        def _():
            u_ref[...] = _rms_fwd(x_ref[...], w_ref[...], k)[0].astype(u_ref.dtype)

        o_ref[...] = dotf(u_ref[...], b_ref[...])

    b_spec = (pl.BlockSpec((tn, k), lambda i, j: (j, 0)) if mode == "nt"
              else pl.BlockSpec((k, tn), lambda i, j: (0, j)))
    return pl.pallas_call(
        body, name=name, grid=(m // tm, n // tn),
        in_specs=[pl.BlockSpec((tm, k), lambda i, j: (i, x_cb)), pl.BlockSpec((1, k), lambda i, j: (0, 0)),
                  b_spec] + [_ANY_SPEC] * len(extra),
        out_specs=[pl.BlockSpec((tm, tn), lambda i, j: (i, j)), pl.BlockSpec((tm, k), lambda i, j: (i, 0))],
        out_shape=[jax.ShapeDtypeStruct((m, n), F32), jax.ShapeDtypeStruct((m, k), _MXU)],
        compiler_params=pltpu.CompilerParams(
            dimension_semantics=("arbitrary", "arbitrary"), vmem_limit_bytes=VMEM_LIMIT),
    )(x, norm_w, b, *extra)


def _pro_mm(name, fn, tiled, full, k, b, resid):
    m = resid.shape[0]
    n = b.shape[1]
    tm = _pick(m, 576, 16)
    n_in = len(tiled) + len(full)

    def body(*refs):
        i = pl.program_id(0)
        u = fn(i, *[r[...] for r in refs[:n_in]]).astype(_MXU)
        b_ref, r_ref, o_ref, u_ref = refs[n_in:]
        u_ref[...] = u
        o_ref[...] = r_ref[...] + _nn(u, b_ref[...])

    row = lambda w: pl.BlockSpec((tm, w), lambda i: (i, 0))
    in_specs = [_tile_spec(t, tm, m) for t in tiled]
    in_specs += [_whole_spec(x) for x in full] + [_whole_spec(b), row(n)]
    return pl.pallas_call(
        body, name=name, grid=(m // tm,), in_specs=in_specs, out_specs=[row(n), row(k)],
        out_shape=[jax.ShapeDtypeStruct((m, n), F32), jax.ShapeDtypeStruct((m, k), _MXU)],
        compiler_params=pltpu.CompilerParams(dimension_semantics=("parallel",), vmem_limit_bytes=VMEM_LIMIT),
    )(*[t.arr for t in tiled], *full, b, resid)


def _ffn_in(h2, norm_w, w_gate_t, w_up_t, conv_w8, conv_b):
    m, k = h2.shape
    n = w_gate_t.shape[0]
    tm = _pick(m, 576, 16)
    tn = _pick(n, 1408, 128)

    def body(x_ref, xp_ref, nw_ref, wg_ref, wu_ref, cw_ref, cb_ref, hn_ref, gp_ref, up_ref, act_ref):
        i = pl.program_id(0)
        nw = nw_ref[...]

        @pl.when(pl.program_id(1) == 0)
        def _():
            hn_ref[...] = _rms_fwd(x_ref[...], nw, k)[0].astype(hn_ref.dtype)

        hn = hn_ref[...]
        hn_prev = _rms_fwd(xp_ref[...], nw, k)[0].astype(_MXU)
        wg = wg_ref[...]
        gp = _nt(hn, wg)
        gp_prev = jnp.where(i > 0, _nt(hn_prev, wg), 0.0)
        up = _nt(hn, wu_ref[...])
        gate = _conv_fwd(gp, gp_prev, cw_ref[...], FFN_CONV) + cb_ref[...]
        gp_ref[...] = gp
        up_ref[...] = up
        act_ref[...] = (_silu_parts(gate)[0] * up).astype(act_ref.dtype)

    r8 = tm // 8
    tile = pl.BlockSpec((tm, tn), lambda i, j: (i, j))
    wblk = pl.BlockSpec((tn, k), lambda i, j: (j, 0))
    return pl.pallas_call(
        body, name="ffn_in", grid=(m // tm, n // tn),
        in_specs=[pl.BlockSpec((tm, k), lambda i, j: (i, 0)),
                  pl.BlockSpec((8, k), lambda i, j: (jnp.maximum(i * r8 - 1, 0), 0)),
                  pl.BlockSpec((1, k), lambda i, j: (0, 0)), wblk, wblk,
                  pl.BlockSpec((8, tn), lambda i, j: (0, j)), pl.BlockSpec((1, tn), lambda i, j: (0, j))],
        out_specs=[pl.BlockSpec((tm, k), lambda i, j: (i, 0)), tile, tile, tile],
        out_shape=[jax.ShapeDtypeStruct((m, k), _MXU), jax.ShapeDtypeStruct((m, n), F32),
                   jax.ShapeDtypeStruct((m, n), F32), jax.ShapeDtypeStruct((m, n), _MXU)],
        compiler_params=pltpu.CompilerParams(
            dimension_semantics=("arbitrary", "arbitrary"), vmem_limit_bytes=VMEM_LIMIT),
    )(h2, h2, norm_w, w_gate_t, w_up_t, conv_w8, conv_b)


def _mm_rows(name, a, b, mode, fn, tiled, full, outs, accs=(), tm_cap=576):
    a_list = list(a) if isinstance(a, (list, tuple)) else [a]
    b_list = list(b) if isinstance(b, (list, tuple)) else [b]
    na = len(a_list)
    m = a_list[0].shape[0]
    tm = _pick(m, tm_cap, 16)
    dotf = _nn if mode == "nn" else _nt
    n_in = len(tiled) + len(full)
    n_out = len(outs)
    first = 2 * na

    def body(*refs):
        i = pl.program_id(0)
        vals = [r[...] for r in refs[first:first + n_in]]
        acc = dotf(refs[0][...], refs[na][...])
        for p in range(1, na):
            acc = acc + dotf(refs[p][...], refs[na + p][...])
        o_t, o_a = fn(i, acc, *vals)
        for r, v in zip(refs[first + n_in:first + n_in + n_out], o_t):
            r[...] = v.astype(r.dtype)
        for r, v in zip(refs[first + n_in + n_out:], o_a):
            @pl.when(i == 0)
            def _():
                r[...] = v

            @pl.when(i > 0)
            def _():
                r[...] += v

    whole = lambda x: pl.BlockSpec(x.shape, lambda i, nd=x.ndim: (0,) * nd)
    in_specs = [pl.BlockSpec((tm, x.shape[1]), lambda i: (i, 0)) for x in a_list] + [_whole_spec(x) for x in b_list]
    in_specs += [_tile_spec(t, tm, m) for t in tiled]
    in_specs += [whole(x) for x in full]
    out_specs = [pl.BlockSpec((tm, w), lambda i: (i, 0)) for w, _ in outs]
    out_specs += [pl.BlockSpec((r, w), lambda i: (0, 0)) for r, w in accs]
    out_shape = [jax.ShapeDtypeStruct((m, w), dt) for w, dt in outs]
    out_shape += [jax.ShapeDtypeStruct((r, w), F32) for r, w in accs]
    return pl.pallas_call(
        body, name=name, grid=(m // tm,), in_specs=in_specs, out_specs=out_specs, out_shape=out_shape,
        compiler_params=pltpu.CompilerParams(dimension_semantics=("arbitrary",), vmem_limit_bytes=VMEM_LIMIT),
    )(*a_list, *b_list, *[t.arr for t in tiled], *full)


ATTN_Q_TILES = 4


def _attn_probs(q, k, row0):
    tq, tp = q.shape[0], k.shape[0]
    s = _nt(q, k) * (1.0 / math.sqrt(QK_HEAD))
    row = row0 + lax.broadcasted_iota(jnp.int32, (tq, tp), 0)
    col = lax.broadcasted_iota(jnp.int32, (tq, tp), 1)
    ok = (col <= row) & (col >= PAD)
    s = jnp.where(ok, s, NEG)
    m = jnp.max(s, axis=-1, keepdims=True)
    e = jnp.exp(s - m)
    return e * (1.0 / jnp.sum(e, axis=-1, keepdims=True))


def _attn_fwd(q, k, v):
    tp = q.shape[0]
    tq = tp // ATTN_Q_TILES

    def body(q_ref, k_ref, v_ref, o_ref):
        for i in range(ATTN_Q_TILES):
            rows = slice(i * tq, (i + 1) * tq)
            keys = slice(0, (i + 1) * tq)
            p = _attn_probs(q_ref[rows, :], k_ref[keys, :], i * tq)
            o_ref[rows, :] = _nn(p, v_ref[keys, :])

    return pl.pallas_call(
        body, name="attn_fwd", grid=(MLA_HEADS,),
        in_specs=[pl.BlockSpec((tp, HP), lambda h: (0, h)),
                  pl.BlockSpec((tp, HP), lambda h: (0, h)),
                  pl.BlockSpec((tp, V_HEAD), lambda h: (0, h))],
        out_specs=pl.BlockSpec((tp, V_HEAD), lambda h: (0, h)),
        out_shape=jax.ShapeDtypeStruct((tp, MLA_HEADS * V_HEAD), F32),
        compiler_params=pltpu.CompilerParams(dimension_semantics=("parallel",), vmem_limit_bytes=VMEM_LIMIT),
    )(q, k, v)


def _attn_bwd(q, k, v, do):
    tp = q.shape[0]
    tq = tp // ATTN_Q_TILES

    def body(q_ref, k_ref, v_ref, do_ref, dq_ref, dk_ref, dv_ref):
        for i in reversed(range(ATTN_Q_TILES)):
            rows = slice(i * tq, (i + 1) * tq)
            keys = slice(0, (i + 1) * tq)
            qb = q_ref[rows, :]
            kk = k_ref[keys, :]
            dob = do_ref[rows, :]
            p = _attn_probs(qb, kk, i * tq)
            dp = _nt(dob, v_ref[keys, :])
            delta = jnp.sum(p * dp, axis=-1, keepdims=True)
            ds = p * (dp - delta) * (1.0 / math.sqrt(QK_HEAD))
            dq_ref[rows, :] = _nn(ds, kk)
            if i == ATTN_Q_TILES - 1:
                dk_ref[...] = _tn(ds, qb)
                dv_ref[...] = _tn(p, dob)
            else:
                dk_ref[keys, :] += _tn(ds, qb)
                dv_ref[keys, :] += _tn(p, dob)

    full = lambda w: pl.BlockSpec((tp, w), lambda h: (0, h))
    return pl.pallas_call(
        body, name="attn_bwd", grid=(MLA_HEADS,),
        in_specs=[full(HP), full(HP), full(V_HEAD), full(V_HEAD)],
        out_specs=[full(HP), full(HP), full(V_HEAD)],
        out_shape=[jax.ShapeDtypeStruct((tp, MLA_HEADS * HP), F32),
                   jax.ShapeDtypeStruct((tp, MLA_HEADS * HP), F32),
                   jax.ShapeDtypeStruct((tp, MLA_HEADS * V_HEAD), F32)],
        compiler_params=pltpu.CompilerParams(dimension_semantics=("parallel",), vmem_limit_bytes=VMEM_LIMIT),
    )(q, k, v, do)


def _gdn_consts():
    c = DN_CHUNK
    r = lax.broadcasted_iota(jnp.int32, (c, c), 0)
    cc = lax.broadcasted_iota(jnp.int32, (c, c), 1)
    incl = r >= cc
    strict = r > cc
    return incl, strict


def _cumsum_rows(x, reverse=False):
    c = x.shape[0]
    row = lax.broadcasted_iota(jnp.int32, x.shape, 0)
    s = 1
    while s < c:
        if reverse:
            x = x + jnp.where(row < c - s, pltpu.roll(x, c - s, 0), 0.0)
        else:
            x = x + jnp.where(row >= s, pltpu.roll(x, s, 0), 0.0)
        s *= 2
    return x


def _each(fn, *lists):
    return [fn(*a) for a in zip(*lists)]


def _interleave(chains):
    chains = list(chains)
    while chains:
        for ch in list(chains):
            try:
                next(ch)
            except StopIteration:
                chains.remove(ch)


def _gdn_chunk_common(q_ref, k_ref, v_ref, g_ref, b_ref):
    c = DN_CHUNK
    incl, strict = _gdn_consts()
    sls = [(slice(c * sub, c * (sub + 1)), slice(DN_DIM * h, DN_DIM * (h + 1)))
           for sub in range(GDN_SUB_CHUNKS) for h in range(DN_HEADS)]
    q = [q_ref[sl] * (1.0 / math.sqrt(DN_DIM)) for sl in sls]
    k = [k_ref[sl] for sl in sls]
    v = [v_ref[sl] for sl in sls]
    g = [g_ref[sl] for sl in sls]
    beta = [b_ref[sl] for sl in sls]
    gc = [_cumsum_rows(x) for x in g]
    grow = [x.T[:c, :] for x in gc]
    kb = _each(jnp.multiply, k, beta)
    kk = _each(_nt, kb, k)
    qk = _each(_nt, q, k)
    gam = [jnp.exp(x) for x in gc]
    g_last = [_rowsum(x) for x in g]
    dm = [jnp.exp(jnp.where(incl, x[:, :c] - y, NEG)) for x, y in zip(gc, grow)]
    vb = _each(jnp.multiply, v, beta)
    kbg = _each(jnp.multiply, kb, gam)
    ek = [jnp.exp(x - y) for x, y in zip(g_last, gc)]
    kd = _each(jnp.multiply, k, ek)
    return dict(q=q, k=k, v=v, beta=beta, gc=gc, gam=gam, g_last=g_last, dm=dm, kb=kb, vb=vb,
                kbg=kbg, kk=kk, ek=ek, kd=kd, qk=qk, incl=incl, strict=strict, sls=sls)


def _gdn_fwd(proj, conv_w8, alog, dtb):
    tp = proj.shape[0]
    c = DN_CHUNK
    nch = tp // c
    blk = GDN_SUB_CHUNKS * c

    def body(x_ref, xp_ref, ab_ref, w8_ref, alog_ref, dtb_ref,
             o_ref, s_ref, t_ref, q_ref, k_ref, v_ref, g_ref, b_ref, s_scr):
        @pl.when(pl.program_id(0) == 0)
        def _():
            s_scr[...] = jnp.zeros_like(s_scr)

        staged, _ = _f_gdn_prep(pl.program_id(0), x_ref[...], xp_ref[...], ab_ref[...], w8_ref[...],
                                alog_ref[...], dtb_ref[...])
        for ref, val in zip((q_ref, k_ref, v_ref, g_ref, b_ref), staged):
            ref[...] = val
        eye = (lax.broadcasted_iota(jnp.int32, (c, c), 0) == lax.broadcasted_iota(jnp.int32, (c, c), 1)).astype(F32)
        x = _gdn_chunk_common(q_ref, k_ref, v_ref, g_ref, b_ref)
        heads = range(DN_HEADS)
        bp = [-jnp.where(x["strict"], kk * dm, 0.0) for kk, dm in zip(x["kk"], x["dm"])]
        t = [eye + b for b in bp]
        for _ in range(5):
            bp = [_nn(b, b, hp="3x") for b in bp]
            t = [tt + _nn(tt, b, hp="3x") for tt, b in zip(t, bp)]
        u = _each(_nn, t, x["vb"])
        w = _each(_nn, t, x["kbg"])
        qg = _each(jnp.multiply, x["q"], x["gam"])
        mqk = _each(jnp.multiply, x["qk"], x["dm"])
        s = [s_scr[h] for h in heads]
        for sub in range(GDN_SUB_CHUNKS):
            e = [DN_HEADS * sub + h for h in heads]
            v_new = [u[i] - _nn(w[i], s[h]) for h, i in zip(heads, e)]
            o = [_nn(qg[i], s[h]) + _nn(mqk[i], v_new[h]) for h, i in zip(heads, e)]
            s_new = [s[h] * jnp.exp(x["g_last"][i]) + _tn(x["kd"][i], v_new[h]) for h, i in zip(heads, e)]
            for h, i in zip(heads, e):
                s_ref[h, sub] = s[h]
                t_ref[h, sub] = t[i]
                o_ref[x["sls"][i]] = o[h]
            s = s_new
        for h in heads:
            s_scr[h] = s[h]

    sub = GDN_SUB_CHUNKS
    rb = lambda n: (n, 0)
    rows = pl.BlockSpec((blk, DN_WIDTH), rb)
    whole = lambda a: pl.BlockSpec(a.shape, lambda n: (0, 0))
    return pl.pallas_call(
        body, name="gdn_fwd", grid=(nch // sub,),
        in_specs=[pl.BlockSpec((blk, 3 * DN_WIDTH), rb),
                  pl.BlockSpec((8, 3 * DN_WIDTH), lambda n: (jnp.maximum(n * (blk // 8) - 1, 0), 0)),
                  pl.BlockSpec((blk, LANE), lambda n: (n, C_AB // LANE)),
                  whole(conv_w8), whole(alog), whole(dtb)],
        out_specs=[rows,
                   pl.BlockSpec((DN_HEADS, sub, DN_DIM, DN_DIM), lambda n: (0, n, 0, 0)),
                   pl.BlockSpec((DN_HEADS, sub, c, c), lambda n: (0, n, 0, 0))] + [rows] * 5,
        out_shape=[jax.ShapeDtypeStruct((tp, DN_WIDTH), F32),
                   jax.ShapeDtypeStruct((DN_HEADS, nch, DN_DIM, DN_DIM), F32),
                   jax.ShapeDtypeStruct((DN_HEADS, nch, c, c), F32)] + [jax.ShapeDtypeStruct((tp, DN_WIDTH), F32)] * 5,
        scratch_shapes=[pltpu.VMEM((DN_HEADS, DN_DIM, DN_DIM), F32)],
        compiler_params=pltpu.CompilerParams(dimension_semantics=("arbitrary",), vmem_limit_bytes=VMEM_LIMIT),
    )(proj, proj, proj, conv_w8, alog, dtb)


def _gdn_bwd(q, k, v, g, beta, s_all, t_all, do, proj, conv_w8, alog, dtb, after):
    tp = q.shape[0]
    c = DN_CHUNK
    nch = tp // c
    nblk = nch // GDN_SUB_CHUNKS
    blk = GDN_SUB_CHUNKS * c

    def body(q_ref, k_ref, v_ref, g_ref, b_ref, s_ref, t_ref, do_ref, x_ref, xp_ref, xn_ref, ab_ref,
             w8_ref, alog_ref, dtb_ref, _after_ref, dqkv_ref, dab_ref, dcw_ref, dalog_ref, ddtb_ref,
             ds_scr, dq_ref, dk_ref, dv_ref, dg_ref, db_ref, nxt_scr):
        step = pl.program_id(0)

        @pl.when(step == 0)
        def _():
            ds_scr[...] = jnp.zeros_like(ds_scr)
            nxt_scr[...] = jnp.zeros_like(nxt_scr)

        xs = _gdn_chunk_common(q_ref, k_ref, v_ref, g_ref, b_ref)

        ds_state = [ds_scr[h] for h in range(DN_HEADS)]

        def chain(sub, h):
            e = DN_HEADS * sub + h
            x = {key: (val[e] if isinstance(val, list) else val) for key, val in xs.items()}
            sl = x["sls"]
            qs, kx, vx, beta_, gam, dm = x["q"], x["k"], x["v"], x["beta"], x["gam"], x["dm"]
            kb, vb, kbg, kd, ek = x["kb"], x["vb"], x["kbg"], x["kd"], x["ek"]
            t = t_ref[h, sub]
            s = s_ref[h, sub]
            dsn = ds_state[h]
            dob = do_ref[sl]
            eg_last = jnp.exp(x["g_last"])
            u = _nn(t, vb)
            w = _nn(t, kbg)
            mqk = x["qk"] * dm
            qd = qs * gam
            dqd = _nt(dob, s)
            dkd_pre = _nn(kd, dsn)
            yield
            v_new = u - _nn(w, s)
            dv_new = _tn(mqk, dob) + dkd_pre
            dq = dqd * gam
            dgam = jnp.sum(dqd * qs, axis=1, keepdims=True)
            yield
            ds_state[h] = _tn(qd, dob) + eg_last * dsn - _tn(w, dv_new)
            dmm = jnp.where(x["incl"], _nt(dob, v_new), 0.0)
            dkd = _nt(v_new, dsn)
            dw = -_nt(dv_new, s)
            dvb = _tn(t, dv_new)
            dt = _nt(dv_new, vb)
            yield
            dqk = dmm * dm
            e_mat = dmm * mqk
            dq = dq + _nn(dqk, kx)
            dk = _tn(dqk, qs) + dkd * ek
            e1 = jnp.sum(dkd * kd, axis=1, keepdims=True)
            dgc = -e1
            dg_last = jnp.sum(e1) + eg_last * jnp.sum(s * dsn)
            dt = dt + _nt(dw, kbg)
            dkbg = _tn(t, dw)
            yield
            tdt = _tn(t, dt, hp="3x")
            yield
            da = jnp.where(x["strict"], -_nt(tdt, t, hp="3x"), 0.0)
            yield
            dkk = da * dm
            e_mat = e_mat + da * x["kk"] * dm
            dkb = _nn(dkk, kx) + dkbg * gam
            dk = dk + _tn(dkk, kb)
            dgam = dgam + jnp.sum(dkbg * kb, axis=1, keepdims=True)
            yield
            dk = dk + dkb * beta_
            dbeta = jnp.sum(dkb * kx, axis=1, keepdims=True) + jnp.sum(dvb * vx, axis=1, keepdims=True)
            dv = dvb * beta_
            dgc = dgc + jnp.sum(e_mat, axis=1, keepdims=True) + dgam * gam
            dgc = dgc - jnp.sum(e_mat.T, axis=1, keepdims=True)
            yield
            dg = _cumsum_rows(dgc, reverse=True) + dg_last
            yield
            dq_ref[sl] = dq * (1.0 / math.sqrt(DN_DIM))
            dk_ref[sl] = dk
            dv_ref[sl] = dv
            dg_ref[sl] = dg
            db_ref[sl] = jnp.broadcast_to(dbeta, (c, LANE))

        chains = []
        for sub in reversed(range(GDN_SUB_CHUNKS)):
            new = [chain(sub, h) for h in range(DN_HEADS)]
            for _ in range(3):
                for ch in new:
                    next(ch)
            chains += new
        _interleave(chains)
        for h in range(DN_HEADS):
            ds_scr[h] = ds_state[h]

        dq, dk, dv = dq_ref[...], dk_ref[...], dv_ref[...]
        outs, accs = _f_gdn_prep_bwd(
            nblk - 1 - step, x_ref[...], xp_ref[...], xn_ref[...], ab_ref[...], dq, nxt_scr[0], dk, nxt_scr[1],
            dv, nxt_scr[2], dg_ref[...], db_ref[...], w8_ref[...], alog_ref[...], dtb_ref[...], nt=nblk)
        nxt_scr[0] = dq[:8]
        nxt_scr[1] = dk[:8]
        nxt_scr[2] = dv[:8]
        dqkv_ref[...] = outs[0].astype(dqkv_ref.dtype)
        dab_ref[...] = outs[1].astype(dab_ref.dtype)
        for ref, val in zip((dcw_ref, dalog_ref, ddtb_ref), accs):
            @pl.when(step == 0)
            def _():
                ref[...] = val

            @pl.when(step > 0)
            def _():
                ref[...] += val

    sub = GDN_SUB_CHUNKS
    r8 = blk // 8
    rb = lambda n: (nblk - 1 - n, 0)
    hs = lambda n: (0, nblk - 1 - n, 0, 0)
    rows = pl.BlockSpec((blk, DN_WIDTH), rb)
    whole = lambda a: pl.BlockSpec(a.shape, lambda n: (0,) * a.ndim)
    wide = 3 * DN_WIDTH
    return pl.pallas_call(
        body, name="gdn_bwd", grid=(nblk,),
        in_specs=[rows] * 5
        + [pl.BlockSpec((DN_HEADS, sub, DN_DIM, DN_DIM), hs), pl.BlockSpec((DN_HEADS, sub, c, c), hs), rows,
           pl.BlockSpec((blk, wide), rb),
           pl.BlockSpec((8, wide), lambda n: (jnp.maximum((nblk - 1 - n) * r8 - 1, 0), 0)),
           pl.BlockSpec((8, wide), lambda n: (jnp.minimum((nblk - n) * r8, tp // 8 - 1), 0)),
           pl.BlockSpec((blk, LANE), lambda n: (nblk - 1 - n, C_AB // LANE)),
           whole(conv_w8), whole(alog), whole(dtb), _ANY_SPEC],
        out_specs=[pl.BlockSpec((blk, wide), rb), pl.BlockSpec((blk, LANE), rb),
                   whole(conv_w8), whole(alog), whole(dtb)],
        out_shape=[jax.ShapeDtypeStruct((tp, wide), _MXU), jax.ShapeDtypeStruct((tp, LANE), _MXU),
                   jax.ShapeDtypeStruct(conv_w8.shape, F32), jax.ShapeDtypeStruct(alog.shape, F32),
                   jax.ShapeDtypeStruct(dtb.shape, F32)],
        scratch_shapes=[pltpu.VMEM((DN_HEADS, DN_DIM, DN_DIM), F32)] + [pltpu.VMEM((blk, DN_WIDTH), F32)] * 5
        + [pltpu.VMEM((3, 8, DN_WIDTH), F32)],
        compiler_params=pltpu.CompilerParams(dimension_semantics=("arbitrary",), vmem_limit_bytes=VMEM_LIMIT),
    )(q, k, v, g, beta, s_all, t_all, do, proj, proj, proj, proj, conv_w8, alog, dtb, after)


def _silu_parts(x):
    s = _sigmoid(x)
    return x * s, s * (1.0 + x * (1.0 - s))


def _f_rms_bwd_add(i, x, dy, dres, w, *, mask_pad):
    dx, dwr = _rms_bwd(x, w, dy, x.shape[1])
    out = dres + dx
    if mask_pad:
        out = jnp.where(_row_ids(i, x.shape[0]) >= PAD, out, 0.0)
    return (out,), (_rowsum(dwr),)


def _rope(x, cos, sin_s):
    return x * cos + _swap_halves(x) * sin_s


def _rope_t(dy, cos, sin_s):
    return dy * cos + _swap_halves(dy * sin_s)


def _f_mla_qk(i, qf, kvf, kpe, cos, sin_s, qw, kw):
    qs, ks, vs = [], [], []
    for h in range(MLA_HEADS):
        qn, _ = _rms_fwd(qf[:, HP * h:HP * (h + 1)], qw, QK_HEAD)
        qs += [qn[:, :QK_NOPE], _rope(qn[:, QK_NOPE:], cos, sin_s)]
        kh = jnp.concatenate([kvf[:, HP * h:HP * h + QK_NOPE], kpe], axis=1)
        kn, _ = _rms_fwd(kh, kw, QK_HEAD)
        ks += [kn[:, :QK_NOPE], _rope(kn[:, QK_NOPE:], cos, sin_s)]
        vs.append(kvf[:, HP * h + QK_NOPE:HP * (h + 1)])
    return (jnp.concatenate(qs, axis=1), jnp.concatenate(ks, axis=1), jnp.concatenate(vs, axis=1)), ()


def _f_mla_front(i, ql, kvl, kpe, cos, sin_s, qaw, kvaw, wq_t, wkv, qw, kw):
    qn = _rms_fwd(ql, qaw, Q_LORA)[0].astype(_MXU)
    kvn = _rms_fwd(kvl, kvaw, KV_LORA)[0].astype(_MXU)
    qf = _nt(qn, wq_t)
    kvf = _nn(kvn, wkv)
    (q, k, v), _ = _f_mla_qk(i, qf, kvf, kpe, cos, sin_s, qw, kw)
    return (qn, kvn, qf, kvf, q, k, v), ()


def _f_mla_back(i, qf, kvf, kpe, cos, sin_s, dq, dk, dv, ql, kvl, qaw, kvaw, wq_t, wkv, qw, kw):
    (dqf, dkvf, dkpe), (dqw, dkw) = _f_mla_qk_bwd(i, qf, kvf, kpe, cos, sin_s, dq, dk, dv, qw, kw)
    dqf = dqf.astype(_MXU)
    dkvf = dkvf.astype(_MXU)
    dql, dqaw = _rms_bwd(ql, qaw, _nn(dqf, wq_t), Q_LORA)
    dkvl, dkvaw = _rms_bwd(kvl, kvaw, _nt(dkvf, wkv), KV_LORA)
    return (dqf, dkvf, dkpe, dql, dkvl), (dqw, dkw, _rowsum(dqaw), _rowsum(dkvaw))


def _f_mla_qk_bwd(i, qf, kvf, kpe, cos, sin_s, dq, dk, dv, qw, kw):
    dqf, dkvf = [], []
    dkpe = None
    dqw = None
    dkw = None
    for h in range(MLA_HEADS):
        dqh = dq[:, HP * h:HP * (h + 1)]
        dqn = jnp.concatenate([dqh[:, :QK_NOPE], _rope_t(dqh[:, QK_NOPE:], cos, sin_s)], axis=1)
        dx, dwr = _rms_bwd(qf[:, HP * h:HP * (h + 1)], qw, dqn, QK_HEAD)
        dqf.append(dx)
        dqw = _rowsum(dwr) if dqw is None else dqw + _rowsum(dwr)
        dkh = dk[:, HP * h:HP * (h + 1)]
        dkn = jnp.concatenate([dkh[:, :QK_NOPE], _rope_t(dkh[:, QK_NOPE:], cos, sin_s)], axis=1)
        kh = jnp.concatenate([kvf[:, HP * h:HP * h + QK_NOPE], kpe], axis=1)
        dx, dwr = _rms_bwd(kh, kw, dkn, QK_HEAD)
        dkvf += [dx[:, :QK_NOPE], dv[:, V_HEAD * h:V_HEAD * (h + 1)]]
        dkpe = dx[:, QK_NOPE:] if dkpe is None else dkpe + dx[:, QK_NOPE:]
        dkw = _rowsum(dwr) if dkw is None else dkw + _rowsum(dwr)
    return (jnp.concatenate(dqf, axis=1), jnp.concatenate(dkvf, axis=1), dkpe), (dqw, dkw)


def _gdn_act(i, x, halo, w8):
    halo = jnp.where(i > 0, halo, 0.0)
    c = _conv_fwd(x, halo, w8, DN_CONV)
    act, dact = _silu_parts(c)
    return act, dact


def _spread_heads(ab):
    tm = ab.shape[0]
    return jnp.concatenate([jnp.broadcast_to(ab[:, h:h + 1], (tm, DN_DIM)) for h in range(2 * DN_HEADS)], axis=1)


def _gather_heads(x):
    tm = x.shape[0]
    lane = lax.broadcasted_iota(jnp.int32, (tm, LANE), 1)
    out = jnp.zeros((tm, LANE), F32)
    for h in range(2 * DN_HEADS):
        out = out + jnp.where(lane == h, x[:, DN_DIM * h:DN_DIM * h + 1], 0.0)
    return out


def _gate_parts(ab, dtb):
    lane1 = lax.broadcasted_iota(jnp.int32, (1, LANE), 1)
    dtb_c = jnp.zeros((1, LANE), F32)
    for h in range(DN_HEADS):
        dtb_c = dtb_c + jnp.where(lane1 == h, dtb[:, DN_DIM * h:DN_DIM * h + 1], 0.0)
    pre = ab + dtb_c
    sig = _sigmoid(pre)
    lane = lax.broadcasted_iota(jnp.int32, ab.shape, 1)
    return jnp.where(lane < DN_HEADS, _softplus(pre), sig), sig


def _f_gdn_prep(i, x, halo, ab, w8, alog, dtb):
    tm = x.shape[0]
    act, _ = _gdn_act(i, x, halo, w8)
    outs = []
    for part in range(2):
        for h in range(DN_HEADS):
            t = act[:, DN_WIDTH * part + DN_DIM * h:DN_WIDTH * part + DN_DIM * (h + 1)]
            outs.append(t * lax.rsqrt(jnp.sum(t * t, axis=-1, keepdims=True) + EPS))
    q = jnp.concatenate(outs[:DN_HEADS], axis=1)
    k = jnp.concatenate(outs[DN_HEADS:], axis=1)
    v = act[:, 2 * DN_WIDTH:]
    abb = _spread_heads(ab)
    valid = _row_ids(i, tm) >= PAD
    g = jnp.where(valid, -jnp.exp(alog) * _softplus(abb[:, :DN_WIDTH] + dtb), 0.0)
    beta = jnp.where(valid, _sigmoid(abb[:, DN_WIDTH:]), 0.0)
    return (q, k, v, g, beta), ()


def _f_gdn_prep_bwd(i, x, x_prev, x_next, ab, dq, dq_next, dk, dk_next, dv, dv_next, dg, dbeta,
                    w8, alog, dtb, *, nt):
    tm = x.shape[0]
    x_prev = jnp.where(i > 0, x_prev, 0.0)
    more = i < nt - 1
    ext = lambda t, t_next: jnp.concatenate([t, jnp.where(more, t_next, 0.0)], axis=0)
    taps = _conv_taps(jnp.concatenate([x, x_next], axis=0), x_prev, DN_CONV)
    c = _conv_from_taps(taps, w8)
    act, dact = _silu_parts(c)
    douts = []
    for part, dd in enumerate((ext(dq, dq_next), ext(dk, dk_next))):
        for h in range(DN_HEADS):
            t = act[:, DN_WIDTH * part + DN_DIM * h:DN_WIDTH * part + DN_DIM * (h + 1)]
            r = lax.rsqrt(jnp.sum(t * t, axis=-1, keepdims=True) + EPS)
            y = t * r
            dy = dd[:, DN_DIM * h:DN_DIM * (h + 1)]
            douts.append(r * (dy - y * jnp.sum(dy * y, axis=-1, keepdims=True)))
    douts.append(ext(dv, dv_next))
    dc = jnp.concatenate(douts, axis=1) * dact
    dqkv = _conv_bwd_x(dc[:tm], dc[tm:], w8, DN_CONV)
    dconv_w = _conv_bwd_w_taps(dc[:tm], taps)
    sp_beta, sig = _gate_parts(ab, dtb)
    spread = _spread_heads(sp_beta)
    valid = _row_ids(i, tm) >= PAD
    ea = jnp.exp(alog)
    g = -ea * spread[:, :DN_WIDTH]
    dg = jnp.where(valid, dg, 0.0)
    dbeta = jnp.where(valid, dbeta, 0.0)
    da = dg * (-ea) * _spread_heads(sig)[:, :DN_WIDTH]
    beta = spread[:, DN_WIDTH:]
    db = dbeta * beta * (1.0 - beta)
    dab = _gather_heads(jnp.concatenate([da, db], axis=1))
    return (dqkv, dab), (dconv_w, _rowsum(dg * g), _rowsum(da))


def _f_mix(i, o_mla, o_dn, z, w_mla, w_dn):
    tm = o_mla.shape[0]
    valid = _row_ids(i, tm) >= PAD
    outs = []
    for h in range(MLA_HEADS):
        y, _ = _rms_fwd(o_mla[:, V_HEAD * h:V_HEAD * (h + 1)], w_mla, V_HEAD)
        outs.append(jnp.where(valid, y, 0.0))
    for h in range(DN_HEADS):
        y, _ = _rms_fwd(o_dn[:, DN_DIM * h:DN_DIM * (h + 1)], w_dn, DN_DIM)
        outs.append(y * _silu_parts(z[:, DN_DIM * h:DN_DIM * (h + 1)])[0])
    return (jnp.concatenate(outs, axis=1),), ()


def _f_mix_bwd(i, o_mla, o_dn, z, dy_mla, dy_dn, w_mla, w_dn):
    tm = o_mla.shape[0]
    valid = _row_ids(i, tm) >= PAD
    d_mla, d_dn, d_z = [], [], []
    dw_mla = None
    dw_dn = None
    for h in range(MLA_HEADS):
        sl = slice(V_HEAD * h, V_HEAD * (h + 1))
        dx, dwr = _rms_bwd(o_mla[:, sl], w_mla, jnp.where(valid, dy_mla[:, sl], 0.0), V_HEAD)
        d_mla.append(dx)
        dw_mla = _rowsum(dwr) if dw_mla is None else dw_mla + _rowsum(dwr)
    for h in range(DN_HEADS):
        sl = slice(DN_DIM * h, DN_DIM * (h + 1))
        y, _ = _rms_fwd(o_dn[:, sl], w_dn, DN_DIM)
        sz, dsz = _silu_parts(z[:, sl])
        d_z.append(dy_dn[:, sl] * y * dsz)
        dx, dwr = _rms_bwd(o_dn[:, sl], w_dn, dy_dn[:, sl] * sz, DN_DIM)
        d_dn.append(dx)
        dw_dn = _rowsum(dwr) if dw_dn is None else dw_dn + _rowsum(dwr)
    return ((jnp.concatenate(d_mla, axis=1), jnp.concatenate(d_dn, axis=1), jnp.concatenate(d_z, axis=1)),
            (dw_mla, dw_dn))


def _f_ffn_act_bwd(i, gp, gp_prev, gp_next, up, up_next, dact, dact_next, w8, b, *, nt):
    tm = gp.shape[0]
    gp_prev = jnp.where(i > 0, gp_prev, 0.0)
    dact_next = jnp.where(i < nt - 1, dact_next, 0.0)
    cat = lambda t, t_next: jnp.concatenate([t, t_next], axis=0)
    taps = _conv_taps(cat(gp, gp_next), gp_prev, FFN_CONV)
    gate = _conv_from_taps(taps, w8) + b
    sg, dsg = _silu_parts(gate)
    dact_e = cat(dact, dact_next)
    dgate = dact_e * cat(up, up_next) * dsg
    dgate_pre = _conv_bwd_x(dgate[:tm], dgate[tm:], w8, FFN_CONV)
    dup = dact * sg[:tm]
    return (dgate_pre, dup), (_conv_bwd_w_taps(dgate[:tm], taps), _rowsum(dgate[:tm]))


def _f_loss(i, h3, tgt):
    tm = h3.shape[0]
    diff = jnp.where(_row_ids(i, tm) >= ROW0, h3 - tgt, 0.0)
    part = 0.5 * jnp.sum(diff * diff) * (1.0 / D_MODEL)
    return (diff * (1.0 / D_MODEL),), (jnp.full((1, LANE), part, F32),)


def _local_step(h0, tgt, w, token, late_weights, grads_ready):
    tp = h0.shape[0]
    proj, u = _norm_mm("in_proj", h0, w["attn_norm_w"], w["w_in"], after=token)
    p_qkv = lambda kind="cur": _In(proj, 3 * DN_WIDTH, 0, kind)
    p_z = _In(proj, DN_WIDTH, C_Z // DN_WIDTH)
    p_ql = _In(proj, Q_LORA, C_QL // Q_LORA)
    p_kvl = _In(proj, KV_LORA, C_KVL // KV_LORA)
    p_kpe = _In(proj, LANE, C_KPE // LANE)
    p_ab = _In(proj, LANE, C_AB // LANE)
    cos, sin_s = _In(w["cos"]), _In(w["sin_s"])

    mla_w = [w["q_a_norm_w"], w["kv_a_norm_w"], w["w_q_b"], w["w_kv_b"], w["q_norm_w"], w["k_norm_w"]]
    tm_mla = _pick(tp, 288, 16)
    wide = MLA_HEADS * HP
    qn, kvn, qf, kvf, q, k, v = _rows(
        "mla_front", _f_mla_front, [p_ql, p_kvl, p_kpe, cos, sin_s], mla_w,
        [(Q_LORA, _MXU), (KV_LORA, _MXU), (wide, F32), (wide, F32), (wide, _MXU), (wide, _MXU),
         (MLA_HEADS * V_HEAD, _MXU)], tm=tm_mla)
    o_mla = _attn_fwd(q, k, v)

    dn_w = [w["dn_conv_w"], w["alog_b"], w["dtb_b"]]
    o_dn, s_all, t_all, gq, gk, gv, gg, gb = _gdn_fwd(proj, *dn_w)

    out_w = [w["mla_out_norm_w"], w["dn_out_norm_w"]]
    w = dict(w, **late_weights((o_mla, o_dn), _LATE[:3]))
    h2, mixed = _pro_mm("mix_out_proj", lambda i, *t: _f_mix(i, *t)[0][0], [_In(o_mla), _In(o_dn), p_z], out_w,
                        D_MODEL, w["w_out"], h0)

    ffn_w = [w["ffn_conv_w"], w["ffn_conv_b"]]
    hn, gate_pre, up, act = _ffn_in(h2, w["ffn_norm_w"], w["w_gate"], w["w_up"], *ffn_w)
    w = dict(w, **late_weights(act, _LATE[3:]))
    dh3, loss = _mm_rows("ffn_down_loss", act, w["w_down"], "nn", lambda i, y, r, t: _f_loss(i, r + y, t),
                         [_In(h2), _In(tgt)], [], [(D_MODEL, F32)], [(1, LANE)])

    g = {}
    dact = _mm("ffn_down_dx", dh3, w["w_down"], "nt")
    g["w_down"] = _mm("ffn_down_dw", act, dh3, "tn", out_dtype=_MXU)
    dgate_pre, dup, g["ffn_conv_w"], g["ffn_conv_b"] = _rows(
        "ffn_act_bwd", functools.partial(_f_ffn_act_bwd, nt=tp // tm_mla),
        [_In(gate_pre), _In(gate_pre, kind="prev"), _In(gate_pre, kind="next"), _In(up), _In(up, kind="next"),
         _In(dact), _In(dact, kind="next")], ffn_w,
        [(D_FF, _MXU), (D_FF, _MXU)], [(8, D_FF), (1, D_FF)], tm=tm_mla)
    g["w_gate"], g["w_up"] = _mm_tn2("ffn_gate_up_dw", dgate_pre, dup, hn, out_dtype=_MXU)
    tok = grads_ready(g, ("w_down", "w_gate", "w_up"))
    dh2, g["ffn_norm_w"] = _mm_rows(
        "ffn_gate_up_dx_rms", [dgate_pre, dup], [w["w_gate"], w["w_up"]], "nn",
        lambda i, dy, x, dres, nw, _tok: _f_rms_bwd_add(i, x, dy, dres, nw, mask_pad=True),
        [_In(h2), _In(dh3)], [w["ffn_norm_w"], tok], [(D_MODEL, F32)], [(1, D_MODEL)], tm_cap=288)

    g["w_out"] = _mm("out_proj_dw", mixed, dh2, "tn", out_dtype=_MXU)
    half = MLA_HEADS * V_HEAD
    do_mla, do_dn, dz, g["mla_out_norm_w"], g["dn_out_norm_w"] = _mm_rows(
        "out_proj_dx_mix", dh2, w["w_out"], "nt",
        lambda i, dm, om, od, z, wm, wd: _f_mix_bwd(i, om, od, z, dm[:, :half], dm[:, half:], wm, wd),
        [_In(o_mla), _In(o_dn), p_z], out_w,
        [(half, F32), (DN_WIDTH, F32), (DN_WIDTH, _MXU)], [(1, V_HEAD), (1, DN_DIM)])

    dq, dk, dv = _attn_bwd(q, k, v, do_mla)
    dqf, dkvf, dkpe, dql, dkvl, g["q_norm_w"], g["k_norm_w"], g["q_a_norm_w"], g["kv_a_norm_w"] = _rows(
        "mla_back", _f_mla_back,
        [_In(qf), _In(kvf), p_kpe, cos, sin_s, _In(dq), _In(dk), _In(dv), p_ql, p_kvl], mla_w,
        [(wide, _MXU), (wide, _MXU), (LANE, _MXU), (Q_LORA, _MXU), (KV_LORA, _MXU)],
        [(1, HP), (1, HP), (1, Q_LORA), (1, KV_LORA)], tm=tm_mla)
    g["w_q_b"], g["w_kv_b"] = _mm_tn_pair("mla_b_dw", dqf, qn, kvn, dkvf)
    tok = grads_ready(g, ("w_out", "w_q_b", "w_kv_b"))

    dqkv, dab, g["dn_conv_w"], g["alog_b"], g["dtb_b"] = _gdn_bwd(
        gq, gk, gv, gg, gb, s_all, t_all, do_dn, proj, *dn_w, tok)

    dproj = jnp.concatenate([dqkv, dz, dql, dkvl, dkpe, dab], axis=1)
    g["w_in"] = _mm("in_proj_dw", dproj, u, "tn", out_dtype=_MXU)
    tok = grads_ready(g, ("w_in",))
    dh0, g["attn_norm_w"] = _mm_rows(
        "in_proj_dx_rms", dproj, w["w_in"], "nn",
        lambda i, du, x, dres, nw, _tok: _f_rms_bwd_add(i, x, du, dres, nw, mask_pad=False),
        [_In(h0), _In(dh2)], [w["attn_norm_w"], tok], [(D_MODEL, F32)], [(1, D_MODEL)])
    return loss, dh0, g


def _w_in_to_padded(w):
    c1, c2, c3 = Q_LORA, Q_LORA + KV_LORA, Q_LORA + KV_LORA + QK_ROPE
    c4 = c3 + 3 * DN_WIDTH
    c5 = c4 + DN_WIDTH
    z = lambda n: jnp.zeros((n, w.shape[1]), w.dtype)
    return jnp.concatenate([w[c3:c4], w[c4:c5], w[:c1], w[c1:c2], w[c2:c3], z(LANE - QK_ROPE),
                            w[c5:], z(LANE - 2 * DN_HEADS)], axis=0)


def _w_in_from_padded(g):
    return jnp.concatenate([g[C_QL:C_QL + Q_LORA], g[C_KVL:C_KVL + KV_LORA], g[C_KPE:C_KPE + QK_ROPE],
                            g[:C_Z + DN_WIDTH], g[C_AB:C_AB + 2 * DN_HEADS]], axis=0)


def _w_q_b_to_padded(w):
    r = w.shape[1]
    w = w.reshape(MLA_HEADS, QK_HEAD, r)
    return jnp.pad(w, ((0, 0), (0, HP - QK_HEAD), (0, 0))).reshape(MLA_HEADS * HP, r)


def _w_q_b_from_padded(g):
    r = g.shape[1]
    return g.reshape(MLA_HEADS, HP, r)[:, :QK_HEAD].reshape(MLA_HEADS * QK_HEAD, r)


def _pad_rows8(w):
    return jnp.pad(w, ((0, 8 - w.shape[0]), (0, 0)))


def _prepare(full, tp):
    w = {}
    mx = lambda a: a.astype(_MXU)
    w["attn_norm_w"] = full["attn_norm_w"]
    w["w_in"] = mx(_w_in_to_padded(full["w_in"]))
    w["q_a_norm_w"] = full["q_a_norm_w"]
    w["kv_a_norm_w"] = full["kv_a_norm_w"]
    w["w_q_b"] = mx(_w_q_b_to_padded(full["w_q_b"]))
    w["w_kv_b"] = mx(full["w_kv_b"])
    w["q_norm_w"] = jnp.pad(full["q_norm_w"], ((0, 0), (0, HP - QK_HEAD)))
    w["k_norm_w"] = jnp.pad(full["k_norm_w"], ((0, 0), (0, HP - QK_HEAD)))
    w["mla_out_norm_w"] = full["mla_out_norm_w"]
    w["dn_out_norm_w"] = full["dn_out_norm_w"]
    w["dn_conv_w"] = _pad_rows8(full["dn_conv_w"])
    w["alog_b"] = jnp.repeat(full["dn_A_log"], DN_DIM, axis=1)
    w["dtb_b"] = jnp.repeat(full["dn_dt_bias"], DN_DIM, axis=1)
    w["ffn_norm_w"] = full["ffn_norm_w"]
    w["ffn_conv_w"] = _pad_rows8(full["ffn_conv_w"])
    w["ffn_conv_b"] = full["ffn_conv_b"]
    for n in _LATE:
        if n in full:
            w[n] = mx(full[n])
    half = QK_ROPE // 2
    inv = ROPE_THETA ** (-jnp.arange(half, dtype=F32) / half)
    ang = (jnp.arange(tp, dtype=jnp.int32) - PAD).astype(F32)[:, None] * inv[None, :]
    zc = jnp.zeros((tp, LANE - QK_ROPE), F32)
    w["cos"] = jnp.concatenate([jnp.cos(ang), jnp.cos(ang), zc], axis=1)
    w["sin_s"] = jnp.concatenate([-jnp.sin(ang), jnp.sin(ang), zc], axis=1)
    return w


def _grads_to_natural(g):
    convert = {
        "w_in": ("w_in", _w_in_from_padded),
        "w_q_b": ("w_q_b", _w_q_b_from_padded),
        "q_norm_w": ("q_norm_w", lambda a: a[:, :QK_HEAD]),
        "k_norm_w": ("k_norm_w", lambda a: a[:, :QK_HEAD]),
        "dn_conv_w": ("dn_conv_w", lambda a: a[:DN_CONV]),
        "ffn_conv_w": ("ffn_conv_w", lambda a: a[:FFN_CONV]),
        "alog_b": ("dn_A_log", lambda a: a[:, ::DN_DIM]),
        "dtb_b": ("dn_dt_bias", lambda a: a[:, ::DN_DIM]),
    }
    n = {}
    for key, a in g.items():
        name, fn = convert.get(key, (key, lambda t: t))
        n[name] = fn(a)
    return n


_MESH = pl.DeviceIdType.MESH
_ANY = pl.BlockSpec(memory_space=pl.ANY)
_CHIP_FLIPS = ((1, 0), (0, 1), (1, 1))


def _me():
    return lax.axis_index("x"), lax.axis_index("y"), lax.axis_index("c")


def _all_gather(name, blk, after):
    after = list(after)

    def body(x_ref, *rest):
        out_ref, send_sems, recv_sems, local_sem = rest[len(after):]
        x, y, c = _me()
        me, sib = (x, y, c), (x, y, 1 - c)
        chips = [(x ^ fx, y ^ fy) for fx, fy in _CHIP_FLIPS]

        def slot(p):
            return out_ref.at[4 * p[0] + 2 * p[1] + p[2]]

        def copy(k, block, to, src=None):
            return pltpu.make_async_remote_copy(
                src_ref=slot(block) if src is None else src, dst_ref=slot(block),
                send_sem=send_sems.at[k], recv_sem=recv_sems.at[k], device_id=to, device_id_type=_MESH)

        mine = pltpu.make_async_copy(x_ref, slot(me), local_sem)
        mine.start()
        first = [copy(0, me, sib, src=x_ref)]
        first += [copy(1 + j, me, (*chip, c), src=x_ref) for j, chip in enumerate(chips)]
        for cp in first:
            cp.start()
        passed = [copy(4 + j, (*chip, c), sib) for j, chip in enumerate(chips)]
        for j, chip in enumerate(chips):
            copy(1 + j, (*chip, c), me).wait_recv()
            passed[j].start()
        copy(0, sib, me).wait_recv()
        for j, chip in enumerate(chips):
            copy(4 + j, (*chip, 1 - c), me).wait_recv()
        for cp in first + passed:
            cp.wait_send()
        mine.wait()

    return pl.pallas_call(
        body, name=name, in_specs=[_ANY] * (1 + len(after)), out_specs=_ANY,
        out_shape=jax.ShapeDtypeStruct((N_DEV,) + blk.shape, blk.dtype),
        scratch_shapes=[pltpu.SemaphoreType.DMA((7,)), pltpu.SemaphoreType.DMA((7,)), pltpu.SemaphoreType.DMA],
    )(blk, *after)


def _row_tile(r):
    divs = [d for d in range(16, min(r, 512) + 1, 16) if r % d == 0]
    return divs[-1] if divs else r


def _adam_math(g, w, m, v):
    m_new = ADAM_B1 * m + (1.0 - ADAM_B1) * g
    v_new = ADAM_B2 * v + (1.0 - ADAM_B2) * (g * g)
    m_hat = m_new / (1.0 - ADAM_B1 ** ADAM_STEP)
    v_hat = v_new / (1.0 - ADAM_B2 ** ADAM_STEP)
    return -ADAM_LR * (m_hat / (jnp.sqrt(v_hat) + ADAM_EPS) + ADAM_WD * w), m_new, v_new


def _adam_vectors(name, row, items, ws, ms, vs):
    k = len(items)

    def body(row_ref, *refs):
        w_refs, m_refs, v_refs = refs[:k], refs[k:2 * k], refs[2 * k:3 * k]
        outs = refs[3 * k:]
        for idx, (off, n, per_head) in enumerate(items):
            if per_head:
                spread = row_ref[:, off:off + DN_WIDTH]
                lane = lax.broadcasted_iota(jnp.int32, (1, LANE), 1)
                g = jnp.zeros((1, LANE), F32)
                for h in range(DN_HEADS):
                    g = g + jnp.where(lane == h, spread[:, DN_DIM * h:DN_DIM * h + 1], 0.0)
                g = g[:, :n]
            else:
                g = row_ref[:, off:off + n]
            d, m_new, v_new = _adam_math(g, w_refs[idx][...], m_refs[idx][...], v_refs[idx][...])
            for kind, val in enumerate((g, d, m_new, v_new)):
                outs[kind * k + idx][...] = val

    shapes = [jax.ShapeDtypeStruct((1, n), F32) for _, n, _ in items]
    res = pl.pallas_call(body, name=name, out_shape=shapes * 4)(row, *ws, *ms, *vs)
    return [list(res[kind * k:(kind + 1) * k]) for kind in range(4)]


def _adam_arrays(name, gs, ws, ms, vs):
    k = len(gs)

    def body(*refs):
        outs = refs[4 * k:]
        for idx in range(k):
            res = _adam_math(refs[idx][...], refs[k + idx][...], refs[2 * k + idx][...], refs[3 * k + idx][...])
            for kind, val in enumerate(res):
                outs[kind * k + idx][...] = val

    shapes = [jax.ShapeDtypeStruct(w.shape, F32) for w in ws]
    res = pl.pallas_call(body, name=name, out_shape=shapes * 3)(*gs, *ws, *ms, *vs)
    return [list(res[kind * k:(kind + 1) * k]) for kind in range(3)]


def _sum_parts(name, parts):
    _, r, cols = parts[0][0].shape
    tm = _row_tile(r)
    idx = jnp.stack([jnp.asarray(s, jnp.int32) for _, s in parts])
    n = len(parts)

    def body(idx_ref, *refs):
        g = refs[0][0].astype(F32)
        for p_ref in refs[1:n]:
            g = g + p_ref[0].astype(F32)
        refs[n][...] = g

    return pl.pallas_call(
        body, name=name,
        grid_spec=pltpu.PrefetchScalarGridSpec(
            num_scalar_prefetch=1, grid=(r // tm,),
            in_specs=[pl.BlockSpec((1, tm, cols), lambda i, idx_ref, p=p: (idx_ref[p], i, 0)) for p in range(n)],
            out_specs=pl.BlockSpec((tm, cols), lambda i, idx_ref: (i, 0))),
        out_shape=jax.ShapeDtypeStruct((r, cols), F32),
        compiler_params=pltpu.CompilerParams(dimension_semantics=("parallel",)),
    )(idx, *[a for a, _ in parts])


def _adam(name, parts, w, m, v):
    r, cols = w.shape
    tm = _row_tile(r)
    tc = cols // 4 if (r // tm < 4 and cols % (4 * LANE) == 0) else cols
    idx = jnp.stack([jnp.asarray(s, jnp.int32) for _, s in parts])
    n = len(parts)

    def body(idx_ref, *refs):
        g = refs[0][0].astype(F32)
        for p_ref in refs[1:n]:
            g = g + p_ref[0].astype(F32)
        w_ref, m_ref, v_ref, g_out, d_out, m_out, v_out = refs[n:]
        g_out[...] = g
        d_out[...], m_out[...], v_out[...] = _adam_math(g, w_ref[...], m_ref[...], v_ref[...])

    part_specs = [pl.BlockSpec((1, tm, tc), lambda i, j, idx_ref, p=p: (idx_ref[p], i, j)) for p in range(n)]
    flat = pl.BlockSpec((tm, tc), lambda i, j, idx_ref: (i, j))
    return pl.pallas_call(
        body, name=name,
        grid_spec=pltpu.PrefetchScalarGridSpec(
            num_scalar_prefetch=1, grid=(r // tm, cols // tc), in_specs=part_specs + [flat] * 3,
            out_specs=[flat] * 4),
        out_shape=[jax.ShapeDtypeStruct((r, cols), F32)] * 4,
        compiler_params=pltpu.CompilerParams(dimension_semantics=("parallel", "parallel")),
    )(idx, *[a for a, _ in parts], w, m, v)


def _all_gather_many(name, blks):
    n = len(blks)

    def body(*refs):
        x_refs, out_refs = refs[:n], refs[n:2 * n]
        send_sems, recv_sems, local_sems = refs[2 * n:]
        x, y, c = _me()
        me, sib = (x, y, c), (x, y, 1 - c)
        chips = [(x ^ fx, y ^ fy) for fx, fy in _CHIP_FLIPS]

        def slot(a, p):
            return out_refs[a].at[4 * p[0] + 2 * p[1] + p[2]]

        def copy(a, k, block, to, src=None):
            return pltpu.make_async_remote_copy(
                src_ref=slot(a, block) if src is None else src, dst_ref=slot(a, block),
                send_sem=send_sems.at[7 * a + k], recv_sem=recv_sems.at[7 * a + k], device_id=to,
                device_id_type=_MESH)

        mine = [pltpu.make_async_copy(x_refs[a], slot(a, me), local_sems.at[a]) for a in range(n)]
        first = []
        for a in range(n):
            mine[a].start()
            first.append(copy(a, 0, me, sib, src=x_refs[a]))
            first += [copy(a, 1 + j, me, (*chip, c), src=x_refs[a]) for j, chip in enumerate(chips)]
        for cp in first:
            cp.start()
        passed = []
        for j, chip in enumerate(chips):
            for a in range(n):
                copy(a, 1 + j, (*chip, c), me).wait_recv()
                cp = copy(a, 4 + j, (*chip, c), sib)
                cp.start()
                passed.append(cp)
        for a in range(n):
            copy(a, 0, sib, me).wait_recv()
            for j, chip in enumerate(chips):
                copy(a, 4 + j, (*chip, 1 - c), me).wait_recv()
        for cp in first + passed:
            cp.wait_send()
        for cp in mine:
            cp.wait()

    return pl.pallas_call(
        body, name=name, in_specs=[_ANY] * n, out_specs=[_ANY] * n,
        out_shape=[jax.ShapeDtypeStruct((N_DEV,) + b.shape, b.dtype) for b in blks],
        scratch_shapes=[pltpu.SemaphoreType.DMA((7 * n,)), pltpu.SemaphoreType.DMA((7 * n,)),
                        pltpu.SemaphoreType.DMA((n,))],
    )(*blks)


_HBM = pl.BlockSpec(memory_space=pltpu.HBM)
_SEM = pl.BlockSpec(memory_space=pltpu.SEMAPHORE)
_EFFECT = pltpu.SideEffectType.DATAFLOW_SIDE_EFFECTING


def _push_copies(src_refs, land_refs, send_sems, recv_sems, src_by_peer, first=0):
    x, y, c = _me()
    my_id = 4 * x + 2 * y + c
    out = []
    for k in range(len(src_refs)):
        a = first + k
        for f in range(1, N_DEV):
            px, py, pc = x ^ (f >> 2), y ^ ((f >> 1) & 1), c ^ (f & 1)
            pid = 4 * px + 2 * py + pc
            src = src_refs[k].at[pid] if src_by_peer else src_refs[k]
            start = pltpu.make_async_remote_copy(
                src_ref=src, dst_ref=land_refs[k].at[my_id], send_sem=send_sems.at[7 * a + f - 1],
                recv_sem=recv_sems.at[7 * a + f - 1], device_id=(px, py, pc), device_id_type=_MESH)
            landed = pltpu.make_async_remote_copy(
                src_ref=src, dst_ref=land_refs[k].at[pid], send_sem=send_sems.at[7 * a + f - 1],
                recv_sem=recv_sems.at[7 * a + f - 1], device_id=(px, py, pc), device_id_type=_MESH)
            out.append((start, landed))
    return out


def _push_start(name, srcs, src_by_peer, after):
    n = len(srcs)
    lands = [jax.ShapeDtypeStruct((N_DEV,) + (s.shape[1:] if src_by_peer else s.shape), s.dtype) for s in srcs]

    def body(*refs):
        src_refs, land_refs = refs[:n], refs[n:2 * n]
        send_sems, recv_sems = refs[2 * n + 1], refs[2 * n + 2]
        token = refs[-1]
        for start, _ in _push_copies(src_refs, land_refs, send_sems, recv_sems, src_by_peer):
            start.start()
        token[...] = jnp.zeros_like(token)

    hbm = lambda a: pltpu.with_memory_space_constraint(a, pltpu.HBM)
    res = pl.pallas_call(
        body, name=name,
        out_shape=(pltpu.SemaphoreType.DMA((7 * n,)), pltpu.SemaphoreType.DMA((7 * n,)),
                   *[pltpu.HBM(s.shape, s.dtype) for s in srcs], *[pltpu.HBM(s.shape, s.dtype) for s in lands],
                   jax.ShapeDtypeStruct((8, LANE), F32)),
        in_specs=[_HBM] * (2 * n) + [_ANY],
        out_specs=(_SEM, _SEM, *[_HBM] * (2 * n), pl.BlockSpec(memory_space=pltpu.VMEM)),
        input_output_aliases={i: 2 + i for i in range(2 * n)},
        compiler_params=pltpu.CompilerParams(has_side_effects=_EFFECT),
    )(*[hbm(s) for s in srcs], *[hbm(lax.empty(s.shape, s.dtype)) for s in lands], after)
    return res[0], res[1], list(res[2:2 + n]), list(res[2 + n:2 + 2 * n]), res[-1]


def _push_wait(name, send_sems, recv_sems, srcs, lands, src_by_peer, after, first=0):
    n = len(srcs)
    after = list(after) if isinstance(after, (list, tuple)) else [after]

    def body(*refs):
        src_refs, land_refs = refs[:n], refs[n:2 * n]
        s_sems, r_sems = refs[2 * n], refs[2 * n + 1]
        for _, landed in _push_copies(src_refs, land_refs, s_sems, r_sems, src_by_peer, first):
            landed.wait_send()
            landed.wait_recv()

    res = pl.pallas_call(
        body, name=name,
        out_shape=tuple(pltpu.HBM(s.shape, s.dtype) for s in list(srcs) + list(lands)),
        in_specs=[_HBM] * (2 * n) + [_SEM, _SEM] + [_ANY] * len(after),
        out_specs=tuple([_HBM] * (2 * n)),
        input_output_aliases={i: i for i in range(2 * n)},
        compiler_params=pltpu.CompilerParams(has_side_effects=_EFFECT),
    )(*srcs, *lands, send_sems, recv_sems, *after)
    return list(res[:n]), list(res[n:])


_SHARDED = (
    ("meta_tokens", 1, (N_META, D_MODEL)),
    ("w_in", 1, (D_MODEL, IN_COLS)),
    ("w_q_b", 1, (Q_LORA, MLA_HEADS * QK_HEAD)),
    ("w_kv_b", 1, (KV_LORA, MLA_HEADS * (QK_NOPE + V_HEAD))),
    ("dn_conv_w", 1, (DN_CONV, 3 * DN_WIDTH)),
    ("w_out", 0, (2 * DN_WIDTH, D_MODEL)),
    ("w_gate", 1, (D_MODEL, D_FF)),
    ("w_up", 1, (D_MODEL, D_FF)),
    ("ffn_conv_w", 1, (FFN_CONV, D_FF)),
    ("w_down", 0, (D_FF, D_MODEL)),
)
_F32_GATHERED = ("meta_tokens", "dn_conv_w", "ffn_conv_w")
_EARLY = ("w_in", "w_q_b", "w_kv_b")
_LATE = ("w_out", "w_gate", "w_up", "w_down")
_TRANSPOSED = ("w_in", "w_q_b", "w_gate", "w_up")
_REPLICATED = (
    ("attn_norm_w", D_MODEL), ("q_a_norm_w", Q_LORA), ("kv_a_norm_w", KV_LORA), ("q_norm_w", QK_HEAD),
    ("k_norm_w", QK_HEAD), ("mla_out_norm_w", V_HEAD), ("dn_A_log", DN_HEADS), ("dn_dt_bias", DN_HEADS),
    ("dn_out_norm_w", DN_DIM), ("ffn_norm_w", D_MODEL), ("ffn_conv_b", D_FF),
)
_SMALL_BLOCK = (8, 512)


def _local_shape(dim, shape):
    return (shape[0] // N_DEV, shape[1]) if dim == 0 else (shape[0], shape[1] // N_DEV)


def _from_blocks(blocks, dim, shape):
    r, c = shape
    if dim == 0:
        return blocks.reshape(r, c)
    return blocks.reshape(N_DEV, r, c // N_DEV).transpose(1, 0, 2).reshape(r, c)


def _split(flat, sizes):
    out, o = [], 0
    for s in sizes:
        out.append(flat[..., o:o + s])
        o += s
    return out


def kernel(x, meta_tokens, attn_norm_w, w_in, q_a_norm_w, w_q_b, kv_a_norm_w, w_kv_b, q_norm_w, k_norm_w, mla_out_norm_w, dn_conv_w, dn_A_log, dn_dt_bias, dn_out_norm_w, w_out, ffn_norm_w, w_gate, w_up, ffn_conv_w, ffn_conv_b, w_down, loss_target, m_meta_tokens, m_attn_norm_w, m_w_in, m_q_a_norm_w, m_w_q_b, m_kv_a_norm_w, m_w_kv_b, m_q_norm_w, m_k_norm_w, m_mla_out_norm_w, m_dn_conv_w, m_dn_A_log, m_dn_dt_bias, m_dn_out_norm_w, m_w_out, m_ffn_norm_w, m_w_gate, m_w_up, m_ffn_conv_w, m_ffn_conv_b, m_w_down, v_meta_tokens, v_attn_norm_w, v_w_in, v_q_a_norm_w, v_w_q_b, v_kv_a_norm_w, v_w_kv_b, v_q_norm_w, v_k_norm_w, v_mla_out_norm_w, v_dn_conv_w, v_dn_A_log, v_dn_dt_bias, v_dn_out_norm_w, v_w_out, v_ffn_norm_w, v_w_gate, v_w_up, v_ffn_conv_w, v_ffn_conv_b, v_w_down):
    names = [n for n, _, _ in _SHARDED] + [n for n, _ in _REPLICATED]
    given = dict(locals())
    two_d = lambda a: a.reshape(a.shape[-2:])
    view = lambda a, n: two_d(a).T if n in _TRANSPOSED else two_d(a)
    wl = {n: view(given[n], n) for n in names}
    ml = {n: view(given["m_" + n], n) for n in names}
    vl = {n: view(given["v_" + n], n) for n in names}
    out_shapes = {n: given[n].shape for n in names}

    spec = {n: (d, s) for n, d, s in _SHARDED}
    small_sizes = [math.prod(_local_shape(*spec[n])) for n in _F32_GATHERED]

    def small_block(d):
        cat = jnp.concatenate([d[n].reshape(d[n].shape[:-2] + (-1,)) for n in _F32_GATHERED], axis=-1)
        pad = [(0, 0)] * (cat.ndim - 1) + [(0, math.prod(_SMALL_BLOCK) - cat.shape[-1])]
        return jnp.pad(cat, pad).reshape(cat.shape[:-1] + _SMALL_BLOCK)

    def shard(n):
        return wl[n].astype(_MXU)

    def from_slots(n, blocks):
        d, s = spec[n]
        if d == 0 or n in _TRANSPOSED:
            return blocks.reshape(-1, blocks.shape[-1])
        return blocks.transpose(1, 0, 2).reshape(s)

    my_id = 4 * lax.axis_index("x") + 2 * lax.axis_index("y") + lax.axis_index("c")
    got = _all_gather_many("gather_early", [shard(n) for n in _EARLY] + [small_block(wl)])
    full = {n: a for n, a in wl.items() if n not in _LATE}
    for n, blocks in zip(_EARLY, got):
        full[n] = from_slots(n, blocks)
    for n, p in zip(_F32_GATHERED, _split(got[-1].reshape(N_DEV, -1), small_sizes)):
        full[n] = _from_blocks(p, *spec[n])
    late_own = [shard(n) for n in _LATE]
    l_send, l_recv, l_src, l_land, token = _push_start("gather_late_start", late_own, False, got[-1])

    def late_weights(after, names):
        first = _LATE.index(names[0])
        sl = slice(first, first + len(names))
        _, lands = _push_wait("gather_late_wait_" + names[0], l_send, l_recv, l_src[sl], l_land[sl], False,
                              after, first)
        out = {}
        for n, land, own in zip(names, lands, late_own[sl]):
            out[n] = from_slots(n, lax.dynamic_update_slice(land, own[None], (my_id, 0, 0))).astype(_MXU)
        return out

    def dest_blocks(n, a):
        d, s = spec[n]
        r, c = _local_shape(d, s)
        if n in _TRANSPOSED:
            return a.reshape(N_DEV, c, r)
        return a.reshape(N_DEV, r, c) if d == 0 else a.reshape(r, N_DEV, c).transpose(1, 0, 2)

    pushed = []

    def grads_ready(g, names):
        nat = _grads_to_natural({n: g[n] for n in names})
        blocks = [dest_blocks(n, nat[n]).astype(_MXU) for n in names]
        sends, recvs, srcs, lands, tok = _push_start("rs_" + names[0] + "_start", blocks, True, token)
        pushed.append((names, sends, recvs, srcs, lands))
        return tok

    seq = x.shape[1]
    tp = ROW0 + seq
    h0 = jnp.concatenate([jnp.zeros((PAD, D_MODEL), F32), full["meta_tokens"], x[0]], axis=0)
    tgt = jnp.concatenate([jnp.zeros((ROW0, D_MODEL), F32), loss_target[0]], axis=0)
    loss, dh0, raw = _local_step(h0, tgt, _prepare(full, tp), token, late_weights, grads_ready)
    g = _grads_to_natural(raw)
    g["meta_tokens"] = dh0[PAD:ROW0]
    grad_x = dh0[ROW0:][None]

    big = [{}, {}, {}, {}]

    def finish(group):
        names, sends, recvs, srcs, lands = group
        srcs, lands = _push_wait("rs_" + names[0] + "_wait", sends, recvs, srcs, lands, True, dh0)
        for n, src, land in zip(names, srcs, lands):
            parts = [(src, my_id)] + [(land, my_id ^ f) for f in range(1, N_DEV)]
            for kind, a in enumerate(_adam("adam_" + n, parts, wl[n], ml[n], vl[n])):
                big[kind][n] = a

    for group in pushed[:-1]:
        finish(group)
    rep_names = [n for n, _ in _REPLICATED]
    raw_key = {"dn_A_log": "alog_b", "dn_dt_bias": "dtb_b"}
    pieces = [raw[raw_key.get(n, n)] for n in rep_names] + [loss]
    pieces += [g[n].reshape(1, -1) for n in _F32_GATHERED]
    widths = [p.shape[1] for p in pieces]
    offs = [sum(widths[:k]) for k in range(len(widths))]
    cat = jnp.concatenate(pieces, axis=1)
    cols = -(-cat.shape[1] // (8 * LANE)) * LANE
    mine = jnp.pad(cat, ((0, 0), (0, 8 * cols - cat.shape[1]))).reshape(8, cols)
    everyone = _all_gather("gather_small_grads", mine, [big[1][n] for group in pushed[:-1] for n in group[0]])
    total = _sum_parts("sum_small_grads", [(everyone, d) for d in range(N_DEV)]).reshape(1, 8 * cols)
    tot = {n: total[0, o:o + wd] for n, o, wd in zip(rep_names + ["loss"] + list(_F32_GATHERED), offs, widths)}
    items = [(o, size, n in raw_key) for (n, size), o in zip(_REPLICATED, offs)]
    sm = _adam_vectors("adam_replicated", total, items, [wl[n] for n in rep_names], [ml[n] for n in rep_names],
                       [vl[n] for n in rep_names])
    sm = [dict(zip(rep_names, kind)) for kind in sm]
    mine_of = {}
    for n in _F32_GATHERED:
        d, s = spec[n]
        r, c = _local_shape(d, s)
        mine_of[n] = lax.dynamic_slice(tot[n].reshape(s), (0, my_id * c), (r, c))
    res = _adam_arrays("adam_small_sharded", [mine_of[n] for n in _F32_GATHERED], [wl[n] for n in _F32_GATHERED],
                       [ml[n] for n in _F32_GATHERED], [vl[n] for n in _F32_GATHERED])
    for kind, arrays in enumerate([[mine_of[n] for n in _F32_GATHERED]] + res):
        big[kind].update(zip(_F32_GATHERED, arrays))

    finish(pushed[-1])

    outs = [tot["loss"][0], grad_x]
    for kind in range(4):
        for n in ("meta_tokens", "attn_norm_w", "w_in", "q_a_norm_w", "w_q_b", "kv_a_norm_w", "w_kv_b", "q_norm_w",
                  "k_norm_w", "mla_out_norm_w", "dn_conv_w", "dn_A_log", "dn_dt_bias", "dn_out_norm_w", "w_out",
                  "ffn_norm_w", "w_gate", "w_up", "ffn_conv_w", "ffn_conv_b", "w_down"):
            src = big[kind] if n in big[kind] else sm[kind]
            a = src[n].T if n in _TRANSPOSED else src[n]
            outs.append(a.reshape(out_shapes[n]))
    return tuple(outs)
```

```python
import functools
import math

import jax
import jax.numpy as jnp
from jax import lax
from jax.experimental import pallas as pl
from jax.experimental.pallas import tpu as pltpu

F32 = jnp.float32
_MXU = jnp.bfloat16
_HI = lax.Precision.HIGHEST

D_MODEL = 1024
N_META = 16
PAD = 112
ROW0 = PAD + N_META
MLA_HEADS = 4
QK_NOPE = 128
QK_ROPE = 64
QK_HEAD = QK_NOPE + QK_ROPE
V_HEAD = 128
Q_LORA = 256
KV_LORA = 256
ROPE_THETA = 10000.0
DN_HEADS = 4
DN_DIM = 128
DN_WIDTH = DN_HEADS * DN_DIM
DN_CONV = 4
DN_CHUNK = 64
GDN_SUB_CHUNKS = 2
D_FF = 2816
FFN_CONV = 3
EPS = 1e-6
HP = 256
C_Z = 1536
C_QL = 2048
C_KVL = 2304
C_KPE = 2560
C_AB = 2688
IN_COLS = 2632

ADAM_LR = 0.001
ADAM_B1 = 0.9
ADAM_B2 = 0.999
ADAM_EPS = 1e-08
ADAM_WD = 0.01
ADAM_STEP = 10

N_DEV = 8
TM = 128
LANE = 128
VMEM_LIMIT = 56 * 1024 * 1024
NEG = -1e30


def _dot(a, b, dims, hp=False):
    if hp:
        return lax.dot_general(a.astype(F32), b.astype(F32), (dims, ((), ())),
                               precision=lax.Precision.HIGH if hp == "3x" else _HI, preferred_element_type=F32)
    return lax.dot_general(a.astype(_MXU), b.astype(_MXU), (dims, ((), ())),
                           preferred_element_type=F32)


def _nn(a, b, hp=False):
    return _dot(a, b, ((1,), (0,)), hp)


def _nt(a, b, hp=False):
    return _dot(a, b, ((1,), (1,)), hp)


def _tn(a, b, hp=False):
    return _dot(a, b, ((0,), (0,)), hp)


def _sigmoid(x):
    return 1.0 / (1.0 + jnp.exp(-x))


def _rms_fwd(x, w, n):
    r = lax.rsqrt(jnp.sum(x * x, axis=-1, keepdims=True) * (1.0 / n) + EPS)
    return x * r * w, r


def _rms_bwd(x, w, dy, n):
    r = lax.rsqrt(jnp.sum(x * x, axis=-1, keepdims=True) * (1.0 / n) + EPS)
    xh = x * r
    gy = dy * w
    dx = r * (gy - xh * (jnp.sum(gy * xh, axis=-1, keepdims=True) * (1.0 / n)))
    return dx, dy * xh


def _rowsum(x):
    return jnp.sum(x, axis=0, keepdims=True)


def _row_ids(i, tm):
    return i * tm + lax.broadcasted_iota(jnp.int32, (tm, 1), 0)


def _shift_down(ext, s, tm):
    if s == 0:
        return ext[8:8 + tm]
    return pltpu.roll(ext, s, 0)[8:8 + tm]


def _shift_up(ext, s, tm):
    if s == 0:
        return ext[0:tm]
    return pltpu.roll(ext, tm + 8 - s, 0)[0:tm]


def _conv_taps(x, halo_prev, width):
    tm = x.shape[0]
    ext = jnp.concatenate([halo_prev, x], axis=0)
    return [_shift_down(ext, width - 1 - j, tm) for j in range(width)]


def _conv_from_taps(taps, w):
    y = None
    for j, tap in enumerate(taps):
        t = w[j:j + 1, :] * tap
        y = t if y is None else y + t
    return y


def _conv_fwd(x, halo_prev, w, width):
    return _conv_from_taps(_conv_taps(x, halo_prev, width), w)


def _conv_bwd_w_taps(dy, taps):
    tm = dy.shape[0]
    rows = [_rowsum(dy * tap[:tm]) for tap in taps]
    rows += [jnp.zeros_like(rows[0])] * (8 - len(taps))
    return jnp.concatenate(rows, axis=0)


def _conv_bwd_x(dy, halo_next, w, width):
    tm = dy.shape[0]
    ext = jnp.concatenate([dy, halo_next], axis=0)
    dx = None
    for j in range(width):
        t = w[j:j + 1, :] * _shift_up(ext, width - 1 - j, tm)
        dx = t if dx is None else dx + t
    return dx


def _softplus(x):
    e = jnp.exp(-jnp.abs(x))
    u = 1.0 + e
    l1p = jnp.where(u == 1.0, e, jnp.log(u) * e / jnp.where(u == 1.0, 1.0, u - 1.0))
    return jnp.maximum(x, 0.0) + l1p


def _swap_halves(x):
    lane = lax.broadcasted_iota(jnp.int32, x.shape, 1)
    return jnp.where(lane < 32, pltpu.roll(x, 96, 1), jnp.where(lane < 64, pltpu.roll(x, 32, 1), 0.0))


class _In:
    def __init__(self, arr, width=None, cb=0, kind="cur"):
        self.arr, self.kind = arr, kind
        self.width = arr.shape[1] if width is None else width
        self.cb = cb


def _whole_spec(x):
    return pl.BlockSpec(x.shape, lambda i, nd=x.ndim: (0,) * nd, pipeline_mode=pl.Buffered(1))


def _tile_spec(t, tm, tp):
    r8 = tm // 8
    if t.kind == "cur":
        return pl.BlockSpec((tm, t.width), lambda i, cb=t.cb: (i, cb))
    if t.kind == "prev":
        return pl.BlockSpec((8, t.width), lambda i, cb=t.cb: (jnp.maximum(i * r8 - 1, 0), cb))
    return pl.BlockSpec((8, t.width), lambda i, cb=t.cb: (jnp.minimum((i + 1) * r8, tp // 8 - 1), cb))


def _rows(name, fn, tiled, full, outs, accs=(), tm=TM):
    tp = tiled[0].arr.shape[0]
    nt = tp // tm
    n_in = len(tiled) + len(full)
    n_out = len(outs)

    def body(*refs):
        i = pl.program_id(0)
        vals = [r[...] for r in refs[:n_in]]
        o_t, o_a = fn(i, *vals)
        for r, v in zip(refs[n_in:n_in + n_out], o_t):
            r[...] = v.astype(r.dtype)
        for r, v in zip(refs[n_in + n_out:], o_a):
            @pl.when(i == 0)
            def _():
                r[...] = v

            @pl.when(i > 0)
            def _():
                r[...] += v

    in_specs = [_tile_spec(t, tm, tp) for t in tiled]
    in_specs += [pl.BlockSpec(a.shape, lambda i, nd=a.ndim: (0,) * nd) for a in full]
    out_specs = [pl.BlockSpec((tm, w), lambda i: (i, 0)) for w, _ in outs]
    out_specs += [pl.BlockSpec((r, w), lambda i: (0, 0)) for r, w in accs]
    out_shape = [jax.ShapeDtypeStruct((tp, w), dt) for w, dt in outs]
    out_shape += [jax.ShapeDtypeStruct((r, w), F32) for r, w in accs]
    res = pl.pallas_call(
        body, name=name, grid=(nt,), in_specs=in_specs, out_specs=out_specs, out_shape=out_shape,
        compiler_params=pltpu.CompilerParams(dimension_semantics=("arbitrary",), vmem_limit_bytes=VMEM_LIMIT),
    )(*[t.arr for t in tiled], *full)
    return res


def _pick(n, cap, mult):
    best = None
    for d in range(mult, min(n, cap) + 1, mult):
        if n % d == 0:
            best = d
    assert best is not None, (n, cap, mult)
    return best


_ANY_SPEC = pl.BlockSpec(memory_space=pl.ANY)


def _mm(name, a, b, mode, out_dtype=F32, resid=None, after=None):
    if mode == "tn":
        m, k = a.shape
        n = b.shape[1]
        tk = _pick(k, 512, 128)
        tn = _pick(n, 1408, 128)

        def body_tn(a_ref, b_ref, o_ref):
            o_ref[...] = _tn(a_ref[...], b_ref[...]).astype(o_ref.dtype)

        return pl.pallas_call(
            body_tn, name=name, grid=(n // tn, k // tk),
            in_specs=[pl.BlockSpec((m, tk), lambda j, p: (0, p)),
                      pl.BlockSpec((m, tn), lambda j, p: (0, j))],
            out_specs=pl.BlockSpec((tk, tn), lambda j, p: (p, j)),
            out_shape=jax.ShapeDtypeStruct((k, n), out_dtype),
            compiler_params=pltpu.CompilerParams(
                dimension_semantics=("parallel", "parallel"), vmem_limit_bytes=VMEM_LIMIT),
        )(a, b)

    m, k = a.shape
    n = b.shape[1] if mode == "nn" else b.shape[0]
    tn = _pick(n, 1408, 128)
    tm = _pick(m, 1152, 16)
    dotf = _nn if mode == "nn" else _nt

    def body(*refs):
        a_ref, b_ref, o_ref = refs[0], refs[1], refs[-1]
        acc = dotf(a_ref[...], b_ref[...])
        if resid is not None:
            acc = refs[2][...] + acc
        o_ref[...] = acc.astype(o_ref.dtype)

    b_spec = (pl.BlockSpec((k, tn), lambda j, i: (0, j)) if mode == "nn"
              else pl.BlockSpec((tn, k), lambda j, i: (j, 0)))
    in_specs = [pl.BlockSpec((tm, k), lambda j, i: (i, 0)), b_spec]
    args = [a, b]
    if resid is not None:
        in_specs.append(pl.BlockSpec((tm, tn), lambda j, i: (i, j)))
        args.append(resid)
    if after is not None:
        in_specs.append(_ANY_SPEC)
        args.append(after)
    return pl.pallas_call(
        body, name=name, grid=(n // tn, m // tm), in_specs=in_specs,
        out_specs=pl.BlockSpec((tm, tn), lambda j, i: (i, j)),
        out_shape=jax.ShapeDtypeStruct((m, n), out_dtype),
        compiler_params=pltpu.CompilerParams(
            dimension_semantics=("parallel", "parallel"), vmem_limit_bytes=VMEM_LIMIT),
    )(*args)


def _mm_tn2(name, a1, a2, b, out_dtype=F32):
    m, k = a1.shape
    n = b.shape[1]
    tk = _pick(k, 512, 128)

    def body(a1_ref, a2_ref, b_ref, o1_ref, o2_ref):
        bb = b_ref[...]
        o1_ref[...] = _tn(a1_ref[...], bb).astype(o1_ref.dtype)
        o2_ref[...] = _tn(a2_ref[...], bb).astype(o2_ref.dtype)

    a_spec = pl.BlockSpec((m, tk), lambda p: (0, p))
    o_spec = pl.BlockSpec((tk, n), lambda p: (p, 0))
    return pl.pallas_call(
        body, name=name, grid=(k // tk,),
        in_specs=[a_spec, a_spec, pl.BlockSpec((m, n), lambda p: (0, 0))],
        out_specs=[o_spec, o_spec], out_shape=[jax.ShapeDtypeStruct((k, n), out_dtype)] * 2,
        compiler_params=pltpu.CompilerParams(dimension_semantics=("parallel",), vmem_limit_bytes=VMEM_LIMIT),
    )(a1, a2, b)


def _mm_tn_pair(name, a1, b1, a2, b2):
    def body(a1_ref, b1_ref, a2_ref, b2_ref, o1_ref, o2_ref):
        o1_ref[...] = _tn(a1_ref[...], b1_ref[...])
        o2_ref[...] = _tn(a2_ref[...], b2_ref[...])

    return pl.pallas_call(
        body, name=name,
        out_shape=[jax.ShapeDtypeStruct((a1.shape[1], b1.shape[1]), F32),
                   jax.ShapeDtypeStruct((a2.shape[1], b2.shape[1]), F32)],
        compiler_params=pltpu.CompilerParams(vmem_limit_bytes=VMEM_LIMIT),
    )(a1, b1, a2, b2)


def _norm_mm(name, x, norm_w, b, mode="nt", x_cb=0, after=None):
    m = x.shape[0]
    k = norm_w.shape[1]
    n = b.shape[0] if mode == "nt" else b.shape[1]
    tn = _pick(n, 1408, 128)
    tm = _pick(m, 1152, 16)
    dotf = _nt if mode == "nt" else _nn
    extra = [] if after is None else [after]

    def body(x_ref, w_ref, b_ref, *rest):
        o_ref, u_ref = rest[-2:]

        @pl.when(pl.program_id(1) == 0)
        def _():
            u_ref[...] = _rms_fwd(x_ref[...], w_ref[...], k)[0].astype(u_ref.dtype)

        o_ref[...] = dotf(u_ref[...], b_ref[...])

    b_spec = (pl.BlockSpec((tn, k), lambda i, j: (j, 0)) if mode == "nt"
              else pl.BlockSpec((k, tn), lambda i, j: (0, j)))
    return pl.pallas_call(
        body, name=name, grid=(m // tm, n // tn),
        in_specs=[pl.BlockSpec((tm, k), lambda i, j: (i, x_cb)), pl.BlockSpec((1, k), lambda i, j: (0, 0)),
                  b_spec] + [_ANY_SPEC] * len(extra),
        out_specs=[pl.BlockSpec((tm, tn), lambda i, j: (i, j)), pl.BlockSpec((tm, k), lambda i, j: (i, 0))],
        out_shape=[jax.ShapeDtypeStruct((m, n), F32), jax.ShapeDtypeStruct((m, k), _MXU)],
        compiler_params=pltpu.CompilerParams(
            dimension_semantics=("arbitrary", "arbitrary"), vmem_limit_bytes=VMEM_LIMIT),
    )(x, norm_w, b, *extra)


def _pro_mm(name, fn, tiled, full, k, b, resid):
    m = resid.shape[0]
    n = b.shape[1]
    tm = _pick(m, 576, 16)
    n_in = len(tiled) + len(full)

    def body(*refs):
        i = pl.program_id(0)
        u = fn(i, *[r[...] for r in refs[:n_in]]).astype(_MXU)
        b_ref, r_ref, o_ref, u_ref = refs[n_in:]
        u_ref[...] = u
        o_ref[...] = r_ref[...] + _nn(u, b_ref[...])

    row = lambda w: pl.BlockSpec((tm, w), lambda i: (i, 0))
    in_specs = [_tile_spec(t, tm, m) for t in tiled]
    in_specs += [_whole_spec(x) for x in full] + [_whole_spec(b), row(n)]
    return pl.pallas_call(
        body, name=name, grid=(m // tm,), in_specs=in_specs, out_specs=[row(n), row(k)],
        out_shape=[jax.ShapeDtypeStruct((m, n), F32), jax.ShapeDtypeStruct((m, k), _MXU)],
        compiler_params=pltpu.CompilerParams(dimension_semantics=("parallel",), vmem_limit_bytes=VMEM_LIMIT),
    )(*[t.arr for t in tiled], *full, b, resid)


def _ffn_in(h2, norm_w, w_gate_t, w_up_t, conv_w8, conv_b):
    m, k = h2.shape
    n = w_gate_t.shape[0]
    tm = _pick(m, 576, 16)
    tn = _pick(n, 1408, 128)

    def body(x_ref, xp_ref, nw_ref, wg_ref, wu_ref, cw_ref, cb_ref, hn_ref, gp_ref, up_ref, act_ref):
        i = pl.program_id(0)
        nw = nw_ref[...]

        @pl.when(pl.program_id(1) == 0)
        def _():
            hn_ref[...] = _rms_fwd(x_ref[...], nw, k)[0].astype(hn_ref.dtype)

        hn = hn_ref[...]
        hn_prev = _rms_fwd(xp_ref[...], nw, k)[0].astype(_MXU)
        wg = wg_ref[...]
        gp = _nt(hn, wg)
        gp_prev = jnp.where(i > 0, _nt(hn_prev, wg), 0.0)
        up = _nt(hn, wu_ref[...])
        gate = _conv_fwd(gp, gp_prev, cw_ref[...], FFN_CONV) + cb_ref[...]
        gp_ref[...] = gp
        up_ref[...] = up
        act_ref[...] = (_silu_parts(gate)[0] * up).astype(act_ref.dtype)

    r8 = tm // 8
    tile = pl.BlockSpec((tm, tn), lambda i, j: (i, j))
    wblk = pl.BlockSpec((tn, k), lambda i, j: (j, 0))
    return pl.pallas_call(
        body, name="ffn_in", grid=(m // tm, n // tn),
        in_specs=[pl.BlockSpec((tm, k), lambda i, j: (i, 0)),
                  pl.BlockSpec((8, k), lambda i, j: (jnp.maximum(i * r8 - 1, 0), 0)),
                  pl.BlockSpec((1, k), lambda i, j: (0, 0)), wblk, wblk,
                  pl.BlockSpec((8, tn), lambda i, j: (0, j)), pl.BlockSpec((1, tn), lambda i, j: (0, j))],
        out_specs=[pl.BlockSpec((tm, k), lambda i, j: (i, 0)), tile, tile, tile],
        out_shape=[jax.ShapeDtypeStruct((m, k), _MXU), jax.ShapeDtypeStruct((m, n), F32),
                   jax.ShapeDtypeStruct((m, n), F32), jax.ShapeDtypeStruct((m, n), _MXU)],
        compiler_params=pltpu.CompilerParams(
            dimension_semantics=("arbitrary", "arbitrary"), vmem_limit_bytes=VMEM_LIMIT),
    )(h2, h2, norm_w, w_gate_t, w_up_t, conv_w8, conv_b)


def _mm_rows(name, a, b, mode, fn, tiled, full, outs, accs=(), tm_cap=576):
    a_list = list(a) if isinstance(a, (list, tuple)) else [a]
    b_list = list(b) if isinstance(b, (list, tuple)) else [b]
    na = len(a_list)
    m = a_list[0].shape[0]
    tm = _pick(m, tm_cap, 16)
    dotf = _nn if mode == "nn" else _nt
    n_in = len(tiled) + len(full)
    n_out = len(outs)
    first = 2 * na

    def body(*refs):
        i = pl.program_id(0)
        vals = [r[...] for r in refs[first:first + n_in]]
        acc = dotf(refs[0][...], refs[na][...])
        for p in range(1, na):
            acc = acc + dotf(refs[p][...], refs[na + p][...])
        o_t, o_a = fn(i, acc, *vals)
        for r, v in zip(refs[first + n_in:first + n_in + n_out], o_t):
            r[...] = v.astype(r.dtype)
        for r, v in zip(refs[first + n_in + n_out:], o_a):
            @pl.when(i == 0)
            def _():
                r[...] = v

            @pl.when(i > 0)
            def _():
                r[...] += v

    whole = lambda x: pl.BlockSpec(x.shape, lambda i, nd=x.ndim: (0,) * nd)
    in_specs = [pl.BlockSpec((tm, x.shape[1]), lambda i: (i, 0)) for x in a_list] + [_whole_spec(x) for x in b_list]
    in_specs += [_tile_spec(t, tm, m) for t in tiled]
    in_specs += [whole(x) for x in full]
    out_specs = [pl.BlockSpec((tm, w), lambda i: (i, 0)) for w, _ in outs]
    out_specs += [pl.BlockSpec((r, w), lambda i: (0, 0)) for r, w in accs]
    out_shape = [jax.ShapeDtypeStruct((m, w), dt) for w, dt in outs]
    out_shape += [jax.ShapeDtypeStruct((r, w), F32) for r, w in accs]
    return pl.pallas_call(
        body, name=name, grid=(m // tm,), in_specs=in_specs, out_specs=out_specs, out_shape=out_shape,
        compiler_params=pltpu.CompilerParams(dimension_semantics=("arbitrary",), vmem_limit_bytes=VMEM_LIMIT),
    )(*a_list, *b_list, *[t.arr for t in tiled], *full)


ATTN_Q_TILES = 4


def _attn_probs(q, k, row0):
    tq, tp = q.shape[0], k.shape[0]
    s = _nt(q, k) * (1.0 / math.sqrt(QK_HEAD))
    row = row0 + lax.broadcasted_iota(jnp.int32, (tq, tp), 0)
    col = lax.broadcasted_iota(jnp.int32, (tq, tp), 1)
    ok = (col <= row) & (col >= PAD)
    s = jnp.where(ok, s, NEG)
    m = jnp.max(s, axis=-1, keepdims=True)
    e = jnp.exp(s - m)
    return e * (1.0 / jnp.sum(e, axis=-1, keepdims=True))


def _attn_fwd(q, k, v):
    tp = q.shape[0]
    tq = tp // ATTN_Q_TILES

    def body(q_ref, k_ref, v_ref, o_ref):
        for i in range(ATTN_Q_TILES):
            rows = slice(i * tq, (i + 1) * tq)
            keys = slice(0, (i + 1) * tq)
            p = _attn_probs(q_ref[rows, :], k_ref[keys, :], i * tq)
            o_ref[rows, :] = _nn(p, v_ref[keys, :])

    return pl.pallas_call(
        body, name="attn_fwd", grid=(MLA_HEADS,),
        in_specs=[pl.BlockSpec((tp, HP), lambda h: (0, h)),
                  pl.BlockSpec((tp, HP), lambda h: (0, h)),
                  pl.BlockSpec((tp, V_HEAD), lambda h: (0, h))],
        out_specs=pl.BlockSpec((tp, V_HEAD), lambda h: (0, h)),
        out_shape=jax.ShapeDtypeStruct((tp, MLA_HEADS * V_HEAD), F32),
        compiler_params=pltpu.CompilerParams(dimension_semantics=("parallel",), vmem_limit_bytes=VMEM_LIMIT),
    )(q, k, v)


def _attn_bwd(q, k, v, do):
    tp = q.shape[0]
    tq = tp // ATTN_Q_TILES

    def body(q_ref, k_ref, v_ref, do_ref, dq_ref, dk_ref, dv_ref):
        for i in reversed(range(ATTN_Q_TILES)):
            rows = slice(i * tq, (i + 1) * tq)
            keys = slice(0, (i + 1) * tq)
            qb = q_ref[rows, :]
            kk = k_ref[keys, :]
            dob = do_ref[rows, :]
            p = _attn_probs(qb, kk, i * tq)
            dp = _nt(dob, v_ref[keys, :])
            delta = jnp.sum(p * dp, axis=-1, keepdims=True)
            ds = p * (dp - delta) * (1.0 / math.sqrt(QK_HEAD))
            dq_ref[rows, :] = _nn(ds, kk)
            if i == ATTN_Q_TILES - 1:
                dk_ref[...] = _tn(ds, qb)
                dv_ref[...] = _tn(p, dob)
            else:
                dk_ref[keys, :] += _tn(ds, qb)
                dv_ref[keys, :] += _tn(p, dob)

    full = lambda w: pl.BlockSpec((tp, w), lambda h: (0, h))
    return pl.pallas_call(
        body, name="attn_bwd", grid=(MLA_HEADS,),
        in_specs=[full(HP), full(HP), full(V_HEAD), full(V_HEAD)],
        out_specs=[full(HP), full(HP), full(V_HEAD)],
        out_shape=[jax.ShapeDtypeStruct((tp, MLA_HEADS * HP), F32),
                   jax.ShapeDtypeStruct((tp, MLA_HEADS * HP), F32),
                   jax.ShapeDtypeStruct((tp, MLA_HEADS * V_HEAD), F32)],
        compiler_params=pltpu.CompilerParams(dimension_semantics=("parallel",), vmem_limit_bytes=VMEM_LIMIT),
    )(q, k, v, do)


def _gdn_consts():
    c = DN_CHUNK
    r = lax.broadcasted_iota(jnp.int32, (c, c), 0)
    cc = lax.broadcasted_iota(jnp.int32, (c, c), 1)
    incl = r >= cc
    strict = r > cc
    return incl, strict


def _cumsum_rows(x, reverse=False):
    c = x.shape[0]
    row = lax.broadcasted_iota(jnp.int32, x.shape, 0)
    s = 1
    while s < c:
        if reverse:
            x = x + jnp.where(row < c - s, pltpu.roll(x, c - s, 0), 0.0)
        else:
            x = x + jnp.where(row >= s, pltpu.roll(x, s, 0), 0.0)
        s *= 2
    return x


def _each(fn, *lists):
    return [fn(*a) for a in zip(*lists)]


def _interleave(chains):
    chains = list(chains)
    while chains:
        for ch in list(chains):
            try:
                next(ch)
            except StopIteration:
                chains.remove(ch)


def _gdn_chunk_common(q_ref, k_ref, v_ref, g_ref, b_ref):
    c = DN_CHUNK
    incl, strict = _gdn_consts()
    sls = [(slice(c * sub, c * (sub + 1)), slice(DN_DIM * h, DN_DIM * (h + 1)))
           for sub in range(GDN_SUB_CHUNKS) for h in range(DN_HEADS)]
    q = [q_ref[sl] * (1.0 / math.sqrt(DN_DIM)) for sl in sls]
    k = [k_ref[sl] for sl in sls]
    v = [v_ref[sl] for sl in sls]
    g = [g_ref[sl] for sl in sls]
    beta = [b_ref[sl] for sl in sls]
    gc = [_cumsum_rows(x) for x in g]
    grow = [x.T[:c, :] for x in gc]
    kb = _each(jnp.multiply, k, beta)
    kk = _each(_nt, kb, k)
    qk = _each(_nt, q, k)
    gam = [jnp.exp(x) for x in gc]
    g_last = [_rowsum(x) for x in g]
    dm = [jnp.exp(jnp.where(incl, x[:, :c] - y, NEG)) for x, y in zip(gc, grow)]
    vb = _each(jnp.multiply, v, beta)
    kbg = _each(jnp.multiply, kb, gam)
    ek = [jnp.exp(x - y) for x, y in zip(g_last, gc)]
    kd = _each(jnp.multiply, k, ek)
    return dict(q=q, k=k, v=v, beta=beta, gc=gc, gam=gam, g_last=g_last, dm=dm, kb=kb, vb=vb,
                kbg=kbg, kk=kk, ek=ek, kd=kd, qk=qk, incl=incl, strict=strict, sls=sls)


def _gdn_fwd(proj, conv_w8, alog, dtb):
    tp = proj.shape[0]
    c = DN_CHUNK
    nch = tp // c
    blk = GDN_SUB_CHUNKS * c

    def body(x_ref, xp_ref, ab_ref, w8_ref, alog_ref, dtb_ref,
             o_ref, s_ref, t_ref, q_ref, k_ref, v_ref, g_ref, b_ref, s_scr):
        @pl.when(pl.program_id(0) == 0)
        def _():
            s_scr[...] = jnp.zeros_like(s_scr)

        staged, _ = _f_gdn_prep(pl.program_id(0), x_ref[...], xp_ref[...], ab_ref[...], w8_ref[...],
                                alog_ref[...], dtb_ref[...])
        for ref, val in zip((q_ref, k_ref, v_ref, g_ref, b_ref), staged):
            ref[...] = val
        eye = (lax.broadcasted_iota(jnp.int32, (c, c), 0) == lax.broadcasted_iota(jnp.int32, (c, c), 1)).astype(F32)
        x = _gdn_chunk_common(q_ref, k_ref, v_ref, g_ref, b_ref)
        heads = range(DN_HEADS)
        bp = [-jnp.where(x["strict"], kk * dm, 0.0) for kk, dm in zip(x["kk"], x["dm"])]
        t = [eye + b for b in bp]
        for _ in range(5):
            bp = [_nn(b, b, hp="3x") for b in bp]
            t = [tt + _nn(tt, b, hp="3x") for tt, b in zip(t, bp)]
        u = _each(_nn, t, x["vb"])
        w = _each(_nn, t, x["kbg"])
        qg = _each(jnp.multiply, x["q"], x["gam"])
        mqk = _each(jnp.multiply, x["qk"], x["dm"])
        s = [s_scr[h] for h in heads]
        for sub in range(GDN_SUB_CHUNKS):
            e = [DN_HEADS * sub + h for h in heads]
            v_new = [u[i] - _nn(w[i], s[h]) for h, i in zip(heads, e)]
            o = [_nn(qg[i], s[h]) + _nn(mqk[i], v_new[h]) for h, i in zip(heads, e)]
            s_new = [s[h] * jnp.exp(x["g_last"][i]) + _tn(x["kd"][i], v_new[h]) for h, i in zip(heads, e)]
            for h, i in zip(heads, e):
                s_ref[h, sub] = s[h]
                t_ref[h, sub] = t[i]
                o_ref[x["sls"][i]] = o[h]
            s = s_new
        for h in heads:
            s_scr[h] = s[h]

    sub = GDN_SUB_CHUNKS
    rb = lambda n: (n, 0)
    rows = pl.BlockSpec((blk, DN_WIDTH), rb)
    whole = lambda a: pl.BlockSpec(a.shape, lambda n: (0, 0))
    return pl.pallas_call(
        body, name="gdn_fwd", grid=(nch // sub,),
        in_specs=[pl.BlockSpec((blk, 3 * DN_WIDTH), rb),
                  pl.BlockSpec((8, 3 * DN_WIDTH), lambda n: (jnp.maximum(n * (blk // 8) - 1, 0), 0)),
                  pl.BlockSpec((blk, LANE), lambda n: (n, C_AB // LANE)),
                  whole(conv_w8), whole(alog), whole(dtb)],
        out_specs=[rows,
                   pl.BlockSpec((DN_HEADS, sub, DN_DIM, DN_DIM), lambda n: (0, n, 0, 0)),
                   pl.BlockSpec((DN_HEADS, sub, c, c), lambda n: (0, n, 0, 0))] + [rows] * 5,
        out_shape=[jax.ShapeDtypeStruct((tp, DN_WIDTH), F32),
                   jax.ShapeDtypeStruct((DN_HEADS, nch, DN_DIM, DN_DIM), F32),
                   jax.ShapeDtypeStruct((DN_HEADS, nch, c, c), F32)] + [jax.ShapeDtypeStruct((tp, DN_WIDTH), F32)] * 5,
        scratch_shapes=[pltpu.VMEM((DN_HEADS, DN_DIM, DN_DIM), F32)],
        compiler_params=pltpu.CompilerParams(dimension_semantics=("arbitrary",), vmem_limit_bytes=VMEM_LIMIT),
    )(proj, proj, proj, conv_w8, alog, dtb)


def _gdn_bwd(q, k, v, g, beta, s_all, t_all, do, proj, conv_w8, alog, dtb, after):
    tp = q.shape[0]
    c = DN_CHUNK
    nch = tp // c
    nblk = nch // GDN_SUB_CHUNKS
    blk = GDN_SUB_CHUNKS * c

    def body(q_ref, k_ref, v_ref, g_ref, b_ref, s_ref, t_ref, do_ref, x_ref, xp_ref, xn_ref, ab_ref,
             w8_ref, alog_ref, dtb_ref, _after_ref, dqkv_ref, dab_ref, dcw_ref, dalog_ref, ddtb_ref,
             ds_scr, dq_ref, dk_ref, dv_ref, dg_ref, db_ref, nxt_scr):
        step = pl.program_id(0)

        @pl.when(step == 0)
        def _():
            ds_scr[...] = jnp.zeros_like(ds_scr)
            nxt_scr[...] = jnp.zeros_like(nxt_scr)

        xs = _gdn_chunk_common(q_ref, k_ref, v_ref, g_ref, b_ref)

        ds_state = [ds_scr[h] for h in range(DN_HEADS)]

        def chain(sub, h):
            e = DN_HEADS * sub + h
            x = {key: (val[e] if isinstance(val, list) else val) for key, val in xs.items()}
            sl = x["sls"]
            qs, kx, vx, beta_, gam, dm = x["q"], x["k"], x["v"], x["beta"], x["gam"], x["dm"]
            kb, vb, kbg, kd, ek = x["kb"], x["vb"], x["kbg"], x["kd"], x["ek"]
            t = t_ref[h, sub]
            s = s_ref[h, sub]
            dsn = ds_state[h]
            dob = do_ref[sl]
            eg_last = jnp.exp(x["g_last"])
            u = _nn(t, vb)
            w = _nn(t, kbg)
            mqk = x["qk"] * dm
            qd = qs * gam
            dqd = _nt(dob, s)
            dkd_pre = _nn(kd, dsn)
            yield
            v_new = u - _nn(w, s)
            dv_new = _tn(mqk, dob) + dkd_pre
            dq = dqd * gam
            dgam = jnp.sum(dqd * qs, axis=1, keepdims=True)
            yield
            ds_state[h] = _tn(qd, dob) + eg_last * dsn - _tn(w, dv_new)
            dmm = jnp.where(x["incl"], _nt(dob, v_new), 0.0)
            dkd = _nt(v_new, dsn)
            dw = -_nt(dv_new, s)
            dvb = _tn(t, dv_new)
            dt = _nt(dv_new, vb)
            yield
            dqk = dmm * dm
            e_mat = dmm * mqk
            dq = dq + _nn(dqk, kx)
            dk = _tn(dqk, qs) + dkd * ek
            e1 = jnp.sum(dkd * kd, axis=1, keepdims=True)
            dgc = -e1
            dg_last = jnp.sum(e1) + eg_last * jnp.sum(s * dsn)
            dt = dt + _nt(dw, kbg)
            dkbg = _tn(t, dw)
            yield
            tdt = _tn(t, dt, hp="3x")
            yield
            da = jnp.where(x["strict"], -_nt(tdt, t, hp="3x"), 0.0)
            yield
            dkk = da * dm
            e_mat = e_mat + da * x["kk"] * dm
            dkb = _nn(dkk, kx) + dkbg * gam
            dk = dk + _tn(dkk, kb)
            dgam = dgam + jnp.sum(dkbg * kb, axis=1, keepdims=True)
            yield
            dk = dk + dkb * beta_
            dbeta = jnp.sum(dkb * kx, axis=1, keepdims=True) + jnp.sum(dvb * vx, axis=1, keepdims=True)
            dv = dvb * beta_
            dgc = dgc + jnp.sum(e_mat, axis=1, keepdims=True) + dgam * gam
            dgc = dgc - jnp.sum(e_mat.T, axis=1, keepdims=True)
            yield
            dg = _cumsum_rows(dgc, reverse=True) + dg_last
            yield
            dq_ref[sl] = dq * (1.0 / math.sqrt(DN_DIM))
            dk_ref[sl] = dk
            dv_ref[sl] = dv
            dg_ref[sl] = dg
            db_ref[sl] = jnp.broadcast_to(dbeta, (c, LANE))

        chains = []
        for sub in reversed(range(GDN_SUB_CHUNKS)):
            new = [chain(sub, h) for h in range(DN_HEADS)]
            for _ in range(3):
                for ch in new:
                    next(ch)
            chains += new
        _interleave(chains)
        for h in range(DN_HEADS):
            ds_scr[h] = ds_state[h]

        dq, dk, dv = dq_ref[...], dk_ref[...], dv_ref[...]
        outs, accs = _f_gdn_prep_bwd(
            nblk - 1 - step, x_ref[...], xp_ref[...], xn_ref[...], ab_ref[...], dq, nxt_scr[0], dk, nxt_scr[1],
            dv, nxt_scr[2], dg_ref[...], db_ref[...], w8_ref[...], alog_ref[...], dtb_ref[...], nt=nblk)
        nxt_scr[0] = dq[:8]
        nxt_scr[1] = dk[:8]
        nxt_scr[2] = dv[:8]
        dqkv_ref[...] = outs[0].astype(dqkv_ref.dtype)
        dab_ref[...] = outs[1].astype(dab_ref.dtype)
        for ref, val in zip((dcw_ref, dalog_ref, ddtb_ref), accs):
            @pl.when(step == 0)
            def _():
                ref[...] = val

            @pl.when(step > 0)
            def _():
                ref[...] += val

    sub = GDN_SUB_CHUNKS
    r8 = blk // 8
    rb = lambda n: (nblk - 1 - n, 0)
    hs = lambda n: (0, nblk - 1 - n, 0, 0)
    rows = pl.BlockSpec((blk, DN_WIDTH), rb)
    whole = lambda a: pl.BlockSpec(a.shape, lambda n: (0,) * a.ndim)
    wide = 3 * DN_WIDTH
    return pl.pallas_call(
        body, name="gdn_bwd", grid=(nblk,),
        in_specs=[rows] * 5
        + [pl.BlockSpec((DN_HEADS, sub, DN_DIM, DN_DIM), hs), pl.BlockSpec((DN_HEADS, sub, c, c), hs), rows,
           pl.BlockSpec((blk, wide), rb),
           pl.BlockSpec((8, wide), lambda n: (jnp.maximum((nblk - 1 - n) * r8 - 1, 0), 0)),
           pl.BlockSpec((8, wide), lambda n: (jnp.minimum((nblk - n) * r8, tp // 8 - 1), 0)),
           pl.BlockSpec((blk, LANE), lambda n: (nblk - 1 - n, C_AB // LANE)),
           whole(conv_w8), whole(alog), whole(dtb), _ANY_SPEC],
        out_specs=[pl.BlockSpec((blk, wide), rb), pl.BlockSpec((blk, LANE), rb),
                   whole(conv_w8), whole(alog), whole(dtb)],
        out_shape=[jax.ShapeDtypeStruct((tp, wide), _MXU), jax.ShapeDtypeStruct((tp, LANE), _MXU),
                   jax.ShapeDtypeStruct(conv_w8.shape, F32), jax.ShapeDtypeStruct(alog.shape, F32),
                   jax.ShapeDtypeStruct(dtb.shape, F32)],
        scratch_shapes=[pltpu.VMEM((DN_HEADS, DN_DIM, DN_DIM), F32)] + [pltpu.VMEM((blk, DN_WIDTH), F32)] * 5
        + [pltpu.VMEM((3, 8, DN_WIDTH), F32)],
        compiler_params=pltpu.CompilerParams(dimension_semantics=("arbitrary",), vmem_limit_bytes=VMEM_LIMIT),
    )(q, k, v, g, beta, s_all, t_all, do, proj, proj, proj, proj, conv_w8, alog, dtb, after)


def _silu_parts(x):
    s = _sigmoid(x)
    return x * s, s * (1.0 + x * (1.0 - s))


def _f_rms_bwd_add(i, x, dy, dres, w, *, mask_pad):
    dx, dwr = _rms_bwd(x, w, dy, x.shape[1])
    out = dres + dx
    if mask_pad:
        out = jnp.where(_row_ids(i, x.shape[0]) >= PAD, out, 0.0)
    return (out,), (_rowsum(dwr),)


def _rope(x, cos, sin_s):
    return x * cos + _swap_halves(x) * sin_s


def _rope_t(dy, cos, sin_s):
    return dy * cos + _swap_halves(dy * sin_s)


def _f_mla_qk(i, qf, kvf, kpe, cos, sin_s, qw, kw):
    qs, ks, vs = [], [], []
    for h in range(MLA_HEADS):
        qn, _ = _rms_fwd(qf[:, HP * h:HP * (h + 1)], qw, QK_HEAD)
        qs += [qn[:, :QK_NOPE], _rope(qn[:, QK_NOPE:], cos, sin_s)]
        kh = jnp.concatenate([kvf[:, HP * h:HP * h + QK_NOPE], kpe], axis=1)
        kn, _ = _rms_fwd(kh, kw, QK_HEAD)
        ks += [kn[:, :QK_NOPE], _rope(kn[:, QK_NOPE:], cos, sin_s)]
        vs.append(kvf[:, HP * h + QK_NOPE:HP * (h + 1)])
    return (jnp.concatenate(qs, axis=1), jnp.concatenate(ks, axis=1), jnp.concatenate(vs, axis=1)), ()


def _f_mla_front(i, ql, kvl, kpe, cos, sin_s, qaw, kvaw, wq_t, wkv, qw, kw):
    qn = _rms_fwd(ql, qaw, Q_LORA)[0].astype(_MXU)
    kvn = _rms_fwd(kvl, kvaw, KV_LORA)[0].astype(_MXU)
    qf = _nt(qn, wq_t)
    kvf = _nn(kvn, wkv)
    (q, k, v), _ = _f_mla_qk(i, qf, kvf, kpe, cos, sin_s, qw, kw)
    return (qn, kvn, qf, kvf, q, k, v), ()


def _f_mla_back(i, qf, kvf, kpe, cos, sin_s, dq, dk, dv, ql, kvl, qaw, kvaw, wq_t, wkv, qw, kw):
    (dqf, dkvf, dkpe), (dqw, dkw) = _f_mla_qk_bwd(i, qf, kvf, kpe, cos, sin_s, dq, dk, dv, qw, kw)
    dqf = dqf.astype(_MXU)
    dkvf = dkvf.astype(_MXU)
    dql, dqaw = _rms_bwd(ql, qaw, _nn(dqf, wq_t), Q_LORA)
    dkvl, dkvaw = _rms_bwd(kvl, kvaw, _nt(dkvf, wkv), KV_LORA)
    return (dqf, dkvf, dkpe, dql, dkvl), (dqw, dkw, _rowsum(dqaw), _rowsum(dkvaw))


def _f_mla_qk_bwd(i, qf, kvf, kpe, cos, sin_s, dq, dk, dv, qw, kw):
    dqf, dkvf = [], []
    dkpe = None
    dqw = None
    dkw = None
    for h in range(MLA_HEADS):
        dqh = dq[:, HP * h:HP * (h + 1)]
        dqn = jnp.concatenate([dqh[:, :QK_NOPE], _rope_t(dqh[:, QK_NOPE:], cos, sin_s)], axis=1)
        dx, dwr = _rms_bwd(qf[:, HP * h:HP * (h + 1)], qw, dqn, QK_HEAD)
        dqf.append(dx)
        dqw = _rowsum(dwr) if dqw is None else dqw + _rowsum(dwr)
        dkh = dk[:, HP * h:HP * (h + 1)]
        dkn = jnp.concatenate([dkh[:, :QK_NOPE], _rope_t(dkh[:, QK_NOPE:], cos, sin_s)], axis=1)
        kh = jnp.concatenate([kvf[:, HP * h:HP * h + QK_NOPE], kpe], axis=1)
        dx, dwr = _rms_bwd(kh, kw, dkn, QK_HEAD)
        dkvf += [dx[:, :QK_NOPE], dv[:, V_HEAD * h:V_HEAD * (h + 1)]]
        dkpe = dx[:, QK_NOPE:] if dkpe is None else dkpe + dx[:, QK_NOPE:]
        dkw = _rowsum(dwr) if dkw is None else dkw + _rowsum(dwr)
    return (jnp.concatenate(dqf, axis=1), jnp.concatenate(dkvf, axis=1), dkpe), (dqw, dkw)


def _gdn_act(i, x, halo, w8):
    halo = jnp.where(i > 0, halo, 0.0)
    c = _conv_fwd(x, halo, w8, DN_CONV)
    act, dact = _silu_parts(c)
    return act, dact


def _spread_heads(ab):
    tm = ab.shape[0]
    return jnp.concatenate([jnp.broadcast_to(ab[:, h:h + 1], (tm, DN_DIM)) for h in range(2 * DN_HEADS)], axis=1)


def _gather_heads(x):
    tm = x.shape[0]
    lane = lax.broadcasted_iota(jnp.int32, (tm, LANE), 1)
    out = jnp.zeros((tm, LANE), F32)
    for h in range(2 * DN_HEADS):
        out = out + jnp.where(lane == h, x[:, DN_DIM * h:DN_DIM * h + 1], 0.0)
    return out


def _gate_parts(ab, dtb):
    lane1 = lax.broadcasted_iota(jnp.int32, (1, LANE), 1)
    dtb_c = jnp.zeros((1, LANE), F32)
    for h in range(DN_HEADS):
        dtb_c = dtb_c + jnp.where(lane1 == h, dtb[:, DN_DIM * h:DN_DIM * h + 1], 0.0)
    pre = ab + dtb_c
    sig = _sigmoid(pre)
    lane = lax.broadcasted_iota(jnp.int32, ab.shape, 1)
    return jnp.where(lane < DN_HEADS, _softplus(pre), sig), sig


def _f_gdn_prep(i, x, halo, ab, w8, alog, dtb):
    tm = x.shape[0]
    act, _ = _gdn_act(i, x, halo, w8)
    outs = []
    for part in range(2):
        for h in range(DN_HEADS):
            t = act[:, DN_WIDTH * part + DN_DIM * h:DN_WIDTH * part + DN_DIM * (h + 1)]
            outs.append(t * lax.rsqrt(jnp.sum(t * t, axis=-1, keepdims=True) + EPS))
    q = jnp.concatenate(outs[:DN_HEADS], axis=1)
    k = jnp.concatenate(outs[DN_HEADS:], axis=1)
    v = act[:, 2 * DN_WIDTH:]
    abb = _spread_heads(ab)
    valid = _row_ids(i, tm) >= PAD
    g = jnp.where(valid, -jnp.exp(alog) * _softplus(abb[:, :DN_WIDTH] + dtb), 0.0)
    beta = jnp.where(valid, _sigmoid(abb[:, DN_WIDTH:]), 0.0)
    return (q, k, v, g, beta), ()


def _f_gdn_prep_bwd(i, x, x_prev, x_next, ab, dq, dq_next, dk, dk_next, dv, dv_next, dg, dbeta,
                    w8, alog, dtb, *, nt):
    tm = x.shape[0]
    x_prev = jnp.where(i > 0, x_prev, 0.0)
    more = i < nt - 1
    ext = lambda t, t_next: jnp.concatenate([t, jnp.where(more, t_next, 0.0)], axis=0)
    taps = _conv_taps(jnp.concatenate([x, x_next], axis=0), x_prev, DN_CONV)
    c = _conv_from_taps(taps, w8)
    act, dact = _silu_parts(c)
    douts = []
    for part, dd in enumerate((ext(dq, dq_next), ext(dk, dk_next))):
        for h in range(DN_HEADS):
            t = act[:, DN_WIDTH * part + DN_DIM * h:DN_WIDTH * part + DN_DIM * (h + 1)]
            r = lax.rsqrt(jnp.sum(t * t, axis=-1, keepdims=True) + EPS)
            y = t * r
            dy = dd[:, DN_DIM * h:DN_DIM * (h + 1)]
            douts.append(r * (dy - y * jnp.sum(dy * y, axis=-1, keepdims=True)))
    douts.append(ext(dv, dv_next))
    dc = jnp.concatenate(douts, axis=1) * dact
    dqkv = _conv_bwd_x(dc[:tm], dc[tm:], w8, DN_CONV)
    dconv_w = _conv_bwd_w_taps(dc[:tm], taps)
    sp_beta, sig = _gate_parts(ab, dtb)
    spread = _spread_heads(sp_beta)
    valid = _row_ids(i, tm) >= PAD
    ea = jnp.exp(alog)
    g = -ea * spread[:, :DN_WIDTH]
    dg = jnp.where(valid, dg, 0.0)
    dbeta = jnp.where(valid, dbeta, 0.0)
    da = dg * (-ea) * _spread_heads(sig)[:, :DN_WIDTH]
    beta = spread[:, DN_WIDTH:]
    db = dbeta * beta * (1.0 - beta)
    dab = _gather_heads(jnp.concatenate([da, db], axis=1))
    return (dqkv, dab), (dconv_w, _rowsum(dg * g), _rowsum(da))


def _f_mix(i, o_mla, o_dn, z, w_mla, w_dn):
    tm = o_mla.shape[0]
    valid = _row_ids(i, tm) >= PAD
    outs = []
    for h in range(MLA_HEADS):
        y, _ = _rms_fwd(o_mla[:, V_HEAD * h:V_HEAD * (h + 1)], w_mla, V_HEAD)
        outs.append(jnp.where(valid, y, 0.0))
    for h in range(DN_HEADS):
        y, _ = _rms_fwd(o_dn[:, DN_DIM * h:DN_DIM * (h + 1)], w_dn, DN_DIM)
        outs.append(y * _silu_parts(z[:, DN_DIM * h:DN_DIM * (h + 1)])[0])
    return (jnp.concatenate(outs, axis=1),), ()


def _f_mix_bwd(i, o_mla, o_dn, z, dy_mla, dy_dn, w_mla, w_dn):
    tm = o_mla.shape[0]
    valid = _row_ids(i, tm) >= PAD
    d_mla, d_dn, d_z = [], [], []
    dw_mla = None
    dw_dn = None
    for h in range(MLA_HEADS):
        sl = slice(V_HEAD * h, V_HEAD * (h + 1))
        dx, dwr = _rms_bwd(o_mla[:, sl], w_mla, jnp.where(valid, dy_mla[:, sl], 0.0), V_HEAD)
        d_mla.append(dx)
        dw_mla = _rowsum(dwr) if dw_mla is None else dw_mla + _rowsum(dwr)
    for h in range(DN_HEADS):
        sl = slice(DN_DIM * h, DN_DIM * (h + 1))
        y, _ = _rms_fwd(o_dn[:, sl], w_dn, DN_DIM)
        sz, dsz = _silu_parts(z[:, sl])
        d_z.append(dy_dn[:, sl] * y * dsz)
        dx, dwr = _rms_bwd(o_dn[:, sl], w_dn, dy_dn[:, sl] * sz, DN_DIM)
        d_dn.append(dx)
        dw_dn = _rowsum(dwr) if dw_dn is None else dw_dn + _rowsum(dwr)
    return ((jnp.concatenate(d_mla, axis=1), jnp.concatenate(d_dn, axis=1), jnp.concatenate(d_z, axis=1)),
            (dw_mla, dw_dn))


def _f_ffn_act_bwd(i, gp, gp_prev, gp_next, up, up_next, dact, dact_next, w8, b, *, nt):
    tm = gp.shape[0]
    gp_prev = jnp.where(i > 0, gp_prev, 0.0)
    dact_next = jnp.where(i < nt - 1, dact_next, 0.0)
    cat = lambda t, t_next: jnp.concatenate([t, t_next], axis=0)
    taps = _conv_taps(cat(gp, gp_next), gp_prev, FFN_CONV)
    gate = _conv_from_taps(taps, w8) + b
    sg, dsg = _silu_parts(gate)
    dact_e = cat(dact, dact_next)
    dgate = dact_e * cat(up, up_next) * dsg
    dgate_pre = _conv_bwd_x(dgate[:tm], dgate[tm:], w8, FFN_CONV)
    dup = dact * sg[:tm]
    return (dgate_pre, dup), (_conv_bwd_w_taps(dgate[:tm], taps), _rowsum(dgate[:tm]))


def _f_loss(i, h3, tgt):
    tm = h3.shape[0]
    diff = jnp.where(_row_ids(i, tm) >= ROW0, h3 - tgt, 0.0)
    part = 0.5 * jnp.sum(diff * diff) * (1.0 / D_MODEL)
    return (diff * (1.0 / D_MODEL),), (jnp.full((1, LANE), part, F32),)


def _local_step(h0, tgt, w, token, late_weights, grads_ready):
    tp = h0.shape[0]
    proj, u = _norm_mm("in_proj", h0, w["attn_norm_w"], w["w_in"], after=token)
    p_qkv = lambda kind="cur": _In(proj, 3 * DN_WIDTH, 0, kind)
    p_z = _In(proj, DN_WIDTH, C_Z // DN_WIDTH)
    p_ql = _In(proj, Q_LORA, C_QL // Q_LORA)
    p_kvl = _In(proj, KV_LORA, C_KVL // KV_LORA)
    p_kpe = _In(proj, LANE, C_KPE // LANE)
    p_ab = _In(proj, LANE, C_AB // LANE)
    cos, sin_s = _In(w["cos"]), _In(w["sin_s"])

    mla_w = [w["q_a_norm_w"], w["kv_a_norm_w"], w["w_q_b"], w["w_kv_b"], w["q_norm_w"], w["k_norm_w"]]
    tm_mla = _pick(tp, 288, 16)
    tm_big = _pick(tp, 576, 16)
    wide = MLA_HEADS * HP
    qn, kvn, qf, kvf, q, k, v = _rows(
        "mla_front", _f_mla_front, [p_ql, p_kvl, p_kpe, cos, sin_s], mla_w,
        [(Q_LORA, _MXU), (KV_LORA, _MXU), (wide, F32), (wide, F32), (wide, _MXU), (wide, _MXU),
         (MLA_HEADS * V_HEAD, _MXU)], tm=tm_big)
    o_mla = _attn_fwd(q, k, v)

    dn_w = [w["dn_conv_w"], w["alog_b"], w["dtb_b"]]
    o_dn, s_all, t_all, gq, gk, gv, gg, gb = _gdn_fwd(proj, *dn_w)

    out_w = [w["mla_out_norm_w"], w["dn_out_norm_w"]]
    w = dict(w, **late_weights((o_mla, o_dn), _LATE[:3]))
    h2, mixed = _pro_mm("mix_out_proj", lambda i, *t: _f_mix(i, *t)[0][0], [_In(o_mla), _In(o_dn), p_z], out_w,
                        D_MODEL, w["w_out"], h0)

    ffn_w = [w["ffn_conv_w"], w["ffn_conv_b"]]
    hn, gate_pre, up, act = _ffn_in(h2, w["ffn_norm_w"], w["w_gate"], w["w_up"], *ffn_w)
    w = dict(w, **late_weights(act, _LATE[3:]))
    dh3, loss = _mm_rows("ffn_down_loss", act, w["w_down"], "nn", lambda i, y, r, t: _f_loss(i, r + y, t),
                         [_In(h2), _In(tgt)], [], [(D_MODEL, F32)], [(1, LANE)])

    g = {}
    dact = _mm("ffn_down_dx", dh3, w["w_down"], "nt")
    g["w_down"] = _mm("ffn_down_dw", act, dh3, "tn", out_dtype=_MXU)
    dgate_pre, dup, g["ffn_conv_w"], g["ffn_conv_b"] = _rows(
        "ffn_act_bwd", functools.partial(_f_ffn_act_bwd, nt=tp // tm_mla),
        [_In(gate_pre), _In(gate_pre, kind="prev"), _In(gate_pre, kind="next"), _In(up), _In(up, kind="next"),
         _In(dact), _In(dact, kind="next")], ffn_w,
        [(D_FF, _MXU), (D_FF, _MXU)], [(8, D_FF), (1, D_FF)], tm=tm_mla)
    g["w_gate"], g["w_up"] = _mm_tn2("ffn_gate_up_dw", dgate_pre, dup, hn, out_dtype=_MXU)
    tok = grads_ready(g, ("w_down", "w_gate", "w_up"))
    dh2, g["ffn_norm_w"] = _mm_rows(
        "ffn_gate_up_dx_rms", [dgate_pre, dup], [w["w_gate"], w["w_up"]], "nn",
        lambda i, dy, x, dres, nw, _tok: _f_rms_bwd_add(i, x, dy, dres, nw, mask_pad=True),
        [_In(h2), _In(dh3)], [w["ffn_norm_w"], tok], [(D_MODEL, F32)], [(1, D_MODEL)], tm_cap=288)

    g["w_out"] = _mm("out_proj_dw", mixed, dh2, "tn", out_dtype=_MXU)
    half = MLA_HEADS * V_HEAD
    do_mla, do_dn, dz, g["mla_out_norm_w"], g["dn_out_norm_w"] = _mm_rows(
        "out_proj_dx_mix", dh2, w["w_out"], "nt",
        lambda i, dm, om, od, z, wm, wd: _f_mix_bwd(i, om, od, z, dm[:, :half], dm[:, half:], wm, wd),
        [_In(o_mla), _In(o_dn), p_z], out_w,
        [(half, F32), (DN_WIDTH, F32), (DN_WIDTH, _MXU)], [(1, V_HEAD), (1, DN_DIM)])

    dq, dk, dv = _attn_bwd(q, k, v, do_mla)
    dqf, dkvf, dkpe, dql, dkvl, g["q_norm_w"], g["k_norm_w"], g["q_a_norm_w"], g["kv_a_norm_w"] = _rows(
        "mla_back", _f_mla_back,
        [_In(qf), _In(kvf), p_kpe, cos, sin_s, _In(dq), _In(dk), _In(dv), p_ql, p_kvl], mla_w,
        [(wide, _MXU), (wide, _MXU), (LANE, _MXU), (Q_LORA, _MXU), (KV_LORA, _MXU)],
        [(1, HP), (1, HP), (1, Q_LORA), (1, KV_LORA)], tm=tm_big)
    g["w_q_b"], g["w_kv_b"] = _mm_tn_pair("mla_b_dw", dqf, qn, kvn, dkvf)
    tok = grads_ready(g, ("w_out", "w_q_b", "w_kv_b"))

    dqkv, dab, g["dn_conv_w"], g["alog_b"], g["dtb_b"] = _gdn_bwd(
        gq, gk, gv, gg, gb, s_all, t_all, do_dn, proj, *dn_w, tok)

    dproj = jnp.concatenate([dqkv, dz, dql, dkvl, dkpe, dab], axis=1)
    g["w_in"] = _mm("in_proj_dw", dproj, u, "tn", out_dtype=_MXU)
    tok = grads_ready(g, ("w_in",))
    dh0, g["attn_norm_w"] = _mm_rows(
        "in_proj_dx_rms", dproj, w["w_in"], "nn",
        lambda i, du, x, dres, nw, _tok: _f_rms_bwd_add(i, x, du, dres, nw, mask_pad=False),
        [_In(h0), _In(dh2)], [w["attn_norm_w"], tok], [(D_MODEL, F32)], [(1, D_MODEL)])
    return loss, dh0, g


def _w_in_to_padded(w):
    c1, c2, c3 = Q_LORA, Q_LORA + KV_LORA, Q_LORA + KV_LORA + QK_ROPE
    c4 = c3 + 3 * DN_WIDTH
    c5 = c4 + DN_WIDTH
    z = lambda n: jnp.zeros((n, w.shape[1]), w.dtype)
    return jnp.concatenate([w[c3:c4], w[c4:c5], w[:c1], w[c1:c2], w[c2:c3], z(LANE - QK_ROPE),
                            w[c5:], z(LANE - 2 * DN_HEADS)], axis=0)


def _w_in_from_padded(g):
    return jnp.concatenate([g[C_QL:C_QL + Q_LORA], g[C_KVL:C_KVL + KV_LORA], g[C_KPE:C_KPE + QK_ROPE],
                            g[:C_Z + DN_WIDTH], g[C_AB:C_AB + 2 * DN_HEADS]], axis=0)


def _w_q_b_to_padded(w):
    r = w.shape[1]
    w = w.reshape(MLA_HEADS, QK_HEAD, r)
    return jnp.pad(w, ((0, 0), (0, HP - QK_HEAD), (0, 0))).reshape(MLA_HEADS * HP, r)


def _w_q_b_from_padded(g):
    r = g.shape[1]
    return g.reshape(MLA_HEADS, HP, r)[:, :QK_HEAD].reshape(MLA_HEADS * QK_HEAD, r)


def _pad_rows8(w):
    return jnp.pad(w, ((0, 8 - w.shape[0]), (0, 0)))


def _prepare(full, tp):
    w = {}
    mx = lambda a: a.astype(_MXU)
    w["attn_norm_w"] = full["attn_norm_w"]
    w["w_in"] = mx(_w_in_to_padded(full["w_in"]))
    w["q_a_norm_w"] = full["q_a_norm_w"]
    w["kv_a_norm_w"] = full["kv_a_norm_w"]
    w["w_q_b"] = mx(_w_q_b_to_padded(full["w_q_b"]))
    w["w_kv_b"] = mx(full["w_kv_b"])
    w["q_norm_w"] = jnp.pad(full["q_norm_w"], ((0, 0), (0, HP - QK_HEAD)))
    w["k_norm_w"] = jnp.pad(full["k_norm_w"], ((0, 0), (0, HP - QK_HEAD)))
    w["mla_out_norm_w"] = full["mla_out_norm_w"]
    w["dn_out_norm_w"] = full["dn_out_norm_w"]
    w["dn_conv_w"] = _pad_rows8(full["dn_conv_w"])
    w["alog_b"] = jnp.repeat(full["dn_A_log"], DN_DIM, axis=1)
    w["dtb_b"] = jnp.repeat(full["dn_dt_bias"], DN_DIM, axis=1)
    w["ffn_norm_w"] = full["ffn_norm_w"]
    w["ffn_conv_w"] = _pad_rows8(full["ffn_conv_w"])
    w["ffn_conv_b"] = full["ffn_conv_b"]
    for n in _LATE:
        if n in full:
            w[n] = mx(full[n])
    half = QK_ROPE // 2
    inv = ROPE_THETA ** (-jnp.arange(half, dtype=F32) / half)
    ang = (jnp.arange(tp, dtype=jnp.int32) - PAD).astype(F32)[:, None] * inv[None, :]
    zc = jnp.zeros((tp, LANE - QK_ROPE), F32)
    w["cos"] = jnp.concatenate([jnp.cos(ang), jnp.cos(ang), zc], axis=1)
    w["sin_s"] = jnp.concatenate([-jnp.sin(ang), jnp.sin(ang), zc], axis=1)
    return w


def _grads_to_natural(g):
    convert = {
        "w_in": ("w_in", _w_in_from_padded),
        "w_q_b": ("w_q_b", _w_q_b_from_padded),
        "q_norm_w": ("q_norm_w", lambda a: a[:, :QK_HEAD]),
        "k_norm_w": ("k_norm_w", lambda a: a[:, :QK_HEAD]),
        "dn_conv_w": ("dn_conv_w", lambda a: a[:DN_CONV]),
        "ffn_conv_w": ("ffn_conv_w", lambda a: a[:FFN_CONV]),
        "alog_b": ("dn_A_log", lambda a: a[:, ::DN_DIM]),
        "dtb_b": ("dn_dt_bias", lambda a: a[:, ::DN_DIM]),
    }
    n = {}
    for key, a in g.items():
        name, fn = convert.get(key, (key, lambda t: t))
        n[name] = fn(a)
    return n


_MESH = pl.DeviceIdType.MESH
_ANY = pl.BlockSpec(memory_space=pl.ANY)
_CHIP_FLIPS = ((1, 0), (0, 1), (1, 1))


def _me():
    return lax.axis_index("x"), lax.axis_index("y"), lax.axis_index("c")


def _all_gather(name, blk, after):
    after = list(after)

    def body(x_ref, *rest):
        out_ref, send_sems, recv_sems, local_sem = rest[len(after):]
        x, y, c = _me()
        me, sib = (x, y, c), (x, y, 1 - c)
        chips = [(x ^ fx, y ^ fy) for fx, fy in _CHIP_FLIPS]

        def slot(p):
            return out_ref.at[4 * p[0] + 2 * p[1] + p[2]]

        def copy(k, block, to, src=None):
            return pltpu.make_async_remote_copy(
                src_ref=slot(block) if src is None else src, dst_ref=slot(block),
                send_sem=send_sems.at[k], recv_sem=recv_sems.at[k], device_id=to, device_id_type=_MESH)

        mine = pltpu.make_async_copy(x_ref, slot(me), local_sem)
        mine.start()
        first = [copy(0, me, sib, src=x_ref)]
        first += [copy(1 + j, me, (*chip, c), src=x_ref) for j, chip in enumerate(chips)]
        for cp in first:
            cp.start()
        passed = [copy(4 + j, (*chip, c), sib) for j, chip in enumerate(chips)]
        for j, chip in enumerate(chips):
            copy(1 + j, (*chip, c), me).wait_recv()
            passed[j].start()
        copy(0, sib, me).wait_recv()
        for j, chip in enumerate(chips):
            copy(4 + j, (*chip, 1 - c), me).wait_recv()
        for cp in first + passed:
            cp.wait_send()
        mine.wait()

    return pl.pallas_call(
        body, name=name, in_specs=[_ANY] * (1 + len(after)), out_specs=_ANY,
        out_shape=jax.ShapeDtypeStruct((N_DEV,) + blk.shape, blk.dtype),
        scratch_shapes=[pltpu.SemaphoreType.DMA((7,)), pltpu.SemaphoreType.DMA((7,)), pltpu.SemaphoreType.DMA],
    )(blk, *after)


def _row_tile(r):
    divs = [d for d in range(16, min(r, 512) + 1, 16) if r % d == 0]
    return divs[-1] if divs else r


def _adam_math(g, w, m, v):
    m_new = ADAM_B1 * m + (1.0 - ADAM_B1) * g
    v_new = ADAM_B2 * v + (1.0 - ADAM_B2) * (g * g)
    m_hat = m_new / (1.0 - ADAM_B1 ** ADAM_STEP)
    v_hat = v_new / (1.0 - ADAM_B2 ** ADAM_STEP)
    return -ADAM_LR * (m_hat / (jnp.sqrt(v_hat) + ADAM_EPS) + ADAM_WD * w), m_new, v_new


def _adam_vectors(name, row, items, ws, ms, vs):
    k = len(items)

    def body(row_ref, *refs):
        w_refs, m_refs, v_refs = refs[:k], refs[k:2 * k], refs[2 * k:3 * k]
        outs = refs[3 * k:]
        for idx, (off, n, per_head) in enumerate(items):
            if per_head:
                spread = row_ref[:, off:off + DN_WIDTH]
                lane = lax.broadcasted_iota(jnp.int32, (1, LANE), 1)
                g = jnp.zeros((1, LANE), F32)
                for h in range(DN_HEADS):
                    g = g + jnp.where(lane == h, spread[:, DN_DIM * h:DN_DIM * h + 1], 0.0)
                g = g[:, :n]
            else:
                g = row_ref[:, off:off + n]
            d, m_new, v_new = _adam_math(g, w_refs[idx][...], m_refs[idx][...], v_refs[idx][...])
            for kind, val in enumerate((g, d, m_new, v_new)):
                outs[kind * k + idx][...] = val

    shapes = [jax.ShapeDtypeStruct((1, n), F32) for _, n, _ in items]
    res = pl.pallas_call(body, name=name, out_shape=shapes * 4)(row, *ws, *ms, *vs)
    return [list(res[kind * k:(kind + 1) * k]) for kind in range(4)]


def _adam_arrays(name, gs, ws, ms, vs):
    k = len(gs)

    def body(*refs):
        outs = refs[4 * k:]
        for idx in range(k):
            res = _adam_math(refs[idx][...], refs[k + idx][...], refs[2 * k + idx][...], refs[3 * k + idx][...])
            for kind, val in enumerate(res):
                outs[kind * k + idx][...] = val

    shapes = [jax.ShapeDtypeStruct(w.shape, F32) for w in ws]
    res = pl.pallas_call(body, name=name, out_shape=shapes * 3)(*gs, *ws, *ms, *vs)
    return [list(res[kind * k:(kind + 1) * k]) for kind in range(3)]


def _sum_parts(name, parts):
    _, r, cols = parts[0][0].shape
    tm = _row_tile(r)
    idx = jnp.stack([jnp.asarray(s, jnp.int32) for _, s in parts])
    n = len(parts)

    def body(idx_ref, *refs):
        g = refs[0][0].astype(F32)
        for p_ref in refs[1:n]:
            g = g + p_ref[0].astype(F32)
        refs[n][...] = g

    return pl.pallas_call(
        body, name=name,
        grid_spec=pltpu.PrefetchScalarGridSpec(
            num_scalar_prefetch=1, grid=(r // tm,),
            in_specs=[pl.BlockSpec((1, tm, cols), lambda i, idx_ref, p=p: (idx_ref[p], i, 0)) for p in range(n)],
            out_specs=pl.BlockSpec((tm, cols), lambda i, idx_ref: (i, 0))),
        out_shape=jax.ShapeDtypeStruct((r, cols), F32),
        compiler_params=pltpu.CompilerParams(dimension_semantics=("parallel",)),
    )(idx, *[a for a, _ in parts])


def _adam(name, parts, w, m, v):
    r, cols = w.shape
    tm = _row_tile(r)
    tc = cols // 4 if (r // tm < 4 and cols % (4 * LANE) == 0) else cols
    idx = jnp.stack([jnp.asarray(s, jnp.int32) for _, s in parts])
    n = len(parts)

    def body(idx_ref, *refs):
        g = refs[0][0].astype(F32)
        for p_ref in refs[1:n]:
            g = g + p_ref[0].astype(F32)
        w_ref, m_ref, v_ref, g_out, d_out, m_out, v_out = refs[n:]
        g_out[...] = g
        d_out[...], m_out[...], v_out[...] = _adam_math(g, w_ref[...], m_ref[...], v_ref[...])

    part_specs = [pl.BlockSpec((1, tm, tc), lambda i, j, idx_ref, p=p: (idx_ref[p], i, j)) for p in range(n)]
    flat = pl.BlockSpec((tm, tc), lambda i, j, idx_ref: (i, j))
    return pl.pallas_call(
        body, name=name,
        grid_spec=pltpu.PrefetchScalarGridSpec(
            num_scalar_prefetch=1, grid=(r // tm, cols // tc), in_specs=part_specs + [flat] * 3,
            out_specs=[flat] * 4),
        out_shape=[jax.ShapeDtypeStruct((r, cols), F32)] * 4,
        compiler_params=pltpu.CompilerParams(dimension_semantics=("parallel", "parallel")),
    )(idx, *[a for a, _ in parts], w, m, v)


def _all_gather_many(name, blks):
    n = len(blks)

    def body(*refs):
        x_refs, out_refs = refs[:n], refs[n:2 * n]
        send_sems, recv_sems, local_sems = refs[2 * n:]
        x, y, c = _me()
        me, sib = (x, y, c), (x, y, 1 - c)
        chips = [(x ^ fx, y ^ fy) for fx, fy in _CHIP_FLIPS]

        def slot(a, p):
            return out_refs[a].at[4 * p[0] + 2 * p[1] + p[2]]

        def copy(a, k, block, to, src=None):
            return pltpu.make_async_remote_copy(
                src_ref=slot(a, block) if src is None else src, dst_ref=slot(a, block),
                send_sem=send_sems.at[7 * a + k], recv_sem=recv_sems.at[7 * a + k], device_id=to,
                device_id_type=_MESH)

        mine = [pltpu.make_async_copy(x_refs[a], slot(a, me), local_sems.at[a]) for a in range(n)]
        first = []
        for a in range(n):
            mine[a].start()
            first.append(copy(a, 0, me, sib, src=x_refs[a]))
            first += [copy(a, 1 + j, me, (*chip, c), src=x_refs[a]) for j, chip in enumerate(chips)]
        for cp in first:
            cp.start()
        passed = []
        for j, chip in enumerate(chips):
            for a in range(n):
                copy(a, 1 + j, (*chip, c), me).wait_recv()
                cp = copy(a, 4 + j, (*chip, c), sib)
                cp.start()
                passed.append(cp)
        for a in range(n):
            copy(a, 0, sib, me).wait_recv()
            for j, chip in enumerate(chips):
                copy(a, 4 + j, (*chip, 1 - c), me).wait_recv()
        for cp in first + passed:
            cp.wait_send()
        for cp in mine:
            cp.wait()

    return pl.pallas_call(
        body, name=name, in_specs=[_ANY] * n, out_specs=[_ANY] * n,
        out_shape=[jax.ShapeDtypeStruct((N_DEV,) + b.shape, b.dtype) for b in blks],
        scratch_shapes=[pltpu.SemaphoreType.DMA((7 * n,)), pltpu.SemaphoreType.DMA((7 * n,)),
                        pltpu.SemaphoreType.DMA((n,))],
    )(*blks)


_HBM = pl.BlockSpec(memory_space=pltpu.HBM)
_SEM = pl.BlockSpec(memory_space=pltpu.SEMAPHORE)
_EFFECT = pltpu.SideEffectType.DATAFLOW_SIDE_EFFECTING


def _push_copies(src_refs, land_refs, send_sems, recv_sems, src_by_peer, first=0):
    x, y, c = _me()
    my_id = 4 * x + 2 * y + c
    out = []
    for k in range(len(src_refs)):
        a = first + k
        for f in range(1, N_DEV):
            px, py, pc = x ^ (f >> 2), y ^ ((f >> 1) & 1), c ^ (f & 1)
            pid = 4 * px + 2 * py + pc
            src = src_refs[k].at[pid] if src_by_peer else src_refs[k]
            start = pltpu.make_async_remote_copy(
                src_ref=src, dst_ref=land_refs[k].at[my_id], send_sem=send_sems.at[7 * a + f - 1],
                recv_sem=recv_sems.at[7 * a + f - 1], device_id=(px, py, pc), device_id_type=_MESH)
            landed = pltpu.make_async_remote_copy(
                src_ref=src, dst_ref=land_refs[k].at[pid], send_sem=send_sems.at[7 * a + f - 1],
                recv_sem=recv_sems.at[7 * a + f - 1], device_id=(px, py, pc), device_id_type=_MESH)
            out.append((start, landed))
    return out


def _push_start(name, srcs, src_by_peer, after):
    n = len(srcs)
    lands = [jax.ShapeDtypeStruct((N_DEV,) + (s.shape[1:] if src_by_peer else s.shape), s.dtype) for s in srcs]

    def body(*refs):
        src_refs, land_refs = refs[:n], refs[n:2 * n]
        send_sems, recv_sems = refs[2 * n + 1], refs[2 * n + 2]
        token = refs[-1]
        for start, _ in _push_copies(src_refs, land_refs, send_sems, recv_sems, src_by_peer):
            start.start()
        token[...] = jnp.zeros_like(token)

    hbm = lambda a: pltpu.with_memory_space_constraint(a, pltpu.HBM)
    res = pl.pallas_call(
        body, name=name,
        out_shape=(pltpu.SemaphoreType.DMA((7 * n,)), pltpu.SemaphoreType.DMA((7 * n,)),
                   *[pltpu.HBM(s.shape, s.dtype) for s in srcs], *[pltpu.HBM(s.shape, s.dtype) for s in lands],
                   jax.ShapeDtypeStruct((8, LANE), F32)),
        in_specs=[_HBM] * (2 * n) + [_ANY],
        out_specs=(_SEM, _SEM, *[_HBM] * (2 * n), pl.BlockSpec(memory_space=pltpu.VMEM)),
        input_output_aliases={i: 2 + i for i in range(2 * n)},
        compiler_params=pltpu.CompilerParams(has_side_effects=_EFFECT),
    )(*[hbm(s) for s in srcs], *[hbm(lax.empty(s.shape, s.dtype)) for s in lands], after)
    return res[0], res[1], list(res[2:2 + n]), list(res[2 + n:2 + 2 * n]), res[-1]


def _push_wait(name, send_sems, recv_sems, srcs, lands, src_by_peer, after, first=0):
    n = len(srcs)
    after = list(after) if isinstance(after, (list, tuple)) else [after]

    def body(*refs):
        src_refs, land_refs = refs[:n], refs[n:2 * n]
        s_sems, r_sems = refs[2 * n], refs[2 * n + 1]
        for _, landed in _push_copies(src_refs, land_refs, s_sems, r_sems, src_by_peer, first):
            landed.wait_send()
            landed.wait_recv()

    res = pl.pallas_call(
        body, name=name,
        out_shape=tuple(pltpu.HBM(s.shape, s.dtype) for s in list(srcs) + list(lands)),
        in_specs=[_HBM] * (2 * n) + [_SEM, _SEM] + [_ANY] * len(after),
        out_specs=tuple([_HBM] * (2 * n)),
        input_output_aliases={i: i for i in range(2 * n)},
        compiler_params=pltpu.CompilerParams(has_side_effects=_EFFECT),
    )(*srcs, *lands, send_sems, recv_sems, *after)
    return list(res[:n]), list(res[n:])


_SHARDED = (
    ("meta_tokens", 1, (N_META, D_MODEL)),
    ("w_in", 1, (D_MODEL, IN_COLS)),
    ("w_q_b", 1, (Q_LORA, MLA_HEADS * QK_HEAD)),
    ("w_kv_b", 1, (KV_LORA, MLA_HEADS * (QK_NOPE + V_HEAD))),
    ("dn_conv_w", 1, (DN_CONV, 3 * DN_WIDTH)),
    ("w_out", 0, (2 * DN_WIDTH, D_MODEL)),
    ("w_gate", 1, (D_MODEL, D_FF)),
    ("w_up", 1, (D_MODEL, D_FF)),
    ("ffn_conv_w", 1, (FFN_CONV, D_FF)),
    ("w_down", 0, (D_FF, D_MODEL)),
)
_F32_GATHERED = ("meta_tokens", "dn_conv_w", "ffn_conv_w")
_EARLY = ("w_in", "w_q_b", "w_kv_b")
_LATE = ("w_out", "w_gate", "w_up", "w_down")
_TRANSPOSED = ("w_in", "w_q_b", "w_gate", "w_up")
_REPLICATED = (
    ("attn_norm_w", D_MODEL), ("q_a_norm_w", Q_LORA), ("kv_a_norm_w", KV_LORA), ("q_norm_w", QK_HEAD),
    ("k_norm_w", QK_HEAD), ("mla_out_norm_w", V_HEAD), ("dn_A_log", DN_HEADS), ("dn_dt_bias", DN_HEADS),
    ("dn_out_norm_w", DN_DIM), ("ffn_norm_w", D_MODEL), ("ffn_conv_b", D_FF),
)
_SMALL_BLOCK = (8, 512)


def _local_shape(dim, shape):
    return (shape[0] // N_DEV, shape[1]) if dim == 0 else (shape[0], shape[1] // N_DEV)


def _from_blocks(blocks, dim, shape):
    r, c = shape
    if dim == 0:
        return blocks.reshape(r, c)
    return blocks.reshape(N_DEV, r, c // N_DEV).transpose(1, 0, 2).reshape(r, c)


def _split(flat, sizes):
    out, o = [], 0
    for s in sizes:
        out.append(flat[..., o:o + s])
        o += s
    return out


def kernel(x, meta_tokens, attn_norm_w, w_in, q_a_norm_w, w_q_b, kv_a_norm_w, w_kv_b, q_norm_w, k_norm_w, mla_out_norm_w, dn_conv_w, dn_A_log, dn_dt_bias, dn_out_norm_w, w_out, ffn_norm_w, w_gate, w_up, ffn_conv_w, ffn_conv_b, w_down, loss_target, m_meta_tokens, m_attn_norm_w, m_w_in, m_q_a_norm_w, m_w_q_b, m_kv_a_norm_w, m_w_kv_b, m_q_norm_w, m_k_norm_w, m_mla_out_norm_w, m_dn_conv_w, m_dn_A_log, m_dn_dt_bias, m_dn_out_norm_w, m_w_out, m_ffn_norm_w, m_w_gate, m_w_up, m_ffn_conv_w, m_ffn_conv_b, m_w_down, v_meta_tokens, v_attn_norm_w, v_w_in, v_q_a_norm_w, v_w_q_b, v_kv_a_norm_w, v_w_kv_b, v_q_norm_w, v_k_norm_w, v_mla_out_norm_w, v_dn_conv_w, v_dn_A_log, v_dn_dt_bias, v_dn_out_norm_w, v_w_out, v_ffn_norm_w, v_w_gate, v_w_up, v_ffn_conv_w, v_ffn_conv_b, v_w_down):
    names = [n for n, _, _ in _SHARDED] + [n for n, _ in _REPLICATED]
    given = dict(locals())
    two_d = lambda a: a.reshape(a.shape[-2:])
    view = lambda a, n: two_d(a).T if n in _TRANSPOSED else two_d(a)
    wl = {n: view(given[n], n) for n in names}
    ml = {n: view(given["m_" + n], n) for n in names}
    vl = {n: view(given["v_" + n], n) for n in names}
    out_shapes = {n: given[n].shape for n in names}

    spec = {n: (d, s) for n, d, s in _SHARDED}
    small_sizes = [math.prod(_local_shape(*spec[n])) for n in _F32_GATHERED]

    def small_block(d):
        cat = jnp.concatenate([d[n].reshape(d[n].shape[:-2] + (-1,)) for n in _F32_GATHERED], axis=-1)
        pad = [(0, 0)] * (cat.ndim - 1) + [(0, math.prod(_SMALL_BLOCK) - cat.shape[-1])]
        return jnp.pad(cat, pad).reshape(cat.shape[:-1] + _SMALL_BLOCK)

    def shard(n):
        return wl[n].astype(_MXU)

    def from_slots(n, blocks):
        d, s = spec[n]
        if d == 0 or n in _TRANSPOSED:
            return blocks.reshape(-1, blocks.shape[-1])
        return blocks.transpose(1, 0, 2).reshape(s)

    my_id = 4 * lax.axis_index("x") + 2 * lax.axis_index("y") + lax.axis_index("c")
    got = _all_gather_many("gather_early", [shard(n) for n in _EARLY] + [small_block(wl)])
    full = {n: a for n, a in wl.items() if n not in _LATE}
    for n, blocks in zip(_EARLY, got):
        full[n] = from_slots(n, blocks)
    for n, p in zip(_F32_GATHERED, _split(got[-1].reshape(N_DEV, -1), small_sizes)):
        full[n] = _from_blocks(p, *spec[n])
    late_own = [shard(n) for n in _LATE]
    l_send, l_recv, l_src, l_land, token = _push_start("gather_late_start", late_own, False, got[-1])

    def late_weights(after, names):
        first = _LATE.index(names[0])
        sl = slice(first, first + len(names))
        _, lands = _push_wait("gather_late_wait_" + names[0], l_send, l_recv, l_src[sl], l_land[sl], False,
                              after, first)
        out = {}
        for n, land, own in zip(names, lands, late_own[sl]):
            out[n] = from_slots(n, lax.dynamic_update_slice(land, own[None], (my_id, 0, 0))).astype(_MXU)
        return out

    def dest_blocks(n, a):
        d, s = spec[n]
        r, c = _local_shape(d, s)
        if n in _TRANSPOSED:
            return a.reshape(N_DEV, c, r)
        return a.reshape(N_DEV, r, c) if d == 0 else a.reshape(r, N_DEV, c).transpose(1, 0, 2)

    pushed = []

    def grads_ready(g, names):
        nat = _grads_to_natural({n: g[n] for n in names})
        blocks = [dest_blocks(n, nat[n]).astype(_MXU) for n in names]
        sends, recvs, srcs, lands, tok = _push_start("rs_" + names[0] + "_start", blocks, True, token)
        pushed.append((names, sends, recvs, srcs, lands))
        return tok

    seq = x.shape[1]
    tp = ROW0 + seq
    h0 = jnp.concatenate([jnp.zeros((PAD, D_MODEL), F32), full["meta_tokens"], x[0]], axis=0)
    tgt = jnp.concatenate([jnp.zeros((ROW0, D_MODEL), F32), loss_target[0]], axis=0)
    loss, dh0, raw = _local_step(h0, tgt, _prepare(full, tp), token, late_weights, grads_ready)
    g = _grads_to_natural(raw)
    g["meta_tokens"] = dh0[PAD:ROW0]
    grad_x = dh0[ROW0:][None]

    big = [{}, {}, {}, {}]

    def finish(group):
        names, sends, recvs, srcs, lands = group
        srcs, lands = _push_wait("rs_" + names[0] + "_wait", sends, recvs, srcs, lands, True, dh0)
        for n, src, land in zip(names, srcs, lands):
            parts = [(src, my_id)] + [(land, my_id ^ f) for f in range(1, N_DEV)]
            for kind, a in enumerate(_adam("adam_" + n, parts, wl[n], ml[n], vl[n])):
                big[kind][n] = a

    for group in pushed[:-1]:
        finish(group)
    rep_names = [n for n, _ in _REPLICATED]
    raw_key = {"dn_A_log": "alog_b", "dn_dt_bias": "dtb_b"}
    pieces = [raw[raw_key.get(n, n)] for n in rep_names] + [loss]
    pieces += [g[n].reshape(1, -1) for n in _F32_GATHERED]
    widths = [p.shape[1] for p in pieces]
    offs = [sum(widths[:k]) for k in range(len(widths))]
    cat = jnp.concatenate(pieces, axis=1)
    cols = -(-cat.shape[1] // (8 * LANE)) * LANE
    mine = jnp.pad(cat, ((0, 0), (0, 8 * cols - cat.shape[1]))).reshape(8, cols)
    everyone = _all_gather("gather_small_grads", mine, [big[1][n] for group in pushed[:-1] for n in group[0]])
    total = _sum_parts("sum_small_grads", [(everyone, d) for d in range(N_DEV)]).reshape(1, 8 * cols)
    tot = {n: total[0, o:o + wd] for n, o, wd in zip(rep_names + ["loss"] + list(_F32_GATHERED), offs, widths)}
    items = [(o, size, n in raw_key) for (n, size), o in zip(_REPLICATED, offs)]
    sm = _adam_vectors("adam_replicated", total, items, [wl[n] for n in rep_names], [ml[n] for n in rep_names],
                       [vl[n] for n in rep_names])
    sm = [dict(zip(rep_names, kind)) for kind in sm]
    mine_of = {}
    for n in _F32_GATHERED:
        d, s = spec[n]
        r, c = _local_shape(d, s)
        mine_of[n] = lax.dynamic_slice(tot[n].reshape(s), (0, my_id * c), (r, c))
    res = _adam_arrays("adam_small_sharded", [mine_of[n] for n in _F32_GATHERED], [wl[n] for n in _F32_GATHERED],
                       [ml[n] for n in _F32_GATHERED], [vl[n] for n in _F32_GATHERED])
    for kind, arrays in enumerate([[mine_of[n] for n in _F32_GATHERED]] + res):
        big[kind].update(zip(_F32_GATHERED, arrays))

    finish(pushed[-1])

    outs = [tot["loss"][0], grad_x]
    for kind in range(4):
        for n in ("meta_tokens", "attn_norm_w", "w_in", "q_a_norm_w", "w_q_b", "kv_a_norm_w", "w_kv_b", "q_norm_w",
                  "k_norm_w", "mla_out_norm_w", "dn_conv_w", "dn_A_log", "dn_dt_bias", "dn_out_norm_w", "w_out",
                  "ffn_norm_w", "w_gate", "w_up", "ffn_conv_w", "ffn_conv_b", "w_down"):
            src = big[kind] if n in big[kind] else sm[kind]
            a = src[n].T if n in _TRANSPOSED else src[n]
            outs.append(a.reshape(out_shapes[n]))
    return tuple(outs)
```

```python
import functools
import math

import jax
import jax.numpy as jnp
from jax import lax
from jax.experimental import pallas as pl
from jax.experimental.pallas import tpu as pltpu

F32 = jnp.float32
_MXU = jnp.bfloat16
_HI = lax.Precision.HIGHEST

D_MODEL = 1024
N_META = 16
PAD = 112
ROW0 = PAD + N_META
MLA_HEADS = 4
QK_NOPE = 128
QK_ROPE = 64
QK_HEAD = QK_NOPE + QK_ROPE
V_HEAD = 128
Q_LORA = 256
KV_LORA = 256
ROPE_THETA = 10000.0
DN_HEADS = 4
DN_DIM = 128
DN_WIDTH = DN_HEADS * DN_DIM
DN_CONV = 4
DN_CHUNK = 64
GDN_SUB_CHUNKS = 2
D_FF = 2816
FFN_CONV = 3
EPS = 1e-6
HP = 256
C_Z = 1536
C_QL = 2048
C_KVL = 2304
C_KPE = 2560
C_AB = 2688
IN_COLS = 2632

ADAM_LR = 0.001
ADAM_B1 = 0.9
ADAM_B2 = 0.999
ADAM_EPS = 1e-08
ADAM_WD = 0.01
ADAM_STEP = 10

N_DEV = 8
TM = 128
LANE = 128
VMEM_LIMIT = 56 * 1024 * 1024
NEG = -1e30


def _dot(a, b, dims, hp=False):
    if hp:
        return lax.dot_general(a.astype(F32), b.astype(F32), (dims, ((), ())),
                               precision=lax.Precision.HIGH if hp == "3x" else _HI, preferred_element_type=F32)
    return lax.dot_general(a.astype(_MXU), b.astype(_MXU), (dims, ((), ())),
                           preferred_element_type=F32)


def _nn(a, b, hp=False):
    return _dot(a, b, ((1,), (0,)), hp)


def _nt(a, b, hp=False):
    return _dot(a, b, ((1,), (1,)), hp)


def _tn(a, b, hp=False):
    return _dot(a, b, ((0,), (0,)), hp)


def _sigmoid(x):
    return 1.0 / (1.0 + jnp.exp(-x))


def _rms_fwd(x, w, n):
    r = lax.rsqrt(jnp.sum(x * x, axis=-1, keepdims=True) * (1.0 / n) + EPS)
    return x * r * w, r


def _rms_bwd(x, w, dy, n):
    r = lax.rsqrt(jnp.sum(x * x, axis=-1, keepdims=True) * (1.0 / n) + EPS)
    xh = x * r
    gy = dy * w
    dx = r * (gy - xh * (jnp.sum(gy * xh, axis=-1, keepdims=True) * (1.0 / n)))
    return dx, dy * xh


def _rowsum(x):
    return jnp.sum(x, axis=0, keepdims=True)


def _row_ids(i, tm):
    return i * tm + lax.broadcasted_iota(jnp.int32, (tm, 1), 0)


def _shift_down(ext, s, tm):
    if s == 0:
        return ext[8:8 + tm]
    return pltpu.roll(ext, s, 0)[8:8 + tm]


def _shift_up(ext, s, tm):
    if s == 0:
        return ext[0:tm]
    return pltpu.roll(ext, tm + 8 - s, 0)[0:tm]


def _conv_taps(x, halo_prev, width):
    tm = x.shape[0]
    ext = jnp.concatenate([halo_prev, x], axis=0)
    return [_shift_down(ext, width - 1 - j, tm) for j in range(width)]


def _conv_from_taps(taps, w):
    y = None
    for j, tap in enumerate(taps):
        t = w[j:j + 1, :] * tap
        y = t if y is None else y + t
    return y


def _conv_fwd(x, halo_prev, w, width):
    return _conv_from_taps(_conv_taps(x, halo_prev, width), w)


def _conv_bwd_w_taps(dy, taps):
    tm = dy.shape[0]
    rows = [_rowsum(dy * tap[:tm]) for tap in taps]
    rows += [jnp.zeros_like(rows[0])] * (8 - len(taps))
    return jnp.concatenate(rows, axis=0)


def _conv_bwd_x(dy, halo_next, w, width):
    tm = dy.shape[0]
    ext = jnp.concatenate([dy, halo_next], axis=0)
    dx = None
    for j in range(width):
        t = w[j:j + 1, :] * _shift_up(ext, width - 1 - j, tm)
        dx = t if dx is None else dx + t
    return dx


def _softplus(x):
    e = jnp.exp(-jnp.abs(x))
    u = 1.0 + e
    l1p = jnp.where(u == 1.0, e, jnp.log(u) * e / jnp.where(u == 1.0, 1.0, u - 1.0))
    return jnp.maximum(x, 0.0) + l1p


def _swap_halves(x):
    lane = lax.broadcasted_iota(jnp.int32, x.shape, 1)
    return jnp.where(lane < 32, pltpu.roll(x, 96, 1), jnp.where(lane < 64, pltpu.roll(x, 32, 1), 0.0))


class _In:
    def __init__(self, arr, width=None, cb=0, kind="cur"):
        self.arr, self.kind = arr, kind
        self.width = arr.shape[1] if width is None else width
        self.cb = cb


def _whole_spec(x):
    return pl.BlockSpec(x.shape, lambda i, nd=x.ndim: (0,) * nd, pipeline_mode=pl.Buffered(1))


def _tile_spec(t, tm, tp):
    r8 = tm // 8
    if t.kind == "cur":
        return pl.BlockSpec((tm, t.width), lambda i, cb=t.cb: (i, cb))
    if t.kind == "prev":
        return pl.BlockSpec((8, t.width), lambda i, cb=t.cb: (jnp.maximum(i * r8 - 1, 0), cb))
    return pl.BlockSpec((8, t.width), lambda i, cb=t.cb: (jnp.minimum((i + 1) * r8, tp // 8 - 1), cb))


def _rows(name, fn, tiled, full, outs, accs=(), tm=TM):
    tp = tiled[0].arr.shape[0]
    nt = tp // tm
    n_in = len(tiled) + len(full)
    n_out = len(outs)

    def body(*refs):
        i = pl.program_id(0)
        vals = [r[...] for r in refs[:n_in]]
        o_t, o_a = fn(i, *vals)
        for r, v in zip(refs[n_in:n_in + n_out], o_t):
            r[...] = v.astype(r.dtype)
        for r, v in zip(refs[n_in + n_out:], o_a):
            @pl.when(i == 0)
            def _():
                r[...] = v

            @pl.when(i > 0)
            def _():
                r[...] += v

    in_specs = [_tile_spec(t, tm, tp) for t in tiled]
    in_specs += [pl.BlockSpec(a.shape, lambda i, nd=a.ndim: (0,) * nd) for a in full]
    out_specs = [pl.BlockSpec((tm, w), lambda i: (i, 0)) for w, _ in outs]
    out_specs += [pl.BlockSpec((r, w), lambda i: (0, 0)) for r, w in accs]
    out_shape = [jax.ShapeDtypeStruct((tp, w), dt) for w, dt in outs]
    out_shape += [jax.ShapeDtypeStruct((r, w), F32) for r, w in accs]
    res = pl.pallas_call(
        body, name=name, grid=(nt,), in_specs=in_specs, out_specs=out_specs, out_shape=out_shape,
        compiler_params=pltpu.CompilerParams(dimension_semantics=("arbitrary",), vmem_limit_bytes=VMEM_LIMIT),
    )(*[t.arr for t in tiled], *full)
    return res


def _pick(n, cap, mult):
    best = None
    for d in range(mult, min(n, cap) + 1, mult):
        if n % d == 0:
            best = d
    assert best is not None, (n, cap, mult)
    return best


_ANY_SPEC = pl.BlockSpec(memory_space=pl.ANY)


def _mm(name, a, b, mode, out_dtype=F32, resid=None, after=None):
    if mode == "tn":
        m, k = a.shape
        n = b.shape[1]
        tk = _pick(k, 512, 128)
        tn = _pick(n, 1408, 128)

        def body_tn(a_ref, b_ref, o_ref):
            o_ref[...] = _tn(a_ref[...], b_ref[...]).astype(o_ref.dtype)

        return pl.pallas_call(
            body_tn, name=name, grid=(n // tn, k // tk),
            in_specs=[pl.BlockSpec((m, tk), lambda j, p: (0, p)),
                      pl.BlockSpec((m, tn), lambda j, p: (0, j))],
            out_specs=pl.BlockSpec((tk, tn), lambda j, p: (p, j)),
            out_shape=jax.ShapeDtypeStruct((k, n), out_dtype),
            compiler_params=pltpu.CompilerParams(
                dimension_semantics=("parallel", "parallel"), vmem_limit_bytes=VMEM_LIMIT),
        )(a, b)

    m, k = a.shape
    n = b.shape[1] if mode == "nn" else b.shape[0]
    tn = _pick(n, 1408, 128)
    tm = _pick(m, 1152, 16)
    dotf = _nn if mode == "nn" else _nt

    def body(*refs):
        a_ref, b_ref, o_ref = refs[0], refs[1], refs[-1]
        acc = dotf(a_ref[...], b_ref[...])
        if resid is not None:
            acc = refs[2][...] + acc
        o_ref[...] = acc.astype(o_ref.dtype)

    b_spec = (pl.BlockSpec((k, tn), lambda j, i: (0, j)) if mode == "nn"
              else pl.BlockSpec((tn, k), lambda j, i: (j, 0)))
    in_specs = [pl.BlockSpec((tm, k), lambda j, i: (i, 0)), b_spec]
    args = [a, b]
    if resid is not None:
        in_specs.append(pl.BlockSpec((tm, tn), lambda j, i: (i, j)))
        args.append(resid)
    if after is not None:
        in_specs.append(_ANY_SPEC)
        args.append(after)
    return pl.pallas_call(
        body, name=name, grid=(n // tn, m // tm), in_specs=in_specs,
        out_specs=pl.BlockSpec((tm, tn), lambda j, i: (i, j)),
        out_shape=jax.ShapeDtypeStruct((m, n), out_dtype),
        compiler_params=pltpu.CompilerParams(
            dimension_semantics=("parallel", "parallel"), vmem_limit_bytes=VMEM_LIMIT),
    )(*args)


def _mm_tn2(name, a1, a2, b, out_dtype=F32):
    m, k = a1.shape
    n = b.shape[1]
    tk = _pick(k, 512, 128)

    def body(a1_ref, a2_ref, b_ref, o1_ref, o2_ref):
        bb = b_ref[...]
        o1_ref[...] = _tn(a1_ref[...], bb).astype(o1_ref.dtype)
        o2_ref[...] = _tn(a2_ref[...], bb).astype(o2_ref.dtype)

    a_spec = pl.BlockSpec((m, tk), lambda p: (0, p))
    o_spec = pl.BlockSpec((tk, n), lambda p: (p, 0))
    return pl.pallas_call(
        body, name=name, grid=(k // tk,),
        in_specs=[a_spec, a_spec, pl.BlockSpec((m, n), lambda p: (0, 0))],
        out_specs=[o_spec, o_spec], out_shape=[jax.ShapeDtypeStruct((k, n), out_dtype)] * 2,
        compiler_params=pltpu.CompilerParams(dimension_semantics=("parallel",), vmem_limit_bytes=VMEM_LIMIT),
    )(a1, a2, b)


def _mm_tn_pair(name, a1, b1, a2, b2):
    def body(a1_ref, b1_ref, a2_ref, b2_ref, o1_ref, o2_ref):
        o1_ref[...] = _tn(a1_ref[...], b1_ref[...])
        o2_ref[...] = _tn(a2_ref[...], b2_ref[...])

    return pl.pallas_call(
        body, name=name,
        out_shape=[jax.ShapeDtypeStruct((a1.shape[1], b1.shape[1]), F32),
                   jax.ShapeDtypeStruct((a2.shape[1], b2.shape[1]), F32)],
        compiler_params=pltpu.CompilerParams(vmem_limit_bytes=VMEM_LIMIT),
    )(a1, b1, a2, b2)


def _norm_mm(name, x, norm_w, b, mode="nt", x_cb=0, after=None):
    m = x.shape[0]
    k = norm_w.shape[1]
    n = b.shape[0] if mode == "nt" else b.shape[1]
    tn = _pick(n, 1408, 128)
    tm = _pick(m, 1152, 16)
    dotf = _nt if mode == "nt" else _nn
    extra = [] if after is None else [after]

    def body(x_ref, w_ref, b_ref, *rest):
        o_ref, u_ref = rest[-2:]

        @pl.when(pl.program_id(1) == 0)
        def _():
            u_ref[...] = _rms_fwd(x_ref[...], w_ref[...], k)[0].astype(u_ref.dtype)

        o_ref[...] = dotf(u_ref[...], b_ref[...])

    b_spec = (pl.BlockSpec((tn, k), lambda i, j: (j, 0)) if mode == "nt"
              else pl.BlockSpec((k, tn), lambda i, j: (0, j)))
    return pl.pallas_call(
        body, name=name, grid=(m // tm, n // tn),
        in_specs=[pl.BlockSpec((tm, k), lambda i, j: (i, x_cb)), pl.BlockSpec((1, k), lambda i, j: (0, 0)),
                  b_spec] + [_ANY_SPEC] * len(extra),
        out_specs=[pl.BlockSpec((tm, tn), lambda i, j: (i, j)), pl.BlockSpec((tm, k), lambda i, j: (i, 0))],
        out_shape=[jax.ShapeDtypeStruct((m, n), F32), jax.ShapeDtypeStruct((m, k), _MXU)],
        compiler_params=pltpu.CompilerParams(
            dimension_semantics=("arbitrary", "arbitrary"), vmem_limit_bytes=VMEM_LIMIT),
    )(x, norm_w, b, *extra)


def _pro_mm(name, fn, tiled, full, k, b, resid):
    m = resid.shape[0]
    n = b.shape[1]
    tm = _pick(m, 576, 16)
    n_in = len(tiled) + len(full)

    def body(*refs):
        i = pl.program_id(0)
        u = fn(i, *[r[...] for r in refs[:n_in]]).astype(_MXU)
        b_ref, r_ref, o_ref, u_ref = refs[n_in:]
        u_ref[...] = u
        o_ref[...] = r_ref[...] + _nn(u, b_ref[...])

    row = lambda w: pl.BlockSpec((tm, w), lambda i: (i, 0))
    in_specs = [_tile_spec(t, tm, m) for t in tiled]
    in_specs += [_whole_spec(x) for x in full] + [_whole_spec(b), row(n)]
    return pl.pallas_call(
        body, name=name, grid=(m // tm,), in_specs=in_specs, out_specs=[row(n), row(k)],
        out_shape=[jax.ShapeDtypeStruct((m, n), F32), jax.ShapeDtypeStruct((m, k), _MXU)],
        compiler_params=pltpu.CompilerParams(dimension_semantics=("parallel",), vmem_limit_bytes=VMEM_LIMIT),
    )(*[t.arr for t in tiled], *full, b, resid)


def _ffn_in(h2, norm_w, w_gate_t, w_up_t, conv_w8, conv_b):
    m, k = h2.shape
    n = w_gate_t.shape[0]
    tm = _pick(m, 576, 16)
    tn = _pick(n, 1408, 128)

    def body(x_ref, xp_ref, nw_ref, wg_ref, wu_ref, cw_ref, cb_ref, hn_ref, gp_ref, up_ref, act_ref):
        i = pl.program_id(0)
        nw = nw_ref[...]

        @pl.when(pl.program_id(1) == 0)
        def _():
            hn_ref[...] = _rms_fwd(x_ref[...], nw, k)[0].astype(hn_ref.dtype)

        hn = hn_ref[...]
        hn_prev = _rms_fwd(xp_ref[...], nw, k)[0].astype(_MXU)
        wg = wg_ref[...]
        gp = _nt(hn, wg)
        gp_prev = jnp.where(i > 0, _nt(hn_prev, wg), 0.0)
        up = _nt(hn, wu_ref[...])
        gate = _conv_fwd(gp, gp_prev, cw_ref[...], FFN_CONV) + cb_ref[...]
        gp_ref[...] = gp
        up_ref[...] = up
        act_ref[...] = (_silu_parts(gate)[0] * up).astype(act_ref.dtype)

    r8 = tm // 8
    tile = pl.BlockSpec((tm, tn), lambda i, j: (i, j))
    wblk = pl.BlockSpec((tn, k), lambda i, j: (j, 0))
    return pl.pallas_call(
        body, name="ffn_in", grid=(m // tm, n // tn),
        in_specs=[pl.BlockSpec((tm, k), lambda i, j: (i, 0)),
                  pl.BlockSpec((8, k), lambda i, j: (jnp.maximum(i * r8 - 1, 0), 0)),
                  pl.BlockSpec((1, k), lambda i, j: (0, 0)), wblk, wblk,
                  pl.BlockSpec((8, tn), lambda i, j: (0, j)), pl.BlockSpec((1, tn), lambda i, j: (0, j))],
        out_specs=[pl.BlockSpec((tm, k), lambda i, j: (i, 0)), tile, tile, tile],
        out_shape=[jax.ShapeDtypeStruct((m, k), _MXU), jax.ShapeDtypeStruct((m, n), F32),
                   jax.ShapeDtypeStruct((m, n), F32), jax.ShapeDtypeStruct((m, n), _MXU)],
        compiler_params=pltpu.CompilerParams(
            dimension_semantics=("arbitrary", "arbitrary"), vmem_limit_bytes=VMEM_LIMIT),
    )(h2, h2, norm_w, w_gate_t, w_up_t, conv_w8, conv_b)


def _mm_rows(name, a, b, mode, fn, tiled, full, outs, accs=(), tm_cap=576):
    a_list = list(a) if isinstance(a, (list, tuple)) else [a]
    b_list = list(b) if isinstance(b, (list, tuple)) else [b]
    na = len(a_list)
    m = a_list[0].shape[0]
    tm = _pick(m, tm_cap, 16)
    dotf = _nn if mode == "nn" else _nt
    n_in = len(tiled) + len(full)
    n_out = len(outs)
    first = 2 * na

    def body(*refs):
        i = pl.program_id(0)
        vals = [r[...] for r in refs[first:first + n_in]]
        acc = dotf(refs[0][...], refs[na][...])
        for p in range(1, na):
            acc = acc + dotf(refs[p][...], refs[na + p][...])
        o_t, o_a = fn(i, acc, *vals)
        for r, v in zip(refs[first + n_in:first + n_in + n_out], o_t):
            r[...] = v.astype(r.dtype)
        for r, v in zip(refs[first + n_in + n_out:], o_a):
            @pl.when(i == 0)
            def _():
                r[...] = v

            @pl.when(i > 0)
            def _():
                r[...] += v

    whole = lambda x: pl.BlockSpec(x.shape, lambda i, nd=x.ndim: (0,) * nd)
    in_specs = [pl.BlockSpec((tm, x.shape[1]), lambda i: (i, 0)) for x in a_list] + [_whole_spec(x) for x in b_list]
    in_specs += [_tile_spec(t, tm, m) for t in tiled]
    in_specs += [whole(x) for x in full]
    out_specs = [pl.BlockSpec((tm, w), lambda i: (i, 0)) for w, _ in outs]
    out_specs += [pl.BlockSpec((r, w), lambda i: (0, 0)) for r, w in accs]
    out_shape = [jax.ShapeDtypeStruct((m, w), dt) for w, dt in outs]
    out_shape += [jax.ShapeDtypeStruct((r, w), F32) for r, w in accs]
    return pl.pallas_call(
        body, name=name, grid=(m // tm,), in_specs=in_specs, out_specs=out_specs, out_shape=out_shape,
        compiler_params=pltpu.CompilerParams(dimension_semantics=("arbitrary",), vmem_limit_bytes=VMEM_LIMIT),
    )(*a_list, *b_list, *[t.arr for t in tiled], *full)


ATTN_Q_TILES = 4


def _attn_probs(q, k, row0):
    tq, tp = q.shape[0], k.shape[0]
    s = _nt(q, k) * (1.0 / math.sqrt(QK_HEAD))
    row = row0 + lax.broadcasted_iota(jnp.int32, (tq, tp), 0)
    col = lax.broadcasted_iota(jnp.int32, (tq, tp), 1)
    ok = (col <= row) & (col >= PAD)
    s = jnp.where(ok, s, NEG)
    m = jnp.max(s, axis=-1, keepdims=True)
    e = jnp.exp(s - m)
    return e * (1.0 / jnp.sum(e, axis=-1, keepdims=True))


def _attn_fwd(q, k, v):
    tp = q.shape[0]
    tq = tp // ATTN_Q_TILES

    def body(q_ref, k_ref, v_ref, o_ref):
        for i in range(ATTN_Q_TILES):
            rows = slice(i * tq, (i + 1) * tq)
            keys = slice(0, (i + 1) * tq)
            p = _attn_probs(q_ref[rows, :], k_ref[keys, :], i * tq)
            o_ref[rows, :] = _nn(p, v_ref[keys, :])

    return pl.pallas_call(
        body, name="attn_fwd", grid=(MLA_HEADS,),
        in_specs=[pl.BlockSpec((tp, HP), lambda h: (0, h)),
                  pl.BlockSpec((tp, HP), lambda h: (0, h)),
                  pl.BlockSpec((tp, V_HEAD), lambda h: (0, h))],
        out_specs=pl.BlockSpec((tp, V_HEAD), lambda h: (0, h)),
        out_shape=jax.ShapeDtypeStruct((tp, MLA_HEADS * V_HEAD), F32),
        compiler_params=pltpu.CompilerParams(dimension_semantics=("parallel",), vmem_limit_bytes=VMEM_LIMIT),
    )(q, k, v)


def _attn_bwd(q, k, v, do):
    tp = q.shape[0]
    tq = tp // ATTN_Q_TILES

    def body(q_ref, k_ref, v_ref, do_ref, dq_ref, dk_ref, dv_ref):
        for i in reversed(range(ATTN_Q_TILES)):
            rows = slice(i * tq, (i + 1) * tq)
            keys = slice(0, (i + 1) * tq)
            qb = q_ref[rows, :]
            kk = k_ref[keys, :]
            dob = do_ref[rows, :]
            p = _attn_probs(qb, kk, i * tq)
            dp = _nt(dob, v_ref[keys, :])
            delta = jnp.sum(p * dp, axis=-1, keepdims=True)
            ds = p * (dp - delta) * (1.0 / math.sqrt(QK_HEAD))
            dq_ref[rows, :] = _nn(ds, kk)
            if i == ATTN_Q_TILES - 1:
                dk_ref[...] = _tn(ds, qb)
                dv_ref[...] = _tn(p, dob)
            else:
                dk_ref[keys, :] += _tn(ds, qb)
                dv_ref[keys, :] += _tn(p, dob)

    full = lambda w: pl.BlockSpec((tp, w), lambda h: (0, h))
    return pl.pallas_call(
        body, name="attn_bwd", grid=(MLA_HEADS,),
        in_specs=[full(HP), full(HP), full(V_HEAD), full(V_HEAD)],
        out_specs=[full(HP), full(HP), full(V_HEAD)],
        out_shape=[jax.ShapeDtypeStruct((tp, MLA_HEADS * HP), F32),
                   jax.ShapeDtypeStruct((tp, MLA_HEADS * HP), F32),
                   jax.ShapeDtypeStruct((tp, MLA_HEADS * V_HEAD), F32)],
        compiler_params=pltpu.CompilerParams(dimension_semantics=("parallel",), vmem_limit_bytes=VMEM_LIMIT),
    )(q, k, v, do)


def _gdn_consts():
    c = DN_CHUNK
    r = lax.broadcasted_iota(jnp.int32, (c, c), 0)
    cc = lax.broadcasted_iota(jnp.int32, (c, c), 1)
    incl = r >= cc
    strict = r > cc
    return incl, strict


def _cumsum_rows(x, reverse=False):
    c = x.shape[0]
    row = lax.broadcasted_iota(jnp.int32, x.shape, 0)
    s = 1
    while s < c:
        if reverse:
            x = x + jnp.where(row < c - s, pltpu.roll(x, c - s, 0), 0.0)
        else:
            x = x + jnp.where(row >= s, pltpu.roll(x, s, 0), 0.0)
        s *= 2
    return x


def _each(fn, *lists):
    return [fn(*a) for a in zip(*lists)]


def _interleave(chains):
    chains = list(chains)
    while chains:
        for ch in list(chains):
            try:
                next(ch)
            except StopIteration:
                chains.remove(ch)


def _gdn_chunk_common(q_ref, k_ref, v_ref, g_ref, b_ref):
    c = DN_CHUNK
    incl, strict = _gdn_consts()
    sls = [(slice(c * sub, c * (sub + 1)), slice(DN_DIM * h, DN_DIM * (h + 1)))
           for sub in range(GDN_SUB_CHUNKS) for h in range(DN_HEADS)]
    q = [q_ref[sl] * (1.0 / math.sqrt(DN_DIM)) for sl in sls]
    k = [k_ref[sl] for sl in sls]
    v = [v_ref[sl] for sl in sls]
    g = [g_ref[sl] for sl in sls]
    beta = [b_ref[sl] for sl in sls]
    gc = [_cumsum_rows(x) for x in g]
    grow = [x.T[:c, :] for x in gc]
    kb = _each(jnp.multiply, k, beta)
    kk = _each(_nt, kb, k)
    qk = _each(_nt, q, k)
    gam = [jnp.exp(x) for x in gc]
    g_last = [_rowsum(x) for x in g]
    dm = [jnp.exp(jnp.where(incl, x[:, :c] - y, NEG)) for x, y in zip(gc, grow)]
    vb = _each(jnp.multiply, v, beta)
    kbg = _each(jnp.multiply, kb, gam)
    ek = [jnp.exp(x - y) for x, y in zip(g_last, gc)]
    kd = _each(jnp.multiply, k, ek)
    return dict(q=q, k=k, v=v, beta=beta, gc=gc, gam=gam, g_last=g_last, dm=dm, kb=kb, vb=vb,
                kbg=kbg, kk=kk, ek=ek, kd=kd, qk=qk, incl=incl, strict=strict, sls=sls)


def _gdn_fwd(proj, conv_w8, alog, dtb):
    tp = proj.shape[0]
    c = DN_CHUNK
    nch = tp // c
    blk = GDN_SUB_CHUNKS * c

    def body(x_ref, xp_ref, ab_ref, w8_ref, alog_ref, dtb_ref,
             o_ref, s_ref, t_ref, q_ref, k_ref, v_ref, g_ref, b_ref, s_scr):
        @pl.when(pl.program_id(0) == 0)
        def _():
            s_scr[...] = jnp.zeros_like(s_scr)

        staged, _ = _f_gdn_prep(pl.program_id(0), x_ref[...], xp_ref[...], ab_ref[...], w8_ref[...],
                                alog_ref[...], dtb_ref[...])
        for ref, val in zip((q_ref, k_ref, v_ref, g_ref, b_ref), staged):
            ref[...] = val
        eye = (lax.broadcasted_iota(jnp.int32, (c, c), 0) == lax.broadcasted_iota(jnp.int32, (c, c), 1)).astype(F32)
        x = _gdn_chunk_common(q_ref, k_ref, v_ref, g_ref, b_ref)
        heads = range(DN_HEADS)
        bp = [-jnp.where(x["strict"], kk * dm, 0.0) for kk, dm in zip(x["kk"], x["dm"])]
        t = [eye + b for b in bp]
        for _ in range(5):
            bp = [_nn(b, b, hp="3x") for b in bp]
            t = [tt + _nn(tt, b, hp="3x") for tt, b in zip(t, bp)]
        u = _each(_nn, t, x["vb"])
        w = _each(_nn, t, x["kbg"])
        qg = _each(jnp.multiply, x["q"], x["gam"])
        mqk = _each(jnp.multiply, x["qk"], x["dm"])
        s = [s_scr[h] for h in heads]
        for sub in range(GDN_SUB_CHUNKS):
            e = [DN_HEADS * sub + h for h in heads]
            v_new = [u[i] - _nn(w[i], s[h]) for h, i in zip(heads, e)]
            o = [_nn(qg[i], s[h]) + _nn(mqk[i], v_new[h]) for h, i in zip(heads, e)]
            s_new = [s[h] * jnp.exp(x["g_last"][i]) + _tn(x["kd"][i], v_new[h]) for h, i in zip(heads, e)]
            for h, i in zip(heads, e):
                s_ref[h, sub] = s[h]
                t_ref[h, sub] = t[i]
                o_ref[x["sls"][i]] = o[h]
            s = s_new
        for h in heads:
            s_scr[h] = s[h]

    sub = GDN_SUB_CHUNKS
    rb = lambda n: (n, 0)
    rows = pl.BlockSpec((blk, DN_WIDTH), rb)
    whole = lambda a: pl.BlockSpec(a.shape, lambda n: (0, 0))
    return pl.pallas_call(
        body, name="gdn_fwd", grid=(nch // sub,),
        in_specs=[pl.BlockSpec((blk, 3 * DN_WIDTH), rb),
                  pl.BlockSpec((8, 3 * DN_WIDTH), lambda n: (jnp.maximum(n * (blk // 8) - 1, 0), 0)),
                  pl.BlockSpec((blk, LANE), lambda n: (n, C_AB // LANE)),
                  whole(conv_w8), whole(alog), whole(dtb)],
        out_specs=[rows,
                   pl.BlockSpec((DN_HEADS, sub, DN_DIM, DN_DIM), lambda n: (0, n, 0, 0)),
                   pl.BlockSpec((DN_HEADS, sub, c, c), lambda n: (0, n, 0, 0))] + [rows] * 5,
        out_shape=[jax.ShapeDtypeStruct((tp, DN_WIDTH), F32),
                   jax.ShapeDtypeStruct((DN_HEADS, nch, DN_DIM, DN_DIM), F32),
                   jax.ShapeDtypeStruct((DN_HEADS, nch, c, c), F32)] + [jax.ShapeDtypeStruct((tp, DN_WIDTH), F32)] * 5,
        scratch_shapes=[pltpu.VMEM((DN_HEADS, DN_DIM, DN_DIM), F32)],
        compiler_params=pltpu.CompilerParams(dimension_semantics=("arbitrary",), vmem_limit_bytes=VMEM_LIMIT),
    )(proj, proj, proj, conv_w8, alog, dtb)


def _gdn_bwd(q, k, v, g, beta, s_all, t_all, do, proj, conv_w8, alog, dtb, after):
    tp = q.shape[0]
    c = DN_CHUNK
    nch = tp // c
    nblk = nch // GDN_SUB_CHUNKS
    blk = GDN_SUB_CHUNKS * c

    def body(q_ref, k_ref, v_ref, g_ref, b_ref, s_ref, t_ref, do_ref, x_ref, xp_ref, xn_ref, ab_ref,
             w8_ref, alog_ref, dtb_ref, _after_ref, dqkv_ref, dab_ref, dcw_ref, dalog_ref, ddtb_ref,
             ds_scr, dq_ref, dk_ref, dv_ref, dg_ref, db_ref, nxt_scr):
        step = pl.program_id(0)

        @pl.when(step == 0)
        def _():
            ds_scr[...] = jnp.zeros_like(ds_scr)
            nxt_scr[...] = jnp.zeros_like(nxt_scr)

        xs = _gdn_chunk_common(q_ref, k_ref, v_ref, g_ref, b_ref)

        ds_state = [ds_scr[h] for h in range(DN_HEADS)]

        def chain(sub, h):
            e = DN_HEADS * sub + h
            x = {key: (val[e] if isinstance(val, list) else val) for key, val in xs.items()}
            sl = x["sls"]
            qs, kx, vx, beta_, gam, dm = x["q"], x["k"], x["v"], x["beta"], x["gam"], x["dm"]
            kb, vb, kbg, kd, ek = x["kb"], x["vb"], x["kbg"], x["kd"], x["ek"]
            t = t_ref[h, sub]
            s = s_ref[h, sub]
            dsn = ds_state[h]
            dob = do_ref[sl]
            eg_last = jnp.exp(x["g_last"])
            u = _nn(t, vb)
            w = _nn(t, kbg)
            mqk = x["qk"] * dm
            qd = qs * gam
            dqd = _nt(dob, s)
            dkd_pre = _nn(kd, dsn)
            yield
            v_new = u - _nn(w, s)
            dv_new = _tn(mqk, dob) + dkd_pre
            dq = dqd * gam
            dgam = jnp.sum(dqd * qs, axis=1, keepdims=True)
            yield
            ds_state[h] = _tn(qd, dob) + eg_last * dsn - _tn(w, dv_new)
            dmm = jnp.where(x["incl"], _nt(dob, v_new), 0.0)
            dkd = _nt(v_new, dsn)
            dw = -_nt(dv_new, s)
            dvb = _tn(t, dv_new)
            dt = _nt(dv_new, vb)
            yield
            dqk = dmm * dm
            e_mat = dmm * mqk
            dq = dq + _nn(dqk, kx)
            dk = _tn(dqk, qs) + dkd * ek
            e1 = jnp.sum(dkd * kd, axis=1, keepdims=True)
            dgc = -e1
            dg_last = jnp.sum(e1) + eg_last * jnp.sum(s * dsn)
            dt = dt + _nt(dw, kbg)
            dkbg = _tn(t, dw)
            yield
            tdt = _tn(t, dt, hp="3x")
            yield
            da = jnp.where(x["strict"], -_nt(tdt, t, hp="3x"), 0.0)
            yield
            dkk = da * dm
            e_mat = e_mat + da * x["kk"] * dm
            dkb = _nn(dkk, kx) + dkbg * gam
            dk = dk + _tn(dkk, kb)
            dgam = dgam + jnp.sum(dkbg * kb, axis=1, keepdims=True)
            yield
            dk = dk + dkb * beta_
            dbeta = jnp.sum(dkb * kx, axis=1, keepdims=True) + jnp.sum(dvb * vx, axis=1, keepdims=True)
            dv = dvb * beta_
            dgc = dgc + jnp.sum(e_mat, axis=1, keepdims=True) + dgam * gam
            dgc = dgc - jnp.sum(e_mat.T, axis=1, keepdims=True)
            yield
            dg = _cumsum_rows(dgc, reverse=True) + dg_last
            yield
            dq_ref[sl] = dq * (1.0 / math.sqrt(DN_DIM))
            dk_ref[sl] = dk
            dv_ref[sl] = dv
            dg_ref[sl] = dg
            db_ref[sl] = jnp.broadcast_to(dbeta, (c, LANE))

        chains = []
        for sub in reversed(range(GDN_SUB_CHUNKS)):
            new = [chain(sub, h) for h in range(DN_HEADS)]
            for _ in range(3):
                for ch in new:
                    next(ch)
            chains += new
        _interleave(chains)
        for h in range(DN_HEADS):
            ds_scr[h] = ds_state[h]

        dq, dk, dv = dq_ref[...], dk_ref[...], dv_ref[...]
        outs, accs = _f_gdn_prep_bwd(
            nblk - 1 - step, x_ref[...], xp_ref[...], xn_ref[...], ab_ref[...], dq, nxt_scr[0], dk, nxt_scr[1],
            dv, nxt_scr[2], dg_ref[...], db_ref[...], w8_ref[...], alog_ref[...], dtb_ref[...], nt=nblk)
        nxt_scr[0] = dq[:8]
        nxt_scr[1] = dk[:8]
        nxt_scr[2] = dv[:8]
        dqkv_ref[...] = outs[0].astype(dqkv_ref.dtype)
        dab_ref[...] = outs[1].astype(dab_ref.dtype)
        for ref, val in zip((dcw_ref, dalog_ref, ddtb_ref), accs):
            @pl.when(step == 0)
            def _():
                ref[...] = val

            @pl.when(step > 0)
            def _():
                ref[...] += val

    sub = GDN_SUB_CHUNKS
    r8 = blk // 8
    rb = lambda n: (nblk - 1 - n, 0)
    hs = lambda n: (0, nblk - 1 - n, 0, 0)
    rows = pl.BlockSpec((blk, DN_WIDTH), rb)
    whole = lambda a: pl.BlockSpec(a.shape, lambda n: (0,) * a.ndim)
    wide = 3 * DN_WIDTH
    return pl.pallas_call(
        body, name="gdn_bwd", grid=(nblk,),
        in_specs=[rows] * 5
        + [pl.BlockSpec((DN_HEADS, sub, DN_DIM, DN_DIM), hs), pl.BlockSpec((DN_HEADS, sub, c, c), hs), rows,
           pl.BlockSpec((blk, wide), rb),
           pl.BlockSpec((8, wide), lambda n: (jnp.maximum((nblk - 1 - n) * r8 - 1, 0), 0)),
           pl.BlockSpec((8, wide), lambda n: (jnp.minimum((nblk - n) * r8, tp // 8 - 1), 0)),
           pl.BlockSpec((blk, LANE), lambda n: (nblk - 1 - n, C_AB // LANE)),
           whole(conv_w8), whole(alog), whole(dtb), _ANY_SPEC],
        out_specs=[pl.BlockSpec((blk, wide), rb), pl.BlockSpec((blk, LANE), rb),
                   whole(conv_w8), whole(alog), whole(dtb)],
        out_shape=[jax.ShapeDtypeStruct((tp, wide), _MXU), jax.ShapeDtypeStruct((tp, LANE), _MXU),
                   jax.ShapeDtypeStruct(conv_w8.shape, F32), jax.ShapeDtypeStruct(alog.shape, F32),
                   jax.ShapeDtypeStruct(dtb.shape, F32)],
        scratch_shapes=[pltpu.VMEM((DN_HEADS, DN_DIM, DN_DIM), F32)] + [pltpu.VMEM((blk, DN_WIDTH), F32)] * 5
        + [pltpu.VMEM((3, 8, DN_WIDTH), F32)],
        compiler_params=pltpu.CompilerParams(dimension_semantics=("arbitrary",), vmem_limit_bytes=VMEM_LIMIT),
    )(q, k, v, g, beta, s_all, t_all, do, proj, proj, proj, proj, conv_w8, alog, dtb, after)


def _silu_parts(x):
    s = _sigmoid(x)
    return x * s, s * (1.0 + x * (1.0 - s))


def _f_rms_bwd_add(i, x, dy, dres, w, *, mask_pad):
    dx, dwr = _rms_bwd(x, w, dy, x.shape[1])
    out = dres + dx
    if mask_pad:
        out = jnp.where(_row_ids(i, x.shape[0]) >= PAD, out, 0.0)
    return (out,), (_rowsum(dwr),)


def _rope(x, cos, sin_s):
    return x * cos + _swap_halves(x) * sin_s


def _rope_t(dy, cos, sin_s):
    return dy * cos + _swap_halves(dy * sin_s)


def _f_mla_qk(i, qf, kvf, kpe, cos, sin_s, qw, kw):
    qs, ks, vs = [], [], []
    for h in range(MLA_HEADS):
        qn, _ = _rms_fwd(qf[:, HP * h:HP * (h + 1)], qw, QK_HEAD)
        qs += [qn[:, :QK_NOPE], _rope(qn[:, QK_NOPE:], cos, sin_s)]
        kh = jnp.concatenate([kvf[:, HP * h:HP * h + QK_NOPE], kpe], axis=1)
        kn, _ = _rms_fwd(kh, kw, QK_HEAD)
        ks += [kn[:, :QK_NOPE], _rope(kn[:, QK_NOPE:], cos, sin_s)]
        vs.append(kvf[:, HP * h + QK_NOPE:HP * (h + 1)])
    return (jnp.concatenate(qs, axis=1), jnp.concatenate(ks, axis=1), jnp.concatenate(vs, axis=1)), ()


def _f_mla_front(i, ql, kvl, kpe, cos, sin_s, qaw, kvaw, wq_t, wkv, qw, kw):
    qn = _rms_fwd(ql, qaw, Q_LORA)[0].astype(_MXU)
    kvn = _rms_fwd(kvl, kvaw, KV_LORA)[0].astype(_MXU)
    qf = _nt(qn, wq_t)
    kvf = _nn(kvn, wkv)
    (q, k, v), _ = _f_mla_qk(i, qf, kvf, kpe, cos, sin_s, qw, kw)
    return (qn, kvn, qf, kvf, q, k, v), ()


def _f_mla_back(i, qf, kvf, kpe, cos, sin_s, dq, dk, dv, ql, kvl, qaw, kvaw, wq_t, wkv, qw, kw):
    (dqf, dkvf, dkpe), (dqw, dkw) = _f_mla_qk_bwd(i, qf, kvf, kpe, cos, sin_s, dq, dk, dv, qw, kw)
    dqf = dqf.astype(_MXU)
    dkvf = dkvf.astype(_MXU)
    dql, dqaw = _rms_bwd(ql, qaw, _nn(dqf, wq_t), Q_LORA)
    dkvl, dkvaw = _rms_bwd(kvl, kvaw, _nt(dkvf, wkv), KV_LORA)
    return (dqf, dkvf, dkpe, dql, dkvl), (dqw, dkw, _rowsum(dqaw), _rowsum(dkvaw))


def _f_mla_qk_bwd(i, qf, kvf, kpe, cos, sin_s, dq, dk, dv, qw, kw):
    dqf, dkvf = [], []
    dkpe = None
    dqw = None
    dkw = None
    for h in range(MLA_HEADS):
        dqh = dq[:, HP * h:HP * (h + 1)]
        dqn = jnp.concatenate([dqh[:, :QK_NOPE], _rope_t(dqh[:, QK_NOPE:], cos, sin_s)], axis=1)
        dx, dwr = _rms_bwd(qf[:, HP * h:HP * (h + 1)], qw, dqn, QK_HEAD)
        dqf.append(dx)
        dqw = _rowsum(dwr) if dqw is None else dqw + _rowsum(dwr)
        dkh = dk[:, HP * h:HP * (h + 1)]
        dkn = jnp.concatenate([dkh[:, :QK_NOPE], _rope_t(dkh[:, QK_NOPE:], cos, sin_s)], axis=1)
        kh = jnp.concatenate([kvf[:, HP * h:HP * h + QK_NOPE], kpe], axis=1)
        dx, dwr = _rms_bwd(kh, kw, dkn, QK_HEAD)
        dkvf += [dx[:, :QK_NOPE], dv[:, V_HEAD * h:V_HEAD * (h + 1)]]
        dkpe = dx[:, QK_NOPE:] if dkpe is None else dkpe + dx[:, QK_NOPE:]
        dkw = _rowsum(dwr) if dkw is None else dkw + _rowsum(dwr)
    return (jnp.concatenate(dqf, axis=1), jnp.concatenate(dkvf, axis=1), dkpe), (dqw, dkw)


def _gdn_act(i, x, halo, w8):
    halo = jnp.where(i > 0, halo, 0.0)
    c = _conv_fwd(x, halo, w8, DN_CONV)
    act, dact = _silu_parts(c)
    return act, dact


def _spread_heads(ab):
    tm = ab.shape[0]
    return jnp.concatenate([jnp.broadcast_to(ab[:, h:h + 1], (tm, DN_DIM)) for h in range(2 * DN_HEADS)], axis=1)


def _gather_heads(x):
    tm = x.shape[0]
    lane = lax.broadcasted_iota(jnp.int32, (tm, LANE), 1)
    out = jnp.zeros((tm, LANE), F32)
    for h in range(2 * DN_HEADS):
        out = out + jnp.where(lane == h, x[:, DN_DIM * h:DN_DIM * h + 1], 0.0)
    return out


def _gate_parts(ab, dtb):
    lane1 = lax.broadcasted_iota(jnp.int32, (1, LANE), 1)
    dtb_c = jnp.zeros((1, LANE), F32)
    for h in range(DN_HEADS):
        dtb_c = dtb_c + jnp.where(lane1 == h, dtb[:, DN_DIM * h:DN_DIM * h + 1], 0.0)
    pre = ab + dtb_c
    sig = _sigmoid(pre)
    lane = lax.broadcasted_iota(jnp.int32, ab.shape, 1)
    return jnp.where(lane < DN_HEADS, _softplus(pre), sig), sig


def _f_gdn_prep(i, x, halo, ab, w8, alog, dtb):
    tm = x.shape[0]
    act, _ = _gdn_act(i, x, halo, w8)
    outs = []
    for part in range(2):
        for h in range(DN_HEADS):
            t = act[:, DN_WIDTH * part + DN_DIM * h:DN_WIDTH * part + DN_DIM * (h + 1)]
            outs.append(t * lax.rsqrt(jnp.sum(t * t, axis=-1, keepdims=True) + EPS))
    q = jnp.concatenate(outs[:DN_HEADS], axis=1)
    k = jnp.concatenate(outs[DN_HEADS:], axis=1)
    v = act[:, 2 * DN_WIDTH:]
    abb = _spread_heads(ab)
    valid = _row_ids(i, tm) >= PAD
    g = jnp.where(valid, -jnp.exp(alog) * _softplus(abb[:, :DN_WIDTH] + dtb), 0.0)
    beta = jnp.where(valid, _sigmoid(abb[:, DN_WIDTH:]), 0.0)
    return (q, k, v, g, beta), ()


def _f_gdn_prep_bwd(i, x, x_prev, x_next, ab, dq, dq_next, dk, dk_next, dv, dv_next, dg, dbeta,
                    w8, alog, dtb, *, nt):
    tm = x.shape[0]
    x_prev = jnp.where(i > 0, x_prev, 0.0)
    more = i < nt - 1
    ext = lambda t, t_next: jnp.concatenate([t, jnp.where(more, t_next, 0.0)], axis=0)
    taps = _conv_taps(jnp.concatenate([x, x_next], axis=0), x_prev, DN_CONV)
    c = _conv_from_taps(taps, w8)
    act, dact = _silu_parts(c)
    douts = []
    for part, dd in enumerate((ext(dq, dq_next), ext(dk, dk_next))):
        for h in range(DN_HEADS):
            t = act[:, DN_WIDTH * part + DN_DIM * h:DN_WIDTH * part + DN_DIM * (h + 1)]
            r = lax.rsqrt(jnp.sum(t * t, axis=-1, keepdims=True) + EPS)
            y = t * r
            dy = dd[:, DN_DIM * h:DN_DIM * (h + 1)]
            douts.append(r * (dy - y * jnp.sum(dy * y, axis=-1, keepdims=True)))
    douts.append(ext(dv, dv_next))
    dc = jnp.concatenate(douts, axis=1) * dact
    dqkv = _conv_bwd_x(dc[:tm], dc[tm:], w8, DN_CONV)
    dconv_w = _conv_bwd_w_taps(dc[:tm], taps)
    sp_beta, sig = _gate_parts(ab, dtb)
    spread = _spread_heads(sp_beta)
    valid = _row_ids(i, tm) >= PAD
    ea = jnp.exp(alog)
    g = -ea * spread[:, :DN_WIDTH]
    dg = jnp.where(valid, dg, 0.0)
    dbeta = jnp.where(valid, dbeta, 0.0)
    da = dg * (-ea) * _spread_heads(sig)[:, :DN_WIDTH]
    beta = spread[:, DN_WIDTH:]
    db = dbeta * beta * (1.0 - beta)
    dab = _gather_heads(jnp.concatenate([da, db], axis=1))
    return (dqkv, dab), (dconv_w, _rowsum(dg * g), _rowsum(da))


def _f_mix(i, o_mla, o_dn, z, w_mla, w_dn):
    tm = o_mla.shape[0]
    valid = _row_ids(i, tm) >= PAD
    outs = []
    for h in range(MLA_HEADS):
        y, _ = _rms_fwd(o_mla[:, V_HEAD * h:V_HEAD * (h + 1)], w_mla, V_HEAD)
        outs.append(jnp.where(valid, y, 0.0))
    for h in range(DN_HEADS):
        y, _ = _rms_fwd(o_dn[:, DN_DIM * h:DN_DIM * (h + 1)], w_dn, DN_DIM)
        outs.append(y * _silu_parts(z[:, DN_DIM * h:DN_DIM * (h + 1)])[0])
    return (jnp.concatenate(outs, axis=1),), ()


def _f_mix_bwd(i, o_mla, o_dn, z, dy_mla, dy_dn, w_mla, w_dn):
    tm = o_mla.shape[0]
    valid = _row_ids(i, tm) >= PAD
    d_mla, d_dn, d_z = [], [], []
    dw_mla = None
    dw_dn = None
    for h in range(MLA_HEADS):
        sl = slice(V_HEAD * h, V_HEAD * (h + 1))
        dx, dwr = _rms_bwd(o_mla[:, sl], w_mla, jnp.where(valid, dy_mla[:, sl], 0.0), V_HEAD)
        d_mla.append(dx)
        dw_mla = _rowsum(dwr) if dw_mla is None else dw_mla + _rowsum(dwr)
    for h in range(DN_HEADS):
        sl = slice(DN_DIM * h, DN_DIM * (h + 1))
        y, _ = _rms_fwd(o_dn[:, sl], w_dn, DN_DIM)
        sz, dsz = _silu_parts(z[:, sl])
        d_z.append(dy_dn[:, sl] * y * dsz)
        dx, dwr = _rms_bwd(o_dn[:, sl], w_dn, dy_dn[:, sl] * sz, DN_DIM)
        d_dn.append(dx)
        dw_dn = _rowsum(dwr) if dw_dn is None else dw_dn + _rowsum(dwr)
    return ((jnp.concatenate(d_mla, axis=1), jnp.concatenate(d_dn, axis=1), jnp.concatenate(d_z, axis=1)),
            (dw_mla, dw_dn))


def _f_ffn_act_bwd(i, gp, gp_prev, gp_next, up, up_next, dact, dact_next, w8, b, *, nt):
    tm = gp.shape[0]
    gp_prev = jnp.where(i > 0, gp_prev, 0.0)
    dact_next = jnp.where(i < nt - 1, dact_next, 0.0)
    cat = lambda t, t_next: jnp.concatenate([t, t_next], axis=0)
    taps = _conv_taps(cat(gp, gp_next), gp_prev, FFN_CONV)
    gate = _conv_from_taps(taps, w8) + b
    sg, dsg = _silu_parts(gate)
    dact_e = cat(dact, dact_next)
    dgate = dact_e * cat(up, up_next) * dsg
    dgate_pre = _conv_bwd_x(dgate[:tm], dgate[tm:], w8, FFN_CONV)
    dup = dact * sg[:tm]
    return (dgate_pre, dup), (_conv_bwd_w_taps(dgate[:tm], taps), _rowsum(dgate[:tm]))


def _f_loss(i, h3, tgt):
    tm = h3.shape[0]
    diff = jnp.where(_row_ids(i, tm) >= ROW0, h3 - tgt, 0.0)
    part = 0.5 * jnp.sum(diff * diff) * (1.0 / D_MODEL)
    return (diff * (1.0 / D_MODEL),), (jnp.full((1, LANE), part, F32),)


def _local_step(h0, tgt, w, token, late_weights, grads_ready):
    tp = h0.shape[0]
    proj, u = _norm_mm("in_proj", h0, w["attn_norm_w"], w["w_in"], after=token)
    p_qkv = lambda kind="cur": _In(proj, 3 * DN_WIDTH, 0, kind)
    p_z = _In(proj, DN_WIDTH, C_Z // DN_WIDTH)
    p_ql = _In(proj, Q_LORA, C_QL // Q_LORA)
    p_kvl = _In(proj, KV_LORA, C_KVL // KV_LORA)
    p_kpe = _In(proj, LANE, C_KPE // LANE)
    p_ab = _In(proj, LANE, C_AB // LANE)
    cos, sin_s = _In(w["cos"]), _In(w["sin_s"])

    mla_w = [w["q_a_norm_w"], w["kv_a_norm_w"], w["w_q_b"], w["w_kv_b"], w["q_norm_w"], w["k_norm_w"]]
    tm_mla = _pick(tp, 288, 16)
    wide = MLA_HEADS * HP
    qn, kvn, qf, kvf, q, k, v = _rows(
        "mla_front", _f_mla_front, [p_ql, p_kvl, p_kpe, cos, sin_s], mla_w,
        [(Q_LORA, _MXU), (KV_LORA, _MXU), (wide, F32), (wide, F32), (wide, _MXU), (wide, _MXU),
         (MLA_HEADS * V_HEAD, _MXU)], tm=tm_mla)
    o_mla = _attn_fwd(q, k, v)

    dn_w = [w["dn_conv_w"], w["alog_b"], w["dtb_b"]]
    o_dn, s_all, t_all, gq, gk, gv, gg, gb = _gdn_fwd(proj, *dn_w)

    out_w = [w["mla_out_norm_w"], w["dn_out_norm_w"]]
    w = dict(w, **late_weights((o_mla, o_dn), _LATE[:3]))
    h2, mixed = _pro_mm("mix_out_proj", lambda i, *t: _f_mix(i, *t)[0][0], [_In(o_mla), _In(o_dn), p_z], out_w,
                        D_MODEL, w["w_out"], h0)

    ffn_w = [w["ffn_conv_w"], w["ffn_conv_b"]]
    hn, gate_pre, up, act = _ffn_in(h2, w["ffn_norm_w"], w["w_gate"], w["w_up"], *ffn_w)
    w = dict(w, **late_weights(act, _LATE[3:]))
    twice = lambda fn: (lambda *a: (lambda o, s: (o + o, s))(*fn(*a)))
    dh3, dh3_mx, loss = _mm_rows(
        "ffn_down_loss", act, w["w_down"], "nn", twice(lambda i, y, r, t: _f_loss(i, r + y, t)),
        [_In(h2), _In(tgt)], [], [(D_MODEL, F32), (D_MODEL, _MXU)], [(1, LANE)])

    g = {}
    dact = _mm("ffn_down_dx", dh3_mx, w["w_down"], "nt")
    g["w_down"] = _mm("ffn_down_dw", act, dh3_mx, "tn", out_dtype=_MXU)
    dgate_pre, dup, g["ffn_conv_w"], g["ffn_conv_b"] = _rows(
        "ffn_act_bwd", functools.partial(_f_ffn_act_bwd, nt=tp // tm_mla),
        [_In(gate_pre), _In(gate_pre, kind="prev"), _In(gate_pre, kind="next"), _In(up), _In(up, kind="next"),
         _In(dact), _In(dact, kind="next")], ffn_w,
        [(D_FF, _MXU), (D_FF, _MXU)], [(8, D_FF), (1, D_FF)], tm=tm_mla)
    g["w_gate"], g["w_up"] = _mm_tn2("ffn_gate_up_dw", dgate_pre, dup, hn, out_dtype=_MXU)
    tok = grads_ready(g, ("w_down", "w_gate", "w_up"))
    dh2, dh2_mx, g["ffn_norm_w"] = _mm_rows(
        "ffn_gate_up_dx_rms", [dgate_pre, dup], [w["w_gate"], w["w_up"]], "nn",
        twice(lambda i, dy, x, dres, nw, _tok: _f_rms_bwd_add(i, x, dy, dres, nw, mask_pad=True)),
        [_In(h2), _In(dh3)], [w["ffn_norm_w"], tok], [(D_MODEL, F32), (D_MODEL, _MXU)], [(1, D_MODEL)], tm_cap=288)

    g["w_out"] = _mm("out_proj_dw", mixed, dh2_mx, "tn", out_dtype=_MXU)
    half = MLA_HEADS * V_HEAD
    do_mla, do_dn, dz, g["mla_out_norm_w"], g["dn_out_norm_w"] = _mm_rows(
        "out_proj_dx_mix", dh2_mx, w["w_out"], "nt",
        lambda i, dm, om, od, z, wm, wd: _f_mix_bwd(i, om, od, z, dm[:, :half], dm[:, half:], wm, wd),
        [_In(o_mla), _In(o_dn), p_z], out_w,
        [(half, F32), (DN_WIDTH, F32), (DN_WIDTH, _MXU)], [(1, V_HEAD), (1, DN_DIM)])

    dq, dk, dv = _attn_bwd(q, k, v, do_mla)
    dqf, dkvf, dkpe, dql, dkvl, g["q_norm_w"], g["k_norm_w"], g["q_a_norm_w"], g["kv_a_norm_w"] = _rows(
        "mla_back", _f_mla_back,
        [_In(qf), _In(kvf), p_kpe, cos, sin_s, _In(dq), _In(dk), _In(dv), p_ql, p_kvl], mla_w,
        [(wide, _MXU), (wide, _MXU), (LANE, _MXU), (Q_LORA, _MXU), (KV_LORA, _MXU)],
        [(1, HP), (1, HP), (1, Q_LORA), (1, KV_LORA)], tm=tm_mla)
    g["w_q_b"], g["w_kv_b"] = _mm_tn_pair("mla_b_dw", dqf, qn, kvn, dkvf)
    tok = grads_ready(g, ("w_out", "w_q_b", "w_kv_b"))

    dqkv, dab, g["dn_conv_w"], g["alog_b"], g["dtb_b"] = _gdn_bwd(
        gq, gk, gv, gg, gb, s_all, t_all, do_dn, proj, *dn_w, tok)

    dproj = jnp.concatenate([dqkv, dz, dql, dkvl, dkpe, dab], axis=1)
    g["w_in"] = _mm("in_proj_dw", dproj, u, "tn", out_dtype=_MXU)
    tok = grads_ready(g, ("w_in",))
    dh0, g["attn_norm_w"] = _mm_rows(
        "in_proj_dx_rms", dproj, w["w_in"], "nn",
        lambda i, du, x, dres, nw, _tok: _f_rms_bwd_add(i, x, du, dres, nw, mask_pad=False),
        [_In(h0), _In(dh2)], [w["attn_norm_w"], tok], [(D_MODEL, F32)], [(1, D_MODEL)])
    return loss, dh0, g


def _w_in_to_padded(w):
    c1, c2, c3 = Q_LORA, Q_LORA + KV_LORA, Q_LORA + KV_LORA + QK_ROPE
    c4 = c3 + 3 * DN_WIDTH
    c5 = c4 + DN_WIDTH
    z = lambda n: jnp.zeros((n, w.shape[1]), w.dtype)
    return jnp.concatenate([w[c3:c4], w[c4:c5], w[:c1], w[c1:c2], w[c2:c3], z(LANE - QK_ROPE),
                            w[c5:], z(LANE - 2 * DN_HEADS)], axis=0)


def _w_in_from_padded(g):
    return jnp.concatenate([g[C_QL:C_QL + Q_LORA], g[C_KVL:C_KVL + KV_LORA], g[C_KPE:C_KPE + QK_ROPE],
                            g[:C_Z + DN_WIDTH], g[C_AB:C_AB + 2 * DN_HEADS]], axis=0)


def _w_q_b_to_padded(w):
    r = w.shape[1]
    w = w.reshape(MLA_HEADS, QK_HEAD, r)
    return jnp.pad(w, ((0, 0), (0, HP - QK_HEAD), (0, 0))).reshape(MLA_HEADS * HP, r)


def _w_q_b_from_padded(g):
    r = g.shape[1]
    return g.reshape(MLA_HEADS, HP, r)[:, :QK_HEAD].reshape(MLA_HEADS * QK_HEAD, r)


def _pad_rows8(w):
    return jnp.pad(w, ((0, 8 - w.shape[0]), (0, 0)))


def _prepare(full, tp):
    w = {}
    mx = lambda a: a.astype(_MXU)
    w["attn_norm_w"] = full["attn_norm_w"]
    w["w_in"] = mx(_w_in_to_padded(full["w_in"]))
    w["q_a_norm_w"] = full["q_a_norm_w"]
    w["kv_a_norm_w"] = full["kv_a_norm_w"]
    w["w_q_b"] = mx(_w_q_b_to_padded(full["w_q_b"]))
    w["w_kv_b"] = mx(full["w_kv_b"])
    w["q_norm_w"] = jnp.pad(full["q_norm_w"], ((0, 0), (0, HP - QK_HEAD)))
    w["k_norm_w"] = jnp.pad(full["k_norm_w"], ((0, 0), (0, HP - QK_HEAD)))
    w["mla_out_norm_w"] = full["mla_out_norm_w"]
    w["dn_out_norm_w"] = full["dn_out_norm_w"]
    w["dn_conv_w"] = _pad_rows8(full["dn_conv_w"])
    w["alog_b"] = jnp.repeat(full["dn_A_log"], DN_DIM, axis=1)
    w["dtb_b"] = jnp.repeat(full["dn_dt_bias"], DN_DIM, axis=1)
    w["ffn_norm_w"] = full["ffn_norm_w"]
    w["ffn_conv_w"] = _pad_rows8(full["ffn_conv_w"])
    w["ffn_conv_b"] = full["ffn_conv_b"]
    for n in _LATE:
        if n in full:
            w[n] = mx(full[n])
    half = QK_ROPE // 2
    inv = ROPE_THETA ** (-jnp.arange(half, dtype=F32) / half)
    ang = (jnp.arange(tp, dtype=jnp.int32) - PAD).astype(F32)[:, None] * inv[None, :]
    zc = jnp.zeros((tp, LANE - QK_ROPE), F32)
    w["cos"] = jnp.concatenate([jnp.cos(ang), jnp.cos(ang), zc], axis=1)
    w["sin_s"] = jnp.concatenate([-jnp.sin(ang), jnp.sin(ang), zc], axis=1)
    return w


def _grads_to_natural(g):
    convert = {
        "w_in": ("w_in", _w_in_from_padded),
        "w_q_b": ("w_q_b", _w_q_b_from_padded),
        "q_norm_w": ("q_norm_w", lambda a: a[:, :QK_HEAD]),
        "k_norm_w": ("k_norm_w", lambda a: a[:, :QK_HEAD]),
        "dn_conv_w": ("dn_conv_w", lambda a: a[:DN_CONV]),
        "ffn_conv_w": ("ffn_conv_w", lambda a: a[:FFN_CONV]),
        "alog_b": ("dn_A_log", lambda a: a[:, ::DN_DIM]),
        "dtb_b": ("dn_dt_bias", lambda a: a[:, ::DN_DIM]),
    }
    n = {}
    for key, a in g.items():
        name, fn = convert.get(key, (key, lambda t: t))
        n[name] = fn(a)
    return n


_MESH = pl.DeviceIdType.MESH
_ANY = pl.BlockSpec(memory_space=pl.ANY)
_CHIP_FLIPS = ((1, 0), (0, 1), (1, 1))


def _me():
    return lax.axis_index("x"), lax.axis_index("y"), lax.axis_index("c")


def _all_gather(name, blk, after):
    after = list(after)

    def body(x_ref, *rest):
        out_ref, send_sems, recv_sems, local_sem = rest[len(after):]
        x, y, c = _me()
        me, sib = (x, y, c), (x, y, 1 - c)
        chips = [(x ^ fx, y ^ fy) for fx, fy in _CHIP_FLIPS]

        def slot(p):
            return out_ref.at[4 * p[0] + 2 * p[1] + p[2]]

        def copy(k, block, to, src=None):
            return pltpu.make_async_remote_copy(
                src_ref=slot(block) if src is None else src, dst_ref=slot(block),
                send_sem=send_sems.at[k], recv_sem=recv_sems.at[k], device_id=to, device_id_type=_MESH)

        mine = pltpu.make_async_copy(x_ref, slot(me), local_sem)
        mine.start()
        first = [copy(0, me, sib, src=x_ref)]
        first += [copy(1 + j, me, (*chip, c), src=x_ref) for j, chip in enumerate(chips)]
        for cp in first:
            cp.start()
        passed = [copy(4 + j, (*chip, c), sib) for j, chip in enumerate(chips)]
        for j, chip in enumerate(chips):
            copy(1 + j, (*chip, c), me).wait_recv()
            passed[j].start()
        copy(0, sib, me).wait_recv()
        for j, chip in enumerate(chips):
            copy(4 + j, (*chip, 1 - c), me).wait_recv()
        for cp in first + passed:
            cp.wait_send()
        mine.wait()

    return pl.pallas_call(
        body, name=name, in_specs=[_ANY] * (1 + len(after)), out_specs=_ANY,
        out_shape=jax.ShapeDtypeStruct((N_DEV,) + blk.shape, blk.dtype),
        scratch_shapes=[pltpu.SemaphoreType.DMA((7,)), pltpu.SemaphoreType.DMA((7,)), pltpu.SemaphoreType.DMA],
    )(blk, *after)


def _row_tile(r):
    divs = [d for d in range(16, min(r, 512) + 1, 16) if r % d == 0]
    return divs[-1] if divs else r


def _adam_math(g, w, m, v):
    m_new = ADAM_B1 * m + (1.0 - ADAM_B1) * g
    v_new = ADAM_B2 * v + (1.0 - ADAM_B2) * (g * g)
    m_hat = m_new / (1.0 - ADAM_B1 ** ADAM_STEP)
    v_hat = v_new / (1.0 - ADAM_B2 ** ADAM_STEP)
    return -ADAM_LR * (m_hat / (jnp.sqrt(v_hat) + ADAM_EPS) + ADAM_WD * w), m_new, v_new


def _adam_vectors(name, row, items, ws, ms, vs):
    k = len(items)

    def body(row_ref, *refs):
        w_refs, m_refs, v_refs = refs[:k], refs[k:2 * k], refs[2 * k:3 * k]
        outs = refs[3 * k:]
        for idx, (off, n, per_head) in enumerate(items):
            if per_head:
                spread = row_ref[:, off:off + DN_WIDTH]
                lane = lax.broadcasted_iota(jnp.int32, (1, LANE), 1)
                g = jnp.zeros((1, LANE), F32)
                for h in range(DN_HEADS):
                    g = g + jnp.where(lane == h, spread[:, DN_DIM * h:DN_DIM * h + 1], 0.0)
                g = g[:, :n]
            else:
                g = row_ref[:, off:off + n]
            d, m_new, v_new = _adam_math(g, w_refs[idx][...], m_refs[idx][...], v_refs[idx][...])
            for kind, val in enumerate((g, d, m_new, v_new)):
                outs[kind * k + idx][...] = val

    shapes = [jax.ShapeDtypeStruct((1, n), F32) for _, n, _ in items]
    res = pl.pallas_call(body, name=name, out_shape=shapes * 4)(row, *ws, *ms, *vs)
    return [list(res[kind * k:(kind + 1) * k]) for kind in range(4)]


def _adam_arrays(name, gs, ws, ms, vs):
    k = len(gs)

    def body(*refs):
        outs = refs[4 * k:]
        for idx in range(k):
            res = _adam_math(refs[idx][...], refs[k + idx][...], refs[2 * k + idx][...], refs[3 * k + idx][...])
            for kind, val in enumerate(res):
                outs[kind * k + idx][...] = val

    shapes = [jax.ShapeDtypeStruct(w.shape, F32) for w in ws]
    res = pl.pallas_call(body, name=name, out_shape=shapes * 3)(*gs, *ws, *ms, *vs)
    return [list(res[kind * k:(kind + 1) * k]) for kind in range(3)]


def _sum_parts(name, parts):
    _, r, cols = parts[0][0].shape
    tm = _row_tile(r)
    idx = jnp.stack([jnp.asarray(s, jnp.int32) for _, s in parts])
    n = len(parts)

    def body(idx_ref, *refs):
        g = refs[0][0].astype(F32)
        for p_ref in refs[1:n]:
            g = g + p_ref[0].astype(F32)
        refs[n][...] = g

    return pl.pallas_call(
        body, name=name,
        grid_spec=pltpu.PrefetchScalarGridSpec(
            num_scalar_prefetch=1, grid=(r // tm,),
            in_specs=[pl.BlockSpec((1, tm, cols), lambda i, idx_ref, p=p: (idx_ref[p], i, 0)) for p in range(n)],
            out_specs=pl.BlockSpec((tm, cols), lambda i, idx_ref: (i, 0))),
        out_shape=jax.ShapeDtypeStruct((r, cols), F32),
        compiler_params=pltpu.CompilerParams(dimension_semantics=("parallel",)),
    )(idx, *[a for a, _ in parts])


def _adam(name, parts, w, m, v):
    r, cols = w.shape
    tm = _row_tile(r)
    tc = cols // 4 if (r // tm < 4 and cols % (4 * LANE) == 0) else cols
    idx = jnp.stack([jnp.asarray(s, jnp.int32) for _, s in parts])
    n = len(parts)

    def body(idx_ref, *refs):
        g = refs[0][0].astype(F32)
        for p_ref in refs[1:n]:
            g = g + p_ref[0].astype(F32)
        w_ref, m_ref, v_ref, g_out, d_out, m_out, v_out = refs[n:]
        g_out[...] = g
        d_out[...], m_out[...], v_out[...] = _adam_math(g, w_ref[...], m_ref[...], v_ref[...])

    part_specs = [pl.BlockSpec((1, tm, tc), lambda i, j, idx_ref, p=p: (idx_ref[p], i, j)) for p in range(n)]
    flat = pl.BlockSpec((tm, tc), lambda i, j, idx_ref: (i, j))
    return pl.pallas_call(
        body, name=name,
        grid_spec=pltpu.PrefetchScalarGridSpec(
            num_scalar_prefetch=1, grid=(r // tm, cols // tc), in_specs=part_specs + [flat] * 3,
            out_specs=[flat] * 4),
        out_shape=[jax.ShapeDtypeStruct((r, cols), F32)] * 4,
        compiler_params=pltpu.CompilerParams(dimension_semantics=("parallel", "parallel")),
    )(idx, *[a for a, _ in parts], w, m, v)


def _all_gather_many(name, blks):
    n = len(blks)

    def body(*refs):
        x_refs, out_refs = refs[:n], refs[n:2 * n]
        send_sems, recv_sems, local_sems = refs[2 * n:]
        x, y, c = _me()
        me, sib = (x, y, c), (x, y, 1 - c)
        chips = [(x ^ fx, y ^ fy) for fx, fy in _CHIP_FLIPS]

        def slot(a, p):
            return out_refs[a].at[4 * p[0] + 2 * p[1] + p[2]]

        def copy(a, k, block, to, src=None):
            return pltpu.make_async_remote_copy(
                src_ref=slot(a, block) if src is None else src, dst_ref=slot(a, block),
                send_sem=send_sems.at[7 * a + k], recv_sem=recv_sems.at[7 * a + k], device_id=to,
                device_id_type=_MESH)

        mine = [pltpu.make_async_copy(x_refs[a], slot(a, me), local_sems.at[a]) for a in range(n)]
        first = []
        for a in range(n):
            mine[a].start()
            first.append(copy(a, 0, me, sib, src=x_refs[a]))
            first += [copy(a, 1 + j, me, (*chip, c), src=x_refs[a]) for j, chip in enumerate(chips)]
        for cp in first:
            cp.start()
        passed = []
        for j, chip in enumerate(chips):
            for a in range(n):
                copy(a, 1 + j, (*chip, c), me).wait_recv()
                cp = copy(a, 4 + j, (*chip, c), sib)
                cp.start()
                passed.append(cp)
        for a in range(n):
            copy(a, 0, sib, me).wait_recv()
            for j, chip in enumerate(chips):
                copy(a, 4 + j, (*chip, 1 - c), me).wait_recv()
        for cp in first + passed:
            cp.wait_send()
        for cp in mine:
            cp.wait()

    return pl.pallas_call(
        body, name=name, in_specs=[_ANY] * n, out_specs=[_ANY] * n,
        out_shape=[jax.ShapeDtypeStruct((N_DEV,) + b.shape, b.dtype) for b in blks],
        scratch_shapes=[pltpu.SemaphoreType.DMA((7 * n,)), pltpu.SemaphoreType.DMA((7 * n,)),
                        pltpu.SemaphoreType.DMA((n,))],
    )(*blks)


_HBM = pl.BlockSpec(memory_space=pltpu.HBM)
_SEM = pl.BlockSpec(memory_space=pltpu.SEMAPHORE)
_EFFECT = pltpu.SideEffectType.DATAFLOW_SIDE_EFFECTING


def _push_copies(src_refs, land_refs, send_sems, recv_sems, src_by_peer, first=0):
    x, y, c = _me()
    my_id = 4 * x + 2 * y + c
    out = []
    for k in range(len(src_refs)):
        a = first + k
        for f in range(1, N_DEV):
            px, py, pc = x ^ (f >> 2), y ^ ((f >> 1) & 1), c ^ (f & 1)
            pid = 4 * px + 2 * py + pc
            src = src_refs[k].at[pid] if src_by_peer else src_refs[k]
            start = pltpu.make_async_remote_copy(
                src_ref=src, dst_ref=land_refs[k].at[my_id], send_sem=send_sems.at[7 * a + f - 1],
                recv_sem=recv_sems.at[7 * a + f - 1], device_id=(px, py, pc), device_id_type=_MESH)
            landed = pltpu.make_async_remote_copy(
                src_ref=src, dst_ref=land_refs[k].at[pid], send_sem=send_sems.at[7 * a + f - 1],
                recv_sem=recv_sems.at[7 * a + f - 1], device_id=(px, py, pc), device_id_type=_MESH)
            out.append((start, landed))
    return out


def _push_start(name, srcs, src_by_peer, after):
    n = len(srcs)
    lands = [jax.ShapeDtypeStruct((N_DEV,) + (s.shape[1:] if src_by_peer else s.shape), s.dtype) for s in srcs]

    def body(*refs):
        src_refs, land_refs = refs[:n], refs[n:2 * n]
        send_sems, recv_sems = refs[2 * n + 1], refs[2 * n + 2]
        token = refs[-1]
        for start, _ in _push_copies(src_refs, land_refs, send_sems, recv_sems, src_by_peer):
            start.start()
        token[...] = jnp.zeros_like(token)

    hbm = lambda a: pltpu.with_memory_space_constraint(a, pltpu.HBM)
    res = pl.pallas_call(
        body, name=name,
        out_shape=(pltpu.SemaphoreType.DMA((7 * n,)), pltpu.SemaphoreType.DMA((7 * n,)),
                   *[pltpu.HBM(s.shape, s.dtype) for s in srcs], *[pltpu.HBM(s.shape, s.dtype) for s in lands],
                   jax.ShapeDtypeStruct((8, LANE), F32)),
        in_specs=[_HBM] * (2 * n) + [_ANY],
        out_specs=(_SEM, _SEM, *[_HBM] * (2 * n), pl.BlockSpec(memory_space=pltpu.VMEM)),
        input_output_aliases={i: 2 + i for i in range(2 * n)},
        compiler_params=pltpu.CompilerParams(has_side_effects=_EFFECT),
    )(*[hbm(s) for s in srcs], *[hbm(lax.empty(s.shape, s.dtype)) for s in lands], after)
    return res[0], res[1], list(res[2:2 + n]), list(res[2 + n:2 + 2 * n]), res[-1]


def _push_wait(name, send_sems, recv_sems, srcs, lands, src_by_peer, after, first=0):
    n = len(srcs)
    after = list(after) if isinstance(after, (list, tuple)) else [after]

    def body(*refs):
        src_refs, land_refs = refs[:n], refs[n:2 * n]
        s_sems, r_sems = refs[2 * n], refs[2 * n + 1]
        for _, landed in _push_copies(src_refs, land_refs, s_sems, r_sems, src_by_peer, first):
            landed.wait_send()
            landed.wait_recv()

    res = pl.pallas_call(
        body, name=name,
        out_shape=tuple(pltpu.HBM(s.shape, s.dtype) for s in list(srcs) + list(lands)),
        in_specs=[_HBM] * (2 * n) + [_SEM, _SEM] + [_ANY] * len(after),
        out_specs=tuple([_HBM] * (2 * n)),
        input_output_aliases={i: i for i in range(2 * n)},
        compiler_params=pltpu.CompilerParams(has_side_effects=_EFFECT),
    )(*srcs, *lands, send_sems, recv_sems, *after)
    return list(res[:n]), list(res[n:])


_SHARDED = (
    ("meta_tokens", 1, (N_META, D_MODEL)),
    ("w_in", 1, (D_MODEL, IN_COLS)),
    ("w_q_b", 1, (Q_LORA, MLA_HEADS * QK_HEAD)),
    ("w_kv_b", 1, (KV_LORA, MLA_HEADS * (QK_NOPE + V_HEAD))),
    ("dn_conv_w", 1, (DN_CONV, 3 * DN_WIDTH)),
    ("w_out", 0, (2 * DN_WIDTH, D_MODEL)),
    ("w_gate", 1, (D_MODEL, D_FF)),
    ("w_up", 1, (D_MODEL, D_FF)),
    ("ffn_conv_w", 1, (FFN_CONV, D_FF)),
    ("w_down", 0, (D_FF, D_MODEL)),
)
_F32_GATHERED = ("meta_tokens", "dn_conv_w", "ffn_conv_w")
_EARLY = ("w_in", "w_q_b", "w_kv_b")
_LATE = ("w_out", "w_gate", "w_up", "w_down")
_TRANSPOSED = ("w_in", "w_q_b", "w_gate", "w_up")
_REPLICATED = (
    ("attn_norm_w", D_MODEL), ("q_a_norm_w", Q_LORA), ("kv_a_norm_w", KV_LORA), ("q_norm_w", QK_HEAD),
    ("k_norm_w", QK_HEAD), ("mla_out_norm_w", V_HEAD), ("dn_A_log", DN_HEADS), ("dn_dt_bias", DN_HEADS),
    ("dn_out_norm_w", DN_DIM), ("ffn_norm_w", D_MODEL), ("ffn_conv_b", D_FF),
)
_SMALL_BLOCK = (8, 512)


def _local_shape(dim, shape):
    return (shape[0] // N_DEV, shape[1]) if dim == 0 else (shape[0], shape[1] // N_DEV)


def _from_blocks(blocks, dim, shape):
    r, c = shape
    if dim == 0:
        return blocks.reshape(r, c)
    return blocks.reshape(N_DEV, r, c // N_DEV).transpose(1, 0, 2).reshape(r, c)


def _split(flat, sizes):
    out, o = [], 0
    for s in sizes:
        out.append(flat[..., o:o + s])
        o += s
    return out


def kernel(x, meta_tokens, attn_norm_w, w_in, q_a_norm_w, w_q_b, kv_a_norm_w, w_kv_b, q_norm_w, k_norm_w, mla_out_norm_w, dn_conv_w, dn_A_log, dn_dt_bias, dn_out_norm_w, w_out, ffn_norm_w, w_gate, w_up, ffn_conv_w, ffn_conv_b, w_down, loss_target, m_meta_tokens, m_attn_norm_w, m_w_in, m_q_a_norm_w, m_w_q_b, m_kv_a_norm_w, m_w_kv_b, m_q_norm_w, m_k_norm_w, m_mla_out_norm_w, m_dn_conv_w, m_dn_A_log, m_dn_dt_bias, m_dn_out_norm_w, m_w_out, m_ffn_norm_w, m_w_gate, m_w_up, m_ffn_conv_w, m_ffn_conv_b, m_w_down, v_meta_tokens, v_attn_norm_w, v_w_in, v_q_a_norm_w, v_w_q_b, v_kv_a_norm_w, v_w_kv_b, v_q_norm_w, v_k_norm_w, v_mla_out_norm_w, v_dn_conv_w, v_dn_A_log, v_dn_dt_bias, v_dn_out_norm_w, v_w_out, v_ffn_norm_w, v_w_gate, v_w_up, v_ffn_conv_w, v_ffn_conv_b, v_w_down):
    names = [n for n, _, _ in _SHARDED] + [n for n, _ in _REPLICATED]
    given = dict(locals())
    two_d = lambda a: a.reshape(a.shape[-2:])
    view = lambda a, n: two_d(a).T if n in _TRANSPOSED else two_d(a)
    wl = {n: view(given[n], n) for n in names}
    ml = {n: view(given["m_" + n], n) for n in names}
    vl = {n: view(given["v_" + n], n) for n in names}
    out_shapes = {n: given[n].shape for n in names}

    spec = {n: (d, s) for n, d, s in _SHARDED}
    small_sizes = [math.prod(_local_shape(*spec[n])) for n in _F32_GATHERED]

    def small_block(d):
        cat = jnp.concatenate([d[n].reshape(d[n].shape[:-2] + (-1,)) for n in _F32_GATHERED], axis=-1)
        pad = [(0, 0)] * (cat.ndim - 1) + [(0, math.prod(_SMALL_BLOCK) - cat.shape[-1])]
        return jnp.pad(cat, pad).reshape(cat.shape[:-1] + _SMALL_BLOCK)

    def shard(n):
        return wl[n].astype(_MXU)

    def from_slots(n, blocks):
        d, s = spec[n]
        if d == 0 or n in _TRANSPOSED:
            return blocks.reshape(-1, blocks.shape[-1])
        return blocks.transpose(1, 0, 2).reshape(s)

    my_id = 4 * lax.axis_index("x") + 2 * lax.axis_index("y") + lax.axis_index("c")
    got = _all_gather_many("gather_early", [shard(n) for n in _EARLY] + [small_block(wl)])
    full = {n: a for n, a in wl.items() if n not in _LATE}
    for n, blocks in zip(_EARLY, got):
        full[n] = from_slots(n, blocks)
    for n, p in zip(_F32_GATHERED, _split(got[-1].reshape(N_DEV, -1), small_sizes)):
        full[n] = _from_blocks(p, *spec[n])
    late_own = [shard(n) for n in _LATE]
    l_send, l_recv, l_src, l_land, token = _push_start("gather_late_start", late_own, False, got[-1])

    def late_weights(after, names):
        first = _LATE.index(names[0])
        sl = slice(first, first + len(names))
        _, lands = _push_wait("gather_late_wait_" + names[0], l_send, l_recv, l_src[sl], l_land[sl], False,
                              after, first)
        out = {}
        for n, land, own in zip(names, lands, late_own[sl]):
            out[n] = from_slots(n, lax.dynamic_update_slice(land, own[None], (my_id, 0, 0))).astype(_MXU)
        return out

    def dest_blocks(n, a):
        d, s = spec[n]
        r, c = _local_shape(d, s)
        if n in _TRANSPOSED:
            return a.reshape(N_DEV, c, r)
        return a.reshape(N_DEV, r, c) if d == 0 else a.reshape(r, N_DEV, c).transpose(1, 0, 2)

    pushed = []

    def grads_ready(g, names):
        nat = _grads_to_natural({n: g[n] for n in names})
        blocks = [dest_blocks(n, nat[n]).astype(_MXU) for n in names]
        sends, recvs, srcs, lands, tok = _push_start("rs_" + names[0] + "_start", blocks, True, token)
        pushed.append((names, sends, recvs, srcs, lands))
        return tok

    seq = x.shape[1]
    tp = ROW0 + seq
    h0 = jnp.concatenate([jnp.zeros((PAD, D_MODEL), F32), full["meta_tokens"], x[0]], axis=0)
    tgt = jnp.concatenate([jnp.zeros((ROW0, D_MODEL), F32), loss_target[0]], axis=0)
    loss, dh0, raw = _local_step(h0, tgt, _prepare(full, tp), token, late_weights, grads_ready)
    g = _grads_to_natural(raw)
    g["meta_tokens"] = dh0[PAD:ROW0]
    grad_x = dh0[ROW0:][None]

    big = [{}, {}, {}, {}]

    def finish(group):
        names, sends, recvs, srcs, lands = group
        srcs, lands = _push_wait("rs_" + names[0] + "_wait", sends, recvs, srcs, lands, True, dh0)
        for n, src, land in zip(names, srcs, lands):
            parts = [(src, my_id)] + [(land, my_id ^ f) for f in range(1, N_DEV)]
            for kind, a in enumerate(_adam("adam_" + n, parts, wl[n], ml[n], vl[n])):
                big[kind][n] = a

    for group in pushed[:-1]:
        finish(group)
    rep_names = [n for n, _ in _REPLICATED]
    raw_key = {"dn_A_log": "alog_b", "dn_dt_bias": "dtb_b"}
    pieces = [raw[raw_key.get(n, n)] for n in rep_names] + [loss]
    pieces += [g[n].reshape(1, -1) for n in _F32_GATHERED]
    widths = [p.shape[1] for p in pieces]
    offs = [sum(widths[:k]) for k in range(len(widths))]
    cat = jnp.concatenate(pieces, axis=1)
    cols = -(-cat.shape[1] // (8 * LANE)) * LANE
    mine = jnp.pad(cat, ((0, 0), (0, 8 * cols - cat.shape[1]))).reshape(8, cols)
    everyone = _all_gather("gather_small_grads", mine, [big[1][n] for group in pushed[:-1] for n in group[0]])
    total = _sum_parts("sum_small_grads", [(everyone, d) for d in range(N_DEV)]).reshape(1, 8 * cols)
    tot = {n: total[0, o:o + wd] for n, o, wd in zip(rep_names + ["loss"] + list(_F32_GATHERED), offs, widths)}
    items = [(o, size, n in raw_key) for (n, size), o in zip(_REPLICATED, offs)]
    sm = _adam_vectors("adam_replicated", total, items, [wl[n] for n in rep_names], [ml[n] for n in rep_names],
                       [vl[n] for n in rep_names])
    sm = [dict(zip(rep_names, kind)) for kind in sm]
    mine_of = {}
    for n in _F32_GATHERED:
        d, s = spec[n]
        r, c = _local_shape(d, s)
        mine_of[n] = lax.dynamic_slice(tot[n].reshape(s), (0, my_id * c), (r, c))
    res = _adam_arrays("adam_small_sharded", [mine_of[n] for n in _F32_GATHERED], [wl[n] for n in _F32_GATHERED],
                       [ml[n] for n in _F32_GATHERED], [vl[n] for n in _F32_GATHERED])
    for kind, arrays in enumerate([[mine_of[n] for n in _F32_GATHERED]] + res):
        big[kind].update(zip(_F32_GATHERED, arrays))

    finish(pushed[-1])

    outs = [tot["loss"][0], grad_x]
    for kind in range(4):
        for n in ("meta_tokens", "attn_norm_w", "w_in", "q_a_norm_w", "w_q_b", "kv_a_norm_w", "w_kv_b", "q_norm_w",
                  "k_norm_w", "mla_out_norm_w", "dn_conv_w", "dn_A_log", "dn_dt_bias", "dn_out_norm_w", "w_out",
                  "ffn_norm_w", "w_gate", "w_up", "ffn_conv_w", "ffn_conv_b", "w_down"):
            src = big[kind] if n in big[kind] else sm[kind]
            a = src[n].T if n in _TRANSPOSED else src[n]
            outs.append(a.reshape(out_shapes[n]))
    return tuple(outs)
```

```python
import functools
import math

import jax
import jax.numpy as jnp
from jax import lax
from jax.experimental import pallas as pl
from jax.experimental.pallas import tpu as pltpu

F32 = jnp.float32
_MXU = jnp.bfloat16
_HI = lax.Precision.HIGHEST

D_MODEL = 1024
N_META = 16
PAD = 112
ROW0 = PAD + N_META
MLA_HEADS = 4
QK_NOPE = 128
QK_ROPE = 64
QK_HEAD = QK_NOPE + QK_ROPE
V_HEAD = 128
Q_LORA = 256
KV_LORA = 256
ROPE_THETA = 10000.0
DN_HEADS = 4
DN_DIM = 128
DN_WIDTH = DN_HEADS * DN_DIM
DN_CONV = 4
DN_CHUNK = 64
GDN_SUB_CHUNKS = 2
D_FF = 2816
FFN_CONV = 3
EPS = 1e-6
HP = 256
C_Z = 1536
C_QL = 2048
C_KVL = 2304
C_KPE = 2560
C_AB = 2688
IN_COLS = 2632

ADAM_LR = 0.001
ADAM_B1 = 0.9
ADAM_B2 = 0.999
ADAM_EPS = 1e-08
ADAM_WD = 0.01
ADAM_STEP = 10

N_DEV = 8
TM = 128
LANE = 128
VMEM_LIMIT = 56 * 1024 * 1024
NEG = -1e30


def _dot(a, b, dims, hp=False):
    if hp:
        return lax.dot_general(a.astype(F32), b.astype(F32), (dims, ((), ())),
                               precision=lax.Precision.HIGH if hp == "3x" else _HI, preferred_element_type=F32)
    return lax.dot_general(a.astype(_MXU), b.astype(_MXU), (dims, ((), ())),
                           preferred_element_type=F32)


def _nn(a, b, hp=False):
    return _dot(a, b, ((1,), (0,)), hp)


def _nt(a, b, hp=False):
    return _dot(a, b, ((1,), (1,)), hp)


def _tn(a, b, hp=False):
    return _dot(a, b, ((0,), (0,)), hp)


def _sigmoid(x):
    return 1.0 / (1.0 + jnp.exp(-x))


def _rms_fwd(x, w, n):
    r = lax.rsqrt(jnp.sum(x * x, axis=-1, keepdims=True) * (1.0 / n) + EPS)
    return x * r * w, r


def _rms_bwd(x, w, dy, n):
    r = lax.rsqrt(jnp.sum(x * x, axis=-1, keepdims=True) * (1.0 / n) + EPS)
    xh = x * r
    gy = dy * w
    dx = r * (gy - xh * (jnp.sum(gy * xh, axis=-1, keepdims=True) * (1.0 / n)))
    return dx, dy * xh


def _rowsum(x):
    return jnp.sum(x, axis=0, keepdims=True)


def _row_ids(i, tm):
    return i * tm + lax.broadcasted_iota(jnp.int32, (tm, 1), 0)


def _shift_down(ext, s, tm):
    if s == 0:
        return ext[8:8 + tm]
    return pltpu.roll(ext, s, 0)[8:8 + tm]


def _shift_up(ext, s, tm):
    if s == 0:
        return ext[0:tm]
    return pltpu.roll(ext, tm + 8 - s, 0)[0:tm]


def _conv_taps(x, halo_prev, width):
    tm = x.shape[0]
    ext = jnp.concatenate([halo_prev, x], axis=0)
    return [_shift_down(ext, width - 1 - j, tm) for j in range(width)]


def _conv_from_taps(taps, w):
    y = None
    for j, tap in enumerate(taps):
        t = w[j:j + 1, :] * tap
        y = t if y is None else y + t
    return y


def _conv_fwd(x, halo_prev, w, width):
    return _conv_from_taps(_conv_taps(x, halo_prev, width), w)


def _conv_bwd_w_taps(dy, taps):
    tm = dy.shape[0]
    rows = [_rowsum(dy * tap[:tm]) for tap in taps]
    rows += [jnp.zeros_like(rows[0])] * (8 - len(taps))
    return jnp.concatenate(rows, axis=0)


def _conv_bwd_x(dy, halo_next, w, width):
    tm = dy.shape[0]
    ext = jnp.concatenate([dy, halo_next], axis=0)
    dx = None
    for j in range(width):
        t = w[j:j + 1, :] * _shift_up(ext, width - 1 - j, tm)
        dx = t if dx is None else dx + t
    return dx


def _softplus(x):
    e = jnp.exp(-jnp.abs(x))
    u = 1.0 + e
    l1p = jnp.where(u == 1.0, e, jnp.log(u) * e / jnp.where(u == 1.0, 1.0, u - 1.0))
    return jnp.maximum(x, 0.0) + l1p


def _swap_halves(x):
    lane = lax.broadcasted_iota(jnp.int32, x.shape, 1)
    return jnp.where(lane < 32, pltpu.roll(x, 96, 1), jnp.where(lane < 64, pltpu.roll(x, 32, 1), 0.0))


class _In:
    def __init__(self, arr, width=None, cb=0, kind="cur"):
        self.arr, self.kind = arr, kind
        self.width = arr.shape[1] if width is None else width
        self.cb = cb


def _whole_spec(x):
    return pl.BlockSpec(x.shape, lambda i, nd=x.ndim: (0,) * nd, pipeline_mode=pl.Buffered(1))


def _tile_spec(t, tm, tp):
    r8 = tm // 8
    if t.kind == "cur":
        return pl.BlockSpec((tm, t.width), lambda i, cb=t.cb: (i, cb))
    if t.kind == "prev":
        return pl.BlockSpec((8, t.width), lambda i, cb=t.cb: (jnp.maximum(i * r8 - 1, 0), cb))
    return pl.BlockSpec((8, t.width), lambda i, cb=t.cb: (jnp.minimum((i + 1) * r8, tp // 8 - 1), cb))


def _rows(name, fn, tiled, full, outs, accs=(), tm=TM):
    tp = tiled[0].arr.shape[0]
    nt = tp // tm
    n_in = len(tiled) + len(full)
    n_out = len(outs)

    def body(*refs):
        i = pl.program_id(0)
        vals = [r[...] for r in refs[:n_in]]
        o_t, o_a = fn(i, *vals)
        for r, v in zip(refs[n_in:n_in + n_out], o_t):
            r[...] = v.astype(r.dtype)
        for r, v in zip(refs[n_in + n_out:], o_a):
            @pl.when(i == 0)
            def _():
                r[...] = v

            @pl.when(i > 0)
            def _():
                r[...] += v

    in_specs = [_tile_spec(t, tm, tp) for t in tiled]
    in_specs += [pl.BlockSpec(a.shape, lambda i, nd=a.ndim: (0,) * nd) for a in full]
    out_specs = [pl.BlockSpec((tm, w), lambda i: (i, 0)) for w, _ in outs]
    out_specs += [pl.BlockSpec((r, w), lambda i: (0, 0)) for r, w in accs]
    out_shape = [jax.ShapeDtypeStruct((tp, w), dt) for w, dt in outs]
    out_shape += [jax.ShapeDtypeStruct((r, w), F32) for r, w in accs]
    res = pl.pallas_call(
        body, name=name, grid=(nt,), in_specs=in_specs, out_specs=out_specs, out_shape=out_shape,
        compiler_params=pltpu.CompilerParams(dimension_semantics=("arbitrary",), vmem_limit_bytes=VMEM_LIMIT),
    )(*[t.arr for t in tiled], *full)
    return res


def _pick(n, cap, mult):
    best = None
    for d in range(mult, min(n, cap) + 1, mult):
        if n % d == 0:
            best = d
    assert best is not None, (n, cap, mult)
    return best


_ANY_SPEC = pl.BlockSpec(memory_space=pl.ANY)


def _mm(name, a, b, mode, out_dtype=F32, resid=None, after=None):
    if mode == "tn":
        m, k = a.shape
        n = b.shape[1]
        tk = _pick(k, 512, 128)
        tn = _pick(n, 1408, 128)

        def body_tn(a_ref, b_ref, o_ref):
            o_ref[...] = _tn(a_ref[...], b_ref[...]).astype(o_ref.dtype)

        return pl.pallas_call(
            body_tn, name=name, grid=(n // tn, k // tk),
            in_specs=[pl.BlockSpec((m, tk), lambda j, p: (0, p)),
                      pl.BlockSpec((m, tn), lambda j, p: (0, j))],
            out_specs=pl.BlockSpec((tk, tn), lambda j, p: (p, j)),
            out_shape=jax.ShapeDtypeStruct((k, n), out_dtype),
            compiler_params=pltpu.CompilerParams(
                dimension_semantics=("parallel", "parallel"), vmem_limit_bytes=VMEM_LIMIT),
        )(a, b)

    m, k = a.shape
    n = b.shape[1] if mode == "nn" else b.shape[0]
    tn = _pick(n, 1408, 128)
    tm = _pick(m, 1152, 16)
    dotf = _nn if mode == "nn" else _nt

    def body(*refs):
        a_ref, b_ref, o_ref = refs[0], refs[1], refs[-1]
        acc = dotf(a_ref[...], b_ref[...])
        if resid is not None:
            acc = refs[2][...] + acc
        o_ref[...] = acc.astype(o_ref.dtype)

    b_spec = (pl.BlockSpec((k, tn), lambda j, i: (0, j)) if mode == "nn"
              else pl.BlockSpec((tn, k), lambda j, i: (j, 0)))
    in_specs = [pl.BlockSpec((tm, k), lambda j, i: (i, 0)), b_spec]
    args = [a, b]
    if resid is not None:
        in_specs.append(pl.BlockSpec((tm, tn), lambda j, i: (i, j)))
        args.append(resid)
    if after is not None:
        in_specs.append(_ANY_SPEC)
        args.append(after)
    return pl.pallas_call(
        body, name=name, grid=(n // tn, m // tm), in_specs=in_specs,
        out_specs=pl.BlockSpec((tm, tn), lambda j, i: (i, j)),
        out_shape=jax.ShapeDtypeStruct((m, n), out_dtype),
        compiler_params=pltpu.CompilerParams(
            dimension_semantics=("parallel", "parallel"), vmem_limit_bytes=VMEM_LIMIT),
    )(*args)


def _mm_tn2(name, a1, a2, b, out_dtype=F32):
    m, k = a1.shape
    n = b.shape[1]
    tk = _pick(k, 512, 128)

    def body(a1_ref, a2_ref, b_ref, o1_ref, o2_ref):
        bb = b_ref[...]
        o1_ref[...] = _tn(a1_ref[...], bb).astype(o1_ref.dtype)
        o2_ref[...] = _tn(a2_ref[...], bb).astype(o2_ref.dtype)

    a_spec = pl.BlockSpec((m, tk), lambda p: (0, p))
    o_spec = pl.BlockSpec((tk, n), lambda p: (p, 0))
    return pl.pallas_call(
        body, name=name, grid=(k // tk,),
        in_specs=[a_spec, a_spec, pl.BlockSpec((m, n), lambda p: (0, 0))],
        out_specs=[o_spec, o_spec], out_shape=[jax.ShapeDtypeStruct((k, n), out_dtype)] * 2,
        compiler_params=pltpu.CompilerParams(dimension_semantics=("parallel",), vmem_limit_bytes=VMEM_LIMIT),
    )(a1, a2, b)


def _mm_tn_pair(name, a1, b1, a2, b2):
    def body(a1_ref, b1_ref, a2_ref, b2_ref, o1_ref, o2_ref):
        o1_ref[...] = _tn(a1_ref[...], b1_ref[...])
        o2_ref[...] = _tn(a2_ref[...], b2_ref[...])

    return pl.pallas_call(
        body, name=name,
        out_shape=[jax.ShapeDtypeStruct((a1.shape[1], b1.shape[1]), F32),
                   jax.ShapeDtypeStruct((a2.shape[1], b2.shape[1]), F32)],
        compiler_params=pltpu.CompilerParams(vmem_limit_bytes=VMEM_LIMIT),
    )(a1, b1, a2, b2)


def _norm_mm(name, x, norm_w, b, mode="nt", x_cb=0, after=None):
    m = x.shape[0]
    k = norm_w.shape[1]
    n = b.shape[0] if mode == "nt" else b.shape[1]
    tn = _pick(n, 1408, 128)
    tm = _pick(m, 1152, 16)
    dotf = _nt if mode == "nt" else _nn
    extra = [] if after is None else [after]

    def body(x_ref, w_ref, b_ref, *rest):
        o_ref, u_ref = rest[-2:]

        @pl.when(pl.program_id(1) == 0)
        def _():
            u_ref[...] = _rms_fwd(x_ref[...], w_ref[...], k)[0].astype(u_ref.dtype)

        o_ref[...] = dotf(u_ref[...], b_ref[...])

    b_spec = (pl.BlockSpec((tn, k), lambda i, j: (j, 0)) if mode == "nt"
              else pl.BlockSpec((k, tn), lambda i, j: (0, j)))
    return pl.pallas_call(
        body, name=name, grid=(m // tm, n // tn),
        in_specs=[pl.BlockSpec((tm, k), lambda i, j: (i, x_cb)), pl.BlockSpec((1, k), lambda i, j: (0, 0)),
                  b_spec] + [_ANY_SPEC] * len(extra),
        out_specs=[pl.BlockSpec((tm, tn), lambda i, j: (i, j)), pl.BlockSpec((tm, k), lambda i, j: (i, 0))],
        out_shape=[jax.ShapeDtypeStruct((m, n), F32), jax.ShapeDtypeStruct((m, k), _MXU)],
        compiler_params=pltpu.CompilerParams(
            dimension_semantics=("arbitrary", "arbitrary"), vmem_limit_bytes=VMEM_LIMIT),
    )(x, norm_w, b, *extra)


def _pro_mm(name, fn, tiled, full, k, b, resid):
    m = resid.shape[0]
    n = b.shape[1]
    tm = _pick(m, 576, 16)
    n_in = len(tiled) + len(full)

    def body(*refs):
        i = pl.program_id(0)
        u = fn(i, *[r[...] for r in refs[:n_in]]).astype(_MXU)
        b_ref, r_ref, o_ref, u_ref = refs[n_in:]
        u_ref[...] = u
        o_ref[...] = r_ref[...] + _nn(u, b_ref[...])

    row = lambda w: pl.BlockSpec((tm, w), lambda i: (i, 0))
    in_specs = [_tile_spec(t, tm, m) for t in tiled]
    in_specs += [_whole_spec(x) for x in full] + [_whole_spec(b), row(n)]
    return pl.pallas_call(
        body, name=name, grid=(m // tm,), in_specs=in_specs, out_specs=[row(n), row(k)],
        out_shape=[jax.ShapeDtypeStruct((m, n), F32), jax.ShapeDtypeStruct((m, k), _MXU)],
        compiler_params=pltpu.CompilerParams(dimension_semantics=("parallel",), vmem_limit_bytes=VMEM_LIMIT),
    )(*[t.arr for t in tiled], *full, b, resid)


def _ffn_in(h2, norm_w, w_gate_t, w_up_t, conv_w8, conv_b):
    m, k = h2.shape
    n = w_gate_t.shape[0]
    tm = _pick(m, 576, 16)
    tn = _pick(n, 1408, 128)

    def body(x_ref, xp_ref, nw_ref, wg_ref, wu_ref, cw_ref, cb_ref, hn_ref, gp_ref, up_ref, act_ref):
        i = pl.program_id(0)
        nw = nw_ref[...]

        @pl.when(pl.program_id(1) == 0)
        def _():
            hn_ref[...] = _rms_fwd(x_ref[...], nw, k)[0].astype(hn_ref.dtype)

        hn = hn_ref[...]
        hn_prev = _rms_fwd(xp_ref[...], nw, k)[0].astype(_MXU)
        wg = wg_ref[...]
        gp = _nt(hn, wg)
        gp_prev = jnp.where(i > 0, _nt(hn_prev, wg), 0.0)
        up = _nt(hn, wu_ref[...])
        gate = _conv_fwd(gp, gp_prev, cw_ref[...], FFN_CONV) + cb_ref[...]
        gp_ref[...] = gp
        up_ref[...] = up
        act_ref[...] = (_silu_parts(gate)[0] * up).astype(act_ref.dtype)

    r8 = tm // 8
    tile = pl.BlockSpec((tm, tn), lambda i, j: (i, j))
    wblk = pl.BlockSpec((tn, k), lambda i, j: (j, 0))
    return pl.pallas_call(
        body, name="ffn_in", grid=(m // tm, n // tn),
        in_specs=[pl.BlockSpec((tm, k), lambda i, j: (i, 0)),
                  pl.BlockSpec((8, k), lambda i, j: (jnp.maximum(i * r8 - 1, 0), 0)),
                  pl.BlockSpec((1, k), lambda i, j: (0, 0)), wblk, wblk,
                  pl.BlockSpec((8, tn), lambda i, j: (0, j)), pl.BlockSpec((1, tn), lambda i, j: (0, j))],
        out_specs=[pl.BlockSpec((tm, k), lambda i, j: (i, 0)), tile, tile, tile],
        out_shape=[jax.ShapeDtypeStruct((m, k), _MXU), jax.ShapeDtypeStruct((m, n), F32),
                   jax.ShapeDtypeStruct((m, n), F32), jax.ShapeDtypeStruct((m, n), _MXU)],
        compiler_params=pltpu.CompilerParams(
            dimension_semantics=("arbitrary", "arbitrary"), vmem_limit_bytes=VMEM_LIMIT),
    )(h2, h2, norm_w, w_gate_t, w_up_t, conv_w8, conv_b)


def _mm_rows(name, a, b, mode, fn, tiled, full, outs, accs=(), tm_cap=576):
    a_list = list(a) if isinstance(a, (list, tuple)) else [a]
    b_list = list(b) if isinstance(b, (list, tuple)) else [b]
    na = len(a_list)
    m = a_list[0].shape[0]
    tm = _pick(m, tm_cap, 16)
    dotf = _nn if mode == "nn" else _nt
    n_in = len(tiled) + len(full)
    n_out = len(outs)
    first = 2 * na

    def body(*refs):
        i = pl.program_id(0)
        vals = [r[...] for r in refs[first:first + n_in]]
        acc = dotf(refs[0][...], refs[na][...])
        for p in range(1, na):
            acc = acc + dotf(refs[p][...], refs[na + p][...])
        o_t, o_a = fn(i, acc, *vals)
        for r, v in zip(refs[first + n_in:first + n_in + n_out], o_t):
            r[...] = v.astype(r.dtype)
        for r, v in zip(refs[first + n_in + n_out:], o_a):
            @pl.when(i == 0)
            def _():
                r[...] = v

            @pl.when(i > 0)
            def _():
                r[...] += v

    whole = lambda x: pl.BlockSpec(x.shape, lambda i, nd=x.ndim: (0,) * nd)
    in_specs = [pl.BlockSpec((tm, x.shape[1]), lambda i: (i, 0)) for x in a_list] + [_whole_spec(x) for x in b_list]
    in_specs += [_tile_spec(t, tm, m) for t in tiled]
    in_specs += [whole(x) for x in full]
    out_specs = [pl.BlockSpec((tm, w), lambda i: (i, 0)) for w, _ in outs]
    out_specs += [pl.BlockSpec((r, w), lambda i: (0, 0)) for r, w in accs]
    out_shape = [jax.ShapeDtypeStruct((m, w), dt) for w, dt in outs]
    out_shape += [jax.ShapeDtypeStruct((r, w), F32) for r, w in accs]
    return pl.pallas_call(
        body, name=name, grid=(m // tm,), in_specs=in_specs, out_specs=out_specs, out_shape=out_shape,
        compiler_params=pltpu.CompilerParams(dimension_semantics=("arbitrary",), vmem_limit_bytes=VMEM_LIMIT),
    )(*a_list, *b_list, *[t.arr for t in tiled], *full)


ATTN_Q_TILES = 4


def _attn_probs(q, k, row0):
    tq, tp = q.shape[0], k.shape[0]
    s = _nt(q, k) * (1.0 / math.sqrt(QK_HEAD))
    row = row0 + lax.broadcasted_iota(jnp.int32, (tq, tp), 0)
    col = lax.broadcasted_iota(jnp.int32, (tq, tp), 1)
    ok = (col <= row) & (col >= PAD)
    s = jnp.where(ok, s, NEG)
    m = jnp.max(s, axis=-1, keepdims=True)
    e = jnp.exp(s - m)
    return e * (1.0 / jnp.sum(e, axis=-1, keepdims=True))


def _attn_fwd(q, k, v):
    tp = q.shape[0]
    tq = tp // ATTN_Q_TILES

    def body(q_ref, k_ref, v_ref, o_ref):
        for i in range(ATTN_Q_TILES):
            rows = slice(i * tq, (i + 1) * tq)
            keys = slice(0, (i + 1) * tq)
            p = _attn_probs(q_ref[rows, :], k_ref[keys, :], i * tq)
            o_ref[rows, :] = _nn(p, v_ref[keys, :])

    return pl.pallas_call(
        body, name="attn_fwd", grid=(MLA_HEADS,),
        in_specs=[pl.BlockSpec((tp, HP), lambda h: (0, h)),
                  pl.BlockSpec((tp, HP), lambda h: (0, h)),
                  pl.BlockSpec((tp, V_HEAD), lambda h: (0, h))],
        out_specs=pl.BlockSpec((tp, V_HEAD), lambda h: (0, h)),
        out_shape=jax.ShapeDtypeStruct((tp, MLA_HEADS * V_HEAD), F32),
        compiler_params=pltpu.CompilerParams(dimension_semantics=("parallel",), vmem_limit_bytes=VMEM_LIMIT),
    )(q, k, v)


def _attn_bwd(q, k, v, do):
    tp = q.shape[0]
    tq = tp // ATTN_Q_TILES

    def body(q_ref, k_ref, v_ref, do_ref, dq_ref, dk_ref, dv_ref):
        for i in reversed(range(ATTN_Q_TILES)):
            rows = slice(i * tq, (i + 1) * tq)
            keys = slice(0, (i + 1) * tq)
            qb = q_ref[rows, :]
            kk = k_ref[keys, :]
            dob = do_ref[rows, :]
            p = _attn_probs(qb, kk, i * tq)
            dp = _nt(dob, v_ref[keys, :])
            delta = jnp.sum(p * dp, axis=-1, keepdims=True)
            ds = p * (dp - delta) * (1.0 / math.sqrt(QK_HEAD))
            dq_ref[rows, :] = _nn(ds, kk)
            if i == ATTN_Q_TILES - 1:
                dk_ref[...] = _tn(ds, qb)
                dv_ref[...] = _tn(p, dob)
            else:
                dk_ref[keys, :] += _tn(ds, qb)
                dv_ref[keys, :] += _tn(p, dob)

    full = lambda w: pl.BlockSpec((tp, w), lambda h: (0, h))
    return pl.pallas_call(
        body, name="attn_bwd", grid=(MLA_HEADS,),
        in_specs=[full(HP), full(HP), full(V_HEAD), full(V_HEAD)],
        out_specs=[full(HP), full(HP), full(V_HEAD)],
        out_shape=[jax.ShapeDtypeStruct((tp, MLA_HEADS * HP), F32),
                   jax.ShapeDtypeStruct((tp, MLA_HEADS * HP), F32),
                   jax.ShapeDtypeStruct((tp, MLA_HEADS * V_HEAD), F32)],
        compiler_params=pltpu.CompilerParams(dimension_semantics=("parallel",), vmem_limit_bytes=VMEM_LIMIT),
    )(q, k, v, do)


def _gdn_consts():
    c = DN_CHUNK
    r = lax.broadcasted_iota(jnp.int32, (c, c), 0)
    cc = lax.broadcasted_iota(jnp.int32, (c, c), 1)
    incl = r >= cc
    strict = r > cc
    return incl, strict


def _cumsum_rows(x, reverse=False):
    c = x.shape[0]
    row = lax.broadcasted_iota(jnp.int32, x.shape, 0)
    s = 1
    while s < c:
        if reverse:
            x = x + jnp.where(row < c - s, pltpu.roll(x, c - s, 0), 0.0)
        else:
            x = x + jnp.where(row >= s, pltpu.roll(x, s, 0), 0.0)
        s *= 2
    return x


def _each(fn, *lists):
    return [fn(*a) for a in zip(*lists)]


def _interleave(chains):
    chains = list(chains)
    while chains:
        for ch in list(chains):
            try:
                next(ch)
            except StopIteration:
                chains.remove(ch)


def _gdn_chunk_common(q_ref, k_ref, v_ref, g_ref, b_ref):
    c = DN_CHUNK
    incl, strict = _gdn_consts()
    sls = [(slice(c * sub, c * (sub + 1)), slice(DN_DIM * h, DN_DIM * (h + 1)))
           for sub in range(GDN_SUB_CHUNKS) for h in range(DN_HEADS)]
    q = [q_ref[sl] * (1.0 / math.sqrt(DN_DIM)) for sl in sls]
    k = [k_ref[sl] for sl in sls]
    v = [v_ref[sl] for sl in sls]
    g = [g_ref[sl] for sl in sls]
    beta = [b_ref[sl] for sl in sls]
    gc = [_cumsum_rows(x) for x in g]
    grow = [x.T[:c, :] for x in gc]
    kb = _each(jnp.multiply, k, beta)
    kk = _each(_nt, kb, k)
    qk = _each(_nt, q, k)
    gam = [jnp.exp(x) for x in gc]
    g_last = [_rowsum(x) for x in g]
    dm = [jnp.exp(jnp.where(incl, x[:, :c] - y, NEG)) for x, y in zip(gc, grow)]
    vb = _each(jnp.multiply, v, beta)
    kbg = _each(jnp.multiply, kb, gam)
    ek = [jnp.exp(x - y) for x, y in zip(g_last, gc)]
    kd = _each(jnp.multiply, k, ek)
    return dict(q=q, k=k, v=v, beta=beta, gc=gc, gam=gam, g_last=g_last, dm=dm, kb=kb, vb=vb,
                kbg=kbg, kk=kk, ek=ek, kd=kd, qk=qk, incl=incl, strict=strict, sls=sls)


def _gdn_fwd(proj, conv_w8, alog, dtb):
    tp = proj.shape[0]
    c = DN_CHUNK
    nch = tp // c
    blk = GDN_SUB_CHUNKS * c

    def body(x_ref, xp_ref, ab_ref, w8_ref, alog_ref, dtb_ref,
             o_ref, s_ref, t_ref, q_ref, k_ref, v_ref, g_ref, b_ref, s_scr):
        @pl.when(pl.program_id(0) == 0)
        def _():
            s_scr[...] = jnp.zeros_like(s_scr)

        staged, _ = _f_gdn_prep(pl.program_id(0), x_ref[...], xp_ref[...], ab_ref[...], w8_ref[...],
                                alog_ref[...], dtb_ref[...])
        for ref, val in zip((q_ref, k_ref, v_ref, g_ref, b_ref), staged):
            ref[...] = val
        eye = (lax.broadcasted_iota(jnp.int32, (c, c), 0) == lax.broadcasted_iota(jnp.int32, (c, c), 1)).astype(F32)
        x = _gdn_chunk_common(q_ref, k_ref, v_ref, g_ref, b_ref)
        heads = range(DN_HEADS)
        bp = [-jnp.where(x["strict"], kk * dm, 0.0) for kk, dm in zip(x["kk"], x["dm"])]
        t = [eye + b for b in bp]
        for _ in range(5):
            bp = [_nn(b, b, hp="3x") for b in bp]
            t = [tt + _nn(tt, b, hp="3x") for tt, b in zip(t, bp)]
        u = _each(_nn, t, x["vb"])
        w = _each(_nn, t, x["kbg"])
        qg = _each(jnp.multiply, x["q"], x["gam"])
        mqk = _each(jnp.multiply, x["qk"], x["dm"])
        s = [s_scr[h] for h in heads]
        for sub in range(GDN_SUB_CHUNKS):
            e = [DN_HEADS * sub + h for h in heads]
            v_new = [u[i] - _nn(w[i], s[h]) for h, i in zip(heads, e)]
            o = [_nn(qg[i], s[h]) + _nn(mqk[i], v_new[h]) for h, i in zip(heads, e)]
            s_new = [s[h] * jnp.exp(x["g_last"][i]) + _tn(x["kd"][i], v_new[h]) for h, i in zip(heads, e)]
            for h, i in zip(heads, e):
                s_ref[h, sub] = s[h]
                t_ref[h, sub] = t[i]
                o_ref[x["sls"][i]] = o[h]
            s = s_new
        for h in heads:
            s_scr[h] = s[h]

    sub = GDN_SUB_CHUNKS
    rb = lambda n: (n, 0)
    rows = pl.BlockSpec((blk, DN_WIDTH), rb)
    whole = lambda a: pl.BlockSpec(a.shape, lambda n: (0, 0))
    return pl.pallas_call(
        body, name="gdn_fwd", grid=(nch // sub,),
        in_specs=[pl.BlockSpec((blk, 3 * DN_WIDTH), rb),
                  pl.BlockSpec((8, 3 * DN_WIDTH), lambda n: (jnp.maximum(n * (blk // 8) - 1, 0), 0)),
                  pl.BlockSpec((blk, LANE), lambda n: (n, C_AB // LANE)),
                  whole(conv_w8), whole(alog), whole(dtb)],
        out_specs=[rows,
                   pl.BlockSpec((DN_HEADS, sub, DN_DIM, DN_DIM), lambda n: (0, n, 0, 0)),
                   pl.BlockSpec((DN_HEADS, sub, c, c), lambda n: (0, n, 0, 0))] + [rows] * 5,
        out_shape=[jax.ShapeDtypeStruct((tp, DN_WIDTH), F32),
                   jax.ShapeDtypeStruct((DN_HEADS, nch, DN_DIM, DN_DIM), F32),
                   jax.ShapeDtypeStruct((DN_HEADS, nch, c, c), F32)] + [jax.ShapeDtypeStruct((tp, DN_WIDTH), F32)] * 5,
        scratch_shapes=[pltpu.VMEM((DN_HEADS, DN_DIM, DN_DIM), F32)],
        compiler_params=pltpu.CompilerParams(dimension_semantics=("arbitrary",), vmem_limit_bytes=VMEM_LIMIT),
    )(proj, proj, proj, conv_w8, alog, dtb)


def _gdn_bwd(q, k, v, g, beta, s_all, t_all, do, proj, conv_w8, alog, dtb, after):
    tp = q.shape[0]
    c = DN_CHUNK
    nch = tp // c
    nblk = nch // GDN_SUB_CHUNKS
    blk = GDN_SUB_CHUNKS * c

    def body(q_ref, k_ref, v_ref, g_ref, b_ref, s_ref, t_ref, do_ref, x_ref, xp_ref, xn_ref, ab_ref,
             w8_ref, alog_ref, dtb_ref, _after_ref, dqkv_ref, dab_ref, dcw_ref, dalog_ref, ddtb_ref,
             ds_scr, dq_ref, dk_ref, dv_ref, dg_ref, db_ref, nxt_scr):
        step = pl.program_id(0)

        @pl.when(step == 0)
        def _():
            ds_scr[...] = jnp.zeros_like(ds_scr)
            nxt_scr[...] = jnp.zeros_like(nxt_scr)

        xs = _gdn_chunk_common(q_ref, k_ref, v_ref, g_ref, b_ref)

        ds_state = [ds_scr[h] for h in range(DN_HEADS)]

        def chain(sub, h):
            e = DN_HEADS * sub + h
            x = {key: (val[e] if isinstance(val, list) else val) for key, val in xs.items()}
            sl = x["sls"]
            qs, kx, vx, beta_, gam, dm = x["q"], x["k"], x["v"], x["beta"], x["gam"], x["dm"]
            kb, vb, kbg, kd, ek = x["kb"], x["vb"], x["kbg"], x["kd"], x["ek"]
            t = t_ref[h, sub]
            s = s_ref[h, sub]
            dsn = ds_state[h]
            dob = do_ref[sl]
            eg_last = jnp.exp(x["g_last"])
            u = _nn(t, vb)
            w = _nn(t, kbg)
            mqk = x["qk"] * dm
            qd = qs * gam
            dqd = _nt(dob, s)
            dkd_pre = _nn(kd, dsn)
            yield
            v_new = u - _nn(w, s)
            dv_new = _tn(mqk, dob) + dkd_pre
            dq = dqd * gam
            dgam = jnp.sum(dqd * qs, axis=1, keepdims=True)
            yield
            ds_state[h] = _tn(qd, dob) + eg_last * dsn - _tn(w, dv_new)
            dmm = jnp.where(x["incl"], _nt(dob, v_new), 0.0)
            dkd = _nt(v_new, dsn)
            dw = -_nt(dv_new, s)
            dvb = _tn(t, dv_new)
            dt = _nt(dv_new, vb)
            yield
            dqk = dmm * dm
            e_mat = dmm * mqk
            dq = dq + _nn(dqk, kx)
            dk = _tn(dqk, qs) + dkd * ek
            e1 = jnp.sum(dkd * kd, axis=1, keepdims=True)
            dgc = -e1
            dg_last = jnp.sum(e1) + eg_last * jnp.sum(s * dsn)
            dt = dt + _nt(dw, kbg)
            dkbg = _tn(t, dw)
            yield
            tdt = _tn(t, dt, hp="3x")
            yield
            da = jnp.where(x["strict"], -_nt(tdt, t, hp="3x"), 0.0)
            yield
            dkk = da * dm
            e_mat = e_mat + da * x["kk"] * dm
            dkb = _nn(dkk, kx) + dkbg * gam
            dk = dk + _tn(dkk, kb)
            dgam = dgam + jnp.sum(dkbg * kb, axis=1, keepdims=True)
            yield
            dk = dk + dkb * beta_
            dbeta = jnp.sum(dkb * kx, axis=1, keepdims=True) + jnp.sum(dvb * vx, axis=1, keepdims=True)
            dv = dvb * beta_
            dgc = dgc + jnp.sum(e_mat, axis=1, keepdims=True) + dgam * gam
            dgc = dgc - jnp.sum(e_mat.T, axis=1, keepdims=True)
            yield
            dg = _cumsum_rows(dgc, reverse=True) + dg_last
            yield
            dq_ref[sl] = dq * (1.0 / math.sqrt(DN_DIM))
            dk_ref[sl] = dk
            dv_ref[sl] = dv
            dg_ref[sl] = dg
            db_ref[sl] = jnp.broadcast_to(dbeta, (c, LANE))

        chains = []
        for sub in reversed(range(GDN_SUB_CHUNKS)):
            new = [chain(sub, h) for h in range(DN_HEADS)]
            for _ in range(3):
                for ch in new:
                    next(ch)
            chains += new
        _interleave(chains)
        for h in range(DN_HEADS):
            ds_scr[h] = ds_state[h]

        dq, dk, dv = dq_ref[...], dk_ref[...], dv_ref[...]
        outs, accs = _f_gdn_prep_bwd(
            nblk - 1 - step, x_ref[...], xp_ref[...], xn_ref[...], ab_ref[...], dq, nxt_scr[0], dk, nxt_scr[1],
            dv, nxt_scr[2], dg_ref[...], db_ref[...], w8_ref[...], alog_ref[...], dtb_ref[...], nt=nblk)
        nxt_scr[0] = dq[:8]
        nxt_scr[1] = dk[:8]
        nxt_scr[2] = dv[:8]
        dqkv_ref[...] = outs[0].astype(dqkv_ref.dtype)
        dab_ref[...] = outs[1].astype(dab_ref.dtype)
        for ref, val in zip((dcw_ref, dalog_ref, ddtb_ref), accs):
            @pl.when(step == 0)
            def _():
                ref[...] = val

            @pl.when(step > 0)
            def _():
                ref[...] += val

    sub = GDN_SUB_CHUNKS
    r8 = blk // 8
    rb = lambda n: (nblk - 1 - n, 0)
    hs = lambda n: (0, nblk - 1 - n, 0, 0)
    rows = pl.BlockSpec((blk, DN_WIDTH), rb)
    whole = lambda a: pl.BlockSpec(a.shape, lambda n: (0,) * a.ndim)
    wide = 3 * DN_WIDTH
    return pl.pallas_call(
        body, name="gdn_bwd", grid=(nblk,),
        in_specs=[rows] * 5
        + [pl.BlockSpec((DN_HEADS, sub, DN_DIM, DN_DIM), hs), pl.BlockSpec((DN_HEADS, sub, c, c), hs), rows,
           pl.BlockSpec((blk, wide), rb),
           pl.BlockSpec((8, wide), lambda n: (jnp.maximum((nblk - 1 - n) * r8 - 1, 0), 0)),
           pl.BlockSpec((8, wide), lambda n: (jnp.minimum((nblk - n) * r8, tp // 8 - 1), 0)),
           pl.BlockSpec((blk, LANE), lambda n: (nblk - 1 - n, C_AB // LANE)),
           whole(conv_w8), whole(alog), whole(dtb), _ANY_SPEC],
        out_specs=[pl.BlockSpec((blk, wide), rb), pl.BlockSpec((blk, LANE), rb),
                   whole(conv_w8), whole(alog), whole(dtb)],
        out_shape=[jax.ShapeDtypeStruct((tp, wide), _MXU), jax.ShapeDtypeStruct((tp, LANE), _MXU),
                   jax.ShapeDtypeStruct(conv_w8.shape, F32), jax.ShapeDtypeStruct(alog.shape, F32),
                   jax.ShapeDtypeStruct(dtb.shape, F32)],
        scratch_shapes=[pltpu.VMEM((DN_HEADS, DN_DIM, DN_DIM), F32)] + [pltpu.VMEM((blk, DN_WIDTH), F32)] * 5
        + [pltpu.VMEM((3, 8, DN_WIDTH), F32)],
        compiler_params=pltpu.CompilerParams(dimension_semantics=("arbitrary",), vmem_limit_bytes=VMEM_LIMIT),
    )(q, k, v, g, beta, s_all, t_all, do, proj, proj, proj, proj, conv_w8, alog, dtb, after)


def _silu_parts(x):
    s = _sigmoid(x)
    return x * s, s * (1.0 + x * (1.0 - s))


def _f_rms_bwd_add(i, x, dy, dres, w, *, mask_pad):
    dx, dwr = _rms_bwd(x, w, dy, x.shape[1])
    out = dres + dx
    if mask_pad:
        out = jnp.where(_row_ids(i, x.shape[0]) >= PAD, out, 0.0)
    return (out,), (_rowsum(dwr),)


def _rope(x, cos, sin_s):
    return x * cos + _swap_halves(x) * sin_s


def _rope_t(dy, cos, sin_s):
    return dy * cos + _swap_halves(dy * sin_s)


def _f_mla_qk(i, qf, kvf, kpe, cos, sin_s, qw, kw):
    qs, ks, vs = [], [], []
    for h in range(MLA_HEADS):
        qn, _ = _rms_fwd(qf[:, HP * h:HP * (h + 1)], qw, QK_HEAD)
        qs += [qn[:, :QK_NOPE], _rope(qn[:, QK_NOPE:], cos, sin_s)]
        kh = jnp.concatenate([kvf[:, HP * h:HP * h + QK_NOPE], kpe], axis=1)
        kn, _ = _rms_fwd(kh, kw, QK_HEAD)
        ks += [kn[:, :QK_NOPE], _rope(kn[:, QK_NOPE:], cos, sin_s)]
        vs.append(kvf[:, HP * h + QK_NOPE:HP * (h + 1)])
    return (jnp.concatenate(qs, axis=1), jnp.concatenate(ks, axis=1), jnp.concatenate(vs, axis=1)), ()


def _f_mla_front(i, ql, kvl, kpe, cos, sin_s, qaw, kvaw, wq_t, wkv, qw, kw):
    qn = _rms_fwd(ql, qaw, Q_LORA)[0].astype(_MXU)
    kvn = _rms_fwd(kvl, kvaw, KV_LORA)[0].astype(_MXU)
    qf = _nt(qn, wq_t)
    kvf = _nn(kvn, wkv)
    (q, k, v), _ = _f_mla_qk(i, qf, kvf, kpe, cos, sin_s, qw, kw)
    return (qn, kvn, qf, kvf, q, k, v), ()


def _f_mla_back(i, qf, kvf, kpe, cos, sin_s, dq, dk, dv, ql, kvl, qaw, kvaw, wq_t, wkv, qw, kw):
    (dqf, dkvf, dkpe), (dqw, dkw) = _f_mla_qk_bwd(i, qf, kvf, kpe, cos, sin_s, dq, dk, dv, qw, kw)
    dqf = dqf.astype(_MXU)
    dkvf = dkvf.astype(_MXU)
    dql, dqaw = _rms_bwd(ql, qaw, _nn(dqf, wq_t), Q_LORA)
    dkvl, dkvaw = _rms_bwd(kvl, kvaw, _nt(dkvf, wkv), KV_LORA)
    return (dqf, dkvf, dkpe, dql, dkvl), (dqw, dkw, _rowsum(dqaw), _rowsum(dkvaw))


def _f_mla_qk_bwd(i, qf, kvf, kpe, cos, sin_s, dq, dk, dv, qw, kw):
    dqf, dkvf = [], []
    dkpe = None
    dqw = None
    dkw = None
    for h in range(MLA_HEADS):
        dqh = dq[:, HP * h:HP * (h + 1)]
        dqn = jnp.concatenate([dqh[:, :QK_NOPE], _rope_t(dqh[:, QK_NOPE:], cos, sin_s)], axis=1)
        dx, dwr = _rms_bwd(qf[:, HP * h:HP * (h + 1)], qw, dqn, QK_HEAD)
        dqf.append(dx)
        dqw = _rowsum(dwr) if dqw is None else dqw + _rowsum(dwr)
        dkh = dk[:, HP * h:HP * (h + 1)]
        dkn = jnp.concatenate([dkh[:, :QK_NOPE], _rope_t(dkh[:, QK_NOPE:], cos, sin_s)], axis=1)
        kh = jnp.concatenate([kvf[:, HP * h:HP * h + QK_NOPE], kpe], axis=1)
        dx, dwr = _rms_bwd(kh, kw, dkn, QK_HEAD)
        dkvf += [dx[:, :QK_NOPE], dv[:, V_HEAD * h:V_HEAD * (h + 1)]]
        dkpe = dx[:, QK_NOPE:] if dkpe is None else dkpe + dx[:, QK_NOPE:]
        dkw = _rowsum(dwr) if dkw is None else dkw + _rowsum(dwr)
    return (jnp.concatenate(dqf, axis=1), jnp.concatenate(dkvf, axis=1), dkpe), (dqw, dkw)


def _gdn_act(i, x, halo, w8):
    halo = jnp.where(i > 0, halo, 0.0)
    c = _conv_fwd(x, halo, w8, DN_CONV)
    act, dact = _silu_parts(c)
    return act, dact


def _spread_heads(ab):
    tm = ab.shape[0]
    return jnp.concatenate([jnp.broadcast_to(ab[:, h:h + 1], (tm, DN_DIM)) for h in range(2 * DN_HEADS)], axis=1)


def _gather_heads(x):
    tm = x.shape[0]
    lane = lax.broadcasted_iota(jnp.int32, (tm, LANE), 1)
    out = jnp.zeros((tm, LANE), F32)
    for h in range(2 * DN_HEADS):
        out = out + jnp.where(lane == h, x[:, DN_DIM * h:DN_DIM * h + 1], 0.0)
    return out


def _gate_parts(ab, dtb):
    lane1 = lax.broadcasted_iota(jnp.int32, (1, LANE), 1)
    dtb_c = jnp.zeros((1, LANE), F32)
    for h in range(DN_HEADS):
        dtb_c = dtb_c + jnp.where(lane1 == h, dtb[:, DN_DIM * h:DN_DIM * h + 1], 0.0)
    pre = ab + dtb_c
    sig = _sigmoid(pre)
    lane = lax.broadcasted_iota(jnp.int32, ab.shape, 1)
    return jnp.where(lane < DN_HEADS, _softplus(pre), sig), sig


def _f_gdn_prep(i, x, halo, ab, w8, alog, dtb):
    tm = x.shape[0]
    act, _ = _gdn_act(i, x, halo, w8)
    outs = []
    for part in range(2):
        for h in range(DN_HEADS):
            t = act[:, DN_WIDTH * part + DN_DIM * h:DN_WIDTH * part + DN_DIM * (h + 1)]
            outs.append(t * lax.rsqrt(jnp.sum(t * t, axis=-1, keepdims=True) + EPS))
    q = jnp.concatenate(outs[:DN_HEADS], axis=1)
    k = jnp.concatenate(outs[DN_HEADS:], axis=1)
    v = act[:, 2 * DN_WIDTH:]
    abb = _spread_heads(ab)
    valid = _row_ids(i, tm) >= PAD
    g = jnp.where(valid, -jnp.exp(alog) * _softplus(abb[:, :DN_WIDTH] + dtb), 0.0)
    beta = jnp.where(valid, _sigmoid(abb[:, DN_WIDTH:]), 0.0)
    return (q, k, v, g, beta), ()


def _f_gdn_prep_bwd(i, x, x_prev, x_next, ab, dq, dq_next, dk, dk_next, dv, dv_next, dg, dbeta,
                    w8, alog, dtb, *, nt):
    tm = x.shape[0]
    x_prev = jnp.where(i > 0, x_prev, 0.0)
    more = i < nt - 1
    ext = lambda t, t_next: jnp.concatenate([t, jnp.where(more, t_next, 0.0)], axis=0)
    taps = _conv_taps(jnp.concatenate([x, x_next], axis=0), x_prev, DN_CONV)
    c = _conv_from_taps(taps, w8)
    act, dact = _silu_parts(c)
    douts = []
    for part, dd in enumerate((ext(dq, dq_next), ext(dk, dk_next))):
        for h in range(DN_HEADS):
            t = act[:, DN_WIDTH * part + DN_DIM * h:DN_WIDTH * part + DN_DIM * (h + 1)]
            r = lax.rsqrt(jnp.sum(t * t, axis=-1, keepdims=True) + EPS)
            y = t * r
            dy = dd[:, DN_DIM * h:DN_DIM * (h + 1)]
            douts.append(r * (dy - y * jnp.sum(dy * y, axis=-1, keepdims=True)))
    douts.append(ext(dv, dv_next))
    dc = jnp.concatenate(douts, axis=1) * dact
    dqkv = _conv_bwd_x(dc[:tm], dc[tm:], w8, DN_CONV)
    dconv_w = _conv_bwd_w_taps(dc[:tm], taps)
    sp_beta, sig = _gate_parts(ab, dtb)
    spread = _spread_heads(sp_beta)
    valid = _row_ids(i, tm) >= PAD
    ea = jnp.exp(alog)
    g = -ea * spread[:, :DN_WIDTH]
    dg = jnp.where(valid, dg, 0.0)
    dbeta = jnp.where(valid, dbeta, 0.0)
    da = dg * (-ea) * _spread_heads(sig)[:, :DN_WIDTH]
    beta = spread[:, DN_WIDTH:]
    db = dbeta * beta * (1.0 - beta)
    dab = _gather_heads(jnp.concatenate([da, db], axis=1))
    return (dqkv, dab), (dconv_w, _rowsum(dg * g), _rowsum(da))


def _f_mix(i, o_mla, o_dn, z, w_mla, w_dn):
    tm = o_mla.shape[0]
    valid = _row_ids(i, tm) >= PAD
    outs = []
    for h in range(MLA_HEADS):
        y, _ = _rms_fwd(o_mla[:, V_HEAD * h:V_HEAD * (h + 1)], w_mla, V_HEAD)
        outs.append(jnp.where(valid, y, 0.0))
    for h in range(DN_HEADS):
        y, _ = _rms_fwd(o_dn[:, DN_DIM * h:DN_DIM * (h + 1)], w_dn, DN_DIM)
        outs.append(y * _silu_parts(z[:, DN_DIM * h:DN_DIM * (h + 1)])[0])
    return (jnp.concatenate(outs, axis=1),), ()


def _f_mix_bwd(i, o_mla, o_dn, z, dy_mla, dy_dn, w_mla, w_dn):
    tm = o_mla.shape[0]
    valid = _row_ids(i, tm) >= PAD
    d_mla, d_dn, d_z = [], [], []
    dw_mla = None
    dw_dn = None
    for h in range(MLA_HEADS):
        sl = slice(V_HEAD * h, V_HEAD * (h + 1))
        dx, dwr = _rms_bwd(o_mla[:, sl], w_mla, jnp.where(valid, dy_mla[:, sl], 0.0), V_HEAD)
        d_mla.append(dx)
        dw_mla = _rowsum(dwr) if dw_mla is None else dw_mla + _rowsum(dwr)
    for h in range(DN_HEADS):
        sl = slice(DN_DIM * h, DN_DIM * (h + 1))
        y, _ = _rms_fwd(o_dn[:, sl], w_dn, DN_DIM)
        sz, dsz = _silu_parts(z[:, sl])
        d_z.append(dy_dn[:, sl] * y * dsz)
        dx, dwr = _rms_bwd(o_dn[:, sl], w_dn, dy_dn[:, sl] * sz, DN_DIM)
        d_dn.append(dx)
        dw_dn = _rowsum(dwr) if dw_dn is None else dw_dn + _rowsum(dwr)
    return ((jnp.concatenate(d_mla, axis=1), jnp.concatenate(d_dn, axis=1), jnp.concatenate(d_z, axis=1)),
            (dw_mla, dw_dn))


def _f_ffn_act_bwd(i, gp, gp_prev, gp_next, up, up_next, dact, dact_next, w8, b, *, nt):
    tm = gp.shape[0]
    gp_prev = jnp.where(i > 0, gp_prev, 0.0)
    dact_next = jnp.where(i < nt - 1, dact_next, 0.0)
    cat = lambda t, t_next: jnp.concatenate([t, t_next], axis=0)
    taps = _conv_taps(cat(gp, gp_next), gp_prev, FFN_CONV)
    gate = _conv_from_taps(taps, w8) + b
    sg, dsg = _silu_parts(gate)
    dact_e = cat(dact, dact_next)
    dgate = dact_e * cat(up, up_next) * dsg
    dgate_pre = _conv_bwd_x(dgate[:tm], dgate[tm:], w8, FFN_CONV)
    dup = dact * sg[:tm]
    return (dgate_pre, dup), (_conv_bwd_w_taps(dgate[:tm], taps), _rowsum(dgate[:tm]))


def _f_loss(i, h3, tgt):
    tm = h3.shape[0]
    diff = jnp.where(_row_ids(i, tm) >= ROW0, h3 - tgt, 0.0)
    part = 0.5 * jnp.sum(diff * diff) * (1.0 / D_MODEL)
    return (diff * (1.0 / D_MODEL),), (jnp.full((1, LANE), part, F32),)


def _local_step(h0, tgt, w, token, late_weights, grads_ready):
    tp = h0.shape[0]
    proj, u = _norm_mm("in_proj", h0, w["attn_norm_w"], w["w_in"], after=token)
    p_qkv = lambda kind="cur": _In(proj, 3 * DN_WIDTH, 0, kind)
    p_z = _In(proj, DN_WIDTH, C_Z // DN_WIDTH)
    p_ql = _In(proj, Q_LORA, C_QL // Q_LORA)
    p_kvl = _In(proj, KV_LORA, C_KVL // KV_LORA)
    p_kpe = _In(proj, LANE, C_KPE // LANE)
    p_ab = _In(proj, LANE, C_AB // LANE)
    cos, sin_s = _In(w["cos"]), _In(w["sin_s"])

    mla_w = [w["q_a_norm_w"], w["kv_a_norm_w"], w["w_q_b"], w["w_kv_b"], w["q_norm_w"], w["k_norm_w"]]
    tm_mla = _pick(tp, 288, 16)
    wide = MLA_HEADS * HP
    qn, kvn, qf, kvf, q, k, v = _rows(
        "mla_front", _f_mla_front, [p_ql, p_kvl, p_kpe, cos, sin_s], mla_w,
        [(Q_LORA, _MXU), (KV_LORA, _MXU), (wide, F32), (wide, F32), (wide, _MXU), (wide, _MXU),
         (MLA_HEADS * V_HEAD, _MXU)], tm=tm_mla)
    o_mla = _attn_fwd(q, k, v)

    dn_w = [w["dn_conv_w"], w["alog_b"], w["dtb_b"]]
    o_dn, s_all, t_all, gq, gk, gv, gg, gb = _gdn_fwd(proj, *dn_w)

    out_w = [w["mla_out_norm_w"], w["dn_out_norm_w"]]
    w = dict(w, **late_weights((o_mla, o_dn), _LATE[:3]))
    h2, mixed = _pro_mm("mix_out_proj", lambda i, *t: _f_mix(i, *t)[0][0], [_In(o_mla), _In(o_dn), p_z], out_w,
                        D_MODEL, w["w_out"], h0)

    ffn_w = [w["ffn_conv_w"], w["ffn_conv_b"]]
    hn, gate_pre, up, act = _ffn_in(h2, w["ffn_norm_w"], w["w_gate"], w["w_up"], *ffn_w)
    w = dict(w, **late_weights(act, _LATE[3:]))
    twice = lambda fn: (lambda *a: (lambda o, s: (o + o, s))(*fn(*a)))
    dh3, dh3_mx, loss = _mm_rows(
        "ffn_down_loss", act, w["w_down"], "nn", twice(lambda i, y, r, t: _f_loss(i, r + y, t)),
        [_In(h2), _In(tgt)], [], [(D_MODEL, F32), (D_MODEL, _MXU)], [(1, LANE)])

    g = {}
    dact = _mm("ffn_down_dx", dh3_mx, w["w_down"], "nt")
    g["w_down"] = _mm("ffn_down_dw", act, dh3_mx, "tn", out_dtype=_MXU)
    dgate_pre, dup, g["ffn_conv_w"], g["ffn_conv_b"] = _rows(
        "ffn_act_bwd", functools.partial(_f_ffn_act_bwd, nt=tp // tm_mla),
        [_In(gate_pre), _In(gate_pre, kind="prev"), _In(gate_pre, kind="next"), _In(up), _In(up, kind="next"),
         _In(dact), _In(dact, kind="next")], ffn_w,
        [(D_FF, _MXU), (D_FF, _MXU)], [(8, D_FF), (1, D_FF)], tm=tm_mla)
    g["w_gate"], g["w_up"] = _mm_tn2("ffn_gate_up_dw", dgate_pre, dup, hn, out_dtype=_MXU)
    tok = grads_ready(g, ("w_down", "w_gate", "w_up"))
    dh2, dh2_mx, g["ffn_norm_w"] = _mm_rows(
        "ffn_gate_up_dx_rms", [dgate_pre, dup], [w["w_gate"], w["w_up"]], "nn",
        twice(lambda i, dy, x, dres, nw, _tok: _f_rms_bwd_add(i, x, dy, dres, nw, mask_pad=True)),
        [_In(h2), _In(dh3)], [w["ffn_norm_w"], tok], [(D_MODEL, F32), (D_MODEL, _MXU)], [(1, D_MODEL)], tm_cap=288)

    g["w_out"] = _mm("out_proj_dw", mixed, dh2_mx, "tn", out_dtype=_MXU)
    half = MLA_HEADS * V_HEAD
    do_mla, do_dn, dz, g["mla_out_norm_w"], g["dn_out_norm_w"] = _mm_rows(
        "out_proj_dx_mix", dh2_mx, w["w_out"], "nt",
        lambda i, dm, om, od, z, wm, wd: _f_mix_bwd(i, om, od, z, dm[:, :half], dm[:, half:], wm, wd),
        [_In(o_mla), _In(o_dn), p_z], out_w,
        [(half, _MXU), (DN_WIDTH, _MXU), (DN_WIDTH, _MXU)], [(1, V_HEAD), (1, DN_DIM)])

    dq, dk, dv = _attn_bwd(q, k, v, do_mla)
    dqf, dkvf, dkpe, dql, dkvl, g["q_norm_w"], g["k_norm_w"], g["q_a_norm_w"], g["kv_a_norm_w"] = _rows(
        "mla_back", _f_mla_back,
        [_In(qf), _In(kvf), p_kpe, cos, sin_s, _In(dq), _In(dk), _In(dv), p_ql, p_kvl], mla_w,
        [(wide, _MXU), (wide, _MXU), (LANE, _MXU), (Q_LORA, _MXU), (KV_LORA, _MXU)],
        [(1, HP), (1, HP), (1, Q_LORA), (1, KV_LORA)], tm=tm_mla)
    g["w_q_b"], g["w_kv_b"] = _mm_tn_pair("mla_b_dw", dqf, qn, kvn, dkvf)
    tok = grads_ready(g, ("w_out", "w_q_b", "w_kv_b"))

    dqkv, dab, g["dn_conv_w"], g["alog_b"], g["dtb_b"] = _gdn_bwd(
        gq, gk, gv, gg, gb, s_all, t_all, do_dn, proj, *dn_w, tok)

    dproj = jnp.concatenate([dqkv, dz, dql, dkvl, dkpe, dab], axis=1)
    g["w_in"] = _mm("in_proj_dw", dproj, u, "tn", out_dtype=_MXU)
    tok = grads_ready(g, ("w_in",))
    dh0, g["attn_norm_w"] = _mm_rows(
        "in_proj_dx_rms", dproj, w["w_in"], "nn",
        lambda i, du, x, dres, nw, _tok: _f_rms_bwd_add(i, x, du, dres, nw, mask_pad=False),
        [_In(h0), _In(dh2)], [w["attn_norm_w"], tok], [(D_MODEL, F32)], [(1, D_MODEL)])
    return loss, dh0, g


def _w_in_to_padded(w):
    c1, c2, c3 = Q_LORA, Q_LORA + KV_LORA, Q_LORA + KV_LORA + QK_ROPE
    c4 = c3 + 3 * DN_WIDTH
    c5 = c4 + DN_WIDTH
    z = lambda n: jnp.zeros((n, w.shape[1]), w.dtype)
    return jnp.concatenate([w[c3:c4], w[c4:c5], w[:c1], w[c1:c2], w[c2:c3], z(LANE - QK_ROPE),
                            w[c5:], z(LANE - 2 * DN_HEADS)], axis=0)


def _w_in_from_padded(g):
    return jnp.concatenate([g[C_QL:C_QL + Q_LORA], g[C_KVL:C_KVL + KV_LORA], g[C_KPE:C_KPE + QK_ROPE],
                            g[:C_Z + DN_WIDTH], g[C_AB:C_AB + 2 * DN_HEADS]], axis=0)


def _w_q_b_to_padded(w):
    r = w.shape[1]
    w = w.reshape(MLA_HEADS, QK_HEAD, r)
    return jnp.pad(w, ((0, 0), (0, HP - QK_HEAD), (0, 0))).reshape(MLA_HEADS * HP, r)


def _w_q_b_from_padded(g):
    r = g.shape[1]
    return g.reshape(MLA_HEADS, HP, r)[:, :QK_HEAD].reshape(MLA_HEADS * QK_HEAD, r)


def _pad_rows8(w):
    return jnp.pad(w, ((0, 8 - w.shape[0]), (0, 0)))


def _prepare(full, tp):
    w = {}
    mx = lambda a: a.astype(_MXU)
    w["attn_norm_w"] = full["attn_norm_w"]
    w["w_in"] = mx(_w_in_to_padded(full["w_in"]))
    w["q_a_norm_w"] = full["q_a_norm_w"]
    w["kv_a_norm_w"] = full["kv_a_norm_w"]
    w["w_q_b"] = mx(_w_q_b_to_padded(full["w_q_b"]))
    w["w_kv_b"] = mx(full["w_kv_b"])
    w["q_norm_w"] = jnp.pad(full["q_norm_w"], ((0, 0), (0, HP - QK_HEAD)))
    w["k_norm_w"] = jnp.pad(full["k_norm_w"], ((0, 0), (0, HP - QK_HEAD)))
    w["mla_out_norm_w"] = full["mla_out_norm_w"]
    w["dn_out_norm_w"] = full["dn_out_norm_w"]
    w["dn_conv_w"] = _pad_rows8(full["dn_conv_w"])
    w["alog_b"] = jnp.repeat(full["dn_A_log"], DN_DIM, axis=1)
    w["dtb_b"] = jnp.repeat(full["dn_dt_bias"], DN_DIM, axis=1)
    w["ffn_norm_w"] = full["ffn_norm_w"]
    w["ffn_conv_w"] = _pad_rows8(full["ffn_conv_w"])
    w["ffn_conv_b"] = full["ffn_conv_b"]
    for n in _LATE:
        if n in full:
            w[n] = mx(full[n])
    half = QK_ROPE // 2
    inv = ROPE_THETA ** (-jnp.arange(half, dtype=F32) / half)
    ang = (jnp.arange(tp, dtype=jnp.int32) - PAD).astype(F32)[:, None] * inv[None, :]
    zc = jnp.zeros((tp, LANE - QK_ROPE), F32)
    w["cos"] = jnp.concatenate([jnp.cos(ang), jnp.cos(ang), zc], axis=1)
    w["sin_s"] = jnp.concatenate([-jnp.sin(ang), jnp.sin(ang), zc], axis=1)
    return w


def _grads_to_natural(g):
    convert = {
        "w_in": ("w_in", _w_in_from_padded),
        "w_q_b": ("w_q_b", _w_q_b_from_padded),
        "q_norm_w": ("q_norm_w", lambda a: a[:, :QK_HEAD]),
        "k_norm_w": ("k_norm_w", lambda a: a[:, :QK_HEAD]),
        "dn_conv_w": ("dn_conv_w", lambda a: a[:DN_CONV]),
        "ffn_conv_w": ("ffn_conv_w", lambda a: a[:FFN_CONV]),
        "alog_b": ("dn_A_log", lambda a: a[:, ::DN_DIM]),
        "dtb_b": ("dn_dt_bias", lambda a: a[:, ::DN_DIM]),
    }
    n = {}
    for key, a in g.items():
        name, fn = convert.get(key, (key, lambda t: t))
        n[name] = fn(a)
    return n


_MESH = pl.DeviceIdType.MESH
_ANY = pl.BlockSpec(memory_space=pl.ANY)
_CHIP_FLIPS = ((1, 0), (0, 1), (1, 1))


def _me():
    return lax.axis_index("x"), lax.axis_index("y"), lax.axis_index("c")


def _all_gather(name, blk, after):
    after = list(after)

    def body(x_ref, *rest):
        out_ref, send_sems, recv_sems, local_sem = rest[len(after):]
        x, y, c = _me()
        me, sib = (x, y, c), (x, y, 1 - c)
        chips = [(x ^ fx, y ^ fy) for fx, fy in _CHIP_FLIPS]

        def slot(p):
            return out_ref.at[4 * p[0] + 2 * p[1] + p[2]]

        def copy(k, block, to, src=None):
            return pltpu.make_async_remote_copy(
                src_ref=slot(block) if src is None else src, dst_ref=slot(block),
                send_sem=send_sems.at[k], recv_sem=recv_sems.at[k], device_id=to, device_id_type=_MESH)

        mine = pltpu.make_async_copy(x_ref, slot(me), local_sem)
        mine.start()
        first = [copy(0, me, sib, src=x_ref)]
        first += [copy(1 + j, me, (*chip, c), src=x_ref) for j, chip in enumerate(chips)]
        for cp in first:
            cp.start()
        passed = [copy(4 + j, (*chip, c), sib) for j, chip in enumerate(chips)]
        for j, chip in enumerate(chips):
            copy(1 + j, (*chip, c), me).wait_recv()
            passed[j].start()
        copy(0, sib, me).wait_recv()
        for j, chip in enumerate(chips):
            copy(4 + j, (*chip, 1 - c), me).wait_recv()
        for cp in first + passed:
            cp.wait_send()
        mine.wait()

    return pl.pallas_call(
        body, name=name, in_specs=[_ANY] * (1 + len(after)), out_specs=_ANY,
        out_shape=jax.ShapeDtypeStruct((N_DEV,) + blk.shape, blk.dtype),
        scratch_shapes=[pltpu.SemaphoreType.DMA((7,)), pltpu.SemaphoreType.DMA((7,)), pltpu.SemaphoreType.DMA],
    )(blk, *after)


def _row_tile(r):
    divs = [d for d in range(16, min(r, 512) + 1, 16) if r % d == 0]
    return divs[-1] if divs else r


def _adam_math(g, w, m, v):
    m_new = ADAM_B1 * m + (1.0 - ADAM_B1) * g
    v_new = ADAM_B2 * v + (1.0 - ADAM_B2) * (g * g)
    m_hat = m_new / (1.0 - ADAM_B1 ** ADAM_STEP)
    v_hat = v_new / (1.0 - ADAM_B2 ** ADAM_STEP)
    return -ADAM_LR * (m_hat / (jnp.sqrt(v_hat) + ADAM_EPS) + ADAM_WD * w), m_new, v_new


def _adam_vectors(name, row, items, ws, ms, vs):
    k = len(items)

    def body(row_ref, *refs):
        w_refs, m_refs, v_refs = refs[:k], refs[k:2 * k], refs[2 * k:3 * k]
        outs = refs[3 * k:]
        for idx, (off, n, per_head) in enumerate(items):
            if per_head:
                spread = row_ref[:, off:off + DN_WIDTH]
                lane = lax.broadcasted_iota(jnp.int32, (1, LANE), 1)
                g = jnp.zeros((1, LANE), F32)
                for h in range(DN_HEADS):
                    g = g + jnp.where(lane == h, spread[:, DN_DIM * h:DN_DIM * h + 1], 0.0)
                g = g[:, :n]
            else:
                g = row_ref[:, off:off + n]
            d, m_new, v_new = _adam_math(g, w_refs[idx][...], m_refs[idx][...], v_refs[idx][...])
            for kind, val in enumerate((g, d, m_new, v_new)):
                outs[kind * k + idx][...] = val

    shapes = [jax.ShapeDtypeStruct((1, n), F32) for _, n, _ in items]
    res = pl.pallas_call(body, name=name, out_shape=shapes * 4)(row, *ws, *ms, *vs)
    return [list(res[kind * k:(kind + 1) * k]) for kind in range(4)]


def _adam_arrays(name, gs, ws, ms, vs):
    k = len(gs)

    def body(*refs):
        outs = refs[4 * k:]
        for idx in range(k):
            res = _adam_math(refs[idx][...], refs[k + idx][...], refs[2 * k + idx][...], refs[3 * k + idx][...])
            for kind, val in enumerate(res):
                outs[kind * k + idx][...] = val

    shapes = [jax.ShapeDtypeStruct(w.shape, F32) for w in ws]
    res = pl.pallas_call(body, name=name, out_shape=shapes * 3)(*gs, *ws, *ms, *vs)
    return [list(res[kind * k:(kind + 1) * k]) for kind in range(3)]


def _sum_parts(name, parts):
    _, r, cols = parts[0][0].shape
    tm = _row_tile(r)
    idx = jnp.stack([jnp.asarray(s, jnp.int32) for _, s in parts])
    n = len(parts)

    def body(idx_ref, *refs):
        g = refs[0][0].astype(F32)
        for p_ref in refs[1:n]:
            g = g + p_ref[0].astype(F32)
        refs[n][...] = g

    return pl.pallas_call(
        body, name=name,
        grid_spec=pltpu.PrefetchScalarGridSpec(
            num_scalar_prefetch=1, grid=(r // tm,),
            in_specs=[pl.BlockSpec((1, tm, cols), lambda i, idx_ref, p=p: (idx_ref[p], i, 0)) for p in range(n)],
            out_specs=pl.BlockSpec((tm, cols), lambda i, idx_ref: (i, 0))),
        out_shape=jax.ShapeDtypeStruct((r, cols), F32),
        compiler_params=pltpu.CompilerParams(dimension_semantics=("parallel",)),
    )(idx, *[a for a, _ in parts])


def _adam(name, parts, w, m, v):
    r, cols = w.shape
    tm = _row_tile(r)
    tc = cols // 4 if (r // tm < 4 and cols % (4 * LANE) == 0) else cols
    idx = jnp.stack([jnp.asarray(s, jnp.int32) for _, s in parts])
    n = len(parts)

    def body(idx_ref, *refs):
        g = refs[0][0].astype(F32)
        for p_ref in refs[1:n]:
            g = g + p_ref[0].astype(F32)
        w_ref, m_ref, v_ref, g_out, d_out, m_out, v_out = refs[n:]
        g_out[...] = g
        d_out[...], m_out[...], v_out[...] = _adam_math(g, w_ref[...], m_ref[...], v_ref[...])

    part_specs = [pl.BlockSpec((1, tm, tc), lambda i, j, idx_ref, p=p: (idx_ref[p], i, j)) for p in range(n)]
    flat = pl.BlockSpec((tm, tc), lambda i, j, idx_ref: (i, j))
    return pl.pallas_call(
        body, name=name,
        grid_spec=pltpu.PrefetchScalarGridSpec(
            num_scalar_prefetch=1, grid=(r // tm, cols // tc), in_specs=part_specs + [flat] * 3,
            out_specs=[flat] * 4),
        out_shape=[jax.ShapeDtypeStruct((r, cols), F32)] * 4,
        compiler_params=pltpu.CompilerParams(dimension_semantics=("parallel", "parallel")),
    )(idx, *[a for a, _ in parts], w, m, v)


def _all_gather_many(name, blks):
    n = len(blks)

    def body(*refs):
        x_refs, out_refs = refs[:n], refs[n:2 * n]
        send_sems, recv_sems, local_sems = refs[2 * n:]
        x, y, c = _me()
        me, sib = (x, y, c), (x, y, 1 - c)
        chips = [(x ^ fx, y ^ fy) for fx, fy in _CHIP_FLIPS]

        def slot(a, p):
            return out_refs[a].at[4 * p[0] + 2 * p[1] + p[2]]

        def copy(a, k, block, to, src=None):
            return pltpu.make_async_remote_copy(
                src_ref=slot(a, block) if src is None else src, dst_ref=slot(a, block),
                send_sem=send_sems.at[7 * a + k], recv_sem=recv_sems.at[7 * a + k], device_id=to,
                device_id_type=_MESH)

        mine = [pltpu.make_async_copy(x_refs[a], slot(a, me), local_sems.at[a]) for a in range(n)]
        first = []
        for a in range(n):
            mine[a].start()
            first.append(copy(a, 0, me, sib, src=x_refs[a]))
            first += [copy(a, 1 + j, me, (*chip, c), src=x_refs[a]) for j, chip in enumerate(chips)]
        for cp in first:
            cp.start()
        passed = []
        for j, chip in enumerate(chips):
            for a in range(n):
                copy(a, 1 + j, (*chip, c), me).wait_recv()
                cp = copy(a, 4 + j, (*chip, c), sib)
                cp.start()
                passed.append(cp)
        for a in range(n):
            copy(a, 0, sib, me).wait_recv()
            for j, chip in enumerate(chips):
                copy(a, 4 + j, (*chip, 1 - c), me).wait_recv()
        for cp in first + passed:
            cp.wait_send()
        for cp in mine:
            cp.wait()

    return pl.pallas_call(
        body, name=name, in_specs=[_ANY] * n, out_specs=[_ANY] * n,
        out_shape=[jax.ShapeDtypeStruct((N_DEV,) + b.shape, b.dtype) for b in blks],
        scratch_shapes=[pltpu.SemaphoreType.DMA((7 * n,)), pltpu.SemaphoreType.DMA((7 * n,)),
                        pltpu.SemaphoreType.DMA((n,))],
    )(*blks)


_HBM = pl.BlockSpec(memory_space=pltpu.HBM)
_SEM = pl.BlockSpec(memory_space=pltpu.SEMAPHORE)
_EFFECT = pltpu.SideEffectType.DATAFLOW_SIDE_EFFECTING


def _push_copies(src_refs, land_refs, send_sems, recv_sems, src_by_peer, first=0):
    x, y, c = _me()
    my_id = 4 * x + 2 * y + c
    out = []
    for k in range(len(src_refs)):
        a = first + k
        for f in range(1, N_DEV):
            px, py, pc = x ^ (f >> 2), y ^ ((f >> 1) & 1), c ^ (f & 1)
            pid = 4 * px + 2 * py + pc
            src = src_refs[k].at[pid] if src_by_peer else src_refs[k]
            start = pltpu.make_async_remote_copy(
                src_ref=src, dst_ref=land_refs[k].at[my_id], send_sem=send_sems.at[7 * a + f - 1],
                recv_sem=recv_sems.at[7 * a + f - 1], device_id=(px, py, pc), device_id_type=_MESH)
            landed = pltpu.make_async_remote_copy(
                src_ref=src, dst_ref=land_refs[k].at[pid], send_sem=send_sems.at[7 * a + f - 1],
                recv_sem=recv_sems.at[7 * a + f - 1], device_id=(px, py, pc), device_id_type=_MESH)
            out.append((start, landed))
    return out


def _push_start(name, srcs, src_by_peer, after):
    n = len(srcs)
    lands = [jax.ShapeDtypeStruct((N_DEV,) + (s.shape[1:] if src_by_peer else s.shape), s.dtype) for s in srcs]

    def body(*refs):
        src_refs, land_refs = refs[:n], refs[n:2 * n]
        send_sems, recv_sems = refs[2 * n + 1], refs[2 * n + 2]
        token = refs[-1]
        for start, _ in _push_copies(src_refs, land_refs, send_sems, recv_sems, src_by_peer):
            start.start()
        token[...] = jnp.zeros_like(token)

    hbm = lambda a: pltpu.with_memory_space_constraint(a, pltpu.HBM)
    res = pl.pallas_call(
        body, name=name,
        out_shape=(pltpu.SemaphoreType.DMA((7 * n,)), pltpu.SemaphoreType.DMA((7 * n,)),
                   *[pltpu.HBM(s.shape, s.dtype) for s in srcs], *[pltpu.HBM(s.shape, s.dtype) for s in lands],
                   jax.ShapeDtypeStruct((8, LANE), F32)),
        in_specs=[_HBM] * (2 * n) + [_ANY],
        out_specs=(_SEM, _SEM, *[_HBM] * (2 * n), pl.BlockSpec(memory_space=pltpu.VMEM)),
        input_output_aliases={i: 2 + i for i in range(2 * n)},
        compiler_params=pltpu.CompilerParams(has_side_effects=_EFFECT),
    )(*[hbm(s) for s in srcs], *[hbm(lax.empty(s.shape, s.dtype)) for s in lands], after)
    return res[0], res[1], list(res[2:2 + n]), list(res[2 + n:2 + 2 * n]), res[-1]


def _push_wait(name, send_sems, recv_sems, srcs, lands, src_by_peer, after, first=0):
    n = len(srcs)
    after = list(after) if isinstance(after, (list, tuple)) else [after]

    def body(*refs):
        src_refs, land_refs = refs[:n], refs[n:2 * n]
        s_sems, r_sems = refs[2 * n], refs[2 * n + 1]
        for _, landed in _push_copies(src_refs, land_refs, s_sems, r_sems, src_by_peer, first):
            landed.wait_send()
            landed.wait_recv()

    res = pl.pallas_call(
        body, name=name,
        out_shape=tuple(pltpu.HBM(s.shape, s.dtype) for s in list(srcs) + list(lands)),
        in_specs=[_HBM] * (2 * n) + [_SEM, _SEM] + [_ANY] * len(after),
        out_specs=tuple([_HBM] * (2 * n)),
        input_output_aliases={i: i for i in range(2 * n)},
        compiler_params=pltpu.CompilerParams(has_side_effects=_EFFECT),
    )(*srcs, *lands, send_sems, recv_sems, *after)
    return list(res[:n]), list(res[n:])


_SHARDED = (
    ("meta_tokens", 1, (N_META, D_MODEL)),
    ("w_in", 1, (D_MODEL, IN_COLS)),
    ("w_q_b", 1, (Q_LORA, MLA_HEADS * QK_HEAD)),
    ("w_kv_b", 1, (KV_LORA, MLA_HEADS * (QK_NOPE + V_HEAD))),
    ("dn_conv_w", 1, (DN_CONV, 3 * DN_WIDTH)),
    ("w_out", 0, (2 * DN_WIDTH, D_MODEL)),
    ("w_gate", 1, (D_MODEL, D_FF)),
    ("w_up", 1, (D_MODEL, D_FF)),
    ("ffn_conv_w", 1, (FFN_CONV, D_FF)),
    ("w_down", 0, (D_FF, D_MODEL)),
)
_F32_GATHERED = ("meta_tokens", "dn_conv_w", "ffn_conv_w")
_EARLY = ("w_in", "w_q_b", "w_kv_b")
_LATE = ("w_out", "w_gate", "w_up", "w_down")
_TRANSPOSED = ("w_in", "w_q_b", "w_gate", "w_up")
_REPLICATED = (
    ("attn_norm_w", D_MODEL), ("q_a_norm_w", Q_LORA), ("kv_a_norm_w", KV_LORA), ("q_norm_w", QK_HEAD),
    ("k_norm_w", QK_HEAD), ("mla_out_norm_w", V_HEAD), ("dn_A_log", DN_HEADS), ("dn_dt_bias", DN_HEADS),
    ("dn_out_norm_w", DN_DIM), ("ffn_norm_w", D_MODEL), ("ffn_conv_b", D_FF),
)
_SMALL_BLOCK = (8, 512)


def _local_shape(dim, shape):
    return (shape[0] // N_DEV, shape[1]) if dim == 0 else (shape[0], shape[1] // N_DEV)


def _from_blocks(blocks, dim, shape):
    r, c = shape
    if dim == 0:
        return blocks.reshape(r, c)
    return blocks.reshape(N_DEV, r, c // N_DEV).transpose(1, 0, 2).reshape(r, c)


def _split(flat, sizes):
    out, o = [], 0
    for s in sizes:
        out.append(flat[..., o:o + s])
        o += s
    return out


def kernel(x, meta_tokens, attn_norm_w, w_in, q_a_norm_w, w_q_b, kv_a_norm_w, w_kv_b, q_norm_w, k_norm_w, mla_out_norm_w, dn_conv_w, dn_A_log, dn_dt_bias, dn_out_norm_w, w_out, ffn_norm_w, w_gate, w_up, ffn_conv_w, ffn_conv_b, w_down, loss_target, m_meta_tokens, m_attn_norm_w, m_w_in, m_q_a_norm_w, m_w_q_b, m_kv_a_norm_w, m_w_kv_b, m_q_norm_w, m_k_norm_w, m_mla_out_norm_w, m_dn_conv_w, m_dn_A_log, m_dn_dt_bias, m_dn_out_norm_w, m_w_out, m_ffn_norm_w, m_w_gate, m_w_up, m_ffn_conv_w, m_ffn_conv_b, m_w_down, v_meta_tokens, v_attn_norm_w, v_w_in, v_q_a_norm_w, v_w_q_b, v_kv_a_norm_w, v_w_kv_b, v_q_norm_w, v_k_norm_w, v_mla_out_norm_w, v_dn_conv_w, v_dn_A_log, v_dn_dt_bias, v_dn_out_norm_w, v_w_out, v_ffn_norm_w, v_w_gate, v_w_up, v_ffn_conv_w, v_ffn_conv_b, v_w_down):
    names = [n for n, _, _ in _SHARDED] + [n for n, _ in _REPLICATED]
    given = dict(locals())
    two_d = lambda a: a.reshape(a.shape[-2:])
    view = lambda a, n: two_d(a).T if n in _TRANSPOSED else two_d(a)
    wl = {n: view(given[n], n) for n in names}
    ml = {n: view(given["m_" + n], n) for n in names}
    vl = {n: view(given["v_" + n], n) for n in names}
    out_shapes = {n: given[n].shape for n in names}

    spec = {n: (d, s) for n, d, s in _SHARDED}
    small_sizes = [math.prod(_local_shape(*spec[n])) for n in _F32_GATHERED]

    def small_block(d):
        cat = jnp.concatenate([d[n].reshape(d[n].shape[:-2] + (-1,)) for n in _F32_GATHERED], axis=-1)
        pad = [(0, 0)] * (cat.ndim - 1) + [(0, math.prod(_SMALL_BLOCK) - cat.shape[-1])]
        return jnp.pad(cat, pad).reshape(cat.shape[:-1] + _SMALL_BLOCK)

    def shard(n):
        return wl[n].astype(_MXU)

    def from_slots(n, blocks):
        d, s = spec[n]
        if d == 0 or n in _TRANSPOSED:
            return blocks.reshape(-1, blocks.shape[-1])
        return blocks.transpose(1, 0, 2).reshape(s)

    my_id = 4 * lax.axis_index("x") + 2 * lax.axis_index("y") + lax.axis_index("c")
    got = _all_gather_many("gather_early", [shard(n) for n in _EARLY] + [small_block(wl)])
    full = {n: a for n, a in wl.items() if n not in _LATE}
    for n, blocks in zip(_EARLY, got):
        full[n] = from_slots(n, blocks)
    for n, p in zip(_F32_GATHERED, _split(got[-1].reshape(N_DEV, -1), small_sizes)):
        full[n] = _from_blocks(p, *spec[n])
    late_own = [shard(n) for n in _LATE]
    l_send, l_recv, l_src, l_land, token = _push_start("gather_late_start", late_own, False, got[-1])

    def late_weights(after, names):
        first = _LATE.index(names[0])
        sl = slice(first, first + len(names))
        _, lands = _push_wait("gather_late_wait_" + names[0], l_send, l_recv, l_src[sl], l_land[sl], False,
                              after, first)
        out = {}
        for n, land, own in zip(names, lands, late_own[sl]):
            out[n] = from_slots(n, lax.dynamic_update_slice(land, own[None], (my_id, 0, 0))).astype(_MXU)
        return out

    def dest_blocks(n, a):
        d, s = spec[n]
        r, c = _local_shape(d, s)
        if n in _TRANSPOSED:
            return a.reshape(N_DEV, c, r)
        return a.reshape(N_DEV, r, c) if d == 0 else a.reshape(r, N_DEV, c).transpose(1, 0, 2)

    pushed = []

    def grads_ready(g, names):
        nat = _grads_to_natural({n: g[n] for n in names})
        blocks = [dest_blocks(n, nat[n]).astype(_MXU) for n in names]
        sends, recvs, srcs, lands, tok = _push_start("rs_" + names[0] + "_start", blocks, True, token)
        pushed.append((names, sends, recvs, srcs, lands))
        return tok

    seq = x.shape[1]
    tp = ROW0 + seq
    h0 = jnp.concatenate([jnp.zeros((PAD, D_MODEL), F32), full["meta_tokens"], x[0]], axis=0)
    tgt = jnp.concatenate([jnp.zeros((ROW0, D_MODEL), F32), loss_target[0]], axis=0)
    loss, dh0, raw = _local_step(h0, tgt, _prepare(full, tp), token, late_weights, grads_ready)
    g = _grads_to_natural(raw)
    g["meta_tokens"] = dh0[PAD:ROW0]
    grad_x = dh0[ROW0:][None]

    big = [{}, {}, {}, {}]

    def finish(group):
        names, sends, recvs, srcs, lands = group
        srcs, lands = _push_wait("rs_" + names[0] + "_wait", sends, recvs, srcs, lands, True, dh0)
        for n, src, land in zip(names, srcs, lands):
            parts = [(src, my_id)] + [(land, my_id ^ f) for f in range(1, N_DEV)]
            for kind, a in enumerate(_adam("adam_" + n, parts, wl[n], ml[n], vl[n])):
                big[kind][n] = a

    for group in pushed[:-1]:
        finish(group)
    rep_names = [n for n, _ in _REPLICATED]
    raw_key = {"dn_A_log": "alog_b", "dn_dt_bias": "dtb_b"}
    pieces = [raw[raw_key.get(n, n)] for n in rep_names] + [loss]
    pieces += [g[n].reshape(1, -1) for n in _F32_GATHERED]
    widths = [p.shape[1] for p in pieces]
    offs = [sum(widths[:k]) for k in range(len(widths))]
    cat = jnp.concatenate(pieces, axis=1)
    cols = -(-cat.shape[1] // (8 * LANE)) * LANE
    mine = jnp.pad(cat, ((0, 0), (0, 8 * cols - cat.shape[1]))).reshape(8, cols)
    everyone = _all_gather("gather_small_grads", mine, [big[1][n] for group in pushed[:-1] for n in group[0]])
    total = _sum_parts("sum_small_grads", [(everyone, d) for d in range(N_DEV)]).reshape(1, 8 * cols)
    tot = {n: total[0, o:o + wd] for n, o, wd in zip(rep_names + ["loss"] + list(_F32_GATHERED), offs, widths)}
    items = [(o, size, n in raw_key) for (n, size), o in zip(_REPLICATED, offs)]
    sm = _adam_vectors("adam_replicated", total, items, [wl[n] for n in rep_names], [ml[n] for n in rep_names],
                       [vl[n] for n in rep_names])
    sm = [dict(zip(rep_names, kind)) for kind in sm]
    mine_of = {}
    for n in _F32_GATHERED:
        d, s = spec[n]
        r, c = _local_shape(d, s)
        mine_of[n] = lax.dynamic_slice(tot[n].reshape(s), (0, my_id * c), (r, c))
    res = _adam_arrays("adam_small_sharded", [mine_of[n] for n in _F32_GATHERED], [wl[n] for n in _F32_GATHERED],
                       [ml[n] for n in _F32_GATHERED], [vl[n] for n in _F32_GATHERED])
    for kind, arrays in enumerate([[mine_of[n] for n in _F32_GATHERED]] + res):
        big[kind].update(zip(_F32_GATHERED, arrays))

    finish(pushed[-1])

    outs = [tot["loss"][0], grad_x]
    for kind in range(4):
        for n in ("meta_tokens", "attn_norm_w", "w_in", "q_a_norm_w", "w_q_b", "kv_a_norm_w", "w_kv_b", "q_norm_w",
                  "k_norm_w", "mla_out_norm_w", "dn_conv_w", "dn_A_log", "dn_dt_bias", "dn_out_norm_w", "w_out",
                  "ffn_norm_w", "w_gate", "w_up", "ffn_conv_w", "ffn_conv_b", "w_down"):
            src = big[kind] if n in big[kind] else sm[kind]
            a = src[n].T if n in _TRANSPOSED else src[n]
            outs.append(a.reshape(out_shapes[n]))
    return tuple(outs)
```
